```python
import jax, jax.numpy as jnp
from jax import lax
import numpy as np

D_MODEL = 2048
BATCH = 8
SEQ = 4096
DEPTH = 1

HEAD_DIM = 128
N_ATT_HEADS = D_MODEL // (2 * HEAD_DIM)
N_RET_HEADS = D_MODEL // (2 * HEAD_DIM)
D_ATT = N_ATT_HEADS * HEAD_DIM
D_RET = N_RET_HEADS * HEAD_DIM
D_MIX = D_ATT + D_RET
D_IN_PROJ = 3 * D_ATT + 4 * D_RET
DILATED_PATTERNS = ((128, 1), (512, 4), (2048, 16))
RET_CHUNK = 128
D_FF = -(-(8 * D_MODEL) // (3 * 256)) * 256
EPS = 1e-6

kernel_name = 'hybrid_dilated_attn_retention_block'


def rmsnorm(x, w):
    xf = x.astype(jnp.float32)
    xf = xf * lax.rsqrt(jnp.mean(xf * xf, axis=-1, keepdims=True) + EPS)
    return (xf * w.astype(jnp.float32)).astype(x.dtype)


def alibi_slopes(n_heads):
    return jnp.exp2(-8.0 * jnp.arange(1, n_heads + 1, dtype=jnp.float32) / n_heads)


def dilated_window_partial(q, k, v, slopes, window, dilation):
    B, H, S, Dh = q.shape
    half = window // (2 * dilation)
    blk = half
    L = S // dilation
    nb = -(-L // blk)
    Lp = nb * blk

    def to_sub(t):
        return t.reshape(B, H, L, dilation, Dh).transpose(0, 1, 3, 2, 4)

    qb = jnp.pad(to_sub(q), ((0, 0), (0, 0), (0, 0), (0, Lp - L), (0, 0)))
    qb = qb.reshape(B, H, dilation, nb, blk, Dh)

    def key_windows(t):
        tp = jnp.pad(to_sub(t), ((0, 0), (0, 0), (0, 0), (blk, Lp - L + blk), (0, 0)))
        tb = tp.reshape(B, H, dilation, nb + 2, blk, Dh)
        return jnp.concatenate([tb[:, :, :, :-2], tb[:, :, :, 1:-1], tb[:, :, :, 2:]], axis=4)

    kw = key_windows(k)
    vw = key_windows(v)
    s = jnp.einsum('bhrnqd,bhrnkd->bhrnqk', qb, kw) * (Dh ** -0.5)
    lq = jnp.arange(nb)[:, None] * blk + jnp.arange(blk)[None, :]
    lk = jnp.arange(nb)[:, None] * blk - blk + jnp.arange(3 * blk)[None, :]
    dist = jnp.abs(lq[:, :, None] - lk[:, None, :])
    valid = (dist <= half) & (lk[:, None, :] >= 0) & (lk[:, None, :] < L)
    bias = -slopes[:, None, None, None, None] * (dilation * dist).astype(jnp.float32)
    s = jnp.where(valid, s + bias, -jnp.inf)
    m = jnp.max(s, axis=-1)
    p = jnp.exp(s - m[..., None])
    den = jnp.sum(p, axis=-1)
    num = jnp.einsum('bhrnqk,bhrnkd->bhrnqd', p, vw)
    num = num.reshape(B, H, dilation, Lp, Dh)[:, :, :, :L].transpose(0, 1, 3, 2, 4).reshape(B, H, S, Dh)
    m = m.reshape(B, H, dilation, Lp)[..., :L].transpose(0, 1, 3, 2).reshape(B, H, S)
    den = den.reshape(B, H, dilation, Lp)[..., :L].transpose(0, 1, 3, 2).reshape(B, H, S)
    return num, m, den


def dilated_mixture_attention(q, k, v, slopes):
    q, k, v = (t.astype(jnp.float32) for t in (q, k, v))
    parts = [dilated_window_partial(q, k, v, slopes, w, d) for (w, d) in DILATED_PATTERNS]
    m_all = jnp.max(jnp.stack([p[1] for p in parts], axis=0), axis=0)
    num = 0.0
    den = 0.0
    for (n_i, m_i, d_i) in parts:
        w_i = jnp.exp(m_i - m_all)
        num = num + w_i[..., None] * n_i
        den = den + w_i * d_i
    return num / den[..., None]


def retention_direction(q, k, v, log_gamma, strict):
    B, H, S, Dh = q.shape
    C = RET_CHUNK
    nc = S // C
    idx = jnp.arange(C, dtype=jnp.float32)
    rel = idx[:, None] - idx[None, :]
    inside = (rel > 0) if strict else (rel >= 0)
    decay_mask = jnp.where(inside, jnp.exp(log_gamma[:, None, None] * jnp.maximum(rel, 0.0)), 0.0)
    q_dec = jnp.exp(log_gamma[:, None] * (idx + 1.0))[..., None]
    k_dec = jnp.exp(log_gamma[:, None] * (C - 1.0 - idx))[..., None]
    chunk_dec = jnp.exp(log_gamma * C)[:, None, None]

    def to_chunks(t):
        return t.reshape(B, H, nc, C, Dh).transpose(2, 0, 1, 3, 4)

    def step(state, qkv):
        qc, kc, vc = qkv
        inner = jnp.einsum('bhid,bhjd->bhij', qc, kc) * decay_mask
        o = jnp.einsum('bhij,bhjd->bhid', inner, vc) + jnp.einsum('bhid,bhde->bhie', qc * q_dec, state)
        state = state * chunk_dec + jnp.einsum('bhjd,bhje->bhde', kc * k_dec, vc)
        return state, o

    state0 = jnp.zeros((B, H, Dh, Dh), jnp.float32)
    _, o = lax.scan(step, state0, (to_chunks(q), to_chunks(k), to_chunks(v)))
    return o.transpose(1, 2, 0, 3, 4).reshape(B, H, S, Dh)


def bidirectional_retention(q, k, v, decay_fwd, decay_bwd):
    q, k, v = (t.astype(jnp.float32) for t in (q, k, v))
    q = q * (q.shape[-1] ** -0.5)
    lg_f = -jnp.exp(decay_fwd.astype(jnp.float32))
    lg_b = -jnp.exp(decay_bwd.astype(jnp.float32))
    o_f = retention_direction(q, k, v, lg_f, strict=False)
    flip = lambda t: jnp.flip(t, axis=2)
    o_b = flip(retention_direction(flip(q), flip(k), flip(v), lg_b, strict=True))
    return o_f + o_b


def _fwd_setup_inputs(seed: int = 0) -> dict:
    key = jax.random.key(seed)
    ks = jax.random.split(key, 16)
    f32 = jnp.float32
    nrm = lambda k, shape, scale: jax.random.normal(k, shape, f32) * scale
    gain = lambda k, shape: 1.0 + 0.01 * jax.random.normal(k, shape, f32)
    base = np.log(-np.log(1.0 - 2.0 ** (-5.0 - np.arange(N_RET_HEADS)))).astype(np.float32)
    base = jnp.asarray(base)[None, :]
    return {
        'x': jax.random.normal(ks[0], (BATCH, SEQ, D_MODEL), f32),
        'norm_mix_w': gain(ks[1], (DEPTH, D_MODEL)),
        'w_in': nrm(ks[2], (DEPTH, D_MODEL, D_IN_PROJ), D_MODEL ** -0.5),
        'ret_decay_fwd': base + 0.05 * jax.random.normal(ks[3], (DEPTH, N_RET_HEADS), f32),
        'ret_decay_bwd': base + 0.05 * jax.random.normal(ks[4], (DEPTH, N_RET_HEADS), f32),
        'ret_norm_w': gain(ks[5], (DEPTH, D_RET)),
        'w_out': nrm(ks[6], (DEPTH, D_MIX, D_MODEL), D_MIX ** -0.5),
        'norm_ffn_w': gain(ks[7], (DEPTH, D_MODEL)),
        'w_gate': nrm(ks[8], (DEPTH, D_MODEL, D_FF), D_MODEL ** -0.5),
        'w_up': nrm(ks[9], (DEPTH, D_MODEL, D_FF), D_MODEL ** -0.5),
        'w_down': nrm(ks[10], (DEPTH, D_FF, D_MODEL), D_FF ** -0.5),
        'norm_final_w': gain(ks[11], (D_MODEL,)),
    }


def _fwd_reference(x, norm_mix_w, w_in, ret_decay_fwd, ret_decay_bwd, ret_norm_w, w_out,
              norm_ffn_w, w_gate, w_up, w_down, norm_final_w):
    B, S, _ = x.shape
    slopes = alibi_slopes(N_ATT_HEADS)
    splits = [D_ATT, 2 * D_ATT, 3 * D_ATT, 3 * D_ATT + D_RET, 3 * D_ATT + 2 * D_RET, 3 * D_ATT + 3 * D_RET]

    def heads(t):
        return t.reshape(B, S, -1, HEAD_DIM).transpose(0, 2, 1, 3)

    def merge(t):
        return t.transpose(0, 2, 1, 3).reshape(B, S, -1)

    h = x
    for layer in range(DEPTH):
        n = rmsnorm(h, norm_mix_w[layer])
        proj = n @ w_in[layer]
        q_a, k_a, v_a, q_r, k_r, v_r, g_r = jnp.split(proj, splits, axis=-1)
        attn = dilated_mixture_attention(heads(q_a), heads(k_a), heads(v_a), slopes)
        ret = bidirectional_retention(heads(q_r), heads(k_r), heads(v_r),
                                      ret_decay_fwd[layer], ret_decay_bwd[layer])
        ret = ret * lax.rsqrt(jnp.mean(ret * ret, axis=-1, keepdims=True) + EPS)
        ret = merge(ret) * ret_norm_w[layer].astype(jnp.float32)
        ret = ret * jax.nn.silu(g_r.astype(jnp.float32))
        mixed = jnp.concatenate([merge(attn), ret], axis=-1).astype(x.dtype)
        h = h + mixed @ w_out[layer]
        n2 = rmsnorm(h, norm_ffn_w[layer])
        h = h + (jax.nn.silu(n2 @ w_gate[layer]) * (n2 @ w_up[layer])) @ w_down[layer]
    return rmsnorm(h, norm_final_w)


import jax as _jax
import jax.numpy as _jnp

TWIN_FORMAT = 'train_step'
FWD_PARAMS = ['x', 'norm_mix_w', 'w_in', 'ret_decay_fwd', 'ret_decay_bwd', 'ret_norm_w', 'w_out', 'norm_ffn_w', 'w_gate', 'w_up', 'w_down', 'norm_final_w']
TWIN_WEIGHTS = ['norm_mix_w', 'w_in', 'ret_decay_fwd', 'ret_decay_bwd', 'ret_norm_w', 'w_out', 'norm_ffn_w', 'w_gate', 'w_up', 'w_down', 'norm_final_w']
TWIN_DIFF_INPUT = 'x'
TWIN_INPUTS = ['x', 'norm_mix_w', 'w_in', 'ret_decay_fwd', 'ret_decay_bwd', 'ret_norm_w', 'w_out', 'norm_ffn_w', 'w_gate', 'w_up', 'w_down', 'norm_final_w', 'loss_target', 'm_norm_mix_w', 'm_w_in', 'm_ret_decay_fwd', 'm_ret_decay_bwd', 'm_ret_norm_w', 'm_w_out', 'm_norm_ffn_w', 'm_w_gate', 'm_w_up', 'm_w_down', 'm_norm_final_w', 'v_norm_mix_w', 'v_w_in', 'v_ret_decay_fwd', 'v_ret_decay_bwd', 'v_ret_norm_w', 'v_w_out', 'v_norm_ffn_w', 'v_w_gate', 'v_w_up', 'v_w_down', 'v_norm_final_w']
TWIN_OUTPUTS = ['loss', 'grad_x', 'grad_norm_mix_w', 'grad_w_in', 'grad_ret_decay_fwd', 'grad_ret_decay_bwd', 'grad_ret_norm_w', 'grad_w_out', 'grad_norm_ffn_w', 'grad_w_gate', 'grad_w_up', 'grad_w_down', 'grad_norm_final_w', 'delta_norm_mix_w', 'delta_w_in', 'delta_ret_decay_fwd', 'delta_ret_decay_bwd', 'delta_ret_norm_w', 'delta_w_out', 'delta_norm_ffn_w', 'delta_w_gate', 'delta_w_up', 'delta_w_down', 'delta_norm_final_w', 'new_m_norm_mix_w', 'new_m_w_in', 'new_m_ret_decay_fwd', 'new_m_ret_decay_bwd', 'new_m_ret_norm_w', 'new_m_w_out', 'new_m_norm_ffn_w', 'new_m_w_gate', 'new_m_w_up', 'new_m_w_down', 'new_m_norm_final_w', 'new_v_norm_mix_w', 'new_v_w_in', 'new_v_ret_decay_fwd', 'new_v_ret_decay_bwd', 'new_v_ret_norm_w', 'new_v_w_out', 'new_v_norm_ffn_w', 'new_v_w_gate', 'new_v_w_up', 'new_v_w_down', 'new_v_norm_final_w']
TWIN_LEAF_KINDS = {'loss': 'loss', 'grad_x': 'grad_x', 'grad_norm_mix_w': 'grad_w', 'grad_w_in': 'grad_w', 'grad_ret_decay_fwd': 'grad_w', 'grad_ret_decay_bwd': 'grad_w', 'grad_ret_norm_w': 'grad_w', 'grad_w_out': 'grad_w', 'grad_norm_ffn_w': 'grad_w', 'grad_w_gate': 'grad_w', 'grad_w_up': 'grad_w', 'grad_w_down': 'grad_w', 'grad_norm_final_w': 'grad_w', 'delta_norm_mix_w': 'delta_w', 'delta_w_in': 'delta_w', 'delta_ret_decay_fwd': 'delta_w', 'delta_ret_decay_bwd': 'delta_w', 'delta_ret_norm_w': 'delta_w', 'delta_w_out': 'delta_w', 'delta_norm_ffn_w': 'delta_w', 'delta_w_gate': 'delta_w', 'delta_w_up': 'delta_w', 'delta_w_down': 'delta_w', 'delta_norm_final_w': 'delta_w', 'new_m_norm_mix_w': 'new_m', 'new_m_w_in': 'new_m', 'new_m_ret_decay_fwd': 'new_m', 'new_m_ret_decay_bwd': 'new_m', 'new_m_ret_norm_w': 'new_m', 'new_m_w_out': 'new_m', 'new_m_norm_ffn_w': 'new_m', 'new_m_w_gate': 'new_m', 'new_m_w_up': 'new_m', 'new_m_w_down': 'new_m', 'new_m_norm_final_w': 'new_m', 'new_v_norm_mix_w': 'new_v', 'new_v_w_in': 'new_v', 'new_v_ret_decay_fwd': 'new_v', 'new_v_ret_decay_bwd': 'new_v', 'new_v_ret_norm_w': 'new_v', 'new_v_w_out': 'new_v', 'new_v_norm_ffn_w': 'new_v', 'new_v_w_gate': 'new_v', 'new_v_w_up': 'new_v', 'new_v_w_down': 'new_v', 'new_v_norm_final_w': 'new_v'}


def _forward(args):
    return _fwd_reference(*[args[k] for k in FWD_PARAMS])


def _output_shape():
    def fwd():
        inp = _fwd_setup_inputs(0)
        return _fwd_reference(*[inp[k] for k in FWD_PARAMS])
    out = _jax.eval_shape(fwd)
    return out.shape, out.dtype

N_MICROBATCH = 1
ADAM_LR = 0.001
ADAM_B1 = 0.9
ADAM_B2 = 0.999
ADAM_EPS = 1e-08
ADAM_WD = 0.01
ADAM_STEP = 10
PER_EXAMPLE_BATCH_AXIS = {'x': 0, 'loss_target': 0}
SHARED_INPUTS = []
_WEIGHT_DTYPES = {'norm_mix_w': _jnp.float32, 'w_in': _jnp.float32, 'ret_decay_fwd': _jnp.float32, 'ret_decay_bwd': _jnp.float32, 'ret_norm_w': _jnp.float32, 'w_out': _jnp.float32, 'norm_ffn_w': _jnp.float32, 'w_gate': _jnp.float32, 'w_up': _jnp.float32, 'w_down': _jnp.float32, 'norm_final_w': _jnp.float32}
MOMENT_SCALE = {'norm_mix_w': 8.843295e-02, 'w_in': 4.326888e-02, 'ret_decay_fwd': 1.941150e-01, 'ret_decay_bwd': 2.245673e-01, 'ret_norm_w': 5.308604e-02, 'w_out': 4.181819e-02, 'norm_ffn_w': 6.288986e-02, 'w_gate': 2.620183e-02, 'w_up': 2.534453e-02, 'w_down': 4.207430e-02, 'norm_final_w': 1.598679e+01}


def _to_microbatches(a, axis):
    t = _jnp.moveaxis(a, axis, 0)
    t = t.reshape((N_MICROBATCH, t.shape[0] // N_MICROBATCH) + t.shape[1:])
    return _jnp.moveaxis(t, 1, axis + 1)


def setup_inputs(seed: int = 0) -> dict:
    inp = _fwd_setup_inputs(seed)
    key = _jax.random.fold_in(_jax.random.key(seed), 7919)
    shape, _ = _output_shape()
    out = dict(inp)
    out["loss_target"] = _jax.random.normal(_jax.random.fold_in(key, 0), shape, _jnp.float32)
    for i, name in enumerate(TWIN_WEIGHTS):
        w = inp[name].astype(_jnp.float32)
        if MOMENT_SCALE is None:
            s = _jnp.sqrt(_jnp.mean(_jnp.square(w)) + 1e-30)
        else:
            s = MOMENT_SCALE[name]
        km, kv = _jax.random.split(_jax.random.fold_in(key, i + 1))
        out[name] = w
        out["m_" + name] = s * _jax.random.normal(km, w.shape, _jnp.float32)
        out["v_" + name] = (s * s) * _jax.random.uniform(kv, w.shape, _jnp.float32, 0.5, 1.5)
    if N_MICROBATCH > 1:
        for name, axis in PER_EXAMPLE_BATCH_AXIS.items():
            out[name] = _to_microbatches(out[name], axis)
    return {'x': out['x'], 'norm_mix_w': out['norm_mix_w'], 'w_in': out['w_in'], 'ret_decay_fwd': out['ret_decay_fwd'], 'ret_decay_bwd': out['ret_decay_bwd'], 'ret_norm_w': out['ret_norm_w'], 'w_out': out['w_out'], 'norm_ffn_w': out['norm_ffn_w'], 'w_gate': out['w_gate'], 'w_up': out['w_up'], 'w_down': out['w_down'], 'norm_final_w': out['norm_final_w'], 'loss_target': out['loss_target'], 'm_norm_mix_w': out['m_norm_mix_w'], 'm_w_in': out['m_w_in'], 'm_ret_decay_fwd': out['m_ret_decay_fwd'], 'm_ret_decay_bwd': out['m_ret_decay_bwd'], 'm_ret_norm_w': out['m_ret_norm_w'], 'm_w_out': out['m_w_out'], 'm_norm_ffn_w': out['m_norm_ffn_w'], 'm_w_gate': out['m_w_gate'], 'm_w_up': out['m_w_up'], 'm_w_down': out['m_w_down'], 'm_norm_final_w': out['m_norm_final_w'], 'v_norm_mix_w': out['v_norm_mix_w'], 'v_w_in': out['v_w_in'], 'v_ret_decay_fwd': out['v_ret_decay_fwd'], 'v_ret_decay_bwd': out['v_ret_decay_bwd'], 'v_ret_norm_w': out['v_ret_norm_w'], 'v_w_out': out['v_w_out'], 'v_norm_ffn_w': out['v_norm_ffn_w'], 'v_w_gate': out['v_w_gate'], 'v_w_up': out['v_w_up'], 'v_w_down': out['v_w_down'], 'v_norm_final_w': out['v_norm_final_w']}


def _loss(weights, diff, rest, loss_target):
    with _jax.named_scope("forward"):
        args = {**rest, TWIN_DIFF_INPUT: diff, **{k: w.astype(_WEIGHT_DTYPES[k]) for k, w in weights.items()}}
        y = _forward(args)
    with _jax.named_scope("loss_head"):
        err = _jnp.square(y.astype(_jnp.float32) - loss_target)
        return 0.5 * _jnp.sum(_jnp.mean(err, axis=-1)) if err.ndim else 0.5 * err


def _adamw(w, g, m, v):
    m = ADAM_B1 * m + (1.0 - ADAM_B1) * g
    v = ADAM_B2 * v + (1.0 - ADAM_B2) * _jnp.square(g)
    m_hat = m / (1.0 - ADAM_B1 ** ADAM_STEP)
    v_hat = v / (1.0 - ADAM_B2 ** ADAM_STEP)
    delta = -ADAM_LR * (m_hat / (_jnp.sqrt(v_hat) + ADAM_EPS) + ADAM_WD * w)
    return delta, m, v


def reference(x, norm_mix_w, w_in, ret_decay_fwd, ret_decay_bwd, ret_norm_w, w_out, norm_ffn_w, w_gate, w_up, w_down, norm_final_w, loss_target, m_norm_mix_w, m_w_in, m_ret_decay_fwd, m_ret_decay_bwd, m_ret_norm_w, m_w_out, m_norm_ffn_w, m_w_gate, m_w_up, m_w_down, m_norm_final_w, v_norm_mix_w, v_w_in, v_ret_decay_fwd, v_ret_decay_bwd, v_ret_norm_w, v_w_out, v_norm_ffn_w, v_w_gate, v_w_up, v_w_down, v_norm_final_w):
    given = dict(x=x, norm_mix_w=norm_mix_w, w_in=w_in, ret_decay_fwd=ret_decay_fwd, ret_decay_bwd=ret_decay_bwd, ret_norm_w=ret_norm_w, w_out=w_out, norm_ffn_w=norm_ffn_w, w_gate=w_gate, w_up=w_up, w_down=w_down, norm_final_w=norm_final_w, loss_target=loss_target, m_norm_mix_w=m_norm_mix_w, m_w_in=m_w_in, m_ret_decay_fwd=m_ret_decay_fwd, m_ret_decay_bwd=m_ret_decay_bwd, m_ret_norm_w=m_ret_norm_w, m_w_out=m_w_out, m_norm_ffn_w=m_norm_ffn_w, m_w_gate=m_w_gate, m_w_up=m_w_up, m_w_down=m_w_down, m_norm_final_w=m_norm_final_w, v_norm_mix_w=v_norm_mix_w, v_w_in=v_w_in, v_ret_decay_fwd=v_ret_decay_fwd, v_ret_decay_bwd=v_ret_decay_bwd, v_ret_norm_w=v_ret_norm_w, v_w_out=v_w_out, v_norm_ffn_w=v_norm_ffn_w, v_w_gate=v_w_gate, v_w_up=v_w_up, v_w_down=v_w_down, v_norm_final_w=v_norm_final_w)
    weights = {n: given[n] for n in TWIN_WEIGHTS}
    shared = {n: given[n] for n in SHARED_INPUTS}
    per_example = {n: given[n] for n in ['x']}
    grad_fn = _jax.value_and_grad(_loss, argnums=(0, 1))

    def one_microbatch(ex, loss_target):
        ex = dict(ex)
        diff = ex.pop(TWIN_DIFF_INPUT)
        return grad_fn(weights, diff, {**shared, **ex}, loss_target)

    if N_MICROBATCH == 1:
        loss, (grad_w, grad_x) = one_microbatch(per_example, given["loss_target"])
    else:
        def body(carry, xs):
            loss_sum, grad_sum = carry
            l_k, (gw_k, gx_k) = one_microbatch(xs[0], xs[1])
            with _jax.named_scope("update"):
                return (loss_sum + l_k, _jax.tree.map(_jnp.add, grad_sum, gw_k)), gx_k

        init = (_jnp.zeros((), _jnp.float32), _jax.tree.map(_jnp.zeros_like, weights))
        (loss, grad_w), grad_x = _jax.lax.scan(body, init, (per_example, given["loss_target"]))
    with _jax.named_scope("update"):
        delta_w, new_m, new_v = {}, {}, {}
        for n in TWIN_WEIGHTS:
            delta_w[n], new_m[n], new_v[n] = _adamw(weights[n], grad_w[n], given["m_" + n], given["v_" + n])
    return (loss, grad_x, *[grad_w[n] for n in TWIN_WEIGHTS], *[delta_w[n] for n in TWIN_WEIGHTS],
            *[new_m[n] for n in TWIN_WEIGHTS], *[new_v[n] for n in TWIN_WEIGHTS])
```

```python
import functools
import math

import numpy as np
import jax
import jax.numpy as jnp
from jax import lax
from jax.experimental import pallas as pl
from jax.experimental.pallas import tpu as pltpu

F32 = jnp.float32
BF16 = jnp.bfloat16
SDS = jax.ShapeDtypeStruct

HEAD_DIM = 128
EPS = 1e-6
RET_CHUNK = 128
DILATIONS = (1, 4, 16)
BAND = 64
Q_TILE = 128
K_TILE = Q_TILE + 2 * BAND
KV_PAD = BAND * max(DILATIONS)
NEG = -1e30
N_DEV = 8
N_GROUPS = 7
ADAM_LR, ADAM_B1, ADAM_B2, ADAM_EPS, ADAM_WD, ADAM_STEP = 0.001, 0.9, 0.999, 1e-08, 0.01, 10
VMEM_LIMIT = 56 * 1024 * 1024
MESH = pl.DeviceIdType.MESH
ANY = pl.BlockSpec(memory_space=pl.ANY)


def _cp(n_grid):
    return pltpu.CompilerParams(dimension_semantics=("arbitrary",) * n_grid, vmem_limit_bytes=VMEM_LIMIT)


def _sigmoid(x):
    return 1.0 / (1.0 + jnp.exp(-x))


def _rms_scale(h):
    return lax.rsqrt(jnp.mean(h * h, axis=-1, keepdims=True) + EPS)


def _rms_bwd(dn, h, w):
    r = _rms_scale(h)
    gw = dn * w
    dh = r * gw - h * (r * r * r) * jnp.mean(gw * h, axis=-1, keepdims=True)
    return dh, jnp.sum(dn * h * r, axis=0, keepdims=True)


def _dot(a, b, dims):
    return lax.dot_general(a.astype(BF16), b.astype(BF16), (dims, ((), ())), preferred_element_type=F32)


_NN = ((1,), (0,))
_NT = ((1,), (1,))
_TN = ((0,), (0,))


def _proj_fwd(x, w_norm, w_blk):
    S, D = x.shape
    nblk, _, NB = w_blk.shape
    tm = min(512, S)

    def body(x_ref, wn_ref, w_ref, proj_ref, n_ref, n_scr):
        @pl.when(pl.program_id(1) == 0)
        def _():
            xf = x_ref[...]
            nb = (xf * _rms_scale(xf) * wn_ref[...]).astype(BF16)
            n_scr[...] = nb
            n_ref[...] = nb
        proj_ref[...] = jnp.dot(n_scr[...], w_ref[0], preferred_element_type=F32)

    return pl.pallas_call(
        body, grid=(S // tm, nblk), name="proj_fwd",
        in_specs=[pl.BlockSpec((tm, D), lambda i, j: (i, 0)), pl.BlockSpec((1, D), lambda i, j: (0, 0)),
                  pl.BlockSpec((1, D, NB), lambda i, j: (j, 0, 0))],
        out_specs=[pl.BlockSpec((tm, NB), lambda i, j: (i, j)), pl.BlockSpec((tm, D), lambda i, j: (i, 0))],
        out_shape=[SDS((S, nblk * NB), F32), SDS((S, D), BF16)],
        scratch_shapes=[pltpu.VMEM((tm, D), BF16)], compiler_params=_cp(2))(x, w_norm, w_blk)


def _out_fwd(x, attn, ret, w_out, w_norm):
    S, D = x.shape
    DA = attn.shape[1]
    tm = min(256, S)

    def body(x_ref, a_ref, r_ref, w_ref, wn_ref, h_ref, mix_ref, n_ref):
        a = a_ref[...].astype(BF16)
        r = r_ref[...].astype(BF16)
        mix_ref[:, :DA] = a
        mix_ref[:, DA:] = r
        h = x_ref[...] + jnp.dot(a, w_ref[:DA, :], preferred_element_type=F32) \
            + jnp.dot(r, w_ref[DA:, :], preferred_element_type=F32)
        h_ref[...] = h
        n_ref[...] = (h * _rms_scale(h) * wn_ref[...]).astype(BF16)

    row = lambda w: pl.BlockSpec((tm, w), lambda i: (i, 0))
    return pl.pallas_call(
        body, grid=(S // tm,), name="out_fwd",
        in_specs=[row(D), row(DA), row(D - DA), pl.BlockSpec((D, D), lambda i: (0, 0)),
                  pl.BlockSpec((1, D), lambda i: (0, 0))],
        out_specs=[row(D), row(D), row(D)],
        out_shape=[SDS((S, D), F32), SDS((S, D), BF16), SDS((S, D), BF16)],
        compiler_params=_cp(1))(x, attn, ret, w_out, w_norm)


def _ffn_up(n2, wg, wu):
    S, D = n2.shape
    nblk, _, FB = wg.shape
    tm = min(512, S)

    def body(n_ref, wg_ref, wu_ref, g_ref, u_ref, a_ref):
        n = n_ref[...]
        g = jnp.dot(n, wg_ref[0], preferred_element_type=F32)
        u = jnp.dot(n, wu_ref[0], preferred_element_type=F32)
        g_ref[0] = g
        u_ref[0] = u
        a_ref[0] = (g * _sigmoid(g) * u).astype(BF16)

    wspec = pl.BlockSpec((1, D, FB), lambda j, i: (j, 0, 0))
    ospec = pl.BlockSpec((1, tm, FB), lambda j, i: (j, i, 0))
    return pl.pallas_call(
        body, grid=(nblk, S // tm), name="ffn_up",
        in_specs=[pl.BlockSpec((tm, D), lambda j, i: (i, 0)), wspec, wspec],
        out_specs=[ospec, ospec, ospec],
        out_shape=[SDS((nblk, S, FB), F32), SDS((nblk, S, FB), F32), SDS((nblk, S, FB), BF16)],
        compiler_params=_cp(2))(n2, wg, wu)


def _ffn_down_loss(act, wd, h1, target, w_norm):
    nblk, S, FB = act.shape
    D = h1.shape[1]
    tm = min(512, S)

    def body(a_ref, wd_ref, h_ref, t_ref, wn_ref, dh_ref, loss_ref, dw_ref, acc):
        i, j = pl.program_id(0), pl.program_id(1)

        @pl.when(j == 0)
        def _():
            acc[...] = h_ref[...]

        @pl.when((i == 0) & (j == 0))
        def _():
            dw_ref[...] = jnp.zeros_like(dw_ref)

        acc[...] += jnp.dot(a_ref[0], wd_ref[...], preferred_element_type=F32)

        @pl.when(j == nblk - 1)
        def _():
            h = acc[...]
            w = wn_ref[...]
            err = h * _rms_scale(h) * w - t_ref[...]
            loss_ref[...] = jnp.full(loss_ref.shape, 0.5 * jnp.sum(err * err) / D, F32)
            dh, dw = _rms_bwd(err * (1.0 / D), h, w)
            dh_ref[...] = dh
            dw_ref[...] += dw

    row = pl.BlockSpec((tm, D), lambda i, j: (i, 0))
    vec = pl.BlockSpec((1, D), lambda i, j: (0, 0))
    return pl.pallas_call(
        body, grid=(S // tm, nblk), name="ffn_down_loss",
        in_specs=[pl.BlockSpec((1, tm, FB), lambda i, j: (j, i, 0)), pl.BlockSpec((FB, D), lambda i, j: (j, 0)),
                  row, row, vec],
        out_specs=[row, pl.BlockSpec((8, 128), lambda i, j: (i, 0)), vec],
        out_shape=[SDS((S, D), F32), SDS((S // tm * 8, 128), F32), SDS((1, D), F32)],
        scratch_shapes=[pltpu.VMEM((tm, D), F32)], compiler_params=_cp(2))(act, wd, h1, target, w_norm)


def _attn_bias():
    n_heads = 8
    slopes = np.exp2(-8.0 * np.arange(1, n_heads + 1, dtype=np.float32) / n_heads)
    dist = np.abs(np.arange(K_TILE)[None, :] - BAND - np.arange(Q_TILE)[:, None])
    out = np.empty((n_heads, len(DILATIONS), Q_TILE, K_TILE), np.float32)
    for h in range(n_heads):
        for p, d in enumerate(DILATIONS):
            out[h, p] = np.where(dist <= BAND, -slopes[h] * (d * dist).astype(np.float32), NEG)
    return jnp.asarray(out)


def _attn_tiles(S, d):
    L = S // d
    per_class = L // Q_TILE
    return L, per_class, d * per_class


def _tile_rows(t, d, per_class):
    r = t // per_class
    a = (t % per_class) * Q_TILE
    q_rows = pl.ds(r + d * a, Q_TILE, stride=d) if d > 1 else pl.ds(pl.multiple_of(a, Q_TILE), Q_TILE)
    k_rows = pl.ds(KV_PAD + r + d * (a - BAND), K_TILE, stride=d) if d > 1 else pl.ds(
        pl.multiple_of(KV_PAD + a - BAND, BAND), K_TILE)
    return a, q_rows, k_rows


def _edge_mask(a, L):
    lk = lax.broadcasted_iota(jnp.int32, (1, K_TILE), 1) + (a - BAND)
    return jnp.where((lk >= 0) & (lk < L), 0.0, NEG).astype(F32)


def _fill_padded(dst, src, S):
    dst[pl.ds(0, KV_PAD), :] = jnp.zeros((KV_PAD, HEAD_DIM), F32)
    dst[pl.ds(KV_PAD + S, KV_PAD), :] = jnp.zeros((KV_PAD, HEAD_DIM), F32)
    dst[pl.ds(KV_PAD, S), :] = src[...]


def _head_specs(S, groups, n_heads):
    return [pl.BlockSpec((S, HEAD_DIM), functools.partial(lambda h, g: (0, g * n_heads + h), g=g)) for g in groups]


def _attn_fwd(proj, bias):
    S = proj.shape[0]
    H = proj.shape[1] // (N_GROUPS * HEAD_DIM)
    scale = HEAD_DIM ** -0.5

    def body(q_ref, k_ref, v_ref, b_ref, o_ref, lse_ref, kp, vp, m_run, l_run):
        _fill_padded(kp, k_ref, S)
        _fill_padded(vp, v_ref, S)
        o_ref[...] = jnp.zeros_like(o_ref)
        m_run[...] = jnp.full(m_run.shape, NEG, F32)
        l_run[...] = jnp.zeros_like(l_run)
        for p, d in enumerate(DILATIONS):
            L, per_class, n_tiles = _attn_tiles(S, d)

            def tile(t, carry, p=p, d=d, L=L, per_class=per_class):
                a, q_rows, k_rows = _tile_rows(t, d, per_class)
                s = _dot(q_ref[q_rows, :], kp[k_rows, :], _NT) * scale + b_ref[0, p] + _edge_mask(a, L)
                m_old = m_run[q_rows, :][:, :1]
                m_new = jnp.maximum(m_old, jnp.max(s, axis=-1, keepdims=True))
                pr = jnp.exp(s - m_new)
                alpha = jnp.exp(m_old - m_new)
                l_new = alpha * l_run[q_rows, :][:, :1] + jnp.sum(pr, axis=-1, keepdims=True)
                o_ref[q_rows, :] = alpha * o_ref[q_rows, :] + _dot(pr, vp[k_rows, :], _NN)
                m_run[q_rows, :] = jnp.broadcast_to(m_new, (Q_TILE, HEAD_DIM))
                l_run[q_rows, :] = jnp.broadcast_to(l_new, (Q_TILE, HEAD_DIM))
                return carry

            lax.fori_loop(0, n_tiles, tile, 0)
        l = l_run[...]
        o_ref[...] = o_ref[...] / l
        lse_ref[...] = m_run[...] + jnp.log(l)

    hspec = pl.BlockSpec((S, HEAD_DIM), lambda h: (0, h))
    return pl.pallas_call(
        body, grid=(H,), name="attn_fwd",
        in_specs=_head_specs(S, (0, 1, 2), H) + [
            pl.BlockSpec((1, len(DILATIONS), Q_TILE, K_TILE), lambda h: (h, 0, 0, 0))],
        out_specs=[hspec, hspec],
        out_shape=[SDS((S, H * HEAD_DIM), F32), SDS((S, H * HEAD_DIM), F32)],
        scratch_shapes=[pltpu.VMEM((S + 2 * KV_PAD, HEAD_DIM), F32), pltpu.VMEM((S + 2 * KV_PAD, HEAD_DIM), F32),
                        pltpu.VMEM((S, HEAD_DIM), F32), pltpu.VMEM((S, HEAD_DIM), F32)],
        compiler_params=_cp(1))(proj, proj, proj, bias)


def _attn_bwd(proj, out, lse, dmix, bias):
    S = proj.shape[0]
    H = proj.shape[1] // (N_GROUPS * HEAD_DIM)
    scale = HEAD_DIM ** -0.5

    def body(q_ref, k_ref, v_ref, o_ref, lse_ref, do_ref, b_ref, dq_ref, dk_ref, dv_ref, kp, vp, dkp, dvp, dsum):
        _fill_padded(kp, k_ref, S)
        _fill_padded(vp, v_ref, S)
        dkp[...] = jnp.zeros_like(dkp)
        dvp[...] = jnp.zeros_like(dvp)
        dq_ref[...] = jnp.zeros_like(dq_ref)
        dsum[...] = jnp.broadcast_to(jnp.sum(do_ref[...] * o_ref[...], axis=-1, keepdims=True), dsum.shape)
        for p, d in enumerate(DILATIONS):
            L, per_class, n_tiles = _attn_tiles(S, d)

            def tile(t, carry, p=p, d=d, L=L, per_class=per_class):
                a, q_rows, k_rows = _tile_rows(t, d, per_class)
                qt = q_ref[q_rows, :]
                kt = kp[k_rows, :]
                dot_ = do_ref[q_rows, :]
                s = _dot(qt, kt, _NT) * scale + b_ref[0, p] + _edge_mask(a, L)
                pr = jnp.exp(s - lse_ref[q_rows, :][:, :1])
                dp = _dot(dot_, vp[k_rows, :], _NT)
                ds = pr * (dp - dsum[q_rows, :][:, :1]) * scale
                dq_ref[q_rows, :] = dq_ref[q_rows, :] + _dot(ds, kt, _NN)
                dkp[k_rows, :] = dkp[k_rows, :] + _dot(ds, qt, _TN)
                dvp[k_rows, :] = dvp[k_rows, :] + _dot(pr, dot_, _TN)
                return carry

            lax.fori_loop(0, n_tiles, tile, 0)
        dk_ref[...] = dkp[pl.ds(KV_PAD, S), :]
        dv_ref[...] = dvp[pl.ds(KV_PAD, S), :]

    hspec = pl.BlockSpec((S, HEAD_DIM), lambda h: (0, h))
    padded = pltpu.VMEM((S + 2 * KV_PAD, HEAD_DIM), F32)
    return pl.pallas_call(
        body, grid=(H,), name="attn_bwd",
        in_specs=_head_specs(S, (0, 1, 2), H) + [hspec, hspec, hspec,
                                                  pl.BlockSpec((1, len(DILATIONS), Q_TILE, K_TILE), lambda h: (h, 0, 0, 0))],
        out_specs=[hspec, hspec, hspec],
        out_shape=[SDS((S, H * HEAD_DIM), F32)] * 3,
        scratch_shapes=[padded, padded, padded, padded, pltpu.VMEM((S, HEAD_DIM), F32)],
        compiler_params=_cp(1))(proj, proj, proj, out, lse, dmix, bias)


def _ret_consts(lg, forward):
    C = RET_CHUNK
    i = lax.broadcasted_iota(jnp.int32, (C, C), 0)
    j = lax.broadcasted_iota(jnp.int32, (C, C), 1)
    rel = (i - j) if forward else (j - i)
    inside = (rel >= 0) if forward else (rel > 0)
    relf = jnp.maximum(rel, 0).astype(F32)
    mask = jnp.where(inside, jnp.exp(lg * relf), 0.0)
    idx = lax.broadcasted_iota(jnp.int32, (C, 1), 0).astype(F32)
    q_exp = (idx + 1.0) if forward else (C - idx)
    k_exp = (C - 1.0 - idx) if forward else idx
    return mask, relf, jnp.exp(lg * q_exp), q_exp, jnp.exp(lg * k_exp), k_exp, jnp.exp(lg * C)


def _log_decay(dec_ref, h):
    return -jnp.exp(jnp.full((1, 1), dec_ref[0, h], F32))


def _chunk(c):
    return pl.ds(pl.multiple_of(c * RET_CHUNK, RET_CHUNK), RET_CHUNK)


def _ret_fwd(proj, dec_f, dec_b, w_norm):
    S = proj.shape[0]
    H = proj.shape[1] // (N_GROUPS * HEAD_DIM)
    nc = S // RET_CHUNK
    scale = HEAD_DIM ** -0.5

    def body(df_ref, db_ref, q_ref, k_ref, v_ref, g_ref, w_ref, y_ref, o_ref):
        h = pl.program_id(0)
        for forward, dref in ((True, df_ref), (False, db_ref)):
            mask, _, q_dec, _, k_dec, _, c_dec = _ret_consts(_log_decay(dref, h), forward)

            def step(n, state, forward=forward, mask=mask, q_dec=q_dec, k_dec=k_dec, c_dec=c_dec):
                rows = _chunk(n if forward else nc - 1 - n)
                qc = q_ref[rows, :] * scale
                kc = k_ref[rows, :]
                vc = v_ref[rows, :]
                o = _dot(_dot(qc, kc, _NT) * mask, vc, _NN) + _dot(qc * q_dec, state, _NN)
                if forward:
                    o_ref[rows, :] = o
                else:
                    o_ref[rows, :] = o_ref[rows, :] + o
                return state * c_dec + _dot(kc * k_dec, vc, _TN)

            lax.fori_loop(0, nc, step, jnp.zeros((HEAD_DIM, HEAD_DIM), F32))
        o = o_ref[...]
        g = g_ref[...]
        y_ref[...] = o * _rms_scale(o) * w_ref[...] * (g * _sigmoid(g))

    hspec = pl.BlockSpec((S, HEAD_DIM), lambda h: (0, h))
    smem = pl.BlockSpec(memory_space=pltpu.SMEM)
    return pl.pallas_call(
        body, grid=(H,), name="ret_fwd",
        in_specs=[smem, smem] + _head_specs(S, (3, 4, 5, 6), H) + [pl.BlockSpec((1, HEAD_DIM), lambda h: (0, h))],
        out_specs=[hspec, hspec],
        out_shape=[SDS((S, H * HEAD_DIM), F32)] * 2,
        compiler_params=_cp(1))(dec_f, dec_b, proj, proj, proj, proj, w_norm)


def _ret_bwd(proj, o_raw, dmix, dec_f, dec_b, w_norm, col0):
    S = proj.shape[0]
    H = proj.shape[1] // (N_GROUPS * HEAD_DIM)
    C = RET_CHUNK
    nc = S // C
    scale = HEAD_DIM ** -0.5

    def body(df_ref, db_ref, q_ref, k_ref, v_ref, g_ref, o_ref, dy_ref, w_ref,
             dq_ref, dk_ref, dv_ref, dg_ref, small_ref, do, states):
        h = pl.program_id(0)
        o = o_ref[...]
        g = g_ref[...]
        dy = dy_ref[...]
        w = w_ref[...]
        rr = _rms_scale(o)
        normed = o * rr
        sg = _sigmoid(g)
        silu = g * sg
        small_ref[0, pl.ds(2, 1), :] = jnp.sum(dy * normed * silu, axis=0, keepdims=True)
        dg_ref[...] = dy * normed * w * (sg * (1.0 + g * (1.0 - sg)))
        dnormed = dy * w * silu
        do[...] = rr * dnormed - o * (rr * rr * rr) * jnp.mean(dnormed * o, axis=-1, keepdims=True)

        for forward, dref, row in ((True, df_ref, 0), (False, db_ref, 1)):
            lg = _log_decay(dref, h)
            mask, relf, q_dec, q_exp, k_dec, k_exp, c_dec = _ret_consts(lg, forward)

            def fwd_step(n, state, forward=forward, k_dec=k_dec, c_dec=c_dec):
                cidx = n if forward else nc - 1 - n
                rows = _chunk(cidx)
                states[cidx] = state
                return state * c_dec + _dot(k_ref[rows, :] * k_dec, v_ref[rows, :], _TN)

            lax.fori_loop(0, nc, fwd_step, jnp.zeros((HEAD_DIM, HEAD_DIM), F32))

            def bwd_step(n, carry, forward=forward, mask=mask, relf=relf, q_dec=q_dec, q_exp=q_exp,
                         k_dec=k_dec, k_exp=k_exp, c_dec=c_dec):
                d_state, dlam = carry
                cidx = (nc - 1 - n) if forward else n
                rows = _chunk(cidx)
                qc = q_ref[rows, :] * scale
                kc = k_ref[rows, :]
                vc = v_ref[rows, :]
                doc = do[rows, :]
                state = states[cidx]
                a0 = _dot(qc, kc, _NT)
                dp = _dot(doc, vc, _NT) * mask
                gq = _dot(doc, state, _NT)
                gk = _dot(vc, d_state, _NT)
                dq = _dot(dp, kc, _NN) + q_dec * gq
                dk = _dot(dp, qc, _TN) + k_dec * gk
                dv = _dot(a0 * mask, doc, _TN) + _dot(kc * k_dec, d_state, _NN)
                if forward:
                    dq_ref[rows, :] = dq * scale
                    dk_ref[rows, :] = dk
                    dv_ref[rows, :] = dv
                else:
                    dq_ref[rows, :] = dq_ref[rows, :] + dq * scale
                    dk_ref[rows, :] = dk_ref[rows, :] + dk
                    dv_ref[rows, :] = dv_ref[rows, :] + dv
                dlam = dlam + jnp.sum(relf * a0 * dp, axis=0, keepdims=True) \
                    + jnp.sum(q_exp * q_dec * qc * gq + k_exp * k_dec * kc * gk, axis=0, keepdims=True) \
                    + (C * c_dec) * jnp.sum(state * d_state, axis=0, keepdims=True)
                return d_state * c_dec + _dot(qc * q_dec, doc, _TN), dlam

            _, dlam = lax.fori_loop(0, nc, bwd_step, (jnp.zeros((HEAD_DIM, HEAD_DIM), F32), jnp.zeros((1, HEAD_DIM), F32)))
            small_ref[0, pl.ds(row, 1), :] = jnp.broadcast_to(jnp.sum(dlam, axis=-1, keepdims=True) * lg, (1, HEAD_DIM))
        small_ref[0, pl.ds(3, 5), :] = jnp.zeros((5, HEAD_DIM), F32)

    hspec = pl.BlockSpec((S, HEAD_DIM), lambda h: (0, h))
    smem = pl.BlockSpec(memory_space=pltpu.SMEM)
    nh0 = col0 // HEAD_DIM
    return pl.pallas_call(
        body, grid=(H,), name="ret_bwd",
        in_specs=[smem, smem] + _head_specs(S, (3, 4, 5, 6), H) + [
            hspec, pl.BlockSpec((S, HEAD_DIM), lambda h: (0, nh0 + h)), pl.BlockSpec((1, HEAD_DIM), lambda h: (0, h))],
        out_specs=[hspec, hspec, hspec, hspec, pl.BlockSpec((1, 8, HEAD_DIM), lambda h: (h, 0, 0))],
        out_shape=[SDS((S, H * HEAD_DIM), F32)] * 4 + [SDS((H, 8, HEAD_DIM), F32)],
        scratch_shapes=[pltpu.VMEM((S, HEAD_DIM), F32), pltpu.VMEM((nc, HEAD_DIM, HEAD_DIM), F32)],
        compiler_params=_cp(1))(dec_f, dec_b, proj, proj, proj, proj, o_raw, dmix, w_norm)


def _ffn_bwd_act(dh2, wd, g, u):
    S, D = dh2.shape
    nblk, _, FB = g.shape
    tm = min(512, S)

    def body(dh_ref, wd_ref, g_ref, u_ref, dg_ref, du_ref):
        dact = _dot(dh_ref[...], wd_ref[...], _NT)
        gg = g_ref[0]
        sg = _sigmoid(gg)
        dg_ref[0] = (dact * u_ref[0] * (sg * (1.0 + gg * (1.0 - sg)))).astype(BF16)
        du_ref[0] = (dact * (gg * sg)).astype(BF16)

    blk = pl.BlockSpec((1, tm, FB), lambda j, i: (j, i, 0))
    return pl.pallas_call(
        body, grid=(nblk, S // tm), name="ffn_bwd_act",
        in_specs=[pl.BlockSpec((tm, D), lambda j, i: (i, 0)), pl.BlockSpec((FB, D), lambda j, i: (j, 0)), blk, blk],
        out_specs=[blk, blk], out_shape=[SDS((nblk, S, FB), BF16)] * 2,
        compiler_params=_cp(2))(dh2, wd, g, u)


def _ffn_bwd_in(dg, du, wg, wu, h1, dh2, w_norm):
    nblk, S, FB = dg.shape
    D = h1.shape[1]
    tm = min(256, S)

    def body(dg_ref, du_ref, wg_ref, wu_ref, h_ref, dh2_ref, wn_ref, dh_ref, dw_ref, acc):
        i, j = pl.program_id(0), pl.program_id(1)

        @pl.when(j == 0)
        def _():
            acc[...] = jnp.zeros_like(acc)

        @pl.when((i == 0) & (j == 0))
        def _():
            dw_ref[...] = jnp.zeros_like(dw_ref)

        acc[...] += _dot(dg_ref[0], wg_ref[0], _NT) + _dot(du_ref[0], wu_ref[0], _NT)

        @pl.when(j == nblk - 1)
        def _():
            dh, dw = _rms_bwd(acc[...], h_ref[...], wn_ref[...])
            dh_ref[...] = dh2_ref[...] + dh
            dw_ref[...] += dw

    blk = pl.BlockSpec((1, tm, FB), lambda i, j: (j, i, 0))
    wspec = pl.BlockSpec((1, D, FB), lambda i, j: (j, 0, 0))
    row = pl.BlockSpec((tm, D), lambda i, j: (i, 0))
    vec = pl.BlockSpec((1, D), lambda i, j: (0, 0))
    return pl.pallas_call(
        body, grid=(S // tm, nblk), name="ffn_bwd_in",
        in_specs=[blk, blk, wspec, wspec, row, row, vec],
        out_specs=[row, vec], out_shape=[SDS((S, D), F32), SDS((1, D), F32)],
        scratch_shapes=[pltpu.VMEM((tm, D), F32)], compiler_params=_cp(2))(dg, du, wg, wu, h1, dh2, w_norm)


def _dmix(dh1, w_out):
    S, D = dh1.shape
    tm = min(512, S)

    def body(dh_ref, w_ref, o_ref):
        o_ref[...] = _dot(dh_ref[...], w_ref[...], _NT)

    row = pl.BlockSpec((tm, D), lambda i: (i, 0))
    return pl.pallas_call(
        body, grid=(S // tm,), name="dmix", in_specs=[row, pl.BlockSpec((D, D), lambda i: (0, 0))],
        out_specs=row, out_shape=SDS((S, D), F32), compiler_params=_cp(1))(dh1, w_out)


def _in_bwd(dproj, w_blk, x, dh1, w_norm):
    S, D = x.shape
    nblk, _, NB = w_blk.shape
    tm = min(512, S)

    def body(dp_ref, w_ref, x_ref, dh1_ref, wn_ref, dx_ref, dw_ref, acc):
        i, j = pl.program_id(0), pl.program_id(1)

        @pl.when(j == 0)
        def _():
            acc[...] = jnp.zeros_like(acc)

        @pl.when((i == 0) & (j == 0))
        def _():
            dw_ref[...] = jnp.zeros_like(dw_ref)

        acc[...] += _dot(dp_ref[...], w_ref[0], _NT)

        @pl.when(j == nblk - 1)
        def _():
            dh, dw = _rms_bwd(acc[...], x_ref[...], wn_ref[...])
            dx_ref[...] = dh1_ref[...] + dh
            dw_ref[...] += dw

    row = pl.BlockSpec((tm, D), lambda i, j: (i, 0))
    vec = pl.BlockSpec((1, D), lambda i, j: (0, 0))
    return pl.pallas_call(
        body, grid=(S // tm, nblk), name="in_bwd",
        in_specs=[pl.BlockSpec((tm, NB), lambda i, j: (i, j)), pl.BlockSpec((1, D, NB), lambda i, j: (j, 0, 0)),
                  row, row, vec],
        out_specs=[row, vec], out_shape=[SDS((S, D), F32), SDS((1, D), F32)],
        scratch_shapes=[pltpu.VMEM((tm, D), F32)], compiler_params=_cp(2))(dproj, w_blk, x, dh1, w_norm)


def _wgrad(a, b, a_spec, b_spec, o_spec, o_shape, grid, name):
    nk = grid[-1]

    def ld(ref):
        return ref[0] if len(ref.shape) == 3 else ref[...]

    def body(a_ref, b_ref, o_ref, acc):
        k = pl.program_id(len(grid) - 1)

        @pl.when(k == 0)
        def _():
            acc[...] = jnp.zeros_like(acc)

        acc[...] += _dot(ld(a_ref), ld(b_ref), _TN)

        @pl.when(k == nk - 1)
        def _():
            if len(o_ref.shape) == 3:
                o_ref[0] = acc[...].astype(o_ref.dtype)
            else:
                o_ref[...] = acc[...].astype(o_ref.dtype)

    return pl.pallas_call(
        body, grid=grid, name=name, in_specs=[a_spec, b_spec], out_specs=o_spec, out_shape=SDS(o_shape, BF16),
        scratch_shapes=[pltpu.VMEM(o_spec.block_shape[-2:], F32)], compiler_params=_cp(len(grid)))(a, b)


def _peer(k):
    x, y, c = lax.axis_index("x"), lax.axis_index("y"), lax.axis_index("c")
    px = 1 - x if k & 4 else x
    py = 1 - y if k & 2 else y
    pc = 1 - c if k & 1 else c
    return (px, py, pc), 4 * px + 2 * py + pc


def _exchange(arrays, gather, name):
    n = len(arrays)

    def body(*refs):
        srcs, dsts = refs[:n], refs[n:2 * n]
        send_sems, recv_sems, local_sems = refs[2 * n:]
        _, me = _peer(0)
        copies = []
        for a in range(n):
            own = srcs[a] if gather else srcs[a].at[me]
            local = pltpu.make_async_copy(own, dsts[a].at[me], local_sems.at[a])
            local.start()
            copies.append((local, None))
            for k in range(1, N_DEV):
                dev, idx = _peer(k)
                sem = a * (N_DEV - 1) + k - 1
                out = pltpu.make_async_remote_copy(
                    src_ref=srcs[a] if gather else srcs[a].at[idx], dst_ref=dsts[a].at[me],
                    send_sem=send_sems.at[sem], recv_sem=recv_sems.at[sem], device_id=dev, device_id_type=MESH)
                out.start()
                arrival = pltpu.make_async_remote_copy(
                    src_ref=srcs[a] if gather else srcs[a].at[idx], dst_ref=dsts[a].at[idx],
                    send_sem=send_sems.at[sem], recv_sem=recv_sems.at[sem], device_id=dev, device_id_type=MESH)
                copies.append((out, arrival))
        for out, arrival in copies:
            if arrival is None:
                out.wait()
            else:
                out.wait_send()
                arrival.wait_recv()

    out_shape = [SDS((N_DEV,) + a.shape if gather else a.shape, a.dtype) for a in arrays]
    return pl.pallas_call(
        body, name=name, in_specs=[ANY] * n, out_specs=[ANY] * n, out_shape=out_shape,
        scratch_shapes=[pltpu.SemaphoreType.DMA((n * (N_DEV - 1),)), pltpu.SemaphoreType.DMA((n * (N_DEV - 1),)),
                        pltpu.SemaphoreType.DMA((n,))])(*arrays)


SMALL_ROWS = 64


def _small_step(part, w, m, v):
    def body(p_ref, w_ref, m_ref, v_ref, g_ref, d_ref, nm_ref, nv_ref, gath, send_sems, recv_sems):
        _, me = _peer(0)
        gath[me] = p_ref[...]
        copies = []
        for k in range(1, N_DEV):
            dev, idx = _peer(k)
            out = pltpu.make_async_remote_copy(src_ref=p_ref, dst_ref=gath.at[me], send_sem=send_sems.at[k - 1],
                                               recv_sem=recv_sems.at[k - 1], device_id=dev, device_id_type=MESH)
            out.start()
            arrival = pltpu.make_async_remote_copy(src_ref=p_ref, dst_ref=gath.at[idx], send_sem=send_sems.at[k - 1],
                                                   recv_sem=recv_sems.at[k - 1], device_id=dev, device_id_type=MESH)
            copies.append((out, arrival))
        for out, arrival in copies:
            out.wait_send()
            arrival.wait_recv()
        g = gath[0]
        for p in range(1, N_DEV):
            g = g + gath[p]
        g_ref[...] = g
        d_ref[...], nm_ref[...], nv_ref[...] = _adamw(w_ref[...], g, m_ref[...], v_ref[...])

    vm = pl.BlockSpec(memory_space=pltpu.VMEM)
    return pl.pallas_call(
        body, name="small_step", in_specs=[vm] * 4, out_specs=[vm] * 4,
        out_shape=[SDS((SMALL_ROWS, 128), F32)] * 4,
        scratch_shapes=[pltpu.VMEM((N_DEV, SMALL_ROWS, 128), F32), pltpu.SemaphoreType.DMA((N_DEV - 1,)),
                        pltpu.SemaphoreType.DMA((N_DEV - 1,))])(part, w, m, v)


def _adamw(w, g, m, v):
    m = ADAM_B1 * m + (1.0 - ADAM_B1) * g
    v = ADAM_B2 * v + (1.0 - ADAM_B2) * (g * g)
    m_hat = m / (1.0 - ADAM_B1 ** ADAM_STEP)
    v_hat = v / (1.0 - ADAM_B2 ** ADAM_STEP)
    delta = -ADAM_LR * (m_hat / (jnp.sqrt(v_hat) + ADAM_EPS) + ADAM_WD * w)
    return delta, m, v


def _adamw_block(parts, w, m, v, name):
    R, C = w.shape
    tr = next(t for t in (256, 128, 64, 32, 16, 8) if R % t == 0 and t * C <= 256 * 1024)

    def body(p_ref, w_ref, m_ref, v_ref, g_ref, d_ref, nm_ref, nv_ref):
        g = p_ref[0].astype(F32)
        for p in range(1, N_DEV):
            g = g + p_ref[p].astype(F32)
        g_ref[...] = g
        d_ref[...], nm_ref[...], nv_ref[...] = _adamw(w_ref[...], g, m_ref[...], v_ref[...])

    row = pl.BlockSpec((tr, C), lambda i: (i, 0))
    return pl.pallas_call(
        body, grid=(R // tr,), name=name, in_specs=[pl.BlockSpec((N_DEV, tr, C), lambda i: (0, i, 0)), row, row, row],
        out_specs=[row] * 4, out_shape=[SDS((R, C), F32)] * 4, compiler_params=_cp(1))(parts, w, m, v)


def _pack_small(mix, ffn, fin, retw, dec_f, dec_b, loss):
    flat = jnp.concatenate([mix.reshape(-1), ffn.reshape(-1), fin.reshape(-1), retw.reshape(-1), dec_f.reshape(-1),
                            dec_b.reshape(-1), loss.reshape(-1)])
    return jnp.pad(flat, (0, SMALL_ROWS * 128 - flat.shape[0])).reshape(SMALL_ROWS, 128)


def _unpack_small(packed, shapes):
    flat = packed.reshape(-1)
    out, at = [], 0
    for s in shapes:
        n = math.prod(s)
        out.append(flat[at:at + n].reshape(s))
        at += n
    return out


def kernel(x, norm_mix_w, w_in, ret_decay_fwd, ret_decay_bwd, ret_norm_w, w_out, norm_ffn_w, w_gate, w_up, w_down, norm_final_w, loss_target, m_norm_mix_w, m_w_in, m_ret_decay_fwd, m_ret_decay_bwd, m_ret_norm_w, m_w_out, m_norm_ffn_w, m_w_gate, m_w_up, m_w_down, m_norm_final_w, v_norm_mix_w, v_w_in, v_ret_decay_fwd, v_ret_decay_bwd, v_ret_norm_w, v_w_out, v_norm_ffn_w, v_w_gate, v_w_up, v_w_down, v_norm_final_w):
    x2 = x[0]
    tgt = loss_target[0]
    S, D = x2.shape
    H = ret_norm_w.shape[1] // HEAD_DIM
    DA = H * HEAD_DIM
    fin_w = norm_final_w.reshape(1, D)
    big = (w_in[0], w_out[0], w_gate[0], w_up[0], w_down[0])

    wi, wo, wg, wu, wd = _exchange([w.astype(BF16) for w in big], True, "gather_weights")
    wo_full = wo.reshape(D, D)
    FB = wd.shape[1]
    wd_full = wd.reshape(N_DEV * FB, D)
    NB = wi.shape[2]

    proj, n1 = _proj_fwd(x2, norm_mix_w, wi)
    bias = _attn_bias()[:H]
    attn, lse = _attn_fwd(proj, bias)
    ret, o_raw = _ret_fwd(proj, ret_decay_fwd, ret_decay_bwd, ret_norm_w)
    h1, mixed, n2 = _out_fwd(x2, attn, ret, wo_full, norm_ffn_w)
    gate, up, act = _ffn_up(n2, wg, wu)
    dh2, loss_parts, g_fin = _ffn_down_loss(act, wd_full, h1, tgt, fin_w)

    dgate, dup = _ffn_bwd_act(dh2, wd_full, gate, up)
    tk = min(512, S)
    nk = S // tk
    g_wd = _wgrad(act, dh2, pl.BlockSpec((1, tk, FB), lambda j, k: (j, k, 0)), pl.BlockSpec((tk, D), lambda j, k: (k, 0)),
                  pl.BlockSpec((1, FB, D), lambda j, k: (j, 0, 0)), (N_DEV, FB, D), (N_DEV, nk), "wgrad_down")
    tmw = min(1024, D)
    gu_specs = (pl.BlockSpec((tk, tmw), lambda j, m, k: (k, m)), pl.BlockSpec((1, tk, FB), lambda j, m, k: (j, k, 0)),
                pl.BlockSpec((1, tmw, FB), lambda j, m, k: (j, m, 0)), (N_DEV, D, FB), (N_DEV, D // tmw, nk))
    g_wg = _wgrad(n2, dgate, *gu_specs, "wgrad_gate")
    g_wu = _wgrad(n2, dup, *gu_specs, "wgrad_up")
    dh1, g_ffn = _ffn_bwd_in(dgate, dup, wg, wu, h1, dh2, norm_ffn_w)
    dmix = _dmix(dh1, wo_full)
    g_wo = _wgrad(mixed, dh1, pl.BlockSpec((tk, tmw), lambda m, k: (k, m)), pl.BlockSpec((tk, D), lambda m, k: (k, 0)),
                  pl.BlockSpec((tmw, D), lambda m, k: (m, 0)), (D, D), (D // tmw, nk), "wgrad_out")
    dq_r, dk_r, dv_r, dg_r, small = _ret_bwd(proj, o_raw, dmix, ret_decay_fwd, ret_decay_bwd, ret_norm_w, DA)
    dq_a, dk_a, dv_a = _attn_bwd(proj, attn, lse, dmix, bias)
    dproj = jnp.concatenate([t.astype(BF16) for t in (dq_a, dk_a, dv_a, dq_r, dk_r, dv_r, dg_r)], axis=1)
    g_wi = _wgrad(n1, dproj, pl.BlockSpec((tk, tmw), lambda j, m, k: (k, m)), pl.BlockSpec((tk, NB), lambda j, m, k: (k, j)),
                  pl.BlockSpec((1, tmw, NB), lambda j, m, k: (j, m, 0)), (N_DEV, D, NB), (N_DEV, D // tmw, nk), "wgrad_in")
    grad_x, g_mix = _in_bwd(dproj, wi, x2, dh1, norm_mix_w)

    parts = _exchange([g_wi, g_wo.reshape(N_DEV, D // N_DEV, D), g_wg, g_wu, g_wd], False, "scatter_grads")
    big_m = (m_w_in[0], m_w_out[0], m_w_gate[0], m_w_up[0], m_w_down[0])
    big_v = (v_w_in[0], v_w_out[0], v_w_gate[0], v_w_up[0], v_w_down[0])
    names = ("adamw_in", "adamw_out", "adamw_gate", "adamw_up", "adamw_down")
    upd = [_adamw_block(p, w, m, v, nm) for p, w, m, v, nm in zip(parts, big, big_m, big_v, names)]

    g_dec_f = small[:, 0, 0].reshape(1, H)
    g_dec_b = small[:, 1, 0].reshape(1, H)
    g_retw = small[:, 2, :].reshape(1, DA)
    loss_local = jnp.sum(loss_parts[::8, 0])
    zero = jnp.zeros((1,), F32)
    part = _pack_small(g_mix, g_ffn, g_fin, g_retw, g_dec_f, g_dec_b, loss_local)
    sw = _pack_small(norm_mix_w, norm_ffn_w, norm_final_w, ret_norm_w, ret_decay_fwd, ret_decay_bwd, zero)
    sm = _pack_small(m_norm_mix_w, m_norm_ffn_w, m_norm_final_w, m_ret_norm_w, m_ret_decay_fwd, m_ret_decay_bwd, zero)
    sv = _pack_small(v_norm_mix_w, v_norm_ffn_w, v_norm_final_w, v_ret_norm_w, v_ret_decay_fwd, v_ret_decay_bwd, zero)
    shapes = [(1, D), (1, D), (D,), (1, DA), (1, H), (1, H), ()]
    sg, sd, snm, snv = [_unpack_small(t, shapes) for t in _small_step(part, sw, sm, sv)]
    loss = sg[6]

    def ordered(small_set, k):
        b = [u[k][None] for u in upd]
        return [small_set[0], b[0], small_set[4], small_set[5], small_set[3], b[1], small_set[1], b[2], b[3], b[4],
                small_set[2]]

    return (loss, grad_x[None], *ordered(sg, 0), *ordered(sd, 1), *ordered(snm, 2), *ordered(snv, 3))
```

```python
import functools
import math

import numpy as np
import jax
import jax.numpy as jnp
from jax import lax
from jax.experimental import pallas as pl
from jax.experimental.pallas import tpu as pltpu

F32 = jnp.float32
BF16 = jnp.bfloat16
SDS = jax.ShapeDtypeStruct

HEAD_DIM = 128
EPS = 1e-6
RET_CHUNK = 128
DILATIONS = (1, 4, 16)
BAND = 64
Q_TILE = 128
K_TILE = Q_TILE + 2 * BAND
KV_PAD = BAND * max(DILATIONS)
NEG = -1e30
N_DEV = 8
N_GROUPS = 7
ADAM_LR, ADAM_B1, ADAM_B2, ADAM_EPS, ADAM_WD, ADAM_STEP = 0.001, 0.9, 0.999, 1e-08, 0.01, 10
VMEM_LIMIT = 56 * 1024 * 1024
MESH = pl.DeviceIdType.MESH
ANY = pl.BlockSpec(memory_space=pl.ANY)


def _cp(n_grid):
    return pltpu.CompilerParams(dimension_semantics=("arbitrary",) * n_grid, vmem_limit_bytes=VMEM_LIMIT)


def _sigmoid(x):
    return 1.0 / (1.0 + jnp.exp(-x))


def _rms_scale(h):
    return lax.rsqrt(jnp.mean(h * h, axis=-1, keepdims=True) + EPS)


def _rms_bwd(dn, h, w):
    r = _rms_scale(h)
    gw = dn * w
    dh = r * gw - h * (r * r * r) * jnp.mean(gw * h, axis=-1, keepdims=True)
    return dh, jnp.sum(dn * h * r, axis=0, keepdims=True)


def _dot(a, b, dims):
    return lax.dot_general(a.astype(BF16), b.astype(BF16), (dims, ((), ())), preferred_element_type=F32)


_NN = ((1,), (0,))
_NT = ((1,), (1,))
_TN = ((0,), (0,))


def _proj_fwd(x, w_norm, w_blk):
    S, D = x.shape
    nblk, _, NB = w_blk.shape
    tm = min(512, S)

    def body(x_ref, wn_ref, w_ref, proj_ref, n_ref, n_scr):
        @pl.when(pl.program_id(1) == 0)
        def _():
            xf = x_ref[...]
            nb = (xf * _rms_scale(xf) * wn_ref[...]).astype(BF16)
            n_scr[...] = nb
            n_ref[...] = nb
        proj_ref[...] = jnp.dot(n_scr[...], w_ref[0], preferred_element_type=F32)

    return pl.pallas_call(
        body, grid=(S // tm, nblk), name="proj_fwd",
        in_specs=[pl.BlockSpec((tm, D), lambda i, j: (i, 0)), pl.BlockSpec((1, D), lambda i, j: (0, 0)),
                  pl.BlockSpec((1, D, NB), lambda i, j: (j, 0, 0))],
        out_specs=[pl.BlockSpec((tm, NB), lambda i, j: (i, j)), pl.BlockSpec((tm, D), lambda i, j: (i, 0))],
        out_shape=[SDS((S, nblk * NB), F32), SDS((S, D), BF16)],
        scratch_shapes=[pltpu.VMEM((tm, D), BF16)], compiler_params=_cp(2))(x, w_norm, w_blk)


def _out_fwd(x, attn, ret, w_out, w_norm):
    S, D = x.shape
    DA = attn.shape[1]
    tm = min(256, S)

    def body(x_ref, a_ref, r_ref, w_ref, wn_ref, h_ref, mix_ref, n_ref):
        a = a_ref[...].astype(BF16)
        r = r_ref[...].astype(BF16)
        mix_ref[:, :DA] = a
        mix_ref[:, DA:] = r
        h = x_ref[...] + jnp.dot(a, w_ref[:DA, :], preferred_element_type=F32) \
            + jnp.dot(r, w_ref[DA:, :], preferred_element_type=F32)
        h_ref[...] = h
        n_ref[...] = (h * _rms_scale(h) * wn_ref[...]).astype(BF16)

    row = lambda w: pl.BlockSpec((tm, w), lambda i: (i, 0))
    return pl.pallas_call(
        body, grid=(S // tm,), name="out_fwd",
        in_specs=[row(D), row(DA), row(D - DA), pl.BlockSpec((D, D), lambda i: (0, 0)),
                  pl.BlockSpec((1, D), lambda i: (0, 0))],
        out_specs=[row(D), row(D), row(D)],
        out_shape=[SDS((S, D), F32), SDS((S, D), BF16), SDS((S, D), BF16)],
        compiler_params=_cp(1))(x, attn, ret, w_out, w_norm)


def _ffn_up(n2, wg, wu):
    S, D = n2.shape
    nblk, _, FB = wg.shape
    tm = min(512, S)

    def body(n_ref, wg_ref, wu_ref, g_ref, u_ref, a_ref):
        n = n_ref[...]
        g = jnp.dot(n, wg_ref[0], preferred_element_type=F32)
        u = jnp.dot(n, wu_ref[0], preferred_element_type=F32)
        g_ref[0] = g
        u_ref[0] = u
        a_ref[0] = (g * _sigmoid(g) * u).astype(BF16)

    wspec = pl.BlockSpec((1, D, FB), lambda j, i: (j, 0, 0))
    ospec = pl.BlockSpec((1, tm, FB), lambda j, i: (j, i, 0))
    return pl.pallas_call(
        body, grid=(nblk, S // tm), name="ffn_up",
        in_specs=[pl.BlockSpec((tm, D), lambda j, i: (i, 0)), wspec, wspec],
        out_specs=[ospec, ospec, ospec],
        out_shape=[SDS((nblk, S, FB), F32), SDS((nblk, S, FB), F32), SDS((nblk, S, FB), BF16)],
        compiler_params=_cp(2))(n2, wg, wu)


def _ffn_down_loss(act, wd, h1, target, w_norm):
    nblk, S, FB = act.shape
    D = h1.shape[1]
    tm = min(512, S)

    def body(a_ref, wd_ref, h_ref, t_ref, wn_ref, dh_ref, loss_ref, dw_ref, acc):
        i, j = pl.program_id(0), pl.program_id(1)

        @pl.when(j == 0)
        def _():
            acc[...] = h_ref[...]

        @pl.when((i == 0) & (j == 0))
        def _():
            dw_ref[...] = jnp.zeros_like(dw_ref)

        acc[...] += jnp.dot(a_ref[0], wd_ref[...], preferred_element_type=F32)

        @pl.when(j == nblk - 1)
        def _():
            h = acc[...]
            w = wn_ref[...]
            err = h * _rms_scale(h) * w - t_ref[...]
            loss_ref[...] = jnp.full(loss_ref.shape, 0.5 * jnp.sum(err * err) / D, F32)
            dh, dw = _rms_bwd(err * (1.0 / D), h, w)
            dh_ref[...] = dh
            dw_ref[...] += dw

    row = pl.BlockSpec((tm, D), lambda i, j: (i, 0))
    vec = pl.BlockSpec((1, D), lambda i, j: (0, 0))
    return pl.pallas_call(
        body, grid=(S // tm, nblk), name="ffn_down_loss",
        in_specs=[pl.BlockSpec((1, tm, FB), lambda i, j: (j, i, 0)), pl.BlockSpec((FB, D), lambda i, j: (j, 0)),
                  row, row, vec],
        out_specs=[row, pl.BlockSpec((8, 128), lambda i, j: (i, 0)), vec],
        out_shape=[SDS((S, D), F32), SDS((S // tm * 8, 128), F32), SDS((1, D), F32)],
        scratch_shapes=[pltpu.VMEM((tm, D), F32)], compiler_params=_cp(2))(act, wd, h1, target, w_norm)


def _attn_bias():
    n_heads = 8
    slopes = np.exp2(-8.0 * np.arange(1, n_heads + 1, dtype=np.float32) / n_heads)
    dist = np.abs(np.arange(K_TILE)[None, :] - BAND - np.arange(Q_TILE)[:, None])
    out = np.empty((n_heads, len(DILATIONS), Q_TILE, K_TILE), np.float32)
    for h in range(n_heads):
        for p, d in enumerate(DILATIONS):
            out[h, p] = np.where(dist <= BAND, -slopes[h] * (d * dist).astype(np.float32), NEG)
    return jnp.asarray(out)


def _attn_tiles(S, d):
    L = S // d
    per_class = L // Q_TILE
    return L, per_class, d * per_class


def _tile_rows(t, d, per_class):
    r = t // per_class
    a = (t % per_class) * Q_TILE
    q_rows = pl.ds(r + d * a, Q_TILE, stride=d) if d > 1 else pl.ds(pl.multiple_of(a, Q_TILE), Q_TILE)
    k_rows = pl.ds(KV_PAD + r + d * (a - BAND), K_TILE, stride=d) if d > 1 else pl.ds(
        pl.multiple_of(KV_PAD + a - BAND, BAND), K_TILE)
    return a, q_rows, k_rows


def _edge_mask(a, L):
    lk = lax.broadcasted_iota(jnp.int32, (1, K_TILE), 1) + (a - BAND)
    return jnp.where((lk >= 0) & (lk < L), 0.0, NEG).astype(F32)


def _fill_padded(dst, src, S):
    dst[pl.ds(0, KV_PAD), :] = jnp.zeros((KV_PAD, HEAD_DIM), F32)
    dst[pl.ds(KV_PAD + S, KV_PAD), :] = jnp.zeros((KV_PAD, HEAD_DIM), F32)
    dst[pl.ds(KV_PAD, S), :] = src[...]


def _head_specs(S, groups, n_heads):
    return [pl.BlockSpec((S, HEAD_DIM), functools.partial(lambda h, g: (0, g * n_heads + h), g=g)) for g in groups]


def _attn_fwd(proj, bias):
    S = proj.shape[0]
    H = proj.shape[1] // (N_GROUPS * HEAD_DIM)
    scale = HEAD_DIM ** -0.5

    def body(q_ref, k_ref, v_ref, b_ref, o_ref, lse_ref, kp, vp, m_run, l_run):
        _fill_padded(kp, k_ref, S)
        _fill_padded(vp, v_ref, S)
        o_ref[...] = jnp.zeros_like(o_ref)
        m_run[...] = jnp.full(m_run.shape, NEG, F32)
        l_run[...] = jnp.zeros_like(l_run)
        for p, d in enumerate(DILATIONS):
            L, per_class, n_tiles = _attn_tiles(S, d)

            def tile(t, carry, p=p, d=d, L=L, per_class=per_class):
                a, q_rows, k_rows = _tile_rows(t, d, per_class)
                s = _dot(q_ref[q_rows, :], kp[k_rows, :], _NT) * scale + b_ref[0, p] + _edge_mask(a, L)
                m_old = m_run[q_rows, :][:, :1]
                m_new = jnp.maximum(m_old, jnp.max(s, axis=-1, keepdims=True))
                pr = jnp.exp(s - m_new)
                alpha = jnp.exp(m_old - m_new)
                l_new = alpha * l_run[q_rows, :][:, :1] + jnp.sum(pr, axis=-1, keepdims=True)
                o_ref[q_rows, :] = alpha * o_ref[q_rows, :] + _dot(pr, vp[k_rows, :], _NN)
                m_run[q_rows, :] = jnp.broadcast_to(m_new, (Q_TILE, HEAD_DIM))
                l_run[q_rows, :] = jnp.broadcast_to(l_new, (Q_TILE, HEAD_DIM))
                return carry

            lax.fori_loop(0, n_tiles, tile, 0)
        l = l_run[...]
        o_ref[...] = o_ref[...] / l
        lse_ref[...] = m_run[...] + jnp.log(l)

    hspec = pl.BlockSpec((S, HEAD_DIM), lambda h: (0, h))
    return pl.pallas_call(
        body, grid=(H,), name="attn_fwd",
        in_specs=_head_specs(S, (0, 1, 2), H) + [
            pl.BlockSpec((1, len(DILATIONS), Q_TILE, K_TILE), lambda h: (h, 0, 0, 0))],
        out_specs=[hspec, hspec],
        out_shape=[SDS((S, H * HEAD_DIM), F32), SDS((S, H * HEAD_DIM), F32)],
        scratch_shapes=[pltpu.VMEM((S + 2 * KV_PAD, HEAD_DIM), F32), pltpu.VMEM((S + 2 * KV_PAD, HEAD_DIM), F32),
                        pltpu.VMEM((S, HEAD_DIM), F32), pltpu.VMEM((S, HEAD_DIM), F32)],
        compiler_params=_cp(1))(proj, proj, proj, bias)


def _attn_bwd(proj, out, lse, dmix, bias):
    S = proj.shape[0]
    H = proj.shape[1] // (N_GROUPS * HEAD_DIM)
    scale = HEAD_DIM ** -0.5

    def body(q_ref, k_ref, v_ref, o_ref, lse_ref, do_ref, b_ref, dq_ref, dk_ref, dv_ref, kp, vp, dkp, dvp, dsum):
        _fill_padded(kp, k_ref, S)
        _fill_padded(vp, v_ref, S)
        dkp[...] = jnp.zeros_like(dkp)
        dvp[...] = jnp.zeros_like(dvp)
        dq_ref[...] = jnp.zeros_like(dq_ref)
        dsum[...] = jnp.broadcast_to(jnp.sum(do_ref[...] * o_ref[...], axis=-1, keepdims=True), dsum.shape)
        for p, d in enumerate(DILATIONS):
            L, per_class, n_tiles = _attn_tiles(S, d)

            def tile(t, carry, p=p, d=d, L=L, per_class=per_class):
                a, q_rows, k_rows = _tile_rows(t, d, per_class)
                qt = q_ref[q_rows, :]
                kt = kp[k_rows, :]
                dot_ = do_ref[q_rows, :]
                s = _dot(qt, kt, _NT) * scale + b_ref[0, p] + _edge_mask(a, L)
                pr = jnp.exp(s - lse_ref[q_rows, :][:, :1])
                dp = _dot(dot_, vp[k_rows, :], _NT)
                ds = pr * (dp - dsum[q_rows, :][:, :1]) * scale
                dq_ref[q_rows, :] = dq_ref[q_rows, :] + _dot(ds, kt, _NN)
                dkp[k_rows, :] = dkp[k_rows, :] + _dot(ds, qt, _TN)
                dvp[k_rows, :] = dvp[k_rows, :] + _dot(pr, dot_, _TN)
                return carry

            lax.fori_loop(0, n_tiles, tile, 0)
        dk_ref[...] = dkp[pl.ds(KV_PAD, S), :]
        dv_ref[...] = dvp[pl.ds(KV_PAD, S), :]

    hspec = pl.BlockSpec((S, HEAD_DIM), lambda h: (0, h))
    padded = pltpu.VMEM((S + 2 * KV_PAD, HEAD_DIM), F32)
    return pl.pallas_call(
        body, grid=(H,), name="attn_bwd",
        in_specs=_head_specs(S, (0, 1, 2), H) + [hspec, hspec, hspec,
                                                  pl.BlockSpec((1, len(DILATIONS), Q_TILE, K_TILE), lambda h: (h, 0, 0, 0))],
        out_specs=[hspec, hspec, hspec],
        out_shape=[SDS((S, H * HEAD_DIM), F32)] * 3,
        scratch_shapes=[padded, padded, padded, padded, pltpu.VMEM((S, HEAD_DIM), F32)],
        compiler_params=_cp(1))(proj, proj, proj, out, lse, dmix, bias)


def _ret_consts(lg, forward):
    C = RET_CHUNK
    i = lax.broadcasted_iota(jnp.int32, (C, C), 0)
    j = lax.broadcasted_iota(jnp.int32, (C, C), 1)
    rel = (i - j) if forward else (j - i)
    inside = (rel >= 0) if forward else (rel > 0)
    relf = jnp.maximum(rel, 0).astype(F32)
    mask = jnp.where(inside, jnp.exp(lg * relf), 0.0)
    idx = lax.broadcasted_iota(jnp.int32, (C, 1), 0).astype(F32)
    q_exp = (idx + 1.0) if forward else (C - idx)
    k_exp = (C - 1.0 - idx) if forward else idx
    return mask, relf, jnp.exp(lg * q_exp), q_exp, jnp.exp(lg * k_exp), k_exp, jnp.exp(lg * C)


def _log_decay(dec_ref, h):
    return -jnp.exp(jnp.full((1, 1), dec_ref[0, h], F32))


def _chunk(c):
    return pl.ds(pl.multiple_of(c * RET_CHUNK, RET_CHUNK), RET_CHUNK)


def _ret_fwd(proj, dec_f, dec_b, w_norm):
    S = proj.shape[0]
    H = proj.shape[1] // (N_GROUPS * HEAD_DIM)
    nc = S // RET_CHUNK
    scale = HEAD_DIM ** -0.5

    def body(df_ref, db_ref, q_ref, k_ref, v_ref, g_ref, w_ref, y_ref, o_ref):
        h = pl.program_id(0)
        for forward, dref in ((True, df_ref), (False, db_ref)):
            mask, _, q_dec, _, k_dec, _, c_dec = _ret_consts(_log_decay(dref, h), forward)

            def step(n, state, forward=forward, mask=mask, q_dec=q_dec, k_dec=k_dec, c_dec=c_dec):
                rows = _chunk(n if forward else nc - 1 - n)
                qc = q_ref[rows, :] * scale
                kc = k_ref[rows, :]
                vc = v_ref[rows, :]
                o = _dot(_dot(qc, kc, _NT) * mask, vc, _NN) + _dot(qc * q_dec, state, _NN)
                if forward:
                    o_ref[rows, :] = o
                else:
                    o_ref[rows, :] = o_ref[rows, :] + o
                return state * c_dec + _dot(kc * k_dec, vc, _TN)

            lax.fori_loop(0, nc, step, jnp.zeros((HEAD_DIM, HEAD_DIM), F32))
        o = o_ref[...]
        g = g_ref[...]
        y_ref[...] = o * _rms_scale(o) * w_ref[...] * (g * _sigmoid(g))

    hspec = pl.BlockSpec((S, HEAD_DIM), lambda h: (0, h))
    smem = pl.BlockSpec(memory_space=pltpu.SMEM)
    return pl.pallas_call(
        body, grid=(H,), name="ret_fwd",
        in_specs=[smem, smem] + _head_specs(S, (3, 4, 5, 6), H) + [pl.BlockSpec((1, HEAD_DIM), lambda h: (0, h))],
        out_specs=[hspec, hspec],
        out_shape=[SDS((S, H * HEAD_DIM), F32)] * 2,
        compiler_params=_cp(1))(dec_f, dec_b, proj, proj, proj, proj, w_norm)


def _ret_bwd(proj, o_raw, dmix, dec_f, dec_b, w_norm, col0):
    S = proj.shape[0]
    H = proj.shape[1] // (N_GROUPS * HEAD_DIM)
    C = RET_CHUNK
    nc = S // C
    scale = HEAD_DIM ** -0.5

    def body(df_ref, db_ref, q_ref, k_ref, v_ref, g_ref, o_ref, dy_ref, w_ref,
             dq_ref, dk_ref, dv_ref, dg_ref, small_ref, do, states):
        h = pl.program_id(0)
        o = o_ref[...]
        g = g_ref[...]
        dy = dy_ref[...]
        w = w_ref[...]
        rr = _rms_scale(o)
        normed = o * rr
        sg = _sigmoid(g)
        silu = g * sg
        small_ref[0, pl.ds(2, 1), :] = jnp.sum(dy * normed * silu, axis=0, keepdims=True)
        dg_ref[...] = dy * normed * w * (sg * (1.0 + g * (1.0 - sg)))
        dnormed = dy * w * silu
        do[...] = rr * dnormed - o * (rr * rr * rr) * jnp.mean(dnormed * o, axis=-1, keepdims=True)

        for forward, dref, row in ((True, df_ref, 0), (False, db_ref, 1)):
            lg = _log_decay(dref, h)
            mask, relf, q_dec, q_exp, k_dec, k_exp, c_dec = _ret_consts(lg, forward)

            def fwd_step(n, state, forward=forward, k_dec=k_dec, c_dec=c_dec):
                cidx = n if forward else nc - 1 - n
                rows = _chunk(cidx)
                states[cidx] = state
                return state * c_dec + _dot(k_ref[rows, :] * k_dec, v_ref[rows, :], _TN)

            lax.fori_loop(0, nc, fwd_step, jnp.zeros((HEAD_DIM, HEAD_DIM), F32))

            def bwd_step(n, carry, forward=forward, mask=mask, relf=relf, q_dec=q_dec, q_exp=q_exp,
                         k_dec=k_dec, k_exp=k_exp, c_dec=c_dec):
                d_state, dlam = carry
                cidx = (nc - 1 - n) if forward else n
                rows = _chunk(cidx)
                qc = q_ref[rows, :] * scale
                kc = k_ref[rows, :]
                vc = v_ref[rows, :]
                doc = do[rows, :]
                state = states[cidx]
                a0 = _dot(qc, kc, _NT)
                dp = _dot(doc, vc, _NT) * mask
                gq = _dot(doc, state, _NT)
                gk = _dot(vc, d_state, _NT)
                dq = _dot(dp, kc, _NN) + q_dec * gq
                dk = _dot(dp, qc, _TN) + k_dec * gk
                dv = _dot(a0 * mask, doc, _TN) + _dot(kc * k_dec, d_state, _NN)
                if forward:
                    dq_ref[rows, :] = dq * scale
                    dk_ref[rows, :] = dk
                    dv_ref[rows, :] = dv
                else:
                    dq_ref[rows, :] = dq_ref[rows, :] + dq * scale
                    dk_ref[rows, :] = dk_ref[rows, :] + dk
                    dv_ref[rows, :] = dv_ref[rows, :] + dv
                dlam = dlam + jnp.sum(relf * a0 * dp, axis=0, keepdims=True) \
                    + jnp.sum(q_exp * q_dec * qc * gq + k_exp * k_dec * kc * gk, axis=0, keepdims=True) \
                    + (C * c_dec) * jnp.sum(state * d_state, axis=0, keepdims=True)
                return d_state * c_dec + _dot(qc * q_dec, doc, _TN), dlam

            _, dlam = lax.fori_loop(0, nc, bwd_step, (jnp.zeros((HEAD_DIM, HEAD_DIM), F32), jnp.zeros((1, HEAD_DIM), F32)))
            small_ref[0, pl.ds(row, 1), :] = jnp.broadcast_to(jnp.sum(dlam, axis=-1, keepdims=True) * lg, (1, HEAD_DIM))
        small_ref[0, pl.ds(3, 5), :] = jnp.zeros((5, HEAD_DIM), F32)

    hspec = pl.BlockSpec((S, HEAD_DIM), lambda h: (0, h))
    smem = pl.BlockSpec(memory_space=pltpu.SMEM)
    nh0 = col0 // HEAD_DIM
    return pl.pallas_call(
        body, grid=(H,), name="ret_bwd",
        in_specs=[smem, smem] + _head_specs(S, (3, 4, 5, 6), H) + [
            hspec, pl.BlockSpec((S, HEAD_DIM), lambda h: (0, nh0 + h)), pl.BlockSpec((1, HEAD_DIM), lambda h: (0, h))],
        out_specs=[hspec, hspec, hspec, hspec, pl.BlockSpec((1, 8, HEAD_DIM), lambda h: (h, 0, 0))],
        out_shape=[SDS((S, H * HEAD_DIM), F32)] * 4 + [SDS((H, 8, HEAD_DIM), F32)],
        scratch_shapes=[pltpu.VMEM((S, HEAD_DIM), F32), pltpu.VMEM((nc, HEAD_DIM, HEAD_DIM), F32)],
        compiler_params=_cp(1))(dec_f, dec_b, proj, proj, proj, proj, o_raw, dmix, w_norm)


def _ffn_bwd_act(dh2, wd, g, u):
    S, D = dh2.shape
    nblk, _, FB = g.shape
    tm = min(512, S)

    def body(dh_ref, wd_ref, g_ref, u_ref, dg_ref, du_ref):
        dact = _dot(dh_ref[...], wd_ref[...], _NT)
        gg = g_ref[0]
        sg = _sigmoid(gg)
        dg_ref[0] = (dact * u_ref[0] * (sg * (1.0 + gg * (1.0 - sg)))).astype(BF16)
        du_ref[0] = (dact * (gg * sg)).astype(BF16)

    blk = pl.BlockSpec((1, tm, FB), lambda j, i: (j, i, 0))
    return pl.pallas_call(
        body, grid=(nblk, S // tm), name="ffn_bwd_act",
        in_specs=[pl.BlockSpec((tm, D), lambda j, i: (i, 0)), pl.BlockSpec((FB, D), lambda j, i: (j, 0)), blk, blk],
        out_specs=[blk, blk], out_shape=[SDS((nblk, S, FB), BF16)] * 2,
        compiler_params=_cp(2))(dh2, wd, g, u)


def _ffn_bwd_in(dg, du, wg, wu, h1, dh2, w_norm):
    nblk, S, FB = dg.shape
    D = h1.shape[1]
    tm = min(256, S)

    def body(dg_ref, du_ref, wg_ref, wu_ref, h_ref, dh2_ref, wn_ref, dh_ref, dw_ref, acc):
        i, j = pl.program_id(0), pl.program_id(1)

        @pl.when(j == 0)
        def _():
            acc[...] = jnp.zeros_like(acc)

        @pl.when((i == 0) & (j == 0))
        def _():
            dw_ref[...] = jnp.zeros_like(dw_ref)

        acc[...] += _dot(dg_ref[0], wg_ref[0], _NT) + _dot(du_ref[0], wu_ref[0], _NT)

        @pl.when(j == nblk - 1)
        def _():
            dh, dw = _rms_bwd(acc[...], h_ref[...], wn_ref[...])
            dh_ref[...] = dh2_ref[...] + dh
            dw_ref[...] += dw

    blk = pl.BlockSpec((1, tm, FB), lambda i, j: (j, i, 0))
    wspec = pl.BlockSpec((1, D, FB), lambda i, j: (j, 0, 0))
    row = pl.BlockSpec((tm, D), lambda i, j: (i, 0))
    vec = pl.BlockSpec((1, D), lambda i, j: (0, 0))
    return pl.pallas_call(
        body, grid=(S // tm, nblk), name="ffn_bwd_in",
        in_specs=[blk, blk, wspec, wspec, row, row, vec],
        out_specs=[row, vec], out_shape=[SDS((S, D), F32), SDS((1, D), F32)],
        scratch_shapes=[pltpu.VMEM((tm, D), F32)], compiler_params=_cp(2))(dg, du, wg, wu, h1, dh2, w_norm)


def _dmix(dh1, w_out):
    S, D = dh1.shape
    tm = min(512, S)

    def body(dh_ref, w_ref, o_ref):
        o_ref[...] = _dot(dh_ref[...], w_ref[...], _NT)

    row = pl.BlockSpec((tm, D), lambda i: (i, 0))
    return pl.pallas_call(
        body, grid=(S // tm,), name="dmix", in_specs=[row, pl.BlockSpec((D, D), lambda i: (0, 0))],
        out_specs=row, out_shape=SDS((S, D), F32), compiler_params=_cp(1))(dh1, w_out)


def _in_bwd(dproj, w_blk, x, dh1, w_norm):
    S, D = x.shape
    nblk, _, NB = w_blk.shape
    tm = min(512, S)

    def body(dp_ref, w_ref, x_ref, dh1_ref, wn_ref, dx_ref, dw_ref, acc):
        i, j = pl.program_id(0), pl.program_id(1)

        @pl.when(j == 0)
        def _():
            acc[...] = jnp.zeros_like(acc)

        @pl.when((i == 0) & (j == 0))
        def _():
            dw_ref[...] = jnp.zeros_like(dw_ref)

        acc[...] += _dot(dp_ref[...], w_ref[0], _NT)

        @pl.when(j == nblk - 1)
        def _():
            dh, dw = _rms_bwd(acc[...], x_ref[...], wn_ref[...])
            dx_ref[...] = dh1_ref[...] + dh
            dw_ref[...] += dw

    row = pl.BlockSpec((tm, D), lambda i, j: (i, 0))
    vec = pl.BlockSpec((1, D), lambda i, j: (0, 0))
    return pl.pallas_call(
        body, grid=(S // tm, nblk), name="in_bwd",
        in_specs=[pl.BlockSpec((tm, NB), lambda i, j: (i, j)), pl.BlockSpec((1, D, NB), lambda i, j: (j, 0, 0)),
                  row, row, vec],
        out_specs=[row, vec], out_shape=[SDS((S, D), F32), SDS((1, D), F32)],
        scratch_shapes=[pltpu.VMEM((tm, D), F32)], compiler_params=_cp(2))(dproj, w_blk, x, dh1, w_norm)


def _wgrad(a, b, a_spec, b_spec, o_spec, o_shape, grid, name):
    nk = grid[-1]

    def ld(ref):
        return ref[0] if len(ref.shape) == 3 else ref[...]

    def body(a_ref, b_ref, o_ref, acc):
        k = pl.program_id(len(grid) - 1)

        @pl.when(k == 0)
        def _():
            acc[...] = jnp.zeros_like(acc)

        acc[...] += _dot(ld(a_ref), ld(b_ref), _TN)

        @pl.when(k == nk - 1)
        def _():
            if len(o_ref.shape) == 3:
                o_ref[0] = acc[...].astype(o_ref.dtype)
            else:
                o_ref[...] = acc[...].astype(o_ref.dtype)

    return pl.pallas_call(
        body, grid=grid, name=name, in_specs=[a_spec, b_spec], out_specs=o_spec, out_shape=SDS(o_shape, BF16),
        scratch_shapes=[pltpu.VMEM(o_spec.block_shape[-2:], F32)], compiler_params=_cp(len(grid)))(a, b)


def _peer(k):
    x, y, c = lax.axis_index("x"), lax.axis_index("y"), lax.axis_index("c")
    px = 1 - x if k & 4 else x
    py = 1 - y if k & 2 else y
    pc = 1 - c if k & 1 else c
    return (px, py, pc), 4 * px + 2 * py + pc


HBM = pl.BlockSpec(memory_space=pltpu.HBM)
SEM = pl.BlockSpec(memory_space=pltpu.SEMAPHORE)
EFFECT = pltpu.SideEffectType.DATAFLOW_SIDE_EFFECTING


def _exchange_copies(srcs, lands, send_sems, recv_sems, which, gather):
    _, me = _peer(0)
    pairs = []
    for pos, a in enumerate(which):
        for k in range(1, N_DEV):
            dev, idx = _peer(k)
            sem = pos * (N_DEV - 1) + k - 1
            src = srcs[a] if gather else srcs[a].at[idx]
            mk = functools.partial(pltpu.make_async_remote_copy, src_ref=src, send_sem=send_sems.at[sem],
                                   recv_sem=recv_sems.at[sem], device_id=dev, device_id_type=MESH)
            pairs.append((mk(dst_ref=lands[a].at[me]), mk(dst_ref=lands[a].at[idx])))
    return pairs


def _exchange_start(arrays, gather, groups, name):
    n, ng = len(arrays), len(groups)
    lands = [lax.empty((N_DEV,) + a.shape if gather else a.shape, a.dtype) for a in arrays]

    def body(*refs):
        srcs, dsts = refs[:n], refs[n:2 * n]
        sems = refs[2 * n:2 * n + 2 * ng]
        token, local_sems = refs[-2], refs[-1]
        _, me = _peer(0)
        local = [pltpu.make_async_copy(srcs[a] if gather else srcs[a].at[me], dsts[a].at[me], local_sems.at[a])
                 for a in range(n)]
        for g, which in enumerate(groups):
            for out, _ in _exchange_copies(srcs, dsts, sems[2 * g], sems[2 * g + 1], which, gather):
                out.start()
        for cp in local:
            cp.start()
        for cp in local:
            cp.wait()
        token[...] = jnp.zeros_like(token)

    sem_shapes = []
    for which in groups:
        sem_shapes += [pltpu.SemaphoreType.DMA((len(which) * (N_DEV - 1),))] * 2
    thru = [pltpu.HBM(a.shape, a.dtype) for a in arrays] + [pltpu.HBM(l.shape, l.dtype) for l in lands]
    outs = pl.pallas_call(
        body, name=name, in_specs=[HBM] * (2 * n),
        out_specs=[SEM] * (2 * ng) + [HBM] * (2 * n) + [pl.BlockSpec(memory_space=pltpu.VMEM)],
        out_shape=sem_shapes + thru + [SDS((8, 128), F32)],
        input_output_aliases={i: 2 * ng + i for i in range(2 * n)},
        scratch_shapes=[pltpu.SemaphoreType.DMA((n,))],
        compiler_params=pltpu.CompilerParams(has_side_effects=EFFECT),
    )(*[pltpu.with_memory_space_constraint(t, pltpu.HBM) for t in list(arrays) + lands])
    sems = [(outs[2 * g], outs[2 * g + 1]) for g in range(ng)]
    return sems, list(outs[2 * ng:2 * ng + n]), list(outs[2 * ng + n:2 * ng + 2 * n]), outs[-1]


def _exchange_wait(sems, srcs, lands, gather, after, name):
    n = len(srcs)
    send_sems, recv_sems = sems

    def body(*refs):
        s, d = refs[:n], refs[n:2 * n]
        for out, arrival in _exchange_copies(s, d, refs[2 * n], refs[2 * n + 1], range(n), gather):
            out.wait_send()
            arrival.wait_recv()

    outs = pl.pallas_call(
        body, name=name, in_specs=[HBM] * (2 * n) + [SEM, SEM, ANY], out_specs=[HBM] * (2 * n),
        out_shape=[pltpu.HBM(t.shape, t.dtype) for t in list(srcs) + list(lands)],
        input_output_aliases={i: i for i in range(2 * n)},
        compiler_params=pltpu.CompilerParams(has_side_effects=EFFECT),
    )(*srcs, *lands, send_sems, recv_sems, after)
    return list(outs[n:])


SMALL_ROWS = 64


def _small_step(part, w, m, v):
    def body(p_ref, w_ref, m_ref, v_ref, g_ref, d_ref, nm_ref, nv_ref, gath, send_sems, recv_sems):
        _, me = _peer(0)
        gath[me] = p_ref[...]
        copies = []
        for k in range(1, N_DEV):
            dev, idx = _peer(k)
            out = pltpu.make_async_remote_copy(src_ref=p_ref, dst_ref=gath.at[me], send_sem=send_sems.at[k - 1],
                                               recv_sem=recv_sems.at[k - 1], device_id=dev, device_id_type=MESH)
            out.start()
            arrival = pltpu.make_async_remote_copy(src_ref=p_ref, dst_ref=gath.at[idx], send_sem=send_sems.at[k - 1],
                                                   recv_sem=recv_sems.at[k - 1], device_id=dev, device_id_type=MESH)
            copies.append((out, arrival))
        for out, arrival in copies:
            out.wait_send()
            arrival.wait_recv()
        g = gath[0]
        for p in range(1, N_DEV):
            g = g + gath[p]
        g_ref[...] = g
        d_ref[...], nm_ref[...], nv_ref[...] = _adamw(w_ref[...], g, m_ref[...], v_ref[...])

    vm = pl.BlockSpec(memory_space=pltpu.VMEM)
    return pl.pallas_call(
        body, name="small_step", in_specs=[vm] * 4, out_specs=[vm] * 4,
        out_shape=[SDS((SMALL_ROWS, 128), F32)] * 4,
        scratch_shapes=[pltpu.VMEM((N_DEV, SMALL_ROWS, 128), F32), pltpu.SemaphoreType.DMA((N_DEV - 1,)),
                        pltpu.SemaphoreType.DMA((N_DEV - 1,))])(part, w, m, v)


def _adamw(w, g, m, v):
    m = ADAM_B1 * m + (1.0 - ADAM_B1) * g
    v = ADAM_B2 * v + (1.0 - ADAM_B2) * (g * g)
    m_hat = m / (1.0 - ADAM_B1 ** ADAM_STEP)
    v_hat = v / (1.0 - ADAM_B2 ** ADAM_STEP)
    delta = -ADAM_LR * (m_hat / (jnp.sqrt(v_hat) + ADAM_EPS) + ADAM_WD * w)
    return delta, m, v


def _adamw_block(parts, w, m, v, name):
    R, C = w.shape
    tr = next(t for t in (256, 128, 64, 32, 16, 8) if R % t == 0 and t * C <= 256 * 1024)

    def body(p_ref, w_ref, m_ref, v_ref, g_ref, d_ref, nm_ref, nv_ref):
        g = p_ref[0].astype(F32)
        for p in range(1, N_DEV):
            g = g + p_ref[p].astype(F32)
        g_ref[...] = g
        d_ref[...], nm_ref[...], nv_ref[...] = _adamw(w_ref[...], g, m_ref[...], v_ref[...])

    row = pl.BlockSpec((tr, C), lambda i: (i, 0))
    return pl.pallas_call(
        body, grid=(R // tr,), name=name, in_specs=[pl.BlockSpec((N_DEV, tr, C), lambda i: (0, i, 0)), row, row, row],
        out_specs=[row] * 4, out_shape=[SDS((R, C), F32)] * 4, compiler_params=_cp(1))(parts, w, m, v)


def _pack_small(mix, ffn, fin, retw, dec_f, dec_b, loss):
    flat = jnp.concatenate([mix.reshape(-1), ffn.reshape(-1), fin.reshape(-1), retw.reshape(-1), dec_f.reshape(-1),
                            dec_b.reshape(-1), loss.reshape(-1)])
    return jnp.pad(flat, (0, SMALL_ROWS * 128 - flat.shape[0])).reshape(SMALL_ROWS, 128)


def _unpack_small(packed, shapes):
    flat = packed.reshape(-1)
    out, at = [], 0
    for s in shapes:
        n = math.prod(s)
        out.append(flat[at:at + n].reshape(s))
        at += n
    return out


def kernel(x, norm_mix_w, w_in, ret_decay_fwd, ret_decay_bwd, ret_norm_w, w_out, norm_ffn_w, w_gate, w_up, w_down, norm_final_w, loss_target, m_norm_mix_w, m_w_in, m_ret_decay_fwd, m_ret_decay_bwd, m_ret_norm_w, m_w_out, m_norm_ffn_w, m_w_gate, m_w_up, m_w_down, m_norm_final_w, v_norm_mix_w, v_w_in, v_ret_decay_fwd, v_ret_decay_bwd, v_ret_norm_w, v_w_out, v_norm_ffn_w, v_w_gate, v_w_up, v_w_down, v_norm_final_w):
    x2 = x[0]
    tgt = loss_target[0]
    S, D = x2.shape
    H = ret_norm_w.shape[1] // HEAD_DIM
    DA = H * HEAD_DIM
    fin_w = norm_final_w.reshape(1, D)
    big = (w_in[0], w_out[0], w_gate[0], w_up[0], w_down[0])

    sems, srcs, lands, tok_g = _exchange_start([w.astype(BF16) for w in big], True, [[0], [1, 2, 3, 4]], "gather_start")
    wi, = _exchange_wait(sems[0], srcs[:1], lands[:1], True, tok_g, "gather_wait_in")
    NB = wi.shape[2]

    proj, n1 = _proj_fwd(x2, norm_mix_w, wi)
    bias = _attn_bias()[:H]
    attn, lse = _attn_fwd(proj, bias)
    ret, o_raw = _ret_fwd(proj, ret_decay_fwd, ret_decay_bwd, ret_norm_w)
    wo, wg, wu, wd = _exchange_wait(sems[1], srcs[1:], lands[1:], True, ret, "gather_wait_rest")
    wo_full = wo.reshape(D, D)
    FB = wd.shape[1]
    wd_full = wd.reshape(N_DEV * FB, D)
    h1, mixed, n2 = _out_fwd(x2, attn, ret, wo_full, norm_ffn_w)
    gate, up, act = _ffn_up(n2, wg, wu)
    dh2, loss_parts, g_fin = _ffn_down_loss(act, wd_full, h1, tgt, fin_w)

    dgate, dup = _ffn_bwd_act(dh2, wd_full, gate, up)
    tk = min(512, S)
    nk = S // tk
    g_wd = _wgrad(act, dh2, pl.BlockSpec((1, tk, FB), lambda j, k: (j, k, 0)), pl.BlockSpec((tk, D), lambda j, k: (k, 0)),
                  pl.BlockSpec((1, FB, D), lambda j, k: (j, 0, 0)), (N_DEV, FB, D), (N_DEV, nk), "wgrad_down")
    tmw = min(1024, D)
    gu_specs = (pl.BlockSpec((tk, tmw), lambda j, m, k: (k, m)), pl.BlockSpec((1, tk, FB), lambda j, m, k: (j, k, 0)),
                pl.BlockSpec((1, tmw, FB), lambda j, m, k: (j, m, 0)), (N_DEV, D, FB), (N_DEV, D // tmw, nk))
    g_wg = _wgrad(n2, dgate, *gu_specs, "wgrad_gate")
    g_wu = _wgrad(n2, dup, *gu_specs, "wgrad_up")
    sem_f, src_f, land_f, tok_f = _exchange_start([g_wg, g_wu, g_wd], False, [[0, 1, 2]], "scatter_start_ffn")
    dh1, g_ffn = _ffn_bwd_in(dgate, dup, wg, wu, h1, dh2, norm_ffn_w + tok_f[0, 0])
    dmix = _dmix(dh1, wo_full)
    g_wo = _wgrad(mixed, dh1, pl.BlockSpec((tk, tmw), lambda m, k: (k, m)), pl.BlockSpec((tk, D), lambda m, k: (k, 0)),
                  pl.BlockSpec((tmw, D), lambda m, k: (m, 0)), (D, D), (D // tmw, nk), "wgrad_out")
    sem_o, src_o, land_o, tok_o = _exchange_start([g_wo.reshape(N_DEV, D // N_DEV, D)], False, [[0]], "scatter_start_out")
    dq_r, dk_r, dv_r, dg_r, small = _ret_bwd(proj, o_raw, dmix, ret_decay_fwd, ret_decay_bwd,
                                             ret_norm_w + tok_o[0, 0], DA)
    dq_a, dk_a, dv_a = _attn_bwd(proj, attn, lse, dmix, bias)
    dproj = jnp.concatenate([t.astype(BF16) for t in (dq_a, dk_a, dv_a, dq_r, dk_r, dv_r, dg_r)], axis=1)
    g_wi = _wgrad(n1, dproj, pl.BlockSpec((tk, tmw), lambda j, m, k: (k, m)), pl.BlockSpec((tk, NB), lambda j, m, k: (k, j)),
                  pl.BlockSpec((1, tmw, NB), lambda j, m, k: (j, m, 0)), (N_DEV, D, NB), (N_DEV, D // tmw, nk), "wgrad_in")
    sem_i, src_i, land_i, tok_i = _exchange_start([g_wi], False, [[0]], "scatter_start_in")
    grad_x, g_mix = _in_bwd(dproj, wi, x2, dh1, norm_mix_w + tok_i[0, 0])

    big_m = (m_w_in[0], m_w_out[0], m_w_gate[0], m_w_up[0], m_w_down[0])
    big_v = (v_w_in[0], v_w_out[0], v_w_gate[0], v_w_up[0], v_w_down[0])
    names = ("adamw_in", "adamw_out", "adamw_gate", "adamw_up", "adamw_down")
    upd = [None] * 5
    parts_f = _exchange_wait(sem_f[0], src_f, land_f, False, grad_x, "scatter_wait_ffn")
    for a, p in zip((2, 3, 4), parts_f):
        upd[a] = _adamw_block(p, big[a], big_m[a], big_v[a], names[a])
    parts_o = _exchange_wait(sem_o[0], src_o, land_o, False, upd[4][0], "scatter_wait_out")
    upd[1] = _adamw_block(parts_o[0], big[1], big_m[1], big_v[1], names[1])
    parts_i = _exchange_wait(sem_i[0], src_i, land_i, False, upd[1][0], "scatter_wait_in")
    upd[0] = _adamw_block(parts_i[0], big[0], big_m[0], big_v[0], names[0])

    g_dec_f = small[:, 0, 0].reshape(1, H)
    g_dec_b = small[:, 1, 0].reshape(1, H)
    g_retw = small[:, 2, :].reshape(1, DA)
    loss_local = jnp.sum(loss_parts[::8, 0])
    zero = jnp.zeros((1,), F32)
    part = _pack_small(g_mix, g_ffn, g_fin, g_retw, g_dec_f, g_dec_b, loss_local)
    sw = _pack_small(norm_mix_w, norm_ffn_w, norm_final_w, ret_norm_w, ret_decay_fwd, ret_decay_bwd, zero)
    sm = _pack_small(m_norm_mix_w, m_norm_ffn_w, m_norm_final_w, m_ret_norm_w, m_ret_decay_fwd, m_ret_decay_bwd, zero)
    sv = _pack_small(v_norm_mix_w, v_norm_ffn_w, v_norm_final_w, v_ret_norm_w, v_ret_decay_fwd, v_ret_decay_bwd, zero)
    shapes = [(1, D), (1, D), (D,), (1, DA), (1, H), (1, H), ()]
    sg, sd, snm, snv = [_unpack_small(t, shapes) for t in _small_step(part, sw, sm, sv)]
    loss = sg[6]

    def ordered(small_set, k):
        b = [u[k][None] for u in upd]
        return [small_set[0], b[0], small_set[4], small_set[5], small_set[3], b[1], small_set[1], b[2], b[3], b[4],
                small_set[2]]

    return (loss, grad_x[None], *ordered(sg, 0), *ordered(sd, 1), *ordered(snm, 2), *ordered(snv, 3))
```

```python
import functools
import math

import numpy as np
import jax
import jax.numpy as jnp
from jax import lax
from jax.experimental import pallas as pl
from jax.experimental.pallas import tpu as pltpu

F32 = jnp.float32
BF16 = jnp.bfloat16
SDS = jax.ShapeDtypeStruct

HEAD_DIM = 128
EPS = 1e-6
RET_CHUNK = 128
DILATIONS = (1, 4, 16)
BAND = 64
Q_TILE = 128
K_TILE = Q_TILE + 2 * BAND
KV_PAD = BAND * max(DILATIONS)
TILE_GROUP = 2
NEG = -1e30
N_DEV = 8
N_GROUPS = 7
ADAM_LR, ADAM_B1, ADAM_B2, ADAM_EPS, ADAM_WD, ADAM_STEP = 0.001, 0.9, 0.999, 1e-08, 0.01, 10
VMEM_LIMIT = 56 * 1024 * 1024
MESH = pl.DeviceIdType.MESH
ANY = pl.BlockSpec(memory_space=pl.ANY)


def _cp(n_grid):
    return pltpu.CompilerParams(dimension_semantics=("arbitrary",) * n_grid, vmem_limit_bytes=VMEM_LIMIT)


def _sigmoid(x):
    return 1.0 / (1.0 + jnp.exp(-x))


def _rms_scale(h):
    return lax.rsqrt(jnp.mean(h * h, axis=-1, keepdims=True) + EPS)


def _rms_bwd(dn, h, w):
    r = _rms_scale(h)
    gw = dn * w
    dh = r * gw - h * (r * r * r) * jnp.mean(gw * h, axis=-1, keepdims=True)
    return dh, jnp.sum(dn * h * r, axis=0, keepdims=True)


def _dot(a, b, dims):
    return lax.dot_general(a.astype(BF16), b.astype(BF16), (dims, ((), ())), preferred_element_type=F32)


_NN = ((1,), (0,))
_NT = ((1,), (1,))
_TN = ((0,), (0,))


def _proj_fwd(x, w_norm, w_blk):
    S, D = x.shape
    nblk, _, NB = w_blk.shape
    tm = min(512, S)

    def body(x_ref, wn_ref, w_ref, proj_ref, n_ref, n_scr):
        @pl.when(pl.program_id(1) == 0)
        def _():
            xf = x_ref[...]
            nb = (xf * _rms_scale(xf) * wn_ref[...]).astype(BF16)
            n_scr[...] = nb
            n_ref[...] = nb
        proj_ref[...] = jnp.dot(n_scr[...], w_ref[0], preferred_element_type=F32)

    return pl.pallas_call(
        body, grid=(S // tm, nblk), name="proj_fwd",
        in_specs=[pl.BlockSpec((tm, D), lambda i, j: (i, 0)), pl.BlockSpec((1, D), lambda i, j: (0, 0)),
                  pl.BlockSpec((1, D, NB), lambda i, j: (j, 0, 0))],
        out_specs=[pl.BlockSpec((tm, NB), lambda i, j: (i, j)), pl.BlockSpec((tm, D), lambda i, j: (i, 0))],
        out_shape=[SDS((S, nblk * NB), F32), SDS((S, D), BF16)],
        scratch_shapes=[pltpu.VMEM((tm, D), BF16)], compiler_params=_cp(2))(x, w_norm, w_blk)


def _out_fwd(x, attn, ret, w_out, w_norm):
    S, D = x.shape
    DA = attn.shape[1]
    tm = min(256, S)

    def body(x_ref, a_ref, r_ref, w_ref, wn_ref, h_ref, mix_ref, n_ref):
        a = a_ref[...].astype(BF16)
        r = r_ref[...].astype(BF16)
        mix_ref[:, :DA] = a
        mix_ref[:, DA:] = r
        h = x_ref[...] + jnp.dot(a, w_ref[:DA, :], preferred_element_type=F32) \
            + jnp.dot(r, w_ref[DA:, :], preferred_element_type=F32)
        h_ref[...] = h
        n_ref[...] = (h * _rms_scale(h) * wn_ref[...]).astype(BF16)

    row = lambda w: pl.BlockSpec((tm, w), lambda i: (i, 0))
    return pl.pallas_call(
        body, grid=(S // tm,), name="out_fwd",
        in_specs=[row(D), row(DA), row(D - DA), pl.BlockSpec((D, D), lambda i: (0, 0)),
                  pl.BlockSpec((1, D), lambda i: (0, 0))],
        out_specs=[row(D), row(D), row(D)],
        out_shape=[SDS((S, D), F32), SDS((S, D), BF16), SDS((S, D), BF16)],
        compiler_params=_cp(1))(x, attn, ret, w_out, w_norm)


def _ffn_up(n2, wg, wu):
    S, D = n2.shape
    nblk, _, FB = wg.shape
    tm = min(512, S)

    def body(n_ref, wg_ref, wu_ref, g_ref, u_ref, a_ref):
        n = n_ref[...]
        g = jnp.dot(n, wg_ref[0], preferred_element_type=F32)
        u = jnp.dot(n, wu_ref[0], preferred_element_type=F32)
        g_ref[0] = g
        u_ref[0] = u
        a_ref[0] = (g * _sigmoid(g) * u).astype(BF16)

    wspec = pl.BlockSpec((1, D, FB), lambda j, i: (j, 0, 0))
    ospec = pl.BlockSpec((1, tm, FB), lambda j, i: (j, i, 0))
    return pl.pallas_call(
        body, grid=(nblk, S // tm), name="ffn_up",
        in_specs=[pl.BlockSpec((tm, D), lambda j, i: (i, 0)), wspec, wspec],
        out_specs=[ospec, ospec, ospec],
        out_shape=[SDS((nblk, S, FB), F32), SDS((nblk, S, FB), F32), SDS((nblk, S, FB), BF16)],
        compiler_params=_cp(2))(n2, wg, wu)


def _ffn_down_loss(act, wd, h1, target, w_norm):
    nblk, S, FB = act.shape
    D = h1.shape[1]
    tm = min(512, S)

    def body(a_ref, wd_ref, h_ref, t_ref, wn_ref, dh_ref, loss_ref, dw_ref, acc):
        i, j = pl.program_id(0), pl.program_id(1)

        @pl.when(j == 0)
        def _():
            acc[...] = h_ref[...]

        @pl.when((i == 0) & (j == 0))
        def _():
            dw_ref[...] = jnp.zeros_like(dw_ref)

        acc[...] += jnp.dot(a_ref[0], wd_ref[...], preferred_element_type=F32)

        @pl.when(j == nblk - 1)
        def _():
            h = acc[...]
            w = wn_ref[...]
            err = h * _rms_scale(h) * w - t_ref[...]
            loss_ref[...] = jnp.full(loss_ref.shape, 0.5 * jnp.sum(err * err) / D, F32)
            dh, dw = _rms_bwd(err * (1.0 / D), h, w)
            dh_ref[...] = dh
            dw_ref[...] += dw

    row = pl.BlockSpec((tm, D), lambda i, j: (i, 0))
    vec = pl.BlockSpec((1, D), lambda i, j: (0, 0))
    return pl.pallas_call(
        body, grid=(S // tm, nblk), name="ffn_down_loss",
        in_specs=[pl.BlockSpec((1, tm, FB), lambda i, j: (j, i, 0)), pl.BlockSpec((FB, D), lambda i, j: (j, 0)),
                  row, row, vec],
        out_specs=[row, pl.BlockSpec((8, 128), lambda i, j: (i, 0)), vec],
        out_shape=[SDS((S, D), F32), SDS((S // tm * 8, 128), F32), SDS((1, D), F32)],
        scratch_shapes=[pltpu.VMEM((tm, D), F32)], compiler_params=_cp(2))(act, wd, h1, target, w_norm)


def _attn_bias():
    n_heads = 8
    slopes = np.exp2(-8.0 * np.arange(1, n_heads + 1, dtype=np.float32) / n_heads)
    dist = np.abs(np.arange(K_TILE)[None, :] - BAND - np.arange(Q_TILE)[:, None])
    out = np.empty((n_heads, len(DILATIONS), Q_TILE, K_TILE), np.float32)
    for h in range(n_heads):
        for p, d in enumerate(DILATIONS):
            out[h, p] = np.where(dist <= BAND, -slopes[h] * (d * dist).astype(np.float32), NEG)
    return jnp.asarray(out)


def _attn_tiles(S, d):
    L = S // d
    per_class = L // Q_TILE
    return L, per_class, d * per_class


def _tile_rows(t, d, per_class):
    r = t // per_class
    a = (t % per_class) * Q_TILE
    q_rows = pl.ds(r + d * a, Q_TILE, stride=d) if d > 1 else pl.ds(pl.multiple_of(a, Q_TILE), Q_TILE)
    k_rows = pl.ds(KV_PAD + r + d * (a - BAND), K_TILE, stride=d) if d > 1 else pl.ds(
        pl.multiple_of(KV_PAD + a - BAND, BAND), K_TILE)
    return a, q_rows, k_rows


def _edge_mask(a, L):
    lk = lax.broadcasted_iota(jnp.int32, (1, K_TILE), 1) + (a - BAND)
    return jnp.where((lk >= 0) & (lk < L), 0.0, NEG).astype(F32)


def _fill_padded(dst, src, S):
    dst[pl.ds(0, KV_PAD), :] = jnp.zeros((KV_PAD, HEAD_DIM), F32)
    dst[pl.ds(KV_PAD + S, KV_PAD), :] = jnp.zeros((KV_PAD, HEAD_DIM), F32)
    dst[pl.ds(KV_PAD, S), :] = src[...]


def _head_specs(S, groups, n_heads):
    return [pl.BlockSpec((S, HEAD_DIM), functools.partial(lambda h, g: (0, g * n_heads + h), g=g)) for g in groups]


def _attn_fwd(proj, bias):
    S = proj.shape[0]
    H = proj.shape[1] // (N_GROUPS * HEAD_DIM)
    scale = HEAD_DIM ** -0.5

    def body(q_ref, k_ref, v_ref, b_ref, o_ref, lse_ref, kp, vp, m_run, l_run):
        _fill_padded(kp, k_ref, S)
        _fill_padded(vp, v_ref, S)
        o_ref[...] = jnp.zeros_like(o_ref)
        m_run[...] = jnp.full(m_run.shape, NEG, F32)
        l_run[...] = jnp.zeros_like(l_run)
        for p, d in enumerate(DILATIONS):
            L, per_class, n_tiles = _attn_tiles(S, d)

            def tiles(t, carry, p=p, d=d, L=L, per_class=per_class, n_tiles=n_tiles):
                rows = [_tile_rows(t + u * (n_tiles // TILE_GROUP), d, per_class) for u in range(TILE_GROUP)]
                got = [(q_ref[qr, :], kp[kr, :], vp[kr, :], m_run[qr, :][:, :1], l_run[qr, :][:, :1], o_ref[qr, :])
                       for _, qr, kr in rows]
                new = []
                for (a, _, _), (qt, kt, vt, m_old, l_old, o_old) in zip(rows, got):
                    s = _dot(qt, kt, _NT) * scale + b_ref[0, p] + _edge_mask(a, L)
                    m_new = jnp.maximum(m_old, jnp.max(s, axis=-1, keepdims=True))
                    pr = jnp.exp(s - m_new)
                    alpha = jnp.exp(m_old - m_new)
                    new.append((m_new, alpha * l_old + jnp.sum(pr, axis=-1, keepdims=True),
                                alpha * o_old + _dot(pr, vt, _NN)))
                for (_, qr, _), (m_new, l_new, o_new) in zip(rows, new):
                    o_ref[qr, :] = o_new
                    m_run[qr, :] = jnp.broadcast_to(m_new, (Q_TILE, HEAD_DIM))
                    l_run[qr, :] = jnp.broadcast_to(l_new, (Q_TILE, HEAD_DIM))
                return carry

            lax.fori_loop(0, n_tiles // TILE_GROUP, tiles, 0)
        l = l_run[...]
        o_ref[...] = o_ref[...] / l
        lse_ref[...] = m_run[...] + jnp.log(l)

    hspec = pl.BlockSpec((S, HEAD_DIM), lambda h: (0, h))
    return pl.pallas_call(
        body, grid=(H,), name="attn_fwd",
        in_specs=_head_specs(S, (0, 1, 2), H) + [
            pl.BlockSpec((1, len(DILATIONS), Q_TILE, K_TILE), lambda h: (h, 0, 0, 0))],
        out_specs=[hspec, hspec],
        out_shape=[SDS((S, H * HEAD_DIM), F32), SDS((S, H * HEAD_DIM), F32)],
        scratch_shapes=[pltpu.VMEM((S + 2 * KV_PAD, HEAD_DIM), F32), pltpu.VMEM((S + 2 * KV_PAD, HEAD_DIM), F32),
                        pltpu.VMEM((S, HEAD_DIM), F32), pltpu.VMEM((S, HEAD_DIM), F32)],
        compiler_params=_cp(1))(proj, proj, proj, bias)


def _attn_bwd(proj, out, lse, dmix, bias):
    S = proj.shape[0]
    H = proj.shape[1] // (N_GROUPS * HEAD_DIM)
    scale = HEAD_DIM ** -0.5

    def body(q_ref, k_ref, v_ref, o_ref, lse_ref, do_ref, b_ref, dq_ref, dk_ref, dv_ref, kp, vp, dkp, dvp, dsum):
        _fill_padded(kp, k_ref, S)
        _fill_padded(vp, v_ref, S)
        dkp[...] = jnp.zeros_like(dkp)
        dvp[...] = jnp.zeros_like(dvp)
        dq_ref[...] = jnp.zeros_like(dq_ref)
        dsum[...] = jnp.broadcast_to(jnp.sum(do_ref[...] * o_ref[...], axis=-1, keepdims=True), dsum.shape)
        for p, d in enumerate(DILATIONS):
            L, per_class, n_tiles = _attn_tiles(S, d)

            def tiles(t, carry, p=p, d=d, L=L, per_class=per_class, n_tiles=n_tiles):
                rows = [_tile_rows(t + u * (n_tiles // TILE_GROUP), d, per_class) for u in range(TILE_GROUP)]
                got = [(q_ref[qr, :], kp[kr, :], vp[kr, :], do_ref[qr, :], lse_ref[qr, :][:, :1], dsum[qr, :][:, :1],
                        dq_ref[qr, :], dkp[kr, :], dvp[kr, :]) for _, qr, kr in rows]
                new = []
                for (a, _, _), (qt, kt, vt, dot_, lse_t, dsum_t, dq_old, dk_old, dv_old) in zip(rows, got):
                    s = _dot(qt, kt, _NT) * scale + b_ref[0, p] + _edge_mask(a, L)
                    pr = jnp.exp(s - lse_t)
                    ds = pr * (_dot(dot_, vt, _NT) - dsum_t) * scale
                    new.append((dq_old + _dot(ds, kt, _NN), dk_old + _dot(ds, qt, _TN), dv_old + _dot(pr, dot_, _TN)))
                for (_, qr, kr), (dq_new, dk_new, dv_new) in zip(rows, new):
                    dq_ref[qr, :] = dq_new
                    dkp[kr, :] = dk_new
                    dvp[kr, :] = dv_new
                return carry

            lax.fori_loop(0, n_tiles // TILE_GROUP, tiles, 0)
        dk_ref[...] = dkp[pl.ds(KV_PAD, S), :]
        dv_ref[...] = dvp[pl.ds(KV_PAD, S), :]

    hspec = pl.BlockSpec((S, HEAD_DIM), lambda h: (0, h))
    padded = pltpu.VMEM((S + 2 * KV_PAD, HEAD_DIM), F32)
    return pl.pallas_call(
        body, grid=(H,), name="attn_bwd",
        in_specs=_head_specs(S, (0, 1, 2), H) + [hspec, hspec, hspec,
                                                  pl.BlockSpec((1, len(DILATIONS), Q_TILE, K_TILE), lambda h: (h, 0, 0, 0))],
        out_specs=[hspec, hspec, hspec],
        out_shape=[SDS((S, H * HEAD_DIM), F32)] * 3,
        scratch_shapes=[padded, padded, padded, padded, pltpu.VMEM((S, HEAD_DIM), F32)],
        compiler_params=_cp(1))(proj, proj, proj, out, lse, dmix, bias)


def _ret_consts(lg, forward):
    C = RET_CHUNK
    i = lax.broadcasted_iota(jnp.int32, (C, C), 0)
    j = lax.broadcasted_iota(jnp.int32, (C, C), 1)
    rel = (i - j) if forward else (j - i)
    inside = (rel >= 0) if forward else (rel > 0)
    relf = jnp.maximum(rel, 0).astype(F32)
    mask = jnp.where(inside, jnp.exp(lg * relf), 0.0)
    idx = lax.broadcasted_iota(jnp.int32, (C, 1), 0).astype(F32)
    q_exp = (idx + 1.0) if forward else (C - idx)
    k_exp = (C - 1.0 - idx) if forward else idx
    return mask, relf, jnp.exp(lg * q_exp), q_exp, jnp.exp(lg * k_exp), k_exp, jnp.exp(lg * C)


def _log_decay(dec_ref, h):
    return -jnp.exp(jnp.full((1, 1), dec_ref[0, h], F32))


def _chunk(c):
    return pl.ds(pl.multiple_of(c * RET_CHUNK, RET_CHUNK), RET_CHUNK)


def _ret_fwd(proj, dec_f, dec_b, w_norm):
    S = proj.shape[0]
    H = proj.shape[1] // (N_GROUPS * HEAD_DIM)
    nc = S // RET_CHUNK
    scale = HEAD_DIM ** -0.5

    def body(df_ref, db_ref, q_ref, k_ref, v_ref, g_ref, w_ref, y_ref, o_ref):
        h = pl.program_id(0)
        consts = [_ret_consts(_log_decay(dref, h), fw) for fw, dref in ((True, df_ref), (False, db_ref))]
        o_ref[...] = jnp.zeros_like(o_ref)

        def step(n, states):
            rows = [_chunk(n), _chunk(nc - 1 - n)]
            got = [(q_ref[r, :] * scale, k_ref[r, :], v_ref[r, :], o_ref[r, :]) for r in rows]
            new_o, new_states = [], []
            for (qc, kc, vc, o_old), (mask, _, q_dec, _, k_dec, _, c_dec), state in zip(got, consts, states):
                new_o.append(o_old + _dot(_dot(qc, kc, _NT) * mask, vc, _NN) + _dot(qc * q_dec, state, _NN))
                new_states.append(state * c_dec + _dot(kc * k_dec, vc, _TN))
            for r, o_new in zip(rows, new_o):
                o_ref[r, :] = o_new
            return tuple(new_states)

        lax.fori_loop(0, nc, step, (jnp.zeros((HEAD_DIM, HEAD_DIM), F32),) * 2)
        o = o_ref[...]
        g = g_ref[...]
        y_ref[...] = o * _rms_scale(o) * w_ref[...] * (g * _sigmoid(g))

    hspec = pl.BlockSpec((S, HEAD_DIM), lambda h: (0, h))
    smem = pl.BlockSpec(memory_space=pltpu.SMEM)
    return pl.pallas_call(
        body, grid=(H,), name="ret_fwd",
        in_specs=[smem, smem] + _head_specs(S, (3, 4, 5, 6), H) + [pl.BlockSpec((1, HEAD_DIM), lambda h: (0, h))],
        out_specs=[hspec, hspec],
        out_shape=[SDS((S, H * HEAD_DIM), F32)] * 2,
        compiler_params=_cp(1))(dec_f, dec_b, proj, proj, proj, proj, w_norm)


def _ret_bwd(proj, o_raw, dmix, dec_f, dec_b, w_norm, col0):
    S = proj.shape[0]
    H = proj.shape[1] // (N_GROUPS * HEAD_DIM)
    C = RET_CHUNK
    nc = S // C
    scale = HEAD_DIM ** -0.5

    def body(df_ref, db_ref, q_ref, k_ref, v_ref, g_ref, o_ref, dy_ref, w_ref,
             dq_ref, dk_ref, dv_ref, dg_ref, small_ref, do, states):
        h = pl.program_id(0)
        o = o_ref[...]
        g = g_ref[...]
        dy = dy_ref[...]
        w = w_ref[...]
        rr = _rms_scale(o)
        normed = o * rr
        sg = _sigmoid(g)
        silu = g * sg
        small_ref[0, pl.ds(2, 1), :] = jnp.sum(dy * normed * silu, axis=0, keepdims=True)
        dg_ref[...] = dy * normed * w * (sg * (1.0 + g * (1.0 - sg)))
        dnormed = dy * w * silu
        do[...] = rr * dnormed - o * (rr * rr * rr) * jnp.mean(dnormed * o, axis=-1, keepdims=True)

        lgs = [_log_decay(df_ref, h), _log_decay(db_ref, h)]
        consts = [_ret_consts(lg, fw) for lg, fw in zip(lgs, (True, False))]
        zero_state = jnp.zeros((HEAD_DIM, HEAD_DIM), F32)

        def fwd_step(n, carry):
            new = []
            for way, (cidx, state) in enumerate(zip((n, nc - 1 - n), carry)):
                k_dec, c_dec = consts[way][4], consts[way][6]
                rows = _chunk(cidx)
                states[way, cidx] = state
                new.append(state * c_dec + _dot(k_ref[rows, :] * k_dec, v_ref[rows, :], _TN))
            return tuple(new)

        lax.fori_loop(0, nc, fwd_step, (zero_state, zero_state))
        for ref in (dq_ref, dk_ref, dv_ref):
            ref[...] = jnp.zeros_like(ref)

        def bwd_step(n, carry):
            cidxs = (nc - 1 - n, n)
            rows = [_chunk(c) for c in cidxs]
            got = [(q_ref[r, :] * scale, k_ref[r, :], v_ref[r, :], do[r, :], states[way, c], dq_ref[r, :], dk_ref[r, :],
                    dv_ref[r, :]) for way, (r, c) in enumerate(zip(rows, cidxs))]
            new_rows, new_carry = [], []
            for (qc, kc, vc, doc, state, dq_old, dk_old, dv_old), cs, (d_state, dlam) in zip(got, consts, carry):
                mask, relf, q_dec, q_exp, k_dec, k_exp, c_dec = cs
                a0 = _dot(qc, kc, _NT)
                dp = _dot(doc, vc, _NT) * mask
                gq = _dot(doc, state, _NT)
                gk = _dot(vc, d_state, _NT)
                new_rows.append((dq_old + (_dot(dp, kc, _NN) + q_dec * gq) * scale,
                                 dk_old + _dot(dp, qc, _TN) + k_dec * gk,
                                 dv_old + _dot(a0 * mask, doc, _TN) + _dot(kc * k_dec, d_state, _NN)))
                dlam = dlam + jnp.sum(relf * a0 * dp, axis=0, keepdims=True) \
                    + jnp.sum(q_exp * q_dec * qc * gq + k_exp * k_dec * kc * gk, axis=0, keepdims=True) \
                    + (C * c_dec) * jnp.sum(state * d_state, axis=0, keepdims=True)
                new_carry.append((d_state * c_dec + _dot(qc * q_dec, doc, _TN), dlam))
            for r, (dq_new, dk_new, dv_new) in zip(rows, new_rows):
                dq_ref[r, :] = dq_new
                dk_ref[r, :] = dk_new
                dv_ref[r, :] = dv_new
            return tuple(new_carry)

        zero_carry = (zero_state, jnp.zeros((1, HEAD_DIM), F32))
        done = lax.fori_loop(0, nc, bwd_step, (zero_carry, zero_carry))
        for row, ((_, dlam), lg) in enumerate(zip(done, lgs)):
            small_ref[0, pl.ds(row, 1), :] = jnp.broadcast_to(jnp.sum(dlam, axis=-1, keepdims=True) * lg, (1, HEAD_DIM))
        small_ref[0, pl.ds(3, 5), :] = jnp.zeros((5, HEAD_DIM), F32)

    hspec = pl.BlockSpec((S, HEAD_DIM), lambda h: (0, h))
    smem = pl.BlockSpec(memory_space=pltpu.SMEM)
    nh0 = col0 // HEAD_DIM
    return pl.pallas_call(
        body, grid=(H,), name="ret_bwd",
        in_specs=[smem, smem] + _head_specs(S, (3, 4, 5, 6), H) + [
            hspec, pl.BlockSpec((S, HEAD_DIM), lambda h: (0, nh0 + h)), pl.BlockSpec((1, HEAD_DIM), lambda h: (0, h))],
        out_specs=[hspec, hspec, hspec, hspec, pl.BlockSpec((1, 8, HEAD_DIM), lambda h: (h, 0, 0))],
        out_shape=[SDS((S, H * HEAD_DIM), F32)] * 4 + [SDS((H, 8, HEAD_DIM), F32)],
        scratch_shapes=[pltpu.VMEM((S, HEAD_DIM), F32), pltpu.VMEM((2, nc, HEAD_DIM, HEAD_DIM), F32)],
        compiler_params=_cp(1))(dec_f, dec_b, proj, proj, proj, proj, o_raw, dmix, w_norm)


def _ffn_bwd_act(dh2, wd, g, u):
    S, D = dh2.shape
    nblk, _, FB = g.shape
    tm = min(512, S)

    def body(dh_ref, wd_ref, g_ref, u_ref, dg_ref, du_ref):
        dact = _dot(dh_ref[...], wd_ref[...], _NT)
        gg = g_ref[0]
        sg = _sigmoid(gg)
        dg_ref[0] = (dact * u_ref[0] * (sg * (1.0 + gg * (1.0 - sg)))).astype(BF16)
        du_ref[0] = (dact * (gg * sg)).astype(BF16)

    blk = pl.BlockSpec((1, tm, FB), lambda j, i: (j, i, 0))
    return pl.pallas_call(
        body, grid=(nblk, S // tm), name="ffn_bwd_act",
        in_specs=[pl.BlockSpec((tm, D), lambda j, i: (i, 0)), pl.BlockSpec((FB, D), lambda j, i: (j, 0)), blk, blk],
        out_specs=[blk, blk], out_shape=[SDS((nblk, S, FB), BF16)] * 2,
        compiler_params=_cp(2))(dh2, wd, g, u)


def _ffn_bwd_in(dg, du, wg, wu, h1, dh2, w_norm):
    nblk, S, FB = dg.shape
    D = h1.shape[1]
    tm = min(256, S)

    def body(dg_ref, du_ref, wg_ref, wu_ref, h_ref, dh2_ref, wn_ref, dh_ref, dw_ref, acc):
        i, j = pl.program_id(0), pl.program_id(1)

        @pl.when(j == 0)
        def _():
            acc[...] = jnp.zeros_like(acc)

        @pl.when((i == 0) & (j == 0))
        def _():
            dw_ref[...] = jnp.zeros_like(dw_ref)

        acc[...] += _dot(dg_ref[0], wg_ref[0], _NT) + _dot(du_ref[0], wu_ref[0], _NT)

        @pl.when(j == nblk - 1)
        def _():
            dh, dw = _rms_bwd(acc[...], h_ref[...], wn_ref[...])
            dh_ref[...] = dh2_ref[...] + dh
            dw_ref[...] += dw

    blk = pl.BlockSpec((1, tm, FB), lambda i, j: (j, i, 0))
    wspec = pl.BlockSpec((1, D, FB), lambda i, j: (j, 0, 0))
    row = pl.BlockSpec((tm, D), lambda i, j: (i, 0))
    vec = pl.BlockSpec((1, D), lambda i, j: (0, 0))
    return pl.pallas_call(
        body, grid=(S // tm, nblk), name="ffn_bwd_in",
        in_specs=[blk, blk, wspec, wspec, row, row, vec],
        out_specs=[row, vec], out_shape=[SDS((S, D), F32), SDS((1, D), F32)],
        scratch_shapes=[pltpu.VMEM((tm, D), F32)], compiler_params=_cp(2))(dg, du, wg, wu, h1, dh2, w_norm)


def _dmix(dh1, w_out):
    S, D = dh1.shape
    tm = min(512, S)

    def body(dh_ref, w_ref, o_ref):
        o_ref[...] = _dot(dh_ref[...], w_ref[...], _NT)

    row = pl.BlockSpec((tm, D), lambda i: (i, 0))
    return pl.pallas_call(
        body, grid=(S // tm,), name="dmix", in_specs=[row, pl.BlockSpec((D, D), lambda i: (0, 0))],
        out_specs=row, out_shape=SDS((S, D), F32), compiler_params=_cp(1))(dh1, w_out)


def _in_bwd(dproj, w_blk, x, dh1, w_norm):
    S, D = x.shape
    nblk, _, NB = w_blk.shape
    tm = min(512, S)

    def body(dp_ref, w_ref, x_ref, dh1_ref, wn_ref, dx_ref, dw_ref, acc):
        i, j = pl.program_id(0), pl.program_id(1)

        @pl.when(j == 0)
        def _():
            acc[...] = jnp.zeros_like(acc)

        @pl.when((i == 0) & (j == 0))
        def _():
            dw_ref[...] = jnp.zeros_like(dw_ref)

        acc[...] += _dot(dp_ref[...], w_ref[0], _NT)

        @pl.when(j == nblk - 1)
        def _():
            dh, dw = _rms_bwd(acc[...], x_ref[...], wn_ref[...])
            dx_ref[...] = dh1_ref[...] + dh
            dw_ref[...] += dw

    row = pl.BlockSpec((tm, D), lambda i, j: (i, 0))
    vec = pl.BlockSpec((1, D), lambda i, j: (0, 0))
    return pl.pallas_call(
        body, grid=(S // tm, nblk), name="in_bwd",
        in_specs=[pl.BlockSpec((tm, NB), lambda i, j: (i, j)), pl.BlockSpec((1, D, NB), lambda i, j: (j, 0, 0)),
                  row, row, vec],
        out_specs=[row, vec], out_shape=[SDS((S, D), F32), SDS((1, D), F32)],
        scratch_shapes=[pltpu.VMEM((tm, D), F32)], compiler_params=_cp(2))(dproj, w_blk, x, dh1, w_norm)


def _wgrad(a, b, a_spec, b_spec, o_spec, o_shape, grid, name):
    nk = grid[-1]

    def ld(ref):
        return ref[0] if len(ref.shape) == 3 else ref[...]

    def body(a_ref, b_ref, o_ref, acc):
        k = pl.program_id(len(grid) - 1)

        @pl.when(k == 0)
        def _():
            acc[...] = jnp.zeros_like(acc)

        acc[...] += _dot(ld(a_ref), ld(b_ref), _TN)

        @pl.when(k == nk - 1)
        def _():
            if len(o_ref.shape) == 3:
                o_ref[0] = acc[...].astype(o_ref.dtype)
            else:
                o_ref[...] = acc[...].astype(o_ref.dtype)

    return pl.pallas_call(
        body, grid=grid, name=name, in_specs=[a_spec, b_spec], out_specs=o_spec, out_shape=SDS(o_shape, BF16),
        scratch_shapes=[pltpu.VMEM(o_spec.block_shape[-2:], F32)], compiler_params=_cp(len(grid)))(a, b)


def _peer(k):
    x, y, c = lax.axis_index("x"), lax.axis_index("y"), lax.axis_index("c")
    px = 1 - x if k & 4 else x
    py = 1 - y if k & 2 else y
    pc = 1 - c if k & 1 else c
    return (px, py, pc), 4 * px + 2 * py + pc


HBM = pl.BlockSpec(memory_space=pltpu.HBM)
SEM = pl.BlockSpec(memory_space=pltpu.SEMAPHORE)
EFFECT = pltpu.SideEffectType.DATAFLOW_SIDE_EFFECTING


def _exchange_copies(srcs, lands, send_sems, recv_sems, which, gather):
    _, me = _peer(0)
    pairs = []
    for pos, a in enumerate(which):
        for k in range(1, N_DEV):
            dev, idx = _peer(k)
            sem = pos * (N_DEV - 1) + k - 1
            src = srcs[a] if gather else srcs[a].at[idx]
            mk = functools.partial(pltpu.make_async_remote_copy, src_ref=src, send_sem=send_sems.at[sem],
                                   recv_sem=recv_sems.at[sem], device_id=dev, device_id_type=MESH)
            pairs.append((mk(dst_ref=lands[a].at[me]), mk(dst_ref=lands[a].at[idx])))
    return pairs


def _exchange_start(arrays, gather, groups, name):
    n, ng = len(arrays), len(groups)
    lands = [lax.empty((N_DEV,) + a.shape if gather else a.shape, a.dtype) for a in arrays]

    def body(*refs):
        srcs, dsts = refs[:n], refs[n:2 * n]
        sems = refs[2 * n:2 * n + 2 * ng]
        token, local_sems = refs[-2], refs[-1]
        _, me = _peer(0)
        local = [pltpu.make_async_copy(srcs[a] if gather else srcs[a].at[me], dsts[a].at[me], local_sems.at[a])
                 for a in range(n)]
        for cp in local:
            cp.start()
        for cp in local:
            cp.wait()
        for g, which in enumerate(groups):
            for out, _ in _exchange_copies(srcs, dsts, sems[2 * g], sems[2 * g + 1], which, gather):
                out.start()
        token[...] = jnp.zeros_like(token)

    sem_shapes = []
    for which in groups:
        sem_shapes += [pltpu.SemaphoreType.DMA((len(which) * (N_DEV - 1),))] * 2
    thru = [pltpu.HBM(a.shape, a.dtype) for a in arrays] + [pltpu.HBM(l.shape, l.dtype) for l in lands]
    outs = pl.pallas_call(
        body, name=name, in_specs=[HBM] * (2 * n),
        out_specs=[SEM] * (2 * ng) + [HBM] * (2 * n) + [pl.BlockSpec(memory_space=pltpu.VMEM)],
        out_shape=sem_shapes + thru + [SDS((8, 128), F32)],
        input_output_aliases={i: 2 * ng + i for i in range(2 * n)},
        scratch_shapes=[pltpu.SemaphoreType.DMA((n,))],
        compiler_params=pltpu.CompilerParams(has_side_effects=EFFECT),
    )(*[pltpu.with_memory_space_constraint(t, pltpu.HBM) for t in list(arrays) + lands])
    sems = [(outs[2 * g], outs[2 * g + 1]) for g in range(ng)]
    return sems, list(outs[2 * ng:2 * ng + n]), list(outs[2 * ng + n:2 * ng + 2 * n]), outs[-1]


def _exchange_wait(sems, srcs, lands, gather, after, name):
    n = len(srcs)
    send_sems, recv_sems = sems

    def body(*refs):
        s, d = refs[:n], refs[n:2 * n]
        for out, arrival in _exchange_copies(s, d, refs[2 * n], refs[2 * n + 1], range(n), gather):
            out.wait_send()
            arrival.wait_recv()

    outs = pl.pallas_call(
        body, name=name, in_specs=[HBM] * (2 * n) + [SEM, SEM, ANY], out_specs=[HBM] * (2 * n),
        out_shape=[pltpu.HBM(t.shape, t.dtype) for t in list(srcs) + list(lands)],
        input_output_aliases={i: i for i in range(2 * n)},
        compiler_params=pltpu.CompilerParams(has_side_effects=EFFECT),
    )(*srcs, *lands, send_sems, recv_sems, after)
    return list(outs[n:])


SMALL_ROWS = 64


def _small_step(part, w, m, v):
    def body(p_ref, w_ref, m_ref, v_ref, g_ref, d_ref, nm_ref, nv_ref, gath, send_sems, recv_sems):
        _, me = _peer(0)
        gath[me] = p_ref[...]
        copies = []
        for k in range(1, N_DEV):
            dev, idx = _peer(k)
            out = pltpu.make_async_remote_copy(src_ref=p_ref, dst_ref=gath.at[me], send_sem=send_sems.at[k - 1],
                                               recv_sem=recv_sems.at[k - 1], device_id=dev, device_id_type=MESH)
            out.start()
            arrival = pltpu.make_async_remote_copy(src_ref=p_ref, dst_ref=gath.at[idx], send_sem=send_sems.at[k - 1],
                                                   recv_sem=recv_sems.at[k - 1], device_id=dev, device_id_type=MESH)
            copies.append((out, arrival))
        for out, arrival in copies:
            out.wait_send()
            arrival.wait_recv()
        g = gath[0]
        for p in range(1, N_DEV):
            g = g + gath[p]
        g_ref[...] = g
        d_ref[...], nm_ref[...], nv_ref[...] = _adamw(w_ref[...], g, m_ref[...], v_ref[...])

    vm = pl.BlockSpec(memory_space=pltpu.VMEM)
    return pl.pallas_call(
        body, name="small_step", in_specs=[vm] * 4, out_specs=[vm] * 4,
        out_shape=[SDS((SMALL_ROWS, 128), F32)] * 4,
        scratch_shapes=[pltpu.VMEM((N_DEV, SMALL_ROWS, 128), F32), pltpu.SemaphoreType.DMA((N_DEV - 1,)),
                        pltpu.SemaphoreType.DMA((N_DEV - 1,))])(part, w, m, v)


def _adamw(w, g, m, v):
    m = ADAM_B1 * m + (1.0 - ADAM_B1) * g
    v = ADAM_B2 * v + (1.0 - ADAM_B2) * (g * g)
    m_hat = m / (1.0 - ADAM_B1 ** ADAM_STEP)
    v_hat = v / (1.0 - ADAM_B2 ** ADAM_STEP)
    delta = -ADAM_LR * (m_hat / (jnp.sqrt(v_hat) + ADAM_EPS) + ADAM_WD * w)
    return delta, m, v


def _adamw_block(parts, w, m, v, name):
    R, C = w.shape
    tr = next(t for t in (256, 128, 64, 32, 16, 8) if R % t == 0 and t * C <= 256 * 1024)

    def body(p_ref, w_ref, m_ref, v_ref, g_ref, d_ref, nm_ref, nv_ref):
        g = p_ref[0].astype(F32)
        for p in range(1, N_DEV):
            g = g + p_ref[p].astype(F32)
        g_ref[...] = g
        d_ref[...], nm_ref[...], nv_ref[...] = _adamw(w_ref[...], g, m_ref[...], v_ref[...])

    row = pl.BlockSpec((tr, C), lambda i: (i, 0))
    return pl.pallas_call(
        body, grid=(R // tr,), name=name, in_specs=[pl.BlockSpec((N_DEV, tr, C), lambda i: (0, i, 0)), row, row, row],
        out_specs=[row] * 4, out_shape=[SDS((R, C), F32)] * 4, compiler_params=_cp(1))(parts, w, m, v)


def _pack_small(mix, ffn, fin, retw, dec_f, dec_b, loss):
    flat = jnp.concatenate([mix.reshape(-1), ffn.reshape(-1), fin.reshape(-1), retw.reshape(-1), dec_f.reshape(-1),
                            dec_b.reshape(-1), loss.reshape(-1)])
    return jnp.pad(flat, (0, SMALL_ROWS * 128 - flat.shape[0])).reshape(SMALL_ROWS, 128)


def _unpack_small(packed, shapes):
    flat = packed.reshape(-1)
    out, at = [], 0
    for s in shapes:
        n = math.prod(s)
        out.append(flat[at:at + n].reshape(s))
        at += n
    return out


def kernel(x, norm_mix_w, w_in, ret_decay_fwd, ret_decay_bwd, ret_norm_w, w_out, norm_ffn_w, w_gate, w_up, w_down, norm_final_w, loss_target, m_norm_mix_w, m_w_in, m_ret_decay_fwd, m_ret_decay_bwd, m_ret_norm_w, m_w_out, m_norm_ffn_w, m_w_gate, m_w_up, m_w_down, m_norm_final_w, v_norm_mix_w, v_w_in, v_ret_decay_fwd, v_ret_decay_bwd, v_ret_norm_w, v_w_out, v_norm_ffn_w, v_w_gate, v_w_up, v_w_down, v_norm_final_w):
    x2 = x[0]
    tgt = loss_target[0]
    S, D = x2.shape
    H = ret_norm_w.shape[1] // HEAD_DIM
    DA = H * HEAD_DIM
    fin_w = norm_final_w.reshape(1, D)
    big = (w_in[0], w_out[0], w_gate[0], w_up[0], w_down[0])

    sems, srcs, lands, tok_g = _exchange_start([w.astype(BF16) for w in big], True, [[0], [1, 2, 3, 4]], "gather_start")
    wi, = _exchange_wait(sems[0], srcs[:1], lands[:1], True, tok_g, "gather_wait_in")
    NB = wi.shape[2]

    proj, n1 = _proj_fwd(x2, norm_mix_w, wi)
    bias = _attn_bias()[:H]
    attn, lse = _attn_fwd(proj, bias)
    ret, o_raw = _ret_fwd(proj, ret_decay_fwd, ret_decay_bwd, ret_norm_w)
    wo, wg, wu, wd = _exchange_wait(sems[1], srcs[1:], lands[1:], True, ret, "gather_wait_rest")
    wo_full = wo.reshape(D, D)
    FB = wd.shape[1]
    wd_full = wd.reshape(N_DEV * FB, D)
    h1, mixed, n2 = _out_fwd(x2, attn, ret, wo_full, norm_ffn_w)
    gate, up, act = _ffn_up(n2, wg, wu)
    dh2, loss_parts, g_fin = _ffn_down_loss(act, wd_full, h1, tgt, fin_w)

    dgate, dup = _ffn_bwd_act(dh2, wd_full, gate, up)
    tk = min(512, S)
    nk = S // tk
    g_wd = _wgrad(act, dh2, pl.BlockSpec((1, tk, FB), lambda j, k: (j, k, 0)), pl.BlockSpec((tk, D), lambda j, k: (k, 0)),
                  pl.BlockSpec((1, FB, D), lambda j, k: (j, 0, 0)), (N_DEV, FB, D), (N_DEV, nk), "wgrad_down")
    tmw = min(1024, D)
    gu_specs = (pl.BlockSpec((tk, tmw), lambda j, m, k: (k, m)), pl.BlockSpec((1, tk, FB), lambda j, m, k: (j, k, 0)),
                pl.BlockSpec((1, tmw, FB), lambda j, m, k: (j, m, 0)), (N_DEV, D, FB), (N_DEV, D // tmw, nk))
    g_wg = _wgrad(n2, dgate, *gu_specs, "wgrad_gate")
    g_wu = _wgrad(n2, dup, *gu_specs, "wgrad_up")
    sem_f, src_f, land_f, tok_f = _exchange_start([g_wg, g_wu, g_wd], False, [[0, 1, 2]], "scatter_start_ffn")
    dh1, g_ffn = _ffn_bwd_in(dgate, dup, wg, wu, h1, dh2, norm_ffn_w + tok_f[0, 0])
    dmix = _dmix(dh1, wo_full)
    g_wo = _wgrad(mixed, dh1, pl.BlockSpec((tk, tmw), lambda m, k: (k, m)), pl.BlockSpec((tk, D), lambda m, k: (k, 0)),
                  pl.BlockSpec((tmw, D), lambda m, k: (m, 0)), (D, D), (D // tmw, nk), "wgrad_out")
    sem_o, src_o, land_o, tok_o = _exchange_start([g_wo.reshape(N_DEV, D // N_DEV, D)], False, [[0]], "scatter_start_out")
    dq_r, dk_r, dv_r, dg_r, small = _ret_bwd(proj, o_raw, dmix, ret_decay_fwd, ret_decay_bwd,
                                             ret_norm_w + tok_o[0, 0], DA)
    dq_a, dk_a, dv_a = _attn_bwd(proj, attn, lse, dmix, bias)
    dproj = jnp.concatenate([t.astype(BF16) for t in (dq_a, dk_a, dv_a, dq_r, dk_r, dv_r, dg_r)], axis=1)
    g_wi = _wgrad(n1, dproj, pl.BlockSpec((tk, tmw), lambda j, m, k: (k, m)), pl.BlockSpec((tk, NB), lambda j, m, k: (k, j)),
                  pl.BlockSpec((1, tmw, NB), lambda j, m, k: (j, m, 0)), (N_DEV, D, NB), (N_DEV, D // tmw, nk), "wgrad_in")
    sem_i, src_i, land_i, tok_i = _exchange_start([g_wi], False, [[0]], "scatter_start_in")
    grad_x, g_mix = _in_bwd(dproj, wi, x2, dh1, norm_mix_w + tok_i[0, 0])

    big_m = (m_w_in[0], m_w_out[0], m_w_gate[0], m_w_up[0], m_w_down[0])
    big_v = (v_w_in[0], v_w_out[0], v_w_gate[0], v_w_up[0], v_w_down[0])
    names = ("adamw_in", "adamw_out", "adamw_gate", "adamw_up", "adamw_down")
    upd = [None] * 5
    parts_f = _exchange_wait(sem_f[0], src_f, land_f, False, grad_x, "scatter_wait_ffn")
    for a, p in zip((2, 3, 4), parts_f):
        upd[a] = _adamw_block(p, big[a], big_m[a], big_v[a], names[a])
    parts_o = _exchange_wait(sem_o[0], src_o, land_o, False, upd[4][0], "scatter_wait_out")
    upd[1] = _adamw_block(parts_o[0], big[1], big_m[1], big_v[1], names[1])
    parts_i = _exchange_wait(sem_i[0], src_i, land_i, False, upd[1][0], "scatter_wait_in")
    upd[0] = _adamw_block(parts_i[0], big[0], big_m[0], big_v[0], names[0])

    g_dec_f = small[:, 0, 0].reshape(1, H)
    g_dec_b = small[:, 1, 0].reshape(1, H)
    g_retw = small[:, 2, :].reshape(1, DA)
    loss_local = jnp.sum(loss_parts[::8, 0])
    zero = jnp.zeros((1,), F32)
    part = _pack_small(g_mix, g_ffn, g_fin, g_retw, g_dec_f, g_dec_b, loss_local)
    sw = _pack_small(norm_mix_w, norm_ffn_w, norm_final_w, ret_norm_w, ret_decay_fwd, ret_decay_bwd, zero)
    sm = _pack_small(m_norm_mix_w, m_norm_ffn_w, m_norm_final_w, m_ret_norm_w, m_ret_decay_fwd, m_ret_decay_bwd, zero)
    sv = _pack_small(v_norm_mix_w, v_norm_ffn_w, v_norm_final_w, v_ret_norm_w, v_ret_decay_fwd, v_ret_decay_bwd, zero)
    shapes = [(1, D), (1, D), (D,), (1, DA), (1, H), (1, H), ()]
    sg, sd, snm, snv = [_unpack_small(t, shapes) for t in _small_step(part, sw, sm, sv)]
    loss = sg[6]

    def ordered(small_set, k):
        b = [u[k][None] for u in upd]
        return [small_set[0], b[0], small_set[4], small_set[5], small_set[3], b[1], small_set[1], b[2], b[3], b[4],
                small_set[2]]

    return (loss, grad_x[None], *ordered(sg, 0), *ordered(sd, 1), *ordered(snm, 2), *ordered(snv, 3))
```

```python
import functools
import math

import numpy as np
import jax
import jax.numpy as jnp
from jax import lax
from jax.experimental import pallas as pl
from jax.experimental.pallas import tpu as pltpu
from jax.experimental.pallas import tpu_sc as plsc

F32 = jnp.float32
BF16 = jnp.bfloat16
SDS = jax.ShapeDtypeStruct

HEAD_DIM = 128
EPS = 1e-6
RET_CHUNK = 128
DILATIONS = (1, 4, 16)
BAND = 64
Q_TILE = 128
K_TILE = Q_TILE + 2 * BAND
KV_PAD = BAND * max(DILATIONS)
TILE_GROUP = 2
NEG = -1e30
N_DEV = 8
N_GROUPS = 7
ADAM_LR, ADAM_B1, ADAM_B2, ADAM_EPS, ADAM_WD, ADAM_STEP = 0.001, 0.9, 0.999, 1e-08, 0.01, 10
VMEM_LIMIT = 56 * 1024 * 1024
MESH = pl.DeviceIdType.MESH
ANY = pl.BlockSpec(memory_space=pl.ANY)


def _cp(n_grid):
    return pltpu.CompilerParams(dimension_semantics=("arbitrary",) * n_grid, vmem_limit_bytes=VMEM_LIMIT)


def _sigmoid(x):
    return 1.0 / (1.0 + jnp.exp(-x))


def _rms_scale(h):
    return lax.rsqrt(jnp.mean(h * h, axis=-1, keepdims=True) + EPS)


def _rms_bwd(dn, h, w):
    r = _rms_scale(h)
    gw = dn * w
    dh = r * gw - h * (r * r * r) * jnp.mean(gw * h, axis=-1, keepdims=True)
    return dh, jnp.sum(dn * h * r, axis=0, keepdims=True)


def _dot(a, b, dims):
    return lax.dot_general(a.astype(BF16), b.astype(BF16), (dims, ((), ())), preferred_element_type=F32)


_NN = ((1,), (0,))
_NT = ((1,), (1,))
_TN = ((0,), (0,))


def _proj_fwd(x, w_norm, w_blk):
    S, D = x.shape
    nblk, _, NB = w_blk.shape
    tm = min(512, S)

    def body(x_ref, wn_ref, w_ref, proj_ref, n_ref, n_scr):
        @pl.when(pl.program_id(1) == 0)
        def _():
            xf = x_ref[...]
            nb = (xf * _rms_scale(xf) * wn_ref[...]).astype(BF16)
            n_scr[...] = nb
            n_ref[...] = nb
        proj_ref[...] = jnp.dot(n_scr[...], w_ref[0], preferred_element_type=F32)

    return pl.pallas_call(
        body, grid=(S // tm, nblk), name="proj_fwd",
        in_specs=[pl.BlockSpec((tm, D), lambda i, j: (i, 0)), pl.BlockSpec((1, D), lambda i, j: (0, 0)),
                  pl.BlockSpec((1, D, NB), lambda i, j: (j, 0, 0))],
        out_specs=[pl.BlockSpec((tm, NB), lambda i, j: (i, j)), pl.BlockSpec((tm, D), lambda i, j: (i, 0))],
        out_shape=[SDS((S, nblk * NB), F32), SDS((S, D), BF16)],
        scratch_shapes=[pltpu.VMEM((tm, D), BF16)], compiler_params=_cp(2))(x, w_norm, w_blk)


def _out_fwd(x, attn, ret, w_out, w_norm):
    S, D = x.shape
    DA = attn.shape[1]
    tm = min(256, S)

    def body(x_ref, a_ref, r_ref, w_ref, wn_ref, h_ref, mix_ref, n_ref):
        a = a_ref[...].astype(BF16)
        r = r_ref[...].astype(BF16)
        mix_ref[:, :DA] = a
        mix_ref[:, DA:] = r
        h = x_ref[...] + jnp.dot(a, w_ref[:DA, :], preferred_element_type=F32) \
            + jnp.dot(r, w_ref[DA:, :], preferred_element_type=F32)
        h_ref[...] = h
        n_ref[...] = (h * _rms_scale(h) * wn_ref[...]).astype(BF16)

    row = lambda w: pl.BlockSpec((tm, w), lambda i: (i, 0))
    return pl.pallas_call(
        body, grid=(S // tm,), name="out_fwd",
        in_specs=[row(D), row(DA), row(D - DA), pl.BlockSpec((D, D), lambda i: (0, 0)),
                  pl.BlockSpec((1, D), lambda i: (0, 0))],
        out_specs=[row(D), row(D), row(D)],
        out_shape=[SDS((S, D), F32), SDS((S, D), BF16), SDS((S, D), BF16)],
        compiler_params=_cp(1))(x, attn, ret, w_out, w_norm)


def _ffn_up(n2, wg, wu):
    S, D = n2.shape
    nblk, _, FB = wg.shape
    tm = min(512, S)

    def body(n_ref, wg_ref, wu_ref, g_ref, u_ref, a_ref):
        n = n_ref[...]
        g = jnp.dot(n, wg_ref[0], preferred_element_type=F32)
        u = jnp.dot(n, wu_ref[0], preferred_element_type=F32)
        g_ref[0] = g
        u_ref[0] = u
        a_ref[0] = (g * _sigmoid(g) * u).astype(BF16)

    wspec = pl.BlockSpec((1, D, FB), lambda j, i: (j, 0, 0))
    ospec = pl.BlockSpec((1, tm, FB), lambda j, i: (j, i, 0))
    return pl.pallas_call(
        body, grid=(nblk, S // tm), name="ffn_up",
        in_specs=[pl.BlockSpec((tm, D), lambda j, i: (i, 0)), wspec, wspec],
        out_specs=[ospec, ospec, ospec],
        out_shape=[SDS((nblk, S, FB), F32), SDS((nblk, S, FB), F32), SDS((nblk, S, FB), BF16)],
        compiler_params=_cp(2))(n2, wg, wu)


def _ffn_down_loss(act, wd, h1, target, w_norm):
    nblk, S, FB = act.shape
    D = h1.shape[1]
    tm = min(512, S)

    def body(a_ref, wd_ref, h_ref, t_ref, wn_ref, dh_ref, loss_ref, dw_ref, acc):
        i, j = pl.program_id(0), pl.program_id(1)

        @pl.when(j == 0)
        def _():
            acc[...] = h_ref[...]

        @pl.when((i == 0) & (j == 0))
        def _():
            dw_ref[...] = jnp.zeros_like(dw_ref)

        acc[...] += jnp.dot(a_ref[0], wd_ref[...], preferred_element_type=F32)

        @pl.when(j == nblk - 1)
        def _():
            h = acc[...]
            w = wn_ref[...]
            err = h * _rms_scale(h) * w - t_ref[...]
            loss_ref[...] = jnp.full(loss_ref.shape, 0.5 * jnp.sum(err * err) / D, F32)
            dh, dw = _rms_bwd(err * (1.0 / D), h, w)
            dh_ref[...] = dh
            dw_ref[...] += dw

    row = pl.BlockSpec((tm, D), lambda i, j: (i, 0))
    vec = pl.BlockSpec((1, D), lambda i, j: (0, 0))
    return pl.pallas_call(
        body, grid=(S // tm, nblk), name="ffn_down_loss",
        in_specs=[pl.BlockSpec((1, tm, FB), lambda i, j: (j, i, 0)), pl.BlockSpec((FB, D), lambda i, j: (j, 0)),
                  row, row, vec],
        out_specs=[row, pl.BlockSpec((8, 128), lambda i, j: (i, 0)), vec],
        out_shape=[SDS((S, D), F32), SDS((S // tm * 8, 128), F32), SDS((1, D), F32)],
        scratch_shapes=[pltpu.VMEM((tm, D), F32)], compiler_params=_cp(2))(act, wd, h1, target, w_norm)


def _attn_bias():
    n_heads = 8
    slopes = np.exp2(-8.0 * np.arange(1, n_heads + 1, dtype=np.float32) / n_heads)
    dist = np.abs(np.arange(K_TILE)[None, :] - BAND - np.arange(Q_TILE)[:, None])
    out = np.empty((n_heads, len(DILATIONS), Q_TILE, K_TILE), np.float32)
    for h in range(n_heads):
        for p, d in enumerate(DILATIONS):
            out[h, p] = np.where(dist <= BAND, -slopes[h] * (d * dist).astype(np.float32), NEG)
    return jnp.asarray(out)


def _attn_tiles(S, d):
    L = S // d
    per_class = L // Q_TILE
    return L, per_class, d * per_class


def _tile_rows(t, d, per_class):
    r = t // per_class
    a = (t % per_class) * Q_TILE
    q_rows = pl.ds(r + d * a, Q_TILE, stride=d) if d > 1 else pl.ds(pl.multiple_of(a, Q_TILE), Q_TILE)
    k_rows = pl.ds(KV_PAD + r + d * (a - BAND), K_TILE, stride=d) if d > 1 else pl.ds(
        pl.multiple_of(KV_PAD + a - BAND, BAND), K_TILE)
    return a, q_rows, k_rows


def _edge_mask(a, L):
    lk = lax.broadcasted_iota(jnp.int32, (1, K_TILE), 1) + (a - BAND)
    return jnp.where((lk >= 0) & (lk < L), 0.0, NEG).astype(F32)


def _fill_padded(dst, src, S):
    dst[pl.ds(0, KV_PAD), :] = jnp.zeros((KV_PAD, HEAD_DIM), F32)
    dst[pl.ds(KV_PAD + S, KV_PAD), :] = jnp.zeros((KV_PAD, HEAD_DIM), F32)
    dst[pl.ds(KV_PAD, S), :] = src[...]


def _head_specs(S, groups, n_heads):
    return [pl.BlockSpec((S, HEAD_DIM), functools.partial(lambda h, g: (0, g * n_heads + h), g=g)) for g in groups]


def _attn_fwd(proj, bias):
    S = proj.shape[0]
    H = proj.shape[1] // (N_GROUPS * HEAD_DIM)
    scale = HEAD_DIM ** -0.5

    def body(q_ref, k_ref, v_ref, b_ref, o_ref, lse_ref, kp, vp, m_run, l_run):
        _fill_padded(kp, k_ref, S)
        _fill_padded(vp, v_ref, S)
        o_ref[...] = jnp.zeros_like(o_ref)
        m_run[...] = jnp.full(m_run.shape, NEG, F32)
        l_run[...] = jnp.zeros_like(l_run)
        for p, d in enumerate(DILATIONS):
            L, per_class, n_tiles = _attn_tiles(S, d)

            def tiles(t, carry, p=p, d=d, L=L, per_class=per_class, n_tiles=n_tiles):
                rows = [_tile_rows(t + u * (n_tiles // TILE_GROUP), d, per_class) for u in range(TILE_GROUP)]
                got = [(q_ref[qr, :], kp[kr, :], vp[kr, :], m_run[qr, :][:, :1], l_run[qr, :][:, :1], o_ref[qr, :])
                       for _, qr, kr in rows]
                new = []
                for (a, _, _), (qt, kt, vt, m_old, l_old, o_old) in zip(rows, got):
                    s = _dot(qt, kt, _NT) * scale + b_ref[0, p] + _edge_mask(a, L)
                    m_new = jnp.maximum(m_old, jnp.max(s, axis=-1, keepdims=True))
                    pr = jnp.exp(s - m_new)
                    alpha = jnp.exp(m_old - m_new)
                    new.append((m_new, alpha * l_old + jnp.sum(pr, axis=-1, keepdims=True),
                                alpha * o_old + _dot(pr, vt, _NN)))
                for (_, qr, _), (m_new, l_new, o_new) in zip(rows, new):
                    o_ref[qr, :] = o_new
                    m_run[qr, :] = jnp.broadcast_to(m_new, (Q_TILE, HEAD_DIM))
                    l_run[qr, :] = jnp.broadcast_to(l_new, (Q_TILE, HEAD_DIM))
                return carry

            lax.fori_loop(0, n_tiles // TILE_GROUP, tiles, 0)
        l = l_run[...]
        o_ref[...] = o_ref[...] / l
        lse_ref[...] = m_run[...] + jnp.log(l)

    hspec = pl.BlockSpec((S, HEAD_DIM), lambda h: (0, h))
    return pl.pallas_call(
        body, grid=(H,), name="attn_fwd",
        in_specs=_head_specs(S, (0, 1, 2), H) + [
            pl.BlockSpec((1, len(DILATIONS), Q_TILE, K_TILE), lambda h: (h, 0, 0, 0))],
        out_specs=[hspec, hspec],
        out_shape=[SDS((S, H * HEAD_DIM), F32), SDS((S, H * HEAD_DIM), F32)],
        scratch_shapes=[pltpu.VMEM((S + 2 * KV_PAD, HEAD_DIM), F32), pltpu.VMEM((S + 2 * KV_PAD, HEAD_DIM), F32),
                        pltpu.VMEM((S, HEAD_DIM), F32), pltpu.VMEM((S, HEAD_DIM), F32)],
        compiler_params=_cp(1))(proj, proj, proj, bias)


def _attn_bwd(proj, out, lse, dmix, bias):
    S = proj.shape[0]
    H = proj.shape[1] // (N_GROUPS * HEAD_DIM)
    scale = HEAD_DIM ** -0.5

    def body(q_ref, k_ref, v_ref, o_ref, lse_ref, do_ref, b_ref, dq_ref, dk_ref, dv_ref, kp, vp, dkp, dvp, dsum):
        _fill_padded(kp, k_ref, S)
        _fill_padded(vp, v_ref, S)
        dkp[...] = jnp.zeros_like(dkp)
        dvp[...] = jnp.zeros_like(dvp)
        dq_ref[...] = jnp.zeros_like(dq_ref)
        dsum[...] = jnp.broadcast_to(jnp.sum(do_ref[...] * o_ref[...], axis=-1, keepdims=True), dsum.shape)
        for p, d in enumerate(DILATIONS):
            L, per_class, n_tiles = _attn_tiles(S, d)

            def tiles(t, carry, p=p, d=d, L=L, per_class=per_class, n_tiles=n_tiles):
                rows = [_tile_rows(t + u * (n_tiles // TILE_GROUP), d, per_class) for u in range(TILE_GROUP)]
                got = [(q_ref[qr, :], kp[kr, :], vp[kr, :], do_ref[qr, :], lse_ref[qr, :][:, :1], dsum[qr, :][:, :1],
                        dq_ref[qr, :], dkp[kr, :], dvp[kr, :]) for _, qr, kr in rows]
                new = []
                for (a, _, _), (qt, kt, vt, dot_, lse_t, dsum_t, dq_old, dk_old, dv_old) in zip(rows, got):
                    s = _dot(qt, kt, _NT) * scale + b_ref[0, p] + _edge_mask(a, L)
                    pr = jnp.exp(s - lse_t)
                    ds = pr * (_dot(dot_, vt, _NT) - dsum_t) * scale
                    new.append((dq_old + _dot(ds, kt, _NN), dk_old + _dot(ds, qt, _TN), dv_old + _dot(pr, dot_, _TN)))
                for (_, qr, kr), (dq_new, dk_new, dv_new) in zip(rows, new):
                    dq_ref[qr, :] = dq_new
                    dkp[kr, :] = dk_new
                    dvp[kr, :] = dv_new
                return carry

            lax.fori_loop(0, n_tiles // TILE_GROUP, tiles, 0)
        dk_ref[...] = dkp[pl.ds(KV_PAD, S), :]
        dv_ref[...] = dvp[pl.ds(KV_PAD, S), :]

    hspec = pl.BlockSpec((S, HEAD_DIM), lambda h: (0, h))
    padded = pltpu.VMEM((S + 2 * KV_PAD, HEAD_DIM), F32)
    return pl.pallas_call(
        body, grid=(H,), name="attn_bwd",
        in_specs=_head_specs(S, (0, 1, 2), H) + [hspec, hspec, hspec,
                                                  pl.BlockSpec((1, len(DILATIONS), Q_TILE, K_TILE), lambda h: (h, 0, 0, 0))],
        out_specs=[hspec, hspec, hspec],
        out_shape=[SDS((S, H * HEAD_DIM), F32)] * 3,
        scratch_shapes=[padded, padded, padded, padded, pltpu.VMEM((S, HEAD_DIM), F32)],
        compiler_params=_cp(1))(proj, proj, proj, out, lse, dmix, bias)


def _ret_consts(lg, forward):
    C = RET_CHUNK
    i = lax.broadcasted_iota(jnp.int32, (C, C), 0)
    j = lax.broadcasted_iota(jnp.int32, (C, C), 1)
    rel = (i - j) if forward else (j - i)
    inside = (rel >= 0) if forward else (rel > 0)
    relf = jnp.maximum(rel, 0).astype(F32)
    mask = jnp.where(inside, jnp.exp(lg * relf), 0.0)
    idx = lax.broadcasted_iota(jnp.int32, (C, 1), 0).astype(F32)
    q_exp = (idx + 1.0) if forward else (C - idx)
    k_exp = (C - 1.0 - idx) if forward else idx
    return mask, relf, jnp.exp(lg * q_exp), q_exp, jnp.exp(lg * k_exp), k_exp, jnp.exp(lg * C)


def _log_decay(dec_ref, h):
    return -jnp.exp(jnp.full((1, 1), dec_ref[0, h], F32))


def _chunk(c):
    return pl.ds(pl.multiple_of(c * RET_CHUNK, RET_CHUNK), RET_CHUNK)


def _ret_fwd(proj, dec_f, dec_b, w_norm):
    S = proj.shape[0]
    H = proj.shape[1] // (N_GROUPS * HEAD_DIM)
    nc = S // RET_CHUNK
    scale = HEAD_DIM ** -0.5

    def body(df_ref, db_ref, q_ref, k_ref, v_ref, g_ref, w_ref, y_ref, o_ref):
        h = pl.program_id(0)
        consts = [_ret_consts(_log_decay(dref, h), fw) for fw, dref in ((True, df_ref), (False, db_ref))]
        o_ref[...] = jnp.zeros_like(o_ref)

        def step(n, states):
            rows = [_chunk(n), _chunk(nc - 1 - n)]
            got = [(q_ref[r, :] * scale, k_ref[r, :], v_ref[r, :], o_ref[r, :]) for r in rows]
            new_o, new_states = [], []
            for (qc, kc, vc, o_old), (mask, _, q_dec, _, k_dec, _, c_dec), state in zip(got, consts, states):
                new_o.append(o_old + _dot(_dot(qc, kc, _NT) * mask, vc, _NN) + _dot(qc * q_dec, state, _NN))
                new_states.append(state * c_dec + _dot(kc * k_dec, vc, _TN))
            for r, o_new in zip(rows, new_o):
                o_ref[r, :] = o_new
            return tuple(new_states)

        lax.fori_loop(0, nc, step, (jnp.zeros((HEAD_DIM, HEAD_DIM), F32),) * 2)
        o = o_ref[...]
        g = g_ref[...]
        y_ref[...] = o * _rms_scale(o) * w_ref[...] * (g * _sigmoid(g))

    hspec = pl.BlockSpec((S, HEAD_DIM), lambda h: (0, h))
    smem = pl.BlockSpec(memory_space=pltpu.SMEM)
    return pl.pallas_call(
        body, grid=(H,), name="ret_fwd",
        in_specs=[smem, smem] + _head_specs(S, (3, 4, 5, 6), H) + [pl.BlockSpec((1, HEAD_DIM), lambda h: (0, h))],
        out_specs=[hspec, hspec],
        out_shape=[SDS((S, H * HEAD_DIM), F32)] * 2,
        compiler_params=_cp(1))(dec_f, dec_b, proj, proj, proj, proj, w_norm)


def _ret_bwd(proj, o_raw, dmix, dec_f, dec_b, w_norm, col0):
    S = proj.shape[0]
    H = proj.shape[1] // (N_GROUPS * HEAD_DIM)
    C = RET_CHUNK
    nc = S // C
    scale = HEAD_DIM ** -0.5

    def body(df_ref, db_ref, q_ref, k_ref, v_ref, g_ref, o_ref, dy_ref, w_ref,
             dq_ref, dk_ref, dv_ref, dg_ref, small_ref, do, states):
        h = pl.program_id(0)
        o = o_ref[...]
        g = g_ref[...]
        dy = dy_ref[...]
        w = w_ref[...]
        rr = _rms_scale(o)
        normed = o * rr
        sg = _sigmoid(g)
        silu = g * sg
        small_ref[0, pl.ds(2, 1), :] = jnp.sum(dy * normed * silu, axis=0, keepdims=True)
        dg_ref[...] = dy * normed * w * (sg * (1.0 + g * (1.0 - sg)))
        dnormed = dy * w * silu
        do[...] = rr * dnormed - o * (rr * rr * rr) * jnp.mean(dnormed * o, axis=-1, keepdims=True)

        lgs = [_log_decay(df_ref, h), _log_decay(db_ref, h)]
        consts = [_ret_consts(lg, fw) for lg, fw in zip(lgs, (True, False))]
        zero_state = jnp.zeros((HEAD_DIM, HEAD_DIM), F32)

        def fwd_step(n, carry):
            new = []
            for way, (cidx, state) in enumerate(zip((n, nc - 1 - n), carry)):
                k_dec, c_dec = consts[way][4], consts[way][6]
                rows = _chunk(cidx)
                states[way, cidx] = state
                new.append(state * c_dec + _dot(k_ref[rows, :] * k_dec, v_ref[rows, :], _TN))
            return tuple(new)

        lax.fori_loop(0, nc, fwd_step, (zero_state, zero_state))
        for ref in (dq_ref, dk_ref, dv_ref):
            ref[...] = jnp.zeros_like(ref)

        def bwd_step(n, carry):
            cidxs = (nc - 1 - n, n)
            rows = [_chunk(c) for c in cidxs]
            got = [(q_ref[r, :] * scale, k_ref[r, :], v_ref[r, :], do[r, :], states[way, c], dq_ref[r, :], dk_ref[r, :],
                    dv_ref[r, :]) for way, (r, c) in enumerate(zip(rows, cidxs))]
            new_rows, new_carry = [], []
            for (qc, kc, vc, doc, state, dq_old, dk_old, dv_old), cs, (d_state, dlam) in zip(got, consts, carry):
                mask, relf, q_dec, q_exp, k_dec, k_exp, c_dec = cs
                a0 = _dot(qc, kc, _NT)
                dp = _dot(doc, vc, _NT) * mask
                gq = _dot(doc, state, _NT)
                gk = _dot(vc, d_state, _NT)
                new_rows.append((dq_old + (_dot(dp, kc, _NN) + q_dec * gq) * scale,
                                 dk_old + _dot(dp, qc, _TN) + k_dec * gk,
                                 dv_old + _dot(a0 * mask, doc, _TN) + _dot(kc * k_dec, d_state, _NN)))
                dlam = dlam + jnp.sum(relf * a0 * dp, axis=0, keepdims=True) \
                    + jnp.sum(q_exp * q_dec * qc * gq + k_exp * k_dec * kc * gk, axis=0, keepdims=True) \
                    + (C * c_dec) * jnp.sum(state * d_state, axis=0, keepdims=True)
                new_carry.append((d_state * c_dec + _dot(qc * q_dec, doc, _TN), dlam))
            for r, (dq_new, dk_new, dv_new) in zip(rows, new_rows):
                dq_ref[r, :] = dq_new
                dk_ref[r, :] = dk_new
                dv_ref[r, :] = dv_new
            return tuple(new_carry)

        zero_carry = (zero_state, jnp.zeros((1, HEAD_DIM), F32))
        done = lax.fori_loop(0, nc, bwd_step, (zero_carry, zero_carry))
        for row, ((_, dlam), lg) in enumerate(zip(done, lgs)):
            small_ref[0, pl.ds(row, 1), :] = jnp.broadcast_to(jnp.sum(dlam, axis=-1, keepdims=True) * lg, (1, HEAD_DIM))
        small_ref[0, pl.ds(3, 5), :] = jnp.zeros((5, HEAD_DIM), F32)

    hspec = pl.BlockSpec((S, HEAD_DIM), lambda h: (0, h))
    smem = pl.BlockSpec(memory_space=pltpu.SMEM)
    nh0 = col0 // HEAD_DIM
    return pl.pallas_call(
        body, grid=(H,), name="ret_bwd",
        in_specs=[smem, smem] + _head_specs(S, (3, 4, 5, 6), H) + [
            hspec, pl.BlockSpec((S, HEAD_DIM), lambda h: (0, nh0 + h)), pl.BlockSpec((1, HEAD_DIM), lambda h: (0, h))],
        out_specs=[hspec, hspec, hspec, hspec, pl.BlockSpec((1, 8, HEAD_DIM), lambda h: (h, 0, 0))],
        out_shape=[SDS((S, H * HEAD_DIM), F32)] * 4 + [SDS((H, 8, HEAD_DIM), F32)],
        scratch_shapes=[pltpu.VMEM((S, HEAD_DIM), F32), pltpu.VMEM((2, nc, HEAD_DIM, HEAD_DIM), F32)],
        compiler_params=_cp(1))(dec_f, dec_b, proj, proj, proj, proj, o_raw, dmix, w_norm)


def _ffn_bwd_act(dh2, wd, g, u):
    S, D = dh2.shape
    nblk, _, FB = g.shape
    tm = min(512, S)

    def body(dh_ref, wd_ref, g_ref, u_ref, dg_ref, du_ref):
        dact = _dot(dh_ref[...], wd_ref[...], _NT)
        gg = g_ref[0]
        sg = _sigmoid(gg)
        dg_ref[0] = (dact * u_ref[0] * (sg * (1.0 + gg * (1.0 - sg)))).astype(BF16)
        du_ref[0] = (dact * (gg * sg)).astype(BF16)

    blk = pl.BlockSpec((1, tm, FB), lambda j, i: (j, i, 0))
    return pl.pallas_call(
        body, grid=(nblk, S // tm), name="ffn_bwd_act",
        in_specs=[pl.BlockSpec((tm, D), lambda j, i: (i, 0)), pl.BlockSpec((FB, D), lambda j, i: (j, 0)), blk, blk],
        out_specs=[blk, blk], out_shape=[SDS((nblk, S, FB), BF16)] * 2,
        compiler_params=_cp(2))(dh2, wd, g, u)


def _ffn_bwd_in(dg, du, wg, wu, h1, dh2, w_norm):
    nblk, S, FB = dg.shape
    D = h1.shape[1]
    tm = min(256, S)

    def body(dg_ref, du_ref, wg_ref, wu_ref, h_ref, dh2_ref, wn_ref, dh_ref, dw_ref, acc):
        i, j = pl.program_id(0), pl.program_id(1)

        @pl.when(j == 0)
        def _():
            acc[...] = jnp.zeros_like(acc)

        @pl.when((i == 0) & (j == 0))
        def _():
            dw_ref[...] = jnp.zeros_like(dw_ref)

        acc[...] += _dot(dg_ref[0], wg_ref[0], _NT) + _dot(du_ref[0], wu_ref[0], _NT)

        @pl.when(j == nblk - 1)
        def _():
            dh, dw = _rms_bwd(acc[...], h_ref[...], wn_ref[...])
            dh_ref[...] = dh2_ref[...] + dh
            dw_ref[...] += dw

    blk = pl.BlockSpec((1, tm, FB), lambda i, j: (j, i, 0))
    wspec = pl.BlockSpec((1, D, FB), lambda i, j: (j, 0, 0))
    row = pl.BlockSpec((tm, D), lambda i, j: (i, 0))
    vec = pl.BlockSpec((1, D), lambda i, j: (0, 0))
    return pl.pallas_call(
        body, grid=(S // tm, nblk), name="ffn_bwd_in",
        in_specs=[blk, blk, wspec, wspec, row, row, vec],
        out_specs=[row, vec], out_shape=[SDS((S, D), F32), SDS((1, D), F32)],
        scratch_shapes=[pltpu.VMEM((tm, D), F32)], compiler_params=_cp(2))(dg, du, wg, wu, h1, dh2, w_norm)


def _dmix(dh1, w_out):
    S, D = dh1.shape
    tm = min(512, S)

    def body(dh_ref, w_ref, o_ref):
        o_ref[...] = _dot(dh_ref[...], w_ref[...], _NT)

    row = pl.BlockSpec((tm, D), lambda i: (i, 0))
    return pl.pallas_call(
        body, grid=(S // tm,), name="dmix", in_specs=[row, pl.BlockSpec((D, D), lambda i: (0, 0))],
        out_specs=row, out_shape=SDS((S, D), F32), compiler_params=_cp(1))(dh1, w_out)


def _in_bwd(dproj, w_blk, x, dh1, w_norm):
    S, D = x.shape
    nblk, _, NB = w_blk.shape
    tm = min(512, S)

    def body(dp_ref, w_ref, x_ref, dh1_ref, wn_ref, dx_ref, dw_ref, acc):
        i, j = pl.program_id(0), pl.program_id(1)

        @pl.when(j == 0)
        def _():
            acc[...] = jnp.zeros_like(acc)

        @pl.when((i == 0) & (j == 0))
        def _():
            dw_ref[...] = jnp.zeros_like(dw_ref)

        acc[...] += _dot(dp_ref[...], w_ref[0], _NT)

        @pl.when(j == nblk - 1)
        def _():
            dh, dw = _rms_bwd(acc[...], x_ref[...], wn_ref[...])
            dx_ref[...] = dh1_ref[...] + dh
            dw_ref[...] += dw

    row = pl.BlockSpec((tm, D), lambda i, j: (i, 0))
    vec = pl.BlockSpec((1, D), lambda i, j: (0, 0))
    return pl.pallas_call(
        body, grid=(S // tm, nblk), name="in_bwd",
        in_specs=[pl.BlockSpec((tm, NB), lambda i, j: (i, j)), pl.BlockSpec((1, D, NB), lambda i, j: (j, 0, 0)),
                  row, row, vec],
        out_specs=[row, vec], out_shape=[SDS((S, D), F32), SDS((1, D), F32)],
        scratch_shapes=[pltpu.VMEM((tm, D), F32)], compiler_params=_cp(2))(dproj, w_blk, x, dh1, w_norm)


def _wgrad(a, b, a_spec, b_spec, o_spec, o_shape, grid, name):
    nk = grid[-1]

    def ld(ref):
        return ref[0] if len(ref.shape) == 3 else ref[...]

    def body(a_ref, b_ref, o_ref, acc):
        k = pl.program_id(len(grid) - 1)

        @pl.when(k == 0)
        def _():
            acc[...] = jnp.zeros_like(acc)

        acc[...] += _dot(ld(a_ref), ld(b_ref), _TN)

        @pl.when(k == nk - 1)
        def _():
            if len(o_ref.shape) == 3:
                o_ref[0] = acc[...].astype(o_ref.dtype)
            else:
                o_ref[...] = acc[...].astype(o_ref.dtype)

    return pl.pallas_call(
        body, grid=grid, name=name, in_specs=[a_spec, b_spec], out_specs=o_spec, out_shape=SDS(o_shape, BF16),
        scratch_shapes=[pltpu.VMEM(o_spec.block_shape[-2:], F32)], compiler_params=_cp(len(grid)))(a, b)


def _peer(k):
    x, y, c = lax.axis_index("x"), lax.axis_index("y"), lax.axis_index("c")
    px = 1 - x if k & 4 else x
    py = 1 - y if k & 2 else y
    pc = 1 - c if k & 1 else c
    return (px, py, pc), 4 * px + 2 * py + pc


HBM = pl.BlockSpec(memory_space=pltpu.HBM)
SEM = pl.BlockSpec(memory_space=pltpu.SEMAPHORE)
EFFECT = pltpu.SideEffectType.DATAFLOW_SIDE_EFFECTING


def _exchange_copies(srcs, lands, send_sems, recv_sems, which, gather):
    _, me = _peer(0)
    pairs = []
    for pos, a in enumerate(which):
        for k in range(1, N_DEV):
            dev, idx = _peer(k)
            sem = pos * (N_DEV - 1) + k - 1
            src = srcs[a] if gather else srcs[a].at[idx]
            mk = functools.partial(pltpu.make_async_remote_copy, src_ref=src, send_sem=send_sems.at[sem],
                                   recv_sem=recv_sems.at[sem], device_id=dev, device_id_type=MESH)
            pairs.append((mk(dst_ref=lands[a].at[me]), mk(dst_ref=lands[a].at[idx])))
    return pairs


def _exchange_start(arrays, gather, groups, name):
    n, ng = len(arrays), len(groups)
    lands = [lax.empty((N_DEV,) + a.shape if gather else a.shape, a.dtype) for a in arrays]

    def body(*refs):
        srcs, dsts = refs[:n], refs[n:2 * n]
        sems = refs[2 * n:2 * n + 2 * ng]
        token, local_sems = refs[-2], refs[-1]
        _, me = _peer(0)
        local = [pltpu.make_async_copy(srcs[a] if gather else srcs[a].at[me], dsts[a].at[me], local_sems.at[a])
                 for a in range(n)]
        for cp in local:
            cp.start()
        for cp in local:
            cp.wait()
        for g, which in enumerate(groups):
            for out, _ in _exchange_copies(srcs, dsts, sems[2 * g], sems[2 * g + 1], which, gather):
                out.start()
        token[...] = jnp.zeros_like(token)

    sem_shapes = []
    for which in groups:
        sem_shapes += [pltpu.SemaphoreType.DMA((len(which) * (N_DEV - 1),))] * 2
    thru = [pltpu.HBM(a.shape, a.dtype) for a in arrays] + [pltpu.HBM(l.shape, l.dtype) for l in lands]
    outs = pl.pallas_call(
        body, name=name, in_specs=[HBM] * (2 * n),
        out_specs=[SEM] * (2 * ng) + [HBM] * (2 * n) + [pl.BlockSpec(memory_space=pltpu.VMEM)],
        out_shape=sem_shapes + thru + [SDS((8, 128), F32)],
        input_output_aliases={i: 2 * ng + i for i in range(2 * n)},
        scratch_shapes=[pltpu.SemaphoreType.DMA((n,))],
        compiler_params=pltpu.CompilerParams(has_side_effects=EFFECT),
    )(*[pltpu.with_memory_space_constraint(t, pltpu.HBM) for t in list(arrays) + lands])
    sems = [(outs[2 * g], outs[2 * g + 1]) for g in range(ng)]
    return sems, list(outs[2 * ng:2 * ng + n]), list(outs[2 * ng + n:2 * ng + 2 * n]), outs[-1]


def _exchange_wait(sems, srcs, lands, gather, after, name):
    n = len(srcs)
    send_sems, recv_sems = sems

    def body(*refs):
        s, d = refs[:n], refs[n:2 * n]
        for out, arrival in _exchange_copies(s, d, refs[2 * n], refs[2 * n + 1], range(n), gather):
            out.wait_send()
            arrival.wait_recv()

    outs = pl.pallas_call(
        body, name=name, in_specs=[HBM] * (2 * n) + [SEM, SEM, ANY], out_specs=[HBM] * (2 * n),
        out_shape=[pltpu.HBM(t.shape, t.dtype) for t in list(srcs) + list(lands)],
        input_output_aliases={i: i for i in range(2 * n)},
        compiler_params=pltpu.CompilerParams(has_side_effects=EFFECT),
    )(*srcs, *lands, send_sems, recv_sems, after)
    return list(outs[n:])


def _sequencer_exchange(arrays, gather, name, collective_id):
    n = len(arrays)
    hbm = pltpu.MemorySpace.HBM
    srcs = [jax.new_ref(a, memory_space=hbm) for a in arrays]
    lands = [jax.empty_ref(SDS((N_DEV,) + a.shape if gather else a.shape, a.dtype), memory_space=hbm) for a in arrays]
    n_sem = n * (N_DEV - 1)

    @pl.kernel(mesh=plsc.ScalarSubcoreMesh(axis_name="sequencer", num_cores=1), name=name,
               scratch_types=(pltpu.SemaphoreType.DMA((n_sem,)), pltpu.SemaphoreType.DMA((n_sem,)),
                              pltpu.SemaphoreType.DMA((n,))),
               compiler_params=pltpu.CompilerParams(collective_id=collective_id))
    def launch(send_sems, recv_sems, local_sems):
        barrier = pltpu.get_barrier_semaphore()
        for k in range(1, N_DEV):
            pl.semaphore_signal(barrier, inc=1, device_id=_peer(k)[0], device_id_type=MESH)
        pl.semaphore_wait(barrier, N_DEV - 1)
        _, me = _peer(0)
        local = [pltpu.make_async_copy(srcs[a] if gather else srcs[a].at[me], lands[a].at[me], local_sems.at[a])
                 for a in range(n)]
        pairs = _exchange_copies(srcs, lands, send_sems, recv_sems, range(n), gather)
        for out, _ in pairs:
            out.start()
        for cp in local:
            cp.start()
        for out, arrival in pairs:
            out.wait_send()
            arrival.wait_recv()
        for cp in local:
            cp.wait()

    launch()
    return [r[...] for r in lands]


SMALL_ROWS = 64


def _small_step(part, w, m, v):
    def body(p_ref, w_ref, m_ref, v_ref, g_ref, d_ref, nm_ref, nv_ref, gath, send_sems, recv_sems):
        _, me = _peer(0)
        gath[me] = p_ref[...]
        copies = []
        for k in range(1, N_DEV):
            dev, idx = _peer(k)
            out = pltpu.make_async_remote_copy(src_ref=p_ref, dst_ref=gath.at[me], send_sem=send_sems.at[k - 1],
                                               recv_sem=recv_sems.at[k - 1], device_id=dev, device_id_type=MESH)
            out.start()
            arrival = pltpu.make_async_remote_copy(src_ref=p_ref, dst_ref=gath.at[idx], send_sem=send_sems.at[k - 1],
                                                   recv_sem=recv_sems.at[k - 1], device_id=dev, device_id_type=MESH)
            copies.append((out, arrival))
        for out, arrival in copies:
            out.wait_send()
            arrival.wait_recv()
        g = gath[0]
        for p in range(1, N_DEV):
            g = g + gath[p]
        g_ref[...] = g
        d_ref[...], nm_ref[...], nv_ref[...] = _adamw(w_ref[...], g, m_ref[...], v_ref[...])

    vm = pl.BlockSpec(memory_space=pltpu.VMEM)
    return pl.pallas_call(
        body, name="small_step", in_specs=[vm] * 4, out_specs=[vm] * 4,
        out_shape=[SDS((SMALL_ROWS, 128), F32)] * 4,
        scratch_shapes=[pltpu.VMEM((N_DEV, SMALL_ROWS, 128), F32), pltpu.SemaphoreType.DMA((N_DEV - 1,)),
                        pltpu.SemaphoreType.DMA((N_DEV - 1,))])(part, w, m, v)


def _adamw(w, g, m, v):
    m = ADAM_B1 * m + (1.0 - ADAM_B1) * g
    v = ADAM_B2 * v + (1.0 - ADAM_B2) * (g * g)
    m_hat = m / (1.0 - ADAM_B1 ** ADAM_STEP)
    v_hat = v / (1.0 - ADAM_B2 ** ADAM_STEP)
    delta = -ADAM_LR * (m_hat / (jnp.sqrt(v_hat) + ADAM_EPS) + ADAM_WD * w)
    return delta, m, v


def _adamw_block(parts, w, m, v, name):
    R, C = w.shape
    tr = next(t for t in (256, 128, 64, 32, 16, 8) if R % t == 0 and t * C <= 256 * 1024)

    def body(p_ref, w_ref, m_ref, v_ref, g_ref, d_ref, nm_ref, nv_ref):
        g = p_ref[0].astype(F32)
        for p in range(1, N_DEV):
            g = g + p_ref[p].astype(F32)
        g_ref[...] = g
        d_ref[...], nm_ref[...], nv_ref[...] = _adamw(w_ref[...], g, m_ref[...], v_ref[...])

    row = pl.BlockSpec((tr, C), lambda i: (i, 0))
    return pl.pallas_call(
        body, grid=(R // tr,), name=name, in_specs=[pl.BlockSpec((N_DEV, tr, C), lambda i: (0, i, 0)), row, row, row],
        out_specs=[row] * 4, out_shape=[SDS((R, C), F32)] * 4, compiler_params=_cp(1))(parts, w, m, v)


def _pack_small(mix, ffn, fin, retw, dec_f, dec_b, loss):
    flat = jnp.concatenate([mix.reshape(-1), ffn.reshape(-1), fin.reshape(-1), retw.reshape(-1), dec_f.reshape(-1),
                            dec_b.reshape(-1), loss.reshape(-1)])
    return jnp.pad(flat, (0, SMALL_ROWS * 128 - flat.shape[0])).reshape(SMALL_ROWS, 128)


def _unpack_small(packed, shapes):
    flat = packed.reshape(-1)
    out, at = [], 0
    for s in shapes:
        n = math.prod(s)
        out.append(flat[at:at + n].reshape(s))
        at += n
    return out


def kernel(x, norm_mix_w, w_in, ret_decay_fwd, ret_decay_bwd, ret_norm_w, w_out, norm_ffn_w, w_gate, w_up, w_down, norm_final_w, loss_target, m_norm_mix_w, m_w_in, m_ret_decay_fwd, m_ret_decay_bwd, m_ret_norm_w, m_w_out, m_norm_ffn_w, m_w_gate, m_w_up, m_w_down, m_norm_final_w, v_norm_mix_w, v_w_in, v_ret_decay_fwd, v_ret_decay_bwd, v_ret_norm_w, v_w_out, v_norm_ffn_w, v_w_gate, v_w_up, v_w_down, v_norm_final_w):
    x2 = x[0]
    tgt = loss_target[0]
    S, D = x2.shape
    H = ret_norm_w.shape[1] // HEAD_DIM
    DA = H * HEAD_DIM
    fin_w = norm_final_w.reshape(1, D)
    big = (w_in[0], w_out[0], w_gate[0], w_up[0], w_down[0])

    big_b = [w.astype(BF16) for w in big]
    wi, = _sequencer_exchange(big_b[:1], True, "gather_in", 0)
    wo, wg, wu, wd = _sequencer_exchange(big_b[1:], True, "gather_rest", 1)
    NB = wi.shape[2]

    proj, n1 = _proj_fwd(x2, norm_mix_w, wi)
    bias = _attn_bias()[:H]
    attn, lse = _attn_fwd(proj, bias)
    ret, o_raw = _ret_fwd(proj, ret_decay_fwd, ret_decay_bwd, ret_norm_w)
    wo_full = wo.reshape(D, D)
    FB = wd.shape[1]
    wd_full = wd.reshape(N_DEV * FB, D)
    h1, mixed, n2 = _out_fwd(x2, attn, ret, wo_full, norm_ffn_w)
    gate, up, act = _ffn_up(n2, wg, wu)
    dh2, loss_parts, g_fin = _ffn_down_loss(act, wd_full, h1, tgt, fin_w)

    dgate, dup = _ffn_bwd_act(dh2, wd_full, gate, up)
    tk = min(512, S)
    nk = S // tk
    g_wd = _wgrad(act, dh2, pl.BlockSpec((1, tk, FB), lambda j, k: (j, k, 0)), pl.BlockSpec((tk, D), lambda j, k: (k, 0)),
                  pl.BlockSpec((1, FB, D), lambda j, k: (j, 0, 0)), (N_DEV, FB, D), (N_DEV, nk), "wgrad_down")
    tmw = min(1024, D)
    gu_specs = (pl.BlockSpec((tk, tmw), lambda j, m, k: (k, m)), pl.BlockSpec((1, tk, FB), lambda j, m, k: (j, k, 0)),
                pl.BlockSpec((1, tmw, FB), lambda j, m, k: (j, m, 0)), (N_DEV, D, FB), (N_DEV, D // tmw, nk))
    g_wg = _wgrad(n2, dgate, *gu_specs, "wgrad_gate")
    g_wu = _wgrad(n2, dup, *gu_specs, "wgrad_up")
    parts_f = _sequencer_exchange([g_wg, g_wu, g_wd], False, "scatter_ffn", 2)
    dh1, g_ffn = _ffn_bwd_in(dgate, dup, wg, wu, h1, dh2, norm_ffn_w)
    dmix = _dmix(dh1, wo_full)
    g_wo = _wgrad(mixed, dh1, pl.BlockSpec((tk, tmw), lambda m, k: (k, m)), pl.BlockSpec((tk, D), lambda m, k: (k, 0)),
                  pl.BlockSpec((tmw, D), lambda m, k: (m, 0)), (D, D), (D // tmw, nk), "wgrad_out")
    parts_o = _sequencer_exchange([g_wo.reshape(N_DEV, D // N_DEV, D)], False, "scatter_out", 3)
    dq_r, dk_r, dv_r, dg_r, small = _ret_bwd(proj, o_raw, dmix, ret_decay_fwd, ret_decay_bwd, ret_norm_w, DA)
    dq_a, dk_a, dv_a = _attn_bwd(proj, attn, lse, dmix, bias)
    dproj = jnp.concatenate([t.astype(BF16) for t in (dq_a, dk_a, dv_a, dq_r, dk_r, dv_r, dg_r)], axis=1)
    g_wi = _wgrad(n1, dproj, pl.BlockSpec((tk, tmw), lambda j, m, k: (k, m)), pl.BlockSpec((tk, NB), lambda j, m, k: (k, j)),
                  pl.BlockSpec((1, tmw, NB), lambda j, m, k: (j, m, 0)), (N_DEV, D, NB), (N_DEV, D // tmw, nk), "wgrad_in")
    parts_i = _sequencer_exchange([g_wi], False, "scatter_in", 4)
    grad_x, g_mix = _in_bwd(dproj, wi, x2, dh1, norm_mix_w)

    big_m = (m_w_in[0], m_w_out[0], m_w_gate[0], m_w_up[0], m_w_down[0])
    big_v = (v_w_in[0], v_w_out[0], v_w_gate[0], v_w_up[0], v_w_down[0])
    names = ("adamw_in", "adamw_out", "adamw_gate", "adamw_up", "adamw_down")
    upd = [None] * 5
    for a, p in zip((2, 3, 4, 1, 0), parts_f + parts_o + parts_i):
        upd[a] = _adamw_block(p, big[a], big_m[a], big_v[a], names[a])

    g_dec_f = small[:, 0, 0].reshape(1, H)
    g_dec_b = small[:, 1, 0].reshape(1, H)
    g_retw = small[:, 2, :].reshape(1, DA)
    loss_local = jnp.sum(loss_parts[::8, 0])
    zero = jnp.zeros((1,), F32)
    part = _pack_small(g_mix, g_ffn, g_fin, g_retw, g_dec_f, g_dec_b, loss_local)
    sw = _pack_small(norm_mix_w, norm_ffn_w, norm_final_w, ret_norm_w, ret_decay_fwd, ret_decay_bwd, zero)
    sm = _pack_small(m_norm_mix_w, m_norm_ffn_w, m_norm_final_w, m_ret_norm_w, m_ret_decay_fwd, m_ret_decay_bwd, zero)
    sv = _pack_small(v_norm_mix_w, v_norm_ffn_w, v_norm_final_w, v_ret_norm_w, v_ret_decay_fwd, v_ret_decay_bwd, zero)
    shapes = [(1, D), (1, D), (D,), (1, DA), (1, H), (1, H), ()]
    sg, sd, snm, snv = [_unpack_small(t, shapes) for t in _small_step(part, sw, sm, sv)]
    loss = sg[6]

    def ordered(small_set, k):
        b = [u[k][None] for u in upd]
        return [small_set[0], b[0], small_set[4], small_set[5], small_set[3], b[1], small_set[1], b[2], b[3], b[4],
                small_set[2]]

    return (loss, grad_x[None], *ordered(sg, 0), *ordered(sd, 1), *ordered(snm, 2), *ordered(snv, 3))
```

```python
import functools
import math

import numpy as np
import jax
import jax.numpy as jnp
from jax import lax
from jax.experimental import pallas as pl
from jax.experimental.pallas import tpu as pltpu
from jax.experimental.pallas import tpu_sc as plsc

F32 = jnp.float32
BF16 = jnp.bfloat16
SDS = jax.ShapeDtypeStruct

HEAD_DIM = 128
EPS = 1e-6
RET_CHUNK = 128
DILATIONS = (1, 4, 16)
BAND = 64
Q_TILE = 128
K_TILE = Q_TILE + 2 * BAND
KV_PAD = BAND * max(DILATIONS)
TILE_GROUP = 2
NEG = -1e30
N_DEV = 8
N_GROUPS = 7
ADAM_LR, ADAM_B1, ADAM_B2, ADAM_EPS, ADAM_WD, ADAM_STEP = 0.001, 0.9, 0.999, 1e-08, 0.01, 10
VMEM_LIMIT = 56 * 1024 * 1024
MESH = pl.DeviceIdType.MESH
ANY = pl.BlockSpec(memory_space=pl.ANY)


def _cp(n_grid):
    return pltpu.CompilerParams(dimension_semantics=("arbitrary",) * n_grid, vmem_limit_bytes=VMEM_LIMIT)


def _sigmoid(x):
    return 1.0 / (1.0 + jnp.exp(-x))


def _rms_scale(h):
    return lax.rsqrt(jnp.mean(h * h, axis=-1, keepdims=True) + EPS)


def _rms_bwd(dn, h, w):
    r = _rms_scale(h)
    gw = dn * w
    dh = r * gw - h * (r * r * r) * jnp.mean(gw * h, axis=-1, keepdims=True)
    return dh, jnp.sum(dn * h * r, axis=0, keepdims=True)


def _dot(a, b, dims):
    return lax.dot_general(a.astype(BF16), b.astype(BF16), (dims, ((), ())), preferred_element_type=F32)


_NN = ((1,), (0,))
_NT = ((1,), (1,))
_TN = ((0,), (0,))


def _proj_fwd(x, w_norm, w_blk):
    S, D = x.shape
    nblk, _, NB = w_blk.shape
    tm = min(1024, S)

    def body(x_ref, wn_ref, w_ref, proj_ref, n_ref, n_scr):
        @pl.when(pl.program_id(1) == 0)
        def _():
            xf = x_ref[...]
            nb = (xf * _rms_scale(xf) * wn_ref[...]).astype(BF16)
            n_scr[...] = nb
            n_ref[...] = nb
        proj_ref[...] = jnp.dot(n_scr[...], w_ref[0], preferred_element_type=F32)

    return pl.pallas_call(
        body, grid=(S // tm, nblk), name="proj_fwd",
        in_specs=[pl.BlockSpec((tm, D), lambda i, j: (i, 0)), pl.BlockSpec((1, D), lambda i, j: (0, 0)),
                  pl.BlockSpec((1, D, NB), lambda i, j: (j, 0, 0))],
        out_specs=[pl.BlockSpec((tm, NB), lambda i, j: (i, j)), pl.BlockSpec((tm, D), lambda i, j: (i, 0))],
        out_shape=[SDS((S, nblk * NB), F32), SDS((S, D), BF16)],
        scratch_shapes=[pltpu.VMEM((tm, D), BF16)], compiler_params=_cp(2))(x, w_norm, w_blk)


def _out_fwd(x, attn, ret, w_out, w_norm):
    S, D = x.shape
    DA = attn.shape[1]
    tm = min(256, S)

    def body(x_ref, a_ref, r_ref, w_ref, wn_ref, h_ref, mix_ref, n_ref):
        a = a_ref[...].astype(BF16)
        r = r_ref[...].astype(BF16)
        mix_ref[:, :DA] = a
        mix_ref[:, DA:] = r
        h = x_ref[...] + jnp.dot(a, w_ref[:DA, :], preferred_element_type=F32) \
            + jnp.dot(r, w_ref[DA:, :], preferred_element_type=F32)
        h_ref[...] = h
        n_ref[...] = (h * _rms_scale(h) * wn_ref[...]).astype(BF16)

    row = lambda w: pl.BlockSpec((tm, w), lambda i: (i, 0))
    return pl.pallas_call(
        body, grid=(S // tm,), name="out_fwd",
        in_specs=[row(D), row(DA), row(D - DA), pl.BlockSpec((D, D), lambda i: (0, 0)),
                  pl.BlockSpec((1, D), lambda i: (0, 0))],
        out_specs=[row(D), row(D), row(D)],
        out_shape=[SDS((S, D), F32), SDS((S, D), BF16), SDS((S, D), BF16)],
        compiler_params=_cp(1))(x, attn, ret, w_out, w_norm)


def _ffn_up(n2, wg, wu):
    S, D = n2.shape
    nblk, _, FB = wg.shape
    tm = min(512, S)

    def body(n_ref, wg_ref, wu_ref, g_ref, u_ref, a_ref):
        n = n_ref[...]
        g = jnp.dot(n, wg_ref[0], preferred_element_type=F32)
        u = jnp.dot(n, wu_ref[0], preferred_element_type=F32)
        g_ref[0] = g
        u_ref[0] = u
        a_ref[0] = (g * _sigmoid(g) * u).astype(BF16)

    wspec = pl.BlockSpec((1, D, FB), lambda j, i: (j, 0, 0))
    ospec = pl.BlockSpec((1, tm, FB), lambda j, i: (j, i, 0))
    return pl.pallas_call(
        body, grid=(nblk, S // tm), name="ffn_up",
        in_specs=[pl.BlockSpec((tm, D), lambda j, i: (i, 0)), wspec, wspec],
        out_specs=[ospec, ospec, ospec],
        out_shape=[SDS((nblk, S, FB), F32), SDS((nblk, S, FB), F32), SDS((nblk, S, FB), BF16)],
        compiler_params=_cp(2))(n2, wg, wu)


def _ffn_down_loss(act, wd, h1, target, w_norm):
    nblk, S, FB = act.shape
    D = h1.shape[1]
    tm = min(512, S)

    def body(a_ref, wd_ref, h_ref, t_ref, wn_ref, dh_ref, dhb_ref, loss_ref, dw_ref, acc):
        i, j = pl.program_id(0), pl.program_id(1)

        @pl.when(j == 0)
        def _():
            acc[...] = h_ref[...]

        @pl.when((i == 0) & (j == 0))
        def _():
            dw_ref[...] = jnp.zeros_like(dw_ref)

        acc[...] += jnp.dot(a_ref[0], wd_ref[...], preferred_element_type=F32)

        @pl.when(j == nblk - 1)
        def _():
            h = acc[...]
            w = wn_ref[...]
            err = h * _rms_scale(h) * w - t_ref[...]
            loss_ref[...] = jnp.full(loss_ref.shape, 0.5 * jnp.sum(err * err) / D, F32)
            dh, dw = _rms_bwd(err * (1.0 / D), h, w)
            dh_ref[...] = dh
            dhb_ref[...] = dh.astype(BF16)
            dw_ref[...] += dw

    row = pl.BlockSpec((tm, D), lambda i, j: (i, 0))
    vec = pl.BlockSpec((1, D), lambda i, j: (0, 0))
    return pl.pallas_call(
        body, grid=(S // tm, nblk), name="ffn_down_loss",
        in_specs=[pl.BlockSpec((1, tm, FB), lambda i, j: (j, i, 0)), pl.BlockSpec((FB, D), lambda i, j: (j, 0)),
                  row, row, vec],
        out_specs=[row, row, pl.BlockSpec((8, 128), lambda i, j: (i, 0)), vec],
        out_shape=[SDS((S, D), F32), SDS((S, D), BF16), SDS((S // tm * 8, 128), F32), SDS((1, D), F32)],
        scratch_shapes=[pltpu.VMEM((tm, D), F32)], compiler_params=_cp(2))(act, wd, h1, target, w_norm)


def _attn_bias():
    n_heads = 8
    slopes = np.exp2(-8.0 * np.arange(1, n_heads + 1, dtype=np.float32) / n_heads)
    dist = np.abs(np.arange(K_TILE)[None, :] - BAND - np.arange(Q_TILE)[:, None])
    out = np.empty((n_heads, len(DILATIONS), Q_TILE, K_TILE), np.float32)
    for h in range(n_heads):
        for p, d in enumerate(DILATIONS):
            out[h, p] = np.where(dist <= BAND, -slopes[h] * (d * dist).astype(np.float32), NEG)
    return jnp.asarray(out)


def _attn_tiles(S, d):
    L = S // d
    per_class = L // Q_TILE
    return L, per_class, d * per_class


def _tile_rows(t, d, per_class):
    r = t // per_class
    a = (t % per_class) * Q_TILE
    q_rows = pl.ds(r + d * a, Q_TILE, stride=d) if d > 1 else pl.ds(pl.multiple_of(a, Q_TILE), Q_TILE)
    k_rows = pl.ds(KV_PAD + r + d * (a - BAND), K_TILE, stride=d) if d > 1 else pl.ds(
        pl.multiple_of(KV_PAD + a - BAND, BAND), K_TILE)
    return a, q_rows, k_rows


def _edge_mask(a, L):
    lk = lax.broadcasted_iota(jnp.int32, (1, K_TILE), 1) + (a - BAND)
    return jnp.where((lk >= 0) & (lk < L), 0.0, NEG).astype(F32)


def _fill_padded(dst, src, S):
    dst[pl.ds(0, KV_PAD), :] = jnp.zeros((KV_PAD, HEAD_DIM), F32)
    dst[pl.ds(KV_PAD + S, KV_PAD), :] = jnp.zeros((KV_PAD, HEAD_DIM), F32)
    dst[pl.ds(KV_PAD, S), :] = src[...]


def _head_specs(S, groups, n_heads):
    return [pl.BlockSpec((S, HEAD_DIM), functools.partial(lambda h, g: (0, g * n_heads + h), g=g)) for g in groups]


def _attn_fwd(proj, bias):
    S = proj.shape[0]
    H = proj.shape[1] // (N_GROUPS * HEAD_DIM)
    scale = HEAD_DIM ** -0.5

    def body(q_ref, k_ref, v_ref, b_ref, o_ref, lse_ref, kp, vp, m_run, l_run):
        _fill_padded(kp, k_ref, S)
        _fill_padded(vp, v_ref, S)
        o_ref[...] = jnp.zeros_like(o_ref)
        m_run[...] = jnp.full(m_run.shape, NEG, F32)
        l_run[...] = jnp.zeros_like(l_run)
        for p, d in enumerate(DILATIONS):
            L, per_class, n_tiles = _attn_tiles(S, d)

            def tiles(t, carry, p=p, d=d, L=L, per_class=per_class, n_tiles=n_tiles):
                rows = [_tile_rows(t + u * (n_tiles // TILE_GROUP), d, per_class) for u in range(TILE_GROUP)]
                got = [(q_ref[qr, :], kp[kr, :], vp[kr, :], m_run[qr, :][:, :1], l_run[qr, :][:, :1], o_ref[qr, :])
                       for _, qr, kr in rows]
                new = []
                for (a, _, _), (qt, kt, vt, m_old, l_old, o_old) in zip(rows, got):
                    s = _dot(qt, kt, _NT) * scale + b_ref[0, p] + _edge_mask(a, L)
                    m_new = jnp.maximum(m_old, jnp.max(s, axis=-1, keepdims=True))
                    pr = jnp.exp(s - m_new)
                    alpha = jnp.exp(m_old - m_new)
                    new.append((m_new, alpha * l_old + jnp.sum(pr, axis=-1, keepdims=True),
                                alpha * o_old + _dot(pr, vt, _NN)))
                for (_, qr, _), (m_new, l_new, o_new) in zip(rows, new):
                    o_ref[qr, :] = o_new
                    m_run[qr, :] = jnp.broadcast_to(m_new, (Q_TILE, HEAD_DIM))
                    l_run[qr, :] = jnp.broadcast_to(l_new, (Q_TILE, HEAD_DIM))
                return carry

            lax.fori_loop(0, n_tiles // TILE_GROUP, tiles, 0)
        l = l_run[...]
        o_ref[...] = o_ref[...] / l
        lse_ref[...] = m_run[...] + jnp.log(l)

    hspec = pl.BlockSpec((S, HEAD_DIM), lambda h: (0, h))
    return pl.pallas_call(
        body, grid=(H,), name="attn_fwd",
        in_specs=_head_specs(S, (0, 1, 2), H) + [
            pl.BlockSpec((1, len(DILATIONS), Q_TILE, K_TILE), lambda h: (h, 0, 0, 0))],
        out_specs=[hspec, hspec],
        out_shape=[SDS((S, H * HEAD_DIM), F32), SDS((S, H * HEAD_DIM), F32)],
        scratch_shapes=[pltpu.VMEM((S + 2 * KV_PAD, HEAD_DIM), F32), pltpu.VMEM((S + 2 * KV_PAD, HEAD_DIM), F32),
                        pltpu.VMEM((S, HEAD_DIM), F32), pltpu.VMEM((S, HEAD_DIM), F32)],
        compiler_params=_cp(1))(proj, proj, proj, bias)


def _attn_bwd(proj, out, lse, dmix, bias):
    S = proj.shape[0]
    H = proj.shape[1] // (N_GROUPS * HEAD_DIM)
    scale = HEAD_DIM ** -0.5

    def body(q_ref, k_ref, v_ref, o_ref, lse_ref, do_ref, b_ref, dq_ref, dk_ref, dv_ref, kp, vp, dkp, dvp, dsum):
        _fill_padded(kp, k_ref, S)
        _fill_padded(vp, v_ref, S)
        dkp[...] = jnp.zeros_like(dkp)
        dvp[...] = jnp.zeros_like(dvp)
        dq_ref[...] = jnp.zeros_like(dq_ref)
        dsum[...] = jnp.broadcast_to(jnp.sum(do_ref[...] * o_ref[...], axis=-1, keepdims=True), dsum.shape)
        for p, d in enumerate(DILATIONS):
            L, per_class, n_tiles = _attn_tiles(S, d)

            def tiles(t, carry, p=p, d=d, L=L, per_class=per_class, n_tiles=n_tiles):
                rows = [_tile_rows(t + u * (n_tiles // TILE_GROUP), d, per_class) for u in range(TILE_GROUP)]
                got = [(q_ref[qr, :], kp[kr, :], vp[kr, :], do_ref[qr, :], lse_ref[qr, :][:, :1], dsum[qr, :][:, :1],
                        dq_ref[qr, :], dkp[kr, :], dvp[kr, :]) for _, qr, kr in rows]
                new = []
                for (a, _, _), (qt, kt, vt, dot_, lse_t, dsum_t, dq_old, dk_old, dv_old) in zip(rows, got):
                    s = _dot(qt, kt, _NT) * scale + b_ref[0, p] + _edge_mask(a, L)
                    pr = jnp.exp(s - lse_t)
                    ds = pr * (_dot(dot_, vt, _NT) - dsum_t) * scale
                    new.append((dq_old + _dot(ds, kt, _NN), dk_old + _dot(ds, qt, _TN), dv_old + _dot(pr, dot_, _TN)))
                for (_, qr, kr), (dq_new, dk_new, dv_new) in zip(rows, new):
                    dq_ref[qr, :] = dq_new
                    dkp[kr, :] = dk_new
                    dvp[kr, :] = dv_new
                return carry

            lax.fori_loop(0, n_tiles // TILE_GROUP, tiles, 0)
        dk_ref[...] = dkp[pl.ds(KV_PAD, S), :]
        dv_ref[...] = dvp[pl.ds(KV_PAD, S), :]

    hspec = pl.BlockSpec((S, HEAD_DIM), lambda h: (0, h))
    padded = pltpu.VMEM((S + 2 * KV_PAD, HEAD_DIM), F32)
    return pl.pallas_call(
        body, grid=(H,), name="attn_bwd",
        in_specs=_head_specs(S, (0, 1, 2), H) + [hspec, hspec, hspec,
                                                  pl.BlockSpec((1, len(DILATIONS), Q_TILE, K_TILE), lambda h: (h, 0, 0, 0))],
        out_specs=[hspec, hspec, hspec],
        out_shape=[SDS((S, H * HEAD_DIM), F32)] * 3,
        scratch_shapes=[padded, padded, padded, padded, pltpu.VMEM((S, HEAD_DIM), F32)],
        compiler_params=_cp(1))(proj, proj, proj, out, lse, dmix, bias)


def _ret_consts(lg, forward):
    C = RET_CHUNK
    i = lax.broadcasted_iota(jnp.int32, (C, C), 0)
    j = lax.broadcasted_iota(jnp.int32, (C, C), 1)
    rel = (i - j) if forward else (j - i)
    inside = (rel >= 0) if forward else (rel > 0)
    relf = jnp.maximum(rel, 0).astype(F32)
    mask = jnp.where(inside, jnp.exp(lg * relf), 0.0)
    idx = lax.broadcasted_iota(jnp.int32, (C, 1), 0).astype(F32)
    q_exp = (idx + 1.0) if forward else (C - idx)
    k_exp = (C - 1.0 - idx) if forward else idx
    return mask, relf, jnp.exp(lg * q_exp), q_exp, jnp.exp(lg * k_exp), k_exp, jnp.exp(lg * C)


def _log_decay(dec_ref, h):
    return -jnp.exp(jnp.full((1, 1), dec_ref[0, h], F32))


def _chunk(c):
    return pl.ds(pl.multiple_of(c * RET_CHUNK, RET_CHUNK), RET_CHUNK)


def _ret_fwd(proj, dec_f, dec_b, w_norm):
    S = proj.shape[0]
    H = proj.shape[1] // (N_GROUPS * HEAD_DIM)
    nc = S // RET_CHUNK
    scale = HEAD_DIM ** -0.5

    def body(df_ref, db_ref, q_ref, k_ref, v_ref, g_ref, w_ref, y_ref, o_ref):
        h = pl.program_id(0)
        consts = [_ret_consts(_log_decay(dref, h), fw) for fw, dref in ((True, df_ref), (False, db_ref))]
        o_ref[...] = jnp.zeros_like(o_ref)

        def step(n, states):
            rows = [_chunk(n), _chunk(nc - 1 - n)]
            got = [(q_ref[r, :] * scale, k_ref[r, :], v_ref[r, :], o_ref[r, :]) for r in rows]
            new_o, new_states = [], []
            for (qc, kc, vc, o_old), (mask, _, q_dec, _, k_dec, _, c_dec), state in zip(got, consts, states):
                new_o.append(o_old + _dot(_dot(qc, kc, _NT) * mask, vc, _NN) + _dot(qc * q_dec, state, _NN))
                new_states.append(state * c_dec + _dot(kc * k_dec, vc, _TN))
            for r, o_new in zip(rows, new_o):
                o_ref[r, :] = o_new
            return tuple(new_states)

        lax.fori_loop(0, nc, step, (jnp.zeros((HEAD_DIM, HEAD_DIM), F32),) * 2)
        o = o_ref[...]
        g = g_ref[...]
        y_ref[...] = o * _rms_scale(o) * w_ref[...] * (g * _sigmoid(g))

    hspec = pl.BlockSpec((S, HEAD_DIM), lambda h: (0, h))
    smem = pl.BlockSpec(memory_space=pltpu.SMEM)
    return pl.pallas_call(
        body, grid=(H,), name="ret_fwd",
        in_specs=[smem, smem] + _head_specs(S, (3, 4, 5, 6), H) + [pl.BlockSpec((1, HEAD_DIM), lambda h: (0, h))],
        out_specs=[hspec, hspec],
        out_shape=[SDS((S, H * HEAD_DIM), F32)] * 2,
        compiler_params=_cp(1))(dec_f, dec_b, proj, proj, proj, proj, w_norm)


def _ret_bwd(proj, o_raw, dmix, dec_f, dec_b, w_norm, col0):
    S = proj.shape[0]
    H = proj.shape[1] // (N_GROUPS * HEAD_DIM)
    C = RET_CHUNK
    nc = S // C
    scale = HEAD_DIM ** -0.5

    def body(df_ref, db_ref, q_ref, k_ref, v_ref, g_ref, o_ref, dy_ref, w_ref,
             dq_ref, dk_ref, dv_ref, dg_ref, small_ref, do, states):
        h = pl.program_id(0)
        o = o_ref[...]
        g = g_ref[...]
        dy = dy_ref[...]
        w = w_ref[...]
        rr = _rms_scale(o)
        normed = o * rr
        sg = _sigmoid(g)
        silu = g * sg
        small_ref[0, pl.ds(2, 1), :] = jnp.sum(dy * normed * silu, axis=0, keepdims=True)
        dg_ref[...] = dy * normed * w * (sg * (1.0 + g * (1.0 - sg)))
        dnormed = dy * w * silu
        do[...] = rr * dnormed - o * (rr * rr * rr) * jnp.mean(dnormed * o, axis=-1, keepdims=True)

        lgs = [_log_decay(df_ref, h), _log_decay(db_ref, h)]
        consts = [_ret_consts(lg, fw) for lg, fw in zip(lgs, (True, False))]
        zero_state = jnp.zeros((HEAD_DIM, HEAD_DIM), F32)

        def fwd_step(n, carry):
            new = []
            for way, (cidx, state) in enumerate(zip((n, nc - 1 - n), carry)):
                k_dec, c_dec = consts[way][4], consts[way][6]
                rows = _chunk(cidx)
                states[way, cidx] = state
                new.append(state * c_dec + _dot(k_ref[rows, :] * k_dec, v_ref[rows, :], _TN))
            return tuple(new)

        lax.fori_loop(0, nc, fwd_step, (zero_state, zero_state))
        for ref in (dq_ref, dk_ref, dv_ref):
            ref[...] = jnp.zeros_like(ref)

        def bwd_step(n, carry):
            cidxs = (nc - 1 - n, n)
            rows = [_chunk(c) for c in cidxs]
            got = [(q_ref[r, :] * scale, k_ref[r, :], v_ref[r, :], do[r, :], states[way, c], dq_ref[r, :], dk_ref[r, :],
                    dv_ref[r, :]) for way, (r, c) in enumerate(zip(rows, cidxs))]
            new_rows, new_carry = [], []
            for (qc, kc, vc, doc, state, dq_old, dk_old, dv_old), cs, (d_state, dlam) in zip(got, consts, carry):
                mask, relf, q_dec, q_exp, k_dec, k_exp, c_dec = cs
                a0 = _dot(qc, kc, _NT)
                dp = _dot(doc, vc, _NT) * mask
                gq = _dot(doc, state, _NT)
                gk = _dot(vc, d_state, _NT)
                new_rows.append((dq_old + (_dot(dp, kc, _NN) + q_dec * gq) * scale,
                                 dk_old + _dot(dp, qc, _TN) + k_dec * gk,
                                 dv_old + _dot(a0 * mask, doc, _TN) + _dot(kc * k_dec, d_state, _NN)))
                dlam = dlam + jnp.sum(relf * a0 * dp, axis=0, keepdims=True) \
                    + jnp.sum(q_exp * q_dec * qc * gq + k_exp * k_dec * kc * gk, axis=0, keepdims=True) \
                    + (C * c_dec) * jnp.sum(state * d_state, axis=0, keepdims=True)
                new_carry.append((d_state * c_dec + _dot(qc * q_dec, doc, _TN), dlam))
            for r, (dq_new, dk_new, dv_new) in zip(rows, new_rows):
                dq_ref[r, :] = dq_new
                dk_ref[r, :] = dk_new
                dv_ref[r, :] = dv_new
            return tuple(new_carry)

        zero_carry = (zero_state, jnp.zeros((1, HEAD_DIM), F32))
        done = lax.fori_loop(0, nc, bwd_step, (zero_carry, zero_carry))
        for row, ((_, dlam), lg) in enumerate(zip(done, lgs)):
            small_ref[0, pl.ds(row, 1), :] = jnp.broadcast_to(jnp.sum(dlam, axis=-1, keepdims=True) * lg, (1, HEAD_DIM))
        small_ref[0, pl.ds(3, 5), :] = jnp.zeros((5, HEAD_DIM), F32)

    hspec = pl.BlockSpec((S, HEAD_DIM), lambda h: (0, h))
    smem = pl.BlockSpec(memory_space=pltpu.SMEM)
    nh0 = col0 // HEAD_DIM
    return pl.pallas_call(
        body, grid=(H,), name="ret_bwd",
        in_specs=[smem, smem] + _head_specs(S, (3, 4, 5, 6), H) + [
            hspec, pl.BlockSpec((S, HEAD_DIM), lambda h: (0, nh0 + h)), pl.BlockSpec((1, HEAD_DIM), lambda h: (0, h))],
        out_specs=[hspec, hspec, hspec, hspec, pl.BlockSpec((1, 8, HEAD_DIM), lambda h: (h, 0, 0))],
        out_shape=[SDS((S, H * HEAD_DIM), F32)] * 4 + [SDS((H, 8, HEAD_DIM), F32)],
        scratch_shapes=[pltpu.VMEM((S, HEAD_DIM), F32), pltpu.VMEM((2, nc, HEAD_DIM, HEAD_DIM), F32)],
        compiler_params=_cp(1))(dec_f, dec_b, proj, proj, proj, proj, o_raw, dmix, w_norm)


def _ffn_bwd_act(dh2, wd, g, u):
    S, D = dh2.shape
    nblk, _, FB = g.shape
    tm = min(512, S)

    def body(dh_ref, wd_ref, g_ref, u_ref, dg_ref, du_ref):
        dact = _dot(dh_ref[...], wd_ref[...], _NT)
        gg = g_ref[0]
        sg = _sigmoid(gg)
        dg_ref[0] = (dact * u_ref[0] * (sg * (1.0 + gg * (1.0 - sg)))).astype(BF16)
        du_ref[0] = (dact * (gg * sg)).astype(BF16)

    blk = pl.BlockSpec((1, tm, FB), lambda j, i: (j, i, 0))
    return pl.pallas_call(
        body, grid=(nblk, S // tm), name="ffn_bwd_act",
        in_specs=[pl.BlockSpec((tm, D), lambda j, i: (i, 0)), pl.BlockSpec((FB, D), lambda j, i: (j, 0)), blk, blk],
        out_specs=[blk, blk], out_shape=[SDS((nblk, S, FB), BF16)] * 2,
        compiler_params=_cp(2))(dh2, wd, g, u)


def _ffn_bwd_in(dg, du, wg, wu, h1, dh2, w_norm):
    nblk, S, FB = dg.shape
    D = h1.shape[1]
    tm = min(512, S)

    def body(dg_ref, du_ref, wg_ref, wu_ref, h_hbm, dh2_hbm, wn_ref, dh_ref, dhb_ref, dw_ref, acc, h_buf, dh2_buf, sems):
        i, j = pl.program_id(0), pl.program_id(1)
        rows = pl.ds(pl.multiple_of(i * tm, tm), tm)
        fetch = [pltpu.make_async_copy(h_hbm.at[rows, :], h_buf, sems.at[0]),
                 pltpu.make_async_copy(dh2_hbm.at[rows, :], dh2_buf, sems.at[1])]

        @pl.when(j == 0)
        def _():
            for cp in fetch:
                cp.start()
            acc[...] = jnp.zeros_like(acc)

        @pl.when((i == 0) & (j == 0))
        def _():
            dw_ref[...] = jnp.zeros_like(dw_ref)

        acc[...] += _dot(dg_ref[0], wg_ref[0], _NT) + _dot(du_ref[0], wu_ref[0], _NT)

        @pl.when(j == nblk - 1)
        def _():
            for cp in fetch:
                cp.wait()
            dh, dw = _rms_bwd(acc[...], h_buf[...], wn_ref[...])
            dh = dh2_buf[...] + dh
            dh_ref[...] = dh
            dhb_ref[...] = dh.astype(BF16)
            dw_ref[...] += dw

    blk = pl.BlockSpec((1, tm, FB), lambda i, j: (j, i, 0))
    wspec = pl.BlockSpec((1, D, FB), lambda i, j: (j, 0, 0))
    row = pl.BlockSpec((tm, D), lambda i, j: (i, 0))
    vec = pl.BlockSpec((1, D), lambda i, j: (0, 0))
    return pl.pallas_call(
        body, grid=(S // tm, nblk), name="ffn_bwd_in",
        in_specs=[blk, blk, wspec, wspec, ANY, ANY, vec],
        out_specs=[row, row, vec], out_shape=[SDS((S, D), F32), SDS((S, D), BF16), SDS((1, D), F32)],
        scratch_shapes=[pltpu.VMEM((tm, D), F32), pltpu.VMEM((tm, D), F32), pltpu.VMEM((tm, D), F32),
                        pltpu.SemaphoreType.DMA((2,))],
        compiler_params=_cp(2))(dg, du, wg, wu, h1, dh2, w_norm)


def _dmix(dh1, w_out):
    S, D = dh1.shape
    tm = min(512, S)

    def body(dh_ref, w_ref, o_ref):
        o_ref[...] = _dot(dh_ref[...], w_ref[...], _NT)

    row = pl.BlockSpec((tm, D), lambda i: (i, 0))
    return pl.pallas_call(
        body, grid=(S // tm,), name="dmix", in_specs=[row, pl.BlockSpec((D, D), lambda i: (0, 0))],
        out_specs=row, out_shape=SDS((S, D), F32), compiler_params=_cp(1))(dh1, w_out)


def _in_bwd(dproj, w_blk, x, dh1, w_norm):
    S, D = x.shape
    nblk, _, NB = w_blk.shape
    tm = min(512, S)

    def body(dp_ref, w_ref, x_ref, dh1_ref, wn_ref, dx_ref, dw_ref, acc):
        i, j = pl.program_id(0), pl.program_id(1)

        @pl.when(j == 0)
        def _():
            acc[...] = jnp.zeros_like(acc)

        @pl.when((i == 0) & (j == 0))
        def _():
            dw_ref[...] = jnp.zeros_like(dw_ref)

        acc[...] += _dot(dp_ref[...], w_ref[0], _NT)

        @pl.when(j == nblk - 1)
        def _():
            dh, dw = _rms_bwd(acc[...], x_ref[...], wn_ref[...])
            dx_ref[...] = dh1_ref[...] + dh
            dw_ref[...] += dw

    row = pl.BlockSpec((tm, D), lambda i, j: (i, 0))
    vec = pl.BlockSpec((1, D), lambda i, j: (0, 0))
    return pl.pallas_call(
        body, grid=(S // tm, nblk), name="in_bwd",
        in_specs=[pl.BlockSpec((tm, NB), lambda i, j: (i, j)), pl.BlockSpec((1, D, NB), lambda i, j: (j, 0, 0)),
                  row, row, vec],
        out_specs=[row, vec], out_shape=[SDS((S, D), F32), SDS((1, D), F32)],
        scratch_shapes=[pltpu.VMEM((tm, D), F32)], compiler_params=_cp(2))(dproj, w_blk, x, dh1, w_norm)


def _wgrad(a, b, a_spec, b_spec, o_spec, o_shape, grid, name):
    nk = grid[-1]

    def ld(ref):
        return ref[0] if len(ref.shape) == 3 else ref[...]

    def body(a_ref, b_ref, o_ref, acc):
        k = pl.program_id(len(grid) - 1)

        @pl.when(k == 0)
        def _():
            acc[...] = jnp.zeros_like(acc)

        acc[...] += _dot(ld(a_ref), ld(b_ref), _TN)

        @pl.when(k == nk - 1)
        def _():
            if len(o_ref.shape) == 3:
                o_ref[0] = acc[...].astype(o_ref.dtype)
            else:
                o_ref[...] = acc[...].astype(o_ref.dtype)

    return pl.pallas_call(
        body, grid=grid, name=name, in_specs=[a_spec, b_spec], out_specs=o_spec, out_shape=SDS(o_shape, BF16),
        scratch_shapes=[pltpu.VMEM(o_spec.block_shape[-2:], F32)], compiler_params=_cp(len(grid)))(a, b)


def _peer(k):
    x, y, c = lax.axis_index("x"), lax.axis_index("y"), lax.axis_index("c")
    px = 1 - x if k & 4 else x
    py = 1 - y if k & 2 else y
    pc = 1 - c if k & 1 else c
    return (px, py, pc), 4 * px + 2 * py + pc


def _exchange_copies(srcs, lands, send_sems, recv_sems, which, gather):
    _, me = _peer(0)
    pairs = []
    for pos, a in enumerate(which):
        for k in range(1, N_DEV):
            dev, idx = _peer(k)
            sem = pos * (N_DEV - 1) + k - 1
            src = srcs[a] if gather else srcs[a].at[idx]
            mk = functools.partial(pltpu.make_async_remote_copy, src_ref=src, send_sem=send_sems.at[sem],
                                   recv_sem=recv_sems.at[sem], device_id=dev, device_id_type=MESH)
            pairs.append((mk(dst_ref=lands[a].at[me]), mk(dst_ref=lands[a].at[idx])))
    return pairs


def _sequencer_kernel(name, collective_id, n_remote, n_local):
    return pl.kernel(mesh=plsc.ScalarSubcoreMesh(axis_name="sequencer", num_cores=1), name=name,
                     scratch_types=(pltpu.SemaphoreType.DMA((n_remote,)), pltpu.SemaphoreType.DMA((n_remote,)),
                                    pltpu.SemaphoreType.DMA((n_local,))),
                     compiler_params=pltpu.CompilerParams(collective_id=collective_id))


def _handshake(ks):
    barrier = pltpu.get_barrier_semaphore()
    for k in ks:
        pl.semaphore_signal(barrier, inc=1, device_id=_peer(k)[0], device_id_type=MESH)
    pl.semaphore_wait(barrier, len(ks))


def _sequencer_scatter(arrays, name, collective_id):
    n = len(arrays)
    hbm = pltpu.MemorySpace.HBM
    srcs = [jax.new_ref(a, memory_space=hbm) for a in arrays]
    lands = [jax.empty_ref(SDS(a.shape, a.dtype), memory_space=hbm) for a in arrays]

    @_sequencer_kernel(name, collective_id, n * (N_DEV - 1), n)
    def launch(send_sems, recv_sems, local_sems):
        _handshake(range(1, N_DEV))
        _, me = _peer(0)
        local = [pltpu.make_async_copy(srcs[a].at[me], lands[a].at[me], local_sems.at[a]) for a in range(n)]
        pairs = _exchange_copies(srcs, lands, send_sems, recv_sems, range(n), False)
        for out, _ in pairs:
            out.start()
        for cp in local:
            cp.start()
        for out, arrival in pairs:
            out.wait_send()
            arrival.wait_recv()
        for cp in local:
            cp.wait()

    launch()
    return [r[...] for r in lands]


SIBLING = 1
OTHER_CHIPS = (2, 4, 6)


def _sequencer_gather(arrays, name, collective_id):
    n = len(arrays)
    hbm = pltpu.MemorySpace.HBM
    srcs = [jax.new_ref(a, memory_space=hbm) for a in arrays]
    lands = [jax.empty_ref(SDS((N_DEV,) + a.shape, a.dtype), memory_space=hbm) for a in arrays]

    @_sequencer_kernel(name, collective_id, n * (N_DEV - 1), n)
    def launch(send_sems, recv_sems, local_sems):
        _handshake((SIBLING,) + OTHER_CHIPS)
        _, me = _peer(0)
        sibling, _ = _peer(SIBLING)

        def copy(a, k, src, block, to):
            sem = a * (N_DEV - 1) + k - 1
            return pltpu.make_async_remote_copy(src_ref=src, dst_ref=lands[a].at[block], send_sem=send_sems.at[sem],
                                                recv_sem=recv_sems.at[sem], device_id=to, device_id_type=MESH)

        local = [pltpu.make_async_copy(srcs[a], lands[a].at[me], local_sems.at[a]) for a in range(n)]
        first = [copy(a, k, srcs[a], me, _peer(k)[0]) for a in range(n) for k in OTHER_CHIPS + (SIBLING,)]
        for cp in first + local:
            cp.start()
        passed = []
        for a in range(n):
            for k in OTHER_CHIPS:
                _, block = _peer(k)
                copy(a, k, srcs[a], block, sibling).wait_recv()
                passed.append(copy(a, k ^ SIBLING, lands[a].at[block], block, sibling))
                passed[-1].start()
        for a in range(n):
            for k in (SIBLING,) + tuple(k ^ SIBLING for k in OTHER_CHIPS):
                copy(a, k, srcs[a], _peer(k)[1], sibling).wait_recv()
        for cp in first + passed:
            cp.wait_send()
        for cp in local:
            cp.wait()

    launch()
    return [r[...] for r in lands]


SMALL_ROWS = 64


def _small_step(part, w, m, v):
    def body(p_ref, w_ref, m_ref, v_ref, g_ref, d_ref, nm_ref, nv_ref, gath, send_sems, recv_sems):
        _, me = _peer(0)
        gath[me] = p_ref[...]
        copies = []
        for k in range(1, N_DEV):
            dev, idx = _peer(k)
            out = pltpu.make_async_remote_copy(src_ref=p_ref, dst_ref=gath.at[me], send_sem=send_sems.at[k - 1],
                                               recv_sem=recv_sems.at[k - 1], device_id=dev, device_id_type=MESH)
            out.start()
            arrival = pltpu.make_async_remote_copy(src_ref=p_ref, dst_ref=gath.at[idx], send_sem=send_sems.at[k - 1],
                                                   recv_sem=recv_sems.at[k - 1], device_id=dev, device_id_type=MESH)
            copies.append((out, arrival))
        for out, arrival in copies:
            out.wait_send()
            arrival.wait_recv()
        g = gath[0]
        for p in range(1, N_DEV):
            g = g + gath[p]
        g_ref[...] = g
        d_ref[...], nm_ref[...], nv_ref[...] = _adamw(w_ref[...], g, m_ref[...], v_ref[...])

    vm = pl.BlockSpec(memory_space=pltpu.VMEM)
    return pl.pallas_call(
        body, name="small_step", in_specs=[vm] * 4, out_specs=[vm] * 4,
        out_shape=[SDS((SMALL_ROWS, 128), F32)] * 4,
        scratch_shapes=[pltpu.VMEM((N_DEV, SMALL_ROWS, 128), F32), pltpu.SemaphoreType.DMA((N_DEV - 1,)),
                        pltpu.SemaphoreType.DMA((N_DEV - 1,))])(part, w, m, v)


def _adamw(w, g, m, v):
    m = ADAM_B1 * m + (1.0 - ADAM_B1) * g
    v = ADAM_B2 * v + (1.0 - ADAM_B2) * (g * g)
    m_hat = m / (1.0 - ADAM_B1 ** ADAM_STEP)
    v_hat = v / (1.0 - ADAM_B2 ** ADAM_STEP)
    delta = -ADAM_LR * (m_hat / (jnp.sqrt(v_hat) + ADAM_EPS) + ADAM_WD * w)
    return delta, m, v


def _adamw_block(parts, w, m, v, name):
    R, C = w.shape
    tr = next(t for t in (256, 128, 64, 32, 16, 8) if R % t == 0 and t * C <= 256 * 1024)

    def body(p_ref, w_ref, m_ref, v_ref, g_ref, d_ref, nm_ref, nv_ref):
        g = p_ref[0].astype(F32)
        for p in range(1, N_DEV):
            g = g + p_ref[p].astype(F32)
        g_ref[...] = g
        d_ref[...], nm_ref[...], nv_ref[...] = _adamw(w_ref[...], g, m_ref[...], v_ref[...])

    row = pl.BlockSpec((tr, C), lambda i: (i, 0))
    return pl.pallas_call(
        body, grid=(R // tr,), name=name, in_specs=[pl.BlockSpec((N_DEV, tr, C), lambda i: (0, i, 0)), row, row, row],
        out_specs=[row] * 4, out_shape=[SDS((R, C), F32)] * 4, compiler_params=_cp(1))(parts, w, m, v)


def _pack_small(mix, ffn, fin, retw, dec_f, dec_b, loss):
    flat = jnp.concatenate([mix.reshape(-1), ffn.reshape(-1), fin.reshape(-1), retw.reshape(-1), dec_f.reshape(-1),
                            dec_b.reshape(-1), loss.reshape(-1)])
    return jnp.pad(flat, (0, SMALL_ROWS * 128 - flat.shape[0])).reshape(SMALL_ROWS, 128)


def _unpack_small(packed, shapes):
    flat = packed.reshape(-1)
    out, at = [], 0
    for s in shapes:
        n = math.prod(s)
        out.append(flat[at:at + n].reshape(s))
        at += n
    return out


def kernel(x, norm_mix_w, w_in, ret_decay_fwd, ret_decay_bwd, ret_norm_w, w_out, norm_ffn_w, w_gate, w_up, w_down, norm_final_w, loss_target, m_norm_mix_w, m_w_in, m_ret_decay_fwd, m_ret_decay_bwd, m_ret_norm_w, m_w_out, m_norm_ffn_w, m_w_gate, m_w_up, m_w_down, m_norm_final_w, v_norm_mix_w, v_w_in, v_ret_decay_fwd, v_ret_decay_bwd, v_ret_norm_w, v_w_out, v_norm_ffn_w, v_w_gate, v_w_up, v_w_down, v_norm_final_w):
    x2 = x[0]
    tgt = loss_target[0]
    S, D = x2.shape
    H = ret_norm_w.shape[1] // HEAD_DIM
    DA = H * HEAD_DIM
    fin_w = norm_final_w.reshape(1, D)
    big = (w_in[0], w_out[0], w_gate[0], w_up[0], w_down[0])

    big_b = [w.astype(BF16) for w in big]
    wi, = _sequencer_gather(big_b[:1], "gather_in", 0)
    wo, wg, wu = _sequencer_gather(big_b[1:4], "gather_mid", 1)
    wd, = _sequencer_gather(big_b[4:], "gather_down", 5)
    NB = wi.shape[2]

    proj, n1 = _proj_fwd(x2, norm_mix_w, wi)
    bias = _attn_bias()[:H]
    attn, lse = _attn_fwd(proj, bias)
    ret, o_raw = _ret_fwd(proj, ret_decay_fwd, ret_decay_bwd, ret_norm_w)
    wo_full = wo.reshape(D, D)
    FB = wd.shape[1]
    wd_full = wd.reshape(N_DEV * FB, D)
    h1, mixed, n2 = _out_fwd(x2, attn, ret, wo_full, norm_ffn_w)
    gate, up, act = _ffn_up(n2, wg, wu)
    dh2, dh2_b, loss_parts, g_fin = _ffn_down_loss(act, wd_full, h1, tgt, fin_w)

    dgate, dup = _ffn_bwd_act(dh2_b, wd_full, gate, up)
    tk = min(512, S)
    nk = S // tk
    g_wd = _wgrad(act, dh2_b, pl.BlockSpec((1, tk, FB), lambda j, k: (j, k, 0)), pl.BlockSpec((tk, D), lambda j, k: (k, 0)),
                  pl.BlockSpec((1, FB, D), lambda j, k: (j, 0, 0)), (N_DEV, FB, D), (N_DEV, nk), "wgrad_down")
    tmw = min(1024, D)
    gu_specs = (pl.BlockSpec((tk, tmw), lambda j, m, k: (k, m)), pl.BlockSpec((1, tk, FB), lambda j, m, k: (j, k, 0)),
                pl.BlockSpec((1, tmw, FB), lambda j, m, k: (j, m, 0)), (N_DEV, D, FB), (N_DEV, D // tmw, nk))
    g_wg = _wgrad(n2, dgate, *gu_specs, "wgrad_gate")
    g_wu = _wgrad(n2, dup, *gu_specs, "wgrad_up")
    parts_f = _sequencer_scatter([g_wg, g_wu, g_wd], "scatter_ffn", 2)
    dh1, dh1_b, g_ffn = _ffn_bwd_in(dgate, dup, wg, wu, h1, dh2, norm_ffn_w)
    dmix = _dmix(dh1_b, wo_full)
    g_wo = _wgrad(mixed, dh1_b, pl.BlockSpec((tk, tmw), lambda m, k: (k, m)), pl.BlockSpec((tk, D), lambda m, k: (k, 0)),
                  pl.BlockSpec((tmw, D), lambda m, k: (m, 0)), (D, D), (D // tmw, nk), "wgrad_out")
    parts_o = _sequencer_scatter([g_wo.reshape(N_DEV, D // N_DEV, D)], "scatter_out", 3)
    dq_r, dk_r, dv_r, dg_r, small = _ret_bwd(proj, o_raw, dmix, ret_decay_fwd, ret_decay_bwd, ret_norm_w, DA)
    dq_a, dk_a, dv_a = _attn_bwd(proj, attn, lse, dmix, bias)
    dproj = jnp.concatenate([t.astype(BF16) for t in (dq_a, dk_a, dv_a, dq_r, dk_r, dv_r, dg_r)], axis=1)
    g_wi = _wgrad(n1, dproj, pl.BlockSpec((tk, tmw), lambda j, m, k: (k, m)), pl.BlockSpec((tk, NB), lambda j, m, k: (k, j)),
                  pl.BlockSpec((1, tmw, NB), lambda j, m, k: (j, m, 0)), (N_DEV, D, NB), (N_DEV, D // tmw, nk), "wgrad_in")
    parts_i = _sequencer_scatter([g_wi], "scatter_in", 4)
    grad_x, g_mix = _in_bwd(dproj, wi, x2, dh1, norm_mix_w)

    big_m = (m_w_in[0], m_w_out[0], m_w_gate[0], m_w_up[0], m_w_down[0])
    big_v = (v_w_in[0], v_w_out[0], v_w_gate[0], v_w_up[0], v_w_down[0])
    names = ("adamw_in", "adamw_out", "adamw_gate", "adamw_up", "adamw_down")
    upd = [None] * 5
    for a, p in zip((2, 3, 4, 1, 0), parts_f + parts_o + parts_i):
        upd[a] = _adamw_block(p, big[a], big_m[a], big_v[a], names[a])

    g_dec_f = small[:, 0, 0].reshape(1, H)
    g_dec_b = small[:, 1, 0].reshape(1, H)
    g_retw = small[:, 2, :].reshape(1, DA)
    loss_local = jnp.sum(loss_parts[::8, 0])
    zero = jnp.zeros((1,), F32)
    part = _pack_small(g_mix, g_ffn, g_fin, g_retw, g_dec_f, g_dec_b, loss_local)
    sw = _pack_small(norm_mix_w, norm_ffn_w, norm_final_w, ret_norm_w, ret_decay_fwd, ret_decay_bwd, zero)
    sm = _pack_small(m_norm_mix_w, m_norm_ffn_w, m_norm_final_w, m_ret_norm_w, m_ret_decay_fwd, m_ret_decay_bwd, zero)
    sv = _pack_small(v_norm_mix_w, v_norm_ffn_w, v_norm_final_w, v_ret_norm_w, v_ret_decay_fwd, v_ret_decay_bwd, zero)
    shapes = [(1, D), (1, D), (D,), (1, DA), (1, H), (1, H), ()]
    sg, sd, snm, snv = [_unpack_small(t, shapes) for t in _small_step(part, sw, sm, sv)]
    loss = sg[6]

    def ordered(small_set, k):
        b = [u[k][None] for u in upd]
        return [small_set[0], b[0], small_set[4], small_set[5], small_set[3], b[1], small_set[1], b[2], b[3], b[4],
                small_set[2]]

    return (loss, grad_x[None], *ordered(sg, 0), *ordered(sd, 1), *ordered(snm, 2), *ordered(snv, 3))
```

```python
import functools
import math

import numpy as np
import jax
import jax.numpy as jnp
from jax import lax
from jax.experimental import pallas as pl
from jax.experimental.pallas import tpu as pltpu
from jax.experimental.pallas import tpu_sc as plsc

F32 = jnp.float32
BF16 = jnp.bfloat16
SDS = jax.ShapeDtypeStruct

HEAD_DIM = 128
EPS = 1e-6
RET_CHUNK = 128
DILATIONS = (1, 4, 16)
BAND = 64
Q_TILE = 128
K_TILE = Q_TILE + 2 * BAND
KV_PAD = BAND * max(DILATIONS)
TILE_GROUP = 2
NEG = -1e30
N_DEV = 8
N_GROUPS = 7
ADAM_LR, ADAM_B1, ADAM_B2, ADAM_EPS, ADAM_WD, ADAM_STEP = 0.001, 0.9, 0.999, 1e-08, 0.01, 10
VMEM_LIMIT = 56 * 1024 * 1024
MESH = pl.DeviceIdType.MESH
ANY = pl.BlockSpec(memory_space=pl.ANY)


def _cp(n_grid):
    return pltpu.CompilerParams(dimension_semantics=("arbitrary",) * n_grid, vmem_limit_bytes=VMEM_LIMIT)


def _sigmoid(x):
    return 1.0 / (1.0 + jnp.exp(-x))


def _rms_scale(h):
    return lax.rsqrt(jnp.mean(h * h, axis=-1, keepdims=True) + EPS)


def _rms_bwd(dn, h, w):
    r = _rms_scale(h)
    gw = dn * w
    dh = r * gw - h * (r * r * r) * jnp.mean(gw * h, axis=-1, keepdims=True)
    return dh, jnp.sum(dn * h * r, axis=0, keepdims=True)


def _dot(a, b, dims):
    return lax.dot_general(a.astype(BF16), b.astype(BF16), (dims, ((), ())), preferred_element_type=F32)


_NN = ((1,), (0,))
_NT = ((1,), (1,))
_TN = ((0,), (0,))


def _proj_fwd(x, w_norm, w_blk):
    S, D = x.shape
    nblk, _, NB = w_blk.shape
    tm = min(1024, S)

    def body(x_ref, wn_ref, w_ref, proj_ref, n_ref, n_scr):
        @pl.when(pl.program_id(1) == 0)
        def _():
            xf = x_ref[...]
            nb = (xf * _rms_scale(xf) * wn_ref[...]).astype(BF16)
            n_scr[...] = nb
            n_ref[...] = nb
        proj_ref[...] = jnp.dot(n_scr[...], w_ref[0], preferred_element_type=F32)

    return pl.pallas_call(
        body, grid=(S // tm, nblk), name="proj_fwd",
        in_specs=[pl.BlockSpec((tm, D), lambda i, j: (i, 0)), pl.BlockSpec((1, D), lambda i, j: (0, 0)),
                  pl.BlockSpec((1, D, NB), lambda i, j: (j, 0, 0))],
        out_specs=[pl.BlockSpec((tm, NB), lambda i, j: (i, j)), pl.BlockSpec((tm, D), lambda i, j: (i, 0))],
        out_shape=[SDS((S, nblk * NB), F32), SDS((S, D), BF16)],
        scratch_shapes=[pltpu.VMEM((tm, D), BF16)], compiler_params=_cp(2))(x, w_norm, w_blk)


def _out_fwd(x, attn, ret, w_out, w_norm):
    S, D = x.shape
    DA = attn.shape[1]
    tm = min(256, S)

    def body(x_ref, a_ref, r_ref, w_ref, wn_ref, h_ref, mix_ref, n_ref):
        a = a_ref[...].astype(BF16)
        r = r_ref[...].astype(BF16)
        mix_ref[:, :DA] = a
        mix_ref[:, DA:] = r
        h = x_ref[...] + jnp.dot(a, w_ref[:DA, :], preferred_element_type=F32) \
            + jnp.dot(r, w_ref[DA:, :], preferred_element_type=F32)
        h_ref[...] = h
        n_ref[...] = (h * _rms_scale(h) * wn_ref[...]).astype(BF16)

    row = lambda w: pl.BlockSpec((tm, w), lambda i: (i, 0))
    return pl.pallas_call(
        body, grid=(S // tm,), name="out_fwd",
        in_specs=[row(D), row(DA), row(D - DA), pl.BlockSpec((D, D), lambda i: (0, 0)),
                  pl.BlockSpec((1, D), lambda i: (0, 0))],
        out_specs=[row(D), row(D), row(D)],
        out_shape=[SDS((S, D), F32), SDS((S, D), BF16), SDS((S, D), BF16)],
        compiler_params=_cp(1))(x, attn, ret, w_out, w_norm)


def _ffn_up(n2, wg, wu):
    S, D = n2.shape
    nblk, _, FB = wg.shape
    tm = min(512, S)

    def body(n_ref, wg_ref, wu_ref, g_ref, u_ref, a_ref):
        n = n_ref[...]
        g = jnp.dot(n, wg_ref[0], preferred_element_type=F32)
        u = jnp.dot(n, wu_ref[0], preferred_element_type=F32)
        g_ref[0] = g
        u_ref[0] = u
        a_ref[0] = (g * _sigmoid(g) * u).astype(BF16)

    wspec = pl.BlockSpec((1, D, FB), lambda j, i: (j, 0, 0))
    ospec = pl.BlockSpec((1, tm, FB), lambda j, i: (j, i, 0))
    return pl.pallas_call(
        body, grid=(nblk, S // tm), name="ffn_up",
        in_specs=[pl.BlockSpec((tm, D), lambda j, i: (i, 0)), wspec, wspec],
        out_specs=[ospec, ospec, ospec],
        out_shape=[SDS((nblk, S, FB), F32), SDS((nblk, S, FB), F32), SDS((nblk, S, FB), BF16)],
        compiler_params=_cp(2))(n2, wg, wu)


def _ffn_down_loss(act, wd, h1, target, w_norm):
    nblk, S, FB = act.shape
    D = h1.shape[1]
    tm = min(512, S)

    def body(a_ref, wd_ref, h_ref, t_ref, wn_ref, dh_ref, dhb_ref, loss_ref, dw_ref, acc):
        i, j = pl.program_id(0), pl.program_id(1)

        @pl.when(j == 0)
        def _():
            acc[...] = h_ref[...]

        @pl.when((i == 0) & (j == 0))
        def _():
            dw_ref[...] = jnp.zeros_like(dw_ref)

        acc[...] += jnp.dot(a_ref[0], wd_ref[...], preferred_element_type=F32)

        @pl.when(j == nblk - 1)
        def _():
            h = acc[...]
            w = wn_ref[...]
            err = h * _rms_scale(h) * w - t_ref[...]
            loss_ref[...] = jnp.full(loss_ref.shape, 0.5 * jnp.sum(err * err) / D, F32)
            dh, dw = _rms_bwd(err * (1.0 / D), h, w)
            dh_ref[...] = dh
            dhb_ref[...] = dh.astype(BF16)
            dw_ref[...] += dw

    row = pl.BlockSpec((tm, D), lambda i, j: (i, 0))
    vec = pl.BlockSpec((1, D), lambda i, j: (0, 0))
    return pl.pallas_call(
        body, grid=(S // tm, nblk), name="ffn_down_loss",
        in_specs=[pl.BlockSpec((1, tm, FB), lambda i, j: (j, i, 0)), pl.BlockSpec((FB, D), lambda i, j: (j, 0)),
                  row, row, vec],
        out_specs=[row, row, pl.BlockSpec((8, 128), lambda i, j: (i, 0)), vec],
        out_shape=[SDS((S, D), F32), SDS((S, D), BF16), SDS((S // tm * 8, 128), F32), SDS((1, D), F32)],
        scratch_shapes=[pltpu.VMEM((tm, D), F32)], compiler_params=_cp(2))(act, wd, h1, target, w_norm)


def _attn_bias():
    n_heads = 8
    slopes = np.exp2(-8.0 * np.arange(1, n_heads + 1, dtype=np.float32) / n_heads)
    dist = np.abs(np.arange(K_TILE)[None, :] - BAND - np.arange(Q_TILE)[:, None])
    out = np.empty((n_heads, len(DILATIONS), Q_TILE, K_TILE), np.float32)
    for h in range(n_heads):
        for p, d in enumerate(DILATIONS):
            out[h, p] = np.where(dist <= BAND, -slopes[h] * (d * dist).astype(np.float32), NEG)
    return jnp.asarray(out)


def _attn_tiles(S, d):
    L = S // d
    per_class = L // Q_TILE
    return L, per_class, d * per_class


def _tile_rows(t, d, per_class):
    r = t // per_class
    a = (t % per_class) * Q_TILE
    q_rows = pl.ds(r + d * a, Q_TILE, stride=d) if d > 1 else pl.ds(pl.multiple_of(a, Q_TILE), Q_TILE)
    k_rows = pl.ds(KV_PAD + r + d * (a - BAND), K_TILE, stride=d) if d > 1 else pl.ds(
        pl.multiple_of(KV_PAD + a - BAND, BAND), K_TILE)
    return a, q_rows, k_rows


def _lanes(x, width):
    return jnp.concatenate([x] * (width // HEAD_DIM), axis=1)


def _edge_mask(a, L):
    lk = lax.broadcasted_iota(jnp.int32, (1, K_TILE), 1) + (a - BAND)
    return jnp.where((lk >= 0) & (lk < L), 0.0, NEG).astype(F32)


def _fill_padded(dst, src, S):
    dst[pl.ds(0, KV_PAD), :] = jnp.zeros((KV_PAD, HEAD_DIM), F32)
    dst[pl.ds(KV_PAD + S, KV_PAD), :] = jnp.zeros((KV_PAD, HEAD_DIM), F32)
    dst[pl.ds(KV_PAD, S), :] = src[...]


def _head_specs(S, groups, n_heads):
    return [pl.BlockSpec((S, HEAD_DIM), functools.partial(lambda h, g: (0, g * n_heads + h), g=g)) for g in groups]


def _attn_fwd(proj, bias):
    S = proj.shape[0]
    H = proj.shape[1] // (N_GROUPS * HEAD_DIM)
    scale = HEAD_DIM ** -0.5

    def body(q_ref, k_ref, v_ref, b_ref, o_ref, lse_ref, kp, vp, m_run, l_run):
        _fill_padded(kp, k_ref, S)
        _fill_padded(vp, v_ref, S)
        o_ref[...] = jnp.zeros_like(o_ref)
        m_run[...] = jnp.full(m_run.shape, NEG, F32)
        l_run[...] = jnp.zeros_like(l_run)
        for p, d in enumerate(DILATIONS):
            L, per_class, n_tiles = _attn_tiles(S, d)

            def tiles(t, carry, p=p, d=d, L=L, per_class=per_class, n_tiles=n_tiles):
                rows = [_tile_rows(t + u * (n_tiles // TILE_GROUP), d, per_class) for u in range(TILE_GROUP)]
                got = [(q_ref[qr, :], kp[kr, :], vp[kr, :], m_run[qr, :], l_run[qr, :], o_ref[qr, :])
                       for _, qr, kr in rows]
                new = []
                for (a, _, _), (qt, kt, vt, m_old, l_old, o_old) in zip(rows, got):
                    s = _dot(qt, kt, _NT) * scale + b_ref[0, p] + _edge_mask(a, L)
                    m_new = jnp.maximum(m_old, jnp.max(s, axis=-1, keepdims=True))
                    pr = jnp.exp(s - _lanes(m_new, K_TILE)).astype(BF16)
                    alpha = jnp.exp(m_old - m_new)
                    new.append((m_new, alpha * l_old + _dot(pr, jnp.ones((K_TILE, HEAD_DIM), BF16), _NN),
                                alpha * o_old + _dot(pr, vt, _NN)))
                for (_, qr, _), (m_new, l_new, o_new) in zip(rows, new):
                    o_ref[qr, :] = o_new
                    m_run[qr, :] = m_new
                    l_run[qr, :] = l_new
                return carry

            lax.fori_loop(0, n_tiles // TILE_GROUP, tiles, 0)
        l = l_run[...]
        o_ref[...] = o_ref[...] / l
        lse_ref[...] = m_run[...] + jnp.log(l)

    hspec = pl.BlockSpec((S, HEAD_DIM), lambda h: (0, h))
    return pl.pallas_call(
        body, grid=(H,), name="attn_fwd",
        in_specs=_head_specs(S, (0, 1, 2), H) + [
            pl.BlockSpec((1, len(DILATIONS), Q_TILE, K_TILE), lambda h: (h, 0, 0, 0))],
        out_specs=[hspec, hspec],
        out_shape=[SDS((S, H * HEAD_DIM), F32), SDS((S, H * HEAD_DIM), F32)],
        scratch_shapes=[pltpu.VMEM((S + 2 * KV_PAD, HEAD_DIM), F32), pltpu.VMEM((S + 2 * KV_PAD, HEAD_DIM), F32),
                        pltpu.VMEM((S, HEAD_DIM), F32), pltpu.VMEM((S, HEAD_DIM), F32)],
        compiler_params=_cp(1))(proj, proj, proj, bias)


def _attn_bwd(proj, out, lse, dmix, bias):
    S = proj.shape[0]
    H = proj.shape[1] // (N_GROUPS * HEAD_DIM)
    scale = HEAD_DIM ** -0.5

    def body(q_ref, k_ref, v_ref, o_ref, lse_ref, do_ref, b_ref, dq_ref, dk_ref, dv_ref, kp, vp, dkp, dvp, dsum):
        _fill_padded(kp, k_ref, S)
        _fill_padded(vp, v_ref, S)
        dkp[...] = jnp.zeros_like(dkp)
        dvp[...] = jnp.zeros_like(dvp)
        dq_ref[...] = jnp.zeros_like(dq_ref)
        dsum[...] = jnp.broadcast_to(jnp.sum(do_ref[...] * o_ref[...], axis=-1, keepdims=True), dsum.shape)
        for p, d in enumerate(DILATIONS):
            L, per_class, n_tiles = _attn_tiles(S, d)

            def tiles(t, carry, p=p, d=d, L=L, per_class=per_class, n_tiles=n_tiles):
                rows = [_tile_rows(t + u * (n_tiles // TILE_GROUP), d, per_class) for u in range(TILE_GROUP)]
                got = [(q_ref[qr, :], kp[kr, :], vp[kr, :], do_ref[qr, :], lse_ref[qr, :], dsum[qr, :],
                        dq_ref[qr, :], dkp[kr, :], dvp[kr, :]) for _, qr, kr in rows]
                new = []
                for (a, _, _), (qt, kt, vt, dot_, lse_t, dsum_t, dq_old, dk_old, dv_old) in zip(rows, got):
                    s = _dot(qt, kt, _NT) * scale + b_ref[0, p] + _edge_mask(a, L)
                    pr = jnp.exp(s - _lanes(lse_t, K_TILE))
                    ds = pr * (_dot(dot_, vt, _NT) - _lanes(dsum_t, K_TILE)) * scale
                    new.append((dq_old + _dot(ds, kt, _NN), dk_old + _dot(ds, qt, _TN), dv_old + _dot(pr, dot_, _TN)))
                for (_, qr, kr), (dq_new, dk_new, dv_new) in zip(rows, new):
                    dq_ref[qr, :] = dq_new
                    dkp[kr, :] = dk_new
                    dvp[kr, :] = dv_new
                return carry

            lax.fori_loop(0, n_tiles // TILE_GROUP, tiles, 0)
        dk_ref[...] = dkp[pl.ds(KV_PAD, S), :]
        dv_ref[...] = dvp[pl.ds(KV_PAD, S), :]

    hspec = pl.BlockSpec((S, HEAD_DIM), lambda h: (0, h))
    padded = pltpu.VMEM((S + 2 * KV_PAD, HEAD_DIM), F32)
    return pl.pallas_call(
        body, grid=(H,), name="attn_bwd",
        in_specs=_head_specs(S, (0, 1, 2), H) + [hspec, hspec, hspec,
                                                  pl.BlockSpec((1, len(DILATIONS), Q_TILE, K_TILE), lambda h: (h, 0, 0, 0))],
        out_specs=[hspec, hspec, hspec],
        out_shape=[SDS((S, H * HEAD_DIM), F32)] * 3,
        scratch_shapes=[padded, padded, padded, padded, pltpu.VMEM((S, HEAD_DIM), F32)],
        compiler_params=_cp(1))(proj, proj, proj, out, lse, dmix, bias)


def _ret_consts(lg, forward):
    C = RET_CHUNK
    i = lax.broadcasted_iota(jnp.int32, (C, C), 0)
    j = lax.broadcasted_iota(jnp.int32, (C, C), 1)
    rel = (i - j) if forward else (j - i)
    inside = (rel >= 0) if forward else (rel > 0)
    relf = jnp.maximum(rel, 0).astype(F32)
    mask = jnp.where(inside, jnp.exp(lg * relf), 0.0)
    idx = lax.broadcasted_iota(jnp.int32, (C, 1), 0).astype(F32)
    q_exp = (idx + 1.0) if forward else (C - idx)
    k_exp = (C - 1.0 - idx) if forward else idx
    return mask, relf, jnp.exp(lg * q_exp), q_exp, jnp.exp(lg * k_exp), k_exp, jnp.exp(lg * C)


def _log_decay(dec_ref, h):
    return -jnp.exp(jnp.full((1, 1), dec_ref[0, h], F32))


def _chunk(c):
    return pl.ds(pl.multiple_of(c * RET_CHUNK, RET_CHUNK), RET_CHUNK)


def _ret_fwd(proj, dec_f, dec_b, w_norm):
    S = proj.shape[0]
    H = proj.shape[1] // (N_GROUPS * HEAD_DIM)
    nc = S // RET_CHUNK
    scale = HEAD_DIM ** -0.5

    def body(df_ref, db_ref, q_ref, k_ref, v_ref, g_ref, w_ref, y_ref, o_ref):
        h = pl.program_id(0)
        consts = [_ret_consts(_log_decay(dref, h), fw) for fw, dref in ((True, df_ref), (False, db_ref))]
        o_ref[...] = jnp.zeros_like(o_ref)

        def step(n, states):
            rows = [_chunk(n), _chunk(nc - 1 - n)]
            got = [(q_ref[r, :] * scale, k_ref[r, :], v_ref[r, :], o_ref[r, :]) for r in rows]
            new_o, new_states = [], []
            for (qc, kc, vc, o_old), (mask, _, q_dec, _, k_dec, _, c_dec), state in zip(got, consts, states):
                new_o.append(o_old + _dot(_dot(qc, kc, _NT) * mask, vc, _NN) + _dot(qc * q_dec, state, _NN))
                new_states.append(state * c_dec + _dot(kc * k_dec, vc, _TN))
            for r, o_new in zip(rows, new_o):
                o_ref[r, :] = o_new
            return tuple(new_states)

        lax.fori_loop(0, nc, step, (jnp.zeros((HEAD_DIM, HEAD_DIM), F32),) * 2)
        o = o_ref[...]
        g = g_ref[...]
        y_ref[...] = o * _rms_scale(o) * w_ref[...] * (g * _sigmoid(g))

    hspec = pl.BlockSpec((S, HEAD_DIM), lambda h: (0, h))
    smem = pl.BlockSpec(memory_space=pltpu.SMEM)
    return pl.pallas_call(
        body, grid=(H,), name="ret_fwd",
        in_specs=[smem, smem] + _head_specs(S, (3, 4, 5, 6), H) + [pl.BlockSpec((1, HEAD_DIM), lambda h: (0, h))],
        out_specs=[hspec, hspec],
        out_shape=[SDS((S, H * HEAD_DIM), F32)] * 2,
        compiler_params=_cp(1))(dec_f, dec_b, proj, proj, proj, proj, w_norm)


def _ret_bwd(proj, o_raw, dmix, dec_f, dec_b, w_norm, col0):
    S = proj.shape[0]
    H = proj.shape[1] // (N_GROUPS * HEAD_DIM)
    C = RET_CHUNK
    nc = S // C
    scale = HEAD_DIM ** -0.5

    def body(df_ref, db_ref, q_ref, k_ref, v_ref, g_ref, o_ref, dy_ref, w_ref,
             dq_ref, dk_ref, dv_ref, dg_ref, small_ref, do, states):
        h = pl.program_id(0)
        o = o_ref[...]
        g = g_ref[...]
        dy = dy_ref[...]
        w = w_ref[...]
        rr = _rms_scale(o)
        normed = o * rr
        sg = _sigmoid(g)
        silu = g * sg
        small_ref[0, pl.ds(2, 1), :] = jnp.sum(dy * normed * silu, axis=0, keepdims=True)
        dg_ref[...] = dy * normed * w * (sg * (1.0 + g * (1.0 - sg)))
        dnormed = dy * w * silu
        do[...] = rr * dnormed - o * (rr * rr * rr) * jnp.mean(dnormed * o, axis=-1, keepdims=True)

        lgs = [_log_decay(df_ref, h), _log_decay(db_ref, h)]
        consts = [_ret_consts(lg, fw) for lg, fw in zip(lgs, (True, False))]
        zero_state = jnp.zeros((HEAD_DIM, HEAD_DIM), F32)

        def fwd_step(n, carry):
            new = []
            for way, (cidx, state) in enumerate(zip((n, nc - 1 - n), carry)):
                k_dec, c_dec = consts[way][4], consts[way][6]
                rows = _chunk(cidx)
                states[way, cidx] = state
                new.append(state * c_dec + _dot(k_ref[rows, :] * k_dec, v_ref[rows, :], _TN))
            return tuple(new)

        lax.fori_loop(0, nc, fwd_step, (zero_state, zero_state))
        for ref in (dq_ref, dk_ref, dv_ref):
            ref[...] = jnp.zeros_like(ref)

        def bwd_step(n, carry):
            cidxs = (nc - 1 - n, n)
            rows = [_chunk(c) for c in cidxs]
            got = [(q_ref[r, :] * scale, k_ref[r, :], v_ref[r, :], do[r, :], states[way, c], dq_ref[r, :], dk_ref[r, :],
                    dv_ref[r, :]) for way, (r, c) in enumerate(zip(rows, cidxs))]
            new_rows, new_carry = [], []
            for (qc, kc, vc, doc, state, dq_old, dk_old, dv_old), cs, (d_state, dlam) in zip(got, consts, carry):
                mask, relf, q_dec, q_exp, k_dec, k_exp, c_dec = cs
                a0 = _dot(qc, kc, _NT)
                dp = _dot(doc, vc, _NT) * mask
                gq = _dot(doc, state, _NT)
                gk = _dot(vc, d_state, _NT)
                new_rows.append((dq_old + (_dot(dp, kc, _NN) + q_dec * gq) * scale,
                                 dk_old + _dot(dp, qc, _TN) + k_dec * gk,
                                 dv_old + _dot(a0 * mask, doc, _TN) + _dot(kc * k_dec, d_state, _NN)))
                dlam = dlam + jnp.sum(relf * a0 * dp, axis=0, keepdims=True) \
                    + jnp.sum(q_exp * q_dec * qc * gq + k_exp * k_dec * kc * gk, axis=0, keepdims=True) \
                    + (C * c_dec) * jnp.sum(state * d_state, axis=0, keepdims=True)
                new_carry.append((d_state * c_dec + _dot(qc * q_dec, doc, _TN), dlam))
            for r, (dq_new, dk_new, dv_new) in zip(rows, new_rows):
                dq_ref[r, :] = dq_new
                dk_ref[r, :] = dk_new
                dv_ref[r, :] = dv_new
            return tuple(new_carry)

        zero_carry = (zero_state, jnp.zeros((1, HEAD_DIM), F32))
        done = lax.fori_loop(0, nc, bwd_step, (zero_carry, zero_carry))
        for row, ((_, dlam), lg) in enumerate(zip(done, lgs)):
            small_ref[0, pl.ds(row, 1), :] = jnp.broadcast_to(jnp.sum(dlam, axis=-1, keepdims=True) * lg, (1, HEAD_DIM))
        small_ref[0, pl.ds(3, 5), :] = jnp.zeros((5, HEAD_DIM), F32)

    hspec = pl.BlockSpec((S, HEAD_DIM), lambda h: (0, h))
    smem = pl.BlockSpec(memory_space=pltpu.SMEM)
    nh0 = col0 // HEAD_DIM
    return pl.pallas_call(
        body, grid=(H,), name="ret_bwd",
        in_specs=[smem, smem] + _head_specs(S, (3, 4, 5, 6), H) + [
            hspec, pl.BlockSpec((S, HEAD_DIM), lambda h: (0, nh0 + h)), pl.BlockSpec((1, HEAD_DIM), lambda h: (0, h))],
        out_specs=[hspec, hspec, hspec, hspec, pl.BlockSpec((1, 8, HEAD_DIM), lambda h: (h, 0, 0))],
        out_shape=[SDS((S, H * HEAD_DIM), F32)] * 4 + [SDS((H, 8, HEAD_DIM), F32)],
        scratch_shapes=[pltpu.VMEM((S, HEAD_DIM), F32), pltpu.VMEM((2, nc, HEAD_DIM, HEAD_DIM), F32)],
        compiler_params=_cp(1))(dec_f, dec_b, proj, proj, proj, proj, o_raw, dmix, w_norm)


def _ffn_bwd_act(dh2, wd, g, u):
    S, D = dh2.shape
    nblk, _, FB = g.shape
    tm = min(512, S)

    def body(dh_ref, wd_ref, g_ref, u_ref, dg_ref, du_ref):
        dact = _dot(dh_ref[...], wd_ref[...], _NT)
        gg = g_ref[0]
        sg = _sigmoid(gg)
        dg_ref[0] = (dact * u_ref[0] * (sg * (1.0 + gg * (1.0 - sg)))).astype(BF16)
        du_ref[0] = (dact * (gg * sg)).astype(BF16)

    blk = pl.BlockSpec((1, tm, FB), lambda j, i: (j, i, 0))
    return pl.pallas_call(
        body, grid=(nblk, S // tm), name="ffn_bwd_act",
        in_specs=[pl.BlockSpec((tm, D), lambda j, i: (i, 0)), pl.BlockSpec((FB, D), lambda j, i: (j, 0)), blk, blk],
        out_specs=[blk, blk], out_shape=[SDS((nblk, S, FB), BF16)] * 2,
        compiler_params=_cp(2))(dh2, wd, g, u)


def _ffn_bwd_in(dg, du, wg, wu, h1, dh2, w_norm):
    nblk, S, FB = dg.shape
    D = h1.shape[1]
    tm = min(512, S)

    def body(dg_ref, du_ref, wg_ref, wu_ref, h_hbm, dh2_hbm, wn_ref, dh_ref, dhb_ref, dw_ref, acc, h_buf, dh2_buf, sems):
        i, j = pl.program_id(0), pl.program_id(1)
        rows = pl.ds(pl.multiple_of(i * tm, tm), tm)
        fetch = [pltpu.make_async_copy(h_hbm.at[rows, :], h_buf, sems.at[0]),
                 pltpu.make_async_copy(dh2_hbm.at[rows, :], dh2_buf, sems.at[1])]

        @pl.when(j == 0)
        def _():
            for cp in fetch:
                cp.start()
            acc[...] = jnp.zeros_like(acc)

        @pl.when((i == 0) & (j == 0))
        def _():
            dw_ref[...] = jnp.zeros_like(dw_ref)

        acc[...] += _dot(dg_ref[0], wg_ref[0], _NT) + _dot(du_ref[0], wu_ref[0], _NT)

        @pl.when(j == nblk - 1)
        def _():
            for cp in fetch:
                cp.wait()
            dh, dw = _rms_bwd(acc[...], h_buf[...], wn_ref[...])
            dh = dh2_buf[...] + dh
            dh_ref[...] = dh
            dhb_ref[...] = dh.astype(BF16)
            dw_ref[...] += dw

    blk = pl.BlockSpec((1, tm, FB), lambda i, j: (j, i, 0))
    wspec = pl.BlockSpec((1, D, FB), lambda i, j: (j, 0, 0))
    row = pl.BlockSpec((tm, D), lambda i, j: (i, 0))
    vec = pl.BlockSpec((1, D), lambda i, j: (0, 0))
    return pl.pallas_call(
        body, grid=(S // tm, nblk), name="ffn_bwd_in",
        in_specs=[blk, blk, wspec, wspec, ANY, ANY, vec],
        out_specs=[row, row, vec], out_shape=[SDS((S, D), F32), SDS((S, D), BF16), SDS((1, D), F32)],
        scratch_shapes=[pltpu.VMEM((tm, D), F32), pltpu.VMEM((tm, D), F32), pltpu.VMEM((tm, D), F32),
                        pltpu.SemaphoreType.DMA((2,))],
        compiler_params=_cp(2))(dg, du, wg, wu, h1, dh2, w_norm)


def _dmix(dh1, w_out):
    S, D = dh1.shape
    tm = min(512, S)

    def body(dh_ref, w_ref, o_ref):
        o_ref[...] = _dot(dh_ref[...], w_ref[...], _NT)

    row = pl.BlockSpec((tm, D), lambda i: (i, 0))
    return pl.pallas_call(
        body, grid=(S // tm,), name="dmix", in_specs=[row, pl.BlockSpec((D, D), lambda i: (0, 0))],
        out_specs=row, out_shape=SDS((S, D), F32), compiler_params=_cp(1))(dh1, w_out)


def _in_bwd(dproj, w_blk, x, dh1, w_norm):
    S, D = x.shape
    nblk, _, NB = w_blk.shape
    tm = min(512, S)

    def body(dp_ref, w_ref, x_ref, dh1_ref, wn_ref, dx_ref, dw_ref, acc):
        i, j = pl.program_id(0), pl.program_id(1)

        @pl.when(j == 0)
        def _():
            acc[...] = jnp.zeros_like(acc)

        @pl.when((i == 0) & (j == 0))
        def _():
            dw_ref[...] = jnp.zeros_like(dw_ref)

        acc[...] += _dot(dp_ref[...], w_ref[0], _NT)

        @pl.when(j == nblk - 1)
        def _():
            dh, dw = _rms_bwd(acc[...], x_ref[...], wn_ref[...])
            dx_ref[...] = dh1_ref[...] + dh
            dw_ref[...] += dw

    row = pl.BlockSpec((tm, D), lambda i, j: (i, 0))
    vec = pl.BlockSpec((1, D), lambda i, j: (0, 0))
    return pl.pallas_call(
        body, grid=(S // tm, nblk), name="in_bwd",
        in_specs=[pl.BlockSpec((tm, NB), lambda i, j: (i, j)), pl.BlockSpec((1, D, NB), lambda i, j: (j, 0, 0)),
                  row, row, vec],
        out_specs=[row, vec], out_shape=[SDS((S, D), F32), SDS((1, D), F32)],
        scratch_shapes=[pltpu.VMEM((tm, D), F32)], compiler_params=_cp(2))(dproj, w_blk, x, dh1, w_norm)


def _wgrad(a, b, a_spec, b_spec, o_spec, o_shape, grid, name):
    nk = grid[-1]

    def ld(ref):
        return ref[0] if len(ref.shape) == 3 else ref[...]

    def body(a_ref, b_ref, o_ref, acc):
        k = pl.program_id(len(grid) - 1)

        @pl.when(k == 0)
        def _():
            acc[...] = jnp.zeros_like(acc)

        acc[...] += _dot(ld(a_ref), ld(b_ref), _TN)

        @pl.when(k == nk - 1)
        def _():
            if len(o_ref.shape) == 3:
                o_ref[0] = acc[...].astype(o_ref.dtype)
            else:
                o_ref[...] = acc[...].astype(o_ref.dtype)

    return pl.pallas_call(
        body, grid=grid, name=name, in_specs=[a_spec, b_spec], out_specs=o_spec, out_shape=SDS(o_shape, BF16),
        scratch_shapes=[pltpu.VMEM(o_spec.block_shape[-2:], F32)], compiler_params=_cp(len(grid)))(a, b)


def _peer(k):
    x, y, c = lax.axis_index("x"), lax.axis_index("y"), lax.axis_index("c")
    px = 1 - x if k & 4 else x
    py = 1 - y if k & 2 else y
    pc = 1 - c if k & 1 else c
    return (px, py, pc), 4 * px + 2 * py + pc


def _exchange_copies(srcs, lands, send_sems, recv_sems, which, gather):
    _, me = _peer(0)
    pairs = []
    for pos, a in enumerate(which):
        for k in range(1, N_DEV):
            dev, idx = _peer(k)
            sem = pos * (N_DEV - 1) + k - 1
            src = srcs[a] if gather else srcs[a].at[idx]
            mk = functools.partial(pltpu.make_async_remote_copy, src_ref=src, send_sem=send_sems.at[sem],
                                   recv_sem=recv_sems.at[sem], device_id=dev, device_id_type=MESH)
            pairs.append((mk(dst_ref=lands[a].at[me]), mk(dst_ref=lands[a].at[idx])))
    return pairs


def _sequencer_kernel(name, collective_id, n_remote, n_local):
    return pl.kernel(mesh=plsc.ScalarSubcoreMesh(axis_name="sequencer", num_cores=1), name=name,
                     scratch_types=(pltpu.SemaphoreType.DMA((n_remote,)), pltpu.SemaphoreType.DMA((n_remote,)),
                                    pltpu.SemaphoreType.DMA((n_local,))),
                     compiler_params=pltpu.CompilerParams(collective_id=collective_id))


def _handshake(ks):
    barrier = pltpu.get_barrier_semaphore()
    for k in ks:
        pl.semaphore_signal(barrier, inc=1, device_id=_peer(k)[0], device_id_type=MESH)
    pl.semaphore_wait(barrier, len(ks))


def _sequencer_scatter(arrays, name, collective_id):
    n = len(arrays)
    hbm = pltpu.MemorySpace.HBM
    srcs = [jax.new_ref(a, memory_space=hbm) for a in arrays]
    lands = [jax.empty_ref(SDS(a.shape, a.dtype), memory_space=hbm) for a in arrays]

    @_sequencer_kernel(name, collective_id, n * (N_DEV - 1), n)
    def launch(send_sems, recv_sems, local_sems):
        _handshake(range(1, N_DEV))
        _, me = _peer(0)
        local = [pltpu.make_async_copy(srcs[a].at[me], lands[a].at[me], local_sems.at[a]) for a in range(n)]
        pairs = _exchange_copies(srcs, lands, send_sems, recv_sems, range(n), False)
        for out, _ in pairs:
            out.start()
        for cp in local:
            cp.start()
        for out, arrival in pairs:
            out.wait_send()
            arrival.wait_recv()
        for cp in local:
            cp.wait()

    launch()
    return [r[...] for r in lands]


SIBLING = 1
OTHER_CHIPS = (2, 4, 6)


def _sequencer_gather(arrays, name, collective_id):
    n = len(arrays)
    hbm = pltpu.MemorySpace.HBM
    srcs = [jax.new_ref(a, memory_space=hbm) for a in arrays]
    lands = [jax.empty_ref(SDS((N_DEV,) + a.shape, a.dtype), memory_space=hbm) for a in arrays]

    @_sequencer_kernel(name, collective_id, n * (N_DEV - 1), n)
    def launch(send_sems, recv_sems, local_sems):
        _handshake((SIBLING,) + OTHER_CHIPS)
        _, me = _peer(0)
        sibling, _ = _peer(SIBLING)

        def copy(a, k, src, block, to):
            sem = a * (N_DEV - 1) + k - 1
            return pltpu.make_async_remote_copy(src_ref=src, dst_ref=lands[a].at[block], send_sem=send_sems.at[sem],
                                                recv_sem=recv_sems.at[sem], device_id=to, device_id_type=MESH)

        local = [pltpu.make_async_copy(srcs[a], lands[a].at[me], local_sems.at[a]) for a in range(n)]
        first = [copy(a, k, srcs[a], me, _peer(k)[0]) for a in range(n) for k in OTHER_CHIPS + (SIBLING,)]
        for cp in first + local:
            cp.start()
        passed = []
        for a in range(n):
            for k in OTHER_CHIPS:
                _, block = _peer(k)
                copy(a, k, srcs[a], block, sibling).wait_recv()
                passed.append(copy(a, k ^ SIBLING, lands[a].at[block], block, sibling))
                passed[-1].start()
        for a in range(n):
            for k in (SIBLING,) + tuple(k ^ SIBLING for k in OTHER_CHIPS):
                copy(a, k, srcs[a], _peer(k)[1], sibling).wait_recv()
        for cp in first + passed:
            cp.wait_send()
        for cp in local:
            cp.wait()

    launch()
    return [r[...] for r in lands]


SMALL_ROWS = 64


def _small_step(part, w, m, v):
    def body(p_ref, w_ref, m_ref, v_ref, g_ref, d_ref, nm_ref, nv_ref, gath, send_sems, recv_sems):
        _, me = _peer(0)
        gath[me] = p_ref[...]
        copies = []
        for k in range(1, N_DEV):
            dev, idx = _peer(k)
            out = pltpu.make_async_remote_copy(src_ref=p_ref, dst_ref=gath.at[me], send_sem=send_sems.at[k - 1],
                                               recv_sem=recv_sems.at[k - 1], device_id=dev, device_id_type=MESH)
            out.start()
            arrival = pltpu.make_async_remote_copy(src_ref=p_ref, dst_ref=gath.at[idx], send_sem=send_sems.at[k - 1],
                                                   recv_sem=recv_sems.at[k - 1], device_id=dev, device_id_type=MESH)
            copies.append((out, arrival))
        for out, arrival in copies:
            out.wait_send()
            arrival.wait_recv()
        g = gath[0]
        for p in range(1, N_DEV):
            g = g + gath[p]
        g_ref[...] = g
        d_ref[...], nm_ref[...], nv_ref[...] = _adamw(w_ref[...], g, m_ref[...], v_ref[...])

    vm = pl.BlockSpec(memory_space=pltpu.VMEM)
    return pl.pallas_call(
        body, name="small_step", in_specs=[vm] * 4, out_specs=[vm] * 4,
        out_shape=[SDS((SMALL_ROWS, 128), F32)] * 4,
        scratch_shapes=[pltpu.VMEM((N_DEV, SMALL_ROWS, 128), F32), pltpu.SemaphoreType.DMA((N_DEV - 1,)),
                        pltpu.SemaphoreType.DMA((N_DEV - 1,))])(part, w, m, v)


def _adamw(w, g, m, v):
    m = ADAM_B1 * m + (1.0 - ADAM_B1) * g
    v = ADAM_B2 * v + (1.0 - ADAM_B2) * (g * g)
    m_hat = m / (1.0 - ADAM_B1 ** ADAM_STEP)
    v_hat = v / (1.0 - ADAM_B2 ** ADAM_STEP)
    delta = -ADAM_LR * (m_hat / (jnp.sqrt(v_hat) + ADAM_EPS) + ADAM_WD * w)
    return delta, m, v


def _adamw_block(parts, w, m, v, name):
    R, C = w.shape
    tr = next(t for t in (256, 128, 64, 32, 16, 8) if R % t == 0 and t * C <= 256 * 1024)

    def body(p_ref, w_ref, m_ref, v_ref, g_ref, d_ref, nm_ref, nv_ref):
        g = p_ref[0].astype(F32)
        for p in range(1, N_DEV):
            g = g + p_ref[p].astype(F32)
        g_ref[...] = g
        d_ref[...], nm_ref[...], nv_ref[...] = _adamw(w_ref[...], g, m_ref[...], v_ref[...])

    row = pl.BlockSpec((tr, C), lambda i: (i, 0))
    return pl.pallas_call(
        body, grid=(R // tr,), name=name, in_specs=[pl.BlockSpec((N_DEV, tr, C), lambda i: (0, i, 0)), row, row, row],
        out_specs=[row] * 4, out_shape=[SDS((R, C), F32)] * 4, compiler_params=_cp(1))(parts, w, m, v)


def _pack_small(mix, ffn, fin, retw, dec_f, dec_b, loss):
    flat = jnp.concatenate([mix.reshape(-1), ffn.reshape(-1), fin.reshape(-1), retw.reshape(-1), dec_f.reshape(-1),
                            dec_b.reshape(-1), loss.reshape(-1)])
    return jnp.pad(flat, (0, SMALL_ROWS * 128 - flat.shape[0])).reshape(SMALL_ROWS, 128)


def _unpack_small(packed, shapes):
    flat = packed.reshape(-1)
    out, at = [], 0
    for s in shapes:
        n = math.prod(s)
        out.append(flat[at:at + n].reshape(s))
        at += n
    return out


def kernel(x, norm_mix_w, w_in, ret_decay_fwd, ret_decay_bwd, ret_norm_w, w_out, norm_ffn_w, w_gate, w_up, w_down, norm_final_w, loss_target, m_norm_mix_w, m_w_in, m_ret_decay_fwd, m_ret_decay_bwd, m_ret_norm_w, m_w_out, m_norm_ffn_w, m_w_gate, m_w_up, m_w_down, m_norm_final_w, v_norm_mix_w, v_w_in, v_ret_decay_fwd, v_ret_decay_bwd, v_ret_norm_w, v_w_out, v_norm_ffn_w, v_w_gate, v_w_up, v_w_down, v_norm_final_w):
    x2 = x[0]
    tgt = loss_target[0]
    S, D = x2.shape
    H = ret_norm_w.shape[1] // HEAD_DIM
    DA = H * HEAD_DIM
    fin_w = norm_final_w.reshape(1, D)
    big = (w_in[0], w_out[0], w_gate[0], w_up[0], w_down[0])

    big_b = [w.astype(BF16) for w in big]
    wi, = _sequencer_gather(big_b[:1], "gather_in", 0)
    wo, wg, wu = _sequencer_gather(big_b[1:4], "gather_mid", 1)
    wd, = _sequencer_gather(big_b[4:], "gather_down", 5)
    NB = wi.shape[2]

    proj, n1 = _proj_fwd(x2, norm_mix_w, wi)
    bias = _attn_bias()[:H]
    attn, lse = _attn_fwd(proj, bias)
    ret, o_raw = _ret_fwd(proj, ret_decay_fwd, ret_decay_bwd, ret_norm_w)
    wo_full = wo.reshape(D, D)
    FB = wd.shape[1]
    wd_full = wd.reshape(N_DEV * FB, D)
    h1, mixed, n2 = _out_fwd(x2, attn, ret, wo_full, norm_ffn_w)
    gate, up, act = _ffn_up(n2, wg, wu)
    dh2, dh2_b, loss_parts, g_fin = _ffn_down_loss(act, wd_full, h1, tgt, fin_w)

    dgate, dup = _ffn_bwd_act(dh2_b, wd_full, gate, up)
    tk = min(512, S)
    nk = S // tk
    g_wd = _wgrad(act, dh2_b, pl.BlockSpec((1, tk, FB), lambda j, k: (j, k, 0)), pl.BlockSpec((tk, D), lambda j, k: (k, 0)),
                  pl.BlockSpec((1, FB, D), lambda j, k: (j, 0, 0)), (N_DEV, FB, D), (N_DEV, nk), "wgrad_down")
    tmw = min(1024, D)
    gu_specs = (pl.BlockSpec((tk, tmw), lambda j, m, k: (k, m)), pl.BlockSpec((1, tk, FB), lambda j, m, k: (j, k, 0)),
                pl.BlockSpec((1, tmw, FB), lambda j, m, k: (j, m, 0)), (N_DEV, D, FB), (N_DEV, D // tmw, nk))
    g_wg = _wgrad(n2, dgate, *gu_specs, "wgrad_gate")
    g_wu = _wgrad(n2, dup, *gu_specs, "wgrad_up")
    parts_f = _sequencer_scatter([g_wg, g_wu, g_wd], "scatter_ffn", 2)
    dh1, dh1_b, g_ffn = _ffn_bwd_in(dgate, dup, wg, wu, h1, dh2, norm_ffn_w)
    dmix = _dmix(dh1_b, wo_full)
    g_wo = _wgrad(mixed, dh1_b, pl.BlockSpec((tk, tmw), lambda m, k: (k, m)), pl.BlockSpec((tk, D), lambda m, k: (k, 0)),
                  pl.BlockSpec((tmw, D), lambda m, k: (m, 0)), (D, D), (D // tmw, nk), "wgrad_out")
    parts_o = _sequencer_scatter([g_wo.reshape(N_DEV, D // N_DEV, D)], "scatter_out", 3)
    dq_r, dk_r, dv_r, dg_r, small = _ret_bwd(proj, o_raw, dmix, ret_decay_fwd, ret_decay_bwd, ret_norm_w, DA)
    dq_a, dk_a, dv_a = _attn_bwd(proj, attn, lse, dmix, bias)
    dproj = jnp.concatenate([t.astype(BF16) for t in (dq_a, dk_a, dv_a, dq_r, dk_r, dv_r, dg_r)], axis=1)
    g_wi = _wgrad(n1, dproj, pl.BlockSpec((tk, tmw), lambda j, m, k: (k, m)), pl.BlockSpec((tk, NB), lambda j, m, k: (k, j)),
                  pl.BlockSpec((1, tmw, NB), lambda j, m, k: (j, m, 0)), (N_DEV, D, NB), (N_DEV, D // tmw, nk), "wgrad_in")
    parts_i = _sequencer_scatter([g_wi], "scatter_in", 4)
    grad_x, g_mix = _in_bwd(dproj, wi, x2, dh1, norm_mix_w)

    big_m = (m_w_in[0], m_w_out[0], m_w_gate[0], m_w_up[0], m_w_down[0])
    big_v = (v_w_in[0], v_w_out[0], v_w_gate[0], v_w_up[0], v_w_down[0])
    names = ("adamw_in", "adamw_out", "adamw_gate", "adamw_up", "adamw_down")
    upd = [None] * 5
    for a, p in zip((2, 3, 4, 1, 0), parts_f + parts_o + parts_i):
        upd[a] = _adamw_block(p, big[a], big_m[a], big_v[a], names[a])

    g_dec_f = small[:, 0, 0].reshape(1, H)
    g_dec_b = small[:, 1, 0].reshape(1, H)
    g_retw = small[:, 2, :].reshape(1, DA)
    loss_local = jnp.sum(loss_parts[::8, 0])
    zero = jnp.zeros((1,), F32)
    part = _pack_small(g_mix, g_ffn, g_fin, g_retw, g_dec_f, g_dec_b, loss_local)
    sw = _pack_small(norm_mix_w, norm_ffn_w, norm_final_w, ret_norm_w, ret_decay_fwd, ret_decay_bwd, zero)
    sm = _pack_small(m_norm_mix_w, m_norm_ffn_w, m_norm_final_w, m_ret_norm_w, m_ret_decay_fwd, m_ret_decay_bwd, zero)
    sv = _pack_small(v_norm_mix_w, v_norm_ffn_w, v_norm_final_w, v_ret_norm_w, v_ret_decay_fwd, v_ret_decay_bwd, zero)
    shapes = [(1, D), (1, D), (D,), (1, DA), (1, H), (1, H), ()]
    sg, sd, snm, snv = [_unpack_small(t, shapes) for t in _small_step(part, sw, sm, sv)]
    loss = sg[6]

    def ordered(small_set, k):
        b = [u[k][None] for u in upd]
        return [small_set[0], b[0], small_set[4], small_set[5], small_set[3], b[1], small_set[1], b[2], b[3], b[4],
                small_set[2]]

    return (loss, grad_x[None], *ordered(sg, 0), *ordered(sd, 1), *ordered(snm, 2), *ordered(snv, 3))
```

```python
import functools
import math

import numpy as np
import jax
import jax.numpy as jnp
from jax import lax
from jax.experimental import pallas as pl
from jax.experimental.pallas import tpu as pltpu
from jax.experimental.pallas import tpu_sc as plsc

F32 = jnp.float32
BF16 = jnp.bfloat16
SDS = jax.ShapeDtypeStruct

HEAD_DIM = 128
EPS = 1e-6
RET_CHUNK = 128
DILATIONS = (1, 4, 16)
BAND = 64
Q_TILE = 128
K_TILE = Q_TILE + 2 * BAND
KV_PAD = BAND * max(DILATIONS)
TILE_GROUP = 2
NEG = -1e30
N_DEV = 8
N_GROUPS = 7
ADAM_LR, ADAM_B1, ADAM_B2, ADAM_EPS, ADAM_WD, ADAM_STEP = 0.001, 0.9, 0.999, 1e-08, 0.01, 10
VMEM_LIMIT = 56 * 1024 * 1024
MESH = pl.DeviceIdType.MESH
ANY = pl.BlockSpec(memory_space=pl.ANY)


def _cp(n_grid):
    return pltpu.CompilerParams(dimension_semantics=("arbitrary",) * n_grid, vmem_limit_bytes=VMEM_LIMIT)


def _sigmoid(x):
    return 1.0 / (1.0 + jnp.exp(-x))


def _rms_scale(h):
    return lax.rsqrt(jnp.mean(h * h, axis=-1, keepdims=True) + EPS)


def _rms_bwd(dn, h, w):
    r = _rms_scale(h)
    gw = dn * w
    dh = r * gw - h * (r * r * r) * jnp.mean(gw * h, axis=-1, keepdims=True)
    return dh, jnp.sum(dn * h * r, axis=0, keepdims=True)


def _dot(a, b, dims):
    return lax.dot_general(a.astype(BF16), b.astype(BF16), (dims, ((), ())), preferred_element_type=F32)


_NN = ((1,), (0,))
_NT = ((1,), (1,))
_TN = ((0,), (0,))


def _proj_fwd(x, w_norm, w_blk):
    S, D = x.shape
    nblk, _, NB = w_blk.shape
    tm = min(1024, S)

    def body(x_ref, wn_ref, w_ref, proj_ref, n_ref, n_scr):
        @pl.when(pl.program_id(1) == 0)
        def _():
            xf = x_ref[...]
            nb = (xf * _rms_scale(xf) * wn_ref[...]).astype(BF16)
            n_scr[...] = nb
            n_ref[...] = nb
        proj_ref[...] = jnp.dot(n_scr[...], w_ref[0], preferred_element_type=F32)

    return pl.pallas_call(
        body, grid=(S // tm, nblk), name="proj_fwd",
        in_specs=[pl.BlockSpec((tm, D), lambda i, j: (i, 0)), pl.BlockSpec((1, D), lambda i, j: (0, 0)),
                  pl.BlockSpec((1, D, NB), lambda i, j: (j, 0, 0))],
        out_specs=[pl.BlockSpec((tm, NB), lambda i, j: (i, j)), pl.BlockSpec((tm, D), lambda i, j: (i, 0))],
        out_shape=[SDS((S, nblk * NB), F32), SDS((S, D), BF16)],
        scratch_shapes=[pltpu.VMEM((tm, D), BF16)], compiler_params=_cp(2))(x, w_norm, w_blk)


def _out_fwd(x, attn, ret, w_out, w_norm):
    S, D = x.shape
    DA = attn.shape[1]
    tm = min(256, S)

    def body(x_ref, a_ref, r_ref, w_ref, wn_ref, h_ref, mix_ref, n_ref):
        a = a_ref[...].astype(BF16)
        r = r_ref[...].astype(BF16)
        mix_ref[:, :DA] = a
        mix_ref[:, DA:] = r
        h = x_ref[...] + jnp.dot(a, w_ref[:DA, :], preferred_element_type=F32) \
            + jnp.dot(r, w_ref[DA:, :], preferred_element_type=F32)
        h_ref[...] = h
        n_ref[...] = (h * _rms_scale(h) * wn_ref[...]).astype(BF16)

    row = lambda w: pl.BlockSpec((tm, w), lambda i: (i, 0))
    return pl.pallas_call(
        body, grid=(S // tm,), name="out_fwd",
        in_specs=[row(D), row(DA), row(D - DA), pl.BlockSpec((D, D), lambda i: (0, 0)),
                  pl.BlockSpec((1, D), lambda i: (0, 0))],
        out_specs=[row(D), row(D), row(D)],
        out_shape=[SDS((S, D), F32), SDS((S, D), BF16), SDS((S, D), BF16)],
        compiler_params=_cp(1))(x, attn, ret, w_out, w_norm)


def _ffn_up(n2, wg, wu):
    S, D = n2.shape
    nblk, FB, _ = wg.shape
    tm = min(512, S)

    def body(n_ref, wg_ref, wu_ref, g_ref, u_ref, a_ref):
        n = n_ref[...]
        g = _dot(n, wg_ref[0], _NT)
        u = _dot(n, wu_ref[0], _NT)
        g_ref[0] = g
        u_ref[0] = u
        a_ref[0] = (g * _sigmoid(g) * u).astype(BF16)

    wspec = pl.BlockSpec((1, FB, D), lambda j, i: (j, 0, 0))
    ospec = pl.BlockSpec((1, tm, FB), lambda j, i: (j, i, 0))
    return pl.pallas_call(
        body, grid=(nblk, S // tm), name="ffn_up",
        in_specs=[pl.BlockSpec((tm, D), lambda j, i: (i, 0)), wspec, wspec],
        out_specs=[ospec, ospec, ospec],
        out_shape=[SDS((nblk, S, FB), F32), SDS((nblk, S, FB), F32), SDS((nblk, S, FB), BF16)],
        compiler_params=_cp(2))(n2, wg, wu)


def _ffn_down_loss(act, wd, h1, target, w_norm):
    nblk, S, FB = act.shape
    D = h1.shape[1]
    tm = min(512, S)

    def body(a_ref, wd_ref, h_ref, t_ref, wn_ref, dh_ref, dhb_ref, loss_ref, dw_ref, acc):
        i, j = pl.program_id(0), pl.program_id(1)

        @pl.when(j == 0)
        def _():
            acc[...] = h_ref[...]

        @pl.when((i == 0) & (j == 0))
        def _():
            dw_ref[...] = jnp.zeros_like(dw_ref)

        acc[...] += jnp.dot(a_ref[0], wd_ref[...], preferred_element_type=F32)

        @pl.when(j == nblk - 1)
        def _():
            h = acc[...]
            w = wn_ref[...]
            err = h * _rms_scale(h) * w - t_ref[...]
            loss_ref[...] = jnp.full(loss_ref.shape, 0.5 * jnp.sum(err * err) / D, F32)
            dh, dw = _rms_bwd(err * (1.0 / D), h, w)
            dh_ref[...] = dh
            dhb_ref[...] = dh.astype(BF16)
            dw_ref[...] += dw

    row = pl.BlockSpec((tm, D), lambda i, j: (i, 0))
    vec = pl.BlockSpec((1, D), lambda i, j: (0, 0))
    return pl.pallas_call(
        body, grid=(S // tm, nblk), name="ffn_down_loss",
        in_specs=[pl.BlockSpec((1, tm, FB), lambda i, j: (j, i, 0)), pl.BlockSpec((FB, D), lambda i, j: (j, 0)),
                  row, row, vec],
        out_specs=[row, row, pl.BlockSpec((8, 128), lambda i, j: (i, 0)), vec],
        out_shape=[SDS((S, D), F32), SDS((S, D), BF16), SDS((S // tm * 8, 128), F32), SDS((1, D), F32)],
        scratch_shapes=[pltpu.VMEM((tm, D), F32)], compiler_params=_cp(2))(act, wd, h1, target, w_norm)


def _attn_bias():
    n_heads = 8
    slopes = np.exp2(-8.0 * np.arange(1, n_heads + 1, dtype=np.float32) / n_heads)
    dist = np.abs(np.arange(K_TILE)[None, :] - BAND - np.arange(Q_TILE)[:, None])
    out = np.empty((n_heads, len(DILATIONS), Q_TILE, K_TILE), np.float32)
    for h in range(n_heads):
        for p, d in enumerate(DILATIONS):
            out[h, p] = np.where(dist <= BAND, -slopes[h] * (d * dist).astype(np.float32), NEG)
    return jnp.asarray(out)


def _attn_tiles(S, d):
    L = S // d
    per_class = L // Q_TILE
    return L, per_class, d * per_class


def _tile_rows(t, d, per_class):
    r = t // per_class
    a = (t % per_class) * Q_TILE
    q_rows = pl.ds(r + d * a, Q_TILE, stride=d) if d > 1 else pl.ds(pl.multiple_of(a, Q_TILE), Q_TILE)
    k_rows = pl.ds(KV_PAD + r + d * (a - BAND), K_TILE, stride=d) if d > 1 else pl.ds(
        pl.multiple_of(KV_PAD + a - BAND, BAND), K_TILE)
    return a, q_rows, k_rows


def _lanes(x, width):
    return jnp.concatenate([x] * (width // HEAD_DIM), axis=1)


def _edge_mask(a, L):
    lk = lax.broadcasted_iota(jnp.int32, (1, K_TILE), 1) + (a - BAND)
    return jnp.where((lk >= 0) & (lk < L), 0.0, NEG).astype(F32)


def _fill_padded(dst, src, S):
    dst[pl.ds(0, KV_PAD), :] = jnp.zeros((KV_PAD, HEAD_DIM), F32)
    dst[pl.ds(KV_PAD + S, KV_PAD), :] = jnp.zeros((KV_PAD, HEAD_DIM), F32)
    dst[pl.ds(KV_PAD, S), :] = src[...]


def _head_specs(S, groups, n_heads):
    return [pl.BlockSpec((S, HEAD_DIM), functools.partial(lambda h, g: (0, g * n_heads + h), g=g)) for g in groups]


def _attn_fwd(proj, bias):
    S = proj.shape[0]
    H = proj.shape[1] // (N_GROUPS * HEAD_DIM)
    scale = HEAD_DIM ** -0.5

    def body(q_ref, k_ref, v_ref, b_ref, o_ref, lse_ref, kp, vp, m_run, l_run):
        _fill_padded(kp, k_ref, S)
        _fill_padded(vp, v_ref, S)
        o_ref[...] = jnp.zeros_like(o_ref)
        m_run[...] = jnp.full(m_run.shape, NEG, F32)
        l_run[...] = jnp.zeros_like(l_run)
        for p, d in enumerate(DILATIONS):
            L, per_class, n_tiles = _attn_tiles(S, d)

            def tiles(t, carry, p=p, d=d, L=L, per_class=per_class, n_tiles=n_tiles):
                rows = [_tile_rows(t + u * (n_tiles // TILE_GROUP), d, per_class) for u in range(TILE_GROUP)]
                got = [(q_ref[qr, :], kp[kr, :], vp[kr, :], m_run[qr, :], l_run[qr, :], o_ref[qr, :])
                       for _, qr, kr in rows]
                new = []
                for (a, _, _), (qt, kt, vt, m_old, l_old, o_old) in zip(rows, got):
                    s = _dot(qt, kt, _NT) * scale + b_ref[0, p] + _edge_mask(a, L)
                    m_new = jnp.maximum(m_old, jnp.max(s, axis=-1, keepdims=True))
                    pr = jnp.exp(s - _lanes(m_new, K_TILE)).astype(BF16)
                    alpha = jnp.exp(m_old - m_new)
                    new.append((m_new, alpha * l_old + _dot(pr, jnp.ones((K_TILE, HEAD_DIM), BF16), _NN),
                                alpha * o_old + _dot(pr, vt, _NN)))
                for (_, qr, _), (m_new, l_new, o_new) in zip(rows, new):
                    o_ref[qr, :] = o_new
                    m_run[qr, :] = m_new
                    l_run[qr, :] = l_new
                return carry

            lax.fori_loop(0, n_tiles // TILE_GROUP, tiles, 0)
        l = l_run[...]
        o_ref[...] = o_ref[...] / l
        lse_ref[...] = m_run[...] + jnp.log(l)

    hspec = pl.BlockSpec((S, HEAD_DIM), lambda h: (0, h))
    return pl.pallas_call(
        body, grid=(H,), name="attn_fwd",
        in_specs=_head_specs(S, (0, 1, 2), H) + [
            pl.BlockSpec((1, len(DILATIONS), Q_TILE, K_TILE), lambda h: (h, 0, 0, 0))],
        out_specs=[hspec, hspec],
        out_shape=[SDS((S, H * HEAD_DIM), F32), SDS((S, H * HEAD_DIM), F32)],
        scratch_shapes=[pltpu.VMEM((S + 2 * KV_PAD, HEAD_DIM), F32), pltpu.VMEM((S + 2 * KV_PAD, HEAD_DIM), F32),
                        pltpu.VMEM((S, HEAD_DIM), F32), pltpu.VMEM((S, HEAD_DIM), F32)],
        compiler_params=_cp(1))(proj, proj, proj, bias)


def _attn_bwd(proj, out, lse, dmix, bias):
    S = proj.shape[0]
    H = proj.shape[1] // (N_GROUPS * HEAD_DIM)
    scale = HEAD_DIM ** -0.5

    def body(q_ref, k_ref, v_ref, o_ref, lse_ref, do_ref, b_ref, dq_ref, dk_ref, dv_ref, kp, vp, dkp, dvp, dsum):
        _fill_padded(kp, k_ref, S)
        _fill_padded(vp, v_ref, S)
        dkp[...] = jnp.zeros_like(dkp)
        dvp[...] = jnp.zeros_like(dvp)
        dq_ref[...] = jnp.zeros_like(dq_ref)
        dsum[...] = jnp.broadcast_to(jnp.sum(do_ref[...] * o_ref[...], axis=-1, keepdims=True), dsum.shape)
        for p, d in enumerate(DILATIONS):
            L, per_class, n_tiles = _attn_tiles(S, d)

            def tiles(t, carry, p=p, d=d, L=L, per_class=per_class, n_tiles=n_tiles):
                rows = [_tile_rows(t + u * (n_tiles // TILE_GROUP), d, per_class) for u in range(TILE_GROUP)]
                got = [(q_ref[qr, :], kp[kr, :], vp[kr, :], do_ref[qr, :], lse_ref[qr, :], dsum[qr, :],
                        dq_ref[qr, :], dkp[kr, :], dvp[kr, :]) for _, qr, kr in rows]
                new = []
                for (a, _, _), (qt, kt, vt, dot_, lse_t, dsum_t, dq_old, dk_old, dv_old) in zip(rows, got):
                    s = _dot(qt, kt, _NT) * scale + b_ref[0, p] + _edge_mask(a, L)
                    pr = jnp.exp(s - _lanes(lse_t, K_TILE))
                    ds = pr * (_dot(dot_, vt, _NT) - _lanes(dsum_t, K_TILE)) * scale
                    new.append((dq_old + _dot(ds, kt, _NN), dk_old + _dot(ds, qt, _TN), dv_old + _dot(pr, dot_, _TN)))
                for (_, qr, kr), (dq_new, dk_new, dv_new) in zip(rows, new):
                    dq_ref[qr, :] = dq_new
                    dkp[kr, :] = dk_new
                    dvp[kr, :] = dv_new
                return carry

            lax.fori_loop(0, n_tiles // TILE_GROUP, tiles, 0)
        dk_ref[...] = dkp[pl.ds(KV_PAD, S), :]
        dv_ref[...] = dvp[pl.ds(KV_PAD, S), :]

    hspec = pl.BlockSpec((S, HEAD_DIM), lambda h: (0, h))
    padded = pltpu.VMEM((S + 2 * KV_PAD, HEAD_DIM), F32)
    return pl.pallas_call(
        body, grid=(H,), name="attn_bwd",
        in_specs=_head_specs(S, (0, 1, 2), H) + [hspec, hspec, hspec,
                                                  pl.BlockSpec((1, len(DILATIONS), Q_TILE, K_TILE), lambda h: (h, 0, 0, 0))],
        out_specs=[hspec, hspec, hspec],
        out_shape=[SDS((S, H * HEAD_DIM), F32)] * 3,
        scratch_shapes=[padded, padded, padded, padded, pltpu.VMEM((S, HEAD_DIM), F32)],
        compiler_params=_cp(1))(proj, proj, proj, out, lse, dmix, bias)


def _ret_consts(lg, forward):
    C = RET_CHUNK
    i = lax.broadcasted_iota(jnp.int32, (C, C), 0)
    j = lax.broadcasted_iota(jnp.int32, (C, C), 1)
    rel = (i - j) if forward else (j - i)
    inside = (rel >= 0) if forward else (rel > 0)
    relf = jnp.maximum(rel, 0).astype(F32)
    mask = jnp.where(inside, jnp.exp(lg * relf), 0.0)
    idx = lax.broadcasted_iota(jnp.int32, (C, 1), 0).astype(F32)
    q_exp = (idx + 1.0) if forward else (C - idx)
    k_exp = (C - 1.0 - idx) if forward else idx
    return mask, relf, jnp.exp(lg * q_exp), q_exp, jnp.exp(lg * k_exp), k_exp, jnp.exp(lg * C)


def _log_decay(dec_ref, h):
    return -jnp.exp(jnp.full((1, 1), dec_ref[0, h], F32))


def _chunk(c):
    return pl.ds(pl.multiple_of(c * RET_CHUNK, RET_CHUNK), RET_CHUNK)


def _ret_fwd(proj, dec_f, dec_b, w_norm):
    S = proj.shape[0]
    H = proj.shape[1] // (N_GROUPS * HEAD_DIM)
    nc = S // RET_CHUNK
    scale = HEAD_DIM ** -0.5

    def body(df_ref, db_ref, q_ref, k_ref, v_ref, g_ref, w_ref, y_ref, o_ref):
        h = pl.program_id(0)
        consts = [_ret_consts(_log_decay(dref, h), fw) for fw, dref in ((True, df_ref), (False, db_ref))]
        o_ref[...] = jnp.zeros_like(o_ref)

        def step(n, states):
            rows = [_chunk(n), _chunk(nc - 1 - n)]
            got = [(q_ref[r, :] * scale, k_ref[r, :], v_ref[r, :], o_ref[r, :]) for r in rows]
            new_o, new_states = [], []
            for (qc, kc, vc, o_old), (mask, _, q_dec, _, k_dec, _, c_dec), state in zip(got, consts, states):
                new_o.append(o_old + _dot(_dot(qc, kc, _NT) * mask, vc, _NN) + _dot(qc * q_dec, state, _NN))
                new_states.append(state * c_dec + _dot(kc * k_dec, vc, _TN))
            for r, o_new in zip(rows, new_o):
                o_ref[r, :] = o_new
            return tuple(new_states)

        lax.fori_loop(0, nc, step, (jnp.zeros((HEAD_DIM, HEAD_DIM), F32),) * 2)
        o = o_ref[...]
        g = g_ref[...]
        y_ref[...] = o * _rms_scale(o) * w_ref[...] * (g * _sigmoid(g))

    hspec = pl.BlockSpec((S, HEAD_DIM), lambda h: (0, h))
    smem = pl.BlockSpec(memory_space=pltpu.SMEM)
    return pl.pallas_call(
        body, grid=(H,), name="ret_fwd",
        in_specs=[smem, smem] + _head_specs(S, (3, 4, 5, 6), H) + [pl.BlockSpec((1, HEAD_DIM), lambda h: (0, h))],
        out_specs=[hspec, hspec],
        out_shape=[SDS((S, H * HEAD_DIM), F32)] * 2,
        compiler_params=_cp(1))(dec_f, dec_b, proj, proj, proj, proj, w_norm)


def _ret_bwd(proj, o_raw, dmix, dec_f, dec_b, w_norm, col0):
    S = proj.shape[0]
    H = proj.shape[1] // (N_GROUPS * HEAD_DIM)
    C = RET_CHUNK
    nc = S // C
    scale = HEAD_DIM ** -0.5

    def body(df_ref, db_ref, q_ref, k_ref, v_ref, g_ref, o_ref, dy_ref, w_ref,
             dq_ref, dk_ref, dv_ref, dg_ref, small_ref, do, states):
        h = pl.program_id(0)
        o = o_ref[...]
        g = g_ref[...]
        dy = dy_ref[...]
        w = w_ref[...]
        rr = _rms_scale(o)
        normed = o * rr
        sg = _sigmoid(g)
        silu = g * sg
        small_ref[0, pl.ds(2, 1), :] = jnp.sum(dy * normed * silu, axis=0, keepdims=True)
        dg_ref[...] = dy * normed * w * (sg * (1.0 + g * (1.0 - sg)))
        dnormed = dy * w * silu
        do[...] = rr * dnormed - o * (rr * rr * rr) * jnp.mean(dnormed * o, axis=-1, keepdims=True)

        lgs = [_log_decay(df_ref, h), _log_decay(db_ref, h)]
        consts = [_ret_consts(lg, fw) for lg, fw in zip(lgs, (True, False))]
        zero_state = jnp.zeros((HEAD_DIM, HEAD_DIM), F32)

        def fwd_step(n, carry):
            new = []
            for way, (cidx, state) in enumerate(zip((n, nc - 1 - n), carry)):
                k_dec, c_dec = consts[way][4], consts[way][6]
                rows = _chunk(cidx)
                states[way, cidx] = state
                new.append(state * c_dec + _dot(k_ref[rows, :] * k_dec, v_ref[rows, :], _TN))
            return tuple(new)

        lax.fori_loop(0, nc, fwd_step, (zero_state, zero_state))
        for ref in (dq_ref, dk_ref, dv_ref):
            ref[...] = jnp.zeros_like(ref)

        def bwd_step(n, carry):
            cidxs = (nc - 1 - n, n)
            rows = [_chunk(c) for c in cidxs]
            got = [(q_ref[r, :] * scale, k_ref[r, :], v_ref[r, :], do[r, :], states[way, c], dq_ref[r, :], dk_ref[r, :],
                    dv_ref[r, :]) for way, (r, c) in enumerate(zip(rows, cidxs))]
            new_rows, new_carry = [], []
            for (qc, kc, vc, doc, state, dq_old, dk_old, dv_old), cs, (d_state, dlam) in zip(got, consts, carry):
                mask, relf, q_dec, q_exp, k_dec, k_exp, c_dec = cs
                a0 = _dot(qc, kc, _NT)
                dp = _dot(doc, vc, _NT) * mask
                gq = _dot(doc, state, _NT)
                gk = _dot(vc, d_state, _NT)
                new_rows.append((dq_old + (_dot(dp, kc, _NN) + q_dec * gq) * scale,
                                 dk_old + _dot(dp, qc, _TN) + k_dec * gk,
                                 dv_old + _dot(a0 * mask, doc, _TN) + _dot(kc * k_dec, d_state, _NN)))
                dlam = dlam + jnp.sum(relf * a0 * dp, axis=0, keepdims=True) \
                    + jnp.sum(q_exp * q_dec * qc * gq + k_exp * k_dec * kc * gk, axis=0, keepdims=True) \
                    + (C * c_dec) * jnp.sum(state * d_state, axis=0, keepdims=True)
                new_carry.append((d_state * c_dec + _dot(qc * q_dec, doc, _TN), dlam))
            for r, (dq_new, dk_new, dv_new) in zip(rows, new_rows):
                dq_ref[r, :] = dq_new
                dk_ref[r, :] = dk_new
                dv_ref[r, :] = dv_new
            return tuple(new_carry)

        zero_carry = (zero_state, jnp.zeros((1, HEAD_DIM), F32))
        done = lax.fori_loop(0, nc, bwd_step, (zero_carry, zero_carry))
        for row, ((_, dlam), lg) in enumerate(zip(done, lgs)):
            small_ref[0, pl.ds(row, 1), :] = jnp.broadcast_to(jnp.sum(dlam, axis=-1, keepdims=True) * lg, (1, HEAD_DIM))
        small_ref[0, pl.ds(3, 5), :] = jnp.zeros((5, HEAD_DIM), F32)

    hspec = pl.BlockSpec((S, HEAD_DIM), lambda h: (0, h))
    smem = pl.BlockSpec(memory_space=pltpu.SMEM)
    nh0 = col0 // HEAD_DIM
    return pl.pallas_call(
        body, grid=(H,), name="ret_bwd",
        in_specs=[smem, smem] + _head_specs(S, (3, 4, 5, 6), H) + [
            hspec, pl.BlockSpec((S, HEAD_DIM), lambda h: (0, nh0 + h)), pl.BlockSpec((1, HEAD_DIM), lambda h: (0, h))],
        out_specs=[hspec, hspec, hspec, hspec, pl.BlockSpec((1, 8, HEAD_DIM), lambda h: (h, 0, 0))],
        out_shape=[SDS((S, H * HEAD_DIM), F32)] * 4 + [SDS((H, 8, HEAD_DIM), F32)],
        scratch_shapes=[pltpu.VMEM((S, HEAD_DIM), F32), pltpu.VMEM((2, nc, HEAD_DIM, HEAD_DIM), F32)],
        compiler_params=_cp(1))(dec_f, dec_b, proj, proj, proj, proj, o_raw, dmix, w_norm)


def _ffn_bwd_act(dh2, wd, g, u):
    S, D = dh2.shape
    nblk, _, FB = g.shape
    tm = min(512, S)

    def body(dh_ref, wd_ref, g_ref, u_ref, dg_ref, du_ref):
        dact = _dot(dh_ref[...], wd_ref[...], _NT)
        gg = g_ref[0]
        sg = _sigmoid(gg)
        dg_ref[0] = (dact * u_ref[0] * (sg * (1.0 + gg * (1.0 - sg)))).astype(BF16)
        du_ref[0] = (dact * (gg * sg)).astype(BF16)

    blk = pl.BlockSpec((1, tm, FB), lambda j, i: (j, i, 0))
    return pl.pallas_call(
        body, grid=(nblk, S // tm), name="ffn_bwd_act",
        in_specs=[pl.BlockSpec((tm, D), lambda j, i: (i, 0)), pl.BlockSpec((FB, D), lambda j, i: (j, 0)), blk, blk],
        out_specs=[blk, blk], out_shape=[SDS((nblk, S, FB), BF16)] * 2,
        compiler_params=_cp(2))(dh2, wd, g, u)


def _ffn_bwd_in(dg, du, wg, wu, h1, dh2, w_norm):
    nblk, S, FB = dg.shape
    D = h1.shape[1]
    tm = min(512, S)

    def body(dg_ref, du_ref, wg_ref, wu_ref, h_hbm, dh2_hbm, wn_ref, dh_ref, dhb_ref, dw_ref, acc, h_buf, dh2_buf, sems):
        i, j = pl.program_id(0), pl.program_id(1)
        rows = pl.ds(pl.multiple_of(i * tm, tm), tm)
        fetch = [pltpu.make_async_copy(h_hbm.at[rows, :], h_buf, sems.at[0]),
                 pltpu.make_async_copy(dh2_hbm.at[rows, :], dh2_buf, sems.at[1])]

        @pl.when(j == 0)
        def _():
            for cp in fetch:
                cp.start()
            acc[...] = jnp.zeros_like(acc)

        @pl.when((i == 0) & (j == 0))
        def _():
            dw_ref[...] = jnp.zeros_like(dw_ref)

        acc[...] += _dot(dg_ref[0], wg_ref[0], _NN) + _dot(du_ref[0], wu_ref[0], _NN)

        @pl.when(j == nblk - 1)
        def _():
            for cp in fetch:
                cp.wait()
            dh, dw = _rms_bwd(acc[...], h_buf[...], wn_ref[...])
            dh = dh2_buf[...] + dh
            dh_ref[...] = dh
            dhb_ref[...] = dh.astype(BF16)
            dw_ref[...] += dw

    blk = pl.BlockSpec((1, tm, FB), lambda i, j: (j, i, 0))
    wspec = pl.BlockSpec((1, FB, D), lambda i, j: (j, 0, 0))
    row = pl.BlockSpec((tm, D), lambda i, j: (i, 0))
    vec = pl.BlockSpec((1, D), lambda i, j: (0, 0))
    return pl.pallas_call(
        body, grid=(S // tm, nblk), name="ffn_bwd_in",
        in_specs=[blk, blk, wspec, wspec, ANY, ANY, vec],
        out_specs=[row, row, vec], out_shape=[SDS((S, D), F32), SDS((S, D), BF16), SDS((1, D), F32)],
        scratch_shapes=[pltpu.VMEM((tm, D), F32), pltpu.VMEM((tm, D), F32), pltpu.VMEM((tm, D), F32),
                        pltpu.SemaphoreType.DMA((2,))],
        compiler_params=_cp(2))(dg, du, wg, wu, h1, dh2, w_norm)


def _dmix(dh1, w_out):
    S, D = dh1.shape
    tm = min(512, S)

    def body(dh_ref, w_ref, o_ref):
        o_ref[...] = _dot(dh_ref[...], w_ref[...], _NT)

    row = pl.BlockSpec((tm, D), lambda i: (i, 0))
    return pl.pallas_call(
        body, grid=(S // tm,), name="dmix", in_specs=[row, pl.BlockSpec((D, D), lambda i: (0, 0))],
        out_specs=row, out_shape=SDS((S, D), F32), compiler_params=_cp(1))(dh1, w_out)


def _in_bwd(dproj, w_blk, x, dh1, w_norm):
    S, D = x.shape
    nblk, _, NB = w_blk.shape
    tm = min(512, S)

    def body(dp_ref, w_ref, x_ref, dh1_ref, wn_ref, dx_ref, dw_ref, acc):
        i, j = pl.program_id(0), pl.program_id(1)

        @pl.when(j == 0)
        def _():
            acc[...] = jnp.zeros_like(acc)

        @pl.when((i == 0) & (j == 0))
        def _():
            dw_ref[...] = jnp.zeros_like(dw_ref)

        acc[...] += _dot(dp_ref[...], w_ref[0], _NT)

        @pl.when(j == nblk - 1)
        def _():
            dh, dw = _rms_bwd(acc[...], x_ref[...], wn_ref[...])
            dx_ref[...] = dh1_ref[...] + dh
            dw_ref[...] += dw

    row = pl.BlockSpec((tm, D), lambda i, j: (i, 0))
    vec = pl.BlockSpec((1, D), lambda i, j: (0, 0))
    return pl.pallas_call(
        body, grid=(S // tm, nblk), name="in_bwd",
        in_specs=[pl.BlockSpec((tm, NB), lambda i, j: (i, j)), pl.BlockSpec((1, D, NB), lambda i, j: (j, 0, 0)),
                  row, row, vec],
        out_specs=[row, vec], out_shape=[SDS((S, D), F32), SDS((1, D), F32)],
        scratch_shapes=[pltpu.VMEM((tm, D), F32)], compiler_params=_cp(2))(dproj, w_blk, x, dh1, w_norm)


def _wgrad(a, b, a_spec, b_spec, o_spec, o_shape, grid, name):
    nk = grid[-1]

    def ld(ref):
        return ref[0] if len(ref.shape) == 3 else ref[...]

    def body(a_ref, b_ref, o_ref, acc):
        k = pl.program_id(len(grid) - 1)

        @pl.when(k == 0)
        def _():
            acc[...] = jnp.zeros_like(acc)

        acc[...] += _dot(ld(a_ref), ld(b_ref), _TN)

        @pl.when(k == nk - 1)
        def _():
            if len(o_ref.shape) == 3:
                o_ref[0] = acc[...].astype(o_ref.dtype)
            else:
                o_ref[...] = acc[...].astype(o_ref.dtype)

    return pl.pallas_call(
        body, grid=grid, name=name, in_specs=[a_spec, b_spec], out_specs=o_spec, out_shape=SDS(o_shape, BF16),
        scratch_shapes=[pltpu.VMEM(o_spec.block_shape[-2:], F32)], compiler_params=_cp(len(grid)))(a, b)


def _peer(k):
    x, y, c = lax.axis_index("x"), lax.axis_index("y"), lax.axis_index("c")
    px = 1 - x if k & 4 else x
    py = 1 - y if k & 2 else y
    pc = 1 - c if k & 1 else c
    return (px, py, pc), 4 * px + 2 * py + pc


def _exchange_copies(srcs, lands, send_sems, recv_sems, which, gather):
    _, me = _peer(0)
    pairs = []
    for pos, a in enumerate(which):
        for k in range(1, N_DEV):
            dev, idx = _peer(k)
            sem = pos * (N_DEV - 1) + k - 1
            src = srcs[a] if gather else srcs[a].at[idx]
            mk = functools.partial(pltpu.make_async_remote_copy, src_ref=src, send_sem=send_sems.at[sem],
                                   recv_sem=recv_sems.at[sem], device_id=dev, device_id_type=MESH)
            pairs.append((mk(dst_ref=lands[a].at[me]), mk(dst_ref=lands[a].at[idx])))
    return pairs


def _sequencer_kernel(name, collective_id, n_remote, n_local):
    return pl.kernel(mesh=plsc.ScalarSubcoreMesh(axis_name="sequencer", num_cores=1), name=name,
                     scratch_types=(pltpu.SemaphoreType.DMA((n_remote,)), pltpu.SemaphoreType.DMA((n_remote,)),
                                    pltpu.SemaphoreType.DMA((n_local,))),
                     compiler_params=pltpu.CompilerParams(collective_id=collective_id))


def _handshake(ks):
    barrier = pltpu.get_barrier_semaphore()
    for k in ks:
        pl.semaphore_signal(barrier, inc=1, device_id=_peer(k)[0], device_id_type=MESH)
    pl.semaphore_wait(barrier, len(ks))


def _sequencer_scatter(arrays, name, collective_id):
    n = len(arrays)
    hbm = pltpu.MemorySpace.HBM
    srcs = [jax.new_ref(a, memory_space=hbm) for a in arrays]
    lands = [jax.empty_ref(SDS(a.shape, a.dtype), memory_space=hbm) for a in arrays]

    @_sequencer_kernel(name, collective_id, n * (N_DEV - 1), n)
    def launch(send_sems, recv_sems, local_sems):
        _handshake(range(1, N_DEV))
        _, me = _peer(0)
        local = [pltpu.make_async_copy(srcs[a].at[me], lands[a].at[me], local_sems.at[a]) for a in range(n)]
        pairs = _exchange_copies(srcs, lands, send_sems, recv_sems, range(n), False)
        for out, _ in pairs:
            out.start()
        for cp in local:
            cp.start()
        for out, arrival in pairs:
            out.wait_send()
            arrival.wait_recv()
        for cp in local:
            cp.wait()

    launch()
    return [r[...] for r in lands]


SIBLING = 1
OTHER_CHIPS = (2, 4, 6)


def _sequencer_gather(arrays, name, collective_id):
    n = len(arrays)
    hbm = pltpu.MemorySpace.HBM
    srcs = [jax.new_ref(a, memory_space=hbm) for a in arrays]
    lands = [jax.empty_ref(SDS((N_DEV,) + a.shape, a.dtype), memory_space=hbm) for a in arrays]

    @_sequencer_kernel(name, collective_id, n * (N_DEV - 1), n)
    def launch(send_sems, recv_sems, local_sems):
        _handshake((SIBLING,) + OTHER_CHIPS)
        _, me = _peer(0)
        sibling, _ = _peer(SIBLING)

        def copy(a, k, src, block, to):
            sem = a * (N_DEV - 1) + k - 1
            return pltpu.make_async_remote_copy(src_ref=src, dst_ref=lands[a].at[block], send_sem=send_sems.at[sem],
                                                recv_sem=recv_sems.at[sem], device_id=to, device_id_type=MESH)

        local = [pltpu.make_async_copy(srcs[a], lands[a].at[me], local_sems.at[a]) for a in range(n)]
        first = [copy(a, k, srcs[a], me, _peer(k)[0]) for a in range(n) for k in OTHER_CHIPS + (SIBLING,)]
        for cp in first + local:
            cp.start()
        passed = []
        for a in range(n):
            for k in OTHER_CHIPS:
                _, block = _peer(k)
                copy(a, k, srcs[a], block, sibling).wait_recv()
                passed.append(copy(a, k ^ SIBLING, lands[a].at[block], block, sibling))
                passed[-1].start()
        for a in range(n):
            for k in (SIBLING,) + tuple(k ^ SIBLING for k in OTHER_CHIPS):
                copy(a, k, srcs[a], _peer(k)[1], sibling).wait_recv()
        for cp in first + passed:
            cp.wait_send()
        for cp in local:
            cp.wait()

    launch()
    return [r[...] for r in lands]


SMALL_ROWS = 64


def _small_step(part, w, m, v):
    def body(p_ref, w_ref, m_ref, v_ref, g_ref, d_ref, nm_ref, nv_ref, gath, send_sems, recv_sems):
        _, me = _peer(0)
        gath[me] = p_ref[...]
        copies = []
        for k in range(1, N_DEV):
            dev, idx = _peer(k)
            out = pltpu.make_async_remote_copy(src_ref=p_ref, dst_ref=gath.at[me], send_sem=send_sems.at[k - 1],
                                               recv_sem=recv_sems.at[k - 1], device_id=dev, device_id_type=MESH)
            out.start()
            arrival = pltpu.make_async_remote_copy(src_ref=p_ref, dst_ref=gath.at[idx], send_sem=send_sems.at[k - 1],
                                                   recv_sem=recv_sems.at[k - 1], device_id=dev, device_id_type=MESH)
            copies.append((out, arrival))
        for out, arrival in copies:
            out.wait_send()
            arrival.wait_recv()
        g = gath[0]
        for p in range(1, N_DEV):
            g = g + gath[p]
        g_ref[...] = g
        d_ref[...], nm_ref[...], nv_ref[...] = _adamw(w_ref[...], g, m_ref[...], v_ref[...])

    vm = pl.BlockSpec(memory_space=pltpu.VMEM)
    return pl.pallas_call(
        body, name="small_step", in_specs=[vm] * 4, out_specs=[vm] * 4,
        out_shape=[SDS((SMALL_ROWS, 128), F32)] * 4,
        scratch_shapes=[pltpu.VMEM((N_DEV, SMALL_ROWS, 128), F32), pltpu.SemaphoreType.DMA((N_DEV - 1,)),
                        pltpu.SemaphoreType.DMA((N_DEV - 1,))])(part, w, m, v)


def _adamw(w, g, m, v):
    m = ADAM_B1 * m + (1.0 - ADAM_B1) * g
    v = ADAM_B2 * v + (1.0 - ADAM_B2) * (g * g)
    m_hat = m / (1.0 - ADAM_B1 ** ADAM_STEP)
    v_hat = v / (1.0 - ADAM_B2 ** ADAM_STEP)
    delta = -ADAM_LR * (m_hat / (jnp.sqrt(v_hat) + ADAM_EPS) + ADAM_WD * w)
    return delta, m, v


def _adamw_block(parts, w, m, v, name):
    R, C = w.shape
    tr = next(t for t in (256, 128, 64, 32, 16, 8) if R % t == 0 and t * C <= 256 * 1024)

    def body(p_ref, w_ref, m_ref, v_ref, g_ref, d_ref, nm_ref, nv_ref):
        g = p_ref[0].astype(F32)
        for p in range(1, N_DEV):
            g = g + p_ref[p].astype(F32)
        g_ref[...] = g
        d_ref[...], nm_ref[...], nv_ref[...] = _adamw(w_ref[...], g, m_ref[...], v_ref[...])

    row = pl.BlockSpec((tr, C), lambda i: (i, 0))
    return pl.pallas_call(
        body, grid=(R // tr,), name=name, in_specs=[pl.BlockSpec((N_DEV, tr, C), lambda i: (0, i, 0)), row, row, row],
        out_specs=[row] * 4, out_shape=[SDS((R, C), F32)] * 4, compiler_params=_cp(1))(parts, w, m, v)


def _pack_small(mix, ffn, fin, retw, dec_f, dec_b, loss):
    flat = jnp.concatenate([mix.reshape(-1), ffn.reshape(-1), fin.reshape(-1), retw.reshape(-1), dec_f.reshape(-1),
                            dec_b.reshape(-1), loss.reshape(-1)])
    return jnp.pad(flat, (0, SMALL_ROWS * 128 - flat.shape[0])).reshape(SMALL_ROWS, 128)


def _unpack_small(packed, shapes):
    flat = packed.reshape(-1)
    out, at = [], 0
    for s in shapes:
        n = math.prod(s)
        out.append(flat[at:at + n].reshape(s))
        at += n
    return out


def kernel(x, norm_mix_w, w_in, ret_decay_fwd, ret_decay_bwd, ret_norm_w, w_out, norm_ffn_w, w_gate, w_up, w_down, norm_final_w, loss_target, m_norm_mix_w, m_w_in, m_ret_decay_fwd, m_ret_decay_bwd, m_ret_norm_w, m_w_out, m_norm_ffn_w, m_w_gate, m_w_up, m_w_down, m_norm_final_w, v_norm_mix_w, v_w_in, v_ret_decay_fwd, v_ret_decay_bwd, v_ret_norm_w, v_w_out, v_norm_ffn_w, v_w_gate, v_w_up, v_w_down, v_norm_final_w):
    x2 = x[0]
    tgt = loss_target[0]
    S, D = x2.shape
    H = ret_norm_w.shape[1] // HEAD_DIM
    DA = H * HEAD_DIM
    fin_w = norm_final_w.reshape(1, D)
    big = (w_in[0], w_out[0], w_gate[0].T, w_up[0].T, w_down[0])

    big_b = [w.astype(BF16) for w in big]
    wi, = _sequencer_gather(big_b[:1], "gather_in", 0)
    wo, wg, wu = _sequencer_gather(big_b[1:4], "gather_mid", 1)
    wd, = _sequencer_gather(big_b[4:], "gather_down", 5)
    NB = wi.shape[2]

    proj, n1 = _proj_fwd(x2, norm_mix_w, wi)
    bias = _attn_bias()[:H]
    attn, lse = _attn_fwd(proj, bias)
    ret, o_raw = _ret_fwd(proj, ret_decay_fwd, ret_decay_bwd, ret_norm_w)
    wo_full = wo.reshape(D, D)
    FB = wd.shape[1]
    wd_full = wd.reshape(N_DEV * FB, D)
    h1, mixed, n2 = _out_fwd(x2, attn, ret, wo_full, norm_ffn_w)
    gate, up, act = _ffn_up(n2, wg, wu)
    dh2, dh2_b, loss_parts, g_fin = _ffn_down_loss(act, wd_full, h1, tgt, fin_w)

    dgate, dup = _ffn_bwd_act(dh2_b, wd_full, gate, up)
    tn = min(1024, D)
    ffn_specs = (pl.BlockSpec((1, S, FB), lambda j, n, k: (j, 0, 0)), pl.BlockSpec((S, tn), lambda j, n, k: (0, n)),
                 pl.BlockSpec((1, FB, tn), lambda j, n, k: (j, 0, n)), (N_DEV, FB, D), (N_DEV, D // tn, 1))
    g_wd = _wgrad(act, dh2_b, *ffn_specs, "wgrad_down")
    g_wg = _wgrad(dgate, n2, *ffn_specs, "wgrad_gate")
    g_wu = _wgrad(dup, n2, *ffn_specs, "wgrad_up")
    parts_f = _sequencer_scatter([g_wg, g_wu, g_wd], "scatter_ffn", 2)
    dh1, dh1_b, g_ffn = _ffn_bwd_in(dgate, dup, wg, wu, h1, dh2, norm_ffn_w)
    dmix = _dmix(dh1_b, wo_full)
    tmw = min(512, D)
    tk = min(2048, S)
    g_wo = _wgrad(mixed, dh1_b, pl.BlockSpec((tk, tmw), lambda m, k: (k, m)), pl.BlockSpec((tk, D), lambda m, k: (k, 0)),
                  pl.BlockSpec((tmw, D), lambda m, k: (m, 0)), (D, D), (D // tmw, S // tk), "wgrad_out")
    parts_o = _sequencer_scatter([g_wo.reshape(N_DEV, D // N_DEV, D)], "scatter_out", 3)
    dq_r, dk_r, dv_r, dg_r, small = _ret_bwd(proj, o_raw, dmix, ret_decay_fwd, ret_decay_bwd, ret_norm_w, DA)
    dq_a, dk_a, dv_a = _attn_bwd(proj, attn, lse, dmix, bias)
    dproj = jnp.concatenate([t.astype(BF16) for t in (dq_a, dk_a, dv_a, dq_r, dk_r, dv_r, dg_r)], axis=1)
    g_wi = _wgrad(n1, dproj, pl.BlockSpec((S, tmw), lambda j, m, k: (0, m)), pl.BlockSpec((S, NB), lambda j, m, k: (0, j)),
                  pl.BlockSpec((1, tmw, NB), lambda j, m, k: (j, m, 0)), (N_DEV, D, NB), (N_DEV, D // tmw, 1), "wgrad_in")
    parts_i = _sequencer_scatter([g_wi], "scatter_in", 4)
    grad_x, g_mix = _in_bwd(dproj, wi, x2, dh1, norm_mix_w)

    big_m = (m_w_in[0], m_w_out[0], m_w_gate[0].T, m_w_up[0].T, m_w_down[0])
    big_v = (v_w_in[0], v_w_out[0], v_w_gate[0].T, v_w_up[0].T, v_w_down[0])
    names = ("adamw_in", "adamw_out", "adamw_gate", "adamw_up", "adamw_down")
    upd = [None] * 5
    for a, p in zip((2, 3, 4, 1, 0), parts_f + parts_o + parts_i):
        upd[a] = _adamw_block(p, big[a], big_m[a], big_v[a], names[a])

    g_dec_f = small[:, 0, 0].reshape(1, H)
    g_dec_b = small[:, 1, 0].reshape(1, H)
    g_retw = small[:, 2, :].reshape(1, DA)
    loss_local = jnp.sum(loss_parts[::8, 0])
    zero = jnp.zeros((1,), F32)
    part = _pack_small(g_mix, g_ffn, g_fin, g_retw, g_dec_f, g_dec_b, loss_local)
    sw = _pack_small(norm_mix_w, norm_ffn_w, norm_final_w, ret_norm_w, ret_decay_fwd, ret_decay_bwd, zero)
    sm = _pack_small(m_norm_mix_w, m_norm_ffn_w, m_norm_final_w, m_ret_norm_w, m_ret_decay_fwd, m_ret_decay_bwd, zero)
    sv = _pack_small(v_norm_mix_w, v_norm_ffn_w, v_norm_final_w, v_ret_norm_w, v_ret_decay_fwd, v_ret_decay_bwd, zero)
    shapes = [(1, D), (1, D), (D,), (1, DA), (1, H), (1, H), ()]
    sg, sd, snm, snv = [_unpack_small(t, shapes) for t in _small_step(part, sw, sm, sv)]
    loss = sg[6]

    def ordered(small_set, k):
        b = [(u[k].T if a in (2, 3) else u[k])[None] for a, u in enumerate(upd)]
        return [small_set[0], b[0], small_set[4], small_set[5], small_set[3], b[1], small_set[1], b[2], b[3], b[4],
                small_set[2]]

    return (loss, grad_x[None], *ordered(sg, 0), *ordered(sd, 1), *ordered(snm, 2), *ordered(snv, 3))
```

```python
import functools
import math

import numpy as np
import jax
import jax.numpy as jnp
from jax import lax
from jax.experimental import pallas as pl
from jax.experimental.pallas import tpu as pltpu
from jax.experimental.pallas import tpu_sc as plsc

F32 = jnp.float32
BF16 = jnp.bfloat16
SDS = jax.ShapeDtypeStruct

HEAD_DIM = 128
EPS = 1e-6
RET_CHUNK = 128
DILATIONS = (1, 4, 16)
BAND = 64
Q_TILE = 128
K_TILE = Q_TILE + 2 * BAND
KV_PAD = BAND * max(DILATIONS)
TILE_GROUP = 8
NEG = -1e30
N_DEV = 8
N_GROUPS = 7
ADAM_LR, ADAM_B1, ADAM_B2, ADAM_EPS, ADAM_WD, ADAM_STEP = 0.001, 0.9, 0.999, 1e-08, 0.01, 10
VMEM_LIMIT = 56 * 1024 * 1024
MESH = pl.DeviceIdType.MESH
ANY = pl.BlockSpec(memory_space=pl.ANY)


def _cp(n_grid):
    return pltpu.CompilerParams(dimension_semantics=("arbitrary",) * n_grid, vmem_limit_bytes=VMEM_LIMIT)


def _sigmoid(x):
    return 1.0 / (1.0 + jnp.exp(-x))


def _rms_scale(h):
    return lax.rsqrt(jnp.mean(h * h, axis=-1, keepdims=True) + EPS)


def _rms_bwd(dn, h, w):
    r = _rms_scale(h)
    gw = dn * w
    dh = r * gw - h * (r * r * r) * jnp.mean(gw * h, axis=-1, keepdims=True)
    return dh, jnp.sum(dn * h * r, axis=0, keepdims=True)


def _dot(a, b, dims):
    return lax.dot_general(a.astype(BF16), b.astype(BF16), (dims, ((), ())), preferred_element_type=F32)


_NN = ((1,), (0,))
_NT = ((1,), (1,))
_TN = ((0,), (0,))


def _proj_fwd(x, w_norm, w_blk):
    S, D = x.shape
    nblk, _, NB = w_blk.shape
    tm = min(1024, S)

    def body(x_ref, wn_ref, w_ref, proj_ref, n_ref, n_scr):
        @pl.when(pl.program_id(1) == 0)
        def _():
            xf = x_ref[...]
            nb = (xf * _rms_scale(xf) * wn_ref[...]).astype(BF16)
            n_scr[...] = nb
            n_ref[...] = nb
        proj_ref[...] = jnp.dot(n_scr[...], w_ref[0], preferred_element_type=F32)

    return pl.pallas_call(
        body, grid=(S // tm, nblk), name="proj_fwd",
        in_specs=[pl.BlockSpec((tm, D), lambda i, j: (i, 0)), pl.BlockSpec((1, D), lambda i, j: (0, 0)),
                  pl.BlockSpec((1, D, NB), lambda i, j: (j, 0, 0))],
        out_specs=[pl.BlockSpec((tm, NB), lambda i, j: (i, j)), pl.BlockSpec((tm, D), lambda i, j: (i, 0))],
        out_shape=[SDS((S, nblk * NB), F32), SDS((S, D), BF16)],
        scratch_shapes=[pltpu.VMEM((tm, D), BF16)], compiler_params=_cp(2))(x, w_norm, w_blk)


def _out_fwd(x, attn, ret, w_out, w_norm):
    S, D = x.shape
    DA = attn.shape[1]
    tm = min(256, S)

    def body(x_ref, a_ref, r_ref, w_ref, wn_ref, h_ref, mix_ref, n_ref):
        a = a_ref[...].astype(BF16)
        r = r_ref[...].astype(BF16)
        mix_ref[:, :DA] = a
        mix_ref[:, DA:] = r
        h = x_ref[...] + jnp.dot(a, w_ref[:DA, :], preferred_element_type=F32) \
            + jnp.dot(r, w_ref[DA:, :], preferred_element_type=F32)
        h_ref[...] = h
        n_ref[...] = (h * _rms_scale(h) * wn_ref[...]).astype(BF16)

    row = lambda w: pl.BlockSpec((tm, w), lambda i: (i, 0))
    return pl.pallas_call(
        body, grid=(S // tm,), name="out_fwd",
        in_specs=[row(D), row(DA), row(D - DA), pl.BlockSpec((D, D), lambda i: (0, 0)),
                  pl.BlockSpec((1, D), lambda i: (0, 0))],
        out_specs=[row(D), row(D), row(D)],
        out_shape=[SDS((S, D), F32), SDS((S, D), BF16), SDS((S, D), BF16)],
        compiler_params=_cp(1))(x, attn, ret, w_out, w_norm)


def _ffn_up(n2, wg, wu):
    S, D = n2.shape
    nblk, FB, _ = wg.shape
    tm = min(512, S)

    def body(n_ref, wg_ref, wu_ref, g_ref, u_ref, a_ref):
        n = n_ref[...]
        g = _dot(n, wg_ref[0], _NT)
        u = _dot(n, wu_ref[0], _NT)
        g_ref[0] = g
        u_ref[0] = u
        a_ref[0] = (g * _sigmoid(g) * u).astype(BF16)

    wspec = pl.BlockSpec((1, FB, D), lambda j, i: (j, 0, 0))
    ospec = pl.BlockSpec((1, tm, FB), lambda j, i: (j, i, 0))
    return pl.pallas_call(
        body, grid=(nblk, S // tm), name="ffn_up",
        in_specs=[pl.BlockSpec((tm, D), lambda j, i: (i, 0)), wspec, wspec],
        out_specs=[ospec, ospec, ospec],
        out_shape=[SDS((nblk, S, FB), F32), SDS((nblk, S, FB), F32), SDS((nblk, S, FB), BF16)],
        compiler_params=_cp(2))(n2, wg, wu)


def _ffn_down_loss(act, wd, h1, target, w_norm):
    nblk, S, FB = act.shape
    D = h1.shape[1]
    tm = min(512, S)

    def body(a_ref, wd_ref, h_ref, t_ref, wn_ref, dh_ref, dhb_ref, loss_ref, dw_ref, acc):
        i, j = pl.program_id(0), pl.program_id(1)

        @pl.when(j == 0)
        def _():
            acc[...] = h_ref[...]

        @pl.when((i == 0) & (j == 0))
        def _():
            dw_ref[...] = jnp.zeros_like(dw_ref)

        acc[...] += jnp.dot(a_ref[0], wd_ref[...], preferred_element_type=F32)

        @pl.when(j == nblk - 1)
        def _():
            h = acc[...]
            w = wn_ref[...]
            err = h * _rms_scale(h) * w - t_ref[...]
            loss_ref[...] = jnp.full(loss_ref.shape, 0.5 * jnp.sum(err * err) / D, F32)
            dh, dw = _rms_bwd(err * (1.0 / D), h, w)
            dh_ref[...] = dh
            dhb_ref[...] = dh.astype(BF16)
            dw_ref[...] += dw

    row = pl.BlockSpec((tm, D), lambda i, j: (i, 0))
    vec = pl.BlockSpec((1, D), lambda i, j: (0, 0))
    return pl.pallas_call(
        body, grid=(S // tm, nblk), name="ffn_down_loss",
        in_specs=[pl.BlockSpec((1, tm, FB), lambda i, j: (j, i, 0)), pl.BlockSpec((FB, D), lambda i, j: (j, 0)),
                  row, row, vec],
        out_specs=[row, row, pl.BlockSpec((8, 128), lambda i, j: (i, 0)), vec],
        out_shape=[SDS((S, D), F32), SDS((S, D), BF16), SDS((S // tm * 8, 128), F32), SDS((1, D), F32)],
        scratch_shapes=[pltpu.VMEM((tm, D), F32)], compiler_params=_cp(2))(act, wd, h1, target, w_norm)


def _attn_bias():
    n_heads = 8
    slopes = np.exp2(-8.0 * np.arange(1, n_heads + 1, dtype=np.float32) / n_heads)
    dist = np.abs(np.arange(K_TILE)[None, :] - BAND - np.arange(Q_TILE)[:, None])
    out = np.empty((n_heads, len(DILATIONS), Q_TILE, K_TILE), np.float32)
    for h in range(n_heads):
        for p, d in enumerate(DILATIONS):
            out[h, p] = np.where(dist <= BAND, -slopes[h] * (d * dist).astype(np.float32), NEG)
    return jnp.asarray(out)


def _attn_tiles(S, d):
    L = S // d
    per_class = L // Q_TILE
    return L, per_class, d * per_class


def _tile_rows(t, d, per_class):
    r = t // per_class
    a = (t % per_class) * Q_TILE
    q_rows = pl.ds(r + d * a, Q_TILE, stride=d) if d > 1 else pl.ds(pl.multiple_of(a, Q_TILE), Q_TILE)
    k_rows = pl.ds(KV_PAD + r + d * (a - BAND), K_TILE, stride=d) if d > 1 else pl.ds(
        pl.multiple_of(KV_PAD + a - BAND, BAND), K_TILE)
    return a, q_rows, k_rows


def _lanes(x, width):
    return jnp.concatenate([x] * (width // HEAD_DIM), axis=-1)


_BNT = (((2,), (2,)), ((0,), (0,)))
_BNN = (((2,), (1,)), ((0,), (0,)))
_BTN = (((1,), (1,)), ((0,), (0,)))


def _bdot(a, b, dims):
    return lax.dot_general(a, b, dims, preferred_element_type=F32)


def _stacked(rows, loaders):
    return [jnp.stack([f(*r) for r in rows]) for f in loaders]


def _edge_mask(a, L):
    lk = lax.broadcasted_iota(jnp.int32, (1, K_TILE), 1) + (a - BAND)
    return jnp.where((lk >= 0) & (lk < L), 0.0, NEG).astype(F32)


def _fill_padded(dst, src, S):
    dst[pl.ds(0, KV_PAD), :] = jnp.zeros((KV_PAD, HEAD_DIM), F32)
    dst[pl.ds(KV_PAD + S, KV_PAD), :] = jnp.zeros((KV_PAD, HEAD_DIM), F32)
    dst[pl.ds(KV_PAD, S), :] = src[...]


def _head_specs(S, groups, n_heads):
    return [pl.BlockSpec((S, HEAD_DIM), functools.partial(lambda h, g: (0, g * n_heads + h), g=g)) for g in groups]


def _attn_fwd(proj, bias):
    S = proj.shape[0]
    H = proj.shape[1] // (N_GROUPS * HEAD_DIM)
    scale = HEAD_DIM ** -0.5

    def body(q_ref, k_ref, v_ref, b_ref, o_ref, lse_ref, kp, vp, m_run, l_run):
        _fill_padded(kp, k_ref, S)
        _fill_padded(vp, v_ref, S)
        o_ref[...] = jnp.zeros_like(o_ref)
        m_run[...] = jnp.full(m_run.shape, NEG, F32)
        l_run[...] = jnp.zeros_like(l_run)
        for p, d in enumerate(DILATIONS):
            L, per_class, n_tiles = _attn_tiles(S, d)

            def tiles(t, carry, p=p, d=d, L=L, per_class=per_class, n_tiles=n_tiles):
                rows = [_tile_rows(t + u * (n_tiles // TILE_GROUP), d, per_class) for u in range(TILE_GROUP)]
                qs, ks, vs, m_old, l_old, o_old, edge = _stacked(rows, (
                    lambda a, qr, kr: q_ref[qr, :].astype(BF16), lambda a, qr, kr: kp[kr, :].astype(BF16),
                    lambda a, qr, kr: vp[kr, :].astype(BF16), lambda a, qr, kr: m_run[qr, :],
                    lambda a, qr, kr: l_run[qr, :], lambda a, qr, kr: o_ref[qr, :], lambda a, qr, kr: _edge_mask(a, L)))
                s = _bdot(qs, ks, _BNT) * scale + b_ref[0, p][None] + edge
                m_new = jnp.maximum(m_old, jnp.max(s, axis=-1, keepdims=True))
                pr = jnp.exp(s - _lanes(m_new, K_TILE)).astype(BF16)
                alpha = jnp.exp(m_old - m_new)
                l_new = alpha * l_old + _bdot(pr, jnp.ones((TILE_GROUP, K_TILE, HEAD_DIM), BF16), _BNN)
                o_new = alpha * o_old + _bdot(pr, vs, _BNN)
                for u, (_, qr, _) in enumerate(rows):
                    o_ref[qr, :] = o_new[u]
                    m_run[qr, :] = m_new[u]
                    l_run[qr, :] = l_new[u]
                return carry

            lax.fori_loop(0, n_tiles // TILE_GROUP, tiles, 0)
        l = l_run[...]
        o_ref[...] = o_ref[...] / l
        lse_ref[...] = m_run[...] + jnp.log(l)

    hspec = pl.BlockSpec((S, HEAD_DIM), lambda h: (0, h))
    return pl.pallas_call(
        body, grid=(H,), name="attn_fwd",
        in_specs=_head_specs(S, (0, 1, 2), H) + [
            pl.BlockSpec((1, len(DILATIONS), Q_TILE, K_TILE), lambda h: (h, 0, 0, 0))],
        out_specs=[hspec, hspec],
        out_shape=[SDS((S, H * HEAD_DIM), F32), SDS((S, H * HEAD_DIM), F32)],
        scratch_shapes=[pltpu.VMEM((S + 2 * KV_PAD, HEAD_DIM), F32), pltpu.VMEM((S + 2 * KV_PAD, HEAD_DIM), F32),
                        pltpu.VMEM((S, HEAD_DIM), F32), pltpu.VMEM((S, HEAD_DIM), F32)],
        compiler_params=_cp(1))(proj, proj, proj, bias)


def _attn_bwd(proj, out, lse, dmix, bias):
    S = proj.shape[0]
    H = proj.shape[1] // (N_GROUPS * HEAD_DIM)
    scale = HEAD_DIM ** -0.5

    def body(q_ref, k_ref, v_ref, o_ref, lse_ref, do_ref, b_ref, dq_ref, dk_ref, dv_ref, kp, vp, dkp, dvp, dsum):
        _fill_padded(kp, k_ref, S)
        _fill_padded(vp, v_ref, S)
        dkp[...] = jnp.zeros_like(dkp)
        dvp[...] = jnp.zeros_like(dvp)
        dq_ref[...] = jnp.zeros_like(dq_ref)
        dsum[...] = jnp.broadcast_to(jnp.sum(do_ref[...] * o_ref[...], axis=-1, keepdims=True), dsum.shape)
        for p, d in enumerate(DILATIONS):
            L, per_class, n_tiles = _attn_tiles(S, d)

            def tiles(t, carry, p=p, d=d, L=L, per_class=per_class, n_tiles=n_tiles):
                rows = [_tile_rows(t + u * (n_tiles // TILE_GROUP), d, per_class) for u in range(TILE_GROUP)]
                qs, ks, vs, dos, lses, dsums, dq_old, dk_old, dv_old, edge = _stacked(rows, (
                    lambda a, qr, kr: q_ref[qr, :].astype(BF16), lambda a, qr, kr: kp[kr, :].astype(BF16),
                    lambda a, qr, kr: vp[kr, :].astype(BF16), lambda a, qr, kr: do_ref[qr, :].astype(BF16),
                    lambda a, qr, kr: lse_ref[qr, :], lambda a, qr, kr: dsum[qr, :], lambda a, qr, kr: dq_ref[qr, :],
                    lambda a, qr, kr: dkp[kr, :], lambda a, qr, kr: dvp[kr, :], lambda a, qr, kr: _edge_mask(a, L)))
                s = _bdot(qs, ks, _BNT) * scale + b_ref[0, p][None] + edge
                pr = jnp.exp(s - _lanes(lses, K_TILE))
                ds = (pr * (_bdot(dos, vs, _BNT) - _lanes(dsums, K_TILE)) * scale).astype(BF16)
                dq_new = dq_old + _bdot(ds, ks, _BNN)
                dk_new = dk_old + _bdot(ds, qs, _BTN)
                dv_new = dv_old + _bdot(pr.astype(BF16), dos, _BTN)
                for u, (_, qr, kr) in enumerate(rows):
                    dq_ref[qr, :] = dq_new[u]
                    dkp[kr, :] = dk_new[u]
                    dvp[kr, :] = dv_new[u]
                return carry

            lax.fori_loop(0, n_tiles // TILE_GROUP, tiles, 0)
        dk_ref[...] = dkp[pl.ds(KV_PAD, S), :]
        dv_ref[...] = dvp[pl.ds(KV_PAD, S), :]

    hspec = pl.BlockSpec((S, HEAD_DIM), lambda h: (0, h))
    padded = pltpu.VMEM((S + 2 * KV_PAD, HEAD_DIM), F32)
    return pl.pallas_call(
        body, grid=(H,), name="attn_bwd",
        in_specs=_head_specs(S, (0, 1, 2), H) + [hspec, hspec, hspec,
                                                  pl.BlockSpec((1, len(DILATIONS), Q_TILE, K_TILE), lambda h: (h, 0, 0, 0))],
        out_specs=[hspec, hspec, hspec],
        out_shape=[SDS((S, H * HEAD_DIM), F32)] * 3,
        scratch_shapes=[padded, padded, padded, padded, pltpu.VMEM((S, HEAD_DIM), F32)],
        compiler_params=_cp(1))(proj, proj, proj, out, lse, dmix, bias)


def _ret_consts(lg, forward):
    C = RET_CHUNK
    i = lax.broadcasted_iota(jnp.int32, (C, C), 0)
    j = lax.broadcasted_iota(jnp.int32, (C, C), 1)
    rel = (i - j) if forward else (j - i)
    inside = (rel >= 0) if forward else (rel > 0)
    relf = jnp.maximum(rel, 0).astype(F32)
    mask = jnp.where(inside, jnp.exp(lg * relf), 0.0)
    idx = lax.broadcasted_iota(jnp.int32, (C, 1), 0).astype(F32)
    q_exp = (idx + 1.0) if forward else (C - idx)
    k_exp = (C - 1.0 - idx) if forward else idx
    return mask, relf, jnp.exp(lg * q_exp), q_exp, jnp.exp(lg * k_exp), k_exp, jnp.exp(lg * C)


def _log_decay(dec_ref, h):
    return -jnp.exp(jnp.full((1, 1), dec_ref[0, h], F32))


def _chunk(c):
    return pl.ds(pl.multiple_of(c * RET_CHUNK, RET_CHUNK), RET_CHUNK)


def _ret_fwd(proj, dec_f, dec_b, w_norm):
    S = proj.shape[0]
    H = proj.shape[1] // (N_GROUPS * HEAD_DIM)
    nc = S // RET_CHUNK
    scale = HEAD_DIM ** -0.5

    def body(df_ref, db_ref, q_ref, k_ref, v_ref, g_ref, w_ref, y_ref, o_ref):
        h = pl.program_id(0)
        consts = [_ret_consts(_log_decay(dref, h), fw) for fw, dref in ((True, df_ref), (False, db_ref))]
        o_ref[...] = jnp.zeros_like(o_ref)

        def step(n, states):
            rows = [_chunk(n), _chunk(nc - 1 - n)]
            got = [(q_ref[r, :] * scale, k_ref[r, :], v_ref[r, :], o_ref[r, :]) for r in rows]
            new_o, new_states = [], []
            for (qc, kc, vc, o_old), (mask, _, q_dec, _, k_dec, _, c_dec), state in zip(got, consts, states):
                new_o.append(o_old + _dot(_dot(qc, kc, _NT) * mask, vc, _NN) + _dot(qc * q_dec, state, _NN))
                new_states.append(state * c_dec + _dot(kc * k_dec, vc, _TN))
            for r, o_new in zip(rows, new_o):
                o_ref[r, :] = o_new
            return tuple(new_states)

        lax.fori_loop(0, nc, step, (jnp.zeros((HEAD_DIM, HEAD_DIM), F32),) * 2)
        o = o_ref[...]
        g = g_ref[...]
        y_ref[...] = o * _rms_scale(o) * w_ref[...] * (g * _sigmoid(g))

    hspec = pl.BlockSpec((S, HEAD_DIM), lambda h: (0, h))
    smem = pl.BlockSpec(memory_space=pltpu.SMEM)
    return pl.pallas_call(
        body, grid=(H,), name="ret_fwd",
        in_specs=[smem, smem] + _head_specs(S, (3, 4, 5, 6), H) + [pl.BlockSpec((1, HEAD_DIM), lambda h: (0, h))],
        out_specs=[hspec, hspec],
        out_shape=[SDS((S, H * HEAD_DIM), F32)] * 2,
        compiler_params=_cp(1))(dec_f, dec_b, proj, proj, proj, proj, w_norm)


def _ret_bwd(proj, o_raw, dmix, dec_f, dec_b, w_norm, col0):
    S = proj.shape[0]
    H = proj.shape[1] // (N_GROUPS * HEAD_DIM)
    C = RET_CHUNK
    nc = S // C
    scale = HEAD_DIM ** -0.5

    def body(df_ref, db_ref, q_ref, k_ref, v_ref, g_ref, o_ref, dy_ref, w_ref,
             dq_ref, dk_ref, dv_ref, dg_ref, small_ref, do, states):
        h = pl.program_id(0)
        o = o_ref[...]
        g = g_ref[...]
        dy = dy_ref[...]
        w = w_ref[...]
        rr = _rms_scale(o)
        normed = o * rr
        sg = _sigmoid(g)
        silu = g * sg
        small_ref[0, pl.ds(2, 1), :] = jnp.sum(dy * normed * silu, axis=0, keepdims=True)
        dg_ref[...] = dy * normed * w * (sg * (1.0 + g * (1.0 - sg)))
        dnormed = dy * w * silu
        do[...] = rr * dnormed - o * (rr * rr * rr) * jnp.mean(dnormed * o, axis=-1, keepdims=True)

        lgs = [_log_decay(df_ref, h), _log_decay(db_ref, h)]
        consts = [_ret_consts(lg, fw) for lg, fw in zip(lgs, (True, False))]
        zero_state = jnp.zeros((HEAD_DIM, HEAD_DIM), F32)

        def fwd_step(n, carry):
            new = []
            for way, (cidx, state) in enumerate(zip((n, nc - 1 - n), carry)):
                k_dec, c_dec = consts[way][4], consts[way][6]
                rows = _chunk(cidx)
                states[way, cidx] = state
                new.append(state * c_dec + _dot(k_ref[rows, :] * k_dec, v_ref[rows, :], _TN))
            return tuple(new)

        lax.fori_loop(0, nc, fwd_step, (zero_state, zero_state))
        for ref in (dq_ref, dk_ref, dv_ref):
            ref[...] = jnp.zeros_like(ref)

        def bwd_step(n, carry):
            cidxs = (nc - 1 - n, n)
            rows = [_chunk(c) for c in cidxs]
            got = [(q_ref[r, :] * scale, k_ref[r, :], v_ref[r, :], do[r, :], states[way, c], dq_ref[r, :], dk_ref[r, :],
                    dv_ref[r, :]) for way, (r, c) in enumerate(zip(rows, cidxs))]
            new_rows, new_carry = [], []
            for (qc, kc, vc, doc, state, dq_old, dk_old, dv_old), cs, (d_state, dlam) in zip(got, consts, carry):
                mask, relf, q_dec, q_exp, k_dec, k_exp, c_dec = cs
                a0 = _dot(qc, kc, _NT)
                dp = _dot(doc, vc, _NT) * mask
                gq = _dot(doc, state, _NT)
                gk = _dot(vc, d_state, _NT)
                new_rows.append((dq_old + (_dot(dp, kc, _NN) + q_dec * gq) * scale,
                                 dk_old + _dot(dp, qc, _TN) + k_dec * gk,
                                 dv_old + _dot(a0 * mask, doc, _TN) + _dot(kc * k_dec, d_state, _NN)))
                dlam = dlam + jnp.sum(relf * a0 * dp, axis=0, keepdims=True) \
                    + jnp.sum(q_exp * q_dec * qc * gq + k_exp * k_dec * kc * gk, axis=0, keepdims=True) \
                    + (C * c_dec) * jnp.sum(state * d_state, axis=0, keepdims=True)
                new_carry.append((d_state * c_dec + _dot(qc * q_dec, doc, _TN), dlam))
            for r, (dq_new, dk_new, dv_new) in zip(rows, new_rows):
                dq_ref[r, :] = dq_new
                dk_ref[r, :] = dk_new
                dv_ref[r, :] = dv_new
            return tuple(new_carry)

        zero_carry = (zero_state, jnp.zeros((1, HEAD_DIM), F32))
        done = lax.fori_loop(0, nc, bwd_step, (zero_carry, zero_carry))
        for row, ((_, dlam), lg) in enumerate(zip(done, lgs)):
            small_ref[0, pl.ds(row, 1), :] = jnp.broadcast_to(jnp.sum(dlam, axis=-1, keepdims=True) * lg, (1, HEAD_DIM))
        small_ref[0, pl.ds(3, 5), :] = jnp.zeros((5, HEAD_DIM), F32)

    hspec = pl.BlockSpec((S, HEAD_DIM), lambda h: (0, h))
    smem = pl.BlockSpec(memory_space=pltpu.SMEM)
    nh0 = col0 // HEAD_DIM
    return pl.pallas_call(
        body, grid=(H,), name="ret_bwd",
        in_specs=[smem, smem] + _head_specs(S, (3, 4, 5, 6), H) + [
            hspec, pl.BlockSpec((S, HEAD_DIM), lambda h: (0, nh0 + h)), pl.BlockSpec((1, HEAD_DIM), lambda h: (0, h))],
        out_specs=[hspec, hspec, hspec, hspec, pl.BlockSpec((1, 8, HEAD_DIM), lambda h: (h, 0, 0))],
        out_shape=[SDS((S, H * HEAD_DIM), F32)] * 4 + [SDS((H, 8, HEAD_DIM), F32)],
        scratch_shapes=[pltpu.VMEM((S, HEAD_DIM), F32), pltpu.VMEM((2, nc, HEAD_DIM, HEAD_DIM), F32)],
        compiler_params=_cp(1))(dec_f, dec_b, proj, proj, proj, proj, o_raw, dmix, w_norm)


def _ffn_bwd_act(dh2, wd, g, u):
    S, D = dh2.shape
    nblk, _, FB = g.shape
    tm = min(512, S)

    def body(dh_ref, wd_ref, g_ref, u_ref, dg_ref, du_ref):
        dact = _dot(dh_ref[...], wd_ref[...], _NT)
        gg = g_ref[0]
        sg = _sigmoid(gg)
        dg_ref[0] = (dact * u_ref[0] * (sg * (1.0 + gg * (1.0 - sg)))).astype(BF16)
        du_ref[0] = (dact * (gg * sg)).astype(BF16)

    blk = pl.BlockSpec((1, tm, FB), lambda j, i: (j, i, 0))
    return pl.pallas_call(
        body, grid=(nblk, S // tm), name="ffn_bwd_act",
        in_specs=[pl.BlockSpec((tm, D), lambda j, i: (i, 0)), pl.BlockSpec((FB, D), lambda j, i: (j, 0)), blk, blk],
        out_specs=[blk, blk], out_shape=[SDS((nblk, S, FB), BF16)] * 2,
        compiler_params=_cp(2))(dh2, wd, g, u)


def _ffn_bwd_in(dg, du, wg, wu, h1, dh2, w_norm):
    nblk, S, FB = dg.shape
    D = h1.shape[1]
    tm = min(512, S)

    def body(dg_ref, du_ref, wg_ref, wu_ref, h_hbm, dh2_hbm, wn_ref, dh_ref, dhb_ref, dw_ref, acc, h_buf, dh2_buf, sems):
        i, j = pl.program_id(0), pl.program_id(1)
        rows = pl.ds(pl.multiple_of(i * tm, tm), tm)
        fetch = [pltpu.make_async_copy(h_hbm.at[rows, :], h_buf, sems.at[0]),
                 pltpu.make_async_copy(dh2_hbm.at[rows, :], dh2_buf, sems.at[1])]

        @pl.when(j == 0)
        def _():
            for cp in fetch:
                cp.start()
            acc[...] = jnp.zeros_like(acc)

        @pl.when((i == 0) & (j == 0))
        def _():
            dw_ref[...] = jnp.zeros_like(dw_ref)

        acc[...] += _dot(dg_ref[0], wg_ref[0], _NN) + _dot(du_ref[0], wu_ref[0], _NN)

        @pl.when(j == nblk - 1)
        def _():
            for cp in fetch:
                cp.wait()
            dh, dw = _rms_bwd(acc[...], h_buf[...], wn_ref[...])
            dh = dh2_buf[...] + dh
            dh_ref[...] = dh
            dhb_ref[...] = dh.astype(BF16)
            dw_ref[...] += dw

    blk = pl.BlockSpec((1, tm, FB), lambda i, j: (j, i, 0))
    wspec = pl.BlockSpec((1, FB, D), lambda i, j: (j, 0, 0))
    row = pl.BlockSpec((tm, D), lambda i, j: (i, 0))
    vec = pl.BlockSpec((1, D), lambda i, j: (0, 0))
    return pl.pallas_call(
        body, grid=(S // tm, nblk), name="ffn_bwd_in",
        in_specs=[blk, blk, wspec, wspec, ANY, ANY, vec],
        out_specs=[row, row, vec], out_shape=[SDS((S, D), F32), SDS((S, D), BF16), SDS((1, D), F32)],
        scratch_shapes=[pltpu.VMEM((tm, D), F32), pltpu.VMEM((tm, D), F32), pltpu.VMEM((tm, D), F32),
                        pltpu.SemaphoreType.DMA((2,))],
        compiler_params=_cp(2))(dg, du, wg, wu, h1, dh2, w_norm)


def _dmix(dh1, w_out):
    S, D = dh1.shape
    tm = min(512, S)

    def body(dh_ref, w_ref, o_ref):
        o_ref[...] = _dot(dh_ref[...], w_ref[...], _NT)

    row = pl.BlockSpec((tm, D), lambda i: (i, 0))
    return pl.pallas_call(
        body, grid=(S // tm,), name="dmix", in_specs=[row, pl.BlockSpec((D, D), lambda i: (0, 0))],
        out_specs=row, out_shape=SDS((S, D), F32), compiler_params=_cp(1))(dh1, w_out)


def _in_bwd(dproj, w_blk, x, dh1, w_norm):
    S, D = x.shape
    nblk, _, NB = w_blk.shape
    tm = min(512, S)

    def body(dp_ref, w_ref, x_ref, dh1_ref, wn_ref, dx_ref, dw_ref, acc):
        i, j = pl.program_id(0), pl.program_id(1)

        @pl.when(j == 0)
        def _():
            acc[...] = jnp.zeros_like(acc)

        @pl.when((i == 0) & (j == 0))
        def _():
            dw_ref[...] = jnp.zeros_like(dw_ref)

        acc[...] += _dot(dp_ref[...], w_ref[0], _NT)

        @pl.when(j == nblk - 1)
        def _():
            dh, dw = _rms_bwd(acc[...], x_ref[...], wn_ref[...])
            dx_ref[...] = dh1_ref[...] + dh
            dw_ref[...] += dw

    row = pl.BlockSpec((tm, D), lambda i, j: (i, 0))
    vec = pl.BlockSpec((1, D), lambda i, j: (0, 0))
    return pl.pallas_call(
        body, grid=(S // tm, nblk), name="in_bwd",
        in_specs=[pl.BlockSpec((tm, NB), lambda i, j: (i, j)), pl.BlockSpec((1, D, NB), lambda i, j: (j, 0, 0)),
                  row, row, vec],
        out_specs=[row, vec], out_shape=[SDS((S, D), F32), SDS((1, D), F32)],
        scratch_shapes=[pltpu.VMEM((tm, D), F32)], compiler_params=_cp(2))(dproj, w_blk, x, dh1, w_norm)


def _wgrad(a, b, a_spec, b_spec, o_spec, o_shape, grid, name):
    nk = grid[-1]

    def ld(ref):
        return ref[0] if len(ref.shape) == 3 else ref[...]

    def body(a_ref, b_ref, o_ref, acc):
        k = pl.program_id(len(grid) - 1)

        @pl.when(k == 0)
        def _():
            acc[...] = jnp.zeros_like(acc)

        acc[...] += _dot(ld(a_ref), ld(b_ref), _TN)

        @pl.when(k == nk - 1)
        def _():
            if len(o_ref.shape) == 3:
                o_ref[0] = acc[...].astype(o_ref.dtype)
            else:
                o_ref[...] = acc[...].astype(o_ref.dtype)

    return pl.pallas_call(
        body, grid=grid, name=name, in_specs=[a_spec, b_spec], out_specs=o_spec, out_shape=SDS(o_shape, BF16),
        scratch_shapes=[pltpu.VMEM(o_spec.block_shape[-2:], F32)], compiler_params=_cp(len(grid)))(a, b)


def _peer(k):
    x, y, c = lax.axis_index("x"), lax.axis_index("y"), lax.axis_index("c")
    px = 1 - x if k & 4 else x
    py = 1 - y if k & 2 else y
    pc = 1 - c if k & 1 else c
    return (px, py, pc), 4 * px + 2 * py + pc


def _exchange_copies(srcs, lands, send_sems, recv_sems, which, gather):
    _, me = _peer(0)
    pairs = []
    for pos, a in enumerate(which):
        for k in range(1, N_DEV):
            dev, idx = _peer(k)
            sem = pos * (N_DEV - 1) + k - 1
            src = srcs[a] if gather else srcs[a].at[idx]
            mk = functools.partial(pltpu.make_async_remote_copy, src_ref=src, send_sem=send_sems.at[sem],
                                   recv_sem=recv_sems.at[sem], device_id=dev, device_id_type=MESH)
            pairs.append((mk(dst_ref=lands[a].at[me]), mk(dst_ref=lands[a].at[idx])))
    return pairs


def _sequencer_kernel(name, collective_id, n_remote, n_local):
    return pl.kernel(mesh=plsc.ScalarSubcoreMesh(axis_name="sequencer", num_cores=1), name=name,
                     scratch_types=(pltpu.SemaphoreType.DMA((n_remote,)), pltpu.SemaphoreType.DMA((n_remote,)),
                                    pltpu.SemaphoreType.DMA((n_local,))),
                     compiler_params=pltpu.CompilerParams(collective_id=collective_id))


def _handshake(ks):
    barrier = pltpu.get_barrier_semaphore()
    for k in ks:
        pl.semaphore_signal(barrier, inc=1, device_id=_peer(k)[0], device_id_type=MESH)
    pl.semaphore_wait(barrier, len(ks))


def _sequencer_scatter(arrays, name, collective_id):
    n = len(arrays)
    hbm = pltpu.MemorySpace.HBM
    srcs = [jax.new_ref(a, memory_space=hbm) for a in arrays]
    lands = [jax.empty_ref(SDS(a.shape, a.dtype), memory_space=hbm) for a in arrays]

    @_sequencer_kernel(name, collective_id, n * (N_DEV - 1), n)
    def launch(send_sems, recv_sems, local_sems):
        _handshake(range(1, N_DEV))
        _, me = _peer(0)
        local = [pltpu.make_async_copy(srcs[a].at[me], lands[a].at[me], local_sems.at[a]) for a in range(n)]
        pairs = _exchange_copies(srcs, lands, send_sems, recv_sems, range(n), False)
        for out, _ in pairs:
            out.start()
        for cp in local:
            cp.start()
        for out, arrival in pairs:
            out.wait_send()
            arrival.wait_recv()
        for cp in local:
            cp.wait()

    launch()
    return [r[...] for r in lands]


SIBLING = 1
OTHER_CHIPS = (2, 4, 6)


def _sequencer_gather(arrays, name, collective_id):
    n = len(arrays)
    hbm = pltpu.MemorySpace.HBM
    srcs = [jax.new_ref(a, memory_space=hbm) for a in arrays]
    lands = [jax.empty_ref(SDS((N_DEV,) + a.shape, a.dtype), memory_space=hbm) for a in arrays]

    @_sequencer_kernel(name, collective_id, n * (N_DEV - 1), n)
    def launch(send_sems, recv_sems, local_sems):
        _handshake((SIBLING,) + OTHER_CHIPS)
        _, me = _peer(0)
        sibling, _ = _peer(SIBLING)

        def copy(a, k, src, block, to):
            sem = a * (N_DEV - 1) + k - 1
            return pltpu.make_async_remote_copy(src_ref=src, dst_ref=lands[a].at[block], send_sem=send_sems.at[sem],
                                                recv_sem=recv_sems.at[sem], device_id=to, device_id_type=MESH)

        local = [pltpu.make_async_copy(srcs[a], lands[a].at[me], local_sems.at[a]) for a in range(n)]
        first = [copy(a, k, srcs[a], me, _peer(k)[0]) for a in range(n) for k in OTHER_CHIPS + (SIBLING,)]
        for cp in first + local:
            cp.start()
        passed = []
        for a in range(n):
            for k in OTHER_CHIPS:
                _, block = _peer(k)
                copy(a, k, srcs[a], block, sibling).wait_recv()
                passed.append(copy(a, k ^ SIBLING, lands[a].at[block], block, sibling))
                passed[-1].start()
        for a in range(n):
            for k in (SIBLING,) + tuple(k ^ SIBLING for k in OTHER_CHIPS):
                copy(a, k, srcs[a], _peer(k)[1], sibling).wait_recv()
        for cp in first + passed:
            cp.wait_send()
        for cp in local:
            cp.wait()

    launch()
    return [r[...] for r in lands]


SMALL_ROWS = 64


def _small_step(part, w, m, v):
    def body(p_ref, w_ref, m_ref, v_ref, g_ref, d_ref, nm_ref, nv_ref, gath, send_sems, recv_sems):
        _, me = _peer(0)
        gath[me] = p_ref[...]
        copies = []
        for k in range(1, N_DEV):
            dev, idx = _peer(k)
            out = pltpu.make_async_remote_copy(src_ref=p_ref, dst_ref=gath.at[me], send_sem=send_sems.at[k - 1],
                                               recv_sem=recv_sems.at[k - 1], device_id=dev, device_id_type=MESH)
            out.start()
            arrival = pltpu.make_async_remote_copy(src_ref=p_ref, dst_ref=gath.at[idx], send_sem=send_sems.at[k - 1],
                                                   recv_sem=recv_sems.at[k - 1], device_id=dev, device_id_type=MESH)
            copies.append((out, arrival))
        for out, arrival in copies:
            out.wait_send()
            arrival.wait_recv()
        g = gath[0]
        for p in range(1, N_DEV):
            g = g + gath[p]
        g_ref[...] = g
        d_ref[...], nm_ref[...], nv_ref[...] = _adamw(w_ref[...], g, m_ref[...], v_ref[...])

    vm = pl.BlockSpec(memory_space=pltpu.VMEM)
    return pl.pallas_call(
        body, name="small_step", in_specs=[vm] * 4, out_specs=[vm] * 4,
        out_shape=[SDS((SMALL_ROWS, 128), F32)] * 4,
        scratch_shapes=[pltpu.VMEM((N_DEV, SMALL_ROWS, 128), F32), pltpu.SemaphoreType.DMA((N_DEV - 1,)),
                        pltpu.SemaphoreType.DMA((N_DEV - 1,))])(part, w, m, v)


def _adamw(w, g, m, v):
    m = ADAM_B1 * m + (1.0 - ADAM_B1) * g
    v = ADAM_B2 * v + (1.0 - ADAM_B2) * (g * g)
    m_hat = m / (1.0 - ADAM_B1 ** ADAM_STEP)
    v_hat = v / (1.0 - ADAM_B2 ** ADAM_STEP)
    delta = -ADAM_LR * (m_hat / (jnp.sqrt(v_hat) + ADAM_EPS) + ADAM_WD * w)
    return delta, m, v


def _adamw_block(parts, w, m, v, name):
    R, C = w.shape
    tr = next(t for t in (256, 128, 64, 32, 16, 8) if R % t == 0 and t * C <= 256 * 1024)

    def body(p_ref, w_ref, m_ref, v_ref, g_ref, d_ref, nm_ref, nv_ref):
        g = p_ref[0].astype(F32)
        for p in range(1, N_DEV):
            g = g + p_ref[p].astype(F32)
        g_ref[...] = g
        d_ref[...], nm_ref[...], nv_ref[...] = _adamw(w_ref[...], g, m_ref[...], v_ref[...])

    row = pl.BlockSpec((tr, C), lambda i: (i, 0))
    return pl.pallas_call(
        body, grid=(R // tr,), name=name, in_specs=[pl.BlockSpec((N_DEV, tr, C), lambda i: (0, i, 0)), row, row, row],
        out_specs=[row] * 4, out_shape=[SDS((R, C), F32)] * 4, compiler_params=_cp(1))(parts, w, m, v)


def _pack_small(mix, ffn, fin, retw, dec_f, dec_b, loss):
    flat = jnp.concatenate([mix.reshape(-1), ffn.reshape(-1), fin.reshape(-1), retw.reshape(-1), dec_f.reshape(-1),
                            dec_b.reshape(-1), loss.reshape(-1)])
    return jnp.pad(flat, (0, SMALL_ROWS * 128 - flat.shape[0])).reshape(SMALL_ROWS, 128)


def _unpack_small(packed, shapes):
    flat = packed.reshape(-1)
    out, at = [], 0
    for s in shapes:
        n = math.prod(s)
        out.append(flat[at:at + n].reshape(s))
        at += n
    return out


def kernel(x, norm_mix_w, w_in, ret_decay_fwd, ret_decay_bwd, ret_norm_w, w_out, norm_ffn_w, w_gate, w_up, w_down, norm_final_w, loss_target, m_norm_mix_w, m_w_in, m_ret_decay_fwd, m_ret_decay_bwd, m_ret_norm_w, m_w_out, m_norm_ffn_w, m_w_gate, m_w_up, m_w_down, m_norm_final_w, v_norm_mix_w, v_w_in, v_ret_decay_fwd, v_ret_decay_bwd, v_ret_norm_w, v_w_out, v_norm_ffn_w, v_w_gate, v_w_up, v_w_down, v_norm_final_w):
    x2 = x[0]
    tgt = loss_target[0]
    S, D = x2.shape
    H = ret_norm_w.shape[1] // HEAD_DIM
    DA = H * HEAD_DIM
    fin_w = norm_final_w.reshape(1, D)
    big = (w_in[0], w_out[0], w_gate[0].T, w_up[0].T, w_down[0])

    big_b = [w.astype(BF16) for w in big]
    wi, = _sequencer_gather(big_b[:1], "gather_in", 0)
    wo, wg, wu = _sequencer_gather(big_b[1:4], "gather_mid", 1)
    wd, = _sequencer_gather(big_b[4:], "gather_down", 5)
    NB = wi.shape[2]

    proj, n1 = _proj_fwd(x2, norm_mix_w, wi)
    bias = _attn_bias()[:H]
    attn, lse = _attn_fwd(proj, bias)
    ret, o_raw = _ret_fwd(proj, ret_decay_fwd, ret_decay_bwd, ret_norm_w)
    wo_full = wo.reshape(D, D)
    FB = wd.shape[1]
    wd_full = wd.reshape(N_DEV * FB, D)
    h1, mixed, n2 = _out_fwd(x2, attn, ret, wo_full, norm_ffn_w)
    gate, up, act = _ffn_up(n2, wg, wu)
    dh2, dh2_b, loss_parts, g_fin = _ffn_down_loss(act, wd_full, h1, tgt, fin_w)

    dgate, dup = _ffn_bwd_act(dh2_b, wd_full, gate, up)
    tn = min(1024, D)
    ffn_specs = (pl.BlockSpec((1, S, FB), lambda j, n, k: (j, 0, 0)), pl.BlockSpec((S, tn), lambda j, n, k: (0, n)),
                 pl.BlockSpec((1, FB, tn), lambda j, n, k: (j, 0, n)), (N_DEV, FB, D), (N_DEV, D // tn, 1))
    g_wd = _wgrad(act, dh2_b, *ffn_specs, "wgrad_down")
    g_wg = _wgrad(dgate, n2, *ffn_specs, "wgrad_gate")
    g_wu = _wgrad(dup, n2, *ffn_specs, "wgrad_up")
    parts_f = _sequencer_scatter([g_wg, g_wu, g_wd], "scatter_ffn", 2)
    dh1, dh1_b, g_ffn = _ffn_bwd_in(dgate, dup, wg, wu, h1, dh2, norm_ffn_w)
    dmix = _dmix(dh1_b, wo_full)
    tmw = min(512, D)
    tk = min(2048, S)
    g_wo = _wgrad(mixed, dh1_b, pl.BlockSpec((tk, tmw), lambda m, k: (k, m)), pl.BlockSpec((tk, D), lambda m, k: (k, 0)),
                  pl.BlockSpec((tmw, D), lambda m, k: (m, 0)), (D, D), (D // tmw, S // tk), "wgrad_out")
    parts_o = _sequencer_scatter([g_wo.reshape(N_DEV, D // N_DEV, D)], "scatter_out", 3)
    dq_r, dk_r, dv_r, dg_r, small = _ret_bwd(proj, o_raw, dmix, ret_decay_fwd, ret_decay_bwd, ret_norm_w, DA)
    dq_a, dk_a, dv_a = _attn_bwd(proj, attn, lse, dmix, bias)
    dproj = jnp.concatenate([t.astype(BF16) for t in (dq_a, dk_a, dv_a, dq_r, dk_r, dv_r, dg_r)], axis=1)
    g_wi = _wgrad(n1, dproj, pl.BlockSpec((S, tmw), lambda j, m, k: (0, m)), pl.BlockSpec((S, NB), lambda j, m, k: (0, j)),
                  pl.BlockSpec((1, tmw, NB), lambda j, m, k: (j, m, 0)), (N_DEV, D, NB), (N_DEV, D // tmw, 1), "wgrad_in")
    parts_i = _sequencer_scatter([g_wi], "scatter_in", 4)
    grad_x, g_mix = _in_bwd(dproj, wi, x2, dh1, norm_mix_w)

    big_m = (m_w_in[0], m_w_out[0], m_w_gate[0].T, m_w_up[0].T, m_w_down[0])
    big_v = (v_w_in[0], v_w_out[0], v_w_gate[0].T, v_w_up[0].T, v_w_down[0])
    names = ("adamw_in", "adamw_out", "adamw_gate", "adamw_up", "adamw_down")
    upd = [None] * 5
    for a, p in zip((2, 3, 4, 1, 0), parts_f + parts_o + parts_i):
        upd[a] = _adamw_block(p, big[a], big_m[a], big_v[a], names[a])

    g_dec_f = small[:, 0, 0].reshape(1, H)
    g_dec_b = small[:, 1, 0].reshape(1, H)
    g_retw = small[:, 2, :].reshape(1, DA)
    loss_local = jnp.sum(loss_parts[::8, 0])
    zero = jnp.zeros((1,), F32)
    part = _pack_small(g_mix, g_ffn, g_fin, g_retw, g_dec_f, g_dec_b, loss_local)
    sw = _pack_small(norm_mix_w, norm_ffn_w, norm_final_w, ret_norm_w, ret_decay_fwd, ret_decay_bwd, zero)
    sm = _pack_small(m_norm_mix_w, m_norm_ffn_w, m_norm_final_w, m_ret_norm_w, m_ret_decay_fwd, m_ret_decay_bwd, zero)
    sv = _pack_small(v_norm_mix_w, v_norm_ffn_w, v_norm_final_w, v_ret_norm_w, v_ret_decay_fwd, v_ret_decay_bwd, zero)
    shapes = [(1, D), (1, D), (D,), (1, DA), (1, H), (1, H), ()]
    sg, sd, snm, snv = [_unpack_small(t, shapes) for t in _small_step(part, sw, sm, sv)]
    loss = sg[6]

    def ordered(small_set, k):
        b = [(u[k].T if a in (2, 3) else u[k])[None] for a, u in enumerate(upd)]
        return [small_set[0], b[0], small_set[4], small_set[5], small_set[3], b[1], small_set[1], b[2], b[3], b[4],
                small_set[2]]

    return (loss, grad_x[None], *ordered(sg, 0), *ordered(sd, 1), *ordered(snm, 2), *ordered(snv, 3))
```

```python
import functools
import math

import numpy as np
import jax
import jax.numpy as jnp
from jax import lax
from jax.experimental import pallas as pl
from jax.experimental.pallas import tpu as pltpu
from jax.experimental.pallas import tpu_sc as plsc

F32 = jnp.float32
BF16 = jnp.bfloat16
SDS = jax.ShapeDtypeStruct

HEAD_DIM = 128
EPS = 1e-6
RET_CHUNK = 128
DILATIONS = (1, 4, 16)
BAND = 64
Q_TILE = 128
K_TILE = Q_TILE + 2 * BAND
KV_PAD = BAND * max(DILATIONS)
TILE_GROUP = 8
NEG = -1e30
N_DEV = 8
N_GROUPS = 7
ADAM_LR, ADAM_B1, ADAM_B2, ADAM_EPS, ADAM_WD, ADAM_STEP = 0.001, 0.9, 0.999, 1e-08, 0.01, 10
VMEM_LIMIT = 56 * 1024 * 1024
MESH = pl.DeviceIdType.MESH
ANY = pl.BlockSpec(memory_space=pl.ANY)


def _cp(n_grid):
    return pltpu.CompilerParams(dimension_semantics=("arbitrary",) * n_grid, vmem_limit_bytes=VMEM_LIMIT)


def _sigmoid(x):
    return 1.0 / (1.0 + jnp.exp(-x))


def _rms_scale(h):
    return lax.rsqrt(jnp.mean(h * h, axis=-1, keepdims=True) + EPS)


def _rms_bwd(dn, h, w):
    r = _rms_scale(h)
    gw = dn * w
    dh = r * gw - h * (r * r * r) * jnp.mean(gw * h, axis=-1, keepdims=True)
    return dh, jnp.sum(dn * h * r, axis=0, keepdims=True)


def _dot(a, b, dims):
    return lax.dot_general(a.astype(BF16), b.astype(BF16), (dims, ((), ())), preferred_element_type=F32)


_NN = ((1,), (0,))
_NT = ((1,), (1,))
_TN = ((0,), (0,))


def _proj_fwd(x, w_norm, w_blk):
    S, D = x.shape
    nblk, _, NB = w_blk.shape
    tm = min(1024, S)

    def body(x_ref, wn_ref, w_ref, proj_ref, n_ref, n_scr):
        @pl.when(pl.program_id(1) == 0)
        def _():
            xf = x_ref[...]
            nb = (xf * _rms_scale(xf) * wn_ref[...]).astype(BF16)
            n_scr[...] = nb
            n_ref[...] = nb
        proj_ref[...] = jnp.dot(n_scr[...], w_ref[0], preferred_element_type=F32)

    return pl.pallas_call(
        body, grid=(S // tm, nblk), name="proj_fwd",
        in_specs=[pl.BlockSpec((tm, D), lambda i, j: (i, 0)), pl.BlockSpec((1, D), lambda i, j: (0, 0)),
                  pl.BlockSpec((1, D, NB), lambda i, j: (j, 0, 0))],
        out_specs=[pl.BlockSpec((tm, NB), lambda i, j: (i, j)), pl.BlockSpec((tm, D), lambda i, j: (i, 0))],
        out_shape=[SDS((S, nblk * NB), F32), SDS((S, D), BF16)],
        scratch_shapes=[pltpu.VMEM((tm, D), BF16)], compiler_params=_cp(2))(x, w_norm, w_blk)


def _out_fwd(x, attn, ret, w_out, w_norm):
    S, D = x.shape
    DA = attn.shape[1]
    tm = min(256, S)

    def body(x_ref, a_ref, r_ref, w_ref, wn_ref, h_ref, mix_ref, n_ref):
        a = a_ref[...].astype(BF16)
        r = r_ref[...].astype(BF16)
        mix_ref[:, :DA] = a
        mix_ref[:, DA:] = r
        h = x_ref[...] + jnp.dot(a, w_ref[:DA, :], preferred_element_type=F32) \
            + jnp.dot(r, w_ref[DA:, :], preferred_element_type=F32)
        h_ref[...] = h
        n_ref[...] = (h * _rms_scale(h) * wn_ref[...]).astype(BF16)

    row = lambda w: pl.BlockSpec((tm, w), lambda i: (i, 0))
    return pl.pallas_call(
        body, grid=(S // tm,), name="out_fwd",
        in_specs=[row(D), row(DA), row(D - DA), pl.BlockSpec((D, D), lambda i: (0, 0)),
                  pl.BlockSpec((1, D), lambda i: (0, 0))],
        out_specs=[row(D), row(D), row(D)],
        out_shape=[SDS((S, D), F32), SDS((S, D), BF16), SDS((S, D), BF16)],
        compiler_params=_cp(1))(x, attn, ret, w_out, w_norm)


def _ffn_up(n2, wg, wu):
    S, D = n2.shape
    nblk, FB, _ = wg.shape
    tm = min(512, S)

    def body(n_ref, wg_ref, wu_ref, g_ref, u_ref, a_ref):
        n = n_ref[...]
        g = _dot(n, wg_ref[0], _NT)
        u = _dot(n, wu_ref[0], _NT)
        g_ref[0] = g
        u_ref[0] = u
        a_ref[0] = (g * _sigmoid(g) * u).astype(BF16)

    wspec = pl.BlockSpec((1, FB, D), lambda j, i: (j, 0, 0))
    ospec = pl.BlockSpec((1, tm, FB), lambda j, i: (j, i, 0))
    return pl.pallas_call(
        body, grid=(nblk, S // tm), name="ffn_up",
        in_specs=[pl.BlockSpec((tm, D), lambda j, i: (i, 0)), wspec, wspec],
        out_specs=[ospec, ospec, ospec],
        out_shape=[SDS((nblk, S, FB), F32), SDS((nblk, S, FB), F32), SDS((nblk, S, FB), BF16)],
        compiler_params=_cp(2))(n2, wg, wu)


def _ffn_down_loss(act, wd, h1, target, w_norm):
    nblk, S, FB = act.shape
    D = h1.shape[1]
    tm = min(512, S)

    def body(a_ref, wd_ref, h_ref, t_ref, wn_ref, dh_ref, dhb_ref, loss_ref, dw_ref, acc):
        i, j = pl.program_id(0), pl.program_id(1)

        @pl.when(j == 0)
        def _():
            acc[...] = h_ref[...]

        @pl.when((i == 0) & (j == 0))
        def _():
            dw_ref[...] = jnp.zeros_like(dw_ref)

        acc[...] += jnp.dot(a_ref[0], wd_ref[...], preferred_element_type=F32)

        @pl.when(j == nblk - 1)
        def _():
            h = acc[...]
            w = wn_ref[...]
            err = h * _rms_scale(h) * w - t_ref[...]
            loss_ref[...] = jnp.full(loss_ref.shape, 0.5 * jnp.sum(err * err) / D, F32)
            dh, dw = _rms_bwd(err * (1.0 / D), h, w)
            dh_ref[...] = dh
            dhb_ref[...] = dh.astype(BF16)
            dw_ref[...] += dw

    row = pl.BlockSpec((tm, D), lambda i, j: (i, 0))
    vec = pl.BlockSpec((1, D), lambda i, j: (0, 0))
    return pl.pallas_call(
        body, grid=(S // tm, nblk), name="ffn_down_loss",
        in_specs=[pl.BlockSpec((1, tm, FB), lambda i, j: (j, i, 0)), pl.BlockSpec((FB, D), lambda i, j: (j, 0)),
                  row, row, vec],
        out_specs=[row, row, pl.BlockSpec((8, 128), lambda i, j: (i, 0)), vec],
        out_shape=[SDS((S, D), F32), SDS((S, D), BF16), SDS((S // tm * 8, 128), F32), SDS((1, D), F32)],
        scratch_shapes=[pltpu.VMEM((tm, D), F32)], compiler_params=_cp(2))(act, wd, h1, target, w_norm)


def _attn_bias():
    n_heads = 8
    slopes = np.exp2(-8.0 * np.arange(1, n_heads + 1, dtype=np.float32) / n_heads)
    dist = np.abs(np.arange(K_TILE)[None, :] - BAND - np.arange(Q_TILE)[:, None])
    out = np.empty((n_heads, len(DILATIONS), Q_TILE, K_TILE), np.float32)
    for h in range(n_heads):
        for p, d in enumerate(DILATIONS):
            out[h, p] = np.where(dist <= BAND, -slopes[h] * (d * dist).astype(np.float32), NEG)
    return jnp.asarray(out)


def _attn_tiles(S, d):
    L = S // d
    per_class = L // Q_TILE
    return L, per_class, d * per_class


def _tile_rows(t, d, per_class):
    r = t // per_class
    a = (t % per_class) * Q_TILE
    q_rows = pl.ds(r + d * a, Q_TILE, stride=d) if d > 1 else pl.ds(pl.multiple_of(a, Q_TILE), Q_TILE)
    k_rows = pl.ds(KV_PAD + r + d * (a - BAND), K_TILE, stride=d) if d > 1 else pl.ds(
        pl.multiple_of(KV_PAD + a - BAND, BAND), K_TILE)
    return a, q_rows, k_rows


def _lanes(x, width):
    return jnp.concatenate([x] * (width // HEAD_DIM), axis=-1)


_BNT = (((2,), (2,)), ((0,), (0,)))
_BNN = (((2,), (1,)), ((0,), (0,)))
_BTN = (((1,), (1,)), ((0,), (0,)))


def _bdot(a, b, dims):
    return lax.dot_general(a, b, dims, preferred_element_type=F32)


def _stacked(rows, loaders):
    return [jnp.stack([f(*r) for r in rows]) for f in loaders]


def _edge_mask(a, L):
    lk = lax.broadcasted_iota(jnp.int32, (1, K_TILE), 1) + (a - BAND)
    return jnp.where((lk >= 0) & (lk < L), 0.0, NEG).astype(F32)


def _fill_padded(dst, src, S):
    dst[pl.ds(0, KV_PAD), :] = jnp.zeros((KV_PAD, HEAD_DIM), F32)
    dst[pl.ds(KV_PAD + S, KV_PAD), :] = jnp.zeros((KV_PAD, HEAD_DIM), F32)
    dst[pl.ds(KV_PAD, S), :] = src[...]


def _head_specs(S, groups, n_heads):
    return [pl.BlockSpec((S, HEAD_DIM), functools.partial(lambda h, g: (0, g * n_heads + h), g=g)) for g in groups]


def _attn_fwd(proj, bias):
    S = proj.shape[0]
    H = proj.shape[1] // (N_GROUPS * HEAD_DIM)
    scale = HEAD_DIM ** -0.5

    def body(q_ref, k_ref, v_ref, b_ref, o_ref, lse_ref, kp, vp, m_run, l_run):
        _fill_padded(kp, k_ref, S)
        _fill_padded(vp, v_ref, S)
        o_ref[...] = jnp.zeros_like(o_ref)
        m_run[...] = jnp.full(m_run.shape, NEG, F32)
        l_run[...] = jnp.zeros_like(l_run)
        for p, d in enumerate(DILATIONS):
            L, per_class, n_tiles = _attn_tiles(S, d)

            def tiles(t, carry, p=p, d=d, L=L, per_class=per_class, n_tiles=n_tiles):
                rows = [_tile_rows(t + u * (n_tiles // TILE_GROUP), d, per_class) for u in range(TILE_GROUP)]
                qs, ks, vs, m_old, l_old, o_old, edge = _stacked(rows, (
                    lambda a, qr, kr: q_ref[qr, :].astype(BF16), lambda a, qr, kr: kp[kr, :].astype(BF16),
                    lambda a, qr, kr: vp[kr, :].astype(BF16), lambda a, qr, kr: m_run[qr, :],
                    lambda a, qr, kr: l_run[qr, :], lambda a, qr, kr: o_ref[qr, :], lambda a, qr, kr: _edge_mask(a, L)))
                s = _bdot(qs, ks, _BNT) * scale + b_ref[0, p][None] + edge
                m_new = jnp.maximum(m_old, jnp.max(s, axis=-1, keepdims=True))
                pr = jnp.exp(s - _lanes(m_new, K_TILE)).astype(BF16)
                alpha = jnp.exp(m_old - m_new)
                l_new = alpha * l_old + _bdot(pr, jnp.ones((TILE_GROUP, K_TILE, HEAD_DIM), BF16), _BNN)
                o_new = alpha * o_old + _bdot(pr, vs, _BNN)
                for u, (_, qr, _) in enumerate(rows):
                    o_ref[qr, :] = o_new[u]
                    m_run[qr, :] = m_new[u]
                    l_run[qr, :] = l_new[u]
                return carry

            lax.fori_loop(0, n_tiles // TILE_GROUP, tiles, 0)
        l = l_run[...]
        o_ref[...] = o_ref[...] / l
        lse_ref[...] = m_run[...] + jnp.log(l)

    hspec = pl.BlockSpec((S, HEAD_DIM), lambda h: (0, h))
    return pl.pallas_call(
        body, grid=(H,), name="attn_fwd",
        in_specs=_head_specs(S, (0, 1, 2), H) + [
            pl.BlockSpec((1, len(DILATIONS), Q_TILE, K_TILE), lambda h: (h, 0, 0, 0))],
        out_specs=[hspec, hspec],
        out_shape=[SDS((S, H * HEAD_DIM), F32), SDS((S, H * HEAD_DIM), F32)],
        scratch_shapes=[pltpu.VMEM((S + 2 * KV_PAD, HEAD_DIM), F32), pltpu.VMEM((S + 2 * KV_PAD, HEAD_DIM), F32),
                        pltpu.VMEM((S, HEAD_DIM), F32), pltpu.VMEM((S, HEAD_DIM), F32)],
        compiler_params=_cp(1))(proj, proj, proj, bias)


def _attn_bwd(proj, out, lse, dmix, bias):
    S = proj.shape[0]
    H = proj.shape[1] // (N_GROUPS * HEAD_DIM)
    scale = HEAD_DIM ** -0.5

    def body(q_ref, k_ref, v_ref, o_ref, lse_ref, do_ref, b_ref, dq_ref, dk_ref, dv_ref, kp, vp, dkp, dvp, dsum):
        _fill_padded(kp, k_ref, S)
        _fill_padded(vp, v_ref, S)
        dkp[...] = jnp.zeros_like(dkp)
        dvp[...] = jnp.zeros_like(dvp)
        dq_ref[...] = jnp.zeros_like(dq_ref)
        dsum[...] = jnp.broadcast_to(jnp.sum(do_ref[...] * o_ref[...], axis=-1, keepdims=True), dsum.shape)
        for p, d in enumerate(DILATIONS):
            L, per_class, n_tiles = _attn_tiles(S, d)

            def tiles(t, carry, p=p, d=d, L=L, per_class=per_class, n_tiles=n_tiles):
                rows = [_tile_rows(t + u * (n_tiles // TILE_GROUP), d, per_class) for u in range(TILE_GROUP)]
                qs, ks, vs, dos, lses, dsums, dq_old, dk_old, dv_old, edge = _stacked(rows, (
                    lambda a, qr, kr: q_ref[qr, :].astype(BF16), lambda a, qr, kr: kp[kr, :].astype(BF16),
                    lambda a, qr, kr: vp[kr, :].astype(BF16), lambda a, qr, kr: do_ref[qr, :].astype(BF16),
                    lambda a, qr, kr: lse_ref[qr, :], lambda a, qr, kr: dsum[qr, :], lambda a, qr, kr: dq_ref[qr, :],
                    lambda a, qr, kr: dkp[kr, :], lambda a, qr, kr: dvp[kr, :], lambda a, qr, kr: _edge_mask(a, L)))
                s = _bdot(qs, ks, _BNT) * scale + b_ref[0, p][None] + edge
                pr = jnp.exp(s - _lanes(lses, K_TILE))
                ds = (pr * (_bdot(dos, vs, _BNT) - _lanes(dsums, K_TILE)) * scale).astype(BF16)
                dq_new = dq_old + _bdot(ds, ks, _BNN)
                dk_new = dk_old + _bdot(ds, qs, _BTN)
                dv_new = dv_old + _bdot(pr.astype(BF16), dos, _BTN)
                for u, (_, qr, kr) in enumerate(rows):
                    dq_ref[qr, :] = dq_new[u]
                    dkp[kr, :] = dk_new[u]
                    dvp[kr, :] = dv_new[u]
                return carry

            lax.fori_loop(0, n_tiles // TILE_GROUP, tiles, 0)
        dk_ref[...] = dkp[pl.ds(KV_PAD, S), :]
        dv_ref[...] = dvp[pl.ds(KV_PAD, S), :]

    hspec = pl.BlockSpec((S, HEAD_DIM), lambda h: (0, h))
    padded = pltpu.VMEM((S + 2 * KV_PAD, HEAD_DIM), F32)
    return pl.pallas_call(
        body, grid=(H,), name="attn_bwd",
        in_specs=_head_specs(S, (0, 1, 2), H) + [hspec, hspec, hspec,
                                                  pl.BlockSpec((1, len(DILATIONS), Q_TILE, K_TILE), lambda h: (h, 0, 0, 0))],
        out_specs=[hspec, hspec, hspec],
        out_shape=[SDS((S, H * HEAD_DIM), F32)] * 3,
        scratch_shapes=[padded, padded, padded, padded, pltpu.VMEM((S, HEAD_DIM), F32)],
        compiler_params=_cp(1))(proj, proj, proj, out, lse, dmix, bias)


def _ret_consts(lg, forward):
    C = RET_CHUNK
    i = lax.broadcasted_iota(jnp.int32, (C, C), 0)
    j = lax.broadcasted_iota(jnp.int32, (C, C), 1)
    rel = (i - j) if forward else (j - i)
    inside = (rel >= 0) if forward else (rel > 0)
    relf = jnp.maximum(rel, 0).astype(F32)
    mask = jnp.where(inside, jnp.exp(lg * relf), 0.0)
    idx = lax.broadcasted_iota(jnp.int32, (C, 1), 0).astype(F32)
    q_exp = (idx + 1.0) if forward else (C - idx)
    k_exp = (C - 1.0 - idx) if forward else idx
    return mask, relf, jnp.exp(lg * q_exp), q_exp, jnp.exp(lg * k_exp), k_exp, jnp.exp(lg * C)


def _log_decay(dec_ref, h):
    return -jnp.exp(jnp.full((1, 1), dec_ref[0, h], F32))


CHUNK_BATCH = 8


def _batch_rows(b):
    n = CHUNK_BATCH * RET_CHUNK
    return pl.ds(pl.multiple_of(b * n, n), n)


def _batch_chunks(b):
    return pl.ds(pl.multiple_of(b * CHUNK_BATCH, CHUNK_BATCH), CHUNK_BATCH)


def _chunks3(x):
    return x.reshape(CHUNK_BATCH, RET_CHUNK, HEAD_DIM)


def _ret_scan(buf, c_decs, nc, reverse):
    def step(n, carry):
        new = []
        for way, r in enumerate(carry):
            c = n if (way == 0) != reverse else nc - 1 - n
            term = buf[way, c]
            buf[way, c] = r
            new.append(r * c_decs[way] + term)
        return tuple(new)

    lax.fori_loop(0, nc, step, (jnp.zeros((HEAD_DIM, HEAD_DIM), F32),) * 2)


def _ret_fwd(proj, dec_f, dec_b, w_norm):
    S = proj.shape[0]
    H = proj.shape[1] // (N_GROUPS * HEAD_DIM)
    nc = S // RET_CHUNK
    scale = HEAD_DIM ** -0.5

    def body(df_ref, db_ref, q_ref, k_ref, v_ref, g_ref, w_ref, y_ref, o_ref, states):
        h = pl.program_id(0)
        consts = [_ret_consts(_log_decay(dref, h), fw) for fw, dref in ((True, df_ref), (False, db_ref))]

        def kv_step(b, carry):
            rows, batch = _batch_rows(b), _batch_chunks(b)
            k3 = _chunks3(k_ref[rows, :])
            v3 = _chunks3(v_ref[rows, :]).astype(BF16)
            for way in range(2):
                states[way, batch] = _bdot((k3 * consts[way][4]).astype(BF16), v3, _BTN)
            return carry

        lax.fori_loop(0, nc // CHUNK_BATCH, kv_step, 0)
        _ret_scan(states, [c[6] for c in consts], nc, False)

        def out_step(b, carry):
            rows, batch = _batch_rows(b), _batch_chunks(b)
            q3 = _chunks3(q_ref[rows, :] * scale)
            k3 = _chunks3(k_ref[rows, :]).astype(BF16)
            v3 = _chunks3(v_ref[rows, :]).astype(BF16)
            a0 = _bdot(q3.astype(BF16), k3, _BNT)
            o = None
            for way in range(2):
                mask, q_dec = consts[way][0], consts[way][2]
                part = _bdot((a0 * mask).astype(BF16), v3, _BNN) \
                    + _bdot((q3 * q_dec).astype(BF16), states[way, batch].astype(BF16), _BNN)
                o = part if o is None else o + part
            o_ref[rows, :] = o.reshape(CHUNK_BATCH * RET_CHUNK, HEAD_DIM)
            return carry

        lax.fori_loop(0, nc // CHUNK_BATCH, out_step, 0)
        o = o_ref[...]
        g = g_ref[...]
        y_ref[...] = o * _rms_scale(o) * w_ref[...] * (g * _sigmoid(g))

    hspec = pl.BlockSpec((S, HEAD_DIM), lambda h: (0, h))
    smem = pl.BlockSpec(memory_space=pltpu.SMEM)
    return pl.pallas_call(
        body, grid=(H,), name="ret_fwd",
        in_specs=[smem, smem] + _head_specs(S, (3, 4, 5, 6), H) + [pl.BlockSpec((1, HEAD_DIM), lambda h: (0, h))],
        out_specs=[hspec, hspec],
        out_shape=[SDS((S, H * HEAD_DIM), F32)] * 2,
        scratch_shapes=[pltpu.VMEM((2, nc, HEAD_DIM, HEAD_DIM), F32)],
        compiler_params=_cp(1))(dec_f, dec_b, proj, proj, proj, proj, w_norm)


def _ret_gate_bwd(proj, o_raw, dmix, w_norm, col0):
    S = proj.shape[0]
    H = proj.shape[1] // (N_GROUPS * HEAD_DIM)

    def body(g_ref, o_ref, dy_ref, w_ref, do_ref, dg_ref, dw_ref):
        o = o_ref[...]
        g = g_ref[...]
        dy = dy_ref[...]
        w = w_ref[...]
        rr = _rms_scale(o)
        normed = o * rr
        sg = _sigmoid(g)
        silu = g * sg
        dw_ref[0] = jnp.broadcast_to(jnp.sum(dy * normed * silu, axis=0, keepdims=True), (8, HEAD_DIM))
        dg_ref[...] = dy * normed * w * (sg * (1.0 + g * (1.0 - sg)))
        dnormed = dy * w * silu
        do_ref[...] = rr * dnormed - o * (rr * rr * rr) * jnp.mean(dnormed * o, axis=-1, keepdims=True)

    hspec = pl.BlockSpec((S, HEAD_DIM), lambda h: (0, h))
    nh0 = col0 // HEAD_DIM
    return pl.pallas_call(
        body, grid=(H,), name="ret_gate_bwd",
        in_specs=_head_specs(S, (6,), H) + [hspec, pl.BlockSpec((S, HEAD_DIM), lambda h: (0, nh0 + h)),
                                            pl.BlockSpec((1, HEAD_DIM), lambda h: (0, h))],
        out_specs=[hspec, hspec, pl.BlockSpec((1, 8, HEAD_DIM), lambda h: (h, 0, 0))],
        out_shape=[SDS((S, H * HEAD_DIM), F32)] * 2 + [SDS((H, 8, HEAD_DIM), F32)],
        compiler_params=_cp(1))(proj, o_raw, dmix, w_norm)


def _ret_bwd(proj, d_out, dec_f, dec_b):
    S = proj.shape[0]
    H = proj.shape[1] // (N_GROUPS * HEAD_DIM)
    C = RET_CHUNK
    nc = S // C
    scale = HEAD_DIM ** -0.5

    def body(df_ref, db_ref, q_ref, k_ref, v_ref, do, dq_ref, dk_ref, dv_ref, small_ref, states, d_states):
        h = pl.program_id(0)
        lgs = [_log_decay(df_ref, h), _log_decay(db_ref, h)]
        consts = [_ret_consts(lg, fw) for lg, fw in zip(lgs, (True, False))]

        def prep_step(b, carry):
            rows, batch = _batch_rows(b), _batch_chunks(b)
            q3 = _chunks3(q_ref[rows, :] * scale)
            k3 = _chunks3(k_ref[rows, :])
            v3 = _chunks3(v_ref[rows, :]).astype(BF16)
            do3 = _chunks3(do[rows, :]).astype(BF16)
            for way in range(2):
                states[way, batch] = _bdot((k3 * consts[way][4]).astype(BF16), v3, _BTN)
                d_states[way, batch] = _bdot((q3 * consts[way][2]).astype(BF16), do3, _BTN)
            return carry

        lax.fori_loop(0, nc // CHUNK_BATCH, prep_step, 0)
        c_decs = [c[6] for c in consts]
        _ret_scan(states, c_decs, nc, False)
        _ret_scan(d_states, c_decs, nc, True)

        def main_step(b, dlams):
            rows, batch = _batch_rows(b), _batch_chunks(b)
            q3 = _chunks3(q_ref[rows, :] * scale)
            k3 = _chunks3(k_ref[rows, :])
            q3b, k3b = q3.astype(BF16), k3.astype(BF16)
            v3b = _chunks3(v_ref[rows, :]).astype(BF16)
            do3b = _chunks3(do[rows, :]).astype(BF16)
            a0 = _bdot(q3b, k3b, _BNT)
            pv = _bdot(do3b, v3b, _BNT)
            dq = dk = dv = None
            new_dlams = []
            for way in range(2):
                mask, relf, q_dec, q_exp, k_dec, k_exp, c_dec = consts[way]
                state, d_state = states[way, batch], d_states[way, batch]
                dp = pv * mask
                dpb = dp.astype(BF16)
                gq = _bdot(do3b, state.astype(BF16), _BNT)
                gk = _bdot(v3b, d_state.astype(BF16), _BNT)
                parts = (_bdot(dpb, k3b, _BNN) + q_dec * gq, _bdot(dpb, q3b, _BTN) + k_dec * gk,
                         _bdot((a0 * mask).astype(BF16), do3b, _BTN)
                         + _bdot((k3 * k_dec).astype(BF16), d_state.astype(BF16), _BNN))
                dq, dk, dv = parts if dq is None else (dq + parts[0], dk + parts[1], dv + parts[2])
                total = lambda x: jnp.sum(jnp.sum(x, axis=0), axis=0, keepdims=True)
                new_dlams.append(dlams[way] + total(relf * a0 * dp)
                                 + total(q_exp * q_dec * q3 * gq + k_exp * k_dec * k3 * gk)
                                 + (C * c_dec) * total(state * d_state))
            flat = lambda x: x.reshape(CHUNK_BATCH * C, HEAD_DIM)
            dq_ref[rows, :] = flat(dq) * scale
            dk_ref[rows, :] = flat(dk)
            dv_ref[rows, :] = flat(dv)
            return tuple(new_dlams)

        dlams = lax.fori_loop(0, nc // CHUNK_BATCH, main_step, (jnp.zeros((1, HEAD_DIM), F32),) * 2)
        for row, (dlam, lg) in enumerate(zip(dlams, lgs)):
            small_ref[0, pl.ds(row, 1), :] = jnp.broadcast_to(jnp.sum(dlam, axis=-1, keepdims=True) * lg, (1, HEAD_DIM))
        small_ref[0, pl.ds(2, 6), :] = jnp.zeros((6, HEAD_DIM), F32)

    hspec = pl.BlockSpec((S, HEAD_DIM), lambda h: (0, h))
    smem = pl.BlockSpec(memory_space=pltpu.SMEM)
    return pl.pallas_call(
        body, grid=(H,), name="ret_bwd",
        in_specs=[smem, smem] + _head_specs(S, (3, 4, 5), H) + [hspec],
        out_specs=[hspec, hspec, hspec, pl.BlockSpec((1, 8, HEAD_DIM), lambda h: (h, 0, 0))],
        out_shape=[SDS((S, H * HEAD_DIM), F32)] * 3 + [SDS((H, 8, HEAD_DIM), F32)],
        scratch_shapes=[pltpu.VMEM((2, nc, HEAD_DIM, HEAD_DIM), F32), pltpu.VMEM((2, nc, HEAD_DIM, HEAD_DIM), F32)],
        compiler_params=_cp(1))(dec_f, dec_b, proj, proj, proj, d_out)


def _ffn_bwd_act(dh2, wd, g, u):
    S, D = dh2.shape
    nblk, _, FB = g.shape
    tm = min(512, S)

    def body(dh_ref, wd_ref, g_ref, u_ref, dg_ref, du_ref):
        dact = _dot(dh_ref[...], wd_ref[...], _NT)
        gg = g_ref[0]
        sg = _sigmoid(gg)
        dg_ref[0] = (dact * u_ref[0] * (sg * (1.0 + gg * (1.0 - sg)))).astype(BF16)
        du_ref[0] = (dact * (gg * sg)).astype(BF16)

    blk = pl.BlockSpec((1, tm, FB), lambda j, i: (j, i, 0))
    return pl.pallas_call(
        body, grid=(nblk, S // tm), name="ffn_bwd_act",
        in_specs=[pl.BlockSpec((tm, D), lambda j, i: (i, 0)), pl.BlockSpec((FB, D), lambda j, i: (j, 0)), blk, blk],
        out_specs=[blk, blk], out_shape=[SDS((nblk, S, FB), BF16)] * 2,
        compiler_params=_cp(2))(dh2, wd, g, u)


def _ffn_bwd_in(dg, du, wg, wu, h1, dh2, w_norm):
    nblk, S, FB = dg.shape
    D = h1.shape[1]
    tm = min(512, S)

    def body(dg_ref, du_ref, wg_ref, wu_ref, h_hbm, dh2_hbm, wn_ref, dh_ref, dhb_ref, dw_ref, acc, h_buf, dh2_buf, sems):
        i, j = pl.program_id(0), pl.program_id(1)
        rows = pl.ds(pl.multiple_of(i * tm, tm), tm)
        fetch = [pltpu.make_async_copy(h_hbm.at[rows, :], h_buf, sems.at[0]),
                 pltpu.make_async_copy(dh2_hbm.at[rows, :], dh2_buf, sems.at[1])]

        @pl.when(j == 0)
        def _():
            for cp in fetch:
                cp.start()
            acc[...] = jnp.zeros_like(acc)

        @pl.when((i == 0) & (j == 0))
        def _():
            dw_ref[...] = jnp.zeros_like(dw_ref)

        acc[...] += _dot(dg_ref[0], wg_ref[0], _NN) + _dot(du_ref[0], wu_ref[0], _NN)

        @pl.when(j == nblk - 1)
        def _():
            for cp in fetch:
                cp.wait()
            dh, dw = _rms_bwd(acc[...], h_buf[...], wn_ref[...])
            dh = dh2_buf[...] + dh
            dh_ref[...] = dh
            dhb_ref[...] = dh.astype(BF16)
            dw_ref[...] += dw

    blk = pl.BlockSpec((1, tm, FB), lambda i, j: (j, i, 0))
    wspec = pl.BlockSpec((1, FB, D), lambda i, j: (j, 0, 0))
    row = pl.BlockSpec((tm, D), lambda i, j: (i, 0))
    vec = pl.BlockSpec((1, D), lambda i, j: (0, 0))
    return pl.pallas_call(
        body, grid=(S // tm, nblk), name="ffn_bwd_in",
        in_specs=[blk, blk, wspec, wspec, ANY, ANY, vec],
        out_specs=[row, row, vec], out_shape=[SDS((S, D), F32), SDS((S, D), BF16), SDS((1, D), F32)],
        scratch_shapes=[pltpu.VMEM((tm, D), F32), pltpu.VMEM((tm, D), F32), pltpu.VMEM((tm, D), F32),
                        pltpu.SemaphoreType.DMA((2,))],
        compiler_params=_cp(2))(dg, du, wg, wu, h1, dh2, w_norm)


def _dmix(dh1, w_out):
    S, D = dh1.shape
    tm = min(512, S)

    def body(dh_ref, w_ref, o_ref):
        o_ref[...] = _dot(dh_ref[...], w_ref[...], _NT)

    row = pl.BlockSpec((tm, D), lambda i: (i, 0))
    return pl.pallas_call(
        body, grid=(S // tm,), name="dmix", in_specs=[row, pl.BlockSpec((D, D), lambda i: (0, 0))],
        out_specs=row, out_shape=SDS((S, D), F32), compiler_params=_cp(1))(dh1, w_out)


def _in_bwd(dproj, w_blk, x, dh1, w_norm):
    S, D = x.shape
    nblk, _, NB = w_blk.shape
    tm = min(512, S)

    def body(dp_ref, w_ref, x_ref, dh1_ref, wn_ref, dx_ref, dw_ref, acc):
        i, j = pl.program_id(0), pl.program_id(1)

        @pl.when(j == 0)
        def _():
            acc[...] = jnp.zeros_like(acc)

        @pl.when((i == 0) & (j == 0))
        def _():
            dw_ref[...] = jnp.zeros_like(dw_ref)

        acc[...] += _dot(dp_ref[...], w_ref[0], _NT)

        @pl.when(j == nblk - 1)
        def _():
            dh, dw = _rms_bwd(acc[...], x_ref[...], wn_ref[...])
            dx_ref[...] = dh1_ref[...] + dh
            dw_ref[...] += dw

    row = pl.BlockSpec((tm, D), lambda i, j: (i, 0))
    vec = pl.BlockSpec((1, D), lambda i, j: (0, 0))
    return pl.pallas_call(
        body, grid=(S // tm, nblk), name="in_bwd",
        in_specs=[pl.BlockSpec((tm, NB), lambda i, j: (i, j)), pl.BlockSpec((1, D, NB), lambda i, j: (j, 0, 0)),
                  row, row, vec],
        out_specs=[row, vec], out_shape=[SDS((S, D), F32), SDS((1, D), F32)],
        scratch_shapes=[pltpu.VMEM((tm, D), F32)], compiler_params=_cp(2))(dproj, w_blk, x, dh1, w_norm)


def _wgrad(a, b, a_spec, b_spec, o_spec, o_shape, grid, name):
    nk = grid[-1]

    def ld(ref):
        return ref[0] if len(ref.shape) == 3 else ref[...]

    def body(a_ref, b_ref, o_ref, acc):
        k = pl.program_id(len(grid) - 1)

        @pl.when(k == 0)
        def _():
            acc[...] = jnp.zeros_like(acc)

        acc[...] += _dot(ld(a_ref), ld(b_ref), _TN)

        @pl.when(k == nk - 1)
        def _():
            if len(o_ref.shape) == 3:
                o_ref[0] = acc[...].astype(o_ref.dtype)
            else:
                o_ref[...] = acc[...].astype(o_ref.dtype)

    return pl.pallas_call(
        body, grid=grid, name=name, in_specs=[a_spec, b_spec], out_specs=o_spec, out_shape=SDS(o_shape, BF16),
        scratch_shapes=[pltpu.VMEM(o_spec.block_shape[-2:], F32)], compiler_params=_cp(len(grid)))(a, b)


def _peer(k):
    x, y, c = lax.axis_index("x"), lax.axis_index("y"), lax.axis_index("c")
    px = 1 - x if k & 4 else x
    py = 1 - y if k & 2 else y
    pc = 1 - c if k & 1 else c
    return (px, py, pc), 4 * px + 2 * py + pc


def _exchange_copies(srcs, lands, send_sems, recv_sems, which, gather):
    _, me = _peer(0)
    pairs = []
    for pos, a in enumerate(which):
        for k in range(1, N_DEV):
            dev, idx = _peer(k)
            sem = pos * (N_DEV - 1) + k - 1
            src = srcs[a] if gather else srcs[a].at[idx]
            mk = functools.partial(pltpu.make_async_remote_copy, src_ref=src, send_sem=send_sems.at[sem],
                                   recv_sem=recv_sems.at[sem], device_id=dev, device_id_type=MESH)
            pairs.append((mk(dst_ref=lands[a].at[me]), mk(dst_ref=lands[a].at[idx])))
    return pairs


def _sequencer_kernel(name, collective_id, n_remote, n_local):
    return pl.kernel(mesh=plsc.ScalarSubcoreMesh(axis_name="sequencer", num_cores=1), name=name,
                     scratch_types=(pltpu.SemaphoreType.DMA((n_remote,)), pltpu.SemaphoreType.DMA((n_remote,)),
                                    pltpu.SemaphoreType.DMA((n_local,))),
                     compiler_params=pltpu.CompilerParams(collective_id=collective_id))


def _handshake(ks):
    barrier = pltpu.get_barrier_semaphore()
    for k in ks:
        pl.semaphore_signal(barrier, inc=1, device_id=_peer(k)[0], device_id_type=MESH)
    pl.semaphore_wait(barrier, len(ks))


def _sequencer_scatter(arrays, name, collective_id):
    n = len(arrays)
    hbm = pltpu.MemorySpace.HBM
    srcs = [jax.new_ref(a, memory_space=hbm) for a in arrays]
    lands = [jax.empty_ref(SDS(a.shape, a.dtype), memory_space=hbm) for a in arrays]

    @_sequencer_kernel(name, collective_id, n * (N_DEV - 1), n)
    def launch(send_sems, recv_sems, local_sems):
        _handshake(range(1, N_DEV))
        _, me = _peer(0)
        local = [pltpu.make_async_copy(srcs[a].at[me], lands[a].at[me], local_sems.at[a]) for a in range(n)]
        pairs = _exchange_copies(srcs, lands, send_sems, recv_sems, range(n), False)
        for out, _ in pairs:
            out.start()
        for cp in local:
            cp.start()
        for out, arrival in pairs:
            out.wait_send()
            arrival.wait_recv()
        for cp in local:
            cp.wait()

    launch()
    return [r[...] for r in lands]


SIBLING = 1
OTHER_CHIPS = (2, 4, 6)


def _sequencer_gather(arrays, name, collective_id):
    n = len(arrays)
    hbm = pltpu.MemorySpace.HBM
    srcs = [jax.new_ref(a, memory_space=hbm) for a in arrays]
    lands = [jax.empty_ref(SDS((N_DEV,) + a.shape, a.dtype), memory_space=hbm) for a in arrays]

    @_sequencer_kernel(name, collective_id, n * (N_DEV - 1), n)
    def launch(send_sems, recv_sems, local_sems):
        _handshake((SIBLING,) + OTHER_CHIPS)
        _, me = _peer(0)
        sibling, _ = _peer(SIBLING)

        def copy(a, k, src, block, to):
            sem = a * (N_DEV - 1) + k - 1
            return pltpu.make_async_remote_copy(src_ref=src, dst_ref=lands[a].at[block], send_sem=send_sems.at[sem],
                                                recv_sem=recv_sems.at[sem], device_id=to, device_id_type=MESH)

        local = [pltpu.make_async_copy(srcs[a], lands[a].at[me], local_sems.at[a]) for a in range(n)]
        first = [copy(a, k, srcs[a], me, _peer(k)[0]) for a in range(n) for k in OTHER_CHIPS + (SIBLING,)]
        for cp in first + local:
            cp.start()
        passed = []
        for a in range(n):
            for k in OTHER_CHIPS:
                _, block = _peer(k)
                copy(a, k, srcs[a], block, sibling).wait_recv()
                passed.append(copy(a, k ^ SIBLING, lands[a].at[block], block, sibling))
                passed[-1].start()
        for a in range(n):
            for k in (SIBLING,) + tuple(k ^ SIBLING for k in OTHER_CHIPS):
                copy(a, k, srcs[a], _peer(k)[1], sibling).wait_recv()
        for cp in first + passed:
            cp.wait_send()
        for cp in local:
            cp.wait()

    launch()
    return [r[...] for r in lands]


SMALL_ROWS = 64


def _small_step(part, w, m, v):
    def body(p_ref, w_ref, m_ref, v_ref, g_ref, d_ref, nm_ref, nv_ref, gath, send_sems, recv_sems):
        _, me = _peer(0)
        gath[me] = p_ref[...]
        copies = []
        for k in range(1, N_DEV):
            dev, idx = _peer(k)
            out = pltpu.make_async_remote_copy(src_ref=p_ref, dst_ref=gath.at[me], send_sem=send_sems.at[k - 1],
                                               recv_sem=recv_sems.at[k - 1], device_id=dev, device_id_type=MESH)
            out.start()
            arrival = pltpu.make_async_remote_copy(src_ref=p_ref, dst_ref=gath.at[idx], send_sem=send_sems.at[k - 1],
                                                   recv_sem=recv_sems.at[k - 1], device_id=dev, device_id_type=MESH)
            copies.append((out, arrival))
        for out, arrival in copies:
            out.wait_send()
            arrival.wait_recv()
        g = gath[0]
        for p in range(1, N_DEV):
            g = g + gath[p]
        g_ref[...] = g
        d_ref[...], nm_ref[...], nv_ref[...] = _adamw(w_ref[...], g, m_ref[...], v_ref[...])

    vm = pl.BlockSpec(memory_space=pltpu.VMEM)
    return pl.pallas_call(
        body, name="small_step", in_specs=[vm] * 4, out_specs=[vm] * 4,
        out_shape=[SDS((SMALL_ROWS, 128), F32)] * 4,
        scratch_shapes=[pltpu.VMEM((N_DEV, SMALL_ROWS, 128), F32), pltpu.SemaphoreType.DMA((N_DEV - 1,)),
                        pltpu.SemaphoreType.DMA((N_DEV - 1,))])(part, w, m, v)


def _adamw(w, g, m, v):
    m = ADAM_B1 * m + (1.0 - ADAM_B1) * g
    v = ADAM_B2 * v + (1.0 - ADAM_B2) * (g * g)
    m_hat = m / (1.0 - ADAM_B1 ** ADAM_STEP)
    v_hat = v / (1.0 - ADAM_B2 ** ADAM_STEP)
    delta = -ADAM_LR * (m_hat / (jnp.sqrt(v_hat) + ADAM_EPS) + ADAM_WD * w)
    return delta, m, v


def _adamw_block(parts, w, m, v, name):
    R, C = w.shape
    tr = next(t for t in (256, 128, 64, 32, 16, 8) if R % t == 0 and t * C <= 256 * 1024)

    def body(p_ref, w_ref, m_ref, v_ref, g_ref, d_ref, nm_ref, nv_ref):
        g = p_ref[0].astype(F32)
        for p in range(1, N_DEV):
            g = g + p_ref[p].astype(F32)
        g_ref[...] = g
        d_ref[...], nm_ref[...], nv_ref[...] = _adamw(w_ref[...], g, m_ref[...], v_ref[...])

    row = pl.BlockSpec((tr, C), lambda i: (i, 0))
    return pl.pallas_call(
        body, grid=(R // tr,), name=name, in_specs=[pl.BlockSpec((N_DEV, tr, C), lambda i: (0, i, 0)), row, row, row],
        out_specs=[row] * 4, out_shape=[SDS((R, C), F32)] * 4, compiler_params=_cp(1))(parts, w, m, v)


def _pack_small(mix, ffn, fin, retw, dec_f, dec_b, loss):
    flat = jnp.concatenate([mix.reshape(-1), ffn.reshape(-1), fin.reshape(-1), retw.reshape(-1), dec_f.reshape(-1),
                            dec_b.reshape(-1), loss.reshape(-1)])
    return jnp.pad(flat, (0, SMALL_ROWS * 128 - flat.shape[0])).reshape(SMALL_ROWS, 128)


def _unpack_small(packed, shapes):
    flat = packed.reshape(-1)
    out, at = [], 0
    for s in shapes:
        n = math.prod(s)
        out.append(flat[at:at + n].reshape(s))
        at += n
    return out


def kernel(x, norm_mix_w, w_in, ret_decay_fwd, ret_decay_bwd, ret_norm_w, w_out, norm_ffn_w, w_gate, w_up, w_down, norm_final_w, loss_target, m_norm_mix_w, m_w_in, m_ret_decay_fwd, m_ret_decay_bwd, m_ret_norm_w, m_w_out, m_norm_ffn_w, m_w_gate, m_w_up, m_w_down, m_norm_final_w, v_norm_mix_w, v_w_in, v_ret_decay_fwd, v_ret_decay_bwd, v_ret_norm_w, v_w_out, v_norm_ffn_w, v_w_gate, v_w_up, v_w_down, v_norm_final_w):
    x2 = x[0]
    tgt = loss_target[0]
    S, D = x2.shape
    H = ret_norm_w.shape[1] // HEAD_DIM
    DA = H * HEAD_DIM
    fin_w = norm_final_w.reshape(1, D)
    big = (w_in[0], w_out[0], w_gate[0].T, w_up[0].T, w_down[0])

    big_b = [w.astype(BF16) for w in big]
    wi, = _sequencer_gather(big_b[:1], "gather_in", 0)
    wo, wg, wu = _sequencer_gather(big_b[1:4], "gather_mid", 1)
    wd, = _sequencer_gather(big_b[4:], "gather_down", 5)
    NB = wi.shape[2]

    proj, n1 = _proj_fwd(x2, norm_mix_w, wi)
    bias = _attn_bias()[:H]
    attn, lse = _attn_fwd(proj, bias)
    ret, o_raw = _ret_fwd(proj, ret_decay_fwd, ret_decay_bwd, ret_norm_w)
    wo_full = wo.reshape(D, D)
    FB = wd.shape[1]
    wd_full = wd.reshape(N_DEV * FB, D)
    h1, mixed, n2 = _out_fwd(x2, attn, ret, wo_full, norm_ffn_w)
    gate, up, act = _ffn_up(n2, wg, wu)
    dh2, dh2_b, loss_parts, g_fin = _ffn_down_loss(act, wd_full, h1, tgt, fin_w)

    dgate, dup = _ffn_bwd_act(dh2_b, wd_full, gate, up)
    tn = min(1024, D)
    ffn_specs = (pl.BlockSpec((1, S, FB), lambda j, n, k: (j, 0, 0)), pl.BlockSpec((S, tn), lambda j, n, k: (0, n)),
                 pl.BlockSpec((1, FB, tn), lambda j, n, k: (j, 0, n)), (N_DEV, FB, D), (N_DEV, D // tn, 1))
    g_wd = _wgrad(act, dh2_b, *ffn_specs, "wgrad_down")
    g_wg = _wgrad(dgate, n2, *ffn_specs, "wgrad_gate")
    g_wu = _wgrad(dup, n2, *ffn_specs, "wgrad_up")
    parts_f = _sequencer_scatter([g_wg, g_wu, g_wd], "scatter_ffn", 2)
    dh1, dh1_b, g_ffn = _ffn_bwd_in(dgate, dup, wg, wu, h1, dh2, norm_ffn_w)
    dmix = _dmix(dh1_b, wo_full)
    tmw = min(512, D)
    tk = min(2048, S)
    g_wo = _wgrad(mixed, dh1_b, pl.BlockSpec((tk, tmw), lambda m, k: (k, m)), pl.BlockSpec((tk, D), lambda m, k: (k, 0)),
                  pl.BlockSpec((tmw, D), lambda m, k: (m, 0)), (D, D), (D // tmw, S // tk), "wgrad_out")
    parts_o = _sequencer_scatter([g_wo.reshape(N_DEV, D // N_DEV, D)], "scatter_out", 3)
    d_ret, dg_r, small_w = _ret_gate_bwd(proj, o_raw, dmix, ret_norm_w, DA)
    dq_r, dk_r, dv_r, small = _ret_bwd(proj, d_ret, ret_decay_fwd, ret_decay_bwd)
    dq_a, dk_a, dv_a = _attn_bwd(proj, attn, lse, dmix, bias)
    dproj = jnp.concatenate([t.astype(BF16) for t in (dq_a, dk_a, dv_a, dq_r, dk_r, dv_r, dg_r)], axis=1)
    g_wi = _wgrad(n1, dproj, pl.BlockSpec((S, tmw), lambda j, m, k: (0, m)), pl.BlockSpec((S, NB), lambda j, m, k: (0, j)),
                  pl.BlockSpec((1, tmw, NB), lambda j, m, k: (j, m, 0)), (N_DEV, D, NB), (N_DEV, D // tmw, 1), "wgrad_in")
    parts_i = _sequencer_scatter([g_wi], "scatter_in", 4)
    grad_x, g_mix = _in_bwd(dproj, wi, x2, dh1, norm_mix_w)

    big_m = (m_w_in[0], m_w_out[0], m_w_gate[0].T, m_w_up[0].T, m_w_down[0])
    big_v = (v_w_in[0], v_w_out[0], v_w_gate[0].T, v_w_up[0].T, v_w_down[0])
    names = ("adamw_in", "adamw_out", "adamw_gate", "adamw_up", "adamw_down")
    upd = [None] * 5
    for a, p in zip((2, 3, 4, 1, 0), parts_f + parts_o + parts_i):
        upd[a] = _adamw_block(p, big[a], big_m[a], big_v[a], names[a])

    g_dec_f = small[:, 0, 0].reshape(1, H)
    g_dec_b = small[:, 1, 0].reshape(1, H)
    g_retw = small_w[:, 0, :].reshape(1, DA)
    loss_local = jnp.sum(loss_parts[::8, 0])
    zero = jnp.zeros((1,), F32)
    part = _pack_small(g_mix, g_ffn, g_fin, g_retw, g_dec_f, g_dec_b, loss_local)
    sw = _pack_small(norm_mix_w, norm_ffn_w, norm_final_w, ret_norm_w, ret_decay_fwd, ret_decay_bwd, zero)
    sm = _pack_small(m_norm_mix_w, m_norm_ffn_w, m_norm_final_w, m_ret_norm_w, m_ret_decay_fwd, m_ret_decay_bwd, zero)
    sv = _pack_small(v_norm_mix_w, v_norm_ffn_w, v_norm_final_w, v_ret_norm_w, v_ret_decay_fwd, v_ret_decay_bwd, zero)
    shapes = [(1, D), (1, D), (D,), (1, DA), (1, H), (1, H), ()]
    sg, sd, snm, snv = [_unpack_small(t, shapes) for t in _small_step(part, sw, sm, sv)]
    loss = sg[6]

    def ordered(small_set, k):
        b = [(u[k].T if a in (2, 3) else u[k])[None] for a, u in enumerate(upd)]
        return [small_set[0], b[0], small_set[4], small_set[5], small_set[3], b[1], small_set[1], b[2], b[3], b[4],
                small_set[2]]

    return (loss, grad_x[None], *ordered(sg, 0), *ordered(sd, 1), *ordered(snm, 2), *ordered(snv, 3))
```

```python
import functools
import math

import numpy as np
import jax
import jax.numpy as jnp
from jax import lax
from jax.experimental import pallas as pl
from jax.experimental.pallas import tpu as pltpu
from jax.experimental.pallas import tpu_sc as plsc

F32 = jnp.float32
BF16 = jnp.bfloat16
SDS = jax.ShapeDtypeStruct

HEAD_DIM = 128
EPS = 1e-6
RET_CHUNK = 128
DILATIONS = (1, 4, 16)
BAND = 64
Q_TILE = 128
K_TILE = Q_TILE + 2 * BAND
KV_PAD = BAND * max(DILATIONS)
TILE_GROUP = 8
NEG = -1e30
N_DEV = 8
N_GROUPS = 7
ADAM_LR, ADAM_B1, ADAM_B2, ADAM_EPS, ADAM_WD, ADAM_STEP = 0.001, 0.9, 0.999, 1e-08, 0.01, 10
VMEM_LIMIT = 56 * 1024 * 1024
MESH = pl.DeviceIdType.MESH
ANY = pl.BlockSpec(memory_space=pl.ANY)


def _cp(n_grid):
    return pltpu.CompilerParams(dimension_semantics=("arbitrary",) * n_grid, vmem_limit_bytes=VMEM_LIMIT)


def _sigmoid(x):
    return 1.0 / (1.0 + jnp.exp(-x))


def _rms_scale(h):
    return lax.rsqrt(jnp.mean(h * h, axis=-1, keepdims=True) + EPS)


def _rms_bwd(dn, h, w):
    r = _rms_scale(h)
    gw = dn * w
    dh = r * gw - h * (r * r * r) * jnp.mean(gw * h, axis=-1, keepdims=True)
    return dh, jnp.sum(dn * h * r, axis=0, keepdims=True)


def _dot(a, b, dims):
    return lax.dot_general(a.astype(BF16), b.astype(BF16), (dims, ((), ())), preferred_element_type=F32)


_NN = ((1,), (0,))
_NT = ((1,), (1,))
_TN = ((0,), (0,))


def _proj_fwd(x, w_norm, w_blk):
    S, D = x.shape
    nblk, _, NB = w_blk.shape
    tm = min(1024, S)

    def body(x_ref, wn_ref, w_ref, proj_ref, n_ref, n_scr):
        @pl.when(pl.program_id(1) == 0)
        def _():
            xf = x_ref[...]
            nb = (xf * _rms_scale(xf) * wn_ref[...]).astype(BF16)
            n_scr[...] = nb
            n_ref[...] = nb
        proj_ref[...] = jnp.dot(n_scr[...], w_ref[0], preferred_element_type=F32)

    return pl.pallas_call(
        body, grid=(S // tm, nblk), name="proj_fwd",
        in_specs=[pl.BlockSpec((tm, D), lambda i, j: (i, 0)), pl.BlockSpec((1, D), lambda i, j: (0, 0)),
                  pl.BlockSpec((1, D, NB), lambda i, j: (j, 0, 0))],
        out_specs=[pl.BlockSpec((tm, NB), lambda i, j: (i, j)), pl.BlockSpec((tm, D), lambda i, j: (i, 0))],
        out_shape=[SDS((S, nblk * NB), F32), SDS((S, D), BF16)],
        scratch_shapes=[pltpu.VMEM((tm, D), BF16)], compiler_params=_cp(2))(x, w_norm, w_blk)


def _out_fwd(x, attn, ret, w_out, w_norm):
    S, D = x.shape
    DA = attn.shape[1]
    tm = min(256, S)

    def body(x_ref, a_ref, r_ref, w_ref, wn_ref, h_ref, mix_ref, n_ref):
        a = a_ref[...].astype(BF16)
        r = r_ref[...].astype(BF16)
        mix_ref[:, :DA] = a
        mix_ref[:, DA:] = r
        h = x_ref[...] + jnp.dot(a, w_ref[:DA, :], preferred_element_type=F32) \
            + jnp.dot(r, w_ref[DA:, :], preferred_element_type=F32)
        h_ref[...] = h
        n_ref[...] = (h * _rms_scale(h) * wn_ref[...]).astype(BF16)

    row = lambda w: pl.BlockSpec((tm, w), lambda i: (i, 0))
    return pl.pallas_call(
        body, grid=(S // tm,), name="out_fwd",
        in_specs=[row(D), row(DA), row(D - DA), pl.BlockSpec((D, D), lambda i: (0, 0)),
                  pl.BlockSpec((1, D), lambda i: (0, 0))],
        out_specs=[row(D), row(D), row(D)],
        out_shape=[SDS((S, D), F32), SDS((S, D), BF16), SDS((S, D), BF16)],
        compiler_params=_cp(1))(x, attn, ret, w_out, w_norm)


def _ffn_up(n2, wg, wu):
    S, D = n2.shape
    nblk, FB, _ = wg.shape
    tm = min(512, S)

    def body(n_ref, wg_ref, wu_ref, g_ref, u_ref, a_ref):
        n = n_ref[...]
        g = _dot(n, wg_ref[0], _NT)
        u = _dot(n, wu_ref[0], _NT)
        g_ref[0] = g.astype(BF16)
        u_ref[0] = u.astype(BF16)
        a_ref[0] = (g * _sigmoid(g) * u).astype(BF16)

    wspec = pl.BlockSpec((1, FB, D), lambda j, i: (j, 0, 0))
    ospec = pl.BlockSpec((1, tm, FB), lambda j, i: (j, i, 0))
    return pl.pallas_call(
        body, grid=(nblk, S // tm), name="ffn_up",
        in_specs=[pl.BlockSpec((tm, D), lambda j, i: (i, 0)), wspec, wspec],
        out_specs=[ospec, ospec, ospec],
        out_shape=[SDS((nblk, S, FB), BF16)] * 3,
        compiler_params=_cp(2))(n2, wg, wu)


def _ffn_down_loss(act, wd, h1, target, w_norm):
    nblk, S, FB = act.shape
    D = h1.shape[1]
    tm = min(512, S)

    def body(a_ref, wd_ref, h_ref, t_ref, wn_ref, dh_ref, dhb_ref, loss_ref, dw_ref, acc):
        i, j = pl.program_id(0), pl.program_id(1)

        @pl.when(j == 0)
        def _():
            acc[...] = h_ref[...]

        @pl.when((i == 0) & (j == 0))
        def _():
            dw_ref[...] = jnp.zeros_like(dw_ref)

        acc[...] += jnp.dot(a_ref[0], wd_ref[...], preferred_element_type=F32)

        @pl.when(j == nblk - 1)
        def _():
            h = acc[...]
            w = wn_ref[...]
            err = h * _rms_scale(h) * w - t_ref[...]
            loss_ref[...] = jnp.full(loss_ref.shape, 0.5 * jnp.sum(err * err) / D, F32)
            dh, dw = _rms_bwd(err * (1.0 / D), h, w)
            dh_ref[...] = dh
            dhb_ref[...] = dh.astype(BF16)
            dw_ref[...] += dw

    row = pl.BlockSpec((tm, D), lambda i, j: (i, 0))
    vec = pl.BlockSpec((1, D), lambda i, j: (0, 0))
    return pl.pallas_call(
        body, grid=(S // tm, nblk), name="ffn_down_loss",
        in_specs=[pl.BlockSpec((1, tm, FB), lambda i, j: (j, i, 0)), pl.BlockSpec((FB, D), lambda i, j: (j, 0)),
                  row, row, vec],
        out_specs=[row, row, pl.BlockSpec((8, 128), lambda i, j: (i, 0)), vec],
        out_shape=[SDS((S, D), F32), SDS((S, D), BF16), SDS((S // tm * 8, 128), F32), SDS((1, D), F32)],
        scratch_shapes=[pltpu.VMEM((tm, D), F32)], compiler_params=_cp(2))(act, wd, h1, target, w_norm)


def _attn_bias():
    n_heads = 8
    slopes = np.exp2(-8.0 * np.arange(1, n_heads + 1, dtype=np.float32) / n_heads)
    dist = np.abs(np.arange(K_TILE)[None, :] - BAND - np.arange(Q_TILE)[:, None])
    out = np.empty((n_heads, len(DILATIONS), Q_TILE, K_TILE), np.float32)
    for h in range(n_heads):
        for p, d in enumerate(DILATIONS):
            out[h, p] = np.where(dist <= BAND, -slopes[h] * (d * dist).astype(np.float32), NEG)
    return jnp.asarray(out)


def _attn_tiles(S, d):
    L = S // d
    per_class = L // Q_TILE
    return L, per_class, d * per_class


def _tile_rows(t, d, per_class):
    r = t // per_class
    a = (t % per_class) * Q_TILE
    q_rows = pl.ds(r + d * a, Q_TILE, stride=d) if d > 1 else pl.ds(pl.multiple_of(a, Q_TILE), Q_TILE)
    k_rows = pl.ds(KV_PAD + r + d * (a - BAND), K_TILE, stride=d) if d > 1 else pl.ds(
        pl.multiple_of(KV_PAD + a - BAND, BAND), K_TILE)
    return a, q_rows, k_rows


def _lanes(x, width):
    return jnp.concatenate([x] * (width // HEAD_DIM), axis=-1)


_BNT = (((2,), (2,)), ((0,), (0,)))
_BNN = (((2,), (1,)), ((0,), (0,)))
_BTN = (((1,), (1,)), ((0,), (0,)))


def _bdot(a, b, dims):
    return lax.dot_general(a, b, dims, preferred_element_type=F32)


def _stacked(rows, loaders):
    return [jnp.stack([f(*r) for r in rows]) for f in loaders]


def _edge_mask(a, L):
    lk = lax.broadcasted_iota(jnp.int32, (1, K_TILE), 1) + (a - BAND)
    return jnp.where((lk >= 0) & (lk < L), 0.0, NEG).astype(F32)


def _fill_padded(dst, src, S):
    dst[pl.ds(0, KV_PAD), :] = jnp.zeros((KV_PAD, HEAD_DIM), F32)
    dst[pl.ds(KV_PAD + S, KV_PAD), :] = jnp.zeros((KV_PAD, HEAD_DIM), F32)
    dst[pl.ds(KV_PAD, S), :] = src[...]


def _head_specs(S, groups, n_heads):
    return [pl.BlockSpec((S, HEAD_DIM), functools.partial(lambda h, g: (0, g * n_heads + h), g=g)) for g in groups]


def _attn_fwd(proj, bias):
    S = proj.shape[0]
    H = proj.shape[1] // (N_GROUPS * HEAD_DIM)
    scale = HEAD_DIM ** -0.5

    def body(q_ref, k_ref, v_ref, b_ref, o_ref, lse_ref, kp, vp, m_run, l_run):
        _fill_padded(kp, k_ref, S)
        _fill_padded(vp, v_ref, S)
        o_ref[...] = jnp.zeros_like(o_ref)
        m_run[...] = jnp.full(m_run.shape, NEG, F32)
        l_run[...] = jnp.zeros_like(l_run)
        for p, d in enumerate(DILATIONS):
            L, per_class, n_tiles = _attn_tiles(S, d)

            def tiles(t, carry, p=p, d=d, L=L, per_class=per_class, n_tiles=n_tiles):
                rows = [_tile_rows(t + u * (n_tiles // TILE_GROUP), d, per_class) for u in range(TILE_GROUP)]
                qs, ks, vs, m_old, l_old, o_old, edge = _stacked(rows, (
                    lambda a, qr, kr: q_ref[qr, :].astype(BF16), lambda a, qr, kr: kp[kr, :].astype(BF16),
                    lambda a, qr, kr: vp[kr, :].astype(BF16), lambda a, qr, kr: m_run[qr, :],
                    lambda a, qr, kr: l_run[qr, :], lambda a, qr, kr: o_ref[qr, :], lambda a, qr, kr: _edge_mask(a, L)))
                s = _bdot(qs, ks, _BNT) * scale + b_ref[0, p][None] + edge
                m_new = jnp.maximum(m_old, jnp.max(s, axis=-1, keepdims=True))
                pr = jnp.exp(s - _lanes(m_new, K_TILE)).astype(BF16)
                alpha = jnp.exp(m_old - m_new)
                l_new = alpha * l_old + _bdot(pr, jnp.ones((TILE_GROUP, K_TILE, HEAD_DIM), BF16), _BNN)
                o_new = alpha * o_old + _bdot(pr, vs, _BNN)
                for u, (_, qr, _) in enumerate(rows):
                    o_ref[qr, :] = o_new[u]
                    m_run[qr, :] = m_new[u]
                    l_run[qr, :] = l_new[u]
                return carry

            lax.fori_loop(0, n_tiles // TILE_GROUP, tiles, 0)
        l = l_run[...]
        o_ref[...] = o_ref[...] / l
        lse_ref[...] = m_run[...] + jnp.log(l)

    hspec = pl.BlockSpec((S, HEAD_DIM), lambda h: (0, h))
    return pl.pallas_call(
        body, grid=(H,), name="attn_fwd",
        in_specs=_head_specs(S, (0, 1, 2), H) + [
            pl.BlockSpec((1, len(DILATIONS), Q_TILE, K_TILE), lambda h: (h, 0, 0, 0))],
        out_specs=[hspec, hspec],
        out_shape=[SDS((S, H * HEAD_DIM), F32), SDS((S, H * HEAD_DIM), F32)],
        scratch_shapes=[pltpu.VMEM((S + 2 * KV_PAD, HEAD_DIM), F32), pltpu.VMEM((S + 2 * KV_PAD, HEAD_DIM), F32),
                        pltpu.VMEM((S, HEAD_DIM), F32), pltpu.VMEM((S, HEAD_DIM), F32)],
        compiler_params=_cp(1))(proj, proj, proj, bias)


def _attn_bwd(proj, out, lse, dmix, bias):
    S = proj.shape[0]
    H = proj.shape[1] // (N_GROUPS * HEAD_DIM)
    scale = HEAD_DIM ** -0.5

    def body(q_ref, k_ref, v_ref, o_ref, lse_ref, do_ref, b_ref, dq_ref, dk_ref, dv_ref, kp, vp, dkp, dvp, dsum):
        _fill_padded(kp, k_ref, S)
        _fill_padded(vp, v_ref, S)
        dkp[...] = jnp.zeros_like(dkp)
        dvp[...] = jnp.zeros_like(dvp)
        dq_ref[...] = jnp.zeros_like(dq_ref)
        dsum[...] = jnp.broadcast_to(jnp.sum(do_ref[...] * o_ref[...], axis=-1, keepdims=True), dsum.shape)
        for p, d in enumerate(DILATIONS):
            L, per_class, n_tiles = _attn_tiles(S, d)

            def tiles(t, carry, p=p, d=d, L=L, per_class=per_class, n_tiles=n_tiles):
                rows = [_tile_rows(t + u * (n_tiles // TILE_GROUP), d, per_class) for u in range(TILE_GROUP)]
                qs, ks, vs, dos, lses, dsums, dq_old, dk_old, dv_old, edge = _stacked(rows, (
                    lambda a, qr, kr: q_ref[qr, :].astype(BF16), lambda a, qr, kr: kp[kr, :].astype(BF16),
                    lambda a, qr, kr: vp[kr, :].astype(BF16), lambda a, qr, kr: do_ref[qr, :].astype(BF16),
                    lambda a, qr, kr: lse_ref[qr, :], lambda a, qr, kr: dsum[qr, :], lambda a, qr, kr: dq_ref[qr, :],
                    lambda a, qr, kr: dkp[kr, :], lambda a, qr, kr: dvp[kr, :], lambda a, qr, kr: _edge_mask(a, L)))
                s = _bdot(qs, ks, _BNT) * scale + b_ref[0, p][None] + edge
                pr = jnp.exp(s - _lanes(lses, K_TILE))
                ds = (pr * (_bdot(dos, vs, _BNT) - _lanes(dsums, K_TILE)) * scale).astype(BF16)
                dq_new = dq_old + _bdot(ds, ks, _BNN)
                dk_new = dk_old + _bdot(ds, qs, _BTN)
                dv_new = dv_old + _bdot(pr.astype(BF16), dos, _BTN)
                for u, (_, qr, kr) in enumerate(rows):
                    dq_ref[qr, :] = dq_new[u]
                    dkp[kr, :] = dk_new[u]
                    dvp[kr, :] = dv_new[u]
                return carry

            lax.fori_loop(0, n_tiles // TILE_GROUP, tiles, 0)
        dk_ref[...] = dkp[pl.ds(KV_PAD, S), :]
        dv_ref[...] = dvp[pl.ds(KV_PAD, S), :]

    hspec = pl.BlockSpec((S, HEAD_DIM), lambda h: (0, h))
    padded = pltpu.VMEM((S + 2 * KV_PAD, HEAD_DIM), F32)
    return pl.pallas_call(
        body, grid=(H,), name="attn_bwd",
        in_specs=_head_specs(S, (0, 1, 2), H) + [hspec, hspec, hspec,
                                                  pl.BlockSpec((1, len(DILATIONS), Q_TILE, K_TILE), lambda h: (h, 0, 0, 0))],
        out_specs=[hspec, hspec, hspec],
        out_shape=[SDS((S, H * HEAD_DIM), F32)] * 3,
        scratch_shapes=[padded, padded, padded, padded, pltpu.VMEM((S, HEAD_DIM), F32)],
        compiler_params=_cp(1))(proj, proj, proj, out, lse, dmix, bias)


def _ret_consts(lg, forward):
    C = RET_CHUNK
    i = lax.broadcasted_iota(jnp.int32, (C, C), 0)
    j = lax.broadcasted_iota(jnp.int32, (C, C), 1)
    rel = (i - j) if forward else (j - i)
    inside = (rel >= 0) if forward else (rel > 0)
    relf = jnp.maximum(rel, 0).astype(F32)
    mask = jnp.where(inside, jnp.exp(lg * relf), 0.0)
    idx = lax.broadcasted_iota(jnp.int32, (C, 1), 0).astype(F32)
    q_exp = (idx + 1.0) if forward else (C - idx)
    k_exp = (C - 1.0 - idx) if forward else idx
    return mask, relf, jnp.exp(lg * q_exp), q_exp, jnp.exp(lg * k_exp), k_exp, jnp.exp(lg * C)


def _log_decay(dec_ref, h):
    return -jnp.exp(jnp.full((1, 1), dec_ref[0, h], F32))


CHUNK_BATCH = 8


def _batch_rows(b):
    n = CHUNK_BATCH * RET_CHUNK
    return pl.ds(pl.multiple_of(b * n, n), n)


def _batch_chunks(b):
    return pl.ds(pl.multiple_of(b * CHUNK_BATCH, CHUNK_BATCH), CHUNK_BATCH)


def _chunks3(x):
    return x.reshape(CHUNK_BATCH, RET_CHUNK, HEAD_DIM)


def _ret_scan(buf, c_decs, nc, reverse):
    def step(n, carry):
        new = []
        for way, r in enumerate(carry):
            c = n if (way == 0) != reverse else nc - 1 - n
            term = buf[way, c]
            buf[way, c] = r
            new.append(r * c_decs[way] + term)
        return tuple(new)

    lax.fori_loop(0, nc, step, (jnp.zeros((HEAD_DIM, HEAD_DIM), F32),) * 2)


def _ret_fwd(proj, dec_f, dec_b, w_norm):
    S = proj.shape[0]
    H = proj.shape[1] // (N_GROUPS * HEAD_DIM)
    nc = S // RET_CHUNK
    scale = HEAD_DIM ** -0.5

    def body(df_ref, db_ref, q_ref, k_ref, v_ref, g_ref, w_ref, y_ref, o_ref, states):
        h = pl.program_id(0)
        consts = [_ret_consts(_log_decay(dref, h), fw) for fw, dref in ((True, df_ref), (False, db_ref))]

        def kv_step(b, carry):
            rows, batch = _batch_rows(b), _batch_chunks(b)
            k3 = _chunks3(k_ref[rows, :])
            v3 = _chunks3(v_ref[rows, :]).astype(BF16)
            for way in range(2):
                states[way, batch] = _bdot((k3 * consts[way][4]).astype(BF16), v3, _BTN)
            return carry

        lax.fori_loop(0, nc // CHUNK_BATCH, kv_step, 0)
        _ret_scan(states, [c[6] for c in consts], nc, False)

        def out_step(b, carry):
            rows, batch = _batch_rows(b), _batch_chunks(b)
            q3 = _chunks3(q_ref[rows, :] * scale)
            k3 = _chunks3(k_ref[rows, :]).astype(BF16)
            v3 = _chunks3(v_ref[rows, :]).astype(BF16)
            a0 = _bdot(q3.astype(BF16), k3, _BNT)
            o = None
            for way in range(2):
                mask, q_dec = consts[way][0], consts[way][2]
                part = _bdot((a0 * mask).astype(BF16), v3, _BNN) \
                    + _bdot((q3 * q_dec).astype(BF16), states[way, batch].astype(BF16), _BNN)
                o = part if o is None else o + part
            o_ref[rows, :] = o.reshape(CHUNK_BATCH * RET_CHUNK, HEAD_DIM)
            return carry

        lax.fori_loop(0, nc // CHUNK_BATCH, out_step, 0)
        o = o_ref[...]
        g = g_ref[...]
        y_ref[...] = o * _rms_scale(o) * w_ref[...] * (g * _sigmoid(g))

    hspec = pl.BlockSpec((S, HEAD_DIM), lambda h: (0, h))
    smem = pl.BlockSpec(memory_space=pltpu.SMEM)
    return pl.pallas_call(
        body, grid=(H,), name="ret_fwd",
        in_specs=[smem, smem] + _head_specs(S, (3, 4, 5, 6), H) + [pl.BlockSpec((1, HEAD_DIM), lambda h: (0, h))],
        out_specs=[hspec, hspec],
        out_shape=[SDS((S, H * HEAD_DIM), F32)] * 2,
        scratch_shapes=[pltpu.VMEM((2, nc, HEAD_DIM, HEAD_DIM), F32)],
        compiler_params=_cp(1))(dec_f, dec_b, proj, proj, proj, proj, w_norm)


def _ret_gate_bwd(proj, o_raw, dmix, w_norm, col0):
    S = proj.shape[0]
    H = proj.shape[1] // (N_GROUPS * HEAD_DIM)

    def body(g_ref, o_ref, dy_ref, w_ref, do_ref, dg_ref, dw_ref):
        o = o_ref[...]
        g = g_ref[...]
        dy = dy_ref[...]
        w = w_ref[...]
        rr = _rms_scale(o)
        normed = o * rr
        sg = _sigmoid(g)
        silu = g * sg
        dw_ref[0] = jnp.broadcast_to(jnp.sum(dy * normed * silu, axis=0, keepdims=True), (8, HEAD_DIM))
        dg_ref[...] = dy * normed * w * (sg * (1.0 + g * (1.0 - sg)))
        dnormed = dy * w * silu
        do_ref[...] = rr * dnormed - o * (rr * rr * rr) * jnp.mean(dnormed * o, axis=-1, keepdims=True)

    hspec = pl.BlockSpec((S, HEAD_DIM), lambda h: (0, h))
    nh0 = col0 // HEAD_DIM
    return pl.pallas_call(
        body, grid=(H,), name="ret_gate_bwd",
        in_specs=_head_specs(S, (6,), H) + [hspec, pl.BlockSpec((S, HEAD_DIM), lambda h: (0, nh0 + h)),
                                            pl.BlockSpec((1, HEAD_DIM), lambda h: (0, h))],
        out_specs=[hspec, hspec, pl.BlockSpec((1, 8, HEAD_DIM), lambda h: (h, 0, 0))],
        out_shape=[SDS((S, H * HEAD_DIM), F32)] * 2 + [SDS((H, 8, HEAD_DIM), F32)],
        compiler_params=_cp(1))(proj, o_raw, dmix, w_norm)


def _ret_bwd(proj, d_out, dec_f, dec_b):
    S = proj.shape[0]
    H = proj.shape[1] // (N_GROUPS * HEAD_DIM)
    C = RET_CHUNK
    nc = S // C
    scale = HEAD_DIM ** -0.5

    def body(df_ref, db_ref, q_ref, k_ref, v_ref, do, dq_ref, dk_ref, dv_ref, small_ref, states, d_states):
        h = pl.program_id(0)
        lgs = [_log_decay(df_ref, h), _log_decay(db_ref, h)]
        consts = [_ret_consts(lg, fw) for lg, fw in zip(lgs, (True, False))]

        def prep_step(b, carry):
            rows, batch = _batch_rows(b), _batch_chunks(b)
            q3 = _chunks3(q_ref[rows, :] * scale)
            k3 = _chunks3(k_ref[rows, :])
            v3 = _chunks3(v_ref[rows, :]).astype(BF16)
            do3 = _chunks3(do[rows, :]).astype(BF16)
            for way in range(2):
                states[way, batch] = _bdot((k3 * consts[way][4]).astype(BF16), v3, _BTN)
                d_states[way, batch] = _bdot((q3 * consts[way][2]).astype(BF16), do3, _BTN)
            return carry

        lax.fori_loop(0, nc // CHUNK_BATCH, prep_step, 0)
        c_decs = [c[6] for c in consts]
        _ret_scan(states, c_decs, nc, False)
        _ret_scan(d_states, c_decs, nc, True)

        def main_step(b, dlams):
            rows, batch = _batch_rows(b), _batch_chunks(b)
            q3 = _chunks3(q_ref[rows, :] * scale)
            k3 = _chunks3(k_ref[rows, :])
            q3b, k3b = q3.astype(BF16), k3.astype(BF16)
            v3b = _chunks3(v_ref[rows, :]).astype(BF16)
            do3b = _chunks3(do[rows, :]).astype(BF16)
            a0 = _bdot(q3b, k3b, _BNT)
            pv = _bdot(do3b, v3b, _BNT)
            dq = dk = dv = None
            new_dlams = []
            for way in range(2):
                mask, relf, q_dec, q_exp, k_dec, k_exp, c_dec = consts[way]
                state, d_state = states[way, batch], d_states[way, batch]
                dp = pv * mask
                dpb = dp.astype(BF16)
                gq = _bdot(do3b, state.astype(BF16), _BNT)
                gk = _bdot(v3b, d_state.astype(BF16), _BNT)
                parts = (_bdot(dpb, k3b, _BNN) + q_dec * gq, _bdot(dpb, q3b, _BTN) + k_dec * gk,
                         _bdot((a0 * mask).astype(BF16), do3b, _BTN)
                         + _bdot((k3 * k_dec).astype(BF16), d_state.astype(BF16), _BNN))
                dq, dk, dv = parts if dq is None else (dq + parts[0], dk + parts[1], dv + parts[2])
                total = lambda x: jnp.sum(jnp.sum(x, axis=0), axis=0, keepdims=True)
                new_dlams.append(dlams[way] + total(relf * a0 * dp)
                                 + total(q_exp * q_dec * q3 * gq + k_exp * k_dec * k3 * gk)
                                 + (C * c_dec) * total(state * d_state))
            flat = lambda x: x.reshape(CHUNK_BATCH * C, HEAD_DIM)
            dq_ref[rows, :] = flat(dq) * scale
            dk_ref[rows, :] = flat(dk)
            dv_ref[rows, :] = flat(dv)
            return tuple(new_dlams)

        dlams = lax.fori_loop(0, nc // CHUNK_BATCH, main_step, (jnp.zeros((1, HEAD_DIM), F32),) * 2)
        for row, (dlam, lg) in enumerate(zip(dlams, lgs)):
            small_ref[0, pl.ds(row, 1), :] = jnp.broadcast_to(jnp.sum(dlam, axis=-1, keepdims=True) * lg, (1, HEAD_DIM))
        small_ref[0, pl.ds(2, 6), :] = jnp.zeros((6, HEAD_DIM), F32)

    hspec = pl.BlockSpec((S, HEAD_DIM), lambda h: (0, h))
    smem = pl.BlockSpec(memory_space=pltpu.SMEM)
    return pl.pallas_call(
        body, grid=(H,), name="ret_bwd",
        in_specs=[smem, smem] + _head_specs(S, (3, 4, 5), H) + [hspec],
        out_specs=[hspec, hspec, hspec, pl.BlockSpec((1, 8, HEAD_DIM), lambda h: (h, 0, 0))],
        out_shape=[SDS((S, H * HEAD_DIM), F32)] * 3 + [SDS((H, 8, HEAD_DIM), F32)],
        scratch_shapes=[pltpu.VMEM((2, nc, HEAD_DIM, HEAD_DIM), F32), pltpu.VMEM((2, nc, HEAD_DIM, HEAD_DIM), F32)],
        compiler_params=_cp(1))(dec_f, dec_b, proj, proj, proj, d_out)


def _ffn_bwd_act(dh2, wd, g, u):
    S, D = dh2.shape
    nblk, _, FB = g.shape
    tm = min(512, S)

    def body(dh_ref, wd_ref, g_ref, u_ref, dg_ref, du_ref):
        dact = _dot(dh_ref[...], wd_ref[...], _NT)
        gg = g_ref[0].astype(F32)
        sg = _sigmoid(gg)
        dg_ref[0] = (dact * u_ref[0].astype(F32) * (sg * (1.0 + gg * (1.0 - sg)))).astype(BF16)
        du_ref[0] = (dact * (gg * sg)).astype(BF16)

    blk = pl.BlockSpec((1, tm, FB), lambda j, i: (j, i, 0))
    return pl.pallas_call(
        body, grid=(nblk, S // tm), name="ffn_bwd_act",
        in_specs=[pl.BlockSpec((tm, D), lambda j, i: (i, 0)), pl.BlockSpec((FB, D), lambda j, i: (j, 0)), blk, blk],
        out_specs=[blk, blk], out_shape=[SDS((nblk, S, FB), BF16)] * 2,
        compiler_params=_cp(2))(dh2, wd, g, u)


def _ffn_bwd_in(dg, du, wg, wu, h1, dh2, w_norm):
    nblk, S, FB = dg.shape
    D = h1.shape[1]
    tm = min(512, S)

    def body(dg_ref, du_ref, wg_ref, wu_ref, h_hbm, dh2_hbm, wn_ref, dh_ref, dhb_ref, dw_ref, acc, h_buf, dh2_buf, sems):
        i, j = pl.program_id(0), pl.program_id(1)
        rows = pl.ds(pl.multiple_of(i * tm, tm), tm)
        fetch = [pltpu.make_async_copy(h_hbm.at[rows, :], h_buf, sems.at[0]),
                 pltpu.make_async_copy(dh2_hbm.at[rows, :], dh2_buf, sems.at[1])]

        @pl.when(j == 0)
        def _():
            for cp in fetch:
                cp.start()
            acc[...] = jnp.zeros_like(acc)

        @pl.when((i == 0) & (j == 0))
        def _():
            dw_ref[...] = jnp.zeros_like(dw_ref)

        acc[...] += _dot(dg_ref[0], wg_ref[0], _NN) + _dot(du_ref[0], wu_ref[0], _NN)

        @pl.when(j == nblk - 1)
        def _():
            for cp in fetch:
                cp.wait()
            dh, dw = _rms_bwd(acc[...], h_buf[...], wn_ref[...])
            dh = dh2_buf[...] + dh
            dh_ref[...] = dh
            dhb_ref[...] = dh.astype(BF16)
            dw_ref[...] += dw

    blk = pl.BlockSpec((1, tm, FB), lambda i, j: (j, i, 0))
    wspec = pl.BlockSpec((1, FB, D), lambda i, j: (j, 0, 0))
    row = pl.BlockSpec((tm, D), lambda i, j: (i, 0))
    vec = pl.BlockSpec((1, D), lambda i, j: (0, 0))
    return pl.pallas_call(
        body, grid=(S // tm, nblk), name="ffn_bwd_in",
        in_specs=[blk, blk, wspec, wspec, ANY, ANY, vec],
        out_specs=[row, row, vec], out_shape=[SDS((S, D), F32), SDS((S, D), BF16), SDS((1, D), F32)],
        scratch_shapes=[pltpu.VMEM((tm, D), F32), pltpu.VMEM((tm, D), F32), pltpu.VMEM((tm, D), F32),
                        pltpu.SemaphoreType.DMA((2,))],
        compiler_params=_cp(2))(dg, du, wg, wu, h1, dh2, w_norm)


def _dmix(dh1, w_out):
    S, D = dh1.shape
    tm = min(512, S)

    def body(dh_ref, w_ref, o_ref):
        o_ref[...] = _dot(dh_ref[...], w_ref[...], _NT)

    row = pl.BlockSpec((tm, D), lambda i: (i, 0))
    return pl.pallas_call(
        body, grid=(S // tm,), name="dmix", in_specs=[row, pl.BlockSpec((D, D), lambda i: (0, 0))],
        out_specs=row, out_shape=SDS((S, D), F32), compiler_params=_cp(1))(dh1, w_out)


def _in_bwd(dproj, w_blk, x, dh1, w_norm):
    S, D = x.shape
    nblk, _, NB = w_blk.shape
    tm = min(512, S)

    def body(dp_ref, w_ref, x_ref, dh1_ref, wn_ref, dx_ref, dw_ref, acc):
        i, j = pl.program_id(0), pl.program_id(1)

        @pl.when(j == 0)
        def _():
            acc[...] = jnp.zeros_like(acc)

        @pl.when((i == 0) & (j == 0))
        def _():
            dw_ref[...] = jnp.zeros_like(dw_ref)

        acc[...] += _dot(dp_ref[...], w_ref[0], _NT)

        @pl.when(j == nblk - 1)
        def _():
            dh, dw = _rms_bwd(acc[...], x_ref[...], wn_ref[...])
            dx_ref[...] = dh1_ref[...] + dh
            dw_ref[...] += dw

    row = pl.BlockSpec((tm, D), lambda i, j: (i, 0))
    vec = pl.BlockSpec((1, D), lambda i, j: (0, 0))
    return pl.pallas_call(
        body, grid=(S // tm, nblk), name="in_bwd",
        in_specs=[pl.BlockSpec((tm, NB), lambda i, j: (i, j)), pl.BlockSpec((1, D, NB), lambda i, j: (j, 0, 0)),
                  row, row, vec],
        out_specs=[row, vec], out_shape=[SDS((S, D), F32), SDS((1, D), F32)],
        scratch_shapes=[pltpu.VMEM((tm, D), F32)], compiler_params=_cp(2))(dproj, w_blk, x, dh1, w_norm)


def _wgrad(a, b, a_spec, b_spec, o_spec, o_shape, grid, name):
    nk = grid[-1]

    def ld(ref):
        return ref[0] if len(ref.shape) == 3 else ref[...]

    def body(a_ref, b_ref, o_ref, acc):
        k = pl.program_id(len(grid) - 1)

        @pl.when(k == 0)
        def _():
            acc[...] = jnp.zeros_like(acc)

        acc[...] += _dot(ld(a_ref), ld(b_ref), _TN)

        @pl.when(k == nk - 1)
        def _():
            if len(o_ref.shape) == 3:
                o_ref[0] = acc[...].astype(o_ref.dtype)
            else:
                o_ref[...] = acc[...].astype(o_ref.dtype)

    return pl.pallas_call(
        body, grid=grid, name=name, in_specs=[a_spec, b_spec], out_specs=o_spec, out_shape=SDS(o_shape, BF16),
        scratch_shapes=[pltpu.VMEM(o_spec.block_shape[-2:], F32)], compiler_params=_cp(len(grid)))(a, b)


def _peer(k):
    x, y, c = lax.axis_index("x"), lax.axis_index("y"), lax.axis_index("c")
    px = 1 - x if k & 4 else x
    py = 1 - y if k & 2 else y
    pc = 1 - c if k & 1 else c
    return (px, py, pc), 4 * px + 2 * py + pc


def _exchange_copies(srcs, lands, send_sems, recv_sems, which, gather):
    _, me = _peer(0)
    pairs = []
    for pos, a in enumerate(which):
        for k in range(1, N_DEV):
            dev, idx = _peer(k)
            sem = pos * (N_DEV - 1) + k - 1
            src = srcs[a] if gather else srcs[a].at[idx]
            mk = functools.partial(pltpu.make_async_remote_copy, src_ref=src, send_sem=send_sems.at[sem],
                                   recv_sem=recv_sems.at[sem], device_id=dev, device_id_type=MESH)
            pairs.append((mk(dst_ref=lands[a].at[me]), mk(dst_ref=lands[a].at[idx])))
    return pairs


def _sequencer_kernel(name, collective_id, n_remote, n_local):
    return pl.kernel(mesh=plsc.ScalarSubcoreMesh(axis_name="sequencer", num_cores=1), name=name,
                     scratch_types=(pltpu.SemaphoreType.DMA((n_remote,)), pltpu.SemaphoreType.DMA((n_remote,)),
                                    pltpu.SemaphoreType.DMA((n_local,))),
                     compiler_params=pltpu.CompilerParams(collective_id=collective_id))


def _handshake(ks):
    barrier = pltpu.get_barrier_semaphore()
    for k in ks:
        pl.semaphore_signal(barrier, inc=1, device_id=_peer(k)[0], device_id_type=MESH)
    pl.semaphore_wait(barrier, len(ks))


def _sequencer_scatter(arrays, name, collective_id):
    n = len(arrays)
    hbm = pltpu.MemorySpace.HBM
    srcs = [jax.new_ref(a, memory_space=hbm) for a in arrays]
    lands = [jax.empty_ref(SDS(a.shape, a.dtype), memory_space=hbm) for a in arrays]

    @_sequencer_kernel(name, collective_id, n * (N_DEV - 1), n)
    def launch(send_sems, recv_sems, local_sems):
        _handshake(range(1, N_DEV))
        _, me = _peer(0)
        local = [pltpu.make_async_copy(srcs[a].at[me], lands[a].at[me], local_sems.at[a]) for a in range(n)]
        pairs = _exchange_copies(srcs, lands, send_sems, recv_sems, range(n), False)
        for out, _ in pairs:
            out.start()
        for cp in local:
            cp.start()
        for out, arrival in pairs:
            out.wait_send()
            arrival.wait_recv()
        for cp in local:
            cp.wait()

    launch()
    return [r[...] for r in lands]


SIBLING = 1
OTHER_CHIPS = (2, 4, 6)


def _sequencer_gather(arrays, name, collective_id):
    n = len(arrays)
    hbm = pltpu.MemorySpace.HBM
    srcs = [jax.new_ref(a, memory_space=hbm) for a in arrays]
    lands = [jax.empty_ref(SDS((N_DEV,) + a.shape, a.dtype), memory_space=hbm) for a in arrays]

    @_sequencer_kernel(name, collective_id, n * (N_DEV - 1), n)
    def launch(send_sems, recv_sems, local_sems):
        _handshake((SIBLING,) + OTHER_CHIPS)
        _, me = _peer(0)
        sibling, _ = _peer(SIBLING)

        def copy(a, k, src, block, to):
            sem = a * (N_DEV - 1) + k - 1
            return pltpu.make_async_remote_copy(src_ref=src, dst_ref=lands[a].at[block], send_sem=send_sems.at[sem],
                                                recv_sem=recv_sems.at[sem], device_id=to, device_id_type=MESH)

        local = [pltpu.make_async_copy(srcs[a], lands[a].at[me], local_sems.at[a]) for a in range(n)]
        first = [copy(a, k, srcs[a], me, _peer(k)[0]) for a in range(n) for k in OTHER_CHIPS + (SIBLING,)]
        for cp in first + local:
            cp.start()
        passed = []
        for a in range(n):
            for k in OTHER_CHIPS:
                _, block = _peer(k)
                copy(a, k, srcs[a], block, sibling).wait_recv()
                passed.append(copy(a, k ^ SIBLING, lands[a].at[block], block, sibling))
                passed[-1].start()
        for a in range(n):
            for k in (SIBLING,) + tuple(k ^ SIBLING for k in OTHER_CHIPS):
                copy(a, k, srcs[a], _peer(k)[1], sibling).wait_recv()
        for cp in first + passed:
            cp.wait_send()
        for cp in local:
            cp.wait()

    launch()
    return [r[...] for r in lands]


SMALL_ROWS = 64


def _small_step(part, w, m, v):
    def body(p_ref, w_ref, m_ref, v_ref, g_ref, d_ref, nm_ref, nv_ref, gath, send_sems, recv_sems):
        _, me = _peer(0)
        gath[me] = p_ref[...]
        copies = []
        for k in range(1, N_DEV):
            dev, idx = _peer(k)
            out = pltpu.make_async_remote_copy(src_ref=p_ref, dst_ref=gath.at[me], send_sem=send_sems.at[k - 1],
                                               recv_sem=recv_sems.at[k - 1], device_id=dev, device_id_type=MESH)
            out.start()
            arrival = pltpu.make_async_remote_copy(src_ref=p_ref, dst_ref=gath.at[idx], send_sem=send_sems.at[k - 1],
                                                   recv_sem=recv_sems.at[k - 1], device_id=dev, device_id_type=MESH)
            copies.append((out, arrival))
        for out, arrival in copies:
            out.wait_send()
            arrival.wait_recv()
        g = gath[0]
        for p in range(1, N_DEV):
            g = g + gath[p]
        g_ref[...] = g
        d_ref[...], nm_ref[...], nv_ref[...] = _adamw(w_ref[...], g, m_ref[...], v_ref[...])

    vm = pl.BlockSpec(memory_space=pltpu.VMEM)
    return pl.pallas_call(
        body, name="small_step", in_specs=[vm] * 4, out_specs=[vm] * 4,
        out_shape=[SDS((SMALL_ROWS, 128), F32)] * 4,
        scratch_shapes=[pltpu.VMEM((N_DEV, SMALL_ROWS, 128), F32), pltpu.SemaphoreType.DMA((N_DEV - 1,)),
                        pltpu.SemaphoreType.DMA((N_DEV - 1,))])(part, w, m, v)


def _adamw(w, g, m, v):
    m = ADAM_B1 * m + (1.0 - ADAM_B1) * g
    v = ADAM_B2 * v + (1.0 - ADAM_B2) * (g * g)
    m_hat = m / (1.0 - ADAM_B1 ** ADAM_STEP)
    v_hat = v / (1.0 - ADAM_B2 ** ADAM_STEP)
    delta = -ADAM_LR * (m_hat / (jnp.sqrt(v_hat) + ADAM_EPS) + ADAM_WD * w)
    return delta, m, v


def _adamw_block(parts, w, m, v, name):
    R, C = w.shape
    n_parts = len(parts)
    Rp = R // n_parts
    tr = next(t for t in (256, 128, 64, 32, 16, 8) if Rp % t == 0 and t * C <= 256 * 1024)
    per_part = Rp // tr

    def body(*refs):
        p_refs = refs[:n_parts]
        w_ref, m_ref, v_ref, g_ref, d_ref, nm_ref, nv_ref = refs[n_parts:]
        for k, p_ref in enumerate(p_refs):
            @pl.when(pl.program_id(0) // per_part == k)
            def _(p_ref=p_ref):
                g = p_ref[0].astype(F32)
                for p in range(1, N_DEV):
                    g = g + p_ref[p].astype(F32)
                g_ref[...] = g
                d_ref[...], nm_ref[...], nv_ref[...] = _adamw(w_ref[...], g, m_ref[...], v_ref[...])

    row = pl.BlockSpec((tr, C), lambda i: (i, 0))
    part_specs = [pl.BlockSpec((N_DEV, tr, C), functools.partial(
        lambda i, k: (0, jnp.clip(i - k * per_part, 0, per_part - 1), 0), k=k)) for k in range(n_parts)]
    return pl.pallas_call(
        body, grid=(R // tr,), name=name, in_specs=part_specs + [row, row, row],
        out_specs=[row] * 4, out_shape=[SDS((R, C), F32)] * 4, compiler_params=_cp(1))(*parts, w, m, v)


def _pack_small(mix, ffn, fin, retw, dec_f, dec_b, loss):
    flat = jnp.concatenate([mix.reshape(-1), ffn.reshape(-1), fin.reshape(-1), retw.reshape(-1), dec_f.reshape(-1),
                            dec_b.reshape(-1), loss.reshape(-1)])
    return jnp.pad(flat, (0, SMALL_ROWS * 128 - flat.shape[0])).reshape(SMALL_ROWS, 128)


def _unpack_small(packed, shapes):
    flat = packed.reshape(-1)
    out, at = [], 0
    for s in shapes:
        n = math.prod(s)
        out.append(flat[at:at + n].reshape(s))
        at += n
    return out


def kernel(x, norm_mix_w, w_in, ret_decay_fwd, ret_decay_bwd, ret_norm_w, w_out, norm_ffn_w, w_gate, w_up, w_down, norm_final_w, loss_target, m_norm_mix_w, m_w_in, m_ret_decay_fwd, m_ret_decay_bwd, m_ret_norm_w, m_w_out, m_norm_ffn_w, m_w_gate, m_w_up, m_w_down, m_norm_final_w, v_norm_mix_w, v_w_in, v_ret_decay_fwd, v_ret_decay_bwd, v_ret_norm_w, v_w_out, v_norm_ffn_w, v_w_gate, v_w_up, v_w_down, v_norm_final_w):
    x2 = x[0]
    tgt = loss_target[0]
    S, D = x2.shape
    H = ret_norm_w.shape[1] // HEAD_DIM
    DA = H * HEAD_DIM
    fin_w = norm_final_w.reshape(1, D)
    big = (w_in[0], w_out[0], w_gate[0].T, w_up[0].T, w_down[0])

    big_b = [w.astype(BF16) for w in big]
    wi, = _sequencer_gather(big_b[:1], "gather_in", 0)
    wo, wg, wu = _sequencer_gather(big_b[1:4], "gather_mid", 1)
    wd, = _sequencer_gather(big_b[4:], "gather_down", 5)
    NB = wi.shape[2]

    proj, n1 = _proj_fwd(x2, norm_mix_w, wi)
    bias = _attn_bias()[:H]
    attn, lse = _attn_fwd(proj, bias)
    ret, o_raw = _ret_fwd(proj, ret_decay_fwd, ret_decay_bwd, ret_norm_w)
    wo_full = wo.reshape(D, D)
    FB = wd.shape[1]
    wd_full = wd.reshape(N_DEV * FB, D)
    h1, mixed, n2 = _out_fwd(x2, attn, ret, wo_full, norm_ffn_w)
    gate, up, act = _ffn_up(n2, wg, wu)
    dh2, dh2_b, loss_parts, g_fin = _ffn_down_loss(act, wd_full, h1, tgt, fin_w)

    dgate, dup = _ffn_bwd_act(dh2_b, wd_full, gate, up)
    tn = min(1024, D)
    ffn_specs = (pl.BlockSpec((1, S, FB), lambda j, n, k: (j, 0, 0)), pl.BlockSpec((S, tn), lambda j, n, k: (0, n)),
                 pl.BlockSpec((1, FB, tn), lambda j, n, k: (j, 0, n)), (N_DEV, FB, D), (N_DEV, D // tn, 1))
    g_wd = _wgrad(act, dh2_b, *ffn_specs, "wgrad_down")
    g_wg = _wgrad(dgate, n2, *ffn_specs, "wgrad_gate")
    g_wu = _wgrad(dup, n2, *ffn_specs, "wgrad_up")
    parts_f = _sequencer_scatter([g_wg, g_wu, g_wd], "scatter_ffn", 2)
    dh1, dh1_b, g_ffn = _ffn_bwd_in(dgate, dup, wg, wu, h1, dh2, norm_ffn_w)
    dmix = _dmix(dh1_b, wo_full)
    tmw = min(512, D)
    tk = min(2048, S)
    g_wo = _wgrad(mixed, dh1_b, pl.BlockSpec((tk, tmw), lambda m, k: (k, m)), pl.BlockSpec((tk, D), lambda m, k: (k, 0)),
                  pl.BlockSpec((tmw, D), lambda m, k: (m, 0)), (D, D), (D // tmw, S // tk), "wgrad_out")
    parts_o = _sequencer_scatter([g_wo.reshape(N_DEV, D // N_DEV, D)], "scatter_out", 3)
    d_ret, dg_r, small_w = _ret_gate_bwd(proj, o_raw, dmix, ret_norm_w, DA)
    dq_r, dk_r, dv_r, small = _ret_bwd(proj, d_ret, ret_decay_fwd, ret_decay_bwd)
    dq_a, dk_a, dv_a = _attn_bwd(proj, attn, lse, dmix, bias)
    dproj = jnp.concatenate([t.astype(BF16) for t in (dq_a, dk_a, dv_a, dq_r, dk_r, dv_r, dg_r)], axis=1)
    half = D // tmw // 2
    parts_i = []
    for part, (name, cid) in enumerate((("in_lo", 4), ("in_hi", 6))):
        g_wi = _wgrad(n1, dproj, pl.BlockSpec((S, tmw), functools.partial(lambda j, m, k, off: (0, m + off), off=part * half)),
                      pl.BlockSpec((S, NB), lambda j, m, k: (0, j)), pl.BlockSpec((1, tmw, NB), lambda j, m, k: (j, m, 0)),
                      (N_DEV, D // 2, NB), (N_DEV, half, 1), "wgrad_" + name)
        parts_i += _sequencer_scatter([g_wi], "scatter_" + name, cid)
    grad_x, g_mix = _in_bwd(dproj, wi, x2, dh1, norm_mix_w)

    big_m = (m_w_in[0], m_w_out[0], m_w_gate[0].T, m_w_up[0].T, m_w_down[0])
    big_v = (v_w_in[0], v_w_out[0], v_w_gate[0].T, v_w_up[0].T, v_w_down[0])
    names = ("adamw_in", "adamw_out", "adamw_gate", "adamw_up", "adamw_down")
    upd = [None] * 5
    for a, p in zip((2, 3, 4, 1, 0), [[t] for t in parts_f + parts_o] + [parts_i]):
        upd[a] = _adamw_block(p, big[a], big_m[a], big_v[a], names[a])

    g_dec_f = small[:, 0, 0].reshape(1, H)
    g_dec_b = small[:, 1, 0].reshape(1, H)
    g_retw = small_w[:, 0, :].reshape(1, DA)
    loss_local = jnp.sum(loss_parts[::8, 0])
    zero = jnp.zeros((1,), F32)
    part = _pack_small(g_mix, g_ffn, g_fin, g_retw, g_dec_f, g_dec_b, loss_local)
    sw = _pack_small(norm_mix_w, norm_ffn_w, norm_final_w, ret_norm_w, ret_decay_fwd, ret_decay_bwd, zero)
    sm = _pack_small(m_norm_mix_w, m_norm_ffn_w, m_norm_final_w, m_ret_norm_w, m_ret_decay_fwd, m_ret_decay_bwd, zero)
    sv = _pack_small(v_norm_mix_w, v_norm_ffn_w, v_norm_final_w, v_ret_norm_w, v_ret_decay_fwd, v_ret_decay_bwd, zero)
    shapes = [(1, D), (1, D), (D,), (1, DA), (1, H), (1, H), ()]
    sg, sd, snm, snv = [_unpack_small(t, shapes) for t in _small_step(part, sw, sm, sv)]
    loss = sg[6]

    def ordered(small_set, k):
        b = [(u[k].T if a in (2, 3) else u[k])[None] for a, u in enumerate(upd)]
        return [small_set[0], b[0], small_set[4], small_set[5], small_set[3], b[1], small_set[1], b[2], b[3], b[4],
                small_set[2]]

    return (loss, grad_x[None], *ordered(sg, 0), *ordered(sd, 1), *ordered(snm, 2), *ordered(snv, 3))
```

```python
import functools
import math

import numpy as np
import jax
import jax.numpy as jnp
from jax import lax
from jax.experimental import pallas as pl
from jax.experimental.pallas import tpu as pltpu
from jax.experimental.pallas import tpu_sc as plsc

F32 = jnp.float32
BF16 = jnp.bfloat16
SDS = jax.ShapeDtypeStruct

HEAD_DIM = 128
EPS = 1e-6
RET_CHUNK = 128
DILATIONS = (1, 4, 16)
BAND = 64
Q_TILE = 128
K_TILE = Q_TILE + 2 * BAND
KV_PAD = BAND * max(DILATIONS)
TILE_GROUP = 8
NEG = -1e30
N_DEV = 8
N_GROUPS = 7
ADAM_LR, ADAM_B1, ADAM_B2, ADAM_EPS, ADAM_WD, ADAM_STEP = 0.001, 0.9, 0.999, 1e-08, 0.01, 10
VMEM_LIMIT = 56 * 1024 * 1024
MESH = pl.DeviceIdType.MESH
ANY = pl.BlockSpec(memory_space=pl.ANY)


def _cp(n_grid):
    return pltpu.CompilerParams(dimension_semantics=("arbitrary",) * n_grid, vmem_limit_bytes=VMEM_LIMIT)


def _sigmoid(x):
    return 1.0 / (1.0 + jnp.exp(-x))


def _rms_scale(h):
    return lax.rsqrt(jnp.mean(h * h, axis=-1, keepdims=True) + EPS)


def _rms_bwd(dn, h, w):
    r = _rms_scale(h)
    gw = dn * w
    dh = r * gw - h * (r * r * r) * jnp.mean(gw * h, axis=-1, keepdims=True)
    return dh, jnp.sum(dn * h * r, axis=0, keepdims=True)


def _dot(a, b, dims):
    return lax.dot_general(a.astype(BF16), b.astype(BF16), (dims, ((), ())), preferred_element_type=F32)


_NN = ((1,), (0,))
_NT = ((1,), (1,))
_TN = ((0,), (0,))


def _proj_fwd(x, w_norm, w_blk):
    S, D = x.shape
    nblk, _, NB = w_blk.shape
    tm = min(1024, S)

    def body(x_ref, wn_ref, w_ref, proj_ref, n_ref, n_scr):
        @pl.when(pl.program_id(1) == 0)
        def _():
            xf = x_ref[...]
            nb = (xf * _rms_scale(xf) * wn_ref[...]).astype(BF16)
            n_scr[...] = nb
            n_ref[...] = nb
        proj_ref[...] = jnp.dot(n_scr[...], w_ref[0], preferred_element_type=F32)

    return pl.pallas_call(
        body, grid=(S // tm, nblk), name="proj_fwd",
        in_specs=[pl.BlockSpec((tm, D), lambda i, j: (i, 0)), pl.BlockSpec((1, D), lambda i, j: (0, 0)),
                  pl.BlockSpec((1, D, NB), lambda i, j: (j, 0, 0))],
        out_specs=[pl.BlockSpec((tm, NB), lambda i, j: (i, j)), pl.BlockSpec((tm, D), lambda i, j: (i, 0))],
        out_shape=[SDS((S, nblk * NB), F32), SDS((S, D), BF16)],
        scratch_shapes=[pltpu.VMEM((tm, D), BF16)], compiler_params=_cp(2))(x, w_norm, w_blk)


def _out_fwd(x, attn, ret, w_out, w_norm):
    S, D = x.shape
    DA = attn.shape[1]
    tm = min(256, S)

    def body(x_ref, a_ref, r_ref, w_ref, wn_ref, h_ref, mix_ref, n_ref):
        a = a_ref[...].astype(BF16)
        r = r_ref[...].astype(BF16)
        mix_ref[:, :DA] = a
        mix_ref[:, DA:] = r
        h = x_ref[...] + jnp.dot(a, w_ref[:DA, :], preferred_element_type=F32) \
            + jnp.dot(r, w_ref[DA:, :], preferred_element_type=F32)
        h_ref[...] = h
        n_ref[...] = (h * _rms_scale(h) * wn_ref[...]).astype(BF16)

    row = lambda w: pl.BlockSpec((tm, w), lambda i: (i, 0))
    return pl.pallas_call(
        body, grid=(S // tm,), name="out_fwd",
        in_specs=[row(D), row(DA), row(D - DA), pl.BlockSpec((D, D), lambda i: (0, 0)),
                  pl.BlockSpec((1, D), lambda i: (0, 0))],
        out_specs=[row(D), row(D), row(D)],
        out_shape=[SDS((S, D), F32), SDS((S, D), BF16), SDS((S, D), BF16)],
        compiler_params=_cp(1))(x, attn, ret, w_out, w_norm)


def _ffn_up(n2, wg, wu):
    S, D = n2.shape
    nblk, FB, _ = wg.shape
    tm = min(512, S)

    def body(n_ref, wg_ref, wu_ref, g_ref, u_ref, a_ref):
        n = n_ref[...]
        g = _dot(n, wg_ref[0], _NT)
        u = _dot(n, wu_ref[0], _NT)
        g_ref[0] = g.astype(BF16)
        u_ref[0] = u.astype(BF16)
        a_ref[0] = (g * _sigmoid(g) * u).astype(BF16)

    wspec = pl.BlockSpec((1, FB, D), lambda j, i: (j, 0, 0))
    ospec = pl.BlockSpec((1, tm, FB), lambda j, i: (j, i, 0))
    return pl.pallas_call(
        body, grid=(nblk, S // tm), name="ffn_up",
        in_specs=[pl.BlockSpec((tm, D), lambda j, i: (i, 0)), wspec, wspec],
        out_specs=[ospec, ospec, ospec],
        out_shape=[SDS((nblk, S, FB), BF16)] * 3,
        compiler_params=_cp(2))(n2, wg, wu)


def _ffn_down_loss(act, wd, h1, target, w_norm):
    nblk, S, FB = act.shape
    D = h1.shape[1]
    tm = min(512, S)

    def body(a_ref, wd_ref, h_ref, t_ref, wn_ref, dh_ref, dhb_ref, loss_ref, dw_ref, acc):
        i, j = pl.program_id(0), pl.program_id(1)

        @pl.when(j == 0)
        def _():
            acc[...] = h_ref[...]

        @pl.when((i == 0) & (j == 0))
        def _():
            dw_ref[...] = jnp.zeros_like(dw_ref)

        acc[...] += jnp.dot(a_ref[0], wd_ref[...], preferred_element_type=F32)

        @pl.when(j == nblk - 1)
        def _():
            h = acc[...]
            w = wn_ref[...]
            err = h * _rms_scale(h) * w - t_ref[...]
            loss_ref[...] = jnp.full(loss_ref.shape, 0.5 * jnp.sum(err * err) / D, F32)
            dh, dw = _rms_bwd(err * (1.0 / D), h, w)
            dh_ref[...] = dh
            dhb_ref[...] = dh.astype(BF16)
            dw_ref[...] += dw

    row = pl.BlockSpec((tm, D), lambda i, j: (i, 0))
    vec = pl.BlockSpec((1, D), lambda i, j: (0, 0))
    return pl.pallas_call(
        body, grid=(S // tm, nblk), name="ffn_down_loss",
        in_specs=[pl.BlockSpec((1, tm, FB), lambda i, j: (j, i, 0)), pl.BlockSpec((FB, D), lambda i, j: (j, 0)),
                  row, row, vec],
        out_specs=[row, row, pl.BlockSpec((8, 128), lambda i, j: (i, 0)), vec],
        out_shape=[SDS((S, D), F32), SDS((S, D), BF16), SDS((S // tm * 8, 128), F32), SDS((1, D), F32)],
        scratch_shapes=[pltpu.VMEM((tm, D), F32)], compiler_params=_cp(2))(act, wd, h1, target, w_norm)


def _attn_bias():
    n_heads = 8
    slopes = np.exp2(-8.0 * np.arange(1, n_heads + 1, dtype=np.float32) / n_heads)
    dist = np.abs(np.arange(K_TILE)[None, :] - BAND - np.arange(Q_TILE)[:, None])
    out = np.empty((n_heads, len(DILATIONS), Q_TILE, K_TILE), np.float32)
    for h in range(n_heads):
        for p, d in enumerate(DILATIONS):
            out[h, p] = np.where(dist <= BAND, -slopes[h] * (d * dist).astype(np.float32), NEG)
    return jnp.asarray(out)


def _attn_tiles(S, d):
    L = S // d
    per_class = L // Q_TILE
    return L, per_class, d * per_class


def _tile_rows(t, d, per_class):
    r = t // per_class
    a = (t % per_class) * Q_TILE
    q_rows = pl.ds(r + d * a, Q_TILE, stride=d) if d > 1 else pl.ds(pl.multiple_of(a, Q_TILE), Q_TILE)
    k_rows = pl.ds(KV_PAD + r + d * (a - BAND), K_TILE, stride=d) if d > 1 else pl.ds(
        pl.multiple_of(KV_PAD + a - BAND, BAND), K_TILE)
    return a, q_rows, k_rows


def _lanes(x, width):
    return jnp.concatenate([x] * (width // HEAD_DIM), axis=-1)


_BNT = (((2,), (2,)), ((0,), (0,)))
_BNN = (((2,), (1,)), ((0,), (0,)))
_BTN = (((1,), (1,)), ((0,), (0,)))


def _bdot(a, b, dims):
    return lax.dot_general(a, b, dims, preferred_element_type=F32)


def _stacked(rows, loaders):
    return [jnp.stack([f(*r) for r in rows]) for f in loaders]


def _edge_mask(a, L):
    lk = lax.broadcasted_iota(jnp.int32, (1, K_TILE), 1) + (a - BAND)
    return jnp.where((lk >= 0) & (lk < L), 0.0, NEG).astype(F32)


def _fill_padded(dst, src, S):
    dst[pl.ds(0, KV_PAD), :] = jnp.zeros((KV_PAD, HEAD_DIM), F32)
    dst[pl.ds(KV_PAD + S, KV_PAD), :] = jnp.zeros((KV_PAD, HEAD_DIM), F32)
    dst[pl.ds(KV_PAD, S), :] = src[...]


def _head_specs(S, groups, n_heads):
    return [pl.BlockSpec((S, HEAD_DIM), functools.partial(lambda h, g: (0, g * n_heads + h), g=g)) for g in groups]


def _attn_fwd(proj, bias):
    S = proj.shape[0]
    H = proj.shape[1] // (N_GROUPS * HEAD_DIM)
    scale = HEAD_DIM ** -0.5

    def body(q_ref, k_ref, v_ref, b_ref, o_ref, lse_ref, kp, vp, m_run, l_run):
        _fill_padded(kp, k_ref, S)
        _fill_padded(vp, v_ref, S)
        o_ref[...] = jnp.zeros_like(o_ref)
        m_run[...] = jnp.full(m_run.shape, NEG, F32)
        l_run[...] = jnp.zeros_like(l_run)
        for p, d in enumerate(DILATIONS):
            L, per_class, n_tiles = _attn_tiles(S, d)

            def tiles(t, carry, p=p, d=d, L=L, per_class=per_class, n_tiles=n_tiles):
                rows = [_tile_rows(t + u * (n_tiles // TILE_GROUP), d, per_class) for u in range(TILE_GROUP)]
                qs, ks, vs, m_old, l_old, o_old, edge = _stacked(rows, (
                    lambda a, qr, kr: q_ref[qr, :].astype(BF16), lambda a, qr, kr: kp[kr, :].astype(BF16),
                    lambda a, qr, kr: vp[kr, :].astype(BF16), lambda a, qr, kr: m_run[qr, :],
                    lambda a, qr, kr: l_run[qr, :], lambda a, qr, kr: o_ref[qr, :], lambda a, qr, kr: _edge_mask(a, L)))
                s = _bdot(qs, ks, _BNT) * scale + b_ref[0, p][None] + edge
                m_new = jnp.maximum(m_old, jnp.max(s, axis=-1, keepdims=True))
                pr = jnp.exp(s - _lanes(m_new, K_TILE)).astype(BF16)
                alpha = jnp.exp(m_old - m_new)
                l_new = alpha * l_old + _bdot(pr, jnp.ones((TILE_GROUP, K_TILE, HEAD_DIM), BF16), _BNN)
                o_new = alpha * o_old + _bdot(pr, vs, _BNN)
                for u, (_, qr, _) in enumerate(rows):
                    o_ref[qr, :] = o_new[u]
                    m_run[qr, :] = m_new[u]
                    l_run[qr, :] = l_new[u]
                return carry

            lax.fori_loop(0, n_tiles // TILE_GROUP, tiles, 0)
        l = l_run[...]
        o_ref[...] = o_ref[...] / l
        lse_ref[...] = m_run[...] + jnp.log(l)

    hspec = pl.BlockSpec((S, HEAD_DIM), lambda h: (0, h))
    return pl.pallas_call(
        body, grid=(H,), name="attn_fwd",
        in_specs=_head_specs(S, (0, 1, 2), H) + [
            pl.BlockSpec((1, len(DILATIONS), Q_TILE, K_TILE), lambda h: (h, 0, 0, 0))],
        out_specs=[hspec, hspec],
        out_shape=[SDS((S, H * HEAD_DIM), F32), SDS((S, H * HEAD_DIM), F32)],
        scratch_shapes=[pltpu.VMEM((S + 2 * KV_PAD, HEAD_DIM), F32), pltpu.VMEM((S + 2 * KV_PAD, HEAD_DIM), F32),
                        pltpu.VMEM((S, HEAD_DIM), F32), pltpu.VMEM((S, HEAD_DIM), F32)],
        compiler_params=_cp(1))(proj, proj, proj, bias)


def _attn_bwd(proj, out, lse, dmix, bias):
    S = proj.shape[0]
    H = proj.shape[1] // (N_GROUPS * HEAD_DIM)
    scale = HEAD_DIM ** -0.5

    def body(q_ref, k_ref, v_ref, o_ref, lse_ref, do_ref, b_ref, dq_ref, dk_ref, dv_ref, kp, vp, dkp, dvp, dsum):
        _fill_padded(kp, k_ref, S)
        _fill_padded(vp, v_ref, S)
        dkp[...] = jnp.zeros_like(dkp)
        dvp[...] = jnp.zeros_like(dvp)
        dq_ref[...] = jnp.zeros_like(dq_ref)
        dsum[...] = jnp.broadcast_to(jnp.sum(do_ref[...] * o_ref[...], axis=-1, keepdims=True), dsum.shape)
        for p, d in enumerate(DILATIONS):
            L, per_class, n_tiles = _attn_tiles(S, d)

            def tiles(t, carry, p=p, d=d, L=L, per_class=per_class, n_tiles=n_tiles):
                rows = [_tile_rows(t + u * (n_tiles // TILE_GROUP), d, per_class) for u in range(TILE_GROUP)]
                qs, ks, vs, dos, lses, dsums, dq_old, dk_old, dv_old, edge = _stacked(rows, (
                    lambda a, qr, kr: q_ref[qr, :].astype(BF16), lambda a, qr, kr: kp[kr, :].astype(BF16),
                    lambda a, qr, kr: vp[kr, :].astype(BF16), lambda a, qr, kr: do_ref[qr, :].astype(BF16),
                    lambda a, qr, kr: lse_ref[qr, :], lambda a, qr, kr: dsum[qr, :], lambda a, qr, kr: dq_ref[qr, :],
                    lambda a, qr, kr: dkp[kr, :], lambda a, qr, kr: dvp[kr, :], lambda a, qr, kr: _edge_mask(a, L)))
                s = _bdot(qs, ks, _BNT) * scale + b_ref[0, p][None] + edge
                pr = jnp.exp(s - _lanes(lses, K_TILE))
                ds = (pr * (_bdot(dos, vs, _BNT) - _lanes(dsums, K_TILE)) * scale).astype(BF16)
                dq_new = dq_old + _bdot(ds, ks, _BNN)
                dk_new = dk_old + _bdot(ds, qs, _BTN)
                dv_new = dv_old + _bdot(pr.astype(BF16), dos, _BTN)
                for u, (_, qr, kr) in enumerate(rows):
                    dq_ref[qr, :] = dq_new[u]
                    dkp[kr, :] = dk_new[u]
                    dvp[kr, :] = dv_new[u]
                return carry

            lax.fori_loop(0, n_tiles // TILE_GROUP, tiles, 0)
        dk_ref[...] = dkp[pl.ds(KV_PAD, S), :]
        dv_ref[...] = dvp[pl.ds(KV_PAD, S), :]

    hspec = pl.BlockSpec((S, HEAD_DIM), lambda h: (0, h))
    padded = pltpu.VMEM((S + 2 * KV_PAD, HEAD_DIM), F32)
    return pl.pallas_call(
        body, grid=(H,), name="attn_bwd",
        in_specs=_head_specs(S, (0, 1, 2), H) + [hspec, hspec, hspec,
                                                  pl.BlockSpec((1, len(DILATIONS), Q_TILE, K_TILE), lambda h: (h, 0, 0, 0))],
        out_specs=[hspec, hspec, hspec],
        out_shape=[SDS((S, H * HEAD_DIM), F32)] * 3,
        scratch_shapes=[padded, padded, padded, padded, pltpu.VMEM((S, HEAD_DIM), F32)],
        compiler_params=_cp(1))(proj, proj, proj, out, lse, dmix, bias)


def _ret_consts(lg, forward):
    C = RET_CHUNK
    i = lax.broadcasted_iota(jnp.int32, (C, C), 0)
    j = lax.broadcasted_iota(jnp.int32, (C, C), 1)
    rel = (i - j) if forward else (j - i)
    inside = (rel >= 0) if forward else (rel > 0)
    relf = jnp.maximum(rel, 0).astype(F32)
    mask = jnp.where(inside, jnp.exp(lg * relf), 0.0)
    idx = lax.broadcasted_iota(jnp.int32, (C, 1), 0).astype(F32)
    q_exp = (idx + 1.0) if forward else (C - idx)
    k_exp = (C - 1.0 - idx) if forward else idx
    return mask, relf, jnp.exp(lg * q_exp), q_exp, jnp.exp(lg * k_exp), k_exp, jnp.exp(lg * C)


def _log_decay(dec_ref, h):
    return -jnp.exp(jnp.full((1, 1), dec_ref[0, h], F32))


FFN_BLOCK = 512
CHUNK_BATCH = 8


def _batch_rows(b):
    n = CHUNK_BATCH * RET_CHUNK
    return pl.ds(pl.multiple_of(b * n, n), n)


def _batch_chunks(b):
    return pl.ds(pl.multiple_of(b * CHUNK_BATCH, CHUNK_BATCH), CHUNK_BATCH)


def _chunks3(x):
    return x.reshape(CHUNK_BATCH, RET_CHUNK, HEAD_DIM)


def _ret_scan(buf, c_decs, nc, reverse):
    def step(n, carry):
        new = []
        for way, r in enumerate(carry):
            c = n if (way == 0) != reverse else nc - 1 - n
            term = buf[way, c]
            buf[way, c] = r
            new.append(r * c_decs[way] + term)
        return tuple(new)

    lax.fori_loop(0, nc, step, (jnp.zeros((HEAD_DIM, HEAD_DIM), F32),) * 2)


def _ret_fwd(proj, dec_f, dec_b, w_norm):
    S = proj.shape[0]
    H = proj.shape[1] // (N_GROUPS * HEAD_DIM)
    nc = S // RET_CHUNK
    scale = HEAD_DIM ** -0.5

    def body(df_ref, db_ref, q_ref, k_ref, v_ref, g_ref, w_ref, y_ref, o_ref, states):
        h = pl.program_id(0)
        consts = [_ret_consts(_log_decay(dref, h), fw) for fw, dref in ((True, df_ref), (False, db_ref))]

        def kv_step(b, carry):
            rows, batch = _batch_rows(b), _batch_chunks(b)
            k3 = _chunks3(k_ref[rows, :])
            v3 = _chunks3(v_ref[rows, :]).astype(BF16)
            for way in range(2):
                states[way, batch] = _bdot((k3 * consts[way][4]).astype(BF16), v3, _BTN)
            return carry

        lax.fori_loop(0, nc // CHUNK_BATCH, kv_step, 0)
        _ret_scan(states, [c[6] for c in consts], nc, False)

        def out_step(b, carry):
            rows, batch = _batch_rows(b), _batch_chunks(b)
            q3 = _chunks3(q_ref[rows, :] * scale)
            k3 = _chunks3(k_ref[rows, :]).astype(BF16)
            v3 = _chunks3(v_ref[rows, :]).astype(BF16)
            a0 = _bdot(q3.astype(BF16), k3, _BNT)
            o = None
            for way in range(2):
                mask, q_dec = consts[way][0], consts[way][2]
                part = _bdot((a0 * mask).astype(BF16), v3, _BNN) \
                    + _bdot((q3 * q_dec).astype(BF16), states[way, batch].astype(BF16), _BNN)
                o = part if o is None else o + part
            o_ref[rows, :] = o.reshape(CHUNK_BATCH * RET_CHUNK, HEAD_DIM)
            return carry

        lax.fori_loop(0, nc // CHUNK_BATCH, out_step, 0)
        o = o_ref[...]
        g = g_ref[...]
        y_ref[...] = o * _rms_scale(o) * w_ref[...] * (g * _sigmoid(g))

    hspec = pl.BlockSpec((S, HEAD_DIM), lambda h: (0, h))
    smem = pl.BlockSpec(memory_space=pltpu.SMEM)
    return pl.pallas_call(
        body, grid=(H,), name="ret_fwd",
        in_specs=[smem, smem] + _head_specs(S, (3, 4, 5, 6), H) + [pl.BlockSpec((1, HEAD_DIM), lambda h: (0, h))],
        out_specs=[hspec, hspec],
        out_shape=[SDS((S, H * HEAD_DIM), F32)] * 2,
        scratch_shapes=[pltpu.VMEM((2, nc, HEAD_DIM, HEAD_DIM), F32)],
        compiler_params=_cp(1))(dec_f, dec_b, proj, proj, proj, proj, w_norm)


def _ret_gate_bwd(proj, o_raw, dmix, w_norm, col0):
    S = proj.shape[0]
    H = proj.shape[1] // (N_GROUPS * HEAD_DIM)

    def body(g_ref, o_ref, dy_ref, w_ref, do_ref, dg_ref, dw_ref):
        o = o_ref[...]
        g = g_ref[...]
        dy = dy_ref[...]
        w = w_ref[...]
        rr = _rms_scale(o)
        normed = o * rr
        sg = _sigmoid(g)
        silu = g * sg
        dw_ref[0] = jnp.broadcast_to(jnp.sum(dy * normed * silu, axis=0, keepdims=True), (8, HEAD_DIM))
        dg_ref[...] = dy * normed * w * (sg * (1.0 + g * (1.0 - sg)))
        dnormed = dy * w * silu
        do_ref[...] = rr * dnormed - o * (rr * rr * rr) * jnp.mean(dnormed * o, axis=-1, keepdims=True)

    hspec = pl.BlockSpec((S, HEAD_DIM), lambda h: (0, h))
    nh0 = col0 // HEAD_DIM
    return pl.pallas_call(
        body, grid=(H,), name="ret_gate_bwd",
        in_specs=_head_specs(S, (6,), H) + [hspec, pl.BlockSpec((S, HEAD_DIM), lambda h: (0, nh0 + h)),
                                            pl.BlockSpec((1, HEAD_DIM), lambda h: (0, h))],
        out_specs=[hspec, hspec, pl.BlockSpec((1, 8, HEAD_DIM), lambda h: (h, 0, 0))],
        out_shape=[SDS((S, H * HEAD_DIM), F32)] * 2 + [SDS((H, 8, HEAD_DIM), F32)],
        compiler_params=_cp(1))(proj, o_raw, dmix, w_norm)


def _ret_bwd(proj, d_out, dec_f, dec_b):
    S = proj.shape[0]
    H = proj.shape[1] // (N_GROUPS * HEAD_DIM)
    C = RET_CHUNK
    nc = S // C
    scale = HEAD_DIM ** -0.5

    def body(df_ref, db_ref, q_ref, k_ref, v_ref, do, dq_ref, dk_ref, dv_ref, small_ref, states, d_states):
        h = pl.program_id(0)
        lgs = [_log_decay(df_ref, h), _log_decay(db_ref, h)]
        consts = [_ret_consts(lg, fw) for lg, fw in zip(lgs, (True, False))]

        def prep_step(b, carry):
            rows, batch = _batch_rows(b), _batch_chunks(b)
            q3 = _chunks3(q_ref[rows, :] * scale)
            k3 = _chunks3(k_ref[rows, :])
            v3 = _chunks3(v_ref[rows, :]).astype(BF16)
            do3 = _chunks3(do[rows, :]).astype(BF16)
            for way in range(2):
                states[way, batch] = _bdot((k3 * consts[way][4]).astype(BF16), v3, _BTN)
                d_states[way, batch] = _bdot((q3 * consts[way][2]).astype(BF16), do3, _BTN)
            return carry

        lax.fori_loop(0, nc // CHUNK_BATCH, prep_step, 0)
        c_decs = [c[6] for c in consts]
        _ret_scan(states, c_decs, nc, False)
        _ret_scan(d_states, c_decs, nc, True)

        def main_step(b, dlams):
            rows, batch = _batch_rows(b), _batch_chunks(b)
            q3 = _chunks3(q_ref[rows, :] * scale)
            k3 = _chunks3(k_ref[rows, :])
            q3b, k3b = q3.astype(BF16), k3.astype(BF16)
            v3b = _chunks3(v_ref[rows, :]).astype(BF16)
            do3b = _chunks3(do[rows, :]).astype(BF16)
            a0 = _bdot(q3b, k3b, _BNT)
            pv = _bdot(do3b, v3b, _BNT)
            dq = dk = dv = None
            new_dlams = []
            for way in range(2):
                mask, relf, q_dec, q_exp, k_dec, k_exp, c_dec = consts[way]
                state, d_state = states[way, batch], d_states[way, batch]
                dp = pv * mask
                dpb = dp.astype(BF16)
                gq = _bdot(do3b, state.astype(BF16), _BNT)
                gk = _bdot(v3b, d_state.astype(BF16), _BNT)
                parts = (_bdot(dpb, k3b, _BNN) + q_dec * gq, _bdot(dpb, q3b, _BTN) + k_dec * gk,
                         _bdot((a0 * mask).astype(BF16), do3b, _BTN)
                         + _bdot((k3 * k_dec).astype(BF16), d_state.astype(BF16), _BNN))
                dq, dk, dv = parts if dq is None else (dq + parts[0], dk + parts[1], dv + parts[2])
                total = lambda x: jnp.sum(jnp.sum(x, axis=0), axis=0, keepdims=True)
                new_dlams.append(dlams[way] + total(relf * a0 * dp)
                                 + total(q_exp * q_dec * q3 * gq + k_exp * k_dec * k3 * gk)
                                 + (C * c_dec) * total(state * d_state))
            flat = lambda x: x.reshape(CHUNK_BATCH * C, HEAD_DIM)
            dq_ref[rows, :] = flat(dq) * scale
            dk_ref[rows, :] = flat(dk)
            dv_ref[rows, :] = flat(dv)
            return tuple(new_dlams)

        dlams = lax.fori_loop(0, nc // CHUNK_BATCH, main_step, (jnp.zeros((1, HEAD_DIM), F32),) * 2)
        for row, (dlam, lg) in enumerate(zip(dlams, lgs)):
            small_ref[0, pl.ds(row, 1), :] = jnp.broadcast_to(jnp.sum(dlam, axis=-1, keepdims=True) * lg, (1, HEAD_DIM))
        small_ref[0, pl.ds(2, 6), :] = jnp.zeros((6, HEAD_DIM), F32)

    hspec = pl.BlockSpec((S, HEAD_DIM), lambda h: (0, h))
    smem = pl.BlockSpec(memory_space=pltpu.SMEM)
    return pl.pallas_call(
        body, grid=(H,), name="ret_bwd",
        in_specs=[smem, smem] + _head_specs(S, (3, 4, 5), H) + [hspec],
        out_specs=[hspec, hspec, hspec, pl.BlockSpec((1, 8, HEAD_DIM), lambda h: (h, 0, 0))],
        out_shape=[SDS((S, H * HEAD_DIM), F32)] * 3 + [SDS((H, 8, HEAD_DIM), F32)],
        scratch_shapes=[pltpu.VMEM((2, nc, HEAD_DIM, HEAD_DIM), F32), pltpu.VMEM((2, nc, HEAD_DIM, HEAD_DIM), F32)],
        compiler_params=_cp(1))(dec_f, dec_b, proj, proj, proj, d_out)


def _ffn_bwd_act(dh2, wd, g, u):
    S, D = dh2.shape
    nblk, _, FB = g.shape
    tm = min(512, S)

    def body(dh_ref, wd_ref, g_ref, u_ref, dg_ref, du_ref):
        dact = _dot(dh_ref[...], wd_ref[...], _NT)
        gg = g_ref[0].astype(F32)
        sg = _sigmoid(gg)
        dg_ref[0] = (dact * u_ref[0].astype(F32) * (sg * (1.0 + gg * (1.0 - sg)))).astype(BF16)
        du_ref[0] = (dact * (gg * sg)).astype(BF16)

    blk = pl.BlockSpec((1, tm, FB), lambda j, i: (j, i, 0))
    return pl.pallas_call(
        body, grid=(nblk, S // tm), name="ffn_bwd_act",
        in_specs=[pl.BlockSpec((tm, D), lambda j, i: (i, 0)), pl.BlockSpec((FB, D), lambda j, i: (j, 0)), blk, blk],
        out_specs=[blk, blk], out_shape=[SDS((nblk, S, FB), BF16)] * 2,
        compiler_params=_cp(2))(dh2, wd, g, u)


def _ffn_bwd_in(dg, du, wg, wu, h1, dh2, w_norm):
    nblk, S, FB = dg.shape
    D = h1.shape[1]
    tm = min(512, S)

    def body(dg_ref, du_ref, wg_ref, wu_ref, h_hbm, dh2_hbm, wn_ref, dh_ref, dhb_ref, dw_ref, acc, h_buf, dh2_buf, sems):
        i, j = pl.program_id(0), pl.program_id(1)
        rows = pl.ds(pl.multiple_of(i * tm, tm), tm)
        fetch = [pltpu.make_async_copy(h_hbm.at[rows, :], h_buf, sems.at[0]),
                 pltpu.make_async_copy(dh2_hbm.at[rows, :], dh2_buf, sems.at[1])]

        @pl.when(j == 0)
        def _():
            for cp in fetch:
                cp.start()
            acc[...] = jnp.zeros_like(acc)

        @pl.when((i == 0) & (j == 0))
        def _():
            dw_ref[...] = jnp.zeros_like(dw_ref)

        acc[...] += _dot(dg_ref[0], wg_ref[0], _NN) + _dot(du_ref[0], wu_ref[0], _NN)

        @pl.when(j == nblk - 1)
        def _():
            for cp in fetch:
                cp.wait()
            dh, dw = _rms_bwd(acc[...], h_buf[...], wn_ref[...])
            dh = dh2_buf[...] + dh
            dh_ref[...] = dh
            dhb_ref[...] = dh.astype(BF16)
            dw_ref[...] += dw

    blk = pl.BlockSpec((1, tm, FB), lambda i, j: (j, i, 0))
    wspec = pl.BlockSpec((1, FB, D), lambda i, j: (j, 0, 0))
    row = pl.BlockSpec((tm, D), lambda i, j: (i, 0))
    vec = pl.BlockSpec((1, D), lambda i, j: (0, 0))
    return pl.pallas_call(
        body, grid=(S // tm, nblk), name="ffn_bwd_in",
        in_specs=[blk, blk, wspec, wspec, ANY, ANY, vec],
        out_specs=[row, row, vec], out_shape=[SDS((S, D), F32), SDS((S, D), BF16), SDS((1, D), F32)],
        scratch_shapes=[pltpu.VMEM((tm, D), F32), pltpu.VMEM((tm, D), F32), pltpu.VMEM((tm, D), F32),
                        pltpu.SemaphoreType.DMA((2,))],
        compiler_params=_cp(2))(dg, du, wg, wu, h1, dh2, w_norm)


def _dmix(dh1, w_out):
    S, D = dh1.shape
    tm = min(512, S)

    def body(dh_ref, w_ref, o_ref):
        o_ref[...] = _dot(dh_ref[...], w_ref[...], _NT)

    row = pl.BlockSpec((tm, D), lambda i: (i, 0))
    return pl.pallas_call(
        body, grid=(S // tm,), name="dmix", in_specs=[row, pl.BlockSpec((D, D), lambda i: (0, 0))],
        out_specs=row, out_shape=SDS((S, D), F32), compiler_params=_cp(1))(dh1, w_out)


def _in_bwd(dproj, w_blk, x, dh1, w_norm):
    S, D = x.shape
    nblk, _, NB = w_blk.shape
    tm = min(512, S)

    def body(dp_ref, w_ref, x_ref, dh1_ref, wn_ref, dx_ref, dw_ref, acc):
        i, j = pl.program_id(0), pl.program_id(1)

        @pl.when(j == 0)
        def _():
            acc[...] = jnp.zeros_like(acc)

        @pl.when((i == 0) & (j == 0))
        def _():
            dw_ref[...] = jnp.zeros_like(dw_ref)

        acc[...] += _dot(dp_ref[...], w_ref[0], _NT)

        @pl.when(j == nblk - 1)
        def _():
            dh, dw = _rms_bwd(acc[...], x_ref[...], wn_ref[...])
            dx_ref[...] = dh1_ref[...] + dh
            dw_ref[...] += dw

    row = pl.BlockSpec((tm, D), lambda i, j: (i, 0))
    vec = pl.BlockSpec((1, D), lambda i, j: (0, 0))
    return pl.pallas_call(
        body, grid=(S // tm, nblk), name="in_bwd",
        in_specs=[pl.BlockSpec((tm, NB), lambda i, j: (i, j)), pl.BlockSpec((1, D, NB), lambda i, j: (j, 0, 0)),
                  row, row, vec],
        out_specs=[row, vec], out_shape=[SDS((S, D), F32), SDS((1, D), F32)],
        scratch_shapes=[pltpu.VMEM((tm, D), F32)], compiler_params=_cp(2))(dproj, w_blk, x, dh1, w_norm)


def _wgrad(a, b, a_spec, b_spec, o_spec, o_shape, grid, name):
    nk = grid[-1]

    def ld(ref):
        return ref[0] if len(ref.shape) == 3 else ref[...]

    def body(a_ref, b_ref, o_ref, acc):
        k = pl.program_id(len(grid) - 1)

        @pl.when(k == 0)
        def _():
            acc[...] = jnp.zeros_like(acc)

        acc[...] += _dot(ld(a_ref), ld(b_ref), _TN)

        @pl.when(k == nk - 1)
        def _():
            if len(o_ref.shape) == 3:
                o_ref[0] = acc[...].astype(o_ref.dtype)
            else:
                o_ref[...] = acc[...].astype(o_ref.dtype)

    return pl.pallas_call(
        body, grid=grid, name=name, in_specs=[a_spec, b_spec], out_specs=o_spec, out_shape=SDS(o_shape, BF16),
        scratch_shapes=[pltpu.VMEM(o_spec.block_shape[-2:], F32)], compiler_params=_cp(len(grid)))(a, b)


def _peer(k):
    x, y, c = lax.axis_index("x"), lax.axis_index("y"), lax.axis_index("c")
    px = 1 - x if k & 4 else x
    py = 1 - y if k & 2 else y
    pc = 1 - c if k & 1 else c
    return (px, py, pc), 4 * px + 2 * py + pc


def _exchange_copies(srcs, lands, send_sems, recv_sems, which, gather):
    _, me = _peer(0)
    pairs = []
    for pos, a in enumerate(which):
        for k in range(1, N_DEV):
            dev, idx = _peer(k)
            sem = pos * (N_DEV - 1) + k - 1
            src = srcs[a] if gather else srcs[a].at[idx]
            mk = functools.partial(pltpu.make_async_remote_copy, src_ref=src, send_sem=send_sems.at[sem],
                                   recv_sem=recv_sems.at[sem], device_id=dev, device_id_type=MESH)
            pairs.append((mk(dst_ref=lands[a].at[me]), mk(dst_ref=lands[a].at[idx])))
    return pairs


def _sequencer_kernel(name, collective_id, n_remote, n_local):
    return pl.kernel(mesh=plsc.ScalarSubcoreMesh(axis_name="sequencer", num_cores=1), name=name,
                     scratch_types=(pltpu.SemaphoreType.DMA((n_remote,)), pltpu.SemaphoreType.DMA((n_remote,)),
                                    pltpu.SemaphoreType.DMA((n_local,))),
                     compiler_params=pltpu.CompilerParams(collective_id=collective_id))


def _handshake(ks):
    barrier = pltpu.get_barrier_semaphore()
    for k in ks:
        pl.semaphore_signal(barrier, inc=1, device_id=_peer(k)[0], device_id_type=MESH)
    pl.semaphore_wait(barrier, len(ks))


def _sequencer_scatter(arrays, name, collective_id):
    n = len(arrays)
    hbm = pltpu.MemorySpace.HBM
    srcs = [jax.new_ref(a, memory_space=hbm) for a in arrays]
    lands = [jax.empty_ref(SDS(a.shape, a.dtype), memory_space=hbm) for a in arrays]

    @_sequencer_kernel(name, collective_id, n * (N_DEV - 1), n)
    def launch(send_sems, recv_sems, local_sems):
        _handshake(range(1, N_DEV))
        _, me = _peer(0)
        local = [pltpu.make_async_copy(srcs[a].at[me], lands[a].at[me], local_sems.at[a]) for a in range(n)]
        pairs = _exchange_copies(srcs, lands, send_sems, recv_sems, range(n), False)
        for out, _ in pairs:
            out.start()
        for cp in local:
            cp.start()
        for out, arrival in pairs:
            out.wait_send()
            arrival.wait_recv()
        for cp in local:
            cp.wait()

    launch()
    return [r[...] for r in lands]


SIBLING = 1
OTHER_CHIPS = (2, 4, 6)


def _sequencer_gather(arrays, name, collective_id):
    n = len(arrays)
    hbm = pltpu.MemorySpace.HBM
    srcs = [jax.new_ref(a, memory_space=hbm) for a in arrays]
    lands = [jax.empty_ref(SDS((N_DEV,) + a.shape, a.dtype), memory_space=hbm) for a in arrays]

    @_sequencer_kernel(name, collective_id, n * (N_DEV - 1), n)
    def launch(send_sems, recv_sems, local_sems):
        _handshake((SIBLING,) + OTHER_CHIPS)
        _, me = _peer(0)
        sibling, _ = _peer(SIBLING)

        def copy(a, k, src, block, to):
            sem = a * (N_DEV - 1) + k - 1
            return pltpu.make_async_remote_copy(src_ref=src, dst_ref=lands[a].at[block], send_sem=send_sems.at[sem],
                                                recv_sem=recv_sems.at[sem], device_id=to, device_id_type=MESH)

        local = [pltpu.make_async_copy(srcs[a], lands[a].at[me], local_sems.at[a]) for a in range(n)]
        first = [copy(a, k, srcs[a], me, _peer(k)[0]) for a in range(n) for k in OTHER_CHIPS + (SIBLING,)]
        for cp in first + local:
            cp.start()
        passed = []
        for a in range(n):
            for k in OTHER_CHIPS:
                _, block = _peer(k)
                copy(a, k, srcs[a], block, sibling).wait_recv()
                passed.append(copy(a, k ^ SIBLING, lands[a].at[block], block, sibling))
                passed[-1].start()
        for a in range(n):
            for k in (SIBLING,) + tuple(k ^ SIBLING for k in OTHER_CHIPS):
                copy(a, k, srcs[a], _peer(k)[1], sibling).wait_recv()
        for cp in first + passed:
            cp.wait_send()
        for cp in local:
            cp.wait()

    launch()
    return [r[...] for r in lands]


SMALL_ROWS = 64


def _small_step(part, w, m, v):
    def body(p_ref, w_ref, m_ref, v_ref, g_ref, d_ref, nm_ref, nv_ref, gath, send_sems, recv_sems):
        _, me = _peer(0)
        gath[me] = p_ref[...]
        copies = []
        for k in range(1, N_DEV):
            dev, idx = _peer(k)
            out = pltpu.make_async_remote_copy(src_ref=p_ref, dst_ref=gath.at[me], send_sem=send_sems.at[k - 1],
                                               recv_sem=recv_sems.at[k - 1], device_id=dev, device_id_type=MESH)
            out.start()
            arrival = pltpu.make_async_remote_copy(src_ref=p_ref, dst_ref=gath.at[idx], send_sem=send_sems.at[k - 1],
                                                   recv_sem=recv_sems.at[k - 1], device_id=dev, device_id_type=MESH)
            copies.append((out, arrival))
        for out, arrival in copies:
            out.wait_send()
            arrival.wait_recv()
        g = gath[0]
        for p in range(1, N_DEV):
            g = g + gath[p]
        g_ref[...] = g
        d_ref[...], nm_ref[...], nv_ref[...] = _adamw(w_ref[...], g, m_ref[...], v_ref[...])

    vm = pl.BlockSpec(memory_space=pltpu.VMEM)
    return pl.pallas_call(
        body, name="small_step", in_specs=[vm] * 4, out_specs=[vm] * 4,
        out_shape=[SDS((SMALL_ROWS, 128), F32)] * 4,
        scratch_shapes=[pltpu.VMEM((N_DEV, SMALL_ROWS, 128), F32), pltpu.SemaphoreType.DMA((N_DEV - 1,)),
                        pltpu.SemaphoreType.DMA((N_DEV - 1,))])(part, w, m, v)


def _adamw(w, g, m, v):
    m = ADAM_B1 * m + (1.0 - ADAM_B1) * g
    v = ADAM_B2 * v + (1.0 - ADAM_B2) * (g * g)
    m_hat = m / (1.0 - ADAM_B1 ** ADAM_STEP)
    v_hat = v / (1.0 - ADAM_B2 ** ADAM_STEP)
    delta = -ADAM_LR * (m_hat / (jnp.sqrt(v_hat) + ADAM_EPS) + ADAM_WD * w)
    return delta, m, v


def _adamw_block(parts, w, m, v, name):
    R, C = w.shape
    n_parts = len(parts)
    Rp = R // n_parts
    tr = next(t for t in (256, 128, 64, 32, 16, 8) if Rp % t == 0 and t * C <= 256 * 1024)
    per_part = Rp // tr

    def body(*refs):
        p_refs = refs[:n_parts]
        w_ref, m_ref, v_ref, g_ref, d_ref, nm_ref, nv_ref = refs[n_parts:]
        for k, p_ref in enumerate(p_refs):
            @pl.when(pl.program_id(0) // per_part == k)
            def _(p_ref=p_ref):
                g = p_ref[0].astype(F32)
                for p in range(1, N_DEV):
                    g = g + p_ref[p].astype(F32)
                g_ref[...] = g
                d_ref[...], nm_ref[...], nv_ref[...] = _adamw(w_ref[...], g, m_ref[...], v_ref[...])

    row = pl.BlockSpec((tr, C), lambda i: (i, 0))
    part_specs = [pl.BlockSpec((N_DEV, tr, C), functools.partial(
        lambda i, k: (0, jnp.clip(i - k * per_part, 0, per_part - 1), 0), k=k)) for k in range(n_parts)]
    return pl.pallas_call(
        body, grid=(R // tr,), name=name, in_specs=part_specs + [row, row, row],
        out_specs=[row] * 4, out_shape=[SDS((R, C), F32)] * 4, compiler_params=_cp(1))(*parts, w, m, v)


def _pack_small(mix, ffn, fin, retw, dec_f, dec_b, loss):
    flat = jnp.concatenate([mix.reshape(-1), ffn.reshape(-1), fin.reshape(-1), retw.reshape(-1), dec_f.reshape(-1),
                            dec_b.reshape(-1), loss.reshape(-1)])
    return jnp.pad(flat, (0, SMALL_ROWS * 128 - flat.shape[0])).reshape(SMALL_ROWS, 128)


def _unpack_small(packed, shapes):
    flat = packed.reshape(-1)
    out, at = [], 0
    for s in shapes:
        n = math.prod(s)
        out.append(flat[at:at + n].reshape(s))
        at += n
    return out


def kernel(x, norm_mix_w, w_in, ret_decay_fwd, ret_decay_bwd, ret_norm_w, w_out, norm_ffn_w, w_gate, w_up, w_down, norm_final_w, loss_target, m_norm_mix_w, m_w_in, m_ret_decay_fwd, m_ret_decay_bwd, m_ret_norm_w, m_w_out, m_norm_ffn_w, m_w_gate, m_w_up, m_w_down, m_norm_final_w, v_norm_mix_w, v_w_in, v_ret_decay_fwd, v_ret_decay_bwd, v_ret_norm_w, v_w_out, v_norm_ffn_w, v_w_gate, v_w_up, v_w_down, v_norm_final_w):
    x2 = x[0]
    tgt = loss_target[0]
    S, D = x2.shape
    H = ret_norm_w.shape[1] // HEAD_DIM
    DA = H * HEAD_DIM
    fin_w = norm_final_w.reshape(1, D)
    big = (w_in[0], w_out[0], w_gate[0].T, w_up[0].T, w_down[0])

    big_b = [w.astype(BF16) for w in big]
    wi, = _sequencer_gather(big_b[:1], "gather_in", 0)
    wo, wg, wu = _sequencer_gather(big_b[1:4], "gather_mid", 1)
    wd, = _sequencer_gather(big_b[4:], "gather_down", 5)
    NB = wi.shape[2]

    proj, n1 = _proj_fwd(x2, norm_mix_w, wi)
    bias = _attn_bias()[:H]
    attn, lse = _attn_fwd(proj, bias)
    ret, o_raw = _ret_fwd(proj, ret_decay_fwd, ret_decay_bwd, ret_norm_w)
    wo_full = wo.reshape(D, D)
    d_ff = N_DEV * wd.shape[1]
    FB = FFN_BLOCK if d_ff % FFN_BLOCK == 0 else wd.shape[1]
    n_fb = d_ff // FB
    wg, wu = wg.reshape(n_fb, FB, D), wu.reshape(n_fb, FB, D)
    wd_full = wd.reshape(d_ff, D)
    h1, mixed, n2 = _out_fwd(x2, attn, ret, wo_full, norm_ffn_w)
    gate, up, act = _ffn_up(n2, wg, wu)
    dh2, dh2_b, loss_parts, g_fin = _ffn_down_loss(act, wd_full, h1, tgt, fin_w)

    dgate, dup = _ffn_bwd_act(dh2_b, wd_full, gate, up)
    tn = min(1024, D)
    ffn_specs = (pl.BlockSpec((1, S, FB), lambda j, n, k: (j, 0, 0)), pl.BlockSpec((S, tn), lambda j, n, k: (0, n)),
                 pl.BlockSpec((1, FB, tn), lambda j, n, k: (j, 0, n)), (n_fb, FB, D), (n_fb, D // tn, 1))
    per_dev = (N_DEV, d_ff // N_DEV, D)
    g_wd = _wgrad(act, dh2_b, *ffn_specs, "wgrad_down").reshape(per_dev)
    g_wg = _wgrad(dgate, n2, *ffn_specs, "wgrad_gate").reshape(per_dev)
    g_wu = _wgrad(dup, n2, *ffn_specs, "wgrad_up").reshape(per_dev)
    parts_f = _sequencer_scatter([g_wg, g_wu, g_wd], "scatter_ffn", 2)
    dh1, dh1_b, g_ffn = _ffn_bwd_in(dgate, dup, wg, wu, h1, dh2, norm_ffn_w)
    dmix = _dmix(dh1_b, wo_full)
    tmw = min(512, D)
    tk = min(2048, S)
    g_wo = _wgrad(mixed, dh1_b, pl.BlockSpec((tk, tmw), lambda m, k: (k, m)), pl.BlockSpec((tk, D), lambda m, k: (k, 0)),
                  pl.BlockSpec((tmw, D), lambda m, k: (m, 0)), (D, D), (D // tmw, S // tk), "wgrad_out")
    parts_o = _sequencer_scatter([g_wo.reshape(N_DEV, D // N_DEV, D)], "scatter_out", 3)
    d_ret, dg_r, small_w = _ret_gate_bwd(proj, o_raw, dmix, ret_norm_w, DA)
    dq_r, dk_r, dv_r, small = _ret_bwd(proj, d_ret, ret_decay_fwd, ret_decay_bwd)
    dq_a, dk_a, dv_a = _attn_bwd(proj, attn, lse, dmix, bias)
    dproj = jnp.concatenate([t.astype(BF16) for t in (dq_a, dk_a, dv_a, dq_r, dk_r, dv_r, dg_r)], axis=1)
    half = D // tmw // 2
    parts_i = []
    for part, (name, cid) in enumerate((("in_lo", 4), ("in_hi", 6))):
        g_wi = _wgrad(n1, dproj, pl.BlockSpec((S, tmw), functools.partial(lambda j, m, k, off: (0, m + off), off=part * half)),
                      pl.BlockSpec((S, NB), lambda j, m, k: (0, j)), pl.BlockSpec((1, tmw, NB), lambda j, m, k: (j, m, 0)),
                      (N_DEV, D // 2, NB), (N_DEV, half, 1), "wgrad_" + name)
        parts_i += _sequencer_scatter([g_wi], "scatter_" + name, cid)
    grad_x, g_mix = _in_bwd(dproj, wi, x2, dh1, norm_mix_w)

    big_m = (m_w_in[0], m_w_out[0], m_w_gate[0].T, m_w_up[0].T, m_w_down[0])
    big_v = (v_w_in[0], v_w_out[0], v_w_gate[0].T, v_w_up[0].T, v_w_down[0])
    names = ("adamw_in", "adamw_out", "adamw_gate", "adamw_up", "adamw_down")
    upd = [None] * 5
    for a, p in zip((2, 3, 4, 1, 0), [[t] for t in parts_f + parts_o] + [parts_i]):
        upd[a] = _adamw_block(p, big[a], big_m[a], big_v[a], names[a])

    g_dec_f = small[:, 0, 0].reshape(1, H)
    g_dec_b = small[:, 1, 0].reshape(1, H)
    g_retw = small_w[:, 0, :].reshape(1, DA)
    loss_local = jnp.sum(loss_parts[::8, 0])
    zero = jnp.zeros((1,), F32)
    part = _pack_small(g_mix, g_ffn, g_fin, g_retw, g_dec_f, g_dec_b, loss_local)
    sw = _pack_small(norm_mix_w, norm_ffn_w, norm_final_w, ret_norm_w, ret_decay_fwd, ret_decay_bwd, zero)
    sm = _pack_small(m_norm_mix_w, m_norm_ffn_w, m_norm_final_w, m_ret_norm_w, m_ret_decay_fwd, m_ret_decay_bwd, zero)
    sv = _pack_small(v_norm_mix_w, v_norm_ffn_w, v_norm_final_w, v_ret_norm_w, v_ret_decay_fwd, v_ret_decay_bwd, zero)
    shapes = [(1, D), (1, D), (D,), (1, DA), (1, H), (1, H), ()]
    sg, sd, snm, snv = [_unpack_small(t, shapes) for t in _small_step(part, sw, sm, sv)]
    loss = sg[6]

    def ordered(small_set, k):
        b = [(u[k].T if a in (2, 3) else u[k])[None] for a, u in enumerate(upd)]
        return [small_set[0], b[0], small_set[4], small_set[5], small_set[3], b[1], small_set[1], b[2], b[3], b[4],
                small_set[2]]

    return (loss, grad_x[None], *ordered(sg, 0), *ordered(sd, 1), *ordered(snm, 2), *ordered(snv, 3))
```

```python
import functools
import math

import numpy as np
import jax
import jax.numpy as jnp
from jax import lax
from jax.experimental import pallas as pl
from jax.experimental.pallas import tpu as pltpu
from jax.experimental.pallas import tpu_sc as plsc

F32 = jnp.float32
BF16 = jnp.bfloat16
SDS = jax.ShapeDtypeStruct

HEAD_DIM = 128
EPS = 1e-6
RET_CHUNK = 128
DILATIONS = (1, 4, 16)
BAND = 64
Q_TILE = 128
K_TILE = Q_TILE + 2 * BAND
KV_PAD = BAND * 4
TILE_GROUP = 8
NEG = -1e30
N_DEV = 8
N_GROUPS = 7
ADAM_LR, ADAM_B1, ADAM_B2, ADAM_EPS, ADAM_WD, ADAM_STEP = 0.001, 0.9, 0.999, 1e-08, 0.01, 10
VMEM_LIMIT = 56 * 1024 * 1024
MESH = pl.DeviceIdType.MESH
ANY = pl.BlockSpec(memory_space=pl.ANY)


def _cp(n_grid):
    return pltpu.CompilerParams(dimension_semantics=("arbitrary",) * n_grid, vmem_limit_bytes=VMEM_LIMIT)


def _sigmoid(x):
    return 1.0 / (1.0 + jnp.exp(-x))


def _rms_scale(h):
    return lax.rsqrt(jnp.mean(h * h, axis=-1, keepdims=True) + EPS)


def _rms_bwd(dn, h, w):
    r = _rms_scale(h)
    gw = dn * w
    dh = r * gw - h * (r * r * r) * jnp.mean(gw * h, axis=-1, keepdims=True)
    return dh, jnp.sum(dn * h * r, axis=0, keepdims=True)


def _dot(a, b, dims):
    return lax.dot_general(a.astype(BF16), b.astype(BF16), (dims, ((), ())), preferred_element_type=F32)


_NN = ((1,), (0,))
_NT = ((1,), (1,))
_TN = ((0,), (0,))


def _proj_fwd(x, w_norm, w_blk):
    S, D = x.shape
    nblk, _, NB = w_blk.shape
    tm = min(1024, S)

    def body(x_ref, wn_ref, w_ref, proj_ref, n_ref, n_scr):
        @pl.when(pl.program_id(1) == 0)
        def _():
            xf = x_ref[...]
            nb = (xf * _rms_scale(xf) * wn_ref[...]).astype(BF16)
            n_scr[...] = nb
            n_ref[...] = nb
        proj_ref[...] = jnp.dot(n_scr[...], w_ref[0], preferred_element_type=F32)

    return pl.pallas_call(
        body, grid=(S // tm, nblk), name="proj_fwd",
        in_specs=[pl.BlockSpec((tm, D), lambda i, j: (i, 0)), pl.BlockSpec((1, D), lambda i, j: (0, 0)),
                  pl.BlockSpec((1, D, NB), lambda i, j: (j, 0, 0))],
        out_specs=[pl.BlockSpec((tm, NB), lambda i, j: (i, j)), pl.BlockSpec((tm, D), lambda i, j: (i, 0))],
        out_shape=[SDS((S, nblk * NB), F32), SDS((S, D), BF16)],
        scratch_shapes=[pltpu.VMEM((tm, D), BF16)], compiler_params=_cp(2))(x, w_norm, w_blk)


def _out_fwd(x, attn, ret, w_out, w_norm):
    S, D = x.shape
    DA = attn.shape[1]
    tm = min(256, S)

    def body(x_ref, a_ref, r_ref, w_ref, wn_ref, h_ref, mix_ref, n_ref):
        a = a_ref[...].astype(BF16)
        r = r_ref[...].astype(BF16)
        mix_ref[:, :DA] = a
        mix_ref[:, DA:] = r
        h = x_ref[...] + jnp.dot(a, w_ref[:DA, :], preferred_element_type=F32) \
            + jnp.dot(r, w_ref[DA:, :], preferred_element_type=F32)
        h_ref[...] = h
        n_ref[...] = (h * _rms_scale(h) * wn_ref[...]).astype(BF16)

    row = lambda w: pl.BlockSpec((tm, w), lambda i: (i, 0))
    return pl.pallas_call(
        body, grid=(S // tm,), name="out_fwd",
        in_specs=[row(D), row(DA), row(D - DA), pl.BlockSpec((D, D), lambda i: (0, 0)),
                  pl.BlockSpec((1, D), lambda i: (0, 0))],
        out_specs=[row(D), row(D), row(D)],
        out_shape=[SDS((S, D), F32), SDS((S, D), BF16), SDS((S, D), BF16)],
        compiler_params=_cp(1))(x, attn, ret, w_out, w_norm)


def _ffn_up(n2, wg, wu):
    S, D = n2.shape
    nblk, FB, _ = wg.shape
    tm = min(512, S)

    def body(n_ref, wg_ref, wu_ref, g_ref, u_ref, a_ref):
        n = n_ref[...]
        g = _dot(n, wg_ref[0], _NT)
        u = _dot(n, wu_ref[0], _NT)
        g_ref[0] = g.astype(BF16)
        u_ref[0] = u.astype(BF16)
        a_ref[0] = (g * _sigmoid(g) * u).astype(BF16)

    wspec = pl.BlockSpec((1, FB, D), lambda j, i: (j, 0, 0))
    ospec = pl.BlockSpec((1, tm, FB), lambda j, i: (j, i, 0))
    return pl.pallas_call(
        body, grid=(nblk, S // tm), name="ffn_up",
        in_specs=[pl.BlockSpec((tm, D), lambda j, i: (i, 0)), wspec, wspec],
        out_specs=[ospec, ospec, ospec],
        out_shape=[SDS((nblk, S, FB), BF16)] * 3,
        compiler_params=_cp(2))(n2, wg, wu)


def _ffn_down_loss(act, wd, h1, target, w_norm):
    nblk, S, FB = act.shape
    D = h1.shape[1]
    tm = min(512, S)

    def body(a_ref, wd_ref, h_ref, t_ref, wn_ref, dh_ref, dhb_ref, loss_ref, dw_ref, acc):
        i, j = pl.program_id(0), pl.program_id(1)

        @pl.when(j == 0)
        def _():
            acc[...] = h_ref[...]

        @pl.when((i == 0) & (j == 0))
        def _():
            dw_ref[...] = jnp.zeros_like(dw_ref)

        acc[...] += jnp.dot(a_ref[0], wd_ref[...], preferred_element_type=F32)

        @pl.when(j == nblk - 1)
        def _():
            h = acc[...]
            w = wn_ref[...]
            err = h * _rms_scale(h) * w - t_ref[...]
            loss_ref[...] = jnp.full(loss_ref.shape, 0.5 * jnp.sum(err * err) / D, F32)
            dh, dw = _rms_bwd(err * (1.0 / D), h, w)
            dh_ref[...] = dh
            dhb_ref[...] = dh.astype(BF16)
            dw_ref[...] += dw

    row = pl.BlockSpec((tm, D), lambda i, j: (i, 0))
    vec = pl.BlockSpec((1, D), lambda i, j: (0, 0))
    return pl.pallas_call(
        body, grid=(S // tm, nblk), name="ffn_down_loss",
        in_specs=[pl.BlockSpec((1, tm, FB), lambda i, j: (j, i, 0)), pl.BlockSpec((FB, D), lambda i, j: (j, 0)),
                  row, row, vec],
        out_specs=[row, row, pl.BlockSpec((8, 128), lambda i, j: (i, 0)), vec],
        out_shape=[SDS((S, D), F32), SDS((S, D), BF16), SDS((S // tm * 8, 128), F32), SDS((1, D), F32)],
        scratch_shapes=[pltpu.VMEM((tm, D), F32)], compiler_params=_cp(2))(act, wd, h1, target, w_norm)


def _attn_bias():
    n_heads = 8
    slopes = np.exp2(-8.0 * np.arange(1, n_heads + 1, dtype=np.float32) / n_heads)
    dist = np.abs(np.arange(K_TILE)[None, :] - BAND - np.arange(Q_TILE)[:, None])
    out = np.empty((n_heads, len(DILATIONS), Q_TILE, K_TILE), np.float32)
    for h in range(n_heads):
        for p, d in enumerate(DILATIONS):
            out[h, p] = np.where(dist <= BAND, -slopes[h] * (d * dist).astype(np.float32), NEG)
    return jnp.asarray(out)


def _attn_tiles(S, d):
    L = S // d
    per_class = L // Q_TILE
    return L, per_class, d * per_class


def _tile_rows(t, d, per_class):
    r = t // per_class
    a = (t % per_class) * Q_TILE
    q_rows = pl.ds(r + d * a, Q_TILE, stride=d) if d > 1 else pl.ds(pl.multiple_of(a, Q_TILE), Q_TILE)
    k_rows = pl.ds(KV_PAD + r + d * (a - BAND), K_TILE, stride=d) if d > 1 else pl.ds(
        pl.multiple_of(KV_PAD + a - BAND, BAND), K_TILE)
    return a, q_rows, k_rows


def _to_quarters(dst, src, n, dst_off=0):
    for r in range(4):
        dst[pl.ds(dst_off + r * (n // 4), n // 4), :] = src[pl.ds(r, n // 4, stride=4), :]


def _quarter_tile_rows(t, S):
    L = S // 16
    per_class = L // Q_TILE
    blk, tt = t // (4 * per_class), t % (4 * per_class)
    r, a = tt // per_class, (tt % per_class) * Q_TILE
    q_rows = pl.ds(blk * (S // 4) + r + 4 * a, Q_TILE, stride=4)
    k_rows = pl.ds(KV_PAD + blk * (S // 4) + r + 4 * (a - BAND), K_TILE, stride=4)
    return a, q_rows, k_rows


def _lanes(x, width):
    return jnp.concatenate([x] * (width // HEAD_DIM), axis=-1)


_BNT = (((2,), (2,)), ((0,), (0,)))
_BNN = (((2,), (1,)), ((0,), (0,)))
_BTN = (((1,), (1,)), ((0,), (0,)))


def _bdot(a, b, dims):
    return lax.dot_general(a, b, dims, preferred_element_type=F32)


def _stacked(rows, loaders):
    return [jnp.stack([f(*r) for r in rows]) for f in loaders]


def _edge_mask(a, L):
    lk = lax.broadcasted_iota(jnp.int32, (1, K_TILE), 1) + (a - BAND)
    return jnp.where((lk >= 0) & (lk < L), 0.0, NEG).astype(F32)


def _fill_padded(dst, src, S):
    dst[pl.ds(0, KV_PAD), :] = jnp.zeros((KV_PAD, HEAD_DIM), F32)
    dst[pl.ds(KV_PAD + S, KV_PAD), :] = jnp.zeros((KV_PAD, HEAD_DIM), F32)
    dst[pl.ds(KV_PAD, S), :] = src[...]


def _head_specs(S, groups, n_heads):
    return [pl.BlockSpec((S, HEAD_DIM), functools.partial(lambda h, g: (0, g * n_heads + h), g=g)) for g in groups]


def _attn_fwd(proj, bias):
    S = proj.shape[0]
    H = proj.shape[1] // (N_GROUPS * HEAD_DIM)
    scale = HEAD_DIM ** -0.5

    def body(q_ref, k_ref, v_ref, b_ref, o_ref, lse_ref, kp, vp, m_run, l_run, q4, m3, l3, acc3):
        _fill_padded(kp, k_ref, S)
        _fill_padded(vp, v_ref, S)
        o_ref[...] = jnp.zeros_like(o_ref)
        m_run[...] = jnp.full(m_run.shape, NEG, F32)
        l_run[...] = jnp.zeros_like(l_run)
        for p, d in enumerate(DILATIONS[:2]):
            L, per_class, n_tiles = _attn_tiles(S, d)

            def tiles(t, carry, p=p, d=d, L=L, per_class=per_class, n_tiles=n_tiles):
                rows = [_tile_rows(t + u * (n_tiles // TILE_GROUP), d, per_class) for u in range(TILE_GROUP)]
                qs, ks, vs, m_old, l_old, o_old, edge = _stacked(rows, (
                    lambda a, qr, kr: q_ref[qr, :].astype(BF16), lambda a, qr, kr: kp[kr, :].astype(BF16),
                    lambda a, qr, kr: vp[kr, :].astype(BF16), lambda a, qr, kr: m_run[qr, :],
                    lambda a, qr, kr: l_run[qr, :], lambda a, qr, kr: o_ref[qr, :], lambda a, qr, kr: _edge_mask(a, L)))
                s = _bdot(qs, ks, _BNT) * scale + b_ref[0, p][None] + edge
                m_new = jnp.maximum(m_old, jnp.max(s, axis=-1, keepdims=True))
                pr = jnp.exp(s - _lanes(m_new, K_TILE)).astype(BF16)
                alpha = jnp.exp(m_old - m_new)
                l_new = alpha * l_old + _bdot(pr, jnp.ones((TILE_GROUP, K_TILE, HEAD_DIM), BF16), _BNN)
                o_new = alpha * o_old + _bdot(pr, vs, _BNN)
                for u, (_, qr, _) in enumerate(rows):
                    o_ref[qr, :] = o_new[u]
                    m_run[qr, :] = m_new[u]
                    l_run[qr, :] = l_new[u]
                return carry

            lax.fori_loop(0, n_tiles // TILE_GROUP, tiles, 0)

        _to_quarters(q4, q_ref, S)
        _to_quarters(kp, k_ref, S, KV_PAD)
        _to_quarters(vp, v_ref, S, KV_PAD)
        n_tiles = _attn_tiles(S, DILATIONS[2])[2]

        def tiles3(t, carry):
            rows = [_quarter_tile_rows(t + u * (n_tiles // TILE_GROUP), S) for u in range(TILE_GROUP)]
            qs, ks, vs, edge = _stacked(rows, (
                lambda a, qr, kr: q4[qr, :].astype(BF16), lambda a, qr, kr: kp[kr, :].astype(BF16),
                lambda a, qr, kr: vp[kr, :].astype(BF16), lambda a, qr, kr: _edge_mask(a, S // DILATIONS[2])))
            s = _bdot(qs, ks, _BNT) * scale + b_ref[0, 2][None] + edge
            m_new = jnp.broadcast_to(jnp.max(s, axis=-1, keepdims=True), (TILE_GROUP, Q_TILE, HEAD_DIM))
            pr = jnp.exp(s - _lanes(m_new, K_TILE)).astype(BF16)
            l_new = _bdot(pr, jnp.ones((TILE_GROUP, K_TILE, HEAD_DIM), BF16), _BNN)
            o_new = _bdot(pr, vs, _BNN)
            for u, (_, qr, _) in enumerate(rows):
                acc3[qr, :] = o_new[u]
                m3[qr, :] = m_new[u]
                l3[qr, :] = l_new[u]
            return carry

        lax.fori_loop(0, n_tiles // TILE_GROUP, tiles3, 0)
        for r in range(4):
            nat, qtr = pl.ds(r, S // 4, stride=4), pl.ds(r * (S // 4), S // 4)
            m_a, m_b = m_run[nat, :], m3[qtr, :]
            m = jnp.maximum(m_a, m_b)
            w_a, w_b = jnp.exp(m_a - m), jnp.exp(m_b - m)
            l = w_a * l_run[nat, :] + w_b * l3[qtr, :]
            o_ref[nat, :] = (w_a * o_ref[nat, :] + w_b * acc3[qtr, :]) / l
            lse_ref[nat, :] = m + jnp.log(l)

    hspec = pl.BlockSpec((S, HEAD_DIM), lambda h: (0, h))
    padded, plain = pltpu.VMEM((S + 2 * KV_PAD, HEAD_DIM), F32), pltpu.VMEM((S, HEAD_DIM), F32)
    return pl.pallas_call(
        body, grid=(H,), name="attn_fwd",
        in_specs=_head_specs(S, (0, 1, 2), H) + [
            pl.BlockSpec((1, len(DILATIONS), Q_TILE, K_TILE), lambda h: (h, 0, 0, 0))],
        out_specs=[hspec, hspec],
        out_shape=[SDS((S, H * HEAD_DIM), F32), SDS((S, H * HEAD_DIM), F32)],
        scratch_shapes=[padded, padded] + [plain] * 6,
        compiler_params=_cp(1))(proj, proj, proj, bias)


def _attn_bwd(proj, out, lse, dmix, bias):
    S = proj.shape[0]
    H = proj.shape[1] // (N_GROUPS * HEAD_DIM)
    scale = HEAD_DIM ** -0.5
    assert S // DILATIONS[2] >= 2 * Q_TILE

    def body(q_ref, k_ref, v_ref, o_ref, lse_ref, do_ref, b_ref, dq_ref, dk_ref, dv_ref,
             kp, vp, dkp, dvp, dsum, q4, do4, lse4, dsum4):
        _fill_padded(kp, k_ref, S)
        _fill_padded(vp, v_ref, S)
        dkp[...] = jnp.zeros_like(dkp)
        dvp[...] = jnp.zeros_like(dvp)
        dq_ref[...] = jnp.zeros_like(dq_ref)
        dsum[...] = jnp.broadcast_to(jnp.sum(do_ref[...] * o_ref[...], axis=-1, keepdims=True), dsum.shape)

        def run(n_tiles, tile_rows, p, L, q_src, do_src, lse_src, dsum_src, dq_dst, dq_adds):
            def tiles(t, carry):
                rows = [tile_rows(t + u * (n_tiles // TILE_GROUP)) for u in range(TILE_GROUP)]
                qs, ks, vs, dos, lses, dsums, dk_old, dv_old, edge = _stacked(rows, (
                    lambda a, qr, kr: q_src[qr, :].astype(BF16), lambda a, qr, kr: kp[kr, :].astype(BF16),
                    lambda a, qr, kr: vp[kr, :].astype(BF16), lambda a, qr, kr: do_src[qr, :].astype(BF16),
                    lambda a, qr, kr: lse_src[qr, :], lambda a, qr, kr: dsum_src[qr, :],
                    lambda a, qr, kr: dkp[kr, :], lambda a, qr, kr: dvp[kr, :], lambda a, qr, kr: _edge_mask(a, L)))
                s = _bdot(qs, ks, _BNT) * scale + b_ref[0, p][None] + edge
                pr = jnp.exp(s - _lanes(lses, K_TILE))
                ds = (pr * (_bdot(dos, vs, _BNT) - _lanes(dsums, K_TILE)) * scale).astype(BF16)
                dq_new = _bdot(ds, ks, _BNN)
                if dq_adds:
                    dq_new = dq_new + jnp.stack([dq_dst[qr, :] for _, qr, _ in rows])
                dk_new = dk_old + _bdot(ds, qs, _BTN)
                dv_new = dv_old + _bdot(pr.astype(BF16), dos, _BTN)
                for u, (_, qr, kr) in enumerate(rows):
                    dq_dst[qr, :] = dq_new[u]
                    dkp[kr, :] = dk_new[u]
                    dvp[kr, :] = dv_new[u]
                return carry

            lax.fori_loop(0, n_tiles // TILE_GROUP, tiles, 0)

        for p, d in enumerate(DILATIONS[:2]):
            L, per_class, n_tiles = _attn_tiles(S, d)
            run(n_tiles, functools.partial(_tile_rows, d=d, per_class=per_class), p, L,
                q_ref, do_ref, lse_ref, dsum, dq_ref, True)
        dk_ref[...] = dkp[pl.ds(KV_PAD, S), :]
        dv_ref[...] = dvp[pl.ds(KV_PAD, S), :]

        for dst, src in ((q4, q_ref), (do4, do_ref), (lse4, lse_ref), (dsum4, dsum)):
            _to_quarters(dst, src, S)
        _to_quarters(kp, k_ref, S, KV_PAD)
        _to_quarters(vp, v_ref, S, KV_PAD)
        dkp[...] = jnp.zeros_like(dkp)
        dvp[...] = jnp.zeros_like(dvp)
        dq3 = dsum
        run(_attn_tiles(S, DILATIONS[2])[2], functools.partial(_quarter_tile_rows, S=S), 2, S // DILATIONS[2],
            q4, do4, lse4, dsum4, dq3, False)
        for r in range(4):
            nat, qtr = pl.ds(r, S // 4, stride=4), pl.ds(r * (S // 4), S // 4)
            pad_qtr = pl.ds(KV_PAD + r * (S // 4), S // 4)
            dq_ref[nat, :] = dq_ref[nat, :] + dq3[qtr, :]
            dk_ref[nat, :] = dk_ref[nat, :] + dkp[pad_qtr, :]
            dv_ref[nat, :] = dv_ref[nat, :] + dvp[pad_qtr, :]

    hspec = pl.BlockSpec((S, HEAD_DIM), lambda h: (0, h))
    once =pl.BlockSpec((S, HEAD_DIM), lambda h: (0, h), pipeline_mode=pl.Buffered(1))
    padded, plain = pltpu.VMEM((S + 2 * KV_PAD, HEAD_DIM), F32), pltpu.VMEM((S, HEAD_DIM), F32)
    return pl.pallas_call(
        body, grid=(H,), name="attn_bwd",
        in_specs=_head_specs(S, (0, 1, 2), H) + [once, once, once,
                                                  pl.BlockSpec((1, len(DILATIONS), Q_TILE, K_TILE), lambda h: (h, 0, 0, 0))],
        out_specs=[hspec, hspec, hspec],
        out_shape=[SDS((S, H * HEAD_DIM), F32)] * 3,
        scratch_shapes=[padded] * 4 + [plain] * 5,
        compiler_params=_cp(1))(proj, proj, proj, out, lse, dmix, bias)


def _ret_consts(lg, forward):
    C = RET_CHUNK
    i = lax.broadcasted_iota(jnp.int32, (C, C), 0)
    j = lax.broadcasted_iota(jnp.int32, (C, C), 1)
    rel = (i - j) if forward else (j - i)
    inside = (rel >= 0) if forward else (rel > 0)
    relf = jnp.maximum(rel, 0).astype(F32)
    mask = jnp.where(inside, jnp.exp(lg * relf), 0.0)
    idx = lax.broadcasted_iota(jnp.int32, (C, 1), 0).astype(F32)
    q_exp = (idx + 1.0) if forward else (C - idx)
    k_exp = (C - 1.0 - idx) if forward else idx
    return mask, relf, jnp.exp(lg * q_exp), q_exp, jnp.exp(lg * k_exp), k_exp, jnp.exp(lg * C)


def _log_decay(dec_ref, h):
    return -jnp.exp(jnp.full((1, 1), dec_ref[0, h], F32))


FFN_BLOCK = 704
CHUNK_BATCH = 8


def _batch_rows(b):
    n = CHUNK_BATCH * RET_CHUNK
    return pl.ds(pl.multiple_of(b * n, n), n)


def _batch_chunks(b):
    return pl.ds(pl.multiple_of(b * CHUNK_BATCH, CHUNK_BATCH), CHUNK_BATCH)


def _chunks3(x):
    return x.reshape(CHUNK_BATCH, RET_CHUNK, HEAD_DIM)


def _ret_scan(buf, c_decs, nc, reverse):
    def step(n, carry):
        new = []
        for way, r in enumerate(carry):
            c = n if (way == 0) != reverse else nc - 1 - n
            term = buf[way, c]
            buf[way, c] = r
            new.append(r * c_decs[way] + term)
        return tuple(new)

    lax.fori_loop(0, nc, step, (jnp.zeros((HEAD_DIM, HEAD_DIM), F32),) * 2)


def _ret_fwd(proj, dec_f, dec_b, w_norm):
    S = proj.shape[0]
    H = proj.shape[1] // (N_GROUPS * HEAD_DIM)
    nc = S // RET_CHUNK
    scale = HEAD_DIM ** -0.5

    def body(df_ref, db_ref, q_ref, k_ref, v_ref, g_ref, w_ref, y_ref, o_ref, states):
        h = pl.program_id(0)
        consts = [_ret_consts(_log_decay(dref, h), fw) for fw, dref in ((True, df_ref), (False, db_ref))]

        def kv_step(b, carry):
            rows, batch = _batch_rows(b), _batch_chunks(b)
            k3 = _chunks3(k_ref[rows, :])
            v3 = _chunks3(v_ref[rows, :]).astype(BF16)
            for way in range(2):
                states[way, batch] = _bdot((k3 * consts[way][4]).astype(BF16), v3, _BTN)
            return carry

        lax.fori_loop(0, nc // CHUNK_BATCH, kv_step, 0)
        _ret_scan(states, [c[6] for c in consts], nc, False)

        def out_step(b, carry):
            rows, batch = _batch_rows(b), _batch_chunks(b)
            q3 = _chunks3(q_ref[rows, :] * scale)
            k3 = _chunks3(k_ref[rows, :]).astype(BF16)
            v3 = _chunks3(v_ref[rows, :]).astype(BF16)
            a0 = _bdot(q3.astype(BF16), k3, _BNT)
            o = None
            for way in range(2):
                mask, q_dec = consts[way][0], consts[way][2]
                part = _bdot((a0 * mask).astype(BF16), v3, _BNN) \
                    + _bdot((q3 * q_dec).astype(BF16), states[way, batch].astype(BF16), _BNN)
                o = part if o is None else o + part
            o_ref[rows, :] = o.reshape(CHUNK_BATCH * RET_CHUNK, HEAD_DIM)
            return carry

        lax.fori_loop(0, nc // CHUNK_BATCH, out_step, 0)
        o = o_ref[...]
        g = g_ref[...]
        y_ref[...] = o * _rms_scale(o) * w_ref[...] * (g * _sigmoid(g))

    hspec = pl.BlockSpec((S, HEAD_DIM), lambda h: (0, h))
    smem = pl.BlockSpec(memory_space=pltpu.SMEM)
    return pl.pallas_call(
        body, grid=(H,), name="ret_fwd",
        in_specs=[smem, smem] + _head_specs(S, (3, 4, 5, 6), H) + [pl.BlockSpec((1, HEAD_DIM), lambda h: (0, h))],
        out_specs=[hspec, hspec],
        out_shape=[SDS((S, H * HEAD_DIM), F32)] * 2,
        scratch_shapes=[pltpu.VMEM((2, nc, HEAD_DIM, HEAD_DIM), F32)],
        compiler_params=_cp(1))(dec_f, dec_b, proj, proj, proj, proj, w_norm)


def _ret_gate_bwd(proj, o_raw, dmix, w_norm, col0):
    S = proj.shape[0]
    H = proj.shape[1] // (N_GROUPS * HEAD_DIM)

    def body(g_ref, o_ref, dy_ref, w_ref, do_ref, dg_ref, dw_ref):
        o = o_ref[...]
        g = g_ref[...]
        dy = dy_ref[...]
        w = w_ref[...]
        rr = _rms_scale(o)
        normed = o * rr
        sg = _sigmoid(g)
        silu = g * sg
        dw_ref[0] = jnp.broadcast_to(jnp.sum(dy * normed * silu, axis=0, keepdims=True), (8, HEAD_DIM))
        dg_ref[...] = dy * normed * w * (sg * (1.0 + g * (1.0 - sg)))
        dnormed = dy * w * silu
        do_ref[...] = rr * dnormed - o * (rr * rr * rr) * jnp.mean(dnormed * o, axis=-1, keepdims=True)

    hspec = pl.BlockSpec((S, HEAD_DIM), lambda h: (0, h))
    nh0 = col0 // HEAD_DIM
    return pl.pallas_call(
        body, grid=(H,), name="ret_gate_bwd",
        in_specs=_head_specs(S, (6,), H) + [hspec, pl.BlockSpec((S, HEAD_DIM), lambda h: (0, nh0 + h)),
                                            pl.BlockSpec((1, HEAD_DIM), lambda h: (0, h))],
        out_specs=[hspec, hspec, pl.BlockSpec((1, 8, HEAD_DIM), lambda h: (h, 0, 0))],
        out_shape=[SDS((S, H * HEAD_DIM), F32)] * 2 + [SDS((H, 8, HEAD_DIM), F32)],
        compiler_params=_cp(1))(proj, o_raw, dmix, w_norm)


def _ret_bwd(proj, d_out, dec_f, dec_b):
    S = proj.shape[0]
    H = proj.shape[1] // (N_GROUPS * HEAD_DIM)
    C = RET_CHUNK
    nc = S // C
    scale = HEAD_DIM ** -0.5

    def body(df_ref, db_ref, q_ref, k_ref, v_ref, do, dq_ref, dk_ref, dv_ref, small_ref, states, d_states):
        h = pl.program_id(0)
        lgs = [_log_decay(df_ref, h), _log_decay(db_ref, h)]
        consts = [_ret_consts(lg, fw) for lg, fw in zip(lgs, (True, False))]

        def prep_step(b, carry):
            rows, batch = _batch_rows(b), _batch_chunks(b)
            q3 = _chunks3(q_ref[rows, :] * scale)
            k3 = _chunks3(k_ref[rows, :])
            v3 = _chunks3(v_ref[rows, :]).astype(BF16)
            do3 = _chunks3(do[rows, :]).astype(BF16)
            for way in range(2):
                states[way, batch] = _bdot((k3 * consts[way][4]).astype(BF16), v3, _BTN)
                d_states[way, batch] = _bdot((q3 * consts[way][2]).astype(BF16), do3, _BTN)
            return carry

        lax.fori_loop(0, nc // CHUNK_BATCH, prep_step, 0)
        c_decs = [c[6] for c in consts]
        _ret_scan(states, c_decs, nc, False)
        _ret_scan(d_states, c_decs, nc, True)

        def main_step(b, dlams):
            rows, batch = _batch_rows(b), _batch_chunks(b)
            q3 = _chunks3(q_ref[rows, :] * scale)
            k3 = _chunks3(k_ref[rows, :])
            q3b, k3b = q3.astype(BF16), k3.astype(BF16)
            v3b = _chunks3(v_ref[rows, :]).astype(BF16)
            do3b = _chunks3(do[rows, :]).astype(BF16)
            a0 = _bdot(q3b, k3b, _BNT)
            pv = _bdot(do3b, v3b, _BNT)
            dq = dk = dv = None
            new_dlams = []
            for way in range(2):
                mask, relf, q_dec, q_exp, k_dec, k_exp, c_dec = consts[way]
                state, d_state = states[way, batch], d_states[way, batch]
                dp = pv * mask
                dpb = dp.astype(BF16)
                gq = _bdot(do3b, state.astype(BF16), _BNT)
                gk = _bdot(v3b, d_state.astype(BF16), _BNT)
                parts = (_bdot(dpb, k3b, _BNN) + q_dec * gq, _bdot(dpb, q3b, _BTN) + k_dec * gk,
                         _bdot((a0 * mask).astype(BF16), do3b, _BTN)
                         + _bdot((k3 * k_dec).astype(BF16), d_state.astype(BF16), _BNN))
                dq, dk, dv = parts if dq is None else (dq + parts[0], dk + parts[1], dv + parts[2])
                total = lambda x: jnp.sum(jnp.sum(x, axis=0), axis=0, keepdims=True)
                new_dlams.append(dlams[way] + total(relf * a0 * dp)
                                 + total(q_exp * q_dec * q3 * gq + k_exp * k_dec * k3 * gk)
                                 + (C * c_dec) * total(state * d_state))
            flat = lambda x: x.reshape(CHUNK_BATCH * C, HEAD_DIM)
            dq_ref[rows, :] = flat(dq) * scale
            dk_ref[rows, :] = flat(dk)
            dv_ref[rows, :] = flat(dv)
            return tuple(new_dlams)

        dlams = lax.fori_loop(0, nc // CHUNK_BATCH, main_step, (jnp.zeros((1, HEAD_DIM), F32),) * 2)
        for row, (dlam, lg) in enumerate(zip(dlams, lgs)):
            small_ref[0, pl.ds(row, 1), :] = jnp.broadcast_to(jnp.sum(dlam, axis=-1, keepdims=True) * lg, (1, HEAD_DIM))
        small_ref[0, pl.ds(2, 6), :] = jnp.zeros((6, HEAD_DIM), F32)

    hspec = pl.BlockSpec((S, HEAD_DIM), lambda h: (0, h))
    smem = pl.BlockSpec(memory_space=pltpu.SMEM)
    return pl.pallas_call(
        body, grid=(H,), name="ret_bwd",
        in_specs=[smem, smem] + _head_specs(S, (3, 4, 5), H) + [hspec],
        out_specs=[hspec, hspec, hspec, pl.BlockSpec((1, 8, HEAD_DIM), lambda h: (h, 0, 0))],
        out_shape=[SDS((S, H * HEAD_DIM), F32)] * 3 + [SDS((H, 8, HEAD_DIM), F32)],
        scratch_shapes=[pltpu.VMEM((2, nc, HEAD_DIM, HEAD_DIM), F32), pltpu.VMEM((2, nc, HEAD_DIM, HEAD_DIM), F32)],
        compiler_params=_cp(1))(dec_f, dec_b, proj, proj, proj, d_out)


def _ffn_bwd_act(dh2, wd, g, u):
    S, D = dh2.shape
    nblk, _, FB = g.shape
    tm = min(512, S)

    def body(dh_ref, wd_ref, g_ref, u_ref, dg_ref, du_ref):
        dact = _dot(dh_ref[...], wd_ref[...], _NT)
        gg = g_ref[0].astype(F32)
        sg = _sigmoid(gg)
        dg_ref[0] = (dact * u_ref[0].astype(F32) * (sg * (1.0 + gg * (1.0 - sg)))).astype(BF16)
        du_ref[0] = (dact * (gg * sg)).astype(BF16)

    blk = pl.BlockSpec((1, tm, FB), lambda j, i: (j, i, 0))
    return pl.pallas_call(
        body, grid=(nblk, S // tm), name="ffn_bwd_act",
        in_specs=[pl.BlockSpec((tm, D), lambda j, i: (i, 0)), pl.BlockSpec((FB, D), lambda j, i: (j, 0)), blk, blk],
        out_specs=[blk, blk], out_shape=[SDS((nblk, S, FB), BF16)] * 2,
        compiler_params=_cp(2))(dh2, wd, g, u)


def _ffn_bwd_in(dg, du, wg, wu, h1, dh2, w_norm):
    nblk, S, FB = dg.shape
    D = h1.shape[1]
    tm = min(512, S)

    def body(dg_ref, du_ref, wg_ref, wu_ref, h_hbm, dh2_hbm, wn_ref, dh_ref, dhb_ref, dw_ref, acc, h_buf, dh2_buf, sems):
        i, j = pl.program_id(0), pl.program_id(1)
        rows = pl.ds(pl.multiple_of(i * tm, tm), tm)
        fetch = [pltpu.make_async_copy(h_hbm.at[rows, :], h_buf, sems.at[0]),
                 pltpu.make_async_copy(dh2_hbm.at[rows, :], dh2_buf, sems.at[1])]

        @pl.when(j == 0)
        def _():
            for cp in fetch:
                cp.start()
            acc[...] = jnp.zeros_like(acc)

        @pl.when((i == 0) & (j == 0))
        def _():
            dw_ref[...] = jnp.zeros_like(dw_ref)

        acc[...] += _dot(dg_ref[0], wg_ref[0], _NN) + _dot(du_ref[0], wu_ref[0], _NN)

        @pl.when(j == nblk - 1)
        def _():
            for cp in fetch:
                cp.wait()
            dh, dw = _rms_bwd(acc[...], h_buf[...], wn_ref[...])
            dh = dh2_buf[...] + dh
            dh_ref[...] = dh
            dhb_ref[...] = dh.astype(BF16)
            dw_ref[...] += dw

    blk = pl.BlockSpec((1, tm, FB), lambda i, j: (j, i, 0))
    wspec = pl.BlockSpec((1, FB, D), lambda i, j: (j, 0, 0))
    row = pl.BlockSpec((tm, D), lambda i, j: (i, 0))
    vec = pl.BlockSpec((1, D), lambda i, j: (0, 0))
    return pl.pallas_call(
        body, grid=(S // tm, nblk), name="ffn_bwd_in",
        in_specs=[blk, blk, wspec, wspec, ANY, ANY, vec],
        out_specs=[row, row, vec], out_shape=[SDS((S, D), F32), SDS((S, D), BF16), SDS((1, D), F32)],
        scratch_shapes=[pltpu.VMEM((tm, D), F32), pltpu.VMEM((tm, D), F32), pltpu.VMEM((tm, D), F32),
                        pltpu.SemaphoreType.DMA((2,))],
        compiler_params=_cp(2))(dg, du, wg, wu, h1, dh2, w_norm)


def _dmix(dh1, w_out):
    S, D = dh1.shape
    tm = min(512, S)

    def body(dh_ref, w_ref, o_ref):
        o_ref[...] = _dot(dh_ref[...], w_ref[...], _NT)

    row = pl.BlockSpec((tm, D), lambda i: (i, 0))
    return pl.pallas_call(
        body, grid=(S // tm,), name="dmix", in_specs=[row, pl.BlockSpec((D, D), lambda i: (0, 0))],
        out_specs=row, out_shape=SDS((S, D), F32), compiler_params=_cp(1))(dh1, w_out)


def _in_bwd(dproj, w_blk, x, dh1, w_norm):
    S, D = x.shape
    nblk, _, NB = w_blk.shape
    tm = min(512, S)

    def body(dp_ref, w_ref, x_ref, dh1_ref, wn_ref, dx_ref, dw_ref, acc):
        i, j = pl.program_id(0), pl.program_id(1)

        @pl.when(j == 0)
        def _():
            acc[...] = jnp.zeros_like(acc)

        @pl.when((i == 0) & (j == 0))
        def _():
            dw_ref[...] = jnp.zeros_like(dw_ref)

        acc[...] += _dot(dp_ref[...], w_ref[0], _NT)

        @pl.when(j == nblk - 1)
        def _():
            dh, dw = _rms_bwd(acc[...], x_ref[...], wn_ref[...])
            dx_ref[...] = dh1_ref[...] + dh
            dw_ref[...] += dw

    row = pl.BlockSpec((tm, D), lambda i, j: (i, 0))
    vec = pl.BlockSpec((1, D), lambda i, j: (0, 0))
    return pl.pallas_call(
        body, grid=(S // tm, nblk), name="in_bwd",
        in_specs=[pl.BlockSpec((tm, NB), lambda i, j: (i, j)), pl.BlockSpec((1, D, NB), lambda i, j: (j, 0, 0)),
                  row, row, vec],
        out_specs=[row, vec], out_shape=[SDS((S, D), F32), SDS((1, D), F32)],
        scratch_shapes=[pltpu.VMEM((tm, D), F32)], compiler_params=_cp(2))(dproj, w_blk, x, dh1, w_norm)


def _wgrad(a, b, a_spec, b_spec, o_spec, o_shape, grid, name):
    nk = grid[-1]

    def ld(ref):
        return ref[0] if len(ref.shape) == 3 else ref[...]

    def body(a_ref, b_ref, o_ref, acc):
        k = pl.program_id(len(grid) - 1)

        @pl.when(k == 0)
        def _():
            acc[...] = jnp.zeros_like(acc)

        acc[...] += _dot(ld(a_ref), ld(b_ref), _TN)

        @pl.when(k == nk - 1)
        def _():
            if len(o_ref.shape) == 3:
                o_ref[0] = acc[...].astype(o_ref.dtype)
            else:
                o_ref[...] = acc[...].astype(o_ref.dtype)

    return pl.pallas_call(
        body, grid=grid, name=name, in_specs=[a_spec, b_spec], out_specs=o_spec, out_shape=SDS(o_shape, BF16),
        scratch_shapes=[pltpu.VMEM(o_spec.block_shape[-2:], F32)], compiler_params=_cp(len(grid)))(a, b)


def _peer(k):
    x, y, c = lax.axis_index("x"), lax.axis_index("y"), lax.axis_index("c")
    px = 1 - x if k & 4 else x
    py = 1 - y if k & 2 else y
    pc = 1 - c if k & 1 else c
    return (px, py, pc), 4 * px + 2 * py + pc


def _exchange_copies(srcs, lands, send_sems, recv_sems, which, gather):
    _, me = _peer(0)
    pairs = []
    for pos, a in enumerate(which):
        for k in range(1, N_DEV):
            dev, idx = _peer(k)
            sem = pos * (N_DEV - 1) + k - 1
            src = srcs[a] if gather else srcs[a].at[idx]
            mk = functools.partial(pltpu.make_async_remote_copy, src_ref=src, send_sem=send_sems.at[sem],
                                   recv_sem=recv_sems.at[sem], device_id=dev, device_id_type=MESH)
            pairs.append((mk(dst_ref=lands[a].at[me]), mk(dst_ref=lands[a].at[idx])))
    return pairs


def _sequencer_kernel(name, collective_id, n_remote, n_local):
    return pl.kernel(mesh=plsc.ScalarSubcoreMesh(axis_name="sequencer", num_cores=1), name=name,
                     scratch_types=(pltpu.SemaphoreType.DMA((n_remote,)), pltpu.SemaphoreType.DMA((n_remote,)),
                                    pltpu.SemaphoreType.DMA((n_local,))),
                     compiler_params=pltpu.CompilerParams(collective_id=collective_id))


def _handshake(ks):
    barrier = pltpu.get_barrier_semaphore()
    for k in ks:
        pl.semaphore_signal(barrier, inc=1, device_id=_peer(k)[0], device_id_type=MESH)
    pl.semaphore_wait(barrier, len(ks))


def _sequencer_scatter(arrays, name, collective_id):
    n = len(arrays)
    hbm = pltpu.MemorySpace.HBM
    srcs = [jax.new_ref(a, memory_space=hbm) for a in arrays]
    lands = [jax.empty_ref(SDS(a.shape, a.dtype), memory_space=hbm) for a in arrays]

    @_sequencer_kernel(name, collective_id, n * (N_DEV - 1), n)
    def launch(send_sems, recv_sems, local_sems):
        _handshake(range(1, N_DEV))
        _, me = _peer(0)
        local = [pltpu.make_async_copy(srcs[a].at[me], lands[a].at[me], local_sems.at[a]) for a in range(n)]
        pairs = _exchange_copies(srcs, lands, send_sems, recv_sems, range(n), False)
        for out, _ in pairs:
            out.start()
        for cp in local:
            cp.start()
        for out, arrival in pairs:
            out.wait_send()
            arrival.wait_recv()
        for cp in local:
            cp.wait()

    launch()
    return [r[...] for r in lands]


SIBLING = 1
OTHER_CHIPS = (2, 4, 6)


def _sequencer_gather(arrays, name, collective_id):
    n = len(arrays)
    hbm = pltpu.MemorySpace.HBM
    srcs = [jax.new_ref(a, memory_space=hbm) for a in arrays]
    lands = [jax.empty_ref(SDS((N_DEV,) + a.shape, a.dtype), memory_space=hbm) for a in arrays]

    @_sequencer_kernel(name, collective_id, n * (N_DEV - 1), n)
    def launch(send_sems, recv_sems, local_sems):
        _handshake((SIBLING,) + OTHER_CHIPS)
        _, me = _peer(0)
        sibling, _ = _peer(SIBLING)

        def copy(a, k, src, block, to):
            sem = a * (N_DEV - 1) + k - 1
            return pltpu.make_async_remote_copy(src_ref=src, dst_ref=lands[a].at[block], send_sem=send_sems.at[sem],
                                                recv_sem=recv_sems.at[sem], device_id=to, device_id_type=MESH)

        local = [pltpu.make_async_copy(srcs[a], lands[a].at[me], local_sems.at[a]) for a in range(n)]
        first = [copy(a, k, srcs[a], me, _peer(k)[0]) for a in range(n) for k in OTHER_CHIPS + (SIBLING,)]
        for cp in first + local:
            cp.start()
        passed = []
        for a in range(n):
            for k in OTHER_CHIPS:
                _, block = _peer(k)
                copy(a, k, srcs[a], block, sibling).wait_recv()
                passed.append(copy(a, k ^ SIBLING, lands[a].at[block], block, sibling))
                passed[-1].start()
        for a in range(n):
            for k in (SIBLING,) + tuple(k ^ SIBLING for k in OTHER_CHIPS):
                copy(a, k, srcs[a], _peer(k)[1], sibling).wait_recv()
        for cp in first + passed:
            cp.wait_send()
        for cp in local:
            cp.wait()

    launch()
    return [r[...] for r in lands]


SMALL_ROWS = 64


def _small_step(part, w, m, v):
    def body(p_ref, w_ref, m_ref, v_ref, g_ref, d_ref, nm_ref, nv_ref, gath, send_sems, recv_sems):
        _, me = _peer(0)
        gath[me] = p_ref[...]
        copies = []
        for k in range(1, N_DEV):
            dev, idx = _peer(k)
            out = pltpu.make_async_remote_copy(src_ref=p_ref, dst_ref=gath.at[me], send_sem=send_sems.at[k - 1],
                                               recv_sem=recv_sems.at[k - 1], device_id=dev, device_id_type=MESH)
            out.start()
            arrival = pltpu.make_async_remote_copy(src_ref=p_ref, dst_ref=gath.at[idx], send_sem=send_sems.at[k - 1],
                                                   recv_sem=recv_sems.at[k - 1], device_id=dev, device_id_type=MESH)
            copies.append((out, arrival))
        for out, arrival in copies:
            out.wait_send()
            arrival.wait_recv()
        g = gath[0]
        for p in range(1, N_DEV):
            g = g + gath[p]
        g_ref[...] = g
        d_ref[...], nm_ref[...], nv_ref[...] = _adamw(w_ref[...], g, m_ref[...], v_ref[...])

    vm = pl.BlockSpec(memory_space=pltpu.VMEM)
    return pl.pallas_call(
        body, name="small_step", in_specs=[vm] * 4, out_specs=[vm] * 4,
        out_shape=[SDS((SMALL_ROWS, 128), F32)] * 4,
        scratch_shapes=[pltpu.VMEM((N_DEV, SMALL_ROWS, 128), F32), pltpu.SemaphoreType.DMA((N_DEV - 1,)),
                        pltpu.SemaphoreType.DMA((N_DEV - 1,))])(part, w, m, v)


def _adamw(w, g, m, v):
    m = ADAM_B1 * m + (1.0 - ADAM_B1) * g
    v = ADAM_B2 * v + (1.0 - ADAM_B2) * (g * g)
    m_hat = m / (1.0 - ADAM_B1 ** ADAM_STEP)
    v_hat = v / (1.0 - ADAM_B2 ** ADAM_STEP)
    delta = -ADAM_LR * (m_hat / (jnp.sqrt(v_hat) + ADAM_EPS) + ADAM_WD * w)
    return delta, m, v


def _adamw_block(parts, w, m, v, name):
    R, C = w.shape
    n_parts = len(parts)
    Rp = R // n_parts
    tr = next(t for t in (256, 128, 64, 32, 16, 8) if Rp % t == 0 and t * C <= 256 * 1024)
    per_part = Rp // tr

    def body(*refs):
        p_refs = refs[:n_parts]
        w_ref, m_ref, v_ref, g_ref, d_ref, nm_ref, nv_ref = refs[n_parts:]
        for k, p_ref in enumerate(p_refs):
            @pl.when(pl.program_id(0) // per_part == k)
            def _(p_ref=p_ref):
                g = p_ref[0].astype(F32)
                for p in range(1, N_DEV):
                    g = g + p_ref[p].astype(F32)
                g_ref[...] = g
                d_ref[...], nm_ref[...], nv_ref[...] = _adamw(w_ref[...], g, m_ref[...], v_ref[...])

    row = pl.BlockSpec((tr, C), lambda i: (i, 0))
    part_specs = [pl.BlockSpec((N_DEV, tr, C), functools.partial(
        lambda i, k: (0, jnp.clip(i - k * per_part, 0, per_part - 1), 0), k=k)) for k in range(n_parts)]
    return pl.pallas_call(
        body, grid=(R // tr,), name=name, in_specs=part_specs + [row, row, row],
        out_specs=[row] * 4, out_shape=[SDS((R, C), F32)] * 4, compiler_params=_cp(1))(*parts, w, m, v)


def _pack_small(mix, ffn, fin, retw, dec_f, dec_b, loss):
    flat = jnp.concatenate([mix.reshape(-1), ffn.reshape(-1), fin.reshape(-1), retw.reshape(-1), dec_f.reshape(-1),
                            dec_b.reshape(-1), loss.reshape(-1)])
    return jnp.pad(flat, (0, SMALL_ROWS * 128 - flat.shape[0])).reshape(SMALL_ROWS, 128)


def _unpack_small(packed, shapes):
    flat = packed.reshape(-1)
    out, at = [], 0
    for s in shapes:
        n = math.prod(s)
        out.append(flat[at:at + n].reshape(s))
        at += n
    return out


def kernel(x, norm_mix_w, w_in, ret_decay_fwd, ret_decay_bwd, ret_norm_w, w_out, norm_ffn_w, w_gate, w_up, w_down, norm_final_w, loss_target, m_norm_mix_w, m_w_in, m_ret_decay_fwd, m_ret_decay_bwd, m_ret_norm_w, m_w_out, m_norm_ffn_w, m_w_gate, m_w_up, m_w_down, m_norm_final_w, v_norm_mix_w, v_w_in, v_ret_decay_fwd, v_ret_decay_bwd, v_ret_norm_w, v_w_out, v_norm_ffn_w, v_w_gate, v_w_up, v_w_down, v_norm_final_w):
    x2 = x[0]
    tgt = loss_target[0]
    S, D = x2.shape
    H = ret_norm_w.shape[1] // HEAD_DIM
    DA = H * HEAD_DIM
    fin_w = norm_final_w.reshape(1, D)
    big = (w_in[0], w_out[0], w_gate[0].T, w_up[0].T, w_down[0])

    big_b = [w.astype(BF16) for w in big]
    wi, = _sequencer_gather(big_b[:1], "gather_in", 0)
    wo, wg, wu = _sequencer_gather(big_b[1:4], "gather_mid", 1)
    wd, = _sequencer_gather(big_b[4:], "gather_down", 5)
    NB = wi.shape[2]

    proj, n1 = _proj_fwd(x2, norm_mix_w, wi)
    bias = _attn_bias()[:H]
    attn, lse = _attn_fwd(proj, bias)
    ret, o_raw = _ret_fwd(proj, ret_decay_fwd, ret_decay_bwd, ret_norm_w)
    wo_full = wo.reshape(D, D)
    d_ff = N_DEV * wd.shape[1]
    FB = FFN_BLOCK if d_ff % FFN_BLOCK == 0 else wd.shape[1]
    n_fb = d_ff // FB
    wg, wu = wg.reshape(n_fb, FB, D), wu.reshape(n_fb, FB, D)
    wd_full = wd.reshape(d_ff, D)
    h1, mixed, n2 = _out_fwd(x2, attn, ret, wo_full, norm_ffn_w)
    gate, up, act = _ffn_up(n2, wg, wu)
    dh2, dh2_b, loss_parts, g_fin = _ffn_down_loss(act, wd_full, h1, tgt, fin_w)

    dgate, dup = _ffn_bwd_act(dh2_b, wd_full, gate, up)
    tn = min(1024, D)
    ffn_specs = (pl.BlockSpec((1, S, FB), lambda j, n, k: (j, 0, 0)), pl.BlockSpec((S, tn), lambda j, n, k: (0, n)),
                 pl.BlockSpec((1, FB, tn), lambda j, n, k: (j, 0, n)), (n_fb, FB, D), (n_fb, D // tn, 1))
    per_dev = (N_DEV, d_ff // N_DEV, D)
    g_wd = _wgrad(act, dh2_b, *ffn_specs, "wgrad_down").reshape(per_dev)
    g_wg = _wgrad(dgate, n2, *ffn_specs, "wgrad_gate").reshape(per_dev)
    g_wu = _wgrad(dup, n2, *ffn_specs, "wgrad_up").reshape(per_dev)
    parts_f = _sequencer_scatter([g_wg, g_wu, g_wd], "scatter_ffn", 2)
    dh1, dh1_b, g_ffn = _ffn_bwd_in(dgate, dup, wg, wu, h1, dh2, norm_ffn_w)
    dmix = _dmix(dh1_b, wo_full)
    tmw = min(512, D)
    tk = min(2048, S)
    g_wo = _wgrad(mixed, dh1_b, pl.BlockSpec((tk, tmw), lambda m, k: (k, m)), pl.BlockSpec((tk, D), lambda m, k: (k, 0)),
                  pl.BlockSpec((tmw, D), lambda m, k: (m, 0)), (D, D), (D // tmw, S // tk), "wgrad_out")
    parts_o = _sequencer_scatter([g_wo.reshape(N_DEV, D // N_DEV, D)], "scatter_out", 3)
    d_ret, dg_r, small_w = _ret_gate_bwd(proj, o_raw, dmix, ret_norm_w, DA)
    dq_r, dk_r, dv_r, small = _ret_bwd(proj, d_ret, ret_decay_fwd, ret_decay_bwd)
    dq_a, dk_a, dv_a = _attn_bwd(proj, attn, lse, dmix, bias)
    dproj = jnp.concatenate([t.astype(BF16) for t in (dq_a, dk_a, dv_a, dq_r, dk_r, dv_r, dg_r)], axis=1)
    half = D // tmw // 2
    parts_i = []
    for part, (name, cid) in enumerate((("in_lo", 4), ("in_hi", 6))):
        g_wi = _wgrad(n1, dproj, pl.BlockSpec((S, tmw), functools.partial(lambda j, m, k, off: (0, m + off), off=part * half)),
                      pl.BlockSpec((S, NB), lambda j, m, k: (0, j)), pl.BlockSpec((1, tmw, NB), lambda j, m, k: (j, m, 0)),
                      (N_DEV, D // 2, NB), (N_DEV, half, 1), "wgrad_" + name)
        parts_i += _sequencer_scatter([g_wi], "scatter_" + name, cid)
    grad_x, g_mix = _in_bwd(dproj, wi, x2, dh1, norm_mix_w)

    big_m = (m_w_in[0], m_w_out[0], m_w_gate[0].T, m_w_up[0].T, m_w_down[0])
    big_v = (v_w_in[0], v_w_out[0], v_w_gate[0].T, v_w_up[0].T, v_w_down[0])
    names = ("adamw_in", "adamw_out", "adamw_gate", "adamw_up", "adamw_down")
    upd = [None] * 5
    for a, p in zip((2, 3, 4, 1, 0), [[t] for t in parts_f + parts_o] + [parts_i]):
        upd[a] = _adamw_block(p, big[a], big_m[a], big_v[a], names[a])

    g_dec_f = small[:, 0, 0].reshape(1, H)
    g_dec_b = small[:, 1, 0].reshape(1, H)
    g_retw = small_w[:, 0, :].reshape(1, DA)
    loss_local = jnp.sum(loss_parts[::8, 0])
    zero = jnp.zeros((1,), F32)
    part = _pack_small(g_mix, g_ffn, g_fin, g_retw, g_dec_f, g_dec_b, loss_local)
    sw = _pack_small(norm_mix_w, norm_ffn_w, norm_final_w, ret_norm_w, ret_decay_fwd, ret_decay_bwd, zero)
    sm = _pack_small(m_norm_mix_w, m_norm_ffn_w, m_norm_final_w, m_ret_norm_w, m_ret_decay_fwd, m_ret_decay_bwd, zero)
    sv = _pack_small(v_norm_mix_w, v_norm_ffn_w, v_norm_final_w, v_ret_norm_w, v_ret_decay_fwd, v_ret_decay_bwd, zero)
    shapes = [(1, D), (1, D), (D,), (1, DA), (1, H), (1, H), ()]
    sg, sd, snm, snv = [_unpack_small(t, shapes) for t in _small_step(part, sw, sm, sv)]
    loss = sg[6]

    def ordered(small_set, k):
        b = [(u[k].T if a in (2, 3) else u[k])[None] for a, u in enumerate(upd)]
        return [small_set[0], b[0], small_set[4], small_set[5], small_set[3], b[1], small_set[1], b[2], b[3], b[4],
                small_set[2]]

    return (loss, grad_x[None], *ordered(sg, 0), *ordered(sd, 1), *ordered(snm, 2), *ordered(snv, 3))
```

```python
import functools
import math

import numpy as np
import jax
import jax.numpy as jnp
from jax import lax
from jax.experimental import pallas as pl
from jax.experimental.pallas import tpu as pltpu
from jax.experimental.pallas import tpu_sc as plsc

F32 = jnp.float32
BF16 = jnp.bfloat16
SDS = jax.ShapeDtypeStruct

HEAD_DIM = 128
EPS = 1e-6
RET_CHUNK = 128
DILATIONS = (1, 4, 16)
BAND = 64
Q_TILE = 128
K_TILE = Q_TILE + 2 * BAND
KV_PAD = BAND * 4
TILE_GROUP = 8
NEG = -1e30
N_DEV = 8
N_GROUPS = 7
ADAM_LR, ADAM_B1, ADAM_B2, ADAM_EPS, ADAM_WD, ADAM_STEP = 0.001, 0.9, 0.999, 1e-08, 0.01, 10
VMEM_LIMIT = 56 * 1024 * 1024
MESH = pl.DeviceIdType.MESH
ANY = pl.BlockSpec(memory_space=pl.ANY)


def _cp(n_grid):
    return pltpu.CompilerParams(dimension_semantics=("arbitrary",) * n_grid, vmem_limit_bytes=VMEM_LIMIT)


def _sigmoid(x):
    return 1.0 / (1.0 + jnp.exp(-x))


def _rms_scale(h):
    return lax.rsqrt(jnp.mean(h * h, axis=-1, keepdims=True) + EPS)


def _rms_bwd(dn, h, w):
    r = _rms_scale(h)
    gw = dn * w
    dh = r * gw - h * (r * r * r) * jnp.mean(gw * h, axis=-1, keepdims=True)
    return dh, jnp.sum(dn * h * r, axis=0, keepdims=True)


def _dot(a, b, dims):
    return lax.dot_general(a.astype(BF16), b.astype(BF16), (dims, ((), ())), preferred_element_type=F32)


_NN = ((1,), (0,))
_NT = ((1,), (1,))
_TN = ((0,), (0,))


RESIDENT_ROWS = 256


def _resident(shape):
    return pl.BlockSpec(shape, lambda i: (0, 0), pipeline_mode=pl.Buffered(1))


def _blocked_matmul(a_ref, w_ref):
    nblk, _, fb = a_ref.shape
    out = None
    for j in range(nblk):
        part = jnp.dot(a_ref[j], w_ref[pl.ds(j * fb, fb), :], preferred_element_type=F32)
        out = part if out is None else out + part
    return out


def _proj_fwd(x, w_norm, w_blk):
    S, D = x.shape
    nblk, _, NB = w_blk.shape
    tm = min(1024, S)

    def body(x_ref, wn_ref, w_ref, proj_ref, n_ref, n_scr):
        @pl.when(pl.program_id(1) == 0)
        def _():
            xf = x_ref[...]
            nb = (xf * _rms_scale(xf) * wn_ref[...]).astype(BF16)
            n_scr[...] = nb
            n_ref[...] = nb
        proj_ref[...] = jnp.dot(n_scr[...], w_ref[0], preferred_element_type=F32)

    return pl.pallas_call(
        body, grid=(S // tm, nblk), name="proj_fwd",
        in_specs=[pl.BlockSpec((tm, D), lambda i, j: (i, 0)), pl.BlockSpec((1, D), lambda i, j: (0, 0)),
                  pl.BlockSpec((1, D, NB), lambda i, j: (j, 0, 0))],
        out_specs=[pl.BlockSpec((tm, NB), lambda i, j: (i, j)), pl.BlockSpec((tm, D), lambda i, j: (i, 0))],
        out_shape=[SDS((S, nblk * NB), F32), SDS((S, D), BF16)],
        scratch_shapes=[pltpu.VMEM((tm, D), BF16)], compiler_params=_cp(2))(x, w_norm, w_blk)


def _out_fwd(x, attn, ret, w_out, w_norm):
    S, D = x.shape
    DA = attn.shape[1]
    tm = min(256, S)

    def body(x_ref, a_ref, r_ref, w_ref, wn_ref, h_ref, mix_ref, n_ref):
        a = a_ref[...].astype(BF16)
        r = r_ref[...].astype(BF16)
        mix_ref[:, :DA] = a
        mix_ref[:, DA:] = r
        h = x_ref[...] + jnp.dot(a, w_ref[:DA, :], preferred_element_type=F32) \
            + jnp.dot(r, w_ref[DA:, :], preferred_element_type=F32)
        h_ref[...] = h
        n_ref[...] = (h * _rms_scale(h) * wn_ref[...]).astype(BF16)

    row = lambda w: pl.BlockSpec((tm, w), lambda i: (i, 0))
    return pl.pallas_call(
        body, grid=(S // tm,), name="out_fwd",
        in_specs=[row(D), row(DA), row(D - DA), pl.BlockSpec((D, D), lambda i: (0, 0)),
                  pl.BlockSpec((1, D), lambda i: (0, 0))],
        out_specs=[row(D), row(D), row(D)],
        out_shape=[SDS((S, D), F32), SDS((S, D), BF16), SDS((S, D), BF16)],
        compiler_params=_cp(1))(x, attn, ret, w_out, w_norm)


def _ffn_up(n2, wg, wu):
    S, D = n2.shape
    nblk, FB, _ = wg.shape
    tm = min(512, S)

    def body(n_ref, wg_ref, wu_ref, g_ref, u_ref, a_ref):
        n = n_ref[...]
        g = _dot(n, wg_ref[0], _NT)
        u = _dot(n, wu_ref[0], _NT)
        g_ref[0] = g.astype(BF16)
        u_ref[0] = u.astype(BF16)
        a_ref[0] = (g * _sigmoid(g) * u).astype(BF16)

    wspec = pl.BlockSpec((1, FB, D), lambda j, i: (j, 0, 0))
    ospec = pl.BlockSpec((1, tm, FB), lambda j, i: (j, i, 0))
    return pl.pallas_call(
        body, grid=(nblk, S // tm), name="ffn_up",
        in_specs=[pl.BlockSpec((tm, D), lambda j, i: (i, 0)), wspec, wspec],
        out_specs=[ospec, ospec, ospec],
        out_shape=[SDS((nblk, S, FB), BF16)] * 3,
        compiler_params=_cp(2))(n2, wg, wu)


def _ffn_down_loss(act, wd, h1, target, w_norm):
    nblk, S, FB = act.shape
    D = h1.shape[1]
    tm = min(RESIDENT_ROWS, S)

    def body(a_ref, wd_ref, h_ref, t_ref, wn_ref, dh_ref, dhb_ref, loss_ref, dw_ref):
        @pl.when(pl.program_id(0) == 0)
        def _():
            dw_ref[...] = jnp.zeros_like(dw_ref)

        h = h_ref[...] + _blocked_matmul(a_ref, wd_ref)
        w = wn_ref[...]
        err = h * _rms_scale(h) * w - t_ref[...]
        loss_ref[...] = jnp.full(loss_ref.shape, 0.5 * jnp.sum(err * err) / D, F32)
        dh, dw = _rms_bwd(err * (1.0 / D), h, w)
        dh_ref[...] = dh
        dhb_ref[...] = dh.astype(BF16)
        dw_ref[...] += dw

    row = pl.BlockSpec((tm, D), lambda i: (i, 0))
    vec = pl.BlockSpec((1, D), lambda i: (0, 0))
    return pl.pallas_call(
        body, grid=(S // tm,), name="ffn_down_loss",
        in_specs=[pl.BlockSpec((nblk, tm, FB), lambda i: (0, i, 0)), _resident((nblk * FB, D)), row, row, vec],
        out_specs=[row, row, pl.BlockSpec((8, 128), lambda i: (i, 0)), vec],
        out_shape=[SDS((S, D), F32), SDS((S, D), BF16), SDS((S // tm * 8, 128), F32), SDS((1, D), F32)],
        compiler_params=_cp(1))(act, wd, h1, target, w_norm)


def _attn_bias():
    n_heads = 8
    slopes = np.exp2(-8.0 * np.arange(1, n_heads + 1, dtype=np.float32) / n_heads)
    dist = np.abs(np.arange(K_TILE)[None, :] - BAND - np.arange(Q_TILE)[:, None])
    out = np.empty((n_heads, len(DILATIONS), Q_TILE, K_TILE), np.float32)
    for h in range(n_heads):
        for p, d in enumerate(DILATIONS):
            out[h, p] = np.where(dist <= BAND, -slopes[h] * (d * dist).astype(np.float32), NEG)
    return jnp.asarray(out)


def _attn_tiles(S, d):
    L = S // d
    per_class = L // Q_TILE
    return L, per_class, d * per_class


def _tile_rows(t, d, per_class):
    r = t // per_class
    a = (t % per_class) * Q_TILE
    q_rows = pl.ds(r + d * a, Q_TILE, stride=d) if d > 1 else pl.ds(pl.multiple_of(a, Q_TILE), Q_TILE)
    k_rows = pl.ds(KV_PAD + r + d * (a - BAND), K_TILE, stride=d) if d > 1 else pl.ds(
        pl.multiple_of(KV_PAD + a - BAND, BAND), K_TILE)
    return a, q_rows, k_rows


def _to_quarters(dst, src, n, dst_off=0):
    for r in range(4):
        dst[pl.ds(dst_off + r * (n // 4), n // 4), :] = src[pl.ds(r, n // 4, stride=4), :]


def _quarter_tile_rows(t, S):
    L = S // 16
    per_class = L // Q_TILE
    blk, tt = t // (4 * per_class), t % (4 * per_class)
    r, a = tt // per_class, (tt % per_class) * Q_TILE
    q_rows = pl.ds(blk * (S // 4) + r + 4 * a, Q_TILE, stride=4)
    k_rows = pl.ds(KV_PAD + blk * (S // 4) + r + 4 * (a - BAND), K_TILE, stride=4)
    return a, q_rows, k_rows


def _lanes(x, width):
    return jnp.concatenate([x] * (width // HEAD_DIM), axis=-1)


_BNT = (((2,), (2,)), ((0,), (0,)))
_BNN = (((2,), (1,)), ((0,), (0,)))
_BTN = (((1,), (1,)), ((0,), (0,)))


def _bdot(a, b, dims):
    return lax.dot_general(a, b, dims, preferred_element_type=F32)


def _stacked(rows, loaders):
    return [jnp.stack([f(*r) for r in rows]) for f in loaders]


def _edge_mask(a, L):
    lk = lax.broadcasted_iota(jnp.int32, (1, K_TILE), 1) + (a - BAND)
    return jnp.where((lk >= 0) & (lk < L), 0.0, NEG).astype(F32)


def _fill_padded(dst, src, S):
    dst[pl.ds(0, KV_PAD), :] = jnp.zeros((KV_PAD, HEAD_DIM), F32)
    dst[pl.ds(KV_PAD + S, KV_PAD), :] = jnp.zeros((KV_PAD, HEAD_DIM), F32)
    dst[pl.ds(KV_PAD, S), :] = src[...]


def _head_specs(S, groups, n_heads):
    return [pl.BlockSpec((S, HEAD_DIM), functools.partial(lambda h, g: (0, g * n_heads + h), g=g)) for g in groups]


def _attn_fwd(proj, bias):
    S = proj.shape[0]
    H = proj.shape[1] // (N_GROUPS * HEAD_DIM)
    scale = HEAD_DIM ** -0.5

    def body(q_ref, k_ref, v_ref, b_ref, o_ref, lse_ref, kp, vp, m_run, l_run, q4, m3, l3, acc3):
        _fill_padded(kp, k_ref, S)
        _fill_padded(vp, v_ref, S)
        o_ref[...] = jnp.zeros_like(o_ref)
        m_run[...] = jnp.full(m_run.shape, NEG, F32)
        l_run[...] = jnp.zeros_like(l_run)
        for p, d in enumerate(DILATIONS[:2]):
            L, per_class, n_tiles = _attn_tiles(S, d)

            def tiles(t, carry, p=p, d=d, L=L, per_class=per_class, n_tiles=n_tiles):
                rows = [_tile_rows(t + u * (n_tiles // TILE_GROUP), d, per_class) for u in range(TILE_GROUP)]
                qs, ks, vs, m_old, l_old, o_old, edge = _stacked(rows, (
                    lambda a, qr, kr: q_ref[qr, :].astype(BF16), lambda a, qr, kr: kp[kr, :].astype(BF16),
                    lambda a, qr, kr: vp[kr, :].astype(BF16), lambda a, qr, kr: m_run[qr, :],
                    lambda a, qr, kr: l_run[qr, :], lambda a, qr, kr: o_ref[qr, :], lambda a, qr, kr: _edge_mask(a, L)))
                s = _bdot(qs, ks, _BNT) * scale + b_ref[0, p][None] + edge
                m_new = jnp.maximum(m_old, jnp.max(s, axis=-1, keepdims=True))
                pr = jnp.exp(s - _lanes(m_new, K_TILE)).astype(BF16)
                alpha = jnp.exp(m_old - m_new)
                l_new = alpha * l_old + _bdot(pr, jnp.ones((TILE_GROUP, K_TILE, HEAD_DIM), BF16), _BNN)
                o_new = alpha * o_old + _bdot(pr, vs, _BNN)
                for u, (_, qr, _) in enumerate(rows):
                    o_ref[qr, :] = o_new[u]
                    m_run[qr, :] = m_new[u]
                    l_run[qr, :] = l_new[u]
                return carry

            lax.fori_loop(0, n_tiles // TILE_GROUP, tiles, 0)

        _to_quarters(q4, q_ref, S)
        _to_quarters(kp, k_ref, S, KV_PAD)
        _to_quarters(vp, v_ref, S, KV_PAD)
        n_tiles = _attn_tiles(S, DILATIONS[2])[2]

        def tiles3(t, carry):
            rows = [_quarter_tile_rows(t + u * (n_tiles // TILE_GROUP), S) for u in range(TILE_GROUP)]
            qs, ks, vs, edge = _stacked(rows, (
                lambda a, qr, kr: q4[qr, :].astype(BF16), lambda a, qr, kr: kp[kr, :].astype(BF16),
                lambda a, qr, kr: vp[kr, :].astype(BF16), lambda a, qr, kr: _edge_mask(a, S // DILATIONS[2])))
            s = _bdot(qs, ks, _BNT) * scale + b_ref[0, 2][None] + edge
            m_new = jnp.broadcast_to(jnp.max(s, axis=-1, keepdims=True), (TILE_GROUP, Q_TILE, HEAD_DIM))
            pr = jnp.exp(s - _lanes(m_new, K_TILE)).astype(BF16)
            l_new = _bdot(pr, jnp.ones((TILE_GROUP, K_TILE, HEAD_DIM), BF16), _BNN)
            o_new = _bdot(pr, vs, _BNN)
            for u, (_, qr, _) in enumerate(rows):
                acc3[qr, :] = o_new[u]
                m3[qr, :] = m_new[u]
                l3[qr, :] = l_new[u]
            return carry

        lax.fori_loop(0, n_tiles // TILE_GROUP, tiles3, 0)
        for r in range(4):
            nat, qtr = pl.ds(r, S // 4, stride=4), pl.ds(r * (S // 4), S // 4)
            m_a, m_b = m_run[nat, :], m3[qtr, :]
            m = jnp.maximum(m_a, m_b)
            w_a, w_b = jnp.exp(m_a - m), jnp.exp(m_b - m)
            l = w_a * l_run[nat, :] + w_b * l3[qtr, :]
            o_ref[nat, :] = (w_a * o_ref[nat, :] + w_b * acc3[qtr, :]) / l
            lse_ref[nat, :] = m + jnp.log(l)

    hspec = pl.BlockSpec((S, HEAD_DIM), lambda h: (0, h))
    padded, plain = pltpu.VMEM((S + 2 * KV_PAD, HEAD_DIM), F32), pltpu.VMEM((S, HEAD_DIM), F32)
    return pl.pallas_call(
        body, grid=(H,), name="attn_fwd",
        in_specs=_head_specs(S, (0, 1, 2), H) + [
            pl.BlockSpec((1, len(DILATIONS), Q_TILE, K_TILE), lambda h: (h, 0, 0, 0))],
        out_specs=[hspec, hspec],
        out_shape=[SDS((S, H * HEAD_DIM), F32), SDS((S, H * HEAD_DIM), F32)],
        scratch_shapes=[padded, padded] + [plain] * 6,
        compiler_params=_cp(1))(proj, proj, proj, bias)


def _attn_bwd(proj, out, lse, dmix, bias):
    S = proj.shape[0]
    H = proj.shape[1] // (N_GROUPS * HEAD_DIM)
    scale = HEAD_DIM ** -0.5
    assert S // DILATIONS[2] >= 2 * Q_TILE

    def body(q_ref, k_ref, v_ref, o_ref, lse_ref, do_ref, b_ref, dq_ref, dk_ref, dv_ref,
             kp, vp, dkp, dvp, dsum, q4, do4, lse4, dsum4):
        _fill_padded(kp, k_ref, S)
        _fill_padded(vp, v_ref, S)
        dkp[...] = jnp.zeros_like(dkp)
        dvp[...] = jnp.zeros_like(dvp)
        dq_ref[...] = jnp.zeros_like(dq_ref)
        dsum[...] = jnp.broadcast_to(jnp.sum(do_ref[...] * o_ref[...], axis=-1, keepdims=True), dsum.shape)

        def run(n_tiles, tile_rows, p, L, q_src, do_src, lse_src, dsum_src, dq_dst, dq_adds):
            def tiles(t, carry):
                rows = [tile_rows(t + u * (n_tiles // TILE_GROUP)) for u in range(TILE_GROUP)]
                qs, ks, vs, dos, lses, dsums, dk_old, dv_old, edge = _stacked(rows, (
                    lambda a, qr, kr: q_src[qr, :].astype(BF16), lambda a, qr, kr: kp[kr, :].astype(BF16),
                    lambda a, qr, kr: vp[kr, :].astype(BF16), lambda a, qr, kr: do_src[qr, :].astype(BF16),
                    lambda a, qr, kr: lse_src[qr, :], lambda a, qr, kr: dsum_src[qr, :],
                    lambda a, qr, kr: dkp[kr, :], lambda a, qr, kr: dvp[kr, :], lambda a, qr, kr: _edge_mask(a, L)))
                s = _bdot(qs, ks, _BNT) * scale + b_ref[0, p][None] + edge
                pr = jnp.exp(s - _lanes(lses, K_TILE))
                ds = (pr * (_bdot(dos, vs, _BNT) - _lanes(dsums, K_TILE)) * scale).astype(BF16)
                dq_new = _bdot(ds, ks, _BNN)
                if dq_adds:
                    dq_new = dq_new + jnp.stack([dq_dst[qr, :] for _, qr, _ in rows])
                dk_new = dk_old + _bdot(ds, qs, _BTN)
                dv_new = dv_old + _bdot(pr.astype(BF16), dos, _BTN)
                for u, (_, qr, kr) in enumerate(rows):
                    dq_dst[qr, :] = dq_new[u]
                    dkp[kr, :] = dk_new[u]
                    dvp[kr, :] = dv_new[u]
                return carry

            lax.fori_loop(0, n_tiles // TILE_GROUP, tiles, 0)

        for p, d in enumerate(DILATIONS[:2]):
            L, per_class, n_tiles = _attn_tiles(S, d)
            run(n_tiles, functools.partial(_tile_rows, d=d, per_class=per_class), p, L,
                q_ref, do_ref, lse_ref, dsum, dq_ref, True)
        dk_ref[...] = dkp[pl.ds(KV_PAD, S), :]
        dv_ref[...] = dvp[pl.ds(KV_PAD, S), :]

        for dst, src in ((q4, q_ref), (do4, do_ref), (lse4, lse_ref), (dsum4, dsum)):
            _to_quarters(dst, src, S)
        _to_quarters(kp, k_ref, S, KV_PAD)
        _to_quarters(vp, v_ref, S, KV_PAD)
        dkp[...] = jnp.zeros_like(dkp)
        dvp[...] = jnp.zeros_like(dvp)
        dq3 = dsum
        run(_attn_tiles(S, DILATIONS[2])[2], functools.partial(_quarter_tile_rows, S=S), 2, S // DILATIONS[2],
            q4, do4, lse4, dsum4, dq3, False)
        for r in range(4):
            nat, qtr = pl.ds(r, S // 4, stride=4), pl.ds(r * (S // 4), S // 4)
            pad_qtr = pl.ds(KV_PAD + r * (S // 4), S // 4)
            dq_ref[nat, :] = dq_ref[nat, :] + dq3[qtr, :]
            dk_ref[nat, :] = dk_ref[nat, :] + dkp[pad_qtr, :]
            dv_ref[nat, :] = dv_ref[nat, :] + dvp[pad_qtr, :]

    hspec = pl.BlockSpec((S, HEAD_DIM), lambda h: (0, h))
    once =pl.BlockSpec((S, HEAD_DIM), lambda h: (0, h), pipeline_mode=pl.Buffered(1))
    padded, plain = pltpu.VMEM((S + 2 * KV_PAD, HEAD_DIM), F32), pltpu.VMEM((S, HEAD_DIM), F32)
    return pl.pallas_call(
        body, grid=(H,), name="attn_bwd",
        in_specs=_head_specs(S, (0, 1, 2), H) + [once, once, once,
                                                  pl.BlockSpec((1, len(DILATIONS), Q_TILE, K_TILE), lambda h: (h, 0, 0, 0))],
        out_specs=[hspec, hspec, hspec],
        out_shape=[SDS((S, H * HEAD_DIM), F32)] * 3,
        scratch_shapes=[padded] * 4 + [plain] * 5,
        compiler_params=_cp(1))(proj, proj, proj, out, lse, dmix, bias)


def _ret_consts(lg, forward):
    C = RET_CHUNK
    i = lax.broadcasted_iota(jnp.int32, (C, C), 0)
    j = lax.broadcasted_iota(jnp.int32, (C, C), 1)
    rel = (i - j) if forward else (j - i)
    inside = (rel >= 0) if forward else (rel > 0)
    relf = jnp.maximum(rel, 0).astype(F32)
    mask = jnp.where(inside, jnp.exp(lg * relf), 0.0)
    idx = lax.broadcasted_iota(jnp.int32, (C, 1), 0).astype(F32)
    q_exp = (idx + 1.0) if forward else (C - idx)
    k_exp = (C - 1.0 - idx) if forward else idx
    return mask, relf, jnp.exp(lg * q_exp), q_exp, jnp.exp(lg * k_exp), k_exp, jnp.exp(lg * C)


def _log_decay(dec_ref, h):
    return -jnp.exp(jnp.full((1, 1), dec_ref[0, h], F32))


FFN_BLOCK = 704
CHUNK_BATCH = 8


def _batch_rows(b):
    n = CHUNK_BATCH * RET_CHUNK
    return pl.ds(pl.multiple_of(b * n, n), n)


def _batch_chunks(b):
    return pl.ds(pl.multiple_of(b * CHUNK_BATCH, CHUNK_BATCH), CHUNK_BATCH)


def _chunks3(x):
    return x.reshape(CHUNK_BATCH, RET_CHUNK, HEAD_DIM)


def _ret_scan(buf, c_decs, nc, reverse):
    def step(n, carry):
        new = []
        for way, r in enumerate(carry):
            c = n if (way == 0) != reverse else nc - 1 - n
            term = buf[way, c]
            buf[way, c] = r
            new.append(r * c_decs[way] + term)
        return tuple(new)

    lax.fori_loop(0, nc, step, (jnp.zeros((HEAD_DIM, HEAD_DIM), F32),) * 2)


def _ret_fwd(proj, dec_f, dec_b, w_norm):
    S = proj.shape[0]
    H = proj.shape[1] // (N_GROUPS * HEAD_DIM)
    nc = S // RET_CHUNK
    scale = HEAD_DIM ** -0.5

    def body(df_ref, db_ref, q_ref, k_ref, v_ref, g_ref, w_ref, y_ref, o_ref, states):
        h = pl.program_id(0)
        consts = [_ret_consts(_log_decay(dref, h), fw) for fw, dref in ((True, df_ref), (False, db_ref))]

        def kv_step(b, carry):
            rows, batch = _batch_rows(b), _batch_chunks(b)
            k3 = _chunks3(k_ref[rows, :])
            v3 = _chunks3(v_ref[rows, :]).astype(BF16)
            for way in range(2):
                states[way, batch] = _bdot((k3 * consts[way][4]).astype(BF16), v3, _BTN)
            return carry

        lax.fori_loop(0, nc // CHUNK_BATCH, kv_step, 0)
        _ret_scan(states, [c[6] for c in consts], nc, False)

        def out_step(b, carry):
            rows, batch = _batch_rows(b), _batch_chunks(b)
            q3 = _chunks3(q_ref[rows, :] * scale)
            k3 = _chunks3(k_ref[rows, :]).astype(BF16)
            v3 = _chunks3(v_ref[rows, :]).astype(BF16)
            a0 = _bdot(q3.astype(BF16), k3, _BNT)
            o = None
            for way in range(2):
                mask, q_dec = consts[way][0], consts[way][2]
                part = _bdot((a0 * mask).astype(BF16), v3, _BNN) \
                    + _bdot((q3 * q_dec).astype(BF16), states[way, batch].astype(BF16), _BNN)
                o = part if o is None else o + part
            o_ref[rows, :] = o.reshape(CHUNK_BATCH * RET_CHUNK, HEAD_DIM)
            return carry

        lax.fori_loop(0, nc // CHUNK_BATCH, out_step, 0)
        o = o_ref[...]
        g = g_ref[...]
        y_ref[...] = o * _rms_scale(o) * w_ref[...] * (g * _sigmoid(g))

    hspec = pl.BlockSpec((S, HEAD_DIM), lambda h: (0, h))
    smem = pl.BlockSpec(memory_space=pltpu.SMEM)
    return pl.pallas_call(
        body, grid=(H,), name="ret_fwd",
        in_specs=[smem, smem] + _head_specs(S, (3, 4, 5, 6), H) + [pl.BlockSpec((1, HEAD_DIM), lambda h: (0, h))],
        out_specs=[hspec, hspec],
        out_shape=[SDS((S, H * HEAD_DIM), F32)] * 2,
        scratch_shapes=[pltpu.VMEM((2, nc, HEAD_DIM, HEAD_DIM), F32)],
        compiler_params=_cp(1))(dec_f, dec_b, proj, proj, proj, proj, w_norm)


def _ret_gate_bwd(proj, o_raw, dmix, w_norm, col0):
    S = proj.shape[0]
    H = proj.shape[1] // (N_GROUPS * HEAD_DIM)

    def body(g_ref, o_ref, dy_ref, w_ref, do_ref, dg_ref, dw_ref):
        o = o_ref[...]
        g = g_ref[...]
        dy = dy_ref[...]
        w = w_ref[...]
        rr = _rms_scale(o)
        normed = o * rr
        sg = _sigmoid(g)
        silu = g * sg
        dw_ref[0] = jnp.broadcast_to(jnp.sum(dy * normed * silu, axis=0, keepdims=True), (8, HEAD_DIM))
        dg_ref[...] = (dy * normed * w * (sg * (1.0 + g * (1.0 - sg)))).astype(BF16)
        dnormed = dy * w * silu
        do_ref[...] = rr * dnormed - o * (rr * rr * rr) * jnp.mean(dnormed * o, axis=-1, keepdims=True)

    hspec = pl.BlockSpec((S, HEAD_DIM), lambda h: (0, h))
    nh0 = col0 // HEAD_DIM
    return pl.pallas_call(
        body, grid=(H,), name="ret_gate_bwd",
        in_specs=_head_specs(S, (6,), H) + [hspec, pl.BlockSpec((S, HEAD_DIM), lambda h: (0, nh0 + h)),
                                            pl.BlockSpec((1, HEAD_DIM), lambda h: (0, h))],
        out_specs=[hspec, hspec, pl.BlockSpec((1, 8, HEAD_DIM), lambda h: (h, 0, 0))],
        out_shape=[SDS((S, H * HEAD_DIM), F32), SDS((S, H * HEAD_DIM), BF16), SDS((H, 8, HEAD_DIM), F32)],
        compiler_params=_cp(1))(proj, o_raw, dmix, w_norm)


def _ret_bwd(proj, d_out, dec_f, dec_b):
    S = proj.shape[0]
    H = proj.shape[1] // (N_GROUPS * HEAD_DIM)
    C = RET_CHUNK
    nc = S // C
    scale = HEAD_DIM ** -0.5

    def body(df_ref, db_ref, q_ref, k_ref, v_ref, do, dq_ref, dk_ref, dv_ref, small_ref, states, d_states):
        h = pl.program_id(0)
        lgs = [_log_decay(df_ref, h), _log_decay(db_ref, h)]
        consts = [_ret_consts(lg, fw) for lg, fw in zip(lgs, (True, False))]

        def prep_step(b, carry):
            rows, batch = _batch_rows(b), _batch_chunks(b)
            q3 = _chunks3(q_ref[rows, :] * scale)
            k3 = _chunks3(k_ref[rows, :])
            v3 = _chunks3(v_ref[rows, :]).astype(BF16)
            do3 = _chunks3(do[rows, :]).astype(BF16)
            for way in range(2):
                states[way, batch] = _bdot((k3 * consts[way][4]).astype(BF16), v3, _BTN)
                d_states[way, batch] = _bdot((q3 * consts[way][2]).astype(BF16), do3, _BTN)
            return carry

        lax.fori_loop(0, nc // CHUNK_BATCH, prep_step, 0)
        c_decs = [c[6] for c in consts]
        _ret_scan(states, c_decs, nc, False)
        _ret_scan(d_states, c_decs, nc, True)

        def main_step(b, dlams):
            rows, batch = _batch_rows(b), _batch_chunks(b)
            q3 = _chunks3(q_ref[rows, :] * scale)
            k3 = _chunks3(k_ref[rows, :])
            q3b, k3b = q3.astype(BF16), k3.astype(BF16)
            v3b = _chunks3(v_ref[rows, :]).astype(BF16)
            do3b = _chunks3(do[rows, :]).astype(BF16)
            a0 = _bdot(q3b, k3b, _BNT)
            pv = _bdot(do3b, v3b, _BNT)
            dq = dk = dv = None
            new_dlams = []
            for way in range(2):
                mask, relf, q_dec, q_exp, k_dec, k_exp, c_dec = consts[way]
                state, d_state = states[way, batch], d_states[way, batch]
                dp = pv * mask
                dpb = dp.astype(BF16)
                gq = _bdot(do3b, state.astype(BF16), _BNT)
                gk = _bdot(v3b, d_state.astype(BF16), _BNT)
                parts = (_bdot(dpb, k3b, _BNN) + q_dec * gq, _bdot(dpb, q3b, _BTN) + k_dec * gk,
                         _bdot((a0 * mask).astype(BF16), do3b, _BTN)
                         + _bdot((k3 * k_dec).astype(BF16), d_state.astype(BF16), _BNN))
                dq, dk, dv = parts if dq is None else (dq + parts[0], dk + parts[1], dv + parts[2])
                total = lambda x: jnp.sum(jnp.sum(x, axis=0), axis=0, keepdims=True)
                new_dlams.append(dlams[way] + total(relf * a0 * dp)
                                 + total(q_exp * q_dec * q3 * gq + k_exp * k_dec * k3 * gk)
                                 + (C * c_dec) * total(state * d_state))
            flat = lambda x: x.reshape(CHUNK_BATCH * C, HEAD_DIM)
            dq_ref[rows, :] = (flat(dq) * scale).astype(BF16)
            dk_ref[rows, :] = flat(dk).astype(BF16)
            dv_ref[rows, :] = flat(dv).astype(BF16)
            return tuple(new_dlams)

        dlams = lax.fori_loop(0, nc // CHUNK_BATCH, main_step, (jnp.zeros((1, HEAD_DIM), F32),) * 2)
        for row, (dlam, lg) in enumerate(zip(dlams, lgs)):
            small_ref[0, pl.ds(row, 1), :] = jnp.broadcast_to(jnp.sum(dlam, axis=-1, keepdims=True) * lg, (1, HEAD_DIM))
        small_ref[0, pl.ds(2, 6), :] = jnp.zeros((6, HEAD_DIM), F32)

    hspec = pl.BlockSpec((S, HEAD_DIM), lambda h: (0, h))
    smem = pl.BlockSpec(memory_space=pltpu.SMEM)
    return pl.pallas_call(
        body, grid=(H,), name="ret_bwd",
        in_specs=[smem, smem] + _head_specs(S, (3, 4, 5), H) + [hspec],
        out_specs=[hspec, hspec, hspec, pl.BlockSpec((1, 8, HEAD_DIM), lambda h: (h, 0, 0))],
        out_shape=[SDS((S, H * HEAD_DIM), BF16)] * 3 + [SDS((H, 8, HEAD_DIM), F32)],
        scratch_shapes=[pltpu.VMEM((2, nc, HEAD_DIM, HEAD_DIM), F32), pltpu.VMEM((2, nc, HEAD_DIM, HEAD_DIM), F32)],
        compiler_params=_cp(1))(dec_f, dec_b, proj, proj, proj, d_out)


def _ffn_bwd_act(dh2, wd, g, u):
    S, D = dh2.shape
    nblk, _, FB = g.shape
    tm = min(512, S)

    def body(dh_ref, wd_ref, g_ref, u_ref, dg_ref, du_ref):
        dact = _dot(dh_ref[...], wd_ref[...], _NT)
        gg = g_ref[0].astype(F32)
        sg = _sigmoid(gg)
        dg_ref[0] = (dact * u_ref[0].astype(F32) * (sg * (1.0 + gg * (1.0 - sg)))).astype(BF16)
        du_ref[0] = (dact * (gg * sg)).astype(BF16)

    blk = pl.BlockSpec((1, tm, FB), lambda j, i: (j, i, 0))
    return pl.pallas_call(
        body, grid=(nblk, S // tm), name="ffn_bwd_act",
        in_specs=[pl.BlockSpec((tm, D), lambda j, i: (i, 0)), pl.BlockSpec((FB, D), lambda j, i: (j, 0)), blk, blk],
        out_specs=[blk, blk], out_shape=[SDS((nblk, S, FB), BF16)] * 2,
        compiler_params=_cp(2))(dh2, wd, g, u)


def _ffn_bwd_in(dg, du, wg, wu, h1, dh2, w_norm):
    nblk, S, FB = dg.shape
    D = h1.shape[1]
    tm = min(RESIDENT_ROWS, S)
    blk = pl.BlockSpec((nblk, tm, FB), lambda i: (0, i, 0))
    row = pl.BlockSpec((tm, D), lambda i: (i, 0))
    vec = pl.BlockSpec((1, D), lambda i: (0, 0))

    def gate_body(dg_ref, wg_ref, part_ref):
        part_ref[...] = _blocked_matmul(dg_ref, wg_ref)

    part = pl.pallas_call(
        gate_body, grid=(S // tm,), name="ffn_bwd_in_gate", in_specs=[blk, _resident((nblk * FB, D))],
        out_specs=row, out_shape=SDS((S, D), F32), compiler_params=_cp(1))(dg, wg.reshape(nblk * FB, D))

    def body(du_ref, wu_ref, part_ref, h_ref, dh2_ref, wn_ref, dh_ref, dhb_ref, dw_ref):
        @pl.when(pl.program_id(0) == 0)
        def _():
            dw_ref[...] = jnp.zeros_like(dw_ref)

        dh, dw = _rms_bwd(part_ref[...] + _blocked_matmul(du_ref, wu_ref), h_ref[...], wn_ref[...])
        dh = dh2_ref[...] + dh
        dh_ref[...] = dh
        dhb_ref[...] = dh.astype(BF16)
        dw_ref[...] += dw

    return pl.pallas_call(
        body, grid=(S // tm,), name="ffn_bwd_in",
        in_specs=[blk, _resident((nblk * FB, D)), row, row, row, vec],
        out_specs=[row, row, vec], out_shape=[SDS((S, D), F32), SDS((S, D), BF16), SDS((1, D), F32)],
        compiler_params=_cp(1))(du, wu.reshape(nblk * FB, D), part, h1, dh2, w_norm)


def _dmix(dh1, w_out):
    S, D = dh1.shape
    tm = min(512, S)

    def body(dh_ref, w_ref, o_ref):
        o_ref[...] = _dot(dh_ref[...], w_ref[...], _NT)

    row = pl.BlockSpec((tm, D), lambda i: (i, 0))
    return pl.pallas_call(
        body, grid=(S // tm,), name="dmix", in_specs=[row, pl.BlockSpec((D, D), lambda i: (0, 0))],
        out_specs=row, out_shape=SDS((S, D), F32), compiler_params=_cp(1))(dh1, w_out)


def _in_bwd(dproj, w_blk, x, dh1, w_norm):
    S, D = x.shape
    nblk, _, NB = w_blk.shape
    tm = min(512, S)

    def body(dp_ref, w_ref, x_ref, dh1_ref, wn_ref, dx_ref, dw_ref, acc):
        i, j = pl.program_id(0), pl.program_id(1)

        @pl.when(j == 0)
        def _():
            acc[...] = jnp.zeros_like(acc)

        @pl.when((i == 0) & (j == 0))
        def _():
            dw_ref[...] = jnp.zeros_like(dw_ref)

        acc[...] += _dot(dp_ref[...], w_ref[0], _NT)

        @pl.when(j == nblk - 1)
        def _():
            dh, dw = _rms_bwd(acc[...], x_ref[...], wn_ref[...])
            dx_ref[...] = dh1_ref[...] + dh
            dw_ref[...] += dw

    row = pl.BlockSpec((tm, D), lambda i, j: (i, 0))
    vec = pl.BlockSpec((1, D), lambda i, j: (0, 0))
    return pl.pallas_call(
        body, grid=(S // tm, nblk), name="in_bwd",
        in_specs=[pl.BlockSpec((tm, NB), lambda i, j: (i, j)), pl.BlockSpec((1, D, NB), lambda i, j: (j, 0, 0)),
                  row, row, vec],
        out_specs=[row, vec], out_shape=[SDS((S, D), F32), SDS((1, D), F32)],
        scratch_shapes=[pltpu.VMEM((tm, D), F32)], compiler_params=_cp(2))(dproj, w_blk, x, dh1, w_norm)


def _wgrad(a, b, a_spec, b_spec, o_spec, o_shape, grid, name):
    nk = grid[-1]

    def ld(ref):
        return ref[0] if len(ref.shape) == 3 else ref[...]

    def body(a_ref, b_ref, o_ref, acc):
        k = pl.program_id(len(grid) - 1)

        @pl.when(k == 0)
        def _():
            acc[...] = jnp.zeros_like(acc)

        acc[...] += _dot(ld(a_ref), ld(b_ref), _TN)

        @pl.when(k == nk - 1)
        def _():
            if len(o_ref.shape) == 3:
                o_ref[0] = acc[...].astype(o_ref.dtype)
            else:
                o_ref[...] = acc[...].astype(o_ref.dtype)

    return pl.pallas_call(
        body, grid=grid, name=name, in_specs=[a_spec, b_spec], out_specs=o_spec, out_shape=SDS(o_shape, BF16),
        scratch_shapes=[pltpu.VMEM(o_spec.block_shape[-2:], F32)], compiler_params=_cp(len(grid)))(a, b)


def _peer(k):
    x, y, c = lax.axis_index("x"), lax.axis_index("y"), lax.axis_index("c")
    px = 1 - x if k & 4 else x
    py = 1 - y if k & 2 else y
    pc = 1 - c if k & 1 else c
    return (px, py, pc), 4 * px + 2 * py + pc


def _exchange_copies(srcs, lands, send_sems, recv_sems, which, gather):
    _, me = _peer(0)
    pairs = []
    for pos, a in enumerate(which):
        for k in range(1, N_DEV):
            dev, idx = _peer(k)
            sem = pos * (N_DEV - 1) + k - 1
            src = srcs[a] if gather else srcs[a].at[idx]
            mk = functools.partial(pltpu.make_async_remote_copy, src_ref=src, send_sem=send_sems.at[sem],
                                   recv_sem=recv_sems.at[sem], device_id=dev, device_id_type=MESH)
            pairs.append((mk(dst_ref=lands[a].at[me]), mk(dst_ref=lands[a].at[idx])))
    return pairs


def _sequencer_kernel(name, collective_id, n_remote, n_local):
    return pl.kernel(mesh=plsc.ScalarSubcoreMesh(axis_name="sequencer", num_cores=1), name=name,
                     scratch_types=(pltpu.SemaphoreType.DMA((n_remote,)), pltpu.SemaphoreType.DMA((n_remote,)),
                                    pltpu.SemaphoreType.DMA((n_local,))),
                     compiler_params=pltpu.CompilerParams(collective_id=collective_id))


def _handshake(ks):
    barrier = pltpu.get_barrier_semaphore()
    for k in ks:
        pl.semaphore_signal(barrier, inc=1, device_id=_peer(k)[0], device_id_type=MESH)
    pl.semaphore_wait(barrier, len(ks))


def _sequencer_scatter(arrays, name, collective_id):
    n = len(arrays)
    hbm = pltpu.MemorySpace.HBM
    srcs = [jax.new_ref(a, memory_space=hbm) for a in arrays]
    lands = [jax.empty_ref(SDS(a.shape, a.dtype), memory_space=hbm) for a in arrays]

    @_sequencer_kernel(name, collective_id, n * (N_DEV - 1), n)
    def launch(send_sems, recv_sems, local_sems):
        _handshake(range(1, N_DEV))
        _, me = _peer(0)
        local = [pltpu.make_async_copy(srcs[a].at[me], lands[a].at[me], local_sems.at[a]) for a in range(n)]
        pairs = _exchange_copies(srcs, lands, send_sems, recv_sems, range(n), False)
        for out, _ in pairs:
            out.start()
        for cp in local:
            cp.start()
        for out, arrival in pairs:
            out.wait_send()
            arrival.wait_recv()
        for cp in local:
            cp.wait()

    launch()
    return [r[...] for r in lands]


SIBLING = 1
OTHER_CHIPS = (2, 4, 6)


def _sequencer_gather(arrays, name, collective_id):
    n = len(arrays)
    hbm = pltpu.MemorySpace.HBM
    srcs = [jax.new_ref(a, memory_space=hbm) for a in arrays]
    lands = [jax.empty_ref(SDS((N_DEV,) + a.shape, a.dtype), memory_space=hbm) for a in arrays]

    @_sequencer_kernel(name, collective_id, n * (N_DEV - 1), n)
    def launch(send_sems, recv_sems, local_sems):
        _handshake((SIBLING,) + OTHER_CHIPS)
        _, me = _peer(0)
        sibling, _ = _peer(SIBLING)

        def copy(a, k, src, block, to):
            sem = a * (N_DEV - 1) + k - 1
            return pltpu.make_async_remote_copy(src_ref=src, dst_ref=lands[a].at[block], send_sem=send_sems.at[sem],
                                                recv_sem=recv_sems.at[sem], device_id=to, device_id_type=MESH)

        local = [pltpu.make_async_copy(srcs[a], lands[a].at[me], local_sems.at[a]) for a in range(n)]
        first = [copy(a, k, srcs[a], me, _peer(k)[0]) for a in range(n) for k in OTHER_CHIPS + (SIBLING,)]
        for cp in first + local:
            cp.start()
        passed = []
        for a in range(n):
            for k in OTHER_CHIPS:
                _, block = _peer(k)
                copy(a, k, srcs[a], block, sibling).wait_recv()
                passed.append(copy(a, k ^ SIBLING, lands[a].at[block], block, sibling))
                passed[-1].start()
        for a in range(n):
            for k in (SIBLING,) + tuple(k ^ SIBLING for k in OTHER_CHIPS):
                copy(a, k, srcs[a], _peer(k)[1], sibling).wait_recv()
        for cp in first + passed:
            cp.wait_send()
        for cp in local:
            cp.wait()

    launch()
    return [r[...] for r in lands]


SMALL_ROWS = 64


def _small_step(part, w, m, v):
    def body(p_ref, w_ref, m_ref, v_ref, g_ref, d_ref, nm_ref, nv_ref, gath, send_sems, recv_sems):
        _, me = _peer(0)
        gath[me] = p_ref[...]
        copies = []
        for k in range(1, N_DEV):
            dev, idx = _peer(k)
            out = pltpu.make_async_remote_copy(src_ref=p_ref, dst_ref=gath.at[me], send_sem=send_sems.at[k - 1],
                                               recv_sem=recv_sems.at[k - 1], device_id=dev, device_id_type=MESH)
            out.start()
            arrival = pltpu.make_async_remote_copy(src_ref=p_ref, dst_ref=gath.at[idx], send_sem=send_sems.at[k - 1],
                                                   recv_sem=recv_sems.at[k - 1], device_id=dev, device_id_type=MESH)
            copies.append((out, arrival))
        for out, arrival in copies:
            out.wait_send()
            arrival.wait_recv()
        g = gath[0]
        for p in range(1, N_DEV):
            g = g + gath[p]
        g_ref[...] = g
        d_ref[...], nm_ref[...], nv_ref[...] = _adamw(w_ref[...], g, m_ref[...], v_ref[...])

    vm = pl.BlockSpec(memory_space=pltpu.VMEM)
    return pl.pallas_call(
        body, name="small_step", in_specs=[vm] * 4, out_specs=[vm] * 4,
        out_shape=[SDS((SMALL_ROWS, 128), F32)] * 4,
        scratch_shapes=[pltpu.VMEM((N_DEV, SMALL_ROWS, 128), F32), pltpu.SemaphoreType.DMA((N_DEV - 1,)),
                        pltpu.SemaphoreType.DMA((N_DEV - 1,))])(part, w, m, v)


def _adamw(w, g, m, v):
    m = ADAM_B1 * m + (1.0 - ADAM_B1) * g
    v = ADAM_B2 * v + (1.0 - ADAM_B2) * (g * g)
    m_hat = m / (1.0 - ADAM_B1 ** ADAM_STEP)
    v_hat = v / (1.0 - ADAM_B2 ** ADAM_STEP)
    delta = -ADAM_LR * (m_hat / (jnp.sqrt(v_hat) + ADAM_EPS) + ADAM_WD * w)
    return delta, m, v


def _adamw_block(parts, w, m, v, name):
    R, C = w.shape
    n_parts = len(parts)
    Rp = R // n_parts
    tr = next(t for t in (256, 128, 64, 32, 16, 8) if Rp % t == 0 and t * C <= 256 * 1024)
    per_part = Rp // tr

    def body(*refs):
        p_refs = refs[:n_parts]
        w_ref, m_ref, v_ref, g_ref, d_ref, nm_ref, nv_ref = refs[n_parts:]
        for k, p_ref in enumerate(p_refs):
            @pl.when(pl.program_id(0) // per_part == k)
            def _(p_ref=p_ref):
                g = p_ref[0].astype(F32)
                for p in range(1, N_DEV):
                    g = g + p_ref[p].astype(F32)
                g_ref[...] = g
                d_ref[...], nm_ref[...], nv_ref[...] = _adamw(w_ref[...], g, m_ref[...], v_ref[...])

    row = pl.BlockSpec((tr, C), lambda i: (i, 0))
    part_specs = [pl.BlockSpec((N_DEV, tr, C), functools.partial(
        lambda i, k: (0, jnp.clip(i - k * per_part, 0, per_part - 1), 0), k=k)) for k in range(n_parts)]
    return pl.pallas_call(
        body, grid=(R // tr,), name=name, in_specs=part_specs + [row, row, row],
        out_specs=[row] * 4, out_shape=[SDS((R, C), F32)] * 4, compiler_params=_cp(1))(*parts, w, m, v)


def _pack_small(mix, ffn, fin, retw, dec_f, dec_b, loss):
    flat = jnp.concatenate([mix.reshape(-1), ffn.reshape(-1), fin.reshape(-1), retw.reshape(-1), dec_f.reshape(-1),
                            dec_b.reshape(-1), loss.reshape(-1)])
    return jnp.pad(flat, (0, SMALL_ROWS * 128 - flat.shape[0])).reshape(SMALL_ROWS, 128)


def _unpack_small(packed, shapes):
    flat = packed.reshape(-1)
    out, at = [], 0
    for s in shapes:
        n = math.prod(s)
        out.append(flat[at:at + n].reshape(s))
        at += n
    return out


def kernel(x, norm_mix_w, w_in, ret_decay_fwd, ret_decay_bwd, ret_norm_w, w_out, norm_ffn_w, w_gate, w_up, w_down, norm_final_w, loss_target, m_norm_mix_w, m_w_in, m_ret_decay_fwd, m_ret_decay_bwd, m_ret_norm_w, m_w_out, m_norm_ffn_w, m_w_gate, m_w_up, m_w_down, m_norm_final_w, v_norm_mix_w, v_w_in, v_ret_decay_fwd, v_ret_decay_bwd, v_ret_norm_w, v_w_out, v_norm_ffn_w, v_w_gate, v_w_up, v_w_down, v_norm_final_w):
    x2 = x[0]
    tgt = loss_target[0]
    S, D = x2.shape
    H = ret_norm_w.shape[1] // HEAD_DIM
    DA = H * HEAD_DIM
    fin_w = norm_final_w.reshape(1, D)
    big = (w_in[0], w_out[0], w_gate[0].T, w_up[0].T, w_down[0])

    big_b = [w.astype(BF16) for w in big]
    wi, = _sequencer_gather(big_b[:1], "gather_in", 0)
    wo, wg, wu = _sequencer_gather(big_b[1:4], "gather_mid", 1)
    wd, = _sequencer_gather(big_b[4:], "gather_down", 5)
    NB = wi.shape[2]

    proj, n1 = _proj_fwd(x2, norm_mix_w, wi)
    bias = _attn_bias()[:H]
    attn, lse = _attn_fwd(proj, bias)
    ret, o_raw = _ret_fwd(proj, ret_decay_fwd, ret_decay_bwd, ret_norm_w)
    wo_full = wo.reshape(D, D)
    d_ff = N_DEV * wd.shape[1]
    FB = FFN_BLOCK if d_ff % FFN_BLOCK == 0 else wd.shape[1]
    n_fb = d_ff // FB
    wg, wu = wg.reshape(n_fb, FB, D), wu.reshape(n_fb, FB, D)
    wd_full = wd.reshape(d_ff, D)
    h1, mixed, n2 = _out_fwd(x2, attn, ret, wo_full, norm_ffn_w)
    gate, up, act = _ffn_up(n2, wg, wu)
    dh2, dh2_b, loss_parts, g_fin = _ffn_down_loss(act, wd_full, h1, tgt, fin_w)

    dgate, dup = _ffn_bwd_act(dh2_b, wd_full, gate, up)
    tn = min(1024, D)
    ffn_specs = (pl.BlockSpec((1, S, FB), lambda j, n, k: (j, 0, 0)), pl.BlockSpec((S, tn), lambda j, n, k: (0, n)),
                 pl.BlockSpec((1, FB, tn), lambda j, n, k: (j, 0, n)), (n_fb, FB, D), (n_fb, D // tn, 1))
    per_dev = (N_DEV, d_ff // N_DEV, D)
    g_wd = _wgrad(act, dh2_b, *ffn_specs, "wgrad_down").reshape(per_dev)
    g_wg = _wgrad(dgate, n2, *ffn_specs, "wgrad_gate").reshape(per_dev)
    g_wu = _wgrad(dup, n2, *ffn_specs, "wgrad_up").reshape(per_dev)
    parts_f = _sequencer_scatter([g_wg, g_wu, g_wd], "scatter_ffn", 2)
    dh1, dh1_b, g_ffn = _ffn_bwd_in(dgate, dup, wg, wu, h1, dh2, norm_ffn_w)
    dmix = _dmix(dh1_b, wo_full)
    tmw = min(512, D)
    tk = min(2048, S)
    g_wo = _wgrad(mixed, dh1_b, pl.BlockSpec((tk, tmw), lambda m, k: (k, m)), pl.BlockSpec((tk, D), lambda m, k: (k, 0)),
                  pl.BlockSpec((tmw, D), lambda m, k: (m, 0)), (D, D), (D // tmw, S // tk), "wgrad_out")
    parts_o = _sequencer_scatter([g_wo.reshape(N_DEV, D // N_DEV, D)], "scatter_out", 3)
    d_ret, dg_r, small_w = _ret_gate_bwd(proj, o_raw, dmix, ret_norm_w, DA)
    dq_r, dk_r, dv_r, small = _ret_bwd(proj, d_ret, ret_decay_fwd, ret_decay_bwd)
    dq_a, dk_a, dv_a = _attn_bwd(proj, attn, lse, dmix, bias)
    dproj = jnp.concatenate([t.astype(BF16) for t in (dq_a, dk_a, dv_a, dq_r, dk_r, dv_r, dg_r)], axis=1)
    half = D // tmw // 2
    parts_i = []
    for part, (name, cid) in enumerate((("in_lo", 4), ("in_hi", 6))):
        g_wi = _wgrad(n1, dproj, pl.BlockSpec((S, tmw), functools.partial(lambda j, m, k, off: (0, m + off), off=part * half)),
                      pl.BlockSpec((S, NB), lambda j, m, k: (0, j)), pl.BlockSpec((1, tmw, NB), lambda j, m, k: (j, m, 0)),
                      (N_DEV, D // 2, NB), (N_DEV, half, 1), "wgrad_" + name)
        parts_i += _sequencer_scatter([g_wi], "scatter_" + name, cid)
    grad_x, g_mix = _in_bwd(dproj, wi, x2, dh1, norm_mix_w)

    big_m = (m_w_in[0], m_w_out[0], m_w_gate[0].T, m_w_up[0].T, m_w_down[0])
    big_v = (v_w_in[0], v_w_out[0], v_w_gate[0].T, v_w_up[0].T, v_w_down[0])
    names = ("adamw_in", "adamw_out", "adamw_gate", "adamw_up", "adamw_down")
    upd = [None] * 5
    for a, p in zip((2, 3, 4, 1, 0), [[t] for t in parts_f + parts_o] + [parts_i]):
        upd[a] = _adamw_block(p, big[a], big_m[a], big_v[a], names[a])

    g_dec_f = small[:, 0, 0].reshape(1, H)
    g_dec_b = small[:, 1, 0].reshape(1, H)
    g_retw = small_w[:, 0, :].reshape(1, DA)
    loss_local = jnp.sum(loss_parts[::8, 0])
    zero = jnp.zeros((1,), F32)
    part = _pack_small(g_mix, g_ffn, g_fin, g_retw, g_dec_f, g_dec_b, loss_local)
    sw = _pack_small(norm_mix_w, norm_ffn_w, norm_final_w, ret_norm_w, ret_decay_fwd, ret_decay_bwd, zero)
    sm = _pack_small(m_norm_mix_w, m_norm_ffn_w, m_norm_final_w, m_ret_norm_w, m_ret_decay_fwd, m_ret_decay_bwd, zero)
    sv = _pack_small(v_norm_mix_w, v_norm_ffn_w, v_norm_final_w, v_ret_norm_w, v_ret_decay_fwd, v_ret_decay_bwd, zero)
    shapes = [(1, D), (1, D), (D,), (1, DA), (1, H), (1, H), ()]
    sg, sd, snm, snv = [_unpack_small(t, shapes) for t in _small_step(part, sw, sm, sv)]
    loss = sg[6]

    def ordered(small_set, k):
        b = [(u[k].T if a in (2, 3) else u[k])[None] for a, u in enumerate(upd)]
        return [small_set[0], b[0], small_set[4], small_set[5], small_set[3], b[1], small_set[1], b[2], b[3], b[4],
                small_set[2]]

    return (loss, grad_x[None], *ordered(sg, 0), *ordered(sd, 1), *ordered(snm, 2), *ordered(snv, 3))
```

```python
import functools
import math

import numpy as np
import jax
import jax.numpy as jnp
from jax import lax
from jax.experimental import pallas as pl
from jax.experimental.pallas import tpu as pltpu
from jax.experimental.pallas import tpu_sc as plsc

F32 = jnp.float32
BF16 = jnp.bfloat16
SDS = jax.ShapeDtypeStruct

HEAD_DIM = 128
EPS = 1e-6
RET_CHUNK = 128
DILATIONS = (1, 4, 16)
BAND = 64
Q_TILE = 128
K_TILE = Q_TILE + 2 * BAND
KV_PAD = BAND * 4
TILE_GROUP = 8
NEG = -1e30
N_DEV = 8
N_GROUPS = 7
ADAM_LR, ADAM_B1, ADAM_B2, ADAM_EPS, ADAM_WD, ADAM_STEP = 0.001, 0.9, 0.999, 1e-08, 0.01, 10
VMEM_LIMIT = 56 * 1024 * 1024
MESH = pl.DeviceIdType.MESH
ANY = pl.BlockSpec(memory_space=pl.ANY)


def _cp(n_grid):
    return pltpu.CompilerParams(dimension_semantics=("arbitrary",) * n_grid, vmem_limit_bytes=VMEM_LIMIT)


def _sigmoid(x):
    return 1.0 / (1.0 + jnp.exp(-x))


def _rms_scale(h):
    return lax.rsqrt(jnp.mean(h * h, axis=-1, keepdims=True) + EPS)


def _rms_bwd(dn, h, w):
    r = _rms_scale(h)
    gw = dn * w
    dh = r * gw - h * (r * r * r) * jnp.mean(gw * h, axis=-1, keepdims=True)
    return dh, jnp.sum(dn * h * r, axis=0, keepdims=True)


def _dot(a, b, dims):
    return lax.dot_general(a.astype(BF16), b.astype(BF16), (dims, ((), ())), preferred_element_type=F32)


_NN = ((1,), (0,))
_NT = ((1,), (1,))
_TN = ((0,), (0,))


RESIDENT_ROWS = 256


def _resident(shape):
    return pl.BlockSpec(shape, lambda i: (0, 0), pipeline_mode=pl.Buffered(1))


def _blocked_matmul(a_ref, w_ref):
    nblk, _, fb = a_ref.shape
    out = None
    for j in range(nblk):
        part = jnp.dot(a_ref[j], w_ref[pl.ds(j * fb, fb), :], preferred_element_type=F32)
        out = part if out is None else out + part
    return out


def _proj_fwd(x, w_norm, w_blk):
    S, D = x.shape
    nblk, _, NB = w_blk.shape
    tm = min(1024, S)

    def body(x_ref, wn_ref, w_ref, proj_ref, n_ref, n_scr):
        @pl.when(pl.program_id(1) == 0)
        def _():
            xf = x_ref[...]
            nb = (xf * _rms_scale(xf) * wn_ref[...]).astype(BF16)
            n_scr[...] = nb
            n_ref[...] = nb
        proj_ref[...] = jnp.dot(n_scr[...], w_ref[0], preferred_element_type=F32)

    return pl.pallas_call(
        body, grid=(S // tm, nblk), name="proj_fwd",
        in_specs=[pl.BlockSpec((tm, D), lambda i, j: (i, 0)), pl.BlockSpec((1, D), lambda i, j: (0, 0)),
                  pl.BlockSpec((1, D, NB), lambda i, j: (j, 0, 0))],
        out_specs=[pl.BlockSpec((tm, NB), lambda i, j: (i, j)), pl.BlockSpec((tm, D), lambda i, j: (i, 0))],
        out_shape=[SDS((S, nblk * NB), F32), SDS((S, D), BF16)],
        scratch_shapes=[pltpu.VMEM((tm, D), BF16)], compiler_params=_cp(2))(x, w_norm, w_blk)


def _out_fwd(x, attn, ret, w_out, w_norm):
    S, D = x.shape
    DA = attn.shape[1]
    tm = min(256, S)

    def body(x_ref, a_ref, r_ref, w_ref, wn_ref, h_ref, mix_ref, n_ref):
        a = a_ref[...].astype(BF16)
        r = r_ref[...].astype(BF16)
        mix_ref[:, :DA] = a
        mix_ref[:, DA:] = r
        h = x_ref[...] + jnp.dot(a, w_ref[:DA, :], preferred_element_type=F32) \
            + jnp.dot(r, w_ref[DA:, :], preferred_element_type=F32)
        h_ref[...] = h
        n_ref[...] = (h * _rms_scale(h) * wn_ref[...]).astype(BF16)

    row = lambda w: pl.BlockSpec((tm, w), lambda i: (i, 0))
    return pl.pallas_call(
        body, grid=(S // tm,), name="out_fwd",
        in_specs=[row(D), row(DA), row(D - DA), pl.BlockSpec((D, D), lambda i: (0, 0)),
                  pl.BlockSpec((1, D), lambda i: (0, 0))],
        out_specs=[row(D), row(D), row(D)],
        out_shape=[SDS((S, D), F32), SDS((S, D), BF16), SDS((S, D), BF16)],
        compiler_params=_cp(1))(x, attn, ret, w_out, w_norm)


def _ffn_up(n2, wg, wu):
    S, D = n2.shape
    nblk, FB, _ = wg.shape
    tm = min(512, S)

    def body(n_ref, wg_ref, wu_ref, g_ref, u_ref, a_ref):
        n = n_ref[...]
        g = _dot(n, wg_ref[0], _NT)
        u = _dot(n, wu_ref[0], _NT)
        g_ref[0] = g.astype(BF16)
        u_ref[0] = u.astype(BF16)
        a_ref[0] = (g * _sigmoid(g) * u).astype(BF16)

    wspec = pl.BlockSpec((1, FB, D), lambda j, i: (j, 0, 0))
    ospec = pl.BlockSpec((1, tm, FB), lambda j, i: (j, i, 0))
    return pl.pallas_call(
        body, grid=(nblk, S // tm), name="ffn_up",
        in_specs=[pl.BlockSpec((tm, D), lambda j, i: (i, 0)), wspec, wspec],
        out_specs=[ospec, ospec, ospec],
        out_shape=[SDS((nblk, S, FB), BF16)] * 3,
        compiler_params=_cp(2))(n2, wg, wu)


def _ffn_down_loss(act, wd, h1, target, w_norm):
    nblk, S, FB = act.shape
    D = h1.shape[1]
    tm = min(RESIDENT_ROWS, S)

    def body(a_ref, wd_ref, h_ref, t_ref, wn_ref, dh_ref, dhb_ref, loss_ref, dw_ref):
        @pl.when(pl.program_id(0) == 0)
        def _():
            dw_ref[...] = jnp.zeros_like(dw_ref)

        h = h_ref[...] + _blocked_matmul(a_ref, wd_ref)
        w = wn_ref[...]
        err = h * _rms_scale(h) * w - t_ref[...]
        loss_ref[...] = jnp.full(loss_ref.shape, 0.5 * jnp.sum(err * err) / D, F32)
        dh, dw = _rms_bwd(err * (1.0 / D), h, w)
        dh_ref[...] = dh
        dhb_ref[...] = dh.astype(BF16)
        dw_ref[...] += dw

    row = pl.BlockSpec((tm, D), lambda i: (i, 0))
    vec = pl.BlockSpec((1, D), lambda i: (0, 0))
    return pl.pallas_call(
        body, grid=(S // tm,), name="ffn_down_loss",
        in_specs=[pl.BlockSpec((nblk, tm, FB), lambda i: (0, i, 0)), _resident((nblk * FB, D)), row, row, vec],
        out_specs=[row, row, pl.BlockSpec((8, 128), lambda i: (i, 0)), vec],
        out_shape=[SDS((S, D), F32), SDS((S, D), BF16), SDS((S // tm * 8, 128), F32), SDS((1, D), F32)],
        compiler_params=_cp(1))(act, wd, h1, target, w_norm)


def _attn_bias():
    n_heads = 8
    slopes = np.exp2(-8.0 * np.arange(1, n_heads + 1, dtype=np.float32) / n_heads)
    dist = np.abs(np.arange(K_TILE)[None, :] - BAND - np.arange(Q_TILE)[:, None])
    out = np.empty((n_heads, len(DILATIONS), Q_TILE, K_TILE), np.float32)
    for h in range(n_heads):
        for p, d in enumerate(DILATIONS):
            out[h, p] = np.where(dist <= BAND, -slopes[h] * (d * dist).astype(np.float32), NEG)
    return jnp.asarray(out)


def _attn_tiles(S, d):
    L = S // d
    per_class = L // Q_TILE
    return L, per_class, d * per_class


def _tile_rows(t, d, per_class):
    r = t // per_class
    a = (t % per_class) * Q_TILE
    q_rows = pl.ds(r + d * a, Q_TILE, stride=d) if d > 1 else pl.ds(pl.multiple_of(a, Q_TILE), Q_TILE)
    k_rows = pl.ds(KV_PAD + r + d * (a - BAND), K_TILE, stride=d) if d > 1 else pl.ds(
        pl.multiple_of(KV_PAD + a - BAND, BAND), K_TILE)
    return a, q_rows, k_rows


def _to_quarters(dst, src, n, dst_off=0):
    for r in range(4):
        dst[pl.ds(dst_off + r * (n // 4), n // 4), :] = src[pl.ds(r, n // 4, stride=4), :]


def _quarter_tile_rows(t, S):
    L = S // 16
    per_class = L // Q_TILE
    blk, tt = t // (4 * per_class), t % (4 * per_class)
    r, a = tt // per_class, (tt % per_class) * Q_TILE
    q_rows = pl.ds(blk * (S // 4) + r + 4 * a, Q_TILE, stride=4)
    k_rows = pl.ds(KV_PAD + blk * (S // 4) + r + 4 * (a - BAND), K_TILE, stride=4)
    return a, q_rows, k_rows


def _lanes(x, width):
    return jnp.concatenate([x] * (width // HEAD_DIM), axis=-1)


_BNT = (((2,), (2,)), ((0,), (0,)))
_BNN = (((2,), (1,)), ((0,), (0,)))
_BTN = (((1,), (1,)), ((0,), (0,)))


def _bdot(a, b, dims):
    return lax.dot_general(a, b, dims, preferred_element_type=F32)


def _stacked(rows, loaders):
    return [jnp.stack([f(*r) for r in rows]) for f in loaders]


def _edge_mask(a, L):
    lk = lax.broadcasted_iota(jnp.int32, (1, K_TILE), 1) + (a - BAND)
    return jnp.where((lk >= 0) & (lk < L), 0.0, NEG).astype(F32)


def _fill_padded(dst, src, S):
    dst[pl.ds(0, KV_PAD), :] = jnp.zeros((KV_PAD, HEAD_DIM), F32)
    dst[pl.ds(KV_PAD + S, KV_PAD), :] = jnp.zeros((KV_PAD, HEAD_DIM), F32)
    dst[pl.ds(KV_PAD, S), :] = src[...]


def _head_specs(S, groups, n_heads):
    return [pl.BlockSpec((S, HEAD_DIM), functools.partial(lambda h, g: (0, g * n_heads + h), g=g)) for g in groups]


def _attn_fwd(proj, bias):
    S = proj.shape[0]
    H = proj.shape[1] // (N_GROUPS * HEAD_DIM)
    scale = HEAD_DIM ** -0.5

    def body(q_ref, k_ref, v_ref, b_ref, o_ref, lse_ref, kp, vp, m_run, l_run, q4, m3, l3, acc3):
        _fill_padded(kp, k_ref, S)
        _fill_padded(vp, v_ref, S)
        o_ref[...] = jnp.zeros_like(o_ref)
        m_run[...] = jnp.full(m_run.shape, NEG, F32)
        l_run[...] = jnp.zeros_like(l_run)
        for p, d in enumerate(DILATIONS[:2]):
            L, per_class, n_tiles = _attn_tiles(S, d)

            def tiles(t, carry, p=p, d=d, L=L, per_class=per_class, n_tiles=n_tiles):
                rows = [_tile_rows(t + u * (n_tiles // TILE_GROUP), d, per_class) for u in range(TILE_GROUP)]
                qs, ks, vs, m_old, l_old, o_old, edge = _stacked(rows, (
                    lambda a, qr, kr: q_ref[qr, :].astype(BF16), lambda a, qr, kr: kp[kr, :].astype(BF16),
                    lambda a, qr, kr: vp[kr, :].astype(BF16), lambda a, qr, kr: m_run[qr, :],
                    lambda a, qr, kr: l_run[qr, :], lambda a, qr, kr: o_ref[qr, :], lambda a, qr, kr: _edge_mask(a, L)))
                s = _bdot(qs, ks, _BNT) * scale + b_ref[0, p][None] + edge
                m_new = jnp.maximum(m_old, jnp.max(s, axis=-1, keepdims=True))
                pr = jnp.exp(s - _lanes(m_new, K_TILE)).astype(BF16)
                alpha = jnp.exp(m_old - m_new)
                l_new = alpha * l_old + _bdot(pr, jnp.ones((TILE_GROUP, K_TILE, HEAD_DIM), BF16), _BNN)
                o_new = alpha * o_old + _bdot(pr, vs, _BNN)
                for u, (_, qr, _) in enumerate(rows):
                    o_ref[qr, :] = o_new[u]
                    m_run[qr, :] = m_new[u]
                    l_run[qr, :] = l_new[u]
                return carry

            lax.fori_loop(0, n_tiles // TILE_GROUP, tiles, 0)

        _to_quarters(q4, q_ref, S)
        _to_quarters(kp, k_ref, S, KV_PAD)
        _to_quarters(vp, v_ref, S, KV_PAD)
        n_tiles = _attn_tiles(S, DILATIONS[2])[2]

        def tiles3(t, carry):
            rows = [_quarter_tile_rows(t + u * (n_tiles // TILE_GROUP), S) for u in range(TILE_GROUP)]
            qs, ks, vs, edge = _stacked(rows, (
                lambda a, qr, kr: q4[qr, :].astype(BF16), lambda a, qr, kr: kp[kr, :].astype(BF16),
                lambda a, qr, kr: vp[kr, :].astype(BF16), lambda a, qr, kr: _edge_mask(a, S // DILATIONS[2])))
            s = _bdot(qs, ks, _BNT) * scale + b_ref[0, 2][None] + edge
            m_new = jnp.broadcast_to(jnp.max(s, axis=-1, keepdims=True), (TILE_GROUP, Q_TILE, HEAD_DIM))
            pr = jnp.exp(s - _lanes(m_new, K_TILE)).astype(BF16)
            l_new = _bdot(pr, jnp.ones((TILE_GROUP, K_TILE, HEAD_DIM), BF16), _BNN)
            o_new = _bdot(pr, vs, _BNN)
            for u, (_, qr, _) in enumerate(rows):
                acc3[qr, :] = o_new[u]
                m3[qr, :] = m_new[u]
                l3[qr, :] = l_new[u]
            return carry

        lax.fori_loop(0, n_tiles // TILE_GROUP, tiles3, 0)
        for r in range(4):
            nat, qtr = pl.ds(r, S // 4, stride=4), pl.ds(r * (S // 4), S // 4)
            m_a, m_b = m_run[nat, :], m3[qtr, :]
            m = jnp.maximum(m_a, m_b)
            w_a, w_b = jnp.exp(m_a - m), jnp.exp(m_b - m)
            l = w_a * l_run[nat, :] + w_b * l3[qtr, :]
            o_ref[nat, :] = (w_a * o_ref[nat, :] + w_b * acc3[qtr, :]) / l
            lse_ref[nat, :] = m + jnp.log(l)

    hspec = pl.BlockSpec((S, HEAD_DIM), lambda h: (0, h))
    padded, plain = pltpu.VMEM((S + 2 * KV_PAD, HEAD_DIM), F32), pltpu.VMEM((S, HEAD_DIM), F32)
    return pl.pallas_call(
        body, grid=(H,), name="attn_fwd",
        in_specs=_head_specs(S, (0, 1, 2), H) + [
            pl.BlockSpec((1, len(DILATIONS), Q_TILE, K_TILE), lambda h: (h, 0, 0, 0))],
        out_specs=[hspec, hspec],
        out_shape=[SDS((S, H * HEAD_DIM), F32), SDS((S, H * HEAD_DIM), F32)],
        scratch_shapes=[padded, padded] + [plain] * 6,
        compiler_params=_cp(1))(proj, proj, proj, bias)


def _attn_bwd(proj, out, lse, dmix, bias):
    S = proj.shape[0]
    H = proj.shape[1] // (N_GROUPS * HEAD_DIM)
    scale = HEAD_DIM ** -0.5
    assert S // DILATIONS[2] >= 2 * Q_TILE

    def body(q_ref, k_ref, v_ref, o_ref, lse_ref, do_ref, b_ref, dq_ref, dk_ref, dv_ref,
             kp, vp, dkp, dvp, dsum, q4, do4, lse4, dsum4):
        _fill_padded(kp, k_ref, S)
        _fill_padded(vp, v_ref, S)
        dkp[...] = jnp.zeros_like(dkp)
        dvp[...] = jnp.zeros_like(dvp)
        dq_ref[...] = jnp.zeros_like(dq_ref)
        dsum[...] = jnp.broadcast_to(jnp.sum(do_ref[...] * o_ref[...], axis=-1, keepdims=True), dsum.shape)

        def run(n_tiles, tile_rows, p, L, q_src, do_src, lse_src, dsum_src, dq_dst, dq_adds):
            def tiles(t, carry):
                rows = [tile_rows(t + u * (n_tiles // TILE_GROUP)) for u in range(TILE_GROUP)]
                qs, ks, vs, dos, lses, dsums, dk_old, dv_old, edge = _stacked(rows, (
                    lambda a, qr, kr: q_src[qr, :].astype(BF16), lambda a, qr, kr: kp[kr, :].astype(BF16),
                    lambda a, qr, kr: vp[kr, :].astype(BF16), lambda a, qr, kr: do_src[qr, :].astype(BF16),
                    lambda a, qr, kr: lse_src[qr, :], lambda a, qr, kr: dsum_src[qr, :],
                    lambda a, qr, kr: dkp[kr, :], lambda a, qr, kr: dvp[kr, :], lambda a, qr, kr: _edge_mask(a, L)))
                s = _bdot(qs, ks, _BNT) * scale + b_ref[0, p][None] + edge
                pr = jnp.exp(s - _lanes(lses, K_TILE))
                ds = (pr * (_bdot(dos, vs, _BNT) - _lanes(dsums, K_TILE)) * scale).astype(BF16)
                dq_new = _bdot(ds, ks, _BNN)
                if dq_adds:
                    dq_new = dq_new + jnp.stack([dq_dst[qr, :] for _, qr, _ in rows])
                dk_new = dk_old + _bdot(ds, qs, _BTN)
                dv_new = dv_old + _bdot(pr.astype(BF16), dos, _BTN)
                for u, (_, qr, kr) in enumerate(rows):
                    dq_dst[qr, :] = dq_new[u]
                    dkp[kr, :] = dk_new[u]
                    dvp[kr, :] = dv_new[u]
                return carry

            lax.fori_loop(0, n_tiles // TILE_GROUP, tiles, 0)

        for p, d in enumerate(DILATIONS[:2]):
            L, per_class, n_tiles = _attn_tiles(S, d)
            run(n_tiles, functools.partial(_tile_rows, d=d, per_class=per_class), p, L,
                q_ref, do_ref, lse_ref, dsum, dq_ref, True)
        dk_ref[...] = dkp[pl.ds(KV_PAD, S), :]
        dv_ref[...] = dvp[pl.ds(KV_PAD, S), :]

        for dst, src in ((q4, q_ref), (do4, do_ref), (lse4, lse_ref), (dsum4, dsum)):
            _to_quarters(dst, src, S)
        _to_quarters(kp, k_ref, S, KV_PAD)
        _to_quarters(vp, v_ref, S, KV_PAD)
        dkp[...] = jnp.zeros_like(dkp)
        dvp[...] = jnp.zeros_like(dvp)
        dq3 = dsum
        run(_attn_tiles(S, DILATIONS[2])[2], functools.partial(_quarter_tile_rows, S=S), 2, S // DILATIONS[2],
            q4, do4, lse4, dsum4, dq3, False)
        for r in range(4):
            nat, qtr = pl.ds(r, S // 4, stride=4), pl.ds(r * (S // 4), S // 4)
            pad_qtr = pl.ds(KV_PAD + r * (S // 4), S // 4)
            dq_ref[nat, :] = dq_ref[nat, :] + dq3[qtr, :]
            dk_ref[nat, :] = dk_ref[nat, :] + dkp[pad_qtr, :]
            dv_ref[nat, :] = dv_ref[nat, :] + dvp[pad_qtr, :]

    hspec = pl.BlockSpec((S, HEAD_DIM), lambda h: (0, h))
    once =pl.BlockSpec((S, HEAD_DIM), lambda h: (0, h), pipeline_mode=pl.Buffered(1))
    padded, plain = pltpu.VMEM((S + 2 * KV_PAD, HEAD_DIM), F32), pltpu.VMEM((S, HEAD_DIM), F32)
    return pl.pallas_call(
        body, grid=(H,), name="attn_bwd",
        in_specs=_head_specs(S, (0, 1, 2), H) + [once, once, once,
                                                  pl.BlockSpec((1, len(DILATIONS), Q_TILE, K_TILE), lambda h: (h, 0, 0, 0))],
        out_specs=[hspec, hspec, hspec],
        out_shape=[SDS((S, H * HEAD_DIM), F32)] * 3,
        scratch_shapes=[padded] * 4 + [plain] * 5,
        compiler_params=_cp(1))(proj, proj, proj, out, lse, dmix, bias)


def _ret_consts(lg, forward):
    C = RET_CHUNK
    i = lax.broadcasted_iota(jnp.int32, (C, C), 0)
    j = lax.broadcasted_iota(jnp.int32, (C, C), 1)
    rel = (i - j) if forward else (j - i)
    inside = (rel >= 0) if forward else (rel > 0)
    relf = jnp.maximum(rel, 0).astype(F32)
    mask = jnp.where(inside, jnp.exp(lg * relf), 0.0)
    idx = lax.broadcasted_iota(jnp.int32, (C, 1), 0).astype(F32)
    q_exp = (idx + 1.0) if forward else (C - idx)
    k_exp = (C - 1.0 - idx) if forward else idx
    return mask, relf, jnp.exp(lg * q_exp), q_exp, jnp.exp(lg * k_exp), k_exp, jnp.exp(lg * C)


def _log_decay(dec_ref, h):
    return -jnp.exp(jnp.full((1, 1), dec_ref[0, h], F32))


FFN_BLOCK = 704
CHUNK_BATCH = 8


def _batch_rows(b):
    n = CHUNK_BATCH * RET_CHUNK
    return pl.ds(pl.multiple_of(b * n, n), n)


def _batch_chunks(b):
    return pl.ds(pl.multiple_of(b * CHUNK_BATCH, CHUNK_BATCH), CHUNK_BATCH)


def _chunks3(x):
    return x.reshape(CHUNK_BATCH, RET_CHUNK, HEAD_DIM)


def _ret_scan(buf, c_decs, nc, reverse):
    def step(n, carry):
        new = []
        for way, r in enumerate(carry):
            c = n if (way == 0) != reverse else nc - 1 - n
            term = buf[way, c]
            buf[way, c] = r
            new.append(r * c_decs[way] + term)
        return tuple(new)

    lax.fori_loop(0, nc, step, (jnp.zeros((HEAD_DIM, HEAD_DIM), F32),) * 2)


def _ret_fwd(proj, dec_f, dec_b, w_norm):
    S = proj.shape[0]
    H = proj.shape[1] // (N_GROUPS * HEAD_DIM)
    nc = S // RET_CHUNK
    scale = HEAD_DIM ** -0.5

    def body(df_ref, db_ref, q_ref, k_ref, v_ref, g_ref, w_ref, y_ref, o_ref, states):
        h = pl.program_id(0)
        consts = [_ret_consts(_log_decay(dref, h), fw) for fw, dref in ((True, df_ref), (False, db_ref))]

        def kv_step(b, carry):
            rows, batch = _batch_rows(b), _batch_chunks(b)
            k3 = _chunks3(k_ref[rows, :])
            v3 = _chunks3(v_ref[rows, :]).astype(BF16)
            for way in range(2):
                states[way, batch] = _bdot((k3 * consts[way][4]).astype(BF16), v3, _BTN)
            return carry

        lax.fori_loop(0, nc // CHUNK_BATCH, kv_step, 0)
        _ret_scan(states, [c[6] for c in consts], nc, False)

        def out_step(b, carry):
            rows, batch = _batch_rows(b), _batch_chunks(b)
            q3 = _chunks3(q_ref[rows, :] * scale)
            k3 = _chunks3(k_ref[rows, :]).astype(BF16)
            v3 = _chunks3(v_ref[rows, :]).astype(BF16)
            a0 = _bdot(q3.astype(BF16), k3, _BNT)
            o = None
            for way in range(2):
                mask, q_dec = consts[way][0], consts[way][2]
                part = _bdot((a0 * mask).astype(BF16), v3, _BNN) \
                    + _bdot((q3 * q_dec).astype(BF16), states[way, batch].astype(BF16), _BNN)
                o = part if o is None else o + part
            o_ref[rows, :] = o.reshape(CHUNK_BATCH * RET_CHUNK, HEAD_DIM)
            return carry

        lax.fori_loop(0, nc // CHUNK_BATCH, out_step, 0)
        o = o_ref[...]
        g = g_ref[...]
        y_ref[...] = o * _rms_scale(o) * w_ref[...] * (g * _sigmoid(g))

    hspec = pl.BlockSpec((S, HEAD_DIM), lambda h: (0, h))
    smem = pl.BlockSpec(memory_space=pltpu.SMEM)
    return pl.pallas_call(
        body, grid=(H,), name="ret_fwd",
        in_specs=[smem, smem] + _head_specs(S, (3, 4, 5, 6), H) + [pl.BlockSpec((1, HEAD_DIM), lambda h: (0, h))],
        out_specs=[hspec, hspec],
        out_shape=[SDS((S, H * HEAD_DIM), F32)] * 2,
        scratch_shapes=[pltpu.VMEM((2, nc, HEAD_DIM, HEAD_DIM), F32)],
        compiler_params=_cp(1))(dec_f, dec_b, proj, proj, proj, proj, w_norm)


def _ret_gate_bwd(proj, o_raw, dmix, w_norm, col0):
    S = proj.shape[0]
    H = proj.shape[1] // (N_GROUPS * HEAD_DIM)

    def body(g_ref, o_ref, dy_ref, w_ref, do_ref, dg_ref, dw_ref):
        o = o_ref[...]
        g = g_ref[...]
        dy = dy_ref[...]
        w = w_ref[...]
        rr = _rms_scale(o)
        normed = o * rr
        sg = _sigmoid(g)
        silu = g * sg
        dw_ref[0] = jnp.broadcast_to(jnp.sum(dy * normed * silu, axis=0, keepdims=True), (8, HEAD_DIM))
        dg_ref[...] = (dy * normed * w * (sg * (1.0 + g * (1.0 - sg)))).astype(BF16)
        dnormed = dy * w * silu
        do_ref[...] = rr * dnormed - o * (rr * rr * rr) * jnp.mean(dnormed * o, axis=-1, keepdims=True)

    hspec = pl.BlockSpec((S, HEAD_DIM), lambda h: (0, h))
    nh0 = col0 // HEAD_DIM
    return pl.pallas_call(
        body, grid=(H,), name="ret_gate_bwd",
        in_specs=_head_specs(S, (6,), H) + [hspec, pl.BlockSpec((S, HEAD_DIM), lambda h: (0, nh0 + h)),
                                            pl.BlockSpec((1, HEAD_DIM), lambda h: (0, h))],
        out_specs=[hspec, hspec, pl.BlockSpec((1, 8, HEAD_DIM), lambda h: (h, 0, 0))],
        out_shape=[SDS((S, H * HEAD_DIM), F32), SDS((S, H * HEAD_DIM), BF16), SDS((H, 8, HEAD_DIM), F32)],
        compiler_params=_cp(1))(proj, o_raw, dmix, w_norm)


def _ret_bwd(proj, d_out, dec_f, dec_b):
    S = proj.shape[0]
    H = proj.shape[1] // (N_GROUPS * HEAD_DIM)
    C = RET_CHUNK
    nc = S // C
    scale = HEAD_DIM ** -0.5

    def body(df_ref, db_ref, q_ref, k_ref, v_ref, do, dq_ref, dk_ref, dv_ref, small_ref, states, d_states):
        h = pl.program_id(0)
        lgs = [_log_decay(df_ref, h), _log_decay(db_ref, h)]
        consts = [_ret_consts(lg, fw) for lg, fw in zip(lgs, (True, False))]

        def prep_step(b, carry):
            rows, batch = _batch_rows(b), _batch_chunks(b)
            q3 = _chunks3(q_ref[rows, :] * scale)
            k3 = _chunks3(k_ref[rows, :])
            v3 = _chunks3(v_ref[rows, :]).astype(BF16)
            do3 = _chunks3(do[rows, :]).astype(BF16)
            for way in range(2):
                states[way, batch] = _bdot((k3 * consts[way][4]).astype(BF16), v3, _BTN)
                d_states[way, batch] = _bdot((q3 * consts[way][2]).astype(BF16), do3, _BTN)
            return carry

        lax.fori_loop(0, nc // CHUNK_BATCH, prep_step, 0)
        c_decs = [c[6] for c in consts]
        _ret_scan(states, c_decs, nc, False)
        _ret_scan(d_states, c_decs, nc, True)

        def main_step(b, dlams):
            rows, batch = _batch_rows(b), _batch_chunks(b)
            q3 = _chunks3(q_ref[rows, :] * scale)
            k3 = _chunks3(k_ref[rows, :])
            q3b, k3b = q3.astype(BF16), k3.astype(BF16)
            v3b = _chunks3(v_ref[rows, :]).astype(BF16)
            do3b = _chunks3(do[rows, :]).astype(BF16)
            a0 = _bdot(q3b, k3b, _BNT)
            pv = _bdot(do3b, v3b, _BNT)
            dq = dk = dv = None
            new_dlams = []
            for way in range(2):
                mask, relf, q_dec, q_exp, k_dec, k_exp, c_dec = consts[way]
                state, d_state = states[way, batch], d_states[way, batch]
                dp = pv * mask
                dpb = dp.astype(BF16)
                gq = _bdot(do3b, state.astype(BF16), _BNT)
                gk = _bdot(v3b, d_state.astype(BF16), _BNT)
                parts = (_bdot(dpb, k3b, _BNN) + q_dec * gq, _bdot(dpb, q3b, _BTN) + k_dec * gk,
                         _bdot((a0 * mask).astype(BF16), do3b, _BTN)
                         + _bdot((k3 * k_dec).astype(BF16), d_state.astype(BF16), _BNN))
                dq, dk, dv = parts if dq is None else (dq + parts[0], dk + parts[1], dv + parts[2])
                total = lambda x: jnp.sum(jnp.sum(x, axis=0), axis=0, keepdims=True)
                new_dlams.append(dlams[way] + total(relf * a0 * dp)
                                 + total(q_exp * q_dec * q3 * gq + k_exp * k_dec * k3 * gk)
                                 + (C * c_dec) * total(state * d_state))
            flat = lambda x: x.reshape(CHUNK_BATCH * C, HEAD_DIM)
            dq_ref[rows, :] = (flat(dq) * scale).astype(BF16)
            dk_ref[rows, :] = flat(dk).astype(BF16)
            dv_ref[rows, :] = flat(dv).astype(BF16)
            return tuple(new_dlams)

        dlams = lax.fori_loop(0, nc // CHUNK_BATCH, main_step, (jnp.zeros((1, HEAD_DIM), F32),) * 2)
        for row, (dlam, lg) in enumerate(zip(dlams, lgs)):
            small_ref[0, pl.ds(row, 1), :] = jnp.broadcast_to(jnp.sum(dlam, axis=-1, keepdims=True) * lg, (1, HEAD_DIM))
        small_ref[0, pl.ds(2, 6), :] = jnp.zeros((6, HEAD_DIM), F32)

    hspec = pl.BlockSpec((S, HEAD_DIM), lambda h: (0, h))
    smem = pl.BlockSpec(memory_space=pltpu.SMEM)
    return pl.pallas_call(
        body, grid=(H,), name="ret_bwd",
        in_specs=[smem, smem] + _head_specs(S, (3, 4, 5), H) + [hspec],
        out_specs=[hspec, hspec, hspec, pl.BlockSpec((1, 8, HEAD_DIM), lambda h: (h, 0, 0))],
        out_shape=[SDS((S, H * HEAD_DIM), BF16)] * 3 + [SDS((H, 8, HEAD_DIM), F32)],
        scratch_shapes=[pltpu.VMEM((2, nc, HEAD_DIM, HEAD_DIM), F32), pltpu.VMEM((2, nc, HEAD_DIM, HEAD_DIM), F32)],
        compiler_params=_cp(1))(dec_f, dec_b, proj, proj, proj, d_out)


def _ffn_bwd_act(dh2, wd, g, u):
    S, D = dh2.shape
    nblk, _, FB = g.shape
    tm = min(512, S)

    def body(dh_ref, wd_ref, g_ref, u_ref, dg_ref, du_ref):
        dact = _dot(dh_ref[...], wd_ref[...], _NT)
        gg = g_ref[0].astype(F32)
        sg = _sigmoid(gg)
        dg_ref[0] = (dact * u_ref[0].astype(F32) * (sg * (1.0 + gg * (1.0 - sg)))).astype(BF16)
        du_ref[0] = (dact * (gg * sg)).astype(BF16)

    blk = pl.BlockSpec((1, tm, FB), lambda j, i: (j, i, 0))
    return pl.pallas_call(
        body, grid=(nblk, S // tm), name="ffn_bwd_act",
        in_specs=[pl.BlockSpec((tm, D), lambda j, i: (i, 0)), pl.BlockSpec((FB, D), lambda j, i: (j, 0)), blk, blk],
        out_specs=[blk, blk], out_shape=[SDS((nblk, S, FB), BF16)] * 2,
        compiler_params=_cp(2))(dh2, wd, g, u)


def _ffn_bwd_in(dg, du, wg, wu, h1, dh2, w_norm):
    nblk, S, FB = dg.shape
    D = h1.shape[1]
    tm = min(RESIDENT_ROWS, S)
    blk = pl.BlockSpec((nblk, tm, FB), lambda i: (0, i, 0))
    row = pl.BlockSpec((tm, D), lambda i: (i, 0))
    vec = pl.BlockSpec((1, D), lambda i: (0, 0))

    def gate_body(dg_ref, wg_ref, part_ref):
        part_ref[...] = _blocked_matmul(dg_ref, wg_ref)

    part = pl.pallas_call(
        gate_body, grid=(S // tm,), name="ffn_bwd_in_gate", in_specs=[blk, _resident((nblk * FB, D))],
        out_specs=row, out_shape=SDS((S, D), F32), compiler_params=_cp(1))(dg, wg.reshape(nblk * FB, D))

    def body(du_ref, wu_ref, part_ref, h_ref, dh2_ref, wn_ref, dh_ref, dhb_ref, dw_ref):
        @pl.when(pl.program_id(0) == 0)
        def _():
            dw_ref[...] = jnp.zeros_like(dw_ref)

        dh, dw = _rms_bwd(part_ref[...] + _blocked_matmul(du_ref, wu_ref), h_ref[...], wn_ref[...])
        dh = dh2_ref[...] + dh
        dh_ref[...] = dh
        dhb_ref[...] = dh.astype(BF16)
        dw_ref[...] += dw

    return pl.pallas_call(
        body, grid=(S // tm,), name="ffn_bwd_in",
        in_specs=[blk, _resident((nblk * FB, D)), row, row, row, vec],
        out_specs=[row, row, vec], out_shape=[SDS((S, D), F32), SDS((S, D), BF16), SDS((1, D), F32)],
        compiler_params=_cp(1))(du, wu.reshape(nblk * FB, D), part, h1, dh2, w_norm)


def _dmix(dh1, w_out):
    S, D = dh1.shape
    tm = min(512, S)

    def body(dh_ref, w_ref, o_ref):
        o_ref[...] = _dot(dh_ref[...], w_ref[...], _NT)

    row = pl.BlockSpec((tm, D), lambda i: (i, 0))
    return pl.pallas_call(
        body, grid=(S // tm,), name="dmix", in_specs=[row, pl.BlockSpec((D, D), lambda i: (0, 0))],
        out_specs=row, out_shape=SDS((S, D), F32), compiler_params=_cp(1))(dh1, w_out)


def _in_bwd(dproj, w_blk, x, dh1, w_norm):
    S, D = x.shape
    nblk, _, NB = w_blk.shape
    tm = min(RESIDENT_ROWS, S)

    def body(dp_ref, w_ref, x_ref, dh1_ref, wn_ref, dx_ref, dw_ref):
        @pl.when(pl.program_id(0) == 0)
        def _():
            dw_ref[...] = jnp.zeros_like(dw_ref)

        dn = None
        for j in range(nblk):
            part = _dot(dp_ref[:, pl.ds(j * NB, NB)], w_ref[j], _NT)
            dn = part if dn is None else dn + part
        dh, dw = _rms_bwd(dn, x_ref[...], wn_ref[...])
        dx_ref[...] = dh1_ref[...] + dh
        dw_ref[...] += dw

    row = pl.BlockSpec((tm, D), lambda i: (i, 0))
    vec = pl.BlockSpec((1, D), lambda i: (0, 0))
    return pl.pallas_call(
        body, grid=(S // tm,), name="in_bwd",
        in_specs=[pl.BlockSpec((tm, nblk * NB), lambda i: (i, 0)),
                  pl.BlockSpec((nblk, D, NB), lambda i: (0, 0, 0), pipeline_mode=pl.Buffered(1)), row, row, vec],
        out_specs=[row, vec], out_shape=[SDS((S, D), F32), SDS((1, D), F32)],
        compiler_params=_cp(1))(dproj, w_blk, x, dh1, w_norm)


def _wgrad(a, b, a_spec, b_spec, o_spec, o_shape, grid, name):
    nk = grid[-1]

    def ld(ref):
        return ref[0] if len(ref.shape) == 3 else ref[...]

    def body(a_ref, b_ref, o_ref, acc):
        k = pl.program_id(len(grid) - 1)

        @pl.when(k == 0)
        def _():
            acc[...] = jnp.zeros_like(acc)

        acc[...] += _dot(ld(a_ref), ld(b_ref), _TN)

        @pl.when(k == nk - 1)
        def _():
            if len(o_ref.shape) == 3:
                o_ref[0] = acc[...].astype(o_ref.dtype)
            else:
                o_ref[...] = acc[...].astype(o_ref.dtype)

    return pl.pallas_call(
        body, grid=grid, name=name, in_specs=[a_spec, b_spec], out_specs=o_spec, out_shape=SDS(o_shape, BF16),
        scratch_shapes=[pltpu.VMEM(o_spec.block_shape[-2:], F32)], compiler_params=_cp(len(grid)))(a, b)


def _peer(k):
    x, y, c = lax.axis_index("x"), lax.axis_index("y"), lax.axis_index("c")
    px = 1 - x if k & 4 else x
    py = 1 - y if k & 2 else y
    pc = 1 - c if k & 1 else c
    return (px, py, pc), 4 * px + 2 * py + pc


def _exchange_copies(srcs, lands, send_sems, recv_sems, which, gather):
    _, me = _peer(0)
    pairs = []
    for pos, a in enumerate(which):
        for k in range(1, N_DEV):
            dev, idx = _peer(k)
            sem = pos * (N_DEV - 1) + k - 1
            src = srcs[a] if gather else srcs[a].at[idx]
            mk = functools.partial(pltpu.make_async_remote_copy, src_ref=src, send_sem=send_sems.at[sem],
                                   recv_sem=recv_sems.at[sem], device_id=dev, device_id_type=MESH)
            pairs.append((mk(dst_ref=lands[a].at[me]), mk(dst_ref=lands[a].at[idx])))
    return pairs


def _sequencer_kernel(name, collective_id, n_remote, n_local):
    return pl.kernel(mesh=plsc.ScalarSubcoreMesh(axis_name="sequencer", num_cores=1), name=name,
                     scratch_types=(pltpu.SemaphoreType.DMA((n_remote,)), pltpu.SemaphoreType.DMA((n_remote,)),
                                    pltpu.SemaphoreType.DMA((n_local,))),
                     compiler_params=pltpu.CompilerParams(collective_id=collective_id))


def _handshake(ks):
    barrier = pltpu.get_barrier_semaphore()
    for k in ks:
        pl.semaphore_signal(barrier, inc=1, device_id=_peer(k)[0], device_id_type=MESH)
    pl.semaphore_wait(barrier, len(ks))


def _sequencer_scatter(arrays, name, collective_id):
    n = len(arrays)
    hbm = pltpu.MemorySpace.HBM
    srcs = [jax.new_ref(a, memory_space=hbm) for a in arrays]
    lands = [jax.empty_ref(SDS(a.shape, a.dtype), memory_space=hbm) for a in arrays]

    @_sequencer_kernel(name, collective_id, n * (N_DEV - 1), n)
    def launch(send_sems, recv_sems, local_sems):
        _handshake(range(1, N_DEV))
        _, me = _peer(0)
        local = [pltpu.make_async_copy(srcs[a].at[me], lands[a].at[me], local_sems.at[a]) for a in range(n)]
        pairs = _exchange_copies(srcs, lands, send_sems, recv_sems, range(n), False)
        for out, _ in pairs:
            out.start()
        for cp in local:
            cp.start()
        for out, arrival in pairs:
            out.wait_send()
            arrival.wait_recv()
        for cp in local:
            cp.wait()

    launch()
    return [r[...] for r in lands]


SIBLING = 1
OTHER_CHIPS = (2, 4, 6)


def _sequencer_gather(arrays, name, collective_id):
    n = len(arrays)
    hbm = pltpu.MemorySpace.HBM
    srcs = [jax.new_ref(a, memory_space=hbm) for a in arrays]
    lands = [jax.empty_ref(SDS((N_DEV,) + a.shape, a.dtype), memory_space=hbm) for a in arrays]

    @_sequencer_kernel(name, collective_id, n * (N_DEV - 1), n)
    def launch(send_sems, recv_sems, local_sems):
        _handshake((SIBLING,) + OTHER_CHIPS)
        _, me = _peer(0)
        sibling, _ = _peer(SIBLING)

        def copy(a, k, src, block, to):
            sem = a * (N_DEV - 1) + k - 1
            return pltpu.make_async_remote_copy(src_ref=src, dst_ref=lands[a].at[block], send_sem=send_sems.at[sem],
                                                recv_sem=recv_sems.at[sem], device_id=to, device_id_type=MESH)

        local = [pltpu.make_async_copy(srcs[a], lands[a].at[me], local_sems.at[a]) for a in range(n)]
        first = [copy(a, k, srcs[a], me, _peer(k)[0]) for a in range(n) for k in OTHER_CHIPS + (SIBLING,)]
        for cp in first + local:
            cp.start()
        passed = []
        for a in range(n):
            for k in OTHER_CHIPS:
                _, block = _peer(k)
                copy(a, k, srcs[a], block, sibling).wait_recv()
                passed.append(copy(a, k ^ SIBLING, lands[a].at[block], block, sibling))
                passed[-1].start()
        for a in range(n):
            for k in (SIBLING,) + tuple(k ^ SIBLING for k in OTHER_CHIPS):
                copy(a, k, srcs[a], _peer(k)[1], sibling).wait_recv()
        for cp in first + passed:
            cp.wait_send()
        for cp in local:
            cp.wait()

    launch()
    return [r[...] for r in lands]


SMALL_ROWS = 64


def _small_step(part, w, m, v):
    def body(p_ref, w_ref, m_ref, v_ref, g_ref, d_ref, nm_ref, nv_ref, gath, send_sems, recv_sems):
        _, me = _peer(0)
        gath[me] = p_ref[...]
        copies = []
        for k in range(1, N_DEV):
            dev, idx = _peer(k)
            out = pltpu.make_async_remote_copy(src_ref=p_ref, dst_ref=gath.at[me], send_sem=send_sems.at[k - 1],
                                               recv_sem=recv_sems.at[k - 1], device_id=dev, device_id_type=MESH)
            out.start()
            arrival = pltpu.make_async_remote_copy(src_ref=p_ref, dst_ref=gath.at[idx], send_sem=send_sems.at[k - 1],
                                                   recv_sem=recv_sems.at[k - 1], device_id=dev, device_id_type=MESH)
            copies.append((out, arrival))
        for out, arrival in copies:
            out.wait_send()
            arrival.wait_recv()
        g = gath[0]
        for p in range(1, N_DEV):
            g = g + gath[p]
        g_ref[...] = g
        d_ref[...], nm_ref[...], nv_ref[...] = _adamw(w_ref[...], g, m_ref[...], v_ref[...])

    vm = pl.BlockSpec(memory_space=pltpu.VMEM)
    return pl.pallas_call(
        body, name="small_step", in_specs=[vm] * 4, out_specs=[vm] * 4,
        out_shape=[SDS((SMALL_ROWS, 128), F32)] * 4,
        scratch_shapes=[pltpu.VMEM((N_DEV, SMALL_ROWS, 128), F32), pltpu.SemaphoreType.DMA((N_DEV - 1,)),
                        pltpu.SemaphoreType.DMA((N_DEV - 1,))])(part, w, m, v)


def _adamw(w, g, m, v):
    m = ADAM_B1 * m + (1.0 - ADAM_B1) * g
    v = ADAM_B2 * v + (1.0 - ADAM_B2) * (g * g)
    m_hat = m / (1.0 - ADAM_B1 ** ADAM_STEP)
    v_hat = v / (1.0 - ADAM_B2 ** ADAM_STEP)
    delta = -ADAM_LR * (m_hat / (jnp.sqrt(v_hat) + ADAM_EPS) + ADAM_WD * w)
    return delta, m, v


def _adamw_block(parts, w, m, v, name):
    R, C = w.shape
    n_parts = len(parts)
    Rp = R // n_parts
    tr = next(t for t in (256, 128, 64, 32, 16, 8) if Rp % t == 0 and t * C <= 256 * 1024)
    per_part = Rp // tr

    def body(*refs):
        p_refs = refs[:n_parts]
        w_ref, m_ref, v_ref, g_ref, d_ref, nm_ref, nv_ref = refs[n_parts:]
        for k, p_ref in enumerate(p_refs):
            @pl.when(pl.program_id(0) // per_part == k)
            def _(p_ref=p_ref):
                g = p_ref[0].astype(F32)
                for p in range(1, N_DEV):
                    g = g + p_ref[p].astype(F32)
                g_ref[...] = g
                d_ref[...], nm_ref[...], nv_ref[...] = _adamw(w_ref[...], g, m_ref[...], v_ref[...])

    row = pl.BlockSpec((tr, C), lambda i: (i, 0))
    part_specs = [pl.BlockSpec((N_DEV, tr, C), functools.partial(
        lambda i, k: (0, jnp.clip(i - k * per_part, 0, per_part - 1), 0), k=k)) for k in range(n_parts)]
    return pl.pallas_call(
        body, grid=(R // tr,), name=name, in_specs=part_specs + [row, row, row],
        out_specs=[row] * 4, out_shape=[SDS((R, C), F32)] * 4, compiler_params=_cp(1))(*parts, w, m, v)


def _pack_small(mix, ffn, fin, retw, dec_f, dec_b, loss):
    flat = jnp.concatenate([mix.reshape(-1), ffn.reshape(-1), fin.reshape(-1), retw.reshape(-1), dec_f.reshape(-1),
                            dec_b.reshape(-1), loss.reshape(-1)])
    return jnp.pad(flat, (0, SMALL_ROWS * 128 - flat.shape[0])).reshape(SMALL_ROWS, 128)


def _unpack_small(packed, shapes):
    flat = packed.reshape(-1)
    out, at = [], 0
    for s in shapes:
        n = math.prod(s)
        out.append(flat[at:at + n].reshape(s))
        at += n
    return out


def kernel(x, norm_mix_w, w_in, ret_decay_fwd, ret_decay_bwd, ret_norm_w, w_out, norm_ffn_w, w_gate, w_up, w_down, norm_final_w, loss_target, m_norm_mix_w, m_w_in, m_ret_decay_fwd, m_ret_decay_bwd, m_ret_norm_w, m_w_out, m_norm_ffn_w, m_w_gate, m_w_up, m_w_down, m_norm_final_w, v_norm_mix_w, v_w_in, v_ret_decay_fwd, v_ret_decay_bwd, v_ret_norm_w, v_w_out, v_norm_ffn_w, v_w_gate, v_w_up, v_w_down, v_norm_final_w):
    x2 = x[0]
    tgt = loss_target[0]
    S, D = x2.shape
    H = ret_norm_w.shape[1] // HEAD_DIM
    DA = H * HEAD_DIM
    fin_w = norm_final_w.reshape(1, D)
    big = (w_in[0], w_out[0], w_gate[0].T, w_up[0].T, w_down[0])

    big_b = [w.astype(BF16) for w in big]
    wi, = _sequencer_gather(big_b[:1], "gather_in", 0)
    wo, wg, wu = _sequencer_gather(big_b[1:4], "gather_mid", 1)
    wd, = _sequencer_gather(big_b[4:], "gather_down", 5)
    NB = wi.shape[2]

    proj, n1 = _proj_fwd(x2, norm_mix_w, wi)
    bias = _attn_bias()[:H]
    attn, lse = _attn_fwd(proj, bias)
    ret, o_raw = _ret_fwd(proj, ret_decay_fwd, ret_decay_bwd, ret_norm_w)
    wo_full = wo.reshape(D, D)
    d_ff = N_DEV * wd.shape[1]
    FB = FFN_BLOCK if d_ff % FFN_BLOCK == 0 else wd.shape[1]
    n_fb = d_ff // FB
    wg, wu = wg.reshape(n_fb, FB, D), wu.reshape(n_fb, FB, D)
    wd_full = wd.reshape(d_ff, D)
    h1, mixed, n2 = _out_fwd(x2, attn, ret, wo_full, norm_ffn_w)
    gate, up, act = _ffn_up(n2, wg, wu)
    dh2, dh2_b, loss_parts, g_fin = _ffn_down_loss(act, wd_full, h1, tgt, fin_w)

    dgate, dup = _ffn_bwd_act(dh2_b, wd_full, gate, up)
    tn = min(1024, D)
    ffn_specs = (pl.BlockSpec((1, S, FB), lambda j, n, k: (j, 0, 0)), pl.BlockSpec((S, tn), lambda j, n, k: (0, n)),
                 pl.BlockSpec((1, FB, tn), lambda j, n, k: (j, 0, n)), (n_fb, FB, D), (n_fb, D // tn, 1))
    per_dev = (N_DEV, d_ff // N_DEV, D)
    g_wd = _wgrad(act, dh2_b, *ffn_specs, "wgrad_down").reshape(per_dev)
    g_wg = _wgrad(dgate, n2, *ffn_specs, "wgrad_gate").reshape(per_dev)
    g_wu = _wgrad(dup, n2, *ffn_specs, "wgrad_up").reshape(per_dev)
    parts_f = _sequencer_scatter([g_wg, g_wu, g_wd], "scatter_ffn", 2)
    dh1, dh1_b, g_ffn = _ffn_bwd_in(dgate, dup, wg, wu, h1, dh2, norm_ffn_w)
    dmix = _dmix(dh1_b, wo_full)
    tmw = min(512, D)
    tk = min(2048, S)
    g_wo = _wgrad(mixed, dh1_b, pl.BlockSpec((tk, tmw), lambda m, k: (k, m)), pl.BlockSpec((tk, D), lambda m, k: (k, 0)),
                  pl.BlockSpec((tmw, D), lambda m, k: (m, 0)), (D, D), (D // tmw, S // tk), "wgrad_out")
    parts_o = _sequencer_scatter([g_wo.reshape(N_DEV, D // N_DEV, D)], "scatter_out", 3)
    d_ret, dg_r, small_w = _ret_gate_bwd(proj, o_raw, dmix, ret_norm_w, DA)
    dq_r, dk_r, dv_r, small = _ret_bwd(proj, d_ret, ret_decay_fwd, ret_decay_bwd)
    dq_a, dk_a, dv_a = _attn_bwd(proj, attn, lse, dmix, bias)
    dproj = jnp.concatenate([t.astype(BF16) for t in (dq_a, dk_a, dv_a, dq_r, dk_r, dv_r, dg_r)], axis=1)
    half = D // tmw // 2
    parts_i = []
    for part, (name, cid) in enumerate((("in_lo", 4), ("in_hi", 6))):
        g_wi = _wgrad(n1, dproj, pl.BlockSpec((S, tmw), functools.partial(lambda j, m, k, off: (0, m + off), off=part * half)),
                      pl.BlockSpec((S, NB), lambda j, m, k: (0, j)), pl.BlockSpec((1, tmw, NB), lambda j, m, k: (j, m, 0)),
                      (N_DEV, D // 2, NB), (N_DEV, half, 1), "wgrad_" + name)
        parts_i += _sequencer_scatter([g_wi], "scatter_" + name, cid)
    grad_x, g_mix = _in_bwd(dproj, wi, x2, dh1, norm_mix_w)

    big_m = (m_w_in[0], m_w_out[0], m_w_gate[0].T, m_w_up[0].T, m_w_down[0])
    big_v = (v_w_in[0], v_w_out[0], v_w_gate[0].T, v_w_up[0].T, v_w_down[0])
    names = ("adamw_in", "adamw_out", "adamw_gate", "adamw_up", "adamw_down")
    upd = [None] * 5
    for a, p in zip((2, 3, 4, 1, 0), [[t] for t in parts_f + parts_o] + [parts_i]):
        upd[a] = _adamw_block(p, big[a], big_m[a], big_v[a], names[a])

    g_dec_f = small[:, 0, 0].reshape(1, H)
    g_dec_b = small[:, 1, 0].reshape(1, H)
    g_retw = small_w[:, 0, :].reshape(1, DA)
    loss_local = jnp.sum(loss_parts[::8, 0])
    zero = jnp.zeros((1,), F32)
    part = _pack_small(g_mix, g_ffn, g_fin, g_retw, g_dec_f, g_dec_b, loss_local)
    sw = _pack_small(norm_mix_w, norm_ffn_w, norm_final_w, ret_norm_w, ret_decay_fwd, ret_decay_bwd, zero)
    sm = _pack_small(m_norm_mix_w, m_norm_ffn_w, m_norm_final_w, m_ret_norm_w, m_ret_decay_fwd, m_ret_decay_bwd, zero)
    sv = _pack_small(v_norm_mix_w, v_norm_ffn_w, v_norm_final_w, v_ret_norm_w, v_ret_decay_fwd, v_ret_decay_bwd, zero)
    shapes = [(1, D), (1, D), (D,), (1, DA), (1, H), (1, H), ()]
    sg, sd, snm, snv = [_unpack_small(t, shapes) for t in _small_step(part, sw, sm, sv)]
    loss = sg[6]

    def ordered(small_set, k):
        b = [(u[k].T if a in (2, 3) else u[k])[None] for a, u in enumerate(upd)]
        return [small_set[0], b[0], small_set[4], small_set[5], small_set[3], b[1], small_set[1], b[2], b[3], b[4],
                small_set[2]]

    return (loss, grad_x[None], *ordered(sg, 0), *ordered(sd, 1), *ordered(snm, 2), *ordered(snv, 3))
```

```python
import functools
import math

import numpy as np
import jax
import jax.numpy as jnp
from jax import lax
from jax.experimental import pallas as pl
from jax.experimental.pallas import tpu as pltpu
from jax.experimental.pallas import tpu_sc as plsc

F32 = jnp.float32
BF16 = jnp.bfloat16
SDS = jax.ShapeDtypeStruct

HEAD_DIM = 128
EPS = 1e-6
RET_CHUNK = 128
DILATIONS = (1, 4, 16)
BAND = 64
Q_TILE = 128
K_TILE = Q_TILE + 2 * BAND
KV_PAD = BAND * 4
TILE_GROUP = 8
NEG = -1e30
N_DEV = 8
N_GROUPS = 7
ADAM_LR, ADAM_B1, ADAM_B2, ADAM_EPS, ADAM_WD, ADAM_STEP = 0.001, 0.9, 0.999, 1e-08, 0.01, 10
VMEM_LIMIT = 56 * 1024 * 1024
MESH = pl.DeviceIdType.MESH
ANY = pl.BlockSpec(memory_space=pl.ANY)


def _cp(n_grid):
    return pltpu.CompilerParams(dimension_semantics=("arbitrary",) * n_grid, vmem_limit_bytes=VMEM_LIMIT)


def _sigmoid(x):
    return 1.0 / (1.0 + jnp.exp(-x))


def _rms_scale(h):
    return lax.rsqrt(jnp.mean(h * h, axis=-1, keepdims=True) + EPS)


def _rms_bwd(dn, h, w):
    r = _rms_scale(h)
    gw = dn * w
    dh = r * gw - h * (r * r * r) * jnp.mean(gw * h, axis=-1, keepdims=True)
    return dh, jnp.sum(dn * h * r, axis=0, keepdims=True)


def _dot(a, b, dims):
    return lax.dot_general(a.astype(BF16), b.astype(BF16), (dims, ((), ())), preferred_element_type=F32)


_NN = ((1,), (0,))
_NT = ((1,), (1,))
_TN = ((0,), (0,))


RESIDENT_ROWS = 256


def _resident(shape):
    return pl.BlockSpec(shape, lambda i: (0, 0), pipeline_mode=pl.Buffered(1))


def _blocked_matmul(a_ref, w_ref):
    nblk, _, fb = a_ref.shape
    out = None
    for j in range(nblk):
        part = jnp.dot(a_ref[j], w_ref[pl.ds(j * fb, fb), :], preferred_element_type=F32)
        out = part if out is None else out + part
    return out


def _proj_fwd(x, w_norm, w_blk):
    S, D = x.shape
    nblk, _, NB = w_blk.shape
    tm = min(1024, S)

    def body(x_ref, wn_ref, w_ref, proj_ref, n_ref, n_scr):
        @pl.when(pl.program_id(1) == 0)
        def _():
            xf = x_ref[...]
            nb = (xf * _rms_scale(xf) * wn_ref[...]).astype(BF16)
            n_scr[...] = nb
            n_ref[...] = nb
        proj_ref[...] = jnp.dot(n_scr[...], w_ref[0], preferred_element_type=F32)

    return pl.pallas_call(
        body, grid=(S // tm, nblk), name="proj_fwd",
        in_specs=[pl.BlockSpec((tm, D), lambda i, j: (i, 0)), pl.BlockSpec((1, D), lambda i, j: (0, 0)),
                  pl.BlockSpec((1, D, NB), lambda i, j: (j, 0, 0))],
        out_specs=[pl.BlockSpec((tm, NB), lambda i, j: (i, j)), pl.BlockSpec((tm, D), lambda i, j: (i, 0))],
        out_shape=[SDS((S, nblk * NB), F32), SDS((S, D), BF16)],
        scratch_shapes=[pltpu.VMEM((tm, D), BF16)], compiler_params=_cp(2))(x, w_norm, w_blk)


def _out_fwd(x, attn, ret, w_out, w_norm):
    S, D = x.shape
    DA = attn.shape[1]
    tm = min(256, S)

    def body(x_ref, a_ref, r_ref, w_ref, wn_ref, h_ref, mix_ref, n_ref):
        a = a_ref[...].astype(BF16)
        r = r_ref[...].astype(BF16)
        mix_ref[:, :DA] = a
        mix_ref[:, DA:] = r
        h = x_ref[...] + jnp.dot(a, w_ref[:DA, :], preferred_element_type=F32) \
            + jnp.dot(r, w_ref[DA:, :], preferred_element_type=F32)
        h_ref[...] = h
        n_ref[...] = (h * _rms_scale(h) * wn_ref[...]).astype(BF16)

    row = lambda w: pl.BlockSpec((tm, w), lambda i: (i, 0))
    return pl.pallas_call(
        body, grid=(S // tm,), name="out_fwd",
        in_specs=[row(D), row(DA), row(D - DA), pl.BlockSpec((D, D), lambda i: (0, 0)),
                  pl.BlockSpec((1, D), lambda i: (0, 0))],
        out_specs=[row(D), row(D), row(D)],
        out_shape=[SDS((S, D), F32), SDS((S, D), BF16), SDS((S, D), BF16)],
        compiler_params=_cp(1))(x, attn, ret, w_out, w_norm)


def _ffn_up(n2, wg, wu):
    S, D = n2.shape
    nblk, FB, _ = wg.shape
    tm = min(512, S)

    def body(n_ref, wg_ref, wu_ref, g_ref, u_ref, a_ref):
        n = n_ref[...]
        g = _dot(n, wg_ref[0], _NT)
        u = _dot(n, wu_ref[0], _NT)
        g_ref[0] = g.astype(BF16)
        u_ref[0] = u.astype(BF16)
        a_ref[0] = (g * _sigmoid(g) * u).astype(BF16)

    wspec = pl.BlockSpec((1, FB, D), lambda j, i: (j, 0, 0))
    ospec = pl.BlockSpec((1, tm, FB), lambda j, i: (j, i, 0))
    return pl.pallas_call(
        body, grid=(nblk, S // tm), name="ffn_up",
        in_specs=[pl.BlockSpec((tm, D), lambda j, i: (i, 0)), wspec, wspec],
        out_specs=[ospec, ospec, ospec],
        out_shape=[SDS((nblk, S, FB), BF16)] * 3,
        compiler_params=_cp(2))(n2, wg, wu)


def _ffn_down_loss(act, wd, h1, target, w_norm):
    nblk, S, FB = act.shape
    D = h1.shape[1]
    tm = min(RESIDENT_ROWS, S)

    def body(a_ref, wd_ref, h_ref, t_ref, wn_ref, dh_ref, dhb_ref, loss_ref, dw_ref):
        @pl.when(pl.program_id(0) == 0)
        def _():
            dw_ref[...] = jnp.zeros_like(dw_ref)

        h = h_ref[...] + _blocked_matmul(a_ref, wd_ref)
        w = wn_ref[...]
        err = h * _rms_scale(h) * w - t_ref[...]
        loss_ref[...] = jnp.full(loss_ref.shape, 0.5 * jnp.sum(err * err) / D, F32)
        dh, dw = _rms_bwd(err * (1.0 / D), h, w)
        dh_ref[...] = dh
        dhb_ref[...] = dh.astype(BF16)
        dw_ref[...] += dw

    row = pl.BlockSpec((tm, D), lambda i: (i, 0))
    vec = pl.BlockSpec((1, D), lambda i: (0, 0))
    return pl.pallas_call(
        body, grid=(S // tm,), name="ffn_down_loss",
        in_specs=[pl.BlockSpec((nblk, tm, FB), lambda i: (0, i, 0)), _resident((nblk * FB, D)), row, row, vec],
        out_specs=[row, row, pl.BlockSpec((8, 128), lambda i: (i, 0)), vec],
        out_shape=[SDS((S, D), F32), SDS((S, D), BF16), SDS((S // tm * 8, 128), F32), SDS((1, D), F32)],
        compiler_params=_cp(1))(act, wd, h1, target, w_norm)


def _attn_bias():
    n_heads = 8
    slopes = np.exp2(-8.0 * np.arange(1, n_heads + 1, dtype=np.float32) / n_heads)
    dist = np.abs(np.arange(K_TILE)[None, :] - BAND - np.arange(Q_TILE)[:, None])
    out = np.empty((n_heads, len(DILATIONS), Q_TILE, K_TILE), np.float32)
    for h in range(n_heads):
        for p, d in enumerate(DILATIONS):
            out[h, p] = np.where(dist <= BAND, -slopes[h] * (d * dist).astype(np.float32), NEG)
    return jnp.asarray(out)


def _attn_tiles(S, d):
    L = S // d
    per_class = L // Q_TILE
    return L, per_class, d * per_class


def _tile_rows(t, d, per_class):
    r = t // per_class
    a = (t % per_class) * Q_TILE
    q_rows = pl.ds(r + d * a, Q_TILE, stride=d) if d > 1 else pl.ds(pl.multiple_of(a, Q_TILE), Q_TILE)
    k_rows = pl.ds(KV_PAD + r + d * (a - BAND), K_TILE, stride=d) if d > 1 else pl.ds(
        pl.multiple_of(KV_PAD + a - BAND, BAND), K_TILE)
    return a, q_rows, k_rows


def _to_quarters(dst, src, n, dst_off=0):
    for r in range(4):
        dst[pl.ds(dst_off + r * (n // 4), n // 4), :] = src[pl.ds(r, n // 4, stride=4), :]


def _quarter_tile_rows(t, S):
    L = S // 16
    per_class = L // Q_TILE
    blk, tt = t // (4 * per_class), t % (4 * per_class)
    r, a = tt // per_class, (tt % per_class) * Q_TILE
    q_rows = pl.ds(blk * (S // 4) + r + 4 * a, Q_TILE, stride=4)
    k_rows = pl.ds(KV_PAD + blk * (S // 4) + r + 4 * (a - BAND), K_TILE, stride=4)
    return a, q_rows, k_rows


def _lanes(x, width):
    return jnp.concatenate([x] * (width // HEAD_DIM), axis=-1)


_BNT = (((2,), (2,)), ((0,), (0,)))
_BNN = (((2,), (1,)), ((0,), (0,)))
_BTN = (((1,), (1,)), ((0,), (0,)))


def _bdot(a, b, dims):
    return lax.dot_general(a, b, dims, preferred_element_type=F32)


def _stacked(rows, loaders):
    return [jnp.stack([f(*r) for r in rows]) for f in loaders]


def _edge_mask(a, L):
    lk = lax.broadcasted_iota(jnp.int32, (1, K_TILE), 1) + (a - BAND)
    return jnp.where((lk >= 0) & (lk < L), 0.0, NEG).astype(F32)


def _fill_padded(dst, src, S):
    dst[pl.ds(0, KV_PAD), :] = jnp.zeros((KV_PAD, HEAD_DIM), F32)
    dst[pl.ds(KV_PAD + S, KV_PAD), :] = jnp.zeros((KV_PAD, HEAD_DIM), F32)
    dst[pl.ds(KV_PAD, S), :] = src[...]


def _head_specs(S, groups, n_heads):
    return [pl.BlockSpec((S, HEAD_DIM), functools.partial(lambda h, g: (0, g * n_heads + h), g=g)) for g in groups]


def _attn_fwd(proj, bias):
    S = proj.shape[0]
    H = proj.shape[1] // (N_GROUPS * HEAD_DIM)
    scale = HEAD_DIM ** -0.5

    def body(q_ref, k_ref, v_ref, b_ref, o_ref, lse_ref, kp, vp, m_run, l_run, q4, m3, l3, acc3):
        _fill_padded(kp, k_ref, S)
        _fill_padded(vp, v_ref, S)
        o_ref[...] = jnp.zeros_like(o_ref)
        m_run[...] = jnp.full(m_run.shape, NEG, F32)
        l_run[...] = jnp.zeros_like(l_run)
        for p, d in enumerate(DILATIONS[:2]):
            L, per_class, n_tiles = _attn_tiles(S, d)

            def tiles(t, carry, p=p, d=d, L=L, per_class=per_class, n_tiles=n_tiles):
                rows = [_tile_rows(t + u * (n_tiles // TILE_GROUP), d, per_class) for u in range(TILE_GROUP)]
                qs, ks, vs, m_old, l_old, o_old, edge = _stacked(rows, (
                    lambda a, qr, kr: q_ref[qr, :].astype(BF16), lambda a, qr, kr: kp[kr, :].astype(BF16),
                    lambda a, qr, kr: vp[kr, :].astype(BF16), lambda a, qr, kr: m_run[qr, :],
                    lambda a, qr, kr: l_run[qr, :], lambda a, qr, kr: o_ref[qr, :], lambda a, qr, kr: _edge_mask(a, L)))
                s = _bdot(qs, ks, _BNT) * scale + b_ref[0, p][None] + edge
                m_new = jnp.maximum(m_old, jnp.max(s, axis=-1, keepdims=True))
                pr = jnp.exp(s - _lanes(m_new, K_TILE)).astype(BF16)
                alpha = jnp.exp(m_old - m_new)
                l_new = alpha * l_old + _bdot(pr, jnp.ones((TILE_GROUP, K_TILE, HEAD_DIM), BF16), _BNN)
                o_new = alpha * o_old + _bdot(pr, vs, _BNN)
                for u, (_, qr, _) in enumerate(rows):
                    o_ref[qr, :] = o_new[u]
                    m_run[qr, :] = m_new[u]
                    l_run[qr, :] = l_new[u]
                return carry

            lax.fori_loop(0, n_tiles // TILE_GROUP, tiles, 0)

        _to_quarters(q4, q_ref, S)
        _to_quarters(kp, k_ref, S, KV_PAD)
        _to_quarters(vp, v_ref, S, KV_PAD)
        n_tiles = _attn_tiles(S, DILATIONS[2])[2]

        def tiles3(t, carry):
            rows = [_quarter_tile_rows(t + u * (n_tiles // TILE_GROUP), S) for u in range(TILE_GROUP)]
            qs, ks, vs, edge = _stacked(rows, (
                lambda a, qr, kr: q4[qr, :].astype(BF16), lambda a, qr, kr: kp[kr, :].astype(BF16),
                lambda a, qr, kr: vp[kr, :].astype(BF16), lambda a, qr, kr: _edge_mask(a, S // DILATIONS[2])))
            s = _bdot(qs, ks, _BNT) * scale + b_ref[0, 2][None] + edge
            m_new = jnp.broadcast_to(jnp.max(s, axis=-1, keepdims=True), (TILE_GROUP, Q_TILE, HEAD_DIM))
            pr = jnp.exp(s - _lanes(m_new, K_TILE)).astype(BF16)
            l_new = _bdot(pr, jnp.ones((TILE_GROUP, K_TILE, HEAD_DIM), BF16), _BNN)
            o_new = _bdot(pr, vs, _BNN)
            for u, (_, qr, _) in enumerate(rows):
                acc3[qr, :] = o_new[u]
                m3[qr, :] = m_new[u]
                l3[qr, :] = l_new[u]
            return carry

        lax.fori_loop(0, n_tiles // TILE_GROUP, tiles3, 0)
        for r in range(4):
            nat, qtr = pl.ds(r, S // 4, stride=4), pl.ds(r * (S // 4), S // 4)
            m_a, m_b = m_run[nat, :], m3[qtr, :]
            m = jnp.maximum(m_a, m_b)
            w_a, w_b = jnp.exp(m_a - m), jnp.exp(m_b - m)
            l = w_a * l_run[nat, :] + w_b * l3[qtr, :]
            o_ref[nat, :] = (w_a * o_ref[nat, :] + w_b * acc3[qtr, :]) / l
            lse_ref[nat, :] = m + jnp.log(l)

    hspec = pl.BlockSpec((S, HEAD_DIM), lambda h: (0, h))
    padded, plain = pltpu.VMEM((S + 2 * KV_PAD, HEAD_DIM), F32), pltpu.VMEM((S, HEAD_DIM), F32)
    return pl.pallas_call(
        body, grid=(H,), name="attn_fwd",
        in_specs=_head_specs(S, (0, 1, 2), H) + [
            pl.BlockSpec((1, len(DILATIONS), Q_TILE, K_TILE), lambda h: (h, 0, 0, 0))],
        out_specs=[hspec, hspec],
        out_shape=[SDS((S, H * HEAD_DIM), F32), SDS((S, H * HEAD_DIM), F32)],
        scratch_shapes=[padded, padded] + [plain] * 6,
        compiler_params=_cp(1))(proj, proj, proj, bias)


def _attn_bwd(proj, out, lse, dmix, bias):
    S = proj.shape[0]
    H = proj.shape[1] // (N_GROUPS * HEAD_DIM)
    scale = HEAD_DIM ** -0.5
    assert S // DILATIONS[2] >= 2 * Q_TILE

    def body(q_ref, k_ref, v_ref, o_ref, lse_ref, do_ref, b_ref, dq_ref, dk_ref, dv_ref,
             kp, vp, dkp, dvp, dsum, q4, do4, lse4, dsum4):
        _fill_padded(kp, k_ref, S)
        _fill_padded(vp, v_ref, S)
        dkp[...] = jnp.zeros_like(dkp)
        dvp[...] = jnp.zeros_like(dvp)
        dq_ref[...] = jnp.zeros_like(dq_ref)
        dsum[...] = jnp.broadcast_to(jnp.sum(do_ref[...] * o_ref[...], axis=-1, keepdims=True), dsum.shape)

        def run(n_tiles, tile_rows, p, L, q_src, do_src, lse_src, dsum_src, dq_dst, dq_adds):
            def tiles(t, carry):
                rows = [tile_rows(t + u * (n_tiles // TILE_GROUP)) for u in range(TILE_GROUP)]
                qs, ks, vs, dos, lses, dsums, dk_old, dv_old, edge = _stacked(rows, (
                    lambda a, qr, kr: q_src[qr, :].astype(BF16), lambda a, qr, kr: kp[kr, :].astype(BF16),
                    lambda a, qr, kr: vp[kr, :].astype(BF16), lambda a, qr, kr: do_src[qr, :].astype(BF16),
                    lambda a, qr, kr: lse_src[qr, :], lambda a, qr, kr: dsum_src[qr, :],
                    lambda a, qr, kr: dkp[kr, :], lambda a, qr, kr: dvp[kr, :], lambda a, qr, kr: _edge_mask(a, L)))
                s = _bdot(qs, ks, _BNT) * scale + b_ref[0, p][None] + edge
                pr = jnp.exp(s - _lanes(lses, K_TILE))
                ds = (pr * (_bdot(dos, vs, _BNT) - _lanes(dsums, K_TILE)) * scale).astype(BF16)
                dq_new = _bdot(ds, ks, _BNN)
                if dq_adds:
                    dq_new = dq_new + jnp.stack([dq_dst[qr, :] for _, qr, _ in rows])
                dk_new = dk_old + _bdot(ds, qs, _BTN)
                dv_new = dv_old + _bdot(pr.astype(BF16), dos, _BTN)
                for u, (_, qr, kr) in enumerate(rows):
                    dq_dst[qr, :] = dq_new[u]
                    dkp[kr, :] = dk_new[u]
                    dvp[kr, :] = dv_new[u]
                return carry

            lax.fori_loop(0, n_tiles // TILE_GROUP, tiles, 0)

        for p, d in enumerate(DILATIONS[:2]):
            L, per_class, n_tiles = _attn_tiles(S, d)
            run(n_tiles, functools.partial(_tile_rows, d=d, per_class=per_class), p, L,
                q_ref, do_ref, lse_ref, dsum, dq_ref, True)
        dk_ref[...] = dkp[pl.ds(KV_PAD, S), :]
        dv_ref[...] = dvp[pl.ds(KV_PAD, S), :]

        for dst, src in ((q4, q_ref), (do4, do_ref), (lse4, lse_ref), (dsum4, dsum)):
            _to_quarters(dst, src, S)
        _to_quarters(kp, k_ref, S, KV_PAD)
        _to_quarters(vp, v_ref, S, KV_PAD)
        dkp[...] = jnp.zeros_like(dkp)
        dvp[...] = jnp.zeros_like(dvp)
        dq3 = dsum
        run(_attn_tiles(S, DILATIONS[2])[2], functools.partial(_quarter_tile_rows, S=S), 2, S // DILATIONS[2],
            q4, do4, lse4, dsum4, dq3, False)
        for r in range(4):
            nat, qtr = pl.ds(r, S // 4, stride=4), pl.ds(r * (S // 4), S // 4)
            pad_qtr = pl.ds(KV_PAD + r * (S // 4), S // 4)
            dq_ref[nat, :] = dq_ref[nat, :] + dq3[qtr, :]
            dk_ref[nat, :] = dk_ref[nat, :] + dkp[pad_qtr, :]
            dv_ref[nat, :] = dv_ref[nat, :] + dvp[pad_qtr, :]

    hspec = pl.BlockSpec((S, HEAD_DIM), lambda h: (0, h))
    once =pl.BlockSpec((S, HEAD_DIM), lambda h: (0, h), pipeline_mode=pl.Buffered(1))
    padded, plain = pltpu.VMEM((S + 2 * KV_PAD, HEAD_DIM), F32), pltpu.VMEM((S, HEAD_DIM), F32)
    return pl.pallas_call(
        body, grid=(H,), name="attn_bwd",
        in_specs=_head_specs(S, (0, 1, 2), H) + [once, once, once,
                                                  pl.BlockSpec((1, len(DILATIONS), Q_TILE, K_TILE), lambda h: (h, 0, 0, 0))],
        out_specs=[hspec, hspec, hspec],
        out_shape=[SDS((S, H * HEAD_DIM), F32)] * 3,
        scratch_shapes=[padded] * 4 + [plain] * 5,
        compiler_params=_cp(1))(proj, proj, proj, out, lse, dmix, bias)


def _ret_consts(lg, forward):
    C = RET_CHUNK
    i = lax.broadcasted_iota(jnp.int32, (C, C), 0)
    j = lax.broadcasted_iota(jnp.int32, (C, C), 1)
    rel = (i - j) if forward else (j - i)
    inside = (rel >= 0) if forward else (rel > 0)
    relf = jnp.maximum(rel, 0).astype(F32)
    mask = jnp.where(inside, jnp.exp(lg * relf), 0.0)
    idx = lax.broadcasted_iota(jnp.int32, (C, 1), 0).astype(F32)
    q_exp = (idx + 1.0) if forward else (C - idx)
    k_exp = (C - 1.0 - idx) if forward else idx
    return mask, relf, jnp.exp(lg * q_exp), q_exp, jnp.exp(lg * k_exp), k_exp, jnp.exp(lg * C)


def _log_decay(dec_ref, h):
    return -jnp.exp(jnp.full((1, 1), dec_ref[0, h], F32))


FFN_BLOCK = 512
CHUNK_BATCH = 8


def _batch_rows(b):
    n = CHUNK_BATCH * RET_CHUNK
    return pl.ds(pl.multiple_of(b * n, n), n)


def _batch_chunks(b):
    return pl.ds(pl.multiple_of(b * CHUNK_BATCH, CHUNK_BATCH), CHUNK_BATCH)


def _chunks3(x):
    return x.reshape(CHUNK_BATCH, RET_CHUNK, HEAD_DIM)


def _ret_scan(buf, c_decs, nc, reverse):
    def step(n, carry):
        new = []
        for way, r in enumerate(carry):
            c = n if (way == 0) != reverse else nc - 1 - n
            term = buf[way, c]
            buf[way, c] = r
            new.append(r * c_decs[way] + term)
        return tuple(new)

    lax.fori_loop(0, nc, step, (jnp.zeros((HEAD_DIM, HEAD_DIM), F32),) * 2)


def _ret_fwd(proj, dec_f, dec_b, w_norm):
    S = proj.shape[0]
    H = proj.shape[1] // (N_GROUPS * HEAD_DIM)
    nc = S // RET_CHUNK
    scale = HEAD_DIM ** -0.5

    def body(df_ref, db_ref, q_ref, k_ref, v_ref, g_ref, w_ref, y_ref, o_ref, states):
        h = pl.program_id(0)
        consts = [_ret_consts(_log_decay(dref, h), fw) for fw, dref in ((True, df_ref), (False, db_ref))]

        def kv_step(b, carry):
            rows, batch = _batch_rows(b), _batch_chunks(b)
            k3 = _chunks3(k_ref[rows, :])
            v3 = _chunks3(v_ref[rows, :]).astype(BF16)
            for way in range(2):
                states[way, batch] = _bdot((k3 * consts[way][4]).astype(BF16), v3, _BTN)
            return carry

        lax.fori_loop(0, nc // CHUNK_BATCH, kv_step, 0)
        _ret_scan(states, [c[6] for c in consts], nc, False)

        def out_step(b, carry):
            rows, batch = _batch_rows(b), _batch_chunks(b)
            q3 = _chunks3(q_ref[rows, :] * scale)
            k3 = _chunks3(k_ref[rows, :]).astype(BF16)
            v3 = _chunks3(v_ref[rows, :]).astype(BF16)
            a0 = _bdot(q3.astype(BF16), k3, _BNT)
            o = None
            for way in range(2):
                mask, q_dec = consts[way][0], consts[way][2]
                part = _bdot((a0 * mask).astype(BF16), v3, _BNN) \
                    + _bdot((q3 * q_dec).astype(BF16), states[way, batch].astype(BF16), _BNN)
                o = part if o is None else o + part
            o_ref[rows, :] = o.reshape(CHUNK_BATCH * RET_CHUNK, HEAD_DIM)
            return carry

        lax.fori_loop(0, nc // CHUNK_BATCH, out_step, 0)
        o = o_ref[...]
        g = g_ref[...]
        y_ref[...] = o * _rms_scale(o) * w_ref[...] * (g * _sigmoid(g))

    hspec = pl.BlockSpec((S, HEAD_DIM), lambda h: (0, h))
    smem = pl.BlockSpec(memory_space=pltpu.SMEM)
    return pl.pallas_call(
        body, grid=(H,), name="ret_fwd",
        in_specs=[smem, smem] + _head_specs(S, (3, 4, 5, 6), H) + [pl.BlockSpec((1, HEAD_DIM), lambda h: (0, h))],
        out_specs=[hspec, hspec],
        out_shape=[SDS((S, H * HEAD_DIM), F32)] * 2,
        scratch_shapes=[pltpu.VMEM((2, nc, HEAD_DIM, HEAD_DIM), F32)],
        compiler_params=_cp(1))(dec_f, dec_b, proj, proj, proj, proj, w_norm)


def _ret_gate_bwd(proj, o_raw, dmix, w_norm, col0):
    S = proj.shape[0]
    H = proj.shape[1] // (N_GROUPS * HEAD_DIM)

    def body(g_ref, o_ref, dy_ref, w_ref, do_ref, dg_ref, dw_ref):
        o = o_ref[...]
        g = g_ref[...]
        dy = dy_ref[...]
        w = w_ref[...]
        rr = _rms_scale(o)
        normed = o * rr
        sg = _sigmoid(g)
        silu = g * sg
        dw_ref[0] = jnp.broadcast_to(jnp.sum(dy * normed * silu, axis=0, keepdims=True), (8, HEAD_DIM))
        dg_ref[...] = (dy * normed * w * (sg * (1.0 + g * (1.0 - sg)))).astype(BF16)
        dnormed = dy * w * silu
        do_ref[...] = rr * dnormed - o * (rr * rr * rr) * jnp.mean(dnormed * o, axis=-1, keepdims=True)

    hspec = pl.BlockSpec((S, HEAD_DIM), lambda h: (0, h))
    nh0 = col0 // HEAD_DIM
    return pl.pallas_call(
        body, grid=(H,), name="ret_gate_bwd",
        in_specs=_head_specs(S, (6,), H) + [hspec, pl.BlockSpec((S, HEAD_DIM), lambda h: (0, nh0 + h)),
                                            pl.BlockSpec((1, HEAD_DIM), lambda h: (0, h))],
        out_specs=[hspec, hspec, pl.BlockSpec((1, 8, HEAD_DIM), lambda h: (h, 0, 0))],
        out_shape=[SDS((S, H * HEAD_DIM), F32), SDS((S, H * HEAD_DIM), BF16), SDS((H, 8, HEAD_DIM), F32)],
        compiler_params=_cp(1))(proj, o_raw, dmix, w_norm)


def _ret_bwd(proj, d_out, dec_f, dec_b):
    S = proj.shape[0]
    H = proj.shape[1] // (N_GROUPS * HEAD_DIM)
    C = RET_CHUNK
    nc = S // C
    scale = HEAD_DIM ** -0.5

    def body(df_ref, db_ref, q_ref, k_ref, v_ref, do, dq_ref, dk_ref, dv_ref, small_ref, states, d_states):
        h = pl.program_id(0)
        lgs = [_log_decay(df_ref, h), _log_decay(db_ref, h)]
        consts = [_ret_consts(lg, fw) for lg, fw in zip(lgs, (True, False))]

        def prep_step(b, carry):
            rows, batch = _batch_rows(b), _batch_chunks(b)
            q3 = _chunks3(q_ref[rows, :] * scale)
            k3 = _chunks3(k_ref[rows, :])
            v3 = _chunks3(v_ref[rows, :]).astype(BF16)
            do3 = _chunks3(do[rows, :]).astype(BF16)
            for way in range(2):
                states[way, batch] = _bdot((k3 * consts[way][4]).astype(BF16), v3, _BTN)
                d_states[way, batch] = _bdot((q3 * consts[way][2]).astype(BF16), do3, _BTN)
            return carry

        lax.fori_loop(0, nc // CHUNK_BATCH, prep_step, 0)
        c_decs = [c[6] for c in consts]
        _ret_scan(states, c_decs, nc, False)
        _ret_scan(d_states, c_decs, nc, True)

        def main_step(b, dlams):
            rows, batch = _batch_rows(b), _batch_chunks(b)
            q3 = _chunks3(q_ref[rows, :] * scale)
            k3 = _chunks3(k_ref[rows, :])
            q3b, k3b = q3.astype(BF16), k3.astype(BF16)
            v3b = _chunks3(v_ref[rows, :]).astype(BF16)
            do3b = _chunks3(do[rows, :]).astype(BF16)
            a0 = _bdot(q3b, k3b, _BNT)
            pv = _bdot(do3b, v3b, _BNT)
            dq = dk = dv = None
            new_dlams = []
            for way in range(2):
                mask, relf, q_dec, q_exp, k_dec, k_exp, c_dec = consts[way]
                state, d_state = states[way, batch], d_states[way, batch]
                dp = pv * mask
                dpb = dp.astype(BF16)
                gq = _bdot(do3b, state.astype(BF16), _BNT)
                gk = _bdot(v3b, d_state.astype(BF16), _BNT)
                parts = (_bdot(dpb, k3b, _BNN) + q_dec * gq, _bdot(dpb, q3b, _BTN) + k_dec * gk,
                         _bdot((a0 * mask).astype(BF16), do3b, _BTN)
                         + _bdot((k3 * k_dec).astype(BF16), d_state.astype(BF16), _BNN))
                dq, dk, dv = parts if dq is None else (dq + parts[0], dk + parts[1], dv + parts[2])
                total = lambda x: jnp.sum(jnp.sum(x, axis=0), axis=0, keepdims=True)
                new_dlams.append(dlams[way] + total(relf * a0 * dp)
                                 + total(q_exp * q_dec * q3 * gq + k_exp * k_dec * k3 * gk)
                                 + (C * c_dec) * total(state * d_state))
            flat = lambda x: x.reshape(CHUNK_BATCH * C, HEAD_DIM)
            dq_ref[rows, :] = (flat(dq) * scale).astype(BF16)
            dk_ref[rows, :] = flat(dk).astype(BF16)
            dv_ref[rows, :] = flat(dv).astype(BF16)
            return tuple(new_dlams)

        dlams = lax.fori_loop(0, nc // CHUNK_BATCH, main_step, (jnp.zeros((1, HEAD_DIM), F32),) * 2)
        for row, (dlam, lg) in enumerate(zip(dlams, lgs)):
            small_ref[0, pl.ds(row, 1), :] = jnp.broadcast_to(jnp.sum(dlam, axis=-1, keepdims=True) * lg, (1, HEAD_DIM))
        small_ref[0, pl.ds(2, 6), :] = jnp.zeros((6, HEAD_DIM), F32)

    hspec = pl.BlockSpec((S, HEAD_DIM), lambda h: (0, h))
    smem = pl.BlockSpec(memory_space=pltpu.SMEM)
    return pl.pallas_call(
        body, grid=(H,), name="ret_bwd",
        in_specs=[smem, smem] + _head_specs(S, (3, 4, 5), H) + [hspec],
        out_specs=[hspec, hspec, hspec, pl.BlockSpec((1, 8, HEAD_DIM), lambda h: (h, 0, 0))],
        out_shape=[SDS((S, H * HEAD_DIM), BF16)] * 3 + [SDS((H, 8, HEAD_DIM), F32)],
        scratch_shapes=[pltpu.VMEM((2, nc, HEAD_DIM, HEAD_DIM), F32), pltpu.VMEM((2, nc, HEAD_DIM, HEAD_DIM), F32)],
        compiler_params=_cp(1))(dec_f, dec_b, proj, proj, proj, d_out)


def _ffn_bwd_act(dh2, wd, g, u):
    S, D = dh2.shape
    nblk, _, FB = g.shape
    tm = min(512, S)

    def body(dh_ref, wd_ref, g_ref, u_ref, dg_ref, du_ref):
        dact = _dot(dh_ref[...], wd_ref[...], _NT)
        gg = g_ref[0].astype(F32)
        sg = _sigmoid(gg)
        dg_ref[0] = (dact * u_ref[0].astype(F32) * (sg * (1.0 + gg * (1.0 - sg)))).astype(BF16)
        du_ref[0] = (dact * (gg * sg)).astype(BF16)

    blk = pl.BlockSpec((1, tm, FB), lambda j, i: (j, i, 0))
    return pl.pallas_call(
        body, grid=(nblk, S // tm), name="ffn_bwd_act",
        in_specs=[pl.BlockSpec((tm, D), lambda j, i: (i, 0)), pl.BlockSpec((FB, D), lambda j, i: (j, 0)), blk, blk],
        out_specs=[blk, blk], out_shape=[SDS((nblk, S, FB), BF16)] * 2,
        compiler_params=_cp(2))(dh2, wd, g, u)


def _ffn_bwd_in(dg, du, wg, wu, h1, dh2, w_norm):
    nblk, S, FB = dg.shape
    D = h1.shape[1]
    tm = min(RESIDENT_ROWS, S)
    blk = pl.BlockSpec((nblk, tm, FB), lambda i: (0, i, 0))
    row = pl.BlockSpec((tm, D), lambda i: (i, 0))
    vec = pl.BlockSpec((1, D), lambda i: (0, 0))

    def gate_body(dg_ref, wg_ref, part_ref):
        part_ref[...] = _blocked_matmul(dg_ref, wg_ref)

    part = pl.pallas_call(
        gate_body, grid=(S // tm,), name="ffn_bwd_in_gate", in_specs=[blk, _resident((nblk * FB, D))],
        out_specs=row, out_shape=SDS((S, D), F32), compiler_params=_cp(1))(dg, wg.reshape(nblk * FB, D))

    def body(du_ref, wu_ref, part_ref, h_ref, dh2_ref, wn_ref, dh_ref, dhb_ref, dw_ref):
        @pl.when(pl.program_id(0) == 0)
        def _():
            dw_ref[...] = jnp.zeros_like(dw_ref)

        dh, dw = _rms_bwd(part_ref[...] + _blocked_matmul(du_ref, wu_ref), h_ref[...], wn_ref[...])
        dh = dh2_ref[...] + dh
        dh_ref[...] = dh
        dhb_ref[...] = dh.astype(BF16)
        dw_ref[...] += dw

    return pl.pallas_call(
        body, grid=(S // tm,), name="ffn_bwd_in",
        in_specs=[blk, _resident((nblk * FB, D)), row, row, row, vec],
        out_specs=[row, row, vec], out_shape=[SDS((S, D), F32), SDS((S, D), BF16), SDS((1, D), F32)],
        compiler_params=_cp(1))(du, wu.reshape(nblk * FB, D), part, h1, dh2, w_norm)


def _dmix(dh1, w_out):
    S, D = dh1.shape
    tm = min(512, S)

    def body(dh_ref, w_ref, o_ref):
        o_ref[...] = _dot(dh_ref[...], w_ref[...], _NT)

    row = pl.BlockSpec((tm, D), lambda i: (i, 0))
    return pl.pallas_call(
        body, grid=(S // tm,), name="dmix", in_specs=[row, pl.BlockSpec((D, D), lambda i: (0, 0))],
        out_specs=row, out_shape=SDS((S, D), F32), compiler_params=_cp(1))(dh1, w_out)


def _in_bwd(dproj, w_blk, x, dh1, w_norm):
    S, D = x.shape
    nblk, _, NB = w_blk.shape
    tm = min(RESIDENT_ROWS, S)

    def body(dp_ref, w_ref, x_ref, dh1_ref, wn_ref, dx_ref, dw_ref):
        @pl.when(pl.program_id(0) == 0)
        def _():
            dw_ref[...] = jnp.zeros_like(dw_ref)

        dn = None
        for j in range(nblk):
            part = _dot(dp_ref[:, pl.ds(j * NB, NB)], w_ref[j], _NT)
            dn = part if dn is None else dn + part
        dh, dw = _rms_bwd(dn, x_ref[...], wn_ref[...])
        dx_ref[...] = dh1_ref[...] + dh
        dw_ref[...] += dw

    row = pl.BlockSpec((tm, D), lambda i: (i, 0))
    vec = pl.BlockSpec((1, D), lambda i: (0, 0))
    return pl.pallas_call(
        body, grid=(S // tm,), name="in_bwd",
        in_specs=[pl.BlockSpec((tm, nblk * NB), lambda i: (i, 0)),
                  pl.BlockSpec((nblk, D, NB), lambda i: (0, 0, 0), pipeline_mode=pl.Buffered(1)), row, row, vec],
        out_specs=[row, vec], out_shape=[SDS((S, D), F32), SDS((1, D), F32)],
        compiler_params=_cp(1))(dproj, w_blk, x, dh1, w_norm)


def _wgrad(a, b, a_spec, b_spec, o_spec, o_shape, grid, name):
    nk = grid[-1]

    def ld(ref):
        return ref[0] if len(ref.shape) == 3 else ref[...]

    def body(a_ref, b_ref, o_ref, acc):
        k = pl.program_id(len(grid) - 1)

        @pl.when(k == 0)
        def _():
            acc[...] = jnp.zeros_like(acc)

        acc[...] += _dot(ld(a_ref), ld(b_ref), _TN)

        @pl.when(k == nk - 1)
        def _():
            if len(o_ref.shape) == 3:
                o_ref[0] = acc[...].astype(o_ref.dtype)
            else:
                o_ref[...] = acc[...].astype(o_ref.dtype)

    return pl.pallas_call(
        body, grid=grid, name=name, in_specs=[a_spec, b_spec], out_specs=o_spec, out_shape=SDS(o_shape, BF16),
        scratch_shapes=[pltpu.VMEM(o_spec.block_shape[-2:], F32)], compiler_params=_cp(len(grid)))(a, b)


def _peer(k):
    x, y, c = lax.axis_index("x"), lax.axis_index("y"), lax.axis_index("c")
    px = 1 - x if k & 4 else x
    py = 1 - y if k & 2 else y
    pc = 1 - c if k & 1 else c
    return (px, py, pc), 4 * px + 2 * py + pc


def _exchange_copies(srcs, lands, send_sems, recv_sems, which, gather):
    _, me = _peer(0)
    pairs = []
    for pos, a in enumerate(which):
        for k in range(1, N_DEV):
            dev, idx = _peer(k)
            sem = pos * (N_DEV - 1) + k - 1
            src = srcs[a] if gather else srcs[a].at[idx]
            mk = functools.partial(pltpu.make_async_remote_copy, src_ref=src, send_sem=send_sems.at[sem],
                                   recv_sem=recv_sems.at[sem], device_id=dev, device_id_type=MESH)
            pairs.append((mk(dst_ref=lands[a].at[me]), mk(dst_ref=lands[a].at[idx])))
    return pairs


def _sequencer_kernel(name, collective_id, n_remote, n_local):
    return pl.kernel(mesh=plsc.ScalarSubcoreMesh(axis_name="sequencer", num_cores=1), name=name,
                     scratch_types=(pltpu.SemaphoreType.DMA((n_remote,)), pltpu.SemaphoreType.DMA((n_remote,)),
                                    pltpu.SemaphoreType.DMA((n_local,))),
                     compiler_params=pltpu.CompilerParams(collective_id=collective_id))


def _handshake(ks):
    barrier = pltpu.get_barrier_semaphore()
    for k in ks:
        pl.semaphore_signal(barrier, inc=1, device_id=_peer(k)[0], device_id_type=MESH)
    pl.semaphore_wait(barrier, len(ks))


def _sequencer_scatter(arrays, name, collective_id):
    n = len(arrays)
    hbm = pltpu.MemorySpace.HBM
    srcs = [jax.new_ref(a, memory_space=hbm) for a in arrays]
    lands = [jax.empty_ref(SDS(a.shape, a.dtype), memory_space=hbm) for a in arrays]

    @_sequencer_kernel(name, collective_id, n * (N_DEV - 1), n)
    def launch(send_sems, recv_sems, local_sems):
        _handshake(range(1, N_DEV))
        _, me = _peer(0)
        local = [pltpu.make_async_copy(srcs[a].at[me], lands[a].at[me], local_sems.at[a]) for a in range(n)]
        pairs = _exchange_copies(srcs, lands, send_sems, recv_sems, range(n), False)
        for out, _ in pairs:
            out.start()
        for cp in local:
            cp.start()
        for out, arrival in pairs:
            out.wait_send()
            arrival.wait_recv()
        for cp in local:
            cp.wait()

    launch()
    return [r[...] for r in lands]


SIBLING = 1
OTHER_CHIPS = (2, 4, 6)


def _sequencer_gather(arrays, name, collective_id):
    n = len(arrays)
    hbm = pltpu.MemorySpace.HBM
    srcs = [jax.new_ref(a, memory_space=hbm) for a in arrays]
    lands = [jax.empty_ref(SDS((N_DEV,) + a.shape, a.dtype), memory_space=hbm) for a in arrays]

    @_sequencer_kernel(name, collective_id, n * (N_DEV - 1), n)
    def launch(send_sems, recv_sems, local_sems):
        _handshake((SIBLING,) + OTHER_CHIPS)
        _, me = _peer(0)
        sibling, _ = _peer(SIBLING)

        def copy(a, k, src, block, to):
            sem = a * (N_DEV - 1) + k - 1
            return pltpu.make_async_remote_copy(src_ref=src, dst_ref=lands[a].at[block], send_sem=send_sems.at[sem],
                                                recv_sem=recv_sems.at[sem], device_id=to, device_id_type=MESH)

        local = [pltpu.make_async_copy(srcs[a], lands[a].at[me], local_sems.at[a]) for a in range(n)]
        first = [copy(a, k, srcs[a], me, _peer(k)[0]) for a in range(n) for k in OTHER_CHIPS + (SIBLING,)]
        for cp in first + local:
            cp.start()
        passed = []
        for a in range(n):
            for k in OTHER_CHIPS:
                _, block = _peer(k)
                copy(a, k, srcs[a], block, sibling).wait_recv()
                passed.append(copy(a, k ^ SIBLING, lands[a].at[block], block, sibling))
                passed[-1].start()
        for a in range(n):
            for k in (SIBLING,) + tuple(k ^ SIBLING for k in OTHER_CHIPS):
                copy(a, k, srcs[a], _peer(k)[1], sibling).wait_recv()
        for cp in first + passed:
            cp.wait_send()
        for cp in local:
            cp.wait()

    launch()
    return [r[...] for r in lands]


SMALL_ROWS = 64


def _small_step(part, w, m, v):
    def body(p_ref, w_ref, m_ref, v_ref, g_ref, d_ref, nm_ref, nv_ref, gath, send_sems, recv_sems):
        _, me = _peer(0)
        gath[me] = p_ref[...]
        copies = []
        for k in range(1, N_DEV):
            dev, idx = _peer(k)
            out = pltpu.make_async_remote_copy(src_ref=p_ref, dst_ref=gath.at[me], send_sem=send_sems.at[k - 1],
                                               recv_sem=recv_sems.at[k - 1], device_id=dev, device_id_type=MESH)
            out.start()
            arrival = pltpu.make_async_remote_copy(src_ref=p_ref, dst_ref=gath.at[idx], send_sem=send_sems.at[k - 1],
                                                   recv_sem=recv_sems.at[k - 1], device_id=dev, device_id_type=MESH)
            copies.append((out, arrival))
        for out, arrival in copies:
            out.wait_send()
            arrival.wait_recv()
        g = gath[0]
        for p in range(1, N_DEV):
            g = g + gath[p]
        g_ref[...] = g
        d_ref[...], nm_ref[...], nv_ref[...] = _adamw(w_ref[...], g, m_ref[...], v_ref[...])

    vm = pl.BlockSpec(memory_space=pltpu.VMEM)
    return pl.pallas_call(
        body, name="small_step", in_specs=[vm] * 4, out_specs=[vm] * 4,
        out_shape=[SDS((SMALL_ROWS, 128), F32)] * 4,
        scratch_shapes=[pltpu.VMEM((N_DEV, SMALL_ROWS, 128), F32), pltpu.SemaphoreType.DMA((N_DEV - 1,)),
                        pltpu.SemaphoreType.DMA((N_DEV - 1,))])(part, w, m, v)


def _adamw(w, g, m, v):
    m = ADAM_B1 * m + (1.0 - ADAM_B1) * g
    v = ADAM_B2 * v + (1.0 - ADAM_B2) * (g * g)
    m_hat = m / (1.0 - ADAM_B1 ** ADAM_STEP)
    v_hat = v / (1.0 - ADAM_B2 ** ADAM_STEP)
    delta = -ADAM_LR * (m_hat / (jnp.sqrt(v_hat) + ADAM_EPS) + ADAM_WD * w)
    return delta, m, v


def _adamw_block(parts, w, m, v, name):
    R, C = w.shape
    n_parts = len(parts)
    Rp = R // n_parts
    tr = next(t for t in (256, 128, 64, 32, 16, 8) if Rp % t == 0 and t * C <= 256 * 1024)
    per_part = Rp // tr

    def body(*refs):
        p_refs = refs[:n_parts]
        w_ref, m_ref, v_ref, g_ref, d_ref, nm_ref, nv_ref = refs[n_parts:]
        for k, p_ref in enumerate(p_refs):
            @pl.when(pl.program_id(0) // per_part == k)
            def _(p_ref=p_ref):
                g = p_ref[0].astype(F32)
                for p in range(1, N_DEV):
                    g = g + p_ref[p].astype(F32)
                g_ref[...] = g
                d_ref[...], nm_ref[...], nv_ref[...] = _adamw(w_ref[...], g, m_ref[...], v_ref[...])

    row = pl.BlockSpec((tr, C), lambda i: (i, 0))
    part_specs = [pl.BlockSpec((N_DEV, tr, C), functools.partial(
        lambda i, k: (0, jnp.clip(i - k * per_part, 0, per_part - 1), 0), k=k)) for k in range(n_parts)]
    return pl.pallas_call(
        body, grid=(R // tr,), name=name, in_specs=part_specs + [row, row, row],
        out_specs=[row] * 4, out_shape=[SDS((R, C), F32)] * 4, compiler_params=_cp(1))(*parts, w, m, v)


def _pack_small(mix, ffn, fin, retw, dec_f, dec_b, loss):
    flat = jnp.concatenate([mix.reshape(-1), ffn.reshape(-1), fin.reshape(-1), retw.reshape(-1), dec_f.reshape(-1),
                            dec_b.reshape(-1), loss.reshape(-1)])
    return jnp.pad(flat, (0, SMALL_ROWS * 128 - flat.shape[0])).reshape(SMALL_ROWS, 128)


def _unpack_small(packed, shapes):
    flat = packed.reshape(-1)
    out, at = [], 0
    for s in shapes:
        n = math.prod(s)
        out.append(flat[at:at + n].reshape(s))
        at += n
    return out


def kernel(x, norm_mix_w, w_in, ret_decay_fwd, ret_decay_bwd, ret_norm_w, w_out, norm_ffn_w, w_gate, w_up, w_down, norm_final_w, loss_target, m_norm_mix_w, m_w_in, m_ret_decay_fwd, m_ret_decay_bwd, m_ret_norm_w, m_w_out, m_norm_ffn_w, m_w_gate, m_w_up, m_w_down, m_norm_final_w, v_norm_mix_w, v_w_in, v_ret_decay_fwd, v_ret_decay_bwd, v_ret_norm_w, v_w_out, v_norm_ffn_w, v_w_gate, v_w_up, v_w_down, v_norm_final_w):
    x2 = x[0]
    tgt = loss_target[0]
    S, D = x2.shape
    H = ret_norm_w.shape[1] // HEAD_DIM
    DA = H * HEAD_DIM
    fin_w = norm_final_w.reshape(1, D)
    big = (w_in[0], w_out[0], w_gate[0].T, w_up[0].T, w_down[0])

    big_b = [w.astype(BF16) for w in big]
    wi, = _sequencer_gather(big_b[:1], "gather_in", 0)
    wo, wg, wu = _sequencer_gather(big_b[1:4], "gather_mid", 1)
    wd, = _sequencer_gather(big_b[4:], "gather_down", 5)
    NB = wi.shape[2]

    proj, n1 = _proj_fwd(x2, norm_mix_w, wi)
    bias = _attn_bias()[:H]
    attn, lse = _attn_fwd(proj, bias)
    ret, o_raw = _ret_fwd(proj, ret_decay_fwd, ret_decay_bwd, ret_norm_w)
    wo_full = wo.reshape(D, D)
    d_ff = N_DEV * wd.shape[1]
    FB = FFN_BLOCK if d_ff % FFN_BLOCK == 0 else wd.shape[1]
    n_fb = d_ff // FB
    wg, wu = wg.reshape(n_fb, FB, D), wu.reshape(n_fb, FB, D)
    wd_full = wd.reshape(d_ff, D)
    h1, mixed, n2 = _out_fwd(x2, attn, ret, wo_full, norm_ffn_w)
    gate, up, act = _ffn_up(n2, wg, wu)
    dh2, dh2_b, loss_parts, g_fin = _ffn_down_loss(act, wd_full, h1, tgt, fin_w)

    dgate, dup = _ffn_bwd_act(dh2_b, wd_full, gate, up)
    tn = min(1024, D)
    ffn_specs = (pl.BlockSpec((1, S, FB), lambda j, n, k: (j, 0, 0)), pl.BlockSpec((S, tn), lambda j, n, k: (0, n)),
                 pl.BlockSpec((1, FB, tn), lambda j, n, k: (j, 0, n)), (n_fb, FB, D), (n_fb, D // tn, 1))
    per_dev = (N_DEV, d_ff // N_DEV, D)
    g_wd = _wgrad(act, dh2_b, *ffn_specs, "wgrad_down").reshape(per_dev)
    g_wg = _wgrad(dgate, n2, *ffn_specs, "wgrad_gate").reshape(per_dev)
    g_wu = _wgrad(dup, n2, *ffn_specs, "wgrad_up").reshape(per_dev)
    parts_f = _sequencer_scatter([g_wg, g_wu, g_wd], "scatter_ffn", 2)
    dh1, dh1_b, g_ffn = _ffn_bwd_in(dgate, dup, wg, wu, h1, dh2, norm_ffn_w)
    dmix = _dmix(dh1_b, wo_full)
    tmw = min(512, D)
    tk = min(2048, S)
    g_wo = _wgrad(mixed, dh1_b, pl.BlockSpec((tk, tmw), lambda m, k: (k, m)), pl.BlockSpec((tk, D), lambda m, k: (k, 0)),
                  pl.BlockSpec((tmw, D), lambda m, k: (m, 0)), (D, D), (D // tmw, S // tk), "wgrad_out")
    parts_o = _sequencer_scatter([g_wo.reshape(N_DEV, D // N_DEV, D)], "scatter_out", 3)
    d_ret, dg_r, small_w = _ret_gate_bwd(proj, o_raw, dmix, ret_norm_w, DA)
    dq_r, dk_r, dv_r, small = _ret_bwd(proj, d_ret, ret_decay_fwd, ret_decay_bwd)
    dq_a, dk_a, dv_a = _attn_bwd(proj, attn, lse, dmix, bias)
    dproj = jnp.concatenate([t.astype(BF16) for t in (dq_a, dk_a, dv_a, dq_r, dk_r, dv_r, dg_r)], axis=1)
    half = D // tmw // 2
    parts_i = []
    for part, (name, cid) in enumerate((("in_lo", 4), ("in_hi", 6))):
        g_wi = _wgrad(n1, dproj, pl.BlockSpec((S, tmw), functools.partial(lambda j, m, k, off: (0, m + off), off=part * half)),
                      pl.BlockSpec((S, NB), lambda j, m, k: (0, j)), pl.BlockSpec((1, tmw, NB), lambda j, m, k: (j, m, 0)),
                      (N_DEV, D // 2, NB), (N_DEV, half, 1), "wgrad_" + name)
        parts_i += _sequencer_scatter([g_wi], "scatter_" + name, cid)
    grad_x, g_mix = _in_bwd(dproj, wi, x2, dh1, norm_mix_w)

    big_m = (m_w_in[0], m_w_out[0], m_w_gate[0].T, m_w_up[0].T, m_w_down[0])
    big_v = (v_w_in[0], v_w_out[0], v_w_gate[0].T, v_w_up[0].T, v_w_down[0])
    names = ("adamw_in", "adamw_out", "adamw_gate", "adamw_up", "adamw_down")
    upd = [None] * 5
    for a, p in zip((2, 3, 4, 1, 0), [[t] for t in parts_f + parts_o] + [parts_i]):
        upd[a] = _adamw_block(p, big[a], big_m[a], big_v[a], names[a])

    g_dec_f = small[:, 0, 0].reshape(1, H)
    g_dec_b = small[:, 1, 0].reshape(1, H)
    g_retw = small_w[:, 0, :].reshape(1, DA)
    loss_local = jnp.sum(loss_parts[::8, 0])
    zero = jnp.zeros((1,), F32)
    part = _pack_small(g_mix, g_ffn, g_fin, g_retw, g_dec_f, g_dec_b, loss_local)
    sw = _pack_small(norm_mix_w, norm_ffn_w, norm_final_w, ret_norm_w, ret_decay_fwd, ret_decay_bwd, zero)
    sm = _pack_small(m_norm_mix_w, m_norm_ffn_w, m_norm_final_w, m_ret_norm_w, m_ret_decay_fwd, m_ret_decay_bwd, zero)
    sv = _pack_small(v_norm_mix_w, v_norm_ffn_w, v_norm_final_w, v_ret_norm_w, v_ret_decay_fwd, v_ret_decay_bwd, zero)
    shapes = [(1, D), (1, D), (D,), (1, DA), (1, H), (1, H), ()]
    sg, sd, snm, snv = [_unpack_small(t, shapes) for t in _small_step(part, sw, sm, sv)]
    loss = sg[6]

    def ordered(small_set, k):
        b = [(u[k].T if a in (2, 3) else u[k])[None] for a, u in enumerate(upd)]
        return [small_set[0], b[0], small_set[4], small_set[5], small_set[3], b[1], small_set[1], b[2], b[3], b[4],
                small_set[2]]

    return (loss, grad_x[None], *ordered(sg, 0), *ordered(sd, 1), *ordered(snm, 2), *ordered(snv, 3))
```

```python
import functools
import math

import numpy as np
import jax
import jax.numpy as jnp
from jax import lax
from jax.experimental import pallas as pl
from jax.experimental.pallas import tpu as pltpu
from jax.experimental.pallas import tpu_sc as plsc

F32 = jnp.float32
BF16 = jnp.bfloat16
SDS = jax.ShapeDtypeStruct

HEAD_DIM = 128
EPS = 1e-6
RET_CHUNK = 128
DILATIONS = (1, 4, 16)
BAND = 64
Q_TILE = 128
K_TILE = Q_TILE + 2 * BAND
KV_PAD = BAND * 4
TILE_GROUP = 8
NEG = -1e30
N_DEV = 8
N_GROUPS = 7
ADAM_LR, ADAM_B1, ADAM_B2, ADAM_EPS, ADAM_WD, ADAM_STEP = 0.001, 0.9, 0.999, 1e-08, 0.01, 10
VMEM_LIMIT = 56 * 1024 * 1024
MESH = pl.DeviceIdType.MESH
ANY = pl.BlockSpec(memory_space=pl.ANY)


def _cp(n_grid):
    return pltpu.CompilerParams(dimension_semantics=("arbitrary",) * n_grid, vmem_limit_bytes=VMEM_LIMIT)


def _sigmoid(x):
    return 1.0 / (1.0 + jnp.exp(-x))


def _rms_scale(h):
    return lax.rsqrt(jnp.mean(h * h, axis=-1, keepdims=True) + EPS)


def _rms_bwd(dn, h, w):
    r = _rms_scale(h)
    gw = dn * w
    dh = r * gw - h * (r * r * r) * jnp.mean(gw * h, axis=-1, keepdims=True)
    return dh, jnp.sum(dn * h * r, axis=0, keepdims=True)


def _dot(a, b, dims):
    return lax.dot_general(a.astype(BF16), b.astype(BF16), (dims, ((), ())), preferred_element_type=F32)


_NN = ((1,), (0,))
_NT = ((1,), (1,))
_TN = ((0,), (0,))


RESIDENT_ROWS = 256


def _resident(shape):
    return pl.BlockSpec(shape, lambda i: (0, 0), pipeline_mode=pl.Buffered(1))


def _blocked_matmul(a_ref, w_ref):
    nblk, _, fb = a_ref.shape
    out = None
    for j in range(nblk):
        part = jnp.dot(a_ref[j], w_ref[pl.ds(j * fb, fb), :], preferred_element_type=F32)
        out = part if out is None else out + part
    return out


def _proj_fwd(x, w_norm, w_blk):
    S, D = x.shape
    nblk, _, NB = w_blk.shape
    tm = min(1024, S)

    def body(x_ref, wn_ref, w_ref, proj_ref, n_ref, n_scr):
        @pl.when(pl.program_id(1) == 0)
        def _():
            xf = x_ref[...]
            nb = (xf * _rms_scale(xf) * wn_ref[...]).astype(BF16)
            n_scr[...] = nb
            n_ref[...] = nb
        proj_ref[...] = jnp.dot(n_scr[...], w_ref[0], preferred_element_type=F32)

    return pl.pallas_call(
        body, grid=(S // tm, nblk), name="proj_fwd",
        in_specs=[pl.BlockSpec((tm, D), lambda i, j: (i, 0)), pl.BlockSpec((1, D), lambda i, j: (0, 0)),
                  pl.BlockSpec((1, D, NB), lambda i, j: (j, 0, 0))],
        out_specs=[pl.BlockSpec((tm, NB), lambda i, j: (i, j)), pl.BlockSpec((tm, D), lambda i, j: (i, 0))],
        out_shape=[SDS((S, nblk * NB), F32), SDS((S, D), BF16)],
        scratch_shapes=[pltpu.VMEM((tm, D), BF16)], compiler_params=_cp(2))(x, w_norm, w_blk)


def _out_fwd(x, attn, ret, w_out, w_norm):
    S, D = x.shape
    DA = attn.shape[1]
    tm = min(256, S)

    def body(x_ref, a_ref, r_ref, w_ref, wn_ref, h_ref, mix_ref, n_ref):
        a = a_ref[...].astype(BF16)
        r = r_ref[...].astype(BF16)
        mix_ref[:, :DA] = a
        mix_ref[:, DA:] = r
        h = x_ref[...] + jnp.dot(a, w_ref[:DA, :], preferred_element_type=F32) \
            + jnp.dot(r, w_ref[DA:, :], preferred_element_type=F32)
        h_ref[...] = h
        n_ref[...] = (h * _rms_scale(h) * wn_ref[...]).astype(BF16)

    row = lambda w: pl.BlockSpec((tm, w), lambda i: (i, 0))
    return pl.pallas_call(
        body, grid=(S // tm,), name="out_fwd",
        in_specs=[row(D), row(DA), row(D - DA), pl.BlockSpec((D, D), lambda i: (0, 0)),
                  pl.BlockSpec((1, D), lambda i: (0, 0))],
        out_specs=[row(D), row(D), row(D)],
        out_shape=[SDS((S, D), F32), SDS((S, D), BF16), SDS((S, D), BF16)],
        compiler_params=_cp(1))(x, attn, ret, w_out, w_norm)


def _ffn_up(n2, wg, wu):
    S, D = n2.shape
    nblk, FB, _ = wg.shape
    tm = min(1024, S)

    def body(n_ref, wg_ref, wu_ref, g_ref, u_ref, a_ref):
        n = n_ref[...]
        g = _dot(n, wg_ref[0], _NT)
        u = _dot(n, wu_ref[0], _NT)
        g_ref[0] = g.astype(BF16)
        u_ref[0] = u.astype(BF16)
        a_ref[0] = (g * _sigmoid(g) * u).astype(BF16)

    wspec = pl.BlockSpec((1, FB, D), lambda j, i: (j, 0, 0))
    ospec = pl.BlockSpec((1, tm, FB), lambda j, i: (j, i, 0))
    return pl.pallas_call(
        body, grid=(nblk, S // tm), name="ffn_up",
        in_specs=[pl.BlockSpec((tm, D), lambda j, i: (i, 0)), wspec, wspec],
        out_specs=[ospec, ospec, ospec],
        out_shape=[SDS((nblk, S, FB), BF16)] * 3,
        compiler_params=_cp(2))(n2, wg, wu)


def _ffn_down_loss(act, wd, h1, target, w_norm):
    nblk, S, FB = act.shape
    D = h1.shape[1]
    tm = min(RESIDENT_ROWS, S)

    def body(a_ref, wd_ref, h_ref, t_ref, wn_ref, dh_ref, dhb_ref, loss_ref, dw_ref):
        @pl.when(pl.program_id(0) == 0)
        def _():
            dw_ref[...] = jnp.zeros_like(dw_ref)

        h = h_ref[...] + _blocked_matmul(a_ref, wd_ref)
        w = wn_ref[...]
        err = h * _rms_scale(h) * w - t_ref[...]
        loss_ref[...] = jnp.full(loss_ref.shape, 0.5 * jnp.sum(err * err) / D, F32)
        dh, dw = _rms_bwd(err * (1.0 / D), h, w)
        dh_ref[...] = dh
        dhb_ref[...] = dh.astype(BF16)
        dw_ref[...] += dw

    row = pl.BlockSpec((tm, D), lambda i: (i, 0))
    vec = pl.BlockSpec((1, D), lambda i: (0, 0))
    return pl.pallas_call(
        body, grid=(S // tm,), name="ffn_down_loss",
        in_specs=[pl.BlockSpec((nblk, tm, FB), lambda i: (0, i, 0)), _resident((nblk * FB, D)), row, row, vec],
        out_specs=[row, row, pl.BlockSpec((8, 128), lambda i: (i, 0)), vec],
        out_shape=[SDS((S, D), F32), SDS((S, D), BF16), SDS((S // tm * 8, 128), F32), SDS((1, D), F32)],
        compiler_params=_cp(1))(act, wd, h1, target, w_norm)


def _attn_bias():
    n_heads = 8
    slopes = np.exp2(-8.0 * np.arange(1, n_heads + 1, dtype=np.float32) / n_heads)
    dist = np.abs(np.arange(K_TILE)[None, :] - BAND - np.arange(Q_TILE)[:, None])
    out = np.empty((n_heads, len(DILATIONS), Q_TILE, K_TILE), np.float32)
    for h in range(n_heads):
        for p, d in enumerate(DILATIONS):
            out[h, p] = np.where(dist <= BAND, -slopes[h] * (d * dist).astype(np.float32), NEG)
    return jnp.asarray(out)


def _attn_tiles(S, d):
    L = S // d
    per_class = L // Q_TILE
    return L, per_class, d * per_class


def _tile_rows(t, d, per_class):
    r = t // per_class
    a = (t % per_class) * Q_TILE
    q_rows = pl.ds(r + d * a, Q_TILE, stride=d) if d > 1 else pl.ds(pl.multiple_of(a, Q_TILE), Q_TILE)
    k_rows = pl.ds(KV_PAD + r + d * (a - BAND), K_TILE, stride=d) if d > 1 else pl.ds(
        pl.multiple_of(KV_PAD + a - BAND, BAND), K_TILE)
    return a, q_rows, k_rows


def _to_quarters(dst, src, n, dst_off=0):
    for r in range(4):
        dst[pl.ds(dst_off + r * (n // 4), n // 4), :] = src[pl.ds(r, n // 4, stride=4), :]


def _quarter_tile_rows(t, S):
    L = S // 16
    per_class = L // Q_TILE
    blk, tt = t // (4 * per_class), t % (4 * per_class)
    r, a = tt // per_class, (tt % per_class) * Q_TILE
    q_rows = pl.ds(blk * (S // 4) + r + 4 * a, Q_TILE, stride=4)
    k_rows = pl.ds(KV_PAD + blk * (S // 4) + r + 4 * (a - BAND), K_TILE, stride=4)
    return a, q_rows, k_rows


def _lanes(x, width):
    return jnp.concatenate([x] * (width // HEAD_DIM), axis=-1)


_BNT = (((2,), (2,)), ((0,), (0,)))
_BNN = (((2,), (1,)), ((0,), (0,)))
_BTN = (((1,), (1,)), ((0,), (0,)))


def _bdot(a, b, dims):
    return lax.dot_general(a, b, dims, preferred_element_type=F32)


def _stacked(rows, loaders):
    return [jnp.stack([f(*r) for r in rows]) for f in loaders]


def _edge_mask(a, L):
    lk = lax.broadcasted_iota(jnp.int32, (1, K_TILE), 1) + (a - BAND)
    return jnp.where((lk >= 0) & (lk < L), 0.0, NEG).astype(F32)


def _fill_padded(dst, src, S):
    dst[pl.ds(0, KV_PAD), :] = jnp.zeros((KV_PAD, HEAD_DIM), F32)
    dst[pl.ds(KV_PAD + S, KV_PAD), :] = jnp.zeros((KV_PAD, HEAD_DIM), F32)
    dst[pl.ds(KV_PAD, S), :] = src[...]


def _head_specs(S, groups, n_heads):
    return [pl.BlockSpec((S, HEAD_DIM), functools.partial(lambda h, g: (0, g * n_heads + h), g=g)) for g in groups]


def _attn_fwd(proj, bias):
    S = proj.shape[0]
    H = proj.shape[1] // (N_GROUPS * HEAD_DIM)
    scale = HEAD_DIM ** -0.5

    def body(q_ref, k_ref, v_ref, b_ref, o_ref, lse_ref, kp, vp, m_run, l_run, q4, m3, l3, acc3):
        _fill_padded(kp, k_ref, S)
        _fill_padded(vp, v_ref, S)
        o_ref[...] = jnp.zeros_like(o_ref)
        m_run[...] = jnp.full(m_run.shape, NEG, F32)
        l_run[...] = jnp.zeros_like(l_run)
        for p, d in enumerate(DILATIONS[:2]):
            L, per_class, n_tiles = _attn_tiles(S, d)

            def tiles(t, carry, p=p, d=d, L=L, per_class=per_class, n_tiles=n_tiles):
                rows = [_tile_rows(t + u * (n_tiles // TILE_GROUP), d, per_class) for u in range(TILE_GROUP)]
                qs, ks, vs, m_old, l_old, o_old, edge = _stacked(rows, (
                    lambda a, qr, kr: q_ref[qr, :].astype(BF16), lambda a, qr, kr: kp[kr, :].astype(BF16),
                    lambda a, qr, kr: vp[kr, :].astype(BF16), lambda a, qr, kr: m_run[qr, :],
                    lambda a, qr, kr: l_run[qr, :], lambda a, qr, kr: o_ref[qr, :], lambda a, qr, kr: _edge_mask(a, L)))
                s = _bdot(qs, ks, _BNT) * scale + b_ref[0, p][None] + edge
                m_new = jnp.maximum(m_old, jnp.max(s, axis=-1, keepdims=True))
                pr = jnp.exp(s - _lanes(m_new, K_TILE)).astype(BF16)
                alpha = jnp.exp(m_old - m_new)
                l_new = alpha * l_old + _bdot(pr, jnp.ones((TILE_GROUP, K_TILE, HEAD_DIM), BF16), _BNN)
                o_new = alpha * o_old + _bdot(pr, vs, _BNN)
                for u, (_, qr, _) in enumerate(rows):
                    o_ref[qr, :] = o_new[u]
                    m_run[qr, :] = m_new[u]
                    l_run[qr, :] = l_new[u]
                return carry

            lax.fori_loop(0, n_tiles // TILE_GROUP, tiles, 0)

        _to_quarters(q4, q_ref, S)
        _to_quarters(kp, k_ref, S, KV_PAD)
        _to_quarters(vp, v_ref, S, KV_PAD)
        n_tiles = _attn_tiles(S, DILATIONS[2])[2]

        def tiles3(t, carry):
            rows = [_quarter_tile_rows(t + u * (n_tiles // TILE_GROUP), S) for u in range(TILE_GROUP)]
            qs, ks, vs, edge = _stacked(rows, (
                lambda a, qr, kr: q4[qr, :].astype(BF16), lambda a, qr, kr: kp[kr, :].astype(BF16),
                lambda a, qr, kr: vp[kr, :].astype(BF16), lambda a, qr, kr: _edge_mask(a, S // DILATIONS[2])))
            s = _bdot(qs, ks, _BNT) * scale + b_ref[0, 2][None] + edge
            m_new = jnp.broadcast_to(jnp.max(s, axis=-1, keepdims=True), (TILE_GROUP, Q_TILE, HEAD_DIM))
            pr = jnp.exp(s - _lanes(m_new, K_TILE)).astype(BF16)
            l_new = _bdot(pr, jnp.ones((TILE_GROUP, K_TILE, HEAD_DIM), BF16), _BNN)
            o_new = _bdot(pr, vs, _BNN)
            for u, (_, qr, _) in enumerate(rows):
                acc3[qr, :] = o_new[u]
                m3[qr, :] = m_new[u]
                l3[qr, :] = l_new[u]
            return carry

        lax.fori_loop(0, n_tiles // TILE_GROUP, tiles3, 0)
        for r in range(4):
            nat, qtr = pl.ds(r, S // 4, stride=4), pl.ds(r * (S // 4), S // 4)
            m_a, m_b = m_run[nat, :], m3[qtr, :]
            m = jnp.maximum(m_a, m_b)
            w_a, w_b = jnp.exp(m_a - m), jnp.exp(m_b - m)
            l = w_a * l_run[nat, :] + w_b * l3[qtr, :]
            o_ref[nat, :] = (w_a * o_ref[nat, :] + w_b * acc3[qtr, :]) / l
            lse_ref[nat, :] = m + jnp.log(l)

    hspec = pl.BlockSpec((S, HEAD_DIM), lambda h: (0, h))
    padded, plain = pltpu.VMEM((S + 2 * KV_PAD, HEAD_DIM), F32), pltpu.VMEM((S, HEAD_DIM), F32)
    return pl.pallas_call(
        body, grid=(H,), name="attn_fwd",
        in_specs=_head_specs(S, (0, 1, 2), H) + [
            pl.BlockSpec((1, len(DILATIONS), Q_TILE, K_TILE), lambda h: (h, 0, 0, 0))],
        out_specs=[hspec, hspec],
        out_shape=[SDS((S, H * HEAD_DIM), F32), SDS((S, H * HEAD_DIM), F32)],
        scratch_shapes=[padded, padded] + [plain] * 6,
        compiler_params=_cp(1))(proj, proj, proj, bias)


def _attn_bwd(proj, out, lse, dmix, bias):
    S = proj.shape[0]
    H = proj.shape[1] // (N_GROUPS * HEAD_DIM)
    scale = HEAD_DIM ** -0.5
    assert S // DILATIONS[2] >= 2 * Q_TILE

    def body(q_ref, k_ref, v_ref, o_ref, lse_ref, do_ref, b_ref, dq_ref, dk_ref, dv_ref,
             kp, vp, dkp, dvp, dsum, q4, do4, lse4, dsum4):
        _fill_padded(kp, k_ref, S)
        _fill_padded(vp, v_ref, S)
        dkp[...] = jnp.zeros_like(dkp)
        dvp[...] = jnp.zeros_like(dvp)
        dq_ref[...] = jnp.zeros_like(dq_ref)
        dsum[...] = jnp.broadcast_to(jnp.sum(do_ref[...] * o_ref[...], axis=-1, keepdims=True), dsum.shape)

        def run(n_tiles, tile_rows, p, L, q_src, do_src, lse_src, dsum_src, dq_dst, dq_adds):
            def tiles(t, carry):
                rows = [tile_rows(t + u * (n_tiles // TILE_GROUP)) for u in range(TILE_GROUP)]
                qs, ks, vs, dos, lses, dsums, dk_old, dv_old, edge = _stacked(rows, (
                    lambda a, qr, kr: q_src[qr, :].astype(BF16), lambda a, qr, kr: kp[kr, :].astype(BF16),
                    lambda a, qr, kr: vp[kr, :].astype(BF16), lambda a, qr, kr: do_src[qr, :].astype(BF16),
                    lambda a, qr, kr: lse_src[qr, :], lambda a, qr, kr: dsum_src[qr, :],
                    lambda a, qr, kr: dkp[kr, :], lambda a, qr, kr: dvp[kr, :], lambda a, qr, kr: _edge_mask(a, L)))
                s = _bdot(qs, ks, _BNT) * scale + b_ref[0, p][None] + edge
                pr = jnp.exp(s - _lanes(lses, K_TILE))
                ds = (pr * (_bdot(dos, vs, _BNT) - _lanes(dsums, K_TILE)) * scale).astype(BF16)
                dq_new = _bdot(ds, ks, _BNN)
                if dq_adds:
                    dq_new = dq_new + jnp.stack([dq_dst[qr, :] for _, qr, _ in rows])
                dk_new = dk_old + _bdot(ds, qs, _BTN)
                dv_new = dv_old + _bdot(pr.astype(BF16), dos, _BTN)
                for u, (_, qr, kr) in enumerate(rows):
                    dq_dst[qr, :] = dq_new[u]
                    dkp[kr, :] = dk_new[u]
                    dvp[kr, :] = dv_new[u]
                return carry

            lax.fori_loop(0, n_tiles // TILE_GROUP, tiles, 0)

        for p, d in enumerate(DILATIONS[:2]):
            L, per_class, n_tiles = _attn_tiles(S, d)
            run(n_tiles, functools.partial(_tile_rows, d=d, per_class=per_class), p, L,
                q_ref, do_ref, lse_ref, dsum, dq_ref, True)
        dk_ref[...] = dkp[pl.ds(KV_PAD, S), :]
        dv_ref[...] = dvp[pl.ds(KV_PAD, S), :]

        for dst, src in ((q4, q_ref), (do4, do_ref), (lse4, lse_ref), (dsum4, dsum)):
            _to_quarters(dst, src, S)
        _to_quarters(kp, k_ref, S, KV_PAD)
        _to_quarters(vp, v_ref, S, KV_PAD)
        dkp[...] = jnp.zeros_like(dkp)
        dvp[...] = jnp.zeros_like(dvp)
        dq3 = dsum
        run(_attn_tiles(S, DILATIONS[2])[2], functools.partial(_quarter_tile_rows, S=S), 2, S // DILATIONS[2],
            q4, do4, lse4, dsum4, dq3, False)
        for r in range(4):
            nat, qtr = pl.ds(r, S // 4, stride=4), pl.ds(r * (S // 4), S // 4)
            pad_qtr = pl.ds(KV_PAD + r * (S // 4), S // 4)
            dq_ref[nat, :] = dq_ref[nat, :] + dq3[qtr, :]
            dk_ref[nat, :] = dk_ref[nat, :] + dkp[pad_qtr, :]
            dv_ref[nat, :] = dv_ref[nat, :] + dvp[pad_qtr, :]

    hspec = pl.BlockSpec((S, HEAD_DIM), lambda h: (0, h))
    once =pl.BlockSpec((S, HEAD_DIM), lambda h: (0, h), pipeline_mode=pl.Buffered(1))
    padded, plain = pltpu.VMEM((S + 2 * KV_PAD, HEAD_DIM), F32), pltpu.VMEM((S, HEAD_DIM), F32)
    return pl.pallas_call(
        body, grid=(H,), name="attn_bwd",
        in_specs=_head_specs(S, (0, 1, 2), H) + [once, once, once,
                                                  pl.BlockSpec((1, len(DILATIONS), Q_TILE, K_TILE), lambda h: (h, 0, 0, 0))],
        out_specs=[hspec, hspec, hspec],
        out_shape=[SDS((S, H * HEAD_DIM), F32)] * 3,
        scratch_shapes=[padded] * 4 + [plain] * 5,
        compiler_params=_cp(1))(proj, proj, proj, out, lse, dmix, bias)


def _ret_consts(lg, forward):
    C = RET_CHUNK
    i = lax.broadcasted_iota(jnp.int32, (C, C), 0)
    j = lax.broadcasted_iota(jnp.int32, (C, C), 1)
    rel = (i - j) if forward else (j - i)
    inside = (rel >= 0) if forward else (rel > 0)
    relf = jnp.maximum(rel, 0).astype(F32)
    mask = jnp.where(inside, jnp.exp(lg * relf), 0.0)
    idx = lax.broadcasted_iota(jnp.int32, (C, 1), 0).astype(F32)
    q_exp = (idx + 1.0) if forward else (C - idx)
    k_exp = (C - 1.0 - idx) if forward else idx
    return mask, relf, jnp.exp(lg * q_exp), q_exp, jnp.exp(lg * k_exp), k_exp, jnp.exp(lg * C)


def _log_decay(dec_ref, h):
    return -jnp.exp(jnp.full((1, 1), dec_ref[0, h], F32))


FFN_BLOCK = 704
CHUNK_BATCH = 8


def _batch_rows(b):
    n = CHUNK_BATCH * RET_CHUNK
    return pl.ds(pl.multiple_of(b * n, n), n)


def _batch_chunks(b):
    return pl.ds(pl.multiple_of(b * CHUNK_BATCH, CHUNK_BATCH), CHUNK_BATCH)


def _chunks3(x):
    return x.reshape(CHUNK_BATCH, RET_CHUNK, HEAD_DIM)


def _ret_scan(buf, c_decs, nc, reverse):
    def step(n, carry):
        new = []
        for way, r in enumerate(carry):
            c = n if (way == 0) != reverse else nc - 1 - n
            term = buf[way, c]
            buf[way, c] = r
            new.append(r * c_decs[way] + term)
        return tuple(new)

    lax.fori_loop(0, nc, step, (jnp.zeros((HEAD_DIM, HEAD_DIM), F32),) * 2)


def _ret_fwd(proj, dec_f, dec_b, w_norm):
    S = proj.shape[0]
    H = proj.shape[1] // (N_GROUPS * HEAD_DIM)
    nc = S // RET_CHUNK
    scale = HEAD_DIM ** -0.5

    def body(df_ref, db_ref, q_ref, k_ref, v_ref, g_ref, w_ref, y_ref, o_ref, states):
        h = pl.program_id(0)
        consts = [_ret_consts(_log_decay(dref, h), fw) for fw, dref in ((True, df_ref), (False, db_ref))]

        def kv_step(b, carry):
            rows, batch = _batch_rows(b), _batch_chunks(b)
            k3 = _chunks3(k_ref[rows, :])
            v3 = _chunks3(v_ref[rows, :]).astype(BF16)
            for way in range(2):
                states[way, batch] = _bdot((k3 * consts[way][4]).astype(BF16), v3, _BTN)
            return carry

        lax.fori_loop(0, nc // CHUNK_BATCH, kv_step, 0)
        _ret_scan(states, [c[6] for c in consts], nc, False)

        def out_step(b, carry):
            rows, batch = _batch_rows(b), _batch_chunks(b)
            q3 = _chunks3(q_ref[rows, :] * scale)
            k3 = _chunks3(k_ref[rows, :]).astype(BF16)
            v3 = _chunks3(v_ref[rows, :]).astype(BF16)
            a0 = _bdot(q3.astype(BF16), k3, _BNT)
            o = None
            for way in range(2):
                mask, q_dec = consts[way][0], consts[way][2]
                part = _bdot((a0 * mask).astype(BF16), v3, _BNN) \
                    + _bdot((q3 * q_dec).astype(BF16), states[way, batch].astype(BF16), _BNN)
                o = part if o is None else o + part
            o_ref[rows, :] = o.reshape(CHUNK_BATCH * RET_CHUNK, HEAD_DIM)
            return carry

        lax.fori_loop(0, nc // CHUNK_BATCH, out_step, 0)
        o = o_ref[...]
        g = g_ref[...]
        y_ref[...] = o * _rms_scale(o) * w_ref[...] * (g * _sigmoid(g))

    hspec = pl.BlockSpec((S, HEAD_DIM), lambda h: (0, h))
    smem = pl.BlockSpec(memory_space=pltpu.SMEM)
    return pl.pallas_call(
        body, grid=(H,), name="ret_fwd",
        in_specs=[smem, smem] + _head_specs(S, (3, 4, 5, 6), H) + [pl.BlockSpec((1, HEAD_DIM), lambda h: (0, h))],
        out_specs=[hspec, hspec],
        out_shape=[SDS((S, H * HEAD_DIM), F32)] * 2,
        scratch_shapes=[pltpu.VMEM((2, nc, HEAD_DIM, HEAD_DIM), F32)],
        compiler_params=_cp(1))(dec_f, dec_b, proj, proj, proj, proj, w_norm)


def _ret_gate_bwd(proj, o_raw, dmix, w_norm, col0):
    S = proj.shape[0]
    H = proj.shape[1] // (N_GROUPS * HEAD_DIM)

    def body(g_ref, o_ref, dy_ref, w_ref, do_ref, dg_ref, dw_ref):
        o = o_ref[...]
        g = g_ref[...]
        dy = dy_ref[...]
        w = w_ref[...]
        rr = _rms_scale(o)
        normed = o * rr
        sg = _sigmoid(g)
        silu = g * sg
        dw_ref[0] = jnp.broadcast_to(jnp.sum(dy * normed * silu, axis=0, keepdims=True), (8, HEAD_DIM))
        dg_ref[...] = (dy * normed * w * (sg * (1.0 + g * (1.0 - sg)))).astype(BF16)
        dnormed = dy * w * silu
        do_ref[...] = rr * dnormed - o * (rr * rr * rr) * jnp.mean(dnormed * o, axis=-1, keepdims=True)

    hspec = pl.BlockSpec((S, HEAD_DIM), lambda h: (0, h))
    nh0 = col0 // HEAD_DIM
    return pl.pallas_call(
        body, grid=(H,), name="ret_gate_bwd",
        in_specs=_head_specs(S, (6,), H) + [hspec, pl.BlockSpec((S, HEAD_DIM), lambda h: (0, nh0 + h)),
                                            pl.BlockSpec((1, HEAD_DIM), lambda h: (0, h))],
        out_specs=[hspec, hspec, pl.BlockSpec((1, 8, HEAD_DIM), lambda h: (h, 0, 0))],
        out_shape=[SDS((S, H * HEAD_DIM), F32), SDS((S, H * HEAD_DIM), BF16), SDS((H, 8, HEAD_DIM), F32)],
        compiler_params=_cp(1))(proj, o_raw, dmix, w_norm)


def _ret_bwd(proj, d_out, dec_f, dec_b):
    S = proj.shape[0]
    H = proj.shape[1] // (N_GROUPS * HEAD_DIM)
    C = RET_CHUNK
    nc = S // C
    scale = HEAD_DIM ** -0.5

    def body(df_ref, db_ref, q_ref, k_ref, v_ref, do, dq_ref, dk_ref, dv_ref, small_ref, states, d_states):
        h = pl.program_id(0)
        lgs = [_log_decay(df_ref, h), _log_decay(db_ref, h)]
        consts = [_ret_consts(lg, fw) for lg, fw in zip(lgs, (True, False))]

        def prep_step(b, carry):
            rows, batch = _batch_rows(b), _batch_chunks(b)
            q3 = _chunks3(q_ref[rows, :] * scale)
            k3 = _chunks3(k_ref[rows, :])
            v3 = _chunks3(v_ref[rows, :]).astype(BF16)
            do3 = _chunks3(do[rows, :]).astype(BF16)
            for way in range(2):
                states[way, batch] = _bdot((k3 * consts[way][4]).astype(BF16), v3, _BTN)
                d_states[way, batch] = _bdot((q3 * consts[way][2]).astype(BF16), do3, _BTN)
            return carry

        lax.fori_loop(0, nc // CHUNK_BATCH, prep_step, 0)
        c_decs = [c[6] for c in consts]
        _ret_scan(states, c_decs, nc, False)
        _ret_scan(d_states, c_decs, nc, True)

        def main_step(b, dlams):
            rows, batch = _batch_rows(b), _batch_chunks(b)
            q3 = _chunks3(q_ref[rows, :] * scale)
            k3 = _chunks3(k_ref[rows, :])
            q3b, k3b = q3.astype(BF16), k3.astype(BF16)
            v3b = _chunks3(v_ref[rows, :]).astype(BF16)
            do3b = _chunks3(do[rows, :]).astype(BF16)
            a0 = _bdot(q3b, k3b, _BNT)
            pv = _bdot(do3b, v3b, _BNT)
            dq = dk = dv = None
            new_dlams = []
            for way in range(2):
                mask, relf, q_dec, q_exp, k_dec, k_exp, c_dec = consts[way]
                state, d_state = states[way, batch], d_states[way, batch]
                dp = pv * mask
                dpb = dp.astype(BF16)
                gq = _bdot(do3b, state.astype(BF16), _BNT)
                gk = _bdot(v3b, d_state.astype(BF16), _BNT)
                parts = (_bdot(dpb, k3b, _BNN) + q_dec * gq, _bdot(dpb, q3b, _BTN) + k_dec * gk,
                         _bdot((a0 * mask).astype(BF16), do3b, _BTN)
                         + _bdot((k3 * k_dec).astype(BF16), d_state.astype(BF16), _BNN))
                dq, dk, dv = parts if dq is None else (dq + parts[0], dk + parts[1], dv + parts[2])
                total = lambda x: jnp.sum(jnp.sum(x, axis=0), axis=0, keepdims=True)
                new_dlams.append(dlams[way] + total(relf * a0 * dp)
                                 + total(q_exp * q_dec * q3 * gq + k_exp * k_dec * k3 * gk)
                                 + (C * c_dec) * total(state * d_state))
            flat = lambda x: x.reshape(CHUNK_BATCH * C, HEAD_DIM)
            dq_ref[rows, :] = (flat(dq) * scale).astype(BF16)
            dk_ref[rows, :] = flat(dk).astype(BF16)
            dv_ref[rows, :] = flat(dv).astype(BF16)
            return tuple(new_dlams)

        dlams = lax.fori_loop(0, nc // CHUNK_BATCH, main_step, (jnp.zeros((1, HEAD_DIM), F32),) * 2)
        for row, (dlam, lg) in enumerate(zip(dlams, lgs)):
            small_ref[0, pl.ds(row, 1), :] = jnp.broadcast_to(jnp.sum(dlam, axis=-1, keepdims=True) * lg, (1, HEAD_DIM))
        small_ref[0, pl.ds(2, 6), :] = jnp.zeros((6, HEAD_DIM), F32)

    hspec = pl.BlockSpec((S, HEAD_DIM), lambda h: (0, h))
    smem = pl.BlockSpec(memory_space=pltpu.SMEM)
    return pl.pallas_call(
        body, grid=(H,), name="ret_bwd",
        in_specs=[smem, smem] + _head_specs(S, (3, 4, 5), H) + [hspec],
        out_specs=[hspec, hspec, hspec, pl.BlockSpec((1, 8, HEAD_DIM), lambda h: (h, 0, 0))],
        out_shape=[SDS((S, H * HEAD_DIM), BF16)] * 3 + [SDS((H, 8, HEAD_DIM), F32)],
        scratch_shapes=[pltpu.VMEM((2, nc, HEAD_DIM, HEAD_DIM), F32), pltpu.VMEM((2, nc, HEAD_DIM, HEAD_DIM), F32)],
        compiler_params=_cp(1))(dec_f, dec_b, proj, proj, proj, d_out)


def _ffn_bwd_act(dh2, wd, g, u):
    S, D = dh2.shape
    nblk, _, FB = g.shape
    tm = min(1024, S)

    def body(dh_ref, wd_ref, g_ref, u_ref, dg_ref, du_ref):
        dact = _dot(dh_ref[...], wd_ref[...], _NT)
        gg = g_ref[0].astype(F32)
        sg = _sigmoid(gg)
        dg_ref[0] = (dact * u_ref[0].astype(F32) * (sg * (1.0 + gg * (1.0 - sg)))).astype(BF16)
        du_ref[0] = (dact * (gg * sg)).astype(BF16)

    blk = pl.BlockSpec((1, tm, FB), lambda j, i: (j, i, 0))
    return pl.pallas_call(
        body, grid=(nblk, S // tm), name="ffn_bwd_act",
        in_specs=[pl.BlockSpec((tm, D), lambda j, i: (i, 0)), pl.BlockSpec((FB, D), lambda j, i: (j, 0)), blk, blk],
        out_specs=[blk, blk], out_shape=[SDS((nblk, S, FB), BF16)] * 2,
        compiler_params=_cp(2))(dh2, wd, g, u)


def _ffn_bwd_in(dg, du, wg, wu, h1, dh2, w_norm):
    nblk, S, FB = dg.shape
    D = h1.shape[1]
    tm = min(RESIDENT_ROWS, S)
    blk = pl.BlockSpec((nblk, tm, FB), lambda i: (0, i, 0))
    row = pl.BlockSpec((tm, D), lambda i: (i, 0))
    vec = pl.BlockSpec((1, D), lambda i: (0, 0))

    def gate_body(dg_ref, wg_ref, part_ref):
        part_ref[...] = _blocked_matmul(dg_ref, wg_ref)

    part = pl.pallas_call(
        gate_body, grid=(S // tm,), name="ffn_bwd_in_gate", in_specs=[blk, _resident((nblk * FB, D))],
        out_specs=row, out_shape=SDS((S, D), F32), compiler_params=_cp(1))(dg, wg.reshape(nblk * FB, D))

    def body(du_ref, wu_ref, part_ref, h_ref, dh2_ref, wn_ref, dh_ref, dhb_ref, dw_ref):
        @pl.when(pl.program_id(0) == 0)
        def _():
            dw_ref[...] = jnp.zeros_like(dw_ref)

        dh, dw = _rms_bwd(part_ref[...] + _blocked_matmul(du_ref, wu_ref), h_ref[...], wn_ref[...])
        dh = dh2_ref[...] + dh
        dh_ref[...] = dh
        dhb_ref[...] = dh.astype(BF16)
        dw_ref[...] += dw

    return pl.pallas_call(
        body, grid=(S // tm,), name="ffn_bwd_in",
        in_specs=[blk, _resident((nblk * FB, D)), row, row, row, vec],
        out_specs=[row, row, vec], out_shape=[SDS((S, D), F32), SDS((S, D), BF16), SDS((1, D), F32)],
        compiler_params=_cp(1))(du, wu.reshape(nblk * FB, D), part, h1, dh2, w_norm)


def _dmix(dh1, w_out):
    S, D = dh1.shape
    tm = min(512, S)

    def body(dh_ref, w_ref, o_ref):
        o_ref[...] = _dot(dh_ref[...], w_ref[...], _NT)

    row = pl.BlockSpec((tm, D), lambda i: (i, 0))
    return pl.pallas_call(
        body, grid=(S // tm,), name="dmix", in_specs=[row, pl.BlockSpec((D, D), lambda i: (0, 0))],
        out_specs=row, out_shape=SDS((S, D), F32), compiler_params=_cp(1))(dh1, w_out)


def _in_bwd(dproj, w_blk, x, dh1, w_norm):
    S, D = x.shape
    nblk, _, NB = w_blk.shape
    tm = min(RESIDENT_ROWS, S)

    def body(dp_ref, w_ref, x_ref, dh1_ref, wn_ref, dx_ref, dw_ref):
        @pl.when(pl.program_id(0) == 0)
        def _():
            dw_ref[...] = jnp.zeros_like(dw_ref)

        dn = None
        for j in range(nblk):
            part = _dot(dp_ref[:, pl.ds(j * NB, NB)], w_ref[j], _NT)
            dn = part if dn is None else dn + part
        dh, dw = _rms_bwd(dn, x_ref[...], wn_ref[...])
        dx_ref[...] = dh1_ref[...] + dh
        dw_ref[...] += dw

    row = pl.BlockSpec((tm, D), lambda i: (i, 0))
    vec = pl.BlockSpec((1, D), lambda i: (0, 0))
    return pl.pallas_call(
        body, grid=(S // tm,), name="in_bwd",
        in_specs=[pl.BlockSpec((tm, nblk * NB), lambda i: (i, 0)),
                  pl.BlockSpec((nblk, D, NB), lambda i: (0, 0, 0), pipeline_mode=pl.Buffered(1)), row, row, vec],
        out_specs=[row, vec], out_shape=[SDS((S, D), F32), SDS((1, D), F32)],
        compiler_params=_cp(1))(dproj, w_blk, x, dh1, w_norm)


def _wgrad(a, b, a_spec, b_spec, o_spec, o_shape, grid, name):
    nk = grid[-1]

    def ld(ref):
        return ref[0] if len(ref.shape) == 3 else ref[...]

    def body(a_ref, b_ref, o_ref, acc):
        k = pl.program_id(len(grid) - 1)

        @pl.when(k == 0)
        def _():
            acc[...] = jnp.zeros_like(acc)

        acc[...] += _dot(ld(a_ref), ld(b_ref), _TN)

        @pl.when(k == nk - 1)
        def _():
            if len(o_ref.shape) == 3:
                o_ref[0] = acc[...].astype(o_ref.dtype)
            else:
                o_ref[...] = acc[...].astype(o_ref.dtype)

    return pl.pallas_call(
        body, grid=grid, name=name, in_specs=[a_spec, b_spec], out_specs=o_spec, out_shape=SDS(o_shape, BF16),
        scratch_shapes=[pltpu.VMEM(o_spec.block_shape[-2:], F32)], compiler_params=_cp(len(grid)))(a, b)


def _peer(k):
    x, y, c = lax.axis_index("x"), lax.axis_index("y"), lax.axis_index("c")
    px = 1 - x if k & 4 else x
    py = 1 - y if k & 2 else y
    pc = 1 - c if k & 1 else c
    return (px, py, pc), 4 * px + 2 * py + pc


def _exchange_copies(srcs, lands, send_sems, recv_sems, which, gather):
    _, me = _peer(0)
    pairs = []
    for pos, a in enumerate(which):
        for k in range(1, N_DEV):
            dev, idx = _peer(k)
            sem = pos * (N_DEV - 1) + k - 1
            src = srcs[a] if gather else srcs[a].at[idx]
            mk = functools.partial(pltpu.make_async_remote_copy, src_ref=src, send_sem=send_sems.at[sem],
                                   recv_sem=recv_sems.at[sem], device_id=dev, device_id_type=MESH)
            pairs.append((mk(dst_ref=lands[a].at[me]), mk(dst_ref=lands[a].at[idx])))
    return pairs


def _sequencer_kernel(name, collective_id, n_remote, n_local):
    return pl.kernel(mesh=plsc.ScalarSubcoreMesh(axis_name="sequencer", num_cores=1), name=name,
                     scratch_types=(pltpu.SemaphoreType.DMA((n_remote,)), pltpu.SemaphoreType.DMA((n_remote,)),
                                    pltpu.SemaphoreType.DMA((n_local,))),
                     compiler_params=pltpu.CompilerParams(collective_id=collective_id))


def _handshake(ks):
    barrier = pltpu.get_barrier_semaphore()
    for k in ks:
        pl.semaphore_signal(barrier, inc=1, device_id=_peer(k)[0], device_id_type=MESH)
    pl.semaphore_wait(barrier, len(ks))


def _sequencer_scatter(arrays, name, collective_id):
    n = len(arrays)
    hbm = pltpu.MemorySpace.HBM
    srcs = [jax.new_ref(a, memory_space=hbm) for a in arrays]
    lands = [jax.empty_ref(SDS(a.shape, a.dtype), memory_space=hbm) for a in arrays]

    @_sequencer_kernel(name, collective_id, n * (N_DEV - 1), n)
    def launch(send_sems, recv_sems, local_sems):
        _handshake(range(1, N_DEV))
        _, me = _peer(0)
        local = [pltpu.make_async_copy(srcs[a].at[me], lands[a].at[me], local_sems.at[a]) for a in range(n)]
        pairs = _exchange_copies(srcs, lands, send_sems, recv_sems, range(n), False)
        for out, _ in pairs:
            out.start()
        for cp in local:
            cp.start()
        for out, arrival in pairs:
            out.wait_send()
            arrival.wait_recv()
        for cp in local:
            cp.wait()

    launch()
    return [r[...] for r in lands]


SIBLING = 1
OTHER_CHIPS = (2, 4, 6)


def _sequencer_gather(arrays, name, collective_id):
    n = len(arrays)
    hbm = pltpu.MemorySpace.HBM
    srcs = [jax.new_ref(a, memory_space=hbm) for a in arrays]
    lands = [jax.empty_ref(SDS((N_DEV,) + a.shape, a.dtype), memory_space=hbm) for a in arrays]

    @_sequencer_kernel(name, collective_id, n * (N_DEV - 1), n)
    def launch(send_sems, recv_sems, local_sems):
        _handshake((SIBLING,) + OTHER_CHIPS)
        _, me = _peer(0)
        sibling, _ = _peer(SIBLING)

        def copy(a, k, src, block, to):
            sem = a * (N_DEV - 1) + k - 1
            return pltpu.make_async_remote_copy(src_ref=src, dst_ref=lands[a].at[block], send_sem=send_sems.at[sem],
                                                recv_sem=recv_sems.at[sem], device_id=to, device_id_type=MESH)

        local = [pltpu.make_async_copy(srcs[a], lands[a].at[me], local_sems.at[a]) for a in range(n)]
        first = [copy(a, k, srcs[a], me, _peer(k)[0]) for a in range(n) for k in OTHER_CHIPS + (SIBLING,)]
        for cp in first + local:
            cp.start()
        passed = []
        for a in range(n):
            for k in OTHER_CHIPS:
                _, block = _peer(k)
                copy(a, k, srcs[a], block, sibling).wait_recv()
                passed.append(copy(a, k ^ SIBLING, lands[a].at[block], block, sibling))
                passed[-1].start()
        for a in range(n):
            for k in (SIBLING,) + tuple(k ^ SIBLING for k in OTHER_CHIPS):
                copy(a, k, srcs[a], _peer(k)[1], sibling).wait_recv()
        for cp in first + passed:
            cp.wait_send()
        for cp in local:
            cp.wait()

    launch()
    return [r[...] for r in lands]


SMALL_ROWS = 64


def _small_step(part, w, m, v):
    def body(p_ref, w_ref, m_ref, v_ref, g_ref, d_ref, nm_ref, nv_ref, gath, send_sems, recv_sems):
        _, me = _peer(0)
        gath[me] = p_ref[...]
        copies = []
        for k in range(1, N_DEV):
            dev, idx = _peer(k)
            out = pltpu.make_async_remote_copy(src_ref=p_ref, dst_ref=gath.at[me], send_sem=send_sems.at[k - 1],
                                               recv_sem=recv_sems.at[k - 1], device_id=dev, device_id_type=MESH)
            out.start()
            arrival = pltpu.make_async_remote_copy(src_ref=p_ref, dst_ref=gath.at[idx], send_sem=send_sems.at[k - 1],
                                                   recv_sem=recv_sems.at[k - 1], device_id=dev, device_id_type=MESH)
            copies.append((out, arrival))
        for out, arrival in copies:
            out.wait_send()
            arrival.wait_recv()
        g = gath[0]
        for p in range(1, N_DEV):
            g = g + gath[p]
        g_ref[...] = g
        d_ref[...], nm_ref[...], nv_ref[...] = _adamw(w_ref[...], g, m_ref[...], v_ref[...])

    vm = pl.BlockSpec(memory_space=pltpu.VMEM)
    return pl.pallas_call(
        body, name="small_step", in_specs=[vm] * 4, out_specs=[vm] * 4,
        out_shape=[SDS((SMALL_ROWS, 128), F32)] * 4,
        scratch_shapes=[pltpu.VMEM((N_DEV, SMALL_ROWS, 128), F32), pltpu.SemaphoreType.DMA((N_DEV - 1,)),
                        pltpu.SemaphoreType.DMA((N_DEV - 1,))])(part, w, m, v)


def _adamw(w, g, m, v):
    m = ADAM_B1 * m + (1.0 - ADAM_B1) * g
    v = ADAM_B2 * v + (1.0 - ADAM_B2) * (g * g)
    m_hat = m / (1.0 - ADAM_B1 ** ADAM_STEP)
    v_hat = v / (1.0 - ADAM_B2 ** ADAM_STEP)
    delta = -ADAM_LR * (m_hat / (jnp.sqrt(v_hat) + ADAM_EPS) + ADAM_WD * w)
    return delta, m, v


def _adamw_block(parts, w, m, v, name):
    R, C = w.shape
    n_parts = len(parts)
    Rp = R // n_parts
    tr = next(t for t in (256, 128, 64, 32, 16, 8) if Rp % t == 0 and t * C <= 256 * 1024)
    per_part = Rp // tr

    def body(*refs):
        p_refs = refs[:n_parts]
        w_ref, m_ref, v_ref, g_ref, d_ref, nm_ref, nv_ref = refs[n_parts:]
        for k, p_ref in enumerate(p_refs):
            @pl.when(pl.program_id(0) // per_part == k)
            def _(p_ref=p_ref):
                g = p_ref[0].astype(F32)
                for p in range(1, N_DEV):
                    g = g + p_ref[p].astype(F32)
                g_ref[...] = g
                d_ref[...], nm_ref[...], nv_ref[...] = _adamw(w_ref[...], g, m_ref[...], v_ref[...])

    row = pl.BlockSpec((tr, C), lambda i: (i, 0))
    part_specs = [pl.BlockSpec((N_DEV, tr, C), functools.partial(
        lambda i, k: (0, jnp.clip(i - k * per_part, 0, per_part - 1), 0), k=k)) for k in range(n_parts)]
    return pl.pallas_call(
        body, grid=(R // tr,), name=name, in_specs=part_specs + [row, row, row],
        out_specs=[row] * 4, out_shape=[SDS((R, C), F32)] * 4, compiler_params=_cp(1))(*parts, w, m, v)


def _pack_small(mix, ffn, fin, retw, dec_f, dec_b, loss):
    flat = jnp.concatenate([mix.reshape(-1), ffn.reshape(-1), fin.reshape(-1), retw.reshape(-1), dec_f.reshape(-1),
                            dec_b.reshape(-1), loss.reshape(-1)])
    return jnp.pad(flat, (0, SMALL_ROWS * 128 - flat.shape[0])).reshape(SMALL_ROWS, 128)


def _unpack_small(packed, shapes):
    flat = packed.reshape(-1)
    out, at = [], 0
    for s in shapes:
        n = math.prod(s)
        out.append(flat[at:at + n].reshape(s))
        at += n
    return out


def kernel(x, norm_mix_w, w_in, ret_decay_fwd, ret_decay_bwd, ret_norm_w, w_out, norm_ffn_w, w_gate, w_up, w_down, norm_final_w, loss_target, m_norm_mix_w, m_w_in, m_ret_decay_fwd, m_ret_decay_bwd, m_ret_norm_w, m_w_out, m_norm_ffn_w, m_w_gate, m_w_up, m_w_down, m_norm_final_w, v_norm_mix_w, v_w_in, v_ret_decay_fwd, v_ret_decay_bwd, v_ret_norm_w, v_w_out, v_norm_ffn_w, v_w_gate, v_w_up, v_w_down, v_norm_final_w):
    x2 = x[0]
    tgt = loss_target[0]
    S, D = x2.shape
    H = ret_norm_w.shape[1] // HEAD_DIM
    DA = H * HEAD_DIM
    fin_w = norm_final_w.reshape(1, D)
    big = (w_in[0], w_out[0], w_gate[0].T, w_up[0].T, w_down[0])

    big_b = [w.astype(BF16) for w in big]
    wi, = _sequencer_gather(big_b[:1], "gather_in", 0)
    wo, wg, wu = _sequencer_gather(big_b[1:4], "gather_mid", 1)
    wd, = _sequencer_gather(big_b[4:], "gather_down", 5)
    NB = wi.shape[2]

    proj, n1 = _proj_fwd(x2, norm_mix_w, wi)
    bias = _attn_bias()[:H]
    attn, lse = _attn_fwd(proj, bias)
    ret, o_raw = _ret_fwd(proj, ret_decay_fwd, ret_decay_bwd, ret_norm_w)
    wo_full = wo.reshape(D, D)
    d_ff = N_DEV * wd.shape[1]
    FB = FFN_BLOCK if d_ff % FFN_BLOCK == 0 else wd.shape[1]
    n_fb = d_ff // FB
    wg, wu = wg.reshape(n_fb, FB, D), wu.reshape(n_fb, FB, D)
    wd_full = wd.reshape(d_ff, D)
    h1, mixed, n2 = _out_fwd(x2, attn, ret, wo_full, norm_ffn_w)
    gate, up, act = _ffn_up(n2, wg, wu)
    dh2, dh2_b, loss_parts, g_fin = _ffn_down_loss(act, wd_full, h1, tgt, fin_w)

    dgate, dup = _ffn_bwd_act(dh2_b, wd_full, gate, up)
    tn = min(1024, D)
    ffn_specs = (pl.BlockSpec((1, S, FB), lambda j, n, k: (j, 0, 0)), pl.BlockSpec((S, tn), lambda j, n, k: (0, n)),
                 pl.BlockSpec((1, FB, tn), lambda j, n, k: (j, 0, n)), (n_fb, FB, D), (n_fb, D // tn, 1))
    per_dev = (N_DEV, d_ff // N_DEV, D)
    g_wd = _wgrad(act, dh2_b, *ffn_specs, "wgrad_down").reshape(per_dev)
    g_wg = _wgrad(dgate, n2, *ffn_specs, "wgrad_gate").reshape(per_dev)
    g_wu = _wgrad(dup, n2, *ffn_specs, "wgrad_up").reshape(per_dev)
    parts_f = _sequencer_scatter([g_wg, g_wu, g_wd], "scatter_ffn", 2)
    dh1, dh1_b, g_ffn = _ffn_bwd_in(dgate, dup, wg, wu, h1, dh2, norm_ffn_w)
    dmix = _dmix(dh1_b, wo_full)
    tmw = min(512, D)
    tk = min(2048, S)
    g_wo = _wgrad(mixed, dh1_b, pl.BlockSpec((tk, tmw), lambda m, k: (k, m)), pl.BlockSpec((tk, D), lambda m, k: (k, 0)),
                  pl.BlockSpec((tmw, D), lambda m, k: (m, 0)), (D, D), (D // tmw, S // tk), "wgrad_out")
    parts_o = _sequencer_scatter([g_wo.reshape(N_DEV, D // N_DEV, D)], "scatter_out", 3)
    d_ret, dg_r, small_w = _ret_gate_bwd(proj, o_raw, dmix, ret_norm_w, DA)
    dq_r, dk_r, dv_r, small = _ret_bwd(proj, d_ret, ret_decay_fwd, ret_decay_bwd)
    dq_a, dk_a, dv_a = _attn_bwd(proj, attn, lse, dmix, bias)
    dproj = jnp.concatenate([t.astype(BF16) for t in (dq_a, dk_a, dv_a, dq_r, dk_r, dv_r, dg_r)], axis=1)
    half = D // tmw // 2
    parts_i = []
    for part, (name, cid) in enumerate((("in_lo", 4), ("in_hi", 6))):
        g_wi = _wgrad(n1, dproj, pl.BlockSpec((S, tmw), functools.partial(lambda j, m, k, off: (0, m + off), off=part * half)),
                      pl.BlockSpec((S, NB), lambda j, m, k: (0, j)), pl.BlockSpec((1, tmw, NB), lambda j, m, k: (j, m, 0)),
                      (N_DEV, D // 2, NB), (N_DEV, half, 1), "wgrad_" + name)
        parts_i += _sequencer_scatter([g_wi], "scatter_" + name, cid)
    grad_x, g_mix = _in_bwd(dproj, wi, x2, dh1, norm_mix_w)

    big_m = (m_w_in[0], m_w_out[0], m_w_gate[0].T, m_w_up[0].T, m_w_down[0])
    big_v = (v_w_in[0], v_w_out[0], v_w_gate[0].T, v_w_up[0].T, v_w_down[0])
    names = ("adamw_in", "adamw_out", "adamw_gate", "adamw_up", "adamw_down")
    upd = [None] * 5
    for a, p in zip((2, 3, 4, 1, 0), [[t] for t in parts_f + parts_o] + [parts_i]):
        upd[a] = _adamw_block(p, big[a], big_m[a], big_v[a], names[a])

    g_dec_f = small[:, 0, 0].reshape(1, H)
    g_dec_b = small[:, 1, 0].reshape(1, H)
    g_retw = small_w[:, 0, :].reshape(1, DA)
    loss_local = jnp.sum(loss_parts[::8, 0])
    zero = jnp.zeros((1,), F32)
    part = _pack_small(g_mix, g_ffn, g_fin, g_retw, g_dec_f, g_dec_b, loss_local)
    sw = _pack_small(norm_mix_w, norm_ffn_w, norm_final_w, ret_norm_w, ret_decay_fwd, ret_decay_bwd, zero)
    sm = _pack_small(m_norm_mix_w, m_norm_ffn_w, m_norm_final_w, m_ret_norm_w, m_ret_decay_fwd, m_ret_decay_bwd, zero)
    sv = _pack_small(v_norm_mix_w, v_norm_ffn_w, v_norm_final_w, v_ret_norm_w, v_ret_decay_fwd, v_ret_decay_bwd, zero)
    shapes = [(1, D), (1, D), (D,), (1, DA), (1, H), (1, H), ()]
    sg, sd, snm, snv = [_unpack_small(t, shapes) for t in _small_step(part, sw, sm, sv)]
    loss = sg[6]

    def ordered(small_set, k):
        b = [(u[k].T if a in (2, 3) else u[k])[None] for a, u in enumerate(upd)]
        return [small_set[0], b[0], small_set[4], small_set[5], small_set[3], b[1], small_set[1], b[2], b[3], b[4],
                small_set[2]]

    return (loss, grad_x[None], *ordered(sg, 0), *ordered(sd, 1), *ordered(snm, 2), *ordered(snv, 3))
```

```python
import functools
import math

import numpy as np
import jax
import jax.numpy as jnp
from jax import lax
from jax.experimental import pallas as pl
from jax.experimental.pallas import tpu as pltpu
from jax.experimental.pallas import tpu_sc as plsc

F32 = jnp.float32
BF16 = jnp.bfloat16
SDS = jax.ShapeDtypeStruct

HEAD_DIM = 128
EPS = 1e-6
RET_CHUNK = 128
DILATIONS = (1, 4, 16)
BAND = 64
Q_TILE = 128
K_TILE = Q_TILE + 2 * BAND
KV_PAD = BAND * 4
TILE_GROUP = 8
NEG = -1e30
N_DEV = 8
N_GROUPS = 7
ADAM_LR, ADAM_B1, ADAM_B2, ADAM_EPS, ADAM_WD, ADAM_STEP = 0.001, 0.9, 0.999, 1e-08, 0.01, 10
VMEM_LIMIT = 56 * 1024 * 1024
MESH = pl.DeviceIdType.MESH
ANY = pl.BlockSpec(memory_space=pl.ANY)


def _cp(n_grid):
    return pltpu.CompilerParams(dimension_semantics=("arbitrary",) * n_grid, vmem_limit_bytes=VMEM_LIMIT)


def _sigmoid(x):
    return 1.0 / (1.0 + jnp.exp(-x))


def _rms_scale(h):
    return lax.rsqrt(jnp.mean(h * h, axis=-1, keepdims=True) + EPS)


def _rms_bwd(dn, h, w):
    r = _rms_scale(h)
    gw = dn * w
    dh = r * gw - h * (r * r * r) * jnp.mean(gw * h, axis=-1, keepdims=True)
    return dh, jnp.sum(dn * h * r, axis=0, keepdims=True)


def _dot(a, b, dims):
    return lax.dot_general(a.astype(BF16), b.astype(BF16), (dims, ((), ())), preferred_element_type=F32)


_NN = ((1,), (0,))
_NT = ((1,), (1,))
_TN = ((0,), (0,))


RESIDENT_ROWS = 256


def _resident(shape):
    return pl.BlockSpec(shape, lambda i: (0, 0), pipeline_mode=pl.Buffered(1))


def _blocked_matmul(a_ref, w_ref):
    nblk, _, fb = a_ref.shape
    out = None
    for j in range(nblk):
        part = jnp.dot(a_ref[j], w_ref[pl.ds(j * fb, fb), :], preferred_element_type=F32)
        out = part if out is None else out + part
    return out


def _proj_fwd(x, w_norm, w_blk):
    S, D = x.shape
    nblk, _, NB = w_blk.shape
    tm = min(1024, S)

    def body(x_ref, wn_ref, w_ref, proj_ref, n_ref, n_scr):
        @pl.when(pl.program_id(1) == 0)
        def _():
            xf = x_ref[...]
            nb = (xf * _rms_scale(xf) * wn_ref[...]).astype(BF16)
            n_scr[...] = nb
            n_ref[...] = nb
        proj_ref[...] = jnp.dot(n_scr[...], w_ref[0], preferred_element_type=F32)

    return pl.pallas_call(
        body, grid=(S // tm, nblk), name="proj_fwd",
        in_specs=[pl.BlockSpec((tm, D), lambda i, j: (i, 0)), pl.BlockSpec((1, D), lambda i, j: (0, 0)),
                  pl.BlockSpec((1, D, NB), lambda i, j: (j, 0, 0))],
        out_specs=[pl.BlockSpec((tm, NB), lambda i, j: (i, j)), pl.BlockSpec((tm, D), lambda i, j: (i, 0))],
        out_shape=[SDS((S, nblk * NB), F32), SDS((S, D), BF16)],
        scratch_shapes=[pltpu.VMEM((tm, D), BF16)], compiler_params=_cp(2))(x, w_norm, w_blk)


def _norm_fwd(x, w_norm):
    S, D = x.shape
    tm = min(1024, S)

    def body(x_ref, wn_ref, n_ref):
        xf = x_ref[...]
        n_ref[...] = (xf * _rms_scale(xf) * wn_ref[...]).astype(BF16)

    row = pl.BlockSpec((tm, D), lambda i: (i, 0))
    return pl.pallas_call(body, grid=(S // tm,), name="norm_fwd", in_specs=[row, pl.BlockSpec((1, D), lambda i: (0, 0))],
                          out_specs=row, out_shape=SDS((S, D), BF16), compiler_params=_cp(1))(x, w_norm)


def _proj_part(n1, w_pairs, chip_ids, proj, n_blocks, name):
    S, D = n1.shape
    P2, _, NB = w_pairs.shape
    tm = min(1024, S)

    def body(ids_ref, n_ref, w_ref, *rest):
        rest[-1][...] = jnp.dot(n_ref[...], w_ref[0], preferred_element_type=F32)

    out_spec = pl.BlockSpec((tm, NB), lambda i, j, ids: (i, 2 * ids[j // 2] + j % 2))
    in_specs = [pl.BlockSpec((tm, D), lambda i, j, ids: (i, 0)), pl.BlockSpec((1, D, NB), lambda i, j, ids: (j, 0, 0))]
    args = [n1, w_pairs]
    if proj is not None:
        in_specs.append(ANY)
        args.append(proj)
    return pl.pallas_call(
        body, name=name, out_shape=SDS((S, n_blocks * NB), F32),
        grid_spec=pltpu.PrefetchScalarGridSpec(num_scalar_prefetch=1, grid=(S // tm, P2), in_specs=in_specs,
                                               out_specs=out_spec),
        input_output_aliases={} if proj is None else {3: 0},
        compiler_params=_cp(2))(chip_ids, *args)


def _out_fwd(x, attn, ret, w_out, w_norm):
    S, D = x.shape
    DA = attn.shape[1]
    tm = min(256, S)

    def body(x_ref, a_ref, r_ref, w_ref, wn_ref, h_ref, mix_ref, n_ref):
        a = a_ref[...].astype(BF16)
        r = r_ref[...].astype(BF16)
        mix_ref[:, :DA] = a
        mix_ref[:, DA:] = r
        h = x_ref[...] + jnp.dot(a, w_ref[:DA, :], preferred_element_type=F32) \
            + jnp.dot(r, w_ref[DA:, :], preferred_element_type=F32)
        h_ref[...] = h
        n_ref[...] = (h * _rms_scale(h) * wn_ref[...]).astype(BF16)

    row = lambda w: pl.BlockSpec((tm, w), lambda i: (i, 0))
    return pl.pallas_call(
        body, grid=(S // tm,), name="out_fwd",
        in_specs=[row(D), row(DA), row(D - DA), pl.BlockSpec((D, D), lambda i: (0, 0)),
                  pl.BlockSpec((1, D), lambda i: (0, 0))],
        out_specs=[row(D), row(D), row(D)],
        out_shape=[SDS((S, D), F32), SDS((S, D), BF16), SDS((S, D), BF16)],
        compiler_params=_cp(1))(x, attn, ret, w_out, w_norm)


def _ffn_up(n2, wg, wu):
    S, D = n2.shape
    nblk, FB, _ = wg.shape
    tm = min(1024, S)

    def body(n_ref, wg_ref, wu_ref, g_ref, u_ref, a_ref):
        n = n_ref[...]
        g = _dot(n, wg_ref[0], _NT)
        u = _dot(n, wu_ref[0], _NT)
        g_ref[0] = g.astype(BF16)
        u_ref[0] = u.astype(BF16)
        a_ref[0] = (g * _sigmoid(g) * u).astype(BF16)

    wspec = pl.BlockSpec((1, FB, D), lambda j, i: (j, 0, 0))
    ospec = pl.BlockSpec((1, tm, FB), lambda j, i: (j, i, 0))
    return pl.pallas_call(
        body, grid=(nblk, S // tm), name="ffn_up",
        in_specs=[pl.BlockSpec((tm, D), lambda j, i: (i, 0)), wspec, wspec],
        out_specs=[ospec, ospec, ospec],
        out_shape=[SDS((nblk, S, FB), BF16)] * 3,
        compiler_params=_cp(2))(n2, wg, wu)


def _ffn_down_loss(act, wd, h1, target, w_norm):
    nblk, S, FB = act.shape
    D = h1.shape[1]
    tm = min(RESIDENT_ROWS, S)

    def body(a_ref, wd_ref, h_ref, t_ref, wn_ref, dh_ref, dhb_ref, loss_ref, dw_ref):
        @pl.when(pl.program_id(0) == 0)
        def _():
            dw_ref[...] = jnp.zeros_like(dw_ref)

        h = h_ref[...] + _blocked_matmul(a_ref, wd_ref)
        w = wn_ref[...]
        err = h * _rms_scale(h) * w - t_ref[...]
        loss_ref[...] = jnp.full(loss_ref.shape, 0.5 * jnp.sum(err * err) / D, F32)
        dh, dw = _rms_bwd(err * (1.0 / D), h, w)
        dh_ref[...] = dh
        dhb_ref[...] = dh.astype(BF16)
        dw_ref[...] += dw

    row = pl.BlockSpec((tm, D), lambda i: (i, 0))
    vec = pl.BlockSpec((1, D), lambda i: (0, 0))
    return pl.pallas_call(
        body, grid=(S // tm,), name="ffn_down_loss",
        in_specs=[pl.BlockSpec((nblk, tm, FB), lambda i: (0, i, 0)), _resident((nblk * FB, D)), row, row, vec],
        out_specs=[row, row, pl.BlockSpec((8, 128), lambda i: (i, 0)), vec],
        out_shape=[SDS((S, D), F32), SDS((S, D), BF16), SDS((S // tm * 8, 128), F32), SDS((1, D), F32)],
        compiler_params=_cp(1))(act, wd, h1, target, w_norm)


def _attn_bias():
    n_heads = 8
    slopes = np.exp2(-8.0 * np.arange(1, n_heads + 1, dtype=np.float32) / n_heads)
    dist = np.abs(np.arange(K_TILE)[None, :] - BAND - np.arange(Q_TILE)[:, None])
    out = np.empty((n_heads, len(DILATIONS), Q_TILE, K_TILE), np.float32)
    for h in range(n_heads):
        for p, d in enumerate(DILATIONS):
            out[h, p] = np.where(dist <= BAND, -slopes[h] * (d * dist).astype(np.float32), NEG)
    return jnp.asarray(out)


def _attn_tiles(S, d):
    L = S // d
    per_class = L // Q_TILE
    return L, per_class, d * per_class


def _tile_rows(t, d, per_class):
    r = t // per_class
    a = (t % per_class) * Q_TILE
    q_rows = pl.ds(r + d * a, Q_TILE, stride=d) if d > 1 else pl.ds(pl.multiple_of(a, Q_TILE), Q_TILE)
    k_rows = pl.ds(KV_PAD + r + d * (a - BAND), K_TILE, stride=d) if d > 1 else pl.ds(
        pl.multiple_of(KV_PAD + a - BAND, BAND), K_TILE)
    return a, q_rows, k_rows


def _to_quarters(dst, src, n, dst_off=0):
    for r in range(4):
        dst[pl.ds(dst_off + r * (n // 4), n // 4), :] = src[pl.ds(r, n // 4, stride=4), :]


def _quarter_tile_rows(t, S):
    L = S // 16
    per_class = L // Q_TILE
    blk, tt = t // (4 * per_class), t % (4 * per_class)
    r, a = tt // per_class, (tt % per_class) * Q_TILE
    q_rows = pl.ds(blk * (S // 4) + r + 4 * a, Q_TILE, stride=4)
    k_rows = pl.ds(KV_PAD + blk * (S // 4) + r + 4 * (a - BAND), K_TILE, stride=4)
    return a, q_rows, k_rows


def _lanes(x, width):
    return jnp.concatenate([x] * (width // HEAD_DIM), axis=-1)


_BNT = (((2,), (2,)), ((0,), (0,)))
_BNN = (((2,), (1,)), ((0,), (0,)))
_BTN = (((1,), (1,)), ((0,), (0,)))


def _bdot(a, b, dims):
    return lax.dot_general(a, b, dims, preferred_element_type=F32)


def _stacked(rows, loaders):
    return [jnp.stack([f(*r) for r in rows]) for f in loaders]


def _edge_mask(a, L):
    lk = lax.broadcasted_iota(jnp.int32, (1, K_TILE), 1) + (a - BAND)
    return jnp.where((lk >= 0) & (lk < L), 0.0, NEG).astype(F32)


def _fill_padded(dst, src, S):
    dst[pl.ds(0, KV_PAD), :] = jnp.zeros((KV_PAD, HEAD_DIM), F32)
    dst[pl.ds(KV_PAD + S, KV_PAD), :] = jnp.zeros((KV_PAD, HEAD_DIM), F32)
    dst[pl.ds(KV_PAD, S), :] = src[...]


def _head_specs(S, groups, n_heads):
    return [pl.BlockSpec((S, HEAD_DIM), functools.partial(lambda h, g: (0, g * n_heads + h), g=g)) for g in groups]


def _attn_fwd(proj, bias):
    S = proj.shape[0]
    H = proj.shape[1] // (N_GROUPS * HEAD_DIM)
    scale = HEAD_DIM ** -0.5

    def body(q_ref, k_ref, v_ref, b_ref, o_ref, lse_ref, kp, vp, m_run, l_run, q4, m3, l3, acc3):
        _fill_padded(kp, k_ref, S)
        _fill_padded(vp, v_ref, S)
        o_ref[...] = jnp.zeros_like(o_ref)
        m_run[...] = jnp.full(m_run.shape, NEG, F32)
        l_run[...] = jnp.zeros_like(l_run)
        for p, d in enumerate(DILATIONS[:2]):
            L, per_class, n_tiles = _attn_tiles(S, d)

            def tiles(t, carry, p=p, d=d, L=L, per_class=per_class, n_tiles=n_tiles):
                rows = [_tile_rows(t + u * (n_tiles // TILE_GROUP), d, per_class) for u in range(TILE_GROUP)]
                qs, ks, vs, m_old, l_old, o_old, edge = _stacked(rows, (
                    lambda a, qr, kr: q_ref[qr, :].astype(BF16), lambda a, qr, kr: kp[kr, :].astype(BF16),
                    lambda a, qr, kr: vp[kr, :].astype(BF16), lambda a, qr, kr: m_run[qr, :],
                    lambda a, qr, kr: l_run[qr, :], lambda a, qr, kr: o_ref[qr, :], lambda a, qr, kr: _edge_mask(a, L)))
                s = _bdot(qs, ks, _BNT) * scale + b_ref[0, p][None] + edge
                m_new = jnp.maximum(m_old, jnp.max(s, axis=-1, keepdims=True))
                pr = jnp.exp(s - _lanes(m_new, K_TILE)).astype(BF16)
                alpha = jnp.exp(m_old - m_new)
                l_new = alpha * l_old + _bdot(pr, jnp.ones((TILE_GROUP, K_TILE, HEAD_DIM), BF16), _BNN)
                o_new = alpha * o_old + _bdot(pr, vs, _BNN)
                for u, (_, qr, _) in enumerate(rows):
                    o_ref[qr, :] = o_new[u]
                    m_run[qr, :] = m_new[u]
                    l_run[qr, :] = l_new[u]
                return carry

            lax.fori_loop(0, n_tiles // TILE_GROUP, tiles, 0)

        _to_quarters(q4, q_ref, S)
        _to_quarters(kp, k_ref, S, KV_PAD)
        _to_quarters(vp, v_ref, S, KV_PAD)
        n_tiles = _attn_tiles(S, DILATIONS[2])[2]

        def tiles3(t, carry):
            rows = [_quarter_tile_rows(t + u * (n_tiles // TILE_GROUP), S) for u in range(TILE_GROUP)]
            qs, ks, vs, edge = _stacked(rows, (
                lambda a, qr, kr: q4[qr, :].astype(BF16), lambda a, qr, kr: kp[kr, :].astype(BF16),
                lambda a, qr, kr: vp[kr, :].astype(BF16), lambda a, qr, kr: _edge_mask(a, S // DILATIONS[2])))
            s = _bdot(qs, ks, _BNT) * scale + b_ref[0, 2][None] + edge
            m_new = jnp.broadcast_to(jnp.max(s, axis=-1, keepdims=True), (TILE_GROUP, Q_TILE, HEAD_DIM))
            pr = jnp.exp(s - _lanes(m_new, K_TILE)).astype(BF16)
            l_new = _bdot(pr, jnp.ones((TILE_GROUP, K_TILE, HEAD_DIM), BF16), _BNN)
            o_new = _bdot(pr, vs, _BNN)
            for u, (_, qr, _) in enumerate(rows):
                acc3[qr, :] = o_new[u]
                m3[qr, :] = m_new[u]
                l3[qr, :] = l_new[u]
            return carry

        lax.fori_loop(0, n_tiles // TILE_GROUP, tiles3, 0)
        for r in range(4):
            nat, qtr = pl.ds(r, S // 4, stride=4), pl.ds(r * (S // 4), S // 4)
            m_a, m_b = m_run[nat, :], m3[qtr, :]
            m = jnp.maximum(m_a, m_b)
            w_a, w_b = jnp.exp(m_a - m), jnp.exp(m_b - m)
            l = w_a * l_run[nat, :] + w_b * l3[qtr, :]
            o_ref[nat, :] = (w_a * o_ref[nat, :] + w_b * acc3[qtr, :]) / l
            lse_ref[nat, :] = m + jnp.log(l)

    hspec = pl.BlockSpec((S, HEAD_DIM), lambda h: (0, h))
    padded, plain = pltpu.VMEM((S + 2 * KV_PAD, HEAD_DIM), F32), pltpu.VMEM((S, HEAD_DIM), F32)
    return pl.pallas_call(
        body, grid=(H,), name="attn_fwd",
        in_specs=_head_specs(S, (0, 1, 2), H) + [
            pl.BlockSpec((1, len(DILATIONS), Q_TILE, K_TILE), lambda h: (h, 0, 0, 0))],
        out_specs=[hspec, hspec],
        out_shape=[SDS((S, H * HEAD_DIM), F32), SDS((S, H * HEAD_DIM), F32)],
        scratch_shapes=[padded, padded] + [plain] * 6,
        compiler_params=_cp(1))(proj, proj, proj, bias)


def _attn_bwd(proj, out, lse, dmix, bias):
    S = proj.shape[0]
    H = proj.shape[1] // (N_GROUPS * HEAD_DIM)
    scale = HEAD_DIM ** -0.5
    assert S // DILATIONS[2] >= 2 * Q_TILE

    def body(q_ref, k_ref, v_ref, o_ref, lse_ref, do_ref, b_ref, dq_ref, dk_ref, dv_ref,
             kp, vp, dkp, dvp, dsum, q4, do4, lse4, dsum4):
        _fill_padded(kp, k_ref, S)
        _fill_padded(vp, v_ref, S)
        dkp[...] = jnp.zeros_like(dkp)
        dvp[...] = jnp.zeros_like(dvp)
        dq_ref[...] = jnp.zeros_like(dq_ref)
        dsum[...] = jnp.broadcast_to(jnp.sum(do_ref[...] * o_ref[...], axis=-1, keepdims=True), dsum.shape)

        def run(n_tiles, tile_rows, p, L, q_src, do_src, lse_src, dsum_src, dq_dst, dq_adds):
            def tiles(t, carry):
                rows = [tile_rows(t + u * (n_tiles // TILE_GROUP)) for u in range(TILE_GROUP)]
                qs, ks, vs, dos, lses, dsums, dk_old, dv_old, edge = _stacked(rows, (
                    lambda a, qr, kr: q_src[qr, :].astype(BF16), lambda a, qr, kr: kp[kr, :].astype(BF16),
                    lambda a, qr, kr: vp[kr, :].astype(BF16), lambda a, qr, kr: do_src[qr, :].astype(BF16),
                    lambda a, qr, kr: lse_src[qr, :], lambda a, qr, kr: dsum_src[qr, :],
                    lambda a, qr, kr: dkp[kr, :], lambda a, qr, kr: dvp[kr, :], lambda a, qr, kr: _edge_mask(a, L)))
                s = _bdot(qs, ks, _BNT) * scale + b_ref[0, p][None] + edge
                pr = jnp.exp(s - _lanes(lses, K_TILE))
                ds = (pr * (_bdot(dos, vs, _BNT) - _lanes(dsums, K_TILE)) * scale).astype(BF16)
                dq_new = _bdot(ds, ks, _BNN)
                if dq_adds:
                    dq_new = dq_new + jnp.stack([dq_dst[qr, :] for _, qr, _ in rows])
                dk_new = dk_old + _bdot(ds, qs, _BTN)
                dv_new = dv_old + _bdot(pr.astype(BF16), dos, _BTN)
                for u, (_, qr, kr) in enumerate(rows):
                    dq_dst[qr, :] = dq_new[u]
                    dkp[kr, :] = dk_new[u]
                    dvp[kr, :] = dv_new[u]
                return carry

            lax.fori_loop(0, n_tiles // TILE_GROUP, tiles, 0)

        for p, d in enumerate(DILATIONS[:2]):
            L, per_class, n_tiles = _attn_tiles(S, d)
            run(n_tiles, functools.partial(_tile_rows, d=d, per_class=per_class), p, L,
                q_ref, do_ref, lse_ref, dsum, dq_ref, True)
        dk_ref[...] = dkp[pl.ds(KV_PAD, S), :]
        dv_ref[...] = dvp[pl.ds(KV_PAD, S), :]

        for dst, src in ((q4, q_ref), (do4, do_ref), (lse4, lse_ref), (dsum4, dsum)):
            _to_quarters(dst, src, S)
        _to_quarters(kp, k_ref, S, KV_PAD)
        _to_quarters(vp, v_ref, S, KV_PAD)
        dkp[...] = jnp.zeros_like(dkp)
        dvp[...] = jnp.zeros_like(dvp)
        dq3 = dsum
        run(_attn_tiles(S, DILATIONS[2])[2], functools.partial(_quarter_tile_rows, S=S), 2, S // DILATIONS[2],
            q4, do4, lse4, dsum4, dq3, False)
        for r in range(4):
            nat, qtr = pl.ds(r, S // 4, stride=4), pl.ds(r * (S // 4), S // 4)
            pad_qtr = pl.ds(KV_PAD + r * (S // 4), S // 4)
            dq_ref[nat, :] = dq_ref[nat, :] + dq3[qtr, :]
            dk_ref[nat, :] = dk_ref[nat, :] + dkp[pad_qtr, :]
            dv_ref[nat, :] = dv_ref[nat, :] + dvp[pad_qtr, :]

    hspec = pl.BlockSpec((S, HEAD_DIM), lambda h: (0, h))
    once =pl.BlockSpec((S, HEAD_DIM), lambda h: (0, h), pipeline_mode=pl.Buffered(1))
    padded, plain = pltpu.VMEM((S + 2 * KV_PAD, HEAD_DIM), F32), pltpu.VMEM((S, HEAD_DIM), F32)
    return pl.pallas_call(
        body, grid=(H,), name="attn_bwd",
        in_specs=_head_specs(S, (0, 1, 2), H) + [once, once, once,
                                                  pl.BlockSpec((1, len(DILATIONS), Q_TILE, K_TILE), lambda h: (h, 0, 0, 0))],
        out_specs=[hspec, hspec, hspec],
        out_shape=[SDS((S, H * HEAD_DIM), F32)] * 3,
        scratch_shapes=[padded] * 4 + [plain] * 5,
        compiler_params=_cp(1))(proj, proj, proj, out, lse, dmix, bias)


def _ret_consts(lg, forward):
    C = RET_CHUNK
    i = lax.broadcasted_iota(jnp.int32, (C, C), 0)
    j = lax.broadcasted_iota(jnp.int32, (C, C), 1)
    rel = (i - j) if forward else (j - i)
    inside = (rel >= 0) if forward else (rel > 0)
    relf = jnp.maximum(rel, 0).astype(F32)
    mask = jnp.where(inside, jnp.exp(lg * relf), 0.0)
    idx = lax.broadcasted_iota(jnp.int32, (C, 1), 0).astype(F32)
    q_exp = (idx + 1.0) if forward else (C - idx)
    k_exp = (C - 1.0 - idx) if forward else idx
    return mask, relf, jnp.exp(lg * q_exp), q_exp, jnp.exp(lg * k_exp), k_exp, jnp.exp(lg * C)


def _log_decay(dec_ref, h):
    return -jnp.exp(jnp.full((1, 1), dec_ref[0, h], F32))


FFN_BLOCK = 704
CHUNK_BATCH = 8


def _batch_rows(b):
    n = CHUNK_BATCH * RET_CHUNK
    return pl.ds(pl.multiple_of(b * n, n), n)


def _batch_chunks(b):
    return pl.ds(pl.multiple_of(b * CHUNK_BATCH, CHUNK_BATCH), CHUNK_BATCH)


def _chunks3(x):
    return x.reshape(CHUNK_BATCH, RET_CHUNK, HEAD_DIM)


def _ret_scan(buf, c_decs, nc, reverse):
    def step(n, carry):
        new = []
        for way, r in enumerate(carry):
            c = n if (way == 0) != reverse else nc - 1 - n
            term = buf[way, c]
            buf[way, c] = r
            new.append(r * c_decs[way] + term)
        return tuple(new)

    lax.fori_loop(0, nc, step, (jnp.zeros((HEAD_DIM, HEAD_DIM), F32),) * 2)


def _ret_fwd(proj, dec_f, dec_b, w_norm):
    S = proj.shape[0]
    H = proj.shape[1] // (N_GROUPS * HEAD_DIM)
    nc = S // RET_CHUNK
    scale = HEAD_DIM ** -0.5

    def body(df_ref, db_ref, q_ref, k_ref, v_ref, g_ref, w_ref, y_ref, o_ref, states):
        h = pl.program_id(0)
        consts = [_ret_consts(_log_decay(dref, h), fw) for fw, dref in ((True, df_ref), (False, db_ref))]

        def kv_step(b, carry):
            rows, batch = _batch_rows(b), _batch_chunks(b)
            k3 = _chunks3(k_ref[rows, :])
            v3 = _chunks3(v_ref[rows, :]).astype(BF16)
            for way in range(2):
                states[way, batch] = _bdot((k3 * consts[way][4]).astype(BF16), v3, _BTN)
            return carry

        lax.fori_loop(0, nc // CHUNK_BATCH, kv_step, 0)
        _ret_scan(states, [c[6] for c in consts], nc, False)

        def out_step(b, carry):
            rows, batch = _batch_rows(b), _batch_chunks(b)
            q3 = _chunks3(q_ref[rows, :] * scale)
            k3 = _chunks3(k_ref[rows, :]).astype(BF16)
            v3 = _chunks3(v_ref[rows, :]).astype(BF16)
            a0 = _bdot(q3.astype(BF16), k3, _BNT)
            o = None
            for way in range(2):
                mask, q_dec = consts[way][0], consts[way][2]
                part = _bdot((a0 * mask).astype(BF16), v3, _BNN) \
                    + _bdot((q3 * q_dec).astype(BF16), states[way, batch].astype(BF16), _BNN)
                o = part if o is None else o + part
            o_ref[rows, :] = o.reshape(CHUNK_BATCH * RET_CHUNK, HEAD_DIM)
            return carry

        lax.fori_loop(0, nc // CHUNK_BATCH, out_step, 0)
        o = o_ref[...]
        g = g_ref[...]
        y_ref[...] = o * _rms_scale(o) * w_ref[...] * (g * _sigmoid(g))

    hspec = pl.BlockSpec((S, HEAD_DIM), lambda h: (0, h))
    smem = pl.BlockSpec(memory_space=pltpu.SMEM)
    return pl.pallas_call(
        body, grid=(H,), name="ret_fwd",
        in_specs=[smem, smem] + _head_specs(S, (3, 4, 5, 6), H) + [pl.BlockSpec((1, HEAD_DIM), lambda h: (0, h))],
        out_specs=[hspec, hspec],
        out_shape=[SDS((S, H * HEAD_DIM), F32)] * 2,
        scratch_shapes=[pltpu.VMEM((2, nc, HEAD_DIM, HEAD_DIM), F32)],
        compiler_params=_cp(1))(dec_f, dec_b, proj, proj, proj, proj, w_norm)


def _ret_gate_bwd(proj, o_raw, dmix, w_norm, col0):
    S = proj.shape[0]
    H = proj.shape[1] // (N_GROUPS * HEAD_DIM)

    def body(g_ref, o_ref, dy_ref, w_ref, do_ref, dg_ref, dw_ref):
        o = o_ref[...]
        g = g_ref[...]
        dy = dy_ref[...]
        w = w_ref[...]
        rr = _rms_scale(o)
        normed = o * rr
        sg = _sigmoid(g)
        silu = g * sg
        dw_ref[0] = jnp.broadcast_to(jnp.sum(dy * normed * silu, axis=0, keepdims=True), (8, HEAD_DIM))
        dg_ref[...] = (dy * normed * w * (sg * (1.0 + g * (1.0 - sg)))).astype(BF16)
        dnormed = dy * w * silu
        do_ref[...] = rr * dnormed - o * (rr * rr * rr) * jnp.mean(dnormed * o, axis=-1, keepdims=True)

    hspec = pl.BlockSpec((S, HEAD_DIM), lambda h: (0, h))
    nh0 = col0 // HEAD_DIM
    return pl.pallas_call(
        body, grid=(H,), name="ret_gate_bwd",
        in_specs=_head_specs(S, (6,), H) + [hspec, pl.BlockSpec((S, HEAD_DIM), lambda h: (0, nh0 + h)),
                                            pl.BlockSpec((1, HEAD_DIM), lambda h: (0, h))],
        out_specs=[hspec, hspec, pl.BlockSpec((1, 8, HEAD_DIM), lambda h: (h, 0, 0))],
        out_shape=[SDS((S, H * HEAD_DIM), F32), SDS((S, H * HEAD_DIM), BF16), SDS((H, 8, HEAD_DIM), F32)],
        compiler_params=_cp(1))(proj, o_raw, dmix, w_norm)


def _ret_bwd(proj, d_out, dec_f, dec_b):
    S = proj.shape[0]
    H = proj.shape[1] // (N_GROUPS * HEAD_DIM)
    C = RET_CHUNK
    nc = S // C
    scale = HEAD_DIM ** -0.5

    def body(df_ref, db_ref, q_ref, k_ref, v_ref, do, dq_ref, dk_ref, dv_ref, small_ref, states, d_states):
        h = pl.program_id(0)
        lgs = [_log_decay(df_ref, h), _log_decay(db_ref, h)]
        consts = [_ret_consts(lg, fw) for lg, fw in zip(lgs, (True, False))]

        def prep_step(b, carry):
            rows, batch = _batch_rows(b), _batch_chunks(b)
            q3 = _chunks3(q_ref[rows, :] * scale)
            k3 = _chunks3(k_ref[rows, :])
            v3 = _chunks3(v_ref[rows, :]).astype(BF16)
            do3 = _chunks3(do[rows, :]).astype(BF16)
            for way in range(2):
                states[way, batch] = _bdot((k3 * consts[way][4]).astype(BF16), v3, _BTN)
                d_states[way, batch] = _bdot((q3 * consts[way][2]).astype(BF16), do3, _BTN)
            return carry

        lax.fori_loop(0, nc // CHUNK_BATCH, prep_step, 0)
        c_decs = [c[6] for c in consts]
        _ret_scan(states, c_decs, nc, False)
        _ret_scan(d_states, c_decs, nc, True)

        def main_step(b, dlams):
            rows, batch = _batch_rows(b), _batch_chunks(b)
            q3 = _chunks3(q_ref[rows, :] * scale)
            k3 = _chunks3(k_ref[rows, :])
            q3b, k3b = q3.astype(BF16), k3.astype(BF16)
            v3b = _chunks3(v_ref[rows, :]).astype(BF16)
            do3b = _chunks3(do[rows, :]).astype(BF16)
            a0 = _bdot(q3b, k3b, _BNT)
            pv = _bdot(do3b, v3b, _BNT)
            dq = dk = dv = None
            new_dlams = []
            for way in range(2):
                mask, relf, q_dec, q_exp, k_dec, k_exp, c_dec = consts[way]
                state, d_state = states[way, batch], d_states[way, batch]
                dp = pv * mask
                dpb = dp.astype(BF16)
                gq = _bdot(do3b, state.astype(BF16), _BNT)
                gk = _bdot(v3b, d_state.astype(BF16), _BNT)
                parts = (_bdot(dpb, k3b, _BNN) + q_dec * gq, _bdot(dpb, q3b, _BTN) + k_dec * gk,
                         _bdot((a0 * mask).astype(BF16), do3b, _BTN)
                         + _bdot((k3 * k_dec).astype(BF16), d_state.astype(BF16), _BNN))
                dq, dk, dv = parts if dq is None else (dq + parts[0], dk + parts[1], dv + parts[2])
                total = lambda x: jnp.sum(jnp.sum(x, axis=0), axis=0, keepdims=True)
                new_dlams.append(dlams[way] + total(relf * a0 * dp)
                                 + total(q_exp * q_dec * q3 * gq + k_exp * k_dec * k3 * gk)
                                 + (C * c_dec) * total(state * d_state))
            flat = lambda x: x.reshape(CHUNK_BATCH * C, HEAD_DIM)
            dq_ref[rows, :] = (flat(dq) * scale).astype(BF16)
            dk_ref[rows, :] = flat(dk).astype(BF16)
            dv_ref[rows, :] = flat(dv).astype(BF16)
            return tuple(new_dlams)

        dlams = lax.fori_loop(0, nc // CHUNK_BATCH, main_step, (jnp.zeros((1, HEAD_DIM), F32),) * 2)
        for row, (dlam, lg) in enumerate(zip(dlams, lgs)):
            small_ref[0, pl.ds(row, 1), :] = jnp.broadcast_to(jnp.sum(dlam, axis=-1, keepdims=True) * lg, (1, HEAD_DIM))
        small_ref[0, pl.ds(2, 6), :] = jnp.zeros((6, HEAD_DIM), F32)

    hspec = pl.BlockSpec((S, HEAD_DIM), lambda h: (0, h))
    smem = pl.BlockSpec(memory_space=pltpu.SMEM)
    return pl.pallas_call(
        body, grid=(H,), name="ret_bwd",
        in_specs=[smem, smem] + _head_specs(S, (3, 4, 5), H) + [hspec],
        out_specs=[hspec, hspec, hspec, pl.BlockSpec((1, 8, HEAD_DIM), lambda h: (h, 0, 0))],
        out_shape=[SDS((S, H * HEAD_DIM), BF16)] * 3 + [SDS((H, 8, HEAD_DIM), F32)],
        scratch_shapes=[pltpu.VMEM((2, nc, HEAD_DIM, HEAD_DIM), F32), pltpu.VMEM((2, nc, HEAD_DIM, HEAD_DIM), F32)],
        compiler_params=_cp(1))(dec_f, dec_b, proj, proj, proj, d_out)


def _ffn_bwd_act(dh2, wd, g, u):
    S, D = dh2.shape
    nblk, _, FB = g.shape
    tm = min(1024, S)

    def body(dh_ref, wd_ref, g_ref, u_ref, dg_ref, du_ref):
        dact = _dot(dh_ref[...], wd_ref[...], _NT)
        gg = g_ref[0].astype(F32)
        sg = _sigmoid(gg)
        dg_ref[0] = (dact * u_ref[0].astype(F32) * (sg * (1.0 + gg * (1.0 - sg)))).astype(BF16)
        du_ref[0] = (dact * (gg * sg)).astype(BF16)

    blk = pl.BlockSpec((1, tm, FB), lambda j, i: (j, i, 0))
    return pl.pallas_call(
        body, grid=(nblk, S // tm), name="ffn_bwd_act",
        in_specs=[pl.BlockSpec((tm, D), lambda j, i: (i, 0)), pl.BlockSpec((FB, D), lambda j, i: (j, 0)), blk, blk],
        out_specs=[blk, blk], out_shape=[SDS((nblk, S, FB), BF16)] * 2,
        compiler_params=_cp(2))(dh2, wd, g, u)


def _ffn_bwd_in(dg, du, wg, wu, h1, dh2, w_norm):
    nblk, S, FB = dg.shape
    D = h1.shape[1]
    tm = min(RESIDENT_ROWS, S)
    blk = pl.BlockSpec((nblk, tm, FB), lambda i: (0, i, 0))
    row = pl.BlockSpec((tm, D), lambda i: (i, 0))
    vec = pl.BlockSpec((1, D), lambda i: (0, 0))

    def gate_body(dg_ref, wg_ref, part_ref):
        part_ref[...] = _blocked_matmul(dg_ref, wg_ref)

    part = pl.pallas_call(
        gate_body, grid=(S // tm,), name="ffn_bwd_in_gate", in_specs=[blk, _resident((nblk * FB, D))],
        out_specs=row, out_shape=SDS((S, D), F32), compiler_params=_cp(1))(dg, wg.reshape(nblk * FB, D))

    def body(du_ref, wu_ref, part_ref, h_ref, dh2_ref, wn_ref, dh_ref, dhb_ref, dw_ref):
        @pl.when(pl.program_id(0) == 0)
        def _():
            dw_ref[...] = jnp.zeros_like(dw_ref)

        dh, dw = _rms_bwd(part_ref[...] + _blocked_matmul(du_ref, wu_ref), h_ref[...], wn_ref[...])
        dh = dh2_ref[...] + dh
        dh_ref[...] = dh
        dhb_ref[...] = dh.astype(BF16)
        dw_ref[...] += dw

    return pl.pallas_call(
        body, grid=(S // tm,), name="ffn_bwd_in",
        in_specs=[blk, _resident((nblk * FB, D)), row, row, row, vec],
        out_specs=[row, row, vec], out_shape=[SDS((S, D), F32), SDS((S, D), BF16), SDS((1, D), F32)],
        compiler_params=_cp(1))(du, wu.reshape(nblk * FB, D), part, h1, dh2, w_norm)


def _dmix(dh1, w_out):
    S, D = dh1.shape
    tm = min(512, S)

    def body(dh_ref, w_ref, o_ref):
        o_ref[...] = _dot(dh_ref[...], w_ref[...], _NT)

    row = pl.BlockSpec((tm, D), lambda i: (i, 0))
    return pl.pallas_call(
        body, grid=(S // tm,), name="dmix", in_specs=[row, pl.BlockSpec((D, D), lambda i: (0, 0))],
        out_specs=row, out_shape=SDS((S, D), F32), compiler_params=_cp(1))(dh1, w_out)


def _in_bwd(dproj, w_blk, x, dh1, w_norm):
    S, D = x.shape
    nblk, _, NB = w_blk.shape
    tm = min(RESIDENT_ROWS, S)

    def body(dp_ref, w_ref, x_ref, dh1_ref, wn_ref, dx_ref, dw_ref):
        @pl.when(pl.program_id(0) == 0)
        def _():
            dw_ref[...] = jnp.zeros_like(dw_ref)

        dn = None
        for j in range(nblk):
            part = _dot(dp_ref[:, pl.ds(j * NB, NB)], w_ref[j], _NT)
            dn = part if dn is None else dn + part
        dh, dw = _rms_bwd(dn, x_ref[...], wn_ref[...])
        dx_ref[...] = dh1_ref[...] + dh
        dw_ref[...] += dw

    row = pl.BlockSpec((tm, D), lambda i: (i, 0))
    vec = pl.BlockSpec((1, D), lambda i: (0, 0))
    return pl.pallas_call(
        body, grid=(S // tm,), name="in_bwd",
        in_specs=[pl.BlockSpec((tm, nblk * NB), lambda i: (i, 0)),
                  pl.BlockSpec((nblk, D, NB), lambda i: (0, 0, 0), pipeline_mode=pl.Buffered(1)), row, row, vec],
        out_specs=[row, vec], out_shape=[SDS((S, D), F32), SDS((1, D), F32)],
        compiler_params=_cp(1))(dproj, w_blk, x, dh1, w_norm)


def _wgrad(a, b, a_spec, b_spec, o_spec, o_shape, grid, name):
    nk = grid[-1]

    def ld(ref):
        return ref[0] if len(ref.shape) == 3 else ref[...]

    def body(a_ref, b_ref, o_ref, acc):
        k = pl.program_id(len(grid) - 1)

        @pl.when(k == 0)
        def _():
            acc[...] = jnp.zeros_like(acc)

        acc[...] += _dot(ld(a_ref), ld(b_ref), _TN)

        @pl.when(k == nk - 1)
        def _():
            if len(o_ref.shape) == 3:
                o_ref[0] = acc[...].astype(o_ref.dtype)
            else:
                o_ref[...] = acc[...].astype(o_ref.dtype)

    return pl.pallas_call(
        body, grid=grid, name=name, in_specs=[a_spec, b_spec], out_specs=o_spec, out_shape=SDS(o_shape, BF16),
        scratch_shapes=[pltpu.VMEM(o_spec.block_shape[-2:], F32)], compiler_params=_cp(len(grid)))(a, b)


def _peer(k):
    x, y, c = lax.axis_index("x"), lax.axis_index("y"), lax.axis_index("c")
    px = 1 - x if k & 4 else x
    py = 1 - y if k & 2 else y
    pc = 1 - c if k & 1 else c
    return (px, py, pc), 4 * px + 2 * py + pc


def _exchange_copies(srcs, lands, send_sems, recv_sems, which, gather):
    _, me = _peer(0)
    pairs = []
    for pos, a in enumerate(which):
        for k in range(1, N_DEV):
            dev, idx = _peer(k)
            sem = pos * (N_DEV - 1) + k - 1
            src = srcs[a] if gather else srcs[a].at[idx]
            mk = functools.partial(pltpu.make_async_remote_copy, src_ref=src, send_sem=send_sems.at[sem],
                                   recv_sem=recv_sems.at[sem], device_id=dev, device_id_type=MESH)
            pairs.append((mk(dst_ref=lands[a].at[me]), mk(dst_ref=lands[a].at[idx])))
    return pairs


def _sequencer_kernel(name, collective_id, n_remote, n_local):
    return pl.kernel(mesh=plsc.ScalarSubcoreMesh(axis_name="sequencer", num_cores=1), name=name,
                     scratch_types=(pltpu.SemaphoreType.DMA((n_remote,)), pltpu.SemaphoreType.DMA((n_remote,)),
                                    pltpu.SemaphoreType.DMA((n_local,))),
                     compiler_params=pltpu.CompilerParams(collective_id=collective_id))


def _handshake(ks):
    barrier = pltpu.get_barrier_semaphore()
    for k in ks:
        pl.semaphore_signal(barrier, inc=1, device_id=_peer(k)[0], device_id_type=MESH)
    pl.semaphore_wait(barrier, len(ks))


def _sequencer_scatter(arrays, name, collective_id):
    n = len(arrays)
    hbm = pltpu.MemorySpace.HBM
    srcs = [jax.new_ref(a, memory_space=hbm) for a in arrays]
    lands = [jax.empty_ref(SDS(a.shape, a.dtype), memory_space=hbm) for a in arrays]

    @_sequencer_kernel(name, collective_id, n * (N_DEV - 1), n)
    def launch(send_sems, recv_sems, local_sems):
        _handshake(range(1, N_DEV))
        _, me = _peer(0)
        local = [pltpu.make_async_copy(srcs[a].at[me], lands[a].at[me], local_sems.at[a]) for a in range(n)]
        pairs = _exchange_copies(srcs, lands, send_sems, recv_sems, range(n), False)
        for out, _ in pairs:
            out.start()
        for cp in local:
            cp.start()
        for out, arrival in pairs:
            out.wait_send()
            arrival.wait_recv()
        for cp in local:
            cp.wait()

    launch()
    return [r[...] for r in lands]


SIBLING = 1
OTHER_CHIPS = (2, 4, 6)


def _sequencer_gather(arrays, name, collective_id):
    n = len(arrays)
    hbm = pltpu.MemorySpace.HBM
    srcs = [jax.new_ref(a, memory_space=hbm) for a in arrays]
    lands = [jax.empty_ref(SDS((N_DEV,) + a.shape, a.dtype), memory_space=hbm) for a in arrays]

    @_sequencer_kernel(name, collective_id, n * (N_DEV - 1), n)
    def launch(send_sems, recv_sems, local_sems):
        _handshake((SIBLING,) + OTHER_CHIPS)
        _, me = _peer(0)
        sibling, _ = _peer(SIBLING)

        def copy(a, k, src, block, to):
            sem = a * (N_DEV - 1) + k - 1
            return pltpu.make_async_remote_copy(src_ref=src, dst_ref=lands[a].at[block], send_sem=send_sems.at[sem],
                                                recv_sem=recv_sems.at[sem], device_id=to, device_id_type=MESH)

        local = [pltpu.make_async_copy(srcs[a], lands[a].at[me], local_sems.at[a]) for a in range(n)]
        first = [copy(a, k, srcs[a], me, _peer(k)[0]) for a in range(n) for k in OTHER_CHIPS + (SIBLING,)]
        for cp in first + local:
            cp.start()
        passed = []
        for a in range(n):
            for k in OTHER_CHIPS:
                _, block = _peer(k)
                copy(a, k, srcs[a], block, sibling).wait_recv()
                passed.append(copy(a, k ^ SIBLING, lands[a].at[block], block, sibling))
                passed[-1].start()
        for a in range(n):
            for k in (SIBLING,) + tuple(k ^ SIBLING for k in OTHER_CHIPS):
                copy(a, k, srcs[a], _peer(k)[1], sibling).wait_recv()
        for cp in first + passed:
            cp.wait_send()
        for cp in local:
            cp.wait()

    launch()
    return [r[...] for r in lands]


def _sequencer_gather_chips(array, name, collective_id, chips):
    hbm = pltpu.MemorySpace.HBM
    src = jax.new_ref(array, memory_space=hbm)
    land = jax.empty_ref(SDS((2 * len(chips),) + array.shape, array.dtype), memory_space=hbm)

    @_sequencer_kernel(name, collective_id, 2 * len(chips), 1)
    def launch(send_sems, recv_sems, local_sems):
        _handshake((SIBLING,) + tuple(k for k in chips if k))
        c = lax.axis_index("c")
        sibling, _ = _peer(SIBLING)

        def copy(sem, src_ref, slot, to):
            return pltpu.make_async_remote_copy(src_ref=src_ref, dst_ref=land.at[slot], send_sem=send_sems.at[sem],
                                                recv_sem=recv_sems.at[sem], device_id=to, device_id_type=MESH)

        started, local = [], None
        for pos, k in enumerate(chips):
            started.append(copy(2 * pos, src, 2 * pos + c, _peer(k)[0] if k else sibling))
            started[-1].start()
            if k == 0:
                local = pltpu.make_async_copy(src, land.at[2 * pos + c], local_sems.at[0])
                local.start()
        for pos, k in enumerate(chips):
            if k:
                copy(2 * pos, src, 2 * pos + c, sibling).wait_recv()
                started.append(copy(2 * pos + 1, land.at[2 * pos + c], 2 * pos + c, sibling))
                started[-1].start()
        for pos, k in enumerate(chips):
            copy(2 * pos + 1 if k else 2 * pos, src, 2 * pos + 1 - c, sibling).wait_recv()
        for cp in started:
            cp.wait_send()
        if local is not None:
            local.wait()

    launch()
    return land[...]


SMALL_ROWS = 64


def _small_step(part, w, m, v):
    def body(p_ref, w_ref, m_ref, v_ref, g_ref, d_ref, nm_ref, nv_ref, gath, send_sems, recv_sems):
        _, me = _peer(0)
        gath[me] = p_ref[...]
        copies = []
        for k in range(1, N_DEV):
            dev, idx = _peer(k)
            out = pltpu.make_async_remote_copy(src_ref=p_ref, dst_ref=gath.at[me], send_sem=send_sems.at[k - 1],
                                               recv_sem=recv_sems.at[k - 1], device_id=dev, device_id_type=MESH)
            out.start()
            arrival = pltpu.make_async_remote_copy(src_ref=p_ref, dst_ref=gath.at[idx], send_sem=send_sems.at[k - 1],
                                                   recv_sem=recv_sems.at[k - 1], device_id=dev, device_id_type=MESH)
            copies.append((out, arrival))
        for out, arrival in copies:
            out.wait_send()
            arrival.wait_recv()
        g = gath[0]
        for p in range(1, N_DEV):
            g = g + gath[p]
        g_ref[...] = g
        d_ref[...], nm_ref[...], nv_ref[...] = _adamw(w_ref[...], g, m_ref[...], v_ref[...])

    vm = pl.BlockSpec(memory_space=pltpu.VMEM)
    return pl.pallas_call(
        body, name="small_step", in_specs=[vm] * 4, out_specs=[vm] * 4,
        out_shape=[SDS((SMALL_ROWS, 128), F32)] * 4,
        scratch_shapes=[pltpu.VMEM((N_DEV, SMALL_ROWS, 128), F32), pltpu.SemaphoreType.DMA((N_DEV - 1,)),
                        pltpu.SemaphoreType.DMA((N_DEV - 1,))])(part, w, m, v)


def _adamw(w, g, m, v):
    m = ADAM_B1 * m + (1.0 - ADAM_B1) * g
    v = ADAM_B2 * v + (1.0 - ADAM_B2) * (g * g)
    m_hat = m / (1.0 - ADAM_B1 ** ADAM_STEP)
    v_hat = v / (1.0 - ADAM_B2 ** ADAM_STEP)
    delta = -ADAM_LR * (m_hat / (jnp.sqrt(v_hat) + ADAM_EPS) + ADAM_WD * w)
    return delta, m, v


def _adamw_block(parts, w, m, v, name):
    R, C = w.shape
    n_parts = len(parts)
    Rp = R // n_parts
    tr = next(t for t in (256, 128, 64, 32, 16, 8) if Rp % t == 0 and t * C <= 256 * 1024)
    per_part = Rp // tr

    def body(*refs):
        p_refs = refs[:n_parts]
        w_ref, m_ref, v_ref, g_ref, d_ref, nm_ref, nv_ref = refs[n_parts:]
        for k, p_ref in enumerate(p_refs):
            @pl.when(pl.program_id(0) // per_part == k)
            def _(p_ref=p_ref):
                g = p_ref[0].astype(F32)
                for p in range(1, N_DEV):
                    g = g + p_ref[p].astype(F32)
                g_ref[...] = g
                d_ref[...], nm_ref[...], nv_ref[...] = _adamw(w_ref[...], g, m_ref[...], v_ref[...])

    row = pl.BlockSpec((tr, C), lambda i: (i, 0))
    part_specs = [pl.BlockSpec((N_DEV, tr, C), functools.partial(
        lambda i, k: (0, jnp.clip(i - k * per_part, 0, per_part - 1), 0), k=k)) for k in range(n_parts)]
    return pl.pallas_call(
        body, grid=(R // tr,), name=name, in_specs=part_specs + [row, row, row],
        out_specs=[row] * 4, out_shape=[SDS((R, C), F32)] * 4, compiler_params=_cp(1))(*parts, w, m, v)


def _pack_small(mix, ffn, fin, retw, dec_f, dec_b, loss):
    flat = jnp.concatenate([mix.reshape(-1), ffn.reshape(-1), fin.reshape(-1), retw.reshape(-1), dec_f.reshape(-1),
                            dec_b.reshape(-1), loss.reshape(-1)])
    return jnp.pad(flat, (0, SMALL_ROWS * 128 - flat.shape[0])).reshape(SMALL_ROWS, 128)


def _unpack_small(packed, shapes):
    flat = packed.reshape(-1)
    out, at = [], 0
    for s in shapes:
        n = math.prod(s)
        out.append(flat[at:at + n].reshape(s))
        at += n
    return out


def kernel(x, norm_mix_w, w_in, ret_decay_fwd, ret_decay_bwd, ret_norm_w, w_out, norm_ffn_w, w_gate, w_up, w_down, norm_final_w, loss_target, m_norm_mix_w, m_w_in, m_ret_decay_fwd, m_ret_decay_bwd, m_ret_norm_w, m_w_out, m_norm_ffn_w, m_w_gate, m_w_up, m_w_down, m_norm_final_w, v_norm_mix_w, v_w_in, v_ret_decay_fwd, v_ret_decay_bwd, v_ret_norm_w, v_w_out, v_norm_ffn_w, v_w_gate, v_w_up, v_w_down, v_norm_final_w):
    x2 = x[0]
    tgt = loss_target[0]
    S, D = x2.shape
    H = ret_norm_w.shape[1] // HEAD_DIM
    DA = H * HEAD_DIM
    fin_w = norm_final_w.reshape(1, D)
    big = (w_in[0], w_out[0], w_gate[0].T, w_up[0].T, w_down[0])

    big_b = [w.astype(BF16) for w in big]
    stages = ((0,), (4, 2), (6,))
    wi_stages = [_sequencer_gather_chips(big_b[0], name, cid, ks)
                 for name, cid, ks in zip(("gather_in_own", "gather_in_near", "gather_in_far"), (0, 7, 8), stages)]
    wo, wg, wu = _sequencer_gather(big_b[1:4], "gather_mid", 1)
    wd, = _sequencer_gather(big_b[4:], "gather_down", 5)
    NB = big_b[0].shape[1]
    ax, ay = lax.axis_index("x"), lax.axis_index("y")
    chip_of = {k: 2 * (1 - ax if k & 4 else ax) + (1 - ay if k & 2 else ay) for k in (0, 2, 4, 6)}

    n1 = _norm_fwd(x2, norm_mix_w)
    proj = None
    wi = lax.empty((N_DEV, D, NB), BF16)
    for ks, w_st, name in zip(stages, wi_stages, ("proj_own", "proj_near", "proj_far")):
        ids = jnp.stack([chip_of[k] for k in ks]).astype(jnp.int32)
        proj = _proj_part(n1, w_st, ids, proj, N_DEV, name)
        for pos, k in enumerate(ks):
            wi = lax.dynamic_update_slice(wi, w_st[2 * pos:2 * pos + 2], (2 * chip_of[k], 0, 0))
    bias = _attn_bias()[:H]
    attn, lse = _attn_fwd(proj, bias)
    ret, o_raw = _ret_fwd(proj, ret_decay_fwd, ret_decay_bwd, ret_norm_w)
    wo_full = wo.reshape(D, D)
    d_ff = N_DEV * wd.shape[1]
    FB = FFN_BLOCK if d_ff % FFN_BLOCK == 0 else wd.shape[1]
    n_fb = d_ff // FB
    wg, wu = wg.reshape(n_fb, FB, D), wu.reshape(n_fb, FB, D)
    wd_full = wd.reshape(d_ff, D)
    h1, mixed, n2 = _out_fwd(x2, attn, ret, wo_full, norm_ffn_w)
    gate, up, act = _ffn_up(n2, wg, wu)
    dh2, dh2_b, loss_parts, g_fin = _ffn_down_loss(act, wd_full, h1, tgt, fin_w)

    dgate, dup = _ffn_bwd_act(dh2_b, wd_full, gate, up)
    tn = min(1024, D)
    ffn_specs = (pl.BlockSpec((1, S, FB), lambda j, n, k: (j, 0, 0)), pl.BlockSpec((S, tn), lambda j, n, k: (0, n)),
                 pl.BlockSpec((1, FB, tn), lambda j, n, k: (j, 0, n)), (n_fb, FB, D), (n_fb, D // tn, 1))
    per_dev = (N_DEV, d_ff // N_DEV, D)
    g_wd = _wgrad(act, dh2_b, *ffn_specs, "wgrad_down").reshape(per_dev)
    g_wg = _wgrad(dgate, n2, *ffn_specs, "wgrad_gate").reshape(per_dev)
    g_wu = _wgrad(dup, n2, *ffn_specs, "wgrad_up").reshape(per_dev)
    parts_f = _sequencer_scatter([g_wg, g_wu, g_wd], "scatter_ffn", 2)
    dh1, dh1_b, g_ffn = _ffn_bwd_in(dgate, dup, wg, wu, h1, dh2, norm_ffn_w)
    dmix = _dmix(dh1_b, wo_full)
    tmw = min(512, D)
    tk = min(2048, S)
    g_wo = _wgrad(mixed, dh1_b, pl.BlockSpec((tk, tmw), lambda m, k: (k, m)), pl.BlockSpec((tk, D), lambda m, k: (k, 0)),
                  pl.BlockSpec((tmw, D), lambda m, k: (m, 0)), (D, D), (D // tmw, S // tk), "wgrad_out")
    parts_o = _sequencer_scatter([g_wo.reshape(N_DEV, D // N_DEV, D)], "scatter_out", 3)
    d_ret, dg_r, small_w = _ret_gate_bwd(proj, o_raw, dmix, ret_norm_w, DA)
    dq_r, dk_r, dv_r, small = _ret_bwd(proj, d_ret, ret_decay_fwd, ret_decay_bwd)
    dq_a, dk_a, dv_a = _attn_bwd(proj, attn, lse, dmix, bias)
    dproj = jnp.concatenate([t.astype(BF16) for t in (dq_a, dk_a, dv_a, dq_r, dk_r, dv_r, dg_r)], axis=1)
    half = D // tmw // 2
    parts_i = []
    for part, (name, cid) in enumerate((("in_lo", 4), ("in_hi", 6))):
        g_wi = _wgrad(n1, dproj, pl.BlockSpec((S, tmw), functools.partial(lambda j, m, k, off: (0, m + off), off=part * half)),
                      pl.BlockSpec((S, NB), lambda j, m, k: (0, j)), pl.BlockSpec((1, tmw, NB), lambda j, m, k: (j, m, 0)),
                      (N_DEV, D // 2, NB), (N_DEV, half, 1), "wgrad_" + name)
        parts_i += _sequencer_scatter([g_wi], "scatter_" + name, cid)
    grad_x, g_mix = _in_bwd(dproj, wi, x2, dh1, norm_mix_w)

    big_m = (m_w_in[0], m_w_out[0], m_w_gate[0].T, m_w_up[0].T, m_w_down[0])
    big_v = (v_w_in[0], v_w_out[0], v_w_gate[0].T, v_w_up[0].T, v_w_down[0])
    names = ("adamw_in", "adamw_out", "adamw_gate", "adamw_up", "adamw_down")
    upd = [None] * 5
    for a, p in zip((2, 3, 4, 1, 0), [[t] for t in parts_f + parts_o] + [parts_i]):
        upd[a] = _adamw_block(p, big[a], big_m[a], big_v[a], names[a])

    g_dec_f = small[:, 0, 0].reshape(1, H)
    g_dec_b = small[:, 1, 0].reshape(1, H)
    g_retw = small_w[:, 0, :].reshape(1, DA)
    loss_local = jnp.sum(loss_parts[::8, 0])
    zero = jnp.zeros((1,), F32)
    part = _pack_small(g_mix, g_ffn, g_fin, g_retw, g_dec_f, g_dec_b, loss_local)
    sw = _pack_small(norm_mix_w, norm_ffn_w, norm_final_w, ret_norm_w, ret_decay_fwd, ret_decay_bwd, zero)
    sm = _pack_small(m_norm_mix_w, m_norm_ffn_w, m_norm_final_w, m_ret_norm_w, m_ret_decay_fwd, m_ret_decay_bwd, zero)
    sv = _pack_small(v_norm_mix_w, v_norm_ffn_w, v_norm_final_w, v_ret_norm_w, v_ret_decay_fwd, v_ret_decay_bwd, zero)
    shapes = [(1, D), (1, D), (D,), (1, DA), (1, H), (1, H), ()]
    sg, sd, snm, snv = [_unpack_small(t, shapes) for t in _small_step(part, sw, sm, sv)]
    loss = sg[6]

    def ordered(small_set, k):
        b = [(u[k].T if a in (2, 3) else u[k])[None] for a, u in enumerate(upd)]
        return [small_set[0], b[0], small_set[4], small_set[5], small_set[3], b[1], small_set[1], b[2], b[3], b[4],
                small_set[2]]

    return (loss, grad_x[None], *ordered(sg, 0), *ordered(sd, 1), *ordered(snm, 2), *ordered(snv, 3))
```

```python
import functools
import math

import numpy as np
import jax
import jax.numpy as jnp
from jax import lax
from jax.experimental import pallas as pl
from jax.experimental.pallas import tpu as pltpu
from jax.experimental.pallas import tpu_sc as plsc

F32 = jnp.float32
BF16 = jnp.bfloat16
SDS = jax.ShapeDtypeStruct

HEAD_DIM = 128
EPS = 1e-6
RET_CHUNK = 128
DILATIONS = (1, 4, 16)
BAND = 64
Q_TILE = 128
K_TILE = Q_TILE + 2 * BAND
KV_PAD = BAND * 4
TILE_GROUP = 8
NEG = -1e30
N_DEV = 8
N_GROUPS = 7
ADAM_LR, ADAM_B1, ADAM_B2, ADAM_EPS, ADAM_WD, ADAM_STEP = 0.001, 0.9, 0.999, 1e-08, 0.01, 10
VMEM_LIMIT = 56 * 1024 * 1024
MESH = pl.DeviceIdType.MESH
ANY = pl.BlockSpec(memory_space=pl.ANY)


def _cp(n_grid):
    return pltpu.CompilerParams(dimension_semantics=("arbitrary",) * n_grid, vmem_limit_bytes=VMEM_LIMIT)


def _sigmoid(x):
    return 1.0 / (1.0 + jnp.exp(-x))


def _rms_scale(h):
    return lax.rsqrt(jnp.mean(h * h, axis=-1, keepdims=True) + EPS)


def _rms_bwd(dn, h, w):
    r = _rms_scale(h)
    gw = dn * w
    dh = r * gw - h * (r * r * r) * jnp.mean(gw * h, axis=-1, keepdims=True)
    return dh, jnp.sum(dn * h * r, axis=0, keepdims=True)


def _dot(a, b, dims):
    return lax.dot_general(a.astype(BF16), b.astype(BF16), (dims, ((), ())), preferred_element_type=F32)


_NN = ((1,), (0,))
_NT = ((1,), (1,))
_TN = ((0,), (0,))


RESIDENT_ROWS = 256


def _resident(shape):
    return pl.BlockSpec(shape, lambda i: (0, 0), pipeline_mode=pl.Buffered(1))


def _blocked_matmul(a_ref, w_ref):
    nblk, _, fb = a_ref.shape
    out = None
    for j in range(nblk):
        part = jnp.dot(a_ref[j], w_ref[pl.ds(j * fb, fb), :], preferred_element_type=F32)
        out = part if out is None else out + part
    return out


def _proj_fwd(x, w_norm, w_blk):
    S, D = x.shape
    nblk, _, NB = w_blk.shape
    tm = min(1024, S)

    def body(x_ref, wn_ref, w_ref, proj_ref, n_ref, n_scr):
        @pl.when(pl.program_id(1) == 0)
        def _():
            xf = x_ref[...]
            nb = (xf * _rms_scale(xf) * wn_ref[...]).astype(BF16)
            n_scr[...] = nb
            n_ref[...] = nb
        proj_ref[...] = jnp.dot(n_scr[...], w_ref[0], preferred_element_type=F32)

    return pl.pallas_call(
        body, grid=(S // tm, nblk), name="proj_fwd",
        in_specs=[pl.BlockSpec((tm, D), lambda i, j: (i, 0)), pl.BlockSpec((1, D), lambda i, j: (0, 0)),
                  pl.BlockSpec((1, D, NB), lambda i, j: (j, 0, 0))],
        out_specs=[pl.BlockSpec((tm, NB), lambda i, j: (i, j)), pl.BlockSpec((tm, D), lambda i, j: (i, 0))],
        out_shape=[SDS((S, nblk * NB), F32), SDS((S, D), BF16)],
        scratch_shapes=[pltpu.VMEM((tm, D), BF16)], compiler_params=_cp(2))(x, w_norm, w_blk)


def _norm_fwd(x, w_norm):
    S, D = x.shape
    tm = min(1024, S)

    def body(x_ref, wn_ref, n_ref):
        xf = x_ref[...]
        n_ref[...] = (xf * _rms_scale(xf) * wn_ref[...]).astype(BF16)

    row = pl.BlockSpec((tm, D), lambda i: (i, 0))
    return pl.pallas_call(body, grid=(S // tm,), name="norm_fwd", in_specs=[row, pl.BlockSpec((1, D), lambda i: (0, 0))],
                          out_specs=row, out_shape=SDS((S, D), BF16), compiler_params=_cp(1))(x, w_norm)


def _proj_part(n1, w_slots, slots, blocks, proj, n_blocks, name):
    S, D = n1.shape
    NB = w_slots.shape[2]
    tm = min(1024, S)

    def body(slots_ref, blocks_ref, n_ref, w_ref, *rest):
        rest[-1][...] = jnp.dot(n_ref[...], w_ref[0], preferred_element_type=F32)

    out_spec = pl.BlockSpec((tm, NB), lambda i, j, slots, blocks: (i, blocks[j]))
    in_specs = [pl.BlockSpec((tm, D), lambda i, j, slots, blocks: (i, 0)),
                pl.BlockSpec((1, D, NB), lambda i, j, slots, blocks: (slots[j], 0, 0))]
    args = [n1, w_slots]
    if proj is not None:
        in_specs.append(ANY)
        args.append(proj)
    return pl.pallas_call(
        body, name=name, out_shape=SDS((S, n_blocks * NB), F32),
        grid_spec=pltpu.PrefetchScalarGridSpec(num_scalar_prefetch=2, grid=(S // tm, slots.shape[0]), in_specs=in_specs,
                                               out_specs=out_spec),
        input_output_aliases={} if proj is None else {4: 0},
        compiler_params=_cp(2))(slots, blocks, *args)


def _out_fwd(x, attn, ret, w_out, w_norm):
    S, D = x.shape
    DA = attn.shape[1]
    tm = min(256, S)

    def body(x_ref, a_ref, r_ref, w_ref, wn_ref, h_ref, mix_ref, n_ref):
        a = a_ref[...].astype(BF16)
        r = r_ref[...].astype(BF16)
        mix_ref[:, :DA] = a
        mix_ref[:, DA:] = r
        h = x_ref[...] + jnp.dot(a, w_ref[:DA, :], preferred_element_type=F32) \
            + jnp.dot(r, w_ref[DA:, :], preferred_element_type=F32)
        h_ref[...] = h
        n_ref[...] = (h * _rms_scale(h) * wn_ref[...]).astype(BF16)

    row = lambda w: pl.BlockSpec((tm, w), lambda i: (i, 0))
    return pl.pallas_call(
        body, grid=(S // tm,), name="out_fwd",
        in_specs=[row(D), row(DA), row(D - DA), pl.BlockSpec((D, D), lambda i: (0, 0)),
                  pl.BlockSpec((1, D), lambda i: (0, 0))],
        out_specs=[row(D), row(D), row(D)],
        out_shape=[SDS((S, D), F32), SDS((S, D), BF16), SDS((S, D), BF16)],
        compiler_params=_cp(1))(x, attn, ret, w_out, w_norm)


def _ffn_up(n2, wg, wu):
    S, D = n2.shape
    nblk, FB, _ = wg.shape
    tm = min(1024, S)

    def body(n_ref, wg_ref, wu_ref, g_ref, u_ref, a_ref):
        n = n_ref[...]
        g = _dot(n, wg_ref[0], _NT)
        u = _dot(n, wu_ref[0], _NT)
        g_ref[0] = g.astype(BF16)
        u_ref[0] = u.astype(BF16)
        a_ref[0] = (g * _sigmoid(g) * u).astype(BF16)

    wspec = pl.BlockSpec((1, FB, D), lambda j, i: (j, 0, 0))
    ospec = pl.BlockSpec((1, tm, FB), lambda j, i: (j, i, 0))
    return pl.pallas_call(
        body, grid=(nblk, S // tm), name="ffn_up",
        in_specs=[pl.BlockSpec((tm, D), lambda j, i: (i, 0)), wspec, wspec],
        out_specs=[ospec, ospec, ospec],
        out_shape=[SDS((nblk, S, FB), BF16)] * 3,
        compiler_params=_cp(2))(n2, wg, wu)


def _ffn_down_loss(act, wd, h1, target, w_norm):
    nblk, S, FB = act.shape
    D = h1.shape[1]
    tm = min(RESIDENT_ROWS, S)

    def body(a_ref, wd_ref, h_ref, t_ref, wn_ref, dh_ref, dhb_ref, loss_ref, dw_ref):
        @pl.when(pl.program_id(0) == 0)
        def _():
            dw_ref[...] = jnp.zeros_like(dw_ref)

        h = h_ref[...] + _blocked_matmul(a_ref, wd_ref)
        w = wn_ref[...]
        err = h * _rms_scale(h) * w - t_ref[...]
        loss_ref[...] = jnp.full(loss_ref.shape, 0.5 * jnp.sum(err * err) / D, F32)
        dh, dw = _rms_bwd(err * (1.0 / D), h, w)
        dh_ref[...] = dh
        dhb_ref[...] = dh.astype(BF16)
        dw_ref[...] += dw

    row = pl.BlockSpec((tm, D), lambda i: (i, 0))
    vec = pl.BlockSpec((1, D), lambda i: (0, 0))
    return pl.pallas_call(
        body, grid=(S // tm,), name="ffn_down_loss",
        in_specs=[pl.BlockSpec((nblk, tm, FB), lambda i: (0, i, 0)), _resident((nblk * FB, D)), row, row, vec],
        out_specs=[row, row, pl.BlockSpec((8, 128), lambda i: (i, 0)), vec],
        out_shape=[SDS((S, D), F32), SDS((S, D), BF16), SDS((S // tm * 8, 128), F32), SDS((1, D), F32)],
        compiler_params=_cp(1))(act, wd, h1, target, w_norm)


def _attn_bias():
    n_heads = 8
    slopes = np.exp2(-8.0 * np.arange(1, n_heads + 1, dtype=np.float32) / n_heads)
    dist = np.abs(np.arange(K_TILE)[None, :] - BAND - np.arange(Q_TILE)[:, None])
    out = np.empty((n_heads, len(DILATIONS), Q_TILE, K_TILE), np.float32)
    for h in range(n_heads):
        for p, d in enumerate(DILATIONS):
            out[h, p] = np.where(dist <= BAND, -slopes[h] * (d * dist).astype(np.float32), NEG)
    return jnp.asarray(out)


def _attn_tiles(S, d):
    L = S // d
    per_class = L // Q_TILE
    return L, per_class, d * per_class


def _tile_rows(t, d, per_class):
    r = t // per_class
    a = (t % per_class) * Q_TILE
    q_rows = pl.ds(r + d * a, Q_TILE, stride=d) if d > 1 else pl.ds(pl.multiple_of(a, Q_TILE), Q_TILE)
    k_rows = pl.ds(KV_PAD + r + d * (a - BAND), K_TILE, stride=d) if d > 1 else pl.ds(
        pl.multiple_of(KV_PAD + a - BAND, BAND), K_TILE)
    return a, q_rows, k_rows


def _to_quarters(dst, src, n, dst_off=0):
    for r in range(4):
        dst[pl.ds(dst_off + r * (n // 4), n // 4), :] = src[pl.ds(r, n // 4, stride=4), :]


def _quarter_tile_rows(t, S):
    L = S // 16
    per_class = L // Q_TILE
    blk, tt = t // (4 * per_class), t % (4 * per_class)
    r, a = tt // per_class, (tt % per_class) * Q_TILE
    q_rows = pl.ds(blk * (S // 4) + r + 4 * a, Q_TILE, stride=4)
    k_rows = pl.ds(KV_PAD + blk * (S // 4) + r + 4 * (a - BAND), K_TILE, stride=4)
    return a, q_rows, k_rows


def _lanes(x, width):
    return jnp.concatenate([x] * (width // HEAD_DIM), axis=-1)


_BNT = (((2,), (2,)), ((0,), (0,)))
_BNN = (((2,), (1,)), ((0,), (0,)))
_BTN = (((1,), (1,)), ((0,), (0,)))


def _bdot(a, b, dims):
    return lax.dot_general(a, b, dims, preferred_element_type=F32)


def _stacked(rows, loaders):
    return [jnp.stack([f(*r) for r in rows]) for f in loaders]


def _edge_mask(a, L):
    lk = lax.broadcasted_iota(jnp.int32, (1, K_TILE), 1) + (a - BAND)
    return jnp.where((lk >= 0) & (lk < L), 0.0, NEG).astype(F32)


def _fill_padded(dst, src, S):
    dst[pl.ds(0, KV_PAD), :] = jnp.zeros((KV_PAD, HEAD_DIM), F32)
    dst[pl.ds(KV_PAD + S, KV_PAD), :] = jnp.zeros((KV_PAD, HEAD_DIM), F32)
    dst[pl.ds(KV_PAD, S), :] = src[...]


def _head_specs(S, groups, n_heads):
    return [pl.BlockSpec((S, HEAD_DIM), functools.partial(lambda h, g: (0, g * n_heads + h), g=g)) for g in groups]


def _attn_fwd(proj, bias):
    S = proj.shape[0]
    H = proj.shape[1] // (N_GROUPS * HEAD_DIM)
    scale = HEAD_DIM ** -0.5

    def body(q_ref, k_ref, v_ref, b_ref, o_ref, lse_ref, kp, vp, m_run, l_run, q4, m3, l3, acc3):
        _fill_padded(kp, k_ref, S)
        _fill_padded(vp, v_ref, S)
        o_ref[...] = jnp.zeros_like(o_ref)
        m_run[...] = jnp.full(m_run.shape, NEG, F32)
        l_run[...] = jnp.zeros_like(l_run)
        for p, d in enumerate(DILATIONS[:2]):
            L, per_class, n_tiles = _attn_tiles(S, d)

            def tiles(t, carry, p=p, d=d, L=L, per_class=per_class, n_tiles=n_tiles):
                rows = [_tile_rows(t + u * (n_tiles // TILE_GROUP), d, per_class) for u in range(TILE_GROUP)]
                qs, ks, vs, m_old, l_old, o_old, edge = _stacked(rows, (
                    lambda a, qr, kr: q_ref[qr, :].astype(BF16), lambda a, qr, kr: kp[kr, :].astype(BF16),
                    lambda a, qr, kr: vp[kr, :].astype(BF16), lambda a, qr, kr: m_run[qr, :],
                    lambda a, qr, kr: l_run[qr, :], lambda a, qr, kr: o_ref[qr, :], lambda a, qr, kr: _edge_mask(a, L)))
                s = _bdot(qs, ks, _BNT) * scale + b_ref[0, p][None] + edge
                m_new = jnp.maximum(m_old, jnp.max(s, axis=-1, keepdims=True))
                pr = jnp.exp(s - _lanes(m_new, K_TILE)).astype(BF16)
                alpha = jnp.exp(m_old - m_new)
                l_new = alpha * l_old + _bdot(pr, jnp.ones((TILE_GROUP, K_TILE, HEAD_DIM), BF16), _BNN)
                o_new = alpha * o_old + _bdot(pr, vs, _BNN)
                for u, (_, qr, _) in enumerate(rows):
                    o_ref[qr, :] = o_new[u]
                    m_run[qr, :] = m_new[u]
                    l_run[qr, :] = l_new[u]
                return carry

            lax.fori_loop(0, n_tiles // TILE_GROUP, tiles, 0)

        _to_quarters(q4, q_ref, S)
        _to_quarters(kp, k_ref, S, KV_PAD)
        _to_quarters(vp, v_ref, S, KV_PAD)
        n_tiles = _attn_tiles(S, DILATIONS[2])[2]

        def tiles3(t, carry):
            rows = [_quarter_tile_rows(t + u * (n_tiles // TILE_GROUP), S) for u in range(TILE_GROUP)]
            qs, ks, vs, edge = _stacked(rows, (
                lambda a, qr, kr: q4[qr, :].astype(BF16), lambda a, qr, kr: kp[kr, :].astype(BF16),
                lambda a, qr, kr: vp[kr, :].astype(BF16), lambda a, qr, kr: _edge_mask(a, S // DILATIONS[2])))
            s = _bdot(qs, ks, _BNT) * scale + b_ref[0, 2][None] + edge
            m_new = jnp.broadcast_to(jnp.max(s, axis=-1, keepdims=True), (TILE_GROUP, Q_TILE, HEAD_DIM))
            pr = jnp.exp(s - _lanes(m_new, K_TILE)).astype(BF16)
            l_new = _bdot(pr, jnp.ones((TILE_GROUP, K_TILE, HEAD_DIM), BF16), _BNN)
            o_new = _bdot(pr, vs, _BNN)
            for u, (_, qr, _) in enumerate(rows):
                acc3[qr, :] = o_new[u]
                m3[qr, :] = m_new[u]
                l3[qr, :] = l_new[u]
            return carry

        lax.fori_loop(0, n_tiles // TILE_GROUP, tiles3, 0)
        for r in range(4):
            nat, qtr = pl.ds(r, S // 4, stride=4), pl.ds(r * (S // 4), S // 4)
            m_a, m_b = m_run[nat, :], m3[qtr, :]
            m = jnp.maximum(m_a, m_b)
            w_a, w_b = jnp.exp(m_a - m), jnp.exp(m_b - m)
            l = w_a * l_run[nat, :] + w_b * l3[qtr, :]
            o_ref[nat, :] = (w_a * o_ref[nat, :] + w_b * acc3[qtr, :]) / l
            lse_ref[nat, :] = m + jnp.log(l)

    hspec = pl.BlockSpec((S, HEAD_DIM), lambda h: (0, h))
    padded, plain = pltpu.VMEM((S + 2 * KV_PAD, HEAD_DIM), F32), pltpu.VMEM((S, HEAD_DIM), F32)
    return pl.pallas_call(
        body, grid=(H,), name="attn_fwd",
        in_specs=_head_specs(S, (0, 1, 2), H) + [
            pl.BlockSpec((1, len(DILATIONS), Q_TILE, K_TILE), lambda h: (h, 0, 0, 0))],
        out_specs=[hspec, hspec],
        out_shape=[SDS((S, H * HEAD_DIM), F32), SDS((S, H * HEAD_DIM), F32)],
        scratch_shapes=[padded, padded] + [plain] * 6,
        compiler_params=_cp(1))(proj, proj, proj, bias)


def _attn_bwd(proj, out, lse, dmix, bias):
    S = proj.shape[0]
    H = proj.shape[1] // (N_GROUPS * HEAD_DIM)
    scale = HEAD_DIM ** -0.5
    assert S // DILATIONS[2] >= 2 * Q_TILE

    def body(q_ref, k_ref, v_ref, o_ref, lse_ref, do_ref, b_ref, dq_ref, dk_ref, dv_ref,
             kp, vp, dkp, dvp, dsum, q4, do4, lse4, dsum4):
        _fill_padded(kp, k_ref, S)
        _fill_padded(vp, v_ref, S)
        dkp[...] = jnp.zeros_like(dkp)
        dvp[...] = jnp.zeros_like(dvp)
        dq_ref[...] = jnp.zeros_like(dq_ref)
        dsum[...] = jnp.broadcast_to(jnp.sum(do_ref[...] * o_ref[...], axis=-1, keepdims=True), dsum.shape)

        def run(n_tiles, tile_rows, p, L, q_src, do_src, lse_src, dsum_src, dq_dst, dq_adds):
            def tiles(t, carry):
                rows = [tile_rows(t + u * (n_tiles // TILE_GROUP)) for u in range(TILE_GROUP)]
                qs, ks, vs, dos, lses, dsums, dk_old, dv_old, edge = _stacked(rows, (
                    lambda a, qr, kr: q_src[qr, :].astype(BF16), lambda a, qr, kr: kp[kr, :].astype(BF16),
                    lambda a, qr, kr: vp[kr, :].astype(BF16), lambda a, qr, kr: do_src[qr, :].astype(BF16),
                    lambda a, qr, kr: lse_src[qr, :], lambda a, qr, kr: dsum_src[qr, :],
                    lambda a, qr, kr: dkp[kr, :], lambda a, qr, kr: dvp[kr, :], lambda a, qr, kr: _edge_mask(a, L)))
                s = _bdot(qs, ks, _BNT) * scale + b_ref[0, p][None] + edge
                pr = jnp.exp(s - _lanes(lses, K_TILE))
                ds = (pr * (_bdot(dos, vs, _BNT) - _lanes(dsums, K_TILE)) * scale).astype(BF16)
                dq_new = _bdot(ds, ks, _BNN)
                if dq_adds:
                    dq_new = dq_new + jnp.stack([dq_dst[qr, :] for _, qr, _ in rows])
                dk_new = dk_old + _bdot(ds, qs, _BTN)
                dv_new = dv_old + _bdot(pr.astype(BF16), dos, _BTN)
                for u, (_, qr, kr) in enumerate(rows):
                    dq_dst[qr, :] = dq_new[u]
                    dkp[kr, :] = dk_new[u]
                    dvp[kr, :] = dv_new[u]
                return carry

            lax.fori_loop(0, n_tiles // TILE_GROUP, tiles, 0)

        for p, d in enumerate(DILATIONS[:2]):
            L, per_class, n_tiles = _attn_tiles(S, d)
            run(n_tiles, functools.partial(_tile_rows, d=d, per_class=per_class), p, L,
                q_ref, do_ref, lse_ref, dsum, dq_ref, True)
        dk_ref[...] = dkp[pl.ds(KV_PAD, S), :]
        dv_ref[...] = dvp[pl.ds(KV_PAD, S), :]

        for dst, src in ((q4, q_ref), (do4, do_ref), (lse4, lse_ref), (dsum4, dsum)):
            _to_quarters(dst, src, S)
        _to_quarters(kp, k_ref, S, KV_PAD)
        _to_quarters(vp, v_ref, S, KV_PAD)
        dkp[...] = jnp.zeros_like(dkp)
        dvp[...] = jnp.zeros_like(dvp)
        dq3 = dsum
        run(_attn_tiles(S, DILATIONS[2])[2], functools.partial(_quarter_tile_rows, S=S), 2, S // DILATIONS[2],
            q4, do4, lse4, dsum4, dq3, False)
        for r in range(4):
            nat, qtr = pl.ds(r, S // 4, stride=4), pl.ds(r * (S // 4), S // 4)
            pad_qtr = pl.ds(KV_PAD + r * (S // 4), S // 4)
            dq_ref[nat, :] = dq_ref[nat, :] + dq3[qtr, :]
            dk_ref[nat, :] = dk_ref[nat, :] + dkp[pad_qtr, :]
            dv_ref[nat, :] = dv_ref[nat, :] + dvp[pad_qtr, :]

    hspec = pl.BlockSpec((S, HEAD_DIM), lambda h: (0, h))
    once =pl.BlockSpec((S, HEAD_DIM), lambda h: (0, h), pipeline_mode=pl.Buffered(1))
    padded, plain = pltpu.VMEM((S + 2 * KV_PAD, HEAD_DIM), F32), pltpu.VMEM((S, HEAD_DIM), F32)
    return pl.pallas_call(
        body, grid=(H,), name="attn_bwd",
        in_specs=_head_specs(S, (0, 1, 2), H) + [once, once, once,
                                                  pl.BlockSpec((1, len(DILATIONS), Q_TILE, K_TILE), lambda h: (h, 0, 0, 0))],
        out_specs=[hspec, hspec, hspec],
        out_shape=[SDS((S, H * HEAD_DIM), F32)] * 3,
        scratch_shapes=[padded] * 4 + [plain] * 5,
        compiler_params=_cp(1))(proj, proj, proj, out, lse, dmix, bias)


def _ret_consts(lg, forward):
    C = RET_CHUNK
    i = lax.broadcasted_iota(jnp.int32, (C, C), 0)
    j = lax.broadcasted_iota(jnp.int32, (C, C), 1)
    rel = (i - j) if forward else (j - i)
    inside = (rel >= 0) if forward else (rel > 0)
    relf = jnp.maximum(rel, 0).astype(F32)
    mask = jnp.where(inside, jnp.exp(lg * relf), 0.0)
    idx = lax.broadcasted_iota(jnp.int32, (C, 1), 0).astype(F32)
    q_exp = (idx + 1.0) if forward else (C - idx)
    k_exp = (C - 1.0 - idx) if forward else idx
    return mask, relf, jnp.exp(lg * q_exp), q_exp, jnp.exp(lg * k_exp), k_exp, jnp.exp(lg * C)


def _log_decay(dec_ref, h):
    return -jnp.exp(jnp.full((1, 1), dec_ref[0, h], F32))


FFN_BLOCK = 704
CHUNK_BATCH = 8


def _batch_rows(b):
    n = CHUNK_BATCH * RET_CHUNK
    return pl.ds(pl.multiple_of(b * n, n), n)


def _batch_chunks(b):
    return pl.ds(pl.multiple_of(b * CHUNK_BATCH, CHUNK_BATCH), CHUNK_BATCH)


def _chunks3(x):
    return x.reshape(CHUNK_BATCH, RET_CHUNK, HEAD_DIM)


def _ret_scan(buf, c_decs, nc, reverse):
    def step(n, carry):
        new = []
        for way, r in enumerate(carry):
            c = n if (way == 0) != reverse else nc - 1 - n
            term = buf[way, c]
            buf[way, c] = r
            new.append(r * c_decs[way] + term)
        return tuple(new)

    lax.fori_loop(0, nc, step, (jnp.zeros((HEAD_DIM, HEAD_DIM), F32),) * 2)


def _ret_fwd(proj, dec_f, dec_b, w_norm):
    S = proj.shape[0]
    H = proj.shape[1] // (N_GROUPS * HEAD_DIM)
    nc = S // RET_CHUNK
    scale = HEAD_DIM ** -0.5

    def body(df_ref, db_ref, q_ref, k_ref, v_ref, g_ref, w_ref, y_ref, o_ref, states):
        h = pl.program_id(0)
        consts = [_ret_consts(_log_decay(dref, h), fw) for fw, dref in ((True, df_ref), (False, db_ref))]

        def kv_step(b, carry):
            rows, batch = _batch_rows(b), _batch_chunks(b)
            k3 = _chunks3(k_ref[rows, :])
            v3 = _chunks3(v_ref[rows, :]).astype(BF16)
            for way in range(2):
                states[way, batch] = _bdot((k3 * consts[way][4]).astype(BF16), v3, _BTN)
            return carry

        lax.fori_loop(0, nc // CHUNK_BATCH, kv_step, 0)
        _ret_scan(states, [c[6] for c in consts], nc, False)

        def out_step(b, carry):
            rows, batch = _batch_rows(b), _batch_chunks(b)
            q3 = _chunks3(q_ref[rows, :] * scale)
            k3 = _chunks3(k_ref[rows, :]).astype(BF16)
            v3 = _chunks3(v_ref[rows, :]).astype(BF16)
            a0 = _bdot(q3.astype(BF16), k3, _BNT)
            o = None
            for way in range(2):
                mask, q_dec = consts[way][0], consts[way][2]
                part = _bdot((a0 * mask).astype(BF16), v3, _BNN) \
                    + _bdot((q3 * q_dec).astype(BF16), states[way, batch].astype(BF16), _BNN)
                o = part if o is None else o + part
            o_ref[rows, :] = o.reshape(CHUNK_BATCH * RET_CHUNK, HEAD_DIM)
            return carry

        lax.fori_loop(0, nc // CHUNK_BATCH, out_step, 0)
        o = o_ref[...]
        g = g_ref[...]
        y_ref[...] = o * _rms_scale(o) * w_ref[...] * (g * _sigmoid(g))

    hspec = pl.BlockSpec((S, HEAD_DIM), lambda h: (0, h))
    smem = pl.BlockSpec(memory_space=pltpu.SMEM)
    return pl.pallas_call(
        body, grid=(H,), name="ret_fwd",
        in_specs=[smem, smem] + _head_specs(S, (3, 4, 5, 6), H) + [pl.BlockSpec((1, HEAD_DIM), lambda h: (0, h))],
        out_specs=[hspec, hspec],
        out_shape=[SDS((S, H * HEAD_DIM), F32)] * 2,
        scratch_shapes=[pltpu.VMEM((2, nc, HEAD_DIM, HEAD_DIM), F32)],
        compiler_params=_cp(1))(dec_f, dec_b, proj, proj, proj, proj, w_norm)


def _ret_gate_bwd(proj, o_raw, dmix, w_norm, col0):
    S = proj.shape[0]
    H = proj.shape[1] // (N_GROUPS * HEAD_DIM)

    def body(g_ref, o_ref, dy_ref, w_ref, do_ref, dg_ref, dw_ref):
        o = o_ref[...]
        g = g_ref[...]
        dy = dy_ref[...]
        w = w_ref[...]
        rr = _rms_scale(o)
        normed = o * rr
        sg = _sigmoid(g)
        silu = g * sg
        dw_ref[0] = jnp.broadcast_to(jnp.sum(dy * normed * silu, axis=0, keepdims=True), (8, HEAD_DIM))
        dg_ref[...] = (dy * normed * w * (sg * (1.0 + g * (1.0 - sg)))).astype(BF16)
        dnormed = dy * w * silu
        do_ref[...] = rr * dnormed - o * (rr * rr * rr) * jnp.mean(dnormed * o, axis=-1, keepdims=True)

    hspec = pl.BlockSpec((S, HEAD_DIM), lambda h: (0, h))
    nh0 = col0 // HEAD_DIM
    return pl.pallas_call(
        body, grid=(H,), name="ret_gate_bwd",
        in_specs=_head_specs(S, (6,), H) + [hspec, pl.BlockSpec((S, HEAD_DIM), lambda h: (0, nh0 + h)),
                                            pl.BlockSpec((1, HEAD_DIM), lambda h: (0, h))],
        out_specs=[hspec, hspec, pl.BlockSpec((1, 8, HEAD_DIM), lambda h: (h, 0, 0))],
        out_shape=[SDS((S, H * HEAD_DIM), F32), SDS((S, H * HEAD_DIM), BF16), SDS((H, 8, HEAD_DIM), F32)],
        compiler_params=_cp(1))(proj, o_raw, dmix, w_norm)


def _ret_bwd(proj, d_out, dec_f, dec_b):
    S = proj.shape[0]
    H = proj.shape[1] // (N_GROUPS * HEAD_DIM)
    C = RET_CHUNK
    nc = S // C
    scale = HEAD_DIM ** -0.5

    def body(df_ref, db_ref, q_ref, k_ref, v_ref, do, dq_ref, dk_ref, dv_ref, small_ref, states, d_states):
        h = pl.program_id(0)
        lgs = [_log_decay(df_ref, h), _log_decay(db_ref, h)]
        consts = [_ret_consts(lg, fw) for lg, fw in zip(lgs, (True, False))]

        def prep_step(b, carry):
            rows, batch = _batch_rows(b), _batch_chunks(b)
            q3 = _chunks3(q_ref[rows, :] * scale)
            k3 = _chunks3(k_ref[rows, :])
            v3 = _chunks3(v_ref[rows, :]).astype(BF16)
            do3 = _chunks3(do[rows, :]).astype(BF16)
            for way in range(2):
                states[way, batch] = _bdot((k3 * consts[way][4]).astype(BF16), v3, _BTN)
                d_states[way, batch] = _bdot((q3 * consts[way][2]).astype(BF16), do3, _BTN)
            return carry

        lax.fori_loop(0, nc // CHUNK_BATCH, prep_step, 0)
        c_decs = [c[6] for c in consts]
        _ret_scan(states, c_decs, nc, False)
        _ret_scan(d_states, c_decs, nc, True)

        def main_step(b, dlams):
            rows, batch = _batch_rows(b), _batch_chunks(b)
            q3 = _chunks3(q_ref[rows, :] * scale)
            k3 = _chunks3(k_ref[rows, :])
            q3b, k3b = q3.astype(BF16), k3.astype(BF16)
            v3b = _chunks3(v_ref[rows, :]).astype(BF16)
            do3b = _chunks3(do[rows, :]).astype(BF16)
            a0 = _bdot(q3b, k3b, _BNT)
            pv = _bdot(do3b, v3b, _BNT)
            dq = dk = dv = None
            new_dlams = []
            for way in range(2):
                mask, relf, q_dec, q_exp, k_dec, k_exp, c_dec = consts[way]
                state, d_state = states[way, batch], d_states[way, batch]
                dp = pv * mask
                dpb = dp.astype(BF16)
                gq = _bdot(do3b, state.astype(BF16), _BNT)
                gk = _bdot(v3b, d_state.astype(BF16), _BNT)
                parts = (_bdot(dpb, k3b, _BNN) + q_dec * gq, _bdot(dpb, q3b, _BTN) + k_dec * gk,
                         _bdot((a0 * mask).astype(BF16), do3b, _BTN)
                         + _bdot((k3 * k_dec).astype(BF16), d_state.astype(BF16), _BNN))
                dq, dk, dv = parts if dq is None else (dq + parts[0], dk + parts[1], dv + parts[2])
                total = lambda x: jnp.sum(jnp.sum(x, axis=0), axis=0, keepdims=True)
                new_dlams.append(dlams[way] + total(relf * a0 * dp)
                                 + total(q_exp * q_dec * q3 * gq + k_exp * k_dec * k3 * gk)
                                 + (C * c_dec) * total(state * d_state))
            flat = lambda x: x.reshape(CHUNK_BATCH * C, HEAD_DIM)
            dq_ref[rows, :] = (flat(dq) * scale).astype(BF16)
            dk_ref[rows, :] = flat(dk).astype(BF16)
            dv_ref[rows, :] = flat(dv).astype(BF16)
            return tuple(new_dlams)

        dlams = lax.fori_loop(0, nc // CHUNK_BATCH, main_step, (jnp.zeros((1, HEAD_DIM), F32),) * 2)
        for row, (dlam, lg) in enumerate(zip(dlams, lgs)):
            small_ref[0, pl.ds(row, 1), :] = jnp.broadcast_to(jnp.sum(dlam, axis=-1, keepdims=True) * lg, (1, HEAD_DIM))
        small_ref[0, pl.ds(2, 6), :] = jnp.zeros((6, HEAD_DIM), F32)

    hspec = pl.BlockSpec((S, HEAD_DIM), lambda h: (0, h))
    smem = pl.BlockSpec(memory_space=pltpu.SMEM)
    return pl.pallas_call(
        body, grid=(H,), name="ret_bwd",
        in_specs=[smem, smem] + _head_specs(S, (3, 4, 5), H) + [hspec],
        out_specs=[hspec, hspec, hspec, pl.BlockSpec((1, 8, HEAD_DIM), lambda h: (h, 0, 0))],
        out_shape=[SDS((S, H * HEAD_DIM), BF16)] * 3 + [SDS((H, 8, HEAD_DIM), F32)],
        scratch_shapes=[pltpu.VMEM((2, nc, HEAD_DIM, HEAD_DIM), F32), pltpu.VMEM((2, nc, HEAD_DIM, HEAD_DIM), F32)],
        compiler_params=_cp(1))(dec_f, dec_b, proj, proj, proj, d_out)


def _ffn_bwd_act(dh2, wd, g, u):
    S, D = dh2.shape
    nblk, _, FB = g.shape
    tm = min(1024, S)

    def body(dh_ref, wd_ref, g_ref, u_ref, dg_ref, du_ref):
        dact = _dot(dh_ref[...], wd_ref[...], _NT)
        gg = g_ref[0].astype(F32)
        sg = _sigmoid(gg)
        dg_ref[0] = (dact * u_ref[0].astype(F32) * (sg * (1.0 + gg * (1.0 - sg)))).astype(BF16)
        du_ref[0] = (dact * (gg * sg)).astype(BF16)

    blk = pl.BlockSpec((1, tm, FB), lambda j, i: (j, i, 0))
    return pl.pallas_call(
        body, grid=(nblk, S // tm), name="ffn_bwd_act",
        in_specs=[pl.BlockSpec((tm, D), lambda j, i: (i, 0)), pl.BlockSpec((FB, D), lambda j, i: (j, 0)), blk, blk],
        out_specs=[blk, blk], out_shape=[SDS((nblk, S, FB), BF16)] * 2,
        compiler_params=_cp(2))(dh2, wd, g, u)


def _ffn_bwd_in(dg, du, wg, wu, h1, dh2, w_norm):
    nblk, S, FB = dg.shape
    D = h1.shape[1]
    tm = min(RESIDENT_ROWS, S)
    blk = pl.BlockSpec((nblk, tm, FB), lambda i: (0, i, 0))
    row = pl.BlockSpec((tm, D), lambda i: (i, 0))
    vec = pl.BlockSpec((1, D), lambda i: (0, 0))

    def gate_body(dg_ref, wg_ref, part_ref):
        part_ref[...] = _blocked_matmul(dg_ref, wg_ref)

    part = pl.pallas_call(
        gate_body, grid=(S // tm,), name="ffn_bwd_in_gate", in_specs=[blk, _resident((nblk * FB, D))],
        out_specs=row, out_shape=SDS((S, D), F32), compiler_params=_cp(1))(dg, wg.reshape(nblk * FB, D))

    def body(du_ref, wu_ref, part_ref, h_ref, dh2_ref, wn_ref, dh_ref, dhb_ref, dw_ref):
        @pl.when(pl.program_id(0) == 0)
        def _():
            dw_ref[...] = jnp.zeros_like(dw_ref)

        dh, dw = _rms_bwd(part_ref[...] + _blocked_matmul(du_ref, wu_ref), h_ref[...], wn_ref[...])
        dh = dh2_ref[...] + dh
        dh_ref[...] = dh
        dhb_ref[...] = dh.astype(BF16)
        dw_ref[...] += dw

    return pl.pallas_call(
        body, grid=(S // tm,), name="ffn_bwd_in",
        in_specs=[blk, _resident((nblk * FB, D)), row, row, row, vec],
        out_specs=[row, row, vec], out_shape=[SDS((S, D), F32), SDS((S, D), BF16), SDS((1, D), F32)],
        compiler_params=_cp(1))(du, wu.reshape(nblk * FB, D), part, h1, dh2, w_norm)


def _dmix(dh1, w_out):
    S, D = dh1.shape
    tm = min(512, S)

    def body(dh_ref, w_ref, o_ref):
        o_ref[...] = _dot(dh_ref[...], w_ref[...], _NT)

    row = pl.BlockSpec((tm, D), lambda i: (i, 0))
    return pl.pallas_call(
        body, grid=(S // tm,), name="dmix", in_specs=[row, pl.BlockSpec((D, D), lambda i: (0, 0))],
        out_specs=row, out_shape=SDS((S, D), F32), compiler_params=_cp(1))(dh1, w_out)


def _in_bwd(dproj, w_blk, x, dh1, w_norm):
    S, D = x.shape
    nblk, _, NB = w_blk.shape
    tm = min(RESIDENT_ROWS, S)

    def body(dp_ref, w_ref, x_ref, dh1_ref, wn_ref, dx_ref, dw_ref):
        @pl.when(pl.program_id(0) == 0)
        def _():
            dw_ref[...] = jnp.zeros_like(dw_ref)

        dn = None
        for j in range(nblk):
            part = _dot(dp_ref[:, pl.ds(j * NB, NB)], w_ref[j], _NT)
            dn = part if dn is None else dn + part
        dh, dw = _rms_bwd(dn, x_ref[...], wn_ref[...])
        dx_ref[...] = dh1_ref[...] + dh
        dw_ref[...] += dw

    row = pl.BlockSpec((tm, D), lambda i: (i, 0))
    vec = pl.BlockSpec((1, D), lambda i: (0, 0))
    return pl.pallas_call(
        body, grid=(S // tm,), name="in_bwd",
        in_specs=[pl.BlockSpec((tm, nblk * NB), lambda i: (i, 0)),
                  pl.BlockSpec((nblk, D, NB), lambda i: (0, 0, 0), pipeline_mode=pl.Buffered(1)), row, row, vec],
        out_specs=[row, vec], out_shape=[SDS((S, D), F32), SDS((1, D), F32)],
        compiler_params=_cp(1))(dproj, w_blk, x, dh1, w_norm)


def _wgrad(a, b, a_spec, b_spec, o_spec, o_shape, grid, name):
    nk = grid[-1]

    def ld(ref):
        return ref[0] if len(ref.shape) == 3 else ref[...]

    def body(a_ref, b_ref, o_ref, acc):
        k = pl.program_id(len(grid) - 1)

        @pl.when(k == 0)
        def _():
            acc[...] = jnp.zeros_like(acc)

        acc[...] += _dot(ld(a_ref), ld(b_ref), _TN)

        @pl.when(k == nk - 1)
        def _():
            if len(o_ref.shape) == 3:
                o_ref[0] = acc[...].astype(o_ref.dtype)
            else:
                o_ref[...] = acc[...].astype(o_ref.dtype)

    return pl.pallas_call(
        body, grid=grid, name=name, in_specs=[a_spec, b_spec], out_specs=o_spec, out_shape=SDS(o_shape, BF16),
        scratch_shapes=[pltpu.VMEM(o_spec.block_shape[-2:], F32)], compiler_params=_cp(len(grid)))(a, b)


def _peer(k):
    x, y, c = lax.axis_index("x"), lax.axis_index("y"), lax.axis_index("c")
    px = 1 - x if k & 4 else x
    py = 1 - y if k & 2 else y
    pc = 1 - c if k & 1 else c
    return (px, py, pc), 4 * px + 2 * py + pc


def _exchange_copies(srcs, lands, send_sems, recv_sems, which, gather):
    _, me = _peer(0)
    pairs = []
    for pos, a in enumerate(which):
        for k in range(1, N_DEV):
            dev, idx = _peer(k)
            sem = pos * (N_DEV - 1) + k - 1
            src = srcs[a] if gather else srcs[a].at[idx]
            mk = functools.partial(pltpu.make_async_remote_copy, src_ref=src, send_sem=send_sems.at[sem],
                                   recv_sem=recv_sems.at[sem], device_id=dev, device_id_type=MESH)
            pairs.append((mk(dst_ref=lands[a].at[me]), mk(dst_ref=lands[a].at[idx])))
    return pairs


def _sequencer_kernel(name, collective_id, n_remote, n_local):
    return pl.kernel(mesh=plsc.ScalarSubcoreMesh(axis_name="sequencer", num_cores=1), name=name,
                     scratch_types=(pltpu.SemaphoreType.DMA((n_remote,)), pltpu.SemaphoreType.DMA((n_remote,)),
                                    pltpu.SemaphoreType.DMA((n_local,))),
                     compiler_params=pltpu.CompilerParams(collective_id=collective_id))


def _handshake(ks):
    barrier = pltpu.get_barrier_semaphore()
    for k in ks:
        pl.semaphore_signal(barrier, inc=1, device_id=_peer(k)[0], device_id_type=MESH)
    pl.semaphore_wait(barrier, len(ks))


def _sequencer_scatter(arrays, name, collective_id):
    n = len(arrays)
    hbm = pltpu.MemorySpace.HBM
    srcs = [jax.new_ref(a, memory_space=hbm) for a in arrays]
    lands = [jax.empty_ref(SDS(a.shape, a.dtype), memory_space=hbm) for a in arrays]

    @_sequencer_kernel(name, collective_id, n * (N_DEV - 1), n)
    def launch(send_sems, recv_sems, local_sems):
        _handshake(range(1, N_DEV))
        _, me = _peer(0)
        local = [pltpu.make_async_copy(srcs[a].at[me], lands[a].at[me], local_sems.at[a]) for a in range(n)]
        pairs = _exchange_copies(srcs, lands, send_sems, recv_sems, range(n), False)
        for out, _ in pairs:
            out.start()
        for cp in local:
            cp.start()
        for out, arrival in pairs:
            out.wait_send()
            arrival.wait_recv()
        for cp in local:
            cp.wait()

    launch()
    return [r[...] for r in lands]


SIBLING = 1
OTHER_CHIPS = (2, 4, 6)


def _sequencer_gather(arrays, name, collective_id):
    n = len(arrays)
    hbm = pltpu.MemorySpace.HBM
    srcs = [jax.new_ref(a, memory_space=hbm) for a in arrays]
    lands = [jax.empty_ref(SDS((N_DEV,) + a.shape, a.dtype), memory_space=hbm) for a in arrays]

    @_sequencer_kernel(name, collective_id, n * (N_DEV - 1), n)
    def launch(send_sems, recv_sems, local_sems):
        _handshake((SIBLING,) + OTHER_CHIPS)
        _, me = _peer(0)
        sibling, _ = _peer(SIBLING)

        def copy(a, k, src, block, to):
            sem = a * (N_DEV - 1) + k - 1
            return pltpu.make_async_remote_copy(src_ref=src, dst_ref=lands[a].at[block], send_sem=send_sems.at[sem],
                                                recv_sem=recv_sems.at[sem], device_id=to, device_id_type=MESH)

        local = [pltpu.make_async_copy(srcs[a], lands[a].at[me], local_sems.at[a]) for a in range(n)]
        first = [copy(a, k, srcs[a], me, _peer(k)[0]) for a in range(n) for k in OTHER_CHIPS + (SIBLING,)]
        for cp in first + local:
            cp.start()
        passed = []
        for a in range(n):
            for k in OTHER_CHIPS:
                _, block = _peer(k)
                copy(a, k, srcs[a], block, sibling).wait_recv()
                passed.append(copy(a, k ^ SIBLING, lands[a].at[block], block, sibling))
                passed[-1].start()
        for a in range(n):
            for k in (SIBLING,) + tuple(k ^ SIBLING for k in OTHER_CHIPS):
                copy(a, k, srcs[a], _peer(k)[1], sibling).wait_recv()
        for cp in first + passed:
            cp.wait_send()
        for cp in local:
            cp.wait()

    launch()
    return [r[...] for r in lands]


def _sequencer_gather_chips(array, name, collective_id, chips):
    hbm = pltpu.MemorySpace.HBM
    src = jax.new_ref(array, memory_space=hbm)
    land = jax.empty_ref(SDS((2 * len(chips),) + array.shape, array.dtype), memory_space=hbm)

    @_sequencer_kernel(name, collective_id, 2 * len(chips), 1)
    def launch(send_sems, recv_sems, local_sems):
        _handshake((SIBLING,) + tuple(k for k in chips if k))
        c = lax.axis_index("c")
        sibling, _ = _peer(SIBLING)

        def copy(sem, src_ref, slot, to):
            return pltpu.make_async_remote_copy(src_ref=src_ref, dst_ref=land.at[slot], send_sem=send_sems.at[sem],
                                                recv_sem=recv_sems.at[sem], device_id=to, device_id_type=MESH)

        started = []
        for pos, k in enumerate(chips):
            started.append(copy(2 * pos, src, 2 * pos + c, _peer(k)[0] if k else sibling))
            started[-1].start()
        for pos, k in enumerate(chips):
            if k:
                copy(2 * pos, src, 2 * pos + c, sibling).wait_recv()
                started.append(copy(2 * pos + 1, land.at[2 * pos + c], 2 * pos + c, sibling))
                started[-1].start()
        for pos, k in enumerate(chips):
            copy(2 * pos + 1 if k else 2 * pos, src, 2 * pos + 1 - c, sibling).wait_recv()
        for cp in started:
            cp.wait_send()

    launch()
    return land[...]


SMALL_ROWS = 64


def _small_step(part, w, m, v):
    def body(p_ref, w_ref, m_ref, v_ref, g_ref, d_ref, nm_ref, nv_ref, gath, send_sems, recv_sems):
        _, me = _peer(0)
        gath[me] = p_ref[...]
        copies = []
        for k in range(1, N_DEV):
            dev, idx = _peer(k)
            out = pltpu.make_async_remote_copy(src_ref=p_ref, dst_ref=gath.at[me], send_sem=send_sems.at[k - 1],
                                               recv_sem=recv_sems.at[k - 1], device_id=dev, device_id_type=MESH)
            out.start()
            arrival = pltpu.make_async_remote_copy(src_ref=p_ref, dst_ref=gath.at[idx], send_sem=send_sems.at[k - 1],
                                                   recv_sem=recv_sems.at[k - 1], device_id=dev, device_id_type=MESH)
            copies.append((out, arrival))
        for out, arrival in copies:
            out.wait_send()
            arrival.wait_recv()
        g = gath[0]
        for p in range(1, N_DEV):
            g = g + gath[p]
        g_ref[...] = g
        d_ref[...], nm_ref[...], nv_ref[...] = _adamw(w_ref[...], g, m_ref[...], v_ref[...])

    vm = pl.BlockSpec(memory_space=pltpu.VMEM)
    return pl.pallas_call(
        body, name="small_step", in_specs=[vm] * 4, out_specs=[vm] * 4,
        out_shape=[SDS((SMALL_ROWS, 128), F32)] * 4,
        scratch_shapes=[pltpu.VMEM((N_DEV, SMALL_ROWS, 128), F32), pltpu.SemaphoreType.DMA((N_DEV - 1,)),
                        pltpu.SemaphoreType.DMA((N_DEV - 1,))])(part, w, m, v)


def _adamw(w, g, m, v):
    m = ADAM_B1 * m + (1.0 - ADAM_B1) * g
    v = ADAM_B2 * v + (1.0 - ADAM_B2) * (g * g)
    m_hat = m / (1.0 - ADAM_B1 ** ADAM_STEP)
    v_hat = v / (1.0 - ADAM_B2 ** ADAM_STEP)
    delta = -ADAM_LR * (m_hat / (jnp.sqrt(v_hat) + ADAM_EPS) + ADAM_WD * w)
    return delta, m, v


def _adamw_block(parts, w, m, v, name):
    R, C = w.shape
    n_parts = len(parts)
    Rp = R // n_parts
    tr = next(t for t in (256, 128, 64, 32, 16, 8) if Rp % t == 0 and t * C <= 256 * 1024)
    per_part = Rp // tr

    def body(*refs):
        p_refs = refs[:n_parts]
        w_ref, m_ref, v_ref, g_ref, d_ref, nm_ref, nv_ref = refs[n_parts:]
        for k, p_ref in enumerate(p_refs):
            @pl.when(pl.program_id(0) // per_part == k)
            def _(p_ref=p_ref):
                g = p_ref[0].astype(F32)
                for p in range(1, N_DEV):
                    g = g + p_ref[p].astype(F32)
                g_ref[...] = g
                d_ref[...], nm_ref[...], nv_ref[...] = _adamw(w_ref[...], g, m_ref[...], v_ref[...])

    row = pl.BlockSpec((tr, C), lambda i: (i, 0))
    part_specs = [pl.BlockSpec((N_DEV, tr, C), functools.partial(
        lambda i, k: (0, jnp.clip(i - k * per_part, 0, per_part - 1), 0), k=k)) for k in range(n_parts)]
    return pl.pallas_call(
        body, grid=(R // tr,), name=name, in_specs=part_specs + [row, row, row],
        out_specs=[row] * 4, out_shape=[SDS((R, C), F32)] * 4, compiler_params=_cp(1))(*parts, w, m, v)


def _pack_small(mix, ffn, fin, retw, dec_f, dec_b, loss):
    flat = jnp.concatenate([mix.reshape(-1), ffn.reshape(-1), fin.reshape(-1), retw.reshape(-1), dec_f.reshape(-1),
                            dec_b.reshape(-1), loss.reshape(-1)])
    return jnp.pad(flat, (0, SMALL_ROWS * 128 - flat.shape[0])).reshape(SMALL_ROWS, 128)


def _unpack_small(packed, shapes):
    flat = packed.reshape(-1)
    out, at = [], 0
    for s in shapes:
        n = math.prod(s)
        out.append(flat[at:at + n].reshape(s))
        at += n
    return out


def kernel(x, norm_mix_w, w_in, ret_decay_fwd, ret_decay_bwd, ret_norm_w, w_out, norm_ffn_w, w_gate, w_up, w_down, norm_final_w, loss_target, m_norm_mix_w, m_w_in, m_ret_decay_fwd, m_ret_decay_bwd, m_ret_norm_w, m_w_out, m_norm_ffn_w, m_w_gate, m_w_up, m_w_down, m_norm_final_w, v_norm_mix_w, v_w_in, v_ret_decay_fwd, v_ret_decay_bwd, v_ret_norm_w, v_w_out, v_norm_ffn_w, v_w_gate, v_w_up, v_w_down, v_norm_final_w):
    x2 = x[0]
    tgt = loss_target[0]
    S, D = x2.shape
    H = ret_norm_w.shape[1] // HEAD_DIM
    DA = H * HEAD_DIM
    fin_w = norm_final_w.reshape(1, D)
    big = (w_in[0], w_out[0], w_gate[0].T, w_up[0].T, w_down[0])

    big_b = [w.astype(BF16) for w in big]
    stages = ((0,), (4, 2), (6,))
    wi_stages = [_sequencer_gather_chips(big_b[0], name, cid, ks)
                 for name, cid, ks in zip(("gather_in_own", "gather_in_near", "gather_in_far"), (0, 7, 8), stages)]
    wo, wg, wu = _sequencer_gather(big_b[1:4], "gather_mid", 1)
    wd, = _sequencer_gather(big_b[4:], "gather_down", 5)
    NB = big_b[0].shape[1]
    ax, ay = lax.axis_index("x"), lax.axis_index("y")
    chip_of = {k: 2 * (1 - ax if k & 4 else ax) + (1 - ay if k & 2 else ay) for k in (0, 2, 4, 6)}

    n1 = _norm_fwd(x2, norm_mix_w)
    ac = lax.axis_index("c")
    me = 2 * chip_of[0] + ac
    vec = lambda *v: jnp.stack([jnp.asarray(t, jnp.int32) for t in v])
    proj = _proj_part(n1, big_b[0][None], vec(0), vec(me), None, N_DEV, "proj_self")
    wi = lax.dynamic_update_slice(lax.empty((N_DEV, D, NB), BF16), big_b[0][None], (me, 0, 0))
    for ks, w_st, name in zip(stages, wi_stages, ("proj_sibling", "proj_near", "proj_far")):
        slots, blocks = [], []
        for pos, k in enumerate(ks):
            for core in ((1 - ac,) if k == 0 else (0, 1)):
                slots.append(2 * pos + core)
                blocks.append(2 * chip_of[k] + core)
                wi = lax.dynamic_update_slice(wi, lax.dynamic_slice_in_dim(w_st, slots[-1], 1), (blocks[-1], 0, 0))
        proj = _proj_part(n1, w_st, vec(*slots), vec(*blocks), proj, N_DEV, name)
    bias = _attn_bias()[:H]
    attn, lse = _attn_fwd(proj, bias)
    ret, o_raw = _ret_fwd(proj, ret_decay_fwd, ret_decay_bwd, ret_norm_w)
    wo_full = wo.reshape(D, D)
    d_ff = N_DEV * wd.shape[1]
    FB = FFN_BLOCK if d_ff % FFN_BLOCK == 0 else wd.shape[1]
    n_fb = d_ff // FB
    wg, wu = wg.reshape(n_fb, FB, D), wu.reshape(n_fb, FB, D)
    wd_full = wd.reshape(d_ff, D)
    h1, mixed, n2 = _out_fwd(x2, attn, ret, wo_full, norm_ffn_w)
    gate, up, act = _ffn_up(n2, wg, wu)
    dh2, dh2_b, loss_parts, g_fin = _ffn_down_loss(act, wd_full, h1, tgt, fin_w)

    dgate, dup = _ffn_bwd_act(dh2_b, wd_full, gate, up)
    tn = min(1024, D)
    ffn_specs = (pl.BlockSpec((1, S, FB), lambda j, n, k: (j, 0, 0)), pl.BlockSpec((S, tn), lambda j, n, k: (0, n)),
                 pl.BlockSpec((1, FB, tn), lambda j, n, k: (j, 0, n)), (n_fb, FB, D), (n_fb, D // tn, 1))
    per_dev = (N_DEV, d_ff // N_DEV, D)
    g_wd = _wgrad(act, dh2_b, *ffn_specs, "wgrad_down").reshape(per_dev)
    g_wg = _wgrad(dgate, n2, *ffn_specs, "wgrad_gate").reshape(per_dev)
    g_wu = _wgrad(dup, n2, *ffn_specs, "wgrad_up").reshape(per_dev)
    parts_f = _sequencer_scatter([g_wg, g_wu, g_wd], "scatter_ffn", 2)
    dh1, dh1_b, g_ffn = _ffn_bwd_in(dgate, dup, wg, wu, h1, dh2, norm_ffn_w)
    dmix = _dmix(dh1_b, wo_full)
    tmw = min(512, D)
    tk = min(2048, S)
    g_wo = _wgrad(mixed, dh1_b, pl.BlockSpec((tk, tmw), lambda m, k: (k, m)), pl.BlockSpec((tk, D), lambda m, k: (k, 0)),
                  pl.BlockSpec((tmw, D), lambda m, k: (m, 0)), (D, D), (D // tmw, S // tk), "wgrad_out")
    parts_o = _sequencer_scatter([g_wo.reshape(N_DEV, D // N_DEV, D)], "scatter_out", 3)
    d_ret, dg_r, small_w = _ret_gate_bwd(proj, o_raw, dmix, ret_norm_w, DA)
    dq_r, dk_r, dv_r, small = _ret_bwd(proj, d_ret, ret_decay_fwd, ret_decay_bwd)
    dq_a, dk_a, dv_a = _attn_bwd(proj, attn, lse, dmix, bias)
    dproj = jnp.concatenate([t.astype(BF16) for t in (dq_a, dk_a, dv_a, dq_r, dk_r, dv_r, dg_r)], axis=1)
    half = D // tmw // 2
    parts_i = []
    for part, (name, cid) in enumerate((("in_lo", 4), ("in_hi", 6))):
        g_wi = _wgrad(n1, dproj, pl.BlockSpec((S, tmw), functools.partial(lambda j, m, k, off: (0, m + off), off=part * half)),
                      pl.BlockSpec((S, NB), lambda j, m, k: (0, j)), pl.BlockSpec((1, tmw, NB), lambda j, m, k: (j, m, 0)),
                      (N_DEV, D // 2, NB), (N_DEV, half, 1), "wgrad_" + name)
        parts_i += _sequencer_scatter([g_wi], "scatter_" + name, cid)
    grad_x, g_mix = _in_bwd(dproj, wi, x2, dh1, norm_mix_w)

    big_m = (m_w_in[0], m_w_out[0], m_w_gate[0].T, m_w_up[0].T, m_w_down[0])
    big_v = (v_w_in[0], v_w_out[0], v_w_gate[0].T, v_w_up[0].T, v_w_down[0])
    names = ("adamw_in", "adamw_out", "adamw_gate", "adamw_up", "adamw_down")
    upd = [None] * 5
    for a, p in zip((2, 3, 4, 1, 0), [[t] for t in parts_f + parts_o] + [parts_i]):
        upd[a] = _adamw_block(p, big[a], big_m[a], big_v[a], names[a])

    g_dec_f = small[:, 0, 0].reshape(1, H)
    g_dec_b = small[:, 1, 0].reshape(1, H)
    g_retw = small_w[:, 0, :].reshape(1, DA)
    loss_local = jnp.sum(loss_parts[::8, 0])
    zero = jnp.zeros((1,), F32)
    part = _pack_small(g_mix, g_ffn, g_fin, g_retw, g_dec_f, g_dec_b, loss_local)
    sw = _pack_small(norm_mix_w, norm_ffn_w, norm_final_w, ret_norm_w, ret_decay_fwd, ret_decay_bwd, zero)
    sm = _pack_small(m_norm_mix_w, m_norm_ffn_w, m_norm_final_w, m_ret_norm_w, m_ret_decay_fwd, m_ret_decay_bwd, zero)
    sv = _pack_small(v_norm_mix_w, v_norm_ffn_w, v_norm_final_w, v_ret_norm_w, v_ret_decay_fwd, v_ret_decay_bwd, zero)
    shapes = [(1, D), (1, D), (D,), (1, DA), (1, H), (1, H), ()]
    sg, sd, snm, snv = [_unpack_small(t, shapes) for t in _small_step(part, sw, sm, sv)]
    loss = sg[6]

    def ordered(small_set, k):
        b = [(u[k].T if a in (2, 3) else u[k])[None] for a, u in enumerate(upd)]
        return [small_set[0], b[0], small_set[4], small_set[5], small_set[3], b[1], small_set[1], b[2], b[3], b[4],
                small_set[2]]

    return (loss, grad_x[None], *ordered(sg, 0), *ordered(sd, 1), *ordered(snm, 2), *ordered(snv, 3))
```

```python
import functools
import math

import numpy as np
import jax
import jax.numpy as jnp
from jax import lax
from jax.experimental import pallas as pl
from jax.experimental.pallas import tpu as pltpu
from jax.experimental.pallas import tpu_sc as plsc

F32 = jnp.float32
BF16 = jnp.bfloat16
SDS = jax.ShapeDtypeStruct

HEAD_DIM = 128
EPS = 1e-6
RET_CHUNK = 128
DILATIONS = (1, 4, 16)
BAND = 64
Q_TILE = 128
K_TILE = Q_TILE + 2 * BAND
KV_PAD = BAND * 4
TILE_GROUP = 8
NEG = -1e30
N_DEV = 8
N_GROUPS = 7
ADAM_LR, ADAM_B1, ADAM_B2, ADAM_EPS, ADAM_WD, ADAM_STEP = 0.001, 0.9, 0.999, 1e-08, 0.01, 10
VMEM_LIMIT = 56 * 1024 * 1024
MESH = pl.DeviceIdType.MESH
ANY = pl.BlockSpec(memory_space=pl.ANY)


def _cp(n_grid):
    return pltpu.CompilerParams(dimension_semantics=("arbitrary",) * n_grid, vmem_limit_bytes=VMEM_LIMIT)


def _sigmoid(x):
    return 1.0 / (1.0 + jnp.exp(-x))


def _rms_scale(h):
    return lax.rsqrt(jnp.mean(h * h, axis=-1, keepdims=True) + EPS)


def _rms_bwd(dn, h, w):
    r = _rms_scale(h)
    gw = dn * w
    dh = r * gw - h * (r * r * r) * jnp.mean(gw * h, axis=-1, keepdims=True)
    return dh, jnp.sum(dn * h * r, axis=0, keepdims=True)


def _dot(a, b, dims):
    return lax.dot_general(a.astype(BF16), b.astype(BF16), (dims, ((), ())), preferred_element_type=F32)


_NN = ((1,), (0,))
_NT = ((1,), (1,))
_TN = ((0,), (0,))


RESIDENT_ROWS = 256


def _resident(shape):
    return pl.BlockSpec(shape, lambda i: (0, 0), pipeline_mode=pl.Buffered(1))


def _blocked_matmul(a_ref, w_ref):
    nblk, _, fb = a_ref.shape
    out = None
    for j in range(nblk):
        part = jnp.dot(a_ref[j], w_ref[pl.ds(j * fb, fb), :], preferred_element_type=F32)
        out = part if out is None else out + part
    return out


def _norm_fwd(x, w_norm):
    S, D = x.shape
    tm = min(1024, S)

    def body(x_ref, wn_ref, n_ref):
        xf = x_ref[...]
        n_ref[...] = (xf * _rms_scale(xf) * wn_ref[...]).astype(BF16)

    row = pl.BlockSpec((tm, D), lambda i: (i, 0))
    return pl.pallas_call(body, grid=(S // tm,), name="norm_fwd", in_specs=[row, pl.BlockSpec((1, D), lambda i: (0, 0))],
                          out_specs=row, out_shape=SDS((S, D), BF16), compiler_params=_cp(1))(x, w_norm)


def _proj_part(n1, w_slots, slots, blocks, proj, n_blocks, name):
    S, D = n1.shape
    NB = w_slots.shape[2]
    tm = min(1024, S)

    def body(slots_ref, blocks_ref, n_ref, w_ref, *rest):
        rest[-1][...] = jnp.dot(n_ref[...], w_ref[0], preferred_element_type=F32)

    out_spec = pl.BlockSpec((tm, NB), lambda i, j, slots, blocks: (i, blocks[j]))
    in_specs = [pl.BlockSpec((tm, D), lambda i, j, slots, blocks: (i, 0)),
                pl.BlockSpec((1, D, NB), lambda i, j, slots, blocks: (slots[j], 0, 0))]
    args = [n1, w_slots]
    if proj is not None:
        in_specs.append(ANY)
        args.append(proj)
    return pl.pallas_call(
        body, name=name, out_shape=SDS((S, n_blocks * NB), F32),
        grid_spec=pltpu.PrefetchScalarGridSpec(num_scalar_prefetch=2, grid=(S // tm, slots.shape[0]), in_specs=in_specs,
                                               out_specs=out_spec),
        input_output_aliases={} if proj is None else {4: 0},
        compiler_params=_cp(2))(slots, blocks, *args)


def _out_fwd(x, attn, ret, w_out, w_norm):
    S, D = x.shape
    DA = attn.shape[1]
    tm = min(256, S)

    def body(x_ref, a_ref, r_ref, w_ref, wn_ref, h_ref, mix_ref, n_ref):
        a = a_ref[...].astype(BF16)
        r = r_ref[...].astype(BF16)
        mix_ref[:, :DA] = a
        mix_ref[:, DA:] = r
        h = x_ref[...] + jnp.dot(a, w_ref[:DA, :], preferred_element_type=F32) \
            + jnp.dot(r, w_ref[DA:, :], preferred_element_type=F32)
        h_ref[...] = h
        n_ref[...] = (h * _rms_scale(h) * wn_ref[...]).astype(BF16)

    row = lambda w: pl.BlockSpec((tm, w), lambda i: (i, 0))
    return pl.pallas_call(
        body, grid=(S // tm,), name="out_fwd",
        in_specs=[row(D), row(DA), row(D - DA), pl.BlockSpec((D, D), lambda i: (0, 0)),
                  pl.BlockSpec((1, D), lambda i: (0, 0))],
        out_specs=[row(D), row(D), row(D)],
        out_shape=[SDS((S, D), F32), SDS((S, D), BF16), SDS((S, D), BF16)],
        compiler_params=_cp(1))(x, attn, ret, w_out, w_norm)


def _ffn_up(n2, wg, wu):
    S, D = n2.shape
    nblk, FB, _ = wg.shape
    tm = min(1024, S)

    def body(n_ref, wg_ref, wu_ref, g_ref, u_ref, a_ref):
        n = n_ref[...]
        g = _dot(n, wg_ref[0], _NT)
        u = _dot(n, wu_ref[0], _NT)
        g_ref[0] = g.astype(BF16)
        u_ref[0] = u.astype(BF16)
        a_ref[0] = (g * _sigmoid(g) * u).astype(BF16)

    wspec = pl.BlockSpec((1, FB, D), lambda j, i: (j, 0, 0))
    ospec = pl.BlockSpec((1, tm, FB), lambda j, i: (j, i, 0))
    return pl.pallas_call(
        body, grid=(nblk, S // tm), name="ffn_up",
        in_specs=[pl.BlockSpec((tm, D), lambda j, i: (i, 0)), wspec, wspec],
        out_specs=[ospec, ospec, ospec],
        out_shape=[SDS((nblk, S, FB), BF16)] * 3,
        compiler_params=_cp(2))(n2, wg, wu)


def _ffn_down_loss(act, wd, h1, target, w_norm):
    nblk, S, FB = act.shape
    D = h1.shape[1]
    tm = min(RESIDENT_ROWS, S)

    def body(a_ref, wd_ref, h_ref, t_ref, wn_ref, dh_ref, dhb_ref, loss_ref, dw_ref):
        @pl.when(pl.program_id(0) == 0)
        def _():
            dw_ref[...] = jnp.zeros_like(dw_ref)

        h = h_ref[...] + _blocked_matmul(a_ref, wd_ref)
        w = wn_ref[...]
        err = h * _rms_scale(h) * w - t_ref[...]
        loss_ref[...] = jnp.full(loss_ref.shape, 0.5 * jnp.sum(err * err) / D, F32)
        dh, dw = _rms_bwd(err * (1.0 / D), h, w)
        dh_ref[...] = dh
        dhb_ref[...] = dh.astype(BF16)
        dw_ref[...] += dw

    row = pl.BlockSpec((tm, D), lambda i: (i, 0))
    vec = pl.BlockSpec((1, D), lambda i: (0, 0))
    return pl.pallas_call(
        body, grid=(S // tm,), name="ffn_down_loss",
        in_specs=[pl.BlockSpec((nblk, tm, FB), lambda i: (0, i, 0)), _resident((nblk * FB, D)), row, row, vec],
        out_specs=[row, row, pl.BlockSpec((8, 128), lambda i: (i, 0)), vec],
        out_shape=[SDS((S, D), F32), SDS((S, D), BF16), SDS((S // tm * 8, 128), F32), SDS((1, D), F32)],
        compiler_params=_cp(1))(act, wd, h1, target, w_norm)


def _attn_bias():
    n_heads = 8
    slopes = np.exp2(-8.0 * np.arange(1, n_heads + 1, dtype=np.float32) / n_heads)
    dist = np.abs(np.arange(K_TILE)[None, :] - BAND - np.arange(Q_TILE)[:, None])
    out = np.empty((n_heads, len(DILATIONS), Q_TILE, K_TILE), np.float32)
    for h in range(n_heads):
        for p, d in enumerate(DILATIONS):
            out[h, p] = np.where(dist <= BAND, -slopes[h] * (d * dist).astype(np.float32), NEG)
    return jnp.asarray(out)


def _attn_tiles(S, d):
    L = S // d
    per_class = L // Q_TILE
    return L, per_class, d * per_class


def _tile_rows(t, d, per_class):
    r = t // per_class
    a = (t % per_class) * Q_TILE
    q_rows = pl.ds(r + d * a, Q_TILE, stride=d) if d > 1 else pl.ds(pl.multiple_of(a, Q_TILE), Q_TILE)
    k_rows = pl.ds(KV_PAD + r + d * (a - BAND), K_TILE, stride=d) if d > 1 else pl.ds(
        pl.multiple_of(KV_PAD + a - BAND, BAND), K_TILE)
    return a, q_rows, k_rows


def _to_quarters(dst, src, n, dst_off=0):
    for r in range(4):
        dst[pl.ds(dst_off + r * (n // 4), n // 4), :] = src[pl.ds(r, n // 4, stride=4), :]


def _quarter_tile_rows(t, S):
    L = S // 16
    per_class = L // Q_TILE
    blk, tt = t // (4 * per_class), t % (4 * per_class)
    r, a = tt // per_class, (tt % per_class) * Q_TILE
    q_rows = pl.ds(blk * (S // 4) + r + 4 * a, Q_TILE, stride=4)
    k_rows = pl.ds(KV_PAD + blk * (S // 4) + r + 4 * (a - BAND), K_TILE, stride=4)
    return a, q_rows, k_rows


def _lanes(x, width):
    return jnp.concatenate([x] * (width // HEAD_DIM), axis=-1)


_BNT = (((2,), (2,)), ((0,), (0,)))
_BNN = (((2,), (1,)), ((0,), (0,)))
_BTN = (((1,), (1,)), ((0,), (0,)))


def _bdot(a, b, dims):
    return lax.dot_general(a, b, dims, preferred_element_type=F32)


def _stacked(rows, loaders):
    return [jnp.stack([f(*r) for r in rows]) for f in loaders]


def _edge_mask(a, L):
    lk = lax.broadcasted_iota(jnp.int32, (1, K_TILE), 1) + (a - BAND)
    return jnp.where((lk >= 0) & (lk < L), 0.0, NEG).astype(F32)


def _fill_padded(dst, src, S):
    dst[pl.ds(0, KV_PAD), :] = jnp.zeros((KV_PAD, HEAD_DIM), F32)
    dst[pl.ds(KV_PAD + S, KV_PAD), :] = jnp.zeros((KV_PAD, HEAD_DIM), F32)
    dst[pl.ds(KV_PAD, S), :] = src[...]


def _head_specs(S, groups, n_heads):
    return [pl.BlockSpec((S, HEAD_DIM), functools.partial(lambda h, g: (0, g * n_heads + h), g=g)) for g in groups]


def _attn_fwd(proj, bias):
    S = proj.shape[0]
    H = proj.shape[1] // (N_GROUPS * HEAD_DIM)
    scale = HEAD_DIM ** -0.5

    def body(q_ref, k_ref, v_ref, b_ref, o_ref, lse_ref, kp, vp, m_run, l_run, q4, m3, l3, acc3):
        _fill_padded(kp, k_ref, S)
        _fill_padded(vp, v_ref, S)
        o_ref[...] = jnp.zeros_like(o_ref)
        m_run[...] = jnp.full(m_run.shape, NEG, F32)
        l_run[...] = jnp.zeros_like(l_run)
        for p, d in enumerate(DILATIONS[:2]):
            L, per_class, n_tiles = _attn_tiles(S, d)

            def tiles(t, carry, p=p, d=d, L=L, per_class=per_class, n_tiles=n_tiles):
                rows = [_tile_rows(t + u * (n_tiles // TILE_GROUP), d, per_class) for u in range(TILE_GROUP)]
                qs, ks, vs, m_old, l_old, o_old, edge = _stacked(rows, (
                    lambda a, qr, kr: q_ref[qr, :].astype(BF16), lambda a, qr, kr: kp[kr, :].astype(BF16),
                    lambda a, qr, kr: vp[kr, :].astype(BF16), lambda a, qr, kr: m_run[qr, :],
                    lambda a, qr, kr: l_run[qr, :], lambda a, qr, kr: o_ref[qr, :], lambda a, qr, kr: _edge_mask(a, L)))
                s = _bdot(qs, ks, _BNT) * scale + b_ref[0, p][None] + edge
                m_new = jnp.maximum(m_old, jnp.max(s, axis=-1, keepdims=True))
                pr = jnp.exp(s - _lanes(m_new, K_TILE)).astype(BF16)
                alpha = jnp.exp(m_old - m_new)
                l_new = alpha * l_old + _bdot(pr, jnp.ones((TILE_GROUP, K_TILE, HEAD_DIM), BF16), _BNN)
                o_new = alpha * o_old + _bdot(pr, vs, _BNN)
                for u, (_, qr, _) in enumerate(rows):
                    o_ref[qr, :] = o_new[u]
                    m_run[qr, :] = m_new[u]
                    l_run[qr, :] = l_new[u]
                return carry

            lax.fori_loop(0, n_tiles // TILE_GROUP, tiles, 0)

        _to_quarters(q4, q_ref, S)
        _to_quarters(kp, k_ref, S, KV_PAD)
        _to_quarters(vp, v_ref, S, KV_PAD)
        n_tiles = _attn_tiles(S, DILATIONS[2])[2]

        def tiles3(t, carry):
            rows = [_quarter_tile_rows(t + u * (n_tiles // TILE_GROUP), S) for u in range(TILE_GROUP)]
            qs, ks, vs, edge = _stacked(rows, (
                lambda a, qr, kr: q4[qr, :].astype(BF16), lambda a, qr, kr: kp[kr, :].astype(BF16),
                lambda a, qr, kr: vp[kr, :].astype(BF16), lambda a, qr, kr: _edge_mask(a, S // DILATIONS[2])))
            s = _bdot(qs, ks, _BNT) * scale + b_ref[0, 2][None] + edge
            m_new = jnp.broadcast_to(jnp.max(s, axis=-1, keepdims=True), (TILE_GROUP, Q_TILE, HEAD_DIM))
            pr = jnp.exp(s - _lanes(m_new, K_TILE)).astype(BF16)
            l_new = _bdot(pr, jnp.ones((TILE_GROUP, K_TILE, HEAD_DIM), BF16), _BNN)
            o_new = _bdot(pr, vs, _BNN)
            for u, (_, qr, _) in enumerate(rows):
                acc3[qr, :] = o_new[u]
                m3[qr, :] = m_new[u]
                l3[qr, :] = l_new[u]
            return carry

        lax.fori_loop(0, n_tiles // TILE_GROUP, tiles3, 0)
        for r in range(4):
            nat, qtr = pl.ds(r, S // 4, stride=4), pl.ds(r * (S // 4), S // 4)
            m_a, m_b = m_run[nat, :], m3[qtr, :]
            m = jnp.maximum(m_a, m_b)
            w_a, w_b = jnp.exp(m_a - m), jnp.exp(m_b - m)
            l = w_a * l_run[nat, :] + w_b * l3[qtr, :]
            o_ref[nat, :] = (w_a * o_ref[nat, :] + w_b * acc3[qtr, :]) / l
            lse_ref[nat, :] = m + jnp.log(l)

    hspec = pl.BlockSpec((S, HEAD_DIM), lambda h: (0, h))
    padded, plain = pltpu.VMEM((S + 2 * KV_PAD, HEAD_DIM), F32), pltpu.VMEM((S, HEAD_DIM), F32)
    return pl.pallas_call(
        body, grid=(H,), name="attn_fwd",
        in_specs=_head_specs(S, (0, 1, 2), H) + [
            pl.BlockSpec((1, len(DILATIONS), Q_TILE, K_TILE), lambda h: (h, 0, 0, 0))],
        out_specs=[hspec, hspec],
        out_shape=[SDS((S, H * HEAD_DIM), F32), SDS((S, H * HEAD_DIM), F32)],
        scratch_shapes=[padded, padded] + [plain] * 6,
        compiler_params=_cp(1))(proj, proj, proj, bias)


def _attn_bwd(proj, out, lse, dmix, bias):
    S = proj.shape[0]
    H = proj.shape[1] // (N_GROUPS * HEAD_DIM)
    scale = HEAD_DIM ** -0.5
    assert S // DILATIONS[2] >= 2 * Q_TILE

    def body(q_ref, k_ref, v_ref, o_ref, lse_ref, do_ref, b_ref, dq_ref, dk_ref, dv_ref,
             kp, vp, dkp, dvp, dsum, q4, do4, lse4, dsum4):
        _fill_padded(kp, k_ref, S)
        _fill_padded(vp, v_ref, S)
        dkp[...] = jnp.zeros_like(dkp)
        dvp[...] = jnp.zeros_like(dvp)
        dq_ref[...] = jnp.zeros_like(dq_ref)
        dsum[...] = jnp.broadcast_to(jnp.sum(do_ref[...] * o_ref[...], axis=-1, keepdims=True), dsum.shape)

        def run(n_tiles, tile_rows, p, L, q_src, do_src, lse_src, dsum_src, dq_dst, dq_adds):
            def tiles(t, carry):
                rows = [tile_rows(t + u * (n_tiles // TILE_GROUP)) for u in range(TILE_GROUP)]
                qs, ks, vs, dos, lses, dsums, dk_old, dv_old, edge = _stacked(rows, (
                    lambda a, qr, kr: q_src[qr, :].astype(BF16), lambda a, qr, kr: kp[kr, :].astype(BF16),
                    lambda a, qr, kr: vp[kr, :].astype(BF16), lambda a, qr, kr: do_src[qr, :].astype(BF16),
                    lambda a, qr, kr: lse_src[qr, :], lambda a, qr, kr: dsum_src[qr, :],
                    lambda a, qr, kr: dkp[kr, :], lambda a, qr, kr: dvp[kr, :], lambda a, qr, kr: _edge_mask(a, L)))
                s = _bdot(qs, ks, _BNT) * scale + b_ref[0, p][None] + edge
                pr = jnp.exp(s - _lanes(lses, K_TILE))
                ds = (pr * (_bdot(dos, vs, _BNT) - _lanes(dsums, K_TILE)) * scale).astype(BF16)
                dq_new = _bdot(ds, ks, _BNN)
                if dq_adds:
                    dq_new = dq_new + jnp.stack([dq_dst[qr, :] for _, qr, _ in rows])
                dk_new = dk_old + _bdot(ds, qs, _BTN)
                dv_new = dv_old + _bdot(pr.astype(BF16), dos, _BTN)
                for u, (_, qr, kr) in enumerate(rows):
                    dq_dst[qr, :] = dq_new[u]
                    dkp[kr, :] = dk_new[u]
                    dvp[kr, :] = dv_new[u]
                return carry

            lax.fori_loop(0, n_tiles // TILE_GROUP, tiles, 0)

        for p, d in enumerate(DILATIONS[:2]):
            L, per_class, n_tiles = _attn_tiles(S, d)
            run(n_tiles, functools.partial(_tile_rows, d=d, per_class=per_class), p, L,
                q_ref, do_ref, lse_ref, dsum, dq_ref, True)
        dk_ref[...] = dkp[pl.ds(KV_PAD, S), :]
        dv_ref[...] = dvp[pl.ds(KV_PAD, S), :]

        for dst, src in ((q4, q_ref), (do4, do_ref), (lse4, lse_ref), (dsum4, dsum)):
            _to_quarters(dst, src, S)
        _to_quarters(kp, k_ref, S, KV_PAD)
        _to_quarters(vp, v_ref, S, KV_PAD)
        dkp[...] = jnp.zeros_like(dkp)
        dvp[...] = jnp.zeros_like(dvp)
        dq3 = dsum
        run(_attn_tiles(S, DILATIONS[2])[2], functools.partial(_quarter_tile_rows, S=S), 2, S // DILATIONS[2],
            q4, do4, lse4, dsum4, dq3, False)
        for r in range(4):
            nat, qtr = pl.ds(r, S // 4, stride=4), pl.ds(r * (S // 4), S // 4)
            pad_qtr = pl.ds(KV_PAD + r * (S // 4), S // 4)
            dq_ref[nat, :] = dq_ref[nat, :] + dq3[qtr, :]
            dk_ref[nat, :] = dk_ref[nat, :] + dkp[pad_qtr, :]
            dv_ref[nat, :] = dv_ref[nat, :] + dvp[pad_qtr, :]

    hspec = pl.BlockSpec((S, HEAD_DIM), lambda h: (0, h))
    once =pl.BlockSpec((S, HEAD_DIM), lambda h: (0, h), pipeline_mode=pl.Buffered(1))
    padded, plain = pltpu.VMEM((S + 2 * KV_PAD, HEAD_DIM), F32), pltpu.VMEM((S, HEAD_DIM), F32)
    return pl.pallas_call(
        body, grid=(H,), name="attn_bwd",
        in_specs=_head_specs(S, (0, 1, 2), H) + [once, once, once,
                                                  pl.BlockSpec((1, len(DILATIONS), Q_TILE, K_TILE), lambda h: (h, 0, 0, 0))],
        out_specs=[hspec, hspec, hspec],
        out_shape=[SDS((S, H * HEAD_DIM), F32)] * 3,
        scratch_shapes=[padded] * 4 + [plain] * 5,
        compiler_params=_cp(1))(proj, proj, proj, out, lse, dmix, bias)


def _ret_consts(lg, forward):
    C = RET_CHUNK
    i = lax.broadcasted_iota(jnp.int32, (C, C), 0)
    j = lax.broadcasted_iota(jnp.int32, (C, C), 1)
    rel = (i - j) if forward else (j - i)
    inside = (rel >= 0) if forward else (rel > 0)
    relf = jnp.maximum(rel, 0).astype(F32)
    mask = jnp.where(inside, jnp.exp(lg * relf), 0.0)
    idx = lax.broadcasted_iota(jnp.int32, (C, 1), 0).astype(F32)
    q_exp = (idx + 1.0) if forward else (C - idx)
    k_exp = (C - 1.0 - idx) if forward else idx
    return mask, relf, jnp.exp(lg * q_exp), q_exp, jnp.exp(lg * k_exp), k_exp, jnp.exp(lg * C)


def _log_decay(dec_ref, h):
    return -jnp.exp(jnp.full((1, 1), dec_ref[0, h], F32))


FFN_BLOCK = 704
CHUNK_BATCH = 8


def _batch_rows(b):
    n = CHUNK_BATCH * RET_CHUNK
    return pl.ds(pl.multiple_of(b * n, n), n)


def _batch_chunks(b):
    return pl.ds(pl.multiple_of(b * CHUNK_BATCH, CHUNK_BATCH), CHUNK_BATCH)


def _chunks3(x):
    return x.reshape(CHUNK_BATCH, RET_CHUNK, HEAD_DIM)


def _ret_scan(buf, c_decs, nc, reverse):
    def step(n, carry):
        new = []
        for way, r in enumerate(carry):
            c = n if (way == 0) != reverse else nc - 1 - n
            term = buf[way, c]
            buf[way, c] = r
            new.append(r * c_decs[way] + term)
        return tuple(new)

    lax.fori_loop(0, nc, step, (jnp.zeros((HEAD_DIM, HEAD_DIM), F32),) * 2)


def _ret_fwd(proj, dec_f, dec_b, w_norm):
    S = proj.shape[0]
    H = proj.shape[1] // (N_GROUPS * HEAD_DIM)
    nc = S // RET_CHUNK
    scale = HEAD_DIM ** -0.5

    def body(df_ref, db_ref, q_ref, k_ref, v_ref, g_ref, w_ref, y_ref, o_ref, states):
        h = pl.program_id(0)
        consts = [_ret_consts(_log_decay(dref, h), fw) for fw, dref in ((True, df_ref), (False, db_ref))]

        def kv_step(b, carry):
            rows, batch = _batch_rows(b), _batch_chunks(b)
            k3 = _chunks3(k_ref[rows, :])
            v3 = _chunks3(v_ref[rows, :]).astype(BF16)
            for way in range(2):
                states[way, batch] = _bdot((k3 * consts[way][4]).astype(BF16), v3, _BTN)
            return carry

        lax.fori_loop(0, nc // CHUNK_BATCH, kv_step, 0)
        _ret_scan(states, [c[6] for c in consts], nc, False)

        def out_step(b, carry):
            rows, batch = _batch_rows(b), _batch_chunks(b)
            q3 = _chunks3(q_ref[rows, :] * scale)
            k3 = _chunks3(k_ref[rows, :]).astype(BF16)
            v3 = _chunks3(v_ref[rows, :]).astype(BF16)
            a0 = _bdot(q3.astype(BF16), k3, _BNT)
            o = None
            for way in range(2):
                mask, q_dec = consts[way][0], consts[way][2]
                part = _bdot((a0 * mask).astype(BF16), v3, _BNN) \
                    + _bdot((q3 * q_dec).astype(BF16), states[way, batch].astype(BF16), _BNN)
                o = part if o is None else o + part
            o_ref[rows, :] = o.reshape(CHUNK_BATCH * RET_CHUNK, HEAD_DIM)
            return carry

        lax.fori_loop(0, nc // CHUNK_BATCH, out_step, 0)
        o = o_ref[...]
        g = g_ref[...]
        y_ref[...] = o * _rms_scale(o) * w_ref[...] * (g * _sigmoid(g))

    hspec = pl.BlockSpec((S, HEAD_DIM), lambda h: (0, h))
    smem = pl.BlockSpec(memory_space=pltpu.SMEM)
    return pl.pallas_call(
        body, grid=(H,), name="ret_fwd",
        in_specs=[smem, smem] + _head_specs(S, (3, 4, 5, 6), H) + [pl.BlockSpec((1, HEAD_DIM), lambda h: (0, h))],
        out_specs=[hspec, hspec],
        out_shape=[SDS((S, H * HEAD_DIM), F32)] * 2,
        scratch_shapes=[pltpu.VMEM((2, nc, HEAD_DIM, HEAD_DIM), F32)],
        compiler_params=_cp(1))(dec_f, dec_b, proj, proj, proj, proj, w_norm)


def _ret_gate_bwd(proj, o_raw, dmix, w_norm, col0):
    S = proj.shape[0]
    H = proj.shape[1] // (N_GROUPS * HEAD_DIM)

    def body(g_ref, o_ref, dy_ref, w_ref, do_ref, dg_ref, dw_ref):
        o = o_ref[...]
        g = g_ref[...]
        dy = dy_ref[...]
        w = w_ref[...]
        rr = _rms_scale(o)
        normed = o * rr
        sg = _sigmoid(g)
        silu = g * sg
        dw_ref[0] = jnp.broadcast_to(jnp.sum(dy * normed * silu, axis=0, keepdims=True), (8, HEAD_DIM))
        dg_ref[...] = (dy * normed * w * (sg * (1.0 + g * (1.0 - sg)))).astype(BF16)
        dnormed = dy * w * silu
        do_ref[...] = rr * dnormed - o * (rr * rr * rr) * jnp.mean(dnormed * o, axis=-1, keepdims=True)

    hspec = pl.BlockSpec((S, HEAD_DIM), lambda h: (0, h))
    nh0 = col0 // HEAD_DIM
    return pl.pallas_call(
        body, grid=(H,), name="ret_gate_bwd",
        in_specs=_head_specs(S, (6,), H) + [hspec, pl.BlockSpec((S, HEAD_DIM), lambda h: (0, nh0 + h)),
                                            pl.BlockSpec((1, HEAD_DIM), lambda h: (0, h))],
        out_specs=[hspec, hspec, pl.BlockSpec((1, 8, HEAD_DIM), lambda h: (h, 0, 0))],
        out_shape=[SDS((S, H * HEAD_DIM), F32), SDS((S, H * HEAD_DIM), BF16), SDS((H, 8, HEAD_DIM), F32)],
        compiler_params=_cp(1))(proj, o_raw, dmix, w_norm)


def _ret_bwd(proj, d_out, dec_f, dec_b):
    S = proj.shape[0]
    H = proj.shape[1] // (N_GROUPS * HEAD_DIM)
    C = RET_CHUNK
    nc = S // C
    scale = HEAD_DIM ** -0.5

    def body(df_ref, db_ref, q_ref, k_ref, v_ref, do, dq_ref, dk_ref, dv_ref, small_ref, states, d_states):
        h = pl.program_id(0)
        lgs = [_log_decay(df_ref, h), _log_decay(db_ref, h)]
        consts = [_ret_consts(lg, fw) for lg, fw in zip(lgs, (True, False))]

        def prep_step(b, carry):
            rows, batch = _batch_rows(b), _batch_chunks(b)
            q3 = _chunks3(q_ref[rows, :] * scale)
            k3 = _chunks3(k_ref[rows, :])
            v3 = _chunks3(v_ref[rows, :]).astype(BF16)
            do3 = _chunks3(do[rows, :]).astype(BF16)
            for way in range(2):
                states[way, batch] = _bdot((k3 * consts[way][4]).astype(BF16), v3, _BTN)
                d_states[way, batch] = _bdot((q3 * consts[way][2]).astype(BF16), do3, _BTN)
            return carry

        lax.fori_loop(0, nc // CHUNK_BATCH, prep_step, 0)
        c_decs = [c[6] for c in consts]
        _ret_scan(states, c_decs, nc, False)
        _ret_scan(d_states, c_decs, nc, True)

        def main_step(b, dlams):
            rows, batch = _batch_rows(b), _batch_chunks(b)
            q3 = _chunks3(q_ref[rows, :] * scale)
            k3 = _chunks3(k_ref[rows, :])
            q3b, k3b = q3.astype(BF16), k3.astype(BF16)
            v3b = _chunks3(v_ref[rows, :]).astype(BF16)
            do3b = _chunks3(do[rows, :]).astype(BF16)
            a0 = _bdot(q3b, k3b, _BNT)
            pv = _bdot(do3b, v3b, _BNT)
            dq = dk = dv = None
            new_dlams = []
            for way in range(2):
                mask, relf, q_dec, q_exp, k_dec, k_exp, c_dec = consts[way]
                state, d_state = states[way, batch], d_states[way, batch]
                dp = pv * mask
                dpb = dp.astype(BF16)
                gq = _bdot(do3b, state.astype(BF16), _BNT)
                gk = _bdot(v3b, d_state.astype(BF16), _BNT)
                parts = (_bdot(dpb, k3b, _BNN) + q_dec * gq, _bdot(dpb, q3b, _BTN) + k_dec * gk,
                         _bdot((a0 * mask).astype(BF16), do3b, _BTN)
                         + _bdot((k3 * k_dec).astype(BF16), d_state.astype(BF16), _BNN))
                dq, dk, dv = parts if dq is None else (dq + parts[0], dk + parts[1], dv + parts[2])
                total = lambda x: jnp.sum(jnp.sum(x, axis=0), axis=0, keepdims=True)
                new_dlams.append(dlams[way] + total(relf * a0 * dp)
                                 + total(q_exp * q_dec * q3 * gq + k_exp * k_dec * k3 * gk)
                                 + (C * c_dec) * total(state * d_state))
            flat = lambda x: x.reshape(CHUNK_BATCH * C, HEAD_DIM)
            dq_ref[rows, :] = (flat(dq) * scale).astype(BF16)
            dk_ref[rows, :] = flat(dk).astype(BF16)
            dv_ref[rows, :] = flat(dv).astype(BF16)
            return tuple(new_dlams)

        dlams = lax.fori_loop(0, nc // CHUNK_BATCH, main_step, (jnp.zeros((1, HEAD_DIM), F32),) * 2)
        for row, (dlam, lg) in enumerate(zip(dlams, lgs)):
            small_ref[0, pl.ds(row, 1), :] = jnp.broadcast_to(jnp.sum(dlam, axis=-1, keepdims=True) * lg, (1, HEAD_DIM))
        small_ref[0, pl.ds(2, 6), :] = jnp.zeros((6, HEAD_DIM), F32)

    hspec = pl.BlockSpec((S, HEAD_DIM), lambda h: (0, h))
    smem = pl.BlockSpec(memory_space=pltpu.SMEM)
    return pl.pallas_call(
        body, grid=(H,), name="ret_bwd",
        in_specs=[smem, smem] + _head_specs(S, (3, 4, 5), H) + [hspec],
        out_specs=[hspec, hspec, hspec, pl.BlockSpec((1, 8, HEAD_DIM), lambda h: (h, 0, 0))],
        out_shape=[SDS((S, H * HEAD_DIM), BF16)] * 3 + [SDS((H, 8, HEAD_DIM), F32)],
        scratch_shapes=[pltpu.VMEM((2, nc, HEAD_DIM, HEAD_DIM), F32), pltpu.VMEM((2, nc, HEAD_DIM, HEAD_DIM), F32)],
        compiler_params=_cp(1))(dec_f, dec_b, proj, proj, proj, d_out)


def _ffn_bwd_act(dh2, wd, g, u):
    S, D = dh2.shape
    nblk, _, FB = g.shape
    tm = min(1024, S)

    def body(dh_ref, wd_ref, g_ref, u_ref, dg_ref, du_ref):
        dact = _dot(dh_ref[...], wd_ref[...], _NT)
        gg = g_ref[0].astype(F32)
        sg = _sigmoid(gg)
        dg_ref[0] = (dact * u_ref[0].astype(F32) * (sg * (1.0 + gg * (1.0 - sg)))).astype(BF16)
        du_ref[0] = (dact * (gg * sg)).astype(BF16)

    blk = pl.BlockSpec((1, tm, FB), lambda j, i: (j, i, 0))
    return pl.pallas_call(
        body, grid=(nblk, S // tm), name="ffn_bwd_act",
        in_specs=[pl.BlockSpec((tm, D), lambda j, i: (i, 0)), pl.BlockSpec((FB, D), lambda j, i: (j, 0)), blk, blk],
        out_specs=[blk, blk], out_shape=[SDS((nblk, S, FB), BF16)] * 2,
        compiler_params=_cp(2))(dh2, wd, g, u)


def _ffn_bwd_in(dg, du, wg, wu, h1, dh2, w_norm):
    nblk, S, FB = dg.shape
    D = h1.shape[1]
    tm = min(RESIDENT_ROWS, S)
    blk = pl.BlockSpec((nblk, tm, FB), lambda i: (0, i, 0))
    row = pl.BlockSpec((tm, D), lambda i: (i, 0))
    vec = pl.BlockSpec((1, D), lambda i: (0, 0))

    def gate_body(dg_ref, wg_ref, part_ref):
        part_ref[...] = _blocked_matmul(dg_ref, wg_ref)

    part = pl.pallas_call(
        gate_body, grid=(S // tm,), name="ffn_bwd_in_gate", in_specs=[blk, _resident((nblk * FB, D))],
        out_specs=row, out_shape=SDS((S, D), F32), compiler_params=_cp(1))(dg, wg.reshape(nblk * FB, D))

    def body(du_ref, wu_ref, part_ref, h_ref, dh2_ref, wn_ref, dh_ref, dhb_ref, dw_ref):
        @pl.when(pl.program_id(0) == 0)
        def _():
            dw_ref[...] = jnp.zeros_like(dw_ref)

        dh, dw = _rms_bwd(part_ref[...] + _blocked_matmul(du_ref, wu_ref), h_ref[...], wn_ref[...])
        dh = dh2_ref[...] + dh
        dh_ref[...] = dh
        dhb_ref[...] = dh.astype(BF16)
        dw_ref[...] += dw

    return pl.pallas_call(
        body, grid=(S // tm,), name="ffn_bwd_in",
        in_specs=[blk, _resident((nblk * FB, D)), row, row, row, vec],
        out_specs=[row, row, vec], out_shape=[SDS((S, D), F32), SDS((S, D), BF16), SDS((1, D), F32)],
        compiler_params=_cp(1))(du, wu.reshape(nblk * FB, D), part, h1, dh2, w_norm)


def _dmix(dh1, w_out):
    S, D = dh1.shape
    tm = min(512, S)

    def body(dh_ref, w_ref, o_ref):
        o_ref[...] = _dot(dh_ref[...], w_ref[...], _NT)

    row = pl.BlockSpec((tm, D), lambda i: (i, 0))
    return pl.pallas_call(
        body, grid=(S // tm,), name="dmix", in_specs=[row, pl.BlockSpec((D, D), lambda i: (0, 0))],
        out_specs=row, out_shape=SDS((S, D), F32), compiler_params=_cp(1))(dh1, w_out)


def _in_bwd(dproj, w_blk, x, dh1, w_norm):
    S, D = x.shape
    nblk, _, NB = w_blk.shape
    tm = min(RESIDENT_ROWS, S)

    def body(dp_ref, w_ref, x_ref, dh1_ref, wn_ref, dx_ref, dw_ref):
        @pl.when(pl.program_id(0) == 0)
        def _():
            dw_ref[...] = jnp.zeros_like(dw_ref)

        dn = None
        for j in range(nblk):
            part = _dot(dp_ref[:, pl.ds(j * NB, NB)], w_ref[j], _NT)
            dn = part if dn is None else dn + part
        dh, dw = _rms_bwd(dn, x_ref[...], wn_ref[...])
        dx_ref[...] = dh1_ref[...] + dh
        dw_ref[...] += dw

    row = pl.BlockSpec((tm, D), lambda i: (i, 0))
    vec = pl.BlockSpec((1, D), lambda i: (0, 0))
    return pl.pallas_call(
        body, grid=(S // tm,), name="in_bwd",
        in_specs=[pl.BlockSpec((tm, nblk * NB), lambda i: (i, 0)),
                  pl.BlockSpec((nblk, D, NB), lambda i: (0, 0, 0), pipeline_mode=pl.Buffered(1)), row, row, vec],
        out_specs=[row, vec], out_shape=[SDS((S, D), F32), SDS((1, D), F32)],
        compiler_params=_cp(1))(dproj, w_blk, x, dh1, w_norm)


def _wgrad(a, b, a_spec, b_spec, o_spec, o_shape, grid, name):
    nk = grid[-1]

    def ld(ref):
        return ref[0] if len(ref.shape) == 3 else ref[...]

    def body(a_ref, b_ref, o_ref, acc):
        k = pl.program_id(len(grid) - 1)

        @pl.when(k == 0)
        def _():
            acc[...] = jnp.zeros_like(acc)

        acc[...] += _dot(ld(a_ref), ld(b_ref), _TN)

        @pl.when(k == nk - 1)
        def _():
            if len(o_ref.shape) == 3:
                o_ref[0] = acc[...].astype(o_ref.dtype)
            else:
                o_ref[...] = acc[...].astype(o_ref.dtype)

    return pl.pallas_call(
        body, grid=grid, name=name, in_specs=[a_spec, b_spec], out_specs=o_spec, out_shape=SDS(o_shape, BF16),
        scratch_shapes=[pltpu.VMEM(o_spec.block_shape[-2:], F32)], compiler_params=_cp(len(grid)))(a, b)


def _peer(k):
    x, y, c = lax.axis_index("x"), lax.axis_index("y"), lax.axis_index("c")
    px = 1 - x if k & 4 else x
    py = 1 - y if k & 2 else y
    pc = 1 - c if k & 1 else c
    return (px, py, pc), 4 * px + 2 * py + pc


def _exchange_copies(srcs, lands, send_sems, recv_sems, which, gather):
    _, me = _peer(0)
    pairs = []
    for pos, a in enumerate(which):
        for k in range(1, N_DEV):
            dev, idx = _peer(k)
            sem = pos * (N_DEV - 1) + k - 1
            src = srcs[a] if gather else srcs[a].at[idx]
            mk = functools.partial(pltpu.make_async_remote_copy, src_ref=src, send_sem=send_sems.at[sem],
                                   recv_sem=recv_sems.at[sem], device_id=dev, device_id_type=MESH)
            pairs.append((mk(dst_ref=lands[a].at[me]), mk(dst_ref=lands[a].at[idx])))
    return pairs


def _sequencer_kernel(name, collective_id, n_remote, n_local):
    return pl.kernel(mesh=plsc.ScalarSubcoreMesh(axis_name="sequencer", num_cores=1), name=name,
                     scratch_types=(pltpu.SemaphoreType.DMA((n_remote,)), pltpu.SemaphoreType.DMA((n_remote,)),
                                    pltpu.SemaphoreType.DMA((n_local,))),
                     compiler_params=pltpu.CompilerParams(collective_id=collective_id))


def _handshake(ks):
    barrier = pltpu.get_barrier_semaphore()
    for k in ks:
        pl.semaphore_signal(barrier, inc=1, device_id=_peer(k)[0], device_id_type=MESH)
    pl.semaphore_wait(barrier, len(ks))


def _sequencer_scatter(arrays, name, collective_id):
    n = len(arrays)
    hbm = pltpu.MemorySpace.HBM
    srcs = [jax.new_ref(a, memory_space=hbm) for a in arrays]
    lands = [jax.empty_ref(SDS(a.shape, a.dtype), memory_space=hbm) for a in arrays]

    @_sequencer_kernel(name, collective_id, n * (N_DEV - 1), n)
    def launch(send_sems, recv_sems, local_sems):
        _handshake(range(1, N_DEV))
        _, me = _peer(0)
        local = [pltpu.make_async_copy(srcs[a].at[me], lands[a].at[me], local_sems.at[a]) for a in range(n)]
        pairs = _exchange_copies(srcs, lands, send_sems, recv_sems, range(n), False)
        for out, _ in pairs:
            out.start()
        for cp in local:
            cp.start()
        for out, arrival in pairs:
            out.wait_send()
            arrival.wait_recv()
        for cp in local:
            cp.wait()

    launch()
    return [r[...] for r in lands]


SIBLING = 1
OTHER_CHIPS = (2, 4, 6)


def _sequencer_gather(arrays, name, collective_id):
    n = len(arrays)
    hbm = pltpu.MemorySpace.HBM
    srcs = [jax.new_ref(a, memory_space=hbm) for a in arrays]
    lands = [jax.empty_ref(SDS((N_DEV,) + a.shape, a.dtype), memory_space=hbm) for a in arrays]

    @_sequencer_kernel(name, collective_id, n * (N_DEV - 1), n)
    def launch(send_sems, recv_sems, local_sems):
        _handshake((SIBLING,) + OTHER_CHIPS)
        _, me = _peer(0)
        sibling, _ = _peer(SIBLING)

        def copy(a, k, src, block, to):
            sem = a * (N_DEV - 1) + k - 1
            return pltpu.make_async_remote_copy(src_ref=src, dst_ref=lands[a].at[block], send_sem=send_sems.at[sem],
                                                recv_sem=recv_sems.at[sem], device_id=to, device_id_type=MESH)

        local = [pltpu.make_async_copy(srcs[a], lands[a].at[me], local_sems.at[a]) for a in range(n)]
        first = [copy(a, k, srcs[a], me, _peer(k)[0]) for a in range(n) for k in OTHER_CHIPS + (SIBLING,)]
        for cp in first + local:
            cp.start()
        passed = []
        for a in range(n):
            for k in OTHER_CHIPS:
                _, block = _peer(k)
                copy(a, k, srcs[a], block, sibling).wait_recv()
                passed.append(copy(a, k ^ SIBLING, lands[a].at[block], block, sibling))
                passed[-1].start()
        for a in range(n):
            for k in (SIBLING,) + tuple(k ^ SIBLING for k in OTHER_CHIPS):
                copy(a, k, srcs[a], _peer(k)[1], sibling).wait_recv()
        for cp in first + passed:
            cp.wait_send()
        for cp in local:
            cp.wait()

    launch()
    return [r[...] for r in lands]


def _sequencer_gather_chips(array, name, collective_id, chips):
    hbm = pltpu.MemorySpace.HBM
    src = jax.new_ref(array, memory_space=hbm)
    land = jax.empty_ref(SDS((2 * len(chips),) + array.shape, array.dtype), memory_space=hbm)

    @_sequencer_kernel(name, collective_id, 2 * len(chips), 1)
    def launch(send_sems, recv_sems, local_sems):
        _handshake((SIBLING,) + tuple(k for k in chips if k))
        c = lax.axis_index("c")
        sibling, _ = _peer(SIBLING)

        def copy(sem, src_ref, slot, to):
            return pltpu.make_async_remote_copy(src_ref=src_ref, dst_ref=land.at[slot], send_sem=send_sems.at[sem],
                                                recv_sem=recv_sems.at[sem], device_id=to, device_id_type=MESH)

        started = []
        for pos, k in enumerate(chips):
            started.append(copy(2 * pos, src, 2 * pos + c, _peer(k)[0] if k else sibling))
            started[-1].start()
        for pos, k in enumerate(chips):
            if k:
                copy(2 * pos, src, 2 * pos + c, sibling).wait_recv()
                started.append(copy(2 * pos + 1, land.at[2 * pos + c], 2 * pos + c, sibling))
                started[-1].start()
        for pos, k in enumerate(chips):
            copy(2 * pos + 1 if k else 2 * pos, src, 2 * pos + 1 - c, sibling).wait_recv()
        for cp in started:
            cp.wait_send()

    launch()
    return land[...]


SMALL_ROWS = 64


def _small_step(part, w, m, v):
    def body(p_ref, w_ref, m_ref, v_ref, g_ref, d_ref, nm_ref, nv_ref, gath, send_sems, recv_sems):
        _, me = _peer(0)
        gath[me] = p_ref[...]
        copies = []
        for k in range(1, N_DEV):
            dev, idx = _peer(k)
            out = pltpu.make_async_remote_copy(src_ref=p_ref, dst_ref=gath.at[me], send_sem=send_sems.at[k - 1],
                                               recv_sem=recv_sems.at[k - 1], device_id=dev, device_id_type=MESH)
            out.start()
            arrival = pltpu.make_async_remote_copy(src_ref=p_ref, dst_ref=gath.at[idx], send_sem=send_sems.at[k - 1],
                                                   recv_sem=recv_sems.at[k - 1], device_id=dev, device_id_type=MESH)
            copies.append((out, arrival))
        for out, arrival in copies:
            out.wait_send()
            arrival.wait_recv()
        g = gath[0]
        for p in range(1, N_DEV):
            g = g + gath[p]
        g_ref[...] = g
        d_ref[...], nm_ref[...], nv_ref[...] = _adamw(w_ref[...], g, m_ref[...], v_ref[...])

    vm = pl.BlockSpec(memory_space=pltpu.VMEM)
    return pl.pallas_call(
        body, name="small_step", in_specs=[vm] * 4, out_specs=[vm] * 4,
        out_shape=[SDS((SMALL_ROWS, 128), F32)] * 4,
        scratch_shapes=[pltpu.VMEM((N_DEV, SMALL_ROWS, 128), F32), pltpu.SemaphoreType.DMA((N_DEV - 1,)),
                        pltpu.SemaphoreType.DMA((N_DEV - 1,))])(part, w, m, v)


def _adamw(w, g, m, v):
    m = ADAM_B1 * m + (1.0 - ADAM_B1) * g
    v = ADAM_B2 * v + (1.0 - ADAM_B2) * (g * g)
    m_hat = m / (1.0 - ADAM_B1 ** ADAM_STEP)
    v_hat = v / (1.0 - ADAM_B2 ** ADAM_STEP)
    delta = -ADAM_LR * (m_hat / (jnp.sqrt(v_hat) + ADAM_EPS) + ADAM_WD * w)
    return delta, m, v


def _adamw_block(parts, w, m, v, name):
    R, C = w.shape
    n_parts = len(parts)
    Rp = R // n_parts
    tr = next(t for t in (256, 128, 64, 32, 16, 8) if Rp % t == 0 and t * C <= 256 * 1024)
    per_part = Rp // tr

    def body(*refs):
        p_refs = refs[:n_parts]
        w_ref, m_ref, v_ref, g_ref, d_ref, nm_ref, nv_ref = refs[n_parts:]
        for k, p_ref in enumerate(p_refs):
            @pl.when(pl.program_id(0) // per_part == k)
            def _(p_ref=p_ref):
                g = p_ref[0].astype(F32)
                for p in range(1, N_DEV):
                    g = g + p_ref[p].astype(F32)
                g_ref[...] = g
                d_ref[...], nm_ref[...], nv_ref[...] = _adamw(w_ref[...], g, m_ref[...], v_ref[...])

    row = pl.BlockSpec((tr, C), lambda i: (i, 0))
    part_specs = [pl.BlockSpec((N_DEV, tr, C), functools.partial(
        lambda i, k: (0, jnp.clip(i - k * per_part, 0, per_part - 1), 0), k=k)) for k in range(n_parts)]
    return pl.pallas_call(
        body, grid=(R // tr,), name=name, in_specs=part_specs + [row, row, row],
        out_specs=[row] * 4, out_shape=[SDS((R, C), F32)] * 4, compiler_params=_cp(1))(*parts, w, m, v)


def _pack_small(mix, ffn, fin, retw, dec_f, dec_b, loss):
    flat = jnp.concatenate([mix.reshape(-1), ffn.reshape(-1), fin.reshape(-1), retw.reshape(-1), dec_f.reshape(-1),
                            dec_b.reshape(-1), loss.reshape(-1)])
    return jnp.pad(flat, (0, SMALL_ROWS * 128 - flat.shape[0])).reshape(SMALL_ROWS, 128)


def _unpack_small(packed, shapes):
    flat = packed.reshape(-1)
    out, at = [], 0
    for s in shapes:
        n = math.prod(s)
        out.append(flat[at:at + n].reshape(s))
        at += n
    return out


def kernel(x, norm_mix_w, w_in, ret_decay_fwd, ret_decay_bwd, ret_norm_w, w_out, norm_ffn_w, w_gate, w_up, w_down, norm_final_w, loss_target, m_norm_mix_w, m_w_in, m_ret_decay_fwd, m_ret_decay_bwd, m_ret_norm_w, m_w_out, m_norm_ffn_w, m_w_gate, m_w_up, m_w_down, m_norm_final_w, v_norm_mix_w, v_w_in, v_ret_decay_fwd, v_ret_decay_bwd, v_ret_norm_w, v_w_out, v_norm_ffn_w, v_w_gate, v_w_up, v_w_down, v_norm_final_w):
    x2 = x[0]
    tgt = loss_target[0]
    S, D = x2.shape
    H = ret_norm_w.shape[1] // HEAD_DIM
    DA = H * HEAD_DIM
    fin_w = norm_final_w.reshape(1, D)
    big = (w_in[0], w_out[0], w_gate[0].T, w_up[0].T, w_down[0])

    big_b = [w.astype(BF16) for w in big]
    stages = ((0,), (4, 2), (6,))
    wi_stages = [_sequencer_gather_chips(big_b[0], name, cid, ks)
                 for name, cid, ks in zip(("gather_in_own", "gather_in_near", "gather_in_far"), (0, 7, 8), stages)]
    wo, = _sequencer_gather(big_b[1:2], "gather_out", 1)
    wg, wu = _sequencer_gather(big_b[2:4], "gather_gate_up", 9)
    wd, = _sequencer_gather(big_b[4:], "gather_down", 5)
    NB = big_b[0].shape[1]
    ax, ay = lax.axis_index("x"), lax.axis_index("y")
    chip_of = {k: 2 * (1 - ax if k & 4 else ax) + (1 - ay if k & 2 else ay) for k in (0, 2, 4, 6)}

    n1 = _norm_fwd(x2, norm_mix_w)
    ac = lax.axis_index("c")
    me = 2 * chip_of[0] + ac
    vec = lambda *v: jnp.stack([jnp.asarray(t, jnp.int32) for t in v])
    proj = _proj_part(n1, big_b[0][None], vec(0), vec(me), None, N_DEV, "proj_self")
    wi = lax.dynamic_update_slice(lax.empty((N_DEV, D, NB), BF16), big_b[0][None], (me, 0, 0))
    for ks, w_st, name in zip(stages, wi_stages, ("proj_sibling", "proj_near", "proj_far")):
        slots, blocks = [], []
        for pos, k in enumerate(ks):
            for core in ((1 - ac,) if k == 0 else (0, 1)):
                slots.append(2 * pos + core)
                blocks.append(2 * chip_of[k] + core)
                wi = lax.dynamic_update_slice(wi, lax.dynamic_slice_in_dim(w_st, slots[-1], 1), (blocks[-1], 0, 0))
        proj = _proj_part(n1, w_st, vec(*slots), vec(*blocks), proj, N_DEV, name)
    bias = _attn_bias()[:H]
    attn, lse = _attn_fwd(proj, bias)
    ret, o_raw = _ret_fwd(proj, ret_decay_fwd, ret_decay_bwd, ret_norm_w)
    wo_full = wo.reshape(D, D)
    d_ff = N_DEV * wd.shape[1]
    FB = FFN_BLOCK if d_ff % FFN_BLOCK == 0 else wd.shape[1]
    n_fb = d_ff // FB
    wg, wu = wg.reshape(n_fb, FB, D), wu.reshape(n_fb, FB, D)
    wd_full = wd.reshape(d_ff, D)
    h1, mixed, n2 = _out_fwd(x2, attn, ret, wo_full, norm_ffn_w)
    gate, up, act = _ffn_up(n2, wg, wu)
    dh2, dh2_b, loss_parts, g_fin = _ffn_down_loss(act, wd_full, h1, tgt, fin_w)

    dgate, dup = _ffn_bwd_act(dh2_b, wd_full, gate, up)
    tn = min(1024, D)
    ffn_specs = (pl.BlockSpec((1, S, FB), lambda j, n, k: (j, 0, 0)), pl.BlockSpec((S, tn), lambda j, n, k: (0, n)),
                 pl.BlockSpec((1, FB, tn), lambda j, n, k: (j, 0, n)), (n_fb, FB, D), (n_fb, D // tn, 1))
    per_dev = (N_DEV, d_ff // N_DEV, D)
    g_wd = _wgrad(act, dh2_b, *ffn_specs, "wgrad_down").reshape(per_dev)
    g_wg = _wgrad(dgate, n2, *ffn_specs, "wgrad_gate").reshape(per_dev)
    g_wu = _wgrad(dup, n2, *ffn_specs, "wgrad_up").reshape(per_dev)
    parts_f = _sequencer_scatter([g_wg, g_wu, g_wd], "scatter_ffn", 2)
    dh1, dh1_b, g_ffn = _ffn_bwd_in(dgate, dup, wg, wu, h1, dh2, norm_ffn_w)
    dmix = _dmix(dh1_b, wo_full)
    tmw = min(512, D)
    tk = min(2048, S)
    g_wo = _wgrad(mixed, dh1_b, pl.BlockSpec((tk, tmw), lambda m, k: (k, m)), pl.BlockSpec((tk, D), lambda m, k: (k, 0)),
                  pl.BlockSpec((tmw, D), lambda m, k: (m, 0)), (D, D), (D // tmw, S // tk), "wgrad_out")
    parts_o = _sequencer_scatter([g_wo.reshape(N_DEV, D // N_DEV, D)], "scatter_out", 3)
    d_ret, dg_r, small_w = _ret_gate_bwd(proj, o_raw, dmix, ret_norm_w, DA)
    dq_r, dk_r, dv_r, small = _ret_bwd(proj, d_ret, ret_decay_fwd, ret_decay_bwd)
    dq_a, dk_a, dv_a = _attn_bwd(proj, attn, lse, dmix, bias)
    dproj = jnp.concatenate([t.astype(BF16) for t in (dq_a, dk_a, dv_a, dq_r, dk_r, dv_r, dg_r)], axis=1)
    half = D // tmw // 2
    parts_i = []
    for part, (name, cid) in enumerate((("in_lo", 4), ("in_hi", 6))):
        g_wi = _wgrad(n1, dproj, pl.BlockSpec((S, tmw), functools.partial(lambda j, m, k, off: (0, m + off), off=part * half)),
                      pl.BlockSpec((S, NB), lambda j, m, k: (0, j)), pl.BlockSpec((1, tmw, NB), lambda j, m, k: (j, m, 0)),
                      (N_DEV, D // 2, NB), (N_DEV, half, 1), "wgrad_" + name)
        parts_i += _sequencer_scatter([g_wi], "scatter_" + name, cid)
    grad_x, g_mix = _in_bwd(dproj, wi, x2, dh1, norm_mix_w)

    big_m = (m_w_in[0], m_w_out[0], m_w_gate[0].T, m_w_up[0].T, m_w_down[0])
    big_v = (v_w_in[0], v_w_out[0], v_w_gate[0].T, v_w_up[0].T, v_w_down[0])
    names = ("adamw_in", "adamw_out", "adamw_gate", "adamw_up", "adamw_down")
    upd = [None] * 5
    for a, p in zip((2, 3, 4, 1, 0), [[t] for t in parts_f + parts_o] + [parts_i]):
        upd[a] = _adamw_block(p, big[a], big_m[a], big_v[a], names[a])

    g_dec_f = small[:, 0, 0].reshape(1, H)
    g_dec_b = small[:, 1, 0].reshape(1, H)
    g_retw = small_w[:, 0, :].reshape(1, DA)
    loss_local = jnp.sum(loss_parts[::8, 0])
    zero = jnp.zeros((1,), F32)
    part = _pack_small(g_mix, g_ffn, g_fin, g_retw, g_dec_f, g_dec_b, loss_local)
    sw = _pack_small(norm_mix_w, norm_ffn_w, norm_final_w, ret_norm_w, ret_decay_fwd, ret_decay_bwd, zero)
    sm = _pack_small(m_norm_mix_w, m_norm_ffn_w, m_norm_final_w, m_ret_norm_w, m_ret_decay_fwd, m_ret_decay_bwd, zero)
    sv = _pack_small(v_norm_mix_w, v_norm_ffn_w, v_norm_final_w, v_ret_norm_w, v_ret_decay_fwd, v_ret_decay_bwd, zero)
    shapes = [(1, D), (1, D), (D,), (1, DA), (1, H), (1, H), ()]
    sg, sd, snm, snv = [_unpack_small(t, shapes) for t in _small_step(part, sw, sm, sv)]
    loss = sg[6]

    def ordered(small_set, k):
        b = [(u[k].T if a in (2, 3) else u[k])[None] for a, u in enumerate(upd)]
        return [small_set[0], b[0], small_set[4], small_set[5], small_set[3], b[1], small_set[1], b[2], b[3], b[4],
                small_set[2]]

    return (loss, grad_x[None], *ordered(sg, 0), *ordered(sd, 1), *ordered(snm, 2), *ordered(snv, 3))
```

```python
import functools
import math

import numpy as np
import jax
import jax.numpy as jnp
from jax import lax
from jax.experimental import pallas as pl
from jax.experimental.pallas import tpu as pltpu
from jax.experimental.pallas import tpu_sc as plsc

F32 = jnp.float32
BF16 = jnp.bfloat16
SDS = jax.ShapeDtypeStruct

HEAD_DIM = 128
EPS = 1e-6
RET_CHUNK = 128
DILATIONS = (1, 4, 16)
BAND = 64
Q_TILE = 128
K_TILE = Q_TILE + 2 * BAND
KV_PAD = BAND * 4
TILE_GROUP = 8
NEG = -1e30
N_DEV = 8
N_GROUPS = 7
ADAM_LR, ADAM_B1, ADAM_B2, ADAM_EPS, ADAM_WD, ADAM_STEP = 0.001, 0.9, 0.999, 1e-08, 0.01, 10
VMEM_LIMIT = 56 * 1024 * 1024
MESH = pl.DeviceIdType.MESH
ANY = pl.BlockSpec(memory_space=pl.ANY)


def _cp(n_grid):
    return pltpu.CompilerParams(dimension_semantics=("arbitrary",) * n_grid, vmem_limit_bytes=VMEM_LIMIT)


def _sigmoid(x):
    return 1.0 / (1.0 + jnp.exp(-x))


def _rms_scale(h):
    return lax.rsqrt(jnp.mean(h * h, axis=-1, keepdims=True) + EPS)


def _rms_bwd(dn, h, w):
    r = _rms_scale(h)
    gw = dn * w
    dh = r * gw - h * (r * r * r) * jnp.mean(gw * h, axis=-1, keepdims=True)
    return dh, jnp.sum(dn * h * r, axis=0, keepdims=True)


def _dot(a, b, dims):
    return lax.dot_general(a.astype(BF16), b.astype(BF16), (dims, ((), ())), preferred_element_type=F32)


_NN = ((1,), (0,))
_NT = ((1,), (1,))
_TN = ((0,), (0,))


RESIDENT_ROWS = 256


def _resident(shape):
    return pl.BlockSpec(shape, lambda i: (0, 0), pipeline_mode=pl.Buffered(1))


def _blocked_matmul(a_ref, w_ref):
    nblk, _, fb = a_ref.shape
    out = None
    for j in range(nblk):
        part = jnp.dot(a_ref[j], w_ref[pl.ds(j * fb, fb), :], preferred_element_type=F32)
        out = part if out is None else out + part
    return out


def _norm_fwd(x, w_norm):
    S, D = x.shape
    tm = min(1024, S)

    def body(x_ref, wn_ref, n_ref):
        xf = x_ref[...]
        n_ref[...] = (xf * _rms_scale(xf) * wn_ref[...]).astype(BF16)

    row = pl.BlockSpec((tm, D), lambda i: (i, 0))
    return pl.pallas_call(body, grid=(S // tm,), name="norm_fwd", in_specs=[row, pl.BlockSpec((1, D), lambda i: (0, 0))],
                          out_specs=row, out_shape=SDS((S, D), BF16), compiler_params=_cp(1))(x, w_norm)


def _proj_part(n1, w_slots, slots, blocks, proj, n_blocks, name):
    S, D = n1.shape
    NB = w_slots.shape[2]
    tm = min(1024, S)

    def body(slots_ref, blocks_ref, n_ref, w_ref, *rest):
        rest[-1][...] = jnp.dot(n_ref[...], w_ref[0], preferred_element_type=F32)

    out_spec = pl.BlockSpec((tm, NB), lambda i, j, slots, blocks: (i, blocks[j]))
    in_specs = [pl.BlockSpec((tm, D), lambda i, j, slots, blocks: (i, 0)),
                pl.BlockSpec((1, D, NB), lambda i, j, slots, blocks: (slots[j], 0, 0))]
    args = [n1, w_slots]
    if proj is not None:
        in_specs.append(ANY)
        args.append(proj)
    return pl.pallas_call(
        body, name=name, out_shape=SDS((S, n_blocks * NB), F32),
        grid_spec=pltpu.PrefetchScalarGridSpec(num_scalar_prefetch=2, grid=(S // tm, slots.shape[0]), in_specs=in_specs,
                                               out_specs=out_spec),
        input_output_aliases={} if proj is None else {4: 0},
        compiler_params=_cp(2))(slots, blocks, *args)


def _out_fwd(x, attn, ret, w_out, w_norm):
    S, D = x.shape
    DA = attn.shape[1]
    tm = min(256, S)

    def body(x_ref, a_ref, r_ref, w_ref, wn_ref, h_ref, mix_ref, n_ref):
        a = a_ref[...].astype(BF16)
        r = r_ref[...].astype(BF16)
        mix_ref[:, :DA] = a
        mix_ref[:, DA:] = r
        h = x_ref[...] + jnp.dot(a, w_ref[:DA, :], preferred_element_type=F32) \
            + jnp.dot(r, w_ref[DA:, :], preferred_element_type=F32)
        h_ref[...] = h
        n_ref[...] = (h * _rms_scale(h) * wn_ref[...]).astype(BF16)

    row = lambda w: pl.BlockSpec((tm, w), lambda i: (i, 0))
    return pl.pallas_call(
        body, grid=(S // tm,), name="out_fwd",
        in_specs=[row(D), row(DA), row(D - DA), pl.BlockSpec((D, D), lambda i: (0, 0)),
                  pl.BlockSpec((1, D), lambda i: (0, 0))],
        out_specs=[row(D), row(D), row(D)],
        out_shape=[SDS((S, D), F32), SDS((S, D), BF16), SDS((S, D), BF16)],
        compiler_params=_cp(1))(x, attn, ret, w_out, w_norm)


def _ffn_up(n2, wg, wu):
    S, D = n2.shape
    nblk, FB, _ = wg.shape
    tm = min(1024, S)

    def body(n_ref, wg_ref, wu_ref, g_ref, u_ref, a_ref):
        n = n_ref[...]
        g = _dot(n, wg_ref[0], _NT)
        u = _dot(n, wu_ref[0], _NT)
        g_ref[0] = g.astype(BF16)
        u_ref[0] = u.astype(BF16)
        a_ref[0] = (g * _sigmoid(g) * u).astype(BF16)

    wspec = pl.BlockSpec((1, FB, D), lambda j, i: (j, 0, 0))
    ospec = pl.BlockSpec((1, tm, FB), lambda j, i: (j, i, 0))
    return pl.pallas_call(
        body, grid=(nblk, S // tm), name="ffn_up",
        in_specs=[pl.BlockSpec((tm, D), lambda j, i: (i, 0)), wspec, wspec],
        out_specs=[ospec, ospec, ospec],
        out_shape=[SDS((nblk, S, FB), BF16)] * 3,
        compiler_params=_cp(2))(n2, wg, wu)


def _ffn_down_loss(act, wd, h1, target, w_norm):
    nblk, S, FB = act.shape
    D = h1.shape[1]
    tm = min(RESIDENT_ROWS, S)

    def body(a_ref, wd_ref, h_ref, t_ref, wn_ref, dh_ref, dhb_ref, loss_ref, dw_ref):
        @pl.when(pl.program_id(0) == 0)
        def _():
            dw_ref[...] = jnp.zeros_like(dw_ref)

        h = h_ref[...] + _blocked_matmul(a_ref, wd_ref)
        w = wn_ref[...]
        err = h * _rms_scale(h) * w - t_ref[...]
        loss_ref[...] = jnp.full(loss_ref.shape, 0.5 * jnp.sum(err * err) / D, F32)
        dh, dw = _rms_bwd(err * (1.0 / D), h, w)
        dh_ref[...] = dh
        dhb_ref[...] = dh.astype(BF16)
        dw_ref[...] += dw

    row = pl.BlockSpec((tm, D), lambda i: (i, 0))
    vec = pl.BlockSpec((1, D), lambda i: (0, 0))
    return pl.pallas_call(
        body, grid=(S // tm,), name="ffn_down_loss",
        in_specs=[pl.BlockSpec((nblk, tm, FB), lambda i: (0, i, 0)), _resident((nblk * FB, D)), row, row, vec],
        out_specs=[row, row, pl.BlockSpec((8, 128), lambda i: (i, 0)), vec],
        out_shape=[SDS((S, D), F32), SDS((S, D), BF16), SDS((S // tm * 8, 128), F32), SDS((1, D), F32)],
        compiler_params=_cp(1))(act, wd, h1, target, w_norm)


def _attn_bias():
    n_heads = 8
    slopes = np.exp2(-8.0 * np.arange(1, n_heads + 1, dtype=np.float32) / n_heads)
    dist = np.abs(np.arange(K_TILE)[None, :] - BAND - np.arange(Q_TILE)[:, None])
    out = np.empty((n_heads, len(DILATIONS), Q_TILE, K_TILE), np.float32)
    for h in range(n_heads):
        for p, d in enumerate(DILATIONS):
            out[h, p] = np.where(dist <= BAND, -slopes[h] * (d * dist).astype(np.float32), NEG)
    return jnp.asarray(out)


def _attn_tiles(S, d):
    L = S // d
    per_class = L // Q_TILE
    return L, per_class, d * per_class


def _tile_rows(t, d, per_class):
    r = t // per_class
    a = (t % per_class) * Q_TILE
    q_rows = pl.ds(r + d * a, Q_TILE, stride=d) if d > 1 else pl.ds(pl.multiple_of(a, Q_TILE), Q_TILE)
    k_rows = pl.ds(KV_PAD + r + d * (a - BAND), K_TILE, stride=d) if d > 1 else pl.ds(
        pl.multiple_of(KV_PAD + a - BAND, BAND), K_TILE)
    return a, q_rows, k_rows


def _to_quarters(dst, src, n, dst_off=0):
    for r in range(4):
        dst[pl.ds(dst_off + r * (n // 4), n // 4), :] = src[pl.ds(r, n // 4, stride=4), :]


def _quarter_tile_rows(t, S):
    L = S // 16
    per_class = L // Q_TILE
    blk, tt = t // (4 * per_class), t % (4 * per_class)
    r, a = tt // per_class, (tt % per_class) * Q_TILE
    q_rows = pl.ds(blk * (S // 4) + r + 4 * a, Q_TILE, stride=4)
    k_rows = pl.ds(KV_PAD + blk * (S // 4) + r + 4 * (a - BAND), K_TILE, stride=4)
    return a, q_rows, k_rows


def _lanes(x, width):
    return jnp.concatenate([x] * (width // HEAD_DIM), axis=-1)


_BNT = (((2,), (2,)), ((0,), (0,)))
_BNN = (((2,), (1,)), ((0,), (0,)))
_BTN = (((1,), (1,)), ((0,), (0,)))


def _bdot(a, b, dims):
    return lax.dot_general(a, b, dims, preferred_element_type=F32)


def _stacked(rows, loaders):
    return [jnp.stack([f(*r) for r in rows]) for f in loaders]


def _edge_mask(a, L):
    lk = lax.broadcasted_iota(jnp.int32, (1, K_TILE), 1) + (a - BAND)
    return jnp.where((lk >= 0) & (lk < L), 0.0, NEG).astype(F32)


def _fill_padded(dst, src, S):
    dst[pl.ds(0, KV_PAD), :] = jnp.zeros((KV_PAD, HEAD_DIM), F32)
    dst[pl.ds(KV_PAD + S, KV_PAD), :] = jnp.zeros((KV_PAD, HEAD_DIM), F32)
    dst[pl.ds(KV_PAD, S), :] = src[...]


def _head_specs(S, groups, n_heads):
    return [pl.BlockSpec((S, HEAD_DIM), functools.partial(lambda h, g: (0, g * n_heads + h), g=g)) for g in groups]


def _attn_fwd(proj, bias):
    S = proj.shape[0]
    H = proj.shape[1] // (N_GROUPS * HEAD_DIM)
    scale = HEAD_DIM ** -0.5

    def body(q_ref, k_ref, v_ref, b_ref, o_ref, lse_ref, kp, vp, m_run, l_run, q4, m3, l3, acc3):
        _fill_padded(kp, k_ref, S)
        _fill_padded(vp, v_ref, S)
        o_ref[...] = jnp.zeros_like(o_ref)
        m_run[...] = jnp.full(m_run.shape, NEG, F32)
        l_run[...] = jnp.zeros_like(l_run)
        for p, d in enumerate(DILATIONS[:2]):
            L, per_class, n_tiles = _attn_tiles(S, d)

            def tiles(t, carry, p=p, d=d, L=L, per_class=per_class, n_tiles=n_tiles):
                rows = [_tile_rows(t + u * (n_tiles // TILE_GROUP), d, per_class) for u in range(TILE_GROUP)]
                qs, ks, vs, m_old, l_old, o_old, edge = _stacked(rows, (
                    lambda a, qr, kr: q_ref[qr, :].astype(BF16), lambda a, qr, kr: kp[kr, :].astype(BF16),
                    lambda a, qr, kr: vp[kr, :].astype(BF16), lambda a, qr, kr: m_run[qr, :],
                    lambda a, qr, kr: l_run[qr, :], lambda a, qr, kr: o_ref[qr, :], lambda a, qr, kr: _edge_mask(a, L)))
                s = _bdot(qs, ks, _BNT) * scale + b_ref[0, p][None] + edge
                m_new = jnp.maximum(m_old, jnp.max(s, axis=-1, keepdims=True))
                pr = jnp.exp(s - _lanes(m_new, K_TILE)).astype(BF16)
                alpha = jnp.exp(m_old - m_new)
                l_new = alpha * l_old + _bdot(pr, jnp.ones((TILE_GROUP, K_TILE, HEAD_DIM), BF16), _BNN)
                o_new = alpha * o_old + _bdot(pr, vs, _BNN)
                for u, (_, qr, _) in enumerate(rows):
                    o_ref[qr, :] = o_new[u]
                    m_run[qr, :] = m_new[u]
                    l_run[qr, :] = l_new[u]
                return carry

            lax.fori_loop(0, n_tiles // TILE_GROUP, tiles, 0)

        _to_quarters(q4, q_ref, S)
        _to_quarters(kp, k_ref, S, KV_PAD)
        _to_quarters(vp, v_ref, S, KV_PAD)
        n_tiles = _attn_tiles(S, DILATIONS[2])[2]

        def tiles3(t, carry):
            rows = [_quarter_tile_rows(t + u * (n_tiles // TILE_GROUP), S) for u in range(TILE_GROUP)]
            qs, ks, vs, edge = _stacked(rows, (
                lambda a, qr, kr: q4[qr, :].astype(BF16), lambda a, qr, kr: kp[kr, :].astype(BF16),
                lambda a, qr, kr: vp[kr, :].astype(BF16), lambda a, qr, kr: _edge_mask(a, S // DILATIONS[2])))
            s = _bdot(qs, ks, _BNT) * scale + b_ref[0, 2][None] + edge
            m_new = jnp.broadcast_to(jnp.max(s, axis=-1, keepdims=True), (TILE_GROUP, Q_TILE, HEAD_DIM))
            pr = jnp.exp(s - _lanes(m_new, K_TILE)).astype(BF16)
            l_new = _bdot(pr, jnp.ones((TILE_GROUP, K_TILE, HEAD_DIM), BF16), _BNN)
            o_new = _bdot(pr, vs, _BNN)
            for u, (_, qr, _) in enumerate(rows):
                acc3[qr, :] = o_new[u]
                m3[qr, :] = m_new[u]
                l3[qr, :] = l_new[u]
            return carry

        lax.fori_loop(0, n_tiles // TILE_GROUP, tiles3, 0)
        for r in range(4):
            nat, qtr = pl.ds(r, S // 4, stride=4), pl.ds(r * (S // 4), S // 4)
            m_a, m_b = m_run[nat, :], m3[qtr, :]
            m = jnp.maximum(m_a, m_b)
            w_a, w_b = jnp.exp(m_a - m), jnp.exp(m_b - m)
            l = w_a * l_run[nat, :] + w_b * l3[qtr, :]
            o_ref[nat, :] = (w_a * o_ref[nat, :] + w_b * acc3[qtr, :]) / l
            lse_ref[nat, :] = m + jnp.log(l)

    hspec = pl.BlockSpec((S, HEAD_DIM), lambda h: (0, h))
    padded, plain = pltpu.VMEM((S + 2 * KV_PAD, HEAD_DIM), F32), pltpu.VMEM((S, HEAD_DIM), F32)
    return pl.pallas_call(
        body, grid=(H,), name="attn_fwd",
        in_specs=_head_specs(S, (0, 1, 2), H) + [
            pl.BlockSpec((1, len(DILATIONS), Q_TILE, K_TILE), lambda h: (h, 0, 0, 0))],
        out_specs=[hspec, hspec],
        out_shape=[SDS((S, H * HEAD_DIM), F32), SDS((S, H * HEAD_DIM), F32)],
        scratch_shapes=[padded, padded] + [plain] * 6,
        compiler_params=_cp(1))(proj, proj, proj, bias)


def _attn_bwd(proj, out, lse, dmix, bias):
    S = proj.shape[0]
    H = proj.shape[1] // (N_GROUPS * HEAD_DIM)
    scale = HEAD_DIM ** -0.5
    assert S // DILATIONS[2] >= 2 * Q_TILE

    def body(q_ref, k_ref, v_ref, o_ref, lse_ref, do_ref, b_ref, dq_ref, dk_ref, dv_ref,
             kp, vp, dkp, dvp, dsum, q4, do4, lse4, dsum4):
        _fill_padded(kp, k_ref, S)
        _fill_padded(vp, v_ref, S)
        dkp[...] = jnp.zeros_like(dkp)
        dvp[...] = jnp.zeros_like(dvp)
        dq_ref[...] = jnp.zeros_like(dq_ref)
        dsum[...] = jnp.broadcast_to(jnp.sum(do_ref[...] * o_ref[...], axis=-1, keepdims=True), dsum.shape)

        def run(n_tiles, tile_rows, p, L, q_src, do_src, lse_src, dsum_src, dq_dst, dq_adds):
            def tiles(t, carry):
                rows = [tile_rows(t + u * (n_tiles // TILE_GROUP)) for u in range(TILE_GROUP)]
                qs, ks, vs, dos, lses, dsums, dk_old, dv_old, edge = _stacked(rows, (
                    lambda a, qr, kr: q_src[qr, :].astype(BF16), lambda a, qr, kr: kp[kr, :].astype(BF16),
                    lambda a, qr, kr: vp[kr, :].astype(BF16), lambda a, qr, kr: do_src[qr, :].astype(BF16),
                    lambda a, qr, kr: lse_src[qr, :], lambda a, qr, kr: dsum_src[qr, :],
                    lambda a, qr, kr: dkp[kr, :], lambda a, qr, kr: dvp[kr, :], lambda a, qr, kr: _edge_mask(a, L)))
                s = _bdot(qs, ks, _BNT) * scale + b_ref[0, p][None] + edge
                pr = jnp.exp(s - _lanes(lses, K_TILE))
                ds = (pr * (_bdot(dos, vs, _BNT) - _lanes(dsums, K_TILE)) * scale).astype(BF16)
                dq_new = _bdot(ds, ks, _BNN)
                if dq_adds:
                    dq_new = dq_new + jnp.stack([dq_dst[qr, :] for _, qr, _ in rows])
                dk_new = dk_old + _bdot(ds, qs, _BTN)
                dv_new = dv_old + _bdot(pr.astype(BF16), dos, _BTN)
                for u, (_, qr, kr) in enumerate(rows):
                    dq_dst[qr, :] = dq_new[u]
                    dkp[kr, :] = dk_new[u]
                    dvp[kr, :] = dv_new[u]
                return carry

            lax.fori_loop(0, n_tiles // TILE_GROUP, tiles, 0)

        for p, d in enumerate(DILATIONS[:2]):
            L, per_class, n_tiles = _attn_tiles(S, d)
            run(n_tiles, functools.partial(_tile_rows, d=d, per_class=per_class), p, L,
                q_ref, do_ref, lse_ref, dsum, dq_ref, True)
        dk_ref[...] = dkp[pl.ds(KV_PAD, S), :]
        dv_ref[...] = dvp[pl.ds(KV_PAD, S), :]

        for dst, src in ((q4, q_ref), (do4, do_ref), (lse4, lse_ref), (dsum4, dsum)):
            _to_quarters(dst, src, S)
        _to_quarters(kp, k_ref, S, KV_PAD)
        _to_quarters(vp, v_ref, S, KV_PAD)
        dkp[...] = jnp.zeros_like(dkp)
        dvp[...] = jnp.zeros_like(dvp)
        dq3 = dsum
        run(_attn_tiles(S, DILATIONS[2])[2], functools.partial(_quarter_tile_rows, S=S), 2, S // DILATIONS[2],
            q4, do4, lse4, dsum4, dq3, False)
        for r in range(4):
            nat, qtr = pl.ds(r, S // 4, stride=4), pl.ds(r * (S // 4), S // 4)
            pad_qtr = pl.ds(KV_PAD + r * (S // 4), S // 4)
            dq_ref[nat, :] = dq_ref[nat, :] + dq3[qtr, :]
            dk_ref[nat, :] = dk_ref[nat, :] + dkp[pad_qtr, :]
            dv_ref[nat, :] = dv_ref[nat, :] + dvp[pad_qtr, :]

    hspec = pl.BlockSpec((S, HEAD_DIM), lambda h: (0, h))
    once =pl.BlockSpec((S, HEAD_DIM), lambda h: (0, h), pipeline_mode=pl.Buffered(1))
    padded, plain = pltpu.VMEM((S + 2 * KV_PAD, HEAD_DIM), F32), pltpu.VMEM((S, HEAD_DIM), F32)
    return pl.pallas_call(
        body, grid=(H,), name="attn_bwd",
        in_specs=_head_specs(S, (0, 1, 2), H) + [once, once, once,
                                                  pl.BlockSpec((1, len(DILATIONS), Q_TILE, K_TILE), lambda h: (h, 0, 0, 0))],
        out_specs=[hspec, hspec, hspec],
        out_shape=[SDS((S, H * HEAD_DIM), F32)] * 3,
        scratch_shapes=[padded] * 4 + [plain] * 5,
        compiler_params=_cp(1))(proj, proj, proj, out, lse, dmix, bias)


def _ret_consts(lg, forward):
    C = RET_CHUNK
    i = lax.broadcasted_iota(jnp.int32, (C, C), 0)
    j = lax.broadcasted_iota(jnp.int32, (C, C), 1)
    rel = (i - j) if forward else (j - i)
    inside = (rel >= 0) if forward else (rel > 0)
    relf = jnp.maximum(rel, 0).astype(F32)
    mask = jnp.where(inside, jnp.exp(lg * relf), 0.0)
    idx = lax.broadcasted_iota(jnp.int32, (C, 1), 0).astype(F32)
    q_exp = (idx + 1.0) if forward else (C - idx)
    k_exp = (C - 1.0 - idx) if forward else idx
    return mask, relf, jnp.exp(lg * q_exp), q_exp, jnp.exp(lg * k_exp), k_exp, jnp.exp(lg * C)


def _log_decay(dec_ref, h):
    return -jnp.exp(jnp.full((1, 1), dec_ref[0, h], F32))


FFN_BLOCK = 704
CHUNK_BATCH = 8


def _batch_rows(b):
    n = CHUNK_BATCH * RET_CHUNK
    return pl.ds(pl.multiple_of(b * n, n), n)


def _batch_chunks(b):
    return pl.ds(pl.multiple_of(b * CHUNK_BATCH, CHUNK_BATCH), CHUNK_BATCH)


def _chunks3(x):
    return x.reshape(CHUNK_BATCH, RET_CHUNK, HEAD_DIM)


def _ret_scan(buf, c_decs, nc, reverse):
    def step(n, carry):
        new = []
        for way, r in enumerate(carry):
            c = n if (way == 0) != reverse else nc - 1 - n
            term = buf[way, c]
            buf[way, c] = r
            new.append(r * c_decs[way] + term)
        return tuple(new)

    lax.fori_loop(0, nc, step, (jnp.zeros((HEAD_DIM, HEAD_DIM), F32),) * 2)


def _ret_fwd(proj, dec_f, dec_b, w_norm):
    S = proj.shape[0]
    H = proj.shape[1] // (N_GROUPS * HEAD_DIM)
    nc = S // RET_CHUNK
    scale = HEAD_DIM ** -0.5

    def body(df_ref, db_ref, q_ref, k_ref, v_ref, g_ref, w_ref, y_ref, o_ref, states):
        h = pl.program_id(0)
        consts = [_ret_consts(_log_decay(dref, h), fw) for fw, dref in ((True, df_ref), (False, db_ref))]

        def kv_step(b, carry):
            rows, batch = _batch_rows(b), _batch_chunks(b)
            k3 = _chunks3(k_ref[rows, :])
            v3 = _chunks3(v_ref[rows, :]).astype(BF16)
            for way in range(2):
                states[way, batch] = _bdot((k3 * consts[way][4]).astype(BF16), v3, _BTN)
            return carry

        lax.fori_loop(0, nc // CHUNK_BATCH, kv_step, 0)
        _ret_scan(states, [c[6] for c in consts], nc, False)

        def out_step(b, carry):
            rows, batch = _batch_rows(b), _batch_chunks(b)
            q3 = _chunks3(q_ref[rows, :] * scale)
            k3 = _chunks3(k_ref[rows, :]).astype(BF16)
            v3 = _chunks3(v_ref[rows, :]).astype(BF16)
            a0 = _bdot(q3.astype(BF16), k3, _BNT)
            o = None
            for way in range(2):
                mask, q_dec = consts[way][0], consts[way][2]
                part = _bdot((a0 * mask).astype(BF16), v3, _BNN) \
                    + _bdot((q3 * q_dec).astype(BF16), states[way, batch].astype(BF16), _BNN)
                o = part if o is None else o + part
            o_ref[rows, :] = o.reshape(CHUNK_BATCH * RET_CHUNK, HEAD_DIM)
            return carry

        lax.fori_loop(0, nc // CHUNK_BATCH, out_step, 0)
        o = o_ref[...]
        g = g_ref[...]
        y_ref[...] = o * _rms_scale(o) * w_ref[...] * (g * _sigmoid(g))

    hspec = pl.BlockSpec((S, HEAD_DIM), lambda h: (0, h))
    smem = pl.BlockSpec(memory_space=pltpu.SMEM)
    return pl.pallas_call(
        body, grid=(H,), name="ret_fwd",
        in_specs=[smem, smem] + _head_specs(S, (3, 4, 5, 6), H) + [pl.BlockSpec((1, HEAD_DIM), lambda h: (0, h))],
        out_specs=[hspec, hspec],
        out_shape=[SDS((S, H * HEAD_DIM), F32)] * 2,
        scratch_shapes=[pltpu.VMEM((2, nc, HEAD_DIM, HEAD_DIM), F32)],
        compiler_params=_cp(1))(dec_f, dec_b, proj, proj, proj, proj, w_norm)


def _ret_gate_bwd(proj, o_raw, dmix, w_norm, col0):
    S = proj.shape[0]
    H = proj.shape[1] // (N_GROUPS * HEAD_DIM)

    def body(g_ref, o_ref, dy_ref, w_ref, do_ref, dg_ref, dw_ref):
        o = o_ref[...]
        g = g_ref[...]
        dy = dy_ref[...]
        w = w_ref[...]
        rr = _rms_scale(o)
        normed = o * rr
        sg = _sigmoid(g)
        silu = g * sg
        dw_ref[0] = jnp.broadcast_to(jnp.sum(dy * normed * silu, axis=0, keepdims=True), (8, HEAD_DIM))
        dg_ref[...] = (dy * normed * w * (sg * (1.0 + g * (1.0 - sg)))).astype(BF16)
        dnormed = dy * w * silu
        do_ref[...] = rr * dnormed - o * (rr * rr * rr) * jnp.mean(dnormed * o, axis=-1, keepdims=True)

    hspec = pl.BlockSpec((S, HEAD_DIM), lambda h: (0, h))
    nh0 = col0 // HEAD_DIM
    return pl.pallas_call(
        body, grid=(H,), name="ret_gate_bwd",
        in_specs=_head_specs(S, (6,), H) + [hspec, pl.BlockSpec((S, HEAD_DIM), lambda h: (0, nh0 + h)),
                                            pl.BlockSpec((1, HEAD_DIM), lambda h: (0, h))],
        out_specs=[hspec, hspec, pl.BlockSpec((1, 8, HEAD_DIM), lambda h: (h, 0, 0))],
        out_shape=[SDS((S, H * HEAD_DIM), F32), SDS((S, H * HEAD_DIM), BF16), SDS((H, 8, HEAD_DIM), F32)],
        compiler_params=_cp(1))(proj, o_raw, dmix, w_norm)


def _ret_bwd(proj, d_out, dec_f, dec_b):
    S = proj.shape[0]
    H = proj.shape[1] // (N_GROUPS * HEAD_DIM)
    C = RET_CHUNK
    nc = S // C
    scale = HEAD_DIM ** -0.5

    def body(df_ref, db_ref, q_ref, k_ref, v_ref, do, dq_ref, dk_ref, dv_ref, small_ref, states, d_states):
        h = pl.program_id(0)
        lgs = [_log_decay(df_ref, h), _log_decay(db_ref, h)]
        consts = [_ret_consts(lg, fw) for lg, fw in zip(lgs, (True, False))]

        def prep_step(b, carry):
            rows, batch = _batch_rows(b), _batch_chunks(b)
            q3 = _chunks3(q_ref[rows, :] * scale)
            k3 = _chunks3(k_ref[rows, :])
            v3 = _chunks3(v_ref[rows, :]).astype(BF16)
            do3 = _chunks3(do[rows, :]).astype(BF16)
            for way in range(2):
                states[way, batch] = _bdot((k3 * consts[way][4]).astype(BF16), v3, _BTN)
                d_states[way, batch] = _bdot((q3 * consts[way][2]).astype(BF16), do3, _BTN)
            return carry

        lax.fori_loop(0, nc // CHUNK_BATCH, prep_step, 0)
        c_decs = [c[6] for c in consts]
        _ret_scan(states, c_decs, nc, False)
        _ret_scan(d_states, c_decs, nc, True)

        def main_step(b, dlams):
            rows, batch = _batch_rows(b), _batch_chunks(b)
            q3 = _chunks3(q_ref[rows, :] * scale)
            k3 = _chunks3(k_ref[rows, :])
            q3b, k3b = q3.astype(BF16), k3.astype(BF16)
            v3b = _chunks3(v_ref[rows, :]).astype(BF16)
            do3b = _chunks3(do[rows, :]).astype(BF16)
            a0 = _bdot(q3b, k3b, _BNT)
            pv = _bdot(do3b, v3b, _BNT)
            dq = dk = dv = None
            new_dlams = []
            for way in range(2):
                mask, relf, q_dec, q_exp, k_dec, k_exp, c_dec = consts[way]
                state, d_state = states[way, batch], d_states[way, batch]
                dp = pv * mask
                dpb = dp.astype(BF16)
                gq = _bdot(do3b, state.astype(BF16), _BNT)
                gk = _bdot(v3b, d_state.astype(BF16), _BNT)
                parts = (_bdot(dpb, k3b, _BNN) + q_dec * gq, _bdot(dpb, q3b, _BTN) + k_dec * gk,
                         _bdot((a0 * mask).astype(BF16), do3b, _BTN)
                         + _bdot((k3 * k_dec).astype(BF16), d_state.astype(BF16), _BNN))
                dq, dk, dv = parts if dq is None else (dq + parts[0], dk + parts[1], dv + parts[2])
                total = lambda x: jnp.sum(jnp.sum(x, axis=0), axis=0, keepdims=True)
                new_dlams.append(dlams[way] + total(relf * a0 * dp)
                                 + total(q_exp * q_dec * q3 * gq + k_exp * k_dec * k3 * gk)
                                 + (C * c_dec) * total(state * d_state))
            flat = lambda x: x.reshape(CHUNK_BATCH * C, HEAD_DIM)
            dq_ref[rows, :] = (flat(dq) * scale).astype(BF16)
            dk_ref[rows, :] = flat(dk).astype(BF16)
            dv_ref[rows, :] = flat(dv).astype(BF16)
            return tuple(new_dlams)

        dlams = lax.fori_loop(0, nc // CHUNK_BATCH, main_step, (jnp.zeros((1, HEAD_DIM), F32),) * 2)
        for row, (dlam, lg) in enumerate(zip(dlams, lgs)):
            small_ref[0, pl.ds(row, 1), :] = jnp.broadcast_to(jnp.sum(dlam, axis=-1, keepdims=True) * lg, (1, HEAD_DIM))
        small_ref[0, pl.ds(2, 6), :] = jnp.zeros((6, HEAD_DIM), F32)

    hspec = pl.BlockSpec((S, HEAD_DIM), lambda h: (0, h))
    smem = pl.BlockSpec(memory_space=pltpu.SMEM)
    return pl.pallas_call(
        body, grid=(H,), name="ret_bwd",
        in_specs=[smem, smem] + _head_specs(S, (3, 4, 5), H) + [hspec],
        out_specs=[hspec, hspec, hspec, pl.BlockSpec((1, 8, HEAD_DIM), lambda h: (h, 0, 0))],
        out_shape=[SDS((S, H * HEAD_DIM), BF16)] * 3 + [SDS((H, 8, HEAD_DIM), F32)],
        scratch_shapes=[pltpu.VMEM((2, nc, HEAD_DIM, HEAD_DIM), F32), pltpu.VMEM((2, nc, HEAD_DIM, HEAD_DIM), F32)],
        compiler_params=_cp(1))(dec_f, dec_b, proj, proj, proj, d_out)


def _ffn_bwd_act(dh2, wd, g, u):
    S, D = dh2.shape
    nblk, _, FB = g.shape
    tm = min(1024, S)

    def body(dh_ref, wd_ref, g_ref, u_ref, dg_ref, du_ref):
        dact = _dot(dh_ref[...], wd_ref[...], _NT)
        gg = g_ref[0].astype(F32)
        sg = _sigmoid(gg)
        dg_ref[0] = (dact * u_ref[0].astype(F32) * (sg * (1.0 + gg * (1.0 - sg)))).astype(BF16)
        du_ref[0] = (dact * (gg * sg)).astype(BF16)

    blk = pl.BlockSpec((1, tm, FB), lambda j, i: (j, i, 0))
    return pl.pallas_call(
        body, grid=(nblk, S // tm), name="ffn_bwd_act",
        in_specs=[pl.BlockSpec((tm, D), lambda j, i: (i, 0)), pl.BlockSpec((FB, D), lambda j, i: (j, 0)), blk, blk],
        out_specs=[blk, blk], out_shape=[SDS((nblk, S, FB), BF16)] * 2,
        compiler_params=_cp(2))(dh2, wd, g, u)


def _ffn_bwd_in(dg, du, wg, wu, h1, dh2, w_norm):
    nblk, S, FB = dg.shape
    D = h1.shape[1]
    tm = min(RESIDENT_ROWS, S)
    blk = pl.BlockSpec((nblk, tm, FB), lambda i: (0, i, 0))
    row = pl.BlockSpec((tm, D), lambda i: (i, 0))
    vec = pl.BlockSpec((1, D), lambda i: (0, 0))

    def gate_body(dg_ref, wg_ref, part_ref):
        part_ref[...] = _blocked_matmul(dg_ref, wg_ref)

    part = pl.pallas_call(
        gate_body, grid=(S // tm,), name="ffn_bwd_in_gate", in_specs=[blk, _resident((nblk * FB, D))],
        out_specs=row, out_shape=SDS((S, D), F32), compiler_params=_cp(1))(dg, wg.reshape(nblk * FB, D))

    def body(du_ref, wu_ref, part_ref, h_ref, dh2_ref, wn_ref, dh_ref, dhb_ref, dw_ref):
        @pl.when(pl.program_id(0) == 0)
        def _():
            dw_ref[...] = jnp.zeros_like(dw_ref)

        dh, dw = _rms_bwd(part_ref[...] + _blocked_matmul(du_ref, wu_ref), h_ref[...], wn_ref[...])
        dh = dh2_ref[...] + dh
        dh_ref[...] = dh
        dhb_ref[...] = dh.astype(BF16)
        dw_ref[...] += dw

    return pl.pallas_call(
        body, grid=(S // tm,), name="ffn_bwd_in",
        in_specs=[blk, _resident((nblk * FB, D)), row, row, row, vec],
        out_specs=[row, row, vec], out_shape=[SDS((S, D), F32), SDS((S, D), BF16), SDS((1, D), F32)],
        compiler_params=_cp(1))(du, wu.reshape(nblk * FB, D), part, h1, dh2, w_norm)


def _dmix(dh1, w_out):
    S, D = dh1.shape
    tm = min(512, S)

    def body(dh_ref, w_ref, o_ref):
        o_ref[...] = _dot(dh_ref[...], w_ref[...], _NT)

    row = pl.BlockSpec((tm, D), lambda i: (i, 0))
    return pl.pallas_call(
        body, grid=(S // tm,), name="dmix", in_specs=[row, pl.BlockSpec((D, D), lambda i: (0, 0))],
        out_specs=row, out_shape=SDS((S, D), F32), compiler_params=_cp(1))(dh1, w_out)


def _in_bwd(dproj, w_blk, x, dh1, w_norm):
    S, D = x.shape
    nblk, _, NB = w_blk.shape
    tm = min(RESIDENT_ROWS, S)

    def body(dp_ref, w_ref, x_ref, dh1_ref, wn_ref, dx_ref, dw_ref):
        @pl.when(pl.program_id(0) == 0)
        def _():
            dw_ref[...] = jnp.zeros_like(dw_ref)

        dn = None
        for j in range(nblk):
            part = _dot(dp_ref[:, pl.ds(j * NB, NB)], w_ref[j], _NT)
            dn = part if dn is None else dn + part
        dh, dw = _rms_bwd(dn, x_ref[...], wn_ref[...])
        dx_ref[...] = dh1_ref[...] + dh
        dw_ref[...] += dw

    row = pl.BlockSpec((tm, D), lambda i: (i, 0))
    vec = pl.BlockSpec((1, D), lambda i: (0, 0))
    return pl.pallas_call(
        body, grid=(S // tm,), name="in_bwd",
        in_specs=[pl.BlockSpec((tm, nblk * NB), lambda i: (i, 0)),
                  pl.BlockSpec((nblk, D, NB), lambda i: (0, 0, 0), pipeline_mode=pl.Buffered(1)), row, row, vec],
        out_specs=[row, vec], out_shape=[SDS((S, D), F32), SDS((1, D), F32)],
        compiler_params=_cp(1))(dproj, w_blk, x, dh1, w_norm)


def _wgrad(a, b, a_spec, b_spec, o_spec, o_shape, grid, name):
    nk = grid[-1]

    def ld(ref):
        return ref[0] if len(ref.shape) == 3 else ref[...]

    def body(a_ref, b_ref, o_ref, acc):
        k = pl.program_id(len(grid) - 1)

        @pl.when(k == 0)
        def _():
            acc[...] = jnp.zeros_like(acc)

        acc[...] += _dot(ld(a_ref), ld(b_ref), _TN)

        @pl.when(k == nk - 1)
        def _():
            if len(o_ref.shape) == 3:
                o_ref[0] = acc[...].astype(o_ref.dtype)
            else:
                o_ref[...] = acc[...].astype(o_ref.dtype)

    return pl.pallas_call(
        body, grid=grid, name=name, in_specs=[a_spec, b_spec], out_specs=o_spec, out_shape=SDS(o_shape, BF16),
        scratch_shapes=[pltpu.VMEM(o_spec.block_shape[-2:], F32)], compiler_params=_cp(len(grid)))(a, b)


def _peer(k):
    x, y, c = lax.axis_index("x"), lax.axis_index("y"), lax.axis_index("c")
    px = 1 - x if k & 4 else x
    py = 1 - y if k & 2 else y
    pc = 1 - c if k & 1 else c
    return (px, py, pc), 4 * px + 2 * py + pc


def _exchange_copies(srcs, lands, send_sems, recv_sems, which, gather):
    _, me = _peer(0)
    pairs = []
    for pos, a in enumerate(which):
        for k in range(1, N_DEV):
            dev, idx = _peer(k)
            sem = pos * (N_DEV - 1) + k - 1
            src = srcs[a] if gather else srcs[a].at[idx]
            mk = functools.partial(pltpu.make_async_remote_copy, src_ref=src, send_sem=send_sems.at[sem],
                                   recv_sem=recv_sems.at[sem], device_id=dev, device_id_type=MESH)
            pairs.append((mk(dst_ref=lands[a].at[me]), mk(dst_ref=lands[a].at[idx])))
    return pairs


def _sequencer_kernel(name, collective_id, n_remote, n_local):
    return pl.kernel(mesh=plsc.ScalarSubcoreMesh(axis_name="sequencer", num_cores=1), name=name,
                     scratch_types=(pltpu.SemaphoreType.DMA((n_remote,)), pltpu.SemaphoreType.DMA((n_remote,)),
                                    pltpu.SemaphoreType.DMA((n_local,))),
                     compiler_params=pltpu.CompilerParams(collective_id=collective_id))


def _handshake(ks):
    barrier = pltpu.get_barrier_semaphore()
    for k in ks:
        pl.semaphore_signal(barrier, inc=1, device_id=_peer(k)[0], device_id_type=MESH)
    pl.semaphore_wait(barrier, len(ks))


def _sequencer_scatter(arrays, name, collective_id):
    n = len(arrays)
    hbm = pltpu.MemorySpace.HBM
    srcs = [jax.new_ref(a, memory_space=hbm) for a in arrays]
    lands = [jax.empty_ref(SDS(a.shape, a.dtype), memory_space=hbm) for a in arrays]

    @_sequencer_kernel(name, collective_id, n * (N_DEV - 1), n)
    def launch(send_sems, recv_sems, local_sems):
        _handshake(range(1, N_DEV))
        _, me = _peer(0)
        local = [pltpu.make_async_copy(srcs[a].at[me], lands[a].at[me], local_sems.at[a]) for a in range(n)]
        pairs = _exchange_copies(srcs, lands, send_sems, recv_sems, range(n), False)
        for out, _ in pairs:
            out.start()
        for cp in local:
            cp.start()
        for out, arrival in pairs:
            out.wait_send()
            arrival.wait_recv()
        for cp in local:
            cp.wait()

    launch()
    return [r[...] for r in lands]


SIBLING = 1
OTHER_CHIPS = (2, 4, 6)


def _sequencer_gather(arrays, name, collective_id):
    n = len(arrays)
    hbm = pltpu.MemorySpace.HBM
    srcs = [jax.new_ref(a, memory_space=hbm) for a in arrays]
    lands = [jax.empty_ref(SDS((N_DEV,) + a.shape, a.dtype), memory_space=hbm) for a in arrays]

    @_sequencer_kernel(name, collective_id, n * (N_DEV - 1), n)
    def launch(send_sems, recv_sems, local_sems):
        _handshake((SIBLING,) + OTHER_CHIPS)
        _, me = _peer(0)
        sibling, _ = _peer(SIBLING)

        def copy(a, k, src, block, to):
            sem = a * (N_DEV - 1) + k - 1
            return pltpu.make_async_remote_copy(src_ref=src, dst_ref=lands[a].at[block], send_sem=send_sems.at[sem],
                                                recv_sem=recv_sems.at[sem], device_id=to, device_id_type=MESH)

        local = [pltpu.make_async_copy(srcs[a], lands[a].at[me], local_sems.at[a]) for a in range(n)]
        first = [copy(a, k, srcs[a], me, _peer(k)[0]) for a in range(n) for k in OTHER_CHIPS + (SIBLING,)]
        for cp in first + local:
            cp.start()
        passed = []
        for a in range(n):
            for k in OTHER_CHIPS:
                _, block = _peer(k)
                copy(a, k, srcs[a], block, sibling).wait_recv()
                passed.append(copy(a, k ^ SIBLING, lands[a].at[block], block, sibling))
                passed[-1].start()
        for a in range(n):
            for k in (SIBLING,) + tuple(k ^ SIBLING for k in OTHER_CHIPS):
                copy(a, k, srcs[a], _peer(k)[1], sibling).wait_recv()
        for cp in first + passed:
            cp.wait_send()
        for cp in local:
            cp.wait()

    launch()
    return [r[...] for r in lands]


def _sequencer_gather_chips(array, name, collective_id, chips):
    hbm = pltpu.MemorySpace.HBM
    src = jax.new_ref(array, memory_space=hbm)
    land = jax.empty_ref(SDS((2 * len(chips),) + array.shape, array.dtype), memory_space=hbm)

    @_sequencer_kernel(name, collective_id, 2 * len(chips), 1)
    def launch(send_sems, recv_sems, local_sems):
        _handshake((SIBLING,) + tuple(k for k in chips if k))
        c = lax.axis_index("c")
        sibling, _ = _peer(SIBLING)

        def copy(sem, src_ref, slot, to):
            return pltpu.make_async_remote_copy(src_ref=src_ref, dst_ref=land.at[slot], send_sem=send_sems.at[sem],
                                                recv_sem=recv_sems.at[sem], device_id=to, device_id_type=MESH)

        started = []
        for pos, k in enumerate(chips):
            started.append(copy(2 * pos, src, 2 * pos + c, _peer(k)[0] if k else sibling))
            started[-1].start()
        for pos, k in enumerate(chips):
            if k:
                copy(2 * pos, src, 2 * pos + c, sibling).wait_recv()
                started.append(copy(2 * pos + 1, land.at[2 * pos + c], 2 * pos + c, sibling))
                started[-1].start()
        for pos, k in enumerate(chips):
            copy(2 * pos + 1 if k else 2 * pos, src, 2 * pos + 1 - c, sibling).wait_recv()
        for cp in started:
            cp.wait_send()

    launch()
    return land[...]


SMALL_ROWS = 64


def _small_step(part, w, m, v):
    def body(p_ref, w_ref, m_ref, v_ref, g_ref, d_ref, nm_ref, nv_ref, gath, send_sems, recv_sems):
        _, me = _peer(0)
        gath[me] = p_ref[...]
        copies = []
        for k in range(1, N_DEV):
            dev, idx = _peer(k)
            out = pltpu.make_async_remote_copy(src_ref=p_ref, dst_ref=gath.at[me], send_sem=send_sems.at[k - 1],
                                               recv_sem=recv_sems.at[k - 1], device_id=dev, device_id_type=MESH)
            out.start()
            arrival = pltpu.make_async_remote_copy(src_ref=p_ref, dst_ref=gath.at[idx], send_sem=send_sems.at[k - 1],
                                                   recv_sem=recv_sems.at[k - 1], device_id=dev, device_id_type=MESH)
            copies.append((out, arrival))
        for out, arrival in copies:
            out.wait_send()
            arrival.wait_recv()
        g = gath[0]
        for p in range(1, N_DEV):
            g = g + gath[p]
        g_ref[...] = g
        d_ref[...], nm_ref[...], nv_ref[...] = _adamw(w_ref[...], g, m_ref[...], v_ref[...])

    vm = pl.BlockSpec(memory_space=pltpu.VMEM)
    return pl.pallas_call(
        body, name="small_step", in_specs=[vm] * 4, out_specs=[vm] * 4,
        out_shape=[SDS((SMALL_ROWS, 128), F32)] * 4,
        scratch_shapes=[pltpu.VMEM((N_DEV, SMALL_ROWS, 128), F32), pltpu.SemaphoreType.DMA((N_DEV - 1,)),
                        pltpu.SemaphoreType.DMA((N_DEV - 1,))])(part, w, m, v)


def _adamw(w, g, m, v):
    m = ADAM_B1 * m + (1.0 - ADAM_B1) * g
    v = ADAM_B2 * v + (1.0 - ADAM_B2) * (g * g)
    m_hat = m / (1.0 - ADAM_B1 ** ADAM_STEP)
    v_hat = v / (1.0 - ADAM_B2 ** ADAM_STEP)
    delta = -ADAM_LR * (m_hat / (jnp.sqrt(v_hat) + ADAM_EPS) + ADAM_WD * w)
    return delta, m, v


def _adamw_block(parts, w, m, v, name):
    R, C = w.shape
    n_parts = len(parts)
    Rp = R // n_parts
    tr = next(t for t in (256, 128, 64, 32, 16, 8) if Rp % t == 0 and t * C <= 256 * 1024)
    per_part = Rp // tr

    def body(*refs):
        p_refs = refs[:n_parts]
        w_ref, m_ref, v_ref, g_ref, d_ref, nm_ref, nv_ref = refs[n_parts:]
        for k, p_ref in enumerate(p_refs):
            @pl.when(pl.program_id(0) // per_part == k)
            def _(p_ref=p_ref):
                g = p_ref[0].astype(F32)
                for p in range(1, N_DEV):
                    g = g + p_ref[p].astype(F32)
                g_ref[...] = g
                d_ref[...], nm_ref[...], nv_ref[...] = _adamw(w_ref[...], g, m_ref[...], v_ref[...])

    row = pl.BlockSpec((tr, C), lambda i: (i, 0))
    part_specs = [pl.BlockSpec((N_DEV, tr, C), functools.partial(
        lambda i, k: (0, jnp.clip(i - k * per_part, 0, per_part - 1), 0), k=k)) for k in range(n_parts)]
    return pl.pallas_call(
        body, grid=(R // tr,), name=name, in_specs=part_specs + [row, row, row],
        out_specs=[row] * 4, out_shape=[SDS((R, C), F32)] * 4, compiler_params=_cp(1))(*parts, w, m, v)


def _pack_small(mix, ffn, fin, retw, dec_f, dec_b, loss):
    flat = jnp.concatenate([mix.reshape(-1), ffn.reshape(-1), fin.reshape(-1), retw.reshape(-1), dec_f.reshape(-1),
                            dec_b.reshape(-1), loss.reshape(-1)])
    return jnp.pad(flat, (0, SMALL_ROWS * 128 - flat.shape[0])).reshape(SMALL_ROWS, 128)


def _unpack_small(packed, shapes):
    flat = packed.reshape(-1)
    out, at = [], 0
    for s in shapes:
        n = math.prod(s)
        out.append(flat[at:at + n].reshape(s))
        at += n
    return out


def kernel(x, norm_mix_w, w_in, ret_decay_fwd, ret_decay_bwd, ret_norm_w, w_out, norm_ffn_w, w_gate, w_up, w_down, norm_final_w, loss_target, m_norm_mix_w, m_w_in, m_ret_decay_fwd, m_ret_decay_bwd, m_ret_norm_w, m_w_out, m_norm_ffn_w, m_w_gate, m_w_up, m_w_down, m_norm_final_w, v_norm_mix_w, v_w_in, v_ret_decay_fwd, v_ret_decay_bwd, v_ret_norm_w, v_w_out, v_norm_ffn_w, v_w_gate, v_w_up, v_w_down, v_norm_final_w):
    x2 = x[0]
    tgt = loss_target[0]
    S, D = x2.shape
    H = ret_norm_w.shape[1] // HEAD_DIM
    DA = H * HEAD_DIM
    fin_w = norm_final_w.reshape(1, D)
    big = (w_in[0], w_out[0], w_gate[0].T, w_up[0].T, w_down[0])

    big_b = [w.astype(BF16) for w in big]
    stages = ((0,), (4, 2), (6,))
    wi_stages = [_sequencer_gather_chips(big_b[0], name, cid, ks)
                 for name, cid, ks in zip(("gather_in_own", "gather_in_near", "gather_in_far"), (0, 7, 8), stages)]
    wo, = _sequencer_gather(big_b[1:2], "gather_out", 1)
    wg, wu = _sequencer_gather(big_b[2:4], "gather_gate_up", 9)
    wd, = _sequencer_gather(big_b[4:], "gather_down", 5)
    wi, = _sequencer_gather(big_b[:1], "gather_in_ordered", 10)
    NB = big_b[0].shape[1]
    ax, ay = lax.axis_index("x"), lax.axis_index("y")
    chip_of = {k: 2 * (1 - ax if k & 4 else ax) + (1 - ay if k & 2 else ay) for k in (0, 2, 4, 6)}

    n1 = _norm_fwd(x2, norm_mix_w)
    ac = lax.axis_index("c")
    me = 2 * chip_of[0] + ac
    vec = lambda *v: jnp.stack([jnp.asarray(t, jnp.int32) for t in v])
    proj = _proj_part(n1, big_b[0][None], vec(0), vec(me), None, N_DEV, "proj_self")
    for ks, w_st, name in zip(stages, wi_stages, ("proj_sibling", "proj_near", "proj_far")):
        slots, blocks = [], []
        for pos, k in enumerate(ks):
            for core in ((1 - ac,) if k == 0 else (0, 1)):
                slots.append(2 * pos + core)
                blocks.append(2 * chip_of[k] + core)
        proj = _proj_part(n1, w_st, vec(*slots), vec(*blocks), proj, N_DEV, name)
    bias = _attn_bias()[:H]
    attn, lse = _attn_fwd(proj, bias)
    ret, o_raw = _ret_fwd(proj, ret_decay_fwd, ret_decay_bwd, ret_norm_w)
    wo_full = wo.reshape(D, D)
    d_ff = N_DEV * wd.shape[1]
    FB = FFN_BLOCK if d_ff % FFN_BLOCK == 0 else wd.shape[1]
    n_fb = d_ff // FB
    wg, wu = wg.reshape(n_fb, FB, D), wu.reshape(n_fb, FB, D)
    wd_full = wd.reshape(d_ff, D)
    h1, mixed, n2 = _out_fwd(x2, attn, ret, wo_full, norm_ffn_w)
    gate, up, act = _ffn_up(n2, wg, wu)
    dh2, dh2_b, loss_parts, g_fin = _ffn_down_loss(act, wd_full, h1, tgt, fin_w)

    dgate, dup = _ffn_bwd_act(dh2_b, wd_full, gate, up)
    tn = min(1024, D)
    ffn_specs = (pl.BlockSpec((1, S, FB), lambda j, n, k: (j, 0, 0)), pl.BlockSpec((S, tn), lambda j, n, k: (0, n)),
                 pl.BlockSpec((1, FB, tn), lambda j, n, k: (j, 0, n)), (n_fb, FB, D), (n_fb, D // tn, 1))
    per_dev = (N_DEV, d_ff // N_DEV, D)
    g_wd = _wgrad(act, dh2_b, *ffn_specs, "wgrad_down").reshape(per_dev)
    g_wg = _wgrad(dgate, n2, *ffn_specs, "wgrad_gate").reshape(per_dev)
    g_wu = _wgrad(dup, n2, *ffn_specs, "wgrad_up").reshape(per_dev)
    parts_f = _sequencer_scatter([g_wg, g_wu, g_wd], "scatter_ffn", 2)
    dh1, dh1_b, g_ffn = _ffn_bwd_in(dgate, dup, wg, wu, h1, dh2, norm_ffn_w)
    dmix = _dmix(dh1_b, wo_full)
    tmw = min(512, D)
    tk = min(2048, S)
    g_wo = _wgrad(mixed, dh1_b, pl.BlockSpec((tk, tmw), lambda m, k: (k, m)), pl.BlockSpec((tk, D), lambda m, k: (k, 0)),
                  pl.BlockSpec((tmw, D), lambda m, k: (m, 0)), (D, D), (D // tmw, S // tk), "wgrad_out")
    parts_o = _sequencer_scatter([g_wo.reshape(N_DEV, D // N_DEV, D)], "scatter_out", 3)
    d_ret, dg_r, small_w = _ret_gate_bwd(proj, o_raw, dmix, ret_norm_w, DA)
    dq_r, dk_r, dv_r, small = _ret_bwd(proj, d_ret, ret_decay_fwd, ret_decay_bwd)
    dq_a, dk_a, dv_a = _attn_bwd(proj, attn, lse, dmix, bias)
    dproj = jnp.concatenate([t.astype(BF16) for t in (dq_a, dk_a, dv_a, dq_r, dk_r, dv_r, dg_r)], axis=1)
    half = D // tmw // 2
    parts_i = []
    for part, (name, cid) in enumerate((("in_lo", 4), ("in_hi", 6))):
        g_wi = _wgrad(n1, dproj, pl.BlockSpec((S, tmw), functools.partial(lambda j, m, k, off: (0, m + off), off=part * half)),
                      pl.BlockSpec((S, NB), lambda j, m, k: (0, j)), pl.BlockSpec((1, tmw, NB), lambda j, m, k: (j, m, 0)),
                      (N_DEV, D // 2, NB), (N_DEV, half, 1), "wgrad_" + name)
        parts_i += _sequencer_scatter([g_wi], "scatter_" + name, cid)
    grad_x, g_mix = _in_bwd(dproj, wi, x2, dh1, norm_mix_w)

    big_m = (m_w_in[0], m_w_out[0], m_w_gate[0].T, m_w_up[0].T, m_w_down[0])
    big_v = (v_w_in[0], v_w_out[0], v_w_gate[0].T, v_w_up[0].T, v_w_down[0])
    names = ("adamw_in", "adamw_out", "adamw_gate", "adamw_up", "adamw_down")
    upd = [None] * 5
    for a, p in zip((2, 3, 4, 1, 0), [[t] for t in parts_f + parts_o] + [parts_i]):
        upd[a] = _adamw_block(p, big[a], big_m[a], big_v[a], names[a])

    g_dec_f = small[:, 0, 0].reshape(1, H)
    g_dec_b = small[:, 1, 0].reshape(1, H)
    g_retw = small_w[:, 0, :].reshape(1, DA)
    loss_local = jnp.sum(loss_parts[::8, 0])
    zero = jnp.zeros((1,), F32)
    part = _pack_small(g_mix, g_ffn, g_fin, g_retw, g_dec_f, g_dec_b, loss_local)
    sw = _pack_small(norm_mix_w, norm_ffn_w, norm_final_w, ret_norm_w, ret_decay_fwd, ret_decay_bwd, zero)
    sm = _pack_small(m_norm_mix_w, m_norm_ffn_w, m_norm_final_w, m_ret_norm_w, m_ret_decay_fwd, m_ret_decay_bwd, zero)
    sv = _pack_small(v_norm_mix_w, v_norm_ffn_w, v_norm_final_w, v_ret_norm_w, v_ret_decay_fwd, v_ret_decay_bwd, zero)
    shapes = [(1, D), (1, D), (D,), (1, DA), (1, H), (1, H), ()]
    sg, sd, snm, snv = [_unpack_small(t, shapes) for t in _small_step(part, sw, sm, sv)]
    loss = sg[6]

    def ordered(small_set, k):
        b = [(u[k].T if a in (2, 3) else u[k])[None] for a, u in enumerate(upd)]
        return [small_set[0], b[0], small_set[4], small_set[5], small_set[3], b[1], small_set[1], b[2], b[3], b[4],
                small_set[2]]

    return (loss, grad_x[None], *ordered(sg, 0), *ordered(sd, 1), *ordered(snm, 2), *ordered(snv, 3))
```

```python
import functools
import math

import numpy as np
import jax
import jax.numpy as jnp
from jax import lax
from jax.experimental import pallas as pl
from jax.experimental.pallas import tpu as pltpu
from jax.experimental.pallas import tpu_sc as plsc

F32 = jnp.float32
BF16 = jnp.bfloat16
SDS = jax.ShapeDtypeStruct

HEAD_DIM = 128
EPS = 1e-6
RET_CHUNK = 128
DILATIONS = (1, 4, 16)
BAND = 64
Q_TILE = 128
K_TILE = Q_TILE + 2 * BAND
KV_PAD = BAND * 4
TILE_GROUP = 8
NEG = -1e30
N_DEV = 8
N_GROUPS = 7
ADAM_LR, ADAM_B1, ADAM_B2, ADAM_EPS, ADAM_WD, ADAM_STEP = 0.001, 0.9, 0.999, 1e-08, 0.01, 10
VMEM_LIMIT = 56 * 1024 * 1024
MESH = pl.DeviceIdType.MESH
ANY = pl.BlockSpec(memory_space=pl.ANY)


def _cp(n_grid):
    return pltpu.CompilerParams(dimension_semantics=("arbitrary",) * n_grid, vmem_limit_bytes=VMEM_LIMIT)


def _sigmoid(x):
    return 1.0 / (1.0 + jnp.exp(-x))


def _rms_scale(h):
    return lax.rsqrt(jnp.mean(h * h, axis=-1, keepdims=True) + EPS)


def _rms_bwd(dn, h, w):
    r = _rms_scale(h)
    gw = dn * w
    dh = r * gw - h * (r * r * r) * jnp.mean(gw * h, axis=-1, keepdims=True)
    return dh, jnp.sum(dn * h * r, axis=0, keepdims=True)


def _dot(a, b, dims):
    return lax.dot_general(a.astype(BF16), b.astype(BF16), (dims, ((), ())), preferred_element_type=F32)


_NN = ((1,), (0,))
_NT = ((1,), (1,))
_TN = ((0,), (0,))


RESIDENT_ROWS = 256


def _resident(shape):
    return pl.BlockSpec(shape, lambda i: (0, 0), pipeline_mode=pl.Buffered(1))


def _blocked_matmul(a_ref, w_ref):
    nblk, _, fb = a_ref.shape
    out = None
    for j in range(nblk):
        part = jnp.dot(a_ref[j], w_ref[pl.ds(j * fb, fb), :], preferred_element_type=F32)
        out = part if out is None else out + part
    return out


def _norm_fwd(x, w_norm):
    S, D = x.shape
    tm = min(1024, S)

    def body(x_ref, wn_ref, n_ref):
        xf = x_ref[...]
        n_ref[...] = (xf * _rms_scale(xf) * wn_ref[...]).astype(BF16)

    row = pl.BlockSpec((tm, D), lambda i: (i, 0))
    return pl.pallas_call(body, grid=(S // tm,), name="norm_fwd", in_specs=[row, pl.BlockSpec((1, D), lambda i: (0, 0))],
                          out_specs=row, out_shape=SDS((S, D), BF16), compiler_params=_cp(1))(x, w_norm)


def _proj_part(n1, w_slots, slots, blocks, proj, n_blocks, name):
    S, D = n1.shape
    NB = w_slots.shape[2]
    tm = min(1024, S)

    def body(slots_ref, blocks_ref, n_ref, w_ref, *rest):
        rest[-1][...] = jnp.dot(n_ref[...], w_ref[0], preferred_element_type=F32)

    out_spec = pl.BlockSpec((tm, NB), lambda i, j, slots, blocks: (i, blocks[j]))
    in_specs = [pl.BlockSpec((tm, D), lambda i, j, slots, blocks: (i, 0)),
                pl.BlockSpec((1, D, NB), lambda i, j, slots, blocks: (slots[j], 0, 0))]
    args = [n1, w_slots]
    if proj is not None:
        in_specs.append(ANY)
        args.append(proj)
    return pl.pallas_call(
        body, name=name, out_shape=SDS((S, n_blocks * NB), F32),
        grid_spec=pltpu.PrefetchScalarGridSpec(num_scalar_prefetch=2, grid=(S // tm, slots.shape[0]), in_specs=in_specs,
                                               out_specs=out_spec),
        input_output_aliases={} if proj is None else {4: 0},
        compiler_params=_cp(2))(slots, blocks, *args)


def _out_fwd(x, attn, ret, w_out, w_norm):
    S, D = x.shape
    DA = attn.shape[1]
    tm = min(256, S)

    def body(x_ref, a_ref, r_ref, w_ref, wn_ref, h_ref, mix_ref, n_ref):
        a = a_ref[...].astype(BF16)
        r = r_ref[...].astype(BF16)
        mix_ref[:, :DA] = a
        mix_ref[:, DA:] = r
        h = x_ref[...] + jnp.dot(a, w_ref[:DA, :], preferred_element_type=F32) \
            + jnp.dot(r, w_ref[DA:, :], preferred_element_type=F32)
        h_ref[...] = h
        n_ref[...] = (h * _rms_scale(h) * wn_ref[...]).astype(BF16)

    row = lambda w: pl.BlockSpec((tm, w), lambda i: (i, 0))
    return pl.pallas_call(
        body, grid=(S // tm,), name="out_fwd",
        in_specs=[row(D), row(DA), row(D - DA), pl.BlockSpec((D, D), lambda i: (0, 0)),
                  pl.BlockSpec((1, D), lambda i: (0, 0))],
        out_specs=[row(D), row(D), row(D)],
        out_shape=[SDS((S, D), F32), SDS((S, D), BF16), SDS((S, D), BF16)],
        compiler_params=_cp(1))(x, attn, ret, w_out, w_norm)


def _ffn_up(n2, wg, wu):
    S, D = n2.shape
    nblk, FB, _ = wg.shape
    tm = min(1024, S)

    def body(n_ref, wg_ref, wu_ref, g_ref, u_ref, a_ref):
        n = n_ref[...]
        g = _dot(n, wg_ref[0], _NT)
        u = _dot(n, wu_ref[0], _NT)
        g_ref[0] = g.astype(BF16)
        u_ref[0] = u.astype(BF16)
        a_ref[0] = (g * _sigmoid(g) * u).astype(BF16)

    wspec = pl.BlockSpec((1, FB, D), lambda j, i: (j, 0, 0))
    ospec = pl.BlockSpec((1, tm, FB), lambda j, i: (j, i, 0))
    return pl.pallas_call(
        body, grid=(nblk, S // tm), name="ffn_up",
        in_specs=[pl.BlockSpec((tm, D), lambda j, i: (i, 0)), wspec, wspec],
        out_specs=[ospec, ospec, ospec],
        out_shape=[SDS((nblk, S, FB), BF16)] * 3,
        compiler_params=_cp(2))(n2, wg, wu)


def _ffn_down_loss(act, wd, h1, target, w_norm):
    nblk, S, FB = act.shape
    D = h1.shape[1]
    tm = min(RESIDENT_ROWS, S)

    def body(a_ref, wd_ref, h_ref, t_ref, wn_ref, dh_ref, dhb_ref, loss_ref, dw_ref):
        @pl.when(pl.program_id(0) == 0)
        def _():
            dw_ref[...] = jnp.zeros_like(dw_ref)

        h = h_ref[...] + _blocked_matmul(a_ref, wd_ref)
        w = wn_ref[...]
        err = h * _rms_scale(h) * w - t_ref[...]
        loss_ref[...] = jnp.full(loss_ref.shape, 0.5 * jnp.sum(err * err) / D, F32)
        dh, dw = _rms_bwd(err * (1.0 / D), h, w)
        dh_ref[...] = dh
        dhb_ref[...] = dh.astype(BF16)
        dw_ref[...] += dw

    row = pl.BlockSpec((tm, D), lambda i: (i, 0))
    vec = pl.BlockSpec((1, D), lambda i: (0, 0))
    return pl.pallas_call(
        body, grid=(S // tm,), name="ffn_down_loss",
        in_specs=[pl.BlockSpec((nblk, tm, FB), lambda i: (0, i, 0)), _resident((nblk * FB, D)), row, row, vec],
        out_specs=[row, row, pl.BlockSpec((8, 128), lambda i: (i, 0)), vec],
        out_shape=[SDS((S, D), F32), SDS((S, D), BF16), SDS((S // tm * 8, 128), F32), SDS((1, D), F32)],
        compiler_params=_cp(1))(act, wd, h1, target, w_norm)


def _attn_bias():
    n_heads = 8
    slopes = np.exp2(-8.0 * np.arange(1, n_heads + 1, dtype=np.float32) / n_heads)
    dist = np.abs(np.arange(K_TILE)[None, :] - BAND - np.arange(Q_TILE)[:, None])
    out = np.empty((n_heads, len(DILATIONS), Q_TILE, K_TILE), np.float32)
    for h in range(n_heads):
        for p, d in enumerate(DILATIONS):
            out[h, p] = np.where(dist <= BAND, -slopes[h] * (d * dist).astype(np.float32), NEG)
    return jnp.asarray(out)


def _attn_tiles(S, d):
    L = S // d
    per_class = L // Q_TILE
    return L, per_class, d * per_class


def _tile_rows(t, d, per_class):
    r = t // per_class
    a = (t % per_class) * Q_TILE
    q_rows = pl.ds(r + d * a, Q_TILE, stride=d) if d > 1 else pl.ds(pl.multiple_of(a, Q_TILE), Q_TILE)
    k_rows = pl.ds(KV_PAD + r + d * (a - BAND), K_TILE, stride=d) if d > 1 else pl.ds(
        pl.multiple_of(KV_PAD + a - BAND, BAND), K_TILE)
    return a, q_rows, k_rows


def _to_quarters(dst, src, n, dst_off=0):
    for r in range(4):
        dst[pl.ds(dst_off + r * (n // 4), n // 4), :] = src[pl.ds(r, n // 4, stride=4), :]


def _quarter_tile_rows(t, S):
    L = S // 16
    per_class = L // Q_TILE
    blk, tt = t // (4 * per_class), t % (4 * per_class)
    r, a = tt // per_class, (tt % per_class) * Q_TILE
    q_rows = pl.ds(blk * (S // 4) + r + 4 * a, Q_TILE, stride=4)
    k_rows = pl.ds(KV_PAD + blk * (S // 4) + r + 4 * (a - BAND), K_TILE, stride=4)
    return a, q_rows, k_rows


def _lanes(x, width):
    return jnp.concatenate([x] * (width // HEAD_DIM), axis=-1)


_BNT = (((2,), (2,)), ((0,), (0,)))
_BNN = (((2,), (1,)), ((0,), (0,)))
_BTN = (((1,), (1,)), ((0,), (0,)))


def _bdot(a, b, dims):
    return lax.dot_general(a, b, dims, preferred_element_type=F32)


def _stacked(rows, loaders):
    return [jnp.stack([f(*r) for r in rows]) for f in loaders]


def _edge_mask(a, L):
    lk = lax.broadcasted_iota(jnp.int32, (1, K_TILE), 1) + (a - BAND)
    return jnp.where((lk >= 0) & (lk < L), 0.0, NEG).astype(F32)


def _fill_padded(dst, src, S):
    dst[pl.ds(0, KV_PAD), :] = jnp.zeros((KV_PAD, HEAD_DIM), F32)
    dst[pl.ds(KV_PAD + S, KV_PAD), :] = jnp.zeros((KV_PAD, HEAD_DIM), F32)
    dst[pl.ds(KV_PAD, S), :] = src[...]


def _head_specs(S, groups, n_heads):
    return [pl.BlockSpec((S, HEAD_DIM), functools.partial(lambda h, g: (0, g * n_heads + h), g=g)) for g in groups]


def _attn_fwd(proj, bias):
    S = proj.shape[0]
    H = proj.shape[1] // (N_GROUPS * HEAD_DIM)
    scale = HEAD_DIM ** -0.5

    def body(q_ref, k_ref, v_ref, b_ref, o_ref, lse_ref, kp, vp, m_run, l_run, q4, m3, l3, acc3):
        _fill_padded(kp, k_ref, S)
        _fill_padded(vp, v_ref, S)
        o_ref[...] = jnp.zeros_like(o_ref)
        m_run[...] = jnp.full(m_run.shape, NEG, F32)
        l_run[...] = jnp.zeros_like(l_run)
        for p, d in enumerate(DILATIONS[:2]):
            L, per_class, n_tiles = _attn_tiles(S, d)

            def tiles(t, carry, p=p, d=d, L=L, per_class=per_class, n_tiles=n_tiles):
                rows = [_tile_rows(t + u * (n_tiles // TILE_GROUP), d, per_class) for u in range(TILE_GROUP)]
                qs, ks, vs, m_old, l_old, o_old, edge = _stacked(rows, (
                    lambda a, qr, kr: q_ref[qr, :].astype(BF16), lambda a, qr, kr: kp[kr, :].astype(BF16),
                    lambda a, qr, kr: vp[kr, :].astype(BF16), lambda a, qr, kr: m_run[qr, :],
                    lambda a, qr, kr: l_run[qr, :], lambda a, qr, kr: o_ref[qr, :], lambda a, qr, kr: _edge_mask(a, L)))
                s = _bdot(qs, ks, _BNT) * scale + b_ref[0, p][None] + edge
                m_new = jnp.maximum(m_old, jnp.max(s, axis=-1, keepdims=True))
                pr = jnp.exp(s - _lanes(m_new, K_TILE)).astype(BF16)
                alpha = jnp.exp(m_old - m_new)
                l_new = alpha * l_old + _bdot(pr, jnp.ones((TILE_GROUP, K_TILE, HEAD_DIM), BF16), _BNN)
                o_new = alpha * o_old + _bdot(pr, vs, _BNN)
                for u, (_, qr, _) in enumerate(rows):
                    o_ref[qr, :] = o_new[u]
                    m_run[qr, :] = m_new[u]
                    l_run[qr, :] = l_new[u]
                return carry

            lax.fori_loop(0, n_tiles // TILE_GROUP, tiles, 0)

        _to_quarters(q4, q_ref, S)
        _to_quarters(kp, k_ref, S, KV_PAD)
        _to_quarters(vp, v_ref, S, KV_PAD)
        n_tiles = _attn_tiles(S, DILATIONS[2])[2]

        def tiles3(t, carry):
            rows = [_quarter_tile_rows(t + u * (n_tiles // TILE_GROUP), S) for u in range(TILE_GROUP)]
            qs, ks, vs, edge = _stacked(rows, (
                lambda a, qr, kr: q4[qr, :].astype(BF16), lambda a, qr, kr: kp[kr, :].astype(BF16),
                lambda a, qr, kr: vp[kr, :].astype(BF16), lambda a, qr, kr: _edge_mask(a, S // DILATIONS[2])))
            s = _bdot(qs, ks, _BNT) * scale + b_ref[0, 2][None] + edge
            m_new = jnp.broadcast_to(jnp.max(s, axis=-1, keepdims=True), (TILE_GROUP, Q_TILE, HEAD_DIM))
            pr = jnp.exp(s - _lanes(m_new, K_TILE)).astype(BF16)
            l_new = _bdot(pr, jnp.ones((TILE_GROUP, K_TILE, HEAD_DIM), BF16), _BNN)
            o_new = _bdot(pr, vs, _BNN)
            for u, (_, qr, _) in enumerate(rows):
                acc3[qr, :] = o_new[u]
                m3[qr, :] = m_new[u]
                l3[qr, :] = l_new[u]
            return carry

        lax.fori_loop(0, n_tiles // TILE_GROUP, tiles3, 0)
        for r in range(4):
            nat, qtr = pl.ds(r, S // 4, stride=4), pl.ds(r * (S // 4), S // 4)
            m_a, m_b = m_run[nat, :], m3[qtr, :]
            m = jnp.maximum(m_a, m_b)
            w_a, w_b = jnp.exp(m_a - m), jnp.exp(m_b - m)
            l = w_a * l_run[nat, :] + w_b * l3[qtr, :]
            o_ref[nat, :] = (w_a * o_ref[nat, :] + w_b * acc3[qtr, :]) / l
            lse_ref[nat, :] = m + jnp.log(l)

    hspec = pl.BlockSpec((S, HEAD_DIM), lambda h: (0, h))
    padded, plain = pltpu.VMEM((S + 2 * KV_PAD, HEAD_DIM), F32), pltpu.VMEM((S, HEAD_DIM), F32)
    return pl.pallas_call(
        body, grid=(H,), name="attn_fwd",
        in_specs=_head_specs(S, (0, 1, 2), H) + [
            pl.BlockSpec((1, len(DILATIONS), Q_TILE, K_TILE), lambda h: (h, 0, 0, 0))],
        out_specs=[hspec, hspec],
        out_shape=[SDS((S, H * HEAD_DIM), F32), SDS((S, H * HEAD_DIM), F32)],
        scratch_shapes=[padded, padded] + [plain] * 6,
        compiler_params=_cp(1))(proj, proj, proj, bias)


def _put_groups(stage, dproj, groups, n_heads, sems):
    h = pl.program_id(0)
    copies = [pltpu.make_async_copy(
        stage.at[i], dproj.at[:, pl.ds(pl.multiple_of((g * n_heads + h) * HEAD_DIM, HEAD_DIM), HEAD_DIM)], sems.at[i])
        for i, g in enumerate(groups)]
    for cp in copies:
        cp.start()
    for cp in copies:
        cp.wait()


def _attn_bwd(proj, out, lse, dmix, bias, dproj):
    S = proj.shape[0]
    H = proj.shape[1] // (N_GROUPS * HEAD_DIM)
    scale = HEAD_DIM ** -0.5
    assert S // DILATIONS[2] >= 2 * Q_TILE

    def body(q_ref, k_ref, v_ref, o_ref, lse_ref, do_ref, b_ref, dproj_in, dproj_out,
             kp, vp, dkp, dvp, dsum, q4, do4, lse4, dsum4, dq_ref, dk_ref, dv_ref, stage, sems):
        _fill_padded(kp, k_ref, S)
        _fill_padded(vp, v_ref, S)
        dkp[...] = jnp.zeros_like(dkp)
        dvp[...] = jnp.zeros_like(dvp)
        dq_ref[...] = jnp.zeros_like(dq_ref)
        dsum[...] = jnp.broadcast_to(jnp.sum(do_ref[...] * o_ref[...], axis=-1, keepdims=True), dsum.shape)

        def run(n_tiles, tile_rows, p, L, q_src, do_src, lse_src, dsum_src, dq_dst, dq_adds):
            def tiles(t, carry):
                rows = [tile_rows(t + u * (n_tiles // TILE_GROUP)) for u in range(TILE_GROUP)]
                qs, ks, vs, dos, lses, dsums, dk_old, dv_old, edge = _stacked(rows, (
                    lambda a, qr, kr: q_src[qr, :].astype(BF16), lambda a, qr, kr: kp[kr, :].astype(BF16),
                    lambda a, qr, kr: vp[kr, :].astype(BF16), lambda a, qr, kr: do_src[qr, :].astype(BF16),
                    lambda a, qr, kr: lse_src[qr, :], lambda a, qr, kr: dsum_src[qr, :],
                    lambda a, qr, kr: dkp[kr, :], lambda a, qr, kr: dvp[kr, :], lambda a, qr, kr: _edge_mask(a, L)))
                s = _bdot(qs, ks, _BNT) * scale + b_ref[0, p][None] + edge
                pr = jnp.exp(s - _lanes(lses, K_TILE))
                ds = (pr * (_bdot(dos, vs, _BNT) - _lanes(dsums, K_TILE)) * scale).astype(BF16)
                dq_new = _bdot(ds, ks, _BNN)
                if dq_adds:
                    dq_new = dq_new + jnp.stack([dq_dst[qr, :] for _, qr, _ in rows])
                dk_new = dk_old + _bdot(ds, qs, _BTN)
                dv_new = dv_old + _bdot(pr.astype(BF16), dos, _BTN)
                for u, (_, qr, kr) in enumerate(rows):
                    dq_dst[qr, :] = dq_new[u]
                    dkp[kr, :] = dk_new[u]
                    dvp[kr, :] = dv_new[u]
                return carry

            lax.fori_loop(0, n_tiles // TILE_GROUP, tiles, 0)

        for p, d in enumerate(DILATIONS[:2]):
            L, per_class, n_tiles = _attn_tiles(S, d)
            run(n_tiles, functools.partial(_tile_rows, d=d, per_class=per_class), p, L,
                q_ref, do_ref, lse_ref, dsum, dq_ref, True)
        dk_ref[...] = dkp[pl.ds(KV_PAD, S), :]
        dv_ref[...] = dvp[pl.ds(KV_PAD, S), :]

        for dst, src in ((q4, q_ref), (do4, do_ref), (lse4, lse_ref), (dsum4, dsum)):
            _to_quarters(dst, src, S)
        _to_quarters(kp, k_ref, S, KV_PAD)
        _to_quarters(vp, v_ref, S, KV_PAD)
        dkp[...] = jnp.zeros_like(dkp)
        dvp[...] = jnp.zeros_like(dvp)
        dq3 = dsum
        run(_attn_tiles(S, DILATIONS[2])[2], functools.partial(_quarter_tile_rows, S=S), 2, S // DILATIONS[2],
            q4, do4, lse4, dsum4, dq3, False)
        for r in range(4):
            nat, qtr = pl.ds(r, S // 4, stride=4), pl.ds(r * (S // 4), S // 4)
            pad_qtr = pl.ds(KV_PAD + r * (S // 4), S // 4)
            dq_ref[nat, :] = dq_ref[nat, :] + dq3[qtr, :]
            dk_ref[nat, :] = dk_ref[nat, :] + dkp[pad_qtr, :]
            dv_ref[nat, :] = dv_ref[nat, :] + dvp[pad_qtr, :]
        for i, acc in enumerate((dq_ref, dk_ref, dv_ref)):
            stage[i] = acc[...].astype(BF16)
        _put_groups(stage, dproj_out, (0, 1, 2), H, sems)

    once = pl.BlockSpec((S, HEAD_DIM), lambda h: (0, h), pipeline_mode=pl.Buffered(1))
    padded, plain = pltpu.VMEM((S + 2 * KV_PAD, HEAD_DIM), F32), pltpu.VMEM((S, HEAD_DIM), F32)
    return pl.pallas_call(
        body, grid=(H,), name="attn_bwd",
        in_specs=_head_specs(S, (0, 1, 2), H) + [
            once, once, once, pl.BlockSpec((1, len(DILATIONS), Q_TILE, K_TILE), lambda h: (h, 0, 0, 0)), ANY],
        out_specs=ANY, out_shape=SDS(dproj.shape, dproj.dtype), input_output_aliases={7: 0},
        scratch_shapes=[padded] * 4 + [plain] * 8 + [pltpu.VMEM((3, S, HEAD_DIM), BF16), pltpu.SemaphoreType.DMA((3,))],
        compiler_params=_cp(1))(proj, proj, proj, out, lse, dmix, bias, dproj)


def _ret_consts(lg, forward):
    C = RET_CHUNK
    i = lax.broadcasted_iota(jnp.int32, (C, C), 0)
    j = lax.broadcasted_iota(jnp.int32, (C, C), 1)
    rel = (i - j) if forward else (j - i)
    inside = (rel >= 0) if forward else (rel > 0)
    relf = jnp.maximum(rel, 0).astype(F32)
    mask = jnp.where(inside, jnp.exp(lg * relf), 0.0)
    idx = lax.broadcasted_iota(jnp.int32, (C, 1), 0).astype(F32)
    q_exp = (idx + 1.0) if forward else (C - idx)
    k_exp = (C - 1.0 - idx) if forward else idx
    return mask, relf, jnp.exp(lg * q_exp), q_exp, jnp.exp(lg * k_exp), k_exp, jnp.exp(lg * C)


def _log_decay(dec_ref, h):
    return -jnp.exp(jnp.full((1, 1), dec_ref[0, h], F32))


FFN_BLOCK = 704
CHUNK_BATCH = 8


def _batch_rows(b):
    n = CHUNK_BATCH * RET_CHUNK
    return pl.ds(pl.multiple_of(b * n, n), n)


def _batch_chunks(b):
    return pl.ds(pl.multiple_of(b * CHUNK_BATCH, CHUNK_BATCH), CHUNK_BATCH)


def _chunks3(x):
    return x.reshape(CHUNK_BATCH, RET_CHUNK, HEAD_DIM)


def _ret_scan(buf, c_decs, nc, reverse):
    def step(n, carry):
        new = []
        for way, r in enumerate(carry):
            c = n if (way == 0) != reverse else nc - 1 - n
            term = buf[way, c]
            buf[way, c] = r
            new.append(r * c_decs[way] + term)
        return tuple(new)

    lax.fori_loop(0, nc, step, (jnp.zeros((HEAD_DIM, HEAD_DIM), F32),) * 2)


def _ret_fwd(proj, dec_f, dec_b, w_norm):
    S = proj.shape[0]
    H = proj.shape[1] // (N_GROUPS * HEAD_DIM)
    nc = S // RET_CHUNK
    scale = HEAD_DIM ** -0.5

    def body(df_ref, db_ref, q_ref, k_ref, v_ref, g_ref, w_ref, y_ref, o_ref, states):
        h = pl.program_id(0)
        consts = [_ret_consts(_log_decay(dref, h), fw) for fw, dref in ((True, df_ref), (False, db_ref))]

        def kv_step(b, carry):
            rows, batch = _batch_rows(b), _batch_chunks(b)
            k3 = _chunks3(k_ref[rows, :])
            v3 = _chunks3(v_ref[rows, :]).astype(BF16)
            for way in range(2):
                states[way, batch] = _bdot((k3 * consts[way][4]).astype(BF16), v3, _BTN)
            return carry

        lax.fori_loop(0, nc // CHUNK_BATCH, kv_step, 0)
        _ret_scan(states, [c[6] for c in consts], nc, False)

        def out_step(b, carry):
            rows, batch = _batch_rows(b), _batch_chunks(b)
            q3 = _chunks3(q_ref[rows, :] * scale)
            k3 = _chunks3(k_ref[rows, :]).astype(BF16)
            v3 = _chunks3(v_ref[rows, :]).astype(BF16)
            a0 = _bdot(q3.astype(BF16), k3, _BNT)
            o = None
            for way in range(2):
                mask, q_dec = consts[way][0], consts[way][2]
                part = _bdot((a0 * mask).astype(BF16), v3, _BNN) \
                    + _bdot((q3 * q_dec).astype(BF16), states[way, batch].astype(BF16), _BNN)
                o = part if o is None else o + part
            o_ref[rows, :] = o.reshape(CHUNK_BATCH * RET_CHUNK, HEAD_DIM)
            return carry

        lax.fori_loop(0, nc // CHUNK_BATCH, out_step, 0)
        o = o_ref[...]
        g = g_ref[...]
        y_ref[...] = o * _rms_scale(o) * w_ref[...] * (g * _sigmoid(g))

    hspec = pl.BlockSpec((S, HEAD_DIM), lambda h: (0, h))
    smem = pl.BlockSpec(memory_space=pltpu.SMEM)
    return pl.pallas_call(
        body, grid=(H,), name="ret_fwd",
        in_specs=[smem, smem] + _head_specs(S, (3, 4, 5, 6), H) + [pl.BlockSpec((1, HEAD_DIM), lambda h: (0, h))],
        out_specs=[hspec, hspec],
        out_shape=[SDS((S, H * HEAD_DIM), F32)] * 2,
        scratch_shapes=[pltpu.VMEM((2, nc, HEAD_DIM, HEAD_DIM), F32)],
        compiler_params=_cp(1))(dec_f, dec_b, proj, proj, proj, proj, w_norm)


def _ret_gate_bwd(proj, o_raw, dmix, w_norm, col0, dproj):
    S = proj.shape[0]
    H = proj.shape[1] // (N_GROUPS * HEAD_DIM)

    def body(g_ref, o_ref, dy_ref, w_ref, dproj_in, do_ref, dw_ref, dproj_out, dg_ref, sems):
        o = o_ref[...]
        g = g_ref[...]
        dy = dy_ref[...]
        w = w_ref[...]
        rr = _rms_scale(o)
        normed = o * rr
        sg = _sigmoid(g)
        silu = g * sg
        dw_ref[0] = jnp.broadcast_to(jnp.sum(dy * normed * silu, axis=0, keepdims=True), (8, HEAD_DIM))
        dg_ref[0] = (dy * normed * w * (sg * (1.0 + g * (1.0 - sg)))).astype(BF16)
        dnormed = dy * w * silu
        do_ref[...] = rr * dnormed - o * (rr * rr * rr) * jnp.mean(dnormed * o, axis=-1, keepdims=True)
        _put_groups(dg_ref, dproj_out, (6,), H, sems)

    hspec = pl.BlockSpec((S, HEAD_DIM), lambda h: (0, h))
    nh0 = col0 // HEAD_DIM
    return pl.pallas_call(
        body, grid=(H,), name="ret_gate_bwd",
        in_specs=_head_specs(S, (6,), H) + [hspec, pl.BlockSpec((S, HEAD_DIM), lambda h: (0, nh0 + h)),
                                            pl.BlockSpec((1, HEAD_DIM), lambda h: (0, h)), ANY],
        out_specs=[hspec, pl.BlockSpec((1, 8, HEAD_DIM), lambda h: (h, 0, 0)), ANY],
        out_shape=[SDS((S, H * HEAD_DIM), F32), SDS((H, 8, HEAD_DIM), F32), SDS(dproj.shape, dproj.dtype)],
        input_output_aliases={4: 2},
        scratch_shapes=[pltpu.VMEM((1, S, HEAD_DIM), BF16), pltpu.SemaphoreType.DMA((1,))],
        compiler_params=_cp(1))(proj, o_raw, dmix, w_norm, dproj)


def _ret_bwd(proj, d_out, dec_f, dec_b, dproj):
    S = proj.shape[0]
    H = proj.shape[1] // (N_GROUPS * HEAD_DIM)
    C = RET_CHUNK
    nc = S // C
    scale = HEAD_DIM ** -0.5

    def body(df_ref, db_ref, q_ref, k_ref, v_ref, do, dproj_in, small_ref, dproj_out, states, d_states, stage, sems):
        h = pl.program_id(0)
        lgs = [_log_decay(df_ref, h), _log_decay(db_ref, h)]
        consts = [_ret_consts(lg, fw) for lg, fw in zip(lgs, (True, False))]

        def prep_step(b, carry):
            rows, batch = _batch_rows(b), _batch_chunks(b)
            q3 = _chunks3(q_ref[rows, :] * scale)
            k3 = _chunks3(k_ref[rows, :])
            v3 = _chunks3(v_ref[rows, :]).astype(BF16)
            do3 = _chunks3(do[rows, :]).astype(BF16)
            for way in range(2):
                states[way, batch] = _bdot((k3 * consts[way][4]).astype(BF16), v3, _BTN)
                d_states[way, batch] = _bdot((q3 * consts[way][2]).astype(BF16), do3, _BTN)
            return carry

        lax.fori_loop(0, nc // CHUNK_BATCH, prep_step, 0)
        c_decs = [c[6] for c in consts]
        _ret_scan(states, c_decs, nc, False)
        _ret_scan(d_states, c_decs, nc, True)

        def main_step(b, dlams):
            rows, batch = _batch_rows(b), _batch_chunks(b)
            q3 = _chunks3(q_ref[rows, :] * scale)
            k3 = _chunks3(k_ref[rows, :])
            q3b, k3b = q3.astype(BF16), k3.astype(BF16)
            v3b = _chunks3(v_ref[rows, :]).astype(BF16)
            do3b = _chunks3(do[rows, :]).astype(BF16)
            a0 = _bdot(q3b, k3b, _BNT)
            pv = _bdot(do3b, v3b, _BNT)
            dq = dk = dv = None
            new_dlams = []
            for way in range(2):
                mask, relf, q_dec, q_exp, k_dec, k_exp, c_dec = consts[way]
                state, d_state = states[way, batch], d_states[way, batch]
                dp = pv * mask
                dpb = dp.astype(BF16)
                gq = _bdot(do3b, state.astype(BF16), _BNT)
                gk = _bdot(v3b, d_state.astype(BF16), _BNT)
                parts = (_bdot(dpb, k3b, _BNN) + q_dec * gq, _bdot(dpb, q3b, _BTN) + k_dec * gk,
                         _bdot((a0 * mask).astype(BF16), do3b, _BTN)
                         + _bdot((k3 * k_dec).astype(BF16), d_state.astype(BF16), _BNN))
                dq, dk, dv = parts if dq is None else (dq + parts[0], dk + parts[1], dv + parts[2])
                total = lambda x: jnp.sum(jnp.sum(x, axis=0), axis=0, keepdims=True)
                new_dlams.append(dlams[way] + total(relf * a0 * dp)
                                 + total(q_exp * q_dec * q3 * gq + k_exp * k_dec * k3 * gk)
                                 + (C * c_dec) * total(state * d_state))
            flat = lambda x: x.reshape(CHUNK_BATCH * C, HEAD_DIM)
            stage[0, rows, :] = (flat(dq) * scale).astype(BF16)
            stage[1, rows, :] = flat(dk).astype(BF16)
            stage[2, rows, :] = flat(dv).astype(BF16)
            return tuple(new_dlams)

        dlams = lax.fori_loop(0, nc // CHUNK_BATCH, main_step, (jnp.zeros((1, HEAD_DIM), F32),) * 2)
        for row, (dlam, lg) in enumerate(zip(dlams, lgs)):
            small_ref[0, pl.ds(row, 1), :] = jnp.broadcast_to(jnp.sum(dlam, axis=-1, keepdims=True) * lg, (1, HEAD_DIM))
        small_ref[0, pl.ds(2, 6), :] = jnp.zeros((6, HEAD_DIM), F32)
        _put_groups(stage, dproj_out, (3, 4, 5), H, sems)

    hspec = pl.BlockSpec((S, HEAD_DIM), lambda h: (0, h))
    smem = pl.BlockSpec(memory_space=pltpu.SMEM)
    return pl.pallas_call(
        body, grid=(H,), name="ret_bwd",
        in_specs=[smem, smem] + _head_specs(S, (3, 4, 5), H) + [hspec, ANY],
        out_specs=[pl.BlockSpec((1, 8, HEAD_DIM), lambda h: (h, 0, 0)), ANY],
        out_shape=[SDS((H, 8, HEAD_DIM), F32), SDS(dproj.shape, dproj.dtype)], input_output_aliases={6: 1},
        scratch_shapes=[pltpu.VMEM((2, nc, HEAD_DIM, HEAD_DIM), F32), pltpu.VMEM((2, nc, HEAD_DIM, HEAD_DIM), F32),
                        pltpu.VMEM((3, S, HEAD_DIM), BF16), pltpu.SemaphoreType.DMA((3,))],
        compiler_params=_cp(1))(dec_f, dec_b, proj, proj, proj, d_out, dproj)


def _ffn_bwd_act(dh2, wd, g, u):
    S, D = dh2.shape
    nblk, _, FB = g.shape
    tm = min(1024, S)

    def body(dh_ref, wd_ref, g_ref, u_ref, dg_ref, du_ref):
        dact = _dot(dh_ref[...], wd_ref[...], _NT)
        gg = g_ref[0].astype(F32)
        sg = _sigmoid(gg)
        dg_ref[0] = (dact * u_ref[0].astype(F32) * (sg * (1.0 + gg * (1.0 - sg)))).astype(BF16)
        du_ref[0] = (dact * (gg * sg)).astype(BF16)

    blk = pl.BlockSpec((1, tm, FB), lambda j, i: (j, i, 0))
    return pl.pallas_call(
        body, grid=(nblk, S // tm), name="ffn_bwd_act",
        in_specs=[pl.BlockSpec((tm, D), lambda j, i: (i, 0)), pl.BlockSpec((FB, D), lambda j, i: (j, 0)), blk, blk],
        out_specs=[blk, blk], out_shape=[SDS((nblk, S, FB), BF16)] * 2,
        compiler_params=_cp(2))(dh2, wd, g, u)


def _ffn_bwd_in(dg, du, wg, wu, h1, dh2, w_norm):
    nblk, S, FB = dg.shape
    D = h1.shape[1]
    tm = min(RESIDENT_ROWS, S)
    blk = pl.BlockSpec((nblk, tm, FB), lambda i: (0, i, 0))
    row = pl.BlockSpec((tm, D), lambda i: (i, 0))
    vec = pl.BlockSpec((1, D), lambda i: (0, 0))

    def gate_body(dg_ref, wg_ref, part_ref):
        part_ref[...] = _blocked_matmul(dg_ref, wg_ref)

    part = pl.pallas_call(
        gate_body, grid=(S // tm,), name="ffn_bwd_in_gate", in_specs=[blk, _resident((nblk * FB, D))],
        out_specs=row, out_shape=SDS((S, D), F32), compiler_params=_cp(1))(dg, wg.reshape(nblk * FB, D))

    def body(du_ref, wu_ref, part_ref, h_ref, dh2_ref, wn_ref, dh_ref, dhb_ref, dw_ref):
        @pl.when(pl.program_id(0) == 0)
        def _():
            dw_ref[...] = jnp.zeros_like(dw_ref)

        dh, dw = _rms_bwd(part_ref[...] + _blocked_matmul(du_ref, wu_ref), h_ref[...], wn_ref[...])
        dh = dh2_ref[...] + dh
        dh_ref[...] = dh
        dhb_ref[...] = dh.astype(BF16)
        dw_ref[...] += dw

    return pl.pallas_call(
        body, grid=(S // tm,), name="ffn_bwd_in",
        in_specs=[blk, _resident((nblk * FB, D)), row, row, row, vec],
        out_specs=[row, row, vec], out_shape=[SDS((S, D), F32), SDS((S, D), BF16), SDS((1, D), F32)],
        compiler_params=_cp(1))(du, wu.reshape(nblk * FB, D), part, h1, dh2, w_norm)


def _dmix(dh1, w_out):
    S, D = dh1.shape
    tm = min(512, S)

    def body(dh_ref, w_ref, o_ref):
        o_ref[...] = _dot(dh_ref[...], w_ref[...], _NT)

    row = pl.BlockSpec((tm, D), lambda i: (i, 0))
    return pl.pallas_call(
        body, grid=(S // tm,), name="dmix", in_specs=[row, pl.BlockSpec((D, D), lambda i: (0, 0))],
        out_specs=row, out_shape=SDS((S, D), F32), compiler_params=_cp(1))(dh1, w_out)


def _in_bwd(dproj, w_blk, x, dh1, w_norm):
    S, D = x.shape
    nblk, _, NB = w_blk.shape
    tm = min(RESIDENT_ROWS, S)

    def body(dp_ref, w_ref, x_ref, dh1_ref, wn_ref, dx_ref, dw_ref):
        @pl.when(pl.program_id(0) == 0)
        def _():
            dw_ref[...] = jnp.zeros_like(dw_ref)

        dn = None
        for j in range(nblk):
            part = _dot(dp_ref[:, pl.ds(j * NB, NB)], w_ref[j], _NT)
            dn = part if dn is None else dn + part
        dh, dw = _rms_bwd(dn, x_ref[...], wn_ref[...])
        dx_ref[...] = dh1_ref[...] + dh
        dw_ref[...] += dw

    row = pl.BlockSpec((tm, D), lambda i: (i, 0))
    vec = pl.BlockSpec((1, D), lambda i: (0, 0))
    return pl.pallas_call(
        body, grid=(S // tm,), name="in_bwd",
        in_specs=[pl.BlockSpec((tm, nblk * NB), lambda i: (i, 0)),
                  pl.BlockSpec((nblk, D, NB), lambda i: (0, 0, 0), pipeline_mode=pl.Buffered(1)), row, row, vec],
        out_specs=[row, vec], out_shape=[SDS((S, D), F32), SDS((1, D), F32)],
        compiler_params=_cp(1))(dproj, w_blk, x, dh1, w_norm)


def _wgrad(a, b, a_spec, b_spec, o_spec, o_shape, grid, name):
    nk = grid[-1]

    def ld(ref):
        return ref[0] if len(ref.shape) == 3 else ref[...]

    def body(a_ref, b_ref, o_ref, acc):
        k = pl.program_id(len(grid) - 1)

        @pl.when(k == 0)
        def _():
            acc[...] = jnp.zeros_like(acc)

        acc[...] += _dot(ld(a_ref), ld(b_ref), _TN)

        @pl.when(k == nk - 1)
        def _():
            if len(o_ref.shape) == 3:
                o_ref[0] = acc[...].astype(o_ref.dtype)
            else:
                o_ref[...] = acc[...].astype(o_ref.dtype)

    return pl.pallas_call(
        body, grid=grid, name=name, in_specs=[a_spec, b_spec], out_specs=o_spec, out_shape=SDS(o_shape, BF16),
        scratch_shapes=[pltpu.VMEM(o_spec.block_shape[-2:], F32)], compiler_params=_cp(len(grid)))(a, b)


def _peer(k):
    x, y, c = lax.axis_index("x"), lax.axis_index("y"), lax.axis_index("c")
    px = 1 - x if k & 4 else x
    py = 1 - y if k & 2 else y
    pc = 1 - c if k & 1 else c
    return (px, py, pc), 4 * px + 2 * py + pc


def _exchange_copies(srcs, lands, send_sems, recv_sems, which, gather):
    _, me = _peer(0)
    pairs = []
    for pos, a in enumerate(which):
        for k in range(1, N_DEV):
            dev, idx = _peer(k)
            sem = pos * (N_DEV - 1) + k - 1
            src = srcs[a] if gather else srcs[a].at[idx]
            mk = functools.partial(pltpu.make_async_remote_copy, src_ref=src, send_sem=send_sems.at[sem],
                                   recv_sem=recv_sems.at[sem], device_id=dev, device_id_type=MESH)
            pairs.append((mk(dst_ref=lands[a].at[me]), mk(dst_ref=lands[a].at[idx])))
    return pairs


def _sequencer_kernel(name, collective_id, n_remote, n_local):
    return pl.kernel(mesh=plsc.ScalarSubcoreMesh(axis_name="sequencer", num_cores=1), name=name,
                     scratch_types=(pltpu.SemaphoreType.DMA((n_remote,)), pltpu.SemaphoreType.DMA((n_remote,)),
                                    pltpu.SemaphoreType.DMA((n_local,))),
                     compiler_params=pltpu.CompilerParams(collective_id=collective_id))


def _handshake(ks):
    barrier = pltpu.get_barrier_semaphore()
    for k in ks:
        pl.semaphore_signal(barrier, inc=1, device_id=_peer(k)[0], device_id_type=MESH)
    pl.semaphore_wait(barrier, len(ks))


def _sequencer_scatter(arrays, name, collective_id):
    n = len(arrays)
    hbm = pltpu.MemorySpace.HBM
    srcs = [jax.new_ref(a, memory_space=hbm) for a in arrays]
    lands = [jax.empty_ref(SDS(a.shape, a.dtype), memory_space=hbm) for a in arrays]

    @_sequencer_kernel(name, collective_id, n * (N_DEV - 1), n)
    def launch(send_sems, recv_sems, local_sems):
        _handshake(range(1, N_DEV))
        _, me = _peer(0)
        local = [pltpu.make_async_copy(srcs[a].at[me], lands[a].at[me], local_sems.at[a]) for a in range(n)]
        pairs = _exchange_copies(srcs, lands, send_sems, recv_sems, range(n), False)
        for out, _ in pairs:
            out.start()
        for cp in local:
            cp.start()
        for out, arrival in pairs:
            out.wait_send()
            arrival.wait_recv()
        for cp in local:
            cp.wait()

    launch()
    return [r[...] for r in lands]


SIBLING = 1
OTHER_CHIPS = (2, 4, 6)


def _sequencer_gather(arrays, name, collective_id):
    n = len(arrays)
    hbm = pltpu.MemorySpace.HBM
    srcs = [jax.new_ref(a, memory_space=hbm) for a in arrays]
    lands = [jax.empty_ref(SDS((N_DEV,) + a.shape, a.dtype), memory_space=hbm) for a in arrays]

    @_sequencer_kernel(name, collective_id, n * (N_DEV - 1), n)
    def launch(send_sems, recv_sems, local_sems):
        _handshake((SIBLING,) + OTHER_CHIPS)
        _, me = _peer(0)
        sibling, _ = _peer(SIBLING)

        def copy(a, k, src, block, to):
            sem = a * (N_DEV - 1) + k - 1
            return pltpu.make_async_remote_copy(src_ref=src, dst_ref=lands[a].at[block], send_sem=send_sems.at[sem],
                                                recv_sem=recv_sems.at[sem], device_id=to, device_id_type=MESH)

        local = [pltpu.make_async_copy(srcs[a], lands[a].at[me], local_sems.at[a]) for a in range(n)]
        first = [copy(a, k, srcs[a], me, _peer(k)[0]) for a in range(n) for k in OTHER_CHIPS + (SIBLING,)]
        for cp in first + local:
            cp.start()
        passed = []
        for a in range(n):
            for k in OTHER_CHIPS:
                _, block = _peer(k)
                copy(a, k, srcs[a], block, sibling).wait_recv()
                passed.append(copy(a, k ^ SIBLING, lands[a].at[block], block, sibling))
                passed[-1].start()
        for a in range(n):
            for k in (SIBLING,) + tuple(k ^ SIBLING for k in OTHER_CHIPS):
                copy(a, k, srcs[a], _peer(k)[1], sibling).wait_recv()
        for cp in first + passed:
            cp.wait_send()
        for cp in local:
            cp.wait()

    launch()
    return [r[...] for r in lands]


def _sequencer_gather_chips(array, name, collective_id, chips):
    hbm = pltpu.MemorySpace.HBM
    src = jax.new_ref(array, memory_space=hbm)
    land = jax.empty_ref(SDS((2 * len(chips),) + array.shape, array.dtype), memory_space=hbm)

    @_sequencer_kernel(name, collective_id, 2 * len(chips), 1)
    def launch(send_sems, recv_sems, local_sems):
        _handshake((SIBLING,) + tuple(k for k in chips if k))
        c = lax.axis_index("c")
        sibling, _ = _peer(SIBLING)

        def copy(sem, src_ref, slot, to):
            return pltpu.make_async_remote_copy(src_ref=src_ref, dst_ref=land.at[slot], send_sem=send_sems.at[sem],
                                                recv_sem=recv_sems.at[sem], device_id=to, device_id_type=MESH)

        started = []
        for pos, k in enumerate(chips):
            started.append(copy(2 * pos, src, 2 * pos + c, _peer(k)[0] if k else sibling))
            started[-1].start()
        for pos, k in enumerate(chips):
            if k:
                copy(2 * pos, src, 2 * pos + c, sibling).wait_recv()
                started.append(copy(2 * pos + 1, land.at[2 * pos + c], 2 * pos + c, sibling))
                started[-1].start()
        for pos, k in enumerate(chips):
            copy(2 * pos + 1 if k else 2 * pos, src, 2 * pos + 1 - c, sibling).wait_recv()
        for cp in started:
            cp.wait_send()

    launch()
    return land[...]


SMALL_ROWS = 64


def _small_step(part, w, m, v):
    def body(p_ref, w_ref, m_ref, v_ref, g_ref, d_ref, nm_ref, nv_ref, gath, send_sems, recv_sems):
        _, me = _peer(0)
        gath[me] = p_ref[...]
        copies = []
        for k in range(1, N_DEV):
            dev, idx = _peer(k)
            out = pltpu.make_async_remote_copy(src_ref=p_ref, dst_ref=gath.at[me], send_sem=send_sems.at[k - 1],
                                               recv_sem=recv_sems.at[k - 1], device_id=dev, device_id_type=MESH)
            out.start()
            arrival = pltpu.make_async_remote_copy(src_ref=p_ref, dst_ref=gath.at[idx], send_sem=send_sems.at[k - 1],
                                                   recv_sem=recv_sems.at[k - 1], device_id=dev, device_id_type=MESH)
            copies.append((out, arrival))
        for out, arrival in copies:
            out.wait_send()
            arrival.wait_recv()
        g = gath[0]
        for p in range(1, N_DEV):
            g = g + gath[p]
        g_ref[...] = g
        d_ref[...], nm_ref[...], nv_ref[...] = _adamw(w_ref[...], g, m_ref[...], v_ref[...])

    vm = pl.BlockSpec(memory_space=pltpu.VMEM)
    return pl.pallas_call(
        body, name="small_step", in_specs=[vm] * 4, out_specs=[vm] * 4,
        out_shape=[SDS((SMALL_ROWS, 128), F32)] * 4,
        scratch_shapes=[pltpu.VMEM((N_DEV, SMALL_ROWS, 128), F32), pltpu.SemaphoreType.DMA((N_DEV - 1,)),
                        pltpu.SemaphoreType.DMA((N_DEV - 1,))])(part, w, m, v)


def _adamw(w, g, m, v):
    m = ADAM_B1 * m + (1.0 - ADAM_B1) * g
    v = ADAM_B2 * v + (1.0 - ADAM_B2) * (g * g)
    m_hat = m / (1.0 - ADAM_B1 ** ADAM_STEP)
    v_hat = v / (1.0 - ADAM_B2 ** ADAM_STEP)
    delta = -ADAM_LR * (m_hat / (jnp.sqrt(v_hat) + ADAM_EPS) + ADAM_WD * w)
    return delta, m, v


def _adamw_block(parts, w, m, v, name):
    R, C = w.shape
    n_parts = len(parts)
    Rp = R // n_parts
    tr = next(t for t in (256, 128, 64, 32, 16, 8) if Rp % t == 0 and t * C <= 256 * 1024)
    per_part = Rp // tr

    def body(*refs):
        p_refs = refs[:n_parts]
        w_ref, m_ref, v_ref, g_ref, d_ref, nm_ref, nv_ref = refs[n_parts:]
        for k, p_ref in enumerate(p_refs):
            @pl.when(pl.program_id(0) // per_part == k)
            def _(p_ref=p_ref):
                g = p_ref[0].astype(F32)
                for p in range(1, N_DEV):
                    g = g + p_ref[p].astype(F32)
                g_ref[...] = g
                d_ref[...], nm_ref[...], nv_ref[...] = _adamw(w_ref[...], g, m_ref[...], v_ref[...])

    row = pl.BlockSpec((tr, C), lambda i: (i, 0))
    part_specs = [pl.BlockSpec((N_DEV, tr, C), functools.partial(
        lambda i, k: (0, jnp.clip(i - k * per_part, 0, per_part - 1), 0), k=k)) for k in range(n_parts)]
    return pl.pallas_call(
        body, grid=(R // tr,), name=name, in_specs=part_specs + [row, row, row],
        out_specs=[row] * 4, out_shape=[SDS((R, C), F32)] * 4, compiler_params=_cp(1))(*parts, w, m, v)


def _pack_small(mix, ffn, fin, retw, dec_f, dec_b, loss):
    flat = jnp.concatenate([mix.reshape(-1), ffn.reshape(-1), fin.reshape(-1), retw.reshape(-1), dec_f.reshape(-1),
                            dec_b.reshape(-1), loss.reshape(-1)])
    return jnp.pad(flat, (0, SMALL_ROWS * 128 - flat.shape[0])).reshape(SMALL_ROWS, 128)


def _unpack_small(packed, shapes):
    flat = packed.reshape(-1)
    out, at = [], 0
    for s in shapes:
        n = math.prod(s)
        out.append(flat[at:at + n].reshape(s))
        at += n
    return out


def kernel(x, norm_mix_w, w_in, ret_decay_fwd, ret_decay_bwd, ret_norm_w, w_out, norm_ffn_w, w_gate, w_up, w_down, norm_final_w, loss_target, m_norm_mix_w, m_w_in, m_ret_decay_fwd, m_ret_decay_bwd, m_ret_norm_w, m_w_out, m_norm_ffn_w, m_w_gate, m_w_up, m_w_down, m_norm_final_w, v_norm_mix_w, v_w_in, v_ret_decay_fwd, v_ret_decay_bwd, v_ret_norm_w, v_w_out, v_norm_ffn_w, v_w_gate, v_w_up, v_w_down, v_norm_final_w):
    x2 = x[0]
    tgt = loss_target[0]
    S, D = x2.shape
    H = ret_norm_w.shape[1] // HEAD_DIM
    DA = H * HEAD_DIM
    fin_w = norm_final_w.reshape(1, D)
    big = (w_in[0], w_out[0], w_gate[0].T, w_up[0].T, w_down[0])

    big_b = [w.astype(BF16) for w in big]
    stages = ((0,), (4, 2), (6,))
    wi_stages = [_sequencer_gather_chips(big_b[0], name, cid, ks)
                 for name, cid, ks in zip(("gather_in_own", "gather_in_near", "gather_in_far"), (0, 7, 8), stages)]
    wo, = _sequencer_gather(big_b[1:2], "gather_out", 1)
    wg, wu = _sequencer_gather(big_b[2:4], "gather_gate_up", 9)
    wd, = _sequencer_gather(big_b[4:], "gather_down", 5)
    wi, = _sequencer_gather(big_b[:1], "gather_in_ordered", 10)
    NB = big_b[0].shape[1]
    ax, ay = lax.axis_index("x"), lax.axis_index("y")
    chip_of = {k: 2 * (1 - ax if k & 4 else ax) + (1 - ay if k & 2 else ay) for k in (0, 2, 4, 6)}

    n1 = _norm_fwd(x2, norm_mix_w)
    ac = lax.axis_index("c")
    me = 2 * chip_of[0] + ac
    vec = lambda *v: jnp.stack([jnp.asarray(t, jnp.int32) for t in v])
    proj = _proj_part(n1, big_b[0][None], vec(0), vec(me), None, N_DEV, "proj_self")
    for ks, w_st, name in zip(stages, wi_stages, ("proj_sibling", "proj_near", "proj_far")):
        slots, blocks = [], []
        for pos, k in enumerate(ks):
            for core in ((1 - ac,) if k == 0 else (0, 1)):
                slots.append(2 * pos + core)
                blocks.append(2 * chip_of[k] + core)
        proj = _proj_part(n1, w_st, vec(*slots), vec(*blocks), proj, N_DEV, name)
    bias = _attn_bias()[:H]
    attn, lse = _attn_fwd(proj, bias)
    ret, o_raw = _ret_fwd(proj, ret_decay_fwd, ret_decay_bwd, ret_norm_w)
    wo_full = wo.reshape(D, D)
    d_ff = N_DEV * wd.shape[1]
    FB = FFN_BLOCK if d_ff % FFN_BLOCK == 0 else wd.shape[1]
    n_fb = d_ff // FB
    wg, wu = wg.reshape(n_fb, FB, D), wu.reshape(n_fb, FB, D)
    wd_full = wd.reshape(d_ff, D)
    h1, mixed, n2 = _out_fwd(x2, attn, ret, wo_full, norm_ffn_w)
    gate, up, act = _ffn_up(n2, wg, wu)
    dh2, dh2_b, loss_parts, g_fin = _ffn_down_loss(act, wd_full, h1, tgt, fin_w)

    dgate, dup = _ffn_bwd_act(dh2_b, wd_full, gate, up)
    tn = min(1024, D)
    ffn_specs = (pl.BlockSpec((1, S, FB), lambda j, n, k: (j, 0, 0)), pl.BlockSpec((S, tn), lambda j, n, k: (0, n)),
                 pl.BlockSpec((1, FB, tn), lambda j, n, k: (j, 0, n)), (n_fb, FB, D), (n_fb, D // tn, 1))
    per_dev = (N_DEV, d_ff // N_DEV, D)
    g_wd = _wgrad(act, dh2_b, *ffn_specs, "wgrad_down").reshape(per_dev)
    g_wg = _wgrad(dgate, n2, *ffn_specs, "wgrad_gate").reshape(per_dev)
    g_wu = _wgrad(dup, n2, *ffn_specs, "wgrad_up").reshape(per_dev)
    parts_f = _sequencer_scatter([g_wg, g_wu, g_wd], "scatter_ffn", 2)
    dh1, dh1_b, g_ffn = _ffn_bwd_in(dgate, dup, wg, wu, h1, dh2, norm_ffn_w)
    dmix = _dmix(dh1_b, wo_full)
    tmw = min(512, D)
    tk = min(2048, S)
    g_wo = _wgrad(mixed, dh1_b, pl.BlockSpec((tk, tmw), lambda m, k: (k, m)), pl.BlockSpec((tk, D), lambda m, k: (k, 0)),
                  pl.BlockSpec((tmw, D), lambda m, k: (m, 0)), (D, D), (D // tmw, S // tk), "wgrad_out")
    parts_o = _sequencer_scatter([g_wo.reshape(N_DEV, D // N_DEV, D)], "scatter_out", 3)
    d_ret, small_w, dproj = _ret_gate_bwd(proj, o_raw, dmix, ret_norm_w, DA, lax.empty(proj.shape, BF16))
    small, dproj = _ret_bwd(proj, d_ret, ret_decay_fwd, ret_decay_bwd, dproj)
    dproj = _attn_bwd(proj, attn, lse, dmix, bias, dproj)
    half = D // tmw // 2
    parts_i = []
    for part, (name, cid) in enumerate((("in_lo", 4), ("in_hi", 6))):
        g_wi = _wgrad(n1, dproj, pl.BlockSpec((S, tmw), functools.partial(lambda j, m, k, off: (0, m + off), off=part * half)),
                      pl.BlockSpec((S, NB), lambda j, m, k: (0, j)), pl.BlockSpec((1, tmw, NB), lambda j, m, k: (j, m, 0)),
                      (N_DEV, D // 2, NB), (N_DEV, half, 1), "wgrad_" + name)
        parts_i += _sequencer_scatter([g_wi], "scatter_" + name, cid)
    grad_x, g_mix = _in_bwd(dproj, wi, x2, dh1, norm_mix_w)

    big_m = (m_w_in[0], m_w_out[0], m_w_gate[0].T, m_w_up[0].T, m_w_down[0])
    big_v = (v_w_in[0], v_w_out[0], v_w_gate[0].T, v_w_up[0].T, v_w_down[0])
    names = ("adamw_in", "adamw_out", "adamw_gate", "adamw_up", "adamw_down")
    upd = [None] * 5
    for a, p in zip((2, 3, 4, 1, 0), [[t] for t in parts_f + parts_o] + [parts_i]):
        upd[a] = _adamw_block(p, big[a], big_m[a], big_v[a], names[a])

    g_dec_f = small[:, 0, 0].reshape(1, H)
    g_dec_b = small[:, 1, 0].reshape(1, H)
    g_retw = small_w[:, 0, :].reshape(1, DA)
    loss_local = jnp.sum(loss_parts[::8, 0])
    zero = jnp.zeros((1,), F32)
    part = _pack_small(g_mix, g_ffn, g_fin, g_retw, g_dec_f, g_dec_b, loss_local)
    sw = _pack_small(norm_mix_w, norm_ffn_w, norm_final_w, ret_norm_w, ret_decay_fwd, ret_decay_bwd, zero)
    sm = _pack_small(m_norm_mix_w, m_norm_ffn_w, m_norm_final_w, m_ret_norm_w, m_ret_decay_fwd, m_ret_decay_bwd, zero)
    sv = _pack_small(v_norm_mix_w, v_norm_ffn_w, v_norm_final_w, v_ret_norm_w, v_ret_decay_fwd, v_ret_decay_bwd, zero)
    shapes = [(1, D), (1, D), (D,), (1, DA), (1, H), (1, H), ()]
    sg, sd, snm, snv = [_unpack_small(t, shapes) for t in _small_step(part, sw, sm, sv)]
    loss = sg[6]

    def ordered(small_set, k):
        b = [(u[k].T if a in (2, 3) else u[k])[None] for a, u in enumerate(upd)]
        return [small_set[0], b[0], small_set[4], small_set[5], small_set[3], b[1], small_set[1], b[2], b[3], b[4],
                small_set[2]]

    return (loss, grad_x[None], *ordered(sg, 0), *ordered(sd, 1), *ordered(snm, 2), *ordered(snv, 3))
```

```python
import functools
import math

import numpy as np
import jax
import jax.numpy as jnp
from jax import lax
from jax.experimental import pallas as pl
from jax.experimental.pallas import tpu as pltpu
from jax.experimental.pallas import tpu_sc as plsc

F32 = jnp.float32
BF16 = jnp.bfloat16
SDS = jax.ShapeDtypeStruct

HEAD_DIM = 128
EPS = 1e-6
RET_CHUNK = 128
DILATIONS = (1, 4, 16)
BAND = 64
Q_TILE = 128
K_TILE = Q_TILE + 2 * BAND
KV_PAD = BAND * 4
TILE_GROUP = 8
NEG = -1e30
N_DEV = 8
N_GROUPS = 7
ADAM_LR, ADAM_B1, ADAM_B2, ADAM_EPS, ADAM_WD, ADAM_STEP = 0.001, 0.9, 0.999, 1e-08, 0.01, 10
VMEM_LIMIT = 56 * 1024 * 1024
MESH = pl.DeviceIdType.MESH
ANY = pl.BlockSpec(memory_space=pl.ANY)


def _cp(n_grid):
    return pltpu.CompilerParams(dimension_semantics=("arbitrary",) * n_grid, vmem_limit_bytes=VMEM_LIMIT)


def _sigmoid(x):
    return 1.0 / (1.0 + jnp.exp(-x))


def _rms_scale(h):
    return lax.rsqrt(jnp.mean(h * h, axis=-1, keepdims=True) + EPS)


def _rms_bwd(dn, h, w):
    r = _rms_scale(h)
    gw = dn * w
    dh = r * gw - h * (r * r * r) * jnp.mean(gw * h, axis=-1, keepdims=True)
    return dh, jnp.sum(dn * h * r, axis=0, keepdims=True)


def _dot(a, b, dims):
    return lax.dot_general(a.astype(BF16), b.astype(BF16), (dims, ((), ())), preferred_element_type=F32)


_NN = ((1,), (0,))
_NT = ((1,), (1,))
_TN = ((0,), (0,))


RESIDENT_ROWS = 256


def _resident(shape):
    return pl.BlockSpec(shape, lambda i: (0, 0), pipeline_mode=pl.Buffered(1))


def _blocked_matmul(a_ref, w_ref):
    nblk, _, fb = a_ref.shape
    out = None
    for j in range(nblk):
        part = jnp.dot(a_ref[j], w_ref[pl.ds(j * fb, fb), :], preferred_element_type=F32)
        out = part if out is None else out + part
    return out


def _norm_fwd(x, w_norm):
    S, D = x.shape
    tm = min(1024, S)

    def body(x_ref, wn_ref, n_ref):
        xf = x_ref[...]
        n_ref[...] = (xf * _rms_scale(xf) * wn_ref[...]).astype(BF16)

    row = pl.BlockSpec((tm, D), lambda i: (i, 0))
    return pl.pallas_call(body, grid=(S // tm,), name="norm_fwd", in_specs=[row, pl.BlockSpec((1, D), lambda i: (0, 0))],
                          out_specs=row, out_shape=SDS((S, D), BF16), compiler_params=_cp(1))(x, w_norm)


def _proj_part(n1, w_slots, slots, blocks, proj, n_blocks, name):
    S, D = n1.shape
    NB = w_slots.shape[2]
    tm = min(1024, S)

    def body(slots_ref, blocks_ref, n_ref, w_ref, *rest):
        rest[-1][...] = jnp.dot(n_ref[...], w_ref[0], preferred_element_type=F32)

    out_spec = pl.BlockSpec((tm, NB), lambda i, j, slots, blocks: (i, blocks[j]))
    in_specs = [pl.BlockSpec((tm, D), lambda i, j, slots, blocks: (i, 0)),
                pl.BlockSpec((1, D, NB), lambda i, j, slots, blocks: (slots[j], 0, 0))]
    args = [n1, w_slots]
    if proj is not None:
        in_specs.append(ANY)
        args.append(proj)
    return pl.pallas_call(
        body, name=name, out_shape=SDS((S, n_blocks * NB), F32),
        grid_spec=pltpu.PrefetchScalarGridSpec(num_scalar_prefetch=2, grid=(S // tm, slots.shape[0]), in_specs=in_specs,
                                               out_specs=out_spec),
        input_output_aliases={} if proj is None else {4: 0},
        compiler_params=_cp(2))(slots, blocks, *args)


def _out_fwd(x, attn, ret, w_out, w_norm):
    S, D = x.shape
    DA = attn.shape[1]
    tm = min(256, S)

    def body(x_ref, a_ref, r_ref, w_ref, wn_ref, h_ref, mix_ref, n_ref):
        a = a_ref[...].astype(BF16)
        r = r_ref[...].astype(BF16)
        mix_ref[:, :DA] = a
        mix_ref[:, DA:] = r
        h = x_ref[...] + jnp.dot(a, w_ref[:DA, :], preferred_element_type=F32) \
            + jnp.dot(r, w_ref[DA:, :], preferred_element_type=F32)
        h_ref[...] = h
        n_ref[...] = (h * _rms_scale(h) * wn_ref[...]).astype(BF16)

    row = lambda w: pl.BlockSpec((tm, w), lambda i: (i, 0))
    return pl.pallas_call(
        body, grid=(S // tm,), name="out_fwd",
        in_specs=[row(D), row(DA), row(D - DA), pl.BlockSpec((D, D), lambda i: (0, 0)),
                  pl.BlockSpec((1, D), lambda i: (0, 0))],
        out_specs=[row(D), row(D), row(D)],
        out_shape=[SDS((S, D), F32), SDS((S, D), BF16), SDS((S, D), BF16)],
        compiler_params=_cp(1))(x, attn, ret, w_out, w_norm)


def _ffn_up(n2, wg, wu):
    S, D = n2.shape
    nblk, FB, _ = wg.shape
    tm = min(1024, S)

    def body(n_ref, wg_ref, wu_ref, g_ref, u_ref, a_ref):
        n = n_ref[...]
        g = _dot(n, wg_ref[0], _NT)
        u = _dot(n, wu_ref[0], _NT)
        g_ref[0] = g.astype(BF16)
        u_ref[0] = u.astype(BF16)
        a_ref[0] = (g * _sigmoid(g) * u).astype(BF16)

    wspec = pl.BlockSpec((1, FB, D), lambda j, i: (j, 0, 0))
    ospec = pl.BlockSpec((1, tm, FB), lambda j, i: (j, i, 0))
    return pl.pallas_call(
        body, grid=(nblk, S // tm), name="ffn_up",
        in_specs=[pl.BlockSpec((tm, D), lambda j, i: (i, 0)), wspec, wspec],
        out_specs=[ospec, ospec, ospec],
        out_shape=[SDS((nblk, S, FB), BF16)] * 3,
        compiler_params=_cp(2))(n2, wg, wu)


def _ffn_down_loss(act, wd, h1, target, w_norm):
    nblk, S, FB = act.shape
    D = h1.shape[1]
    tm = min(RESIDENT_ROWS, S)

    def body(a_ref, wd_ref, h_ref, t_ref, wn_ref, dh_ref, dhb_ref, loss_ref, dw_ref):
        @pl.when(pl.program_id(0) == 0)
        def _():
            dw_ref[...] = jnp.zeros_like(dw_ref)

        h = h_ref[...] + _blocked_matmul(a_ref, wd_ref)
        w = wn_ref[...]
        err = h * _rms_scale(h) * w - t_ref[...]
        loss_ref[...] = jnp.full(loss_ref.shape, 0.5 * jnp.sum(err * err) / D, F32)
        dh, dw = _rms_bwd(err * (1.0 / D), h, w)
        dh_ref[...] = dh
        dhb_ref[...] = dh.astype(BF16)
        dw_ref[...] += dw

    row = pl.BlockSpec((tm, D), lambda i: (i, 0))
    vec = pl.BlockSpec((1, D), lambda i: (0, 0))
    return pl.pallas_call(
        body, grid=(S // tm,), name="ffn_down_loss",
        in_specs=[pl.BlockSpec((nblk, tm, FB), lambda i: (0, i, 0)), _resident((nblk * FB, D)), row, row, vec],
        out_specs=[row, row, pl.BlockSpec((8, 128), lambda i: (i, 0)), vec],
        out_shape=[SDS((S, D), F32), SDS((S, D), BF16), SDS((S // tm * 8, 128), F32), SDS((1, D), F32)],
        compiler_params=_cp(1))(act, wd, h1, target, w_norm)


def _attn_bias():
    n_heads = 8
    slopes = np.exp2(-8.0 * np.arange(1, n_heads + 1, dtype=np.float32) / n_heads)
    dist = np.abs(np.arange(K_TILE)[None, :] - BAND - np.arange(Q_TILE)[:, None])
    out = np.empty((n_heads, len(DILATIONS), Q_TILE, K_TILE), np.float32)
    for h in range(n_heads):
        for p, d in enumerate(DILATIONS):
            out[h, p] = np.where(dist <= BAND, -slopes[h] * (d * dist).astype(np.float32), NEG)
    return jnp.asarray(out)


def _attn_tiles(S, d):
    L = S // d
    per_class = L // Q_TILE
    return L, per_class, d * per_class


def _tile_rows(t, d, per_class):
    r = t // per_class
    a = (t % per_class) * Q_TILE
    q_rows = pl.ds(r + d * a, Q_TILE, stride=d) if d > 1 else pl.ds(pl.multiple_of(a, Q_TILE), Q_TILE)
    k_rows = pl.ds(KV_PAD + r + d * (a - BAND), K_TILE, stride=d) if d > 1 else pl.ds(
        pl.multiple_of(KV_PAD + a - BAND, BAND), K_TILE)
    return a, q_rows, k_rows


def _to_quarters(dst, src, n, dst_off=0):
    for r in range(4):
        dst[pl.ds(dst_off + r * (n // 4), n // 4), :] = src[pl.ds(r, n // 4, stride=4), :]


def _quarter_tile_rows(t, S):
    L = S // 16
    per_class = L // Q_TILE
    blk, tt = t // (4 * per_class), t % (4 * per_class)
    r, a = tt // per_class, (tt % per_class) * Q_TILE
    q_rows = pl.ds(blk * (S // 4) + r + 4 * a, Q_TILE, stride=4)
    k_rows = pl.ds(KV_PAD + blk * (S // 4) + r + 4 * (a - BAND), K_TILE, stride=4)
    return a, q_rows, k_rows


def _lanes(x, width):
    return jnp.concatenate([x] * (width // HEAD_DIM), axis=-1)


_BNT = (((2,), (2,)), ((0,), (0,)))
_BNN = (((2,), (1,)), ((0,), (0,)))
_BTN = (((1,), (1,)), ((0,), (0,)))


def _bdot(a, b, dims):
    return lax.dot_general(a, b, dims, preferred_element_type=F32)


def _stacked(rows, loaders):
    return [jnp.stack([f(*r) for r in rows]) for f in loaders]


def _edge_mask(a, L):
    lk = lax.broadcasted_iota(jnp.int32, (1, K_TILE), 1) + (a - BAND)
    return jnp.where((lk >= 0) & (lk < L), 0.0, NEG).astype(F32)


def _fill_padded(dst, src, S):
    dst[pl.ds(0, KV_PAD), :] = jnp.zeros((KV_PAD, HEAD_DIM), F32)
    dst[pl.ds(KV_PAD + S, KV_PAD), :] = jnp.zeros((KV_PAD, HEAD_DIM), F32)
    dst[pl.ds(KV_PAD, S), :] = src[...]


def _head_specs(S, groups, n_heads):
    return [pl.BlockSpec((S, HEAD_DIM), functools.partial(lambda h, g: (0, g * n_heads + h), g=g)) for g in groups]


def _attn_fwd(proj, bias):
    S = proj.shape[0]
    H = proj.shape[1] // (N_GROUPS * HEAD_DIM)
    scale = HEAD_DIM ** -0.5

    def body(q_ref, k_ref, v_ref, b_ref, o_ref, lse_ref, kp, vp, m_run, l_run, q4, m3, l3, acc3):
        _fill_padded(kp, k_ref, S)
        _fill_padded(vp, v_ref, S)
        o_ref[...] = jnp.zeros_like(o_ref)
        m_run[...] = jnp.full(m_run.shape, NEG, F32)
        l_run[...] = jnp.zeros_like(l_run)
        for p, d in enumerate(DILATIONS[:2]):
            L, per_class, n_tiles = _attn_tiles(S, d)

            def tiles(t, carry, p=p, d=d, L=L, per_class=per_class, n_tiles=n_tiles):
                rows = [_tile_rows(t + u * (n_tiles // TILE_GROUP), d, per_class) for u in range(TILE_GROUP)]
                qs, ks, vs, m_old, l_old, o_old, edge = _stacked(rows, (
                    lambda a, qr, kr: q_ref[qr, :].astype(BF16), lambda a, qr, kr: kp[kr, :].astype(BF16),
                    lambda a, qr, kr: vp[kr, :].astype(BF16), lambda a, qr, kr: m_run[qr, :],
                    lambda a, qr, kr: l_run[qr, :], lambda a, qr, kr: o_ref[qr, :], lambda a, qr, kr: _edge_mask(a, L)))
                s = _bdot(qs, ks, _BNT) * scale + b_ref[0, p][None] + edge
                m_new = jnp.maximum(m_old, jnp.max(s, axis=-1, keepdims=True))
                pr = jnp.exp(s - _lanes(m_new, K_TILE)).astype(BF16)
                alpha = jnp.exp(m_old - m_new)
                l_new = alpha * l_old + _bdot(pr, jnp.ones((TILE_GROUP, K_TILE, HEAD_DIM), BF16), _BNN)
                o_new = alpha * o_old + _bdot(pr, vs, _BNN)
                for u, (_, qr, _) in enumerate(rows):
                    o_ref[qr, :] = o_new[u]
                    m_run[qr, :] = m_new[u]
                    l_run[qr, :] = l_new[u]
                return carry

            lax.fori_loop(0, n_tiles // TILE_GROUP, tiles, 0)

        _to_quarters(q4, q_ref, S)
        _to_quarters(kp, k_ref, S, KV_PAD)
        _to_quarters(vp, v_ref, S, KV_PAD)
        n_tiles = _attn_tiles(S, DILATIONS[2])[2]

        def tiles3(t, carry):
            rows = [_quarter_tile_rows(t + u * (n_tiles // TILE_GROUP), S) for u in range(TILE_GROUP)]
            qs, ks, vs, edge = _stacked(rows, (
                lambda a, qr, kr: q4[qr, :].astype(BF16), lambda a, qr, kr: kp[kr, :].astype(BF16),
                lambda a, qr, kr: vp[kr, :].astype(BF16), lambda a, qr, kr: _edge_mask(a, S // DILATIONS[2])))
            s = _bdot(qs, ks, _BNT) * scale + b_ref[0, 2][None] + edge
            m_new = jnp.broadcast_to(jnp.max(s, axis=-1, keepdims=True), (TILE_GROUP, Q_TILE, HEAD_DIM))
            pr = jnp.exp(s - _lanes(m_new, K_TILE)).astype(BF16)
            l_new = _bdot(pr, jnp.ones((TILE_GROUP, K_TILE, HEAD_DIM), BF16), _BNN)
            o_new = _bdot(pr, vs, _BNN)
            for u, (_, qr, _) in enumerate(rows):
                acc3[qr, :] = o_new[u]
                m3[qr, :] = m_new[u]
                l3[qr, :] = l_new[u]
            return carry

        lax.fori_loop(0, n_tiles // TILE_GROUP, tiles3, 0)
        for r in range(4):
            nat, qtr = pl.ds(r, S // 4, stride=4), pl.ds(r * (S // 4), S // 4)
            m_a, m_b = m_run[nat, :], m3[qtr, :]
            m = jnp.maximum(m_a, m_b)
            w_a, w_b = jnp.exp(m_a - m), jnp.exp(m_b - m)
            l = w_a * l_run[nat, :] + w_b * l3[qtr, :]
            o_ref[nat, :] = (w_a * o_ref[nat, :] + w_b * acc3[qtr, :]) / l
            lse_ref[nat, :] = m + jnp.log(l)

    hspec = pl.BlockSpec((S, HEAD_DIM), lambda h: (0, h))
    padded, plain = pltpu.VMEM((S + 2 * KV_PAD, HEAD_DIM), F32), pltpu.VMEM((S, HEAD_DIM), F32)
    return pl.pallas_call(
        body, grid=(H,), name="attn_fwd",
        in_specs=_head_specs(S, (0, 1, 2), H) + [
            pl.BlockSpec((1, len(DILATIONS), Q_TILE, K_TILE), lambda h: (h, 0, 0, 0))],
        out_specs=[hspec, hspec],
        out_shape=[SDS((S, H * HEAD_DIM), F32), SDS((S, H * HEAD_DIM), F32)],
        scratch_shapes=[padded, padded] + [plain] * 6,
        compiler_params=_cp(1))(proj, proj, proj, bias)


def _put_groups(stage, dproj, groups, n_heads, sems):
    h = pl.program_id(0)
    copies = [pltpu.make_async_copy(
        stage.at[i], dproj.at[:, pl.ds(pl.multiple_of((g * n_heads + h) * HEAD_DIM, HEAD_DIM), HEAD_DIM)], sems.at[i])
        for i, g in enumerate(groups)]
    for cp in copies:
        cp.start()
    for cp in copies:
        cp.wait()


def _attn_bwd(proj, out, lse, dmix, bias, dproj):
    S = proj.shape[0]
    H = proj.shape[1] // (N_GROUPS * HEAD_DIM)
    scale = HEAD_DIM ** -0.5
    assert S // DILATIONS[2] >= 2 * Q_TILE

    def body(q_ref, k_ref, v_ref, o_ref, lse_ref, do_ref, b_ref, dproj_in, dproj_out,
             kp, vp, dkp, dvp, dsum, q4, do4, lse4, dsum4, dq_ref, dk_ref, dv_ref, stage, sems):
        _fill_padded(kp, k_ref, S)
        _fill_padded(vp, v_ref, S)
        dkp[...] = jnp.zeros_like(dkp)
        dvp[...] = jnp.zeros_like(dvp)
        dq_ref[...] = jnp.zeros_like(dq_ref)
        dsum[...] = jnp.broadcast_to(jnp.sum(do_ref[...] * o_ref[...], axis=-1, keepdims=True), dsum.shape)

        def run(n_tiles, tile_rows, p, L, q_src, do_src, lse_src, dsum_src, dq_dst, dq_adds):
            def tiles(t, carry):
                rows = [tile_rows(t + u * (n_tiles // TILE_GROUP)) for u in range(TILE_GROUP)]
                qs, ks, vs, dos, lses, dsums, dk_old, dv_old, edge = _stacked(rows, (
                    lambda a, qr, kr: q_src[qr, :].astype(BF16), lambda a, qr, kr: kp[kr, :].astype(BF16),
                    lambda a, qr, kr: vp[kr, :].astype(BF16), lambda a, qr, kr: do_src[qr, :].astype(BF16),
                    lambda a, qr, kr: lse_src[qr, :], lambda a, qr, kr: dsum_src[qr, :],
                    lambda a, qr, kr: dkp[kr, :], lambda a, qr, kr: dvp[kr, :], lambda a, qr, kr: _edge_mask(a, L)))
                s = _bdot(qs, ks, _BNT) * scale + b_ref[0, p][None] + edge
                pr = jnp.exp(s - _lanes(lses, K_TILE))
                ds = (pr * (_bdot(dos, vs, _BNT) - _lanes(dsums, K_TILE)) * scale).astype(BF16)
                dq_new = _bdot(ds, ks, _BNN)
                if dq_adds:
                    dq_new = dq_new + jnp.stack([dq_dst[qr, :] for _, qr, _ in rows])
                dk_new = dk_old + _bdot(ds, qs, _BTN)
                dv_new = dv_old + _bdot(pr.astype(BF16), dos, _BTN)
                for u, (_, qr, kr) in enumerate(rows):
                    dq_dst[qr, :] = dq_new[u]
                    dkp[kr, :] = dk_new[u]
                    dvp[kr, :] = dv_new[u]
                return carry

            lax.fori_loop(0, n_tiles // TILE_GROUP, tiles, 0)

        for p, d in enumerate(DILATIONS[:2]):
            L, per_class, n_tiles = _attn_tiles(S, d)
            run(n_tiles, functools.partial(_tile_rows, d=d, per_class=per_class), p, L,
                q_ref, do_ref, lse_ref, dsum, dq_ref, True)
        dk_ref[...] = dkp[pl.ds(KV_PAD, S), :]
        dv_ref[...] = dvp[pl.ds(KV_PAD, S), :]

        for dst, src in ((q4, q_ref), (do4, do_ref), (lse4, lse_ref), (dsum4, dsum)):
            _to_quarters(dst, src, S)
        _to_quarters(kp, k_ref, S, KV_PAD)
        _to_quarters(vp, v_ref, S, KV_PAD)
        dkp[...] = jnp.zeros_like(dkp)
        dvp[...] = jnp.zeros_like(dvp)
        dq3 = dsum
        run(_attn_tiles(S, DILATIONS[2])[2], functools.partial(_quarter_tile_rows, S=S), 2, S // DILATIONS[2],
            q4, do4, lse4, dsum4, dq3, False)
        for r in range(4):
            nat, qtr = pl.ds(r, S // 4, stride=4), pl.ds(r * (S // 4), S // 4)
            pad_qtr = pl.ds(KV_PAD + r * (S // 4), S // 4)
            dq_ref[nat, :] = dq_ref[nat, :] + dq3[qtr, :]
            dk_ref[nat, :] = dk_ref[nat, :] + dkp[pad_qtr, :]
            dv_ref[nat, :] = dv_ref[nat, :] + dvp[pad_qtr, :]
        for i, acc in enumerate((dq_ref, dk_ref, dv_ref)):
            stage[i] = acc[...].astype(BF16)
        _put_groups(stage, dproj_out, (0, 1, 2), H, sems)

    hspec = pl.BlockSpec((S, HEAD_DIM), lambda h: (0, h))
    once = pl.BlockSpec((S, HEAD_DIM), lambda h: (0, h), pipeline_mode=pl.Buffered(1))
    padded, plain = pltpu.VMEM((S + 2 * KV_PAD, HEAD_DIM), F32), pltpu.VMEM((S, HEAD_DIM), F32)
    return pl.pallas_call(
        body, grid=(H,), name="attn_bwd",
        in_specs=_head_specs(S, (0, 1, 2), H) + [
            once, hspec, hspec, pl.BlockSpec((1, len(DILATIONS), Q_TILE, K_TILE), lambda h: (h, 0, 0, 0)), ANY],
        out_specs=ANY, out_shape=SDS(dproj.shape, dproj.dtype), input_output_aliases={7: 0},
        scratch_shapes=[padded] * 4 + [plain] * 8 + [pltpu.VMEM((3, S, HEAD_DIM), BF16), pltpu.SemaphoreType.DMA((3,))],
        compiler_params=_cp(1))(proj, proj, proj, out, lse, dmix, bias, dproj)


def _ret_consts(lg, forward):
    C = RET_CHUNK
    i = lax.broadcasted_iota(jnp.int32, (C, C), 0)
    j = lax.broadcasted_iota(jnp.int32, (C, C), 1)
    rel = (i - j) if forward else (j - i)
    inside = (rel >= 0) if forward else (rel > 0)
    relf = jnp.maximum(rel, 0).astype(F32)
    mask = jnp.where(inside, jnp.exp(lg * relf), 0.0)
    idx = lax.broadcasted_iota(jnp.int32, (C, 1), 0).astype(F32)
    q_exp = (idx + 1.0) if forward else (C - idx)
    k_exp = (C - 1.0 - idx) if forward else idx
    return mask, relf, jnp.exp(lg * q_exp), q_exp, jnp.exp(lg * k_exp), k_exp, jnp.exp(lg * C)


def _log_decay(dec_ref, h):
    return -jnp.exp(jnp.full((1, 1), dec_ref[0, h], F32))


FFN_BLOCK = 704
CHUNK_BATCH = 8


def _batch_rows(b):
    n = CHUNK_BATCH * RET_CHUNK
    return pl.ds(pl.multiple_of(b * n, n), n)


def _batch_chunks(b):
    return pl.ds(pl.multiple_of(b * CHUNK_BATCH, CHUNK_BATCH), CHUNK_BATCH)


def _chunks3(x):
    return x.reshape(CHUNK_BATCH, RET_CHUNK, HEAD_DIM)


def _ret_scan(buf, c_decs, nc, reverse):
    def step(n, carry):
        new = []
        for way, r in enumerate(carry):
            c = n if (way == 0) != reverse else nc - 1 - n
            term = buf[way, c]
            buf[way, c] = r
            new.append(r * c_decs[way] + term)
        return tuple(new)

    lax.fori_loop(0, nc, step, (jnp.zeros((HEAD_DIM, HEAD_DIM), F32),) * 2)


def _ret_fwd(proj, dec_f, dec_b, w_norm):
    S = proj.shape[0]
    H = proj.shape[1] // (N_GROUPS * HEAD_DIM)
    nc = S // RET_CHUNK
    scale = HEAD_DIM ** -0.5

    def body(df_ref, db_ref, q_ref, k_ref, v_ref, g_ref, w_ref, y_ref, o_ref, states):
        h = pl.program_id(0)
        consts = [_ret_consts(_log_decay(dref, h), fw) for fw, dref in ((True, df_ref), (False, db_ref))]

        def kv_step(b, carry):
            rows, batch = _batch_rows(b), _batch_chunks(b)
            k3 = _chunks3(k_ref[rows, :])
            v3 = _chunks3(v_ref[rows, :]).astype(BF16)
            for way in range(2):
                states[way, batch] = _bdot((k3 * consts[way][4]).astype(BF16), v3, _BTN)
            return carry

        lax.fori_loop(0, nc // CHUNK_BATCH, kv_step, 0)
        _ret_scan(states, [c[6] for c in consts], nc, False)

        def out_step(b, carry):
            rows, batch = _batch_rows(b), _batch_chunks(b)
            q3 = _chunks3(q_ref[rows, :] * scale)
            k3 = _chunks3(k_ref[rows, :]).astype(BF16)
            v3 = _chunks3(v_ref[rows, :]).astype(BF16)
            a0 = _bdot(q3.astype(BF16), k3, _BNT)
            o = None
            for way in range(2):
                mask, q_dec = consts[way][0], consts[way][2]
                part = _bdot((a0 * mask).astype(BF16), v3, _BNN) \
                    + _bdot((q3 * q_dec).astype(BF16), states[way, batch].astype(BF16), _BNN)
                o = part if o is None else o + part
            o_ref[rows, :] = o.reshape(CHUNK_BATCH * RET_CHUNK, HEAD_DIM)
            return carry

        lax.fori_loop(0, nc // CHUNK_BATCH, out_step, 0)
        o = o_ref[...]
        g = g_ref[...]
        y_ref[...] = o * _rms_scale(o) * w_ref[...] * (g * _sigmoid(g))

    hspec = pl.BlockSpec((S, HEAD_DIM), lambda h: (0, h))
    smem = pl.BlockSpec(memory_space=pltpu.SMEM)
    return pl.pallas_call(
        body, grid=(H,), name="ret_fwd",
        in_specs=[smem, smem] + _head_specs(S, (3, 4, 5, 6), H) + [pl.BlockSpec((1, HEAD_DIM), lambda h: (0, h))],
        out_specs=[hspec, hspec],
        out_shape=[SDS((S, H * HEAD_DIM), F32)] * 2,
        scratch_shapes=[pltpu.VMEM((2, nc, HEAD_DIM, HEAD_DIM), F32)],
        compiler_params=_cp(1))(dec_f, dec_b, proj, proj, proj, proj, w_norm)


def _ret_gate_bwd(proj, o_raw, dmix, w_norm, col0, dproj):
    S = proj.shape[0]
    H = proj.shape[1] // (N_GROUPS * HEAD_DIM)

    def body(g_ref, o_ref, dy_ref, w_ref, dproj_in, do_ref, dw_ref, dproj_out, dg_ref, sems):
        o = o_ref[...]
        g = g_ref[...]
        dy = dy_ref[...]
        w = w_ref[...]
        rr = _rms_scale(o)
        normed = o * rr
        sg = _sigmoid(g)
        silu = g * sg
        dw_ref[0] = jnp.broadcast_to(jnp.sum(dy * normed * silu, axis=0, keepdims=True), (8, HEAD_DIM))
        dg_ref[0] = (dy * normed * w * (sg * (1.0 + g * (1.0 - sg)))).astype(BF16)
        dnormed = dy * w * silu
        do_ref[...] = rr * dnormed - o * (rr * rr * rr) * jnp.mean(dnormed * o, axis=-1, keepdims=True)
        _put_groups(dg_ref, dproj_out, (6,), H, sems)

    hspec = pl.BlockSpec((S, HEAD_DIM), lambda h: (0, h))
    nh0 = col0 // HEAD_DIM
    return pl.pallas_call(
        body, grid=(H,), name="ret_gate_bwd",
        in_specs=_head_specs(S, (6,), H) + [hspec, pl.BlockSpec((S, HEAD_DIM), lambda h: (0, nh0 + h)),
                                            pl.BlockSpec((1, HEAD_DIM), lambda h: (0, h)), ANY],
        out_specs=[hspec, pl.BlockSpec((1, 8, HEAD_DIM), lambda h: (h, 0, 0)), ANY],
        out_shape=[SDS((S, H * HEAD_DIM), F32), SDS((H, 8, HEAD_DIM), F32), SDS(dproj.shape, dproj.dtype)],
        input_output_aliases={4: 2},
        scratch_shapes=[pltpu.VMEM((1, S, HEAD_DIM), BF16), pltpu.SemaphoreType.DMA((1,))],
        compiler_params=_cp(1))(proj, o_raw, dmix, w_norm, dproj)


def _ret_bwd(proj, d_out, dec_f, dec_b, dproj):
    S = proj.shape[0]
    H = proj.shape[1] // (N_GROUPS * HEAD_DIM)
    C = RET_CHUNK
    nc = S // C
    scale = HEAD_DIM ** -0.5

    def body(df_ref, db_ref, q_ref, k_ref, v_ref, do, dproj_in, small_ref, dproj_out, states, d_states, stage, sems):
        h = pl.program_id(0)
        lgs = [_log_decay(df_ref, h), _log_decay(db_ref, h)]
        consts = [_ret_consts(lg, fw) for lg, fw in zip(lgs, (True, False))]

        def prep_step(b, carry):
            rows, batch = _batch_rows(b), _batch_chunks(b)
            q3 = _chunks3(q_ref[rows, :] * scale)
            k3 = _chunks3(k_ref[rows, :])
            v3 = _chunks3(v_ref[rows, :]).astype(BF16)
            do3 = _chunks3(do[rows, :]).astype(BF16)
            for way in range(2):
                states[way, batch] = _bdot((k3 * consts[way][4]).astype(BF16), v3, _BTN)
                d_states[way, batch] = _bdot((q3 * consts[way][2]).astype(BF16), do3, _BTN)
            return carry

        lax.fori_loop(0, nc // CHUNK_BATCH, prep_step, 0)
        c_decs = [c[6] for c in consts]
        _ret_scan(states, c_decs, nc, False)
        _ret_scan(d_states, c_decs, nc, True)

        def main_step(b, dlams):
            rows, batch = _batch_rows(b), _batch_chunks(b)
            q3 = _chunks3(q_ref[rows, :] * scale)
            k3 = _chunks3(k_ref[rows, :])
            q3b, k3b = q3.astype(BF16), k3.astype(BF16)
            v3b = _chunks3(v_ref[rows, :]).astype(BF16)
            do3b = _chunks3(do[rows, :]).astype(BF16)
            a0 = _bdot(q3b, k3b, _BNT)
            pv = _bdot(do3b, v3b, _BNT)
            dq = dk = dv = None
            new_dlams = []
            for way in range(2):
                mask, relf, q_dec, q_exp, k_dec, k_exp, c_dec = consts[way]
                state, d_state = states[way, batch], d_states[way, batch]
                dp = pv * mask
                dpb = dp.astype(BF16)
                gq = _bdot(do3b, state.astype(BF16), _BNT)
                gk = _bdot(v3b, d_state.astype(BF16), _BNT)
                parts = (_bdot(dpb, k3b, _BNN) + q_dec * gq, _bdot(dpb, q3b, _BTN) + k_dec * gk,
                         _bdot((a0 * mask).astype(BF16), do3b, _BTN)
                         + _bdot((k3 * k_dec).astype(BF16), d_state.astype(BF16), _BNN))
                dq, dk, dv = parts if dq is None else (dq + parts[0], dk + parts[1], dv + parts[2])
                total = lambda x: jnp.sum(jnp.sum(x, axis=0), axis=0, keepdims=True)
                new_dlams.append(dlams[way] + total(relf * a0 * dp)
                                 + total(q_exp * q_dec * q3 * gq + k_exp * k_dec * k3 * gk)
                                 + (C * c_dec) * total(state * d_state))
            flat = lambda x: x.reshape(CHUNK_BATCH * C, HEAD_DIM)
            stage[0, rows, :] = (flat(dq) * scale).astype(BF16)
            stage[1, rows, :] = flat(dk).astype(BF16)
            stage[2, rows, :] = flat(dv).astype(BF16)
            return tuple(new_dlams)

        dlams = lax.fori_loop(0, nc // CHUNK_BATCH, main_step, (jnp.zeros((1, HEAD_DIM), F32),) * 2)
        for row, (dlam, lg) in enumerate(zip(dlams, lgs)):
            small_ref[0, pl.ds(row, 1), :] = jnp.broadcast_to(jnp.sum(dlam, axis=-1, keepdims=True) * lg, (1, HEAD_DIM))
        small_ref[0, pl.ds(2, 6), :] = jnp.zeros((6, HEAD_DIM), F32)
        _put_groups(stage, dproj_out, (3, 4, 5), H, sems)

    hspec = pl.BlockSpec((S, HEAD_DIM), lambda h: (0, h))
    smem = pl.BlockSpec(memory_space=pltpu.SMEM)
    return pl.pallas_call(
        body, grid=(H,), name="ret_bwd",
        in_specs=[smem, smem] + _head_specs(S, (3, 4, 5), H) + [hspec, ANY],
        out_specs=[pl.BlockSpec((1, 8, HEAD_DIM), lambda h: (h, 0, 0)), ANY],
        out_shape=[SDS((H, 8, HEAD_DIM), F32), SDS(dproj.shape, dproj.dtype)], input_output_aliases={6: 1},
        scratch_shapes=[pltpu.VMEM((2, nc, HEAD_DIM, HEAD_DIM), F32), pltpu.VMEM((2, nc, HEAD_DIM, HEAD_DIM), F32),
                        pltpu.VMEM((3, S, HEAD_DIM), BF16), pltpu.SemaphoreType.DMA((3,))],
        compiler_params=_cp(1))(dec_f, dec_b, proj, proj, proj, d_out, dproj)


def _ffn_bwd_act(dh2, wd, g, u):
    S, D = dh2.shape
    nblk, _, FB = g.shape
    tm = min(1024, S)

    def body(dh_ref, wd_ref, g_ref, u_ref, dg_ref, du_ref):
        dact = _dot(dh_ref[...], wd_ref[...], _NT)
        gg = g_ref[0].astype(F32)
        sg = _sigmoid(gg)
        dg_ref[0] = (dact * u_ref[0].astype(F32) * (sg * (1.0 + gg * (1.0 - sg)))).astype(BF16)
        du_ref[0] = (dact * (gg * sg)).astype(BF16)

    blk = pl.BlockSpec((1, tm, FB), lambda j, i: (j, i, 0))
    return pl.pallas_call(
        body, grid=(nblk, S // tm), name="ffn_bwd_act",
        in_specs=[pl.BlockSpec((tm, D), lambda j, i: (i, 0)), pl.BlockSpec((FB, D), lambda j, i: (j, 0)), blk, blk],
        out_specs=[blk, blk], out_shape=[SDS((nblk, S, FB), BF16)] * 2,
        compiler_params=_cp(2))(dh2, wd, g, u)


def _ffn_bwd_in(dg, du, wg, wu, h1, dh2, w_norm):
    nblk, S, FB = dg.shape
    D = h1.shape[1]
    tm = min(RESIDENT_ROWS, S)
    blk = pl.BlockSpec((nblk, tm, FB), lambda i: (0, i, 0))
    row = pl.BlockSpec((tm, D), lambda i: (i, 0))
    vec = pl.BlockSpec((1, D), lambda i: (0, 0))

    def gate_body(dg_ref, wg_ref, part_ref):
        part_ref[...] = _blocked_matmul(dg_ref, wg_ref)

    part = pl.pallas_call(
        gate_body, grid=(S // tm,), name="ffn_bwd_in_gate", in_specs=[blk, _resident((nblk * FB, D))],
        out_specs=row, out_shape=SDS((S, D), F32), compiler_params=_cp(1))(dg, wg.reshape(nblk * FB, D))

    def body(du_ref, wu_ref, part_ref, h_ref, dh2_ref, wn_ref, dh_ref, dhb_ref, dw_ref):
        @pl.when(pl.program_id(0) == 0)
        def _():
            dw_ref[...] = jnp.zeros_like(dw_ref)

        dh, dw = _rms_bwd(part_ref[...] + _blocked_matmul(du_ref, wu_ref), h_ref[...], wn_ref[...])
        dh = dh2_ref[...] + dh
        dh_ref[...] = dh
        dhb_ref[...] = dh.astype(BF16)
        dw_ref[...] += dw

    return pl.pallas_call(
        body, grid=(S // tm,), name="ffn_bwd_in",
        in_specs=[blk, _resident((nblk * FB, D)), row, row, row, vec],
        out_specs=[row, row, vec], out_shape=[SDS((S, D), F32), SDS((S, D), BF16), SDS((1, D), F32)],
        compiler_params=_cp(1))(du, wu.reshape(nblk * FB, D), part, h1, dh2, w_norm)


def _dmix(dh1, w_out):
    S, D = dh1.shape
    tm = min(512, S)

    def body(dh_ref, w_ref, o_ref):
        o_ref[...] = _dot(dh_ref[...], w_ref[...], _NT)

    row = pl.BlockSpec((tm, D), lambda i: (i, 0))
    return pl.pallas_call(
        body, grid=(S // tm,), name="dmix", in_specs=[row, pl.BlockSpec((D, D), lambda i: (0, 0))],
        out_specs=row, out_shape=SDS((S, D), F32), compiler_params=_cp(1))(dh1, w_out)


def _in_bwd(dproj, w_blk, x, dh1, w_norm):
    S, D = x.shape
    nblk, _, NB = w_blk.shape
    tm = min(RESIDENT_ROWS, S)

    def body(dp_ref, w_ref, x_ref, dh1_ref, wn_ref, dx_ref, dw_ref):
        @pl.when(pl.program_id(0) == 0)
        def _():
            dw_ref[...] = jnp.zeros_like(dw_ref)

        dn = None
        for j in range(nblk):
            part = _dot(dp_ref[:, pl.ds(j * NB, NB)], w_ref[j], _NT)
            dn = part if dn is None else dn + part
        dh, dw = _rms_bwd(dn, x_ref[...], wn_ref[...])
        dx_ref[...] = dh1_ref[...] + dh
        dw_ref[...] += dw

    row = pl.BlockSpec((tm, D), lambda i: (i, 0))
    vec = pl.BlockSpec((1, D), lambda i: (0, 0))
    return pl.pallas_call(
        body, grid=(S // tm,), name="in_bwd",
        in_specs=[pl.BlockSpec((tm, nblk * NB), lambda i: (i, 0)),
                  pl.BlockSpec((nblk, D, NB), lambda i: (0, 0, 0), pipeline_mode=pl.Buffered(1)), row, row, vec],
        out_specs=[row, vec], out_shape=[SDS((S, D), F32), SDS((1, D), F32)],
        compiler_params=_cp(1))(dproj, w_blk, x, dh1, w_norm)


def _wgrad(a, b, a_spec, b_spec, o_spec, o_shape, grid, name):
    nk = grid[-1]

    def ld(ref):
        return ref[0] if len(ref.shape) == 3 else ref[...]

    def body(a_ref, b_ref, o_ref, acc):
        k = pl.program_id(len(grid) - 1)

        @pl.when(k == 0)
        def _():
            acc[...] = jnp.zeros_like(acc)

        acc[...] += _dot(ld(a_ref), ld(b_ref), _TN)

        @pl.when(k == nk - 1)
        def _():
            if len(o_ref.shape) == 3:
                o_ref[0] = acc[...].astype(o_ref.dtype)
            else:
                o_ref[...] = acc[...].astype(o_ref.dtype)

    return pl.pallas_call(
        body, grid=grid, name=name, in_specs=[a_spec, b_spec], out_specs=o_spec, out_shape=SDS(o_shape, BF16),
        scratch_shapes=[pltpu.VMEM(o_spec.block_shape[-2:], F32)], compiler_params=_cp(len(grid)))(a, b)


def _peer(k):
    x, y, c = lax.axis_index("x"), lax.axis_index("y"), lax.axis_index("c")
    px = 1 - x if k & 4 else x
    py = 1 - y if k & 2 else y
    pc = 1 - c if k & 1 else c
    return (px, py, pc), 4 * px + 2 * py + pc


def _exchange_copies(srcs, lands, send_sems, recv_sems, which, gather):
    _, me = _peer(0)
    pairs = []
    for pos, a in enumerate(which):
        for k in range(1, N_DEV):
            dev, idx = _peer(k)
            sem = pos * (N_DEV - 1) + k - 1
            src = srcs[a] if gather else srcs[a].at[idx]
            mk = functools.partial(pltpu.make_async_remote_copy, src_ref=src, send_sem=send_sems.at[sem],
                                   recv_sem=recv_sems.at[sem], device_id=dev, device_id_type=MESH)
            pairs.append((mk(dst_ref=lands[a].at[me]), mk(dst_ref=lands[a].at[idx])))
    return pairs


def _sequencer_kernel(name, collective_id, n_remote, n_local):
    return pl.kernel(mesh=plsc.ScalarSubcoreMesh(axis_name="sequencer", num_cores=1), name=name,
                     scratch_types=(pltpu.SemaphoreType.DMA((n_remote,)), pltpu.SemaphoreType.DMA((n_remote,)),
                                    pltpu.SemaphoreType.DMA((n_local,))),
                     compiler_params=pltpu.CompilerParams(collective_id=collective_id))


def _handshake(ks):
    barrier = pltpu.get_barrier_semaphore()
    for k in ks:
        pl.semaphore_signal(barrier, inc=1, device_id=_peer(k)[0], device_id_type=MESH)
    pl.semaphore_wait(barrier, len(ks))


def _sequencer_scatter(arrays, name, collective_id):
    n = len(arrays)
    hbm = pltpu.MemorySpace.HBM
    srcs = [jax.new_ref(a, memory_space=hbm) for a in arrays]
    lands = [jax.empty_ref(SDS(a.shape, a.dtype), memory_space=hbm) for a in arrays]

    @_sequencer_kernel(name, collective_id, n * (N_DEV - 1), n)
    def launch(send_sems, recv_sems, local_sems):
        _handshake(range(1, N_DEV))
        _, me = _peer(0)
        local = [pltpu.make_async_copy(srcs[a].at[me], lands[a].at[me], local_sems.at[a]) for a in range(n)]
        pairs = _exchange_copies(srcs, lands, send_sems, recv_sems, range(n), False)
        for out, _ in pairs:
            out.start()
        for cp in local:
            cp.start()
        for out, arrival in pairs:
            out.wait_send()
            arrival.wait_recv()
        for cp in local:
            cp.wait()

    launch()
    return [r[...] for r in lands]


SIBLING = 1
OTHER_CHIPS = (2, 4, 6)


def _sequencer_gather(arrays, name, collective_id):
    n = len(arrays)
    hbm = pltpu.MemorySpace.HBM
    srcs = [jax.new_ref(a, memory_space=hbm) for a in arrays]
    lands = [jax.empty_ref(SDS((N_DEV,) + a.shape, a.dtype), memory_space=hbm) for a in arrays]

    @_sequencer_kernel(name, collective_id, n * (N_DEV - 1), n)
    def launch(send_sems, recv_sems, local_sems):
        _handshake((SIBLING,) + OTHER_CHIPS)
        _, me = _peer(0)
        sibling, _ = _peer(SIBLING)

        def copy(a, k, src, block, to):
            sem = a * (N_DEV - 1) + k - 1
            return pltpu.make_async_remote_copy(src_ref=src, dst_ref=lands[a].at[block], send_sem=send_sems.at[sem],
                                                recv_sem=recv_sems.at[sem], device_id=to, device_id_type=MESH)

        local = [pltpu.make_async_copy(srcs[a], lands[a].at[me], local_sems.at[a]) for a in range(n)]
        first = [copy(a, k, srcs[a], me, _peer(k)[0]) for a in range(n) for k in OTHER_CHIPS + (SIBLING,)]
        for cp in first + local:
            cp.start()
        passed = []
        for a in range(n):
            for k in OTHER_CHIPS:
                _, block = _peer(k)
                copy(a, k, srcs[a], block, sibling).wait_recv()
                passed.append(copy(a, k ^ SIBLING, lands[a].at[block], block, sibling))
                passed[-1].start()
        for a in range(n):
            for k in (SIBLING,) + tuple(k ^ SIBLING for k in OTHER_CHIPS):
                copy(a, k, srcs[a], _peer(k)[1], sibling).wait_recv()
        for cp in first + passed:
            cp.wait_send()
        for cp in local:
            cp.wait()

    launch()
    return [r[...] for r in lands]


def _sequencer_gather_chips(array, name, collective_id, chips):
    hbm = pltpu.MemorySpace.HBM
    src = jax.new_ref(array, memory_space=hbm)
    land = jax.empty_ref(SDS((2 * len(chips),) + array.shape, array.dtype), memory_space=hbm)

    @_sequencer_kernel(name, collective_id, 2 * len(chips), 1)
    def launch(send_sems, recv_sems, local_sems):
        _handshake((SIBLING,) + tuple(k for k in chips if k))
        c = lax.axis_index("c")
        sibling, _ = _peer(SIBLING)

        def copy(sem, src_ref, slot, to):
            return pltpu.make_async_remote_copy(src_ref=src_ref, dst_ref=land.at[slot], send_sem=send_sems.at[sem],
                                                recv_sem=recv_sems.at[sem], device_id=to, device_id_type=MESH)

        started = []
        for pos, k in enumerate(chips):
            started.append(copy(2 * pos, src, 2 * pos + c, _peer(k)[0] if k else sibling))
            started[-1].start()
        for pos, k in enumerate(chips):
            if k:
                copy(2 * pos, src, 2 * pos + c, sibling).wait_recv()
                started.append(copy(2 * pos + 1, land.at[2 * pos + c], 2 * pos + c, sibling))
                started[-1].start()
        for pos, k in enumerate(chips):
            copy(2 * pos + 1 if k else 2 * pos, src, 2 * pos + 1 - c, sibling).wait_recv()
        for cp in started:
            cp.wait_send()

    launch()
    return land[...]


SMALL_ROWS = 64


def _small_step(part, w, m, v):
    def body(p_ref, w_ref, m_ref, v_ref, g_ref, d_ref, nm_ref, nv_ref, gath, send_sems, recv_sems):
        _, me = _peer(0)
        gath[me] = p_ref[...]
        copies = []
        for k in range(1, N_DEV):
            dev, idx = _peer(k)
            out = pltpu.make_async_remote_copy(src_ref=p_ref, dst_ref=gath.at[me], send_sem=send_sems.at[k - 1],
                                               recv_sem=recv_sems.at[k - 1], device_id=dev, device_id_type=MESH)
            out.start()
            arrival = pltpu.make_async_remote_copy(src_ref=p_ref, dst_ref=gath.at[idx], send_sem=send_sems.at[k - 1],
                                                   recv_sem=recv_sems.at[k - 1], device_id=dev, device_id_type=MESH)
            copies.append((out, arrival))
        for out, arrival in copies:
            out.wait_send()
            arrival.wait_recv()
        g = gath[0]
        for p in range(1, N_DEV):
            g = g + gath[p]
        g_ref[...] = g
        d_ref[...], nm_ref[...], nv_ref[...] = _adamw(w_ref[...], g, m_ref[...], v_ref[...])

    vm = pl.BlockSpec(memory_space=pltpu.VMEM)
    return pl.pallas_call(
        body, name="small_step", in_specs=[vm] * 4, out_specs=[vm] * 4,
        out_shape=[SDS((SMALL_ROWS, 128), F32)] * 4,
        scratch_shapes=[pltpu.VMEM((N_DEV, SMALL_ROWS, 128), F32), pltpu.SemaphoreType.DMA((N_DEV - 1,)),
                        pltpu.SemaphoreType.DMA((N_DEV - 1,))])(part, w, m, v)


def _adamw(w, g, m, v):
    m = ADAM_B1 * m + (1.0 - ADAM_B1) * g
    v = ADAM_B2 * v + (1.0 - ADAM_B2) * (g * g)
    m_hat = m / (1.0 - ADAM_B1 ** ADAM_STEP)
    v_hat = v / (1.0 - ADAM_B2 ** ADAM_STEP)
    delta = -ADAM_LR * (m_hat / (jnp.sqrt(v_hat) + ADAM_EPS) + ADAM_WD * w)
    return delta, m, v


def _adamw_block(parts, w, m, v, name):
    R, C = w.shape
    n_parts = len(parts)
    Rp = R // n_parts
    tr = next(t for t in (256, 128, 64, 32, 16, 8) if Rp % t == 0 and t * C <= 256 * 1024)
    per_part = Rp // tr

    def body(*refs):
        p_refs = refs[:n_parts]
        w_ref, m_ref, v_ref, g_ref, d_ref, nm_ref, nv_ref = refs[n_parts:]
        for k, p_ref in enumerate(p_refs):
            @pl.when(pl.program_id(0) // per_part == k)
            def _(p_ref=p_ref):
                g = p_ref[0].astype(F32)
                for p in range(1, N_DEV):
                    g = g + p_ref[p].astype(F32)
                g_ref[...] = g
                d_ref[...], nm_ref[...], nv_ref[...] = _adamw(w_ref[...], g, m_ref[...], v_ref[...])

    row = pl.BlockSpec((tr, C), lambda i: (i, 0))
    part_specs = [pl.BlockSpec((N_DEV, tr, C), functools.partial(
        lambda i, k: (0, jnp.clip(i - k * per_part, 0, per_part - 1), 0), k=k)) for k in range(n_parts)]
    return pl.pallas_call(
        body, grid=(R // tr,), name=name, in_specs=part_specs + [row, row, row],
        out_specs=[row] * 4, out_shape=[SDS((R, C), F32)] * 4, compiler_params=_cp(1))(*parts, w, m, v)


def _pack_small(mix, ffn, fin, retw, dec_f, dec_b, loss):
    flat = jnp.concatenate([mix.reshape(-1), ffn.reshape(-1), fin.reshape(-1), retw.reshape(-1), dec_f.reshape(-1),
                            dec_b.reshape(-1), loss.reshape(-1)])
    return jnp.pad(flat, (0, SMALL_ROWS * 128 - flat.shape[0])).reshape(SMALL_ROWS, 128)


def _unpack_small(packed, shapes):
    flat = packed.reshape(-1)
    out, at = [], 0
    for s in shapes:
        n = math.prod(s)
        out.append(flat[at:at + n].reshape(s))
        at += n
    return out


def kernel(x, norm_mix_w, w_in, ret_decay_fwd, ret_decay_bwd, ret_norm_w, w_out, norm_ffn_w, w_gate, w_up, w_down, norm_final_w, loss_target, m_norm_mix_w, m_w_in, m_ret_decay_fwd, m_ret_decay_bwd, m_ret_norm_w, m_w_out, m_norm_ffn_w, m_w_gate, m_w_up, m_w_down, m_norm_final_w, v_norm_mix_w, v_w_in, v_ret_decay_fwd, v_ret_decay_bwd, v_ret_norm_w, v_w_out, v_norm_ffn_w, v_w_gate, v_w_up, v_w_down, v_norm_final_w):
    x2 = x[0]
    tgt = loss_target[0]
    S, D = x2.shape
    H = ret_norm_w.shape[1] // HEAD_DIM
    DA = H * HEAD_DIM
    fin_w = norm_final_w.reshape(1, D)
    big = (w_in[0], w_out[0], w_gate[0].T, w_up[0].T, w_down[0])

    big_b = [w.astype(BF16) for w in big]
    stages = ((0,), (4, 2), (6,))
    wi_stages = [_sequencer_gather_chips(big_b[0], name, cid, ks)
                 for name, cid, ks in zip(("gather_in_own", "gather_in_near", "gather_in_far"), (0, 7, 8), stages)]
    wo, = _sequencer_gather(big_b[1:2], "gather_out", 1)
    wg, wu = _sequencer_gather(big_b[2:4], "gather_gate_up", 9)
    wd, = _sequencer_gather(big_b[4:], "gather_down", 5)
    wi, = _sequencer_gather(big_b[:1], "gather_in_ordered", 10)
    NB = big_b[0].shape[1]
    ax, ay = lax.axis_index("x"), lax.axis_index("y")
    chip_of = {k: 2 * (1 - ax if k & 4 else ax) + (1 - ay if k & 2 else ay) for k in (0, 2, 4, 6)}

    n1 = _norm_fwd(x2, norm_mix_w)
    ac = lax.axis_index("c")
    me = 2 * chip_of[0] + ac
    vec = lambda *v: jnp.stack([jnp.asarray(t, jnp.int32) for t in v])
    proj = _proj_part(n1, big_b[0][None], vec(0), vec(me), None, N_DEV, "proj_self")
    for ks, w_st, name in zip(stages, wi_stages, ("proj_sibling", "proj_near", "proj_far")):
        slots, blocks = [], []
        for pos, k in enumerate(ks):
            for core in ((1 - ac,) if k == 0 else (0, 1)):
                slots.append(2 * pos + core)
                blocks.append(2 * chip_of[k] + core)
        proj = _proj_part(n1, w_st, vec(*slots), vec(*blocks), proj, N_DEV, name)
    bias = _attn_bias()[:H]
    attn, lse = _attn_fwd(proj, bias)
    ret, o_raw = _ret_fwd(proj, ret_decay_fwd, ret_decay_bwd, ret_norm_w)
    wo_full = wo.reshape(D, D)
    d_ff = N_DEV * wd.shape[1]
    FB = FFN_BLOCK if d_ff % FFN_BLOCK == 0 else wd.shape[1]
    n_fb = d_ff // FB
    wg, wu = wg.reshape(n_fb, FB, D), wu.reshape(n_fb, FB, D)
    wd_full = wd.reshape(d_ff, D)
    h1, mixed, n2 = _out_fwd(x2, attn, ret, wo_full, norm_ffn_w)
    gate, up, act = _ffn_up(n2, wg, wu)
    dh2, dh2_b, loss_parts, g_fin = _ffn_down_loss(act, wd_full, h1, tgt, fin_w)

    dgate, dup = _ffn_bwd_act(dh2_b, wd_full, gate, up)
    tn = min(1024, D)
    ffn_specs = (pl.BlockSpec((1, S, FB), lambda j, n, k: (j, 0, 0)), pl.BlockSpec((S, tn), lambda j, n, k: (0, n)),
                 pl.BlockSpec((1, FB, tn), lambda j, n, k: (j, 0, n)), (n_fb, FB, D), (n_fb, D // tn, 1))
    per_dev = (N_DEV, d_ff // N_DEV, D)
    g_wd = _wgrad(act, dh2_b, *ffn_specs, "wgrad_down").reshape(per_dev)
    g_wg = _wgrad(dgate, n2, *ffn_specs, "wgrad_gate").reshape(per_dev)
    g_wu = _wgrad(dup, n2, *ffn_specs, "wgrad_up").reshape(per_dev)
    parts_f = _sequencer_scatter([g_wg, g_wu, g_wd], "scatter_ffn", 2)
    dh1, dh1_b, g_ffn = _ffn_bwd_in(dgate, dup, wg, wu, h1, dh2, norm_ffn_w)
    dmix = _dmix(dh1_b, wo_full)
    tmw = min(512, D)
    tk = min(2048, S)
    g_wo = _wgrad(mixed, dh1_b, pl.BlockSpec((tk, tmw), lambda m, k: (k, m)), pl.BlockSpec((tk, D), lambda m, k: (k, 0)),
                  pl.BlockSpec((tmw, D), lambda m, k: (m, 0)), (D, D), (D // tmw, S // tk), "wgrad_out")
    parts_o = _sequencer_scatter([g_wo.reshape(N_DEV, D // N_DEV, D)], "scatter_out", 3)
    d_ret, small_w, dproj = _ret_gate_bwd(proj, o_raw, dmix, ret_norm_w, DA, lax.empty(proj.shape, BF16))
    small, dproj = _ret_bwd(proj, d_ret, ret_decay_fwd, ret_decay_bwd, dproj)
    dproj = _attn_bwd(proj, attn, lse, dmix, bias, dproj)
    half = D // tmw // 2
    parts_i = []
    for part, (name, cid) in enumerate((("in_lo", 4), ("in_hi", 6))):
        g_wi = _wgrad(n1, dproj, pl.BlockSpec((S, tmw), functools.partial(lambda j, m, k, off: (0, m + off), off=part * half)),
                      pl.BlockSpec((S, NB), lambda j, m, k: (0, j)), pl.BlockSpec((1, tmw, NB), lambda j, m, k: (j, m, 0)),
                      (N_DEV, D // 2, NB), (N_DEV, half, 1), "wgrad_" + name)
        parts_i += _sequencer_scatter([g_wi], "scatter_" + name, cid)
    grad_x, g_mix = _in_bwd(dproj, wi, x2, dh1, norm_mix_w)

    big_m = (m_w_in[0], m_w_out[0], m_w_gate[0].T, m_w_up[0].T, m_w_down[0])
    big_v = (v_w_in[0], v_w_out[0], v_w_gate[0].T, v_w_up[0].T, v_w_down[0])
    names = ("adamw_in", "adamw_out", "adamw_gate", "adamw_up", "adamw_down")
    upd = [None] * 5
    for a, p in zip((2, 3, 4, 1, 0), [[t] for t in parts_f + parts_o] + [parts_i]):
        upd[a] = _adamw_block(p, big[a], big_m[a], big_v[a], names[a])

    g_dec_f = small[:, 0, 0].reshape(1, H)
    g_dec_b = small[:, 1, 0].reshape(1, H)
    g_retw = small_w[:, 0, :].reshape(1, DA)
    loss_local = jnp.sum(loss_parts[::8, 0])
    zero = jnp.zeros((1,), F32)
    part = _pack_small(g_mix, g_ffn, g_fin, g_retw, g_dec_f, g_dec_b, loss_local)
    sw = _pack_small(norm_mix_w, norm_ffn_w, norm_final_w, ret_norm_w, ret_decay_fwd, ret_decay_bwd, zero)
    sm = _pack_small(m_norm_mix_w, m_norm_ffn_w, m_norm_final_w, m_ret_norm_w, m_ret_decay_fwd, m_ret_decay_bwd, zero)
    sv = _pack_small(v_norm_mix_w, v_norm_ffn_w, v_norm_final_w, v_ret_norm_w, v_ret_decay_fwd, v_ret_decay_bwd, zero)
    shapes = [(1, D), (1, D), (D,), (1, DA), (1, H), (1, H), ()]
    sg, sd, snm, snv = [_unpack_small(t, shapes) for t in _small_step(part, sw, sm, sv)]
    loss = sg[6]

    def ordered(small_set, k):
        b = [(u[k].T if a in (2, 3) else u[k])[None] for a, u in enumerate(upd)]
        return [small_set[0], b[0], small_set[4], small_set[5], small_set[3], b[1], small_set[1], b[2], b[3], b[4],
                small_set[2]]

    return (loss, grad_x[None], *ordered(sg, 0), *ordered(sd, 1), *ordered(snm, 2), *ordered(snv, 3))
```

```python
import functools
import math

import numpy as np
import jax
import jax.numpy as jnp
from jax import lax
from jax.experimental import pallas as pl
from jax.experimental.pallas import tpu as pltpu
from jax.experimental.pallas import tpu_sc as plsc

F32 = jnp.float32
BF16 = jnp.bfloat16
SDS = jax.ShapeDtypeStruct

HEAD_DIM = 128
EPS = 1e-6
RET_CHUNK = 128
DILATIONS = (1, 4, 16)
BAND = 64
Q_TILE = 128
K_TILE = Q_TILE + 2 * BAND
KV_PAD = BAND * 4
TILE_GROUP = 8
BWD_TILE_GROUP = 4
NEG = -1e30
N_DEV = 8
N_GROUPS = 7
ADAM_LR, ADAM_B1, ADAM_B2, ADAM_EPS, ADAM_WD, ADAM_STEP = 0.001, 0.9, 0.999, 1e-08, 0.01, 10
VMEM_LIMIT = 56 * 1024 * 1024
MESH = pl.DeviceIdType.MESH
ANY = pl.BlockSpec(memory_space=pl.ANY)


def _cp(n_grid):
    return pltpu.CompilerParams(dimension_semantics=("arbitrary",) * n_grid, vmem_limit_bytes=VMEM_LIMIT)


def _sigmoid(x):
    return 1.0 / (1.0 + jnp.exp(-x))


def _rms_scale(h):
    return lax.rsqrt(jnp.mean(h * h, axis=-1, keepdims=True) + EPS)


def _rms_bwd(dn, h, w):
    r = _rms_scale(h)
    gw = dn * w
    dh = r * gw - h * (r * r * r) * jnp.mean(gw * h, axis=-1, keepdims=True)
    return dh, jnp.sum(dn * h * r, axis=0, keepdims=True)


def _dot(a, b, dims):
    return lax.dot_general(a.astype(BF16), b.astype(BF16), (dims, ((), ())), preferred_element_type=F32)


_NN = ((1,), (0,))
_NT = ((1,), (1,))
_TN = ((0,), (0,))


RESIDENT_ROWS = 256


def _resident(shape):
    return pl.BlockSpec(shape, lambda i: (0, 0), pipeline_mode=pl.Buffered(1))


def _blocked_matmul(a_ref, w_ref):
    nblk, _, fb = a_ref.shape
    out = None
    for j in range(nblk):
        part = jnp.dot(a_ref[j], w_ref[pl.ds(j * fb, fb), :], preferred_element_type=F32)
        out = part if out is None else out + part
    return out


def _norm_fwd(x, w_norm):
    S, D = x.shape
    tm = min(1024, S)

    def body(x_ref, wn_ref, n_ref):
        xf = x_ref[...]
        n_ref[...] = (xf * _rms_scale(xf) * wn_ref[...]).astype(BF16)

    row = pl.BlockSpec((tm, D), lambda i: (i, 0))
    return pl.pallas_call(body, grid=(S // tm,), name="norm_fwd", in_specs=[row, pl.BlockSpec((1, D), lambda i: (0, 0))],
                          out_specs=row, out_shape=SDS((S, D), BF16), compiler_params=_cp(1))(x, w_norm)


def _proj_part(n1, w_slots, slots, blocks, proj, n_blocks, name):
    S, D = n1.shape
    NB = w_slots.shape[2]
    tm = min(1024, S)

    def body(slots_ref, blocks_ref, n_ref, w_ref, *rest):
        rest[-1][...] = jnp.dot(n_ref[...], w_ref[0], preferred_element_type=F32)

    out_spec = pl.BlockSpec((tm, NB), lambda i, j, slots, blocks: (i, blocks[j]))
    in_specs = [pl.BlockSpec((tm, D), lambda i, j, slots, blocks: (i, 0)),
                pl.BlockSpec((1, D, NB), lambda i, j, slots, blocks: (slots[j], 0, 0))]
    args = [n1, w_slots]
    if proj is not None:
        in_specs.append(ANY)
        args.append(proj)
    return pl.pallas_call(
        body, name=name, out_shape=SDS((S, n_blocks * NB), F32),
        grid_spec=pltpu.PrefetchScalarGridSpec(num_scalar_prefetch=2, grid=(S // tm, slots.shape[0]), in_specs=in_specs,
                                               out_specs=out_spec),
        input_output_aliases={} if proj is None else {4: 0},
        compiler_params=_cp(2))(slots, blocks, *args)


def _out_fwd(x, attn, ret, w_out, w_norm):
    S, D = x.shape
    DA = attn.shape[1]
    tm = min(256, S)

    def body(x_ref, a_ref, r_ref, w_ref, wn_ref, h_ref, mix_ref, n_ref):
        a = a_ref[...].astype(BF16)
        r = r_ref[...].astype(BF16)
        mix_ref[:, :DA] = a
        mix_ref[:, DA:] = r
        h = x_ref[...] + jnp.dot(a, w_ref[:DA, :], preferred_element_type=F32) \
            + jnp.dot(r, w_ref[DA:, :], preferred_element_type=F32)
        h_ref[...] = h
        n_ref[...] = (h * _rms_scale(h) * wn_ref[...]).astype(BF16)

    row = lambda w: pl.BlockSpec((tm, w), lambda i: (i, 0))
    return pl.pallas_call(
        body, grid=(S // tm,), name="out_fwd",
        in_specs=[row(D), row(DA), row(D - DA), pl.BlockSpec((D, D), lambda i: (0, 0)),
                  pl.BlockSpec((1, D), lambda i: (0, 0))],
        out_specs=[row(D), row(D), row(D)],
        out_shape=[SDS((S, D), F32), SDS((S, D), BF16), SDS((S, D), BF16)],
        compiler_params=_cp(1))(x, attn, ret, w_out, w_norm)


def _ffn_up(n2, wg, wu):
    S, D = n2.shape
    nblk, FB, _ = wg.shape
    tm = min(1024, S)

    def body(n_ref, wg_ref, wu_ref, g_ref, u_ref, a_ref):
        n = n_ref[...]
        g = _dot(n, wg_ref[0], _NT)
        u = _dot(n, wu_ref[0], _NT)
        g_ref[0] = g.astype(BF16)
        u_ref[0] = u.astype(BF16)
        a_ref[0] = (g * _sigmoid(g) * u).astype(BF16)

    wspec = pl.BlockSpec((1, FB, D), lambda j, i: (j, 0, 0))
    ospec = pl.BlockSpec((1, tm, FB), lambda j, i: (j, i, 0))
    return pl.pallas_call(
        body, grid=(nblk, S // tm), name="ffn_up",
        in_specs=[pl.BlockSpec((tm, D), lambda j, i: (i, 0)), wspec, wspec],
        out_specs=[ospec, ospec, ospec],
        out_shape=[SDS((nblk, S, FB), BF16)] * 3,
        compiler_params=_cp(2))(n2, wg, wu)


def _ffn_down_loss(act, wd, h1, target, w_norm):
    nblk, S, FB = act.shape
    D = h1.shape[1]
    tm = min(RESIDENT_ROWS, S)

    def body(a_ref, wd_ref, h_ref, t_ref, wn_ref, dh_ref, dhb_ref, loss_ref, dw_ref):
        @pl.when(pl.program_id(0) == 0)
        def _():
            dw_ref[...] = jnp.zeros_like(dw_ref)

        h = h_ref[...] + _blocked_matmul(a_ref, wd_ref)
        w = wn_ref[...]
        err = h * _rms_scale(h) * w - t_ref[...]
        loss_ref[...] = jnp.full(loss_ref.shape, 0.5 * jnp.sum(err * err) / D, F32)
        dh, dw = _rms_bwd(err * (1.0 / D), h, w)
        dh_ref[...] = dh
        dhb_ref[...] = dh.astype(BF16)
        dw_ref[...] += dw

    row = pl.BlockSpec((tm, D), lambda i: (i, 0))
    vec = pl.BlockSpec((1, D), lambda i: (0, 0))
    return pl.pallas_call(
        body, grid=(S // tm,), name="ffn_down_loss",
        in_specs=[pl.BlockSpec((nblk, tm, FB), lambda i: (0, i, 0)), _resident((nblk * FB, D)), row, row, vec],
        out_specs=[row, row, pl.BlockSpec((8, 128), lambda i: (i, 0)), vec],
        out_shape=[SDS((S, D), F32), SDS((S, D), BF16), SDS((S // tm * 8, 128), F32), SDS((1, D), F32)],
        compiler_params=_cp(1))(act, wd, h1, target, w_norm)


def _attn_bias():
    n_heads = 8
    slopes = np.exp2(-8.0 * np.arange(1, n_heads + 1, dtype=np.float32) / n_heads)
    dist = np.abs(np.arange(K_TILE)[None, :] - BAND - np.arange(Q_TILE)[:, None])
    out = np.empty((n_heads, len(DILATIONS), Q_TILE, K_TILE), np.float32)
    for h in range(n_heads):
        for p, d in enumerate(DILATIONS):
            out[h, p] = np.where(dist <= BAND, -slopes[h] * (d * dist).astype(np.float32), NEG)
    return jnp.asarray(out)


def _attn_tiles(S, d):
    L = S // d
    per_class = L // Q_TILE
    return L, per_class, d * per_class


def _tile_rows(t, d, per_class):
    r = t // per_class
    a = (t % per_class) * Q_TILE
    q_rows = pl.ds(r + d * a, Q_TILE, stride=d) if d > 1 else pl.ds(pl.multiple_of(a, Q_TILE), Q_TILE)
    k_rows = pl.ds(KV_PAD + r + d * (a - BAND), K_TILE, stride=d) if d > 1 else pl.ds(
        pl.multiple_of(KV_PAD + a - BAND, BAND), K_TILE)
    return a, q_rows, k_rows


def _to_quarters(dst, src, n, dst_off=0):
    for r in range(4):
        dst[pl.ds(dst_off + r * (n // 4), n // 4), :] = src[pl.ds(r, n // 4, stride=4), :]


def _quarter_tile_rows(t, S):
    L = S // 16
    per_class = L // Q_TILE
    blk, tt = t // (4 * per_class), t % (4 * per_class)
    r, a = tt // per_class, (tt % per_class) * Q_TILE
    q_rows = pl.ds(blk * (S // 4) + r + 4 * a, Q_TILE, stride=4)
    k_rows = pl.ds(KV_PAD + blk * (S // 4) + r + 4 * (a - BAND), K_TILE, stride=4)
    return a, q_rows, k_rows


def _lanes(x, width):
    return jnp.concatenate([x] * (width // HEAD_DIM), axis=-1)


_BNT = (((2,), (2,)), ((0,), (0,)))
_BNN = (((2,), (1,)), ((0,), (0,)))
_BTN = (((1,), (1,)), ((0,), (0,)))


def _bdot(a, b, dims):
    return lax.dot_general(a, b, dims, preferred_element_type=F32)


def _stacked(rows, loaders):
    return [jnp.stack([f(*r) for r in rows]) for f in loaders]


def _edge_mask(a, L):
    lk = lax.broadcasted_iota(jnp.int32, (1, K_TILE), 1) + (a - BAND)
    return jnp.where((lk >= 0) & (lk < L), 0.0, NEG).astype(F32)


def _fill_padded(dst, src, S):
    dst[pl.ds(0, KV_PAD), :] = jnp.zeros((KV_PAD, HEAD_DIM), F32)
    dst[pl.ds(KV_PAD + S, KV_PAD), :] = jnp.zeros((KV_PAD, HEAD_DIM), F32)
    dst[pl.ds(KV_PAD, S), :] = src[...]


def _head_specs(S, groups, n_heads):
    return [pl.BlockSpec((S, HEAD_DIM), functools.partial(lambda h, g: (0, g * n_heads + h), g=g)) for g in groups]


def _attn_fwd(proj, bias):
    S = proj.shape[0]
    H = proj.shape[1] // (N_GROUPS * HEAD_DIM)
    scale = HEAD_DIM ** -0.5

    def body(q_ref, k_ref, v_ref, b_ref, o_ref, lse_ref, kp, vp, m_run, l_run, q4, m3, l3, acc3):
        _fill_padded(kp, k_ref, S)
        _fill_padded(vp, v_ref, S)
        o_ref[...] = jnp.zeros_like(o_ref)
        m_run[...] = jnp.full(m_run.shape, NEG, F32)
        l_run[...] = jnp.zeros_like(l_run)
        for p, d in enumerate(DILATIONS[:2]):
            L, per_class, n_tiles = _attn_tiles(S, d)

            def tiles(t, carry, p=p, d=d, L=L, per_class=per_class, n_tiles=n_tiles):
                rows = [_tile_rows(t + u * (n_tiles // TILE_GROUP), d, per_class) for u in range(TILE_GROUP)]
                qs, ks, vs, m_old, l_old, o_old, edge = _stacked(rows, (
                    lambda a, qr, kr: q_ref[qr, :].astype(BF16), lambda a, qr, kr: kp[kr, :].astype(BF16),
                    lambda a, qr, kr: vp[kr, :].astype(BF16), lambda a, qr, kr: m_run[qr, :],
                    lambda a, qr, kr: l_run[qr, :], lambda a, qr, kr: o_ref[qr, :], lambda a, qr, kr: _edge_mask(a, L)))
                s = _bdot(qs, ks, _BNT) * scale + b_ref[0, p][None] + edge
                m_new = jnp.maximum(m_old, jnp.max(s, axis=-1, keepdims=True))
                pr = jnp.exp(s - _lanes(m_new, K_TILE)).astype(BF16)
                alpha = jnp.exp(m_old - m_new)
                l_new = alpha * l_old + _bdot(pr, jnp.ones((TILE_GROUP, K_TILE, HEAD_DIM), BF16), _BNN)
                o_new = alpha * o_old + _bdot(pr, vs, _BNN)
                for u, (_, qr, _) in enumerate(rows):
                    o_ref[qr, :] = o_new[u]
                    m_run[qr, :] = m_new[u]
                    l_run[qr, :] = l_new[u]
                return carry

            lax.fori_loop(0, n_tiles // TILE_GROUP, tiles, 0)

        _to_quarters(q4, q_ref, S)
        _to_quarters(kp, k_ref, S, KV_PAD)
        _to_quarters(vp, v_ref, S, KV_PAD)
        n_tiles = _attn_tiles(S, DILATIONS[2])[2]

        def tiles3(t, carry):
            rows = [_quarter_tile_rows(t + u * (n_tiles // TILE_GROUP), S) for u in range(TILE_GROUP)]
            qs, ks, vs, edge = _stacked(rows, (
                lambda a, qr, kr: q4[qr, :].astype(BF16), lambda a, qr, kr: kp[kr, :].astype(BF16),
                lambda a, qr, kr: vp[kr, :].astype(BF16), lambda a, qr, kr: _edge_mask(a, S // DILATIONS[2])))
            s = _bdot(qs, ks, _BNT) * scale + b_ref[0, 2][None] + edge
            m_new = jnp.broadcast_to(jnp.max(s, axis=-1, keepdims=True), (TILE_GROUP, Q_TILE, HEAD_DIM))
            pr = jnp.exp(s - _lanes(m_new, K_TILE)).astype(BF16)
            l_new = _bdot(pr, jnp.ones((TILE_GROUP, K_TILE, HEAD_DIM), BF16), _BNN)
            o_new = _bdot(pr, vs, _BNN)
            for u, (_, qr, _) in enumerate(rows):
                acc3[qr, :] = o_new[u]
                m3[qr, :] = m_new[u]
                l3[qr, :] = l_new[u]
            return carry

        lax.fori_loop(0, n_tiles // TILE_GROUP, tiles3, 0)
        for r in range(4):
            nat, qtr = pl.ds(r, S // 4, stride=4), pl.ds(r * (S // 4), S // 4)
            m_a, m_b = m_run[nat, :], m3[qtr, :]
            m = jnp.maximum(m_a, m_b)
            w_a, w_b = jnp.exp(m_a - m), jnp.exp(m_b - m)
            l = w_a * l_run[nat, :] + w_b * l3[qtr, :]
            o_ref[nat, :] = (w_a * o_ref[nat, :] + w_b * acc3[qtr, :]) / l
            lse_ref[nat, :] = m + jnp.log(l)

    hspec = pl.BlockSpec((S, HEAD_DIM), lambda h: (0, h))
    padded, plain = pltpu.VMEM((S + 2 * KV_PAD, HEAD_DIM), F32), pltpu.VMEM((S, HEAD_DIM), F32)
    return pl.pallas_call(
        body, grid=(H,), name="attn_fwd",
        in_specs=_head_specs(S, (0, 1, 2), H) + [
            pl.BlockSpec((1, len(DILATIONS), Q_TILE, K_TILE), lambda h: (h, 0, 0, 0))],
        out_specs=[hspec, hspec],
        out_shape=[SDS((S, H * HEAD_DIM), F32), SDS((S, H * HEAD_DIM), F32)],
        scratch_shapes=[padded, padded] + [plain] * 6,
        compiler_params=_cp(1))(proj, proj, proj, bias)


def _put_groups(stage, dproj, groups, n_heads, sems):
    h = pl.program_id(0)
    copies = [pltpu.make_async_copy(
        stage.at[i], dproj.at[:, pl.ds(pl.multiple_of((g * n_heads + h) * HEAD_DIM, HEAD_DIM), HEAD_DIM)], sems.at[i])
        for i, g in enumerate(groups)]
    for cp in copies:
        cp.start()
    for cp in copies:
        cp.wait()


def _attn_bwd(proj, out, lse, dmix, bias, dproj):
    S = proj.shape[0]
    H = proj.shape[1] // (N_GROUPS * HEAD_DIM)
    scale = HEAD_DIM ** -0.5
    assert S // DILATIONS[2] >= 2 * Q_TILE

    def body(q_ref, k_ref, v_ref, o_ref, lse_ref, do_ref, b_ref, dproj_in, dproj_out,
             kp, vp, dkp, dvp, dsum, q4, do4, lse4, dsum4, dq_ref, dk_ref, dv_ref, stage, sems):
        _fill_padded(kp, k_ref, S)
        _fill_padded(vp, v_ref, S)
        dkp[...] = jnp.zeros_like(dkp)
        dvp[...] = jnp.zeros_like(dvp)
        dq_ref[...] = jnp.zeros_like(dq_ref)
        dsum[...] = jnp.broadcast_to(jnp.sum(do_ref[...] * o_ref[...], axis=-1, keepdims=True), dsum.shape)

        def run(n_tiles, tile_rows, p, L, q_src, do_src, lse_src, dsum_src, dq_dst, dq_adds):
            def tiles(t, carry):
                rows = [tile_rows(t + u * (n_tiles // BWD_TILE_GROUP)) for u in range(BWD_TILE_GROUP)]
                qs, ks, vs, dos, lses, dsums, dk_old, dv_old, edge = _stacked(rows, (
                    lambda a, qr, kr: q_src[qr, :].astype(BF16), lambda a, qr, kr: kp[kr, :].astype(BF16),
                    lambda a, qr, kr: vp[kr, :].astype(BF16), lambda a, qr, kr: do_src[qr, :].astype(BF16),
                    lambda a, qr, kr: lse_src[qr, :], lambda a, qr, kr: dsum_src[qr, :],
                    lambda a, qr, kr: dkp[kr, :], lambda a, qr, kr: dvp[kr, :], lambda a, qr, kr: _edge_mask(a, L)))
                s = _bdot(qs, ks, _BNT) * scale + b_ref[0, p][None] + edge
                pr = jnp.exp(s - _lanes(lses, K_TILE))
                ds = (pr * (_bdot(dos, vs, _BNT) - _lanes(dsums, K_TILE)) * scale).astype(BF16)
                dq_new = _bdot(ds, ks, _BNN)
                if dq_adds:
                    dq_new = dq_new + jnp.stack([dq_dst[qr, :] for _, qr, _ in rows])
                dk_new = dk_old + _bdot(ds, qs, _BTN)
                dv_new = dv_old + _bdot(pr.astype(BF16), dos, _BTN)
                for u, (_, qr, kr) in enumerate(rows):
                    dq_dst[qr, :] = dq_new[u]
                    dkp[kr, :] = dk_new[u]
                    dvp[kr, :] = dv_new[u]
                return carry

            lax.fori_loop(0, n_tiles // BWD_TILE_GROUP, tiles, 0)

        for p, d in enumerate(DILATIONS[:2]):
            L, per_class, n_tiles = _attn_tiles(S, d)
            run(n_tiles, functools.partial(_tile_rows, d=d, per_class=per_class), p, L,
                q_ref, do_ref, lse_ref, dsum, dq_ref, True)
        dk_ref[...] = dkp[pl.ds(KV_PAD, S), :]
        dv_ref[...] = dvp[pl.ds(KV_PAD, S), :]

        for dst, src in ((q4, q_ref), (do4, do_ref), (lse4, lse_ref), (dsum4, dsum)):
            _to_quarters(dst, src, S)
        _to_quarters(kp, k_ref, S, KV_PAD)
        _to_quarters(vp, v_ref, S, KV_PAD)
        dkp[...] = jnp.zeros_like(dkp)
        dvp[...] = jnp.zeros_like(dvp)
        dq3 = dsum
        run(_attn_tiles(S, DILATIONS[2])[2], functools.partial(_quarter_tile_rows, S=S), 2, S // DILATIONS[2],
            q4, do4, lse4, dsum4, dq3, False)
        for r in range(4):
            nat, qtr = pl.ds(r, S // 4, stride=4), pl.ds(r * (S // 4), S // 4)
            pad_qtr = pl.ds(KV_PAD + r * (S // 4), S // 4)
            dq_ref[nat, :] = dq_ref[nat, :] + dq3[qtr, :]
            dk_ref[nat, :] = dk_ref[nat, :] + dkp[pad_qtr, :]
            dv_ref[nat, :] = dv_ref[nat, :] + dvp[pad_qtr, :]
        for i, acc in enumerate((dq_ref, dk_ref, dv_ref)):
            stage[i] = acc[...].astype(BF16)
        _put_groups(stage, dproj_out, (0, 1, 2), H, sems)

    hspec = pl.BlockSpec((S, HEAD_DIM), lambda h: (0, h))
    once = pl.BlockSpec((S, HEAD_DIM), lambda h: (0, h), pipeline_mode=pl.Buffered(1))
    padded, plain = pltpu.VMEM((S + 2 * KV_PAD, HEAD_DIM), F32), pltpu.VMEM((S, HEAD_DIM), F32)
    return pl.pallas_call(
        body, grid=(H,), name="attn_bwd",
        in_specs=_head_specs(S, (0, 1, 2), H) + [
            once, hspec, hspec, pl.BlockSpec((1, len(DILATIONS), Q_TILE, K_TILE), lambda h: (h, 0, 0, 0)), ANY],
        out_specs=ANY, out_shape=SDS(dproj.shape, dproj.dtype), input_output_aliases={7: 0},
        scratch_shapes=[padded] * 4 + [plain] * 8 + [pltpu.VMEM((3, S, HEAD_DIM), BF16), pltpu.SemaphoreType.DMA((3,))],
        compiler_params=_cp(1))(proj, proj, proj, out, lse, dmix, bias, dproj)


def _ret_consts(lg, forward):
    C = RET_CHUNK
    i = lax.broadcasted_iota(jnp.int32, (C, C), 0)
    j = lax.broadcasted_iota(jnp.int32, (C, C), 1)
    rel = (i - j) if forward else (j - i)
    inside = (rel >= 0) if forward else (rel > 0)
    relf = jnp.maximum(rel, 0).astype(F32)
    mask = jnp.where(inside, jnp.exp(lg * relf), 0.0)
    idx = lax.broadcasted_iota(jnp.int32, (C, 1), 0).astype(F32)
    q_exp = (idx + 1.0) if forward else (C - idx)
    k_exp = (C - 1.0 - idx) if forward else idx
    return mask, relf, jnp.exp(lg * q_exp), q_exp, jnp.exp(lg * k_exp), k_exp, jnp.exp(lg * C)


def _log_decay(dec_ref, h):
    return -jnp.exp(jnp.full((1, 1), dec_ref[0, h], F32))


FFN_BLOCK = 704
CHUNK_BATCH = 8


def _batch_rows(b):
    n = CHUNK_BATCH * RET_CHUNK
    return pl.ds(pl.multiple_of(b * n, n), n)


def _batch_chunks(b):
    return pl.ds(pl.multiple_of(b * CHUNK_BATCH, CHUNK_BATCH), CHUNK_BATCH)


def _chunks3(x):
    return x.reshape(CHUNK_BATCH, RET_CHUNK, HEAD_DIM)


def _ret_scan(buf, c_decs, nc, reverse):
    def step(n, carry):
        new = []
        for way, r in enumerate(carry):
            c = n if (way == 0) != reverse else nc - 1 - n
            term = buf[way, c]
            buf[way, c] = r
            new.append(r * c_decs[way] + term)
        return tuple(new)

    lax.fori_loop(0, nc, step, (jnp.zeros((HEAD_DIM, HEAD_DIM), F32),) * 2)


def _ret_fwd(proj, dec_f, dec_b, w_norm):
    S = proj.shape[0]
    H = proj.shape[1] // (N_GROUPS * HEAD_DIM)
    nc = S // RET_CHUNK
    scale = HEAD_DIM ** -0.5

    def body(df_ref, db_ref, q_ref, k_ref, v_ref, g_ref, w_ref, y_ref, o_ref, states):
        h = pl.program_id(0)
        consts = [_ret_consts(_log_decay(dref, h), fw) for fw, dref in ((True, df_ref), (False, db_ref))]

        def kv_step(b, carry):
            rows, batch = _batch_rows(b), _batch_chunks(b)
            k3 = _chunks3(k_ref[rows, :])
            v3 = _chunks3(v_ref[rows, :]).astype(BF16)
            for way in range(2):
                states[way, batch] = _bdot((k3 * consts[way][4]).astype(BF16), v3, _BTN)
            return carry

        lax.fori_loop(0, nc // CHUNK_BATCH, kv_step, 0)
        _ret_scan(states, [c[6] for c in consts], nc, False)

        def out_step(b, carry):
            rows, batch = _batch_rows(b), _batch_chunks(b)
            q3 = _chunks3(q_ref[rows, :] * scale)
            k3 = _chunks3(k_ref[rows, :]).astype(BF16)
            v3 = _chunks3(v_ref[rows, :]).astype(BF16)
            a0 = _bdot(q3.astype(BF16), k3, _BNT)
            o = None
            for way in range(2):
                mask, q_dec = consts[way][0], consts[way][2]
                part = _bdot((a0 * mask).astype(BF16), v3, _BNN) \
                    + _bdot((q3 * q_dec).astype(BF16), states[way, batch].astype(BF16), _BNN)
                o = part if o is None else o + part
            o_ref[rows, :] = o.reshape(CHUNK_BATCH * RET_CHUNK, HEAD_DIM)
            return carry

        lax.fori_loop(0, nc // CHUNK_BATCH, out_step, 0)
        o = o_ref[...]
        g = g_ref[...]
        y_ref[...] = o * _rms_scale(o) * w_ref[...] * (g * _sigmoid(g))

    hspec = pl.BlockSpec((S, HEAD_DIM), lambda h: (0, h))
    smem = pl.BlockSpec(memory_space=pltpu.SMEM)
    return pl.pallas_call(
        body, grid=(H,), name="ret_fwd",
        in_specs=[smem, smem] + _head_specs(S, (3, 4, 5, 6), H) + [pl.BlockSpec((1, HEAD_DIM), lambda h: (0, h))],
        out_specs=[hspec, hspec],
        out_shape=[SDS((S, H * HEAD_DIM), F32)] * 2,
        scratch_shapes=[pltpu.VMEM((2, nc, HEAD_DIM, HEAD_DIM), F32)],
        compiler_params=_cp(1))(dec_f, dec_b, proj, proj, proj, proj, w_norm)


def _ret_gate_bwd(proj, o_raw, dmix, w_norm, col0, dproj):
    S = proj.shape[0]
    H = proj.shape[1] // (N_GROUPS * HEAD_DIM)

    def body(g_ref, o_ref, dy_ref, w_ref, dproj_in, do_ref, dw_ref, dproj_out, dg_ref, sems):
        o = o_ref[...]
        g = g_ref[...]
        dy = dy_ref[...]
        w = w_ref[...]
        rr = _rms_scale(o)
        normed = o * rr
        sg = _sigmoid(g)
        silu = g * sg
        dw_ref[0] = jnp.broadcast_to(jnp.sum(dy * normed * silu, axis=0, keepdims=True), (8, HEAD_DIM))
        dg_ref[0] = (dy * normed * w * (sg * (1.0 + g * (1.0 - sg)))).astype(BF16)
        dnormed = dy * w * silu
        do_ref[...] = rr * dnormed - o * (rr * rr * rr) * jnp.mean(dnormed * o, axis=-1, keepdims=True)
        _put_groups(dg_ref, dproj_out, (6,), H, sems)

    hspec = pl.BlockSpec((S, HEAD_DIM), lambda h: (0, h))
    nh0 = col0 // HEAD_DIM
    return pl.pallas_call(
        body, grid=(H,), name="ret_gate_bwd",
        in_specs=_head_specs(S, (6,), H) + [hspec, pl.BlockSpec((S, HEAD_DIM), lambda h: (0, nh0 + h)),
                                            pl.BlockSpec((1, HEAD_DIM), lambda h: (0, h)), ANY],
        out_specs=[hspec, pl.BlockSpec((1, 8, HEAD_DIM), lambda h: (h, 0, 0)), ANY],
        out_shape=[SDS((S, H * HEAD_DIM), F32), SDS((H, 8, HEAD_DIM), F32), SDS(dproj.shape, dproj.dtype)],
        input_output_aliases={4: 2},
        scratch_shapes=[pltpu.VMEM((1, S, HEAD_DIM), BF16), pltpu.SemaphoreType.DMA((1,))],
        compiler_params=_cp(1))(proj, o_raw, dmix, w_norm, dproj)


def _ret_bwd(proj, d_out, dec_f, dec_b, dproj):
    S = proj.shape[0]
    H = proj.shape[1] // (N_GROUPS * HEAD_DIM)
    C = RET_CHUNK
    nc = S // C
    scale = HEAD_DIM ** -0.5

    def body(df_ref, db_ref, q_ref, k_ref, v_ref, do, dproj_in, small_ref, dproj_out, states, d_states, stage, sems):
        h = pl.program_id(0)
        lgs = [_log_decay(df_ref, h), _log_decay(db_ref, h)]
        consts = [_ret_consts(lg, fw) for lg, fw in zip(lgs, (True, False))]

        def prep_step(b, carry):
            rows, batch = _batch_rows(b), _batch_chunks(b)
            q3 = _chunks3(q_ref[rows, :] * scale)
            k3 = _chunks3(k_ref[rows, :])
            v3 = _chunks3(v_ref[rows, :]).astype(BF16)
            do3 = _chunks3(do[rows, :]).astype(BF16)
            for way in range(2):
                states[way, batch] = _bdot((k3 * consts[way][4]).astype(BF16), v3, _BTN)
                d_states[way, batch] = _bdot((q3 * consts[way][2]).astype(BF16), do3, _BTN)
            return carry

        lax.fori_loop(0, nc // CHUNK_BATCH, prep_step, 0)
        c_decs = [c[6] for c in consts]
        _ret_scan(states, c_decs, nc, False)
        _ret_scan(d_states, c_decs, nc, True)

        def main_step(b, dlams):
            rows, batch = _batch_rows(b), _batch_chunks(b)
            q3 = _chunks3(q_ref[rows, :] * scale)
            k3 = _chunks3(k_ref[rows, :])
            q3b, k3b = q3.astype(BF16), k3.astype(BF16)
            v3b = _chunks3(v_ref[rows, :]).astype(BF16)
            do3b = _chunks3(do[rows, :]).astype(BF16)
            a0 = _bdot(q3b, k3b, _BNT)
            pv = _bdot(do3b, v3b, _BNT)
            dq = dk = dv = None
            new_dlams = []
            for way in range(2):
                mask, relf, q_dec, q_exp, k_dec, k_exp, c_dec = consts[way]
                state, d_state = states[way, batch], d_states[way, batch]
                dp = pv * mask
                dpb = dp.astype(BF16)
                gq = _bdot(do3b, state.astype(BF16), _BNT)
                gk = _bdot(v3b, d_state.astype(BF16), _BNT)
                parts = (_bdot(dpb, k3b, _BNN) + q_dec * gq, _bdot(dpb, q3b, _BTN) + k_dec * gk,
                         _bdot((a0 * mask).astype(BF16), do3b, _BTN)
                         + _bdot((k3 * k_dec).astype(BF16), d_state.astype(BF16), _BNN))
                dq, dk, dv = parts if dq is None else (dq + parts[0], dk + parts[1], dv + parts[2])
                total = lambda x: jnp.sum(jnp.sum(x, axis=0), axis=0, keepdims=True)
                new_dlams.append(dlams[way] + total(relf * a0 * dp)
                                 + total(q_exp * q_dec * q3 * gq + k_exp * k_dec * k3 * gk)
                                 + (C * c_dec) * total(state * d_state))
            flat = lambda x: x.reshape(CHUNK_BATCH * C, HEAD_DIM)
            stage[0, rows, :] = (flat(dq) * scale).astype(BF16)
            stage[1, rows, :] = flat(dk).astype(BF16)
            stage[2, rows, :] = flat(dv).astype(BF16)
            return tuple(new_dlams)

        dlams = lax.fori_loop(0, nc // CHUNK_BATCH, main_step, (jnp.zeros((1, HEAD_DIM), F32),) * 2)
        for row, (dlam, lg) in enumerate(zip(dlams, lgs)):
            small_ref[0, pl.ds(row, 1), :] = jnp.broadcast_to(jnp.sum(dlam, axis=-1, keepdims=True) * lg, (1, HEAD_DIM))
        small_ref[0, pl.ds(2, 6), :] = jnp.zeros((6, HEAD_DIM), F32)
        _put_groups(stage, dproj_out, (3, 4, 5), H, sems)

    hspec = pl.BlockSpec((S, HEAD_DIM), lambda h: (0, h))
    smem = pl.BlockSpec(memory_space=pltpu.SMEM)
    return pl.pallas_call(
        body, grid=(H,), name="ret_bwd",
        in_specs=[smem, smem] + _head_specs(S, (3, 4, 5), H) + [hspec, ANY],
        out_specs=[pl.BlockSpec((1, 8, HEAD_DIM), lambda h: (h, 0, 0)), ANY],
        out_shape=[SDS((H, 8, HEAD_DIM), F32), SDS(dproj.shape, dproj.dtype)], input_output_aliases={6: 1},
        scratch_shapes=[pltpu.VMEM((2, nc, HEAD_DIM, HEAD_DIM), F32), pltpu.VMEM((2, nc, HEAD_DIM, HEAD_DIM), F32),
                        pltpu.VMEM((3, S, HEAD_DIM), BF16), pltpu.SemaphoreType.DMA((3,))],
        compiler_params=_cp(1))(dec_f, dec_b, proj, proj, proj, d_out, dproj)


def _ffn_bwd_act(dh2, wd, g, u):
    S, D = dh2.shape
    nblk, _, FB = g.shape
    tm = min(1024, S)

    def body(dh_ref, wd_ref, g_ref, u_ref, dg_ref, du_ref):
        dact = _dot(dh_ref[...], wd_ref[...], _NT)
        gg = g_ref[0].astype(F32)
        sg = _sigmoid(gg)
        dg_ref[0] = (dact * u_ref[0].astype(F32) * (sg * (1.0 + gg * (1.0 - sg)))).astype(BF16)
        du_ref[0] = (dact * (gg * sg)).astype(BF16)

    blk = pl.BlockSpec((1, tm, FB), lambda j, i: (j, i, 0))
    return pl.pallas_call(
        body, grid=(nblk, S // tm), name="ffn_bwd_act",
        in_specs=[pl.BlockSpec((tm, D), lambda j, i: (i, 0)), pl.BlockSpec((FB, D), lambda j, i: (j, 0)), blk, blk],
        out_specs=[blk, blk], out_shape=[SDS((nblk, S, FB), BF16)] * 2,
        compiler_params=_cp(2))(dh2, wd, g, u)


def _ffn_bwd_in(dg, du, wg, wu, h1, dh2, w_norm):
    nblk, S, FB = dg.shape
    D = h1.shape[1]
    tm = min(RESIDENT_ROWS, S)
    blk = pl.BlockSpec((nblk, tm, FB), lambda i: (0, i, 0))
    row = pl.BlockSpec((tm, D), lambda i: (i, 0))
    vec = pl.BlockSpec((1, D), lambda i: (0, 0))

    def gate_body(dg_ref, wg_ref, part_ref):
        part_ref[...] = _blocked_matmul(dg_ref, wg_ref)

    part = pl.pallas_call(
        gate_body, grid=(S // tm,), name="ffn_bwd_in_gate", in_specs=[blk, _resident((nblk * FB, D))],
        out_specs=row, out_shape=SDS((S, D), F32), compiler_params=_cp(1))(dg, wg.reshape(nblk * FB, D))

    def body(du_ref, wu_ref, part_ref, h_ref, dh2_ref, wn_ref, dh_ref, dhb_ref, dw_ref):
        @pl.when(pl.program_id(0) == 0)
        def _():
            dw_ref[...] = jnp.zeros_like(dw_ref)

        dh, dw = _rms_bwd(part_ref[...] + _blocked_matmul(du_ref, wu_ref), h_ref[...], wn_ref[...])
        dh = dh2_ref[...] + dh
        dh_ref[...] = dh
        dhb_ref[...] = dh.astype(BF16)
        dw_ref[...] += dw

    return pl.pallas_call(
        body, grid=(S // tm,), name="ffn_bwd_in",
        in_specs=[blk, _resident((nblk * FB, D)), row, row, row, vec],
        out_specs=[row, row, vec], out_shape=[SDS((S, D), F32), SDS((S, D), BF16), SDS((1, D), F32)],
        compiler_params=_cp(1))(du, wu.reshape(nblk * FB, D), part, h1, dh2, w_norm)


def _dmix(dh1, w_out):
    S, D = dh1.shape
    tm = min(512, S)

    def body(dh_ref, w_ref, o_ref):
        o_ref[...] = _dot(dh_ref[...], w_ref[...], _NT)

    row = pl.BlockSpec((tm, D), lambda i: (i, 0))
    return pl.pallas_call(
        body, grid=(S // tm,), name="dmix", in_specs=[row, pl.BlockSpec((D, D), lambda i: (0, 0))],
        out_specs=row, out_shape=SDS((S, D), F32), compiler_params=_cp(1))(dh1, w_out)


def _in_bwd(dproj, w_blk, x, dh1, w_norm):
    S, D = x.shape
    nblk, _, NB = w_blk.shape
    tm = min(RESIDENT_ROWS, S)

    def body(dp_ref, w_ref, x_ref, dh1_ref, wn_ref, dx_ref, dw_ref):
        @pl.when(pl.program_id(0) == 0)
        def _():
            dw_ref[...] = jnp.zeros_like(dw_ref)

        dn = None
        for j in range(nblk):
            part = _dot(dp_ref[:, pl.ds(j * NB, NB)], w_ref[j], _NT)
            dn = part if dn is None else dn + part
        dh, dw = _rms_bwd(dn, x_ref[...], wn_ref[...])
        dx_ref[...] = dh1_ref[...] + dh
        dw_ref[...] += dw

    row = pl.BlockSpec((tm, D), lambda i: (i, 0))
    vec = pl.BlockSpec((1, D), lambda i: (0, 0))
    return pl.pallas_call(
        body, grid=(S // tm,), name="in_bwd",
        in_specs=[pl.BlockSpec((tm, nblk * NB), lambda i: (i, 0)),
                  pl.BlockSpec((nblk, D, NB), lambda i: (0, 0, 0), pipeline_mode=pl.Buffered(1)), row, row, vec],
        out_specs=[row, vec], out_shape=[SDS((S, D), F32), SDS((1, D), F32)],
        compiler_params=_cp(1))(dproj, w_blk, x, dh1, w_norm)


def _wgrad(a, b, a_spec, b_spec, o_spec, o_shape, grid, name):
    nk = grid[-1]

    def ld(ref):
        return ref[0] if len(ref.shape) == 3 else ref[...]

    def body(a_ref, b_ref, o_ref, acc):
        k = pl.program_id(len(grid) - 1)

        @pl.when(k == 0)
        def _():
            acc[...] = jnp.zeros_like(acc)

        acc[...] += _dot(ld(a_ref), ld(b_ref), _TN)

        @pl.when(k == nk - 1)
        def _():
            if len(o_ref.shape) == 3:
                o_ref[0] = acc[...].astype(o_ref.dtype)
            else:
                o_ref[...] = acc[...].astype(o_ref.dtype)

    return pl.pallas_call(
        body, grid=grid, name=name, in_specs=[a_spec, b_spec], out_specs=o_spec, out_shape=SDS(o_shape, BF16),
        scratch_shapes=[pltpu.VMEM(o_spec.block_shape[-2:], F32)], compiler_params=_cp(len(grid)))(a, b)


def _peer(k):
    x, y, c = lax.axis_index("x"), lax.axis_index("y"), lax.axis_index("c")
    px = 1 - x if k & 4 else x
    py = 1 - y if k & 2 else y
    pc = 1 - c if k & 1 else c
    return (px, py, pc), 4 * px + 2 * py + pc


def _exchange_copies(srcs, lands, send_sems, recv_sems, which, gather):
    _, me = _peer(0)
    pairs = []
    for pos, a in enumerate(which):
        for k in range(1, N_DEV):
            dev, idx = _peer(k)
            sem = pos * (N_DEV - 1) + k - 1
            src = srcs[a] if gather else srcs[a].at[idx]
            mk = functools.partial(pltpu.make_async_remote_copy, src_ref=src, send_sem=send_sems.at[sem],
                                   recv_sem=recv_sems.at[sem], device_id=dev, device_id_type=MESH)
            pairs.append((mk(dst_ref=lands[a].at[me]), mk(dst_ref=lands[a].at[idx])))
    return pairs


def _sequencer_kernel(name, collective_id, n_remote, n_local):
    return pl.kernel(mesh=plsc.ScalarSubcoreMesh(axis_name="sequencer", num_cores=1), name=name,
                     scratch_types=(pltpu.SemaphoreType.DMA((n_remote,)), pltpu.SemaphoreType.DMA((n_remote,)),
                                    pltpu.SemaphoreType.DMA((n_local,))),
                     compiler_params=pltpu.CompilerParams(collective_id=collective_id))


def _handshake(ks):
    barrier = pltpu.get_barrier_semaphore()
    for k in ks:
        pl.semaphore_signal(barrier, inc=1, device_id=_peer(k)[0], device_id_type=MESH)
    pl.semaphore_wait(barrier, len(ks))


def _sequencer_scatter(arrays, name, collective_id):
    n = len(arrays)
    hbm = pltpu.MemorySpace.HBM
    srcs = [jax.new_ref(a, memory_space=hbm) for a in arrays]
    lands = [jax.empty_ref(SDS(a.shape, a.dtype), memory_space=hbm) for a in arrays]

    @_sequencer_kernel(name, collective_id, n * (N_DEV - 1), n)
    def launch(send_sems, recv_sems, local_sems):
        _handshake(range(1, N_DEV))
        _, me = _peer(0)
        local = [pltpu.make_async_copy(srcs[a].at[me], lands[a].at[me], local_sems.at[a]) for a in range(n)]
        pairs = _exchange_copies(srcs, lands, send_sems, recv_sems, range(n), False)
        for out, _ in pairs:
            out.start()
        for cp in local:
            cp.start()
        for out, arrival in pairs:
            out.wait_send()
            arrival.wait_recv()
        for cp in local:
            cp.wait()

    launch()
    return [r[...] for r in lands]


SIBLING = 1
OTHER_CHIPS = (2, 4, 6)


def _sequencer_gather(arrays, name, collective_id):
    n = len(arrays)
    hbm = pltpu.MemorySpace.HBM
    srcs = [jax.new_ref(a, memory_space=hbm) for a in arrays]
    lands = [jax.empty_ref(SDS((N_DEV,) + a.shape, a.dtype), memory_space=hbm) for a in arrays]

    @_sequencer_kernel(name, collective_id, n * (N_DEV - 1), n)
    def launch(send_sems, recv_sems, local_sems):
        _handshake((SIBLING,) + OTHER_CHIPS)
        _, me = _peer(0)
        sibling, _ = _peer(SIBLING)

        def copy(a, k, src, block, to):
            sem = a * (N_DEV - 1) + k - 1
            return pltpu.make_async_remote_copy(src_ref=src, dst_ref=lands[a].at[block], send_sem=send_sems.at[sem],
                                                recv_sem=recv_sems.at[sem], device_id=to, device_id_type=MESH)

        local = [pltpu.make_async_copy(srcs[a], lands[a].at[me], local_sems.at[a]) for a in range(n)]
        first = [copy(a, k, srcs[a], me, _peer(k)[0]) for a in range(n) for k in OTHER_CHIPS + (SIBLING,)]
        for cp in first + local:
            cp.start()
        passed = []
        for a in range(n):
            for k in OTHER_CHIPS:
                _, block = _peer(k)
                copy(a, k, srcs[a], block, sibling).wait_recv()
                passed.append(copy(a, k ^ SIBLING, lands[a].at[block], block, sibling))
                passed[-1].start()
        for a in range(n):
            for k in (SIBLING,) + tuple(k ^ SIBLING for k in OTHER_CHIPS):
                copy(a, k, srcs[a], _peer(k)[1], sibling).wait_recv()
        for cp in first + passed:
            cp.wait_send()
        for cp in local:
            cp.wait()

    launch()
    return [r[...] for r in lands]


def _sequencer_gather_chips(array, name, collective_id, chips):
    hbm = pltpu.MemorySpace.HBM
    src = jax.new_ref(array, memory_space=hbm)
    land = jax.empty_ref(SDS((2 * len(chips),) + array.shape, array.dtype), memory_space=hbm)

    @_sequencer_kernel(name, collective_id, 2 * len(chips), 1)
    def launch(send_sems, recv_sems, local_sems):
        _handshake((SIBLING,) + tuple(k for k in chips if k))
        c = lax.axis_index("c")
        sibling, _ = _peer(SIBLING)

        def copy(sem, src_ref, slot, to):
            return pltpu.make_async_remote_copy(src_ref=src_ref, dst_ref=land.at[slot], send_sem=send_sems.at[sem],
                                                recv_sem=recv_sems.at[sem], device_id=to, device_id_type=MESH)

        started = []
        for pos, k in enumerate(chips):
            started.append(copy(2 * pos, src, 2 * pos + c, _peer(k)[0] if k else sibling))
            started[-1].start()
        for pos, k in enumerate(chips):
            if k:
                copy(2 * pos, src, 2 * pos + c, sibling).wait_recv()
                started.append(copy(2 * pos + 1, land.at[2 * pos + c], 2 * pos + c, sibling))
                started[-1].start()
        for pos, k in enumerate(chips):
            copy(2 * pos + 1 if k else 2 * pos, src, 2 * pos + 1 - c, sibling).wait_recv()
        for cp in started:
            cp.wait_send()

    launch()
    return land[...]


SMALL_ROWS = 64


def _small_step(part, w, m, v):
    def body(p_ref, w_ref, m_ref, v_ref, g_ref, d_ref, nm_ref, nv_ref, gath, send_sems, recv_sems):
        _, me = _peer(0)
        gath[me] = p_ref[...]
        copies = []
        for k in range(1, N_DEV):
            dev, idx = _peer(k)
            out = pltpu.make_async_remote_copy(src_ref=p_ref, dst_ref=gath.at[me], send_sem=send_sems.at[k - 1],
                                               recv_sem=recv_sems.at[k - 1], device_id=dev, device_id_type=MESH)
            out.start()
            arrival = pltpu.make_async_remote_copy(src_ref=p_ref, dst_ref=gath.at[idx], send_sem=send_sems.at[k - 1],
                                                   recv_sem=recv_sems.at[k - 1], device_id=dev, device_id_type=MESH)
            copies.append((out, arrival))
        for out, arrival in copies:
            out.wait_send()
            arrival.wait_recv()
        g = gath[0]
        for p in range(1, N_DEV):
            g = g + gath[p]
        g_ref[...] = g
        d_ref[...], nm_ref[...], nv_ref[...] = _adamw(w_ref[...], g, m_ref[...], v_ref[...])

    vm = pl.BlockSpec(memory_space=pltpu.VMEM)
    return pl.pallas_call(
        body, name="small_step", in_specs=[vm] * 4, out_specs=[vm] * 4,
        out_shape=[SDS((SMALL_ROWS, 128), F32)] * 4,
        scratch_shapes=[pltpu.VMEM((N_DEV, SMALL_ROWS, 128), F32), pltpu.SemaphoreType.DMA((N_DEV - 1,)),
                        pltpu.SemaphoreType.DMA((N_DEV - 1,))])(part, w, m, v)


def _adamw(w, g, m, v):
    m = ADAM_B1 * m + (1.0 - ADAM_B1) * g
    v = ADAM_B2 * v + (1.0 - ADAM_B2) * (g * g)
    m_hat = m / (1.0 - ADAM_B1 ** ADAM_STEP)
    v_hat = v / (1.0 - ADAM_B2 ** ADAM_STEP)
    delta = -ADAM_LR * (m_hat / (jnp.sqrt(v_hat) + ADAM_EPS) + ADAM_WD * w)
    return delta, m, v


def _adamw_block(parts, w, m, v, name):
    R, C = w.shape
    n_parts = len(parts)
    Rp = R // n_parts
    tr = next(t for t in (256, 128, 64, 32, 16, 8) if Rp % t == 0 and t * C <= 256 * 1024)
    per_part = Rp // tr

    def body(*refs):
        p_refs = refs[:n_parts]
        w_ref, m_ref, v_ref, g_ref, d_ref, nm_ref, nv_ref = refs[n_parts:]
        for k, p_ref in enumerate(p_refs):
            @pl.when(pl.program_id(0) // per_part == k)
            def _(p_ref=p_ref):
                g = p_ref[0].astype(F32)
                for p in range(1, N_DEV):
                    g = g + p_ref[p].astype(F32)
                g_ref[...] = g
                d_ref[...], nm_ref[...], nv_ref[...] = _adamw(w_ref[...], g, m_ref[...], v_ref[...])

    row = pl.BlockSpec((tr, C), lambda i: (i, 0))
    part_specs = [pl.BlockSpec((N_DEV, tr, C), functools.partial(
        lambda i, k: (0, jnp.clip(i - k * per_part, 0, per_part - 1), 0), k=k)) for k in range(n_parts)]
    return pl.pallas_call(
        body, grid=(R // tr,), name=name, in_specs=part_specs + [row, row, row],
        out_specs=[row] * 4, out_shape=[SDS((R, C), F32)] * 4, compiler_params=_cp(1))(*parts, w, m, v)


def _pack_small(mix, ffn, fin, retw, dec_f, dec_b, loss):
    flat = jnp.concatenate([mix.reshape(-1), ffn.reshape(-1), fin.reshape(-1), retw.reshape(-1), dec_f.reshape(-1),
                            dec_b.reshape(-1), loss.reshape(-1)])
    return jnp.pad(flat, (0, SMALL_ROWS * 128 - flat.shape[0])).reshape(SMALL_ROWS, 128)


def _unpack_small(packed, shapes):
    flat = packed.reshape(-1)
    out, at = [], 0
    for s in shapes:
        n = math.prod(s)
        out.append(flat[at:at + n].reshape(s))
        at += n
    return out


def kernel(x, norm_mix_w, w_in, ret_decay_fwd, ret_decay_bwd, ret_norm_w, w_out, norm_ffn_w, w_gate, w_up, w_down, norm_final_w, loss_target, m_norm_mix_w, m_w_in, m_ret_decay_fwd, m_ret_decay_bwd, m_ret_norm_w, m_w_out, m_norm_ffn_w, m_w_gate, m_w_up, m_w_down, m_norm_final_w, v_norm_mix_w, v_w_in, v_ret_decay_fwd, v_ret_decay_bwd, v_ret_norm_w, v_w_out, v_norm_ffn_w, v_w_gate, v_w_up, v_w_down, v_norm_final_w):
    x2 = x[0]
    tgt = loss_target[0]
    S, D = x2.shape
    H = ret_norm_w.shape[1] // HEAD_DIM
    DA = H * HEAD_DIM
    fin_w = norm_final_w.reshape(1, D)
    big = (w_in[0], w_out[0], w_gate[0].T, w_up[0].T, w_down[0])

    big_b = [w.astype(BF16) for w in big]
    stages = ((0,), (4, 2), (6,))
    wi_stages = [_sequencer_gather_chips(big_b[0], name, cid, ks)
                 for name, cid, ks in zip(("gather_in_own", "gather_in_near", "gather_in_far"), (0, 7, 8), stages)]
    wo, = _sequencer_gather(big_b[1:2], "gather_out", 1)
    wg, wu = _sequencer_gather(big_b[2:4], "gather_gate_up", 9)
    wd, = _sequencer_gather(big_b[4:], "gather_down", 5)
    wi, = _sequencer_gather(big_b[:1], "gather_in_ordered", 10)
    NB = big_b[0].shape[1]
    ax, ay = lax.axis_index("x"), lax.axis_index("y")
    chip_of = {k: 2 * (1 - ax if k & 4 else ax) + (1 - ay if k & 2 else ay) for k in (0, 2, 4, 6)}

    n1 = _norm_fwd(x2, norm_mix_w)
    ac = lax.axis_index("c")
    me = 2 * chip_of[0] + ac
    vec = lambda *v: jnp.stack([jnp.asarray(t, jnp.int32) for t in v])
    proj = _proj_part(n1, big_b[0][None], vec(0), vec(me), None, N_DEV, "proj_self")
    for ks, w_st, name in zip(stages, wi_stages, ("proj_sibling", "proj_near", "proj_far")):
        slots, blocks = [], []
        for pos, k in enumerate(ks):
            for core in ((1 - ac,) if k == 0 else (0, 1)):
                slots.append(2 * pos + core)
                blocks.append(2 * chip_of[k] + core)
        proj = _proj_part(n1, w_st, vec(*slots), vec(*blocks), proj, N_DEV, name)
    bias = _attn_bias()[:H]
    attn, lse = _attn_fwd(proj, bias)
    ret, o_raw = _ret_fwd(proj, ret_decay_fwd, ret_decay_bwd, ret_norm_w)
    wo_full = wo.reshape(D, D)
    d_ff = N_DEV * wd.shape[1]
    FB = FFN_BLOCK if d_ff % FFN_BLOCK == 0 else wd.shape[1]
    n_fb = d_ff // FB
    wg, wu = wg.reshape(n_fb, FB, D), wu.reshape(n_fb, FB, D)
    wd_full = wd.reshape(d_ff, D)
    h1, mixed, n2 = _out_fwd(x2, attn, ret, wo_full, norm_ffn_w)
    gate, up, act = _ffn_up(n2, wg, wu)
    dh2, dh2_b, loss_parts, g_fin = _ffn_down_loss(act, wd_full, h1, tgt, fin_w)

    dgate, dup = _ffn_bwd_act(dh2_b, wd_full, gate, up)
    tn = min(1024, D)
    ffn_specs = (pl.BlockSpec((1, S, FB), lambda j, n, k: (j, 0, 0)), pl.BlockSpec((S, tn), lambda j, n, k: (0, n)),
                 pl.BlockSpec((1, FB, tn), lambda j, n, k: (j, 0, n)), (n_fb, FB, D), (n_fb, D // tn, 1))
    per_dev = (N_DEV, d_ff // N_DEV, D)
    g_wd = _wgrad(act, dh2_b, *ffn_specs, "wgrad_down").reshape(per_dev)
    g_wg = _wgrad(dgate, n2, *ffn_specs, "wgrad_gate").reshape(per_dev)
    g_wu = _wgrad(dup, n2, *ffn_specs, "wgrad_up").reshape(per_dev)
    parts_f = _sequencer_scatter([g_wg, g_wu, g_wd], "scatter_ffn", 2)
    dh1, dh1_b, g_ffn = _ffn_bwd_in(dgate, dup, wg, wu, h1, dh2, norm_ffn_w)
    dmix = _dmix(dh1_b, wo_full)
    tmw = min(512, D)
    tk = min(2048, S)
    g_wo = _wgrad(mixed, dh1_b, pl.BlockSpec((tk, tmw), lambda m, k: (k, m)), pl.BlockSpec((tk, D), lambda m, k: (k, 0)),
                  pl.BlockSpec((tmw, D), lambda m, k: (m, 0)), (D, D), (D // tmw, S // tk), "wgrad_out")
    parts_o = _sequencer_scatter([g_wo.reshape(N_DEV, D // N_DEV, D)], "scatter_out", 3)
    d_ret, small_w, dproj = _ret_gate_bwd(proj, o_raw, dmix, ret_norm_w, DA, lax.empty(proj.shape, BF16))
    small, dproj = _ret_bwd(proj, d_ret, ret_decay_fwd, ret_decay_bwd, dproj)
    dproj = _attn_bwd(proj, attn, lse, dmix, bias, dproj)
    half = D // tmw // 2
    parts_i = []
    for part, (name, cid) in enumerate((("in_lo", 4), ("in_hi", 6))):
        g_wi = _wgrad(n1, dproj, pl.BlockSpec((S, tmw), functools.partial(lambda j, m, k, off: (0, m + off), off=part * half)),
                      pl.BlockSpec((S, NB), lambda j, m, k: (0, j)), pl.BlockSpec((1, tmw, NB), lambda j, m, k: (j, m, 0)),
                      (N_DEV, D // 2, NB), (N_DEV, half, 1), "wgrad_" + name)
        parts_i += _sequencer_scatter([g_wi], "scatter_" + name, cid)
    grad_x, g_mix = _in_bwd(dproj, wi, x2, dh1, norm_mix_w)

    big_m = (m_w_in[0], m_w_out[0], m_w_gate[0].T, m_w_up[0].T, m_w_down[0])
    big_v = (v_w_in[0], v_w_out[0], v_w_gate[0].T, v_w_up[0].T, v_w_down[0])
    names = ("adamw_in", "adamw_out", "adamw_gate", "adamw_up", "adamw_down")
    upd = [None] * 5
    for a, p in zip((2, 3, 4, 1, 0), [[t] for t in parts_f + parts_o] + [parts_i]):
        upd[a] = _adamw_block(p, big[a], big_m[a], big_v[a], names[a])

    g_dec_f = small[:, 0, 0].reshape(1, H)
    g_dec_b = small[:, 1, 0].reshape(1, H)
    g_retw = small_w[:, 0, :].reshape(1, DA)
    loss_local = jnp.sum(loss_parts[::8, 0])
    zero = jnp.zeros((1,), F32)
    part = _pack_small(g_mix, g_ffn, g_fin, g_retw, g_dec_f, g_dec_b, loss_local)
    sw = _pack_small(norm_mix_w, norm_ffn_w, norm_final_w, ret_norm_w, ret_decay_fwd, ret_decay_bwd, zero)
    sm = _pack_small(m_norm_mix_w, m_norm_ffn_w, m_norm_final_w, m_ret_norm_w, m_ret_decay_fwd, m_ret_decay_bwd, zero)
    sv = _pack_small(v_norm_mix_w, v_norm_ffn_w, v_norm_final_w, v_ret_norm_w, v_ret_decay_fwd, v_ret_decay_bwd, zero)
    shapes = [(1, D), (1, D), (D,), (1, DA), (1, H), (1, H), ()]
    sg, sd, snm, snv = [_unpack_small(t, shapes) for t in _small_step(part, sw, sm, sv)]
    loss = sg[6]

    def ordered(small_set, k):
        b = [(u[k].T if a in (2, 3) else u[k])[None] for a, u in enumerate(upd)]
        return [small_set[0], b[0], small_set[4], small_set[5], small_set[3], b[1], small_set[1], b[2], b[3], b[4],
                small_set[2]]

    return (loss, grad_x[None], *ordered(sg, 0), *ordered(sd, 1), *ordered(snm, 2), *ordered(snv, 3))
```

```python
import functools
import math

import numpy as np
import jax
import jax.numpy as jnp
from jax import lax
from jax.experimental import pallas as pl
from jax.experimental.pallas import tpu as pltpu
from jax.experimental.pallas import tpu_sc as plsc

F32 = jnp.float32
BF16 = jnp.bfloat16
SDS = jax.ShapeDtypeStruct

HEAD_DIM = 128
EPS = 1e-6
RET_CHUNK = 128
DILATIONS = (1, 4, 16)
BAND = 64
Q_TILE = 128
K_TILE = Q_TILE + 2 * BAND
KV_PAD = BAND * 4
TILE_GROUP = 8
BWD_TILE_GROUP = 8
NEG = -1e30
N_DEV = 8
N_GROUPS = 7
ADAM_LR, ADAM_B1, ADAM_B2, ADAM_EPS, ADAM_WD, ADAM_STEP = 0.001, 0.9, 0.999, 1e-08, 0.01, 10
VMEM_LIMIT = 56 * 1024 * 1024
MESH = pl.DeviceIdType.MESH
ANY = pl.BlockSpec(memory_space=pl.ANY)


def _cp(n_grid):
    return pltpu.CompilerParams(dimension_semantics=("arbitrary",) * n_grid, vmem_limit_bytes=VMEM_LIMIT)


def _sigmoid(x):
    return 1.0 / (1.0 + jnp.exp(-x))


def _rms_scale(h):
    return lax.rsqrt(jnp.mean(h * h, axis=-1, keepdims=True) + EPS)


def _rms_bwd(dn, h, w):
    r = _rms_scale(h)
    gw = dn * w
    dh = r * gw - h * (r * r * r) * jnp.mean(gw * h, axis=-1, keepdims=True)
    return dh, jnp.sum(dn * h * r, axis=0, keepdims=True)


def _dot(a, b, dims):
    return lax.dot_general(a.astype(BF16), b.astype(BF16), (dims, ((), ())), preferred_element_type=F32)


_NN = ((1,), (0,))
_NT = ((1,), (1,))
_TN = ((0,), (0,))


RESIDENT_ROWS = 256


def _resident(shape):
    return pl.BlockSpec(shape, lambda i: (0, 0), pipeline_mode=pl.Buffered(1))


def _blocked_matmul(a_ref, w_ref):
    nblk, _, fb = a_ref.shape
    out = None
    for j in range(nblk):
        part = jnp.dot(a_ref[j], w_ref[pl.ds(j * fb, fb), :], preferred_element_type=F32)
        out = part if out is None else out + part
    return out


def _norm_fwd(x, w_norm):
    S, D = x.shape
    tm = min(1024, S)

    def body(x_ref, wn_ref, n_ref):
        xf = x_ref[...]
        n_ref[...] = (xf * _rms_scale(xf) * wn_ref[...]).astype(BF16)

    row = pl.BlockSpec((tm, D), lambda i: (i, 0))
    return pl.pallas_call(body, grid=(S // tm,), name="norm_fwd", in_specs=[row, pl.BlockSpec((1, D), lambda i: (0, 0))],
                          out_specs=row, out_shape=SDS((S, D), BF16), compiler_params=_cp(1))(x, w_norm)


def _proj_part(n1, w_slots, slots, blocks, proj, n_blocks, name):
    S, D = n1.shape
    NB = w_slots.shape[2]
    tm = min(1024, S)

    def body(slots_ref, blocks_ref, n_ref, w_ref, *rest):
        rest[-1][...] = jnp.dot(n_ref[...], w_ref[0], preferred_element_type=F32)

    out_spec = pl.BlockSpec((tm, NB), lambda i, j, slots, blocks: (i, blocks[j]))
    in_specs = [pl.BlockSpec((tm, D), lambda i, j, slots, blocks: (i, 0)),
                pl.BlockSpec((1, D, NB), lambda i, j, slots, blocks: (slots[j], 0, 0))]
    args = [n1, w_slots]
    if proj is not None:
        in_specs.append(ANY)
        args.append(proj)
    return pl.pallas_call(
        body, name=name, out_shape=SDS((S, n_blocks * NB), F32),
        grid_spec=pltpu.PrefetchScalarGridSpec(num_scalar_prefetch=2, grid=(S // tm, slots.shape[0]), in_specs=in_specs,
                                               out_specs=out_spec),
        input_output_aliases={} if proj is None else {4: 0},
        compiler_params=_cp(2))(slots, blocks, *args)


def _out_fwd(x, attn, ret, w_out, w_norm):
    S, D = x.shape
    DA = attn.shape[1]
    tm = min(256, S)

    def body(x_ref, a_ref, r_ref, w_ref, wn_ref, h_ref, mix_ref, n_ref):
        a = a_ref[...].astype(BF16)
        r = r_ref[...].astype(BF16)
        mix_ref[:, :DA] = a
        mix_ref[:, DA:] = r
        h = x_ref[...] + jnp.dot(a, w_ref[:DA, :], preferred_element_type=F32) \
            + jnp.dot(r, w_ref[DA:, :], preferred_element_type=F32)
        h_ref[...] = h
        n_ref[...] = (h * _rms_scale(h) * wn_ref[...]).astype(BF16)

    row = lambda w: pl.BlockSpec((tm, w), lambda i: (i, 0))
    return pl.pallas_call(
        body, grid=(S // tm,), name="out_fwd",
        in_specs=[row(D), row(DA), row(D - DA), pl.BlockSpec((D, D), lambda i: (0, 0)),
                  pl.BlockSpec((1, D), lambda i: (0, 0))],
        out_specs=[row(D), row(D), row(D)],
        out_shape=[SDS((S, D), F32), SDS((S, D), BF16), SDS((S, D), BF16)],
        compiler_params=_cp(1))(x, attn, ret, w_out, w_norm)


def _ffn_up(n2, wg, wu):
    S, D = n2.shape
    nblk, FB, _ = wg.shape
    tm = min(1024, S)

    def body(n_ref, wg_ref, wu_ref, g_ref, u_ref, a_ref):
        n = n_ref[...]
        g = _dot(n, wg_ref[0], _NT)
        u = _dot(n, wu_ref[0], _NT)
        g_ref[0] = g.astype(BF16)
        u_ref[0] = u.astype(BF16)
        a_ref[0] = (g * _sigmoid(g) * u).astype(BF16)

    wspec = pl.BlockSpec((1, FB, D), lambda j, i: (j, 0, 0))
    ospec = pl.BlockSpec((1, tm, FB), lambda j, i: (j, i, 0))
    return pl.pallas_call(
        body, grid=(nblk, S // tm), name="ffn_up",
        in_specs=[pl.BlockSpec((tm, D), lambda j, i: (i, 0)), wspec, wspec],
        out_specs=[ospec, ospec, ospec],
        out_shape=[SDS((nblk, S, FB), BF16)] * 3,
        compiler_params=_cp(2))(n2, wg, wu)


def _ffn_down_loss(act, wd, h1, target, w_norm):
    nblk, S, FB = act.shape
    D = h1.shape[1]
    tm = min(RESIDENT_ROWS, S)

    def body(a_ref, wd_ref, h_ref, t_ref, wn_ref, dh_ref, dhb_ref, loss_ref, dw_ref):
        @pl.when(pl.program_id(0) == 0)
        def _():
            dw_ref[...] = jnp.zeros_like(dw_ref)

        h = h_ref[...] + _blocked_matmul(a_ref, wd_ref)
        w = wn_ref[...]
        err = h * _rms_scale(h) * w - t_ref[...]
        loss_ref[...] = jnp.full(loss_ref.shape, 0.5 * jnp.sum(err * err) / D, F32)
        dh, dw = _rms_bwd(err * (1.0 / D), h, w)
        dh_ref[...] = dh
        dhb_ref[...] = dh.astype(BF16)
        dw_ref[...] += dw

    row = pl.BlockSpec((tm, D), lambda i: (i, 0))
    vec = pl.BlockSpec((1, D), lambda i: (0, 0))
    return pl.pallas_call(
        body, grid=(S // tm,), name="ffn_down_loss",
        in_specs=[pl.BlockSpec((nblk, tm, FB), lambda i: (0, i, 0)), _resident((nblk * FB, D)), row, row, vec],
        out_specs=[row, row, pl.BlockSpec((8, 128), lambda i: (i, 0)), vec],
        out_shape=[SDS((S, D), F32), SDS((S, D), BF16), SDS((S // tm * 8, 128), F32), SDS((1, D), F32)],
        compiler_params=_cp(1))(act, wd, h1, target, w_norm)


def _attn_bias():
    n_heads = 8
    slopes = np.exp2(-8.0 * np.arange(1, n_heads + 1, dtype=np.float32) / n_heads)
    dist = np.abs(np.arange(K_TILE)[None, :] - BAND - np.arange(Q_TILE)[:, None])
    out = np.empty((n_heads, len(DILATIONS), Q_TILE, K_TILE), np.float32)
    for h in range(n_heads):
        for p, d in enumerate(DILATIONS):
            out[h, p] = np.where(dist <= BAND, -slopes[h] * (d * dist).astype(np.float32), NEG)
    return jnp.asarray(out)


def _attn_tiles(S, d):
    L = S // d
    per_class = L // Q_TILE
    return L, per_class, d * per_class


def _tile_rows(t, d, per_class):
    r = t // per_class
    a = (t % per_class) * Q_TILE
    q_rows = pl.ds(r + d * a, Q_TILE, stride=d) if d > 1 else pl.ds(pl.multiple_of(a, Q_TILE), Q_TILE)
    k_rows = pl.ds(KV_PAD + r + d * (a - BAND), K_TILE, stride=d) if d > 1 else pl.ds(
        pl.multiple_of(KV_PAD + a - BAND, BAND), K_TILE)
    return a, q_rows, k_rows


def _to_quarters(dst, src, n, dst_off=0):
    for r in range(4):
        dst[pl.ds(dst_off + r * (n // 4), n // 4), :] = src[pl.ds(r, n // 4, stride=4), :]


def _quarter_tile_rows(t, S):
    L = S // 16
    per_class = L // Q_TILE
    blk, tt = t // (4 * per_class), t % (4 * per_class)
    r, a = tt // per_class, (tt % per_class) * Q_TILE
    q_rows = pl.ds(blk * (S // 4) + r + 4 * a, Q_TILE, stride=4)
    k_rows = pl.ds(KV_PAD + blk * (S // 4) + r + 4 * (a - BAND), K_TILE, stride=4)
    return a, q_rows, k_rows


def _quarter_band_rows(t, S):
    L = S // 4
    per_quarter = L // Q_TILE
    blk, a = t // per_quarter, (t % per_quarter) * Q_TILE
    q_rows = pl.ds(pl.multiple_of(blk * L + a, Q_TILE), Q_TILE)
    k_rows = pl.ds(pl.multiple_of(KV_PAD + blk * L + a - BAND, BAND), K_TILE)
    return a, q_rows, k_rows


def _lanes(x, width):
    return jnp.concatenate([x] * (width // HEAD_DIM), axis=-1)


_BNT = (((2,), (2,)), ((0,), (0,)))
_BNN = (((2,), (1,)), ((0,), (0,)))
_BTN = (((1,), (1,)), ((0,), (0,)))


def _bdot(a, b, dims):
    return lax.dot_general(a, b, dims, preferred_element_type=F32)


def _stacked(rows, loaders):
    return [jnp.stack([f(*r) for r in rows]) for f in loaders]


def _edge_mask(a, L):
    lk = lax.broadcasted_iota(jnp.int32, (1, K_TILE), 1) + (a - BAND)
    return jnp.where((lk >= 0) & (lk < L), 0.0, NEG).astype(F32)


def _fill_padded(dst, src, S):
    dst[pl.ds(0, KV_PAD), :] = jnp.zeros((KV_PAD, HEAD_DIM), F32)
    dst[pl.ds(KV_PAD + S, KV_PAD), :] = jnp.zeros((KV_PAD, HEAD_DIM), F32)
    dst[pl.ds(KV_PAD, S), :] = src[...]


def _head_specs(S, groups, n_heads):
    return [pl.BlockSpec((S, HEAD_DIM), functools.partial(lambda h, g: (0, g * n_heads + h), g=g)) for g in groups]


def _attn_fwd(proj, bias):
    S = proj.shape[0]
    H = proj.shape[1] // (N_GROUPS * HEAD_DIM)
    scale = HEAD_DIM ** -0.5

    def body(q_ref, k_ref, v_ref, b_ref, o_ref, lse_ref, kp, vp, m_run, l_run, q4, m3, l3, acc3):
        _fill_padded(kp, k_ref, S)
        _fill_padded(vp, v_ref, S)
        o_ref[...] = jnp.zeros_like(o_ref)
        m_run[...] = jnp.full(m_run.shape, NEG, F32)
        l_run[...] = jnp.zeros_like(l_run)
        def online(n_tiles, tile_rows, p, L, q_src, m_buf, l_buf, o_buf):
            def tiles(t, carry):
                rows = [tile_rows(t + u * (n_tiles // TILE_GROUP)) for u in range(TILE_GROUP)]
                qs, ks, vs, m_old, l_old, o_old, edge = _stacked(rows, (
                    lambda a, qr, kr: q_src[qr, :].astype(BF16), lambda a, qr, kr: kp[kr, :].astype(BF16),
                    lambda a, qr, kr: vp[kr, :].astype(BF16), lambda a, qr, kr: m_buf[qr, :],
                    lambda a, qr, kr: l_buf[qr, :], lambda a, qr, kr: o_buf[qr, :], lambda a, qr, kr: _edge_mask(a, L)))
                s = _bdot(qs, ks, _BNT) * scale + b_ref[0, p][None] + edge
                m_new = jnp.maximum(m_old, jnp.max(s, axis=-1, keepdims=True))
                pr = jnp.exp(s - _lanes(m_new, K_TILE)).astype(BF16)
                alpha = jnp.exp(m_old - m_new)
                l_new = alpha * l_old + _bdot(pr, jnp.ones((TILE_GROUP, K_TILE, HEAD_DIM), BF16), _BNN)
                o_new = alpha * o_old + _bdot(pr, vs, _BNN)
                for u, (_, qr, _) in enumerate(rows):
                    o_buf[qr, :] = o_new[u]
                    m_buf[qr, :] = m_new[u]
                    l_buf[qr, :] = l_new[u]
                return carry

            lax.fori_loop(0, n_tiles // TILE_GROUP, tiles, 0)

        L, per_class, n_tiles = _attn_tiles(S, DILATIONS[0])
        online(n_tiles, functools.partial(_tile_rows, d=DILATIONS[0], per_class=per_class), 0, L, q_ref, m_run, l_run, o_ref)

        _to_quarters(q4, q_ref, S)
        _to_quarters(kp, k_ref, S, KV_PAD)
        _to_quarters(vp, v_ref, S, KV_PAD)
        n_tiles = _attn_tiles(S, DILATIONS[2])[2]

        def tiles3(t, carry):
            rows = [_quarter_tile_rows(t + u * (n_tiles // TILE_GROUP), S) for u in range(TILE_GROUP)]
            qs, ks, vs, edge = _stacked(rows, (
                lambda a, qr, kr: q4[qr, :].astype(BF16), lambda a, qr, kr: kp[kr, :].astype(BF16),
                lambda a, qr, kr: vp[kr, :].astype(BF16), lambda a, qr, kr: _edge_mask(a, S // DILATIONS[2])))
            s = _bdot(qs, ks, _BNT) * scale + b_ref[0, 2][None] + edge
            m_new = jnp.broadcast_to(jnp.max(s, axis=-1, keepdims=True), (TILE_GROUP, Q_TILE, HEAD_DIM))
            pr = jnp.exp(s - _lanes(m_new, K_TILE)).astype(BF16)
            l_new = _bdot(pr, jnp.ones((TILE_GROUP, K_TILE, HEAD_DIM), BF16), _BNN)
            o_new = _bdot(pr, vs, _BNN)
            for u, (_, qr, _) in enumerate(rows):
                acc3[qr, :] = o_new[u]
                m3[qr, :] = m_new[u]
                l3[qr, :] = l_new[u]
            return carry

        lax.fori_loop(0, n_tiles // TILE_GROUP, tiles3, 0)
        online(_attn_tiles(S, DILATIONS[1])[2], functools.partial(_quarter_band_rows, S=S), 1, S // DILATIONS[1],
               q4, m3, l3, acc3)
        for r in range(4):
            nat, qtr = pl.ds(r, S // 4, stride=4), pl.ds(r * (S // 4), S // 4)
            m_a, m_b = m_run[nat, :], m3[qtr, :]
            m = jnp.maximum(m_a, m_b)
            w_a, w_b = jnp.exp(m_a - m), jnp.exp(m_b - m)
            l = w_a * l_run[nat, :] + w_b * l3[qtr, :]
            o_ref[nat, :] = (w_a * o_ref[nat, :] + w_b * acc3[qtr, :]) / l
            lse_ref[nat, :] = m + jnp.log(l)

    hspec = pl.BlockSpec((S, HEAD_DIM), lambda h: (0, h))
    padded, plain = pltpu.VMEM((S + 2 * KV_PAD, HEAD_DIM), F32), pltpu.VMEM((S, HEAD_DIM), F32)
    return pl.pallas_call(
        body, grid=(H,), name="attn_fwd",
        in_specs=_head_specs(S, (0, 1, 2), H) + [
            pl.BlockSpec((1, len(DILATIONS), Q_TILE, K_TILE), lambda h: (h, 0, 0, 0))],
        out_specs=[hspec, hspec],
        out_shape=[SDS((S, H * HEAD_DIM), F32), SDS((S, H * HEAD_DIM), F32)],
        scratch_shapes=[padded, padded] + [plain] * 6,
        compiler_params=_cp(1))(proj, proj, proj, bias)


def _put_groups(stage, dproj, groups, n_heads, sems):
    h = pl.program_id(0)
    copies = [pltpu.make_async_copy(
        stage.at[i], dproj.at[:, pl.ds(pl.multiple_of((g * n_heads + h) * HEAD_DIM, HEAD_DIM), HEAD_DIM)], sems.at[i])
        for i, g in enumerate(groups)]
    for cp in copies:
        cp.start()
    for cp in copies:
        cp.wait()


def _attn_bwd(proj, out, lse, dmix, bias, dproj):
    S = proj.shape[0]
    H = proj.shape[1] // (N_GROUPS * HEAD_DIM)
    scale = HEAD_DIM ** -0.5
    assert S // DILATIONS[2] >= 2 * Q_TILE

    def body(q_ref, k_ref, v_ref, o_ref, lse_ref, do_ref, b_ref, dproj_in, dproj_out,
             kp, vp, dkp, dvp, dsum, q4, do4, lse4, dsum4, dq_ref, dk_ref, dv_ref, stage, sems):
        _fill_padded(kp, k_ref, S)
        _fill_padded(vp, v_ref, S)
        dkp[...] = jnp.zeros_like(dkp)
        dvp[...] = jnp.zeros_like(dvp)
        dq_ref[...] = jnp.zeros_like(dq_ref)
        dsum[...] = jnp.broadcast_to(jnp.sum(do_ref[...] * o_ref[...], axis=-1, keepdims=True), dsum.shape)

        def run(n_tiles, tile_rows, p, L, q_src, do_src, lse_src, dsum_src, dq_dst, dq_adds):
            def tiles(t, carry):
                rows = [tile_rows(t + u * (n_tiles // BWD_TILE_GROUP)) for u in range(BWD_TILE_GROUP)]
                qs, ks, vs, dos, lses, dsums, dk_old, dv_old, edge = _stacked(rows, (
                    lambda a, qr, kr: q_src[qr, :].astype(BF16), lambda a, qr, kr: kp[kr, :].astype(BF16),
                    lambda a, qr, kr: vp[kr, :].astype(BF16), lambda a, qr, kr: do_src[qr, :].astype(BF16),
                    lambda a, qr, kr: lse_src[qr, :], lambda a, qr, kr: dsum_src[qr, :],
                    lambda a, qr, kr: dkp[kr, :], lambda a, qr, kr: dvp[kr, :], lambda a, qr, kr: _edge_mask(a, L)))
                s = _bdot(qs, ks, _BNT) * scale + b_ref[0, p][None] + edge
                pr = jnp.exp(s - _lanes(lses, K_TILE))
                ds = (pr * (_bdot(dos, vs, _BNT) - _lanes(dsums, K_TILE)) * scale).astype(BF16)
                dq_new = _bdot(ds, ks, _BNN)
                if dq_adds:
                    dq_new = dq_new + jnp.stack([dq_dst[qr, :] for _, qr, _ in rows])
                dk_new = dk_old + _bdot(ds, qs, _BTN)
                dv_new = dv_old + _bdot(pr.astype(BF16), dos, _BTN)
                for u, (_, qr, kr) in enumerate(rows):
                    dq_dst[qr, :] = dq_new[u]
                    dkp[kr, :] = dk_new[u]
                    dvp[kr, :] = dv_new[u]
                return carry

            lax.fori_loop(0, n_tiles // BWD_TILE_GROUP, tiles, 0)

        L, per_class, n_tiles = _attn_tiles(S, DILATIONS[0])
        run(n_tiles, functools.partial(_tile_rows, d=DILATIONS[0], per_class=per_class), 0, L,
            q_ref, do_ref, lse_ref, dsum, dq_ref, True)
        dk_ref[...] = dkp[pl.ds(KV_PAD, S), :]
        dv_ref[...] = dvp[pl.ds(KV_PAD, S), :]

        for dst, src in ((q4, q_ref), (do4, do_ref), (lse4, lse_ref), (dsum4, dsum)):
            _to_quarters(dst, src, S)
        _to_quarters(kp, k_ref, S, KV_PAD)
        _to_quarters(vp, v_ref, S, KV_PAD)
        dkp[...] = jnp.zeros_like(dkp)
        dvp[...] = jnp.zeros_like(dvp)
        dq3 = dsum
        run(_attn_tiles(S, DILATIONS[2])[2], functools.partial(_quarter_tile_rows, S=S), 2, S // DILATIONS[2],
            q4, do4, lse4, dsum4, dq3, False)
        run(_attn_tiles(S, DILATIONS[1])[2], functools.partial(_quarter_band_rows, S=S), 1, S // DILATIONS[1],
            q4, do4, lse4, dsum4, dq3, True)
        for r in range(4):
            nat, qtr = pl.ds(r, S // 4, stride=4), pl.ds(r * (S // 4), S // 4)
            pad_qtr = pl.ds(KV_PAD + r * (S // 4), S // 4)
            dq_ref[nat, :] = dq_ref[nat, :] + dq3[qtr, :]
            dk_ref[nat, :] = dk_ref[nat, :] + dkp[pad_qtr, :]
            dv_ref[nat, :] = dv_ref[nat, :] + dvp[pad_qtr, :]
        for i, acc in enumerate((dq_ref, dk_ref, dv_ref)):
            stage[i] = acc[...].astype(BF16)
        _put_groups(stage, dproj_out, (0, 1, 2), H, sems)

    hspec = pl.BlockSpec((S, HEAD_DIM), lambda h: (0, h))
    once = pl.BlockSpec((S, HEAD_DIM), lambda h: (0, h), pipeline_mode=pl.Buffered(1))
    padded, plain = pltpu.VMEM((S + 2 * KV_PAD, HEAD_DIM), F32), pltpu.VMEM((S, HEAD_DIM), F32)
    return pl.pallas_call(
        body, grid=(H,), name="attn_bwd",
        in_specs=_head_specs(S, (0, 1, 2), H) + [
            once, hspec, hspec, pl.BlockSpec((1, len(DILATIONS), Q_TILE, K_TILE), lambda h: (h, 0, 0, 0)), ANY],
        out_specs=ANY, out_shape=SDS(dproj.shape, dproj.dtype), input_output_aliases={7: 0},
        scratch_shapes=[padded] * 4 + [plain] * 8 + [pltpu.VMEM((3, S, HEAD_DIM), BF16), pltpu.SemaphoreType.DMA((3,))],
        compiler_params=_cp(1))(proj, proj, proj, out, lse, dmix, bias, dproj)


def _ret_consts(lg, forward):
    C = RET_CHUNK
    i = lax.broadcasted_iota(jnp.int32, (C, C), 0)
    j = lax.broadcasted_iota(jnp.int32, (C, C), 1)
    rel = (i - j) if forward else (j - i)
    inside = (rel >= 0) if forward else (rel > 0)
    relf = jnp.maximum(rel, 0).astype(F32)
    mask = jnp.where(inside, jnp.exp(lg * relf), 0.0)
    idx = lax.broadcasted_iota(jnp.int32, (C, 1), 0).astype(F32)
    q_exp = (idx + 1.0) if forward else (C - idx)
    k_exp = (C - 1.0 - idx) if forward else idx
    return mask, relf, jnp.exp(lg * q_exp), q_exp, jnp.exp(lg * k_exp), k_exp, jnp.exp(lg * C)


def _log_decay(dec_ref, h):
    return -jnp.exp(jnp.full((1, 1), dec_ref[0, h], F32))


FFN_BLOCK = 704
CHUNK_BATCH = 8


def _batch_rows(b):
    n = CHUNK_BATCH * RET_CHUNK
    return pl.ds(pl.multiple_of(b * n, n), n)


def _batch_chunks(b):
    return pl.ds(pl.multiple_of(b * CHUNK_BATCH, CHUNK_BATCH), CHUNK_BATCH)


def _chunks3(x):
    return x.reshape(CHUNK_BATCH, RET_CHUNK, HEAD_DIM)


def _ret_scan(buf, c_decs, nc, reverse):
    def step(n, carry):
        new = []
        for way, r in enumerate(carry):
            c = n if (way == 0) != reverse else nc - 1 - n
            term = buf[way, c]
            buf[way, c] = r
            new.append(r * c_decs[way] + term)
        return tuple(new)

    lax.fori_loop(0, nc, step, (jnp.zeros((HEAD_DIM, HEAD_DIM), F32),) * 2)


def _ret_fwd(proj, dec_f, dec_b, w_norm):
    S = proj.shape[0]
    H = proj.shape[1] // (N_GROUPS * HEAD_DIM)
    nc = S // RET_CHUNK
    scale = HEAD_DIM ** -0.5

    def body(df_ref, db_ref, q_ref, k_ref, v_ref, g_ref, w_ref, y_ref, o_ref, states):
        h = pl.program_id(0)
        consts = [_ret_consts(_log_decay(dref, h), fw) for fw, dref in ((True, df_ref), (False, db_ref))]

        def kv_step(b, carry):
            rows, batch = _batch_rows(b), _batch_chunks(b)
            k3 = _chunks3(k_ref[rows, :])
            v3 = _chunks3(v_ref[rows, :]).astype(BF16)
            for way in range(2):
                states[way, batch] = _bdot((k3 * consts[way][4]).astype(BF16), v3, _BTN)
            return carry

        lax.fori_loop(0, nc // CHUNK_BATCH, kv_step, 0)
        _ret_scan(states, [c[6] for c in consts], nc, False)

        def out_step(b, carry):
            rows, batch = _batch_rows(b), _batch_chunks(b)
            q3 = _chunks3(q_ref[rows, :] * scale)
            k3 = _chunks3(k_ref[rows, :]).astype(BF16)
            v3 = _chunks3(v_ref[rows, :]).astype(BF16)
            a0 = _bdot(q3.astype(BF16), k3, _BNT)
            o = None
            for way in range(2):
                mask, q_dec = consts[way][0], consts[way][2]
                part = _bdot((a0 * mask).astype(BF16), v3, _BNN) \
                    + _bdot((q3 * q_dec).astype(BF16), states[way, batch].astype(BF16), _BNN)
                o = part if o is None else o + part
            o_ref[rows, :] = o.reshape(CHUNK_BATCH * RET_CHUNK, HEAD_DIM)
            return carry

        lax.fori_loop(0, nc // CHUNK_BATCH, out_step, 0)
        o = o_ref[...]
        g = g_ref[...]
        y_ref[...] = o * _rms_scale(o) * w_ref[...] * (g * _sigmoid(g))

    hspec = pl.BlockSpec((S, HEAD_DIM), lambda h: (0, h))
    smem = pl.BlockSpec(memory_space=pltpu.SMEM)
    return pl.pallas_call(
        body, grid=(H,), name="ret_fwd",
        in_specs=[smem, smem] + _head_specs(S, (3, 4, 5, 6), H) + [pl.BlockSpec((1, HEAD_DIM), lambda h: (0, h))],
        out_specs=[hspec, hspec],
        out_shape=[SDS((S, H * HEAD_DIM), F32)] * 2,
        scratch_shapes=[pltpu.VMEM((2, nc, HEAD_DIM, HEAD_DIM), F32)],
        compiler_params=_cp(1))(dec_f, dec_b, proj, proj, proj, proj, w_norm)


def _ret_gate_bwd(proj, o_raw, dmix, w_norm, col0, dproj):
    S = proj.shape[0]
    H = proj.shape[1] // (N_GROUPS * HEAD_DIM)

    def body(g_ref, o_ref, dy_ref, w_ref, dproj_in, do_ref, dw_ref, dproj_out, dg_ref, sems):
        o = o_ref[...]
        g = g_ref[...]
        dy = dy_ref[...]
        w = w_ref[...]
        rr = _rms_scale(o)
        normed = o * rr
        sg = _sigmoid(g)
        silu = g * sg
        dw_ref[0] = jnp.broadcast_to(jnp.sum(dy * normed * silu, axis=0, keepdims=True), (8, HEAD_DIM))
        dg_ref[0] = (dy * normed * w * (sg * (1.0 + g * (1.0 - sg)))).astype(BF16)
        dnormed = dy * w * silu
        do_ref[...] = rr * dnormed - o * (rr * rr * rr) * jnp.mean(dnormed * o, axis=-1, keepdims=True)
        _put_groups(dg_ref, dproj_out, (6,), H, sems)

    hspec = pl.BlockSpec((S, HEAD_DIM), lambda h: (0, h))
    nh0 = col0 // HEAD_DIM
    return pl.pallas_call(
        body, grid=(H,), name="ret_gate_bwd",
        in_specs=_head_specs(S, (6,), H) + [hspec, pl.BlockSpec((S, HEAD_DIM), lambda h: (0, nh0 + h)),
                                            pl.BlockSpec((1, HEAD_DIM), lambda h: (0, h)), ANY],
        out_specs=[hspec, pl.BlockSpec((1, 8, HEAD_DIM), lambda h: (h, 0, 0)), ANY],
        out_shape=[SDS((S, H * HEAD_DIM), F32), SDS((H, 8, HEAD_DIM), F32), SDS(dproj.shape, dproj.dtype)],
        input_output_aliases={4: 2},
        scratch_shapes=[pltpu.VMEM((1, S, HEAD_DIM), BF16), pltpu.SemaphoreType.DMA((1,))],
        compiler_params=_cp(1))(proj, o_raw, dmix, w_norm, dproj)


def _ret_bwd(proj, d_out, dec_f, dec_b, dproj):
    S = proj.shape[0]
    H = proj.shape[1] // (N_GROUPS * HEAD_DIM)
    C = RET_CHUNK
    nc = S // C
    scale = HEAD_DIM ** -0.5

    def body(df_ref, db_ref, q_ref, k_ref, v_ref, do, dproj_in, small_ref, dproj_out, states, d_states, stage, sems):
        h = pl.program_id(0)
        lgs = [_log_decay(df_ref, h), _log_decay(db_ref, h)]
        consts = [_ret_consts(lg, fw) for lg, fw in zip(lgs, (True, False))]

        def prep_step(b, carry):
            rows, batch = _batch_rows(b), _batch_chunks(b)
            q3 = _chunks3(q_ref[rows, :] * scale)
            k3 = _chunks3(k_ref[rows, :])
            v3 = _chunks3(v_ref[rows, :]).astype(BF16)
            do3 = _chunks3(do[rows, :]).astype(BF16)
            for way in range(2):
                states[way, batch] = _bdot((k3 * consts[way][4]).astype(BF16), v3, _BTN)
                d_states[way, batch] = _bdot((q3 * consts[way][2]).astype(BF16), do3, _BTN)
            return carry

        lax.fori_loop(0, nc // CHUNK_BATCH, prep_step, 0)
        c_decs = [c[6] for c in consts]
        _ret_scan(states, c_decs, nc, False)
        _ret_scan(d_states, c_decs, nc, True)

        def main_step(b, dlams):
            rows, batch = _batch_rows(b), _batch_chunks(b)
            q3 = _chunks3(q_ref[rows, :] * scale)
            k3 = _chunks3(k_ref[rows, :])
            q3b, k3b = q3.astype(BF16), k3.astype(BF16)
            v3b = _chunks3(v_ref[rows, :]).astype(BF16)
            do3b = _chunks3(do[rows, :]).astype(BF16)
            a0 = _bdot(q3b, k3b, _BNT)
            pv = _bdot(do3b, v3b, _BNT)
            dq = dk = dv = None
            new_dlams = []
            for way in range(2):
                mask, relf, q_dec, q_exp, k_dec, k_exp, c_dec = consts[way]
                state, d_state = states[way, batch], d_states[way, batch]
                dp = pv * mask
                dpb = dp.astype(BF16)
                gq = _bdot(do3b, state.astype(BF16), _BNT)
                gk = _bdot(v3b, d_state.astype(BF16), _BNT)
                parts = (_bdot(dpb, k3b, _BNN) + q_dec * gq, _bdot(dpb, q3b, _BTN) + k_dec * gk,
                         _bdot((a0 * mask).astype(BF16), do3b, _BTN)
                         + _bdot((k3 * k_dec).astype(BF16), d_state.astype(BF16), _BNN))
                dq, dk, dv = parts if dq is None else (dq + parts[0], dk + parts[1], dv + parts[2])
                total = lambda x: jnp.sum(jnp.sum(x, axis=0), axis=0, keepdims=True)
                new_dlams.append(dlams[way] + total(relf * a0 * dp)
                                 + total(q_exp * q_dec * q3 * gq + k_exp * k_dec * k3 * gk)
                                 + (C * c_dec) * total(state * d_state))
            flat = lambda x: x.reshape(CHUNK_BATCH * C, HEAD_DIM)
            stage[0, rows, :] = (flat(dq) * scale).astype(BF16)
            stage[1, rows, :] = flat(dk).astype(BF16)
            stage[2, rows, :] = flat(dv).astype(BF16)
            return tuple(new_dlams)

        dlams = lax.fori_loop(0, nc // CHUNK_BATCH, main_step, (jnp.zeros((1, HEAD_DIM), F32),) * 2)
        for row, (dlam, lg) in enumerate(zip(dlams, lgs)):
            small_ref[0, pl.ds(row, 1), :] = jnp.broadcast_to(jnp.sum(dlam, axis=-1, keepdims=True) * lg, (1, HEAD_DIM))
        small_ref[0, pl.ds(2, 6), :] = jnp.zeros((6, HEAD_DIM), F32)
        _put_groups(stage, dproj_out, (3, 4, 5), H, sems)

    hspec = pl.BlockSpec((S, HEAD_DIM), lambda h: (0, h))
    smem = pl.BlockSpec(memory_space=pltpu.SMEM)
    return pl.pallas_call(
        body, grid=(H,), name="ret_bwd",
        in_specs=[smem, smem] + _head_specs(S, (3, 4, 5), H) + [hspec, ANY],
        out_specs=[pl.BlockSpec((1, 8, HEAD_DIM), lambda h: (h, 0, 0)), ANY],
        out_shape=[SDS((H, 8, HEAD_DIM), F32), SDS(dproj.shape, dproj.dtype)], input_output_aliases={6: 1},
        scratch_shapes=[pltpu.VMEM((2, nc, HEAD_DIM, HEAD_DIM), F32), pltpu.VMEM((2, nc, HEAD_DIM, HEAD_DIM), F32),
                        pltpu.VMEM((3, S, HEAD_DIM), BF16), pltpu.SemaphoreType.DMA((3,))],
        compiler_params=_cp(1))(dec_f, dec_b, proj, proj, proj, d_out, dproj)


def _ffn_bwd_act(dh2, wd, g, u):
    S, D = dh2.shape
    nblk, _, FB = g.shape
    tm = min(1024, S)

    def body(dh_ref, wd_ref, g_ref, u_ref, dg_ref, du_ref):
        dact = _dot(dh_ref[...], wd_ref[...], _NT)
        gg = g_ref[0].astype(F32)
        sg = _sigmoid(gg)
        dg_ref[0] = (dact * u_ref[0].astype(F32) * (sg * (1.0 + gg * (1.0 - sg)))).astype(BF16)
        du_ref[0] = (dact * (gg * sg)).astype(BF16)

    blk = pl.BlockSpec((1, tm, FB), lambda j, i: (j, i, 0))
    return pl.pallas_call(
        body, grid=(nblk, S // tm), name="ffn_bwd_act",
        in_specs=[pl.BlockSpec((tm, D), lambda j, i: (i, 0)), pl.BlockSpec((FB, D), lambda j, i: (j, 0)), blk, blk],
        out_specs=[blk, blk], out_shape=[SDS((nblk, S, FB), BF16)] * 2,
        compiler_params=_cp(2))(dh2, wd, g, u)


def _ffn_bwd_in(dg, du, wg, wu, h1, dh2, w_norm):
    nblk, S, FB = dg.shape
    D = h1.shape[1]
    tm = min(RESIDENT_ROWS, S)
    blk = pl.BlockSpec((nblk, tm, FB), lambda i: (0, i, 0))
    row = pl.BlockSpec((tm, D), lambda i: (i, 0))
    vec = pl.BlockSpec((1, D), lambda i: (0, 0))

    def gate_body(dg_ref, wg_ref, part_ref):
        part_ref[...] = _blocked_matmul(dg_ref, wg_ref)

    part = pl.pallas_call(
        gate_body, grid=(S // tm,), name="ffn_bwd_in_gate", in_specs=[blk, _resident((nblk * FB, D))],
        out_specs=row, out_shape=SDS((S, D), F32), compiler_params=_cp(1))(dg, wg.reshape(nblk * FB, D))

    def body(du_ref, wu_ref, part_ref, h_ref, dh2_ref, wn_ref, dh_ref, dhb_ref, dw_ref):
        @pl.when(pl.program_id(0) == 0)
        def _():
            dw_ref[...] = jnp.zeros_like(dw_ref)

        dh, dw = _rms_bwd(part_ref[...] + _blocked_matmul(du_ref, wu_ref), h_ref[...], wn_ref[...])
        dh = dh2_ref[...] + dh
        dh_ref[...] = dh
        dhb_ref[...] = dh.astype(BF16)
        dw_ref[...] += dw

    return pl.pallas_call(
        body, grid=(S // tm,), name="ffn_bwd_in",
        in_specs=[blk, _resident((nblk * FB, D)), row, row, row, vec],
        out_specs=[row, row, vec], out_shape=[SDS((S, D), F32), SDS((S, D), BF16), SDS((1, D), F32)],
        compiler_params=_cp(1))(du, wu.reshape(nblk * FB, D), part, h1, dh2, w_norm)


def _dmix(dh1, w_out):
    S, D = dh1.shape
    tm = min(512, S)

    def body(dh_ref, w_ref, o_ref):
        o_ref[...] = _dot(dh_ref[...], w_ref[...], _NT)

    row = pl.BlockSpec((tm, D), lambda i: (i, 0))
    return pl.pallas_call(
        body, grid=(S // tm,), name="dmix", in_specs=[row, pl.BlockSpec((D, D), lambda i: (0, 0))],
        out_specs=row, out_shape=SDS((S, D), F32), compiler_params=_cp(1))(dh1, w_out)


def _in_bwd(dproj, w_blk, x, dh1, w_norm):
    S, D = x.shape
    nblk, _, NB = w_blk.shape
    tm = min(RESIDENT_ROWS, S)

    def body(dp_ref, w_ref, x_ref, dh1_ref, wn_ref, dx_ref, dw_ref):
        @pl.when(pl.program_id(0) == 0)
        def _():
            dw_ref[...] = jnp.zeros_like(dw_ref)

        dn = None
        for j in range(nblk):
            part = _dot(dp_ref[:, pl.ds(j * NB, NB)], w_ref[j], _NT)
            dn = part if dn is None else dn + part
        dh, dw = _rms_bwd(dn, x_ref[...], wn_ref[...])
        dx_ref[...] = dh1_ref[...] + dh
        dw_ref[...] += dw

    row = pl.BlockSpec((tm, D), lambda i: (i, 0))
    vec = pl.BlockSpec((1, D), lambda i: (0, 0))
    return pl.pallas_call(
        body, grid=(S // tm,), name="in_bwd",
        in_specs=[pl.BlockSpec((tm, nblk * NB), lambda i: (i, 0)),
                  pl.BlockSpec((nblk, D, NB), lambda i: (0, 0, 0), pipeline_mode=pl.Buffered(1)), row, row, vec],
        out_specs=[row, vec], out_shape=[SDS((S, D), F32), SDS((1, D), F32)],
        compiler_params=_cp(1))(dproj, w_blk, x, dh1, w_norm)


def _wgrad(a, b, a_spec, b_spec, o_spec, o_shape, grid, name):
    nk = grid[-1]

    def ld(ref):
        return ref[0] if len(ref.shape) == 3 else ref[...]

    def body(a_ref, b_ref, o_ref, acc):
        k = pl.program_id(len(grid) - 1)

        @pl.when(k == 0)
        def _():
            acc[...] = jnp.zeros_like(acc)

        acc[...] += _dot(ld(a_ref), ld(b_ref), _TN)

        @pl.when(k == nk - 1)
        def _():
            if len(o_ref.shape) == 3:
                o_ref[0] = acc[...].astype(o_ref.dtype)
            else:
                o_ref[...] = acc[...].astype(o_ref.dtype)

    return pl.pallas_call(
        body, grid=grid, name=name, in_specs=[a_spec, b_spec], out_specs=o_spec, out_shape=SDS(o_shape, BF16),
        scratch_shapes=[pltpu.VMEM(o_spec.block_shape[-2:], F32)], compiler_params=_cp(len(grid)))(a, b)


def _peer(k):
    x, y, c = lax.axis_index("x"), lax.axis_index("y"), lax.axis_index("c")
    px = 1 - x if k & 4 else x
    py = 1 - y if k & 2 else y
    pc = 1 - c if k & 1 else c
    return (px, py, pc), 4 * px + 2 * py + pc


def _exchange_copies(srcs, lands, send_sems, recv_sems, which, gather):
    _, me = _peer(0)
    pairs = []
    for pos, a in enumerate(which):
        for k in range(1, N_DEV):
            dev, idx = _peer(k)
            sem = pos * (N_DEV - 1) + k - 1
            src = srcs[a] if gather else srcs[a].at[idx]
            mk = functools.partial(pltpu.make_async_remote_copy, src_ref=src, send_sem=send_sems.at[sem],
                                   recv_sem=recv_sems.at[sem], device_id=dev, device_id_type=MESH)
            pairs.append((mk(dst_ref=lands[a].at[me]), mk(dst_ref=lands[a].at[idx])))
    return pairs


def _sequencer_kernel(name, collective_id, n_remote, n_local):
    return pl.kernel(mesh=plsc.ScalarSubcoreMesh(axis_name="sequencer", num_cores=1), name=name,
                     scratch_types=(pltpu.SemaphoreType.DMA((n_remote,)), pltpu.SemaphoreType.DMA((n_remote,)),
                                    pltpu.SemaphoreType.DMA((n_local,))),
                     compiler_params=pltpu.CompilerParams(collective_id=collective_id))


def _handshake(ks):
    barrier = pltpu.get_barrier_semaphore()
    for k in ks:
        pl.semaphore_signal(barrier, inc=1, device_id=_peer(k)[0], device_id_type=MESH)
    pl.semaphore_wait(barrier, len(ks))


def _sequencer_scatter(arrays, name, collective_id):
    n = len(arrays)
    hbm = pltpu.MemorySpace.HBM
    srcs = [jax.new_ref(a, memory_space=hbm) for a in arrays]
    lands = [jax.empty_ref(SDS(a.shape, a.dtype), memory_space=hbm) for a in arrays]

    @_sequencer_kernel(name, collective_id, n * (N_DEV - 1), n)
    def launch(send_sems, recv_sems, local_sems):
        _handshake(range(1, N_DEV))
        _, me = _peer(0)
        local = [pltpu.make_async_copy(srcs[a].at[me], lands[a].at[me], local_sems.at[a]) for a in range(n)]
        pairs = _exchange_copies(srcs, lands, send_sems, recv_sems, range(n), False)
        for out, _ in pairs:
            out.start()
        for cp in local:
            cp.start()
        for out, arrival in pairs:
            out.wait_send()
            arrival.wait_recv()
        for cp in local:
            cp.wait()

    launch()
    return [r[...] for r in lands]


SIBLING = 1
OTHER_CHIPS = (2, 4, 6)


def _sequencer_gather(arrays, name, collective_id):
    n = len(arrays)
    hbm = pltpu.MemorySpace.HBM
    srcs = [jax.new_ref(a, memory_space=hbm) for a in arrays]
    lands = [jax.empty_ref(SDS((N_DEV,) + a.shape, a.dtype), memory_space=hbm) for a in arrays]

    @_sequencer_kernel(name, collective_id, n * (N_DEV - 1), n)
    def launch(send_sems, recv_sems, local_sems):
        _handshake((SIBLING,) + OTHER_CHIPS)
        _, me = _peer(0)
        sibling, _ = _peer(SIBLING)

        def copy(a, k, src, block, to):
            sem = a * (N_DEV - 1) + k - 1
            return pltpu.make_async_remote_copy(src_ref=src, dst_ref=lands[a].at[block], send_sem=send_sems.at[sem],
                                                recv_sem=recv_sems.at[sem], device_id=to, device_id_type=MESH)

        local = [pltpu.make_async_copy(srcs[a], lands[a].at[me], local_sems.at[a]) for a in range(n)]
        first = [copy(a, k, srcs[a], me, _peer(k)[0]) for a in range(n) for k in OTHER_CHIPS + (SIBLING,)]
        for cp in first + local:
            cp.start()
        passed = []
        for a in range(n):
            for k in OTHER_CHIPS:
                _, block = _peer(k)
                copy(a, k, srcs[a], block, sibling).wait_recv()
                passed.append(copy(a, k ^ SIBLING, lands[a].at[block], block, sibling))
                passed[-1].start()
        for a in range(n):
            for k in (SIBLING,) + tuple(k ^ SIBLING for k in OTHER_CHIPS):
                copy(a, k, srcs[a], _peer(k)[1], sibling).wait_recv()
        for cp in first + passed:
            cp.wait_send()
        for cp in local:
            cp.wait()

    launch()
    return [r[...] for r in lands]


def _sequencer_gather_chips(array, name, collective_id, chips):
    hbm = pltpu.MemorySpace.HBM
    src = jax.new_ref(array, memory_space=hbm)
    land = jax.empty_ref(SDS((2 * len(chips),) + array.shape, array.dtype), memory_space=hbm)

    @_sequencer_kernel(name, collective_id, 2 * len(chips), 1)
    def launch(send_sems, recv_sems, local_sems):
        _handshake((SIBLING,) + tuple(k for k in chips if k))
        c = lax.axis_index("c")
        sibling, _ = _peer(SIBLING)

        def copy(sem, src_ref, slot, to):
            return pltpu.make_async_remote_copy(src_ref=src_ref, dst_ref=land.at[slot], send_sem=send_sems.at[sem],
                                                recv_sem=recv_sems.at[sem], device_id=to, device_id_type=MESH)

        started = []
        for pos, k in enumerate(chips):
            started.append(copy(2 * pos, src, 2 * pos + c, _peer(k)[0] if k else sibling))
            started[-1].start()
        for pos, k in enumerate(chips):
            if k:
                copy(2 * pos, src, 2 * pos + c, sibling).wait_recv()
                started.append(copy(2 * pos + 1, land.at[2 * pos + c], 2 * pos + c, sibling))
                started[-1].start()
        for pos, k in enumerate(chips):
            copy(2 * pos + 1 if k else 2 * pos, src, 2 * pos + 1 - c, sibling).wait_recv()
        for cp in started:
            cp.wait_send()

    launch()
    return land[...]


SMALL_ROWS = 64


def _small_step(part, w, m, v):
    def body(p_ref, w_ref, m_ref, v_ref, g_ref, d_ref, nm_ref, nv_ref, gath, send_sems, recv_sems):
        _, me = _peer(0)
        gath[me] = p_ref[...]
        copies = []
        for k in range(1, N_DEV):
            dev, idx = _peer(k)
            out = pltpu.make_async_remote_copy(src_ref=p_ref, dst_ref=gath.at[me], send_sem=send_sems.at[k - 1],
                                               recv_sem=recv_sems.at[k - 1], device_id=dev, device_id_type=MESH)
            out.start()
            arrival = pltpu.make_async_remote_copy(src_ref=p_ref, dst_ref=gath.at[idx], send_sem=send_sems.at[k - 1],
                                                   recv_sem=recv_sems.at[k - 1], device_id=dev, device_id_type=MESH)
            copies.append((out, arrival))
        for out, arrival in copies:
            out.wait_send()
            arrival.wait_recv()
        g = gath[0]
        for p in range(1, N_DEV):
            g = g + gath[p]
        g_ref[...] = g
        d_ref[...], nm_ref[...], nv_ref[...] = _adamw(w_ref[...], g, m_ref[...], v_ref[...])

    vm = pl.BlockSpec(memory_space=pltpu.VMEM)
    return pl.pallas_call(
        body, name="small_step", in_specs=[vm] * 4, out_specs=[vm] * 4,
        out_shape=[SDS((SMALL_ROWS, 128), F32)] * 4,
        scratch_shapes=[pltpu.VMEM((N_DEV, SMALL_ROWS, 128), F32), pltpu.SemaphoreType.DMA((N_DEV - 1,)),
                        pltpu.SemaphoreType.DMA((N_DEV - 1,))])(part, w, m, v)


def _adamw(w, g, m, v):
    m = ADAM_B1 * m + (1.0 - ADAM_B1) * g
    v = ADAM_B2 * v + (1.0 - ADAM_B2) * (g * g)
    m_hat = m / (1.0 - ADAM_B1 ** ADAM_STEP)
    v_hat = v / (1.0 - ADAM_B2 ** ADAM_STEP)
    delta = -ADAM_LR * (m_hat / (jnp.sqrt(v_hat) + ADAM_EPS) + ADAM_WD * w)
    return delta, m, v


def _adamw_block(parts, w, m, v, name):
    R, C = w.shape
    n_parts = len(parts)
    Rp = R // n_parts
    tr = next(t for t in (256, 128, 64, 32, 16, 8) if Rp % t == 0 and t * C <= 256 * 1024)
    per_part = Rp // tr

    def body(*refs):
        p_refs = refs[:n_parts]
        w_ref, m_ref, v_ref, g_ref, d_ref, nm_ref, nv_ref = refs[n_parts:]
        for k, p_ref in enumerate(p_refs):
            @pl.when(pl.program_id(0) // per_part == k)
            def _(p_ref=p_ref):
                g = p_ref[0].astype(F32)
                for p in range(1, N_DEV):
                    g = g + p_ref[p].astype(F32)
                g_ref[...] = g
                d_ref[...], nm_ref[...], nv_ref[...] = _adamw(w_ref[...], g, m_ref[...], v_ref[...])

    row = pl.BlockSpec((tr, C), lambda i: (i, 0))
    part_specs = [pl.BlockSpec((N_DEV, tr, C), functools.partial(
        lambda i, k: (0, jnp.clip(i - k * per_part, 0, per_part - 1), 0), k=k)) for k in range(n_parts)]
    return pl.pallas_call(
        body, grid=(R // tr,), name=name, in_specs=part_specs + [row, row, row],
        out_specs=[row] * 4, out_shape=[SDS((R, C), F32)] * 4, compiler_params=_cp(1))(*parts, w, m, v)


def _pack_small(mix, ffn, fin, retw, dec_f, dec_b, loss):
    flat = jnp.concatenate([mix.reshape(-1), ffn.reshape(-1), fin.reshape(-1), retw.reshape(-1), dec_f.reshape(-1),
                            dec_b.reshape(-1), loss.reshape(-1)])
    return jnp.pad(flat, (0, SMALL_ROWS * 128 - flat.shape[0])).reshape(SMALL_ROWS, 128)


def _unpack_small(packed, shapes):
    flat = packed.reshape(-1)
    out, at = [], 0
    for s in shapes:
        n = math.prod(s)
        out.append(flat[at:at + n].reshape(s))
        at += n
    return out


def kernel(x, norm_mix_w, w_in, ret_decay_fwd, ret_decay_bwd, ret_norm_w, w_out, norm_ffn_w, w_gate, w_up, w_down, norm_final_w, loss_target, m_norm_mix_w, m_w_in, m_ret_decay_fwd, m_ret_decay_bwd, m_ret_norm_w, m_w_out, m_norm_ffn_w, m_w_gate, m_w_up, m_w_down, m_norm_final_w, v_norm_mix_w, v_w_in, v_ret_decay_fwd, v_ret_decay_bwd, v_ret_norm_w, v_w_out, v_norm_ffn_w, v_w_gate, v_w_up, v_w_down, v_norm_final_w):
    x2 = x[0]
    tgt = loss_target[0]
    S, D = x2.shape
    H = ret_norm_w.shape[1] // HEAD_DIM
    DA = H * HEAD_DIM
    fin_w = norm_final_w.reshape(1, D)
    big = (w_in[0], w_out[0], w_gate[0].T, w_up[0].T, w_down[0])

    big_b = [w.astype(BF16) for w in big]
    stages = ((0,), (4, 2), (6,))
    wi_stages = [_sequencer_gather_chips(big_b[0], name, cid, ks)
                 for name, cid, ks in zip(("gather_in_own", "gather_in_near", "gather_in_far"), (0, 7, 8), stages)]
    wo, = _sequencer_gather(big_b[1:2], "gather_out", 1)
    wg, wu = _sequencer_gather(big_b[2:4], "gather_gate_up", 9)
    wd, = _sequencer_gather(big_b[4:], "gather_down", 5)
    wi, = _sequencer_gather(big_b[:1], "gather_in_ordered", 10)
    NB = big_b[0].shape[1]
    ax, ay = lax.axis_index("x"), lax.axis_index("y")
    chip_of = {k: 2 * (1 - ax if k & 4 else ax) + (1 - ay if k & 2 else ay) for k in (0, 2, 4, 6)}

    n1 = _norm_fwd(x2, norm_mix_w)
    ac = lax.axis_index("c")
    me = 2 * chip_of[0] + ac
    vec = lambda *v: jnp.stack([jnp.asarray(t, jnp.int32) for t in v])
    proj = _proj_part(n1, big_b[0][None], vec(0), vec(me), None, N_DEV, "proj_self")
    for ks, w_st, name in zip(stages, wi_stages, ("proj_sibling", "proj_near", "proj_far")):
        slots, blocks = [], []
        for pos, k in enumerate(ks):
            for core in ((1 - ac,) if k == 0 else (0, 1)):
                slots.append(2 * pos + core)
                blocks.append(2 * chip_of[k] + core)
        proj = _proj_part(n1, w_st, vec(*slots), vec(*blocks), proj, N_DEV, name)
    bias = _attn_bias()[:H]
    attn, lse = _attn_fwd(proj, bias)
    ret, o_raw = _ret_fwd(proj, ret_decay_fwd, ret_decay_bwd, ret_norm_w)
    wo_full = wo.reshape(D, D)
    d_ff = N_DEV * wd.shape[1]
    FB = FFN_BLOCK if d_ff % FFN_BLOCK == 0 else wd.shape[1]
    n_fb = d_ff // FB
    wg, wu = wg.reshape(n_fb, FB, D), wu.reshape(n_fb, FB, D)
    wd_full = wd.reshape(d_ff, D)
    h1, mixed, n2 = _out_fwd(x2, attn, ret, wo_full, norm_ffn_w)
    gate, up, act = _ffn_up(n2, wg, wu)
    dh2, dh2_b, loss_parts, g_fin = _ffn_down_loss(act, wd_full, h1, tgt, fin_w)

    dgate, dup = _ffn_bwd_act(dh2_b, wd_full, gate, up)
    tn = min(1024, D)
    ffn_specs = (pl.BlockSpec((1, S, FB), lambda j, n, k: (j, 0, 0)), pl.BlockSpec((S, tn), lambda j, n, k: (0, n)),
                 pl.BlockSpec((1, FB, tn), lambda j, n, k: (j, 0, n)), (n_fb, FB, D), (n_fb, D // tn, 1))
    per_dev = (N_DEV, d_ff // N_DEV, D)
    g_wd = _wgrad(act, dh2_b, *ffn_specs, "wgrad_down").reshape(per_dev)
    g_wg = _wgrad(dgate, n2, *ffn_specs, "wgrad_gate").reshape(per_dev)
    g_wu = _wgrad(dup, n2, *ffn_specs, "wgrad_up").reshape(per_dev)
    parts_f = _sequencer_scatter([g_wg, g_wu, g_wd], "scatter_ffn", 2)
    dh1, dh1_b, g_ffn = _ffn_bwd_in(dgate, dup, wg, wu, h1, dh2, norm_ffn_w)
    dmix = _dmix(dh1_b, wo_full)
    tmw = min(512, D)
    tk = min(2048, S)
    g_wo = _wgrad(mixed, dh1_b, pl.BlockSpec((tk, tmw), lambda m, k: (k, m)), pl.BlockSpec((tk, D), lambda m, k: (k, 0)),
                  pl.BlockSpec((tmw, D), lambda m, k: (m, 0)), (D, D), (D // tmw, S // tk), "wgrad_out")
    parts_o = _sequencer_scatter([g_wo.reshape(N_DEV, D // N_DEV, D)], "scatter_out", 3)
    d_ret, small_w, dproj = _ret_gate_bwd(proj, o_raw, dmix, ret_norm_w, DA, lax.empty(proj.shape, BF16))
    small, dproj = _ret_bwd(proj, d_ret, ret_decay_fwd, ret_decay_bwd, dproj)
    dproj = _attn_bwd(proj, attn, lse, dmix, bias, dproj)
    half = D // tmw // 2
    parts_i = []
    for part, (name, cid) in enumerate((("in_lo", 4), ("in_hi", 6))):
        g_wi = _wgrad(n1, dproj, pl.BlockSpec((S, tmw), functools.partial(lambda j, m, k, off: (0, m + off), off=part * half)),
                      pl.BlockSpec((S, NB), lambda j, m, k: (0, j)), pl.BlockSpec((1, tmw, NB), lambda j, m, k: (j, m, 0)),
                      (N_DEV, D // 2, NB), (N_DEV, half, 1), "wgrad_" + name)
        parts_i += _sequencer_scatter([g_wi], "scatter_" + name, cid)
    grad_x, g_mix = _in_bwd(dproj, wi, x2, dh1, norm_mix_w)

    big_m = (m_w_in[0], m_w_out[0], m_w_gate[0].T, m_w_up[0].T, m_w_down[0])
    big_v = (v_w_in[0], v_w_out[0], v_w_gate[0].T, v_w_up[0].T, v_w_down[0])
    names = ("adamw_in", "adamw_out", "adamw_gate", "adamw_up", "adamw_down")
    upd = [None] * 5
    for a, p in zip((2, 3, 4, 1, 0), [[t] for t in parts_f + parts_o] + [parts_i]):
        upd[a] = _adamw_block(p, big[a], big_m[a], big_v[a], names[a])

    g_dec_f = small[:, 0, 0].reshape(1, H)
    g_dec_b = small[:, 1, 0].reshape(1, H)
    g_retw = small_w[:, 0, :].reshape(1, DA)
    loss_local = jnp.sum(loss_parts[::8, 0])
    zero = jnp.zeros((1,), F32)
    part = _pack_small(g_mix, g_ffn, g_fin, g_retw, g_dec_f, g_dec_b, loss_local)
    sw = _pack_small(norm_mix_w, norm_ffn_w, norm_final_w, ret_norm_w, ret_decay_fwd, ret_decay_bwd, zero)
    sm = _pack_small(m_norm_mix_w, m_norm_ffn_w, m_norm_final_w, m_ret_norm_w, m_ret_decay_fwd, m_ret_decay_bwd, zero)
    sv = _pack_small(v_norm_mix_w, v_norm_ffn_w, v_norm_final_w, v_ret_norm_w, v_ret_decay_fwd, v_ret_decay_bwd, zero)
    shapes = [(1, D), (1, D), (D,), (1, DA), (1, H), (1, H), ()]
    sg, sd, snm, snv = [_unpack_small(t, shapes) for t in _small_step(part, sw, sm, sv)]
    loss = sg[6]

    def ordered(small_set, k):
        b = [(u[k].T if a in (2, 3) else u[k])[None] for a, u in enumerate(upd)]
        return [small_set[0], b[0], small_set[4], small_set[5], small_set[3], b[1], small_set[1], b[2], b[3], b[4],
                small_set[2]]

    return (loss, grad_x[None], *ordered(sg, 0), *ordered(sd, 1), *ordered(snm, 2), *ordered(snv, 3))
```

```python
import functools
import math

import numpy as np
import jax
import jax.numpy as jnp
from jax import lax
from jax.experimental import pallas as pl
from jax.experimental.pallas import tpu as pltpu
from jax.experimental.pallas import tpu_sc as plsc

F32 = jnp.float32
BF16 = jnp.bfloat16
SDS = jax.ShapeDtypeStruct

HEAD_DIM = 128
EPS = 1e-6
RET_CHUNK = 128
DILATIONS = (1, 4, 16)
BAND = 64
Q_TILE = 128
K_TILE = Q_TILE + 2 * BAND
KV_PAD = BAND * 4
TILE_GROUP = 8
BWD_TILE_GROUP = 8
NEG = -1e30
N_DEV = 8
N_GROUPS = 7
ADAM_LR, ADAM_B1, ADAM_B2, ADAM_EPS, ADAM_WD, ADAM_STEP = 0.001, 0.9, 0.999, 1e-08, 0.01, 10
VMEM_LIMIT = 56 * 1024 * 1024
MESH = pl.DeviceIdType.MESH
ANY = pl.BlockSpec(memory_space=pl.ANY)


def _cp(n_grid):
    return pltpu.CompilerParams(dimension_semantics=("arbitrary",) * n_grid, vmem_limit_bytes=VMEM_LIMIT)


def _sigmoid(x):
    return 1.0 / (1.0 + jnp.exp(-x))


def _rms_scale(h):
    return lax.rsqrt(jnp.mean(h * h, axis=-1, keepdims=True) + EPS)


def _rms_bwd(dn, h, w):
    r = _rms_scale(h)
    gw = dn * w
    dh = r * gw - h * (r * r * r) * jnp.mean(gw * h, axis=-1, keepdims=True)
    return dh, jnp.sum(dn * h * r, axis=0, keepdims=True)


def _dot(a, b, dims):
    return lax.dot_general(a.astype(BF16), b.astype(BF16), (dims, ((), ())), preferred_element_type=F32)


_NN = ((1,), (0,))
_NT = ((1,), (1,))
_TN = ((0,), (0,))


RESIDENT_ROWS = 256


def _resident(shape):
    return pl.BlockSpec(shape, lambda i: (0, 0), pipeline_mode=pl.Buffered(1))


def _blocked_matmul(a_ref, w_ref):
    nblk, _, fb = a_ref.shape
    out = None
    for j in range(nblk):
        part = jnp.dot(a_ref[j], w_ref[pl.ds(j * fb, fb), :], preferred_element_type=F32)
        out = part if out is None else out + part
    return out


def _norm_fwd(x, w_norm):
    S, D = x.shape
    tm = min(1024, S)

    def body(x_ref, wn_ref, n_ref):
        xf = x_ref[...]
        n_ref[...] = (xf * _rms_scale(xf) * wn_ref[...]).astype(BF16)

    row = pl.BlockSpec((tm, D), lambda i: (i, 0))
    return pl.pallas_call(body, grid=(S // tm,), name="norm_fwd", in_specs=[row, pl.BlockSpec((1, D), lambda i: (0, 0))],
                          out_specs=row, out_shape=SDS((S, D), BF16), compiler_params=_cp(1))(x, w_norm)


def _proj_part(n1, w_slots, slots, blocks, proj, n_blocks, name):
    S, D = n1.shape
    NB = w_slots.shape[2]
    tm = min(1024, S)

    def body(slots_ref, blocks_ref, n_ref, w_ref, *rest):
        rest[-1][...] = jnp.dot(n_ref[...], w_ref[0], preferred_element_type=F32)

    out_spec = pl.BlockSpec((tm, NB), lambda i, j, slots, blocks: (i, blocks[j]))
    in_specs = [pl.BlockSpec((tm, D), lambda i, j, slots, blocks: (i, 0)),
                pl.BlockSpec((1, D, NB), lambda i, j, slots, blocks: (slots[j], 0, 0))]
    args = [n1, w_slots]
    if proj is not None:
        in_specs.append(ANY)
        args.append(proj)
    return pl.pallas_call(
        body, name=name, out_shape=SDS((S, n_blocks * NB), F32),
        grid_spec=pltpu.PrefetchScalarGridSpec(num_scalar_prefetch=2, grid=(S // tm, slots.shape[0]), in_specs=in_specs,
                                               out_specs=out_spec),
        input_output_aliases={} if proj is None else {4: 0},
        compiler_params=_cp(2))(slots, blocks, *args)


def _out_fwd(x, attn, ret, w_out, w_norm):
    S, D = x.shape
    DA = attn.shape[1]
    tm = min(256, S)

    def body(x_ref, a_ref, r_ref, w_ref, wn_ref, h_ref, mix_ref, n_ref):
        a = a_ref[...].astype(BF16)
        r = r_ref[...].astype(BF16)
        mix_ref[:, :DA] = a
        mix_ref[:, DA:] = r
        h = x_ref[...] + jnp.dot(a, w_ref[:DA, :], preferred_element_type=F32) \
            + jnp.dot(r, w_ref[DA:, :], preferred_element_type=F32)
        h_ref[...] = h
        n_ref[...] = (h * _rms_scale(h) * wn_ref[...]).astype(BF16)

    row = lambda w: pl.BlockSpec((tm, w), lambda i: (i, 0))
    return pl.pallas_call(
        body, grid=(S // tm,), name="out_fwd",
        in_specs=[row(D), row(DA), row(D - DA), pl.BlockSpec((D, D), lambda i: (0, 0)),
                  pl.BlockSpec((1, D), lambda i: (0, 0))],
        out_specs=[row(D), row(D), row(D)],
        out_shape=[SDS((S, D), F32), SDS((S, D), BF16), SDS((S, D), BF16)],
        compiler_params=_cp(1))(x, attn, ret, w_out, w_norm)


def _ffn_up(n2, wg, wu):
    S, D = n2.shape
    nblk, FB, _ = wg.shape
    tm = min(1024, S)

    def body(n_ref, wg_ref, wu_ref, g_ref, u_ref, a_ref):
        n = n_ref[...]
        g = _dot(n, wg_ref[0], _NT)
        u = _dot(n, wu_ref[0], _NT)
        g_ref[0] = g.astype(BF16)
        u_ref[0] = u.astype(BF16)
        a_ref[0] = (g * _sigmoid(g) * u).astype(BF16)

    wspec = pl.BlockSpec((1, FB, D), lambda j, i: (j, 0, 0))
    ospec = pl.BlockSpec((1, tm, FB), lambda j, i: (j, i, 0))
    return pl.pallas_call(
        body, grid=(nblk, S // tm), name="ffn_up",
        in_specs=[pl.BlockSpec((tm, D), lambda j, i: (i, 0)), wspec, wspec],
        out_specs=[ospec, ospec, ospec],
        out_shape=[SDS((nblk, S, FB), BF16)] * 3,
        compiler_params=_cp(2))(n2, wg, wu)


def _ffn_down_loss(act, wd, h1, target, w_norm):
    nblk, S, FB = act.shape
    D = h1.shape[1]
    tm = min(RESIDENT_ROWS, S)

    def body(a_ref, wd_ref, h_ref, t_ref, wn_ref, dh_ref, dhb_ref, loss_ref, dw_ref):
        @pl.when(pl.program_id(0) == 0)
        def _():
            dw_ref[...] = jnp.zeros_like(dw_ref)

        h = h_ref[...] + _blocked_matmul(a_ref, wd_ref)
        w = wn_ref[...]
        err = h * _rms_scale(h) * w - t_ref[...]
        loss_ref[...] = jnp.full(loss_ref.shape, 0.5 * jnp.sum(err * err) / D, F32)
        dh, dw = _rms_bwd(err * (1.0 / D), h, w)
        dh_ref[...] = dh
        dhb_ref[...] = dh.astype(BF16)
        dw_ref[...] += dw

    row = pl.BlockSpec((tm, D), lambda i: (i, 0))
    vec = pl.BlockSpec((1, D), lambda i: (0, 0))
    return pl.pallas_call(
        body, grid=(S // tm,), name="ffn_down_loss",
        in_specs=[pl.BlockSpec((nblk, tm, FB), lambda i: (0, i, 0)), _resident((nblk * FB, D)), row, row, vec],
        out_specs=[row, row, pl.BlockSpec((8, 128), lambda i: (i, 0)), vec],
        out_shape=[SDS((S, D), F32), SDS((S, D), BF16), SDS((S // tm * 8, 128), F32), SDS((1, D), F32)],
        compiler_params=_cp(1))(act, wd, h1, target, w_norm)


def _attn_bias():
    n_heads = 8
    slopes = np.exp2(-8.0 * np.arange(1, n_heads + 1, dtype=np.float32) / n_heads)
    dist = np.abs(np.arange(K_TILE)[None, :] - BAND - np.arange(Q_TILE)[:, None])
    out = np.empty((n_heads, len(DILATIONS), Q_TILE, K_TILE), np.float32)
    for h in range(n_heads):
        for p, d in enumerate(DILATIONS):
            out[h, p] = np.where(dist <= BAND, -slopes[h] * (d * dist).astype(np.float32), NEG)
    return jnp.asarray(out)


def _attn_tiles(S, d):
    L = S // d
    per_class = L // Q_TILE
    return L, per_class, d * per_class


def _tile_rows(t, d, per_class):
    r = t // per_class
    a = (t % per_class) * Q_TILE
    q_rows = pl.ds(r + d * a, Q_TILE, stride=d) if d > 1 else pl.ds(pl.multiple_of(a, Q_TILE), Q_TILE)
    k_rows = pl.ds(KV_PAD + r + d * (a - BAND), K_TILE, stride=d) if d > 1 else pl.ds(
        pl.multiple_of(KV_PAD + a - BAND, BAND), K_TILE)
    return a, q_rows, k_rows


def _to_quarters(dst, src, n, dst_off=0):
    for r in range(4):
        dst[pl.ds(dst_off + r * (n // 4), n // 4), :] = src[pl.ds(r, n // 4, stride=4), :]


def _quarter_tile_rows(t, S):
    L = S // 16
    per_class = L // Q_TILE
    blk, tt = t // (4 * per_class), t % (4 * per_class)
    r, a = tt // per_class, (tt % per_class) * Q_TILE
    q_rows = pl.ds(blk * (S // 4) + r + 4 * a, Q_TILE, stride=4)
    k_rows = pl.ds(KV_PAD + blk * (S // 4) + r + 4 * (a - BAND), K_TILE, stride=4)
    return a, q_rows, k_rows


def _quarter_band_rows(t, S):
    L = S // 4
    per_quarter = L // Q_TILE
    blk, a = t // per_quarter, (t % per_quarter) * Q_TILE
    q_rows = pl.ds(pl.multiple_of(blk * L + a, Q_TILE), Q_TILE)
    k_rows = pl.ds(pl.multiple_of(KV_PAD + blk * L + a - BAND, BAND), K_TILE)
    return a, q_rows, k_rows


def _lanes(x, width):
    return jnp.concatenate([x] * (width // HEAD_DIM), axis=-1)


_BNT = (((2,), (2,)), ((0,), (0,)))
_BNN = (((2,), (1,)), ((0,), (0,)))
_BTN = (((1,), (1,)), ((0,), (0,)))


def _bdot(a, b, dims):
    return lax.dot_general(a, b, dims, preferred_element_type=F32)


def _stacked(rows, loaders):
    return [jnp.stack([f(*r) for r in rows]) for f in loaders]


def _edge_mask(a, L):
    lk = lax.broadcasted_iota(jnp.int32, (1, K_TILE), 1) + (a - BAND)
    return jnp.where((lk >= 0) & (lk < L), 0.0, NEG).astype(F32)


def _fill_padded(dst, src, S):
    dst[pl.ds(0, KV_PAD), :] = jnp.zeros((KV_PAD, HEAD_DIM), F32)
    dst[pl.ds(KV_PAD + S, KV_PAD), :] = jnp.zeros((KV_PAD, HEAD_DIM), F32)
    dst[pl.ds(KV_PAD, S), :] = src[...]


def _head_specs(S, groups, n_heads):
    return [pl.BlockSpec((S, HEAD_DIM), functools.partial(lambda h, g: (0, g * n_heads + h), g=g)) for g in groups]


def _attn_fwd(proj, bias):
    S = proj.shape[0]
    H = proj.shape[1] // (N_GROUPS * HEAD_DIM)
    scale = HEAD_DIM ** -0.5

    def body(q_ref, k_ref, v_ref, b_ref, o_ref, lse_ref, kp, vp, m_run, l_run, q4, m3, l3, acc3):
        _fill_padded(kp, k_ref, S)
        _fill_padded(vp, v_ref, S)
        o_ref[...] = jnp.zeros_like(o_ref)
        m_run[...] = jnp.full(m_run.shape, NEG, F32)
        l_run[...] = jnp.zeros_like(l_run)
        def online(n_tiles, tile_rows, p, L, q_src, m_buf, l_buf, o_buf):
            def tiles(t, carry):
                rows = [tile_rows(t + u * (n_tiles // TILE_GROUP)) for u in range(TILE_GROUP)]
                qs, ks, vs, m_old, l_old, o_old, edge = _stacked(rows, (
                    lambda a, qr, kr: q_src[qr, :].astype(BF16), lambda a, qr, kr: kp[kr, :].astype(BF16),
                    lambda a, qr, kr: vp[kr, :].astype(BF16), lambda a, qr, kr: m_buf[qr, :],
                    lambda a, qr, kr: l_buf[qr, :], lambda a, qr, kr: o_buf[qr, :], lambda a, qr, kr: _edge_mask(a, L)))
                s = _bdot(qs, ks, _BNT) * scale + b_ref[0, p][None] + edge
                m_new = jnp.maximum(m_old, jnp.max(s, axis=-1, keepdims=True))
                pr = jnp.exp(s - _lanes(m_new, K_TILE)).astype(BF16)
                alpha = jnp.exp(m_old - m_new)
                l_new = alpha * l_old + _bdot(pr, jnp.ones((TILE_GROUP, K_TILE, HEAD_DIM), BF16), _BNN)
                o_new = alpha * o_old + _bdot(pr, vs, _BNN)
                for u, (_, qr, _) in enumerate(rows):
                    o_buf[qr, :] = o_new[u]
                    m_buf[qr, :] = m_new[u]
                    l_buf[qr, :] = l_new[u]
                return carry

            lax.fori_loop(0, n_tiles // TILE_GROUP, tiles, 0)

        L, per_class, n_tiles = _attn_tiles(S, DILATIONS[0])
        online(n_tiles, functools.partial(_tile_rows, d=DILATIONS[0], per_class=per_class), 0, L, q_ref, m_run, l_run, o_ref)

        _to_quarters(q4, q_ref, S)
        _to_quarters(kp, k_ref, S, KV_PAD)
        _to_quarters(vp, v_ref, S, KV_PAD)
        n_tiles = _attn_tiles(S, DILATIONS[2])[2]

        def tiles3(t, carry):
            rows = [_quarter_tile_rows(t + u * (n_tiles // TILE_GROUP), S) for u in range(TILE_GROUP)]
            qs, ks, vs, edge = _stacked(rows, (
                lambda a, qr, kr: q4[qr, :].astype(BF16), lambda a, qr, kr: kp[kr, :].astype(BF16),
                lambda a, qr, kr: vp[kr, :].astype(BF16), lambda a, qr, kr: _edge_mask(a, S // DILATIONS[2])))
            s = _bdot(qs, ks, _BNT) * scale + b_ref[0, 2][None] + edge
            m_new = jnp.broadcast_to(jnp.max(s, axis=-1, keepdims=True), (TILE_GROUP, Q_TILE, HEAD_DIM))
            pr = jnp.exp(s - _lanes(m_new, K_TILE)).astype(BF16)
            l_new = _bdot(pr, jnp.ones((TILE_GROUP, K_TILE, HEAD_DIM), BF16), _BNN)
            o_new = _bdot(pr, vs, _BNN)
            for u, (_, qr, _) in enumerate(rows):
                acc3[qr, :] = o_new[u]
                m3[qr, :] = m_new[u]
                l3[qr, :] = l_new[u]
            return carry

        lax.fori_loop(0, n_tiles // TILE_GROUP, tiles3, 0)
        online(_attn_tiles(S, DILATIONS[1])[2], functools.partial(_quarter_band_rows, S=S), 1, S // DILATIONS[1],
               q4, m3, l3, acc3)
        for r in range(4):
            nat, qtr = pl.ds(r, S // 4, stride=4), pl.ds(r * (S // 4), S // 4)
            m_a, m_b = m_run[nat, :], m3[qtr, :]
            m = jnp.maximum(m_a, m_b)
            w_a, w_b = jnp.exp(m_a - m), jnp.exp(m_b - m)
            l = w_a * l_run[nat, :] + w_b * l3[qtr, :]
            o_ref[nat, :] = (w_a * o_ref[nat, :] + w_b * acc3[qtr, :]) / l
            lse_ref[nat, :] = m + jnp.log(l)

    hspec = pl.BlockSpec((S, HEAD_DIM), lambda h: (0, h))
    padded, plain = pltpu.VMEM((S + 2 * KV_PAD, HEAD_DIM), F32), pltpu.VMEM((S, HEAD_DIM), F32)
    return pl.pallas_call(
        body, grid=(H,), name="attn_fwd",
        in_specs=_head_specs(S, (0, 1, 2), H) + [
            pl.BlockSpec((1, len(DILATIONS), Q_TILE, K_TILE), lambda h: (h, 0, 0, 0))],
        out_specs=[hspec, hspec],
        out_shape=[SDS((S, H * HEAD_DIM), F32), SDS((S, H * HEAD_DIM), F32)],
        scratch_shapes=[padded, padded] + [plain] * 6,
        compiler_params=_cp(1))(proj, proj, proj, bias)


def _put_groups(stage, dproj, groups, n_heads, sems):
    h = pl.program_id(0)
    copies = [pltpu.make_async_copy(
        stage.at[i], dproj.at[:, pl.ds(pl.multiple_of((g * n_heads + h) * HEAD_DIM, HEAD_DIM), HEAD_DIM)], sems.at[i])
        for i, g in enumerate(groups)]
    for cp in copies:
        cp.start()
    for cp in copies:
        cp.wait()


def _attn_bwd(proj, out, lse, dmix, bias, dproj):
    S = proj.shape[0]
    H = proj.shape[1] // (N_GROUPS * HEAD_DIM)
    scale = HEAD_DIM ** -0.5
    assert S // DILATIONS[2] >= 2 * Q_TILE

    def body(q_ref, k_ref, v_ref, o_ref, lse_ref, do_ref, b_ref, dproj_in, dproj_out,
             kp, vp, dkp, dvp, dsum, q4, do4, lse4, dsum4, dq_ref, dk_ref, dv_ref, stage, sems):
        _fill_padded(kp, k_ref, S)
        _fill_padded(vp, v_ref, S)
        dkp[...] = jnp.zeros_like(dkp)
        dvp[...] = jnp.zeros_like(dvp)
        dq_ref[...] = jnp.zeros_like(dq_ref)
        dsum[...] = jnp.broadcast_to(jnp.sum(do_ref[...] * o_ref[...], axis=-1, keepdims=True), dsum.shape)

        def run(n_tiles, tile_rows, p, L, q_src, do_src, lse_src, dsum_src, dq_dst, dq_adds):
            def tiles(t, carry):
                rows = [tile_rows(t + u * (n_tiles // BWD_TILE_GROUP)) for u in range(BWD_TILE_GROUP)]
                qs, ks, vs, dos, lses, dsums, dk_old, dv_old, edge = _stacked(rows, (
                    lambda a, qr, kr: q_src[qr, :].astype(BF16), lambda a, qr, kr: kp[kr, :].astype(BF16),
                    lambda a, qr, kr: vp[kr, :].astype(BF16), lambda a, qr, kr: do_src[qr, :].astype(BF16),
                    lambda a, qr, kr: lse_src[qr, :], lambda a, qr, kr: dsum_src[qr, :],
                    lambda a, qr, kr: dkp[kr, :], lambda a, qr, kr: dvp[kr, :], lambda a, qr, kr: _edge_mask(a, L)))
                s = _bdot(qs, ks, _BNT) * scale + b_ref[0, p][None] + edge
                pr = jnp.exp(s - _lanes(lses, K_TILE))
                ds = (pr * (_bdot(dos, vs, _BNT) - _lanes(dsums, K_TILE)) * scale).astype(BF16)
                dq_new = _bdot(ds, ks, _BNN)
                if dq_adds:
                    dq_new = dq_new + jnp.stack([dq_dst[qr, :] for _, qr, _ in rows])
                dk_new = dk_old + _bdot(ds, qs, _BTN)
                dv_new = dv_old + _bdot(pr.astype(BF16), dos, _BTN)
                for u, (_, qr, kr) in enumerate(rows):
                    dq_dst[qr, :] = dq_new[u]
                    dkp[kr, :] = dk_new[u]
                    dvp[kr, :] = dv_new[u]
                return carry

            lax.fori_loop(0, n_tiles // BWD_TILE_GROUP, tiles, 0)

        L, per_class, n_tiles = _attn_tiles(S, DILATIONS[0])
        run(n_tiles, functools.partial(_tile_rows, d=DILATIONS[0], per_class=per_class), 0, L,
            q_ref, do_ref, lse_ref, dsum, dq_ref, True)
        dk_ref[...] = dkp[pl.ds(KV_PAD, S), :]
        dv_ref[...] = dvp[pl.ds(KV_PAD, S), :]

        for dst, src in ((q4, q_ref), (do4, do_ref), (lse4, lse_ref), (dsum4, dsum)):
            _to_quarters(dst, src, S)
        _to_quarters(kp, k_ref, S, KV_PAD)
        _to_quarters(vp, v_ref, S, KV_PAD)
        dkp[...] = jnp.zeros_like(dkp)
        dvp[...] = jnp.zeros_like(dvp)
        dq3 = dsum
        run(_attn_tiles(S, DILATIONS[2])[2], functools.partial(_quarter_tile_rows, S=S), 2, S // DILATIONS[2],
            q4, do4, lse4, dsum4, dq3, False)
        run(_attn_tiles(S, DILATIONS[1])[2], functools.partial(_quarter_band_rows, S=S), 1, S // DILATIONS[1],
            q4, do4, lse4, dsum4, dq3, True)
        for r in range(4):
            nat, qtr = pl.ds(r, S // 4, stride=4), pl.ds(r * (S // 4), S // 4)
            pad_qtr = pl.ds(KV_PAD + r * (S // 4), S // 4)
            dq_ref[nat, :] = dq_ref[nat, :] + dq3[qtr, :]
            dk_ref[nat, :] = dk_ref[nat, :] + dkp[pad_qtr, :]
            dv_ref[nat, :] = dv_ref[nat, :] + dvp[pad_qtr, :]
        for i, acc in enumerate((dq_ref, dk_ref, dv_ref)):
            stage[i] = acc[...].astype(BF16)
        _put_groups(stage, dproj_out, (0, 1, 2), H, sems)

    hspec = pl.BlockSpec((S, HEAD_DIM), lambda h: (0, h))
    once = pl.BlockSpec((S, HEAD_DIM), lambda h: (0, h), pipeline_mode=pl.Buffered(1))
    padded, plain = pltpu.VMEM((S + 2 * KV_PAD, HEAD_DIM), F32), pltpu.VMEM((S, HEAD_DIM), F32)
    return pl.pallas_call(
        body, grid=(H,), name="attn_bwd",
        in_specs=_head_specs(S, (0, 1, 2), H) + [
            once, hspec, hspec, pl.BlockSpec((1, len(DILATIONS), Q_TILE, K_TILE), lambda h: (h, 0, 0, 0)), ANY],
        out_specs=ANY, out_shape=SDS(dproj.shape, dproj.dtype), input_output_aliases={7: 0},
        scratch_shapes=[padded] * 4 + [plain] * 8 + [pltpu.VMEM((3, S, HEAD_DIM), BF16), pltpu.SemaphoreType.DMA((3,))],
        compiler_params=_cp(1))(proj, proj, proj, out, lse, dmix, bias, dproj)


def _ret_consts(lg, forward):
    C = RET_CHUNK
    i = lax.broadcasted_iota(jnp.int32, (C, C), 0)
    j = lax.broadcasted_iota(jnp.int32, (C, C), 1)
    rel = (i - j) if forward else (j - i)
    inside = (rel >= 0) if forward else (rel > 0)
    relf = jnp.maximum(rel, 0).astype(F32)
    mask = jnp.where(inside, jnp.exp(lg * relf), 0.0)
    idx = lax.broadcasted_iota(jnp.int32, (C, 1), 0).astype(F32)
    q_exp = (idx + 1.0) if forward else (C - idx)
    k_exp = (C - 1.0 - idx) if forward else idx
    return mask, relf, jnp.exp(lg * q_exp), q_exp, jnp.exp(lg * k_exp), k_exp, jnp.exp(lg * C)


def _log_decay(dec_ref, h):
    return -jnp.exp(jnp.full((1, 1), dec_ref[0, h], F32))


FFN_BLOCK = 704
CHUNK_BATCH = 8


def _batch_rows(b):
    n = CHUNK_BATCH * RET_CHUNK
    return pl.ds(pl.multiple_of(b * n, n), n)


def _batch_chunks(b):
    return pl.ds(pl.multiple_of(b * CHUNK_BATCH, CHUNK_BATCH), CHUNK_BATCH)


def _chunks3(x):
    return x.reshape(CHUNK_BATCH, RET_CHUNK, HEAD_DIM)


def _ret_scan(buf, c_decs, nc, reverse):
    def step(n, carry):
        new = []
        for way, r in enumerate(carry):
            c = n if (way == 0) != reverse else nc - 1 - n
            term = buf[way, c]
            buf[way, c] = r
            new.append(r * c_decs[way] + term)
        return tuple(new)

    lax.fori_loop(0, nc, step, (jnp.zeros((HEAD_DIM, HEAD_DIM), F32),) * 2)


def _ret_fwd(proj, dec_f, dec_b, w_norm):
    S = proj.shape[0]
    H = proj.shape[1] // (N_GROUPS * HEAD_DIM)
    nc = S // RET_CHUNK
    scale = HEAD_DIM ** -0.5

    def body(df_ref, db_ref, q_ref, k_ref, v_ref, g_ref, w_ref, y_ref, o_ref, states):
        h = pl.program_id(0)
        consts = [_ret_consts(_log_decay(dref, h), fw) for fw, dref in ((True, df_ref), (False, db_ref))]

        def kv_step(b, carry):
            rows, batch = _batch_rows(b), _batch_chunks(b)
            k3 = _chunks3(k_ref[rows, :])
            v3 = _chunks3(v_ref[rows, :]).astype(BF16)
            for way in range(2):
                states[way, batch] = _bdot((k3 * consts[way][4]).astype(BF16), v3, _BTN)
            return carry

        lax.fori_loop(0, nc // CHUNK_BATCH, kv_step, 0)
        _ret_scan(states, [c[6] for c in consts], nc, False)

        def out_step(b, carry):
            rows, batch = _batch_rows(b), _batch_chunks(b)
            q3 = _chunks3(q_ref[rows, :] * scale)
            k3 = _chunks3(k_ref[rows, :]).astype(BF16)
            v3 = _chunks3(v_ref[rows, :]).astype(BF16)
            a0 = _bdot(q3.astype(BF16), k3, _BNT)
            o = None
            for way in range(2):
                mask, q_dec = consts[way][0], consts[way][2]
                part = _bdot((a0 * mask).astype(BF16), v3, _BNN) \
                    + _bdot((q3 * q_dec).astype(BF16), states[way, batch].astype(BF16), _BNN)
                o = part if o is None else o + part
            o_ref[rows, :] = o.reshape(CHUNK_BATCH * RET_CHUNK, HEAD_DIM)
            return carry

        lax.fori_loop(0, nc // CHUNK_BATCH, out_step, 0)
        o = o_ref[...]
        g = g_ref[...]
        y_ref[...] = o * _rms_scale(o) * w_ref[...] * (g * _sigmoid(g))

    hspec = pl.BlockSpec((S, HEAD_DIM), lambda h: (0, h))
    smem = pl.BlockSpec(memory_space=pltpu.SMEM)
    return pl.pallas_call(
        body, grid=(H,), name="ret_fwd",
        in_specs=[smem, smem] + _head_specs(S, (3, 4, 5, 6), H) + [pl.BlockSpec((1, HEAD_DIM), lambda h: (0, h))],
        out_specs=[hspec, hspec],
        out_shape=[SDS((S, H * HEAD_DIM), F32)] * 2,
        scratch_shapes=[pltpu.VMEM((2, nc, HEAD_DIM, HEAD_DIM), F32)],
        compiler_params=_cp(1))(dec_f, dec_b, proj, proj, proj, proj, w_norm)


def _ret_gate_bwd(proj, o_raw, dmix, w_norm, col0, dproj):
    S = proj.shape[0]
    H = proj.shape[1] // (N_GROUPS * HEAD_DIM)

    def body(g_ref, o_ref, dy_ref, w_ref, dproj_in, do_ref, dw_ref, dproj_out, dg_ref, sems):
        o = o_ref[...]
        g = g_ref[...]
        dy = dy_ref[...]
        w = w_ref[...]
        rr = _rms_scale(o)
        normed = o * rr
        sg = _sigmoid(g)
        silu = g * sg
        dw_ref[0] = jnp.broadcast_to(jnp.sum(dy * normed * silu, axis=0, keepdims=True), (8, HEAD_DIM))
        dg_ref[0] = (dy * normed * w * (sg * (1.0 + g * (1.0 - sg)))).astype(BF16)
        dnormed = dy * w * silu
        do_ref[...] = rr * dnormed - o * (rr * rr * rr) * jnp.mean(dnormed * o, axis=-1, keepdims=True)
        _put_groups(dg_ref, dproj_out, (6,), H, sems)

    hspec = pl.BlockSpec((S, HEAD_DIM), lambda h: (0, h))
    nh0 = col0 // HEAD_DIM
    return pl.pallas_call(
        body, grid=(H,), name="ret_gate_bwd",
        in_specs=_head_specs(S, (6,), H) + [hspec, pl.BlockSpec((S, HEAD_DIM), lambda h: (0, nh0 + h)),
                                            pl.BlockSpec((1, HEAD_DIM), lambda h: (0, h)), ANY],
        out_specs=[hspec, pl.BlockSpec((1, 8, HEAD_DIM), lambda h: (h, 0, 0)), ANY],
        out_shape=[SDS((S, H * HEAD_DIM), F32), SDS((H, 8, HEAD_DIM), F32), SDS(dproj.shape, dproj.dtype)],
        input_output_aliases={4: 2},
        scratch_shapes=[pltpu.VMEM((1, S, HEAD_DIM), BF16), pltpu.SemaphoreType.DMA((1,))],
        compiler_params=_cp(1))(proj, o_raw, dmix, w_norm, dproj)


def _ret_bwd(proj, d_out, dec_f, dec_b, dproj):
    S = proj.shape[0]
    H = proj.shape[1] // (N_GROUPS * HEAD_DIM)
    C = RET_CHUNK
    nc = S // C
    scale = HEAD_DIM ** -0.5

    def body(df_ref, db_ref, q_ref, k_ref, v_ref, do, dproj_in, small_ref, dproj_out, states, d_states, stage, sems):
        h = pl.program_id(0)
        lgs = [_log_decay(df_ref, h), _log_decay(db_ref, h)]
        consts = [_ret_consts(lg, fw) for lg, fw in zip(lgs, (True, False))]

        def prep_step(b, carry):
            rows, batch = _batch_rows(b), _batch_chunks(b)
            q3 = _chunks3(q_ref[rows, :] * scale)
            k3 = _chunks3(k_ref[rows, :])
            v3 = _chunks3(v_ref[rows, :]).astype(BF16)
            do3 = _chunks3(do[rows, :]).astype(BF16)
            for way in range(2):
                states[way, batch] = _bdot((k3 * consts[way][4]).astype(BF16), v3, _BTN)
                d_states[way, batch] = _bdot((q3 * consts[way][2]).astype(BF16), do3, _BTN)
            return carry

        lax.fori_loop(0, nc // CHUNK_BATCH, prep_step, 0)
        c_decs = [c[6] for c in consts]
        _ret_scan(states, c_decs, nc, False)
        _ret_scan(d_states, c_decs, nc, True)

        def main_step(b, dlams):
            rows, batch = _batch_rows(b), _batch_chunks(b)
            q3 = _chunks3(q_ref[rows, :] * scale)
            k3 = _chunks3(k_ref[rows, :])
            q3b, k3b = q3.astype(BF16), k3.astype(BF16)
            v3b = _chunks3(v_ref[rows, :]).astype(BF16)
            do3b = _chunks3(do[rows, :]).astype(BF16)
            a0 = _bdot(q3b, k3b, _BNT)
            pv = _bdot(do3b, v3b, _BNT)
            dq = dk = dv = None
            new_dlams = []
            for way in range(2):
                mask, relf, q_dec, q_exp, k_dec, k_exp, c_dec = consts[way]
                state, d_state = states[way, batch], d_states[way, batch]
                dp = pv * mask
                dpb = dp.astype(BF16)
                gq = _bdot(do3b, state.astype(BF16), _BNT)
                gk = _bdot(v3b, d_state.astype(BF16), _BNT)
                parts = (_bdot(dpb, k3b, _BNN) + q_dec * gq, _bdot(dpb, q3b, _BTN) + k_dec * gk,
                         _bdot((a0 * mask).astype(BF16), do3b, _BTN)
                         + _bdot((k3 * k_dec).astype(BF16), d_state.astype(BF16), _BNN))
                dq, dk, dv = parts if dq is None else (dq + parts[0], dk + parts[1], dv + parts[2])
                total = lambda x: jnp.sum(jnp.sum(x, axis=0), axis=0, keepdims=True)
                new_dlams.append(dlams[way] + total(relf * a0 * dp)
                                 + total(q_exp * q_dec * q3 * gq + k_exp * k_dec * k3 * gk)
                                 + (C * c_dec) * total(state * d_state))
            flat = lambda x: x.reshape(CHUNK_BATCH * C, HEAD_DIM)
            stage[0, rows, :] = (flat(dq) * scale).astype(BF16)
            stage[1, rows, :] = flat(dk).astype(BF16)
            stage[2, rows, :] = flat(dv).astype(BF16)
            return tuple(new_dlams)

        dlams = lax.fori_loop(0, nc // CHUNK_BATCH, main_step, (jnp.zeros((1, HEAD_DIM), F32),) * 2)
        for row, (dlam, lg) in enumerate(zip(dlams, lgs)):
            small_ref[0, pl.ds(row, 1), :] = jnp.broadcast_to(jnp.sum(dlam, axis=-1, keepdims=True) * lg, (1, HEAD_DIM))
        small_ref[0, pl.ds(2, 6), :] = jnp.zeros((6, HEAD_DIM), F32)
        _put_groups(stage, dproj_out, (3, 4, 5), H, sems)

    hspec = pl.BlockSpec((S, HEAD_DIM), lambda h: (0, h))
    smem = pl.BlockSpec(memory_space=pltpu.SMEM)
    return pl.pallas_call(
        body, grid=(H,), name="ret_bwd",
        in_specs=[smem, smem] + _head_specs(S, (3, 4, 5), H) + [hspec, ANY],
        out_specs=[pl.BlockSpec((1, 8, HEAD_DIM), lambda h: (h, 0, 0)), ANY],
        out_shape=[SDS((H, 8, HEAD_DIM), F32), SDS(dproj.shape, dproj.dtype)], input_output_aliases={6: 1},
        scratch_shapes=[pltpu.VMEM((2, nc, HEAD_DIM, HEAD_DIM), F32), pltpu.VMEM((2, nc, HEAD_DIM, HEAD_DIM), F32),
                        pltpu.VMEM((3, S, HEAD_DIM), BF16), pltpu.SemaphoreType.DMA((3,))],
        compiler_params=_cp(1))(dec_f, dec_b, proj, proj, proj, d_out, dproj)


def _ffn_bwd_act(dh2, wd, g, u):
    S, D = dh2.shape
    nblk, _, FB = g.shape
    tm = min(1024, S)

    def body(dh_ref, wd_ref, g_ref, u_ref, dg_ref, du_ref):
        dact = _dot(dh_ref[...], wd_ref[...], _NT)
        gg = g_ref[0].astype(F32)
        sg = _sigmoid(gg)
        dg_ref[0] = (dact * u_ref[0].astype(F32) * (sg * (1.0 + gg * (1.0 - sg)))).astype(BF16)
        du_ref[0] = (dact * (gg * sg)).astype(BF16)

    blk = pl.BlockSpec((1, tm, FB), lambda j, i: (j, i, 0))
    return pl.pallas_call(
        body, grid=(nblk, S // tm), name="ffn_bwd_act",
        in_specs=[pl.BlockSpec((tm, D), lambda j, i: (i, 0)), pl.BlockSpec((FB, D), lambda j, i: (j, 0)), blk, blk],
        out_specs=[blk, blk], out_shape=[SDS((nblk, S, FB), BF16)] * 2,
        compiler_params=_cp(2))(dh2, wd, g, u)


def _ffn_bwd_in(dg, du, wg, wu, h1, dh2, w_norm):
    nblk, S, FB = dg.shape
    D = h1.shape[1]
    tm = min(RESIDENT_ROWS, S)
    blk = pl.BlockSpec((nblk, tm, FB), lambda i: (0, i, 0))
    row = pl.BlockSpec((tm, D), lambda i: (i, 0))
    vec = pl.BlockSpec((1, D), lambda i: (0, 0))

    def gate_body(dg_ref, wg_ref, part_ref):
        part_ref[...] = _blocked_matmul(dg_ref, wg_ref)

    part = pl.pallas_call(
        gate_body, grid=(S // tm,), name="ffn_bwd_in_gate", in_specs=[blk, _resident((nblk * FB, D))],
        out_specs=row, out_shape=SDS((S, D), F32), compiler_params=_cp(1))(dg, wg.reshape(nblk * FB, D))

    def body(du_ref, wu_ref, part_ref, h_ref, dh2_ref, wn_ref, dh_ref, dhb_ref, dw_ref):
        @pl.when(pl.program_id(0) == 0)
        def _():
            dw_ref[...] = jnp.zeros_like(dw_ref)

        dh, dw = _rms_bwd(part_ref[...] + _blocked_matmul(du_ref, wu_ref), h_ref[...], wn_ref[...])
        dh = dh2_ref[...] + dh
        dh_ref[...] = dh
        dhb_ref[...] = dh.astype(BF16)
        dw_ref[...] += dw

    return pl.pallas_call(
        body, grid=(S // tm,), name="ffn_bwd_in",
        in_specs=[blk, _resident((nblk * FB, D)), row, row, row, vec],
        out_specs=[row, row, vec], out_shape=[SDS((S, D), F32), SDS((S, D), BF16), SDS((1, D), F32)],
        compiler_params=_cp(1))(du, wu.reshape(nblk * FB, D), part, h1, dh2, w_norm)


def _dmix(dh1, w_out):
    S, D = dh1.shape
    tm = min(512, S)

    def body(dh_ref, w_ref, o_ref):
        o_ref[...] = _dot(dh_ref[...], w_ref[...], _NT)

    row = pl.BlockSpec((tm, D), lambda i: (i, 0))
    return pl.pallas_call(
        body, grid=(S // tm,), name="dmix", in_specs=[row, pl.BlockSpec((D, D), lambda i: (0, 0))],
        out_specs=row, out_shape=SDS((S, D), F32), compiler_params=_cp(1))(dh1, w_out)


def _in_bwd(dproj, w_blk, x, dh1, w_norm):
    S, D = x.shape
    nblk, _, NB = w_blk.shape
    tm = min(RESIDENT_ROWS, S)

    def body(dp_ref, w_ref, x_ref, dh1_ref, wn_ref, dx_ref, dw_ref):
        @pl.when(pl.program_id(0) == 0)
        def _():
            dw_ref[...] = jnp.zeros_like(dw_ref)

        dn = None
        for j in range(nblk):
            part = _dot(dp_ref[:, pl.ds(j * NB, NB)], w_ref[j], _NT)
            dn = part if dn is None else dn + part
        dh, dw = _rms_bwd(dn, x_ref[...], wn_ref[...])
        dx_ref[...] = dh1_ref[...] + dh
        dw_ref[...] += dw

    row = pl.BlockSpec((tm, D), lambda i: (i, 0))
    vec = pl.BlockSpec((1, D), lambda i: (0, 0))
    return pl.pallas_call(
        body, grid=(S // tm,), name="in_bwd",
        in_specs=[pl.BlockSpec((tm, nblk * NB), lambda i: (i, 0)),
                  pl.BlockSpec((nblk, D, NB), lambda i: (0, 0, 0), pipeline_mode=pl.Buffered(1)), row, row, vec],
        out_specs=[row, vec], out_shape=[SDS((S, D), F32), SDS((1, D), F32)],
        compiler_params=_cp(1))(dproj, w_blk, x, dh1, w_norm)


def _wgrad(a, b, a_spec, b_spec, o_spec, o_shape, grid, name):
    nk = grid[-1]

    def ld(ref):
        return ref[0] if len(ref.shape) == 3 else ref[...]

    def body(a_ref, b_ref, o_ref, acc):
        k = pl.program_id(len(grid) - 1)

        @pl.when(k == 0)
        def _():
            acc[...] = jnp.zeros_like(acc)

        acc[...] += _dot(ld(a_ref), ld(b_ref), _TN)

        @pl.when(k == nk - 1)
        def _():
            if len(o_ref.shape) == 3:
                o_ref[0] = acc[...].astype(o_ref.dtype)
            else:
                o_ref[...] = acc[...].astype(o_ref.dtype)

    return pl.pallas_call(
        body, grid=grid, name=name, in_specs=[a_spec, b_spec], out_specs=o_spec, out_shape=SDS(o_shape, BF16),
        scratch_shapes=[pltpu.VMEM(o_spec.block_shape[-2:], F32)], compiler_params=_cp(len(grid)))(a, b)


def _peer(k):
    x, y, c = lax.axis_index("x"), lax.axis_index("y"), lax.axis_index("c")
    px = 1 - x if k & 4 else x
    py = 1 - y if k & 2 else y
    pc = 1 - c if k & 1 else c
    return (px, py, pc), 4 * px + 2 * py + pc


def _exchange_copies(srcs, lands, send_sems, recv_sems, which, gather):
    _, me = _peer(0)
    pairs = []
    for pos, a in enumerate(which):
        for k in range(1, N_DEV):
            dev, idx = _peer(k)
            sem = pos * (N_DEV - 1) + k - 1
            src = srcs[a] if gather else srcs[a].at[idx]
            mk = functools.partial(pltpu.make_async_remote_copy, src_ref=src, send_sem=send_sems.at[sem],
                                   recv_sem=recv_sems.at[sem], device_id=dev, device_id_type=MESH)
            pairs.append((mk(dst_ref=lands[a].at[me]), mk(dst_ref=lands[a].at[idx])))
    return pairs


def _sequencer_kernel(name, collective_id, n_remote, n_local):
    return pl.kernel(mesh=plsc.ScalarSubcoreMesh(axis_name="sequencer", num_cores=1), name=name,
                     scratch_types=(pltpu.SemaphoreType.DMA((n_remote,)), pltpu.SemaphoreType.DMA((n_remote,)),
                                    pltpu.SemaphoreType.DMA((n_local,))),
                     compiler_params=pltpu.CompilerParams(collective_id=collective_id))


def _handshake(ks):
    barrier = pltpu.get_barrier_semaphore()
    for k in ks:
        pl.semaphore_signal(barrier, inc=1, device_id=_peer(k)[0], device_id_type=MESH)
    pl.semaphore_wait(barrier, len(ks))


def _sequencer_scatter(arrays, name, collective_id):
    n = len(arrays)
    hbm = pltpu.MemorySpace.HBM
    srcs = [jax.new_ref(a, memory_space=hbm) for a in arrays]
    lands = [jax.empty_ref(SDS(a.shape, a.dtype), memory_space=hbm) for a in arrays]

    @_sequencer_kernel(name, collective_id, n * (N_DEV - 1), n)
    def launch(send_sems, recv_sems, local_sems):
        _handshake(range(1, N_DEV))
        _, me = _peer(0)
        local = [pltpu.make_async_copy(srcs[a].at[me], lands[a].at[me], local_sems.at[a]) for a in range(n)]
        pairs = _exchange_copies(srcs, lands, send_sems, recv_sems, range(n), False)
        for out, _ in pairs:
            out.start()
        for cp in local:
            cp.start()
        for out, arrival in pairs:
            out.wait_send()
            arrival.wait_recv()
        for cp in local:
            cp.wait()

    launch()
    return [r[...] for r in lands]


SIBLING = 1
OTHER_CHIPS = (2, 4, 6)


def _sequencer_gather(arrays, name, collective_id):
    n = len(arrays)
    hbm = pltpu.MemorySpace.HBM
    srcs = [jax.new_ref(a, memory_space=hbm) for a in arrays]
    lands = [jax.empty_ref(SDS((N_DEV,) + a.shape, a.dtype), memory_space=hbm) for a in arrays]

    @_sequencer_kernel(name, collective_id, n * (N_DEV - 1), n)
    def launch(send_sems, recv_sems, local_sems):
        _handshake((SIBLING,) + OTHER_CHIPS)
        _, me = _peer(0)
        sibling, _ = _peer(SIBLING)

        def copy(a, k, src, block, to):
            sem = a * (N_DEV - 1) + k - 1
            return pltpu.make_async_remote_copy(src_ref=src, dst_ref=lands[a].at[block], send_sem=send_sems.at[sem],
                                                recv_sem=recv_sems.at[sem], device_id=to, device_id_type=MESH)

        local = [pltpu.make_async_copy(srcs[a], lands[a].at[me], local_sems.at[a]) for a in range(n)]
        first = [copy(a, k, srcs[a], me, _peer(k)[0]) for a in range(n) for k in OTHER_CHIPS + (SIBLING,)]
        for cp in first + local:
            cp.start()
        passed = []
        for a in range(n):
            for k in OTHER_CHIPS:
                _, block = _peer(k)
                copy(a, k, srcs[a], block, sibling).wait_recv()
                passed.append(copy(a, k ^ SIBLING, lands[a].at[block], block, sibling))
                passed[-1].start()
        for a in range(n):
            for k in (SIBLING,) + tuple(k ^ SIBLING for k in OTHER_CHIPS):
                copy(a, k, srcs[a], _peer(k)[1], sibling).wait_recv()
        for cp in first + passed:
            cp.wait_send()
        for cp in local:
            cp.wait()

    launch()
    return [r[...] for r in lands]


def _sequencer_gather_chips(array, name, collective_id, chips):
    hbm = pltpu.MemorySpace.HBM
    src = jax.new_ref(array, memory_space=hbm)
    land = jax.empty_ref(SDS((2 * len(chips),) + array.shape, array.dtype), memory_space=hbm)

    @_sequencer_kernel(name, collective_id, 2 * len(chips), 1)
    def launch(send_sems, recv_sems, local_sems):
        _handshake((SIBLING,) + tuple(k for k in chips if k))
        c = lax.axis_index("c")
        sibling, _ = _peer(SIBLING)

        def copy(sem, src_ref, slot, to):
            return pltpu.make_async_remote_copy(src_ref=src_ref, dst_ref=land.at[slot], send_sem=send_sems.at[sem],
                                                recv_sem=recv_sems.at[sem], device_id=to, device_id_type=MESH)

        started = []
        for pos, k in enumerate(chips):
            started.append(copy(2 * pos, src, 2 * pos + c, _peer(k)[0] if k else sibling))
            started[-1].start()
        for pos, k in enumerate(chips):
            if k:
                copy(2 * pos, src, 2 * pos + c, sibling).wait_recv()
                started.append(copy(2 * pos + 1, land.at[2 * pos + c], 2 * pos + c, sibling))
                started[-1].start()
        for pos, k in enumerate(chips):
            copy(2 * pos + 1 if k else 2 * pos, src, 2 * pos + 1 - c, sibling).wait_recv()
        for cp in started:
            cp.wait_send()

    launch()
    return land[...]


SMALL_ROWS = 64


def _small_step(part, w, m, v):
    def body(p_ref, w_ref, m_ref, v_ref, g_ref, d_ref, nm_ref, nv_ref, gath, send_sems, recv_sems):
        _, me = _peer(0)
        gath[me] = p_ref[...]
        copies = []
        for k in range(1, N_DEV):
            dev, idx = _peer(k)
            out = pltpu.make_async_remote_copy(src_ref=p_ref, dst_ref=gath.at[me], send_sem=send_sems.at[k - 1],
                                               recv_sem=recv_sems.at[k - 1], device_id=dev, device_id_type=MESH)
            out.start()
            arrival = pltpu.make_async_remote_copy(src_ref=p_ref, dst_ref=gath.at[idx], send_sem=send_sems.at[k - 1],
                                                   recv_sem=recv_sems.at[k - 1], device_id=dev, device_id_type=MESH)
            copies.append((out, arrival))
        for out, arrival in copies:
            out.wait_send()
            arrival.wait_recv()
        g = gath[0]
        for p in range(1, N_DEV):
            g = g + gath[p]
        g_ref[...] = g
        d_ref[...], nm_ref[...], nv_ref[...] = _adamw(w_ref[...], g, m_ref[...], v_ref[...])

    vm = pl.BlockSpec(memory_space=pltpu.VMEM)
    return pl.pallas_call(
        body, name="small_step", in_specs=[vm] * 4, out_specs=[vm] * 4,
        out_shape=[SDS((SMALL_ROWS, 128), F32)] * 4,
        scratch_shapes=[pltpu.VMEM((N_DEV, SMALL_ROWS, 128), F32), pltpu.SemaphoreType.DMA((N_DEV - 1,)),
                        pltpu.SemaphoreType.DMA((N_DEV - 1,))])(part, w, m, v)


def _adamw(w, g, m, v):
    m = ADAM_B1 * m + (1.0 - ADAM_B1) * g
    v = ADAM_B2 * v + (1.0 - ADAM_B2) * (g * g)
    m_hat = m / (1.0 - ADAM_B1 ** ADAM_STEP)
    v_hat = v / (1.0 - ADAM_B2 ** ADAM_STEP)
    delta = -ADAM_LR * (m_hat / (jnp.sqrt(v_hat) + ADAM_EPS) + ADAM_WD * w)
    return delta, m, v


def _adamw_block(parts, w, m, v, name):
    R, C = w.shape
    n_parts = len(parts)
    Rp = R // n_parts
    tr = next(t for t in (256, 128, 64, 32, 16, 8) if Rp % t == 0 and t * C <= 256 * 1024)
    per_part = Rp // tr

    def body(*refs):
        p_refs = refs[:n_parts]
        w_ref, m_ref, v_ref, g_ref, d_ref, nm_ref, nv_ref = refs[n_parts:]
        for k, p_ref in enumerate(p_refs):
            @pl.when(pl.program_id(0) // per_part == k)
            def _(p_ref=p_ref):
                g = p_ref[0].astype(F32)
                for p in range(1, N_DEV):
                    g = g + p_ref[p].astype(F32)
                g_ref[...] = g
                d_ref[...], nm_ref[...], nv_ref[...] = _adamw(w_ref[...], g, m_ref[...], v_ref[...])

    row = pl.BlockSpec((tr, C), lambda i: (i, 0))
    part_specs = [pl.BlockSpec((N_DEV, tr, C), functools.partial(
        lambda i, k: (0, jnp.clip(i - k * per_part, 0, per_part - 1), 0), k=k)) for k in range(n_parts)]
    return pl.pallas_call(
        body, grid=(R // tr,), name=name, in_specs=part_specs + [row, row, row],
        out_specs=[row] * 4, out_shape=[SDS((R, C), F32)] * 4, compiler_params=_cp(1))(*parts, w, m, v)


def _pack_small(mix, ffn, fin, retw, dec_f, dec_b, loss):
    flat = jnp.concatenate([mix.reshape(-1), ffn.reshape(-1), fin.reshape(-1), retw.reshape(-1), dec_f.reshape(-1),
                            dec_b.reshape(-1), loss.reshape(-1)])
    return jnp.pad(flat, (0, SMALL_ROWS * 128 - flat.shape[0])).reshape(SMALL_ROWS, 128)


def _unpack_small(packed, shapes):
    flat = packed.reshape(-1)
    out, at = [], 0
    for s in shapes:
        n = math.prod(s)
        out.append(flat[at:at + n].reshape(s))
        at += n
    return out


def kernel(x, norm_mix_w, w_in, ret_decay_fwd, ret_decay_bwd, ret_norm_w, w_out, norm_ffn_w, w_gate, w_up, w_down, norm_final_w, loss_target, m_norm_mix_w, m_w_in, m_ret_decay_fwd, m_ret_decay_bwd, m_ret_norm_w, m_w_out, m_norm_ffn_w, m_w_gate, m_w_up, m_w_down, m_norm_final_w, v_norm_mix_w, v_w_in, v_ret_decay_fwd, v_ret_decay_bwd, v_ret_norm_w, v_w_out, v_norm_ffn_w, v_w_gate, v_w_up, v_w_down, v_norm_final_w):
    x2 = x[0]
    tgt = loss_target[0]
    S, D = x2.shape
    H = ret_norm_w.shape[1] // HEAD_DIM
    DA = H * HEAD_DIM
    fin_w = norm_final_w.reshape(1, D)
    big = (w_in[0], w_out[0], w_gate[0].T, w_up[0].T, w_down[0])

    big_b = [w.astype(BF16) for w in big]
    stages = ((0,), (4, 2), (6,))
    wi_stages = [_sequencer_gather_chips(big_b[0], name, cid, ks)
                 for name, cid, ks in zip(("gather_in_own", "gather_in_near", "gather_in_far"), (0, 7, 8), stages)]
    wo, = _sequencer_gather(big_b[1:2], "gather_out", 1)
    wg, wu = _sequencer_gather(big_b[2:4], "gather_gate_up", 9)
    wd, = _sequencer_gather(big_b[4:], "gather_down", 5)
    wi, = _sequencer_gather(big_b[:1], "gather_in_ordered", 10)
    NB = big_b[0].shape[1]
    ax, ay = lax.axis_index("x"), lax.axis_index("y")
    chip_of = {k: 2 * (1 - ax if k & 4 else ax) + (1 - ay if k & 2 else ay) for k in (0, 2, 4, 6)}

    n1 = _norm_fwd(x2, norm_mix_w)
    ac = lax.axis_index("c")
    me = 2 * chip_of[0] + ac
    vec = lambda *v: jnp.stack([jnp.asarray(t, jnp.int32) for t in v])
    proj = _proj_part(n1, big_b[0][None], vec(0), vec(me), None, N_DEV, "proj_self")
    for ks, w_st, name in zip(stages, wi_stages, ("proj_sibling", "proj_near", "proj_far")):
        slots, blocks = [], []
        for pos, k in enumerate(ks):
            for core in ((1 - ac,) if k == 0 else (0, 1)):
                slots.append(2 * pos + core)
                blocks.append(2 * chip_of[k] + core)
        proj = _proj_part(n1, w_st, vec(*slots), vec(*blocks), proj, N_DEV, name)
    bias = _attn_bias()[:H]
    attn, lse = _attn_fwd(proj, bias)
    ret, o_raw = _ret_fwd(proj, ret_decay_fwd, ret_decay_bwd, ret_norm_w)
    wo_full = wo.reshape(D, D)
    d_ff = N_DEV * wd.shape[1]
    FB = FFN_BLOCK if d_ff % FFN_BLOCK == 0 else wd.shape[1]
    n_fb = d_ff // FB
    wg, wu = wg.reshape(n_fb, FB, D), wu.reshape(n_fb, FB, D)
    wd_full = wd.reshape(d_ff, D)
    h1, mixed, n2 = _out_fwd(x2, attn, ret, wo_full, norm_ffn_w)
    gate, up, act = _ffn_up(n2, wg, wu)
    dh2, dh2_b, loss_parts, g_fin = _ffn_down_loss(act, wd_full, h1, tgt, fin_w)

    tn = min(1024, D)
    ffn_specs = (pl.BlockSpec((1, S, FB), lambda j, n, k: (j, 0, 0)), pl.BlockSpec((S, tn), lambda j, n, k: (0, n)),
                 pl.BlockSpec((1, FB, tn), lambda j, n, k: (j, 0, n)), (n_fb, FB, D), (n_fb, D // tn, 1))
    per_dev = (N_DEV, d_ff // N_DEV, D)
    g_wd = _wgrad(act, dh2_b, *ffn_specs, "wgrad_down").reshape(per_dev)
    part_d, = _sequencer_scatter([g_wd], "scatter_down", 2)
    dgate, dup = _ffn_bwd_act(dh2_b, wd_full, gate, up)
    g_wg = _wgrad(dgate, n2, *ffn_specs, "wgrad_gate").reshape(per_dev)
    part_g, = _sequencer_scatter([g_wg], "scatter_gate", 11)
    g_wu = _wgrad(dup, n2, *ffn_specs, "wgrad_up").reshape(per_dev)
    part_u, = _sequencer_scatter([g_wu], "scatter_up", 12)
    parts_f = [part_g, part_u, part_d]
    dh1, dh1_b, g_ffn = _ffn_bwd_in(dgate, dup, wg, wu, h1, dh2, norm_ffn_w)
    dmix = _dmix(dh1_b, wo_full)
    tmw = min(512, D)
    tk = min(2048, S)
    g_wo = _wgrad(mixed, dh1_b, pl.BlockSpec((tk, tmw), lambda m, k: (k, m)), pl.BlockSpec((tk, D), lambda m, k: (k, 0)),
                  pl.BlockSpec((tmw, D), lambda m, k: (m, 0)), (D, D), (D // tmw, S // tk), "wgrad_out")
    parts_o = _sequencer_scatter([g_wo.reshape(N_DEV, D // N_DEV, D)], "scatter_out", 3)
    d_ret, small_w, dproj = _ret_gate_bwd(proj, o_raw, dmix, ret_norm_w, DA, lax.empty(proj.shape, BF16))
    small, dproj = _ret_bwd(proj, d_ret, ret_decay_fwd, ret_decay_bwd, dproj)
    dproj = _attn_bwd(proj, attn, lse, dmix, bias, dproj)
    half = D // tmw // 2
    parts_i = []
    for part, (name, cid) in enumerate((("in_lo", 4), ("in_hi", 6))):
        g_wi = _wgrad(n1, dproj, pl.BlockSpec((S, tmw), functools.partial(lambda j, m, k, off: (0, m + off), off=part * half)),
                      pl.BlockSpec((S, NB), lambda j, m, k: (0, j)), pl.BlockSpec((1, tmw, NB), lambda j, m, k: (j, m, 0)),
                      (N_DEV, D // 2, NB), (N_DEV, half, 1), "wgrad_" + name)
        parts_i += _sequencer_scatter([g_wi], "scatter_" + name, cid)
    grad_x, g_mix = _in_bwd(dproj, wi, x2, dh1, norm_mix_w)

    big_m = (m_w_in[0], m_w_out[0], m_w_gate[0].T, m_w_up[0].T, m_w_down[0])
    big_v = (v_w_in[0], v_w_out[0], v_w_gate[0].T, v_w_up[0].T, v_w_down[0])
    names = ("adamw_in", "adamw_out", "adamw_gate", "adamw_up", "adamw_down")
    upd = [None] * 5
    for a, p in zip((2, 3, 4, 1, 0), [[t] for t in parts_f + parts_o] + [parts_i]):
        upd[a] = _adamw_block(p, big[a], big_m[a], big_v[a], names[a])

    g_dec_f = small[:, 0, 0].reshape(1, H)
    g_dec_b = small[:, 1, 0].reshape(1, H)
    g_retw = small_w[:, 0, :].reshape(1, DA)
    loss_local = jnp.sum(loss_parts[::8, 0])
    zero = jnp.zeros((1,), F32)
    part = _pack_small(g_mix, g_ffn, g_fin, g_retw, g_dec_f, g_dec_b, loss_local)
    sw = _pack_small(norm_mix_w, norm_ffn_w, norm_final_w, ret_norm_w, ret_decay_fwd, ret_decay_bwd, zero)
    sm = _pack_small(m_norm_mix_w, m_norm_ffn_w, m_norm_final_w, m_ret_norm_w, m_ret_decay_fwd, m_ret_decay_bwd, zero)
    sv = _pack_small(v_norm_mix_w, v_norm_ffn_w, v_norm_final_w, v_ret_norm_w, v_ret_decay_fwd, v_ret_decay_bwd, zero)
    shapes = [(1, D), (1, D), (D,), (1, DA), (1, H), (1, H), ()]
    sg, sd, snm, snv = [_unpack_small(t, shapes) for t in _small_step(part, sw, sm, sv)]
    loss = sg[6]

    def ordered(small_set, k):
        b = [(u[k].T if a in (2, 3) else u[k])[None] for a, u in enumerate(upd)]
        return [small_set[0], b[0], small_set[4], small_set[5], small_set[3], b[1], small_set[1], b[2], b[3], b[4],
                small_set[2]]

    return (loss, grad_x[None], *ordered(sg, 0), *ordered(sd, 1), *ordered(snm, 2), *ordered(snv, 3))
```

```python
import functools
import math

import numpy as np
import jax
import jax.numpy as jnp
from jax import lax
from jax.experimental import pallas as pl
from jax.experimental.pallas import tpu as pltpu
from jax.experimental.pallas import tpu_sc as plsc

F32 = jnp.float32
BF16 = jnp.bfloat16
SDS = jax.ShapeDtypeStruct

HEAD_DIM = 128
EPS = 1e-6
RET_CHUNK = 128
DILATIONS = (1, 4, 16)
BAND = 64
Q_TILE = 128
K_TILE = Q_TILE + 2 * BAND
KV_PAD = BAND * 4
TILE_GROUP = 8
BWD_TILE_GROUP = 8
NEG = -1e30
N_DEV = 8
N_GROUPS = 7
ADAM_LR, ADAM_B1, ADAM_B2, ADAM_EPS, ADAM_WD, ADAM_STEP = 0.001, 0.9, 0.999, 1e-08, 0.01, 10
VMEM_LIMIT = 56 * 1024 * 1024
MESH = pl.DeviceIdType.MESH
ANY = pl.BlockSpec(memory_space=pl.ANY)


def _cp(n_grid):
    return pltpu.CompilerParams(dimension_semantics=("arbitrary",) * n_grid, vmem_limit_bytes=VMEM_LIMIT)


def _sigmoid(x):
    return 1.0 / (1.0 + jnp.exp(-x))


def _rms_scale(h):
    return lax.rsqrt(jnp.mean(h * h, axis=-1, keepdims=True) + EPS)


def _rms_bwd(dn, h, w):
    r = _rms_scale(h)
    gw = dn * w
    dh = r * gw - h * (r * r * r) * jnp.mean(gw * h, axis=-1, keepdims=True)
    return dh, jnp.sum(dn * h * r, axis=0, keepdims=True)


def _dot(a, b, dims):
    return lax.dot_general(a.astype(BF16), b.astype(BF16), (dims, ((), ())), preferred_element_type=F32)


_NN = ((1,), (0,))
_NT = ((1,), (1,))
_TN = ((0,), (0,))


RESIDENT_ROWS = 256


def _resident(shape):
    return pl.BlockSpec(shape, lambda i: (0, 0), pipeline_mode=pl.Buffered(1))


def _blocked_matmul(a_ref, w_ref):
    nblk, _, fb = a_ref.shape
    out = None
    for j in range(nblk):
        part = jnp.dot(a_ref[j], w_ref[pl.ds(j * fb, fb), :], preferred_element_type=F32)
        out = part if out is None else out + part
    return out


def _norm_fwd(x, w_norm):
    S, D = x.shape
    tm = min(1024, S)

    def body(x_ref, wn_ref, n_ref):
        xf = x_ref[...]
        n_ref[...] = (xf * _rms_scale(xf) * wn_ref[...]).astype(BF16)

    row = pl.BlockSpec((tm, D), lambda i: (i, 0))
    return pl.pallas_call(body, grid=(S // tm,), name="norm_fwd", in_specs=[row, pl.BlockSpec((1, D), lambda i: (0, 0))],
                          out_specs=row, out_shape=SDS((S, D), BF16), compiler_params=_cp(1))(x, w_norm)


def _proj_part(n1, w_slots, slots, blocks, proj, n_blocks, name):
    S, D = n1.shape
    NB = w_slots.shape[2]
    tm = min(1024, S)

    def body(slots_ref, blocks_ref, n_ref, w_ref, *rest):
        rest[-1][...] = jnp.dot(n_ref[...], w_ref[0], preferred_element_type=F32)

    out_spec = pl.BlockSpec((tm, NB), lambda i, j, slots, blocks: (i, blocks[j]))
    in_specs = [pl.BlockSpec((tm, D), lambda i, j, slots, blocks: (i, 0)),
                pl.BlockSpec((1, D, NB), lambda i, j, slots, blocks: (slots[j], 0, 0))]
    args = [n1, w_slots]
    if proj is not None:
        in_specs.append(ANY)
        args.append(proj)
    return pl.pallas_call(
        body, name=name, out_shape=SDS((S, n_blocks * NB), F32),
        grid_spec=pltpu.PrefetchScalarGridSpec(num_scalar_prefetch=2, grid=(S // tm, slots.shape[0]), in_specs=in_specs,
                                               out_specs=out_spec),
        input_output_aliases={} if proj is None else {4: 0},
        compiler_params=_cp(2))(slots, blocks, *args)


def _out_fwd(x, attn, ret, w_out, w_norm):
    S, D = x.shape
    DA = attn.shape[1]
    tm = min(512, S)

    def body(x_ref, a_ref, r_ref, w_ref, wn_ref, h_ref, mix_ref, n_ref):
        a = a_ref[...].astype(BF16)
        r = r_ref[...].astype(BF16)
        mix_ref[:, :DA] = a
        mix_ref[:, DA:] = r
        h = x_ref[...] + jnp.dot(a, w_ref[:DA, :], preferred_element_type=F32) \
            + jnp.dot(r, w_ref[DA:, :], preferred_element_type=F32)
        h_ref[...] = h
        n_ref[...] = (h * _rms_scale(h) * wn_ref[...]).astype(BF16)

    row = lambda w: pl.BlockSpec((tm, w), lambda i: (i, 0))
    return pl.pallas_call(
        body, grid=(S // tm,), name="out_fwd",
        in_specs=[row(D), row(DA), row(D - DA), pl.BlockSpec((D, D), lambda i: (0, 0)),
                  pl.BlockSpec((1, D), lambda i: (0, 0))],
        out_specs=[row(D), row(D), row(D)],
        out_shape=[SDS((S, D), F32), SDS((S, D), BF16), SDS((S, D), BF16)],
        compiler_params=_cp(1))(x, attn, ret, w_out, w_norm)


def _ffn_up(n2, wg, wu):
    S, D = n2.shape
    nblk, FB, _ = wg.shape
    tm = min(1024, S)

    def body(n_ref, wg_ref, wu_ref, g_ref, u_ref, a_ref):
        n = n_ref[...]
        g = _dot(n, wg_ref[0], _NT)
        u = _dot(n, wu_ref[0], _NT)
        g_ref[0] = g.astype(BF16)
        u_ref[0] = u.astype(BF16)
        a_ref[0] = (g * _sigmoid(g) * u).astype(BF16)

    wspec = pl.BlockSpec((1, FB, D), lambda j, i: (j, 0, 0))
    ospec = pl.BlockSpec((1, tm, FB), lambda j, i: (j, i, 0))
    return pl.pallas_call(
        body, grid=(nblk, S // tm), name="ffn_up",
        in_specs=[pl.BlockSpec((tm, D), lambda j, i: (i, 0)), wspec, wspec],
        out_specs=[ospec, ospec, ospec],
        out_shape=[SDS((nblk, S, FB), BF16)] * 3,
        compiler_params=_cp(2))(n2, wg, wu)


def _ffn_down_loss(act, wd, h1, target, w_norm):
    nblk, S, FB = act.shape
    D = h1.shape[1]
    tm = min(RESIDENT_ROWS, S)

    def body(a_ref, wd_ref, h_ref, t_ref, wn_ref, dh_ref, dhb_ref, loss_ref, dw_ref):
        @pl.when(pl.program_id(0) == 0)
        def _():
            dw_ref[...] = jnp.zeros_like(dw_ref)

        h = h_ref[...] + _blocked_matmul(a_ref, wd_ref)
        w = wn_ref[...]
        err = h * _rms_scale(h) * w - t_ref[...]
        loss_ref[...] = jnp.full(loss_ref.shape, 0.5 * jnp.sum(err * err) / D, F32)
        dh, dw = _rms_bwd(err * (1.0 / D), h, w)
        dh_ref[...] = dh
        dhb_ref[...] = dh.astype(BF16)
        dw_ref[...] += dw

    row = pl.BlockSpec((tm, D), lambda i: (i, 0))
    vec = pl.BlockSpec((1, D), lambda i: (0, 0))
    return pl.pallas_call(
        body, grid=(S // tm,), name="ffn_down_loss",
        in_specs=[pl.BlockSpec((nblk, tm, FB), lambda i: (0, i, 0)), _resident((nblk * FB, D)), row, row, vec],
        out_specs=[row, row, pl.BlockSpec((8, 128), lambda i: (i, 0)), vec],
        out_shape=[SDS((S, D), F32), SDS((S, D), BF16), SDS((S // tm * 8, 128), F32), SDS((1, D), F32)],
        compiler_params=_cp(1))(act, wd, h1, target, w_norm)


def _attn_bias():
    n_heads = 8
    slopes = np.exp2(-8.0 * np.arange(1, n_heads + 1, dtype=np.float32) / n_heads)
    dist = np.abs(np.arange(K_TILE)[None, :] - BAND - np.arange(Q_TILE)[:, None])
    out = np.empty((n_heads, len(DILATIONS), Q_TILE, K_TILE), np.float32)
    for h in range(n_heads):
        for p, d in enumerate(DILATIONS):
            out[h, p] = np.where(dist <= BAND, -slopes[h] * (d * dist).astype(np.float32), NEG)
    return jnp.asarray(out)


def _attn_tiles(S, d):
    L = S // d
    per_class = L // Q_TILE
    return L, per_class, d * per_class


def _tile_rows(t, d, per_class):
    r = t // per_class
    a = (t % per_class) * Q_TILE
    q_rows = pl.ds(r + d * a, Q_TILE, stride=d) if d > 1 else pl.ds(pl.multiple_of(a, Q_TILE), Q_TILE)
    k_rows = pl.ds(KV_PAD + r + d * (a - BAND), K_TILE, stride=d) if d > 1 else pl.ds(
        pl.multiple_of(KV_PAD + a - BAND, BAND), K_TILE)
    return a, q_rows, k_rows


def _to_quarters(dst, src, n, dst_off=0):
    for r in range(4):
        dst[pl.ds(dst_off + r * (n // 4), n // 4), :] = src[pl.ds(r, n // 4, stride=4), :]


def _quarter_tile_rows(t, S):
    L = S // 16
    per_class = L // Q_TILE
    blk, tt = t // (4 * per_class), t % (4 * per_class)
    r, a = tt // per_class, (tt % per_class) * Q_TILE
    q_rows = pl.ds(blk * (S // 4) + r + 4 * a, Q_TILE, stride=4)
    k_rows = pl.ds(KV_PAD + blk * (S // 4) + r + 4 * (a - BAND), K_TILE, stride=4)
    return a, q_rows, k_rows


def _quarter_band_rows(t, S):
    L = S // 4
    per_quarter = L // Q_TILE
    blk, a = t // per_quarter, (t % per_quarter) * Q_TILE
    q_rows = pl.ds(pl.multiple_of(blk * L + a, Q_TILE), Q_TILE)
    k_rows = pl.ds(pl.multiple_of(KV_PAD + blk * L + a - BAND, BAND), K_TILE)
    return a, q_rows, k_rows


def _lanes(x, width):
    return jnp.concatenate([x] * (width // HEAD_DIM), axis=-1)


_BNT = (((2,), (2,)), ((0,), (0,)))
_BNN = (((2,), (1,)), ((0,), (0,)))
_BTN = (((1,), (1,)), ((0,), (0,)))


def _bdot(a, b, dims):
    return lax.dot_general(a, b, dims, preferred_element_type=F32)


def _stacked(rows, loaders):
    return [jnp.stack([f(*r) for r in rows]) for f in loaders]


def _edge_mask(a, L):
    lk = lax.broadcasted_iota(jnp.int32, (1, K_TILE), 1) + (a - BAND)
    return jnp.where((lk >= 0) & (lk < L), 0.0, NEG).astype(F32)


def _fill_padded(dst, src, S):
    dst[pl.ds(0, KV_PAD), :] = jnp.zeros((KV_PAD, HEAD_DIM), F32)
    dst[pl.ds(KV_PAD + S, KV_PAD), :] = jnp.zeros((KV_PAD, HEAD_DIM), F32)
    dst[pl.ds(KV_PAD, S), :] = src[...]


def _head_specs(S, groups, n_heads):
    return [pl.BlockSpec((S, HEAD_DIM), functools.partial(lambda h, g: (0, g * n_heads + h), g=g)) for g in groups]


def _attn_fwd(proj, bias):
    S = proj.shape[0]
    H = proj.shape[1] // (N_GROUPS * HEAD_DIM)
    scale = HEAD_DIM ** -0.5

    def body(q_ref, k_ref, v_ref, b_ref, o_ref, lse_ref, kp, vp, m_run, l_run, q4, m3, l3, acc3):
        _fill_padded(kp, k_ref, S)
        _fill_padded(vp, v_ref, S)
        o_ref[...] = jnp.zeros_like(o_ref)
        m_run[...] = jnp.full(m_run.shape, NEG, F32)
        l_run[...] = jnp.zeros_like(l_run)
        def online(n_tiles, tile_rows, p, L, q_src, m_buf, l_buf, o_buf):
            def tiles(t, carry):
                rows = [tile_rows(t + u * (n_tiles // TILE_GROUP)) for u in range(TILE_GROUP)]
                qs, ks, vs, m_old, l_old, o_old, edge = _stacked(rows, (
                    lambda a, qr, kr: q_src[qr, :].astype(BF16), lambda a, qr, kr: kp[kr, :].astype(BF16),
                    lambda a, qr, kr: vp[kr, :].astype(BF16), lambda a, qr, kr: m_buf[qr, :],
                    lambda a, qr, kr: l_buf[qr, :], lambda a, qr, kr: o_buf[qr, :], lambda a, qr, kr: _edge_mask(a, L)))
                s = _bdot(qs, ks, _BNT) * scale + b_ref[0, p][None] + edge
                m_new = jnp.maximum(m_old, jnp.max(s, axis=-1, keepdims=True))
                pr = jnp.exp(s - _lanes(m_new, K_TILE)).astype(BF16)
                alpha = jnp.exp(m_old - m_new)
                l_new = alpha * l_old + _bdot(pr, jnp.ones((TILE_GROUP, K_TILE, HEAD_DIM), BF16), _BNN)
                o_new = alpha * o_old + _bdot(pr, vs, _BNN)
                for u, (_, qr, _) in enumerate(rows):
                    o_buf[qr, :] = o_new[u]
                    m_buf[qr, :] = m_new[u]
                    l_buf[qr, :] = l_new[u]
                return carry

            lax.fori_loop(0, n_tiles // TILE_GROUP, tiles, 0)

        L, per_class, n_tiles = _attn_tiles(S, DILATIONS[0])
        online(n_tiles, functools.partial(_tile_rows, d=DILATIONS[0], per_class=per_class), 0, L, q_ref, m_run, l_run, o_ref)

        _to_quarters(q4, q_ref, S)
        _to_quarters(kp, k_ref, S, KV_PAD)
        _to_quarters(vp, v_ref, S, KV_PAD)
        n_tiles = _attn_tiles(S, DILATIONS[2])[2]

        def tiles3(t, carry):
            rows = [_quarter_tile_rows(t + u * (n_tiles // TILE_GROUP), S) for u in range(TILE_GROUP)]
            qs, ks, vs, edge = _stacked(rows, (
                lambda a, qr, kr: q4[qr, :].astype(BF16), lambda a, qr, kr: kp[kr, :].astype(BF16),
                lambda a, qr, kr: vp[kr, :].astype(BF16), lambda a, qr, kr: _edge_mask(a, S // DILATIONS[2])))
            s = _bdot(qs, ks, _BNT) * scale + b_ref[0, 2][None] + edge
            m_new = jnp.broadcast_to(jnp.max(s, axis=-1, keepdims=True), (TILE_GROUP, Q_TILE, HEAD_DIM))
            pr = jnp.exp(s - _lanes(m_new, K_TILE)).astype(BF16)
            l_new = _bdot(pr, jnp.ones((TILE_GROUP, K_TILE, HEAD_DIM), BF16), _BNN)
            o_new = _bdot(pr, vs, _BNN)
            for u, (_, qr, _) in enumerate(rows):
                acc3[qr, :] = o_new[u]
                m3[qr, :] = m_new[u]
                l3[qr, :] = l_new[u]
            return carry

        lax.fori_loop(0, n_tiles // TILE_GROUP, tiles3, 0)
        online(_attn_tiles(S, DILATIONS[1])[2], functools.partial(_quarter_band_rows, S=S), 1, S // DILATIONS[1],
               q4, m3, l3, acc3)
        for r in range(4):
            nat, qtr = pl.ds(r, S // 4, stride=4), pl.ds(r * (S // 4), S // 4)
            m_a, m_b = m_run[nat, :], m3[qtr, :]
            m = jnp.maximum(m_a, m_b)
            w_a, w_b = jnp.exp(m_a - m), jnp.exp(m_b - m)
            l = w_a * l_run[nat, :] + w_b * l3[qtr, :]
            o_ref[nat, :] = (w_a * o_ref[nat, :] + w_b * acc3[qtr, :]) / l
            lse_ref[nat, :] = m + jnp.log(l)

    hspec = pl.BlockSpec((S, HEAD_DIM), lambda h: (0, h))
    padded, plain = pltpu.VMEM((S + 2 * KV_PAD, HEAD_DIM), F32), pltpu.VMEM((S, HEAD_DIM), F32)
    return pl.pallas_call(
        body, grid=(H,), name="attn_fwd",
        in_specs=_head_specs(S, (0, 1, 2), H) + [
            pl.BlockSpec((1, len(DILATIONS), Q_TILE, K_TILE), lambda h: (h, 0, 0, 0))],
        out_specs=[hspec, hspec],
        out_shape=[SDS((S, H * HEAD_DIM), F32), SDS((S, H * HEAD_DIM), F32)],
        scratch_shapes=[padded, padded] + [plain] * 6,
        compiler_params=_cp(1))(proj, proj, proj, bias)


def _put_groups(stage, dproj, groups, n_heads, sems):
    h = pl.program_id(0)
    copies = [pltpu.make_async_copy(
        stage.at[i], dproj.at[:, pl.ds(pl.multiple_of((g * n_heads + h) * HEAD_DIM, HEAD_DIM), HEAD_DIM)], sems.at[i])
        for i, g in enumerate(groups)]
    for cp in copies:
        cp.start()
    for cp in copies:
        cp.wait()


def _attn_bwd(proj, out, lse, dmix, bias, dproj):
    S = proj.shape[0]
    H = proj.shape[1] // (N_GROUPS * HEAD_DIM)
    scale = HEAD_DIM ** -0.5
    assert S // DILATIONS[2] >= 2 * Q_TILE

    def body(q_ref, k_ref, v_ref, o_ref, lse_ref, do_ref, b_ref, dproj_in, dproj_out,
             kp, vp, dkp, dvp, dsum, q4, do4, lse4, dsum4, dq_ref, dk_ref, dv_ref, stage, sems):
        _fill_padded(kp, k_ref, S)
        _fill_padded(vp, v_ref, S)
        dkp[...] = jnp.zeros_like(dkp)
        dvp[...] = jnp.zeros_like(dvp)
        dq_ref[...] = jnp.zeros_like(dq_ref)
        dsum[...] = jnp.broadcast_to(jnp.sum(do_ref[...] * o_ref[...], axis=-1, keepdims=True), dsum.shape)

        def run(n_tiles, tile_rows, p, L, q_src, do_src, lse_src, dsum_src, dq_dst, dq_adds):
            def tiles(t, carry):
                rows = [tile_rows(t + u * (n_tiles // BWD_TILE_GROUP)) for u in range(BWD_TILE_GROUP)]
                qs, ks, vs, dos, lses, dsums, dk_old, dv_old, edge = _stacked(rows, (
                    lambda a, qr, kr: q_src[qr, :].astype(BF16), lambda a, qr, kr: kp[kr, :].astype(BF16),
                    lambda a, qr, kr: vp[kr, :].astype(BF16), lambda a, qr, kr: do_src[qr, :].astype(BF16),
                    lambda a, qr, kr: lse_src[qr, :], lambda a, qr, kr: dsum_src[qr, :],
                    lambda a, qr, kr: dkp[kr, :], lambda a, qr, kr: dvp[kr, :], lambda a, qr, kr: _edge_mask(a, L)))
                s = _bdot(qs, ks, _BNT) * scale + b_ref[0, p][None] + edge
                pr = jnp.exp(s - _lanes(lses, K_TILE))
                ds = (pr * (_bdot(dos, vs, _BNT) - _lanes(dsums, K_TILE)) * scale).astype(BF16)
                dq_new = _bdot(ds, ks, _BNN)
                if dq_adds:
                    dq_new = dq_new + jnp.stack([dq_dst[qr, :] for _, qr, _ in rows])
                dk_new = dk_old + _bdot(ds, qs, _BTN)
                dv_new = dv_old + _bdot(pr.astype(BF16), dos, _BTN)
                for u, (_, qr, kr) in enumerate(rows):
                    dq_dst[qr, :] = dq_new[u]
                    dkp[kr, :] = dk_new[u]
                    dvp[kr, :] = dv_new[u]
                return carry

            lax.fori_loop(0, n_tiles // BWD_TILE_GROUP, tiles, 0)

        L, per_class, n_tiles = _attn_tiles(S, DILATIONS[0])
        run(n_tiles, functools.partial(_tile_rows, d=DILATIONS[0], per_class=per_class), 0, L,
            q_ref, do_ref, lse_ref, dsum, dq_ref, True)
        dk_ref[...] = dkp[pl.ds(KV_PAD, S), :]
        dv_ref[...] = dvp[pl.ds(KV_PAD, S), :]

        for dst, src in ((q4, q_ref), (do4, do_ref), (lse4, lse_ref), (dsum4, dsum)):
            _to_quarters(dst, src, S)
        _to_quarters(kp, k_ref, S, KV_PAD)
        _to_quarters(vp, v_ref, S, KV_PAD)
        dkp[...] = jnp.zeros_like(dkp)
        dvp[...] = jnp.zeros_like(dvp)
        dq3 = dsum
        run(_attn_tiles(S, DILATIONS[2])[2], functools.partial(_quarter_tile_rows, S=S), 2, S // DILATIONS[2],
            q4, do4, lse4, dsum4, dq3, False)
        run(_attn_tiles(S, DILATIONS[1])[2], functools.partial(_quarter_band_rows, S=S), 1, S // DILATIONS[1],
            q4, do4, lse4, dsum4, dq3, True)
        for r in range(4):
            nat, qtr = pl.ds(r, S // 4, stride=4), pl.ds(r * (S // 4), S // 4)
            pad_qtr = pl.ds(KV_PAD + r * (S // 4), S // 4)
            dq_ref[nat, :] = dq_ref[nat, :] + dq3[qtr, :]
            dk_ref[nat, :] = dk_ref[nat, :] + dkp[pad_qtr, :]
            dv_ref[nat, :] = dv_ref[nat, :] + dvp[pad_qtr, :]
        for i, acc in enumerate((dq_ref, dk_ref, dv_ref)):
            stage[i] = acc[...].astype(BF16)
        _put_groups(stage, dproj_out, (0, 1, 2), H, sems)

    hspec = pl.BlockSpec((S, HEAD_DIM), lambda h: (0, h))
    once = pl.BlockSpec((S, HEAD_DIM), lambda h: (0, h), pipeline_mode=pl.Buffered(1))
    padded, plain = pltpu.VMEM((S + 2 * KV_PAD, HEAD_DIM), F32), pltpu.VMEM((S, HEAD_DIM), F32)
    return pl.pallas_call(
        body, grid=(H,), name="attn_bwd",
        in_specs=_head_specs(S, (0, 1, 2), H) + [
            once, hspec, hspec, pl.BlockSpec((1, len(DILATIONS), Q_TILE, K_TILE), lambda h: (h, 0, 0, 0)), ANY],
        out_specs=ANY, out_shape=SDS(dproj.shape, dproj.dtype), input_output_aliases={7: 0},
        scratch_shapes=[padded] * 4 + [plain] * 8 + [pltpu.VMEM((3, S, HEAD_DIM), BF16), pltpu.SemaphoreType.DMA((3,))],
        compiler_params=_cp(1))(proj, proj, proj, out, lse, dmix, bias, dproj)


def _ret_consts(lg, forward):
    C = RET_CHUNK
    i = lax.broadcasted_iota(jnp.int32, (C, C), 0)
    j = lax.broadcasted_iota(jnp.int32, (C, C), 1)
    rel = (i - j) if forward else (j - i)
    inside = (rel >= 0) if forward else (rel > 0)
    relf = jnp.maximum(rel, 0).astype(F32)
    mask = jnp.where(inside, jnp.exp(lg * relf), 0.0)
    idx = lax.broadcasted_iota(jnp.int32, (C, 1), 0).astype(F32)
    q_exp = (idx + 1.0) if forward else (C - idx)
    k_exp = (C - 1.0 - idx) if forward else idx
    return mask, relf, jnp.exp(lg * q_exp), q_exp, jnp.exp(lg * k_exp), k_exp, jnp.exp(lg * C)


def _log_decay(dec_ref, h):
    return -jnp.exp(jnp.full((1, 1), dec_ref[0, h], F32))


FFN_BLOCK = 704
CHUNK_BATCH = 16


def _batch_rows(b):
    n = CHUNK_BATCH * RET_CHUNK
    return pl.ds(pl.multiple_of(b * n, n), n)


def _batch_chunks(b):
    return pl.ds(pl.multiple_of(b * CHUNK_BATCH, CHUNK_BATCH), CHUNK_BATCH)


def _chunks3(x):
    return x.reshape(CHUNK_BATCH, RET_CHUNK, HEAD_DIM)


def _ret_scan(buf, c_decs, nc, reverse):
    def step(n, carry):
        new = []
        for way, r in enumerate(carry):
            c = n if (way == 0) != reverse else nc - 1 - n
            term = buf[way, c]
            buf[way, c] = r
            new.append(r * c_decs[way] + term)
        return tuple(new)

    lax.fori_loop(0, nc, step, (jnp.zeros((HEAD_DIM, HEAD_DIM), F32),) * 2)


def _ret_fwd(proj, dec_f, dec_b, w_norm):
    S = proj.shape[0]
    H = proj.shape[1] // (N_GROUPS * HEAD_DIM)
    nc = S // RET_CHUNK
    scale = HEAD_DIM ** -0.5

    def body(df_ref, db_ref, q_ref, k_ref, v_ref, g_ref, w_ref, y_ref, o_ref, states):
        h = pl.program_id(0)
        consts = [_ret_consts(_log_decay(dref, h), fw) for fw, dref in ((True, df_ref), (False, db_ref))]

        def kv_step(b, carry):
            rows, batch = _batch_rows(b), _batch_chunks(b)
            k3 = _chunks3(k_ref[rows, :])
            v3 = _chunks3(v_ref[rows, :]).astype(BF16)
            for way in range(2):
                states[way, batch] = _bdot((k3 * consts[way][4]).astype(BF16), v3, _BTN)
            return carry

        lax.fori_loop(0, nc // CHUNK_BATCH, kv_step, 0)
        _ret_scan(states, [c[6] for c in consts], nc, False)

        def out_step(b, carry):
            rows, batch = _batch_rows(b), _batch_chunks(b)
            q3 = _chunks3(q_ref[rows, :] * scale)
            k3 = _chunks3(k_ref[rows, :]).astype(BF16)
            v3 = _chunks3(v_ref[rows, :]).astype(BF16)
            a0 = _bdot(q3.astype(BF16), k3, _BNT)
            o = None
            for way in range(2):
                mask, q_dec = consts[way][0], consts[way][2]
                part = _bdot((a0 * mask).astype(BF16), v3, _BNN) \
                    + _bdot((q3 * q_dec).astype(BF16), states[way, batch].astype(BF16), _BNN)
                o = part if o is None else o + part
            o_ref[rows, :] = o.reshape(CHUNK_BATCH * RET_CHUNK, HEAD_DIM)
            return carry

        lax.fori_loop(0, nc // CHUNK_BATCH, out_step, 0)
        o = o_ref[...]
        g = g_ref[...]
        y_ref[...] = o * _rms_scale(o) * w_ref[...] * (g * _sigmoid(g))

    hspec = pl.BlockSpec((S, HEAD_DIM), lambda h: (0, h))
    smem = pl.BlockSpec(memory_space=pltpu.SMEM)
    return pl.pallas_call(
        body, grid=(H,), name="ret_fwd",
        in_specs=[smem, smem] + _head_specs(S, (3, 4, 5, 6), H) + [pl.BlockSpec((1, HEAD_DIM), lambda h: (0, h))],
        out_specs=[hspec, hspec],
        out_shape=[SDS((S, H * HEAD_DIM), F32)] * 2,
        scratch_shapes=[pltpu.VMEM((2, nc, HEAD_DIM, HEAD_DIM), F32)],
        compiler_params=_cp(1))(dec_f, dec_b, proj, proj, proj, proj, w_norm)


def _ret_gate_bwd(proj, o_raw, dmix, w_norm, col0, dproj):
    S = proj.shape[0]
    H = proj.shape[1] // (N_GROUPS * HEAD_DIM)

    def body(g_ref, o_ref, dy_ref, w_ref, dproj_in, do_ref, dw_ref, dproj_out, dg_ref, sems):
        o = o_ref[...]
        g = g_ref[...]
        dy = dy_ref[...]
        w = w_ref[...]
        rr = _rms_scale(o)
        normed = o * rr
        sg = _sigmoid(g)
        silu = g * sg
        dw_ref[0] = jnp.broadcast_to(jnp.sum(dy * normed * silu, axis=0, keepdims=True), (8, HEAD_DIM))
        dg_ref[0] = (dy * normed * w * (sg * (1.0 + g * (1.0 - sg)))).astype(BF16)
        dnormed = dy * w * silu
        do_ref[...] = rr * dnormed - o * (rr * rr * rr) * jnp.mean(dnormed * o, axis=-1, keepdims=True)
        _put_groups(dg_ref, dproj_out, (6,), H, sems)

    hspec = pl.BlockSpec((S, HEAD_DIM), lambda h: (0, h))
    nh0 = col0 // HEAD_DIM
    return pl.pallas_call(
        body, grid=(H,), name="ret_gate_bwd",
        in_specs=_head_specs(S, (6,), H) + [hspec, pl.BlockSpec((S, HEAD_DIM), lambda h: (0, nh0 + h)),
                                            pl.BlockSpec((1, HEAD_DIM), lambda h: (0, h)), ANY],
        out_specs=[hspec, pl.BlockSpec((1, 8, HEAD_DIM), lambda h: (h, 0, 0)), ANY],
        out_shape=[SDS((S, H * HEAD_DIM), F32), SDS((H, 8, HEAD_DIM), F32), SDS(dproj.shape, dproj.dtype)],
        input_output_aliases={4: 2},
        scratch_shapes=[pltpu.VMEM((1, S, HEAD_DIM), BF16), pltpu.SemaphoreType.DMA((1,))],
        compiler_params=_cp(1))(proj, o_raw, dmix, w_norm, dproj)


def _ret_bwd(proj, d_out, dec_f, dec_b, dproj):
    S = proj.shape[0]
    H = proj.shape[1] // (N_GROUPS * HEAD_DIM)
    C = RET_CHUNK
    nc = S // C
    scale = HEAD_DIM ** -0.5

    def body(df_ref, db_ref, q_ref, k_ref, v_ref, do, dproj_in, small_ref, dproj_out, states, d_states, stage, sems):
        h = pl.program_id(0)
        lgs = [_log_decay(df_ref, h), _log_decay(db_ref, h)]
        consts = [_ret_consts(lg, fw) for lg, fw in zip(lgs, (True, False))]

        def prep_step(b, carry):
            rows, batch = _batch_rows(b), _batch_chunks(b)
            q3 = _chunks3(q_ref[rows, :] * scale)
            k3 = _chunks3(k_ref[rows, :])
            v3 = _chunks3(v_ref[rows, :]).astype(BF16)
            do3 = _chunks3(do[rows, :]).astype(BF16)
            for way in range(2):
                states[way, batch] = _bdot((k3 * consts[way][4]).astype(BF16), v3, _BTN)
                d_states[way, batch] = _bdot((q3 * consts[way][2]).astype(BF16), do3, _BTN)
            return carry

        lax.fori_loop(0, nc // CHUNK_BATCH, prep_step, 0)
        c_decs = [c[6] for c in consts]
        _ret_scan(states, c_decs, nc, False)
        _ret_scan(d_states, c_decs, nc, True)

        def main_step(b, dlams):
            rows, batch = _batch_rows(b), _batch_chunks(b)
            q3 = _chunks3(q_ref[rows, :] * scale)
            k3 = _chunks3(k_ref[rows, :])
            q3b, k3b = q3.astype(BF16), k3.astype(BF16)
            v3b = _chunks3(v_ref[rows, :]).astype(BF16)
            do3b = _chunks3(do[rows, :]).astype(BF16)
            a0 = _bdot(q3b, k3b, _BNT)
            pv = _bdot(do3b, v3b, _BNT)
            dq = dk = dv = None
            new_dlams = []
            for way in range(2):
                mask, relf, q_dec, q_exp, k_dec, k_exp, c_dec = consts[way]
                state, d_state = states[way, batch], d_states[way, batch]
                dp = pv * mask
                dpb = dp.astype(BF16)
                gq = _bdot(do3b, state.astype(BF16), _BNT)
                gk = _bdot(v3b, d_state.astype(BF16), _BNT)
                parts = (_bdot(dpb, k3b, _BNN) + q_dec * gq, _bdot(dpb, q3b, _BTN) + k_dec * gk,
                         _bdot((a0 * mask).astype(BF16), do3b, _BTN)
                         + _bdot((k3 * k_dec).astype(BF16), d_state.astype(BF16), _BNN))
                dq, dk, dv = parts if dq is None else (dq + parts[0], dk + parts[1], dv + parts[2])
                total = lambda x: jnp.sum(jnp.sum(x, axis=0), axis=0, keepdims=True)
                new_dlams.append(dlams[way] + total(relf * a0 * dp)
                                 + total(q_exp * q_dec * q3 * gq + k_exp * k_dec * k3 * gk)
                                 + (C * c_dec) * total(state * d_state))
            flat = lambda x: x.reshape(CHUNK_BATCH * C, HEAD_DIM)
            stage[0, rows, :] = (flat(dq) * scale).astype(BF16)
            stage[1, rows, :] = flat(dk).astype(BF16)
            stage[2, rows, :] = flat(dv).astype(BF16)
            return tuple(new_dlams)

        dlams = lax.fori_loop(0, nc // CHUNK_BATCH, main_step, (jnp.zeros((1, HEAD_DIM), F32),) * 2)
        for row, (dlam, lg) in enumerate(zip(dlams, lgs)):
            small_ref[0, pl.ds(row, 1), :] = jnp.broadcast_to(jnp.sum(dlam, axis=-1, keepdims=True) * lg, (1, HEAD_DIM))
        small_ref[0, pl.ds(2, 6), :] = jnp.zeros((6, HEAD_DIM), F32)
        _put_groups(stage, dproj_out, (3, 4, 5), H, sems)

    hspec = pl.BlockSpec((S, HEAD_DIM), lambda h: (0, h))
    smem = pl.BlockSpec(memory_space=pltpu.SMEM)
    return pl.pallas_call(
        body, grid=(H,), name="ret_bwd",
        in_specs=[smem, smem] + _head_specs(S, (3, 4, 5), H) + [hspec, ANY],
        out_specs=[pl.BlockSpec((1, 8, HEAD_DIM), lambda h: (h, 0, 0)), ANY],
        out_shape=[SDS((H, 8, HEAD_DIM), F32), SDS(dproj.shape, dproj.dtype)], input_output_aliases={6: 1},
        scratch_shapes=[pltpu.VMEM((2, nc, HEAD_DIM, HEAD_DIM), F32), pltpu.VMEM((2, nc, HEAD_DIM, HEAD_DIM), F32),
                        pltpu.VMEM((3, S, HEAD_DIM), BF16), pltpu.SemaphoreType.DMA((3,))],
        compiler_params=_cp(1))(dec_f, dec_b, proj, proj, proj, d_out, dproj)


def _ffn_bwd_act(dh2, wd, g, u):
    S, D = dh2.shape
    nblk, _, FB = g.shape
    tm = min(1024, S)

    def body(dh_ref, wd_ref, g_ref, u_ref, dg_ref, du_ref):
        dact = _dot(dh_ref[...], wd_ref[...], _NT)
        gg = g_ref[0].astype(F32)
        sg = _sigmoid(gg)
        dg_ref[0] = (dact * u_ref[0].astype(F32) * (sg * (1.0 + gg * (1.0 - sg)))).astype(BF16)
        du_ref[0] = (dact * (gg * sg)).astype(BF16)

    blk = pl.BlockSpec((1, tm, FB), lambda j, i: (j, i, 0))
    return pl.pallas_call(
        body, grid=(nblk, S // tm), name="ffn_bwd_act",
        in_specs=[pl.BlockSpec((tm, D), lambda j, i: (i, 0)), pl.BlockSpec((FB, D), lambda j, i: (j, 0)), blk, blk],
        out_specs=[blk, blk], out_shape=[SDS((nblk, S, FB), BF16)] * 2,
        compiler_params=_cp(2))(dh2, wd, g, u)


def _ffn_bwd_in(dg, du, wg, wu, h1, dh2, w_norm):
    nblk, S, FB = dg.shape
    D = h1.shape[1]
    tm = min(RESIDENT_ROWS, S)
    blk = pl.BlockSpec((nblk, tm, FB), lambda i: (0, i, 0))
    row = pl.BlockSpec((tm, D), lambda i: (i, 0))
    vec = pl.BlockSpec((1, D), lambda i: (0, 0))

    def gate_body(dg_ref, wg_ref, part_ref):
        part_ref[...] = _blocked_matmul(dg_ref, wg_ref)

    part = pl.pallas_call(
        gate_body, grid=(S // tm,), name="ffn_bwd_in_gate", in_specs=[blk, _resident((nblk * FB, D))],
        out_specs=row, out_shape=SDS((S, D), F32), compiler_params=_cp(1))(dg, wg.reshape(nblk * FB, D))

    def body(du_ref, wu_ref, part_ref, h_ref, dh2_ref, wn_ref, dh_ref, dhb_ref, dw_ref):
        @pl.when(pl.program_id(0) == 0)
        def _():
            dw_ref[...] = jnp.zeros_like(dw_ref)

        dh, dw = _rms_bwd(part_ref[...] + _blocked_matmul(du_ref, wu_ref), h_ref[...], wn_ref[...])
        dh = dh2_ref[...] + dh
        dh_ref[...] = dh
        dhb_ref[...] = dh.astype(BF16)
        dw_ref[...] += dw

    return pl.pallas_call(
        body, grid=(S // tm,), name="ffn_bwd_in",
        in_specs=[blk, _resident((nblk * FB, D)), row, row, row, vec],
        out_specs=[row, row, vec], out_shape=[SDS((S, D), F32), SDS((S, D), BF16), SDS((1, D), F32)],
        compiler_params=_cp(1))(du, wu.reshape(nblk * FB, D), part, h1, dh2, w_norm)


def _dmix(dh1, w_out):
    S, D = dh1.shape
    tm = min(512, S)

    def body(dh_ref, w_ref, o_ref):
        o_ref[...] = _dot(dh_ref[...], w_ref[...], _NT)

    row = pl.BlockSpec((tm, D), lambda i: (i, 0))
    return pl.pallas_call(
        body, grid=(S // tm,), name="dmix", in_specs=[row, pl.BlockSpec((D, D), lambda i: (0, 0))],
        out_specs=row, out_shape=SDS((S, D), F32), compiler_params=_cp(1))(dh1, w_out)


def _in_bwd(dproj, w_blk, x, dh1, w_norm):
    S, D = x.shape
    nblk, _, NB = w_blk.shape
    tm = min(RESIDENT_ROWS, S)

    def body(dp_ref, w_ref, x_ref, dh1_ref, wn_ref, dx_ref, dw_ref):
        @pl.when(pl.program_id(0) == 0)
        def _():
            dw_ref[...] = jnp.zeros_like(dw_ref)

        dn = None
        for j in range(nblk):
            part = _dot(dp_ref[:, pl.ds(j * NB, NB)], w_ref[j], _NT)
            dn = part if dn is None else dn + part
        dh, dw = _rms_bwd(dn, x_ref[...], wn_ref[...])
        dx_ref[...] = dh1_ref[...] + dh
        dw_ref[...] += dw

    row = pl.BlockSpec((tm, D), lambda i: (i, 0))
    vec = pl.BlockSpec((1, D), lambda i: (0, 0))
    return pl.pallas_call(
        body, grid=(S // tm,), name="in_bwd",
        in_specs=[pl.BlockSpec((tm, nblk * NB), lambda i: (i, 0)),
                  pl.BlockSpec((nblk, D, NB), lambda i: (0, 0, 0), pipeline_mode=pl.Buffered(1)), row, row, vec],
        out_specs=[row, vec], out_shape=[SDS((S, D), F32), SDS((1, D), F32)],
        compiler_params=_cp(1))(dproj, w_blk, x, dh1, w_norm)


def _wgrad(a, b, a_spec, b_spec, o_spec, o_shape, grid, name):
    nk = grid[-1]

    def ld(ref):
        return ref[0] if len(ref.shape) == 3 else ref[...]

    def body(a_ref, b_ref, o_ref, acc):
        k = pl.program_id(len(grid) - 1)

        @pl.when(k == 0)
        def _():
            acc[...] = jnp.zeros_like(acc)

        acc[...] += _dot(ld(a_ref), ld(b_ref), _TN)

        @pl.when(k == nk - 1)
        def _():
            if len(o_ref.shape) == 3:
                o_ref[0] = acc[...].astype(o_ref.dtype)
            else:
                o_ref[...] = acc[...].astype(o_ref.dtype)

    return pl.pallas_call(
        body, grid=grid, name=name, in_specs=[a_spec, b_spec], out_specs=o_spec, out_shape=SDS(o_shape, BF16),
        scratch_shapes=[pltpu.VMEM(o_spec.block_shape[-2:], F32)], compiler_params=_cp(len(grid)))(a, b)


def _peer(k):
    x, y, c = lax.axis_index("x"), lax.axis_index("y"), lax.axis_index("c")
    px = 1 - x if k & 4 else x
    py = 1 - y if k & 2 else y
    pc = 1 - c if k & 1 else c
    return (px, py, pc), 4 * px + 2 * py + pc


def _exchange_copies(srcs, lands, send_sems, recv_sems, which, gather):
    _, me = _peer(0)
    pairs = []
    for pos, a in enumerate(which):
        for k in range(1, N_DEV):
            dev, idx = _peer(k)
            sem = pos * (N_DEV - 1) + k - 1
            src = srcs[a] if gather else srcs[a].at[idx]
            mk = functools.partial(pltpu.make_async_remote_copy, src_ref=src, send_sem=send_sems.at[sem],
                                   recv_sem=recv_sems.at[sem], device_id=dev, device_id_type=MESH)
            pairs.append((mk(dst_ref=lands[a].at[me]), mk(dst_ref=lands[a].at[idx])))
    return pairs


def _sequencer_kernel(name, collective_id, n_remote, n_local):
    return pl.kernel(mesh=plsc.ScalarSubcoreMesh(axis_name="sequencer", num_cores=1), name=name,
                     scratch_types=(pltpu.SemaphoreType.DMA((n_remote,)), pltpu.SemaphoreType.DMA((n_remote,)),
                                    pltpu.SemaphoreType.DMA((n_local,))),
                     compiler_params=pltpu.CompilerParams(collective_id=collective_id))


def _handshake(ks):
    barrier = pltpu.get_barrier_semaphore()
    for k in ks:
        pl.semaphore_signal(barrier, inc=1, device_id=_peer(k)[0], device_id_type=MESH)
    pl.semaphore_wait(barrier, len(ks))


def _sequencer_scatter(arrays, name, collective_id):
    n = len(arrays)
    hbm = pltpu.MemorySpace.HBM
    srcs = [jax.new_ref(a, memory_space=hbm) for a in arrays]
    lands = [jax.empty_ref(SDS(a.shape, a.dtype), memory_space=hbm) for a in arrays]

    @_sequencer_kernel(name, collective_id, n * (N_DEV - 1), n)
    def launch(send_sems, recv_sems, local_sems):
        _handshake(range(1, N_DEV))
        _, me = _peer(0)
        local = [pltpu.make_async_copy(srcs[a].at[me], lands[a].at[me], local_sems.at[a]) for a in range(n)]
        pairs = _exchange_copies(srcs, lands, send_sems, recv_sems, range(n), False)
        for out, _ in pairs:
            out.start()
        for cp in local:
            cp.start()
        for out, arrival in pairs:
            out.wait_send()
            arrival.wait_recv()
        for cp in local:
            cp.wait()

    launch()
    return [r[...] for r in lands]


SIBLING = 1
OTHER_CHIPS = (2, 4, 6)


def _sequencer_gather(arrays, name, collective_id):
    n = len(arrays)
    hbm = pltpu.MemorySpace.HBM
    srcs = [jax.new_ref(a, memory_space=hbm) for a in arrays]
    lands = [jax.empty_ref(SDS((N_DEV,) + a.shape, a.dtype), memory_space=hbm) for a in arrays]

    @_sequencer_kernel(name, collective_id, n * (N_DEV - 1), n)
    def launch(send_sems, recv_sems, local_sems):
        _handshake((SIBLING,) + OTHER_CHIPS)
        _, me = _peer(0)
        sibling, _ = _peer(SIBLING)

        def copy(a, k, src, block, to):
            sem = a * (N_DEV - 1) + k - 1
            return pltpu.make_async_remote_copy(src_ref=src, dst_ref=lands[a].at[block], send_sem=send_sems.at[sem],
                                                recv_sem=recv_sems.at[sem], device_id=to, device_id_type=MESH)

        local = [pltpu.make_async_copy(srcs[a], lands[a].at[me], local_sems.at[a]) for a in range(n)]
        first = [copy(a, k, srcs[a], me, _peer(k)[0]) for a in range(n) for k in OTHER_CHIPS + (SIBLING,)]
        for cp in first + local:
            cp.start()
        passed = []
        for a in range(n):
            for k in OTHER_CHIPS:
                _, block = _peer(k)
                copy(a, k, srcs[a], block, sibling).wait_recv()
                passed.append(copy(a, k ^ SIBLING, lands[a].at[block], block, sibling))
                passed[-1].start()
        for a in range(n):
            for k in (SIBLING,) + tuple(k ^ SIBLING for k in OTHER_CHIPS):
                copy(a, k, srcs[a], _peer(k)[1], sibling).wait_recv()
        for cp in first + passed:
            cp.wait_send()
        for cp in local:
            cp.wait()

    launch()
    return [r[...] for r in lands]


def _sequencer_gather_chips(array, name, collective_id, chips):
    hbm = pltpu.MemorySpace.HBM
    src = jax.new_ref(array, memory_space=hbm)
    land = jax.empty_ref(SDS((2 * len(chips),) + array.shape, array.dtype), memory_space=hbm)

    @_sequencer_kernel(name, collective_id, 2 * len(chips), 1)
    def launch(send_sems, recv_sems, local_sems):
        _handshake((SIBLING,) + tuple(k for k in chips if k))
        c = lax.axis_index("c")
        sibling, _ = _peer(SIBLING)

        def copy(sem, src_ref, slot, to):
            return pltpu.make_async_remote_copy(src_ref=src_ref, dst_ref=land.at[slot], send_sem=send_sems.at[sem],
                                                recv_sem=recv_sems.at[sem], device_id=to, device_id_type=MESH)

        started = []
        for pos, k in enumerate(chips):
            started.append(copy(2 * pos, src, 2 * pos + c, _peer(k)[0] if k else sibling))
            started[-1].start()
        for pos, k in enumerate(chips):
            if k:
                copy(2 * pos, src, 2 * pos + c, sibling).wait_recv()
                started.append(copy(2 * pos + 1, land.at[2 * pos + c], 2 * pos + c, sibling))
                started[-1].start()
        for pos, k in enumerate(chips):
            copy(2 * pos + 1 if k else 2 * pos, src, 2 * pos + 1 - c, sibling).wait_recv()
        for cp in started:
            cp.wait_send()

    launch()
    return land[...]


SMALL_ROWS = 64


def _small_step(part, w, m, v):
    def body(p_ref, w_ref, m_ref, v_ref, g_ref, d_ref, nm_ref, nv_ref, gath, send_sems, recv_sems):
        _, me = _peer(0)
        gath[me] = p_ref[...]
        copies = []
        for k in range(1, N_DEV):
            dev, idx = _peer(k)
            out = pltpu.make_async_remote_copy(src_ref=p_ref, dst_ref=gath.at[me], send_sem=send_sems.at[k - 1],
                                               recv_sem=recv_sems.at[k - 1], device_id=dev, device_id_type=MESH)
            out.start()
            arrival = pltpu.make_async_remote_copy(src_ref=p_ref, dst_ref=gath.at[idx], send_sem=send_sems.at[k - 1],
                                                   recv_sem=recv_sems.at[k - 1], device_id=dev, device_id_type=MESH)
            copies.append((out, arrival))
        for out, arrival in copies:
            out.wait_send()
            arrival.wait_recv()
        g = gath[0]
        for p in range(1, N_DEV):
            g = g + gath[p]
        g_ref[...] = g
        d_ref[...], nm_ref[...], nv_ref[...] = _adamw(w_ref[...], g, m_ref[...], v_ref[...])

    vm = pl.BlockSpec(memory_space=pltpu.VMEM)
    return pl.pallas_call(
        body, name="small_step", in_specs=[vm] * 4, out_specs=[vm] * 4,
        out_shape=[SDS((SMALL_ROWS, 128), F32)] * 4,
        scratch_shapes=[pltpu.VMEM((N_DEV, SMALL_ROWS, 128), F32), pltpu.SemaphoreType.DMA((N_DEV - 1,)),
                        pltpu.SemaphoreType.DMA((N_DEV - 1,))])(part, w, m, v)


def _adamw(w, g, m, v):
    m = ADAM_B1 * m + (1.0 - ADAM_B1) * g
    v = ADAM_B2 * v + (1.0 - ADAM_B2) * (g * g)
    m_hat = m / (1.0 - ADAM_B1 ** ADAM_STEP)
    v_hat = v / (1.0 - ADAM_B2 ** ADAM_STEP)
    delta = -ADAM_LR * (m_hat / (jnp.sqrt(v_hat) + ADAM_EPS) + ADAM_WD * w)
    return delta, m, v


def _adamw_block(parts, w, m, v, name):
    R, C = w.shape
    n_parts = len(parts)
    Rp = R // n_parts
    tr = next(t for t in (256, 128, 64, 32, 16, 8) if Rp % t == 0 and t * C <= 256 * 1024)
    per_part = Rp // tr

    def body(*refs):
        p_refs = refs[:n_parts]
        w_ref, m_ref, v_ref, g_ref, d_ref, nm_ref, nv_ref = refs[n_parts:]
        for k, p_ref in enumerate(p_refs):
            @pl.when(pl.program_id(0) // per_part == k)
            def _(p_ref=p_ref):
                g = p_ref[0].astype(F32)
                for p in range(1, N_DEV):
                    g = g + p_ref[p].astype(F32)
                g_ref[...] = g
                d_ref[...], nm_ref[...], nv_ref[...] = _adamw(w_ref[...], g, m_ref[...], v_ref[...])

    row = pl.BlockSpec((tr, C), lambda i: (i, 0))
    part_specs = [pl.BlockSpec((N_DEV, tr, C), functools.partial(
        lambda i, k: (0, jnp.clip(i - k * per_part, 0, per_part - 1), 0), k=k)) for k in range(n_parts)]
    return pl.pallas_call(
        body, grid=(R // tr,), name=name, in_specs=part_specs + [row, row, row],
        out_specs=[row] * 4, out_shape=[SDS((R, C), F32)] * 4, compiler_params=_cp(1))(*parts, w, m, v)


def _pack_small(mix, ffn, fin, retw, dec_f, dec_b, loss):
    flat = jnp.concatenate([mix.reshape(-1), ffn.reshape(-1), fin.reshape(-1), retw.reshape(-1), dec_f.reshape(-1),
                            dec_b.reshape(-1), loss.reshape(-1)])
    return jnp.pad(flat, (0, SMALL_ROWS * 128 - flat.shape[0])).reshape(SMALL_ROWS, 128)


def _unpack_small(packed, shapes):
    flat = packed.reshape(-1)
    out, at = [], 0
    for s in shapes:
        n = math.prod(s)
        out.append(flat[at:at + n].reshape(s))
        at += n
    return out


def kernel(x, norm_mix_w, w_in, ret_decay_fwd, ret_decay_bwd, ret_norm_w, w_out, norm_ffn_w, w_gate, w_up, w_down, norm_final_w, loss_target, m_norm_mix_w, m_w_in, m_ret_decay_fwd, m_ret_decay_bwd, m_ret_norm_w, m_w_out, m_norm_ffn_w, m_w_gate, m_w_up, m_w_down, m_norm_final_w, v_norm_mix_w, v_w_in, v_ret_decay_fwd, v_ret_decay_bwd, v_ret_norm_w, v_w_out, v_norm_ffn_w, v_w_gate, v_w_up, v_w_down, v_norm_final_w):
    x2 = x[0]
    tgt = loss_target[0]
    S, D = x2.shape
    H = ret_norm_w.shape[1] // HEAD_DIM
    DA = H * HEAD_DIM
    fin_w = norm_final_w.reshape(1, D)
    big = (w_in[0], w_out[0], w_gate[0].T, w_up[0].T, w_down[0])

    big_b = [w.astype(BF16) for w in big]
    stages = ((0,), (4, 2), (6,))
    wi_stages = [_sequencer_gather_chips(big_b[0], name, cid, ks)
                 for name, cid, ks in zip(("gather_in_own", "gather_in_near", "gather_in_far"), (0, 7, 8), stages)]
    wo, = _sequencer_gather(big_b[1:2], "gather_out", 1)
    wg, wu = _sequencer_gather(big_b[2:4], "gather_gate_up", 9)
    wd, = _sequencer_gather(big_b[4:], "gather_down", 5)
    wi, = _sequencer_gather(big_b[:1], "gather_in_ordered", 10)
    NB = big_b[0].shape[1]
    ax, ay = lax.axis_index("x"), lax.axis_index("y")
    chip_of = {k: 2 * (1 - ax if k & 4 else ax) + (1 - ay if k & 2 else ay) for k in (0, 2, 4, 6)}

    n1 = _norm_fwd(x2, norm_mix_w)
    ac = lax.axis_index("c")
    me = 2 * chip_of[0] + ac
    vec = lambda *v: jnp.stack([jnp.asarray(t, jnp.int32) for t in v])
    proj = _proj_part(n1, big_b[0][None], vec(0), vec(me), None, N_DEV, "proj_self")
    for ks, w_st, name in zip(stages, wi_stages, ("proj_sibling", "proj_near", "proj_far")):
        slots, blocks = [], []
        for pos, k in enumerate(ks):
            for core in ((1 - ac,) if k == 0 else (0, 1)):
                slots.append(2 * pos + core)
                blocks.append(2 * chip_of[k] + core)
        proj = _proj_part(n1, w_st, vec(*slots), vec(*blocks), proj, N_DEV, name)
    bias = _attn_bias()[:H]
    attn, lse = _attn_fwd(proj, bias)
    ret, o_raw = _ret_fwd(proj, ret_decay_fwd, ret_decay_bwd, ret_norm_w)
    wo_full = wo.reshape(D, D)
    d_ff = N_DEV * wd.shape[1]
    FB = FFN_BLOCK if d_ff % FFN_BLOCK == 0 else wd.shape[1]
    n_fb = d_ff // FB
    wg, wu = wg.reshape(n_fb, FB, D), wu.reshape(n_fb, FB, D)
    wd_full = wd.reshape(d_ff, D)
    h1, mixed, n2 = _out_fwd(x2, attn, ret, wo_full, norm_ffn_w)
    gate, up, act = _ffn_up(n2, wg, wu)
    dh2, dh2_b, loss_parts, g_fin = _ffn_down_loss(act, wd_full, h1, tgt, fin_w)

    dgate, dup = _ffn_bwd_act(dh2_b, wd_full, gate, up)
    tn = min(1024, D)
    ffn_specs = (pl.BlockSpec((1, S, FB), lambda j, n, k: (j, 0, 0)), pl.BlockSpec((S, tn), lambda j, n, k: (0, n)),
                 pl.BlockSpec((1, FB, tn), lambda j, n, k: (j, 0, n)), (n_fb, FB, D), (n_fb, D // tn, 1))
    per_dev = (N_DEV, d_ff // N_DEV, D)
    g_wd = _wgrad(act, dh2_b, *ffn_specs, "wgrad_down").reshape(per_dev)
    g_wg = _wgrad(dgate, n2, *ffn_specs, "wgrad_gate").reshape(per_dev)
    g_wu = _wgrad(dup, n2, *ffn_specs, "wgrad_up").reshape(per_dev)
    parts_f = _sequencer_scatter([g_wg, g_wu, g_wd], "scatter_ffn", 2)
    dh1, dh1_b, g_ffn = _ffn_bwd_in(dgate, dup, wg, wu, h1, dh2, norm_ffn_w)
    dmix = _dmix(dh1_b, wo_full)
    tmw = min(512, D)
    tk = min(2048, S)
    g_wo = _wgrad(mixed, dh1_b, pl.BlockSpec((tk, tmw), lambda m, k: (k, m)), pl.BlockSpec((tk, D), lambda m, k: (k, 0)),
                  pl.BlockSpec((tmw, D), lambda m, k: (m, 0)), (D, D), (D // tmw, S // tk), "wgrad_out")
    parts_o = _sequencer_scatter([g_wo.reshape(N_DEV, D // N_DEV, D)], "scatter_out", 3)
    d_ret, small_w, dproj = _ret_gate_bwd(proj, o_raw, dmix, ret_norm_w, DA, lax.empty(proj.shape, BF16))
    small, dproj = _ret_bwd(proj, d_ret, ret_decay_fwd, ret_decay_bwd, dproj)
    dproj = _attn_bwd(proj, attn, lse, dmix, bias, dproj)
    half = D // tmw // 2
    parts_i = []
    for part, (name, cid) in enumerate((("in_lo", 4), ("in_hi", 6))):
        g_wi = _wgrad(n1, dproj, pl.BlockSpec((S, tmw), functools.partial(lambda j, m, k, off: (0, m + off), off=part * half)),
                      pl.BlockSpec((S, NB), lambda j, m, k: (0, j)), pl.BlockSpec((1, tmw, NB), lambda j, m, k: (j, m, 0)),
                      (N_DEV, D // 2, NB), (N_DEV, half, 1), "wgrad_" + name)
        parts_i += _sequencer_scatter([g_wi], "scatter_" + name, cid)
    grad_x, g_mix = _in_bwd(dproj, wi, x2, dh1, norm_mix_w)

    big_m = (m_w_in[0], m_w_out[0], m_w_gate[0].T, m_w_up[0].T, m_w_down[0])
    big_v = (v_w_in[0], v_w_out[0], v_w_gate[0].T, v_w_up[0].T, v_w_down[0])
    names = ("adamw_in", "adamw_out", "adamw_gate", "adamw_up", "adamw_down")
    upd = [None] * 5
    for a, p in zip((2, 3, 4, 1, 0), [[t] for t in parts_f + parts_o] + [parts_i]):
        upd[a] = _adamw_block(p, big[a], big_m[a], big_v[a], names[a])

    g_dec_f = small[:, 0, 0].reshape(1, H)
    g_dec_b = small[:, 1, 0].reshape(1, H)
    g_retw = small_w[:, 0, :].reshape(1, DA)
    loss_local = jnp.sum(loss_parts[::8, 0])
    zero = jnp.zeros((1,), F32)
    part = _pack_small(g_mix, g_ffn, g_fin, g_retw, g_dec_f, g_dec_b, loss_local)
    sw = _pack_small(norm_mix_w, norm_ffn_w, norm_final_w, ret_norm_w, ret_decay_fwd, ret_decay_bwd, zero)
    sm = _pack_small(m_norm_mix_w, m_norm_ffn_w, m_norm_final_w, m_ret_norm_w, m_ret_decay_fwd, m_ret_decay_bwd, zero)
    sv = _pack_small(v_norm_mix_w, v_norm_ffn_w, v_norm_final_w, v_ret_norm_w, v_ret_decay_fwd, v_ret_decay_bwd, zero)
    shapes = [(1, D), (1, D), (D,), (1, DA), (1, H), (1, H), ()]
    sg, sd, snm, snv = [_unpack_small(t, shapes) for t in _small_step(part, sw, sm, sv)]
    loss = sg[6]

    def ordered(small_set, k):
        b = [(u[k].T if a in (2, 3) else u[k])[None] for a, u in enumerate(upd)]
        return [small_set[0], b[0], small_set[4], small_set[5], small_set[3], b[1], small_set[1], b[2], b[3], b[4],
                small_set[2]]

    return (loss, grad_x[None], *ordered(sg, 0), *ordered(sd, 1), *ordered(snm, 2), *ordered(snv, 3))
```

```python
import functools
import math

import numpy as np
import jax
import jax.numpy as jnp
from jax import lax
from jax.experimental import pallas as pl
from jax.experimental.pallas import tpu as pltpu
from jax.experimental.pallas import tpu_sc as plsc

F32 = jnp.float32
BF16 = jnp.bfloat16
SDS = jax.ShapeDtypeStruct

HEAD_DIM = 128
EPS = 1e-6
RET_CHUNK = 128
DILATIONS = (1, 4, 16)
BAND = 64
Q_TILE = 128
K_TILE = Q_TILE + 2 * BAND
KV_PAD = BAND * 4
TILE_GROUP = 8
BWD_TILE_GROUP = 8
NEG = -1e30
N_DEV = 8
N_GROUPS = 7
ADAM_LR, ADAM_B1, ADAM_B2, ADAM_EPS, ADAM_WD, ADAM_STEP = 0.001, 0.9, 0.999, 1e-08, 0.01, 10
VMEM_LIMIT = 56 * 1024 * 1024
MESH = pl.DeviceIdType.MESH
ANY = pl.BlockSpec(memory_space=pl.ANY)


def _cp(n_grid):
    return pltpu.CompilerParams(dimension_semantics=("arbitrary",) * n_grid, vmem_limit_bytes=VMEM_LIMIT)


def _sigmoid(x):
    return pl.reciprocal(1.0 + jnp.exp(-x), approx=True)


def _rms_scale(h):
    return lax.rsqrt(jnp.mean(h * h, axis=-1, keepdims=True) + EPS)


def _rms_bwd(dn, h, w):
    r = _rms_scale(h)
    gw = dn * w
    dh = r * gw - h * (r * r * r) * jnp.mean(gw * h, axis=-1, keepdims=True)
    return dh, jnp.sum(dn * h * r, axis=0, keepdims=True)


def _dot(a, b, dims):
    return lax.dot_general(a.astype(BF16), b.astype(BF16), (dims, ((), ())), preferred_element_type=F32)


_NN = ((1,), (0,))
_NT = ((1,), (1,))
_TN = ((0,), (0,))


RESIDENT_ROWS = 256


def _resident(shape):
    return pl.BlockSpec(shape, lambda i: (0, 0), pipeline_mode=pl.Buffered(1))


def _blocked_matmul(a_ref, w_ref):
    nblk, _, fb = a_ref.shape
    out = None
    for j in range(nblk):
        part = jnp.dot(a_ref[j], w_ref[pl.ds(j * fb, fb), :], preferred_element_type=F32)
        out = part if out is None else out + part
    return out


def _norm_fwd(x, w_norm):
    S, D = x.shape
    tm = min(1024, S)

    def body(x_ref, wn_ref, n_ref):
        xf = x_ref[...]
        n_ref[...] = (xf * _rms_scale(xf) * wn_ref[...]).astype(BF16)

    row = pl.BlockSpec((tm, D), lambda i: (i, 0))
    return pl.pallas_call(body, grid=(S // tm,), name="norm_fwd", in_specs=[row, pl.BlockSpec((1, D), lambda i: (0, 0))],
                          out_specs=row, out_shape=SDS((S, D), BF16), compiler_params=_cp(1))(x, w_norm)


def _proj_part(n1, w_slots, slots, blocks, proj, n_blocks, name):
    S, D = n1.shape
    NB = w_slots.shape[2]
    tm = min(1024, S)

    def body(slots_ref, blocks_ref, n_ref, w_ref, *rest):
        rest[-1][...] = jnp.dot(n_ref[...], w_ref[0], preferred_element_type=F32)

    out_spec = pl.BlockSpec((tm, NB), lambda i, j, slots, blocks: (i, blocks[j]))
    in_specs = [pl.BlockSpec((tm, D), lambda i, j, slots, blocks: (i, 0)),
                pl.BlockSpec((1, D, NB), lambda i, j, slots, blocks: (slots[j], 0, 0))]
    args = [n1, w_slots]
    if proj is not None:
        in_specs.append(ANY)
        args.append(proj)
    return pl.pallas_call(
        body, name=name, out_shape=SDS((S, n_blocks * NB), F32),
        grid_spec=pltpu.PrefetchScalarGridSpec(num_scalar_prefetch=2, grid=(S // tm, slots.shape[0]), in_specs=in_specs,
                                               out_specs=out_spec),
        input_output_aliases={} if proj is None else {4: 0},
        compiler_params=_cp(2))(slots, blocks, *args)


def _out_fwd(x, attn, ret, w_out, w_norm):
    S, D = x.shape
    DA = attn.shape[1]
    tm = min(512, S)

    def body(x_ref, a_ref, r_ref, w_ref, wn_ref, h_ref, mix_ref, n_ref):
        a = a_ref[...].astype(BF16)
        r = r_ref[...].astype(BF16)
        mix_ref[:, :DA] = a
        mix_ref[:, DA:] = r
        h = x_ref[...] + jnp.dot(a, w_ref[:DA, :], preferred_element_type=F32) \
            + jnp.dot(r, w_ref[DA:, :], preferred_element_type=F32)
        h_ref[...] = h
        n_ref[...] = (h * _rms_scale(h) * wn_ref[...]).astype(BF16)

    row = lambda w: pl.BlockSpec((tm, w), lambda i: (i, 0))
    return pl.pallas_call(
        body, grid=(S // tm,), name="out_fwd",
        in_specs=[row(D), row(DA), row(D - DA), pl.BlockSpec((D, D), lambda i: (0, 0)),
                  pl.BlockSpec((1, D), lambda i: (0, 0))],
        out_specs=[row(D), row(D), row(D)],
        out_shape=[SDS((S, D), F32), SDS((S, D), BF16), SDS((S, D), BF16)],
        compiler_params=_cp(1))(x, attn, ret, w_out, w_norm)


def _ffn_up(n2, wg, wu):
    S, D = n2.shape
    nblk, FB, _ = wg.shape
    tm = min(1024, S)

    def body(n_ref, wg_ref, wu_ref, g_ref, u_ref, a_ref):
        n = n_ref[...]
        g = _dot(n, wg_ref[0], _NT)
        u = _dot(n, wu_ref[0], _NT)
        g_ref[0] = g.astype(BF16)
        u_ref[0] = u.astype(BF16)
        a_ref[0] = (g * _sigmoid(g) * u).astype(BF16)

    wspec = pl.BlockSpec((1, FB, D), lambda j, i: (j, 0, 0))
    ospec = pl.BlockSpec((1, tm, FB), lambda j, i: (j, i, 0))
    return pl.pallas_call(
        body, grid=(nblk, S // tm), name="ffn_up",
        in_specs=[pl.BlockSpec((tm, D), lambda j, i: (i, 0)), wspec, wspec],
        out_specs=[ospec, ospec, ospec],
        out_shape=[SDS((nblk, S, FB), BF16)] * 3,
        compiler_params=_cp(2))(n2, wg, wu)


def _ffn_down_loss(act, wd, h1, target, w_norm):
    nblk, S, FB = act.shape
    D = h1.shape[1]
    tm = min(RESIDENT_ROWS, S)

    def body(a_ref, wd_ref, h_ref, t_ref, wn_ref, dh_ref, dhb_ref, loss_ref, dw_ref):
        @pl.when(pl.program_id(0) == 0)
        def _():
            dw_ref[...] = jnp.zeros_like(dw_ref)

        h = h_ref[...] + _blocked_matmul(a_ref, wd_ref)
        w = wn_ref[...]
        err = h * _rms_scale(h) * w - t_ref[...]
        loss_ref[...] = jnp.full(loss_ref.shape, 0.5 * jnp.sum(err * err) / D, F32)
        dh, dw = _rms_bwd(err * (1.0 / D), h, w)
        dh_ref[...] = dh
        dhb_ref[...] = dh.astype(BF16)
        dw_ref[...] += dw

    row = pl.BlockSpec((tm, D), lambda i: (i, 0))
    vec = pl.BlockSpec((1, D), lambda i: (0, 0))
    return pl.pallas_call(
        body, grid=(S // tm,), name="ffn_down_loss",
        in_specs=[pl.BlockSpec((nblk, tm, FB), lambda i: (0, i, 0)), _resident((nblk * FB, D)), row, row, vec],
        out_specs=[row, row, pl.BlockSpec((8, 128), lambda i: (i, 0)), vec],
        out_shape=[SDS((S, D), F32), SDS((S, D), BF16), SDS((S // tm * 8, 128), F32), SDS((1, D), F32)],
        compiler_params=_cp(1))(act, wd, h1, target, w_norm)


def _attn_bias():
    n_heads = 8
    slopes = np.exp2(-8.0 * np.arange(1, n_heads + 1, dtype=np.float32) / n_heads)
    dist = np.abs(np.arange(K_TILE)[None, :] - BAND - np.arange(Q_TILE)[:, None])
    out = np.empty((n_heads, len(DILATIONS), Q_TILE, K_TILE), np.float32)
    for h in range(n_heads):
        for p, d in enumerate(DILATIONS):
            out[h, p] = np.where(dist <= BAND, -slopes[h] * (d * dist).astype(np.float32), NEG)
    return jnp.asarray(out)


def _attn_tiles(S, d):
    L = S // d
    per_class = L // Q_TILE
    return L, per_class, d * per_class


def _tile_rows(t, d, per_class):
    r = t // per_class
    a = (t % per_class) * Q_TILE
    q_rows = pl.ds(r + d * a, Q_TILE, stride=d) if d > 1 else pl.ds(pl.multiple_of(a, Q_TILE), Q_TILE)
    k_rows = pl.ds(KV_PAD + r + d * (a - BAND), K_TILE, stride=d) if d > 1 else pl.ds(
        pl.multiple_of(KV_PAD + a - BAND, BAND), K_TILE)
    return a, q_rows, k_rows


def _to_quarters(dst, src, n, dst_off=0):
    for r in range(4):
        dst[pl.ds(dst_off + r * (n // 4), n // 4), :] = src[pl.ds(r, n // 4, stride=4), :]


def _quarter_tile_rows(t, S):
    L = S // 16
    per_class = L // Q_TILE
    blk, tt = t // (4 * per_class), t % (4 * per_class)
    r, a = tt // per_class, (tt % per_class) * Q_TILE
    q_rows = pl.ds(blk * (S // 4) + r + 4 * a, Q_TILE, stride=4)
    k_rows = pl.ds(KV_PAD + blk * (S // 4) + r + 4 * (a - BAND), K_TILE, stride=4)
    return a, q_rows, k_rows


def _quarter_band_rows(t, S):
    L = S // 4
    per_quarter = L // Q_TILE
    blk, a = t // per_quarter, (t % per_quarter) * Q_TILE
    q_rows = pl.ds(pl.multiple_of(blk * L + a, Q_TILE), Q_TILE)
    k_rows = pl.ds(pl.multiple_of(KV_PAD + blk * L + a - BAND, BAND), K_TILE)
    return a, q_rows, k_rows


def _lanes(x, width):
    return jnp.concatenate([x] * (width // HEAD_DIM), axis=-1)


_BNT = (((2,), (2,)), ((0,), (0,)))
_BNN = (((2,), (1,)), ((0,), (0,)))
_BTN = (((1,), (1,)), ((0,), (0,)))


def _bdot(a, b, dims):
    return lax.dot_general(a, b, dims, preferred_element_type=F32)


def _stacked(rows, loaders):
    return [jnp.stack([f(*r) for r in rows]) for f in loaders]


def _edge_mask(a, L):
    lk = lax.broadcasted_iota(jnp.int32, (1, K_TILE), 1) + (a - BAND)
    return jnp.where((lk >= 0) & (lk < L), 0.0, NEG).astype(F32)


def _fill_padded(dst, src, S):
    dst[pl.ds(0, KV_PAD), :] = jnp.zeros((KV_PAD, HEAD_DIM), F32)
    dst[pl.ds(KV_PAD + S, KV_PAD), :] = jnp.zeros((KV_PAD, HEAD_DIM), F32)
    dst[pl.ds(KV_PAD, S), :] = src[...]


def _head_specs(S, groups, n_heads):
    return [pl.BlockSpec((S, HEAD_DIM), functools.partial(lambda h, g: (0, g * n_heads + h), g=g)) for g in groups]


def _attn_fwd(proj, bias):
    S = proj.shape[0]
    H = proj.shape[1] // (N_GROUPS * HEAD_DIM)
    scale = HEAD_DIM ** -0.5

    def body(q_ref, k_ref, v_ref, b_ref, o_ref, lse_ref, kp, vp, m_run, l_run, q4, m3, l3, acc3):
        _fill_padded(kp, k_ref, S)
        _fill_padded(vp, v_ref, S)
        o_ref[...] = jnp.zeros_like(o_ref)
        m_run[...] = jnp.full(m_run.shape, NEG, F32)
        l_run[...] = jnp.zeros_like(l_run)
        def online(n_tiles, tile_rows, p, L, q_src, m_buf, l_buf, o_buf):
            def tiles(t, carry):
                rows = [tile_rows(t + u * (n_tiles // TILE_GROUP)) for u in range(TILE_GROUP)]
                qs, ks, vs, m_old, l_old, o_old, edge = _stacked(rows, (
                    lambda a, qr, kr: q_src[qr, :].astype(BF16), lambda a, qr, kr: kp[kr, :].astype(BF16),
                    lambda a, qr, kr: vp[kr, :].astype(BF16), lambda a, qr, kr: m_buf[qr, :],
                    lambda a, qr, kr: l_buf[qr, :], lambda a, qr, kr: o_buf[qr, :], lambda a, qr, kr: _edge_mask(a, L)))
                s = _bdot(qs, ks, _BNT) * scale + b_ref[0, p][None] + edge
                m_new = jnp.maximum(m_old, jnp.max(s, axis=-1, keepdims=True))
                pr = jnp.exp(s - _lanes(m_new, K_TILE)).astype(BF16)
                alpha = jnp.exp(m_old - m_new)
                l_new = alpha * l_old + _bdot(pr, jnp.ones((TILE_GROUP, K_TILE, HEAD_DIM), BF16), _BNN)
                o_new = alpha * o_old + _bdot(pr, vs, _BNN)
                for u, (_, qr, _) in enumerate(rows):
                    o_buf[qr, :] = o_new[u]
                    m_buf[qr, :] = m_new[u]
                    l_buf[qr, :] = l_new[u]
                return carry

            lax.fori_loop(0, n_tiles // TILE_GROUP, tiles, 0)

        L, per_class, n_tiles = _attn_tiles(S, DILATIONS[0])
        online(n_tiles, functools.partial(_tile_rows, d=DILATIONS[0], per_class=per_class), 0, L, q_ref, m_run, l_run, o_ref)

        _to_quarters(q4, q_ref, S)
        _to_quarters(kp, k_ref, S, KV_PAD)
        _to_quarters(vp, v_ref, S, KV_PAD)
        n_tiles = _attn_tiles(S, DILATIONS[2])[2]

        def tiles3(t, carry):
            rows = [_quarter_tile_rows(t + u * (n_tiles // TILE_GROUP), S) for u in range(TILE_GROUP)]
            qs, ks, vs, edge = _stacked(rows, (
                lambda a, qr, kr: q4[qr, :].astype(BF16), lambda a, qr, kr: kp[kr, :].astype(BF16),
                lambda a, qr, kr: vp[kr, :].astype(BF16), lambda a, qr, kr: _edge_mask(a, S // DILATIONS[2])))
            s = _bdot(qs, ks, _BNT) * scale + b_ref[0, 2][None] + edge
            m_new = jnp.broadcast_to(jnp.max(s, axis=-1, keepdims=True), (TILE_GROUP, Q_TILE, HEAD_DIM))
            pr = jnp.exp(s - _lanes(m_new, K_TILE)).astype(BF16)
            l_new = _bdot(pr, jnp.ones((TILE_GROUP, K_TILE, HEAD_DIM), BF16), _BNN)
            o_new = _bdot(pr, vs, _BNN)
            for u, (_, qr, _) in enumerate(rows):
                acc3[qr, :] = o_new[u]
                m3[qr, :] = m_new[u]
                l3[qr, :] = l_new[u]
            return carry

        lax.fori_loop(0, n_tiles // TILE_GROUP, tiles3, 0)
        online(_attn_tiles(S, DILATIONS[1])[2], functools.partial(_quarter_band_rows, S=S), 1, S // DILATIONS[1],
               q4, m3, l3, acc3)
        for r in range(4):
            nat, qtr = pl.ds(r, S // 4, stride=4), pl.ds(r * (S // 4), S // 4)
            m_a, m_b = m_run[nat, :], m3[qtr, :]
            m = jnp.maximum(m_a, m_b)
            w_a, w_b = jnp.exp(m_a - m), jnp.exp(m_b - m)
            l = w_a * l_run[nat, :] + w_b * l3[qtr, :]
            o_ref[nat, :] = (w_a * o_ref[nat, :] + w_b * acc3[qtr, :]) / l
            lse_ref[nat, :] = m + jnp.log(l)

    hspec = pl.BlockSpec((S, HEAD_DIM), lambda h: (0, h))
    padded, plain = pltpu.VMEM((S + 2 * KV_PAD, HEAD_DIM), F32), pltpu.VMEM((S, HEAD_DIM), F32)
    return pl.pallas_call(
        body, grid=(H,), name="attn_fwd",
        in_specs=_head_specs(S, (0, 1, 2), H) + [
            pl.BlockSpec((1, len(DILATIONS), Q_TILE, K_TILE), lambda h: (h, 0, 0, 0))],
        out_specs=[hspec, hspec],
        out_shape=[SDS((S, H * HEAD_DIM), F32), SDS((S, H * HEAD_DIM), F32)],
        scratch_shapes=[padded, padded] + [plain] * 6,
        compiler_params=_cp(1))(proj, proj, proj, bias)


def _put_groups(stage, dproj, groups, n_heads, sems):
    h = pl.program_id(0)
    copies = [pltpu.make_async_copy(
        stage.at[i], dproj.at[:, pl.ds(pl.multiple_of((g * n_heads + h) * HEAD_DIM, HEAD_DIM), HEAD_DIM)], sems.at[i])
        for i, g in enumerate(groups)]
    for cp in copies:
        cp.start()
    for cp in copies:
        cp.wait()


def _attn_bwd(proj, out, lse, dmix, bias, dproj):
    S = proj.shape[0]
    H = proj.shape[1] // (N_GROUPS * HEAD_DIM)
    scale = HEAD_DIM ** -0.5
    assert S // DILATIONS[2] >= 2 * Q_TILE

    def body(q_ref, k_ref, v_ref, o_ref, lse_ref, do_ref, b_ref, dproj_in, dproj_out,
             kp, vp, dkp, dvp, dsum, q4, do4, lse4, dsum4, dq_ref, dk_ref, dv_ref, stage, sems):
        _fill_padded(kp, k_ref, S)
        _fill_padded(vp, v_ref, S)
        dkp[...] = jnp.zeros_like(dkp)
        dvp[...] = jnp.zeros_like(dvp)
        dq_ref[...] = jnp.zeros_like(dq_ref)
        dsum[...] = jnp.broadcast_to(jnp.sum(do_ref[...] * o_ref[...], axis=-1, keepdims=True), dsum.shape)

        def run(n_tiles, tile_rows, p, L, q_src, do_src, lse_src, dsum_src, dq_dst, dq_adds):
            def tiles(t, carry):
                rows = [tile_rows(t + u * (n_tiles // BWD_TILE_GROUP)) for u in range(BWD_TILE_GROUP)]
                qs, ks, vs, dos, lses, dsums, dk_old, dv_old, edge = _stacked(rows, (
                    lambda a, qr, kr: q_src[qr, :].astype(BF16), lambda a, qr, kr: kp[kr, :].astype(BF16),
                    lambda a, qr, kr: vp[kr, :].astype(BF16), lambda a, qr, kr: do_src[qr, :].astype(BF16),
                    lambda a, qr, kr: lse_src[qr, :], lambda a, qr, kr: dsum_src[qr, :],
                    lambda a, qr, kr: dkp[kr, :], lambda a, qr, kr: dvp[kr, :], lambda a, qr, kr: _edge_mask(a, L)))
                s = _bdot(qs, ks, _BNT) * scale + b_ref[0, p][None] + edge
                pr = jnp.exp(s - _lanes(lses, K_TILE))
                ds = (pr * (_bdot(dos, vs, _BNT) - _lanes(dsums, K_TILE)) * scale).astype(BF16)
                dq_new = _bdot(ds, ks, _BNN)
                if dq_adds:
                    dq_new = dq_new + jnp.stack([dq_dst[qr, :] for _, qr, _ in rows])
                dk_new = dk_old + _bdot(ds, qs, _BTN)
                dv_new = dv_old + _bdot(pr.astype(BF16), dos, _BTN)
                for u, (_, qr, kr) in enumerate(rows):
                    dq_dst[qr, :] = dq_new[u]
                    dkp[kr, :] = dk_new[u]
                    dvp[kr, :] = dv_new[u]
                return carry

            lax.fori_loop(0, n_tiles // BWD_TILE_GROUP, tiles, 0)

        L, per_class, n_tiles = _attn_tiles(S, DILATIONS[0])
        run(n_tiles, functools.partial(_tile_rows, d=DILATIONS[0], per_class=per_class), 0, L,
            q_ref, do_ref, lse_ref, dsum, dq_ref, True)
        dk_ref[...] = dkp[pl.ds(KV_PAD, S), :]
        dv_ref[...] = dvp[pl.ds(KV_PAD, S), :]

        for dst, src in ((q4, q_ref), (do4, do_ref), (lse4, lse_ref), (dsum4, dsum)):
            _to_quarters(dst, src, S)
        _to_quarters(kp, k_ref, S, KV_PAD)
        _to_quarters(vp, v_ref, S, KV_PAD)
        dkp[...] = jnp.zeros_like(dkp)
        dvp[...] = jnp.zeros_like(dvp)
        dq3 = dsum
        run(_attn_tiles(S, DILATIONS[2])[2], functools.partial(_quarter_tile_rows, S=S), 2, S // DILATIONS[2],
            q4, do4, lse4, dsum4, dq3, False)
        run(_attn_tiles(S, DILATIONS[1])[2], functools.partial(_quarter_band_rows, S=S), 1, S // DILATIONS[1],
            q4, do4, lse4, dsum4, dq3, True)
        for r in range(4):
            nat, qtr = pl.ds(r, S // 4, stride=4), pl.ds(r * (S // 4), S // 4)
            pad_qtr = pl.ds(KV_PAD + r * (S // 4), S // 4)
            dq_ref[nat, :] = dq_ref[nat, :] + dq3[qtr, :]
            dk_ref[nat, :] = dk_ref[nat, :] + dkp[pad_qtr, :]
            dv_ref[nat, :] = dv_ref[nat, :] + dvp[pad_qtr, :]
        for i, acc in enumerate((dq_ref, dk_ref, dv_ref)):
            stage[i] = acc[...].astype(BF16)
        _put_groups(stage, dproj_out, (0, 1, 2), H, sems)

    hspec = pl.BlockSpec((S, HEAD_DIM), lambda h: (0, h))
    once = pl.BlockSpec((S, HEAD_DIM), lambda h: (0, h), pipeline_mode=pl.Buffered(1))
    padded, plain = pltpu.VMEM((S + 2 * KV_PAD, HEAD_DIM), F32), pltpu.VMEM((S, HEAD_DIM), F32)
    return pl.pallas_call(
        body, grid=(H,), name="attn_bwd",
        in_specs=_head_specs(S, (0, 1, 2), H) + [
            once, hspec, hspec, pl.BlockSpec((1, len(DILATIONS), Q_TILE, K_TILE), lambda h: (h, 0, 0, 0)), ANY],
        out_specs=ANY, out_shape=SDS(dproj.shape, dproj.dtype), input_output_aliases={7: 0},
        scratch_shapes=[padded] * 4 + [plain] * 8 + [pltpu.VMEM((3, S, HEAD_DIM), BF16), pltpu.SemaphoreType.DMA((3,))],
        compiler_params=_cp(1))(proj, proj, proj, out, lse, dmix, bias, dproj)


def _ret_consts(lg, forward):
    C = RET_CHUNK
    i = lax.broadcasted_iota(jnp.int32, (C, C), 0)
    j = lax.broadcasted_iota(jnp.int32, (C, C), 1)
    rel = (i - j) if forward else (j - i)
    inside = (rel >= 0) if forward else (rel > 0)
    relf = jnp.maximum(rel, 0).astype(F32)
    mask = jnp.where(inside, jnp.exp(lg * relf), 0.0)
    idx = lax.broadcasted_iota(jnp.int32, (C, 1), 0).astype(F32)
    q_exp = (idx + 1.0) if forward else (C - idx)
    k_exp = (C - 1.0 - idx) if forward else idx
    return mask, relf, jnp.exp(lg * q_exp), q_exp, jnp.exp(lg * k_exp), k_exp, jnp.exp(lg * C)


def _log_decay(dec_ref, h):
    return -jnp.exp(jnp.full((1, 1), dec_ref[0, h], F32))


FFN_BLOCK = 704
CHUNK_BATCH = 16


def _batch_rows(b):
    n = CHUNK_BATCH * RET_CHUNK
    return pl.ds(pl.multiple_of(b * n, n), n)


def _batch_chunks(b):
    return pl.ds(pl.multiple_of(b * CHUNK_BATCH, CHUNK_BATCH), CHUNK_BATCH)


def _chunks3(x):
    return x.reshape(CHUNK_BATCH, RET_CHUNK, HEAD_DIM)


def _ret_scan(buf, c_decs, nc, reverse):
    def step(n, carry):
        new = []
        for way, r in enumerate(carry):
            c = n if (way == 0) != reverse else nc - 1 - n
            term = buf[way, c]
            buf[way, c] = r
            new.append(r * c_decs[way] + term)
        return tuple(new)

    lax.fori_loop(0, nc, step, (jnp.zeros((HEAD_DIM, HEAD_DIM), F32),) * 2)


def _ret_fwd(proj, dec_f, dec_b, w_norm):
    S = proj.shape[0]
    H = proj.shape[1] // (N_GROUPS * HEAD_DIM)
    nc = S // RET_CHUNK
    scale = HEAD_DIM ** -0.5

    def body(df_ref, db_ref, q_ref, k_ref, v_ref, g_ref, w_ref, y_ref, o_ref, states):
        h = pl.program_id(0)
        consts = [_ret_consts(_log_decay(dref, h), fw) for fw, dref in ((True, df_ref), (False, db_ref))]

        def kv_step(b, carry):
            rows, batch = _batch_rows(b), _batch_chunks(b)
            k3 = _chunks3(k_ref[rows, :])
            v3 = _chunks3(v_ref[rows, :]).astype(BF16)
            for way in range(2):
                states[way, batch] = _bdot((k3 * consts[way][4]).astype(BF16), v3, _BTN)
            return carry

        lax.fori_loop(0, nc // CHUNK_BATCH, kv_step, 0)
        _ret_scan(states, [c[6] for c in consts], nc, False)

        def out_step(b, carry):
            rows, batch = _batch_rows(b), _batch_chunks(b)
            q3 = _chunks3(q_ref[rows, :] * scale)
            k3 = _chunks3(k_ref[rows, :]).astype(BF16)
            v3 = _chunks3(v_ref[rows, :]).astype(BF16)
            a0 = _bdot(q3.astype(BF16), k3, _BNT)
            o = None
            for way in range(2):
                mask, q_dec = consts[way][0], consts[way][2]
                part = _bdot((a0 * mask).astype(BF16), v3, _BNN) \
                    + _bdot((q3 * q_dec).astype(BF16), states[way, batch].astype(BF16), _BNN)
                o = part if o is None else o + part
            o_ref[rows, :] = o.reshape(CHUNK_BATCH * RET_CHUNK, HEAD_DIM)
            return carry

        lax.fori_loop(0, nc // CHUNK_BATCH, out_step, 0)
        o = o_ref[...]
        g = g_ref[...]
        y_ref[...] = o * _rms_scale(o) * w_ref[...] * (g * _sigmoid(g))

    hspec = pl.BlockSpec((S, HEAD_DIM), lambda h: (0, h))
    smem = pl.BlockSpec(memory_space=pltpu.SMEM)
    return pl.pallas_call(
        body, grid=(H,), name="ret_fwd",
        in_specs=[smem, smem] + _head_specs(S, (3, 4, 5, 6), H) + [pl.BlockSpec((1, HEAD_DIM), lambda h: (0, h))],
        out_specs=[hspec, hspec],
        out_shape=[SDS((S, H * HEAD_DIM), F32)] * 2,
        scratch_shapes=[pltpu.VMEM((2, nc, HEAD_DIM, HEAD_DIM), F32)],
        compiler_params=_cp(1))(dec_f, dec_b, proj, proj, proj, proj, w_norm)


def _ret_gate_bwd(proj, o_raw, dmix, w_norm, col0, dproj):
    S = proj.shape[0]
    H = proj.shape[1] // (N_GROUPS * HEAD_DIM)

    def body(g_ref, o_ref, dy_ref, w_ref, dproj_in, do_ref, dw_ref, dproj_out, dg_ref, sems):
        o = o_ref[...]
        g = g_ref[...]
        dy = dy_ref[...]
        w = w_ref[...]
        rr = _rms_scale(o)
        normed = o * rr
        sg = _sigmoid(g)
        silu = g * sg
        dw_ref[0] = jnp.broadcast_to(jnp.sum(dy * normed * silu, axis=0, keepdims=True), (8, HEAD_DIM))
        dg_ref[0] = (dy * normed * w * (sg * (1.0 + g * (1.0 - sg)))).astype(BF16)
        dnormed = dy * w * silu
        do_ref[...] = rr * dnormed - o * (rr * rr * rr) * jnp.mean(dnormed * o, axis=-1, keepdims=True)
        _put_groups(dg_ref, dproj_out, (6,), H, sems)

    hspec = pl.BlockSpec((S, HEAD_DIM), lambda h: (0, h))
    nh0 = col0 // HEAD_DIM
    return pl.pallas_call(
        body, grid=(H,), name="ret_gate_bwd",
        in_specs=_head_specs(S, (6,), H) + [hspec, pl.BlockSpec((S, HEAD_DIM), lambda h: (0, nh0 + h)),
                                            pl.BlockSpec((1, HEAD_DIM), lambda h: (0, h)), ANY],
        out_specs=[hspec, pl.BlockSpec((1, 8, HEAD_DIM), lambda h: (h, 0, 0)), ANY],
        out_shape=[SDS((S, H * HEAD_DIM), F32), SDS((H, 8, HEAD_DIM), F32), SDS(dproj.shape, dproj.dtype)],
        input_output_aliases={4: 2},
        scratch_shapes=[pltpu.VMEM((1, S, HEAD_DIM), BF16), pltpu.SemaphoreType.DMA((1,))],
        compiler_params=_cp(1))(proj, o_raw, dmix, w_norm, dproj)


def _ret_bwd(proj, d_out, dec_f, dec_b, dproj):
    S = proj.shape[0]
    H = proj.shape[1] // (N_GROUPS * HEAD_DIM)
    C = RET_CHUNK
    nc = S // C
    scale = HEAD_DIM ** -0.5

    def body(df_ref, db_ref, q_ref, k_ref, v_ref, do, dproj_in, small_ref, dproj_out, states, d_states, stage, sems):
        h = pl.program_id(0)
        lgs = [_log_decay(df_ref, h), _log_decay(db_ref, h)]
        consts = [_ret_consts(lg, fw) for lg, fw in zip(lgs, (True, False))]

        def prep_step(b, carry):
            rows, batch = _batch_rows(b), _batch_chunks(b)
            q3 = _chunks3(q_ref[rows, :] * scale)
            k3 = _chunks3(k_ref[rows, :])
            v3 = _chunks3(v_ref[rows, :]).astype(BF16)
            do3 = _chunks3(do[rows, :]).astype(BF16)
            for way in range(2):
                states[way, batch] = _bdot((k3 * consts[way][4]).astype(BF16), v3, _BTN)
                d_states[way, batch] = _bdot((q3 * consts[way][2]).astype(BF16), do3, _BTN)
            return carry

        lax.fori_loop(0, nc // CHUNK_BATCH, prep_step, 0)
        c_decs = [c[6] for c in consts]
        _ret_scan(states, c_decs, nc, False)
        _ret_scan(d_states, c_decs, nc, True)

        def main_step(b, dlams):
            rows, batch = _batch_rows(b), _batch_chunks(b)
            q3 = _chunks3(q_ref[rows, :] * scale)
            k3 = _chunks3(k_ref[rows, :])
            q3b, k3b = q3.astype(BF16), k3.astype(BF16)
            v3b = _chunks3(v_ref[rows, :]).astype(BF16)
            do3b = _chunks3(do[rows, :]).astype(BF16)
            a0 = _bdot(q3b, k3b, _BNT)
            pv = _bdot(do3b, v3b, _BNT)
            dq = dk = dv = None
            new_dlams = []
            for way in range(2):
                mask, relf, q_dec, q_exp, k_dec, k_exp, c_dec = consts[way]
                state, d_state = states[way, batch], d_states[way, batch]
                dp = pv * mask
                dpb = dp.astype(BF16)
                gq = _bdot(do3b, state.astype(BF16), _BNT)
                gk = _bdot(v3b, d_state.astype(BF16), _BNT)
                parts = (_bdot(dpb, k3b, _BNN) + q_dec * gq, _bdot(dpb, q3b, _BTN) + k_dec * gk,
                         _bdot((a0 * mask).astype(BF16), do3b, _BTN)
                         + _bdot((k3 * k_dec).astype(BF16), d_state.astype(BF16), _BNN))
                dq, dk, dv = parts if dq is None else (dq + parts[0], dk + parts[1], dv + parts[2])
                total = lambda x: jnp.sum(jnp.sum(x, axis=0), axis=0, keepdims=True)
                new_dlams.append(dlams[way] + total(relf * a0 * dp)
                                 + total(q_exp * q_dec * q3 * gq + k_exp * k_dec * k3 * gk)
                                 + (C * c_dec) * total(state * d_state))
            flat = lambda x: x.reshape(CHUNK_BATCH * C, HEAD_DIM)
            stage[0, rows, :] = (flat(dq) * scale).astype(BF16)
            stage[1, rows, :] = flat(dk).astype(BF16)
            stage[2, rows, :] = flat(dv).astype(BF16)
            return tuple(new_dlams)

        dlams = lax.fori_loop(0, nc // CHUNK_BATCH, main_step, (jnp.zeros((1, HEAD_DIM), F32),) * 2)
        for row, (dlam, lg) in enumerate(zip(dlams, lgs)):
            small_ref[0, pl.ds(row, 1), :] = jnp.broadcast_to(jnp.sum(dlam, axis=-1, keepdims=True) * lg, (1, HEAD_DIM))
        small_ref[0, pl.ds(2, 6), :] = jnp.zeros((6, HEAD_DIM), F32)
        _put_groups(stage, dproj_out, (3, 4, 5), H, sems)

    hspec = pl.BlockSpec((S, HEAD_DIM), lambda h: (0, h))
    smem = pl.BlockSpec(memory_space=pltpu.SMEM)
    return pl.pallas_call(
        body, grid=(H,), name="ret_bwd",
        in_specs=[smem, smem] + _head_specs(S, (3, 4, 5), H) + [hspec, ANY],
        out_specs=[pl.BlockSpec((1, 8, HEAD_DIM), lambda h: (h, 0, 0)), ANY],
        out_shape=[SDS((H, 8, HEAD_DIM), F32), SDS(dproj.shape, dproj.dtype)], input_output_aliases={6: 1},
        scratch_shapes=[pltpu.VMEM((2, nc, HEAD_DIM, HEAD_DIM), F32), pltpu.VMEM((2, nc, HEAD_DIM, HEAD_DIM), F32),
                        pltpu.VMEM((3, S, HEAD_DIM), BF16), pltpu.SemaphoreType.DMA((3,))],
        compiler_params=_cp(1))(dec_f, dec_b, proj, proj, proj, d_out, dproj)


def _ffn_bwd_act(dh2, wd, g, u):
    S, D = dh2.shape
    nblk, _, FB = g.shape
    tm = min(1024, S)

    def body(dh_ref, wd_ref, g_ref, u_ref, dg_ref, du_ref):
        dact = _dot(dh_ref[...], wd_ref[...], _NT)
        gg = g_ref[0].astype(F32)
        sg = _sigmoid(gg)
        dg_ref[0] = (dact * u_ref[0].astype(F32) * (sg * (1.0 + gg * (1.0 - sg)))).astype(BF16)
        du_ref[0] = (dact * (gg * sg)).astype(BF16)

    blk = pl.BlockSpec((1, tm, FB), lambda j, i: (j, i, 0))
    return pl.pallas_call(
        body, grid=(nblk, S // tm), name="ffn_bwd_act",
        in_specs=[pl.BlockSpec((tm, D), lambda j, i: (i, 0)), pl.BlockSpec((FB, D), lambda j, i: (j, 0)), blk, blk],
        out_specs=[blk, blk], out_shape=[SDS((nblk, S, FB), BF16)] * 2,
        compiler_params=_cp(2))(dh2, wd, g, u)


def _ffn_bwd_in(dg, du, wg, wu, h1, dh2, w_norm):
    nblk, S, FB = dg.shape
    D = h1.shape[1]
    tm = min(RESIDENT_ROWS, S)
    blk = pl.BlockSpec((nblk, tm, FB), lambda i: (0, i, 0))
    row = pl.BlockSpec((tm, D), lambda i: (i, 0))
    vec = pl.BlockSpec((1, D), lambda i: (0, 0))

    def gate_body(dg_ref, wg_ref, part_ref):
        part_ref[...] = _blocked_matmul(dg_ref, wg_ref)

    part = pl.pallas_call(
        gate_body, grid=(S // tm,), name="ffn_bwd_in_gate", in_specs=[blk, _resident((nblk * FB, D))],
        out_specs=row, out_shape=SDS((S, D), F32), compiler_params=_cp(1))(dg, wg.reshape(nblk * FB, D))

    def body(du_ref, wu_ref, part_ref, h_ref, dh2_ref, wn_ref, dh_ref, dhb_ref, dw_ref):
        @pl.when(pl.program_id(0) == 0)
        def _():
            dw_ref[...] = jnp.zeros_like(dw_ref)

        dh, dw = _rms_bwd(part_ref[...] + _blocked_matmul(du_ref, wu_ref), h_ref[...], wn_ref[...])
        dh = dh2_ref[...] + dh
        dh_ref[...] = dh
        dhb_ref[...] = dh.astype(BF16)
        dw_ref[...] += dw

    return pl.pallas_call(
        body, grid=(S // tm,), name="ffn_bwd_in",
        in_specs=[blk, _resident((nblk * FB, D)), row, row, row, vec],
        out_specs=[row, row, vec], out_shape=[SDS((S, D), F32), SDS((S, D), BF16), SDS((1, D), F32)],
        compiler_params=_cp(1))(du, wu.reshape(nblk * FB, D), part, h1, dh2, w_norm)


def _dmix(dh1, w_out):
    S, D = dh1.shape
    tm = min(512, S)

    def body(dh_ref, w_ref, o_ref):
        o_ref[...] = _dot(dh_ref[...], w_ref[...], _NT)

    row = pl.BlockSpec((tm, D), lambda i: (i, 0))
    return pl.pallas_call(
        body, grid=(S // tm,), name="dmix", in_specs=[row, pl.BlockSpec((D, D), lambda i: (0, 0))],
        out_specs=row, out_shape=SDS((S, D), F32), compiler_params=_cp(1))(dh1, w_out)


def _in_bwd(dproj, w_blk, x, dh1, w_norm):
    S, D = x.shape
    nblk, _, NB = w_blk.shape
    tm = min(RESIDENT_ROWS, S)

    def body(dp_ref, w_ref, x_ref, dh1_ref, wn_ref, dx_ref, dw_ref):
        @pl.when(pl.program_id(0) == 0)
        def _():
            dw_ref[...] = jnp.zeros_like(dw_ref)

        dn = None
        for j in range(nblk):
            part = _dot(dp_ref[:, pl.ds(j * NB, NB)], w_ref[j], _NT)
            dn = part if dn is None else dn + part
        dh, dw = _rms_bwd(dn, x_ref[...], wn_ref[...])
        dx_ref[...] = dh1_ref[...] + dh
        dw_ref[...] += dw

    row = pl.BlockSpec((tm, D), lambda i: (i, 0))
    vec = pl.BlockSpec((1, D), lambda i: (0, 0))
    return pl.pallas_call(
        body, grid=(S // tm,), name="in_bwd",
        in_specs=[pl.BlockSpec((tm, nblk * NB), lambda i: (i, 0)),
                  pl.BlockSpec((nblk, D, NB), lambda i: (0, 0, 0), pipeline_mode=pl.Buffered(1)), row, row, vec],
        out_specs=[row, vec], out_shape=[SDS((S, D), F32), SDS((1, D), F32)],
        compiler_params=_cp(1))(dproj, w_blk, x, dh1, w_norm)


def _wgrad(a, b, a_spec, b_spec, o_spec, o_shape, grid, name):
    nk = grid[-1]

    def ld(ref):
        return ref[0] if len(ref.shape) == 3 else ref[...]

    def body(a_ref, b_ref, o_ref, acc):
        k = pl.program_id(len(grid) - 1)

        @pl.when(k == 0)
        def _():
            acc[...] = jnp.zeros_like(acc)

        acc[...] += _dot(ld(a_ref), ld(b_ref), _TN)

        @pl.when(k == nk - 1)
        def _():
            if len(o_ref.shape) == 3:
                o_ref[0] = acc[...].astype(o_ref.dtype)
            else:
                o_ref[...] = acc[...].astype(o_ref.dtype)

    return pl.pallas_call(
        body, grid=grid, name=name, in_specs=[a_spec, b_spec], out_specs=o_spec, out_shape=SDS(o_shape, BF16),
        scratch_shapes=[pltpu.VMEM(o_spec.block_shape[-2:], F32)], compiler_params=_cp(len(grid)))(a, b)


def _peer(k):
    x, y, c = lax.axis_index("x"), lax.axis_index("y"), lax.axis_index("c")
    px = 1 - x if k & 4 else x
    py = 1 - y if k & 2 else y
    pc = 1 - c if k & 1 else c
    return (px, py, pc), 4 * px + 2 * py + pc


def _exchange_copies(srcs, lands, send_sems, recv_sems, which, gather):
    _, me = _peer(0)
    pairs = []
    for pos, a in enumerate(which):
        for k in range(1, N_DEV):
            dev, idx = _peer(k)
            sem = pos * (N_DEV - 1) + k - 1
            src = srcs[a] if gather else srcs[a].at[idx]
            mk = functools.partial(pltpu.make_async_remote_copy, src_ref=src, send_sem=send_sems.at[sem],
                                   recv_sem=recv_sems.at[sem], device_id=dev, device_id_type=MESH)
            pairs.append((mk(dst_ref=lands[a].at[me]), mk(dst_ref=lands[a].at[idx])))
    return pairs


def _sequencer_kernel(name, collective_id, n_remote, n_local):
    return pl.kernel(mesh=plsc.ScalarSubcoreMesh(axis_name="sequencer", num_cores=1), name=name,
                     scratch_types=(pltpu.SemaphoreType.DMA((n_remote,)), pltpu.SemaphoreType.DMA((n_remote,)),
                                    pltpu.SemaphoreType.DMA((n_local,))),
                     compiler_params=pltpu.CompilerParams(collective_id=collective_id))


def _handshake(ks):
    barrier = pltpu.get_barrier_semaphore()
    for k in ks:
        pl.semaphore_signal(barrier, inc=1, device_id=_peer(k)[0], device_id_type=MESH)
    pl.semaphore_wait(barrier, len(ks))


def _sequencer_scatter(arrays, name, collective_id):
    n = len(arrays)
    hbm = pltpu.MemorySpace.HBM
    srcs = [jax.new_ref(a, memory_space=hbm) for a in arrays]
    lands = [jax.empty_ref(SDS(a.shape, a.dtype), memory_space=hbm) for a in arrays]

    @_sequencer_kernel(name, collective_id, n * (N_DEV - 1), n)
    def launch(send_sems, recv_sems, local_sems):
        _handshake(range(1, N_DEV))
        _, me = _peer(0)
        local = [pltpu.make_async_copy(srcs[a].at[me], lands[a].at[me], local_sems.at[a]) for a in range(n)]
        pairs = _exchange_copies(srcs, lands, send_sems, recv_sems, range(n), False)
        for out, _ in pairs:
            out.start()
        for cp in local:
            cp.start()
        for out, arrival in pairs:
            out.wait_send()
            arrival.wait_recv()
        for cp in local:
            cp.wait()

    launch()
    return [r[...] for r in lands]


SIBLING = 1
OTHER_CHIPS = (2, 4, 6)


def _sequencer_gather(arrays, name, collective_id):
    n = len(arrays)
    hbm = pltpu.MemorySpace.HBM
    srcs = [jax.new_ref(a, memory_space=hbm) for a in arrays]
    lands = [jax.empty_ref(SDS((N_DEV,) + a.shape, a.dtype), memory_space=hbm) for a in arrays]

    @_sequencer_kernel(name, collective_id, n * (N_DEV - 1), n)
    def launch(send_sems, recv_sems, local_sems):
        _handshake((SIBLING,) + OTHER_CHIPS)
        _, me = _peer(0)
        sibling, _ = _peer(SIBLING)

        def copy(a, k, src, block, to):
            sem = a * (N_DEV - 1) + k - 1
            return pltpu.make_async_remote_copy(src_ref=src, dst_ref=lands[a].at[block], send_sem=send_sems.at[sem],
                                                recv_sem=recv_sems.at[sem], device_id=to, device_id_type=MESH)

        local = [pltpu.make_async_copy(srcs[a], lands[a].at[me], local_sems.at[a]) for a in range(n)]
        first = [copy(a, k, srcs[a], me, _peer(k)[0]) for a in range(n) for k in OTHER_CHIPS + (SIBLING,)]
        for cp in first + local:
            cp.start()
        passed = []
        for a in range(n):
            for k in OTHER_CHIPS:
                _, block = _peer(k)
                copy(a, k, srcs[a], block, sibling).wait_recv()
                passed.append(copy(a, k ^ SIBLING, lands[a].at[block], block, sibling))
                passed[-1].start()
        for a in range(n):
            for k in (SIBLING,) + tuple(k ^ SIBLING for k in OTHER_CHIPS):
                copy(a, k, srcs[a], _peer(k)[1], sibling).wait_recv()
        for cp in first + passed:
            cp.wait_send()
        for cp in local:
            cp.wait()

    launch()
    return [r[...] for r in lands]


def _sequencer_gather_chips(array, name, collective_id, chips):
    hbm = pltpu.MemorySpace.HBM
    src = jax.new_ref(array, memory_space=hbm)
    land = jax.empty_ref(SDS((2 * len(chips),) + array.shape, array.dtype), memory_space=hbm)

    @_sequencer_kernel(name, collective_id, 2 * len(chips), 1)
    def launch(send_sems, recv_sems, local_sems):
        _handshake((SIBLING,) + tuple(k for k in chips if k))
        c = lax.axis_index("c")
        sibling, _ = _peer(SIBLING)

        def copy(sem, src_ref, slot, to):
            return pltpu.make_async_remote_copy(src_ref=src_ref, dst_ref=land.at[slot], send_sem=send_sems.at[sem],
                                                recv_sem=recv_sems.at[sem], device_id=to, device_id_type=MESH)

        started = []
        for pos, k in enumerate(chips):
            started.append(copy(2 * pos, src, 2 * pos + c, _peer(k)[0] if k else sibling))
            started[-1].start()
        for pos, k in enumerate(chips):
            if k:
                copy(2 * pos, src, 2 * pos + c, sibling).wait_recv()
                started.append(copy(2 * pos + 1, land.at[2 * pos + c], 2 * pos + c, sibling))
                started[-1].start()
        for pos, k in enumerate(chips):
            copy(2 * pos + 1 if k else 2 * pos, src, 2 * pos + 1 - c, sibling).wait_recv()
        for cp in started:
            cp.wait_send()

    launch()
    return land[...]


SMALL_ROWS = 64


def _small_step(part, w, m, v):
    def body(p_ref, w_ref, m_ref, v_ref, g_ref, d_ref, nm_ref, nv_ref, gath, send_sems, recv_sems):
        _, me = _peer(0)
        gath[me] = p_ref[...]
        copies = []
        for k in range(1, N_DEV):
            dev, idx = _peer(k)
            out = pltpu.make_async_remote_copy(src_ref=p_ref, dst_ref=gath.at[me], send_sem=send_sems.at[k - 1],
                                               recv_sem=recv_sems.at[k - 1], device_id=dev, device_id_type=MESH)
            out.start()
            arrival = pltpu.make_async_remote_copy(src_ref=p_ref, dst_ref=gath.at[idx], send_sem=send_sems.at[k - 1],
                                                   recv_sem=recv_sems.at[k - 1], device_id=dev, device_id_type=MESH)
            copies.append((out, arrival))
        for out, arrival in copies:
            out.wait_send()
            arrival.wait_recv()
        g = gath[0]
        for p in range(1, N_DEV):
            g = g + gath[p]
        g_ref[...] = g
        d_ref[...], nm_ref[...], nv_ref[...] = _adamw(w_ref[...], g, m_ref[...], v_ref[...])

    vm = pl.BlockSpec(memory_space=pltpu.VMEM)
    return pl.pallas_call(
        body, name="small_step", in_specs=[vm] * 4, out_specs=[vm] * 4,
        out_shape=[SDS((SMALL_ROWS, 128), F32)] * 4,
        scratch_shapes=[pltpu.VMEM((N_DEV, SMALL_ROWS, 128), F32), pltpu.SemaphoreType.DMA((N_DEV - 1,)),
                        pltpu.SemaphoreType.DMA((N_DEV - 1,))])(part, w, m, v)


def _adamw(w, g, m, v):
    m = ADAM_B1 * m + (1.0 - ADAM_B1) * g
    v = ADAM_B2 * v + (1.0 - ADAM_B2) * (g * g)
    m_hat = m / (1.0 - ADAM_B1 ** ADAM_STEP)
    v_hat = v / (1.0 - ADAM_B2 ** ADAM_STEP)
    delta = -ADAM_LR * (m_hat / (jnp.sqrt(v_hat) + ADAM_EPS) + ADAM_WD * w)
    return delta, m, v


def _adamw_block(parts, w, m, v, name):
    R, C = w.shape
    n_parts = len(parts)
    Rp = R // n_parts
    tr = next(t for t in (256, 128, 64, 32, 16, 8) if Rp % t == 0 and t * C <= 256 * 1024)
    per_part = Rp // tr

    def body(*refs):
        p_refs = refs[:n_parts]
        w_ref, m_ref, v_ref, g_ref, d_ref, nm_ref, nv_ref = refs[n_parts:]
        for k, p_ref in enumerate(p_refs):
            @pl.when(pl.program_id(0) // per_part == k)
            def _(p_ref=p_ref):
                g = p_ref[0].astype(F32)
                for p in range(1, N_DEV):
                    g = g + p_ref[p].astype(F32)
                g_ref[...] = g
                d_ref[...], nm_ref[...], nv_ref[...] = _adamw(w_ref[...], g, m_ref[...], v_ref[...])

    row = pl.BlockSpec((tr, C), lambda i: (i, 0))
    part_specs = [pl.BlockSpec((N_DEV, tr, C), functools.partial(
        lambda i, k: (0, jnp.clip(i - k * per_part, 0, per_part - 1), 0), k=k)) for k in range(n_parts)]
    return pl.pallas_call(
        body, grid=(R // tr,), name=name, in_specs=part_specs + [row, row, row],
        out_specs=[row] * 4, out_shape=[SDS((R, C), F32)] * 4, compiler_params=_cp(1))(*parts, w, m, v)


def _pack_small(mix, ffn, fin, retw, dec_f, dec_b, loss):
    flat = jnp.concatenate([mix.reshape(-1), ffn.reshape(-1), fin.reshape(-1), retw.reshape(-1), dec_f.reshape(-1),
                            dec_b.reshape(-1), loss.reshape(-1)])
    return jnp.pad(flat, (0, SMALL_ROWS * 128 - flat.shape[0])).reshape(SMALL_ROWS, 128)


def _unpack_small(packed, shapes):
    flat = packed.reshape(-1)
    out, at = [], 0
    for s in shapes:
        n = math.prod(s)
        out.append(flat[at:at + n].reshape(s))
        at += n
    return out


def kernel(x, norm_mix_w, w_in, ret_decay_fwd, ret_decay_bwd, ret_norm_w, w_out, norm_ffn_w, w_gate, w_up, w_down, norm_final_w, loss_target, m_norm_mix_w, m_w_in, m_ret_decay_fwd, m_ret_decay_bwd, m_ret_norm_w, m_w_out, m_norm_ffn_w, m_w_gate, m_w_up, m_w_down, m_norm_final_w, v_norm_mix_w, v_w_in, v_ret_decay_fwd, v_ret_decay_bwd, v_ret_norm_w, v_w_out, v_norm_ffn_w, v_w_gate, v_w_up, v_w_down, v_norm_final_w):
    x2 = x[0]
    tgt = loss_target[0]
    S, D = x2.shape
    H = ret_norm_w.shape[1] // HEAD_DIM
    DA = H * HEAD_DIM
    fin_w = norm_final_w.reshape(1, D)
    big = (w_in[0], w_out[0], w_gate[0].T, w_up[0].T, w_down[0])

    big_b = [w.astype(BF16) for w in big]
    stages = ((0,), (4, 2), (6,))
    wi_stages = [_sequencer_gather_chips(big_b[0], name, cid, ks)
                 for name, cid, ks in zip(("gather_in_own", "gather_in_near", "gather_in_far"), (0, 7, 8), stages)]
    wo, = _sequencer_gather(big_b[1:2], "gather_out", 1)
    wg, wu = _sequencer_gather(big_b[2:4], "gather_gate_up", 9)
    wd, = _sequencer_gather(big_b[4:], "gather_down", 5)
    wi, = _sequencer_gather(big_b[:1], "gather_in_ordered", 10)
    NB = big_b[0].shape[1]
    ax, ay = lax.axis_index("x"), lax.axis_index("y")
    chip_of = {k: 2 * (1 - ax if k & 4 else ax) + (1 - ay if k & 2 else ay) for k in (0, 2, 4, 6)}

    n1 = _norm_fwd(x2, norm_mix_w)
    ac = lax.axis_index("c")
    me = 2 * chip_of[0] + ac
    vec = lambda *v: jnp.stack([jnp.asarray(t, jnp.int32) for t in v])
    proj = _proj_part(n1, big_b[0][None], vec(0), vec(me), None, N_DEV, "proj_self")
    for ks, w_st, name in zip(stages, wi_stages, ("proj_sibling", "proj_near", "proj_far")):
        slots, blocks = [], []
        for pos, k in enumerate(ks):
            for core in ((1 - ac,) if k == 0 else (0, 1)):
                slots.append(2 * pos + core)
                blocks.append(2 * chip_of[k] + core)
        proj = _proj_part(n1, w_st, vec(*slots), vec(*blocks), proj, N_DEV, name)
    bias = _attn_bias()[:H]
    attn, lse = _attn_fwd(proj, bias)
    ret, o_raw = _ret_fwd(proj, ret_decay_fwd, ret_decay_bwd, ret_norm_w)
    wo_full = wo.reshape(D, D)
    d_ff = N_DEV * wd.shape[1]
    FB = FFN_BLOCK if d_ff % FFN_BLOCK == 0 else wd.shape[1]
    n_fb = d_ff // FB
    wg, wu = wg.reshape(n_fb, FB, D), wu.reshape(n_fb, FB, D)
    wd_full = wd.reshape(d_ff, D)
    h1, mixed, n2 = _out_fwd(x2, attn, ret, wo_full, norm_ffn_w)
    gate, up, act = _ffn_up(n2, wg, wu)
    dh2, dh2_b, loss_parts, g_fin = _ffn_down_loss(act, wd_full, h1, tgt, fin_w)

    dgate, dup = _ffn_bwd_act(dh2_b, wd_full, gate, up)
    tn = min(1024, D)
    ffn_specs = (pl.BlockSpec((1, S, FB), lambda j, n, k: (j, 0, 0)), pl.BlockSpec((S, tn), lambda j, n, k: (0, n)),
                 pl.BlockSpec((1, FB, tn), lambda j, n, k: (j, 0, n)), (n_fb, FB, D), (n_fb, D // tn, 1))
    per_dev = (N_DEV, d_ff // N_DEV, D)
    g_wd = _wgrad(act, dh2_b, *ffn_specs, "wgrad_down").reshape(per_dev)
    g_wg = _wgrad(dgate, n2, *ffn_specs, "wgrad_gate").reshape(per_dev)
    g_wu = _wgrad(dup, n2, *ffn_specs, "wgrad_up").reshape(per_dev)
    parts_f = _sequencer_scatter([g_wg, g_wu, g_wd], "scatter_ffn", 2)
    dh1, dh1_b, g_ffn = _ffn_bwd_in(dgate, dup, wg, wu, h1, dh2, norm_ffn_w)
    dmix = _dmix(dh1_b, wo_full)
    tmw = min(512, D)
    tk = min(2048, S)
    g_wo = _wgrad(mixed, dh1_b, pl.BlockSpec((tk, tmw), lambda m, k: (k, m)), pl.BlockSpec((tk, D), lambda m, k: (k, 0)),
                  pl.BlockSpec((tmw, D), lambda m, k: (m, 0)), (D, D), (D // tmw, S // tk), "wgrad_out")
    parts_o = _sequencer_scatter([g_wo.reshape(N_DEV, D // N_DEV, D)], "scatter_out", 3)
    d_ret, small_w, dproj = _ret_gate_bwd(proj, o_raw, dmix, ret_norm_w, DA, lax.empty(proj.shape, BF16))
    small, dproj = _ret_bwd(proj, d_ret, ret_decay_fwd, ret_decay_bwd, dproj)
    dproj = _attn_bwd(proj, attn, lse, dmix, bias, dproj)
    half = D // tmw // 2
    parts_i = []
    for part, (name, cid) in enumerate((("in_lo", 4), ("in_hi", 6))):
        g_wi = _wgrad(n1, dproj, pl.BlockSpec((S, tmw), functools.partial(lambda j, m, k, off: (0, m + off), off=part * half)),
                      pl.BlockSpec((S, NB), lambda j, m, k: (0, j)), pl.BlockSpec((1, tmw, NB), lambda j, m, k: (j, m, 0)),
                      (N_DEV, D // 2, NB), (N_DEV, half, 1), "wgrad_" + name)
        parts_i += _sequencer_scatter([g_wi], "scatter_" + name, cid)
    grad_x, g_mix = _in_bwd(dproj, wi, x2, dh1, norm_mix_w)

    big_m = (m_w_in[0], m_w_out[0], m_w_gate[0].T, m_w_up[0].T, m_w_down[0])
    big_v = (v_w_in[0], v_w_out[0], v_w_gate[0].T, v_w_up[0].T, v_w_down[0])
    names = ("adamw_in", "adamw_out", "adamw_gate", "adamw_up", "adamw_down")
    upd = [None] * 5
    for a, p in zip((2, 3, 4, 1, 0), [[t] for t in parts_f + parts_o] + [parts_i]):
        upd[a] = _adamw_block(p, big[a], big_m[a], big_v[a], names[a])

    g_dec_f = small[:, 0, 0].reshape(1, H)
    g_dec_b = small[:, 1, 0].reshape(1, H)
    g_retw = small_w[:, 0, :].reshape(1, DA)
    loss_local = jnp.sum(loss_parts[::8, 0])
    zero = jnp.zeros((1,), F32)
    part = _pack_small(g_mix, g_ffn, g_fin, g_retw, g_dec_f, g_dec_b, loss_local)
    sw = _pack_small(norm_mix_w, norm_ffn_w, norm_final_w, ret_norm_w, ret_decay_fwd, ret_decay_bwd, zero)
    sm = _pack_small(m_norm_mix_w, m_norm_ffn_w, m_norm_final_w, m_ret_norm_w, m_ret_decay_fwd, m_ret_decay_bwd, zero)
    sv = _pack_small(v_norm_mix_w, v_norm_ffn_w, v_norm_final_w, v_ret_norm_w, v_ret_decay_fwd, v_ret_decay_bwd, zero)
    shapes = [(1, D), (1, D), (D,), (1, DA), (1, H), (1, H), ()]
    sg, sd, snm, snv = [_unpack_small(t, shapes) for t in _small_step(part, sw, sm, sv)]
    loss = sg[6]

    def ordered(small_set, k):
        b = [(u[k].T if a in (2, 3) else u[k])[None] for a, u in enumerate(upd)]
        return [small_set[0], b[0], small_set[4], small_set[5], small_set[3], b[1], small_set[1], b[2], b[3], b[4],
                small_set[2]]

    return (loss, grad_x[None], *ordered(sg, 0), *ordered(sd, 1), *ordered(snm, 2), *ordered(snv, 3))
```

```python
import functools
import math

import numpy as np
import jax
import jax.numpy as jnp
from jax import lax
from jax.experimental import pallas as pl
from jax.experimental.pallas import tpu as pltpu
from jax.experimental.pallas import tpu_sc as plsc

F32 = jnp.float32
BF16 = jnp.bfloat16
SDS = jax.ShapeDtypeStruct

HEAD_DIM = 128
EPS = 1e-6
RET_CHUNK = 128
DILATIONS = (1, 4, 16)
BAND = 64
Q_TILE = 128
K_TILE = Q_TILE + 2 * BAND
KV_PAD = BAND * 4
TILE_GROUP = 8
BWD_TILE_GROUP = 8
NEG = -1e30
N_DEV = 8
N_GROUPS = 7
ADAM_LR, ADAM_B1, ADAM_B2, ADAM_EPS, ADAM_WD, ADAM_STEP = 0.001, 0.9, 0.999, 1e-08, 0.01, 10
VMEM_LIMIT = 56 * 1024 * 1024
MESH = pl.DeviceIdType.MESH
ANY = pl.BlockSpec(memory_space=pl.ANY)


def _cp(n_grid):
    return pltpu.CompilerParams(dimension_semantics=("arbitrary",) * n_grid, vmem_limit_bytes=VMEM_LIMIT)


def _sigmoid(x):
    return 1.0 / (1.0 + jnp.exp(-x))


def _rms_scale(h):
    return lax.rsqrt(jnp.mean(h * h, axis=-1, keepdims=True) + EPS)


def _rms_bwd(dn, h, w):
    r = _rms_scale(h)
    gw = dn * w
    dh = r * gw - h * (r * r * r) * jnp.mean(gw * h, axis=-1, keepdims=True)
    return dh, jnp.sum(dn * h * r, axis=0, keepdims=True)


def _dot(a, b, dims):
    return lax.dot_general(a.astype(BF16), b.astype(BF16), (dims, ((), ())), preferred_element_type=F32)


_NN = ((1,), (0,))
_NT = ((1,), (1,))
_TN = ((0,), (0,))


RESIDENT_ROWS = 256


def _resident(shape):
    return pl.BlockSpec(shape, lambda i: (0, 0), pipeline_mode=pl.Buffered(1))


def _blocked_matmul(a_ref, w_ref):
    nblk, _, fb = a_ref.shape
    out = None
    for j in range(nblk):
        part = jnp.dot(a_ref[j], w_ref[pl.ds(j * fb, fb), :], preferred_element_type=F32)
        out = part if out is None else out + part
    return out


def _norm_fwd(x, w_norm):
    S, D = x.shape
    tm = min(1024, S)

    def body(x_ref, wn_ref, n_ref):
        xf = x_ref[...]
        n_ref[...] = (xf * _rms_scale(xf) * wn_ref[...]).astype(BF16)

    row = pl.BlockSpec((tm, D), lambda i: (i, 0))
    return pl.pallas_call(body, grid=(S // tm,), name="norm_fwd", in_specs=[row, pl.BlockSpec((1, D), lambda i: (0, 0))],
                          out_specs=row, out_shape=SDS((S, D), BF16), compiler_params=_cp(1))(x, w_norm)


def _proj_part(n1, w_slots, slots, blocks, proj, n_blocks, name):
    S, D = n1.shape
    NB = w_slots.shape[2]
    tm = min(1024, S)

    def body(slots_ref, blocks_ref, n_ref, w_ref, *rest):
        rest[-1][...] = jnp.dot(n_ref[...], w_ref[0], preferred_element_type=F32)

    out_spec = pl.BlockSpec((tm, NB), lambda i, j, slots, blocks: (i, blocks[j]))
    in_specs = [pl.BlockSpec((tm, D), lambda i, j, slots, blocks: (i, 0)),
                pl.BlockSpec((1, D, NB), lambda i, j, slots, blocks: (slots[j], 0, 0))]
    args = [n1, w_slots]
    if proj is not None:
        in_specs.append(ANY)
        args.append(proj)
    return pl.pallas_call(
        body, name=name, out_shape=SDS((S, n_blocks * NB), F32),
        grid_spec=pltpu.PrefetchScalarGridSpec(num_scalar_prefetch=2, grid=(S // tm, slots.shape[0]), in_specs=in_specs,
                                               out_specs=out_spec),
        input_output_aliases={} if proj is None else {4: 0},
        compiler_params=_cp(2))(slots, blocks, *args)


def _out_fwd(x, attn, ret, w_out, w_norm):
    S, D = x.shape
    DA = attn.shape[1]
    tm = min(512, S)

    def body(x_ref, a_ref, r_ref, w_ref, wn_ref, h_ref, mix_ref, n_ref):
        a = a_ref[...].astype(BF16)
        r = r_ref[...].astype(BF16)
        mix_ref[:, :DA] = a
        mix_ref[:, DA:] = r
        h = x_ref[...] + jnp.dot(a, w_ref[:DA, :], preferred_element_type=F32) \
            + jnp.dot(r, w_ref[DA:, :], preferred_element_type=F32)
        h_ref[...] = h
        n_ref[...] = (h * _rms_scale(h) * wn_ref[...]).astype(BF16)

    row = lambda w: pl.BlockSpec((tm, w), lambda i: (i, 0))
    return pl.pallas_call(
        body, grid=(S // tm,), name="out_fwd",
        in_specs=[row(D), row(DA), row(D - DA), pl.BlockSpec((D, D), lambda i: (0, 0)),
                  pl.BlockSpec((1, D), lambda i: (0, 0))],
        out_specs=[row(D), row(D), row(D)],
        out_shape=[SDS((S, D), F32), SDS((S, D), BF16), SDS((S, D), BF16)],
        compiler_params=_cp(1))(x, attn, ret, w_out, w_norm)


def _ffn_up(n2, wg, wu):
    S, D = n2.shape
    nblk, FB, _ = wg.shape
    tm = min(1024, S)

    def body(n_ref, wg_ref, wu_ref, g_ref, u_ref, a_ref):
        n = n_ref[...]
        g = _dot(n, wg_ref[0], _NT)
        u = _dot(n, wu_ref[0], _NT)
        g_ref[0] = g.astype(BF16)
        u_ref[0] = u.astype(BF16)
        a_ref[0] = (g * _sigmoid(g) * u).astype(BF16)

    wspec = pl.BlockSpec((1, FB, D), lambda j, i: (j, 0, 0))
    ospec = pl.BlockSpec((1, tm, FB), lambda j, i: (j, i, 0))
    return pl.pallas_call(
        body, grid=(nblk, S // tm), name="ffn_up",
        in_specs=[pl.BlockSpec((tm, D), lambda j, i: (i, 0)), wspec, wspec],
        out_specs=[ospec, ospec, ospec],
        out_shape=[SDS((nblk, S, FB), BF16)] * 3,
        compiler_params=_cp(2))(n2, wg, wu)


def _ffn_down_loss(act, wd, h1, target, w_norm):
    nblk, S, FB = act.shape
    D = h1.shape[1]
    tm = min(RESIDENT_ROWS, S)

    def body(a_ref, wd_ref, h_ref, t_ref, wn_ref, dh_ref, dhb_ref, loss_ref, dw_ref):
        @pl.when(pl.program_id(0) == 0)
        def _():
            dw_ref[...] = jnp.zeros_like(dw_ref)

        h = h_ref[...] + _blocked_matmul(a_ref, wd_ref)
        w = wn_ref[...]
        err = h * _rms_scale(h) * w - t_ref[...]
        loss_ref[...] = jnp.full(loss_ref.shape, 0.5 * jnp.sum(err * err) / D, F32)
        dh, dw = _rms_bwd(err * (1.0 / D), h, w)
        dh_ref[...] = dh
        dhb_ref[...] = dh.astype(BF16)
        dw_ref[...] += dw

    row = pl.BlockSpec((tm, D), lambda i: (i, 0))
    vec = pl.BlockSpec((1, D), lambda i: (0, 0))
    return pl.pallas_call(
        body, grid=(S // tm,), name="ffn_down_loss",
        in_specs=[pl.BlockSpec((nblk, tm, FB), lambda i: (0, i, 0)), _resident((nblk * FB, D)), row, row, vec],
        out_specs=[row, row, pl.BlockSpec((8, 128), lambda i: (i, 0)), vec],
        out_shape=[SDS((S, D), F32), SDS((S, D), BF16), SDS((S // tm * 8, 128), F32), SDS((1, D), F32)],
        compiler_params=_cp(1))(act, wd, h1, target, w_norm)


def _attn_bias():
    n_heads = 8
    slopes = np.exp2(-8.0 * np.arange(1, n_heads + 1, dtype=np.float32) / n_heads)
    dist = np.abs(np.arange(K_TILE)[None, :] - BAND - np.arange(Q_TILE)[:, None])
    out = np.empty((n_heads, len(DILATIONS), Q_TILE, K_TILE), np.float32)
    for h in range(n_heads):
        for p, d in enumerate(DILATIONS):
            out[h, p] = np.where(dist <= BAND, -slopes[h] * (d * dist).astype(np.float32), NEG)
    return jnp.asarray(out)


def _attn_tiles(S, d):
    L = S // d
    per_class = L // Q_TILE
    return L, per_class, d * per_class


def _tile_rows(t, d, per_class):
    r = t // per_class
    a = (t % per_class) * Q_TILE
    q_rows = pl.ds(r + d * a, Q_TILE, stride=d) if d > 1 else pl.ds(pl.multiple_of(a, Q_TILE), Q_TILE)
    k_rows = pl.ds(KV_PAD + r + d * (a - BAND), K_TILE, stride=d) if d > 1 else pl.ds(
        pl.multiple_of(KV_PAD + a - BAND, BAND), K_TILE)
    return a, q_rows, k_rows


def _to_quarters(dst, src, n, dst_off=0):
    for r in range(4):
        dst[pl.ds(dst_off + r * (n // 4), n // 4), :] = src[pl.ds(r, n // 4, stride=4), :]


def _quarter_tile_rows(t, S):
    L = S // 16
    per_class = L // Q_TILE
    blk, tt = t // (4 * per_class), t % (4 * per_class)
    r, a = tt // per_class, (tt % per_class) * Q_TILE
    q_rows = pl.ds(blk * (S // 4) + r + 4 * a, Q_TILE, stride=4)
    k_rows = pl.ds(KV_PAD + blk * (S // 4) + r + 4 * (a - BAND), K_TILE, stride=4)
    return a, q_rows, k_rows


def _quarter_band_rows(t, S):
    L = S // 4
    per_quarter = L // Q_TILE
    blk, a = t // per_quarter, (t % per_quarter) * Q_TILE
    q_rows = pl.ds(pl.multiple_of(blk * L + a, Q_TILE), Q_TILE)
    k_rows = pl.ds(pl.multiple_of(KV_PAD + blk * L + a - BAND, BAND), K_TILE)
    return a, q_rows, k_rows


def _lanes(x, width):
    return jnp.concatenate([x] * (width // HEAD_DIM), axis=-1)


_BNT = (((2,), (2,)), ((0,), (0,)))
_BNN = (((2,), (1,)), ((0,), (0,)))
_BTN = (((1,), (1,)), ((0,), (0,)))


def _bdot(a, b, dims):
    return lax.dot_general(a, b, dims, preferred_element_type=F32)


def _stacked(rows, loaders):
    return [jnp.stack([f(*r) for r in rows]) for f in loaders]


def _edge_mask(a, L):
    lk = lax.broadcasted_iota(jnp.int32, (1, K_TILE), 1) + (a - BAND)
    return jnp.where((lk >= 0) & (lk < L), 0.0, NEG).astype(F32)


def _fill_padded(dst, src, S):
    dst[pl.ds(0, KV_PAD), :] = jnp.zeros((KV_PAD, HEAD_DIM), F32)
    dst[pl.ds(KV_PAD + S, KV_PAD), :] = jnp.zeros((KV_PAD, HEAD_DIM), F32)
    dst[pl.ds(KV_PAD, S), :] = src[...]


def _head_specs(S, groups, n_heads):
    return [pl.BlockSpec((S, HEAD_DIM), functools.partial(lambda h, g: (0, g * n_heads + h), g=g)) for g in groups]


def _attn_fwd(proj, bias):
    S = proj.shape[0]
    H = proj.shape[1] // (N_GROUPS * HEAD_DIM)
    scale = HEAD_DIM ** -0.5

    def body(q_ref, k_ref, v_ref, b_ref, o_ref, lse_ref, kp, vp, m_run, l_run, q4, m3, l3, acc3):
        _fill_padded(kp, k_ref, S)
        _fill_padded(vp, v_ref, S)
        o_ref[...] = jnp.zeros_like(o_ref)
        m_run[...] = jnp.full(m_run.shape, NEG, F32)
        l_run[...] = jnp.zeros_like(l_run)
        def online(n_tiles, tile_rows, p, L, q_src, m_buf, l_buf, o_buf):
            def tiles(t, carry):
                rows = [tile_rows(t + u * (n_tiles // TILE_GROUP)) for u in range(TILE_GROUP)]
                qs, ks, vs, m_old, l_old, o_old, edge = _stacked(rows, (
                    lambda a, qr, kr: q_src[qr, :].astype(BF16), lambda a, qr, kr: kp[kr, :].astype(BF16),
                    lambda a, qr, kr: vp[kr, :].astype(BF16), lambda a, qr, kr: m_buf[qr, :],
                    lambda a, qr, kr: l_buf[qr, :], lambda a, qr, kr: o_buf[qr, :], lambda a, qr, kr: _edge_mask(a, L)))
                s = _bdot(qs, ks, _BNT) * scale + b_ref[0, p][None] + edge
                m_new = jnp.maximum(m_old, jnp.max(s, axis=-1, keepdims=True))
                pr = jnp.exp(s - _lanes(m_new, K_TILE)).astype(BF16)
                alpha = jnp.exp(m_old - m_new)
                l_new = alpha * l_old + _bdot(pr, jnp.ones((TILE_GROUP, K_TILE, HEAD_DIM), BF16), _BNN)
                o_new = alpha * o_old + _bdot(pr, vs, _BNN)
                for u, (_, qr, _) in enumerate(rows):
                    o_buf[qr, :] = o_new[u]
                    m_buf[qr, :] = m_new[u]
                    l_buf[qr, :] = l_new[u]
                return carry

            lax.fori_loop(0, n_tiles // TILE_GROUP, tiles, 0)

        L, per_class, n_tiles = _attn_tiles(S, DILATIONS[0])
        online(n_tiles, functools.partial(_tile_rows, d=DILATIONS[0], per_class=per_class), 0, L, q_ref, m_run, l_run, o_ref)

        _to_quarters(q4, q_ref, S)
        _to_quarters(kp, k_ref, S, KV_PAD)
        _to_quarters(vp, v_ref, S, KV_PAD)
        n_tiles = _attn_tiles(S, DILATIONS[2])[2]

        def tiles3(t, carry):
            rows = [_quarter_tile_rows(t + u * (n_tiles // TILE_GROUP), S) for u in range(TILE_GROUP)]
            qs, ks, vs, edge = _stacked(rows, (
                lambda a, qr, kr: q4[qr, :].astype(BF16), lambda a, qr, kr: kp[kr, :].astype(BF16),
                lambda a, qr, kr: vp[kr, :].astype(BF16), lambda a, qr, kr: _edge_mask(a, S // DILATIONS[2])))
            s = _bdot(qs, ks, _BNT) * scale + b_ref[0, 2][None] + edge
            m_new = jnp.broadcast_to(jnp.max(s, axis=-1, keepdims=True), (TILE_GROUP, Q_TILE, HEAD_DIM))
            pr = jnp.exp(s - _lanes(m_new, K_TILE)).astype(BF16)
            l_new = _bdot(pr, jnp.ones((TILE_GROUP, K_TILE, HEAD_DIM), BF16), _BNN)
            o_new = _bdot(pr, vs, _BNN)
            for u, (_, qr, _) in enumerate(rows):
                acc3[qr, :] = o_new[u]
                m3[qr, :] = m_new[u]
                l3[qr, :] = l_new[u]
            return carry

        lax.fori_loop(0, n_tiles // TILE_GROUP, tiles3, 0)
        online(_attn_tiles(S, DILATIONS[1])[2], functools.partial(_quarter_band_rows, S=S), 1, S // DILATIONS[1],
               q4, m3, l3, acc3)
        for r in range(4):
            nat, qtr = pl.ds(r, S // 4, stride=4), pl.ds(r * (S // 4), S // 4)
            m_a, m_b = m_run[nat, :], m3[qtr, :]
            m = jnp.maximum(m_a, m_b)
            w_a, w_b = jnp.exp(m_a - m), jnp.exp(m_b - m)
            l = w_a * l_run[nat, :] + w_b * l3[qtr, :]
            o_ref[nat, :] = (w_a * o_ref[nat, :] + w_b * acc3[qtr, :]) / l
            lse_ref[nat, :] = m + jnp.log(l)

    hspec = pl.BlockSpec((S, HEAD_DIM), lambda h: (0, h))
    padded, plain = pltpu.VMEM((S + 2 * KV_PAD, HEAD_DIM), F32), pltpu.VMEM((S, HEAD_DIM), F32)
    return pl.pallas_call(
        body, grid=(H,), name="attn_fwd",
        in_specs=_head_specs(S, (0, 1, 2), H) + [
            pl.BlockSpec((1, len(DILATIONS), Q_TILE, K_TILE), lambda h: (h, 0, 0, 0))],
        out_specs=[hspec, hspec],
        out_shape=[SDS((S, H * HEAD_DIM), F32), SDS((S, H * HEAD_DIM), F32)],
        scratch_shapes=[padded, padded] + [plain] * 6,
        compiler_params=_cp(1))(proj, proj, proj, bias)


def _put_groups(stage, dproj, groups, n_heads, sems):
    h = pl.program_id(0)
    copies = [pltpu.make_async_copy(
        stage.at[i], dproj.at[:, pl.ds(pl.multiple_of((g * n_heads + h) * HEAD_DIM, HEAD_DIM), HEAD_DIM)], sems.at[i])
        for i, g in enumerate(groups)]
    for cp in copies:
        cp.start()
    for cp in copies:
        cp.wait()


def _attn_bwd(proj, out, lse, dmix, bias, dproj):
    S = proj.shape[0]
    H = proj.shape[1] // (N_GROUPS * HEAD_DIM)
    scale = HEAD_DIM ** -0.5
    assert S // DILATIONS[2] >= 2 * Q_TILE

    def body(q_ref, k_ref, v_ref, o_ref, lse_ref, do_ref, b_ref, dproj_in, dproj_out,
             kp, vp, dkp, dvp, dsum, q4, do4, lse4, dsum4, dq_ref, dk_ref, dv_ref, stage, sems):
        _fill_padded(kp, k_ref, S)
        _fill_padded(vp, v_ref, S)
        dkp[...] = jnp.zeros_like(dkp)
        dvp[...] = jnp.zeros_like(dvp)
        dq_ref[...] = jnp.zeros_like(dq_ref)
        dsum[...] = jnp.broadcast_to(jnp.sum(do_ref[...] * o_ref[...], axis=-1, keepdims=True), dsum.shape)

        def run(n_tiles, tile_rows, p, L, q_src, do_src, lse_src, dsum_src, dq_dst, dq_adds):
            def tiles(t, carry):
                rows = [tile_rows(t + u * (n_tiles // BWD_TILE_GROUP)) for u in range(BWD_TILE_GROUP)]
                qs, ks, vs, dos, lses, dsums, dk_old, dv_old, edge = _stacked(rows, (
                    lambda a, qr, kr: q_src[qr, :].astype(BF16), lambda a, qr, kr: kp[kr, :].astype(BF16),
                    lambda a, qr, kr: vp[kr, :].astype(BF16), lambda a, qr, kr: do_src[qr, :].astype(BF16),
                    lambda a, qr, kr: lse_src[qr, :], lambda a, qr, kr: dsum_src[qr, :],
                    lambda a, qr, kr: dkp[kr, :], lambda a, qr, kr: dvp[kr, :], lambda a, qr, kr: _edge_mask(a, L)))
                s = _bdot(qs, ks, _BNT) * scale + b_ref[0, p][None] + edge
                pr = jnp.exp(s - _lanes(lses, K_TILE))
                ds = (pr * (_bdot(dos, vs, _BNT) - _lanes(dsums, K_TILE)) * scale).astype(BF16)
                dq_new = _bdot(ds, ks, _BNN)
                if dq_adds:
                    dq_new = dq_new + jnp.stack([dq_dst[qr, :] for _, qr, _ in rows])
                dk_new = dk_old + _bdot(ds, qs, _BTN)
                dv_new = dv_old + _bdot(pr.astype(BF16), dos, _BTN)
                for u, (_, qr, kr) in enumerate(rows):
                    dq_dst[qr, :] = dq_new[u]
                    dkp[kr, :] = dk_new[u]
                    dvp[kr, :] = dv_new[u]
                return carry

            lax.fori_loop(0, n_tiles // BWD_TILE_GROUP, tiles, 0)

        L, per_class, n_tiles = _attn_tiles(S, DILATIONS[0])
        run(n_tiles, functools.partial(_tile_rows, d=DILATIONS[0], per_class=per_class), 0, L,
            q_ref, do_ref, lse_ref, dsum, dq_ref, True)
        dk_ref[...] = dkp[pl.ds(KV_PAD, S), :]
        dv_ref[...] = dvp[pl.ds(KV_PAD, S), :]

        for dst, src in ((q4, q_ref), (do4, do_ref), (lse4, lse_ref), (dsum4, dsum)):
            _to_quarters(dst, src, S)
        _to_quarters(kp, k_ref, S, KV_PAD)
        _to_quarters(vp, v_ref, S, KV_PAD)
        dkp[...] = jnp.zeros_like(dkp)
        dvp[...] = jnp.zeros_like(dvp)
        dq3 = dsum
        run(_attn_tiles(S, DILATIONS[2])[2], functools.partial(_quarter_tile_rows, S=S), 2, S // DILATIONS[2],
            q4, do4, lse4, dsum4, dq3, False)
        run(_attn_tiles(S, DILATIONS[1])[2], functools.partial(_quarter_band_rows, S=S), 1, S // DILATIONS[1],
            q4, do4, lse4, dsum4, dq3, True)
        for r in range(4):
            nat, qtr = pl.ds(r, S // 4, stride=4), pl.ds(r * (S // 4), S // 4)
            pad_qtr = pl.ds(KV_PAD + r * (S // 4), S // 4)
            dq_ref[nat, :] = dq_ref[nat, :] + dq3[qtr, :]
            dk_ref[nat, :] = dk_ref[nat, :] + dkp[pad_qtr, :]
            dv_ref[nat, :] = dv_ref[nat, :] + dvp[pad_qtr, :]
        for i, acc in enumerate((dq_ref, dk_ref, dv_ref)):
            stage[i] = acc[...].astype(BF16)
        _put_groups(stage, dproj_out, (0, 1, 2), H, sems)

    hspec = pl.BlockSpec((S, HEAD_DIM), lambda h: (0, h))
    once = pl.BlockSpec((S, HEAD_DIM), lambda h: (0, h), pipeline_mode=pl.Buffered(1))
    padded, plain = pltpu.VMEM((S + 2 * KV_PAD, HEAD_DIM), F32), pltpu.VMEM((S, HEAD_DIM), F32)
    return pl.pallas_call(
        body, grid=(H,), name="attn_bwd",
        in_specs=_head_specs(S, (0, 1, 2), H) + [
            once, hspec, hspec, pl.BlockSpec((1, len(DILATIONS), Q_TILE, K_TILE), lambda h: (h, 0, 0, 0)), ANY],
        out_specs=ANY, out_shape=SDS(dproj.shape, dproj.dtype), input_output_aliases={7: 0},
        scratch_shapes=[padded] * 4 + [plain] * 8 + [pltpu.VMEM((3, S, HEAD_DIM), BF16), pltpu.SemaphoreType.DMA((3,))],
        compiler_params=_cp(1))(proj, proj, proj, out, lse, dmix, bias, dproj)


def _ret_consts(lg, forward):
    C = RET_CHUNK
    i = lax.broadcasted_iota(jnp.int32, (C, C), 0)
    j = lax.broadcasted_iota(jnp.int32, (C, C), 1)
    rel = (i - j) if forward else (j - i)
    inside = (rel >= 0) if forward else (rel > 0)
    relf = jnp.maximum(rel, 0).astype(F32)
    mask = jnp.where(inside, jnp.exp(lg * relf), 0.0)
    idx = lax.broadcasted_iota(jnp.int32, (C, 1), 0).astype(F32)
    q_exp = (idx + 1.0) if forward else (C - idx)
    k_exp = (C - 1.0 - idx) if forward else idx
    return mask, relf, jnp.exp(lg * q_exp), q_exp, jnp.exp(lg * k_exp), k_exp, jnp.exp(lg * C)


def _log_decay(dec_ref, h):
    return -jnp.exp(jnp.full((1, 1), dec_ref[0, h], F32))


FFN_BLOCK = 704
CHUNK_BATCH = 16


def _batch_rows(b):
    n = CHUNK_BATCH * RET_CHUNK
    return pl.ds(pl.multiple_of(b * n, n), n)


def _batch_chunks(b):
    return pl.ds(pl.multiple_of(b * CHUNK_BATCH, CHUNK_BATCH), CHUNK_BATCH)


def _chunks3(x):
    return x.reshape(CHUNK_BATCH, RET_CHUNK, HEAD_DIM)


def _ret_scan(buf, c_decs, nc, reverse):
    def step(n, carry):
        new = []
        for way, r in enumerate(carry):
            c = n if (way == 0) != reverse else nc - 1 - n
            term = buf[way, c]
            buf[way, c] = r
            new.append(r * c_decs[way] + term)
        return tuple(new)

    lax.fori_loop(0, nc, step, (jnp.zeros((HEAD_DIM, HEAD_DIM), F32),) * 2)


def _ret_fwd(proj, dec_f, dec_b, w_norm):
    S = proj.shape[0]
    H = proj.shape[1] // (N_GROUPS * HEAD_DIM)
    nc = S // RET_CHUNK
    scale = HEAD_DIM ** -0.5

    def body(df_ref, db_ref, q_ref, k_ref, v_ref, g_ref, w_ref, y_ref, o_ref, states):
        h = pl.program_id(0)
        consts = [_ret_consts(_log_decay(dref, h), fw) for fw, dref in ((True, df_ref), (False, db_ref))]

        def kv_step(b, carry):
            rows, batch = _batch_rows(b), _batch_chunks(b)
            k3 = _chunks3(k_ref[rows, :])
            v3 = _chunks3(v_ref[rows, :]).astype(BF16)
            for way in range(2):
                states[way, batch] = _bdot((k3 * consts[way][4]).astype(BF16), v3, _BTN)
            return carry

        lax.fori_loop(0, nc // CHUNK_BATCH, kv_step, 0)
        _ret_scan(states, [c[6] for c in consts], nc, False)

        def out_step(b, carry):
            rows, batch = _batch_rows(b), _batch_chunks(b)
            q3 = _chunks3(q_ref[rows, :] * scale)
            k3 = _chunks3(k_ref[rows, :]).astype(BF16)
            v3 = _chunks3(v_ref[rows, :]).astype(BF16)
            a0 = _bdot(q3.astype(BF16), k3, _BNT)
            o = None
            for way in range(2):
                mask, q_dec = consts[way][0], consts[way][2]
                part = _bdot((a0 * mask).astype(BF16), v3, _BNN) \
                    + _bdot((q3 * q_dec).astype(BF16), states[way, batch].astype(BF16), _BNN)
                o = part if o is None else o + part
            o_ref[rows, :] = o.reshape(CHUNK_BATCH * RET_CHUNK, HEAD_DIM)
            return carry

        lax.fori_loop(0, nc // CHUNK_BATCH, out_step, 0)
        o = o_ref[...]
        g = g_ref[...]
        y_ref[...] = o * _rms_scale(o) * w_ref[...] * (g * _sigmoid(g))

    hspec = pl.BlockSpec((S, HEAD_DIM), lambda h: (0, h))
    smem = pl.BlockSpec(memory_space=pltpu.SMEM)
    return pl.pallas_call(
        body, grid=(H,), name="ret_fwd",
        in_specs=[smem, smem] + _head_specs(S, (3, 4, 5, 6), H) + [pl.BlockSpec((1, HEAD_DIM), lambda h: (0, h))],
        out_specs=[hspec, hspec],
        out_shape=[SDS((S, H * HEAD_DIM), F32)] * 2,
        scratch_shapes=[pltpu.VMEM((2, nc, HEAD_DIM, HEAD_DIM), F32)],
        compiler_params=_cp(1))(dec_f, dec_b, proj, proj, proj, proj, w_norm)


def _ret_gate_bwd(proj, o_raw, dmix, w_norm, col0, dproj):
    S = proj.shape[0]
    H = proj.shape[1] // (N_GROUPS * HEAD_DIM)

    def body(g_ref, o_ref, dy_ref, w_ref, dproj_in, do_ref, dw_ref, dproj_out, dg_ref, sems):
        o = o_ref[...]
        g = g_ref[...]
        dy = dy_ref[...]
        w = w_ref[...]
        rr = _rms_scale(o)
        normed = o * rr
        sg = _sigmoid(g)
        silu = g * sg
        dw_ref[0] = jnp.broadcast_to(jnp.sum(dy * normed * silu, axis=0, keepdims=True), (8, HEAD_DIM))
        dg_ref[0] = (dy * normed * w * (sg * (1.0 + g * (1.0 - sg)))).astype(BF16)
        dnormed = dy * w * silu
        do_ref[...] = rr * dnormed - o * (rr * rr * rr) * jnp.mean(dnormed * o, axis=-1, keepdims=True)
        _put_groups(dg_ref, dproj_out, (6,), H, sems)

    hspec = pl.BlockSpec((S, HEAD_DIM), lambda h: (0, h))
    nh0 = col0 // HEAD_DIM
    return pl.pallas_call(
        body, grid=(H,), name="ret_gate_bwd",
        in_specs=_head_specs(S, (6,), H) + [hspec, pl.BlockSpec((S, HEAD_DIM), lambda h: (0, nh0 + h)),
                                            pl.BlockSpec((1, HEAD_DIM), lambda h: (0, h)), ANY],
        out_specs=[hspec, pl.BlockSpec((1, 8, HEAD_DIM), lambda h: (h, 0, 0)), ANY],
        out_shape=[SDS((S, H * HEAD_DIM), F32), SDS((H, 8, HEAD_DIM), F32), SDS(dproj.shape, dproj.dtype)],
        input_output_aliases={4: 2},
        scratch_shapes=[pltpu.VMEM((1, S, HEAD_DIM), BF16), pltpu.SemaphoreType.DMA((1,))],
        compiler_params=_cp(1))(proj, o_raw, dmix, w_norm, dproj)


def _ret_bwd(proj, d_out, dec_f, dec_b, dproj):
    S = proj.shape[0]
    H = proj.shape[1] // (N_GROUPS * HEAD_DIM)
    C = RET_CHUNK
    nc = S // C
    scale = HEAD_DIM ** -0.5

    def body(df_ref, db_ref, q_ref, k_ref, v_ref, do, dproj_in, small_ref, dproj_out, states, d_states, stage, sems):
        h = pl.program_id(0)
        lgs = [_log_decay(df_ref, h), _log_decay(db_ref, h)]
        consts = [_ret_consts(lg, fw) for lg, fw in zip(lgs, (True, False))]

        def prep_step(b, carry):
            rows, batch = _batch_rows(b), _batch_chunks(b)
            q3 = _chunks3(q_ref[rows, :] * scale)
            k3 = _chunks3(k_ref[rows, :])
            v3 = _chunks3(v_ref[rows, :]).astype(BF16)
            do3 = _chunks3(do[rows, :]).astype(BF16)
            for way in range(2):
                states[way, batch] = _bdot((k3 * consts[way][4]).astype(BF16), v3, _BTN)
                d_states[way, batch] = _bdot((q3 * consts[way][2]).astype(BF16), do3, _BTN)
            return carry

        lax.fori_loop(0, nc // CHUNK_BATCH, prep_step, 0)
        c_decs = [c[6] for c in consts]
        _ret_scan(states, c_decs, nc, False)
        _ret_scan(d_states, c_decs, nc, True)

        def main_step(b, dlams):
            rows, batch = _batch_rows(b), _batch_chunks(b)
            q3 = _chunks3(q_ref[rows, :] * scale)
            k3 = _chunks3(k_ref[rows, :])
            q3b, k3b = q3.astype(BF16), k3.astype(BF16)
            v3b = _chunks3(v_ref[rows, :]).astype(BF16)
            do3b = _chunks3(do[rows, :]).astype(BF16)
            a0 = _bdot(q3b, k3b, _BNT)
            pv = _bdot(do3b, v3b, _BNT)
            dq = dk = dv = None
            new_dlams = []
            for way in range(2):
                mask, relf, q_dec, q_exp, k_dec, k_exp, c_dec = consts[way]
                state, d_state = states[way, batch], d_states[way, batch]
                dp = pv * mask
                dpb = dp.astype(BF16)
                gq = _bdot(do3b, state.astype(BF16), _BNT)
                gk = _bdot(v3b, d_state.astype(BF16), _BNT)
                parts = (_bdot(dpb, k3b, _BNN) + q_dec * gq, _bdot(dpb, q3b, _BTN) + k_dec * gk,
                         _bdot((a0 * mask).astype(BF16), do3b, _BTN)
                         + _bdot((k3 * k_dec).astype(BF16), d_state.astype(BF16), _BNN))
                dq, dk, dv = parts if dq is None else (dq + parts[0], dk + parts[1], dv + parts[2])
                total = lambda x: jnp.sum(jnp.sum(x, axis=0), axis=0, keepdims=True)
                new_dlams.append(dlams[way] + total(relf * a0 * dp)
                                 + total(q_exp * q_dec * q3 * gq + k_exp * k_dec * k3 * gk)
                                 + (C * c_dec) * total(state * d_state))
            flat = lambda x: x.reshape(CHUNK_BATCH * C, HEAD_DIM)
            stage[0, rows, :] = (flat(dq) * scale).astype(BF16)
            stage[1, rows, :] = flat(dk).astype(BF16)
            stage[2, rows, :] = flat(dv).astype(BF16)
            return tuple(new_dlams)

        dlams = lax.fori_loop(0, nc // CHUNK_BATCH, main_step, (jnp.zeros((1, HEAD_DIM), F32),) * 2)
        for row, (dlam, lg) in enumerate(zip(dlams, lgs)):
            small_ref[0, pl.ds(row, 1), :] = jnp.broadcast_to(jnp.sum(dlam, axis=-1, keepdims=True) * lg, (1, HEAD_DIM))
        small_ref[0, pl.ds(2, 6), :] = jnp.zeros((6, HEAD_DIM), F32)
        _put_groups(stage, dproj_out, (3, 4, 5), H, sems)

    hspec = pl.BlockSpec((S, HEAD_DIM), lambda h: (0, h))
    smem = pl.BlockSpec(memory_space=pltpu.SMEM)
    return pl.pallas_call(
        body, grid=(H,), name="ret_bwd",
        in_specs=[smem, smem] + _head_specs(S, (3, 4, 5), H) + [hspec, ANY],
        out_specs=[pl.BlockSpec((1, 8, HEAD_DIM), lambda h: (h, 0, 0)), ANY],
        out_shape=[SDS((H, 8, HEAD_DIM), F32), SDS(dproj.shape, dproj.dtype)], input_output_aliases={6: 1},
        scratch_shapes=[pltpu.VMEM((2, nc, HEAD_DIM, HEAD_DIM), F32), pltpu.VMEM((2, nc, HEAD_DIM, HEAD_DIM), F32),
                        pltpu.VMEM((3, S, HEAD_DIM), BF16), pltpu.SemaphoreType.DMA((3,))],
        compiler_params=_cp(1))(dec_f, dec_b, proj, proj, proj, d_out, dproj)


def _ffn_bwd_act(dh2, wd, g, u):
    S, D = dh2.shape
    nblk, _, FB = g.shape
    tm = min(1024, S)

    def body(dh_ref, wd_ref, g_ref, u_ref, dg_ref, du_ref):
        dact = _dot(dh_ref[...], wd_ref[...], _NT)
        gg = g_ref[0].astype(F32)
        sg = _sigmoid(gg)
        dg_ref[0] = (dact * u_ref[0].astype(F32) * (sg * (1.0 + gg * (1.0 - sg)))).astype(BF16)
        du_ref[0] = (dact * (gg * sg)).astype(BF16)

    blk = pl.BlockSpec((1, tm, FB), lambda j, i: (j, i, 0))
    return pl.pallas_call(
        body, grid=(nblk, S // tm), name="ffn_bwd_act",
        in_specs=[pl.BlockSpec((tm, D), lambda j, i: (i, 0)), pl.BlockSpec((FB, D), lambda j, i: (j, 0)), blk, blk],
        out_specs=[blk, blk], out_shape=[SDS((nblk, S, FB), BF16)] * 2,
        compiler_params=_cp(2))(dh2, wd, g, u)


def _ffn_bwd_in(dg, du, wg, wu, h1, dh2, w_norm):
    nblk, S, FB = dg.shape
    D = h1.shape[1]
    tm = min(RESIDENT_ROWS, S)
    blk = pl.BlockSpec((nblk, tm, FB), lambda i: (0, i, 0))
    row = pl.BlockSpec((tm, D), lambda i: (i, 0))
    vec = pl.BlockSpec((1, D), lambda i: (0, 0))

    def gate_body(dg_ref, wg_ref, part_ref):
        part_ref[...] = _blocked_matmul(dg_ref, wg_ref)

    tg = min(2 * tm, S)
    part = pl.pallas_call(
        gate_body, grid=(S // tg,), name="ffn_bwd_in_gate",
        in_specs=[pl.BlockSpec((nblk, tg, FB), lambda i: (0, i, 0)), _resident((nblk * FB, D))],
        out_specs=pl.BlockSpec((tg, D), lambda i: (i, 0)), out_shape=SDS((S, D), F32),
        compiler_params=_cp(1))(dg, wg.reshape(nblk * FB, D))

    def body(du_ref, wu_ref, part_ref, h_ref, dh2_ref, wn_ref, dh_ref, dhb_ref, dw_ref):
        @pl.when(pl.program_id(0) == 0)
        def _():
            dw_ref[...] = jnp.zeros_like(dw_ref)

        dh, dw = _rms_bwd(part_ref[...] + _blocked_matmul(du_ref, wu_ref), h_ref[...], wn_ref[...])
        dh = dh2_ref[...] + dh
        dh_ref[...] = dh
        dhb_ref[...] = dh.astype(BF16)
        dw_ref[...] += dw

    return pl.pallas_call(
        body, grid=(S // tm,), name="ffn_bwd_in",
        in_specs=[blk, _resident((nblk * FB, D)), row, row, row, vec],
        out_specs=[row, row, vec], out_shape=[SDS((S, D), F32), SDS((S, D), BF16), SDS((1, D), F32)],
        compiler_params=_cp(1))(du, wu.reshape(nblk * FB, D), part, h1, dh2, w_norm)


def _dmix(dh1, w_out):
    S, D = dh1.shape
    tm = min(512, S)

    def body(dh_ref, w_ref, o_ref):
        o_ref[...] = _dot(dh_ref[...], w_ref[...], _NT)

    row = pl.BlockSpec((tm, D), lambda i: (i, 0))
    return pl.pallas_call(
        body, grid=(S // tm,), name="dmix", in_specs=[row, pl.BlockSpec((D, D), lambda i: (0, 0))],
        out_specs=row, out_shape=SDS((S, D), F32), compiler_params=_cp(1))(dh1, w_out)


def _in_bwd(dproj, w_blk, x, dh1, w_norm):
    S, D = x.shape
    nblk, _, NB = w_blk.shape
    tm = min(RESIDENT_ROWS, S)

    def body(dp_ref, w_ref, x_ref, dh1_ref, wn_ref, dx_ref, dw_ref):
        @pl.when(pl.program_id(0) == 0)
        def _():
            dw_ref[...] = jnp.zeros_like(dw_ref)

        dn = None
        for j in range(nblk):
            part = _dot(dp_ref[:, pl.ds(j * NB, NB)], w_ref[j], _NT)
            dn = part if dn is None else dn + part
        dh, dw = _rms_bwd(dn, x_ref[...], wn_ref[...])
        dx_ref[...] = dh1_ref[...] + dh
        dw_ref[...] += dw

    row = pl.BlockSpec((tm, D), lambda i: (i, 0))
    vec = pl.BlockSpec((1, D), lambda i: (0, 0))
    return pl.pallas_call(
        body, grid=(S // tm,), name="in_bwd",
        in_specs=[pl.BlockSpec((tm, nblk * NB), lambda i: (i, 0)),
                  pl.BlockSpec((nblk, D, NB), lambda i: (0, 0, 0), pipeline_mode=pl.Buffered(1)), row, row, vec],
        out_specs=[row, vec], out_shape=[SDS((S, D), F32), SDS((1, D), F32)],
        compiler_params=_cp(1))(dproj, w_blk, x, dh1, w_norm)


def _wgrad(a, b, a_spec, b_spec, o_spec, o_shape, grid, name):
    nk = grid[-1]

    def ld(ref):
        return ref[0] if len(ref.shape) == 3 else ref[...]

    def body(a_ref, b_ref, o_ref, acc):
        k = pl.program_id(len(grid) - 1)

        @pl.when(k == 0)
        def _():
            acc[...] = jnp.zeros_like(acc)

        acc[...] += _dot(ld(a_ref), ld(b_ref), _TN)

        @pl.when(k == nk - 1)
        def _():
            if len(o_ref.shape) == 3:
                o_ref[0] = acc[...].astype(o_ref.dtype)
            else:
                o_ref[...] = acc[...].astype(o_ref.dtype)

    return pl.pallas_call(
        body, grid=grid, name=name, in_specs=[a_spec, b_spec], out_specs=o_spec, out_shape=SDS(o_shape, BF16),
        scratch_shapes=[pltpu.VMEM(o_spec.block_shape[-2:], F32)], compiler_params=_cp(len(grid)))(a, b)


def _peer(k):
    x, y, c = lax.axis_index("x"), lax.axis_index("y"), lax.axis_index("c")
    px = 1 - x if k & 4 else x
    py = 1 - y if k & 2 else y
    pc = 1 - c if k & 1 else c
    return (px, py, pc), 4 * px + 2 * py + pc


def _exchange_copies(srcs, lands, send_sems, recv_sems, which, gather):
    _, me = _peer(0)
    pairs = []
    for pos, a in enumerate(which):
        for k in range(1, N_DEV):
            dev, idx = _peer(k)
            sem = pos * (N_DEV - 1) + k - 1
            src = srcs[a] if gather else srcs[a].at[idx]
            mk = functools.partial(pltpu.make_async_remote_copy, src_ref=src, send_sem=send_sems.at[sem],
                                   recv_sem=recv_sems.at[sem], device_id=dev, device_id_type=MESH)
            pairs.append((mk(dst_ref=lands[a].at[me]), mk(dst_ref=lands[a].at[idx])))
    return pairs


def _sequencer_kernel(name, collective_id, n_remote, n_local):
    return pl.kernel(mesh=plsc.ScalarSubcoreMesh(axis_name="sequencer", num_cores=1), name=name,
                     scratch_types=(pltpu.SemaphoreType.DMA((n_remote,)), pltpu.SemaphoreType.DMA((n_remote,)),
                                    pltpu.SemaphoreType.DMA((n_local,))),
                     compiler_params=pltpu.CompilerParams(collective_id=collective_id))


def _handshake(ks):
    barrier = pltpu.get_barrier_semaphore()
    for k in ks:
        pl.semaphore_signal(barrier, inc=1, device_id=_peer(k)[0], device_id_type=MESH)
    pl.semaphore_wait(barrier, len(ks))


def _sequencer_scatter(arrays, name, collective_id):
    n = len(arrays)
    hbm = pltpu.MemorySpace.HBM
    srcs = [jax.new_ref(a, memory_space=hbm) for a in arrays]
    lands = [jax.empty_ref(SDS(a.shape, a.dtype), memory_space=hbm) for a in arrays]

    @_sequencer_kernel(name, collective_id, n * (N_DEV - 1), n)
    def launch(send_sems, recv_sems, local_sems):
        _handshake(range(1, N_DEV))
        _, me = _peer(0)
        local = [pltpu.make_async_copy(srcs[a].at[me], lands[a].at[me], local_sems.at[a]) for a in range(n)]
        pairs = _exchange_copies(srcs, lands, send_sems, recv_sems, range(n), False)
        for out, _ in pairs:
            out.start()
        for cp in local:
            cp.start()
        for out, arrival in pairs:
            out.wait_send()
            arrival.wait_recv()
        for cp in local:
            cp.wait()

    launch()
    return [r[...] for r in lands]


SIBLING = 1
OTHER_CHIPS = (2, 4, 6)


def _sequencer_gather(arrays, name, collective_id):
    n = len(arrays)
    hbm = pltpu.MemorySpace.HBM
    srcs = [jax.new_ref(a, memory_space=hbm) for a in arrays]
    lands = [jax.empty_ref(SDS((N_DEV,) + a.shape, a.dtype), memory_space=hbm) for a in arrays]

    @_sequencer_kernel(name, collective_id, n * (N_DEV - 1), n)
    def launch(send_sems, recv_sems, local_sems):
        _handshake((SIBLING,) + OTHER_CHIPS)
        _, me = _peer(0)
        sibling, _ = _peer(SIBLING)

        def copy(a, k, src, block, to):
            sem = a * (N_DEV - 1) + k - 1
            return pltpu.make_async_remote_copy(src_ref=src, dst_ref=lands[a].at[block], send_sem=send_sems.at[sem],
                                                recv_sem=recv_sems.at[sem], device_id=to, device_id_type=MESH)

        local = [pltpu.make_async_copy(srcs[a], lands[a].at[me], local_sems.at[a]) for a in range(n)]
        first = [copy(a, k, srcs[a], me, _peer(k)[0]) for a in range(n) for k in OTHER_CHIPS + (SIBLING,)]
        for cp in first + local:
            cp.start()
        passed = []
        for a in range(n):
            for k in OTHER_CHIPS:
                _, block = _peer(k)
                copy(a, k, srcs[a], block, sibling).wait_recv()
                passed.append(copy(a, k ^ SIBLING, lands[a].at[block], block, sibling))
                passed[-1].start()
        for a in range(n):
            for k in (SIBLING,) + tuple(k ^ SIBLING for k in OTHER_CHIPS):
                copy(a, k, srcs[a], _peer(k)[1], sibling).wait_recv()
        for cp in first + passed:
            cp.wait_send()
        for cp in local:
            cp.wait()

    launch()
    return [r[...] for r in lands]


def _sequencer_gather_chips(array, name, collective_id, chips):
    hbm = pltpu.MemorySpace.HBM
    src = jax.new_ref(array, memory_space=hbm)
    land = jax.empty_ref(SDS((2 * len(chips),) + array.shape, array.dtype), memory_space=hbm)

    @_sequencer_kernel(name, collective_id, 2 * len(chips), 1)
    def launch(send_sems, recv_sems, local_sems):
        _handshake((SIBLING,) + tuple(k for k in chips if k))
        c = lax.axis_index("c")
        sibling, _ = _peer(SIBLING)

        def copy(sem, src_ref, slot, to):
            return pltpu.make_async_remote_copy(src_ref=src_ref, dst_ref=land.at[slot], send_sem=send_sems.at[sem],
                                                recv_sem=recv_sems.at[sem], device_id=to, device_id_type=MESH)

        started = []
        for pos, k in enumerate(chips):
            started.append(copy(2 * pos, src, 2 * pos + c, _peer(k)[0] if k else sibling))
            started[-1].start()
        for pos, k in enumerate(chips):
            if k:
                copy(2 * pos, src, 2 * pos + c, sibling).wait_recv()
                started.append(copy(2 * pos + 1, land.at[2 * pos + c], 2 * pos + c, sibling))
                started[-1].start()
        for pos, k in enumerate(chips):
            copy(2 * pos + 1 if k else 2 * pos, src, 2 * pos + 1 - c, sibling).wait_recv()
        for cp in started:
            cp.wait_send()

    launch()
    return land[...]


SMALL_ROWS = 64


def _small_step(part, w, m, v):
    def body(p_ref, w_ref, m_ref, v_ref, g_ref, d_ref, nm_ref, nv_ref, gath, send_sems, recv_sems):
        _, me = _peer(0)
        gath[me] = p_ref[...]
        copies = []
        for k in range(1, N_DEV):
            dev, idx = _peer(k)
            out = pltpu.make_async_remote_copy(src_ref=p_ref, dst_ref=gath.at[me], send_sem=send_sems.at[k - 1],
                                               recv_sem=recv_sems.at[k - 1], device_id=dev, device_id_type=MESH)
            out.start()
            arrival = pltpu.make_async_remote_copy(src_ref=p_ref, dst_ref=gath.at[idx], send_sem=send_sems.at[k - 1],
                                                   recv_sem=recv_sems.at[k - 1], device_id=dev, device_id_type=MESH)
            copies.append((out, arrival))
        for out, arrival in copies:
            out.wait_send()
            arrival.wait_recv()
        g = gath[0]
        for p in range(1, N_DEV):
            g = g + gath[p]
        g_ref[...] = g
        d_ref[...], nm_ref[...], nv_ref[...] = _adamw(w_ref[...], g, m_ref[...], v_ref[...])

    vm = pl.BlockSpec(memory_space=pltpu.VMEM)
    return pl.pallas_call(
        body, name="small_step", in_specs=[vm] * 4, out_specs=[vm] * 4,
        out_shape=[SDS((SMALL_ROWS, 128), F32)] * 4,
        scratch_shapes=[pltpu.VMEM((N_DEV, SMALL_ROWS, 128), F32), pltpu.SemaphoreType.DMA((N_DEV - 1,)),
                        pltpu.SemaphoreType.DMA((N_DEV - 1,))])(part, w, m, v)


def _adamw(w, g, m, v):
    m = ADAM_B1 * m + (1.0 - ADAM_B1) * g
    v = ADAM_B2 * v + (1.0 - ADAM_B2) * (g * g)
    m_hat = m / (1.0 - ADAM_B1 ** ADAM_STEP)
    v_hat = v / (1.0 - ADAM_B2 ** ADAM_STEP)
    delta = -ADAM_LR * (m_hat / (jnp.sqrt(v_hat) + ADAM_EPS) + ADAM_WD * w)
    return delta, m, v


def _adamw_block(parts, w, m, v, name):
    R, C = w.shape
    n_parts = len(parts)
    Rp = R // n_parts
    tr = next(t for t in (256, 128, 64, 32, 16, 8) if Rp % t == 0 and t * C <= 256 * 1024)
    per_part = Rp // tr

    def body(*refs):
        p_refs = refs[:n_parts]
        w_ref, m_ref, v_ref, g_ref, d_ref, nm_ref, nv_ref = refs[n_parts:]
        for k, p_ref in enumerate(p_refs):
            @pl.when(pl.program_id(0) // per_part == k)
            def _(p_ref=p_ref):
                g = p_ref[0].astype(F32)
                for p in range(1, N_DEV):
                    g = g + p_ref[p].astype(F32)
                g_ref[...] = g
                d_ref[...], nm_ref[...], nv_ref[...] = _adamw(w_ref[...], g, m_ref[...], v_ref[...])

    row = pl.BlockSpec((tr, C), lambda i: (i, 0))
    part_specs = [pl.BlockSpec((N_DEV, tr, C), functools.partial(
        lambda i, k: (0, jnp.clip(i - k * per_part, 0, per_part - 1), 0), k=k)) for k in range(n_parts)]
    return pl.pallas_call(
        body, grid=(R // tr,), name=name, in_specs=part_specs + [row, row, row],
        out_specs=[row] * 4, out_shape=[SDS((R, C), F32)] * 4, compiler_params=_cp(1))(*parts, w, m, v)


def _pack_small(mix, ffn, fin, retw, dec_f, dec_b, loss):
    flat = jnp.concatenate([mix.reshape(-1), ffn.reshape(-1), fin.reshape(-1), retw.reshape(-1), dec_f.reshape(-1),
                            dec_b.reshape(-1), loss.reshape(-1)])
    return jnp.pad(flat, (0, SMALL_ROWS * 128 - flat.shape[0])).reshape(SMALL_ROWS, 128)


def _unpack_small(packed, shapes):
    flat = packed.reshape(-1)
    out, at = [], 0
    for s in shapes:
        n = math.prod(s)
        out.append(flat[at:at + n].reshape(s))
        at += n
    return out


def kernel(x, norm_mix_w, w_in, ret_decay_fwd, ret_decay_bwd, ret_norm_w, w_out, norm_ffn_w, w_gate, w_up, w_down, norm_final_w, loss_target, m_norm_mix_w, m_w_in, m_ret_decay_fwd, m_ret_decay_bwd, m_ret_norm_w, m_w_out, m_norm_ffn_w, m_w_gate, m_w_up, m_w_down, m_norm_final_w, v_norm_mix_w, v_w_in, v_ret_decay_fwd, v_ret_decay_bwd, v_ret_norm_w, v_w_out, v_norm_ffn_w, v_w_gate, v_w_up, v_w_down, v_norm_final_w):
    x2 = x[0]
    tgt = loss_target[0]
    S, D = x2.shape
    H = ret_norm_w.shape[1] // HEAD_DIM
    DA = H * HEAD_DIM
    fin_w = norm_final_w.reshape(1, D)
    big = (w_in[0], w_out[0], w_gate[0].T, w_up[0].T, w_down[0])

    big_b = [w.astype(BF16) for w in big]
    stages = ((0,), (4, 2), (6,))
    wi_stages = [_sequencer_gather_chips(big_b[0], name, cid, ks)
                 for name, cid, ks in zip(("gather_in_own", "gather_in_near", "gather_in_far"), (0, 7, 8), stages)]
    wo, = _sequencer_gather(big_b[1:2], "gather_out", 1)
    wg, wu = _sequencer_gather(big_b[2:4], "gather_gate_up", 9)
    wd, = _sequencer_gather(big_b[4:], "gather_down", 5)
    wi, = _sequencer_gather(big_b[:1], "gather_in_ordered", 10)
    NB = big_b[0].shape[1]
    ax, ay = lax.axis_index("x"), lax.axis_index("y")
    chip_of = {k: 2 * (1 - ax if k & 4 else ax) + (1 - ay if k & 2 else ay) for k in (0, 2, 4, 6)}

    n1 = _norm_fwd(x2, norm_mix_w)
    ac = lax.axis_index("c")
    me = 2 * chip_of[0] + ac
    vec = lambda *v: jnp.stack([jnp.asarray(t, jnp.int32) for t in v])
    proj = _proj_part(n1, big_b[0][None], vec(0), vec(me), None, N_DEV, "proj_self")
    for ks, w_st, name in zip(stages, wi_stages, ("proj_sibling", "proj_near", "proj_far")):
        slots, blocks = [], []
        for pos, k in enumerate(ks):
            for core in ((1 - ac,) if k == 0 else (0, 1)):
                slots.append(2 * pos + core)
                blocks.append(2 * chip_of[k] + core)
        proj = _proj_part(n1, w_st, vec(*slots), vec(*blocks), proj, N_DEV, name)
    bias = _attn_bias()[:H]
    attn, lse = _attn_fwd(proj, bias)
    ret, o_raw = _ret_fwd(proj, ret_decay_fwd, ret_decay_bwd, ret_norm_w)
    wo_full = wo.reshape(D, D)
    d_ff = N_DEV * wd.shape[1]
    FB = FFN_BLOCK if d_ff % FFN_BLOCK == 0 else wd.shape[1]
    n_fb = d_ff // FB
    wg, wu = wg.reshape(n_fb, FB, D), wu.reshape(n_fb, FB, D)
    wd_full = wd.reshape(d_ff, D)
    h1, mixed, n2 = _out_fwd(x2, attn, ret, wo_full, norm_ffn_w)
    gate, up, act = _ffn_up(n2, wg, wu)
    dh2, dh2_b, loss_parts, g_fin = _ffn_down_loss(act, wd_full, h1, tgt, fin_w)

    dgate, dup = _ffn_bwd_act(dh2_b, wd_full, gate, up)
    tn = min(1024, D)
    ffn_specs = (pl.BlockSpec((1, S, FB), lambda j, n, k: (j, 0, 0)), pl.BlockSpec((S, tn), lambda j, n, k: (0, n)),
                 pl.BlockSpec((1, FB, tn), lambda j, n, k: (j, 0, n)), (n_fb, FB, D), (n_fb, D // tn, 1))
    per_dev = (N_DEV, d_ff // N_DEV, D)
    g_wd = _wgrad(act, dh2_b, *ffn_specs, "wgrad_down").reshape(per_dev)
    g_wg = _wgrad(dgate, n2, *ffn_specs, "wgrad_gate").reshape(per_dev)
    g_wu = _wgrad(dup, n2, *ffn_specs, "wgrad_up").reshape(per_dev)
    parts_f = _sequencer_scatter([g_wg, g_wu, g_wd], "scatter_ffn", 2)
    dh1, dh1_b, g_ffn = _ffn_bwd_in(dgate, dup, wg, wu, h1, dh2, norm_ffn_w)
    dmix = _dmix(dh1_b, wo_full)
    tmw = min(512, D)
    tk = min(2048, S)
    g_wo = _wgrad(mixed, dh1_b, pl.BlockSpec((tk, tmw), lambda m, k: (k, m)), pl.BlockSpec((tk, D), lambda m, k: (k, 0)),
                  pl.BlockSpec((tmw, D), lambda m, k: (m, 0)), (D, D), (D // tmw, S // tk), "wgrad_out")
    parts_o = _sequencer_scatter([g_wo.reshape(N_DEV, D // N_DEV, D)], "scatter_out", 3)
    d_ret, small_w, dproj = _ret_gate_bwd(proj, o_raw, dmix, ret_norm_w, DA, lax.empty(proj.shape, BF16))
    small, dproj = _ret_bwd(proj, d_ret, ret_decay_fwd, ret_decay_bwd, dproj)
    dproj = _attn_bwd(proj, attn, lse, dmix, bias, dproj)
    half = D // tmw // 2
    parts_i = []
    for part, (name, cid) in enumerate((("in_lo", 4), ("in_hi", 6))):
        g_wi = _wgrad(n1, dproj, pl.BlockSpec((S, tmw), functools.partial(lambda j, m, k, off: (0, m + off), off=part * half)),
                      pl.BlockSpec((S, NB), lambda j, m, k: (0, j)), pl.BlockSpec((1, tmw, NB), lambda j, m, k: (j, m, 0)),
                      (N_DEV, D // 2, NB), (N_DEV, half, 1), "wgrad_" + name)
        parts_i += _sequencer_scatter([g_wi], "scatter_" + name, cid)
    grad_x, g_mix = _in_bwd(dproj, wi, x2, dh1, norm_mix_w)

    big_m = (m_w_in[0], m_w_out[0], m_w_gate[0].T, m_w_up[0].T, m_w_down[0])
    big_v = (v_w_in[0], v_w_out[0], v_w_gate[0].T, v_w_up[0].T, v_w_down[0])
    names = ("adamw_in", "adamw_out", "adamw_gate", "adamw_up", "adamw_down")
    upd = [None] * 5
    for a, p in zip((2, 3, 4, 1, 0), [[t] for t in parts_f + parts_o] + [parts_i]):
        upd[a] = _adamw_block(p, big[a], big_m[a], big_v[a], names[a])

    g_dec_f = small[:, 0, 0].reshape(1, H)
    g_dec_b = small[:, 1, 0].reshape(1, H)
    g_retw = small_w[:, 0, :].reshape(1, DA)
    loss_local = jnp.sum(loss_parts[::8, 0])
    zero = jnp.zeros((1,), F32)
    part = _pack_small(g_mix, g_ffn, g_fin, g_retw, g_dec_f, g_dec_b, loss_local)
    sw = _pack_small(norm_mix_w, norm_ffn_w, norm_final_w, ret_norm_w, ret_decay_fwd, ret_decay_bwd, zero)
    sm = _pack_small(m_norm_mix_w, m_norm_ffn_w, m_norm_final_w, m_ret_norm_w, m_ret_decay_fwd, m_ret_decay_bwd, zero)
    sv = _pack_small(v_norm_mix_w, v_norm_ffn_w, v_norm_final_w, v_ret_norm_w, v_ret_decay_fwd, v_ret_decay_bwd, zero)
    shapes = [(1, D), (1, D), (D,), (1, DA), (1, H), (1, H), ()]
    sg, sd, snm, snv = [_unpack_small(t, shapes) for t in _small_step(part, sw, sm, sv)]
    loss = sg[6]

    def ordered(small_set, k):
        b = [(u[k].T if a in (2, 3) else u[k])[None] for a, u in enumerate(upd)]
        return [small_set[0], b[0], small_set[4], small_set[5], small_set[3], b[1], small_set[1], b[2], b[3], b[4],
                small_set[2]]

    return (loss, grad_x[None], *ordered(sg, 0), *ordered(sd, 1), *ordered(snm, 2), *ordered(snv, 3))
```

```python
import functools
import math

import numpy as np
import jax
import jax.numpy as jnp
from jax import lax
from jax.experimental import pallas as pl
from jax.experimental.pallas import tpu as pltpu
from jax.experimental.pallas import tpu_sc as plsc

F32 = jnp.float32
BF16 = jnp.bfloat16
SDS = jax.ShapeDtypeStruct

HEAD_DIM = 128
EPS = 1e-6
RET_CHUNK = 128
DILATIONS = (1, 4, 16)
BAND = 64
Q_TILE = 128
K_TILE = Q_TILE + 2 * BAND
KV_PAD = BAND * 4
TILE_GROUP = 8
BWD_TILE_GROUP = 8
NEG = -1e30
N_DEV = 8
N_GROUPS = 7
ADAM_LR, ADAM_B1, ADAM_B2, ADAM_EPS, ADAM_WD, ADAM_STEP = 0.001, 0.9, 0.999, 1e-08, 0.01, 10
VMEM_LIMIT = 56 * 1024 * 1024
MESH = pl.DeviceIdType.MESH
ANY = pl.BlockSpec(memory_space=pl.ANY)


def _cp(n_grid):
    return pltpu.CompilerParams(dimension_semantics=("arbitrary",) * n_grid, vmem_limit_bytes=VMEM_LIMIT)


def _sigmoid(x):
    return 1.0 / (1.0 + jnp.exp(-x))


def _rms_scale(h):
    return lax.rsqrt(jnp.mean(h * h, axis=-1, keepdims=True) + EPS)


def _rms_bwd(dn, h, w):
    r = _rms_scale(h)
    gw = dn * w
    dh = r * gw - h * (r * r * r) * jnp.mean(gw * h, axis=-1, keepdims=True)
    return dh, jnp.sum(dn * h * r, axis=0, keepdims=True)


def _dot(a, b, dims):
    return lax.dot_general(a.astype(BF16), b.astype(BF16), (dims, ((), ())), preferred_element_type=F32)


_NN = ((1,), (0,))
_NT = ((1,), (1,))
_TN = ((0,), (0,))


RESIDENT_ROWS = 256


def _resident(shape):
    return pl.BlockSpec(shape, lambda i: (0, 0), pipeline_mode=pl.Buffered(1))


def _blocked_matmul(a_ref, w_ref):
    nblk, _, fb = a_ref.shape
    out = None
    for j in range(nblk):
        part = jnp.dot(a_ref[j], w_ref[pl.ds(j * fb, fb), :], preferred_element_type=F32)
        out = part if out is None else out + part
    return out


def _norm_fwd(x, w_norm):
    S, D = x.shape
    tm = min(1024, S)

    def body(x_ref, wn_ref, n_ref):
        xf = x_ref[...]
        n_ref[...] = (xf * _rms_scale(xf) * wn_ref[...]).astype(BF16)

    row = pl.BlockSpec((tm, D), lambda i: (i, 0))
    return pl.pallas_call(body, grid=(S // tm,), name="norm_fwd", in_specs=[row, pl.BlockSpec((1, D), lambda i: (0, 0))],
                          out_specs=row, out_shape=SDS((S, D), BF16), compiler_params=_cp(1))(x, w_norm)


def _proj_part(n1, w_slots, slots, blocks, proj, n_blocks, name):
    S, D = n1.shape
    NB = w_slots.shape[2]
    tm = min(1024, S)

    def body(slots_ref, blocks_ref, n_ref, w_ref, *rest):
        rest[-1][...] = jnp.dot(n_ref[...], w_ref[0], preferred_element_type=F32)

    out_spec = pl.BlockSpec((tm, NB), lambda i, j, slots, blocks: (i, blocks[j]))
    in_specs = [pl.BlockSpec((tm, D), lambda i, j, slots, blocks: (i, 0)),
                pl.BlockSpec((1, D, NB), lambda i, j, slots, blocks: (slots[j], 0, 0))]
    args = [n1, w_slots]
    if proj is not None:
        in_specs.append(ANY)
        args.append(proj)
    return pl.pallas_call(
        body, name=name, out_shape=SDS((S, n_blocks * NB), F32),
        grid_spec=pltpu.PrefetchScalarGridSpec(num_scalar_prefetch=2, grid=(S // tm, slots.shape[0]), in_specs=in_specs,
                                               out_specs=out_spec),
        input_output_aliases={} if proj is None else {4: 0},
        compiler_params=_cp(2))(slots, blocks, *args)


def _out_fwd(x, attn, ret, w_out, w_norm):
    S, D = x.shape
    DA = attn.shape[1]
    tm = min(512, S)

    def body(x_ref, a_ref, r_ref, w_ref, wn_ref, h_ref, mix_ref, n_ref):
        a = a_ref[...].astype(BF16)
        r = r_ref[...].astype(BF16)
        mix_ref[:, :DA] = a
        mix_ref[:, DA:] = r
        h = x_ref[...] + jnp.dot(a, w_ref[:DA, :], preferred_element_type=F32) \
            + jnp.dot(r, w_ref[DA:, :], preferred_element_type=F32)
        h_ref[...] = h
        n_ref[...] = (h * _rms_scale(h) * wn_ref[...]).astype(BF16)

    row = lambda w: pl.BlockSpec((tm, w), lambda i: (i, 0))
    return pl.pallas_call(
        body, grid=(S // tm,), name="out_fwd",
        in_specs=[row(D), row(DA), row(D - DA), pl.BlockSpec((D, D), lambda i: (0, 0)),
                  pl.BlockSpec((1, D), lambda i: (0, 0))],
        out_specs=[row(D), row(D), row(D)],
        out_shape=[SDS((S, D), F32), SDS((S, D), BF16), SDS((S, D), BF16)],
        compiler_params=_cp(1))(x, attn, ret, w_out, w_norm)


def _ffn_up(n2, wg, wu):
    S, D = n2.shape
    nblk, FB, _ = wg.shape
    tm = min(1024, S)

    def body(n_ref, wg_ref, wu_ref, g_ref, u_ref, a_ref):
        n = n_ref[...]
        g = _dot(n, wg_ref[0], _NT)
        u = _dot(n, wu_ref[0], _NT)
        g_ref[0] = g.astype(BF16)
        u_ref[0] = u.astype(BF16)
        a_ref[0] = (g * _sigmoid(g) * u).astype(BF16)

    wspec = pl.BlockSpec((1, FB, D), lambda j, i: (j, 0, 0))
    ospec = pl.BlockSpec((1, tm, FB), lambda j, i: (j, i, 0))
    return pl.pallas_call(
        body, grid=(nblk, S // tm), name="ffn_up",
        in_specs=[pl.BlockSpec((tm, D), lambda j, i: (i, 0)), wspec, wspec],
        out_specs=[ospec, ospec, ospec],
        out_shape=[SDS((nblk, S, FB), BF16)] * 3,
        compiler_params=_cp(2))(n2, wg, wu)


def _ffn_down_loss(act, wd, h1, target, w_norm):
    nblk, S, FB = act.shape
    D = h1.shape[1]
    tm = min(RESIDENT_ROWS, S)

    def body(a_ref, wd_ref, h_ref, t_ref, wn_ref, dh_ref, dhb_ref, loss_ref, dw_ref):
        @pl.when(pl.program_id(0) == 0)
        def _():
            dw_ref[...] = jnp.zeros_like(dw_ref)

        h = h_ref[...] + _blocked_matmul(a_ref, wd_ref)
        w = wn_ref[...]
        err = h * _rms_scale(h) * w - t_ref[...]
        loss_ref[...] = jnp.full(loss_ref.shape, 0.5 * jnp.sum(err * err) / D, F32)
        dh, dw = _rms_bwd(err * (1.0 / D), h, w)
        dh_ref[...] = dh
        dhb_ref[...] = dh.astype(BF16)
        dw_ref[...] += dw

    row = pl.BlockSpec((tm, D), lambda i: (i, 0))
    vec = pl.BlockSpec((1, D), lambda i: (0, 0))
    return pl.pallas_call(
        body, grid=(S // tm,), name="ffn_down_loss",
        in_specs=[pl.BlockSpec((nblk, tm, FB), lambda i: (0, i, 0)), _resident((nblk * FB, D)), row, row, vec],
        out_specs=[row, row, pl.BlockSpec((8, 128), lambda i: (i, 0)), vec],
        out_shape=[SDS((S, D), F32), SDS((S, D), BF16), SDS((S // tm * 8, 128), F32), SDS((1, D), F32)],
        compiler_params=_cp(1))(act, wd, h1, target, w_norm)


def _attn_bias():
    n_heads = 8
    slopes = np.exp2(-8.0 * np.arange(1, n_heads + 1, dtype=np.float32) / n_heads)
    dist = np.abs(np.arange(K_TILE)[None, :] - BAND - np.arange(Q_TILE)[:, None])
    out = np.empty((n_heads, len(DILATIONS), Q_TILE, K_TILE), np.float32)
    for h in range(n_heads):
        for p, d in enumerate(DILATIONS):
            out[h, p] = np.where(dist <= BAND, -slopes[h] * (d * dist).astype(np.float32), NEG)
    return jnp.asarray(out)


def _attn_tiles(S, d):
    L = S // d
    per_class = L // Q_TILE
    return L, per_class, d * per_class


def _tile_rows(t, d, per_class):
    r = t // per_class
    a = (t % per_class) * Q_TILE
    q_rows = pl.ds(r + d * a, Q_TILE, stride=d) if d > 1 else pl.ds(pl.multiple_of(a, Q_TILE), Q_TILE)
    k_rows = pl.ds(KV_PAD + r + d * (a - BAND), K_TILE, stride=d) if d > 1 else pl.ds(
        pl.multiple_of(KV_PAD + a - BAND, BAND), K_TILE)
    return a, q_rows, k_rows


def _to_quarters(dst, src, n, dst_off=0):
    for r in range(4):
        dst[pl.ds(dst_off + r * (n // 4), n // 4), :] = src[pl.ds(r, n // 4, stride=4), :]


def _quarter_tile_rows(t, S):
    L = S // 16
    per_class = L // Q_TILE
    blk, tt = t // (4 * per_class), t % (4 * per_class)
    r, a = tt // per_class, (tt % per_class) * Q_TILE
    q_rows = pl.ds(blk * (S // 4) + r + 4 * a, Q_TILE, stride=4)
    k_rows = pl.ds(KV_PAD + blk * (S // 4) + r + 4 * (a - BAND), K_TILE, stride=4)
    return a, q_rows, k_rows


def _quarter_band_rows(t, S):
    L = S // 4
    per_quarter = L // Q_TILE
    blk, a = t // per_quarter, (t % per_quarter) * Q_TILE
    q_rows = pl.ds(pl.multiple_of(blk * L + a, Q_TILE), Q_TILE)
    k_rows = pl.ds(pl.multiple_of(KV_PAD + blk * L + a - BAND, BAND), K_TILE)
    return a, q_rows, k_rows


def _lanes(x, width):
    return jnp.concatenate([x] * (width // HEAD_DIM), axis=-1)


_BNT = (((2,), (2,)), ((0,), (0,)))
_BNN = (((2,), (1,)), ((0,), (0,)))
_BTN = (((1,), (1,)), ((0,), (0,)))


def _bdot(a, b, dims):
    return lax.dot_general(a, b, dims, preferred_element_type=F32)


def _stacked(rows, loaders):
    return [jnp.stack([f(*r) for r in rows]) for f in loaders]


def _edge_mask(a, L):
    lk = lax.broadcasted_iota(jnp.int32, (1, K_TILE), 1) + (a - BAND)
    return jnp.where((lk >= 0) & (lk < L), 0.0, NEG).astype(F32)


def _fill_padded(dst, src, S):
    dst[pl.ds(0, KV_PAD), :] = jnp.zeros((KV_PAD, HEAD_DIM), F32)
    dst[pl.ds(KV_PAD + S, KV_PAD), :] = jnp.zeros((KV_PAD, HEAD_DIM), F32)
    dst[pl.ds(KV_PAD, S), :] = src[...]


def _head_specs(S, groups, n_heads):
    return [pl.BlockSpec((S, HEAD_DIM), functools.partial(lambda h, g: (0, g * n_heads + h), g=g)) for g in groups]


def _attn_fwd(proj, bias):
    S = proj.shape[0]
    H = proj.shape[1] // (N_GROUPS * HEAD_DIM)
    scale = HEAD_DIM ** -0.5

    def body(q_ref, k_ref, v_ref, b_ref, o_ref, lse_ref, kp, vp, m_run, l_run, q4, m3, l3, acc3):
        _fill_padded(kp, k_ref, S)
        _fill_padded(vp, v_ref, S)
        o_ref[...] = jnp.zeros_like(o_ref)
        m_run[...] = jnp.full(m_run.shape, NEG, F32)
        l_run[...] = jnp.zeros_like(l_run)
        def online(n_tiles, tile_rows, p, L, q_src, m_buf, l_buf, o_buf):
            def tiles(t, carry):
                rows = [tile_rows(t + u * (n_tiles // TILE_GROUP)) for u in range(TILE_GROUP)]
                qs, ks, vs, m_old, l_old, o_old, edge = _stacked(rows, (
                    lambda a, qr, kr: q_src[qr, :].astype(BF16), lambda a, qr, kr: kp[kr, :].astype(BF16),
                    lambda a, qr, kr: vp[kr, :].astype(BF16), lambda a, qr, kr: m_buf[qr, :],
                    lambda a, qr, kr: l_buf[qr, :], lambda a, qr, kr: o_buf[qr, :], lambda a, qr, kr: _edge_mask(a, L)))
                s = _bdot(qs, ks, _BNT) * scale + b_ref[0, p][None] + edge
                m_new = jnp.maximum(m_old, jnp.max(s, axis=-1, keepdims=True))
                pr = jnp.exp(s - _lanes(m_new, K_TILE)).astype(BF16)
                alpha = jnp.exp(m_old - m_new)
                l_new = alpha * l_old + _bdot(pr, jnp.ones((TILE_GROUP, K_TILE, HEAD_DIM), BF16), _BNN)
                o_new = alpha * o_old + _bdot(pr, vs, _BNN)
                for u, (_, qr, _) in enumerate(rows):
                    o_buf[qr, :] = o_new[u]
                    m_buf[qr, :] = m_new[u]
                    l_buf[qr, :] = l_new[u]
                return carry

            lax.fori_loop(0, n_tiles // TILE_GROUP, tiles, 0)

        L, per_class, n_tiles = _attn_tiles(S, DILATIONS[0])
        online(n_tiles, functools.partial(_tile_rows, d=DILATIONS[0], per_class=per_class), 0, L, q_ref, m_run, l_run, o_ref)

        _to_quarters(q4, q_ref, S)
        _to_quarters(kp, k_ref, S, KV_PAD)
        _to_quarters(vp, v_ref, S, KV_PAD)
        n_tiles = _attn_tiles(S, DILATIONS[2])[2]

        def tiles3(t, carry):
            rows = [_quarter_tile_rows(t + u * (n_tiles // TILE_GROUP), S) for u in range(TILE_GROUP)]
            qs, ks, vs, edge = _stacked(rows, (
                lambda a, qr, kr: q4[qr, :].astype(BF16), lambda a, qr, kr: kp[kr, :].astype(BF16),
                lambda a, qr, kr: vp[kr, :].astype(BF16), lambda a, qr, kr: _edge_mask(a, S // DILATIONS[2])))
            s = _bdot(qs, ks, _BNT) * scale + b_ref[0, 2][None] + edge
            m_new = jnp.broadcast_to(jnp.max(s, axis=-1, keepdims=True), (TILE_GROUP, Q_TILE, HEAD_DIM))
            pr = jnp.exp(s - _lanes(m_new, K_TILE)).astype(BF16)
            l_new = _bdot(pr, jnp.ones((TILE_GROUP, K_TILE, HEAD_DIM), BF16), _BNN)
            o_new = _bdot(pr, vs, _BNN)
            for u, (_, qr, _) in enumerate(rows):
                acc3[qr, :] = o_new[u]
                m3[qr, :] = m_new[u]
                l3[qr, :] = l_new[u]
            return carry

        lax.fori_loop(0, n_tiles // TILE_GROUP, tiles3, 0)
        online(_attn_tiles(S, DILATIONS[1])[2], functools.partial(_quarter_band_rows, S=S), 1, S // DILATIONS[1],
               q4, m3, l3, acc3)
        for r in range(4):
            nat, qtr = pl.ds(r, S // 4, stride=4), pl.ds(r * (S // 4), S // 4)
            m_a, m_b = m_run[nat, :], m3[qtr, :]
            m = jnp.maximum(m_a, m_b)
            w_a, w_b = jnp.exp(m_a - m), jnp.exp(m_b - m)
            l = w_a * l_run[nat, :] + w_b * l3[qtr, :]
            o_ref[nat, :] = (w_a * o_ref[nat, :] + w_b * acc3[qtr, :]) / l
            lse_ref[nat, :] = m + jnp.log(l)

    hspec = pl.BlockSpec((S, HEAD_DIM), lambda h: (0, h))
    padded, plain = pltpu.VMEM((S + 2 * KV_PAD, HEAD_DIM), F32), pltpu.VMEM((S, HEAD_DIM), F32)
    return pl.pallas_call(
        body, grid=(H,), name="attn_fwd",
        in_specs=_head_specs(S, (0, 1, 2), H) + [
            pl.BlockSpec((1, len(DILATIONS), Q_TILE, K_TILE), lambda h: (h, 0, 0, 0))],
        out_specs=[hspec, hspec],
        out_shape=[SDS((S, H * HEAD_DIM), F32), SDS((S, H * HEAD_DIM), F32)],
        scratch_shapes=[padded, padded] + [plain] * 6,
        compiler_params=_cp(1))(proj, proj, proj, bias)


def _put_groups(stage, dproj, groups, n_heads, sems):
    h = pl.program_id(0)
    copies = [pltpu.make_async_copy(
        stage.at[i], dproj.at[:, pl.ds(pl.multiple_of((g * n_heads + h) * HEAD_DIM, HEAD_DIM), HEAD_DIM)], sems.at[i])
        for i, g in enumerate(groups)]
    for cp in copies:
        cp.start()
    for cp in copies:
        cp.wait()


def _attn_bwd(proj, out, lse, dmix, bias, dproj):
    S = proj.shape[0]
    H = proj.shape[1] // (N_GROUPS * HEAD_DIM)
    scale = HEAD_DIM ** -0.5
    assert S // DILATIONS[2] >= 2 * Q_TILE

    def body(q_ref, k_ref, v_ref, o_ref, lse_ref, do_ref, b_ref, dproj_in, dproj_out,
             kp, vp, dkp, dvp, dsum, q4, do4, lse4, dsum4, dq_ref, dk_ref, dv_ref, stage, sems):
        _fill_padded(kp, k_ref, S)
        _fill_padded(vp, v_ref, S)
        dkp[...] = jnp.zeros_like(dkp)
        dvp[...] = jnp.zeros_like(dvp)
        dq_ref[...] = jnp.zeros_like(dq_ref)
        dsum[...] = jnp.broadcast_to(jnp.sum(do_ref[...] * o_ref[...], axis=-1, keepdims=True), dsum.shape)

        def run(n_tiles, tile_rows, p, L, q_src, do_src, lse_src, dsum_src, dq_dst, dq_adds):
            def tiles(t, carry):
                rows = [tile_rows(t + u * (n_tiles // BWD_TILE_GROUP)) for u in range(BWD_TILE_GROUP)]
                qs, ks, vs, dos, lses, dsums, dk_old, dv_old, edge = _stacked(rows, (
                    lambda a, qr, kr: q_src[qr, :].astype(BF16), lambda a, qr, kr: kp[kr, :].astype(BF16),
                    lambda a, qr, kr: vp[kr, :].astype(BF16), lambda a, qr, kr: do_src[qr, :].astype(BF16),
                    lambda a, qr, kr: lse_src[qr, :], lambda a, qr, kr: dsum_src[qr, :],
                    lambda a, qr, kr: dkp[kr, :], lambda a, qr, kr: dvp[kr, :], lambda a, qr, kr: _edge_mask(a, L)))
                s = _bdot(qs, ks, _BNT) * scale + b_ref[0, p][None] + edge
                pr = jnp.exp(s - _lanes(lses, K_TILE))
                ds = (pr * (_bdot(dos, vs, _BNT) - _lanes(dsums, K_TILE)) * scale).astype(BF16)
                dq_new = _bdot(ds, ks, _BNN)
                if dq_adds:
                    dq_new = dq_new + jnp.stack([dq_dst[qr, :] for _, qr, _ in rows])
                dk_new = dk_old + _bdot(ds, qs, _BTN)
                dv_new = dv_old + _bdot(pr.astype(BF16), dos, _BTN)
                for u, (_, qr, kr) in enumerate(rows):
                    dq_dst[qr, :] = dq_new[u]
                    dkp[kr, :] = dk_new[u]
                    dvp[kr, :] = dv_new[u]
                return carry

            lax.fori_loop(0, n_tiles // BWD_TILE_GROUP, tiles, 0)

        L, per_class, n_tiles = _attn_tiles(S, DILATIONS[0])
        run(n_tiles, functools.partial(_tile_rows, d=DILATIONS[0], per_class=per_class), 0, L,
            q_ref, do_ref, lse_ref, dsum, dq_ref, True)
        dk_ref[...] = dkp[pl.ds(KV_PAD, S), :]
        dv_ref[...] = dvp[pl.ds(KV_PAD, S), :]

        for dst, src in ((q4, q_ref), (do4, do_ref), (lse4, lse_ref), (dsum4, dsum)):
            _to_quarters(dst, src, S)
        _to_quarters(kp, k_ref, S, KV_PAD)
        _to_quarters(vp, v_ref, S, KV_PAD)
        dkp[...] = jnp.zeros_like(dkp)
        dvp[...] = jnp.zeros_like(dvp)
        dq3 = dsum
        run(_attn_tiles(S, DILATIONS[2])[2], functools.partial(_quarter_tile_rows, S=S), 2, S // DILATIONS[2],
            q4, do4, lse4, dsum4, dq3, False)
        run(_attn_tiles(S, DILATIONS[1])[2], functools.partial(_quarter_band_rows, S=S), 1, S // DILATIONS[1],
            q4, do4, lse4, dsum4, dq3, True)
        for r in range(4):
            nat, qtr = pl.ds(r, S // 4, stride=4), pl.ds(r * (S // 4), S // 4)
            pad_qtr = pl.ds(KV_PAD + r * (S // 4), S // 4)
            dq_ref[nat, :] = dq_ref[nat, :] + dq3[qtr, :]
            dk_ref[nat, :] = dk_ref[nat, :] + dkp[pad_qtr, :]
            dv_ref[nat, :] = dv_ref[nat, :] + dvp[pad_qtr, :]
        for i, acc in enumerate((dq_ref, dk_ref, dv_ref)):
            stage[i] = acc[...].astype(BF16)
        _put_groups(stage, dproj_out, (0, 1, 2), H, sems)

    hspec = pl.BlockSpec((S, HEAD_DIM), lambda h: (0, h))
    once = pl.BlockSpec((S, HEAD_DIM), lambda h: (0, h), pipeline_mode=pl.Buffered(1))
    padded, plain = pltpu.VMEM((S + 2 * KV_PAD, HEAD_DIM), F32), pltpu.VMEM((S, HEAD_DIM), F32)
    return pl.pallas_call(
        body, grid=(H,), name="attn_bwd",
        in_specs=_head_specs(S, (0, 1, 2), H) + [
            once, hspec, hspec, pl.BlockSpec((1, len(DILATIONS), Q_TILE, K_TILE), lambda h: (h, 0, 0, 0)), ANY],
        out_specs=ANY, out_shape=SDS(dproj.shape, dproj.dtype), input_output_aliases={7: 0},
        scratch_shapes=[padded] * 4 + [plain] * 8 + [pltpu.VMEM((3, S, HEAD_DIM), BF16), pltpu.SemaphoreType.DMA((3,))],
        compiler_params=_cp(1))(proj, proj, proj, out, lse, dmix, bias, dproj)


def _ret_consts(lg, forward):
    C = RET_CHUNK
    i = lax.broadcasted_iota(jnp.int32, (C, C), 0)
    j = lax.broadcasted_iota(jnp.int32, (C, C), 1)
    rel = (i - j) if forward else (j - i)
    inside = (rel >= 0) if forward else (rel > 0)
    relf = jnp.maximum(rel, 0).astype(F32)
    mask = jnp.where(inside, jnp.exp(lg * relf), 0.0)
    idx = lax.broadcasted_iota(jnp.int32, (C, 1), 0).astype(F32)
    q_exp = (idx + 1.0) if forward else (C - idx)
    k_exp = (C - 1.0 - idx) if forward else idx
    return mask, relf, jnp.exp(lg * q_exp), q_exp, jnp.exp(lg * k_exp), k_exp, jnp.exp(lg * C)


def _log_decay(dec_ref, h):
    return -jnp.exp(jnp.full((1, 1), dec_ref[0, h], F32))


FFN_BLOCK = 704
CHUNK_BATCH = 16


def _batch_rows(b):
    n = CHUNK_BATCH * RET_CHUNK
    return pl.ds(pl.multiple_of(b * n, n), n)


def _batch_chunks(b):
    return pl.ds(pl.multiple_of(b * CHUNK_BATCH, CHUNK_BATCH), CHUNK_BATCH)


def _chunks3(x):
    return x.reshape(CHUNK_BATCH, RET_CHUNK, HEAD_DIM)


def _ret_scan(buf, c_decs, nc, reverse):
    def step(n, carry):
        new = []
        for way, r in enumerate(carry):
            c = n if (way == 0) != reverse else nc - 1 - n
            term = buf[way, c]
            buf[way, c] = r
            new.append(r * c_decs[way] + term)
        return tuple(new)

    lax.fori_loop(0, nc, step, (jnp.zeros((HEAD_DIM, HEAD_DIM), F32),) * 2)


def _ret_fwd(proj, dec_f, dec_b, w_norm):
    S = proj.shape[0]
    H = proj.shape[1] // (N_GROUPS * HEAD_DIM)
    nc = S // RET_CHUNK
    scale = HEAD_DIM ** -0.5

    def body(df_ref, db_ref, q_ref, k_ref, v_ref, g_ref, w_ref, y_ref, o_ref, states):
        h = pl.program_id(0)
        consts = [_ret_consts(_log_decay(dref, h), fw) for fw, dref in ((True, df_ref), (False, db_ref))]

        def kv_step(b, carry):
            rows, batch = _batch_rows(b), _batch_chunks(b)
            k3 = _chunks3(k_ref[rows, :])
            v3 = _chunks3(v_ref[rows, :]).astype(BF16)
            for way in range(2):
                states[way, batch] = _bdot((k3 * consts[way][4]).astype(BF16), v3, _BTN)
            return carry

        lax.fori_loop(0, nc // CHUNK_BATCH, kv_step, 0)
        _ret_scan(states, [c[6] for c in consts], nc, False)

        def out_step(b, carry):
            rows, batch = _batch_rows(b), _batch_chunks(b)
            q3 = _chunks3(q_ref[rows, :] * scale)
            k3 = _chunks3(k_ref[rows, :]).astype(BF16)
            v3 = _chunks3(v_ref[rows, :]).astype(BF16)
            a0 = _bdot(q3.astype(BF16), k3, _BNT)
            o = None
            for way in range(2):
                mask, q_dec = consts[way][0], consts[way][2]
                part = _bdot((a0 * mask).astype(BF16), v3, _BNN) \
                    + _bdot((q3 * q_dec).astype(BF16), states[way, batch].astype(BF16), _BNN)
                o = part if o is None else o + part
            o_ref[rows, :] = o.reshape(CHUNK_BATCH * RET_CHUNK, HEAD_DIM)
            return carry

        lax.fori_loop(0, nc // CHUNK_BATCH, out_step, 0)
        o = o_ref[...]
        g = g_ref[...]
        y_ref[...] = o * _rms_scale(o) * w_ref[...] * (g * _sigmoid(g))

    hspec = pl.BlockSpec((S, HEAD_DIM), lambda h: (0, h))
    smem = pl.BlockSpec(memory_space=pltpu.SMEM)
    return pl.pallas_call(
        body, grid=(H,), name="ret_fwd",
        in_specs=[smem, smem] + _head_specs(S, (3, 4, 5, 6), H) + [pl.BlockSpec((1, HEAD_DIM), lambda h: (0, h))],
        out_specs=[hspec, hspec],
        out_shape=[SDS((S, H * HEAD_DIM), F32)] * 2,
        scratch_shapes=[pltpu.VMEM((2, nc, HEAD_DIM, HEAD_DIM), F32)],
        compiler_params=_cp(1))(dec_f, dec_b, proj, proj, proj, proj, w_norm)


def _ret_gate_bwd(proj, o_raw, dmix, w_norm, col0, dproj):
    S = proj.shape[0]
    H = proj.shape[1] // (N_GROUPS * HEAD_DIM)

    def body(g_ref, o_ref, dy_ref, w_ref, dproj_in, do_ref, dw_ref, dproj_out, dg_ref, sems):
        o = o_ref[...]
        g = g_ref[...]
        dy = dy_ref[...]
        w = w_ref[...]
        rr = _rms_scale(o)
        normed = o * rr
        sg = _sigmoid(g)
        silu = g * sg
        dw_ref[0] = jnp.broadcast_to(jnp.sum(dy * normed * silu, axis=0, keepdims=True), (8, HEAD_DIM))
        dg_ref[0] = (dy * normed * w * (sg * (1.0 + g * (1.0 - sg)))).astype(BF16)
        dnormed = dy * w * silu
        do_ref[...] = rr * dnormed - o * (rr * rr * rr) * jnp.mean(dnormed * o, axis=-1, keepdims=True)
        _put_groups(dg_ref, dproj_out, (6,), H, sems)

    hspec = pl.BlockSpec((S, HEAD_DIM), lambda h: (0, h))
    nh0 = col0 // HEAD_DIM
    return pl.pallas_call(
        body, grid=(H,), name="ret_gate_bwd",
        in_specs=_head_specs(S, (6,), H) + [hspec, pl.BlockSpec((S, HEAD_DIM), lambda h: (0, nh0 + h)),
                                            pl.BlockSpec((1, HEAD_DIM), lambda h: (0, h)), ANY],
        out_specs=[hspec, pl.BlockSpec((1, 8, HEAD_DIM), lambda h: (h, 0, 0)), ANY],
        out_shape=[SDS((S, H * HEAD_DIM), F32), SDS((H, 8, HEAD_DIM), F32), SDS(dproj.shape, dproj.dtype)],
        input_output_aliases={4: 2},
        scratch_shapes=[pltpu.VMEM((1, S, HEAD_DIM), BF16), pltpu.SemaphoreType.DMA((1,))],
        compiler_params=_cp(1))(proj, o_raw, dmix, w_norm, dproj)


def _ret_bwd(proj, d_out, dec_f, dec_b, dproj):
    S = proj.shape[0]
    H = proj.shape[1] // (N_GROUPS * HEAD_DIM)
    C = RET_CHUNK
    nc = S // C
    scale = HEAD_DIM ** -0.5

    def body(df_ref, db_ref, q_ref, k_ref, v_ref, do, dproj_in, small_ref, dproj_out, states, d_states, stage, sems):
        h = pl.program_id(0)
        lgs = [_log_decay(df_ref, h), _log_decay(db_ref, h)]
        consts = [_ret_consts(lg, fw) for lg, fw in zip(lgs, (True, False))]

        def prep_step(b, carry):
            rows, batch = _batch_rows(b), _batch_chunks(b)
            q3 = _chunks3(q_ref[rows, :] * scale)
            k3 = _chunks3(k_ref[rows, :])
            v3 = _chunks3(v_ref[rows, :]).astype(BF16)
            do3 = _chunks3(do[rows, :]).astype(BF16)
            for way in range(2):
                states[way, batch] = _bdot((k3 * consts[way][4]).astype(BF16), v3, _BTN)
                d_states[way, batch] = _bdot((q3 * consts[way][2]).astype(BF16), do3, _BTN)
            return carry

        lax.fori_loop(0, nc // CHUNK_BATCH, prep_step, 0)
        c_decs = [c[6] for c in consts]
        _ret_scan(states, c_decs, nc, False)
        _ret_scan(d_states, c_decs, nc, True)

        def main_step(b, dlams):
            rows, batch = _batch_rows(b), _batch_chunks(b)
            q3 = _chunks3(q_ref[rows, :] * scale)
            k3 = _chunks3(k_ref[rows, :])
            q3b, k3b = q3.astype(BF16), k3.astype(BF16)
            v3b = _chunks3(v_ref[rows, :]).astype(BF16)
            do3b = _chunks3(do[rows, :]).astype(BF16)
            a0 = _bdot(q3b, k3b, _BNT)
            pv = _bdot(do3b, v3b, _BNT)
            dq = dk = dv = None
            new_dlams = []
            for way in range(2):
                mask, relf, q_dec, q_exp, k_dec, k_exp, c_dec = consts[way]
                state, d_state = states[way, batch], d_states[way, batch]
                dp = pv * mask
                dpb = dp.astype(BF16)
                gq = _bdot(do3b, state.astype(BF16), _BNT)
                gk = _bdot(v3b, d_state.astype(BF16), _BNT)
                parts = (_bdot(dpb, k3b, _BNN) + q_dec * gq, _bdot(dpb, q3b, _BTN) + k_dec * gk,
                         _bdot((a0 * mask).astype(BF16), do3b, _BTN)
                         + _bdot((k3 * k_dec).astype(BF16), d_state.astype(BF16), _BNN))
                dq, dk, dv = parts if dq is None else (dq + parts[0], dk + parts[1], dv + parts[2])
                total = lambda x: jnp.sum(jnp.sum(x, axis=0), axis=0, keepdims=True)
                new_dlams.append(dlams[way] + total(relf * a0 * dp)
                                 + total(q_exp * q_dec * q3 * gq + k_exp * k_dec * k3 * gk)
                                 + (C * c_dec) * total(state * d_state))
            flat = lambda x: x.reshape(CHUNK_BATCH * C, HEAD_DIM)
            stage[0, rows, :] = (flat(dq) * scale).astype(BF16)
            stage[1, rows, :] = flat(dk).astype(BF16)
            stage[2, rows, :] = flat(dv).astype(BF16)
            return tuple(new_dlams)

        dlams = lax.fori_loop(0, nc // CHUNK_BATCH, main_step, (jnp.zeros((1, HEAD_DIM), F32),) * 2)
        for row, (dlam, lg) in enumerate(zip(dlams, lgs)):
            small_ref[0, pl.ds(row, 1), :] = jnp.broadcast_to(jnp.sum(dlam, axis=-1, keepdims=True) * lg, (1, HEAD_DIM))
        small_ref[0, pl.ds(2, 6), :] = jnp.zeros((6, HEAD_DIM), F32)
        _put_groups(stage, dproj_out, (3, 4, 5), H, sems)

    hspec = pl.BlockSpec((S, HEAD_DIM), lambda h: (0, h))
    smem = pl.BlockSpec(memory_space=pltpu.SMEM)
    return pl.pallas_call(
        body, grid=(H,), name="ret_bwd",
        in_specs=[smem, smem] + _head_specs(S, (3, 4, 5), H) + [hspec, ANY],
        out_specs=[pl.BlockSpec((1, 8, HEAD_DIM), lambda h: (h, 0, 0)), ANY],
        out_shape=[SDS((H, 8, HEAD_DIM), F32), SDS(dproj.shape, dproj.dtype)], input_output_aliases={6: 1},
        scratch_shapes=[pltpu.VMEM((2, nc, HEAD_DIM, HEAD_DIM), F32), pltpu.VMEM((2, nc, HEAD_DIM, HEAD_DIM), F32),
                        pltpu.VMEM((3, S, HEAD_DIM), BF16), pltpu.SemaphoreType.DMA((3,))],
        compiler_params=_cp(1))(dec_f, dec_b, proj, proj, proj, d_out, dproj)


def _ffn_bwd_act(dh2, wd, g, u):
    S, D = dh2.shape
    nblk, _, FB = g.shape
    tm = min(1024, S)

    def body(dh_ref, wd_ref, g_ref, u_ref, dg_ref, du_ref):
        dact = _dot(dh_ref[...], wd_ref[...], _NT)
        gg = g_ref[0].astype(F32)
        sg = _sigmoid(gg)
        dg_ref[0] = (dact * u_ref[0].astype(F32) * (sg * (1.0 + gg * (1.0 - sg)))).astype(BF16)
        du_ref[0] = (dact * (gg * sg)).astype(BF16)

    blk = pl.BlockSpec((1, tm, FB), lambda j, i: (j, i, 0))
    return pl.pallas_call(
        body, grid=(nblk, S // tm), name="ffn_bwd_act",
        in_specs=[pl.BlockSpec((tm, D), lambda j, i: (i, 0)), pl.BlockSpec((FB, D), lambda j, i: (j, 0)), blk, blk],
        out_specs=[blk, blk], out_shape=[SDS((nblk, S, FB), BF16)] * 2,
        compiler_params=_cp(2))(dh2, wd, g, u)


def _ffn_bwd_in(dg, du, wg, wu, h1, dh2, w_norm):
    nblk, S, FB = dg.shape
    D = h1.shape[1]
    tm = min(RESIDENT_ROWS, S)
    blk = pl.BlockSpec((nblk, tm, FB), lambda i: (0, i, 0))
    row = pl.BlockSpec((tm, D), lambda i: (i, 0))
    vec = pl.BlockSpec((1, D), lambda i: (0, 0))

    def gate_body(dg_ref, wg_ref, part_ref):
        part_ref[...] = _blocked_matmul(dg_ref, wg_ref)

    part = pl.pallas_call(
        gate_body, grid=(S // tm,), name="ffn_bwd_in_gate", in_specs=[blk, _resident((nblk * FB, D))],
        out_specs=row, out_shape=SDS((S, D), F32), compiler_params=_cp(1))(dg, wg.reshape(nblk * FB, D))

    def body(du_ref, wu_ref, part_ref, h_ref, dh2_ref, wn_ref, dh_ref, dhb_ref, dw_ref):
        @pl.when(pl.program_id(0) == 0)
        def _():
            dw_ref[...] = jnp.zeros_like(dw_ref)

        dh, dw = _rms_bwd(part_ref[...] + _blocked_matmul(du_ref, wu_ref), h_ref[...], wn_ref[...])
        dh = dh2_ref[...] + dh
        dh_ref[...] = dh
        dhb_ref[...] = dh.astype(BF16)
        dw_ref[...] += dw

    return pl.pallas_call(
        body, grid=(S // tm,), name="ffn_bwd_in",
        in_specs=[blk, _resident((nblk * FB, D)), row, row, row, vec],
        out_specs=[row, row, vec], out_shape=[SDS((S, D), F32), SDS((S, D), BF16), SDS((1, D), F32)],
        compiler_params=_cp(1))(du, wu.reshape(nblk * FB, D), part, h1, dh2, w_norm)


def _dmix(dh1, w_out):
    S, D = dh1.shape
    tm = min(512, S)

    def body(dh_ref, w_ref, o_ref):
        o_ref[...] = _dot(dh_ref[...], w_ref[...], _NT)

    row = pl.BlockSpec((tm, D), lambda i: (i, 0))
    return pl.pallas_call(
        body, grid=(S // tm,), name="dmix", in_specs=[row, pl.BlockSpec((D, D), lambda i: (0, 0))],
        out_specs=row, out_shape=SDS((S, D), F32), compiler_params=_cp(1))(dh1, w_out)


def _in_bwd(dproj, w_blk, x, dh1, w_norm):
    S, D = x.shape
    nblk, _, NB = w_blk.shape
    tm = min(RESIDENT_ROWS, S)

    def body(dp_ref, w_ref, x_ref, dh1_ref, wn_ref, dx_ref, dw_ref):
        @pl.when(pl.program_id(0) == 0)
        def _():
            dw_ref[...] = jnp.zeros_like(dw_ref)

        dn = None
        for j in range(nblk):
            part = _dot(dp_ref[:, pl.ds(j * NB, NB)], w_ref[j], _NT)
            dn = part if dn is None else dn + part
        dh, dw = _rms_bwd(dn, x_ref[...], wn_ref[...])
        dx_ref[...] = dh1_ref[...] + dh
        dw_ref[...] += dw

    row = pl.BlockSpec((tm, D), lambda i: (i, 0))
    vec = pl.BlockSpec((1, D), lambda i: (0, 0))
    return pl.pallas_call(
        body, grid=(S // tm,), name="in_bwd",
        in_specs=[pl.BlockSpec((tm, nblk * NB), lambda i: (i, 0)),
                  pl.BlockSpec((nblk, D, NB), lambda i: (0, 0, 0), pipeline_mode=pl.Buffered(1)), row, row, vec],
        out_specs=[row, vec], out_shape=[SDS((S, D), F32), SDS((1, D), F32)],
        compiler_params=_cp(1))(dproj, w_blk, x, dh1, w_norm)


def _wgrad(a, b, a_spec, b_spec, o_spec, o_shape, grid, name):
    nk = grid[-1]

    def ld(ref):
        return ref[0] if len(ref.shape) == 3 else ref[...]

    def body(a_ref, b_ref, o_ref, acc):
        k = pl.program_id(len(grid) - 1)

        @pl.when(k == 0)
        def _():
            acc[...] = jnp.zeros_like(acc)

        acc[...] += _dot(ld(a_ref), ld(b_ref), _TN)

        @pl.when(k == nk - 1)
        def _():
            if len(o_ref.shape) == 3:
                o_ref[0] = acc[...].astype(o_ref.dtype)
            else:
                o_ref[...] = acc[...].astype(o_ref.dtype)

    return pl.pallas_call(
        body, grid=grid, name=name, in_specs=[a_spec, b_spec], out_specs=o_spec, out_shape=SDS(o_shape, BF16),
        scratch_shapes=[pltpu.VMEM(o_spec.block_shape[-2:], F32)], compiler_params=_cp(len(grid)))(a, b)


def _peer(k):
    x, y, c = lax.axis_index("x"), lax.axis_index("y"), lax.axis_index("c")
    px = 1 - x if k & 4 else x
    py = 1 - y if k & 2 else y
    pc = 1 - c if k & 1 else c
    return (px, py, pc), 4 * px + 2 * py + pc


def _exchange_copies(srcs, lands, send_sems, recv_sems, which, gather):
    _, me = _peer(0)
    pairs = []
    for pos, a in enumerate(which):
        for k in range(1, N_DEV):
            dev, idx = _peer(k)
            sem = pos * (N_DEV - 1) + k - 1
            src = srcs[a] if gather else srcs[a].at[idx]
            mk = functools.partial(pltpu.make_async_remote_copy, src_ref=src, send_sem=send_sems.at[sem],
                                   recv_sem=recv_sems.at[sem], device_id=dev, device_id_type=MESH)
            pairs.append((mk(dst_ref=lands[a].at[me]), mk(dst_ref=lands[a].at[idx])))
    return pairs


def _sequencer_kernel(name, collective_id, n_remote, n_local):
    return pl.kernel(mesh=plsc.ScalarSubcoreMesh(axis_name="sequencer", num_cores=1), name=name,
                     scratch_types=(pltpu.SemaphoreType.DMA((n_remote,)), pltpu.SemaphoreType.DMA((n_remote,)),
                                    pltpu.SemaphoreType.DMA((n_local,))),
                     compiler_params=pltpu.CompilerParams(collective_id=collective_id))


def _handshake(ks):
    barrier = pltpu.get_barrier_semaphore()
    for k in ks:
        pl.semaphore_signal(barrier, inc=1, device_id=_peer(k)[0], device_id_type=MESH)
    pl.semaphore_wait(barrier, len(ks))


def _sequencer_scatter(arrays, name, collective_id):
    n = len(arrays)
    hbm = pltpu.MemorySpace.HBM
    srcs = [jax.new_ref(a, memory_space=hbm) for a in arrays]
    lands = [jax.empty_ref(SDS(a.shape, a.dtype), memory_space=hbm) for a in arrays]

    @_sequencer_kernel(name, collective_id, n * (N_DEV - 1), n)
    def launch(send_sems, recv_sems, local_sems):
        _handshake(range(1, N_DEV))
        _, me = _peer(0)
        local = [pltpu.make_async_copy(srcs[a].at[me], lands[a].at[me], local_sems.at[a]) for a in range(n)]
        pairs = _exchange_copies(srcs, lands, send_sems, recv_sems, range(n), False)
        for out, _ in pairs:
            out.start()
        for cp in local:
            cp.start()
        for out, arrival in pairs:
            out.wait_send()
            arrival.wait_recv()
        for cp in local:
            cp.wait()

    launch()
    return [r[...] for r in lands]


SIBLING = 1
OTHER_CHIPS = (2, 4, 6)


def _sequencer_gather(arrays, name, collective_id):
    n = len(arrays)
    hbm = pltpu.MemorySpace.HBM
    srcs = [jax.new_ref(a, memory_space=hbm) for a in arrays]
    lands = [jax.empty_ref(SDS((N_DEV,) + a.shape, a.dtype), memory_space=hbm) for a in arrays]

    @_sequencer_kernel(name, collective_id, n * (N_DEV - 1), n)
    def launch(send_sems, recv_sems, local_sems):
        _handshake((SIBLING,) + OTHER_CHIPS)
        _, me = _peer(0)
        sibling, _ = _peer(SIBLING)

        def copy(a, k, src, block, to):
            sem = a * (N_DEV - 1) + k - 1
            return pltpu.make_async_remote_copy(src_ref=src, dst_ref=lands[a].at[block], send_sem=send_sems.at[sem],
                                                recv_sem=recv_sems.at[sem], device_id=to, device_id_type=MESH)

        local = [pltpu.make_async_copy(srcs[a], lands[a].at[me], local_sems.at[a]) for a in range(n)]
        first = [copy(a, k, srcs[a], me, _peer(k)[0]) for a in range(n) for k in OTHER_CHIPS + (SIBLING,)]
        for cp in first + local:
            cp.start()
        passed = []
        for a in range(n):
            for k in OTHER_CHIPS:
                _, block = _peer(k)
                copy(a, k, srcs[a], block, sibling).wait_recv()
                passed.append(copy(a, k ^ SIBLING, lands[a].at[block], block, sibling))
                passed[-1].start()
        for a in range(n):
            for k in (SIBLING,) + tuple(k ^ SIBLING for k in OTHER_CHIPS):
                copy(a, k, srcs[a], _peer(k)[1], sibling).wait_recv()
        for cp in first + passed:
            cp.wait_send()
        for cp in local:
            cp.wait()

    launch()
    return [r[...] for r in lands]


def _sequencer_gather_chips(array, name, collective_id, chips):
    hbm = pltpu.MemorySpace.HBM
    src = jax.new_ref(array, memory_space=hbm)
    land = jax.empty_ref(SDS((2 * len(chips),) + array.shape, array.dtype), memory_space=hbm)

    @_sequencer_kernel(name, collective_id, 2 * len(chips), 1)
    def launch(send_sems, recv_sems, local_sems):
        _handshake((SIBLING,) + tuple(k for k in chips if k))
        c = lax.axis_index("c")
        sibling, _ = _peer(SIBLING)

        def copy(sem, src_ref, slot, to):
            return pltpu.make_async_remote_copy(src_ref=src_ref, dst_ref=land.at[slot], send_sem=send_sems.at[sem],
                                                recv_sem=recv_sems.at[sem], device_id=to, device_id_type=MESH)

        started = []
        for pos, k in enumerate(chips):
            started.append(copy(2 * pos, src, 2 * pos + c, _peer(k)[0] if k else sibling))
            started[-1].start()
        for pos, k in enumerate(chips):
            if k:
                copy(2 * pos, src, 2 * pos + c, sibling).wait_recv()
                started.append(copy(2 * pos + 1, land.at[2 * pos + c], 2 * pos + c, sibling))
                started[-1].start()
        for pos, k in enumerate(chips):
            copy(2 * pos + 1 if k else 2 * pos, src, 2 * pos + 1 - c, sibling).wait_recv()
        for cp in started:
            cp.wait_send()

    launch()
    return land[...]


def _sequencer_relay_far(near, name, collective_id):
    hbm = pltpu.MemorySpace.HBM
    src = jax.new_ref(near, memory_space=hbm)
    land = jax.empty_ref(SDS((2,) + near.shape[1:], near.dtype), memory_space=hbm)
    half = near.shape[1] // 2

    @_sequencer_kernel(name, collective_id, 3, 1)
    def launch(send_sems, recv_sems, local_sems):
        _handshake((SIBLING, 4, 2))
        c = lax.axis_index("c")
        sibling, _ = _peer(SIBLING)

        def copy(sem, src_ref, dst_ref, to):
            return pltpu.make_async_remote_copy(src_ref=src_ref, dst_ref=dst_ref, send_sem=send_sems.at[sem],
                                                recv_sem=recv_sems.at[sem], device_id=to, device_id_type=MESH)

        upper, lower = pl.ds(0, half), pl.ds(half, half)
        halves = [copy(0, src.at[c, upper, :], land.at[c, upper, :], _peer(2)[0]),
                  copy(1, src.at[2 + c, lower, :], land.at[c, lower, :], _peer(4)[0])]
        for cp in halves:
            cp.start()
        for cp in halves:
            cp.wait_recv()
        passed = copy(2, land.at[c], land.at[c], sibling)
        passed.start()
        copy(2, land.at[1 - c], land.at[1 - c], sibling).wait_recv()
        for cp in halves + [passed]:
            cp.wait_send()

    launch()
    return land[...]


SMALL_ROWS = 64


def _small_step(part, w, m, v):
    def body(p_ref, w_ref, m_ref, v_ref, g_ref, d_ref, nm_ref, nv_ref, gath, send_sems, recv_sems):
        _, me = _peer(0)
        gath[me] = p_ref[...]
        copies = []
        for k in range(1, N_DEV):
            dev, idx = _peer(k)
            out = pltpu.make_async_remote_copy(src_ref=p_ref, dst_ref=gath.at[me], send_sem=send_sems.at[k - 1],
                                               recv_sem=recv_sems.at[k - 1], device_id=dev, device_id_type=MESH)
            out.start()
            arrival = pltpu.make_async_remote_copy(src_ref=p_ref, dst_ref=gath.at[idx], send_sem=send_sems.at[k - 1],
                                                   recv_sem=recv_sems.at[k - 1], device_id=dev, device_id_type=MESH)
            copies.append((out, arrival))
        for out, arrival in copies:
            out.wait_send()
            arrival.wait_recv()
        g = gath[0]
        for p in range(1, N_DEV):
            g = g + gath[p]
        g_ref[...] = g
        d_ref[...], nm_ref[...], nv_ref[...] = _adamw(w_ref[...], g, m_ref[...], v_ref[...])

    vm = pl.BlockSpec(memory_space=pltpu.VMEM)
    return pl.pallas_call(
        body, name="small_step", in_specs=[vm] * 4, out_specs=[vm] * 4,
        out_shape=[SDS((SMALL_ROWS, 128), F32)] * 4,
        scratch_shapes=[pltpu.VMEM((N_DEV, SMALL_ROWS, 128), F32), pltpu.SemaphoreType.DMA((N_DEV - 1,)),
                        pltpu.SemaphoreType.DMA((N_DEV - 1,))])(part, w, m, v)


def _adamw(w, g, m, v):
    m = ADAM_B1 * m + (1.0 - ADAM_B1) * g
    v = ADAM_B2 * v + (1.0 - ADAM_B2) * (g * g)
    m_hat = m / (1.0 - ADAM_B1 ** ADAM_STEP)
    v_hat = v / (1.0 - ADAM_B2 ** ADAM_STEP)
    delta = -ADAM_LR * (m_hat / (jnp.sqrt(v_hat) + ADAM_EPS) + ADAM_WD * w)
    return delta, m, v


def _adamw_block(parts, w, m, v, name):
    R, C = w.shape
    n_parts = len(parts)
    Rp = R // n_parts
    tr = next(t for t in (256, 128, 64, 32, 16, 8) if Rp % t == 0 and t * C <= 256 * 1024)
    per_part = Rp // tr

    def body(*refs):
        p_refs = refs[:n_parts]
        w_ref, m_ref, v_ref, g_ref, d_ref, nm_ref, nv_ref = refs[n_parts:]
        for k, p_ref in enumerate(p_refs):
            @pl.when(pl.program_id(0) // per_part == k)
            def _(p_ref=p_ref):
                g = p_ref[0].astype(F32)
                for p in range(1, N_DEV):
                    g = g + p_ref[p].astype(F32)
                g_ref[...] = g
                d_ref[...], nm_ref[...], nv_ref[...] = _adamw(w_ref[...], g, m_ref[...], v_ref[...])

    row = pl.BlockSpec((tr, C), lambda i: (i, 0))
    part_specs = [pl.BlockSpec((N_DEV, tr, C), functools.partial(
        lambda i, k: (0, jnp.clip(i - k * per_part, 0, per_part - 1), 0), k=k)) for k in range(n_parts)]
    return pl.pallas_call(
        body, grid=(R // tr,), name=name, in_specs=part_specs + [row, row, row],
        out_specs=[row] * 4, out_shape=[SDS((R, C), F32)] * 4, compiler_params=_cp(1))(*parts, w, m, v)


def _pack_small(mix, ffn, fin, retw, dec_f, dec_b, loss):
    flat = jnp.concatenate([mix.reshape(-1), ffn.reshape(-1), fin.reshape(-1), retw.reshape(-1), dec_f.reshape(-1),
                            dec_b.reshape(-1), loss.reshape(-1)])
    return jnp.pad(flat, (0, SMALL_ROWS * 128 - flat.shape[0])).reshape(SMALL_ROWS, 128)


def _unpack_small(packed, shapes):
    flat = packed.reshape(-1)
    out, at = [], 0
    for s in shapes:
        n = math.prod(s)
        out.append(flat[at:at + n].reshape(s))
        at += n
    return out


def kernel(x, norm_mix_w, w_in, ret_decay_fwd, ret_decay_bwd, ret_norm_w, w_out, norm_ffn_w, w_gate, w_up, w_down, norm_final_w, loss_target, m_norm_mix_w, m_w_in, m_ret_decay_fwd, m_ret_decay_bwd, m_ret_norm_w, m_w_out, m_norm_ffn_w, m_w_gate, m_w_up, m_w_down, m_norm_final_w, v_norm_mix_w, v_w_in, v_ret_decay_fwd, v_ret_decay_bwd, v_ret_norm_w, v_w_out, v_norm_ffn_w, v_w_gate, v_w_up, v_w_down, v_norm_final_w):
    x2 = x[0]
    tgt = loss_target[0]
    S, D = x2.shape
    H = ret_norm_w.shape[1] // HEAD_DIM
    DA = H * HEAD_DIM
    fin_w = norm_final_w.reshape(1, D)
    big = (w_in[0], w_out[0], w_gate[0].T, w_up[0].T, w_down[0])

    big_b = [w.astype(BF16) for w in big]
    stages = ((0,), (4, 2), (6,))
    wi_stages = [_sequencer_gather_chips(big_b[0], name, cid, ks)
                 for name, cid, ks in zip(("gather_in_own", "gather_in_near"), (0, 7), stages)]
    wi_stages.append(_sequencer_relay_far(wi_stages[1], "gather_in_far", 8))
    wo, = _sequencer_gather(big_b[1:2], "gather_out", 1)
    wg, wu = _sequencer_gather(big_b[2:4], "gather_gate_up", 9)
    wd, = _sequencer_gather(big_b[4:], "gather_down", 5)
    wi, = _sequencer_gather(big_b[:1], "gather_in_ordered", 10)
    NB = big_b[0].shape[1]
    ax, ay = lax.axis_index("x"), lax.axis_index("y")
    chip_of = {k: 2 * (1 - ax if k & 4 else ax) + (1 - ay if k & 2 else ay) for k in (0, 2, 4, 6)}

    n1 = _norm_fwd(x2, norm_mix_w)
    ac = lax.axis_index("c")
    me = 2 * chip_of[0] + ac
    vec = lambda *v: jnp.stack([jnp.asarray(t, jnp.int32) for t in v])
    proj = _proj_part(n1, big_b[0][None], vec(0), vec(me), None, N_DEV, "proj_self")
    for ks, w_st, name in zip(stages, wi_stages, ("proj_sibling", "proj_near", "proj_far")):
        slots, blocks = [], []
        for pos, k in enumerate(ks):
            for core in ((1 - ac,) if k == 0 else (0, 1)):
                slots.append(2 * pos + core)
                blocks.append(2 * chip_of[k] + core)
        proj = _proj_part(n1, w_st, vec(*slots), vec(*blocks), proj, N_DEV, name)
    bias = _attn_bias()[:H]
    attn, lse = _attn_fwd(proj, bias)
    ret, o_raw = _ret_fwd(proj, ret_decay_fwd, ret_decay_bwd, ret_norm_w)
    wo_full = wo.reshape(D, D)
    d_ff = N_DEV * wd.shape[1]
    FB = FFN_BLOCK if d_ff % FFN_BLOCK == 0 else wd.shape[1]
    n_fb = d_ff // FB
    wg, wu = wg.reshape(n_fb, FB, D), wu.reshape(n_fb, FB, D)
    wd_full = wd.reshape(d_ff, D)
    h1, mixed, n2 = _out_fwd(x2, attn, ret, wo_full, norm_ffn_w)
    gate, up, act = _ffn_up(n2, wg, wu)
    dh2, dh2_b, loss_parts, g_fin = _ffn_down_loss(act, wd_full, h1, tgt, fin_w)

    dgate, dup = _ffn_bwd_act(dh2_b, wd_full, gate, up)
    tn = min(1024, D)
    ffn_specs = (pl.BlockSpec((1, S, FB), lambda j, n, k: (j, 0, 0)), pl.BlockSpec((S, tn), lambda j, n, k: (0, n)),
                 pl.BlockSpec((1, FB, tn), lambda j, n, k: (j, 0, n)), (n_fb, FB, D), (n_fb, D // tn, 1))
    per_dev = (N_DEV, d_ff // N_DEV, D)
    g_wd = _wgrad(act, dh2_b, *ffn_specs, "wgrad_down").reshape(per_dev)
    g_wg = _wgrad(dgate, n2, *ffn_specs, "wgrad_gate").reshape(per_dev)
    g_wu = _wgrad(dup, n2, *ffn_specs, "wgrad_up").reshape(per_dev)
    parts_f = _sequencer_scatter([g_wg, g_wu, g_wd], "scatter_ffn", 2)
    dh1, dh1_b, g_ffn = _ffn_bwd_in(dgate, dup, wg, wu, h1, dh2, norm_ffn_w)
    dmix = _dmix(dh1_b, wo_full)
    tmw = min(512, D)
    tk = min(2048, S)
    g_wo = _wgrad(mixed, dh1_b, pl.BlockSpec((tk, tmw), lambda m, k: (k, m)), pl.BlockSpec((tk, D), lambda m, k: (k, 0)),
                  pl.BlockSpec((tmw, D), lambda m, k: (m, 0)), (D, D), (D // tmw, S // tk), "wgrad_out")
    parts_o = _sequencer_scatter([g_wo.reshape(N_DEV, D // N_DEV, D)], "scatter_out", 3)
    d_ret, small_w, dproj = _ret_gate_bwd(proj, o_raw, dmix, ret_norm_w, DA, lax.empty(proj.shape, BF16))
    small, dproj = _ret_bwd(proj, d_ret, ret_decay_fwd, ret_decay_bwd, dproj)
    dproj = _attn_bwd(proj, attn, lse, dmix, bias, dproj)
    half = D // tmw // 2
    parts_i = []
    for part, (name, cid) in enumerate((("in_lo", 4), ("in_hi", 6))):
        g_wi = _wgrad(n1, dproj, pl.BlockSpec((S, tmw), functools.partial(lambda j, m, k, off: (0, m + off), off=part * half)),
                      pl.BlockSpec((S, NB), lambda j, m, k: (0, j)), pl.BlockSpec((1, tmw, NB), lambda j, m, k: (j, m, 0)),
                      (N_DEV, D // 2, NB), (N_DEV, half, 1), "wgrad_" + name)
        parts_i += _sequencer_scatter([g_wi], "scatter_" + name, cid)
    grad_x, g_mix = _in_bwd(dproj, wi, x2, dh1, norm_mix_w)

    big_m = (m_w_in[0], m_w_out[0], m_w_gate[0].T, m_w_up[0].T, m_w_down[0])
    big_v = (v_w_in[0], v_w_out[0], v_w_gate[0].T, v_w_up[0].T, v_w_down[0])
    names = ("adamw_in", "adamw_out", "adamw_gate", "adamw_up", "adamw_down")
    upd = [None] * 5
    for a, p in zip((2, 3, 4, 1, 0), [[t] for t in parts_f + parts_o] + [parts_i]):
        upd[a] = _adamw_block(p, big[a], big_m[a], big_v[a], names[a])

    g_dec_f = small[:, 0, 0].reshape(1, H)
    g_dec_b = small[:, 1, 0].reshape(1, H)
    g_retw = small_w[:, 0, :].reshape(1, DA)
    loss_local = jnp.sum(loss_parts[::8, 0])
    zero = jnp.zeros((1,), F32)
    part = _pack_small(g_mix, g_ffn, g_fin, g_retw, g_dec_f, g_dec_b, loss_local)
    sw = _pack_small(norm_mix_w, norm_ffn_w, norm_final_w, ret_norm_w, ret_decay_fwd, ret_decay_bwd, zero)
    sm = _pack_small(m_norm_mix_w, m_norm_ffn_w, m_norm_final_w, m_ret_norm_w, m_ret_decay_fwd, m_ret_decay_bwd, zero)
    sv = _pack_small(v_norm_mix_w, v_norm_ffn_w, v_norm_final_w, v_ret_norm_w, v_ret_decay_fwd, v_ret_decay_bwd, zero)
    shapes = [(1, D), (1, D), (D,), (1, DA), (1, H), (1, H), ()]
    sg, sd, snm, snv = [_unpack_small(t, shapes) for t in _small_step(part, sw, sm, sv)]
    loss = sg[6]

    def ordered(small_set, k):
        b = [(u[k].T if a in (2, 3) else u[k])[None] for a, u in enumerate(upd)]
        return [small_set[0], b[0], small_set[4], small_set[5], small_set[3], b[1], small_set[1], b[2], b[3], b[4],
                small_set[2]]

    return (loss, grad_x[None], *ordered(sg, 0), *ordered(sd, 1), *ordered(snm, 2), *ordered(snv, 3))
```

```python
import functools
import math

import numpy as np
import jax
import jax.numpy as jnp
from jax import lax
from jax.experimental import pallas as pl
from jax.experimental.pallas import tpu as pltpu
from jax.experimental.pallas import tpu_sc as plsc

F32 = jnp.float32
BF16 = jnp.bfloat16
SDS = jax.ShapeDtypeStruct

HEAD_DIM = 128
EPS = 1e-6
RET_CHUNK = 128
DILATIONS = (1, 4, 16)
BAND = 64
Q_TILE = 128
K_TILE = Q_TILE + 2 * BAND
KV_PAD = BAND * 4
TILE_GROUP = 8
BWD_TILE_GROUP = 8
NEG = -1e30
N_DEV = 8
N_GROUPS = 7
ADAM_LR, ADAM_B1, ADAM_B2, ADAM_EPS, ADAM_WD, ADAM_STEP = 0.001, 0.9, 0.999, 1e-08, 0.01, 10
VMEM_LIMIT = 56 * 1024 * 1024
MESH = pl.DeviceIdType.MESH
ANY = pl.BlockSpec(memory_space=pl.ANY)


def _cp(n_grid):
    return pltpu.CompilerParams(dimension_semantics=("arbitrary",) * n_grid, vmem_limit_bytes=VMEM_LIMIT)


def _sigmoid(x):
    return 1.0 / (1.0 + jnp.exp(-x))


def _rms_scale(h):
    return lax.rsqrt(jnp.mean(h * h, axis=-1, keepdims=True) + EPS)


def _rms_bwd(dn, h, w):
    r = _rms_scale(h)
    gw = dn * w
    dh = r * gw - h * (r * r * r) * jnp.mean(gw * h, axis=-1, keepdims=True)
    return dh, jnp.sum(dn * h * r, axis=0, keepdims=True)


def _dot(a, b, dims):
    return lax.dot_general(a.astype(BF16), b.astype(BF16), (dims, ((), ())), preferred_element_type=F32)


_NN = ((1,), (0,))
_NT = ((1,), (1,))
_TN = ((0,), (0,))


RESIDENT_ROWS = 256


def _resident(shape):
    return pl.BlockSpec(shape, lambda i: (0, 0), pipeline_mode=pl.Buffered(1))


def _blocked_matmul(a_ref, w_ref):
    nblk, _, fb = a_ref.shape
    out = None
    for j in range(nblk):
        part = jnp.dot(a_ref[j], w_ref[pl.ds(j * fb, fb), :], preferred_element_type=F32)
        out = part if out is None else out + part
    return out


def _norm_fwd(x, w_norm):
    S, D = x.shape
    tm = min(1024, S)

    def body(x_ref, wn_ref, n_ref):
        xf = x_ref[...]
        n_ref[...] = (xf * _rms_scale(xf) * wn_ref[...]).astype(BF16)

    row = pl.BlockSpec((tm, D), lambda i: (i, 0))
    return pl.pallas_call(body, grid=(S // tm,), name="norm_fwd", in_specs=[row, pl.BlockSpec((1, D), lambda i: (0, 0))],
                          out_specs=row, out_shape=SDS((S, D), BF16), compiler_params=_cp(1))(x, w_norm)


def _proj_part(n1, w_slots, slots, blocks, proj, n_blocks, name):
    S, D = n1.shape
    NB = w_slots.shape[2]
    tm = min(1024, S)

    def body(slots_ref, blocks_ref, n_ref, w_ref, *rest):
        rest[-1][...] = jnp.dot(n_ref[...], w_ref[0], preferred_element_type=F32)

    out_spec = pl.BlockSpec((tm, NB), lambda i, j, slots, blocks: (i, blocks[j]))
    in_specs = [pl.BlockSpec((tm, D), lambda i, j, slots, blocks: (i, 0)),
                pl.BlockSpec((1, D, NB), lambda i, j, slots, blocks: (slots[j], 0, 0))]
    args = [n1, w_slots]
    if proj is not None:
        in_specs.append(ANY)
        args.append(proj)
    return pl.pallas_call(
        body, name=name, out_shape=SDS((S, n_blocks * NB), F32),
        grid_spec=pltpu.PrefetchScalarGridSpec(num_scalar_prefetch=2, grid=(S // tm, slots.shape[0]), in_specs=in_specs,
                                               out_specs=out_spec),
        input_output_aliases={} if proj is None else {4: 0},
        compiler_params=_cp(2))(slots, blocks, *args)


def _out_fwd(x, attn, ret, w_out, w_norm):
    S, D = x.shape
    DA = attn.shape[1]
    tm = min(512, S)

    def body(x_ref, a_ref, r_ref, w_ref, wn_ref, h_ref, mix_ref, n_ref):
        a = a_ref[...].astype(BF16)
        r = r_ref[...].astype(BF16)
        mix_ref[:, :DA] = a
        mix_ref[:, DA:] = r
        h = x_ref[...] + jnp.dot(a, w_ref[:DA, :], preferred_element_type=F32) \
            + jnp.dot(r, w_ref[DA:, :], preferred_element_type=F32)
        h_ref[...] = h
        n_ref[...] = (h * _rms_scale(h) * wn_ref[...]).astype(BF16)

    row = lambda w: pl.BlockSpec((tm, w), lambda i: (i, 0))
    return pl.pallas_call(
        body, grid=(S // tm,), name="out_fwd",
        in_specs=[row(D), row(DA), row(D - DA), pl.BlockSpec((D, D), lambda i: (0, 0)),
                  pl.BlockSpec((1, D), lambda i: (0, 0))],
        out_specs=[row(D), row(D), row(D)],
        out_shape=[SDS((S, D), F32), SDS((S, D), BF16), SDS((S, D), BF16)],
        compiler_params=_cp(1))(x, attn, ret, w_out, w_norm)


def _ffn_up(n2, wg, wu):
    S, D = n2.shape
    nblk, FB, _ = wg.shape
    tm = min(1024, S)

    def body(n_ref, wg_ref, wu_ref, g_ref, u_ref, a_ref):
        n = n_ref[...]
        g = _dot(n, wg_ref[0], _NT)
        u = _dot(n, wu_ref[0], _NT)
        g_ref[0] = g.astype(BF16)
        u_ref[0] = u.astype(BF16)
        a_ref[0] = (g * _sigmoid(g) * u).astype(BF16)

    wspec = pl.BlockSpec((1, FB, D), lambda j, i: (j, 0, 0))
    ospec = pl.BlockSpec((1, tm, FB), lambda j, i: (j, i, 0))
    return pl.pallas_call(
        body, grid=(nblk, S // tm), name="ffn_up",
        in_specs=[pl.BlockSpec((tm, D), lambda j, i: (i, 0)), wspec, wspec],
        out_specs=[ospec, ospec, ospec],
        out_shape=[SDS((nblk, S, FB), BF16)] * 3,
        compiler_params=_cp(2))(n2, wg, wu)


def _ffn_down_loss(act, wd, h1, target, w_norm):
    nblk, S, FB = act.shape
    D = h1.shape[1]
    tm = min(RESIDENT_ROWS, S)

    def body(a_ref, wd_ref, h_ref, t_ref, wn_ref, dh_ref, dhb_ref, loss_ref, dw_ref):
        @pl.when(pl.program_id(0) == 0)
        def _():
            dw_ref[...] = jnp.zeros_like(dw_ref)

        h = h_ref[...] + _blocked_matmul(a_ref, wd_ref)
        w = wn_ref[...]
        err = h * _rms_scale(h) * w - t_ref[...]
        loss_ref[...] = jnp.full(loss_ref.shape, 0.5 * jnp.sum(err * err) / D, F32)
        dh, dw = _rms_bwd(err * (1.0 / D), h, w)
        dh_ref[...] = dh
        dhb_ref[...] = dh.astype(BF16)
        dw_ref[...] += dw

    row = pl.BlockSpec((tm, D), lambda i: (i, 0))
    vec = pl.BlockSpec((1, D), lambda i: (0, 0))
    return pl.pallas_call(
        body, grid=(S // tm,), name="ffn_down_loss",
        in_specs=[pl.BlockSpec((nblk, tm, FB), lambda i: (0, i, 0)), _resident((nblk * FB, D)), row, row, vec],
        out_specs=[row, row, pl.BlockSpec((8, 128), lambda i: (i, 0)), vec],
        out_shape=[SDS((S, D), F32), SDS((S, D), BF16), SDS((S // tm * 8, 128), F32), SDS((1, D), F32)],
        compiler_params=_cp(1))(act, wd, h1, target, w_norm)


def _attn_bias():
    n_heads = 8
    slopes = np.exp2(-8.0 * np.arange(1, n_heads + 1, dtype=np.float32) / n_heads)
    dist = np.abs(np.arange(K_TILE)[None, :] - BAND - np.arange(Q_TILE)[:, None])
    out = np.empty((n_heads, len(DILATIONS), Q_TILE, K_TILE), np.float32)
    for h in range(n_heads):
        for p, d in enumerate(DILATIONS):
            out[h, p] = np.where(dist <= BAND, -slopes[h] * (d * dist).astype(np.float32), NEG)
    return jnp.asarray(out)


def _attn_tiles(S, d):
    L = S // d
    per_class = L // Q_TILE
    return L, per_class, d * per_class


def _tile_rows(t, d, per_class):
    r = t // per_class
    a = (t % per_class) * Q_TILE
    q_rows = pl.ds(r + d * a, Q_TILE, stride=d) if d > 1 else pl.ds(pl.multiple_of(a, Q_TILE), Q_TILE)
    k_rows = pl.ds(KV_PAD + r + d * (a - BAND), K_TILE, stride=d) if d > 1 else pl.ds(
        pl.multiple_of(KV_PAD + a - BAND, BAND), K_TILE)
    return a, q_rows, k_rows


def _to_quarters(dst, src, n, dst_off=0):
    for r in range(4):
        dst[pl.ds(dst_off + r * (n // 4), n // 4), :] = src[pl.ds(r, n // 4, stride=4), :]


def _quarter_tile_rows(t, S):
    L = S // 16
    per_class = L // Q_TILE
    blk, tt = t // (4 * per_class), t % (4 * per_class)
    r, a = tt // per_class, (tt % per_class) * Q_TILE
    q_rows = pl.ds(blk * (S // 4) + r + 4 * a, Q_TILE, stride=4)
    k_rows = pl.ds(KV_PAD + blk * (S // 4) + r + 4 * (a - BAND), K_TILE, stride=4)
    return a, q_rows, k_rows


def _quarter_band_rows(t, S):
    L = S // 4
    per_quarter = L // Q_TILE
    blk, a = t // per_quarter, (t % per_quarter) * Q_TILE
    q_rows = pl.ds(pl.multiple_of(blk * L + a, Q_TILE), Q_TILE)
    k_rows = pl.ds(pl.multiple_of(KV_PAD + blk * L + a - BAND, BAND), K_TILE)
    return a, q_rows, k_rows


def _lanes(x, width):
    return jnp.concatenate([x] * (width // HEAD_DIM), axis=-1)


_BNT = (((2,), (2,)), ((0,), (0,)))
_BNN = (((2,), (1,)), ((0,), (0,)))
_BTN = (((1,), (1,)), ((0,), (0,)))


def _bdot(a, b, dims):
    return lax.dot_general(a, b, dims, preferred_element_type=F32)


def _stacked(rows, loaders):
    return [jnp.stack([f(*r) for r in rows]) for f in loaders]


def _edge_mask(a, L):
    lk = lax.broadcasted_iota(jnp.int32, (1, K_TILE), 1) + (a - BAND)
    return jnp.where((lk >= 0) & (lk < L), 0.0, NEG).astype(F32)


def _fill_padded(dst, src, S):
    dst[pl.ds(0, KV_PAD), :] = jnp.zeros((KV_PAD, HEAD_DIM), F32)
    dst[pl.ds(KV_PAD + S, KV_PAD), :] = jnp.zeros((KV_PAD, HEAD_DIM), F32)
    dst[pl.ds(KV_PAD, S), :] = src[...]


def _head_specs(S, groups, n_heads):
    return [pl.BlockSpec((S, HEAD_DIM), functools.partial(lambda h, g: (0, g * n_heads + h), g=g)) for g in groups]


def _attn_fwd(proj, bias):
    S = proj.shape[0]
    H = proj.shape[1] // (N_GROUPS * HEAD_DIM)
    scale = HEAD_DIM ** -0.5

    def body(q_ref, k_ref, v_ref, b_ref, o_ref, lse_ref, kp, vp, m_run, l_run, q4, m3, l3, acc3):
        _fill_padded(kp, k_ref, S)
        _fill_padded(vp, v_ref, S)
        o_ref[...] = jnp.zeros_like(o_ref)
        m_run[...] = jnp.full(m_run.shape, NEG, F32)
        l_run[...] = jnp.zeros_like(l_run)
        def online(n_tiles, tile_rows, p, L, q_src, m_buf, l_buf, o_buf):
            def tiles(t, carry):
                rows = [tile_rows(t + u * (n_tiles // TILE_GROUP)) for u in range(TILE_GROUP)]
                qs, ks, vs, m_old, l_old, o_old, edge = _stacked(rows, (
                    lambda a, qr, kr: q_src[qr, :].astype(BF16), lambda a, qr, kr: kp[kr, :].astype(BF16),
                    lambda a, qr, kr: vp[kr, :].astype(BF16), lambda a, qr, kr: m_buf[qr, :],
                    lambda a, qr, kr: l_buf[qr, :], lambda a, qr, kr: o_buf[qr, :], lambda a, qr, kr: _edge_mask(a, L)))
                s = _bdot(qs, ks, _BNT) * scale + b_ref[0, p][None] + edge
                m_new = jnp.maximum(m_old, jnp.max(s, axis=-1, keepdims=True))
                pr = jnp.exp(s - _lanes(m_new, K_TILE)).astype(BF16)
                alpha = jnp.exp(m_old - m_new)
                l_new = alpha * l_old + _bdot(pr, jnp.ones((TILE_GROUP, K_TILE, HEAD_DIM), BF16), _BNN)
                o_new = alpha * o_old + _bdot(pr, vs, _BNN)
                for u, (_, qr, _) in enumerate(rows):
                    o_buf[qr, :] = o_new[u]
                    m_buf[qr, :] = m_new[u]
                    l_buf[qr, :] = l_new[u]
                return carry

            lax.fori_loop(0, n_tiles // TILE_GROUP, tiles, 0)

        L, per_class, n_tiles = _attn_tiles(S, DILATIONS[0])
        online(n_tiles, functools.partial(_tile_rows, d=DILATIONS[0], per_class=per_class), 0, L, q_ref, m_run, l_run, o_ref)

        _to_quarters(q4, q_ref, S)
        _to_quarters(kp, k_ref, S, KV_PAD)
        _to_quarters(vp, v_ref, S, KV_PAD)
        n_tiles = _attn_tiles(S, DILATIONS[2])[2]

        def tiles3(t, carry):
            rows = [_quarter_tile_rows(t + u * (n_tiles // TILE_GROUP), S) for u in range(TILE_GROUP)]
            qs, ks, vs, edge = _stacked(rows, (
                lambda a, qr, kr: q4[qr, :].astype(BF16), lambda a, qr, kr: kp[kr, :].astype(BF16),
                lambda a, qr, kr: vp[kr, :].astype(BF16), lambda a, qr, kr: _edge_mask(a, S // DILATIONS[2])))
            s = _bdot(qs, ks, _BNT) * scale + b_ref[0, 2][None] + edge
            m_new = jnp.broadcast_to(jnp.max(s, axis=-1, keepdims=True), (TILE_GROUP, Q_TILE, HEAD_DIM))
            pr = jnp.exp(s - _lanes(m_new, K_TILE)).astype(BF16)
            l_new = _bdot(pr, jnp.ones((TILE_GROUP, K_TILE, HEAD_DIM), BF16), _BNN)
            o_new = _bdot(pr, vs, _BNN)
            for u, (_, qr, _) in enumerate(rows):
                acc3[qr, :] = o_new[u]
                m3[qr, :] = m_new[u]
                l3[qr, :] = l_new[u]
            return carry

        lax.fori_loop(0, n_tiles // TILE_GROUP, tiles3, 0)
        online(_attn_tiles(S, DILATIONS[1])[2], functools.partial(_quarter_band_rows, S=S), 1, S // DILATIONS[1],
               q4, m3, l3, acc3)
        for r in range(4):
            nat, qtr = pl.ds(r, S // 4, stride=4), pl.ds(r * (S // 4), S // 4)
            m_a, m_b = m_run[nat, :], m3[qtr, :]
            m = jnp.maximum(m_a, m_b)
            w_a, w_b = jnp.exp(m_a - m), jnp.exp(m_b - m)
            l = w_a * l_run[nat, :] + w_b * l3[qtr, :]
            o_ref[nat, :] = (w_a * o_ref[nat, :] + w_b * acc3[qtr, :]) / l
            lse_ref[nat, :] = m + jnp.log(l)

    hspec = pl.BlockSpec((S, HEAD_DIM), lambda h: (0, h))
    padded, plain = pltpu.VMEM((S + 2 * KV_PAD, HEAD_DIM), F32), pltpu.VMEM((S, HEAD_DIM), F32)
    return pl.pallas_call(
        body, grid=(H,), name="attn_fwd",
        in_specs=_head_specs(S, (0, 1, 2), H) + [
            pl.BlockSpec((1, len(DILATIONS), Q_TILE, K_TILE), lambda h: (h, 0, 0, 0))],
        out_specs=[hspec, hspec],
        out_shape=[SDS((S, H * HEAD_DIM), F32), SDS((S, H * HEAD_DIM), F32)],
        scratch_shapes=[padded, padded] + [plain] * 6,
        compiler_params=_cp(1))(proj, proj, proj, bias)


def _put_groups(stage, dproj, groups, n_heads, sems):
    h = pl.program_id(0)
    copies = [pltpu.make_async_copy(
        stage.at[i], dproj.at[:, pl.ds(pl.multiple_of((g * n_heads + h) * HEAD_DIM, HEAD_DIM), HEAD_DIM)], sems.at[i])
        for i, g in enumerate(groups)]
    for cp in copies:
        cp.start()
    for cp in copies:
        cp.wait()


def _attn_bwd(proj, out, lse, dmix, bias, dproj):
    S = proj.shape[0]
    H = proj.shape[1] // (N_GROUPS * HEAD_DIM)
    scale = HEAD_DIM ** -0.5
    assert S // DILATIONS[2] >= 2 * Q_TILE

    def body(q_ref, k_ref, v_ref, o_ref, lse_ref, do_ref, b_ref, dproj_in, dproj_out,
             kp, vp, dkp, dvp, dsum, q4, do4, lse4, dsum4, dq_ref, dk_ref, dv_ref, stage, sems):
        _fill_padded(kp, k_ref, S)
        _fill_padded(vp, v_ref, S)
        dkp[...] = jnp.zeros_like(dkp)
        dvp[...] = jnp.zeros_like(dvp)
        dq_ref[...] = jnp.zeros_like(dq_ref)
        dsum[...] = jnp.broadcast_to(jnp.sum(do_ref[...] * o_ref[...], axis=-1, keepdims=True), dsum.shape)

        def run(n_tiles, tile_rows, p, L, q_src, do_src, lse_src, dsum_src, dq_dst, dq_adds):
            def tiles(t, carry):
                rows = [tile_rows(t + u * (n_tiles // BWD_TILE_GROUP)) for u in range(BWD_TILE_GROUP)]
                qs, ks, vs, dos, lses, dsums, dk_old, dv_old, edge = _stacked(rows, (
                    lambda a, qr, kr: q_src[qr, :].astype(BF16), lambda a, qr, kr: kp[kr, :].astype(BF16),
                    lambda a, qr, kr: vp[kr, :].astype(BF16), lambda a, qr, kr: do_src[qr, :].astype(BF16),
                    lambda a, qr, kr: lse_src[qr, :], lambda a, qr, kr: dsum_src[qr, :],
                    lambda a, qr, kr: dkp[kr, :], lambda a, qr, kr: dvp[kr, :], lambda a, qr, kr: _edge_mask(a, L)))
                s = _bdot(qs, ks, _BNT) * scale + b_ref[0, p][None] + edge
                pr = jnp.exp(s - _lanes(lses, K_TILE))
                ds = (pr * (_bdot(dos, vs, _BNT) - _lanes(dsums, K_TILE)) * scale).astype(BF16)
                dq_new = _bdot(ds, ks, _BNN)
                if dq_adds:
                    dq_new = dq_new + jnp.stack([dq_dst[qr, :] for _, qr, _ in rows])
                dk_new = dk_old + _bdot(ds, qs, _BTN)
                dv_new = dv_old + _bdot(pr.astype(BF16), dos, _BTN)
                for u, (_, qr, kr) in enumerate(rows):
                    dq_dst[qr, :] = dq_new[u]
                    dkp[kr, :] = dk_new[u]
                    dvp[kr, :] = dv_new[u]
                return carry

            lax.fori_loop(0, n_tiles // BWD_TILE_GROUP, tiles, 0)

        L, per_class, n_tiles = _attn_tiles(S, DILATIONS[0])
        run(n_tiles, functools.partial(_tile_rows, d=DILATIONS[0], per_class=per_class), 0, L,
            q_ref, do_ref, lse_ref, dsum, dq_ref, True)
        dk_ref[...] = dkp[pl.ds(KV_PAD, S), :]
        dv_ref[...] = dvp[pl.ds(KV_PAD, S), :]

        for dst, src in ((q4, q_ref), (do4, do_ref), (lse4, lse_ref), (dsum4, dsum)):
            _to_quarters(dst, src, S)
        _to_quarters(kp, k_ref, S, KV_PAD)
        _to_quarters(vp, v_ref, S, KV_PAD)
        dkp[...] = jnp.zeros_like(dkp)
        dvp[...] = jnp.zeros_like(dvp)
        dq3 = dsum
        run(_attn_tiles(S, DILATIONS[2])[2], functools.partial(_quarter_tile_rows, S=S), 2, S // DILATIONS[2],
            q4, do4, lse4, dsum4, dq3, False)
        run(_attn_tiles(S, DILATIONS[1])[2], functools.partial(_quarter_band_rows, S=S), 1, S // DILATIONS[1],
            q4, do4, lse4, dsum4, dq3, True)
        for r in range(4):
            nat, qtr = pl.ds(r, S // 4, stride=4), pl.ds(r * (S // 4), S // 4)
            pad_qtr = pl.ds(KV_PAD + r * (S // 4), S // 4)
            dq_ref[nat, :] = dq_ref[nat, :] + dq3[qtr, :]
            dk_ref[nat, :] = dk_ref[nat, :] + dkp[pad_qtr, :]
            dv_ref[nat, :] = dv_ref[nat, :] + dvp[pad_qtr, :]
        for i, acc in enumerate((dq_ref, dk_ref, dv_ref)):
            stage[i] = acc[...].astype(BF16)
        _put_groups(stage, dproj_out, (0, 1, 2), H, sems)

    hspec = pl.BlockSpec((S, HEAD_DIM), lambda h: (0, h))
    once = pl.BlockSpec((S, HEAD_DIM), lambda h: (0, h), pipeline_mode=pl.Buffered(1))
    padded, plain = pltpu.VMEM((S + 2 * KV_PAD, HEAD_DIM), F32), pltpu.VMEM((S, HEAD_DIM), F32)
    return pl.pallas_call(
        body, grid=(H,), name="attn_bwd",
        in_specs=_head_specs(S, (0, 1, 2), H) + [
            once, hspec, hspec, pl.BlockSpec((1, len(DILATIONS), Q_TILE, K_TILE), lambda h: (h, 0, 0, 0)), ANY],
        out_specs=ANY, out_shape=SDS(dproj.shape, dproj.dtype), input_output_aliases={7: 0},
        scratch_shapes=[padded] * 4 + [plain] * 8 + [pltpu.VMEM((3, S, HEAD_DIM), BF16), pltpu.SemaphoreType.DMA((3,))],
        compiler_params=_cp(1))(proj, proj, proj, out, lse, dmix, bias, dproj)


def _ret_consts(lg, forward):
    C = RET_CHUNK
    i = lax.broadcasted_iota(jnp.int32, (C, C), 0)
    j = lax.broadcasted_iota(jnp.int32, (C, C), 1)
    rel = (i - j) if forward else (j - i)
    inside = (rel >= 0) if forward else (rel > 0)
    relf = jnp.maximum(rel, 0).astype(F32)
    mask = jnp.where(inside, jnp.exp(lg * relf), 0.0)
    idx = lax.broadcasted_iota(jnp.int32, (C, 1), 0).astype(F32)
    q_exp = (idx + 1.0) if forward else (C - idx)
    k_exp = (C - 1.0 - idx) if forward else idx
    return mask, relf, jnp.exp(lg * q_exp), q_exp, jnp.exp(lg * k_exp), k_exp, jnp.exp(lg * C)


def _log_decay(dec_ref, h):
    return -jnp.exp(jnp.full((1, 1), dec_ref[0, h], F32))


FFN_BLOCK = 704
CHUNK_BATCH = 16


def _batch_rows(b):
    n = CHUNK_BATCH * RET_CHUNK
    return pl.ds(pl.multiple_of(b * n, n), n)


def _batch_chunks(b):
    return pl.ds(pl.multiple_of(b * CHUNK_BATCH, CHUNK_BATCH), CHUNK_BATCH)


def _chunks3(x):
    return x.reshape(CHUNK_BATCH, RET_CHUNK, HEAD_DIM)


def _ret_scan(buf, c_decs, nc, reverse):
    def step(n, carry):
        new = []
        for way, r in enumerate(carry):
            c = n if (way == 0) != reverse else nc - 1 - n
            term = buf[way, c]
            buf[way, c] = r
            new.append(r * c_decs[way] + term)
        return tuple(new)

    lax.fori_loop(0, nc, step, (jnp.zeros((HEAD_DIM, HEAD_DIM), F32),) * 2)


def _ret_fwd(proj, dec_f, dec_b, w_norm):
    S = proj.shape[0]
    H = proj.shape[1] // (N_GROUPS * HEAD_DIM)
    nc = S // RET_CHUNK
    scale = HEAD_DIM ** -0.5

    def body(df_ref, db_ref, q_ref, k_ref, v_ref, g_ref, w_ref, y_ref, o_ref, states):
        h = pl.program_id(0)
        consts = [_ret_consts(_log_decay(dref, h), fw) for fw, dref in ((True, df_ref), (False, db_ref))]

        def kv_step(b, carry):
            rows, batch = _batch_rows(b), _batch_chunks(b)
            k3 = _chunks3(k_ref[rows, :])
            v3 = _chunks3(v_ref[rows, :]).astype(BF16)
            for way in range(2):
                states[way, batch] = _bdot((k3 * consts[way][4]).astype(BF16), v3, _BTN)
            return carry

        lax.fori_loop(0, nc // CHUNK_BATCH, kv_step, 0)
        _ret_scan(states, [c[6] for c in consts], nc, False)

        def out_step(b, carry):
            rows, batch = _batch_rows(b), _batch_chunks(b)
            q3 = _chunks3(q_ref[rows, :] * scale)
            k3 = _chunks3(k_ref[rows, :]).astype(BF16)
            v3 = _chunks3(v_ref[rows, :]).astype(BF16)
            a0 = _bdot(q3.astype(BF16), k3, _BNT)
            o = None
            for way in range(2):
                mask, q_dec = consts[way][0], consts[way][2]
                part = _bdot((a0 * mask).astype(BF16), v3, _BNN) \
                    + _bdot((q3 * q_dec).astype(BF16), states[way, batch].astype(BF16), _BNN)
                o = part if o is None else o + part
            o_ref[rows, :] = o.reshape(CHUNK_BATCH * RET_CHUNK, HEAD_DIM)
            return carry

        lax.fori_loop(0, nc // CHUNK_BATCH, out_step, 0)
        o = o_ref[...]
        g = g_ref[...]
        y_ref[...] = o * _rms_scale(o) * w_ref[...] * (g * _sigmoid(g))

    hspec = pl.BlockSpec((S, HEAD_DIM), lambda h: (0, h))
    smem = pl.BlockSpec(memory_space=pltpu.SMEM)
    return pl.pallas_call(
        body, grid=(H,), name="ret_fwd",
        in_specs=[smem, smem] + _head_specs(S, (3, 4, 5, 6), H) + [pl.BlockSpec((1, HEAD_DIM), lambda h: (0, h))],
        out_specs=[hspec, hspec],
        out_shape=[SDS((S, H * HEAD_DIM), F32)] * 2,
        scratch_shapes=[pltpu.VMEM((2, nc, HEAD_DIM, HEAD_DIM), F32)],
        compiler_params=_cp(1))(dec_f, dec_b, proj, proj, proj, proj, w_norm)


def _ret_gate_bwd(proj, o_raw, dmix, w_norm, col0, dproj):
    S = proj.shape[0]
    H = proj.shape[1] // (N_GROUPS * HEAD_DIM)

    def body(g_ref, o_ref, dy_ref, w_ref, dproj_in, do_ref, dw_ref, dproj_out, dg_ref, sems):
        o = o_ref[...]
        g = g_ref[...]
        dy = dy_ref[...]
        w = w_ref[...]
        rr = _rms_scale(o)
        normed = o * rr
        sg = _sigmoid(g)
        silu = g * sg
        dw_ref[0] = jnp.broadcast_to(jnp.sum(dy * normed * silu, axis=0, keepdims=True), (8, HEAD_DIM))
        dg_ref[0] = (dy * normed * w * (sg * (1.0 + g * (1.0 - sg)))).astype(BF16)
        dnormed = dy * w * silu
        do_ref[...] = rr * dnormed - o * (rr * rr * rr) * jnp.mean(dnormed * o, axis=-1, keepdims=True)
        _put_groups(dg_ref, dproj_out, (6,), H, sems)

    hspec = pl.BlockSpec((S, HEAD_DIM), lambda h: (0, h))
    nh0 = col0 // HEAD_DIM
    return pl.pallas_call(
        body, grid=(H,), name="ret_gate_bwd",
        in_specs=_head_specs(S, (6,), H) + [hspec, pl.BlockSpec((S, HEAD_DIM), lambda h: (0, nh0 + h)),
                                            pl.BlockSpec((1, HEAD_DIM), lambda h: (0, h)), ANY],
        out_specs=[hspec, pl.BlockSpec((1, 8, HEAD_DIM), lambda h: (h, 0, 0)), ANY],
        out_shape=[SDS((S, H * HEAD_DIM), F32), SDS((H, 8, HEAD_DIM), F32), SDS(dproj.shape, dproj.dtype)],
        input_output_aliases={4: 2},
        scratch_shapes=[pltpu.VMEM((1, S, HEAD_DIM), BF16), pltpu.SemaphoreType.DMA((1,))],
        compiler_params=_cp(1))(proj, o_raw, dmix, w_norm, dproj)


def _ret_bwd(proj, d_out, dec_f, dec_b, dproj):
    S = proj.shape[0]
    H = proj.shape[1] // (N_GROUPS * HEAD_DIM)
    C = RET_CHUNK
    nc = S // C
    scale = HEAD_DIM ** -0.5

    def body(df_ref, db_ref, q_ref, k_ref, v_ref, do, dproj_in, small_ref, dproj_out, states, d_states, stage, sems):
        h = pl.program_id(0)
        lgs = [_log_decay(df_ref, h), _log_decay(db_ref, h)]
        consts = [_ret_consts(lg, fw) for lg, fw in zip(lgs, (True, False))]

        def prep_step(b, carry):
            rows, batch = _batch_rows(b), _batch_chunks(b)
            q3 = _chunks3(q_ref[rows, :] * scale)
            k3 = _chunks3(k_ref[rows, :])
            v3 = _chunks3(v_ref[rows, :]).astype(BF16)
            do3 = _chunks3(do[rows, :]).astype(BF16)
            for way in range(2):
                states[way, batch] = _bdot((k3 * consts[way][4]).astype(BF16), v3, _BTN)
                d_states[way, batch] = _bdot((q3 * consts[way][2]).astype(BF16), do3, _BTN)
            return carry

        lax.fori_loop(0, nc // CHUNK_BATCH, prep_step, 0)
        c_decs = [c[6] for c in consts]
        _ret_scan(states, c_decs, nc, False)
        _ret_scan(d_states, c_decs, nc, True)

        def main_step(b, dlams):
            rows, batch = _batch_rows(b), _batch_chunks(b)
            q3 = _chunks3(q_ref[rows, :] * scale)
            k3 = _chunks3(k_ref[rows, :])
            q3b, k3b = q3.astype(BF16), k3.astype(BF16)
            v3b = _chunks3(v_ref[rows, :]).astype(BF16)
            do3b = _chunks3(do[rows, :]).astype(BF16)
            a0 = _bdot(q3b, k3b, _BNT)
            pv = _bdot(do3b, v3b, _BNT)
            dq = dk = dv = None
            new_dlams = []
            for way in range(2):
                mask, relf, q_dec, q_exp, k_dec, k_exp, c_dec = consts[way]
                state, d_state = states[way, batch], d_states[way, batch]
                dp = pv * mask
                dpb = dp.astype(BF16)
                gq = _bdot(do3b, state.astype(BF16), _BNT)
                gk = _bdot(v3b, d_state.astype(BF16), _BNT)
                parts = (_bdot(dpb, k3b, _BNN) + q_dec * gq, _bdot(dpb, q3b, _BTN) + k_dec * gk,
                         _bdot((a0 * mask).astype(BF16), do3b, _BTN)
                         + _bdot((k3 * k_dec).astype(BF16), d_state.astype(BF16), _BNN))
                dq, dk, dv = parts if dq is None else (dq + parts[0], dk + parts[1], dv + parts[2])
                total = lambda x: jnp.sum(jnp.sum(x, axis=0), axis=0, keepdims=True)
                new_dlams.append(dlams[way] + total(relf * a0 * dp)
                                 + total(q_exp * q_dec * q3 * gq + k_exp * k_dec * k3 * gk)
                                 + (C * c_dec) * total(state * d_state))
            flat = lambda x: x.reshape(CHUNK_BATCH * C, HEAD_DIM)
            stage[0, rows, :] = (flat(dq) * scale).astype(BF16)
            stage[1, rows, :] = flat(dk).astype(BF16)
            stage[2, rows, :] = flat(dv).astype(BF16)
            return tuple(new_dlams)

        dlams = lax.fori_loop(0, nc // CHUNK_BATCH, main_step, (jnp.zeros((1, HEAD_DIM), F32),) * 2)
        for row, (dlam, lg) in enumerate(zip(dlams, lgs)):
            small_ref[0, pl.ds(row, 1), :] = jnp.broadcast_to(jnp.sum(dlam, axis=-1, keepdims=True) * lg, (1, HEAD_DIM))
        small_ref[0, pl.ds(2, 6), :] = jnp.zeros((6, HEAD_DIM), F32)
        _put_groups(stage, dproj_out, (3, 4, 5), H, sems)

    hspec = pl.BlockSpec((S, HEAD_DIM), lambda h: (0, h))
    smem = pl.BlockSpec(memory_space=pltpu.SMEM)
    return pl.pallas_call(
        body, grid=(H,), name="ret_bwd",
        in_specs=[smem, smem] + _head_specs(S, (3, 4, 5), H) + [hspec, ANY],
        out_specs=[pl.BlockSpec((1, 8, HEAD_DIM), lambda h: (h, 0, 0)), ANY],
        out_shape=[SDS((H, 8, HEAD_DIM), F32), SDS(dproj.shape, dproj.dtype)], input_output_aliases={6: 1},
        scratch_shapes=[pltpu.VMEM((2, nc, HEAD_DIM, HEAD_DIM), F32), pltpu.VMEM((2, nc, HEAD_DIM, HEAD_DIM), F32),
                        pltpu.VMEM((3, S, HEAD_DIM), BF16), pltpu.SemaphoreType.DMA((3,))],
        compiler_params=_cp(1))(dec_f, dec_b, proj, proj, proj, d_out, dproj)


def _ffn_bwd_act(dh2, wd, g, u):
    S, D = dh2.shape
    nblk, _, FB = g.shape
    tm = min(1024, S)

    def body(dh_ref, wd_ref, g_ref, u_ref, dg_ref, du_ref):
        dact = _dot(dh_ref[...], wd_ref[...], _NT)
        gg = g_ref[0].astype(F32)
        sg = _sigmoid(gg)
        dg_ref[0] = (dact * u_ref[0].astype(F32) * (sg * (1.0 + gg * (1.0 - sg)))).astype(BF16)
        du_ref[0] = (dact * (gg * sg)).astype(BF16)

    blk = pl.BlockSpec((1, tm, FB), lambda j, i: (j, i, 0))
    return pl.pallas_call(
        body, grid=(nblk, S // tm), name="ffn_bwd_act",
        in_specs=[pl.BlockSpec((tm, D), lambda j, i: (i, 0)), pl.BlockSpec((FB, D), lambda j, i: (j, 0)), blk, blk],
        out_specs=[blk, blk], out_shape=[SDS((nblk, S, FB), BF16)] * 2,
        compiler_params=_cp(2))(dh2, wd, g, u)


def _ffn_bwd_in(dg, du, wg, wu, h1, dh2, w_norm):
    nblk, S, FB = dg.shape
    D = h1.shape[1]
    tm = min(RESIDENT_ROWS, S)
    blk = pl.BlockSpec((nblk, tm, FB), lambda i: (0, i, 0))
    row = pl.BlockSpec((tm, D), lambda i: (i, 0))
    vec = pl.BlockSpec((1, D), lambda i: (0, 0))

    def gate_body(dg_ref, wg_ref, part_ref):
        part_ref[...] = _blocked_matmul(dg_ref, wg_ref)

    part = pl.pallas_call(
        gate_body, grid=(S // tm,), name="ffn_bwd_in_gate", in_specs=[blk, _resident((nblk * FB, D))],
        out_specs=row, out_shape=SDS((S, D), F32), compiler_params=_cp(1))(dg, wg.reshape(nblk * FB, D))

    def body(du_ref, wu_ref, part_ref, h_ref, dh2_ref, wn_ref, dh_ref, dhb_ref, dw_ref):
        @pl.when(pl.program_id(0) == 0)
        def _():
            dw_ref[...] = jnp.zeros_like(dw_ref)

        dh, dw = _rms_bwd(part_ref[...] + _blocked_matmul(du_ref, wu_ref), h_ref[...], wn_ref[...])
        dh = dh2_ref[...] + dh
        dh_ref[...] = dh
        dhb_ref[...] = dh.astype(BF16)
        dw_ref[...] += dw

    return pl.pallas_call(
        body, grid=(S // tm,), name="ffn_bwd_in",
        in_specs=[blk, _resident((nblk * FB, D)), row, row, row, vec],
        out_specs=[row, row, vec], out_shape=[SDS((S, D), F32), SDS((S, D), BF16), SDS((1, D), F32)],
        compiler_params=_cp(1))(du, wu.reshape(nblk * FB, D), part, h1, dh2, w_norm)


def _dmix(dh1, w_out):
    S, D = dh1.shape
    tm = min(512, S)

    def body(dh_ref, w_ref, o_ref):
        o_ref[...] = _dot(dh_ref[...], w_ref[...], _NT)

    row = pl.BlockSpec((tm, D), lambda i: (i, 0))
    return pl.pallas_call(
        body, grid=(S // tm,), name="dmix", in_specs=[row, pl.BlockSpec((D, D), lambda i: (0, 0))],
        out_specs=row, out_shape=SDS((S, D), F32), compiler_params=_cp(1))(dh1, w_out)


def _in_bwd(dproj, w_blk, x, dh1, w_norm):
    S, D = x.shape
    nblk, _, NB = w_blk.shape
    tm = min(RESIDENT_ROWS, S)

    def body(dp_ref, w_ref, x_ref, dh1_ref, wn_ref, dx_ref, dw_ref):
        @pl.when(pl.program_id(0) == 0)
        def _():
            dw_ref[...] = jnp.zeros_like(dw_ref)

        dn = None
        for j in range(nblk):
            part = _dot(dp_ref[:, pl.ds(j * NB, NB)], w_ref[j], _NT)
            dn = part if dn is None else dn + part
        dh, dw = _rms_bwd(dn, x_ref[...], wn_ref[...])
        dx_ref[...] = dh1_ref[...] + dh
        dw_ref[...] += dw

    row = pl.BlockSpec((tm, D), lambda i: (i, 0))
    vec = pl.BlockSpec((1, D), lambda i: (0, 0))
    return pl.pallas_call(
        body, grid=(S // tm,), name="in_bwd",
        in_specs=[pl.BlockSpec((tm, nblk * NB), lambda i: (i, 0)),
                  pl.BlockSpec((nblk, D, NB), lambda i: (0, 0, 0), pipeline_mode=pl.Buffered(1)), row, row, vec],
        out_specs=[row, vec], out_shape=[SDS((S, D), F32), SDS((1, D), F32)],
        compiler_params=_cp(1))(dproj, w_blk, x, dh1, w_norm)


def _wgrad(a, b, a_spec, b_spec, o_spec, o_shape, grid, name):
    nk = grid[-1]

    def ld(ref):
        return ref[0] if len(ref.shape) == 3 else ref[...]

    def body(a_ref, b_ref, o_ref, acc):
        k = pl.program_id(len(grid) - 1)

        @pl.when(k == 0)
        def _():
            acc[...] = jnp.zeros_like(acc)

        acc[...] += _dot(ld(a_ref), ld(b_ref), _TN)

        @pl.when(k == nk - 1)
        def _():
            if len(o_ref.shape) == 3:
                o_ref[0] = acc[...].astype(o_ref.dtype)
            else:
                o_ref[...] = acc[...].astype(o_ref.dtype)

    return pl.pallas_call(
        body, grid=grid, name=name, in_specs=[a_spec, b_spec], out_specs=o_spec, out_shape=SDS(o_shape, BF16),
        scratch_shapes=[pltpu.VMEM(o_spec.block_shape[-2:], F32)], compiler_params=_cp(len(grid)))(a, b)


def _peer(k):
    x, y, c = lax.axis_index("x"), lax.axis_index("y"), lax.axis_index("c")
    px = 1 - x if k & 4 else x
    py = 1 - y if k & 2 else y
    pc = 1 - c if k & 1 else c
    return (px, py, pc), 4 * px + 2 * py + pc


def _exchange_copies(srcs, lands, send_sems, recv_sems, which, gather):
    _, me = _peer(0)
    pairs = []
    for pos, a in enumerate(which):
        for k in range(1, N_DEV):
            dev, idx = _peer(k)
            sem = pos * (N_DEV - 1) + k - 1
            src = srcs[a] if gather else srcs[a].at[idx]
            mk = functools.partial(pltpu.make_async_remote_copy, src_ref=src, send_sem=send_sems.at[sem],
                                   recv_sem=recv_sems.at[sem], device_id=dev, device_id_type=MESH)
            pairs.append((mk(dst_ref=lands[a].at[me]), mk(dst_ref=lands[a].at[idx])))
    return pairs


def _sequencer_kernel(name, collective_id, n_remote, n_local):
    return pl.kernel(mesh=plsc.ScalarSubcoreMesh(axis_name="sequencer", num_cores=1), name=name,
                     scratch_types=(pltpu.SemaphoreType.DMA((n_remote,)), pltpu.SemaphoreType.DMA((n_remote,)),
                                    pltpu.SemaphoreType.DMA((n_local,))),
                     compiler_params=pltpu.CompilerParams(collective_id=collective_id))


def _handshake(ks):
    barrier = pltpu.get_barrier_semaphore()
    for k in ks:
        pl.semaphore_signal(barrier, inc=1, device_id=_peer(k)[0], device_id_type=MESH)
    pl.semaphore_wait(barrier, len(ks))


def _sequencer_scatter(arrays, name, collective_id):
    n = len(arrays)
    hbm = pltpu.MemorySpace.HBM
    srcs = [jax.new_ref(a, memory_space=hbm) for a in arrays]
    lands = [jax.empty_ref(SDS(a.shape, a.dtype), memory_space=hbm) for a in arrays]

    @_sequencer_kernel(name, collective_id, n * (N_DEV - 1), n)
    def launch(send_sems, recv_sems, local_sems):
        _handshake(range(1, N_DEV))
        _, me = _peer(0)
        local = [pltpu.make_async_copy(srcs[a].at[me], lands[a].at[me], local_sems.at[a]) for a in range(n)]
        pairs = _exchange_copies(srcs, lands, send_sems, recv_sems, range(n), False)
        for out, _ in pairs:
            out.start()
        for cp in local:
            cp.start()
        for out, arrival in pairs:
            out.wait_send()
            arrival.wait_recv()
        for cp in local:
            cp.wait()

    launch()
    return [r[...] for r in lands]


SIBLING = 1
OTHER_CHIPS = (2, 4, 6)


def _sequencer_gather(arrays, name, collective_id):
    n = len(arrays)
    hbm = pltpu.MemorySpace.HBM
    srcs = [jax.new_ref(a, memory_space=hbm) for a in arrays]
    lands = [jax.empty_ref(SDS((N_DEV,) + a.shape, a.dtype), memory_space=hbm) for a in arrays]

    @_sequencer_kernel(name, collective_id, n * (N_DEV - 1), n)
    def launch(send_sems, recv_sems, local_sems):
        _handshake((SIBLING,) + OTHER_CHIPS)
        _, me = _peer(0)
        sibling, _ = _peer(SIBLING)

        def copy(a, k, src, block, to):
            sem = a * (N_DEV - 1) + k - 1
            return pltpu.make_async_remote_copy(src_ref=src, dst_ref=lands[a].at[block], send_sem=send_sems.at[sem],
                                                recv_sem=recv_sems.at[sem], device_id=to, device_id_type=MESH)

        local = [pltpu.make_async_copy(srcs[a], lands[a].at[me], local_sems.at[a]) for a in range(n)]
        first = [copy(a, k, srcs[a], me, _peer(k)[0]) for a in range(n) for k in OTHER_CHIPS + (SIBLING,)]
        for cp in first + local:
            cp.start()
        passed = []
        for a in range(n):
            for k in OTHER_CHIPS:
                _, block = _peer(k)
                copy(a, k, srcs[a], block, sibling).wait_recv()
                passed.append(copy(a, k ^ SIBLING, lands[a].at[block], block, sibling))
                passed[-1].start()
        for a in range(n):
            for k in (SIBLING,) + tuple(k ^ SIBLING for k in OTHER_CHIPS):
                copy(a, k, srcs[a], _peer(k)[1], sibling).wait_recv()
        for cp in first + passed:
            cp.wait_send()
        for cp in local:
            cp.wait()

    launch()
    return [r[...] for r in lands]


def _sequencer_gather_chips(array, name, collective_id, chips):
    hbm = pltpu.MemorySpace.HBM
    src = jax.new_ref(array, memory_space=hbm)
    land = jax.empty_ref(SDS((2 * len(chips),) + array.shape, array.dtype), memory_space=hbm)

    @_sequencer_kernel(name, collective_id, 2 * len(chips), 1)
    def launch(send_sems, recv_sems, local_sems):
        _handshake((SIBLING,) + tuple(k for k in chips if k))
        c = lax.axis_index("c")
        sibling, _ = _peer(SIBLING)

        def copy(sem, src_ref, slot, to):
            return pltpu.make_async_remote_copy(src_ref=src_ref, dst_ref=land.at[slot], send_sem=send_sems.at[sem],
                                                recv_sem=recv_sems.at[sem], device_id=to, device_id_type=MESH)

        started = []
        for pos, k in enumerate(chips):
            started.append(copy(2 * pos, src, 2 * pos + c, _peer(k)[0] if k else sibling))
            started[-1].start()
        for pos, k in enumerate(chips):
            if k:
                copy(2 * pos, src, 2 * pos + c, sibling).wait_recv()
                started.append(copy(2 * pos + 1, land.at[2 * pos + c], 2 * pos + c, sibling))
                started[-1].start()
        for pos, k in enumerate(chips):
            copy(2 * pos + 1 if k else 2 * pos, src, 2 * pos + 1 - c, sibling).wait_recv()
        for cp in started:
            cp.wait_send()

    launch()
    return land[...]


def _sequencer_relay_far(near, name, collective_id):
    hbm = pltpu.MemorySpace.HBM
    src = jax.new_ref(near, memory_space=hbm)
    land = jax.empty_ref(SDS((2,) + near.shape[1:], near.dtype), memory_space=hbm)
    half = near.shape[1] // 2

    @_sequencer_kernel(name, collective_id, 3, 1)
    def launch(send_sems, recv_sems, local_sems):
        _handshake((SIBLING, 4, 2))
        c = lax.axis_index("c")
        sibling, _ = _peer(SIBLING)

        def copy(sem, src_ref, dst_ref, to):
            return pltpu.make_async_remote_copy(src_ref=src_ref, dst_ref=dst_ref, send_sem=send_sems.at[sem],
                                                recv_sem=recv_sems.at[sem], device_id=to, device_id_type=MESH)

        upper, lower = pl.ds(0, half), pl.ds(half, half)
        halves = [copy(0, src.at[c, upper, :], land.at[c, upper, :], _peer(2)[0]),
                  copy(1, src.at[2 + c, lower, :], land.at[c, lower, :], _peer(4)[0])]
        for cp in halves:
            cp.start()
        for cp in halves:
            cp.wait_recv()
        passed = copy(2, land.at[c], land.at[c], sibling)
        passed.start()
        copy(2, land.at[1 - c], land.at[1 - c], sibling).wait_recv()
        for cp in halves + [passed]:
            cp.wait_send()

    launch()
    return land[...]


SMALL_ROWS = 64


def _small_step(part, w, m, v):
    def body(p_ref, w_ref, m_ref, v_ref, g_ref, d_ref, nm_ref, nv_ref, gath, send_sems, recv_sems):
        _, me = _peer(0)
        gath[me] = p_ref[...]
        copies = []
        for k in range(1, N_DEV):
            dev, idx = _peer(k)
            out = pltpu.make_async_remote_copy(src_ref=p_ref, dst_ref=gath.at[me], send_sem=send_sems.at[k - 1],
                                               recv_sem=recv_sems.at[k - 1], device_id=dev, device_id_type=MESH)
            out.start()
            arrival = pltpu.make_async_remote_copy(src_ref=p_ref, dst_ref=gath.at[idx], send_sem=send_sems.at[k - 1],
                                                   recv_sem=recv_sems.at[k - 1], device_id=dev, device_id_type=MESH)
            copies.append((out, arrival))
        for out, arrival in copies:
            out.wait_send()
            arrival.wait_recv()
        g = gath[0]
        for p in range(1, N_DEV):
            g = g + gath[p]
        g_ref[...] = g
        d_ref[...], nm_ref[...], nv_ref[...] = _adamw(w_ref[...], g, m_ref[...], v_ref[...])

    vm = pl.BlockSpec(memory_space=pltpu.VMEM)
    return pl.pallas_call(
        body, name="small_step", in_specs=[vm] * 4, out_specs=[vm] * 4,
        out_shape=[SDS((SMALL_ROWS, 128), F32)] * 4,
        scratch_shapes=[pltpu.VMEM((N_DEV, SMALL_ROWS, 128), F32), pltpu.SemaphoreType.DMA((N_DEV - 1,)),
                        pltpu.SemaphoreType.DMA((N_DEV - 1,))])(part, w, m, v)


def _adamw(w, g, m, v):
    m = ADAM_B1 * m + (1.0 - ADAM_B1) * g
    v = ADAM_B2 * v + (1.0 - ADAM_B2) * (g * g)
    m_hat = m / (1.0 - ADAM_B1 ** ADAM_STEP)
    v_hat = v / (1.0 - ADAM_B2 ** ADAM_STEP)
    delta = -ADAM_LR * (m_hat / (jnp.sqrt(v_hat) + ADAM_EPS) + ADAM_WD * w)
    return delta, m, v


def _adamw_block(parts, w, m, v, name):
    R, C = w.shape
    n_parts = len(parts)
    Rp = R // n_parts
    tr = next(t for t in (256, 128, 64, 32, 16, 8) if Rp % t == 0 and t * C <= 256 * 1024)
    per_part = Rp // tr

    def body(*refs):
        p_refs = refs[:n_parts]
        w_ref, m_ref, v_ref, g_ref, d_ref, nm_ref, nv_ref = refs[n_parts:]
        for k, p_ref in enumerate(p_refs):
            @pl.when(pl.program_id(0) // per_part == k)
            def _(p_ref=p_ref):
                g = p_ref[0].astype(F32)
                for p in range(1, N_DEV):
                    g = g + p_ref[p].astype(F32)
                g_ref[...] = g
                d_ref[...], nm_ref[...], nv_ref[...] = _adamw(w_ref[...], g, m_ref[...], v_ref[...])

    row = pl.BlockSpec((tr, C), lambda i: (i, 0))
    part_specs = [pl.BlockSpec((N_DEV, tr, C), functools.partial(
        lambda i, k: (0, jnp.clip(i - k * per_part, 0, per_part - 1), 0), k=k)) for k in range(n_parts)]
    return pl.pallas_call(
        body, grid=(R // tr,), name=name, in_specs=part_specs + [row, row, row],
        out_specs=[row] * 4, out_shape=[SDS((R, C), F32)] * 4, compiler_params=_cp(1))(*parts, w, m, v)


def _pack_small(mix, ffn, fin, retw, dec_f, dec_b, loss):
    flat = jnp.concatenate([mix.reshape(-1), ffn.reshape(-1), fin.reshape(-1), retw.reshape(-1), dec_f.reshape(-1),
                            dec_b.reshape(-1), loss.reshape(-1)])
    return jnp.pad(flat, (0, SMALL_ROWS * 128 - flat.shape[0])).reshape(SMALL_ROWS, 128)


def _unpack_small(packed, shapes):
    flat = packed.reshape(-1)
    out, at = [], 0
    for s in shapes:
        n = math.prod(s)
        out.append(flat[at:at + n].reshape(s))
        at += n
    return out


def kernel(x, norm_mix_w, w_in, ret_decay_fwd, ret_decay_bwd, ret_norm_w, w_out, norm_ffn_w, w_gate, w_up, w_down, norm_final_w, loss_target, m_norm_mix_w, m_w_in, m_ret_decay_fwd, m_ret_decay_bwd, m_ret_norm_w, m_w_out, m_norm_ffn_w, m_w_gate, m_w_up, m_w_down, m_norm_final_w, v_norm_mix_w, v_w_in, v_ret_decay_fwd, v_ret_decay_bwd, v_ret_norm_w, v_w_out, v_norm_ffn_w, v_w_gate, v_w_up, v_w_down, v_norm_final_w):
    x2 = x[0]
    tgt = loss_target[0]
    S, D = x2.shape
    H = ret_norm_w.shape[1] // HEAD_DIM
    DA = H * HEAD_DIM
    fin_w = norm_final_w.reshape(1, D)
    big = (w_in[0], w_out[0], w_gate[0].T, w_up[0].T, w_down[0])

    big_b = [w.astype(BF16) for w in big]
    stages = ((0,), (4, 2), (6,))
    wi_stages = [_sequencer_gather_chips(big_b[0], name, cid, ks)
                 for name, cid, ks in zip(("gather_in_own", "gather_in_near"), (0, 7), stages)]
    NB = big_b[0].shape[1]
    ax, ay = lax.axis_index("x"), lax.axis_index("y")
    chip_of = {k: 2 * (1 - ax if k & 4 else ax) + (1 - ay if k & 2 else ay) for k in (0, 2, 4, 6)}

    n1 = _norm_fwd(x2, norm_mix_w)
    ac = lax.axis_index("c")
    me = 2 * chip_of[0] + ac
    vec = lambda *v: jnp.stack([jnp.asarray(t, jnp.int32) for t in v])
    proj = _proj_part(n1, big_b[0][None], vec(0), vec(me), None, N_DEV, "proj_self")
    for ks, name in zip(stages, ("proj_sibling", "proj_near", "proj_far")):
        if name == "proj_near":
            near, proj, big_b = lax.optimization_barrier((wi_stages[1], proj, big_b))
            wi_stages[1] = near
            wi_stages.append(_sequencer_relay_far(near, "gather_in_far", 8))
            wo, = _sequencer_gather(big_b[1:2], "gather_out", 1)
            wg, wu = _sequencer_gather(big_b[2:4], "gather_gate_up", 9)
            wd, = _sequencer_gather(big_b[4:], "gather_down", 5)
            wi, = _sequencer_gather(big_b[:1], "gather_in_ordered", 10)
        w_st = wi_stages[len(wi_stages) - 1 if name == "proj_far" else ("proj_sibling", "proj_near").index(name)]
        slots, blocks = [], []
        for pos, k in enumerate(ks):
            for core in ((1 - ac,) if k == 0 else (0, 1)):
                slots.append(2 * pos + core)
                blocks.append(2 * chip_of[k] + core)
        proj = _proj_part(n1, w_st, vec(*slots), vec(*blocks), proj, N_DEV, name)
    bias = _attn_bias()[:H]
    attn, lse = _attn_fwd(proj, bias)
    ret, o_raw = _ret_fwd(proj, ret_decay_fwd, ret_decay_bwd, ret_norm_w)
    wo_full = wo.reshape(D, D)
    d_ff = N_DEV * wd.shape[1]
    FB = FFN_BLOCK if d_ff % FFN_BLOCK == 0 else wd.shape[1]
    n_fb = d_ff // FB
    wg, wu = wg.reshape(n_fb, FB, D), wu.reshape(n_fb, FB, D)
    wd_full = wd.reshape(d_ff, D)
    h1, mixed, n2 = _out_fwd(x2, attn, ret, wo_full, norm_ffn_w)
    gate, up, act = _ffn_up(n2, wg, wu)
    dh2, dh2_b, loss_parts, g_fin = _ffn_down_loss(act, wd_full, h1, tgt, fin_w)

    dgate, dup = _ffn_bwd_act(dh2_b, wd_full, gate, up)
    tn = min(1024, D)
    ffn_specs = (pl.BlockSpec((1, S, FB), lambda j, n, k: (j, 0, 0)), pl.BlockSpec((S, tn), lambda j, n, k: (0, n)),
                 pl.BlockSpec((1, FB, tn), lambda j, n, k: (j, 0, n)), (n_fb, FB, D), (n_fb, D // tn, 1))
    per_dev = (N_DEV, d_ff // N_DEV, D)
    g_wd = _wgrad(act, dh2_b, *ffn_specs, "wgrad_down").reshape(per_dev)
    g_wg = _wgrad(dgate, n2, *ffn_specs, "wgrad_gate").reshape(per_dev)
    g_wu = _wgrad(dup, n2, *ffn_specs, "wgrad_up").reshape(per_dev)
    parts_f = _sequencer_scatter([g_wg, g_wu, g_wd], "scatter_ffn", 2)
    dh1, dh1_b, g_ffn = _ffn_bwd_in(dgate, dup, wg, wu, h1, dh2, norm_ffn_w)
    dmix = _dmix(dh1_b, wo_full)
    tmw = min(512, D)
    tk = min(2048, S)
    g_wo = _wgrad(mixed, dh1_b, pl.BlockSpec((tk, tmw), lambda m, k: (k, m)), pl.BlockSpec((tk, D), lambda m, k: (k, 0)),
                  pl.BlockSpec((tmw, D), lambda m, k: (m, 0)), (D, D), (D // tmw, S // tk), "wgrad_out")
    parts_o = _sequencer_scatter([g_wo.reshape(N_DEV, D // N_DEV, D)], "scatter_out", 3)
    d_ret, small_w, dproj = _ret_gate_bwd(proj, o_raw, dmix, ret_norm_w, DA, lax.empty(proj.shape, BF16))
    small, dproj = _ret_bwd(proj, d_ret, ret_decay_fwd, ret_decay_bwd, dproj)
    dproj = _attn_bwd(proj, attn, lse, dmix, bias, dproj)
    half = D // tmw // 2
    parts_i = []
    for part, (name, cid) in enumerate((("in_lo", 4), ("in_hi", 6))):
        g_wi = _wgrad(n1, dproj, pl.BlockSpec((S, tmw), functools.partial(lambda j, m, k, off: (0, m + off), off=part * half)),
                      pl.BlockSpec((S, NB), lambda j, m, k: (0, j)), pl.BlockSpec((1, tmw, NB), lambda j, m, k: (j, m, 0)),
                      (N_DEV, D // 2, NB), (N_DEV, half, 1), "wgrad_" + name)
        parts_i += _sequencer_scatter([g_wi], "scatter_" + name, cid)
    grad_x, g_mix = _in_bwd(dproj, wi, x2, dh1, norm_mix_w)

    big_m = (m_w_in[0], m_w_out[0], m_w_gate[0].T, m_w_up[0].T, m_w_down[0])
    big_v = (v_w_in[0], v_w_out[0], v_w_gate[0].T, v_w_up[0].T, v_w_down[0])
    names = ("adamw_in", "adamw_out", "adamw_gate", "adamw_up", "adamw_down")
    upd = [None] * 5
    for a, p in zip((2, 3, 4, 1, 0), [[t] for t in parts_f + parts_o] + [parts_i]):
        upd[a] = _adamw_block(p, big[a], big_m[a], big_v[a], names[a])

    g_dec_f = small[:, 0, 0].reshape(1, H)
    g_dec_b = small[:, 1, 0].reshape(1, H)
    g_retw = small_w[:, 0, :].reshape(1, DA)
    loss_local = jnp.sum(loss_parts[::8, 0])
    zero = jnp.zeros((1,), F32)
    part = _pack_small(g_mix, g_ffn, g_fin, g_retw, g_dec_f, g_dec_b, loss_local)
    sw = _pack_small(norm_mix_w, norm_ffn_w, norm_final_w, ret_norm_w, ret_decay_fwd, ret_decay_bwd, zero)
    sm = _pack_small(m_norm_mix_w, m_norm_ffn_w, m_norm_final_w, m_ret_norm_w, m_ret_decay_fwd, m_ret_decay_bwd, zero)
    sv = _pack_small(v_norm_mix_w, v_norm_ffn_w, v_norm_final_w, v_ret_norm_w, v_ret_decay_fwd, v_ret_decay_bwd, zero)
    shapes = [(1, D), (1, D), (D,), (1, DA), (1, H), (1, H), ()]
    sg, sd, snm, snv = [_unpack_small(t, shapes) for t in _small_step(part, sw, sm, sv)]
    loss = sg[6]

    def ordered(small_set, k):
        b = [(u[k].T if a in (2, 3) else u[k])[None] for a, u in enumerate(upd)]
        return [small_set[0], b[0], small_set[4], small_set[5], small_set[3], b[1], small_set[1], b[2], b[3], b[4],
                small_set[2]]

    return (loss, grad_x[None], *ordered(sg, 0), *ordered(sd, 1), *ordered(snm, 2), *ordered(snv, 3))
```

```python
import functools
import math

import numpy as np
import jax
import jax.numpy as jnp
from jax import lax
from jax.experimental import pallas as pl
from jax.experimental.pallas import tpu as pltpu
from jax.experimental.pallas import tpu_sc as plsc

F32 = jnp.float32
BF16 = jnp.bfloat16
SDS = jax.ShapeDtypeStruct

HEAD_DIM = 128
EPS = 1e-6
RET_CHUNK = 128
DILATIONS = (1, 4, 16)
BAND = 64
Q_TILE = 128
K_TILE = Q_TILE + 2 * BAND
KV_PAD = BAND * 4
TILE_GROUP = 8
BWD_TILE_GROUP = 8
NEG = -1e30
N_DEV = 8
N_GROUPS = 7
ADAM_LR, ADAM_B1, ADAM_B2, ADAM_EPS, ADAM_WD, ADAM_STEP = 0.001, 0.9, 0.999, 1e-08, 0.01, 10
VMEM_LIMIT = 56 * 1024 * 1024
MESH = pl.DeviceIdType.MESH
ANY = pl.BlockSpec(memory_space=pl.ANY)


def _cp(n_grid):
    return pltpu.CompilerParams(dimension_semantics=("arbitrary",) * n_grid, vmem_limit_bytes=VMEM_LIMIT)


def _sigmoid(x):
    return 1.0 / (1.0 + jnp.exp(-x))


def _rms_scale(h):
    return lax.rsqrt(jnp.mean(h * h, axis=-1, keepdims=True) + EPS)


def _rms_bwd(dn, h, w):
    r = _rms_scale(h)
    gw = dn * w
    dh = r * gw - h * (r * r * r) * jnp.mean(gw * h, axis=-1, keepdims=True)
    return dh, jnp.sum(dn * h * r, axis=0, keepdims=True)


def _dot(a, b, dims):
    return lax.dot_general(a.astype(BF16), b.astype(BF16), (dims, ((), ())), preferred_element_type=F32)


_NN = ((1,), (0,))
_NT = ((1,), (1,))
_TN = ((0,), (0,))


RESIDENT_ROWS = 256


def _resident(shape):
    return pl.BlockSpec(shape, lambda i: (0, 0), pipeline_mode=pl.Buffered(1))


def _blocked_matmul(a_ref, w_ref):
    nblk, _, fb = a_ref.shape
    out = None
    for j in range(nblk):
        part = jnp.dot(a_ref[j], w_ref[pl.ds(j * fb, fb), :], preferred_element_type=F32)
        out = part if out is None else out + part
    return out


def _norm_fwd(x, w_norm):
    S, D = x.shape
    tm = min(1024, S)

    def body(x_ref, wn_ref, n_ref):
        xf = x_ref[...]
        n_ref[...] = (xf * _rms_scale(xf) * wn_ref[...]).astype(BF16)

    row = pl.BlockSpec((tm, D), lambda i: (i, 0))
    return pl.pallas_call(body, grid=(S // tm,), name="norm_fwd", in_specs=[row, pl.BlockSpec((1, D), lambda i: (0, 0))],
                          out_specs=row, out_shape=SDS((S, D), BF16), compiler_params=_cp(1))(x, w_norm)


def _proj_part(n1, w_slots, slots, blocks, proj, n_blocks, name):
    S, D = n1.shape
    NB = w_slots.shape[2]
    tm = min(1024, S)

    def body(slots_ref, blocks_ref, n_ref, w_ref, *rest):
        rest[-1][...] = jnp.dot(n_ref[...], w_ref[0], preferred_element_type=F32)

    out_spec = pl.BlockSpec((tm, NB), lambda i, j, slots, blocks: (i, blocks[j]))
    in_specs = [pl.BlockSpec((tm, D), lambda i, j, slots, blocks: (i, 0)),
                pl.BlockSpec((1, D, NB), lambda i, j, slots, blocks: (slots[j], 0, 0))]
    args = [n1, w_slots]
    if proj is not None:
        in_specs.append(ANY)
        args.append(proj)
    return pl.pallas_call(
        body, name=name, out_shape=SDS((S, n_blocks * NB), F32),
        grid_spec=pltpu.PrefetchScalarGridSpec(num_scalar_prefetch=2, grid=(S // tm, slots.shape[0]), in_specs=in_specs,
                                               out_specs=out_spec),
        input_output_aliases={} if proj is None else {4: 0},
        compiler_params=_cp(2))(slots, blocks, *args)


def _out_fwd(x, attn, ret, w_out, w_norm):
    S, D = x.shape
    DA = attn.shape[1]
    tm = min(512, S)

    def body(x_ref, a_ref, r_ref, w_ref, wn_ref, h_ref, mix_ref, n_ref):
        a = a_ref[...].astype(BF16)
        r = r_ref[...].astype(BF16)
        mix_ref[:, :DA] = a
        mix_ref[:, DA:] = r
        h = x_ref[...] + jnp.dot(a, w_ref[:DA, :], preferred_element_type=F32) \
            + jnp.dot(r, w_ref[DA:, :], preferred_element_type=F32)
        h_ref[...] = h
        n_ref[...] = (h * _rms_scale(h) * wn_ref[...]).astype(BF16)

    row = lambda w: pl.BlockSpec((tm, w), lambda i: (i, 0))
    return pl.pallas_call(
        body, grid=(S // tm,), name="out_fwd",
        in_specs=[row(D), row(DA), row(D - DA), pl.BlockSpec((D, D), lambda i: (0, 0)),
                  pl.BlockSpec((1, D), lambda i: (0, 0))],
        out_specs=[row(D), row(D), row(D)],
        out_shape=[SDS((S, D), F32), SDS((S, D), BF16), SDS((S, D), BF16)],
        compiler_params=_cp(1))(x, attn, ret, w_out, w_norm)


def _ffn_up(n2, wg, wu):
    S, D = n2.shape
    nblk, FB, _ = wg.shape
    tm = min(1024, S)

    def body(n_ref, wg_ref, wu_ref, g_ref, u_ref, a_ref):
        n = n_ref[...]
        g = _dot(n, wg_ref[0], _NT)
        u = _dot(n, wu_ref[0], _NT)
        g_ref[0] = g.astype(BF16)
        u_ref[0] = u.astype(BF16)
        a_ref[0] = (g * _sigmoid(g) * u).astype(BF16)

    wspec = pl.BlockSpec((1, FB, D), lambda j, i: (j, 0, 0))
    ospec = pl.BlockSpec((1, tm, FB), lambda j, i: (j, i, 0))
    return pl.pallas_call(
        body, grid=(nblk, S // tm), name="ffn_up",
        in_specs=[pl.BlockSpec((tm, D), lambda j, i: (i, 0)), wspec, wspec],
        out_specs=[ospec, ospec, ospec],
        out_shape=[SDS((nblk, S, FB), BF16)] * 3,
        compiler_params=_cp(2))(n2, wg, wu)


def _ffn_down_loss(act, wd, h1, target, w_norm):
    nblk, S, FB = act.shape
    D = h1.shape[1]
    tm = min(RESIDENT_ROWS, S)

    def body(a_ref, wd_ref, h_ref, t_ref, wn_ref, dh_ref, dhb_ref, loss_ref, dw_ref):
        @pl.when(pl.program_id(0) == 0)
        def _():
            dw_ref[...] = jnp.zeros_like(dw_ref)

        h = h_ref[...] + _blocked_matmul(a_ref, wd_ref)
        w = wn_ref[...]
        err = h * _rms_scale(h) * w - t_ref[...]
        loss_ref[...] = jnp.full(loss_ref.shape, 0.5 * jnp.sum(err * err) / D, F32)
        dh, dw = _rms_bwd(err * (1.0 / D), h, w)
        dh_ref[...] = dh
        dhb_ref[...] = dh.astype(BF16)
        dw_ref[...] += dw

    row = pl.BlockSpec((tm, D), lambda i: (i, 0))
    vec = pl.BlockSpec((1, D), lambda i: (0, 0))
    return pl.pallas_call(
        body, grid=(S // tm,), name="ffn_down_loss",
        in_specs=[pl.BlockSpec((nblk, tm, FB), lambda i: (0, i, 0)), _resident((nblk * FB, D)), row, row, vec],
        out_specs=[row, row, pl.BlockSpec((8, 128), lambda i: (i, 0)), vec],
        out_shape=[SDS((S, D), F32), SDS((S, D), BF16), SDS((S // tm * 8, 128), F32), SDS((1, D), F32)],
        compiler_params=_cp(1))(act, wd, h1, target, w_norm)


def _attn_bias():
    n_heads = 8
    slopes = np.exp2(-8.0 * np.arange(1, n_heads + 1, dtype=np.float32) / n_heads)
    dist = np.abs(np.arange(K_TILE)[None, :] - BAND - np.arange(Q_TILE)[:, None])
    out = np.empty((n_heads, len(DILATIONS), Q_TILE, K_TILE), np.float32)
    for h in range(n_heads):
        for p, d in enumerate(DILATIONS):
            out[h, p] = np.where(dist <= BAND, -slopes[h] * (d * dist).astype(np.float32), NEG)
    return jnp.asarray(out)


def _attn_tiles(S, d):
    L = S // d
    per_class = L // Q_TILE
    return L, per_class, d * per_class


def _tile_rows(t, d, per_class):
    r = t // per_class
    a = (t % per_class) * Q_TILE
    q_rows = pl.ds(r + d * a, Q_TILE, stride=d) if d > 1 else pl.ds(pl.multiple_of(a, Q_TILE), Q_TILE)
    k_rows = pl.ds(KV_PAD + r + d * (a - BAND), K_TILE, stride=d) if d > 1 else pl.ds(
        pl.multiple_of(KV_PAD + a - BAND, BAND), K_TILE)
    return a, q_rows, k_rows


def _to_quarters(dst, src, n, dst_off=0):
    for r in range(4):
        dst[pl.ds(dst_off + r * (n // 4), n // 4), :] = src[pl.ds(r, n // 4, stride=4), :]


def _quarter_tile_rows(t, S):
    L = S // 16
    per_class = L // Q_TILE
    blk, tt = t // (4 * per_class), t % (4 * per_class)
    r, a = tt // per_class, (tt % per_class) * Q_TILE
    q_rows = pl.ds(blk * (S // 4) + r + 4 * a, Q_TILE, stride=4)
    k_rows = pl.ds(KV_PAD + blk * (S // 4) + r + 4 * (a - BAND), K_TILE, stride=4)
    return a, q_rows, k_rows


def _quarter_band_rows(t, S):
    L = S // 4
    per_quarter = L // Q_TILE
    blk, a = t // per_quarter, (t % per_quarter) * Q_TILE
    q_rows = pl.ds(pl.multiple_of(blk * L + a, Q_TILE), Q_TILE)
    k_rows = pl.ds(pl.multiple_of(KV_PAD + blk * L + a - BAND, BAND), K_TILE)
    return a, q_rows, k_rows


def _lanes(x, width):
    return jnp.concatenate([x] * (width // HEAD_DIM), axis=-1)


_BNT = (((2,), (2,)), ((0,), (0,)))
_BNN = (((2,), (1,)), ((0,), (0,)))
_BTN = (((1,), (1,)), ((0,), (0,)))


def _bdot(a, b, dims):
    return lax.dot_general(a, b, dims, preferred_element_type=F32)


def _stacked(rows, loaders):
    return [jnp.stack([f(*r) for r in rows]) for f in loaders]


def _edge_mask(a, L):
    lk = lax.broadcasted_iota(jnp.int32, (1, K_TILE), 1) + (a - BAND)
    return jnp.where((lk >= 0) & (lk < L), 0.0, NEG).astype(F32)


def _fill_padded(dst, src, S):
    dst[pl.ds(0, KV_PAD), :] = jnp.zeros((KV_PAD, HEAD_DIM), F32)
    dst[pl.ds(KV_PAD + S, KV_PAD), :] = jnp.zeros((KV_PAD, HEAD_DIM), F32)
    dst[pl.ds(KV_PAD, S), :] = src[...]


def _head_specs(S, groups, n_heads):
    return [pl.BlockSpec((S, HEAD_DIM), functools.partial(lambda h, g: (0, g * n_heads + h), g=g)) for g in groups]


def _attn_fwd(proj, bias):
    S = proj.shape[0]
    H = proj.shape[1] // (N_GROUPS * HEAD_DIM)
    scale = HEAD_DIM ** -0.5

    def body(q_ref, k_ref, v_ref, b_ref, o_ref, lse_ref, kp, vp, m_run, l_run, q4, m3, l3, acc3):
        _fill_padded(kp, k_ref, S)
        _fill_padded(vp, v_ref, S)
        o_ref[...] = jnp.zeros_like(o_ref)
        m_run[...] = jnp.full(m_run.shape, NEG, F32)
        l_run[...] = jnp.zeros_like(l_run)
        def online(n_tiles, tile_rows, p, L, q_src, m_buf, l_buf, o_buf):
            def tiles(t, carry):
                rows = [tile_rows(t + u * (n_tiles // TILE_GROUP)) for u in range(TILE_GROUP)]
                qs, ks, vs, m_old, l_old, o_old, edge = _stacked(rows, (
                    lambda a, qr, kr: q_src[qr, :].astype(BF16), lambda a, qr, kr: kp[kr, :].astype(BF16),
                    lambda a, qr, kr: vp[kr, :].astype(BF16), lambda a, qr, kr: m_buf[qr, :],
                    lambda a, qr, kr: l_buf[qr, :], lambda a, qr, kr: o_buf[qr, :], lambda a, qr, kr: _edge_mask(a, L)))
                s = _bdot(qs, ks, _BNT) * scale + b_ref[0, p][None] + edge
                m_new = jnp.maximum(m_old, jnp.max(s, axis=-1, keepdims=True))
                pr = jnp.exp(s - _lanes(m_new, K_TILE)).astype(BF16)
                alpha = jnp.exp(m_old - m_new)
                l_new = alpha * l_old + _bdot(pr, jnp.ones((TILE_GROUP, K_TILE, HEAD_DIM), BF16), _BNN)
                o_new = alpha * o_old + _bdot(pr, vs, _BNN)
                for u, (_, qr, _) in enumerate(rows):
                    o_buf[qr, :] = o_new[u]
                    m_buf[qr, :] = m_new[u]
                    l_buf[qr, :] = l_new[u]
                return carry

            lax.fori_loop(0, n_tiles // TILE_GROUP, tiles, 0)

        L, per_class, n_tiles = _attn_tiles(S, DILATIONS[0])
        online(n_tiles, functools.partial(_tile_rows, d=DILATIONS[0], per_class=per_class), 0, L, q_ref, m_run, l_run, o_ref)

        _to_quarters(q4, q_ref, S)
        _to_quarters(kp, k_ref, S, KV_PAD)
        _to_quarters(vp, v_ref, S, KV_PAD)
        n_tiles = _attn_tiles(S, DILATIONS[2])[2]

        def tiles3(t, carry):
            rows = [_quarter_tile_rows(t + u * (n_tiles // TILE_GROUP), S) for u in range(TILE_GROUP)]
            qs, ks, vs, edge = _stacked(rows, (
                lambda a, qr, kr: q4[qr, :].astype(BF16), lambda a, qr, kr: kp[kr, :].astype(BF16),
                lambda a, qr, kr: vp[kr, :].astype(BF16), lambda a, qr, kr: _edge_mask(a, S // DILATIONS[2])))
            s = _bdot(qs, ks, _BNT) * scale + b_ref[0, 2][None] + edge
            m_new = jnp.broadcast_to(jnp.max(s, axis=-1, keepdims=True), (TILE_GROUP, Q_TILE, HEAD_DIM))
            pr = jnp.exp(s - _lanes(m_new, K_TILE)).astype(BF16)
            l_new = _bdot(pr, jnp.ones((TILE_GROUP, K_TILE, HEAD_DIM), BF16), _BNN)
            o_new = _bdot(pr, vs, _BNN)
            for u, (_, qr, _) in enumerate(rows):
                acc3[qr, :] = o_new[u]
                m3[qr, :] = m_new[u]
                l3[qr, :] = l_new[u]
            return carry

        lax.fori_loop(0, n_tiles // TILE_GROUP, tiles3, 0)
        online(_attn_tiles(S, DILATIONS[1])[2], functools.partial(_quarter_band_rows, S=S), 1, S // DILATIONS[1],
               q4, m3, l3, acc3)
        for r in range(4):
            nat, qtr = pl.ds(r, S // 4, stride=4), pl.ds(r * (S // 4), S // 4)
            m_a, m_b = m_run[nat, :], m3[qtr, :]
            m = jnp.maximum(m_a, m_b)
            w_a, w_b = jnp.exp(m_a - m), jnp.exp(m_b - m)
            l = w_a * l_run[nat, :] + w_b * l3[qtr, :]
            o_ref[nat, :] = (w_a * o_ref[nat, :] + w_b * acc3[qtr, :]) / l
            lse_ref[nat, :] = m + jnp.log(l)

    hspec = pl.BlockSpec((S, HEAD_DIM), lambda h: (0, h))
    padded, plain = pltpu.VMEM((S + 2 * KV_PAD, HEAD_DIM), F32), pltpu.VMEM((S, HEAD_DIM), F32)
    return pl.pallas_call(
        body, grid=(H,), name="attn_fwd",
        in_specs=_head_specs(S, (0, 1, 2), H) + [
            pl.BlockSpec((1, len(DILATIONS), Q_TILE, K_TILE), lambda h: (h, 0, 0, 0))],
        out_specs=[hspec, hspec],
        out_shape=[SDS((S, H * HEAD_DIM), F32), SDS((S, H * HEAD_DIM), F32)],
        scratch_shapes=[padded, padded] + [plain] * 6,
        compiler_params=_cp(1))(proj, proj, proj, bias)


def _put_groups(stage, dproj, groups, n_heads, sems):
    h = pl.program_id(0)
    copies = [pltpu.make_async_copy(
        stage.at[i], dproj.at[:, pl.ds(pl.multiple_of((g * n_heads + h) * HEAD_DIM, HEAD_DIM), HEAD_DIM)], sems.at[i])
        for i, g in enumerate(groups)]
    for cp in copies:
        cp.start()
    for cp in copies:
        cp.wait()


def _attn_bwd(proj, out, lse, dmix, bias, dproj):
    S = proj.shape[0]
    H = proj.shape[1] // (N_GROUPS * HEAD_DIM)
    scale = HEAD_DIM ** -0.5
    assert S // DILATIONS[2] >= 2 * Q_TILE

    def body(q_ref, k_ref, v_ref, o_ref, lse_ref, do_ref, b_ref, dproj_in, dproj_out,
             kp, vp, dkp, dvp, dsum, q4, do4, lse4, dsum4, dq_ref, dk_ref, dv_ref, stage, sems):
        _fill_padded(kp, k_ref, S)
        _fill_padded(vp, v_ref, S)
        dkp[...] = jnp.zeros_like(dkp)
        dvp[...] = jnp.zeros_like(dvp)
        dq_ref[...] = jnp.zeros_like(dq_ref)
        dsum[...] = jnp.broadcast_to(jnp.sum(do_ref[...] * o_ref[...], axis=-1, keepdims=True), dsum.shape)

        def run(n_tiles, tile_rows, p, L, q_src, do_src, lse_src, dsum_src, dq_dst, dq_adds):
            def tiles(t, carry):
                rows = [tile_rows(t + u * (n_tiles // BWD_TILE_GROUP)) for u in range(BWD_TILE_GROUP)]
                qs, ks, vs, dos, lses, dsums, dk_old, dv_old, edge = _stacked(rows, (
                    lambda a, qr, kr: q_src[qr, :].astype(BF16), lambda a, qr, kr: kp[kr, :].astype(BF16),
                    lambda a, qr, kr: vp[kr, :].astype(BF16), lambda a, qr, kr: do_src[qr, :].astype(BF16),
                    lambda a, qr, kr: lse_src[qr, :], lambda a, qr, kr: dsum_src[qr, :],
                    lambda a, qr, kr: dkp[kr, :], lambda a, qr, kr: dvp[kr, :], lambda a, qr, kr: _edge_mask(a, L)))
                s = _bdot(qs, ks, _BNT) * scale + b_ref[0, p][None] + edge
                pr = jnp.exp(s - _lanes(lses, K_TILE))
                ds = (pr * (_bdot(dos, vs, _BNT) - _lanes(dsums, K_TILE)) * scale).astype(BF16)
                dq_new = _bdot(ds, ks, _BNN)
                if dq_adds:
                    dq_new = dq_new + jnp.stack([dq_dst[qr, :] for _, qr, _ in rows])
                dk_new = dk_old + _bdot(ds, qs, _BTN)
                dv_new = dv_old + _bdot(pr.astype(BF16), dos, _BTN)
                for u, (_, qr, kr) in enumerate(rows):
                    dq_dst[qr, :] = dq_new[u]
                    dkp[kr, :] = dk_new[u]
                    dvp[kr, :] = dv_new[u]
                return carry

            lax.fori_loop(0, n_tiles // BWD_TILE_GROUP, tiles, 0)

        L, per_class, n_tiles = _attn_tiles(S, DILATIONS[0])
        run(n_tiles, functools.partial(_tile_rows, d=DILATIONS[0], per_class=per_class), 0, L,
            q_ref, do_ref, lse_ref, dsum, dq_ref, True)
        dk_ref[...] = dkp[pl.ds(KV_PAD, S), :]
        dv_ref[...] = dvp[pl.ds(KV_PAD, S), :]

        for dst, src in ((q4, q_ref), (do4, do_ref), (lse4, lse_ref), (dsum4, dsum)):
            _to_quarters(dst, src, S)
        _to_quarters(kp, k_ref, S, KV_PAD)
        _to_quarters(vp, v_ref, S, KV_PAD)
        dkp[...] = jnp.zeros_like(dkp)
        dvp[...] = jnp.zeros_like(dvp)
        dq3 = dsum
        run(_attn_tiles(S, DILATIONS[2])[2], functools.partial(_quarter_tile_rows, S=S), 2, S // DILATIONS[2],
            q4, do4, lse4, dsum4, dq3, False)
        run(_attn_tiles(S, DILATIONS[1])[2], functools.partial(_quarter_band_rows, S=S), 1, S // DILATIONS[1],
            q4, do4, lse4, dsum4, dq3, True)
        for r in range(4):
            nat, qtr = pl.ds(r, S // 4, stride=4), pl.ds(r * (S // 4), S // 4)
            pad_qtr = pl.ds(KV_PAD + r * (S // 4), S // 4)
            dq_ref[nat, :] = dq_ref[nat, :] + dq3[qtr, :]
            dk_ref[nat, :] = dk_ref[nat, :] + dkp[pad_qtr, :]
            dv_ref[nat, :] = dv_ref[nat, :] + dvp[pad_qtr, :]
        for i, acc in enumerate((dq_ref, dk_ref, dv_ref)):
            stage[i] = acc[...].astype(BF16)
        _put_groups(stage, dproj_out, (0, 1, 2), H, sems)

    hspec = pl.BlockSpec((S, HEAD_DIM), lambda h: (0, h))
    once = pl.BlockSpec((S, HEAD_DIM), lambda h: (0, h), pipeline_mode=pl.Buffered(1))
    padded, plain = pltpu.VMEM((S + 2 * KV_PAD, HEAD_DIM), F32), pltpu.VMEM((S, HEAD_DIM), F32)
    return pl.pallas_call(
        body, grid=(H,), name="attn_bwd",
        in_specs=_head_specs(S, (0, 1, 2), H) + [
            once, hspec, hspec, pl.BlockSpec((1, len(DILATIONS), Q_TILE, K_TILE), lambda h: (h, 0, 0, 0)), ANY],
        out_specs=ANY, out_shape=SDS(dproj.shape, dproj.dtype), input_output_aliases={7: 0},
        scratch_shapes=[padded] * 4 + [plain] * 8 + [pltpu.VMEM((3, S, HEAD_DIM), BF16), pltpu.SemaphoreType.DMA((3,))],
        compiler_params=_cp(1))(proj, proj, proj, out, lse, dmix, bias, dproj)


def _ret_consts(lg, forward):
    C = RET_CHUNK
    i = lax.broadcasted_iota(jnp.int32, (C, C), 0)
    j = lax.broadcasted_iota(jnp.int32, (C, C), 1)
    rel = (i - j) if forward else (j - i)
    inside = (rel >= 0) if forward else (rel > 0)
    relf = jnp.maximum(rel, 0).astype(F32)
    mask = jnp.where(inside, jnp.exp(lg * relf), 0.0)
    idx = lax.broadcasted_iota(jnp.int32, (C, 1), 0).astype(F32)
    q_exp = (idx + 1.0) if forward else (C - idx)
    k_exp = (C - 1.0 - idx) if forward else idx
    return mask, relf, jnp.exp(lg * q_exp), q_exp, jnp.exp(lg * k_exp), k_exp, jnp.exp(lg * C)


def _log_decay(dec_ref, h):
    return -jnp.exp(jnp.full((1, 1), dec_ref[0, h], F32))


FFN_BLOCK = 704
CHUNK_BATCH = 16


def _batch_rows(b):
    n = CHUNK_BATCH * RET_CHUNK
    return pl.ds(pl.multiple_of(b * n, n), n)


def _batch_chunks(b):
    return pl.ds(pl.multiple_of(b * CHUNK_BATCH, CHUNK_BATCH), CHUNK_BATCH)


def _chunks3(x):
    return x.reshape(CHUNK_BATCH, RET_CHUNK, HEAD_DIM)


def _ret_scan(buf, c_decs, nc, reverse):
    def step(n, carry):
        new = []
        for way, r in enumerate(carry):
            c = n if (way == 0) != reverse else nc - 1 - n
            term = buf[way, c]
            buf[way, c] = r
            new.append(r * c_decs[way] + term)
        return tuple(new)

    lax.fori_loop(0, nc, step, (jnp.zeros((HEAD_DIM, HEAD_DIM), F32),) * 2)


def _ret_fwd(proj, dec_f, dec_b, w_norm):
    S = proj.shape[0]
    H = proj.shape[1] // (N_GROUPS * HEAD_DIM)
    nc = S // RET_CHUNK
    scale = HEAD_DIM ** -0.5

    def body(df_ref, db_ref, q_ref, k_ref, v_ref, g_ref, w_ref, y_ref, o_ref, states):
        h = pl.program_id(0)
        consts = [_ret_consts(_log_decay(dref, h), fw) for fw, dref in ((True, df_ref), (False, db_ref))]

        def kv_step(b, carry):
            rows, batch = _batch_rows(b), _batch_chunks(b)
            k3 = _chunks3(k_ref[rows, :])
            v3 = _chunks3(v_ref[rows, :]).astype(BF16)
            for way in range(2):
                states[way, batch] = _bdot((k3 * consts[way][4]).astype(BF16), v3, _BTN)
            return carry

        lax.fori_loop(0, nc // CHUNK_BATCH, kv_step, 0)
        _ret_scan(states, [c[6] for c in consts], nc, False)

        def out_step(b, carry):
            rows, batch = _batch_rows(b), _batch_chunks(b)
            q3 = _chunks3(q_ref[rows, :] * scale)
            k3 = _chunks3(k_ref[rows, :]).astype(BF16)
            v3 = _chunks3(v_ref[rows, :]).astype(BF16)
            a0 = _bdot(q3.astype(BF16), k3, _BNT)
            o = None
            for way in range(2):
                mask, q_dec = consts[way][0], consts[way][2]
                part = _bdot((a0 * mask).astype(BF16), v3, _BNN) \
                    + _bdot((q3 * q_dec).astype(BF16), states[way, batch].astype(BF16), _BNN)
                o = part if o is None else o + part
            o_ref[rows, :] = o.reshape(CHUNK_BATCH * RET_CHUNK, HEAD_DIM)
            return carry

        lax.fori_loop(0, nc // CHUNK_BATCH, out_step, 0)
        o = o_ref[...]
        g = g_ref[...]
        y_ref[...] = o * _rms_scale(o) * w_ref[...] * (g * _sigmoid(g))

    hspec = pl.BlockSpec((S, HEAD_DIM), lambda h: (0, h))
    smem = pl.BlockSpec(memory_space=pltpu.SMEM)
    return pl.pallas_call(
        body, grid=(H,), name="ret_fwd",
        in_specs=[smem, smem] + _head_specs(S, (3, 4, 5, 6), H) + [pl.BlockSpec((1, HEAD_DIM), lambda h: (0, h))],
        out_specs=[hspec, hspec],
        out_shape=[SDS((S, H * HEAD_DIM), F32)] * 2,
        scratch_shapes=[pltpu.VMEM((2, nc, HEAD_DIM, HEAD_DIM), F32)],
        compiler_params=_cp(1))(dec_f, dec_b, proj, proj, proj, proj, w_norm)


def _ret_gate_bwd(proj, o_raw, dmix, w_norm, col0, dproj):
    S = proj.shape[0]
    H = proj.shape[1] // (N_GROUPS * HEAD_DIM)

    def body(g_ref, o_ref, dy_ref, w_ref, dproj_in, do_ref, dw_ref, dproj_out, dg_ref, sems):
        o = o_ref[...]
        g = g_ref[...]
        dy = dy_ref[...]
        w = w_ref[...]
        rr = _rms_scale(o)
        normed = o * rr
        sg = _sigmoid(g)
        silu = g * sg
        dw_ref[0] = jnp.broadcast_to(jnp.sum(dy * normed * silu, axis=0, keepdims=True), (8, HEAD_DIM))
        dg_ref[0] = (dy * normed * w * (sg * (1.0 + g * (1.0 - sg)))).astype(BF16)
        dnormed = dy * w * silu
        do_ref[...] = rr * dnormed - o * (rr * rr * rr) * jnp.mean(dnormed * o, axis=-1, keepdims=True)
        _put_groups(dg_ref, dproj_out, (6,), H, sems)

    hspec = pl.BlockSpec((S, HEAD_DIM), lambda h: (0, h))
    nh0 = col0 // HEAD_DIM
    return pl.pallas_call(
        body, grid=(H,), name="ret_gate_bwd",
        in_specs=_head_specs(S, (6,), H) + [hspec, pl.BlockSpec((S, HEAD_DIM), lambda h: (0, nh0 + h)),
                                            pl.BlockSpec((1, HEAD_DIM), lambda h: (0, h)), ANY],
        out_specs=[hspec, pl.BlockSpec((1, 8, HEAD_DIM), lambda h: (h, 0, 0)), ANY],
        out_shape=[SDS((S, H * HEAD_DIM), F32), SDS((H, 8, HEAD_DIM), F32), SDS(dproj.shape, dproj.dtype)],
        input_output_aliases={4: 2},
        scratch_shapes=[pltpu.VMEM((1, S, HEAD_DIM), BF16), pltpu.SemaphoreType.DMA((1,))],
        compiler_params=_cp(1))(proj, o_raw, dmix, w_norm, dproj)


def _ret_bwd(proj, d_out, dec_f, dec_b, dproj):
    S = proj.shape[0]
    H = proj.shape[1] // (N_GROUPS * HEAD_DIM)
    C = RET_CHUNK
    nc = S // C
    scale = HEAD_DIM ** -0.5

    def body(df_ref, db_ref, q_ref, k_ref, v_ref, do, dproj_in, small_ref, dproj_out, states, d_states, stage, sems):
        h = pl.program_id(0)
        lgs = [_log_decay(df_ref, h), _log_decay(db_ref, h)]
        consts = [_ret_consts(lg, fw) for lg, fw in zip(lgs, (True, False))]

        def prep_step(b, carry):
            rows, batch = _batch_rows(b), _batch_chunks(b)
            q3 = _chunks3(q_ref[rows, :] * scale)
            k3 = _chunks3(k_ref[rows, :])
            v3 = _chunks3(v_ref[rows, :]).astype(BF16)
            do3 = _chunks3(do[rows, :]).astype(BF16)
            for way in range(2):
                states[way, batch] = _bdot((k3 * consts[way][4]).astype(BF16), v3, _BTN)
                d_states[way, batch] = _bdot((q3 * consts[way][2]).astype(BF16), do3, _BTN)
            return carry

        lax.fori_loop(0, nc // CHUNK_BATCH, prep_step, 0)
        c_decs = [c[6] for c in consts]
        _ret_scan(states, c_decs, nc, False)
        _ret_scan(d_states, c_decs, nc, True)

        def main_step(b, dlams):
            rows, batch = _batch_rows(b), _batch_chunks(b)
            q3 = _chunks3(q_ref[rows, :] * scale)
            k3 = _chunks3(k_ref[rows, :])
            q3b, k3b = q3.astype(BF16), k3.astype(BF16)
            v3b = _chunks3(v_ref[rows, :]).astype(BF16)
            do3b = _chunks3(do[rows, :]).astype(BF16)
            a0 = _bdot(q3b, k3b, _BNT)
            pv = _bdot(do3b, v3b, _BNT)
            dq = dk = dv = None
            new_dlams = []
            for way in range(2):
                mask, relf, q_dec, q_exp, k_dec, k_exp, c_dec = consts[way]
                state, d_state = states[way, batch], d_states[way, batch]
                dp = pv * mask
                dpb = dp.astype(BF16)
                gq = _bdot(do3b, state.astype(BF16), _BNT)
                gk = _bdot(v3b, d_state.astype(BF16), _BNT)
                parts = (_bdot(dpb, k3b, _BNN) + q_dec * gq, _bdot(dpb, q3b, _BTN) + k_dec * gk,
                         _bdot((a0 * mask).astype(BF16), do3b, _BTN)
                         + _bdot((k3 * k_dec).astype(BF16), d_state.astype(BF16), _BNN))
                dq, dk, dv = parts if dq is None else (dq + parts[0], dk + parts[1], dv + parts[2])
                total = lambda x: jnp.sum(jnp.sum(x, axis=0), axis=0, keepdims=True)
                new_dlams.append(dlams[way] + total(relf * a0 * dp)
                                 + total(q_exp * q_dec * q3 * gq + k_exp * k_dec * k3 * gk)
                                 + (C * c_dec) * total(state * d_state))
            flat = lambda x: x.reshape(CHUNK_BATCH * C, HEAD_DIM)
            stage[0, rows, :] = (flat(dq) * scale).astype(BF16)
            stage[1, rows, :] = flat(dk).astype(BF16)
            stage[2, rows, :] = flat(dv).astype(BF16)
            return tuple(new_dlams)

        dlams = lax.fori_loop(0, nc // CHUNK_BATCH, main_step, (jnp.zeros((1, HEAD_DIM), F32),) * 2)
        for row, (dlam, lg) in enumerate(zip(dlams, lgs)):
            small_ref[0, pl.ds(row, 1), :] = jnp.broadcast_to(jnp.sum(dlam, axis=-1, keepdims=True) * lg, (1, HEAD_DIM))
        small_ref[0, pl.ds(2, 6), :] = jnp.zeros((6, HEAD_DIM), F32)
        _put_groups(stage, dproj_out, (3, 4, 5), H, sems)

    hspec = pl.BlockSpec((S, HEAD_DIM), lambda h: (0, h))
    smem = pl.BlockSpec(memory_space=pltpu.SMEM)
    return pl.pallas_call(
        body, grid=(H,), name="ret_bwd",
        in_specs=[smem, smem] + _head_specs(S, (3, 4, 5), H) + [hspec, ANY],
        out_specs=[pl.BlockSpec((1, 8, HEAD_DIM), lambda h: (h, 0, 0)), ANY],
        out_shape=[SDS((H, 8, HEAD_DIM), F32), SDS(dproj.shape, dproj.dtype)], input_output_aliases={6: 1},
        scratch_shapes=[pltpu.VMEM((2, nc, HEAD_DIM, HEAD_DIM), F32), pltpu.VMEM((2, nc, HEAD_DIM, HEAD_DIM), F32),
                        pltpu.VMEM((3, S, HEAD_DIM), BF16), pltpu.SemaphoreType.DMA((3,))],
        compiler_params=_cp(1))(dec_f, dec_b, proj, proj, proj, d_out, dproj)


def _ffn_bwd_act(dh2, wd, g, u):
    S, D = dh2.shape
    nblk, _, FB = g.shape
    tm = min(1024, S)

    def body(dh_ref, wd_ref, g_ref, u_ref, dg_ref, du_ref):
        dact = _dot(dh_ref[...], wd_ref[...], _NT)
        gg = g_ref[0].astype(F32)
        sg = _sigmoid(gg)
        dg_ref[0] = (dact * u_ref[0].astype(F32) * (sg * (1.0 + gg * (1.0 - sg)))).astype(BF16)
        du_ref[0] = (dact * (gg * sg)).astype(BF16)

    blk = pl.BlockSpec((1, tm, FB), lambda j, i: (j, i, 0))
    return pl.pallas_call(
        body, grid=(nblk, S // tm), name="ffn_bwd_act",
        in_specs=[pl.BlockSpec((tm, D), lambda j, i: (i, 0)), pl.BlockSpec((FB, D), lambda j, i: (j, 0)), blk, blk],
        out_specs=[blk, blk], out_shape=[SDS((nblk, S, FB), BF16)] * 2,
        compiler_params=_cp(2))(dh2, wd, g, u)


def _ffn_bwd_in(dg, du, wg, wu, h1, dh2, w_norm):
    nblk, S, FB = dg.shape
    D = h1.shape[1]
    tm = min(RESIDENT_ROWS, S)
    blk = pl.BlockSpec((nblk, tm, FB), lambda i: (0, i, 0))
    row = pl.BlockSpec((tm, D), lambda i: (i, 0))
    vec = pl.BlockSpec((1, D), lambda i: (0, 0))

    def gate_body(dg_ref, wg_ref, part_ref):
        part_ref[...] = _blocked_matmul(dg_ref, wg_ref)

    part = pl.pallas_call(
        gate_body, grid=(S // tm,), name="ffn_bwd_in_gate", in_specs=[blk, _resident((nblk * FB, D))],
        out_specs=row, out_shape=SDS((S, D), F32), compiler_params=_cp(1))(dg, wg.reshape(nblk * FB, D))

    def body(du_ref, wu_ref, part_ref, h_ref, dh2_ref, wn_ref, dh_ref, dhb_ref, dw_ref):
        @pl.when(pl.program_id(0) == 0)
        def _():
            dw_ref[...] = jnp.zeros_like(dw_ref)

        dh, dw = _rms_bwd(part_ref[...] + _blocked_matmul(du_ref, wu_ref), h_ref[...], wn_ref[...])
        dh = dh2_ref[...] + dh
        dh_ref[...] = dh
        dhb_ref[...] = dh.astype(BF16)
        dw_ref[...] += dw

    return pl.pallas_call(
        body, grid=(S // tm,), name="ffn_bwd_in",
        in_specs=[blk, _resident((nblk * FB, D)), row, row, row, vec],
        out_specs=[row, row, vec], out_shape=[SDS((S, D), F32), SDS((S, D), BF16), SDS((1, D), F32)],
        compiler_params=_cp(1))(du, wu.reshape(nblk * FB, D), part, h1, dh2, w_norm)


def _dmix(dh1, w_out):
    S, D = dh1.shape
    tm = min(512, S)

    def body(dh_ref, w_ref, o_ref):
        o_ref[...] = _dot(dh_ref[...], w_ref[...], _NT)

    row = pl.BlockSpec((tm, D), lambda i: (i, 0))
    return pl.pallas_call(
        body, grid=(S // tm,), name="dmix", in_specs=[row, pl.BlockSpec((D, D), lambda i: (0, 0))],
        out_specs=row, out_shape=SDS((S, D), F32), compiler_params=_cp(1))(dh1, w_out)


def _in_bwd(dproj, w_blk, x, dh1, w_norm):
    S, D = x.shape
    nblk, _, NB = w_blk.shape
    tm = min(RESIDENT_ROWS, S)

    def body(dp_ref, w_ref, x_ref, dh1_ref, wn_ref, dx_ref, dw_ref):
        @pl.when(pl.program_id(0) == 0)
        def _():
            dw_ref[...] = jnp.zeros_like(dw_ref)

        dn = None
        for j in range(nblk):
            part = _dot(dp_ref[:, pl.ds(j * NB, NB)], w_ref[j], _NT)
            dn = part if dn is None else dn + part
        dh, dw = _rms_bwd(dn, x_ref[...], wn_ref[...])
        dx_ref[...] = dh1_ref[...] + dh
        dw_ref[...] += dw

    row = pl.BlockSpec((tm, D), lambda i: (i, 0))
    vec = pl.BlockSpec((1, D), lambda i: (0, 0))
    return pl.pallas_call(
        body, grid=(S // tm,), name="in_bwd",
        in_specs=[pl.BlockSpec((tm, nblk * NB), lambda i: (i, 0)),
                  pl.BlockSpec((nblk, D, NB), lambda i: (0, 0, 0), pipeline_mode=pl.Buffered(1)), row, row, vec],
        out_specs=[row, vec], out_shape=[SDS((S, D), F32), SDS((1, D), F32)],
        compiler_params=_cp(1))(dproj, w_blk, x, dh1, w_norm)


def _wgrad(a, b, a_spec, b_spec, o_spec, o_shape, grid, name):
    nk = grid[-1]

    def ld(ref):
        return ref[0] if len(ref.shape) == 3 else ref[...]

    def body(a_ref, b_ref, o_ref, acc):
        k = pl.program_id(len(grid) - 1)

        @pl.when(k == 0)
        def _():
            acc[...] = jnp.zeros_like(acc)

        acc[...] += _dot(ld(a_ref), ld(b_ref), _TN)

        @pl.when(k == nk - 1)
        def _():
            if len(o_ref.shape) == 3:
                o_ref[0] = acc[...].astype(o_ref.dtype)
            else:
                o_ref[...] = acc[...].astype(o_ref.dtype)

    return pl.pallas_call(
        body, grid=grid, name=name, in_specs=[a_spec, b_spec], out_specs=o_spec, out_shape=SDS(o_shape, BF16),
        scratch_shapes=[pltpu.VMEM(o_spec.block_shape[-2:], F32)], compiler_params=_cp(len(grid)))(a, b)


def _peer(k):
    x, y, c = lax.axis_index("x"), lax.axis_index("y"), lax.axis_index("c")
    px = 1 - x if k & 4 else x
    py = 1 - y if k & 2 else y
    pc = 1 - c if k & 1 else c
    return (px, py, pc), 4 * px + 2 * py + pc


def _exchange_copies(srcs, lands, send_sems, recv_sems, which, gather):
    _, me = _peer(0)
    pairs = []
    for pos, a in enumerate(which):
        for k in range(1, N_DEV):
            dev, idx = _peer(k)
            sem = pos * (N_DEV - 1) + k - 1
            src = srcs[a] if gather else srcs[a].at[idx]
            mk = functools.partial(pltpu.make_async_remote_copy, src_ref=src, send_sem=send_sems.at[sem],
                                   recv_sem=recv_sems.at[sem], device_id=dev, device_id_type=MESH)
            pairs.append((mk(dst_ref=lands[a].at[me]), mk(dst_ref=lands[a].at[idx])))
    return pairs


def _sequencer_kernel(name, collective_id, n_remote, n_local):
    return pl.kernel(mesh=plsc.ScalarSubcoreMesh(axis_name="sequencer", num_cores=1), name=name,
                     scratch_types=(pltpu.SemaphoreType.DMA((n_remote,)), pltpu.SemaphoreType.DMA((n_remote,)),
                                    pltpu.SemaphoreType.DMA((n_local,))),
                     compiler_params=pltpu.CompilerParams(collective_id=collective_id))


def _handshake(ks):
    barrier = pltpu.get_barrier_semaphore()
    for k in ks:
        pl.semaphore_signal(barrier, inc=1, device_id=_peer(k)[0], device_id_type=MESH)
    pl.semaphore_wait(barrier, len(ks))


def _sequencer_scatter(arrays, name, collective_id):
    n = len(arrays)
    hbm = pltpu.MemorySpace.HBM
    srcs = [jax.new_ref(a, memory_space=hbm) for a in arrays]
    lands = [jax.empty_ref(SDS(a.shape, a.dtype), memory_space=hbm) for a in arrays]

    @_sequencer_kernel(name, collective_id, n * (N_DEV - 1), n)
    def launch(send_sems, recv_sems, local_sems):
        _handshake(range(1, N_DEV))
        _, me = _peer(0)
        local = [pltpu.make_async_copy(srcs[a].at[me], lands[a].at[me], local_sems.at[a]) for a in range(n)]
        pairs = _exchange_copies(srcs, lands, send_sems, recv_sems, range(n), False)
        for out, _ in pairs:
            out.start()
        for cp in local:
            cp.start()
        for out, arrival in pairs:
            out.wait_send()
            arrival.wait_recv()
        for cp in local:
            cp.wait()

    launch()
    return [r[...] for r in lands]


SIBLING = 1
OTHER_CHIPS = (2, 4, 6)


def _sequencer_gather(arrays, name, collective_id):
    n = len(arrays)
    hbm = pltpu.MemorySpace.HBM
    srcs = [jax.new_ref(a, memory_space=hbm) for a in arrays]
    lands = [jax.empty_ref(SDS((N_DEV,) + a.shape, a.dtype), memory_space=hbm) for a in arrays]

    @_sequencer_kernel(name, collective_id, n * (N_DEV - 1), n)
    def launch(send_sems, recv_sems, local_sems):
        _handshake((SIBLING,) + OTHER_CHIPS)
        _, me = _peer(0)
        sibling, _ = _peer(SIBLING)

        def copy(a, k, src, block, to):
            sem = a * (N_DEV - 1) + k - 1
            return pltpu.make_async_remote_copy(src_ref=src, dst_ref=lands[a].at[block], send_sem=send_sems.at[sem],
                                                recv_sem=recv_sems.at[sem], device_id=to, device_id_type=MESH)

        local = [pltpu.make_async_copy(srcs[a], lands[a].at[me], local_sems.at[a]) for a in range(n)]
        first = [copy(a, k, srcs[a], me, _peer(k)[0]) for a in range(n) for k in OTHER_CHIPS + (SIBLING,)]
        for cp in first + local:
            cp.start()
        passed = []
        for a in range(n):
            for k in OTHER_CHIPS:
                _, block = _peer(k)
                copy(a, k, srcs[a], block, sibling).wait_recv()
                passed.append(copy(a, k ^ SIBLING, lands[a].at[block], block, sibling))
                passed[-1].start()
        for a in range(n):
            for k in (SIBLING,) + tuple(k ^ SIBLING for k in OTHER_CHIPS):
                copy(a, k, srcs[a], _peer(k)[1], sibling).wait_recv()
        for cp in first + passed:
            cp.wait_send()
        for cp in local:
            cp.wait()

    launch()
    return [r[...] for r in lands]


def _sequencer_gather_chips(array, name, collective_id, chips):
    hbm = pltpu.MemorySpace.HBM
    src = jax.new_ref(array, memory_space=hbm)
    land = jax.empty_ref(SDS((2 * len(chips),) + array.shape, array.dtype), memory_space=hbm)

    @_sequencer_kernel(name, collective_id, 2 * len(chips), 1)
    def launch(send_sems, recv_sems, local_sems):
        _handshake((SIBLING,) + tuple(k for k in chips if k))
        c = lax.axis_index("c")
        sibling, _ = _peer(SIBLING)

        def copy(sem, src_ref, slot, to):
            return pltpu.make_async_remote_copy(src_ref=src_ref, dst_ref=land.at[slot], send_sem=send_sems.at[sem],
                                                recv_sem=recv_sems.at[sem], device_id=to, device_id_type=MESH)

        started = []
        for pos, k in enumerate(chips):
            started.append(copy(2 * pos, src, 2 * pos + c, _peer(k)[0] if k else sibling))
            started[-1].start()
        for pos, k in enumerate(chips):
            if k:
                copy(2 * pos, src, 2 * pos + c, sibling).wait_recv()
                started.append(copy(2 * pos + 1, land.at[2 * pos + c], 2 * pos + c, sibling))
                started[-1].start()
        for pos, k in enumerate(chips):
            copy(2 * pos + 1 if k else 2 * pos, src, 2 * pos + 1 - c, sibling).wait_recv()
        for cp in started:
            cp.wait_send()

    launch()
    return land[...]


def _sequencer_relay_far(near, name, collective_id):
    hbm = pltpu.MemorySpace.HBM
    src = jax.new_ref(near, memory_space=hbm)
    land = jax.empty_ref(SDS((2,) + near.shape[1:], near.dtype), memory_space=hbm)
    half = near.shape[1] // 2

    @_sequencer_kernel(name, collective_id, 3, 1)
    def launch(send_sems, recv_sems, local_sems):
        _handshake((SIBLING, 4, 2))
        c = lax.axis_index("c")
        sibling, _ = _peer(SIBLING)

        def copy(sem, src_ref, dst_ref, to):
            return pltpu.make_async_remote_copy(src_ref=src_ref, dst_ref=dst_ref, send_sem=send_sems.at[sem],
                                                recv_sem=recv_sems.at[sem], device_id=to, device_id_type=MESH)

        upper, lower = pl.ds(0, half), pl.ds(half, half)
        halves = [copy(0, src.at[c, upper, :], land.at[c, upper, :], _peer(2)[0]),
                  copy(1, src.at[2 + c, lower, :], land.at[c, lower, :], _peer(4)[0])]
        for cp in halves:
            cp.start()
        for cp in halves:
            cp.wait_recv()
        passed = copy(2, land.at[c], land.at[c], sibling)
        passed.start()
        copy(2, land.at[1 - c], land.at[1 - c], sibling).wait_recv()
        for cp in halves + [passed]:
            cp.wait_send()

    launch()
    return land[...]


SMALL_ROWS = 64


def _small_step(part, w, m, v):
    def body(p_ref, w_ref, m_ref, v_ref, g_ref, d_ref, nm_ref, nv_ref, gath, send_sems, recv_sems):
        _, me = _peer(0)
        gath[me] = p_ref[...]
        copies = []
        for k in range(1, N_DEV):
            dev, idx = _peer(k)
            out = pltpu.make_async_remote_copy(src_ref=p_ref, dst_ref=gath.at[me], send_sem=send_sems.at[k - 1],
                                               recv_sem=recv_sems.at[k - 1], device_id=dev, device_id_type=MESH)
            out.start()
            arrival = pltpu.make_async_remote_copy(src_ref=p_ref, dst_ref=gath.at[idx], send_sem=send_sems.at[k - 1],
                                                   recv_sem=recv_sems.at[k - 1], device_id=dev, device_id_type=MESH)
            copies.append((out, arrival))
        for out, arrival in copies:
            out.wait_send()
            arrival.wait_recv()
        g = gath[0]
        for p in range(1, N_DEV):
            g = g + gath[p]
        g_ref[...] = g
        d_ref[...], nm_ref[...], nv_ref[...] = _adamw(w_ref[...], g, m_ref[...], v_ref[...])

    vm = pl.BlockSpec(memory_space=pltpu.VMEM)
    return pl.pallas_call(
        body, name="small_step", in_specs=[vm] * 4, out_specs=[vm] * 4,
        out_shape=[SDS((SMALL_ROWS, 128), F32)] * 4,
        scratch_shapes=[pltpu.VMEM((N_DEV, SMALL_ROWS, 128), F32), pltpu.SemaphoreType.DMA((N_DEV - 1,)),
                        pltpu.SemaphoreType.DMA((N_DEV - 1,))])(part, w, m, v)


def _adamw(w, g, m, v):
    m = ADAM_B1 * m + (1.0 - ADAM_B1) * g
    v = ADAM_B2 * v + (1.0 - ADAM_B2) * (g * g)
    m_hat = m / (1.0 - ADAM_B1 ** ADAM_STEP)
    v_hat = v / (1.0 - ADAM_B2 ** ADAM_STEP)
    delta = -ADAM_LR * (m_hat / (jnp.sqrt(v_hat) + ADAM_EPS) + ADAM_WD * w)
    return delta, m, v


def _adamw_block(parts, w, m, v, name):
    R, C = w.shape
    n_parts = len(parts)
    Rp = R // n_parts
    tr = next(t for t in (256, 128, 64, 32, 16, 8) if Rp % t == 0 and t * C <= 256 * 1024)
    per_part = Rp // tr

    def body(*refs):
        p_refs = refs[:n_parts]
        w_ref, m_ref, v_ref, g_ref, d_ref, nm_ref, nv_ref = refs[n_parts:]
        for k, p_ref in enumerate(p_refs):
            @pl.when(pl.program_id(0) // per_part == k)
            def _(p_ref=p_ref):
                g = p_ref[0].astype(F32)
                for p in range(1, N_DEV):
                    g = g + p_ref[p].astype(F32)
                g_ref[...] = g
                d_ref[...], nm_ref[...], nv_ref[...] = _adamw(w_ref[...], g, m_ref[...], v_ref[...])

    row = pl.BlockSpec((tr, C), lambda i: (i, 0))
    part_specs = [pl.BlockSpec((N_DEV, tr, C), functools.partial(
        lambda i, k: (0, jnp.clip(i - k * per_part, 0, per_part - 1), 0), k=k)) for k in range(n_parts)]
    return pl.pallas_call(
        body, grid=(R // tr,), name=name, in_specs=part_specs + [row, row, row],
        out_specs=[row] * 4, out_shape=[SDS((R, C), F32)] * 4, compiler_params=_cp(1))(*parts, w, m, v)


def _pack_small(mix, ffn, fin, retw, dec_f, dec_b, loss):
    flat = jnp.concatenate([mix.reshape(-1), ffn.reshape(-1), fin.reshape(-1), retw.reshape(-1), dec_f.reshape(-1),
                            dec_b.reshape(-1), loss.reshape(-1)])
    return jnp.pad(flat, (0, SMALL_ROWS * 128 - flat.shape[0])).reshape(SMALL_ROWS, 128)


def _unpack_small(packed, shapes):
    flat = packed.reshape(-1)
    out, at = [], 0
    for s in shapes:
        n = math.prod(s)
        out.append(flat[at:at + n].reshape(s))
        at += n
    return out


def kernel(x, norm_mix_w, w_in, ret_decay_fwd, ret_decay_bwd, ret_norm_w, w_out, norm_ffn_w, w_gate, w_up, w_down, norm_final_w, loss_target, m_norm_mix_w, m_w_in, m_ret_decay_fwd, m_ret_decay_bwd, m_ret_norm_w, m_w_out, m_norm_ffn_w, m_w_gate, m_w_up, m_w_down, m_norm_final_w, v_norm_mix_w, v_w_in, v_ret_decay_fwd, v_ret_decay_bwd, v_ret_norm_w, v_w_out, v_norm_ffn_w, v_w_gate, v_w_up, v_w_down, v_norm_final_w):
    x2 = x[0]
    tgt = loss_target[0]
    S, D = x2.shape
    H = ret_norm_w.shape[1] // HEAD_DIM
    DA = H * HEAD_DIM
    fin_w = norm_final_w.reshape(1, D)
    big = (w_in[0], w_out[0], w_gate[0].T, w_up[0].T, w_down[0])

    big_b = [big[0].astype(BF16)]
    stages = ((0,), (4, 2), (6,))
    wi_stages = [_sequencer_gather_chips(big_b[0], name, cid, ks)
                 for name, cid, ks in zip(("gather_in_own", "gather_in_near"), (0, 7), stages)]
    NB = big_b[0].shape[1]
    ax, ay = lax.axis_index("x"), lax.axis_index("y")
    chip_of = {k: 2 * (1 - ax if k & 4 else ax) + (1 - ay if k & 2 else ay) for k in (0, 2, 4, 6)}

    n1 = _norm_fwd(x2, norm_mix_w)
    ac = lax.axis_index("c")
    me = 2 * chip_of[0] + ac
    vec = lambda *v: jnp.stack([jnp.asarray(t, jnp.int32) for t in v])
    proj = _proj_part(n1, big_b[0][None], vec(0), vec(me), None, N_DEV, "proj_self")
    for ks, name in zip(stages, ("proj_sibling", "proj_near", "proj_far")):
        if name == "proj_near":
            rest, proj = lax.optimization_barrier((tuple(big[1:]), proj))
            big_b = big_b + [w.astype(BF16) for w in rest]
            near, proj, big_b = lax.optimization_barrier((wi_stages[1], proj, big_b))
            wi_stages[1] = near
            wi_stages.append(_sequencer_relay_far(near, "gather_in_far", 8))
            wo, = _sequencer_gather(big_b[1:2], "gather_out", 1)
            wg, wu = _sequencer_gather(big_b[2:4], "gather_gate_up", 9)
            wd, = _sequencer_gather(big_b[4:], "gather_down", 5)
            wi, = _sequencer_gather(big_b[:1], "gather_in_ordered", 10)
        w_st = wi_stages[len(wi_stages) - 1 if name == "proj_far" else ("proj_sibling", "proj_near").index(name)]
        slots, blocks = [], []
        for pos, k in enumerate(ks):
            for core in ((1 - ac,) if k == 0 else (0, 1)):
                slots.append(2 * pos + core)
                blocks.append(2 * chip_of[k] + core)
        proj = _proj_part(n1, w_st, vec(*slots), vec(*blocks), proj, N_DEV, name)
    bias = _attn_bias()[:H]
    attn, lse = _attn_fwd(proj, bias)
    ret, o_raw = _ret_fwd(proj, ret_decay_fwd, ret_decay_bwd, ret_norm_w)
    wo_full = wo.reshape(D, D)
    d_ff = N_DEV * wd.shape[1]
    FB = FFN_BLOCK if d_ff % FFN_BLOCK == 0 else wd.shape[1]
    n_fb = d_ff // FB
    wg, wu = wg.reshape(n_fb, FB, D), wu.reshape(n_fb, FB, D)
    wd_full = wd.reshape(d_ff, D)
    h1, mixed, n2 = _out_fwd(x2, attn, ret, wo_full, norm_ffn_w)
    gate, up, act = _ffn_up(n2, wg, wu)
    dh2, dh2_b, loss_parts, g_fin = _ffn_down_loss(act, wd_full, h1, tgt, fin_w)

    dgate, dup = _ffn_bwd_act(dh2_b, wd_full, gate, up)
    tn = min(1024, D)
    ffn_specs = (pl.BlockSpec((1, S, FB), lambda j, n, k: (j, 0, 0)), pl.BlockSpec((S, tn), lambda j, n, k: (0, n)),
                 pl.BlockSpec((1, FB, tn), lambda j, n, k: (j, 0, n)), (n_fb, FB, D), (n_fb, D // tn, 1))
    per_dev = (N_DEV, d_ff // N_DEV, D)
    g_wd = _wgrad(act, dh2_b, *ffn_specs, "wgrad_down").reshape(per_dev)
    g_wg = _wgrad(dgate, n2, *ffn_specs, "wgrad_gate").reshape(per_dev)
    g_wu = _wgrad(dup, n2, *ffn_specs, "wgrad_up").reshape(per_dev)
    parts_f = _sequencer_scatter([g_wg, g_wu, g_wd], "scatter_ffn", 2)
    dh1, dh1_b, g_ffn = _ffn_bwd_in(dgate, dup, wg, wu, h1, dh2, norm_ffn_w)
    dmix = _dmix(dh1_b, wo_full)
    tmw = min(512, D)
    tk = min(2048, S)
    g_wo = _wgrad(mixed, dh1_b, pl.BlockSpec((tk, tmw), lambda m, k: (k, m)), pl.BlockSpec((tk, D), lambda m, k: (k, 0)),
                  pl.BlockSpec((tmw, D), lambda m, k: (m, 0)), (D, D), (D // tmw, S // tk), "wgrad_out")
    parts_o = _sequencer_scatter([g_wo.reshape(N_DEV, D // N_DEV, D)], "scatter_out", 3)
    d_ret, small_w, dproj = _ret_gate_bwd(proj, o_raw, dmix, ret_norm_w, DA, lax.empty(proj.shape, BF16))
    small, dproj = _ret_bwd(proj, d_ret, ret_decay_fwd, ret_decay_bwd, dproj)
    dproj = _attn_bwd(proj, attn, lse, dmix, bias, dproj)
    half = D // tmw // 2
    parts_i = []
    for part, (name, cid) in enumerate((("in_lo", 4), ("in_hi", 6))):
        g_wi = _wgrad(n1, dproj, pl.BlockSpec((S, tmw), functools.partial(lambda j, m, k, off: (0, m + off), off=part * half)),
                      pl.BlockSpec((S, NB), lambda j, m, k: (0, j)), pl.BlockSpec((1, tmw, NB), lambda j, m, k: (j, m, 0)),
                      (N_DEV, D // 2, NB), (N_DEV, half, 1), "wgrad_" + name)
        parts_i += _sequencer_scatter([g_wi], "scatter_" + name, cid)
    grad_x, g_mix = _in_bwd(dproj, wi, x2, dh1, norm_mix_w)

    big_m = (m_w_in[0], m_w_out[0], m_w_gate[0].T, m_w_up[0].T, m_w_down[0])
    big_v = (v_w_in[0], v_w_out[0], v_w_gate[0].T, v_w_up[0].T, v_w_down[0])
    names = ("adamw_in", "adamw_out", "adamw_gate", "adamw_up", "adamw_down")
    upd = [None] * 5
    for a, p in zip((2, 3, 4, 1, 0), [[t] for t in parts_f + parts_o] + [parts_i]):
        upd[a] = _adamw_block(p, big[a], big_m[a], big_v[a], names[a])

    g_dec_f = small[:, 0, 0].reshape(1, H)
    g_dec_b = small[:, 1, 0].reshape(1, H)
    g_retw = small_w[:, 0, :].reshape(1, DA)
    loss_local = jnp.sum(loss_parts[::8, 0])
    zero = jnp.zeros((1,), F32)
    part = _pack_small(g_mix, g_ffn, g_fin, g_retw, g_dec_f, g_dec_b, loss_local)
    sw = _pack_small(norm_mix_w, norm_ffn_w, norm_final_w, ret_norm_w, ret_decay_fwd, ret_decay_bwd, zero)
    sm = _pack_small(m_norm_mix_w, m_norm_ffn_w, m_norm_final_w, m_ret_norm_w, m_ret_decay_fwd, m_ret_decay_bwd, zero)
    sv = _pack_small(v_norm_mix_w, v_norm_ffn_w, v_norm_final_w, v_ret_norm_w, v_ret_decay_fwd, v_ret_decay_bwd, zero)
    shapes = [(1, D), (1, D), (D,), (1, DA), (1, H), (1, H), ()]
    sg, sd, snm, snv = [_unpack_small(t, shapes) for t in _small_step(part, sw, sm, sv)]
    loss = sg[6]

    def ordered(small_set, k):
        b = [(u[k].T if a in (2, 3) else u[k])[None] for a, u in enumerate(upd)]
        return [small_set[0], b[0], small_set[4], small_set[5], small_set[3], b[1], small_set[1], b[2], b[3], b[4],
                small_set[2]]

    return (loss, grad_x[None], *ordered(sg, 0), *ordered(sd, 1), *ordered(snm, 2), *ordered(snv, 3))
```

```python
import functools
import math

import numpy as np
import jax
import jax.numpy as jnp
from jax import lax
from jax.experimental import pallas as pl
from jax.experimental.pallas import tpu as pltpu
from jax.experimental.pallas import tpu_sc as plsc

F32 = jnp.float32
BF16 = jnp.bfloat16
SDS = jax.ShapeDtypeStruct

HEAD_DIM = 128
EPS = 1e-6
RET_CHUNK = 128
DILATIONS = (1, 4, 16)
BAND = 64
Q_TILE = 128
K_TILE = Q_TILE + 2 * BAND
KV_PAD = BAND * 4
TILE_GROUP = 8
BWD_TILE_GROUP = 8
NEG = -1e30
N_DEV = 8
N_GROUPS = 7
ADAM_LR, ADAM_B1, ADAM_B2, ADAM_EPS, ADAM_WD, ADAM_STEP = 0.001, 0.9, 0.999, 1e-08, 0.01, 10
VMEM_LIMIT = 56 * 1024 * 1024
MESH = pl.DeviceIdType.MESH
ANY = pl.BlockSpec(memory_space=pl.ANY)


def _cp(n_grid):
    return pltpu.CompilerParams(dimension_semantics=("arbitrary",) * n_grid, vmem_limit_bytes=VMEM_LIMIT)


def _sigmoid(x):
    return 1.0 / (1.0 + jnp.exp(-x))


def _rms_scale(h):
    return lax.rsqrt(jnp.mean(h * h, axis=-1, keepdims=True) + EPS)


def _rms_bwd(dn, h, w):
    r = _rms_scale(h)
    gw = dn * w
    dh = r * gw - h * (r * r * r) * jnp.mean(gw * h, axis=-1, keepdims=True)
    return dh, jnp.sum(dn * h * r, axis=0, keepdims=True)


def _dot(a, b, dims):
    return lax.dot_general(a.astype(BF16), b.astype(BF16), (dims, ((), ())), preferred_element_type=F32)


_NN = ((1,), (0,))
_NT = ((1,), (1,))
_TN = ((0,), (0,))


RESIDENT_ROWS = 256


def _resident(shape):
    return pl.BlockSpec(shape, lambda i: (0, 0), pipeline_mode=pl.Buffered(1))


def _blocked_matmul(a_ref, w_ref):
    nblk, _, fb = a_ref.shape
    out = None
    for j in range(nblk):
        part = jnp.dot(a_ref[j], w_ref[pl.ds(j * fb, fb), :], preferred_element_type=F32)
        out = part if out is None else out + part
    return out


def _norm_fwd(x, w_norm):
    S, D = x.shape
    tm = min(1024, S)

    def body(x_ref, wn_ref, n_ref):
        xf = x_ref[...]
        n_ref[...] = (xf * _rms_scale(xf) * wn_ref[...]).astype(BF16)

    row = pl.BlockSpec((tm, D), lambda i: (i, 0))
    return pl.pallas_call(body, grid=(S // tm,), name="norm_fwd", in_specs=[row, pl.BlockSpec((1, D), lambda i: (0, 0))],
                          out_specs=row, out_shape=SDS((S, D), BF16), compiler_params=_cp(1))(x, w_norm)


def _proj_part(n1, w_slots, slots, blocks, proj, n_blocks, name):
    S, D = n1.shape
    NB = w_slots.shape[2]
    tm = min(1024, S)

    def body(slots_ref, blocks_ref, n_ref, w_ref, *rest):
        rest[-1][...] = jnp.dot(n_ref[...], w_ref[0], preferred_element_type=F32)

    out_spec = pl.BlockSpec((tm, NB), lambda i, j, slots, blocks: (i, blocks[j]))
    in_specs = [pl.BlockSpec((tm, D), lambda i, j, slots, blocks: (i, 0)),
                pl.BlockSpec((1, D, NB), lambda i, j, slots, blocks: (slots[j], 0, 0))]
    args = [n1, w_slots]
    if proj is not None:
        in_specs.append(ANY)
        args.append(proj)
    return pl.pallas_call(
        body, name=name, out_shape=SDS((S, n_blocks * NB), F32),
        grid_spec=pltpu.PrefetchScalarGridSpec(num_scalar_prefetch=2, grid=(S // tm, slots.shape[0]), in_specs=in_specs,
                                               out_specs=out_spec),
        input_output_aliases={} if proj is None else {4: 0},
        compiler_params=_cp(2))(slots, blocks, *args)


def _out_fwd(x, attn, ret, w_out, w_norm):
    S, D = x.shape
    DA = attn.shape[1]
    tm = min(512, S)

    def body(x_ref, a_ref, r_ref, w_ref, wn_ref, h_ref, mix_ref, n_ref):
        a = a_ref[...].astype(BF16)
        r = r_ref[...].astype(BF16)
        mix_ref[:, :DA] = a
        mix_ref[:, DA:] = r
        h = x_ref[...] + jnp.dot(a, w_ref[:DA, :], preferred_element_type=F32) \
            + jnp.dot(r, w_ref[DA:, :], preferred_element_type=F32)
        h_ref[...] = h
        n_ref[...] = (h * _rms_scale(h) * wn_ref[...]).astype(BF16)

    row = lambda w: pl.BlockSpec((tm, w), lambda i: (i, 0))
    return pl.pallas_call(
        body, grid=(S // tm,), name="out_fwd",
        in_specs=[row(D), row(DA), row(D - DA), pl.BlockSpec((D, D), lambda i: (0, 0)),
                  pl.BlockSpec((1, D), lambda i: (0, 0))],
        out_specs=[row(D), row(D), row(D)],
        out_shape=[SDS((S, D), F32), SDS((S, D), BF16), SDS((S, D), BF16)],
        compiler_params=_cp(1))(x, attn, ret, w_out, w_norm)


def _ffn_up(n2, wg, wu):
    S, D = n2.shape
    nblk, FB, _ = wg.shape
    tm = min(1024, S)

    def body(n_ref, wg_ref, wu_ref, g_ref, u_ref, a_ref):
        n = n_ref[...]
        g = _dot(n, wg_ref[0], _NT)
        u = _dot(n, wu_ref[0], _NT)
        g_ref[0] = g.astype(BF16)
        u_ref[0] = u.astype(BF16)
        a_ref[0] = (g * _sigmoid(g) * u).astype(BF16)

    wspec = pl.BlockSpec((1, FB, D), lambda j, i: (j, 0, 0))
    ospec = pl.BlockSpec((1, tm, FB), lambda j, i: (j, i, 0))
    return pl.pallas_call(
        body, grid=(nblk, S // tm), name="ffn_up",
        in_specs=[pl.BlockSpec((tm, D), lambda j, i: (i, 0)), wspec, wspec],
        out_specs=[ospec, ospec, ospec],
        out_shape=[SDS((nblk, S, FB), BF16)] * 3,
        compiler_params=_cp(2))(n2, wg, wu)


def _ffn_down_loss(act, wd, h1, target, w_norm):
    nblk, S, FB = act.shape
    D = h1.shape[1]
    tm = min(RESIDENT_ROWS, S)

    def body(a_ref, wd_ref, h_ref, t_ref, wn_ref, dh_ref, dhb_ref, loss_ref, dw_ref):
        @pl.when(pl.program_id(0) == 0)
        def _():
            dw_ref[...] = jnp.zeros_like(dw_ref)

        h = h_ref[...] + _blocked_matmul(a_ref, wd_ref)
        w = wn_ref[...]
        err = h * _rms_scale(h) * w - t_ref[...]
        loss_ref[...] = jnp.full(loss_ref.shape, 0.5 * jnp.sum(err * err) / D, F32)
        dh, dw = _rms_bwd(err * (1.0 / D), h, w)
        dh_ref[...] = dh
        dhb_ref[...] = dh.astype(BF16)
        dw_ref[...] += dw

    row = pl.BlockSpec((tm, D), lambda i: (i, 0))
    vec = pl.BlockSpec((1, D), lambda i: (0, 0))
    return pl.pallas_call(
        body, grid=(S // tm,), name="ffn_down_loss",
        in_specs=[pl.BlockSpec((nblk, tm, FB), lambda i: (0, i, 0)), _resident((nblk * FB, D)), row, row, vec],
        out_specs=[row, row, pl.BlockSpec((8, 128), lambda i: (i, 0)), vec],
        out_shape=[SDS((S, D), F32), SDS((S, D), BF16), SDS((S // tm * 8, 128), F32), SDS((1, D), F32)],
        compiler_params=_cp(1))(act, wd, h1, target, w_norm)


def _attn_bias():
    n_heads = 8
    slopes = np.exp2(-8.0 * np.arange(1, n_heads + 1, dtype=np.float32) / n_heads)
    dist = np.abs(np.arange(K_TILE)[None, :] - BAND - np.arange(Q_TILE)[:, None])
    out = np.empty((n_heads, len(DILATIONS), Q_TILE, K_TILE), np.float32)
    for h in range(n_heads):
        for p, d in enumerate(DILATIONS):
            out[h, p] = np.where(dist <= BAND, -slopes[h] * (d * dist).astype(np.float32), NEG)
    return jnp.asarray(out)


def _attn_tiles(S, d):
    L = S // d
    per_class = L // Q_TILE
    return L, per_class, d * per_class


def _tile_rows(t, d, per_class):
    r = t // per_class
    a = (t % per_class) * Q_TILE
    q_rows = pl.ds(r + d * a, Q_TILE, stride=d) if d > 1 else pl.ds(pl.multiple_of(a, Q_TILE), Q_TILE)
    k_rows = pl.ds(KV_PAD + r + d * (a - BAND), K_TILE, stride=d) if d > 1 else pl.ds(
        pl.multiple_of(KV_PAD + a - BAND, BAND), K_TILE)
    return a, q_rows, k_rows


def _to_quarters(dst, src, n, dst_off=0):
    for r in range(4):
        dst[pl.ds(dst_off + r * (n // 4), n // 4), :] = src[pl.ds(r, n // 4, stride=4), :]


def _quarter_tile_rows(t, S):
    L = S // 16
    per_class = L // Q_TILE
    blk, tt = t // (4 * per_class), t % (4 * per_class)
    r, a = tt // per_class, (tt % per_class) * Q_TILE
    q_rows = pl.ds(blk * (S // 4) + r + 4 * a, Q_TILE, stride=4)
    k_rows = pl.ds(KV_PAD + blk * (S // 4) + r + 4 * (a - BAND), K_TILE, stride=4)
    return a, q_rows, k_rows


def _quarter_band_rows(t, S):
    L = S // 4
    per_quarter = L // Q_TILE
    blk, a = t // per_quarter, (t % per_quarter) * Q_TILE
    q_rows = pl.ds(pl.multiple_of(blk * L + a, Q_TILE), Q_TILE)
    k_rows = pl.ds(pl.multiple_of(KV_PAD + blk * L + a - BAND, BAND), K_TILE)
    return a, q_rows, k_rows


def _lanes(x, width):
    return jnp.concatenate([x] * (width // HEAD_DIM), axis=-1)


_BNT = (((2,), (2,)), ((0,), (0,)))
_BNN = (((2,), (1,)), ((0,), (0,)))
_BTN = (((1,), (1,)), ((0,), (0,)))


def _bdot(a, b, dims):
    return lax.dot_general(a, b, dims, preferred_element_type=F32)


def _stacked(rows, loaders):
    return [jnp.stack([f(*r) for r in rows]) for f in loaders]


def _edge_mask(a, L):
    lk = lax.broadcasted_iota(jnp.int32, (1, K_TILE), 1) + (a - BAND)
    return jnp.where((lk >= 0) & (lk < L), 0.0, NEG).astype(F32)


def _fill_padded(dst, src, S):
    dst[pl.ds(0, KV_PAD), :] = jnp.zeros((KV_PAD, HEAD_DIM), F32)
    dst[pl.ds(KV_PAD + S, KV_PAD), :] = jnp.zeros((KV_PAD, HEAD_DIM), F32)
    dst[pl.ds(KV_PAD, S), :] = src[...]


def _head_specs(S, groups, n_heads):
    return [pl.BlockSpec((S, HEAD_DIM), functools.partial(lambda h, g: (0, g * n_heads + h), g=g)) for g in groups]


def _attn_fwd(proj, bias):
    S = proj.shape[0]
    H = proj.shape[1] // (N_GROUPS * HEAD_DIM)
    scale = HEAD_DIM ** -0.5

    def body(q_ref, k_ref, v_ref, b_ref, o_ref, lse_ref, kp, vp, m_run, l_run, q4, m3, l3, acc3):
        _fill_padded(kp, k_ref, S)
        _fill_padded(vp, v_ref, S)
        o_ref[...] = jnp.zeros_like(o_ref)
        m_run[...] = jnp.full(m_run.shape, NEG, F32)
        l_run[...] = jnp.zeros_like(l_run)
        def online(n_tiles, tile_rows, p, L, q_src, m_buf, l_buf, o_buf):
            def tiles(t, carry):
                rows = [tile_rows(t + u * (n_tiles // TILE_GROUP)) for u in range(TILE_GROUP)]
                qs, ks, vs, m_old, l_old, o_old, edge = _stacked(rows, (
                    lambda a, qr, kr: q_src[qr, :].astype(BF16), lambda a, qr, kr: kp[kr, :].astype(BF16),
                    lambda a, qr, kr: vp[kr, :].astype(BF16), lambda a, qr, kr: m_buf[qr, :],
                    lambda a, qr, kr: l_buf[qr, :], lambda a, qr, kr: o_buf[qr, :], lambda a, qr, kr: _edge_mask(a, L)))
                s = _bdot(qs, ks, _BNT) * scale + b_ref[0, p][None] + edge
                m_new = jnp.maximum(m_old, jnp.max(s, axis=-1, keepdims=True))
                pr = jnp.exp(s - _lanes(m_new, K_TILE)).astype(BF16)
                alpha = jnp.exp(m_old - m_new)
                l_new = alpha * l_old + _bdot(pr, jnp.ones((TILE_GROUP, K_TILE, HEAD_DIM), BF16), _BNN)
                o_new = alpha * o_old + _bdot(pr, vs, _BNN)
                for u, (_, qr, _) in enumerate(rows):
                    o_buf[qr, :] = o_new[u]
                    m_buf[qr, :] = m_new[u]
                    l_buf[qr, :] = l_new[u]
                return carry

            lax.fori_loop(0, n_tiles // TILE_GROUP, tiles, 0)

        L, per_class, n_tiles = _attn_tiles(S, DILATIONS[0])
        online(n_tiles, functools.partial(_tile_rows, d=DILATIONS[0], per_class=per_class), 0, L, q_ref, m_run, l_run, o_ref)

        _to_quarters(q4, q_ref, S)
        _to_quarters(kp, k_ref, S, KV_PAD)
        _to_quarters(vp, v_ref, S, KV_PAD)
        n_tiles = _attn_tiles(S, DILATIONS[2])[2]

        def tiles3(t, carry):
            rows = [_quarter_tile_rows(t + u * (n_tiles // TILE_GROUP), S) for u in range(TILE_GROUP)]
            qs, ks, vs, edge = _stacked(rows, (
                lambda a, qr, kr: q4[qr, :].astype(BF16), lambda a, qr, kr: kp[kr, :].astype(BF16),
                lambda a, qr, kr: vp[kr, :].astype(BF16), lambda a, qr, kr: _edge_mask(a, S // DILATIONS[2])))
            s = _bdot(qs, ks, _BNT) * scale + b_ref[0, 2][None] + edge
            m_new = jnp.broadcast_to(jnp.max(s, axis=-1, keepdims=True), (TILE_GROUP, Q_TILE, HEAD_DIM))
            pr = jnp.exp(s - _lanes(m_new, K_TILE)).astype(BF16)
            l_new = _bdot(pr, jnp.ones((TILE_GROUP, K_TILE, HEAD_DIM), BF16), _BNN)
            o_new = _bdot(pr, vs, _BNN)
            for u, (_, qr, _) in enumerate(rows):
                acc3[qr, :] = o_new[u]
                m3[qr, :] = m_new[u]
                l3[qr, :] = l_new[u]
            return carry

        lax.fori_loop(0, n_tiles // TILE_GROUP, tiles3, 0)
        online(_attn_tiles(S, DILATIONS[1])[2], functools.partial(_quarter_band_rows, S=S), 1, S // DILATIONS[1],
               q4, m3, l3, acc3)
        for r in range(4):
            nat, qtr = pl.ds(r, S // 4, stride=4), pl.ds(r * (S // 4), S // 4)
            m_a, m_b = m_run[nat, :], m3[qtr, :]
            m = jnp.maximum(m_a, m_b)
            w_a, w_b = jnp.exp(m_a - m), jnp.exp(m_b - m)
            l = w_a * l_run[nat, :] + w_b * l3[qtr, :]
            o_ref[nat, :] = (w_a * o_ref[nat, :] + w_b * acc3[qtr, :]) / l
            lse_ref[nat, :] = m + jnp.log(l)

    hspec = pl.BlockSpec((S, HEAD_DIM), lambda h: (0, h))
    padded, plain = pltpu.VMEM((S + 2 * KV_PAD, HEAD_DIM), F32), pltpu.VMEM((S, HEAD_DIM), F32)
    return pl.pallas_call(
        body, grid=(H,), name="attn_fwd",
        in_specs=_head_specs(S, (0, 1, 2), H) + [
            pl.BlockSpec((1, len(DILATIONS), Q_TILE, K_TILE), lambda h: (h, 0, 0, 0))],
        out_specs=[hspec, hspec],
        out_shape=[SDS((S, H * HEAD_DIM), F32), SDS((S, H * HEAD_DIM), F32)],
        scratch_shapes=[padded, padded] + [plain] * 6,
        compiler_params=_cp(1))(proj, proj, proj, bias)


def _put_groups(stage, dproj, groups, n_heads, sems):
    h = pl.program_id(0)
    copies = [pltpu.make_async_copy(
        stage.at[i], dproj.at[:, pl.ds(pl.multiple_of((g * n_heads + h) * HEAD_DIM, HEAD_DIM), HEAD_DIM)], sems.at[i])
        for i, g in enumerate(groups)]
    for cp in copies:
        cp.start()
    for cp in copies:
        cp.wait()


def _attn_bwd(proj, out, lse, dmix, bias, dproj):
    S = proj.shape[0]
    H = proj.shape[1] // (N_GROUPS * HEAD_DIM)
    scale = HEAD_DIM ** -0.5
    assert S // DILATIONS[2] >= 2 * Q_TILE

    def body(q_ref, k_ref, v_ref, o_ref, lse_ref, do_ref, b_ref, dproj_in, dproj_out,
             kp, vp, dkp, dvp, dsum, q4, do4, lse4, dsum4, dq_ref, dk_ref, dv_ref, stage, sems):
        _fill_padded(kp, k_ref, S)
        _fill_padded(vp, v_ref, S)
        dkp[...] = jnp.zeros_like(dkp)
        dvp[...] = jnp.zeros_like(dvp)
        dq_ref[...] = jnp.zeros_like(dq_ref)
        dsum[...] = jnp.broadcast_to(jnp.sum(do_ref[...] * o_ref[...], axis=-1, keepdims=True), dsum.shape)

        def run(n_tiles, tile_rows, p, L, q_src, do_src, lse_src, dsum_src, dq_dst, dq_adds):
            def tiles(t, carry):
                rows = [tile_rows(t + u * (n_tiles // BWD_TILE_GROUP)) for u in range(BWD_TILE_GROUP)]
                qs, ks, vs, dos, lses, dsums, dk_old, dv_old, edge = _stacked(rows, (
                    lambda a, qr, kr: q_src[qr, :].astype(BF16), lambda a, qr, kr: kp[kr, :].astype(BF16),
                    lambda a, qr, kr: vp[kr, :].astype(BF16), lambda a, qr, kr: do_src[qr, :].astype(BF16),
                    lambda a, qr, kr: lse_src[qr, :], lambda a, qr, kr: dsum_src[qr, :],
                    lambda a, qr, kr: dkp[kr, :], lambda a, qr, kr: dvp[kr, :], lambda a, qr, kr: _edge_mask(a, L)))
                s = _bdot(qs, ks, _BNT) * scale + b_ref[0, p][None] + edge
                pr = jnp.exp(s - _lanes(lses, K_TILE))
                ds = (pr * (_bdot(dos, vs, _BNT) - _lanes(dsums, K_TILE)) * scale).astype(BF16)
                dq_new = _bdot(ds, ks, _BNN)
                if dq_adds:
                    dq_new = dq_new + jnp.stack([dq_dst[qr, :] for _, qr, _ in rows])
                dk_new = dk_old + _bdot(ds, qs, _BTN)
                dv_new = dv_old + _bdot(pr.astype(BF16), dos, _BTN)
                for u, (_, qr, kr) in enumerate(rows):
                    dq_dst[qr, :] = dq_new[u]
                    dkp[kr, :] = dk_new[u]
                    dvp[kr, :] = dv_new[u]
                return carry

            lax.fori_loop(0, n_tiles // BWD_TILE_GROUP, tiles, 0)

        L, per_class, n_tiles = _attn_tiles(S, DILATIONS[0])
        run(n_tiles, functools.partial(_tile_rows, d=DILATIONS[0], per_class=per_class), 0, L,
            q_ref, do_ref, lse_ref, dsum, dq_ref, True)
        dk_ref[...] = dkp[pl.ds(KV_PAD, S), :]
        dv_ref[...] = dvp[pl.ds(KV_PAD, S), :]

        for dst, src in ((q4, q_ref), (do4, do_ref), (lse4, lse_ref), (dsum4, dsum)):
            _to_quarters(dst, src, S)
        _to_quarters(kp, k_ref, S, KV_PAD)
        _to_quarters(vp, v_ref, S, KV_PAD)
        dkp[...] = jnp.zeros_like(dkp)
        dvp[...] = jnp.zeros_like(dvp)
        dq3 = dsum
        run(_attn_tiles(S, DILATIONS[2])[2], functools.partial(_quarter_tile_rows, S=S), 2, S // DILATIONS[2],
            q4, do4, lse4, dsum4, dq3, False)
        run(_attn_tiles(S, DILATIONS[1])[2], functools.partial(_quarter_band_rows, S=S), 1, S // DILATIONS[1],
            q4, do4, lse4, dsum4, dq3, True)
        for r in range(4):
            nat, qtr = pl.ds(r, S // 4, stride=4), pl.ds(r * (S // 4), S // 4)
            pad_qtr = pl.ds(KV_PAD + r * (S // 4), S // 4)
            dq_ref[nat, :] = dq_ref[nat, :] + dq3[qtr, :]
            dk_ref[nat, :] = dk_ref[nat, :] + dkp[pad_qtr, :]
            dv_ref[nat, :] = dv_ref[nat, :] + dvp[pad_qtr, :]
        for i, acc in enumerate((dq_ref, dk_ref, dv_ref)):
            stage[i] = acc[...].astype(BF16)
        _put_groups(stage, dproj_out, (0, 1, 2), H, sems)

    hspec = pl.BlockSpec((S, HEAD_DIM), lambda h: (0, h))
    once = pl.BlockSpec((S, HEAD_DIM), lambda h: (0, h), pipeline_mode=pl.Buffered(1))
    padded, plain = pltpu.VMEM((S + 2 * KV_PAD, HEAD_DIM), F32), pltpu.VMEM((S, HEAD_DIM), F32)
    return pl.pallas_call(
        body, grid=(H,), name="attn_bwd",
        in_specs=_head_specs(S, (0, 1, 2), H) + [
            once, hspec, hspec, pl.BlockSpec((1, len(DILATIONS), Q_TILE, K_TILE), lambda h: (h, 0, 0, 0)), ANY],
        out_specs=ANY, out_shape=SDS(dproj.shape, dproj.dtype), input_output_aliases={7: 0},
        scratch_shapes=[padded] * 4 + [plain] * 8 + [pltpu.VMEM((3, S, HEAD_DIM), BF16), pltpu.SemaphoreType.DMA((3,))],
        compiler_params=_cp(1))(proj, proj, proj, out, lse, dmix, bias, dproj)


def _ret_consts(lg, forward):
    C = RET_CHUNK
    i = lax.broadcasted_iota(jnp.int32, (C, C), 0)
    j = lax.broadcasted_iota(jnp.int32, (C, C), 1)
    rel = (i - j) if forward else (j - i)
    inside = (rel >= 0) if forward else (rel > 0)
    relf = jnp.maximum(rel, 0).astype(F32)
    mask = jnp.where(inside, jnp.exp(lg * relf), 0.0)
    idx = lax.broadcasted_iota(jnp.int32, (C, 1), 0).astype(F32)
    q_exp = (idx + 1.0) if forward else (C - idx)
    k_exp = (C - 1.0 - idx) if forward else idx
    return mask, relf, jnp.exp(lg * q_exp), q_exp, jnp.exp(lg * k_exp), k_exp, jnp.exp(lg * C)


def _log_decay(dec_ref, h):
    return -jnp.exp(jnp.full((1, 1), dec_ref[0, h], F32))


FFN_BLOCK = 704
CHUNK_BATCH = 16


def _batch_rows(b):
    n = CHUNK_BATCH * RET_CHUNK
    return pl.ds(pl.multiple_of(b * n, n), n)


def _batch_chunks(b):
    return pl.ds(pl.multiple_of(b * CHUNK_BATCH, CHUNK_BATCH), CHUNK_BATCH)


def _chunks3(x):
    return x.reshape(CHUNK_BATCH, RET_CHUNK, HEAD_DIM)


def _ret_scan(buf, c_decs, nc, reverse):
    def step(n, carry):
        new = []
        for way, r in enumerate(carry):
            c = n if (way == 0) != reverse else nc - 1 - n
            term = buf[way, c]
            buf[way, c] = r
            new.append(r * c_decs[way] + term)
        return tuple(new)

    lax.fori_loop(0, nc, step, (jnp.zeros((HEAD_DIM, HEAD_DIM), F32),) * 2)


def _ret_fwd(proj, dec_f, dec_b, w_norm):
    S = proj.shape[0]
    H = proj.shape[1] // (N_GROUPS * HEAD_DIM)
    nc = S // RET_CHUNK
    scale = HEAD_DIM ** -0.5

    def body(df_ref, db_ref, q_ref, k_ref, v_ref, g_ref, w_ref, y_ref, o_ref, states):
        h = pl.program_id(0)
        consts = [_ret_consts(_log_decay(dref, h), fw) for fw, dref in ((True, df_ref), (False, db_ref))]

        def kv_step(b, carry):
            rows, batch = _batch_rows(b), _batch_chunks(b)
            k3 = _chunks3(k_ref[rows, :])
            v3 = _chunks3(v_ref[rows, :]).astype(BF16)
            for way in range(2):
                states[way, batch] = _bdot((k3 * consts[way][4]).astype(BF16), v3, _BTN)
            return carry

        lax.fori_loop(0, nc // CHUNK_BATCH, kv_step, 0)
        _ret_scan(states, [c[6] for c in consts], nc, False)

        def out_step(b, carry):
            rows, batch = _batch_rows(b), _batch_chunks(b)
            q3 = _chunks3(q_ref[rows, :] * scale)
            k3 = _chunks3(k_ref[rows, :]).astype(BF16)
            v3 = _chunks3(v_ref[rows, :]).astype(BF16)
            a0 = _bdot(q3.astype(BF16), k3, _BNT)
            o = None
            for way in range(2):
                mask, q_dec = consts[way][0], consts[way][2]
                part = _bdot((a0 * mask).astype(BF16), v3, _BNN) \
                    + _bdot((q3 * q_dec).astype(BF16), states[way, batch].astype(BF16), _BNN)
                o = part if o is None else o + part
            o_ref[rows, :] = o.reshape(CHUNK_BATCH * RET_CHUNK, HEAD_DIM)
            return carry

        lax.fori_loop(0, nc // CHUNK_BATCH, out_step, 0)
        o = o_ref[...]
        g = g_ref[...]
        y_ref[...] = o * _rms_scale(o) * w_ref[...] * (g * _sigmoid(g))

    hspec = pl.BlockSpec((S, HEAD_DIM), lambda h: (0, h))
    smem = pl.BlockSpec(memory_space=pltpu.SMEM)
    return pl.pallas_call(
        body, grid=(H,), name="ret_fwd",
        in_specs=[smem, smem] + _head_specs(S, (3, 4, 5, 6), H) + [pl.BlockSpec((1, HEAD_DIM), lambda h: (0, h))],
        out_specs=[hspec, hspec],
        out_shape=[SDS((S, H * HEAD_DIM), F32)] * 2,
        scratch_shapes=[pltpu.VMEM((2, nc, HEAD_DIM, HEAD_DIM), F32)],
        compiler_params=_cp(1))(dec_f, dec_b, proj, proj, proj, proj, w_norm)


def _ret_gate_bwd(proj, o_raw, dmix, w_norm, col0, dproj):
    S = proj.shape[0]
    H = proj.shape[1] // (N_GROUPS * HEAD_DIM)

    def body(g_ref, o_ref, dy_ref, w_ref, dproj_in, do_ref, dw_ref, dproj_out, dg_ref, sems):
        o = o_ref[...]
        g = g_ref[...]
        dy = dy_ref[...]
        w = w_ref[...]
        rr = _rms_scale(o)
        normed = o * rr
        sg = _sigmoid(g)
        silu = g * sg
        dw_ref[0] = jnp.broadcast_to(jnp.sum(dy * normed * silu, axis=0, keepdims=True), (8, HEAD_DIM))
        dg_ref[0] = (dy * normed * w * (sg * (1.0 + g * (1.0 - sg)))).astype(BF16)
        dnormed = dy * w * silu
        do_ref[...] = rr * dnormed - o * (rr * rr * rr) * jnp.mean(dnormed * o, axis=-1, keepdims=True)
        _put_groups(dg_ref, dproj_out, (6,), H, sems)

    hspec = pl.BlockSpec((S, HEAD_DIM), lambda h: (0, h))
    nh0 = col0 // HEAD_DIM
    return pl.pallas_call(
        body, grid=(H,), name="ret_gate_bwd",
        in_specs=_head_specs(S, (6,), H) + [hspec, pl.BlockSpec((S, HEAD_DIM), lambda h: (0, nh0 + h)),
                                            pl.BlockSpec((1, HEAD_DIM), lambda h: (0, h)), ANY],
        out_specs=[hspec, pl.BlockSpec((1, 8, HEAD_DIM), lambda h: (h, 0, 0)), ANY],
        out_shape=[SDS((S, H * HEAD_DIM), F32), SDS((H, 8, HEAD_DIM), F32), SDS(dproj.shape, dproj.dtype)],
        input_output_aliases={4: 2},
        scratch_shapes=[pltpu.VMEM((1, S, HEAD_DIM), BF16), pltpu.SemaphoreType.DMA((1,))],
        compiler_params=_cp(1))(proj, o_raw, dmix, w_norm, dproj)


def _ret_bwd(proj, d_out, dec_f, dec_b, dproj):
    S = proj.shape[0]
    H = proj.shape[1] // (N_GROUPS * HEAD_DIM)
    C = RET_CHUNK
    nc = S // C
    scale = HEAD_DIM ** -0.5

    def body(df_ref, db_ref, q_ref, k_ref, v_ref, do, dproj_in, small_ref, dproj_out, states, d_states, stage, sems):
        h = pl.program_id(0)
        lgs = [_log_decay(df_ref, h), _log_decay(db_ref, h)]
        consts = [_ret_consts(lg, fw) for lg, fw in zip(lgs, (True, False))]

        def prep_step(b, carry):
            rows, batch = _batch_rows(b), _batch_chunks(b)
            q3 = _chunks3(q_ref[rows, :] * scale)
            k3 = _chunks3(k_ref[rows, :])
            v3 = _chunks3(v_ref[rows, :]).astype(BF16)
            do3 = _chunks3(do[rows, :]).astype(BF16)
            for way in range(2):
                states[way, batch] = _bdot((k3 * consts[way][4]).astype(BF16), v3, _BTN)
                d_states[way, batch] = _bdot((q3 * consts[way][2]).astype(BF16), do3, _BTN)
            return carry

        lax.fori_loop(0, nc // CHUNK_BATCH, prep_step, 0)
        c_decs = [c[6] for c in consts]
        _ret_scan(states, c_decs, nc, False)
        _ret_scan(d_states, c_decs, nc, True)

        def main_step(b, dlams):
            rows, batch = _batch_rows(b), _batch_chunks(b)
            q3 = _chunks3(q_ref[rows, :] * scale)
            k3 = _chunks3(k_ref[rows, :])
            q3b, k3b = q3.astype(BF16), k3.astype(BF16)
            v3b = _chunks3(v_ref[rows, :]).astype(BF16)
            do3b = _chunks3(do[rows, :]).astype(BF16)
            a0 = _bdot(q3b, k3b, _BNT)
            pv = _bdot(do3b, v3b, _BNT)
            dq = dk = dv = None
            new_dlams = []
            for way in range(2):
                mask, relf, q_dec, q_exp, k_dec, k_exp, c_dec = consts[way]
                state, d_state = states[way, batch], d_states[way, batch]
                dp = pv * mask
                dpb = dp.astype(BF16)
                gq = _bdot(do3b, state.astype(BF16), _BNT)
                gk = _bdot(v3b, d_state.astype(BF16), _BNT)
                parts = (_bdot(dpb, k3b, _BNN) + q_dec * gq, _bdot(dpb, q3b, _BTN) + k_dec * gk,
                         _bdot((a0 * mask).astype(BF16), do3b, _BTN)
                         + _bdot((k3 * k_dec).astype(BF16), d_state.astype(BF16), _BNN))
                dq, dk, dv = parts if dq is None else (dq + parts[0], dk + parts[1], dv + parts[2])
                total = lambda x: jnp.sum(jnp.sum(x, axis=0), axis=0, keepdims=True)
                new_dlams.append(dlams[way] + total(relf * a0 * dp)
                                 + total(q_exp * q_dec * q3 * gq + k_exp * k_dec * k3 * gk)
                                 + (C * c_dec) * total(state * d_state))
            flat = lambda x: x.reshape(CHUNK_BATCH * C, HEAD_DIM)
            stage[0, rows, :] = (flat(dq) * scale).astype(BF16)
            stage[1, rows, :] = flat(dk).astype(BF16)
            stage[2, rows, :] = flat(dv).astype(BF16)
            return tuple(new_dlams)

        dlams = lax.fori_loop(0, nc // CHUNK_BATCH, main_step, (jnp.zeros((1, HEAD_DIM), F32),) * 2)
        for row, (dlam, lg) in enumerate(zip(dlams, lgs)):
            small_ref[0, pl.ds(row, 1), :] = jnp.broadcast_to(jnp.sum(dlam, axis=-1, keepdims=True) * lg, (1, HEAD_DIM))
        small_ref[0, pl.ds(2, 6), :] = jnp.zeros((6, HEAD_DIM), F32)
        _put_groups(stage, dproj_out, (3, 4, 5), H, sems)

    hspec = pl.BlockSpec((S, HEAD_DIM), lambda h: (0, h))
    smem = pl.BlockSpec(memory_space=pltpu.SMEM)
    return pl.pallas_call(
        body, grid=(H,), name="ret_bwd",
        in_specs=[smem, smem] + _head_specs(S, (3, 4, 5), H) + [hspec, ANY],
        out_specs=[pl.BlockSpec((1, 8, HEAD_DIM), lambda h: (h, 0, 0)), ANY],
        out_shape=[SDS((H, 8, HEAD_DIM), F32), SDS(dproj.shape, dproj.dtype)], input_output_aliases={6: 1},
        scratch_shapes=[pltpu.VMEM((2, nc, HEAD_DIM, HEAD_DIM), F32), pltpu.VMEM((2, nc, HEAD_DIM, HEAD_DIM), F32),
                        pltpu.VMEM((3, S, HEAD_DIM), BF16), pltpu.SemaphoreType.DMA((3,))],
        compiler_params=_cp(1))(dec_f, dec_b, proj, proj, proj, d_out, dproj)


def _ffn_bwd_act(dh2, wd, g, u):
    S, D = dh2.shape
    nblk, _, FB = g.shape
    tm = min(1024, S)

    def body(dh_ref, wd_ref, g_ref, u_ref, dg_ref, du_ref):
        dact = _dot(dh_ref[...], wd_ref[...], _NT)
        gg = g_ref[0].astype(F32)
        sg = _sigmoid(gg)
        dg_ref[0] = (dact * u_ref[0].astype(F32) * (sg * (1.0 + gg * (1.0 - sg)))).astype(BF16)
        du_ref[0] = (dact * (gg * sg)).astype(BF16)

    blk = pl.BlockSpec((1, tm, FB), lambda j, i: (j, i, 0))
    return pl.pallas_call(
        body, grid=(nblk, S // tm), name="ffn_bwd_act",
        in_specs=[pl.BlockSpec((tm, D), lambda j, i: (i, 0)), pl.BlockSpec((FB, D), lambda j, i: (j, 0)), blk, blk],
        out_specs=[blk, blk], out_shape=[SDS((nblk, S, FB), BF16)] * 2,
        compiler_params=_cp(2))(dh2, wd, g, u)


def _ffn_bwd_in(dg, du, wg, wu, h1, dh2, w_norm):
    nblk, S, FB = dg.shape
    D = h1.shape[1]
    tm = min(RESIDENT_ROWS, S)
    blk = pl.BlockSpec((nblk, tm, FB), lambda i: (0, i, 0))
    row = pl.BlockSpec((tm, D), lambda i: (i, 0))
    vec = pl.BlockSpec((1, D), lambda i: (0, 0))

    def gate_body(dg_ref, wg_ref, part_ref):
        part_ref[...] = _blocked_matmul(dg_ref, wg_ref)

    part = pl.pallas_call(
        gate_body, grid=(S // tm,), name="ffn_bwd_in_gate", in_specs=[blk, _resident((nblk * FB, D))],
        out_specs=row, out_shape=SDS((S, D), F32), compiler_params=_cp(1))(dg, wg.reshape(nblk * FB, D))

    def body(du_ref, wu_ref, part_ref, h_ref, dh2_ref, wn_ref, dh_ref, dhb_ref, dw_ref):
        @pl.when(pl.program_id(0) == 0)
        def _():
            dw_ref[...] = jnp.zeros_like(dw_ref)

        dh, dw = _rms_bwd(part_ref[...] + _blocked_matmul(du_ref, wu_ref), h_ref[...], wn_ref[...])
        dh = dh2_ref[...] + dh
        dh_ref[...] = dh
        dhb_ref[...] = dh.astype(BF16)
        dw_ref[...] += dw

    return pl.pallas_call(
        body, grid=(S // tm,), name="ffn_bwd_in",
        in_specs=[blk, _resident((nblk * FB, D)), row, row, row, vec],
        out_specs=[row, row, vec], out_shape=[SDS((S, D), F32), SDS((S, D), BF16), SDS((1, D), F32)],
        compiler_params=_cp(1))(du, wu.reshape(nblk * FB, D), part, h1, dh2, w_norm)


def _dmix(dh1, w_out):
    S, D = dh1.shape
    tm = min(512, S)

    def body(dh_ref, w_ref, o_ref):
        o_ref[...] = _dot(dh_ref[...], w_ref[...], _NT)

    row = pl.BlockSpec((tm, D), lambda i: (i, 0))
    return pl.pallas_call(
        body, grid=(S // tm,), name="dmix", in_specs=[row, pl.BlockSpec((D, D), lambda i: (0, 0))],
        out_specs=row, out_shape=SDS((S, D), F32), compiler_params=_cp(1))(dh1, w_out)


def _in_bwd(dproj, w_blk, x, dh1, w_norm):
    S, D = x.shape
    nblk, _, NB = w_blk.shape
    tm = min(RESIDENT_ROWS, S)

    def body(dp_ref, w_ref, x_ref, dh1_ref, wn_ref, dx_ref, dw_ref):
        @pl.when(pl.program_id(0) == 0)
        def _():
            dw_ref[...] = jnp.zeros_like(dw_ref)

        dn = None
        for j in range(nblk):
            part = _dot(dp_ref[:, pl.ds(j * NB, NB)], w_ref[j], _NT)
            dn = part if dn is None else dn + part
        dh, dw = _rms_bwd(dn, x_ref[...], wn_ref[...])
        dx_ref[...] = dh1_ref[...] + dh
        dw_ref[...] += dw

    row = pl.BlockSpec((tm, D), lambda i: (i, 0))
    vec = pl.BlockSpec((1, D), lambda i: (0, 0))
    return pl.pallas_call(
        body, grid=(S // tm,), name="in_bwd",
        in_specs=[pl.BlockSpec((tm, nblk * NB), lambda i: (i, 0)),
                  pl.BlockSpec((nblk, D, NB), lambda i: (0, 0, 0), pipeline_mode=pl.Buffered(1)), row, row, vec],
        out_specs=[row, vec], out_shape=[SDS((S, D), F32), SDS((1, D), F32)],
        compiler_params=_cp(1))(dproj, w_blk, x, dh1, w_norm)


def _wgrad(a, b, a_spec, b_spec, o_spec, o_shape, grid, name):
    nk = grid[-1]

    def ld(ref):
        return ref[0] if len(ref.shape) == 3 else ref[...]

    def body(a_ref, b_ref, o_ref, acc):
        k = pl.program_id(len(grid) - 1)

        @pl.when(k == 0)
        def _():
            acc[...] = jnp.zeros_like(acc)

        acc[...] += _dot(ld(a_ref), ld(b_ref), _TN)

        @pl.when(k == nk - 1)
        def _():
            if len(o_ref.shape) == 3:
                o_ref[0] = acc[...].astype(o_ref.dtype)
            else:
                o_ref[...] = acc[...].astype(o_ref.dtype)

    return pl.pallas_call(
        body, grid=grid, name=name, in_specs=[a_spec, b_spec], out_specs=o_spec, out_shape=SDS(o_shape, BF16),
        scratch_shapes=[pltpu.VMEM(o_spec.block_shape[-2:], F32)], compiler_params=_cp(len(grid)))(a, b)


def _peer(k):
    x, y, c = lax.axis_index("x"), lax.axis_index("y"), lax.axis_index("c")
    px = 1 - x if k & 4 else x
    py = 1 - y if k & 2 else y
    pc = 1 - c if k & 1 else c
    return (px, py, pc), 4 * px + 2 * py + pc


def _exchange_copies(srcs, lands, send_sems, recv_sems, which, gather):
    _, me = _peer(0)
    pairs = []
    for pos, a in enumerate(which):
        for k in range(1, N_DEV):
            dev, idx = _peer(k)
            sem = pos * (N_DEV - 1) + k - 1
            src = srcs[a] if gather else srcs[a].at[idx]
            mk = functools.partial(pltpu.make_async_remote_copy, src_ref=src, send_sem=send_sems.at[sem],
                                   recv_sem=recv_sems.at[sem], device_id=dev, device_id_type=MESH)
            pairs.append((mk(dst_ref=lands[a].at[me]), mk(dst_ref=lands[a].at[idx])))
    return pairs


def _sequencer_kernel(name, collective_id, n_remote, n_local):
    return pl.kernel(mesh=plsc.ScalarSubcoreMesh(axis_name="sequencer", num_cores=1), name=name,
                     scratch_types=(pltpu.SemaphoreType.DMA((n_remote,)), pltpu.SemaphoreType.DMA((n_remote,)),
                                    pltpu.SemaphoreType.DMA((n_local,))),
                     compiler_params=pltpu.CompilerParams(collective_id=collective_id))


def _handshake(ks):
    barrier = pltpu.get_barrier_semaphore()
    for k in ks:
        pl.semaphore_signal(barrier, inc=1, device_id=_peer(k)[0], device_id_type=MESH)
    pl.semaphore_wait(barrier, len(ks))


def _sequencer_scatter(arrays, name, collective_id):
    n = len(arrays)
    hbm = pltpu.MemorySpace.HBM
    srcs = [jax.new_ref(a, memory_space=hbm) for a in arrays]
    lands = [jax.empty_ref(SDS(a.shape, a.dtype), memory_space=hbm) for a in arrays]

    @_sequencer_kernel(name, collective_id, n * (N_DEV - 1), n)
    def launch(send_sems, recv_sems, local_sems):
        _handshake(range(1, N_DEV))
        _, me = _peer(0)
        local = [pltpu.make_async_copy(srcs[a].at[me], lands[a].at[me], local_sems.at[a]) for a in range(n)]
        pairs = _exchange_copies(srcs, lands, send_sems, recv_sems, range(n), False)
        for out, _ in pairs:
            out.start()
        for cp in local:
            cp.start()
        for out, arrival in pairs:
            out.wait_send()
            arrival.wait_recv()
        for cp in local:
            cp.wait()

    launch()
    return [r[...] for r in lands]


SIBLING = 1
OTHER_CHIPS = (2, 4, 6)


def _sequencer_gather(arrays, name, collective_id):
    n = len(arrays)
    hbm = pltpu.MemorySpace.HBM
    srcs = [jax.new_ref(a, memory_space=hbm) for a in arrays]
    lands = [jax.empty_ref(SDS((N_DEV,) + a.shape, a.dtype), memory_space=hbm) for a in arrays]

    @_sequencer_kernel(name, collective_id, n * (N_DEV - 1), n)
    def launch(send_sems, recv_sems, local_sems):
        _handshake((SIBLING,) + OTHER_CHIPS)
        _, me = _peer(0)
        sibling, _ = _peer(SIBLING)

        def copy(a, k, src, block, to):
            sem = a * (N_DEV - 1) + k - 1
            return pltpu.make_async_remote_copy(src_ref=src, dst_ref=lands[a].at[block], send_sem=send_sems.at[sem],
                                                recv_sem=recv_sems.at[sem], device_id=to, device_id_type=MESH)

        local = [pltpu.make_async_copy(srcs[a], lands[a].at[me], local_sems.at[a]) for a in range(n)]
        first = [copy(a, k, srcs[a], me, _peer(k)[0]) for a in range(n) for k in OTHER_CHIPS + (SIBLING,)]
        for cp in first + local:
            cp.start()
        passed = []
        for a in range(n):
            for k in OTHER_CHIPS:
                _, block = _peer(k)
                copy(a, k, srcs[a], block, sibling).wait_recv()
                passed.append(copy(a, k ^ SIBLING, lands[a].at[block], block, sibling))
                passed[-1].start()
        for a in range(n):
            for k in (SIBLING,) + tuple(k ^ SIBLING for k in OTHER_CHIPS):
                copy(a, k, srcs[a], _peer(k)[1], sibling).wait_recv()
        for cp in first + passed:
            cp.wait_send()
        for cp in local:
            cp.wait()

    launch()
    return [r[...] for r in lands]


def _sequencer_gather_chips(array, name, collective_id, chips):
    hbm = pltpu.MemorySpace.HBM
    src = jax.new_ref(array, memory_space=hbm)
    land = jax.empty_ref(SDS((2 * len(chips),) + array.shape, array.dtype), memory_space=hbm)

    @_sequencer_kernel(name, collective_id, 2 * len(chips), 1)
    def launch(send_sems, recv_sems, local_sems):
        _handshake((SIBLING,) + tuple(k for k in chips if k))
        c = lax.axis_index("c")
        sibling, _ = _peer(SIBLING)

        def copy(sem, src_ref, slot, to):
            return pltpu.make_async_remote_copy(src_ref=src_ref, dst_ref=land.at[slot], send_sem=send_sems.at[sem],
                                                recv_sem=recv_sems.at[sem], device_id=to, device_id_type=MESH)

        started = []
        for pos, k in enumerate(chips):
            started.append(copy(2 * pos, src, 2 * pos + c, _peer(k)[0] if k else sibling))
            started[-1].start()
        for pos, k in enumerate(chips):
            if k:
                copy(2 * pos, src, 2 * pos + c, sibling).wait_recv()
                started.append(copy(2 * pos + 1, land.at[2 * pos + c], 2 * pos + c, sibling))
                started[-1].start()
        for pos, k in enumerate(chips):
            copy(2 * pos + 1 if k else 2 * pos, src, 2 * pos + 1 - c, sibling).wait_recv()
        for cp in started:
            cp.wait_send()

    launch()
    return land[...]


def _sequencer_relay_far(near, name, collective_id):
    hbm = pltpu.MemorySpace.HBM
    src = jax.new_ref(near, memory_space=hbm)
    land = jax.empty_ref(SDS((2,) + near.shape[1:], near.dtype), memory_space=hbm)
    half = near.shape[1] // 2

    @_sequencer_kernel(name, collective_id, 3, 1)
    def launch(send_sems, recv_sems, local_sems):
        _handshake((SIBLING, 4, 2))
        c = lax.axis_index("c")
        sibling, _ = _peer(SIBLING)

        def copy(sem, src_ref, dst_ref, to):
            return pltpu.make_async_remote_copy(src_ref=src_ref, dst_ref=dst_ref, send_sem=send_sems.at[sem],
                                                recv_sem=recv_sems.at[sem], device_id=to, device_id_type=MESH)

        upper, lower = pl.ds(0, half), pl.ds(half, half)
        halves = [copy(0, src.at[c, upper, :], land.at[c, upper, :], _peer(2)[0]),
                  copy(1, src.at[2 + c, lower, :], land.at[c, lower, :], _peer(4)[0])]
        for cp in halves:
            cp.start()
        for cp in halves:
            cp.wait_recv()
        passed = copy(2, land.at[c], land.at[c], sibling)
        passed.start()
        copy(2, land.at[1 - c], land.at[1 - c], sibling).wait_recv()
        for cp in halves + [passed]:
            cp.wait_send()

    launch()
    return land[...]


SMALL_ROWS = 64


def _small_step(part, w, m, v):
    def body(p_ref, w_ref, m_ref, v_ref, g_ref, d_ref, nm_ref, nv_ref, gath, send_sems, recv_sems):
        _, me = _peer(0)
        gath[me] = p_ref[...]
        copies = []
        for k in range(1, N_DEV):
            dev, idx = _peer(k)
            out = pltpu.make_async_remote_copy(src_ref=p_ref, dst_ref=gath.at[me], send_sem=send_sems.at[k - 1],
                                               recv_sem=recv_sems.at[k - 1], device_id=dev, device_id_type=MESH)
            out.start()
            arrival = pltpu.make_async_remote_copy(src_ref=p_ref, dst_ref=gath.at[idx], send_sem=send_sems.at[k - 1],
                                                   recv_sem=recv_sems.at[k - 1], device_id=dev, device_id_type=MESH)
            copies.append((out, arrival))
        for out, arrival in copies:
            out.wait_send()
            arrival.wait_recv()
        g = gath[0]
        for p in range(1, N_DEV):
            g = g + gath[p]
        g_ref[...] = g
        d_ref[...], nm_ref[...], nv_ref[...] = _adamw(w_ref[...], g, m_ref[...], v_ref[...])

    vm = pl.BlockSpec(memory_space=pltpu.VMEM)
    return pl.pallas_call(
        body, name="small_step", in_specs=[vm] * 4, out_specs=[vm] * 4,
        out_shape=[SDS((SMALL_ROWS, 128), F32)] * 4,
        scratch_shapes=[pltpu.VMEM((N_DEV, SMALL_ROWS, 128), F32), pltpu.SemaphoreType.DMA((N_DEV - 1,)),
                        pltpu.SemaphoreType.DMA((N_DEV - 1,))])(part, w, m, v)


def _adamw(w, g, m, v):
    m = ADAM_B1 * m + (1.0 - ADAM_B1) * g
    v = ADAM_B2 * v + (1.0 - ADAM_B2) * (g * g)
    m_hat = m / (1.0 - ADAM_B1 ** ADAM_STEP)
    v_hat = v / (1.0 - ADAM_B2 ** ADAM_STEP)
    delta = -ADAM_LR * (m_hat / (jnp.sqrt(v_hat) + ADAM_EPS) + ADAM_WD * w)
    return delta, m, v


def _adamw_block(parts, w, m, v, name):
    R, C = w.shape
    n_parts = len(parts)
    Rp = R // n_parts
    tr = next(t for t in (256, 128, 64, 32, 16, 8) if Rp % t == 0 and t * C <= 256 * 1024)
    per_part = Rp // tr

    def body(*refs):
        p_refs = refs[:n_parts]
        w_ref, m_ref, v_ref, g_ref, d_ref, nm_ref, nv_ref = refs[n_parts:]
        for k, p_ref in enumerate(p_refs):
            @pl.when(pl.program_id(0) // per_part == k)
            def _(p_ref=p_ref):
                g = p_ref[0].astype(F32)
                for p in range(1, N_DEV):
                    g = g + p_ref[p].astype(F32)
                g_ref[...] = g
                d_ref[...], nm_ref[...], nv_ref[...] = _adamw(w_ref[...], g, m_ref[...], v_ref[...])

    row = pl.BlockSpec((tr, C), lambda i: (i, 0))
    part_specs = [pl.BlockSpec((N_DEV, tr, C), functools.partial(
        lambda i, k: (0, jnp.clip(i - k * per_part, 0, per_part - 1), 0), k=k)) for k in range(n_parts)]
    return pl.pallas_call(
        body, grid=(R // tr,), name=name, in_specs=part_specs + [row, row, row],
        out_specs=[row] * 4, out_shape=[SDS((R, C), F32)] * 4, compiler_params=_cp(1))(*parts, w, m, v)


def _pack_small(mix, ffn, fin, retw, dec_f, dec_b, loss):
    flat = jnp.concatenate([mix.reshape(-1), ffn.reshape(-1), fin.reshape(-1), retw.reshape(-1), dec_f.reshape(-1),
                            dec_b.reshape(-1), loss.reshape(-1)])
    return jnp.pad(flat, (0, SMALL_ROWS * 128 - flat.shape[0])).reshape(SMALL_ROWS, 128)


def _unpack_small(packed, shapes):
    flat = packed.reshape(-1)
    out, at = [], 0
    for s in shapes:
        n = math.prod(s)
        out.append(flat[at:at + n].reshape(s))
        at += n
    return out


def kernel(x, norm_mix_w, w_in, ret_decay_fwd, ret_decay_bwd, ret_norm_w, w_out, norm_ffn_w, w_gate, w_up, w_down, norm_final_w, loss_target, m_norm_mix_w, m_w_in, m_ret_decay_fwd, m_ret_decay_bwd, m_ret_norm_w, m_w_out, m_norm_ffn_w, m_w_gate, m_w_up, m_w_down, m_norm_final_w, v_norm_mix_w, v_w_in, v_ret_decay_fwd, v_ret_decay_bwd, v_ret_norm_w, v_w_out, v_norm_ffn_w, v_w_gate, v_w_up, v_w_down, v_norm_final_w):
    x2 = x[0]
    tgt = loss_target[0]
    S, D = x2.shape
    H = ret_norm_w.shape[1] // HEAD_DIM
    DA = H * HEAD_DIM
    fin_w = norm_final_w.reshape(1, D)
    big = (w_in[0], w_out[0], w_gate[0].T, w_up[0].T, w_down[0])

    big_b = [big[0].astype(BF16)]
    stages = ((0,), (4, 2), (6,))
    wi_stages = [_sequencer_gather_chips(big_b[0], name, cid, ks)
                 for name, cid, ks in zip(("gather_in_own", "gather_in_near"), (0, 7), stages)]
    NB = big_b[0].shape[1]
    ax, ay = lax.axis_index("x"), lax.axis_index("y")
    chip_of = {k: 2 * (1 - ax if k & 4 else ax) + (1 - ay if k & 2 else ay) for k in (0, 2, 4, 6)}

    n1 = _norm_fwd(x2, norm_mix_w)
    ac = lax.axis_index("c")
    me = 2 * chip_of[0] + ac
    vec = lambda *v: jnp.stack([jnp.asarray(t, jnp.int32) for t in v])
    proj = _proj_part(n1, big_b[0][None], vec(0), vec(me), None, N_DEV, "proj_self")
    for ks, name in zip(stages, ("proj_sibling", "proj_near", "proj_far")):
        if name == "proj_near":
            rest, proj = lax.optimization_barrier((tuple(big[1:]), proj))
            big_b = big_b + [w.astype(BF16) for w in rest]
            near, proj, big_b = lax.optimization_barrier((wi_stages[1], proj, big_b))
            wi_stages[1] = near
            wi_stages.append(_sequencer_relay_far(near, "gather_in_far", 8))
            wo, = _sequencer_gather(big_b[1:2], "gather_out", 1)
            wg, wu = _sequencer_gather(big_b[2:4], "gather_gate_up", 9)
            wd, = _sequencer_gather(big_b[4:], "gather_down", 5)
            wi, = _sequencer_gather(big_b[:1], "gather_in_ordered", 10)
        w_st = wi_stages[len(wi_stages) - 1 if name == "proj_far" else ("proj_sibling", "proj_near").index(name)]
        slots, blocks = [], []
        for pos, k in enumerate(ks):
            for core in ((1 - ac,) if k == 0 else (0, 1)):
                slots.append(2 * pos + core)
                blocks.append(2 * chip_of[k] + core)
        proj = _proj_part(n1, w_st, vec(*slots), vec(*blocks), proj, N_DEV, name)
    bias = _attn_bias()[:H]
    attn, lse = _attn_fwd(proj, bias)
    ret, o_raw = _ret_fwd(proj, ret_decay_fwd, ret_decay_bwd, ret_norm_w)
    wo_full = wo.reshape(D, D)
    d_ff = N_DEV * wd.shape[1]
    FB = FFN_BLOCK if d_ff % FFN_BLOCK == 0 else wd.shape[1]
    n_fb = d_ff // FB
    wg, wu = wg.reshape(n_fb, FB, D), wu.reshape(n_fb, FB, D)
    wd_full = wd.reshape(d_ff, D)
    h1, mixed, n2 = _out_fwd(x2, attn, ret, wo_full, norm_ffn_w)
    gate, up, act = _ffn_up(n2, wg, wu)
    dh2, dh2_b, loss_parts, g_fin = _ffn_down_loss(act, wd_full, h1, tgt, fin_w)

    tn = min(1024, D)
    ffn_specs = (pl.BlockSpec((1, S, FB), lambda j, n, k: (j, 0, 0)), pl.BlockSpec((S, tn), lambda j, n, k: (0, n)),
                 pl.BlockSpec((1, FB, tn), lambda j, n, k: (j, 0, n)), (n_fb, FB, D), (n_fb, D // tn, 1))
    per_dev = (N_DEV, d_ff // N_DEV, D)
    g_wd = _wgrad(act, dh2_b, *ffn_specs, "wgrad_down").reshape(per_dev)
    part_d, = _sequencer_scatter([g_wd], "scatter_down", 2)
    dgate, dup = _ffn_bwd_act(dh2_b, wd_full, gate, up)
    g_wg = _wgrad(dgate, n2, *ffn_specs, "wgrad_gate").reshape(per_dev)
    g_wu = _wgrad(dup, n2, *ffn_specs, "wgrad_up").reshape(per_dev)
    parts_f = _sequencer_scatter([g_wg, g_wu], "scatter_gate_up", 11) + [part_d]
    dh1, dh1_b, g_ffn = _ffn_bwd_in(dgate, dup, wg, wu, h1, dh2, norm_ffn_w)
    dmix = _dmix(dh1_b, wo_full)
    tmw = min(512, D)
    tk = min(2048, S)
    g_wo = _wgrad(mixed, dh1_b, pl.BlockSpec((tk, tmw), lambda m, k: (k, m)), pl.BlockSpec((tk, D), lambda m, k: (k, 0)),
                  pl.BlockSpec((tmw, D), lambda m, k: (m, 0)), (D, D), (D // tmw, S // tk), "wgrad_out")
    parts_o = _sequencer_scatter([g_wo.reshape(N_DEV, D // N_DEV, D)], "scatter_out", 3)
    d_ret, small_w, dproj = _ret_gate_bwd(proj, o_raw, dmix, ret_norm_w, DA, lax.empty(proj.shape, BF16))
    small, dproj = _ret_bwd(proj, d_ret, ret_decay_fwd, ret_decay_bwd, dproj)
    dproj = _attn_bwd(proj, attn, lse, dmix, bias, dproj)
    half = D // tmw // 2
    parts_i = []
    for part, (name, cid) in enumerate((("in_lo", 4), ("in_hi", 6))):
        g_wi = _wgrad(n1, dproj, pl.BlockSpec((S, tmw), functools.partial(lambda j, m, k, off: (0, m + off), off=part * half)),
                      pl.BlockSpec((S, NB), lambda j, m, k: (0, j)), pl.BlockSpec((1, tmw, NB), lambda j, m, k: (j, m, 0)),
                      (N_DEV, D // 2, NB), (N_DEV, half, 1), "wgrad_" + name)
        parts_i += _sequencer_scatter([g_wi], "scatter_" + name, cid)
    grad_x, g_mix = _in_bwd(dproj, wi, x2, dh1, norm_mix_w)

    big_m = (m_w_in[0], m_w_out[0], m_w_gate[0].T, m_w_up[0].T, m_w_down[0])
    big_v = (v_w_in[0], v_w_out[0], v_w_gate[0].T, v_w_up[0].T, v_w_down[0])
    names = ("adamw_in", "adamw_out", "adamw_gate", "adamw_up", "adamw_down")
    upd = [None] * 5
    for a, p in zip((2, 3, 4, 1, 0), [[t] for t in parts_f + parts_o] + [parts_i]):
        upd[a] = _adamw_block(p, big[a], big_m[a], big_v[a], names[a])

    g_dec_f = small[:, 0, 0].reshape(1, H)
    g_dec_b = small[:, 1, 0].reshape(1, H)
    g_retw = small_w[:, 0, :].reshape(1, DA)
    loss_local = jnp.sum(loss_parts[::8, 0])
    zero = jnp.zeros((1,), F32)
    part = _pack_small(g_mix, g_ffn, g_fin, g_retw, g_dec_f, g_dec_b, loss_local)
    sw = _pack_small(norm_mix_w, norm_ffn_w, norm_final_w, ret_norm_w, ret_decay_fwd, ret_decay_bwd, zero)
    sm = _pack_small(m_norm_mix_w, m_norm_ffn_w, m_norm_final_w, m_ret_norm_w, m_ret_decay_fwd, m_ret_decay_bwd, zero)
    sv = _pack_small(v_norm_mix_w, v_norm_ffn_w, v_norm_final_w, v_ret_norm_w, v_ret_decay_fwd, v_ret_decay_bwd, zero)
    shapes = [(1, D), (1, D), (D,), (1, DA), (1, H), (1, H), ()]
    sg, sd, snm, snv = [_unpack_small(t, shapes) for t in _small_step(part, sw, sm, sv)]
    loss = sg[6]

    def ordered(small_set, k):
        b = [(u[k].T if a in (2, 3) else u[k])[None] for a, u in enumerate(upd)]
        return [small_set[0], b[0], small_set[4], small_set[5], small_set[3], b[1], small_set[1], b[2], b[3], b[4],
                small_set[2]]

    return (loss, grad_x[None], *ordered(sg, 0), *ordered(sd, 1), *ordered(snm, 2), *ordered(snv, 3))
```
